```python
import math
import jax, jax.numpy as jnp
from jax import lax
import numpy as np

D_MODEL = 1024
BATCH = 8
SEQ = 8192
DEPTH = 1

PLE_DIM = 256
D_POOL = D_MODEL // 2
POOL_WINDOWS = (2, 4, 8, 16)
N_POOL_GROUPS = len(POOL_WINDOWS)
POOL_GROUP_DIM = D_POOL // N_POOL_GROUPS
SB_HEADS = 8
SB_HEAD_DIM = 64
D_SB = SB_HEADS * SB_HEAD_DIM
Q_BLOCK = 128
D_FF = ((8 * D_MODEL // 3 + 255) // 256) * 256
D_IN = D_POOL + 3 * D_SB + 2 * D_MODEL
RMS_EPS = 1e-6

kernel_name = "hybrid_pool_stickbreaking_gated_block"


def rms_norm(x, gain):
    xf = x.astype(jnp.float32)
    inv = lax.rsqrt(jnp.mean(xf * xf, axis=-1, keepdims=True) + RMS_EPS)
    return (xf * inv * gain.astype(jnp.float32)).astype(x.dtype)


def pool_mixer(u, w_pool, pool_scale):
    B, S, _ = u.shape
    uf = u.astype(jnp.float32)
    csum = jnp.cumsum(uf, axis=1)
    t = jnp.arange(S, dtype=jnp.int32)
    outs = []
    for g, w in enumerate(POOL_WINDOWS):
        sl = slice(g * POOL_GROUP_DIM, (g + 1) * POOL_GROUP_DIM)
        cg = csum[..., sl]
        prev = jnp.pad(cg, ((0, 0), (w, 0), (0, 0)))[:, :S]
        count = jnp.minimum(t + 1, w).astype(jnp.float32)[None, :, None]
        outs.append((cg - prev) / count - uf[..., sl])
    pooled = jnp.stack(outs, axis=2).astype(u.dtype)
    mixed = jnp.einsum('bsgc,gcd->bsgd', pooled, w_pool).reshape(B, S, D_POOL)
    return mixed * pool_scale


def stick_breaking_attention(q, k, v):
    B, H, S, dh = q.shape
    n_blk = S // Q_BLOCK
    scale = 1.0 / math.sqrt(dh)
    q_blocks = q.reshape(B, H, n_blk, Q_BLOCK, dh).transpose(2, 0, 1, 3, 4)
    starts = jnp.arange(n_blk, dtype=jnp.int32) * Q_BLOCK
    k_pos = jnp.arange(S, dtype=jnp.int32)

    def one_block(args):
        q_i, start = args
        z = jnp.einsum('bhqd,bhkd->bhqk', q_i, k).astype(jnp.float32) * scale
        t_pos = start + jnp.arange(Q_BLOCK, dtype=jnp.int32)
        mask = k_pos[None, :] < t_pos[:, None]
        log_fail = jnp.where(mask, jax.nn.log_sigmoid(-z), 0.0)
        suffix = lax.cumsum(log_fail, axis=3, reverse=True) - log_fail
        a = jnp.where(mask, jnp.exp(jax.nn.log_sigmoid(z) + suffix), 0.0)
        return jnp.einsum('bhqk,bhkd->bhqd', a.astype(v.dtype), v)

    out = lax.map(one_block, (q_blocks, starts))
    return out.transpose(1, 2, 0, 3, 4).reshape(B, H, S, dh)


def _fwd_setup_inputs(seed: int = 0) -> dict:
    key = jax.random.key(seed)
    ks = jax.random.split(key, 20)
    f32 = jnp.float32

    def nrm(k, shape, fan_in):
        return jax.random.normal(k, shape, f32) * (fan_in ** -0.5)

    def gain(k, shape):
        return jnp.ones(shape, f32) + 0.02 * jax.random.normal(k, shape, f32)

    return {
        "x": jax.random.normal(ks[0], (BATCH, SEQ, D_MODEL), f32),
        "p": jax.random.normal(ks[1], (DEPTH, BATCH, SEQ, PLE_DIM), f32),
        "norm_mix": gain(ks[2], (DEPTH, D_MODEL)),
        "w_in": nrm(ks[3], (DEPTH, D_MODEL, D_IN), D_MODEL),
        "w_pool": nrm(ks[4], (DEPTH, N_POOL_GROUPS, POOL_GROUP_DIM, POOL_GROUP_DIM), POOL_GROUP_DIM),
        "pool_scale": gain(ks[5], (DEPTH, D_POOL)),
        "w_branch_a": nrm(ks[6], (DEPTH, D_POOL, D_MODEL), D_POOL),
        "w_branch_b": nrm(ks[7], (DEPTH, D_SB, D_MODEL), D_SB),
        "w_out": nrm(ks[8], (DEPTH, D_MODEL, D_MODEL), D_MODEL),
        "norm_ffn": gain(ks[9], (DEPTH, D_MODEL)),
        "w_ffn_gate": nrm(ks[10], (DEPTH, D_MODEL, D_FF), D_MODEL),
        "w_ffn_up": nrm(ks[11], (DEPTH, D_MODEL, D_FF), D_MODEL),
        "w_ffn_down": nrm(ks[12], (DEPTH, D_FF, D_MODEL), D_FF),
        "norm_ple": gain(ks[13], (DEPTH, D_MODEL)),
        "w_ple_gate": nrm(ks[14], (DEPTH, D_MODEL, D_MODEL), D_MODEL),
        "w_ple_proj": nrm(ks[15], (DEPTH, PLE_DIM, D_MODEL), PLE_DIM),
        "norm_final": gain(ks[16], (D_MODEL,)),
    }


def _fwd_reference(x, p, norm_mix, w_in, w_pool, pool_scale, w_branch_a, w_branch_b, w_out,
              norm_ffn, w_ffn_gate, w_ffn_up, w_ffn_down, norm_ple, w_ple_gate, w_ple_proj,
              norm_final):
    B, S, _ = x.shape
    split_at = [D_POOL, D_POOL + D_SB, D_POOL + 2 * D_SB, D_POOL + 3 * D_SB,
                D_POOL + 3 * D_SB + D_MODEL]
    for i in range(DEPTH):
        h = rms_norm(x, norm_mix[i])
        proj = h @ w_in[i]
        u_pool, q, k, v, g_a, g_b = jnp.split(proj, split_at, axis=-1)
        y_a = pool_mixer(u_pool, w_pool[i], pool_scale[i])
        to_heads = lambda t: t.reshape(B, S, SB_HEADS, SB_HEAD_DIM).transpose(0, 2, 1, 3)
        y_b = stick_breaking_attention(to_heads(q), to_heads(k), to_heads(v))
        y_b = y_b.transpose(0, 2, 1, 3).reshape(B, S, D_SB)
        merged = (jax.nn.sigmoid(g_a) * (y_a @ w_branch_a[i])
                  + jax.nn.sigmoid(g_b) * (y_b @ w_branch_b[i]))
        x = x + merged @ w_out[i]
        h = rms_norm(x, norm_ffn[i])
        x = x + (jax.nn.silu(h @ w_ffn_gate[i]) * (h @ w_ffn_up[i])) @ w_ffn_down[i]
        gate = jax.nn.sigmoid(rms_norm(x, norm_ple[i]) @ w_ple_gate[i])
        x = x + gate * (p[i] @ w_ple_proj[i])
    return rms_norm(x, norm_final)


import jax as _jax
import jax.numpy as _jnp

TWIN_FORMAT = 'train_step'
FWD_PARAMS = ['x', 'p', 'norm_mix', 'w_in', 'w_pool', 'pool_scale', 'w_branch_a', 'w_branch_b', 'w_out', 'norm_ffn', 'w_ffn_gate', 'w_ffn_up', 'w_ffn_down', 'norm_ple', 'w_ple_gate', 'w_ple_proj', 'norm_final']
TWIN_WEIGHTS = ['norm_mix', 'w_in', 'w_pool', 'pool_scale', 'w_branch_a', 'w_branch_b', 'w_out', 'norm_ffn', 'w_ffn_gate', 'w_ffn_up', 'w_ffn_down', 'norm_ple', 'w_ple_gate', 'w_ple_proj', 'norm_final']
TWIN_DIFF_INPUT = 'x'
TWIN_INPUTS = ['x', 'p', 'norm_mix', 'w_in', 'w_pool', 'pool_scale', 'w_branch_a', 'w_branch_b', 'w_out', 'norm_ffn', 'w_ffn_gate', 'w_ffn_up', 'w_ffn_down', 'norm_ple', 'w_ple_gate', 'w_ple_proj', 'norm_final', 'loss_target', 'm_norm_mix', 'm_w_in', 'm_w_pool', 'm_pool_scale', 'm_w_branch_a', 'm_w_branch_b', 'm_w_out', 'm_norm_ffn', 'm_w_ffn_gate', 'm_w_ffn_up', 'm_w_ffn_down', 'm_norm_ple', 'm_w_ple_gate', 'm_w_ple_proj', 'm_norm_final', 'v_norm_mix', 'v_w_in', 'v_w_pool', 'v_pool_scale', 'v_w_branch_a', 'v_w_branch_b', 'v_w_out', 'v_norm_ffn', 'v_w_ffn_gate', 'v_w_ffn_up', 'v_w_ffn_down', 'v_norm_ple', 'v_w_ple_gate', 'v_w_ple_proj', 'v_norm_final']
TWIN_OUTPUTS = ['loss', 'grad_x', 'grad_norm_mix', 'grad_w_in', 'grad_w_pool', 'grad_pool_scale', 'grad_w_branch_a', 'grad_w_branch_b', 'grad_w_out', 'grad_norm_ffn', 'grad_w_ffn_gate', 'grad_w_ffn_up', 'grad_w_ffn_down', 'grad_norm_ple', 'grad_w_ple_gate', 'grad_w_ple_proj', 'grad_norm_final', 'delta_norm_mix', 'delta_w_in', 'delta_w_pool', 'delta_pool_scale', 'delta_w_branch_a', 'delta_w_branch_b', 'delta_w_out', 'delta_norm_ffn', 'delta_w_ffn_gate', 'delta_w_ffn_up', 'delta_w_ffn_down', 'delta_norm_ple', 'delta_w_ple_gate', 'delta_w_ple_proj', 'delta_norm_final', 'new_m_norm_mix', 'new_m_w_in', 'new_m_w_pool', 'new_m_pool_scale', 'new_m_w_branch_a', 'new_m_w_branch_b', 'new_m_w_out', 'new_m_norm_ffn', 'new_m_w_ffn_gate', 'new_m_w_ffn_up', 'new_m_w_ffn_down', 'new_m_norm_ple', 'new_m_w_ple_gate', 'new_m_w_ple_proj', 'new_m_norm_final', 'new_v_norm_mix', 'new_v_w_in', 'new_v_w_pool', 'new_v_pool_scale', 'new_v_w_branch_a', 'new_v_w_branch_b', 'new_v_w_out', 'new_v_norm_ffn', 'new_v_w_ffn_gate', 'new_v_w_ffn_up', 'new_v_w_ffn_down', 'new_v_norm_ple', 'new_v_w_ple_gate', 'new_v_w_ple_proj', 'new_v_norm_final']
TWIN_LEAF_KINDS = {'loss': 'loss', 'grad_x': 'grad_x', 'grad_norm_mix': 'grad_w', 'grad_w_in': 'grad_w', 'grad_w_pool': 'grad_w', 'grad_pool_scale': 'grad_w', 'grad_w_branch_a': 'grad_w', 'grad_w_branch_b': 'grad_w', 'grad_w_out': 'grad_w', 'grad_norm_ffn': 'grad_w', 'grad_w_ffn_gate': 'grad_w', 'grad_w_ffn_up': 'grad_w', 'grad_w_ffn_down': 'grad_w', 'grad_norm_ple': 'grad_w', 'grad_w_ple_gate': 'grad_w', 'grad_w_ple_proj': 'grad_w', 'grad_norm_final': 'grad_w', 'delta_norm_mix': 'delta_w', 'delta_w_in': 'delta_w', 'delta_w_pool': 'delta_w', 'delta_pool_scale': 'delta_w', 'delta_w_branch_a': 'delta_w', 'delta_w_branch_b': 'delta_w', 'delta_w_out': 'delta_w', 'delta_norm_ffn': 'delta_w', 'delta_w_ffn_gate': 'delta_w', 'delta_w_ffn_up': 'delta_w', 'delta_w_ffn_down': 'delta_w', 'delta_norm_ple': 'delta_w', 'delta_w_ple_gate': 'delta_w', 'delta_w_ple_proj': 'delta_w', 'delta_norm_final': 'delta_w', 'new_m_norm_mix': 'new_m', 'new_m_w_in': 'new_m', 'new_m_w_pool': 'new_m', 'new_m_pool_scale': 'new_m', 'new_m_w_branch_a': 'new_m', 'new_m_w_branch_b': 'new_m', 'new_m_w_out': 'new_m', 'new_m_norm_ffn': 'new_m', 'new_m_w_ffn_gate': 'new_m', 'new_m_w_ffn_up': 'new_m', 'new_m_w_ffn_down': 'new_m', 'new_m_norm_ple': 'new_m', 'new_m_w_ple_gate': 'new_m', 'new_m_w_ple_proj': 'new_m', 'new_m_norm_final': 'new_m', 'new_v_norm_mix': 'new_v', 'new_v_w_in': 'new_v', 'new_v_w_pool': 'new_v', 'new_v_pool_scale': 'new_v', 'new_v_w_branch_a': 'new_v', 'new_v_w_branch_b': 'new_v', 'new_v_w_out': 'new_v', 'new_v_norm_ffn': 'new_v', 'new_v_w_ffn_gate': 'new_v', 'new_v_w_ffn_up': 'new_v', 'new_v_w_ffn_down': 'new_v', 'new_v_norm_ple': 'new_v', 'new_v_w_ple_gate': 'new_v', 'new_v_w_ple_proj': 'new_v', 'new_v_norm_final': 'new_v'}


def _forward(args):
    return _fwd_reference(*[args[k] for k in FWD_PARAMS])


def _output_shape():
    def fwd():
        inp = _fwd_setup_inputs(0)
        return _fwd_reference(*[inp[k] for k in FWD_PARAMS])
    out = _jax.eval_shape(fwd)
    return out.shape, out.dtype

N_MICROBATCH = 1
ADAM_LR = 0.001
ADAM_B1 = 0.9
ADAM_B2 = 0.999
ADAM_EPS = 1e-08
ADAM_WD = 0.01
ADAM_STEP = 10
PER_EXAMPLE_BATCH_AXIS = {'x': 0, 'p': 1, 'loss_target': 0}
SHARED_INPUTS = []
_WEIGHT_DTYPES = {'norm_mix': _jnp.float32, 'w_in': _jnp.float32, 'w_pool': _jnp.float32, 'pool_scale': _jnp.float32, 'w_branch_a': _jnp.float32, 'w_branch_b': _jnp.float32, 'w_out': _jnp.float32, 'norm_ffn': _jnp.float32, 'w_ffn_gate': _jnp.float32, 'w_ffn_up': _jnp.float32, 'w_ffn_down': _jnp.float32, 'norm_ple': _jnp.float32, 'w_ple_gate': _jnp.float32, 'w_ple_proj': _jnp.float32, 'norm_final': _jnp.float32}
MOMENT_SCALE = {'norm_mix': 1.496586e-01, 'w_in': 7.574362e-02, 'w_pool': 1.485425e-01, 'pool_scale': 1.525063e-01, 'w_branch_a': 1.070494e-01, 'w_branch_b': 7.863400e-02, 'w_out': 1.324463e-01, 'norm_ffn': 1.556507e-01, 'w_ffn_gate': 6.681296e-02, 'w_ffn_up': 6.468017e-02, 'w_ffn_down': 1.073150e-01, 'norm_ple': 3.738230e-02, 'w_ple_gate': 3.767845e-02, 'w_ple_proj': 9.684140e-02, 'norm_final': 6.407503e+01}


def _to_microbatches(a, axis):
    t = _jnp.moveaxis(a, axis, 0)
    t = t.reshape((N_MICROBATCH, t.shape[0] // N_MICROBATCH) + t.shape[1:])
    return _jnp.moveaxis(t, 1, axis + 1)


def setup_inputs(seed: int = 0) -> dict:
    inp = _fwd_setup_inputs(seed)
    key = _jax.random.fold_in(_jax.random.key(seed), 7919)
    shape, _ = _output_shape()
    out = dict(inp)
    out["loss_target"] = _jax.random.normal(_jax.random.fold_in(key, 0), shape, _jnp.float32)
    for i, name in enumerate(TWIN_WEIGHTS):
        w = inp[name].astype(_jnp.float32)
        if MOMENT_SCALE is None:
            s = _jnp.sqrt(_jnp.mean(_jnp.square(w)) + 1e-30)
        else:
            s = MOMENT_SCALE[name]
        km, kv = _jax.random.split(_jax.random.fold_in(key, i + 1))
        out[name] = w
        out["m_" + name] = s * _jax.random.normal(km, w.shape, _jnp.float32)
        out["v_" + name] = (s * s) * _jax.random.uniform(kv, w.shape, _jnp.float32, 0.5, 1.5)
    if N_MICROBATCH > 1:
        for name, axis in PER_EXAMPLE_BATCH_AXIS.items():
            out[name] = _to_microbatches(out[name], axis)
    return {'x': out['x'], 'p': out['p'], 'norm_mix': out['norm_mix'], 'w_in': out['w_in'], 'w_pool': out['w_pool'], 'pool_scale': out['pool_scale'], 'w_branch_a': out['w_branch_a'], 'w_branch_b': out['w_branch_b'], 'w_out': out['w_out'], 'norm_ffn': out['norm_ffn'], 'w_ffn_gate': out['w_ffn_gate'], 'w_ffn_up': out['w_ffn_up'], 'w_ffn_down': out['w_ffn_down'], 'norm_ple': out['norm_ple'], 'w_ple_gate': out['w_ple_gate'], 'w_ple_proj': out['w_ple_proj'], 'norm_final': out['norm_final'], 'loss_target': out['loss_target'], 'm_norm_mix': out['m_norm_mix'], 'm_w_in': out['m_w_in'], 'm_w_pool': out['m_w_pool'], 'm_pool_scale': out['m_pool_scale'], 'm_w_branch_a': out['m_w_branch_a'], 'm_w_branch_b': out['m_w_branch_b'], 'm_w_out': out['m_w_out'], 'm_norm_ffn': out['m_norm_ffn'], 'm_w_ffn_gate': out['m_w_ffn_gate'], 'm_w_ffn_up': out['m_w_ffn_up'], 'm_w_ffn_down': out['m_w_ffn_down'], 'm_norm_ple': out['m_norm_ple'], 'm_w_ple_gate': out['m_w_ple_gate'], 'm_w_ple_proj': out['m_w_ple_proj'], 'm_norm_final': out['m_norm_final'], 'v_norm_mix': out['v_norm_mix'], 'v_w_in': out['v_w_in'], 'v_w_pool': out['v_w_pool'], 'v_pool_scale': out['v_pool_scale'], 'v_w_branch_a': out['v_w_branch_a'], 'v_w_branch_b': out['v_w_branch_b'], 'v_w_out': out['v_w_out'], 'v_norm_ffn': out['v_norm_ffn'], 'v_w_ffn_gate': out['v_w_ffn_gate'], 'v_w_ffn_up': out['v_w_ffn_up'], 'v_w_ffn_down': out['v_w_ffn_down'], 'v_norm_ple': out['v_norm_ple'], 'v_w_ple_gate': out['v_w_ple_gate'], 'v_w_ple_proj': out['v_w_ple_proj'], 'v_norm_final': out['v_norm_final']}


def _loss(weights, diff, rest, loss_target):
    with _jax.named_scope("forward"):
        args = {**rest, TWIN_DIFF_INPUT: diff, **{k: w.astype(_WEIGHT_DTYPES[k]) for k, w in weights.items()}}
        y = _forward(args)
    with _jax.named_scope("loss_head"):
        err = _jnp.square(y.astype(_jnp.float32) - loss_target)
        return 0.5 * _jnp.sum(_jnp.mean(err, axis=-1)) if err.ndim else 0.5 * err


def _adamw(w, g, m, v):
    m = ADAM_B1 * m + (1.0 - ADAM_B1) * g
    v = ADAM_B2 * v + (1.0 - ADAM_B2) * _jnp.square(g)
    m_hat = m / (1.0 - ADAM_B1 ** ADAM_STEP)
    v_hat = v / (1.0 - ADAM_B2 ** ADAM_STEP)
    delta = -ADAM_LR * (m_hat / (_jnp.sqrt(v_hat) + ADAM_EPS) + ADAM_WD * w)
    return delta, m, v


def reference(x, p, norm_mix, w_in, w_pool, pool_scale, w_branch_a, w_branch_b, w_out, norm_ffn, w_ffn_gate, w_ffn_up, w_ffn_down, norm_ple, w_ple_gate, w_ple_proj, norm_final, loss_target, m_norm_mix, m_w_in, m_w_pool, m_pool_scale, m_w_branch_a, m_w_branch_b, m_w_out, m_norm_ffn, m_w_ffn_gate, m_w_ffn_up, m_w_ffn_down, m_norm_ple, m_w_ple_gate, m_w_ple_proj, m_norm_final, v_norm_mix, v_w_in, v_w_pool, v_pool_scale, v_w_branch_a, v_w_branch_b, v_w_out, v_norm_ffn, v_w_ffn_gate, v_w_ffn_up, v_w_ffn_down, v_norm_ple, v_w_ple_gate, v_w_ple_proj, v_norm_final):
    given = dict(x=x, p=p, norm_mix=norm_mix, w_in=w_in, w_pool=w_pool, pool_scale=pool_scale, w_branch_a=w_branch_a, w_branch_b=w_branch_b, w_out=w_out, norm_ffn=norm_ffn, w_ffn_gate=w_ffn_gate, w_ffn_up=w_ffn_up, w_ffn_down=w_ffn_down, norm_ple=norm_ple, w_ple_gate=w_ple_gate, w_ple_proj=w_ple_proj, norm_final=norm_final, loss_target=loss_target, m_norm_mix=m_norm_mix, m_w_in=m_w_in, m_w_pool=m_w_pool, m_pool_scale=m_pool_scale, m_w_branch_a=m_w_branch_a, m_w_branch_b=m_w_branch_b, m_w_out=m_w_out, m_norm_ffn=m_norm_ffn, m_w_ffn_gate=m_w_ffn_gate, m_w_ffn_up=m_w_ffn_up, m_w_ffn_down=m_w_ffn_down, m_norm_ple=m_norm_ple, m_w_ple_gate=m_w_ple_gate, m_w_ple_proj=m_w_ple_proj, m_norm_final=m_norm_final, v_norm_mix=v_norm_mix, v_w_in=v_w_in, v_w_pool=v_w_pool, v_pool_scale=v_pool_scale, v_w_branch_a=v_w_branch_a, v_w_branch_b=v_w_branch_b, v_w_out=v_w_out, v_norm_ffn=v_norm_ffn, v_w_ffn_gate=v_w_ffn_gate, v_w_ffn_up=v_w_ffn_up, v_w_ffn_down=v_w_ffn_down, v_norm_ple=v_norm_ple, v_w_ple_gate=v_w_ple_gate, v_w_ple_proj=v_w_ple_proj, v_norm_final=v_norm_final)
    weights = {n: given[n] for n in TWIN_WEIGHTS}
    shared = {n: given[n] for n in SHARED_INPUTS}
    per_example = {n: given[n] for n in ['x', 'p']}
    grad_fn = _jax.value_and_grad(_loss, argnums=(0, 1))

    def one_microbatch(ex, loss_target):
        ex = dict(ex)
        diff = ex.pop(TWIN_DIFF_INPUT)
        return grad_fn(weights, diff, {**shared, **ex}, loss_target)

    if N_MICROBATCH == 1:
        loss, (grad_w, grad_x) = one_microbatch(per_example, given["loss_target"])
    else:
        def body(carry, xs):
            loss_sum, grad_sum = carry
            l_k, (gw_k, gx_k) = one_microbatch(xs[0], xs[1])
            with _jax.named_scope("update"):
                return (loss_sum + l_k, _jax.tree.map(_jnp.add, grad_sum, gw_k)), gx_k

        init = (_jnp.zeros((), _jnp.float32), _jax.tree.map(_jnp.zeros_like, weights))
        (loss, grad_w), grad_x = _jax.lax.scan(body, init, (per_example, given["loss_target"]))
    with _jax.named_scope("update"):
        delta_w, new_m, new_v = {}, {}, {}
        for n in TWIN_WEIGHTS:
            delta_w[n], new_m[n], new_v[n] = _adamw(weights[n], grad_w[n], given["m_" + n], given["v_" + n])
    return (loss, grad_x, *[grad_w[n] for n in TWIN_WEIGHTS], *[delta_w[n] for n in TWIN_WEIGHTS],
            *[new_m[n] for n in TWIN_WEIGHTS], *[new_v[n] for n in TWIN_WEIGHTS])
```

```python
import functools
import math

import jax
import jax.numpy as jnp
from jax import lax
from jax.experimental import pallas as pl
from jax.experimental.pallas import tpu as pltpu

F32 = jnp.float32
BF16 = jnp.bfloat16
MESH = pl.DeviceIdType.MESH

RMS_EPS = 1e-6
POOL_WINDOWS = (2, 4, 8, 16)
POOL_HALO = 16
HEAD_DIM = 64
LANES = 128
ATT_BLOCK = 128
ATT_SCALE = 1.0 / math.sqrt(HEAD_DIM)
ADAM_LR, ADAM_B1, ADAM_B2, ADAM_EPS, ADAM_WD, ADAM_STEP = 0.001, 0.9, 0.999, 1e-08, 0.01, 10
V7X_VMEM_LIMIT_BYTES = 56 * 1024 * 1024
N_CHIPS = 4
N_DEV = 8


def _params(*semantics):
    return pltpu.CompilerParams(dimension_semantics=semantics, vmem_limit_bytes=V7X_VMEM_LIMIT_BYTES)


def _sigmoid(z):
    return 1.0 / (1.0 + jnp.exp(-z))


def _tiled_spec(shape, tm, tn, n_total):
    rows, width = shape
    if rows == 1:
        if width == n_total:
            return pl.BlockSpec((1, tn), lambda i, j: (0, j))
        return pl.BlockSpec((1, width), lambda i, j: (0, 0))
    if width == n_total:
        return pl.BlockSpec((tm, tn), lambda i, j: (i, j))
    assert tn == n_total, "an operand narrower than the output needs whole output rows per tile"
    return pl.BlockSpec((tm, width), lambda i, j: (i, 0))


def _mm(name, a_list, b_list, mode, out_shapes, epilogue=None, extras=(), tm=512, tn=None):
    m_total = a_list[0].shape[0]
    n_total = b_list[0].shape[1] if mode == "nn" else b_list[0].shape[0]
    tn = n_total if tn is None else tn
    tm = min(tm, m_total)
    assert m_total % tm == 0 and n_total % tn == 0
    n_pairs, n_extra = len(a_list), len(extras)
    dims = (((1,), (0,)), ((), ())) if mode == "nn" else (((1,), (1,)), ((), ()))

    def body(*refs):
        a_refs, b_refs = refs[:n_pairs], refs[n_pairs:2 * n_pairs]
        e_refs = refs[2 * n_pairs:2 * n_pairs + n_extra]
        o_refs = refs[2 * n_pairs + n_extra:]
        acc = None
        for a_ref, b_ref in zip(a_refs, b_refs):
            a = a_ref[...]
            if a.dtype != BF16:
                a = a.astype(BF16)
            d = lax.dot_general(a, b_ref[...], dims, preferred_element_type=F32)
            acc = d if acc is None else acc + d
        outs = (acc,) if epilogue is None else epilogue(acc, *[e[...] for e in e_refs])
        for o_ref, o in zip(o_refs, outs):
            o_ref[...] = o.astype(o_ref.dtype)

    in_specs = [pl.BlockSpec((tm, a.shape[1]), lambda i, j: (i, 0)) for a in a_list]
    if mode == "nn":
        in_specs += [pl.BlockSpec((b.shape[0], tn), lambda i, j: (0, j)) for b in b_list]
    else:
        in_specs += [pl.BlockSpec((tn, b.shape[1]), lambda i, j: (j, 0)) for b in b_list]
    in_specs += [_tiled_spec(e.shape, tm, tn, n_total) for e in extras]
    out_specs = [_tiled_spec(o.shape, tm, tn, n_total) for o in out_shapes]
    res = pl.pallas_call(
        body, name=name, grid=(m_total // tm, n_total // tn), in_specs=in_specs, out_specs=out_specs,
        out_shape=list(out_shapes), compiler_params=_params("parallel", "parallel"),
    )(*a_list, *b_list, *extras)
    return res


def _mm_tn(name, a, b, tmm=512):
    m_total, k = a.shape
    n = b.shape[1]
    tmm = min(tmm, m_total)
    assert m_total % tmm == 0

    def body(a_ref, b_ref, o_ref):
        @pl.when(pl.program_id(0) == 0)
        def _():
            o_ref[...] = jnp.zeros_like(o_ref)

        av, bv = a_ref[...], b_ref[...]
        if av.dtype != BF16:
            av = av.astype(BF16)
        if bv.dtype != BF16:
            bv = bv.astype(BF16)
        o_ref[...] += lax.dot_general(av, bv, (((0,), (0,)), ((), ())), preferred_element_type=F32)

    return pl.pallas_call(
        body, name=name, grid=(m_total // tmm,),
        in_specs=[pl.BlockSpec((tmm, k), lambda m: (m, 0)), pl.BlockSpec((tmm, n), lambda m: (m, 0))],
        out_specs=pl.BlockSpec((k, n), lambda m: (0, 0)),
        out_shape=jax.ShapeDtypeStruct((k, n), F32), compiler_params=_params("arbitrary"),
    )(a, b)


def _rows(name, fn, ins, tile_outs, sum_outs=(), tr=512):
    t_total = max(a.shape[0] for a in ins)
    tr = min(tr, t_total)
    assert t_total % tr == 0
    n_in, n_tile = len(ins), len(tile_outs)

    def body(*refs):
        outs = fn(*[r[...] for r in refs[:n_in]])
        for o_ref, o in zip(refs[n_in:n_in + n_tile], outs[:n_tile]):
            o_ref[...] = o.astype(o_ref.dtype)
        if sum_outs:
            @pl.when(pl.program_id(0) == 0)
            def _():
                for s_ref in refs[n_in + n_tile:]:
                    s_ref[...] = jnp.zeros_like(s_ref)

            for s_ref, s in zip(refs[n_in + n_tile:], outs[n_tile:]):
                s_ref[...] += s

    def spec(shape):
        if shape[0] == 1:
            return pl.BlockSpec(shape, lambda i: (0, 0))
        return pl.BlockSpec((tr, shape[1]), lambda i: (i, 0))

    return pl.pallas_call(
        body, name=name, grid=(t_total // tr,), in_specs=[spec(a.shape) for a in ins],
        out_specs=[spec(o.shape) for o in tile_outs] + [spec(s.shape) for s in sum_outs],
        out_shape=list(tile_outs) + list(sum_outs),
        compiler_params=_params("arbitrary" if sum_outs else "parallel"),
    )(*ins)


def _norm_fwd(name, x, gain):
    def fn(xv, g):
        inv = lax.rsqrt(jnp.mean(xv * xv, axis=-1, keepdims=True) + RMS_EPS)
        return (xv * inv * g,)

    return _rows(name, fn, [x, gain], [jax.ShapeDtypeStruct(x.shape, BF16)])[0]


def _norm_bwd(name, dh, x, gain, dres):
    def fn(dhv, xv, g, dr):
        inv = lax.rsqrt(jnp.mean(xv * xv, axis=-1, keepdims=True) + RMS_EPS)
        xn = xv * inv
        dxn = dhv * g
        dx = inv * (dxn - xn * jnp.mean(dxn * xn, axis=-1, keepdims=True)) + dr
        return dx, jnp.sum(dhv * xn, axis=0, keepdims=True)

    d = x.shape[1]
    return _rows(name, fn, [dh, x, gain, dres], [jax.ShapeDtypeStruct(x.shape, F32)],
                 [jax.ShapeDtypeStruct((1, d), F32)])


def _final_loss(x3, target, gain):
    d = x3.shape[1]

    def fn(xv, tv, g):
        inv = lax.rsqrt(jnp.mean(xv * xv, axis=-1, keepdims=True) + RMS_EPS)
        xn = xv * inv
        err = xn * g - tv
        dy = err * (1.0 / d)
        dxn = dy * g
        dx = inv * (dxn - xn * jnp.mean(dxn * xn, axis=-1, keepdims=True))
        return dx, jnp.sum(dy * xn, axis=0, keepdims=True), (0.5 / d) * jnp.sum(err * err, axis=0, keepdims=True)

    return _rows("final_loss", fn, [x3, target, gain], [jax.ShapeDtypeStruct(x3.shape, F32)],
                 [jax.ShapeDtypeStruct((1, d), F32), jax.ShapeDtypeStruct((1, d), F32)])


def _window_counts(t_pos, w):
    return jnp.minimum(t_pos + 1, w).astype(F32)


def _pool_fwd(u, w_pool, scale, tr=512):
    t_total, width = u.shape
    tr = min(tr, t_total)
    n_groups = len(POOL_WINDOWS)
    gdim = width // n_groups
    ext = tr + POOL_HALO

    def body(u_ref, halo_ref, w_ref, s_ref, pooled_ref, ya_ref):
        i = pl.program_id(0)
        halo = jnp.where(i == 0, 0.0, halo_ref[...])
        t_pos = i * tr + lax.broadcasted_iota(jnp.int32, (tr, 1), 0)
        for g, w in enumerate(POOL_WINDOWS):
            cols = slice(g * gdim, (g + 1) * gdim)
            main = u_ref[:, cols]
            win = jnp.concatenate([halo[:, cols], main], axis=0)
            span = 1
            while span < w:
                win = win + pltpu.roll(win, span, 0)
                span *= 2
            pooled = win[POOL_HALO:, :] * (1.0 / _window_counts(t_pos, w)) - main
            pooled_b = pooled.astype(BF16)
            pooled_ref[:, cols] = pooled_b
            mixed = jnp.dot(pooled_b, w_ref[g], preferred_element_type=F32)
            ya_ref[:, cols] = (mixed * s_ref[:, cols]).astype(BF16)

    hb = tr // POOL_HALO
    return pl.pallas_call(
        body, name="pool_fwd", grid=(t_total // tr,),
        in_specs=[pl.BlockSpec((tr, width), lambda i: (i, 0)),
                  pl.BlockSpec((POOL_HALO, width), lambda i: (jnp.maximum(i * hb - 1, 0), 0)),
                  pl.BlockSpec((n_groups, gdim, gdim), lambda i: (0, 0, 0)),
                  pl.BlockSpec((1, width), lambda i: (0, 0))],
        out_specs=[pl.BlockSpec((tr, width), lambda i: (i, 0)), pl.BlockSpec((tr, width), lambda i: (i, 0))],
        out_shape=[jax.ShapeDtypeStruct(u.shape, BF16), jax.ShapeDtypeStruct(u.shape, BF16)],
        compiler_params=_params("parallel"),
    )(u, u, w_pool, scale)


def _pool_bwd(dya, pooled, w_pool, scale, tr=512):
    t_total, width = dya.shape
    tr = min(tr, t_total)
    n_groups = len(POOL_WINDOWS)
    gdim = width // n_groups
    ext = tr + POOL_HALO
    n_tiles = t_total // tr

    def body(d_ref, halo_ref, p_ref, w_ref, s_ref, du_ref, dw_ref, ds_ref):
        i = pl.program_id(0)

        @pl.when(i == 0)
        def _():
            dw_ref[...] = jnp.zeros_like(dw_ref)
            ds_ref[...] = jnp.zeros_like(ds_ref)

        halo = jnp.where(i == n_tiles - 1, 0.0, halo_ref[...])
        t_pos = i * tr + lax.broadcasted_iota(jnp.int32, (ext, 1), 0)
        for g, w in enumerate(POOL_WINDOWS):
            cols = slice(g * gdim, (g + 1) * gdim)
            sc = s_ref[:, cols]
            d_main = d_ref[:, cols]
            pooled_b = p_ref[:, cols]
            mixed = jnp.dot(pooled_b, w_ref[g], preferred_element_type=F32)
            ds_ref[:, cols] += jnp.sum(d_main * mixed, axis=0, keepdims=True)
            dmix = (jnp.concatenate([d_main, halo[:, cols]], axis=0) * sc).astype(BF16)
            dw_ref[g] += lax.dot_general(pooled_b, dmix[:tr, :], (((0,), (0,)), ((), ())),
                                         preferred_element_type=F32)
            dpool = lax.dot_general(dmix, w_ref[g], (((1,), (1,)), ((), ())), preferred_element_type=F32)
            win = dpool * (1.0 / _window_counts(t_pos, w))
            span = 1
            while span < w:
                win = win + pltpu.roll(win, ext - span, 0)
                span *= 2
            du_ref[:, cols] = (win[:tr, :] - dpool[:tr, :]).astype(BF16)

    hb = tr // POOL_HALO
    last_halo = t_total // POOL_HALO - 1
    return pl.pallas_call(
        body, name="pool_bwd", grid=(n_tiles,),
        in_specs=[pl.BlockSpec((tr, width), lambda i: (i, 0)),
                  pl.BlockSpec((POOL_HALO, width), lambda i: (jnp.minimum((i + 1) * hb, last_halo), 0)),
                  pl.BlockSpec((tr, width), lambda i: (i, 0)),
                  pl.BlockSpec((n_groups, gdim, gdim), lambda i: (0, 0, 0)),
                  pl.BlockSpec((1, width), lambda i: (0, 0))],
        out_specs=[pl.BlockSpec((tr, width), lambda i: (i, 0)),
                   pl.BlockSpec((n_groups, gdim, gdim), lambda i: (0, 0, 0)),
                   pl.BlockSpec((1, width), lambda i: (0, 0))],
        out_shape=[jax.ShapeDtypeStruct(dya.shape, BF16), jax.ShapeDtypeStruct((n_groups, gdim, gdim), F32),
                   jax.ShapeDtypeStruct((1, width), F32)],
        compiler_params=_params("arbitrary"),
    )(dya, dya, pooled, w_pool, scale)


def _head_masks():
    lane = lax.broadcasted_iota(jnp.int32, (1, LANES), 1)
    return lane < HEAD_DIM


def _stack_heads(tile, first):
    zero = jnp.zeros_like(tile)
    return jnp.concatenate([jnp.where(first, tile, zero), jnp.where(first, zero, tile)], axis=0)


def _split_bf16(v):
    hi = v.astype(BF16)
    lo = (v - hi.astype(F32)).astype(BF16)
    return hi, lo


def _block_scores(q2, kb, t_pos, k_start):
    z = lax.dot_general(q2, kb, (((1,), (1,)), ((), ())), preferred_element_type=F32)
    k_pos = k_start + lax.broadcasted_iota(jnp.int32, (1, ATT_BLOCK), 1)
    mask = k_pos < t_pos
    e = jnp.exp(-jnp.abs(z))
    log_fail = jnp.where(mask, -(jnp.maximum(z, 0.0) + jnp.log(1.0 + e)), 0.0)
    return z, mask, log_fail, e


def _tri(upper):
    r = lax.broadcasted_iota(jnp.int32, (ATT_BLOCK, ATT_BLOCK), 0)
    c = lax.broadcasted_iota(jnp.int32, (ATT_BLOCK, ATT_BLOCK), 1)
    return jnp.where(r > c if upper else r < c, 1.0, 0.0).astype(BF16)


def _scan_block(v, tri):
    hi, lo = _split_bf16(v)
    return (jnp.dot(hi, tri, preferred_element_type=F32) + jnp.dot(lo, tri, preferred_element_type=F32))


def _lane_bcast(col):
    return jnp.broadcast_to(col, (col.shape[0], ATT_BLOCK))


def _attn_fwd(q_src, q_col, kv_src, k_col, v_col, n_pairs=4):
    t_total = q_src.shape[0]
    blk = ATT_BLOCK
    n_blocks = t_total // blk

    def body(q_ref, k_ref, v_ref, o_ref):
        i = pl.program_id(1)
        first = _head_masks()
        q2 = _stack_heads(q_ref[...] * ATT_SCALE, first)
        row = lax.broadcasted_iota(jnp.int32, (blk, 1), 0)
        t_pos = i * blk + jnp.concatenate([row, row], axis=0)
        suffix_tri = _tri(upper=True)

        def step(jj, carry):
            acc, right = carry
            k_start = pl.multiple_of((i - jj) * blk, blk)
            kb = k_ref[pl.ds(k_start, blk), :]
            vb = v_ref[pl.ds(k_start, blk), :]
            z, mask, log_fail, _ = _block_scores(q2, kb, t_pos, k_start)
            suffix = _scan_block(log_fail, suffix_tri)
            a = jnp.exp(jnp.where(mask, z + log_fail + suffix + right, -1e30)).astype(BF16)
            right = right + _lane_bcast(suffix[:, 0:1] + log_fail[:, 0:1])
            a_cat = jnp.concatenate([a[:blk], a[blk:]], axis=1)
            acc = acc + jnp.dot(a_cat, _stack_heads(vb, first), preferred_element_type=F32)
            return acc, right

        acc, _ = lax.fori_loop(0, i + 1, step, (jnp.zeros((blk, LANES), F32), jnp.zeros((2 * blk, blk), F32)))
        o_ref[...] = acc.astype(BF16)

    return pl.pallas_call(
        body, name="attn_fwd", grid=(n_pairs, n_blocks),
        in_specs=[pl.BlockSpec((blk, LANES), lambda h, i: (i, q_col + h)),
                  pl.BlockSpec((t_total, LANES), lambda h, i: (0, k_col + h)),
                  pl.BlockSpec((t_total, LANES), lambda h, i: (0, v_col + h))],
        out_specs=pl.BlockSpec((blk, LANES), lambda h, i: (i, h)),
        out_shape=jax.ShapeDtypeStruct((t_total, n_pairs * LANES), BF16),
        compiler_params=_params("parallel", "parallel"),
    )(q_src, kv_src, kv_src)


def _attn_bwd(q_src, q_col, kv_src, k_col, v_col, dy, n_pairs=4):
    t_total = q_src.shape[0]
    blk = ATT_BLOCK
    n_blocks = t_total // blk

    def body(q_ref, dy_ref, k_ref, v_ref, dq_ref, dk_ref, dv_ref, g_s, sig_s, dk_acc, dv_acc):
        i = pl.program_id(1)

        @pl.when(i == 0)
        def _():
            dk_acc[...] = jnp.zeros_like(dk_acc)
            dv_acc[...] = jnp.zeros_like(dv_acc)

        first = _head_masks()
        q2 = _stack_heads(q_ref[...] * ATT_SCALE, first)
        dy2 = _stack_heads(dy_ref[...], first)
        row = lax.broadcasted_iota(jnp.int32, (blk, 1), 0)
        t_pos = i * blk + jnp.concatenate([row, row], axis=0)
        suffix_tri = _tri(upper=True)
        prefix_tri = _tri(upper=False)

        def sweep1(jj, right):
            k_start = pl.multiple_of((i - jj) * blk, blk)
            kb = k_ref[pl.ds(k_start, blk), :]
            vb = v_ref[pl.ds(k_start, blk), :]
            z, mask, log_fail, e = _block_scores(q2, kb, t_pos, k_start)
            suffix = _scan_block(log_fail, suffix_tri)
            a = jnp.exp(jnp.where(mask, z + log_fail + suffix + right, -1e30))
            da = lax.dot_general(dy2, vb, (((1,), (1,)), ((), ())), preferred_element_type=F32)
            g_s[:, pl.ds(k_start, blk)] = da * a
            sig_s[:, pl.ds(k_start, blk)] = jnp.where(z >= 0.0, 1.0, e) / (1.0 + e)
            dv_acc[pl.ds(k_start, blk), :] += lax.dot_general(
                a.astype(BF16), dy2, (((0,), (0,)), ((), ())), preferred_element_type=F32)
            return right + _lane_bcast(suffix[:, 0:1] + log_fail[:, 0:1])

        lax.fori_loop(0, i + 1, sweep1, jnp.zeros((2 * blk, blk), F32))

        def sweep2(j, carry):
            dq, left = carry
            k_start = pl.multiple_of(j * blk, blk)
            kb = k_ref[pl.ds(k_start, blk), :]
            g = g_s[:, pl.ds(k_start, blk)]
            sig = sig_s[:, pl.ds(k_start, blk)]
            prefix = _scan_block(g, prefix_tri)
            k_pos = k_start + lax.broadcasted_iota(jnp.int32, (1, blk), 1)
            dz = jnp.where(k_pos < t_pos, g * (1.0 - sig) - sig * (prefix + left), 0.0).astype(BF16)
            left = left + _lane_bcast(prefix[:, blk - 1:blk] + g[:, blk - 1:blk])
            dz_cat = jnp.concatenate([dz[:blk], dz[blk:]], axis=1)
            dq = dq + jnp.dot(dz_cat, _stack_heads(kb, first), preferred_element_type=F32)
            dk_acc[pl.ds(k_start, blk), :] += lax.dot_general(
                dz, q2, (((0,), (0,)), ((), ())), preferred_element_type=F32)
            return dq, left

        dq, _ = lax.fori_loop(0, i + 1, sweep2, (jnp.zeros((blk, LANES), F32), jnp.zeros((2 * blk, blk), F32)))
        dq_ref[...] = (dq * ATT_SCALE).astype(BF16)

        @pl.when(i == n_blocks - 1)
        def _():
            dk_ref[...] = dk_acc[...].astype(BF16)
            dv_ref[...] = dv_acc[...].astype(BF16)

    out = jax.ShapeDtypeStruct((t_total, n_pairs * LANES), BF16)
    return pl.pallas_call(
        body, name="attn_bwd", grid=(n_pairs, n_blocks),
        in_specs=[pl.BlockSpec((blk, LANES), lambda h, i: (i, q_col + h)),
                  pl.BlockSpec((blk, LANES), lambda h, i: (i, h)),
                  pl.BlockSpec((t_total, LANES), lambda h, i: (0, k_col + h)),
                  pl.BlockSpec((t_total, LANES), lambda h, i: (0, v_col + h))],
        out_specs=[pl.BlockSpec((blk, LANES), lambda h, i: (i, h)),
                   pl.BlockSpec((t_total, LANES), lambda h, i: (0, h)),
                   pl.BlockSpec((t_total, LANES), lambda h, i: (0, h))],
        out_shape=[out, out, out],
        scratch_shapes=[pltpu.VMEM((2 * blk, t_total), F32), pltpu.VMEM((2 * blk, t_total), F32),
                        pltpu.VMEM((t_total, LANES), F32), pltpu.VMEM((t_total, LANES), F32)],
        compiler_params=_params("arbitrary", "arbitrary"),
    )(q_src, dy, kv_src, kv_src)


def _adamw(name, w, g, m, v):
    def fn(wv, gv, mv, vv):
        mn = ADAM_B1 * mv + (1.0 - ADAM_B1) * gv
        vn = ADAM_B2 * vv + (1.0 - ADAM_B2) * (gv * gv)
        m_hat = mn / (1.0 - ADAM_B1 ** ADAM_STEP)
        v_hat = vn / (1.0 - ADAM_B2 ** ADAM_STEP)
        return -ADAM_LR * (m_hat / (jnp.sqrt(v_hat) + ADAM_EPS) + ADAM_WD * wv), mn, vn

    rows = w.shape[0]
    tr = _row_tile(rows)
    shp = jax.ShapeDtypeStruct(w.shape, F32)
    if rows == 1:
        def body(w_ref, g_ref, m_ref, v_ref, d_ref, mo_ref, vo_ref):
            d, mn, vn = fn(w_ref[...], g_ref[...], m_ref[...], v_ref[...])
            d_ref[...], mo_ref[...], vo_ref[...] = d, mn, vn

        return pl.pallas_call(body, name=name, out_shape=[shp, shp, shp])(w, g, m, v)
    return _rows(name, fn, [w, g, m, v], [shp, shp, shp], tr=tr)


def _place():
    return lax.axis_index("x"), lax.axis_index("y"), lax.axis_index("c")


def _other_chips(x, y):
    return [(1 - x, y), (x, 1 - y), (1 - x, 1 - y)]


ANY = pl.BlockSpec(memory_space=pl.ANY)


def _all_gather_weights(shards):
    n_w = len(shards)

    def body(*refs):
        ins, outs = refs[:n_w], refs[n_w:2 * n_w]
        send_sems, recv_sems, pass_send, pass_recv, local_sems = refs[2 * n_w:]
        x, y, c = _place()
        my_chip = 2 * x + y
        chips = _other_chips(x, y)
        local = [pltpu.make_async_copy(ins[w], outs[w].at[my_chip], local_sems.at[w]) for w in range(n_w)]
        for cp in local:
            cp.start()

        def half(w, core):
            h = shards[w].shape[0] // 2
            return pl.ds(core * h, h)

        sends = []
        for p, (ox, oy) in enumerate(chips):
            for w in range(n_w):
                sends.append(pltpu.make_async_remote_copy(
                    src_ref=ins[w].at[half(w, c)], dst_ref=outs[w].at[my_chip, half(w, c)],
                    send_sem=send_sems.at[p, w], recv_sem=recv_sems.at[p, w],
                    device_id=(ox, oy, c), device_id_type=MESH))
        for cp in sends:
            cp.start()
        passes = []
        for p, (ox, oy) in enumerate(chips):
            chip = 2 * ox + oy
            for w in range(n_w):
                landed = outs[w].at[chip, half(w, c)]
                pltpu.make_async_remote_copy(
                    src_ref=landed, dst_ref=landed, send_sem=send_sems.at[p, w], recv_sem=recv_sems.at[p, w],
                    device_id=(ox, oy, c), device_id_type=MESH).wait_recv()
                cp = pltpu.make_async_remote_copy(
                    src_ref=landed, dst_ref=landed, send_sem=pass_send.at[p, w], recv_sem=pass_recv.at[p, w],
                    device_id=(x, y, 1 - c), device_id_type=MESH)
                cp.start()
                passes.append(cp)
        for p, (ox, oy) in enumerate(chips):
            chip = 2 * ox + oy
            for w in range(n_w):
                theirs = outs[w].at[chip, half(w, 1 - c)]
                pltpu.make_async_remote_copy(
                    src_ref=theirs, dst_ref=theirs, send_sem=pass_send.at[p, w], recv_sem=pass_recv.at[p, w],
                    device_id=(x, y, 1 - c), device_id_type=MESH).wait_recv()
        for cp in sends + passes:
            cp.wait_send()
        for cp in local:
            cp.wait()

    return pl.pallas_call(
        body, name="all_gather_weights", in_specs=[ANY] * n_w, out_specs=[ANY] * n_w,
        out_shape=[jax.ShapeDtypeStruct((N_CHIPS,) + s.shape, s.dtype) for s in shards],
        scratch_shapes=[pltpu.SemaphoreType.DMA((3, n_w)), pltpu.SemaphoreType.DMA((3, n_w)),
                        pltpu.SemaphoreType.DMA((3, n_w)), pltpu.SemaphoreType.DMA((3, n_w)),
                        pltpu.SemaphoreType.DMA((n_w,))],
    )(*shards)


def _pair_exchange(grads):
    n_w = len(grads)

    def halves(w):
        return grads[w].shape[1] // 2

    def body(*refs):
        ins, mine, theirs = refs[:n_w], refs[n_w:2 * n_w], refs[2 * n_w:3 * n_w]
        send_sems, recv_sems, local_sems = refs[3 * n_w:]
        x, y, c = _place()
        local = [pltpu.make_async_copy(ins[w].at[:, pl.ds(c * halves(w), halves(w)), :], mine[w], local_sems.at[w])
                 for w in range(n_w)]
        sends = [pltpu.make_async_remote_copy(
            src_ref=ins[w].at[:, pl.ds((1 - c) * halves(w), halves(w)), :], dst_ref=theirs[w],
            send_sem=send_sems.at[w], recv_sem=recv_sems.at[w], device_id=(x, y, 1 - c), device_id_type=MESH)
            for w in range(n_w)]
        for cp in local + sends:
            cp.start()
        for cp in sends:
            cp.wait_recv()
        for cp in sends:
            cp.wait_send()
        for cp in local:
            cp.wait()

    shapes = [jax.ShapeDtypeStruct((N_CHIPS, halves(w), grads[w].shape[2]), F32) for w in range(n_w)]
    res = pl.pallas_call(
        body, name="pair_exchange", in_specs=[ANY] * n_w, out_specs=[ANY] * (2 * n_w), out_shape=shapes + shapes,
        scratch_shapes=[pltpu.SemaphoreType.DMA((n_w,)), pltpu.SemaphoreType.DMA((n_w,)),
                        pltpu.SemaphoreType.DMA((n_w,))],
    )(*grads)
    return res[:n_w], res[n_w:]


def _chip_exchange(pair_sums):
    n_w = len(pair_sums)

    def body(*refs):
        ins, outs = refs[:n_w], refs[n_w:2 * n_w]
        send_sems, recv_sems, local_sems = refs[2 * n_w:]
        x, y, c = _place()
        my_chip = 2 * x + y
        chips = _other_chips(x, y)
        local = [pltpu.make_async_copy(ins[w].at[my_chip], outs[w].at[my_chip], local_sems.at[w]) for w in range(n_w)]
        sends = []
        for p, (ox, oy) in enumerate(chips):
            for w in range(n_w):
                sends.append(pltpu.make_async_remote_copy(
                    src_ref=ins[w].at[2 * ox + oy], dst_ref=outs[w].at[my_chip],
                    send_sem=send_sems.at[p, w], recv_sem=recv_sems.at[p, w],
                    device_id=(ox, oy, c), device_id_type=MESH))
        for cp in local + sends:
            cp.start()
        for p, (ox, oy) in enumerate(chips):
            for w in range(n_w):
                landed = outs[w].at[2 * ox + oy]
                pltpu.make_async_remote_copy(
                    src_ref=landed, dst_ref=landed, send_sem=send_sems.at[p, w], recv_sem=recv_sems.at[p, w],
                    device_id=(ox, oy, c), device_id_type=MESH).wait_recv()
        for cp in sends:
            cp.wait_send()
        for cp in local:
            cp.wait()

    return pl.pallas_call(
        body, name="chip_exchange", in_specs=[ANY] * n_w, out_specs=[ANY] * n_w,
        out_shape=[jax.ShapeDtypeStruct(s.shape, s.dtype) for s in pair_sums],
        scratch_shapes=[pltpu.SemaphoreType.DMA((3, n_w)), pltpu.SemaphoreType.DMA((3, n_w)),
                        pltpu.SemaphoreType.DMA((n_w,))],
    )(*pair_sums)


def _pair_share(halves):
    n_w = len(halves)

    def body(*refs):
        ins, outs = refs[:n_w], refs[n_w:2 * n_w]
        send_sems, recv_sems, local_sems = refs[2 * n_w:]
        x, y, c = _place()

        def rows(w, core):
            h = halves[w].shape[0]
            return pl.ds(core * h, h)

        local = [pltpu.make_async_copy(ins[w], outs[w].at[rows(w, c)], local_sems.at[w]) for w in range(n_w)]
        sends = [pltpu.make_async_remote_copy(
            src_ref=ins[w], dst_ref=outs[w].at[rows(w, c)], send_sem=send_sems.at[w], recv_sem=recv_sems.at[w],
            device_id=(x, y, 1 - c), device_id_type=MESH) for w in range(n_w)]
        for cp in local + sends:
            cp.start()
        for w in range(n_w):
            theirs = outs[w].at[rows(w, 1 - c)]
            pltpu.make_async_remote_copy(
                src_ref=theirs, dst_ref=theirs, send_sem=send_sems.at[w], recv_sem=recv_sems.at[w],
                device_id=(x, y, 1 - c), device_id_type=MESH).wait_recv()
        for cp in sends:
            cp.wait_send()
        for cp in local:
            cp.wait()

    return pl.pallas_call(
        body, name="pair_share", in_specs=[ANY] * n_w, out_specs=[ANY] * n_w,
        out_shape=[jax.ShapeDtypeStruct((2 * h.shape[0], h.shape[1]), F32) for h in halves],
        scratch_shapes=[pltpu.SemaphoreType.DMA((n_w,)), pltpu.SemaphoreType.DMA((n_w,)),
                        pltpu.SemaphoreType.DMA((n_w,))],
    )(*halves)


def _all_reduce_small(vec):
    rows = vec.shape[0]

    def body(v_ref, o_ref, slots, send_sems, recv_sems):
        x, y, c = _place()
        me = 4 * x + 2 * y + c
        slots[me] = v_ref[...]
        sends = []
        for k in range(1, N_DEV):
            peer = (x ^ (k >> 2), y ^ ((k >> 1) & 1), c ^ (k & 1))
            sends.append(pltpu.make_async_remote_copy(
                src_ref=v_ref, dst_ref=slots.at[me], send_sem=send_sems.at[k - 1], recv_sem=recv_sems.at[k - 1],
                device_id=peer, device_id_type=MESH))
        for cp in sends:
            cp.start()
        for k in range(1, N_DEV):
            px, py, pc = x ^ (k >> 2), y ^ ((k >> 1) & 1), c ^ (k & 1)
            landed = slots.at[4 * px + 2 * py + pc]
            pltpu.make_async_remote_copy(
                src_ref=landed, dst_ref=landed, send_sem=send_sems.at[k - 1], recv_sem=recv_sems.at[k - 1],
                device_id=(px, py, pc), device_id_type=MESH).wait_recv()
        for cp in sends:
            cp.wait_send()
        total = slots[0]
        for d in range(1, N_DEV):
            total = total + slots[d]
        o_ref[...] = total

    vm = pl.BlockSpec(memory_space=pltpu.VMEM)
    return pl.pallas_call(
        body, name="all_reduce_small", in_specs=[vm], out_specs=vm, out_shape=jax.ShapeDtypeStruct(vec.shape, F32),
        scratch_shapes=[pltpu.VMEM((N_DEV, rows, LANES), F32), pltpu.SemaphoreType.DMA((N_DEV - 1,)),
                        pltpu.SemaphoreType.DMA((N_DEV - 1,))],
    )(vec)


def _add_cast(name, a, b):
    n, r, c = a.shape
    out = _rows(name, lambda av, bv: (av + bv,), [a.reshape(n * r, c), b.reshape(n * r, c)],
                [jax.ShapeDtypeStruct((n * r, c), BF16)], tr=_row_tile(n * r))[0]
    return out.reshape(n, r, c)


def _row_tile(rows):
    for tr in (256, 128, 64, 32, 16):
        if rows % tr == 0:
            return tr
    return rows


def _sum_chips(name, q):
    n, r, c = q.shape
    tr = _row_tile(r)

    def body(q_ref, o_ref):
        total = q_ref[0].astype(F32)
        for j in range(1, n):
            total = total + q_ref[j].astype(F32)
        o_ref[...] = total

    return pl.pallas_call(
        body, name=name, grid=(r // tr,), in_specs=[pl.BlockSpec((n, tr, c), lambda i: (0, i, 0))],
        out_specs=pl.BlockSpec((tr, c), lambda i: (i, 0)), out_shape=jax.ShapeDtypeStruct((r, c), F32),
        compiler_params=_params("parallel"),
    )(q)


BIG = ("w_in", "w_branch_a", "w_branch_b", "w_out", "w_ffn_gate", "w_ffn_up", "w_ffn_down", "w_ple_gate", "w_ple_proj")
COLUMN_SHARDED = ("w_in", "w_branch_a", "w_branch_b", "w_ffn_gate", "w_ffn_up", "w_ple_proj")
SMALL = ("norm_mix", "w_pool", "pool_scale", "norm_ffn", "norm_ple", "norm_final")


def _join_columns(w4):
    return jnp.concatenate([w4[j] for j in range(N_CHIPS)], axis=1)


def _split_columns(g):
    k, n = g.shape
    return g.reshape(k, N_CHIPS, n // N_CHIPS).transpose(1, 0, 2)


def _sds(shape, dtype):
    return jax.ShapeDtypeStruct(shape, dtype)


def _local_step(x, p, target, wf, small):
    t, d = x.shape
    w_in, w_gate, w_up, w_down = wf["w_in"], wf["w_ffn_gate"], wf["w_ffn_up"], wf["w_ffn_down"]
    w_a, w_b, w_pp = _join_columns(wf["w_branch_a"]), _join_columns(wf["w_branch_b"]), _join_columns(wf["w_ple_proj"])
    w_out = wf["w_out"].reshape(d, d)
    w_pg = wf["w_ple_gate"].reshape(d, d)
    w_pool_b = small["w_pool"].astype(BF16)
    dp = w_pool_b.shape[0] * w_pool_b.shape[1]
    dff = w_gate.shape[2]

    h1 = _norm_fwd("norm_mix", x, small["norm_mix"])
    u, q = _mm("proj_uq", [h1], [w_in[0]], "nn", [_sds((t, dp), F32), _sds((t, dp), BF16)],
               epilogue=lambda acc: (acc[:, :dp], acc[:, dp:]))
    kv, = _mm("proj_kv", [h1], [w_in[1]], "nn", [_sds((t, d), BF16)])
    ga, = _mm("proj_ga", [h1], [w_in[2]], "nn", [_sds((t, d), F32)])
    gb, = _mm("proj_gb", [h1], [w_in[3]], "nn", [_sds((t, d), F32)])
    pooled, ya = _pool_fwd(u, w_pool_b, small["pool_scale"])
    n_pairs = dp // LANES
    yb = _attn_fwd(q, 0, kv, 0, n_pairs, n_pairs)
    ta, = _mm("branch_a", [ya], [w_a], "nn", [_sds((t, d), F32)])
    tb, merged = _mm("branch_b_merge", [yb], [w_b], "nn", [_sds((t, d), F32), _sds((t, d), BF16)],
                     extras=[ta, ga, gb],
                     epilogue=lambda acc, tav, gav, gbv: (acc, _sigmoid(gav) * tav + _sigmoid(gbv) * acc))
    x1, = _mm("mix_out", [merged], [w_out], "nn", [_sds((t, d), F32)], extras=[x], epilogue=lambda acc, xv: (acc + xv,))
    h2 = _norm_fwd("norm_ffn", x1, small["norm_ffn"])
    gates, ups, acts = [], [], []
    for j in range(N_CHIPS):
        gj, = _mm(f"ffn_gate{j}", [h2], [w_gate[j]], "nn", [_sds((t, dff), F32)])
        uj, aj = _mm(f"ffn_up{j}", [h2], [w_up[j]], "nn", [_sds((t, dff), F32), _sds((t, dff), BF16)], extras=[gj],
                     epilogue=lambda acc, gv: (acc, gv * _sigmoid(gv) * acc))
        gates.append(gj), ups.append(uj), acts.append(aj)
    x2, = _mm("ffn_down", acts, [w_down[j] for j in range(N_CHIPS)], "nn", [_sds((t, d), F32)], extras=[x1],
              epilogue=lambda acc, xv: (acc + xv,))
    h3 = _norm_fwd("norm_ple", x2, small["norm_ple"])
    gp, = _mm("ple_gate", [h3], [w_pg], "nn", [_sds((t, d), F32)])
    pp, x3 = _mm("ple_proj", [p], [w_pp], "nn", [_sds((t, d), F32), _sds((t, d), F32)], extras=[gp, x2],
                 epilogue=lambda acc, gv, xv: (acc, xv + _sigmoid(gv) * acc))
    (dx3,), (d_norm_final, loss_row) = _split2(_final_loss(x3, target, small["norm_final"].reshape(1, d)), 1)

    def ple_bwd(dxv, gv, pv):
        s = _sigmoid(gv)
        return dxv * s, dxv * pv * s * (1.0 - s)

    d_pp, d_gp = _rows("ple_bwd", ple_bwd, [dx3, gp, pp], [_sds((t, d), BF16), _sds((t, d), BF16)])
    g_w_pp = _mm_tn("g_ple_proj", p, d_pp)
    g_w_pg = _mm_tn("g_ple_gate", h3, d_gp)
    dh3, = _mm("d_h3", [d_gp], [w_pg], "nt", [_sds((t, d), F32)])
    (dx2,), (d_norm_ple,) = _split2(_norm_bwd("norm_ple_bwd", dh3, x2, small["norm_ple"], dx3), 1)

    def ffn_bwd(acc, gv, uv):
        s = _sigmoid(gv)
        return acc * uv * (s * (1.0 + gv * (1.0 - s))), acc * (gv * s)

    d_gates, d_ups, g_w_gate, g_w_up, g_w_down = [], [], [], [], []
    for j in range(N_CHIPS):
        dgj, duj = _mm(f"d_act{j}", [dx2], [w_down[j]], "nt", [_sds((t, dff), BF16), _sds((t, dff), BF16)],
                       extras=[gates[j], ups[j]], epilogue=ffn_bwd)
        d_gates.append(dgj), d_ups.append(duj)
        g_w_down.append(_mm_tn(f"g_ffn_down{j}", acts[j], dx2))
        g_w_gate.append(_mm_tn(f"g_ffn_gate{j}", h2, dgj))
        g_w_up.append(_mm_tn(f"g_ffn_up{j}", h2, duj))
    dh2a, = _mm("d_h2_gate", d_gates, [w_gate[j] for j in range(N_CHIPS)], "nt", [_sds((t, d), F32)])
    dh2, = _mm("d_h2_up", d_ups, [w_up[j] for j in range(N_CHIPS)], "nt", [_sds((t, d), F32)], extras=[dh2a],
               epilogue=lambda acc, prev: (acc + prev,))
    (dx1,), (d_norm_ffn,) = _split2(_norm_bwd("norm_ffn_bwd", dh2, x1, small["norm_ffn"], dx2), 1)

    def merge_bwd(acc, tav, tbv, gav, gbv):
        sa, sb = _sigmoid(gav), _sigmoid(gbv)
        return acc * sa, acc * sb, acc * tav * sa * (1.0 - sa), acc * tbv * sb * (1.0 - sb)

    d_ta, d_tb, d_ga, d_gb = _mm("d_merged", [dx1], [w_out], "nt", [_sds((t, d), BF16)] * 4,
                                 extras=[ta, tb, ga, gb], epilogue=merge_bwd)
    g_w_out = _mm_tn("g_w_out", merged, dx1)
    g_w_a = _mm_tn("g_branch_a", ya, d_ta)
    g_w_b = _mm_tn("g_branch_b", yb, d_tb)
    d_ya, = _mm("d_ya", [d_ta], [w_a], "nt", [_sds((t, dp), F32)])
    d_yb, = _mm("d_yb", [d_tb], [w_b], "nt", [_sds((t, dp), BF16)])
    d_u, g_w_pool, d_pool_scale = _pool_bwd(d_ya, pooled, w_pool_b, small["pool_scale"])
    d_q, d_k, d_v = _attn_bwd(q, 0, kv, 0, n_pairs, d_yb, n_pairs)
    d_proj = [jnp.concatenate([d_u, d_q], axis=1), jnp.concatenate([d_k, d_v], axis=1), d_ga, d_gb]
    g_w_in = [_mm_tn(f"g_w_in{j}", h1, d_proj[j]) for j in range(N_CHIPS)]
    dh1, = _mm("d_h1", d_proj, [w_in[j] for j in range(N_CHIPS)], "nt", [_sds((t, d), F32)])
    (grad_x,), (d_norm_mix,) = _split2(_norm_bwd("norm_mix_bwd", dh1, x, small["norm_mix"], dx1), 1)

    big = {
        "w_in": jnp.stack(g_w_in), "w_branch_a": _split_columns(g_w_a), "w_branch_b": _split_columns(g_w_b),
        "w_out": g_w_out.reshape(wf["w_out"].shape), "w_ffn_gate": jnp.stack(g_w_gate), "w_ffn_up": jnp.stack(g_w_up),
        "w_ffn_down": jnp.stack(g_w_down), "w_ple_gate": g_w_pg.reshape(wf["w_ple_gate"].shape),
        "w_ple_proj": _split_columns(g_w_pp),
    }
    small_g = {"norm_mix": d_norm_mix, "w_pool": g_w_pool, "pool_scale": d_pool_scale, "norm_ffn": d_norm_ffn,
               "norm_ple": d_norm_ple, "norm_final": d_norm_final}
    return grad_x, big, small_g, loss_row


def _split2(res, n):
    return res[:n], res[n:]


def _pack_small(small_g, loss_row):
    parts, layout = [], []
    for name in SMALL + ("loss",):
        v = (loss_row if name == "loss" else small_g[name]).reshape(-1, LANES)
        pad = (-v.shape[0]) % 8
        if pad:
            v = jnp.concatenate([v, jnp.zeros((pad, LANES), F32)], axis=0)
        layout.append((name, sum(q.shape[0] for q in parts), v.shape[0]))
        parts.append(v)
    return jnp.concatenate(parts, axis=0), layout


def kernel(x, p, norm_mix, w_in, w_pool, pool_scale, w_branch_a, w_branch_b, w_out, norm_ffn, w_ffn_gate, w_ffn_up, w_ffn_down, norm_ple, w_ple_gate, w_ple_proj, norm_final, loss_target, m_norm_mix, m_w_in, m_w_pool, m_pool_scale, m_w_branch_a, m_w_branch_b, m_w_out, m_norm_ffn, m_w_ffn_gate, m_w_ffn_up, m_w_ffn_down, m_norm_ple, m_w_ple_gate, m_w_ple_proj, m_norm_final, v_norm_mix, v_w_in, v_w_pool, v_pool_scale, v_w_branch_a, v_w_branch_b, v_w_out, v_norm_ffn, v_w_ffn_gate, v_w_ffn_up, v_w_ffn_down, v_norm_ple, v_w_ple_gate, v_w_ple_proj, v_norm_final):
    given = dict(locals())
    names = BIG + SMALL
    order = ("norm_mix", "w_in", "w_pool", "pool_scale", "w_branch_a", "w_branch_b", "w_out", "norm_ffn", "w_ffn_gate",
             "w_ffn_up", "w_ffn_down", "norm_ple", "w_ple_gate", "w_ple_proj", "norm_final")
    t, d = x.shape[1], x.shape[2]
    shard = {n: given[n][0] for n in BIG}
    small = {"norm_mix": norm_mix, "w_pool": w_pool[0], "pool_scale": pool_scale, "norm_ffn": norm_ffn,
             "norm_ple": norm_ple, "norm_final": norm_final}

    gathered = _all_gather_weights([shard[n].astype(BF16) for n in BIG])
    wf = dict(zip(BIG, gathered))
    grad_x, big_g, small_g, loss_row = _local_step(
        x.reshape(t, d), p.reshape(t, p.shape[-1]), loss_target.reshape(t, d), wf, small)

    mine, theirs = _pair_exchange([big_g[n] for n in BIG])
    pair_sums = [_add_cast(f"pair_sum_{n}", a, b) for n, a, b in zip(BIG, mine, theirs)]
    landed = _chip_exchange(pair_sums)
    halves = [_sum_chips(f"chip_sum_{n}", q) for n, q in zip(BIG, landed)]
    grads = dict(zip(BIG, _pair_share(halves)))

    packed, layout = _pack_small(small_g, loss_row)
    reduced = _all_reduce_small(packed)
    for name, start, rows in layout:
        if name == "loss":
            loss = jnp.sum(reduced[start:start + rows])
        else:
            n_el = small[name].size
            grads[name] = reduced[start:start + rows].reshape(-1)[:n_el]

    deltas, new_m, new_v = {}, {}, {}
    for n in order:
        w = shard[n] if n in BIG else small[n]
        shape2 = w.shape if w.ndim == 2 else ((1, w.shape[0]) if w.ndim == 1 else (w.shape[0] * w.shape[1], w.shape[2]))
        g2 = grads[n].reshape(shape2)
        dl, mn, vn = _adamw(f"adamw_{n}", w.reshape(shape2), g2, given["m_" + n].reshape(shape2),
                            given["v_" + n].reshape(shape2))
        full = given[n].shape
        grads[n], deltas[n], new_m[n], new_v[n] = g2.reshape(full), dl.reshape(full), mn.reshape(full), vn.reshape(full)

    return (loss, grad_x.reshape(x.shape), *[grads[n] for n in order], *[deltas[n] for n in order],
            *[new_m[n] for n in order], *[new_v[n] for n in order])
```

```python
import functools
import math

import jax
import jax.numpy as jnp
from jax import lax
from jax.experimental import pallas as pl
from jax.experimental.pallas import tpu as pltpu

F32 = jnp.float32
BF16 = jnp.bfloat16
MESH = pl.DeviceIdType.MESH

RMS_EPS = 1e-6
POOL_WINDOWS = (2, 4, 8, 16)
POOL_HALO = 16
HEAD_DIM = 64
LANES = 128
ATT_BLOCK = 128
ATT_CHUNK = 256
ATT_SLAB = 256
ATT_SCALE = 1.0 / math.sqrt(HEAD_DIM)
ATT_EXIT_BELOW = -104.0
ADAM_LR, ADAM_B1, ADAM_B2, ADAM_EPS, ADAM_WD, ADAM_STEP = 0.001, 0.9, 0.999, 1e-08, 0.01, 10
V7X_VMEM_LIMIT_BYTES = 56 * 1024 * 1024
N_CHIPS = 4
N_DEV = 8


def _params(*semantics):
    return pltpu.CompilerParams(dimension_semantics=semantics, vmem_limit_bytes=V7X_VMEM_LIMIT_BYTES)


def _sigmoid(z):
    return 1.0 / (1.0 + jnp.exp(-z))


def _tiled_spec(shape, tm, tn, n_total):
    rows, width = shape
    if rows == 1:
        if width == n_total:
            return pl.BlockSpec((1, tn), lambda i, j: (0, j))
        return pl.BlockSpec((1, width), lambda i, j: (0, 0))
    if width == n_total:
        return pl.BlockSpec((tm, tn), lambda i, j: (i, j))
    assert tn == n_total, "an operand narrower than the output needs whole output rows per tile"
    return pl.BlockSpec((tm, width), lambda i, j: (i, 0))


def _mm(name, a_list, b_list, mode, out_shapes, epilogue=None, extras=(), tm=512, tn=None):
    m_total = a_list[0].shape[0]
    n_total = b_list[0].shape[1] if mode == "nn" else b_list[0].shape[0]
    tn = n_total if tn is None else tn
    tm = min(tm, m_total)
    assert m_total % tm == 0 and n_total % tn == 0
    n_pairs, n_extra = len(a_list), len(extras)
    dims = (((1,), (0,)), ((), ())) if mode == "nn" else (((1,), (1,)), ((), ()))

    def body(*refs):
        a_refs, b_refs = refs[:n_pairs], refs[n_pairs:2 * n_pairs]
        e_refs = refs[2 * n_pairs:2 * n_pairs + n_extra]
        o_refs = refs[2 * n_pairs + n_extra:]
        acc = None
        for a_ref, b_ref in zip(a_refs, b_refs):
            a = a_ref[...]
            if a.dtype != BF16:
                a = a.astype(BF16)
            d = lax.dot_general(a, b_ref[...], dims, preferred_element_type=F32)
            acc = d if acc is None else acc + d
        outs = (acc,) if epilogue is None else epilogue(acc, *[e[...] for e in e_refs])
        for o_ref, o in zip(o_refs, outs):
            o_ref[...] = o.astype(o_ref.dtype)

    in_specs = [pl.BlockSpec((tm, a.shape[1]), lambda i, j: (i, 0)) for a in a_list]
    if mode == "nn":
        in_specs += [pl.BlockSpec((b.shape[0], tn), lambda i, j: (0, j)) for b in b_list]
    else:
        in_specs += [pl.BlockSpec((tn, b.shape[1]), lambda i, j: (j, 0)) for b in b_list]
    in_specs += [_tiled_spec(e.shape, tm, tn, n_total) for e in extras]
    out_specs = [_tiled_spec(o.shape, tm, tn, n_total) for o in out_shapes]
    res = pl.pallas_call(
        body, name=name, grid=(m_total // tm, n_total // tn), in_specs=in_specs, out_specs=out_specs,
        out_shape=list(out_shapes), compiler_params=_params("parallel", "parallel"),
    )(*a_list, *b_list, *extras)
    return res


def _mm_tn(name, a, b, tmm=512):
    m_total, k = a.shape
    n = b.shape[1]
    tmm = min(tmm, m_total)
    assert m_total % tmm == 0

    def body(a_ref, b_ref, o_ref):
        @pl.when(pl.program_id(0) == 0)
        def _():
            o_ref[...] = jnp.zeros_like(o_ref)

        av, bv = a_ref[...], b_ref[...]
        if av.dtype != BF16:
            av = av.astype(BF16)
        if bv.dtype != BF16:
            bv = bv.astype(BF16)
        o_ref[...] += lax.dot_general(av, bv, (((0,), (0,)), ((), ())), preferred_element_type=F32)

    return pl.pallas_call(
        body, name=name, grid=(m_total // tmm,),
        in_specs=[pl.BlockSpec((tmm, k), lambda m: (m, 0)), pl.BlockSpec((tmm, n), lambda m: (m, 0))],
        out_specs=pl.BlockSpec((k, n), lambda m: (0, 0)),
        out_shape=jax.ShapeDtypeStruct((k, n), F32), compiler_params=_params("arbitrary"),
    )(a, b)


def _rows(name, fn, ins, tile_outs, sum_outs=(), tr=512):
    t_total = max(a.shape[0] for a in ins)
    tr = min(tr, t_total)
    assert t_total % tr == 0
    n_in, n_tile = len(ins), len(tile_outs)

    def body(*refs):
        outs = fn(*[r[...] for r in refs[:n_in]])
        for o_ref, o in zip(refs[n_in:n_in + n_tile], outs[:n_tile]):
            o_ref[...] = o.astype(o_ref.dtype)
        if sum_outs:
            @pl.when(pl.program_id(0) == 0)
            def _():
                for s_ref in refs[n_in + n_tile:]:
                    s_ref[...] = jnp.zeros_like(s_ref)

            for s_ref, s in zip(refs[n_in + n_tile:], outs[n_tile:]):
                s_ref[...] += s

    def spec(shape):
        if shape[0] == 1:
            return pl.BlockSpec(shape, lambda i: (0, 0))
        return pl.BlockSpec((tr, shape[1]), lambda i: (i, 0))

    return pl.pallas_call(
        body, name=name, grid=(t_total // tr,), in_specs=[spec(a.shape) for a in ins],
        out_specs=[spec(o.shape) for o in tile_outs] + [spec(s.shape) for s in sum_outs],
        out_shape=list(tile_outs) + list(sum_outs),
        compiler_params=_params("arbitrary" if sum_outs else "parallel"),
    )(*ins)


def _norm_fwd(name, x, gain):
    def fn(xv, g):
        inv = lax.rsqrt(jnp.mean(xv * xv, axis=-1, keepdims=True) + RMS_EPS)
        return (xv * inv * g,)

    return _rows(name, fn, [x, gain], [jax.ShapeDtypeStruct(x.shape, BF16)])[0]


def _norm_bwd(name, dh, x, gain, dres):
    def fn(dhv, xv, g, dr):
        inv = lax.rsqrt(jnp.mean(xv * xv, axis=-1, keepdims=True) + RMS_EPS)
        xn = xv * inv
        dxn = dhv * g
        dx = inv * (dxn - xn * jnp.mean(dxn * xn, axis=-1, keepdims=True)) + dr
        return dx, jnp.sum(dhv * xn, axis=0, keepdims=True)

    d = x.shape[1]
    return _rows(name, fn, [dh, x, gain, dres], [jax.ShapeDtypeStruct(x.shape, F32)],
                 [jax.ShapeDtypeStruct((1, d), F32)])


def _final_loss(x3, target, gain):
    d = x3.shape[1]

    def fn(xv, tv, g):
        inv = lax.rsqrt(jnp.mean(xv * xv, axis=-1, keepdims=True) + RMS_EPS)
        xn = xv * inv
        err = xn * g - tv
        dy = err * (1.0 / d)
        dxn = dy * g
        dx = inv * (dxn - xn * jnp.mean(dxn * xn, axis=-1, keepdims=True))
        return dx, jnp.sum(dy * xn, axis=0, keepdims=True), (0.5 / d) * jnp.sum(err * err, axis=0, keepdims=True)

    return _rows("final_loss", fn, [x3, target, gain], [jax.ShapeDtypeStruct(x3.shape, F32)],
                 [jax.ShapeDtypeStruct((1, d), F32), jax.ShapeDtypeStruct((1, d), F32)])


def _window_counts(t_pos, w):
    return jnp.minimum(t_pos + 1, w).astype(F32)


def _pool_fwd(u, w_pool, scale, tr=512):
    t_total, width = u.shape
    tr = min(tr, t_total)
    n_groups = len(POOL_WINDOWS)
    gdim = width // n_groups
    ext = tr + POOL_HALO

    def body(u_ref, halo_ref, w_ref, s_ref, pooled_ref, ya_ref):
        i = pl.program_id(0)
        halo = jnp.where(i == 0, 0.0, halo_ref[...])
        t_pos = i * tr + lax.broadcasted_iota(jnp.int32, (tr, 1), 0)
        for g, w in enumerate(POOL_WINDOWS):
            cols = slice(g * gdim, (g + 1) * gdim)
            main = u_ref[:, cols]
            win = jnp.concatenate([halo[:, cols], main], axis=0)
            span = 1
            while span < w:
                win = win + pltpu.roll(win, span, 0)
                span *= 2
            pooled = win[POOL_HALO:, :] * (1.0 / _window_counts(t_pos, w)) - main
            pooled_b = pooled.astype(BF16)
            pooled_ref[:, cols] = pooled_b
            mixed = jnp.dot(pooled_b, w_ref[g], preferred_element_type=F32)
            ya_ref[:, cols] = (mixed * s_ref[:, cols]).astype(BF16)

    hb = tr // POOL_HALO
    return pl.pallas_call(
        body, name="pool_fwd", grid=(t_total // tr,),
        in_specs=[pl.BlockSpec((tr, width), lambda i: (i, 0)),
                  pl.BlockSpec((POOL_HALO, width), lambda i: (jnp.maximum(i * hb - 1, 0), 0)),
                  pl.BlockSpec((n_groups, gdim, gdim), lambda i: (0, 0, 0)),
                  pl.BlockSpec((1, width), lambda i: (0, 0))],
        out_specs=[pl.BlockSpec((tr, width), lambda i: (i, 0)), pl.BlockSpec((tr, width), lambda i: (i, 0))],
        out_shape=[jax.ShapeDtypeStruct(u.shape, BF16), jax.ShapeDtypeStruct(u.shape, BF16)],
        compiler_params=_params("parallel"),
    )(u, u, w_pool, scale)


def _pool_bwd(dya, pooled, w_pool, scale, tr=512):
    t_total, width = dya.shape
    tr = min(tr, t_total)
    n_groups = len(POOL_WINDOWS)
    gdim = width // n_groups
    ext = tr + POOL_HALO
    n_tiles = t_total // tr

    def body(d_ref, halo_ref, p_ref, w_ref, s_ref, du_ref, dw_ref, ds_ref):
        i = pl.program_id(0)

        @pl.when(i == 0)
        def _():
            dw_ref[...] = jnp.zeros_like(dw_ref)
            ds_ref[...] = jnp.zeros_like(ds_ref)

        halo = jnp.where(i == n_tiles - 1, 0.0, halo_ref[...])
        t_pos = i * tr + lax.broadcasted_iota(jnp.int32, (ext, 1), 0)
        for g, w in enumerate(POOL_WINDOWS):
            cols = slice(g * gdim, (g + 1) * gdim)
            sc = s_ref[:, cols]
            d_main = d_ref[:, cols]
            pooled_b = p_ref[:, cols]
            mixed = jnp.dot(pooled_b, w_ref[g], preferred_element_type=F32)
            ds_ref[:, cols] += jnp.sum(d_main * mixed, axis=0, keepdims=True)
            dmix = (jnp.concatenate([d_main, halo[:, cols]], axis=0) * sc).astype(BF16)
            dw_ref[g] += lax.dot_general(pooled_b, dmix[:tr, :], (((0,), (0,)), ((), ())),
                                         preferred_element_type=F32)
            dpool = lax.dot_general(dmix, w_ref[g], (((1,), (1,)), ((), ())), preferred_element_type=F32)
            win = dpool * (1.0 / _window_counts(t_pos, w))
            span = 1
            while span < w:
                win = win + pltpu.roll(win, ext - span, 0)
                span *= 2
            du_ref[:, cols] = (win[:tr, :] - dpool[:tr, :]).astype(BF16)

    hb = tr // POOL_HALO
    last_halo = t_total // POOL_HALO - 1
    return pl.pallas_call(
        body, name="pool_bwd", grid=(n_tiles,),
        in_specs=[pl.BlockSpec((tr, width), lambda i: (i, 0)),
                  pl.BlockSpec((POOL_HALO, width), lambda i: (jnp.minimum((i + 1) * hb, last_halo), 0)),
                  pl.BlockSpec((tr, width), lambda i: (i, 0)),
                  pl.BlockSpec((n_groups, gdim, gdim), lambda i: (0, 0, 0)),
                  pl.BlockSpec((1, width), lambda i: (0, 0))],
        out_specs=[pl.BlockSpec((tr, width), lambda i: (i, 0)),
                   pl.BlockSpec((n_groups, gdim, gdim), lambda i: (0, 0, 0)),
                   pl.BlockSpec((1, width), lambda i: (0, 0))],
        out_shape=[jax.ShapeDtypeStruct(dya.shape, BF16), jax.ShapeDtypeStruct((n_groups, gdim, gdim), F32),
                   jax.ShapeDtypeStruct((1, width), F32)],
        compiler_params=_params("arbitrary"),
    )(dya, dya, pooled, w_pool, scale)


def _head_masks():
    lane = lax.broadcasted_iota(jnp.int32, (1, LANES), 1)
    return lane < HEAD_DIM


def _stack_heads(tile, first):
    zero = jnp.zeros_like(tile)
    return jnp.concatenate([jnp.where(first, tile, zero), jnp.where(first, zero, tile)], axis=0)


def _split_bf16(v):
    hi = v.astype(BF16)
    lo = (v - hi.astype(F32)).astype(BF16)
    return hi, lo


def _slab_scores(q, kd, t_pos, k_start):
    z = lax.dot_general(q, kd, (((1,), (1,)), ((), ())), preferred_element_type=F32)
    col = lax.broadcasted_iota(jnp.int32, (1, 2 * ATT_SLAB), 1)
    mask = k_start + (col & (ATT_SLAB - 1)) < t_pos
    e = jnp.exp(-jnp.abs(z))
    log_fail = jnp.where(mask, -(jnp.maximum(z, 0.0) + jnp.log(1.0 + e)), 0.0)
    return z, mask, log_fail, e


def _tri(upper):
    r = lax.broadcasted_iota(jnp.int32, (ATT_CHUNK, ATT_CHUNK), 0)
    c = lax.broadcasted_iota(jnp.int32, (ATT_CHUNK, ATT_CHUNK), 1)
    return jnp.where(r > c if upper else r < c, 1.0, 0.0).astype(BF16)


def _scan_chunk(v, tri):
    hi, lo = _split_bf16(v)
    return (jnp.dot(hi, tri, preferred_element_type=F32) + jnp.dot(lo, tri, preferred_element_type=F32))


def _lane_bcast(col):
    return jnp.broadcast_to(col, (col.shape[0], LANES))


def _scan_slab(v, tri, carries, from_right):
    n_chunks = ATT_SLAB // ATT_CHUNK
    edge = 0 if from_right else ATT_CHUNK - 1
    parts, new_carries = [None] * (2 * n_chunks), []
    for head in range(2):
        run = carries[head]
        for c in (reversed(range(n_chunks)) if from_right else range(n_chunks)):
            lo_col = head * ATT_SLAB + c * ATT_CHUNK
            vc = v[:, lo_col:lo_col + ATT_CHUNK]
            sc = _scan_chunk(vc, tri)
            parts[head * n_chunks + c] = sc + jnp.concatenate([run] * (ATT_CHUNK // LANES), axis=1)
            run = run + _lane_bcast(sc[:, edge:edge + 1] + vc[:, edge:edge + 1])
        new_carries.append(run)
    return jnp.concatenate(parts, axis=1), new_carries


def _fold_heads(stacked, first):
    s = stacked.shape[0] // 2
    return jnp.where(first, stacked[:s], stacked[s:])


def _attn_fwd(q_src, q_col, kv_src, k_col, v_col, n_pairs=4):
    t_total = q_src.shape[0]
    blk = ATT_BLOCK
    n_blocks = t_total // blk
    assert t_total % ATT_SLAB == 0

    def body(q_ref, k_ref, v_ref, o_ref):
        i = pl.program_id(1)
        first = _head_masks()
        q = q_ref[...] * ATT_SCALE
        t_pos = i * blk + lax.broadcasted_iota(jnp.int32, (blk, 1), 0)
        suffix_tri = _tri(upper=True)

        def more(state):
            slab, reach = state[0], state[1]
            return jnp.logical_and(slab >= 0, reach > ATT_EXIT_BELOW)

        def step(state):
            slab, _, acc, right_a, right_b = state
            k_start = pl.multiple_of(slab * ATT_SLAB, ATT_SLAB)
            kd = _stack_heads(k_ref[pl.ds(k_start, ATT_SLAB), :], first)
            vd = _stack_heads(v_ref[pl.ds(k_start, ATT_SLAB), :], first)
            z, mask, log_fail, _ = _slab_scores(q, kd, t_pos, k_start)
            suffix, (right_a, right_b) = _scan_slab(log_fail, suffix_tri, (right_a, right_b), from_right=True)
            a = jnp.exp(jnp.where(mask, z + log_fail + suffix, -1e30)).astype(BF16)
            acc = acc + jnp.dot(a, vd, preferred_element_type=F32)
            return slab - 1, jnp.max(jnp.maximum(right_a, right_b)), acc, right_a, right_b

        zero = jnp.zeros((blk, LANES), F32)
        state = lax.while_loop(more, step, ((i * blk) // ATT_SLAB, jnp.float32(0.0), zero, zero, zero))
        o_ref[...] = state[2].astype(BF16)

    return pl.pallas_call(
        body, name="attn_fwd", grid=(n_pairs, n_blocks),
        in_specs=[pl.BlockSpec((blk, LANES), lambda h, i: (i, q_col + h)),
                  pl.BlockSpec((t_total, LANES), lambda h, i: (0, k_col + h)),
                  pl.BlockSpec((t_total, LANES), lambda h, i: (0, v_col + h))],
        out_specs=pl.BlockSpec((blk, LANES), lambda h, i: (i, h)),
        out_shape=jax.ShapeDtypeStruct((t_total, n_pairs * LANES), BF16),
        compiler_params=_params("parallel", "parallel"),
    )(q_src, kv_src, kv_src)


def _attn_bwd(q_src, q_col, kv_src, k_col, v_col, dy, n_pairs=4):
    t_total = q_src.shape[0]
    blk = ATT_BLOCK
    n_blocks = t_total // blk
    n_slabs = t_total // ATT_SLAB
    assert t_total % ATT_SLAB == 0

    def body(q_ref, dy_ref, k_ref, v_ref, dq_ref, dk_ref, dv_ref, g_s, sig_s, dk_acc, dv_acc):
        i = pl.program_id(1)

        @pl.when(i == 0)
        def _():
            dk_acc[...] = jnp.zeros_like(dk_acc)
            dv_acc[...] = jnp.zeros_like(dv_acc)

        first = _head_masks()
        q = q_ref[...] * ATT_SCALE
        dy = dy_ref[...]
        t_pos = i * blk + lax.broadcasted_iota(jnp.int32, (blk, 1), 0)
        suffix_tri = _tri(upper=True)
        prefix_tri = _tri(upper=False)
        diag = (i * blk) // ATT_SLAB

        def more(state):
            slab, reach = state[0], state[1]
            return jnp.logical_and(slab >= 0, reach > ATT_EXIT_BELOW)

        def sweep1(state):
            slab, _, right_a, right_b = state
            k_start = pl.multiple_of(slab * ATT_SLAB, ATT_SLAB)
            kd = _stack_heads(k_ref[pl.ds(k_start, ATT_SLAB), :], first)
            vd = _stack_heads(v_ref[pl.ds(k_start, ATT_SLAB), :], first)
            z, mask, log_fail, e = _slab_scores(q, kd, t_pos, k_start)
            suffix, (right_a, right_b) = _scan_slab(log_fail, suffix_tri, (right_a, right_b), from_right=True)
            a = jnp.exp(jnp.where(mask, z + log_fail + suffix, -1e30))
            da = lax.dot_general(dy, vd, (((1,), (1,)), ((), ())), preferred_element_type=F32)
            g_s[slab] = da * a
            sig_s[slab] = jnp.where(z >= 0.0, 1.0, e) / (1.0 + e)
            dv_acc[pl.ds(k_start, ATT_SLAB), :] += _fold_heads(lax.dot_general(
                a.astype(BF16), dy, (((0,), (0,)), ((), ())), preferred_element_type=F32), first)
            return slab - 1, jnp.max(jnp.maximum(right_a, right_b)), right_a, right_b

        zero = jnp.zeros((blk, LANES), F32)
        end = lax.while_loop(more, sweep1, (diag, jnp.float32(0.0), zero, zero))[0]

        def sweep2(slab, carry):
            dq, left_a, left_b = carry
            k_start = pl.multiple_of(slab * ATT_SLAB, ATT_SLAB)
            kd = _stack_heads(k_ref[pl.ds(k_start, ATT_SLAB), :], first)
            g = g_s[slab]
            sig = sig_s[slab]
            prefix, (left_a, left_b) = _scan_slab(g, prefix_tri, (left_a, left_b), from_right=False)
            col = lax.broadcasted_iota(jnp.int32, (1, 2 * ATT_SLAB), 1)
            mask = k_start + (col & (ATT_SLAB - 1)) < t_pos
            dz = jnp.where(mask, g * (1.0 - sig) - sig * prefix, 0.0).astype(BF16)
            dq = dq + jnp.dot(dz, kd, preferred_element_type=F32)
            dk_acc[pl.ds(k_start, ATT_SLAB), :] += _fold_heads(lax.dot_general(
                dz, q, (((0,), (0,)), ((), ())), preferred_element_type=F32), first)
            return dq, left_a, left_b

        dq = lax.fori_loop(end + 1, diag + 1, sweep2, (zero, zero, zero))[0]
        dq_ref[...] = (dq * ATT_SCALE).astype(BF16)

        @pl.when(i == n_blocks - 1)
        def _():
            dk_ref[...] = dk_acc[...].astype(BF16)
            dv_ref[...] = dv_acc[...].astype(BF16)

    out = jax.ShapeDtypeStruct((t_total, n_pairs * LANES), BF16)
    return pl.pallas_call(
        body, name="attn_bwd", grid=(n_pairs, n_blocks),
        in_specs=[pl.BlockSpec((blk, LANES), lambda h, i: (i, q_col + h)),
                  pl.BlockSpec((blk, LANES), lambda h, i: (i, h)),
                  pl.BlockSpec((t_total, LANES), lambda h, i: (0, k_col + h)),
                  pl.BlockSpec((t_total, LANES), lambda h, i: (0, v_col + h))],
        out_specs=[pl.BlockSpec((blk, LANES), lambda h, i: (i, h)),
                   pl.BlockSpec((t_total, LANES), lambda h, i: (0, h)),
                   pl.BlockSpec((t_total, LANES), lambda h, i: (0, h))],
        out_shape=[out, out, out],
        scratch_shapes=[pltpu.VMEM((n_slabs, blk, 2 * ATT_SLAB), F32), pltpu.VMEM((n_slabs, blk, 2 * ATT_SLAB), F32),
                        pltpu.VMEM((t_total, LANES), F32), pltpu.VMEM((t_total, LANES), F32)],
        compiler_params=_params("arbitrary", "arbitrary"),
    )(q_src, dy, kv_src, kv_src)


def _adamw(name, w, g, m, v):
    def fn(wv, gv, mv, vv):
        mn = ADAM_B1 * mv + (1.0 - ADAM_B1) * gv
        vn = ADAM_B2 * vv + (1.0 - ADAM_B2) * (gv * gv)
        m_hat = mn / (1.0 - ADAM_B1 ** ADAM_STEP)
        v_hat = vn / (1.0 - ADAM_B2 ** ADAM_STEP)
        return -ADAM_LR * (m_hat / (jnp.sqrt(v_hat) + ADAM_EPS) + ADAM_WD * wv), mn, vn

    rows = w.shape[0]
    tr = _row_tile(rows)
    shp = jax.ShapeDtypeStruct(w.shape, F32)
    if rows == 1:
        def body(w_ref, g_ref, m_ref, v_ref, d_ref, mo_ref, vo_ref):
            d, mn, vn = fn(w_ref[...], g_ref[...], m_ref[...], v_ref[...])
            d_ref[...], mo_ref[...], vo_ref[...] = d, mn, vn

        return pl.pallas_call(body, name=name, out_shape=[shp, shp, shp])(w, g, m, v)
    return _rows(name, fn, [w, g, m, v], [shp, shp, shp], tr=tr)


def _place():
    return lax.axis_index("x"), lax.axis_index("y"), lax.axis_index("c")


def _other_chips(x, y):
    return [(1 - x, y), (x, 1 - y), (1 - x, 1 - y)]


ANY = pl.BlockSpec(memory_space=pl.ANY)


def _all_gather_weights(shards):
    n_w = len(shards)

    def body(*refs):
        ins, outs = refs[:n_w], refs[n_w:2 * n_w]
        send_sems, recv_sems, pass_send, pass_recv, local_sems = refs[2 * n_w:]
        x, y, c = _place()
        my_chip = 2 * x + y
        chips = _other_chips(x, y)
        local = [pltpu.make_async_copy(ins[w], outs[w].at[my_chip], local_sems.at[w]) for w in range(n_w)]
        for cp in local:
            cp.start()

        def half(w, core):
            h = shards[w].shape[0] // 2
            return pl.ds(core * h, h)

        sends = []
        for p, (ox, oy) in enumerate(chips):
            for w in range(n_w):
                sends.append(pltpu.make_async_remote_copy(
                    src_ref=ins[w].at[half(w, c)], dst_ref=outs[w].at[my_chip, half(w, c)],
                    send_sem=send_sems.at[p, w], recv_sem=recv_sems.at[p, w],
                    device_id=(ox, oy, c), device_id_type=MESH))
        for cp in sends:
            cp.start()
        passes = []
        for p, (ox, oy) in enumerate(chips):
            chip = 2 * ox + oy
            for w in range(n_w):
                landed = outs[w].at[chip, half(w, c)]
                pltpu.make_async_remote_copy(
                    src_ref=landed, dst_ref=landed, send_sem=send_sems.at[p, w], recv_sem=recv_sems.at[p, w],
                    device_id=(ox, oy, c), device_id_type=MESH).wait_recv()
                cp = pltpu.make_async_remote_copy(
                    src_ref=landed, dst_ref=landed, send_sem=pass_send.at[p, w], recv_sem=pass_recv.at[p, w],
                    device_id=(x, y, 1 - c), device_id_type=MESH)
                cp.start()
                passes.append(cp)
        for p, (ox, oy) in enumerate(chips):
            chip = 2 * ox + oy
            for w in range(n_w):
                theirs = outs[w].at[chip, half(w, 1 - c)]
                pltpu.make_async_remote_copy(
                    src_ref=theirs, dst_ref=theirs, send_sem=pass_send.at[p, w], recv_sem=pass_recv.at[p, w],
                    device_id=(x, y, 1 - c), device_id_type=MESH).wait_recv()
        for cp in sends + passes:
            cp.wait_send()
        for cp in local:
            cp.wait()

    return pl.pallas_call(
        body, name="all_gather_weights", in_specs=[ANY] * n_w, out_specs=[ANY] * n_w,
        out_shape=[jax.ShapeDtypeStruct((N_CHIPS,) + s.shape, s.dtype) for s in shards],
        scratch_shapes=[pltpu.SemaphoreType.DMA((3, n_w)), pltpu.SemaphoreType.DMA((3, n_w)),
                        pltpu.SemaphoreType.DMA((3, n_w)), pltpu.SemaphoreType.DMA((3, n_w)),
                        pltpu.SemaphoreType.DMA((n_w,))],
    )(*shards)


def _pair_exchange(grads):
    n_w = len(grads)

    def halves(w):
        return grads[w].shape[1] // 2

    def body(*refs):
        ins, mine, theirs = refs[:n_w], refs[n_w:2 * n_w], refs[2 * n_w:3 * n_w]
        send_sems, recv_sems, local_sems = refs[3 * n_w:]
        x, y, c = _place()
        local = [pltpu.make_async_copy(ins[w].at[:, pl.ds(c * halves(w), halves(w)), :], mine[w], local_sems.at[w])
                 for w in range(n_w)]
        sends = [pltpu.make_async_remote_copy(
            src_ref=ins[w].at[:, pl.ds((1 - c) * halves(w), halves(w)), :], dst_ref=theirs[w],
            send_sem=send_sems.at[w], recv_sem=recv_sems.at[w], device_id=(x, y, 1 - c), device_id_type=MESH)
            for w in range(n_w)]
        for cp in local + sends:
            cp.start()
        for cp in sends:
            cp.wait_recv()
        for cp in sends:
            cp.wait_send()
        for cp in local:
            cp.wait()

    shapes = [jax.ShapeDtypeStruct((N_CHIPS, halves(w), grads[w].shape[2]), F32) for w in range(n_w)]
    res = pl.pallas_call(
        body, name="pair_exchange", in_specs=[ANY] * n_w, out_specs=[ANY] * (2 * n_w), out_shape=shapes + shapes,
        scratch_shapes=[pltpu.SemaphoreType.DMA((n_w,)), pltpu.SemaphoreType.DMA((n_w,)),
                        pltpu.SemaphoreType.DMA((n_w,))],
    )(*grads)
    return res[:n_w], res[n_w:]


def _chip_exchange(pair_sums):
    n_w = len(pair_sums)

    def body(*refs):
        ins, outs = refs[:n_w], refs[n_w:2 * n_w]
        send_sems, recv_sems, local_sems = refs[2 * n_w:]
        x, y, c = _place()
        my_chip = 2 * x + y
        chips = _other_chips(x, y)
        local = [pltpu.make_async_copy(ins[w].at[my_chip], outs[w].at[my_chip], local_sems.at[w]) for w in range(n_w)]
        sends = []
        for p, (ox, oy) in enumerate(chips):
            for w in range(n_w):
                sends.append(pltpu.make_async_remote_copy(
                    src_ref=ins[w].at[2 * ox + oy], dst_ref=outs[w].at[my_chip],
                    send_sem=send_sems.at[p, w], recv_sem=recv_sems.at[p, w],
                    device_id=(ox, oy, c), device_id_type=MESH))
        for cp in local + sends:
            cp.start()
        for p, (ox, oy) in enumerate(chips):
            for w in range(n_w):
                landed = outs[w].at[2 * ox + oy]
                pltpu.make_async_remote_copy(
                    src_ref=landed, dst_ref=landed, send_sem=send_sems.at[p, w], recv_sem=recv_sems.at[p, w],
                    device_id=(ox, oy, c), device_id_type=MESH).wait_recv()
        for cp in sends:
            cp.wait_send()
        for cp in local:
            cp.wait()

    return pl.pallas_call(
        body, name="chip_exchange", in_specs=[ANY] * n_w, out_specs=[ANY] * n_w,
        out_shape=[jax.ShapeDtypeStruct(s.shape, s.dtype) for s in pair_sums],
        scratch_shapes=[pltpu.SemaphoreType.DMA((3, n_w)), pltpu.SemaphoreType.DMA((3, n_w)),
                        pltpu.SemaphoreType.DMA((n_w,))],
    )(*pair_sums)


def _pair_share(halves):
    n_w = len(halves)

    def body(*refs):
        ins, outs = refs[:n_w], refs[n_w:2 * n_w]
        send_sems, recv_sems, local_sems = refs[2 * n_w:]
        x, y, c = _place()

        def rows(w, core):
            h = halves[w].shape[0]
            return pl.ds(core * h, h)

        local = [pltpu.make_async_copy(ins[w], outs[w].at[rows(w, c)], local_sems.at[w]) for w in range(n_w)]
        sends = [pltpu.make_async_remote_copy(
            src_ref=ins[w], dst_ref=outs[w].at[rows(w, c)], send_sem=send_sems.at[w], recv_sem=recv_sems.at[w],
            device_id=(x, y, 1 - c), device_id_type=MESH) for w in range(n_w)]
        for cp in local + sends:
            cp.start()
        for w in range(n_w):
            theirs = outs[w].at[rows(w, 1 - c)]
            pltpu.make_async_remote_copy(
                src_ref=theirs, dst_ref=theirs, send_sem=send_sems.at[w], recv_sem=recv_sems.at[w],
                device_id=(x, y, 1 - c), device_id_type=MESH).wait_recv()
        for cp in sends:
            cp.wait_send()
        for cp in local:
            cp.wait()

    return pl.pallas_call(
        body, name="pair_share", in_specs=[ANY] * n_w, out_specs=[ANY] * n_w,
        out_shape=[jax.ShapeDtypeStruct((2 * h.shape[0], h.shape[1]), F32) for h in halves],
        scratch_shapes=[pltpu.SemaphoreType.DMA((n_w,)), pltpu.SemaphoreType.DMA((n_w,)),
                        pltpu.SemaphoreType.DMA((n_w,))],
    )(*halves)


def _all_reduce_small(vec):
    rows = vec.shape[0]

    def body(v_ref, o_ref, slots, send_sems, recv_sems):
        x, y, c = _place()
        me = 4 * x + 2 * y + c
        slots[me] = v_ref[...]
        sends = []
        for k in range(1, N_DEV):
            peer = (x ^ (k >> 2), y ^ ((k >> 1) & 1), c ^ (k & 1))
            sends.append(pltpu.make_async_remote_copy(
                src_ref=v_ref, dst_ref=slots.at[me], send_sem=send_sems.at[k - 1], recv_sem=recv_sems.at[k - 1],
                device_id=peer, device_id_type=MESH))
        for cp in sends:
            cp.start()
        for k in range(1, N_DEV):
            px, py, pc = x ^ (k >> 2), y ^ ((k >> 1) & 1), c ^ (k & 1)
            landed = slots.at[4 * px + 2 * py + pc]
            pltpu.make_async_remote_copy(
                src_ref=landed, dst_ref=landed, send_sem=send_sems.at[k - 1], recv_sem=recv_sems.at[k - 1],
                device_id=(px, py, pc), device_id_type=MESH).wait_recv()
        for cp in sends:
            cp.wait_send()
        total = slots[0]
        for d in range(1, N_DEV):
            total = total + slots[d]
        o_ref[...] = total

    vm = pl.BlockSpec(memory_space=pltpu.VMEM)
    return pl.pallas_call(
        body, name="all_reduce_small", in_specs=[vm], out_specs=vm, out_shape=jax.ShapeDtypeStruct(vec.shape, F32),
        scratch_shapes=[pltpu.VMEM((N_DEV, rows, LANES), F32), pltpu.SemaphoreType.DMA((N_DEV - 1,)),
                        pltpu.SemaphoreType.DMA((N_DEV - 1,))],
    )(vec)


def _add_cast(name, a, b):
    n, r, c = a.shape
    out = _rows(name, lambda av, bv: (av + bv,), [a.reshape(n * r, c), b.reshape(n * r, c)],
                [jax.ShapeDtypeStruct((n * r, c), BF16)], tr=_row_tile(n * r))[0]
    return out.reshape(n, r, c)


def _row_tile(rows):
    for tr in (256, 128, 64, 32, 16):
        if rows % tr == 0:
            return tr
    return rows


def _sum_chips(name, q):
    n, r, c = q.shape
    tr = _row_tile(r)

    def body(q_ref, o_ref):
        total = q_ref[0].astype(F32)
        for j in range(1, n):
            total = total + q_ref[j].astype(F32)
        o_ref[...] = total

    return pl.pallas_call(
        body, name=name, grid=(r // tr,), in_specs=[pl.BlockSpec((n, tr, c), lambda i: (0, i, 0))],
        out_specs=pl.BlockSpec((tr, c), lambda i: (i, 0)), out_shape=jax.ShapeDtypeStruct((r, c), F32),
        compiler_params=_params("parallel"),
    )(q)


BIG = ("w_in", "w_branch_a", "w_branch_b", "w_out", "w_ffn_gate", "w_ffn_up", "w_ffn_down", "w_ple_gate", "w_ple_proj")
COLUMN_SHARDED = ("w_in", "w_branch_a", "w_branch_b", "w_ffn_gate", "w_ffn_up", "w_ple_proj")
SMALL = ("norm_mix", "w_pool", "pool_scale", "norm_ffn", "norm_ple", "norm_final")


def _join_columns(w4):
    return jnp.concatenate([w4[j] for j in range(N_CHIPS)], axis=1)


def _split_columns(g):
    k, n = g.shape
    return g.reshape(k, N_CHIPS, n // N_CHIPS).transpose(1, 0, 2)


def _sds(shape, dtype):
    return jax.ShapeDtypeStruct(shape, dtype)


def _local_step(x, p, target, wf, small):
    t, d = x.shape
    w_in, w_gate, w_up, w_down = wf["w_in"], wf["w_ffn_gate"], wf["w_ffn_up"], wf["w_ffn_down"]
    w_a, w_b, w_pp = _join_columns(wf["w_branch_a"]), _join_columns(wf["w_branch_b"]), _join_columns(wf["w_ple_proj"])
    w_out = wf["w_out"].reshape(d, d)
    w_pg = wf["w_ple_gate"].reshape(d, d)
    w_pool_b = small["w_pool"].astype(BF16)
    dp = w_pool_b.shape[0] * w_pool_b.shape[1]
    dff = w_gate.shape[2]

    h1 = _norm_fwd("norm_mix", x, small["norm_mix"])
    u, q = _mm("proj_uq", [h1], [w_in[0]], "nn", [_sds((t, dp), F32), _sds((t, dp), BF16)],
               epilogue=lambda acc: (acc[:, :dp], acc[:, dp:]))
    kv, = _mm("proj_kv", [h1], [w_in[1]], "nn", [_sds((t, d), BF16)])
    ga, = _mm("proj_ga", [h1], [w_in[2]], "nn", [_sds((t, d), F32)])
    gb, = _mm("proj_gb", [h1], [w_in[3]], "nn", [_sds((t, d), F32)])
    pooled, ya = _pool_fwd(u, w_pool_b, small["pool_scale"])
    n_pairs = dp // LANES
    yb = _attn_fwd(q, 0, kv, 0, n_pairs, n_pairs)
    ta, = _mm("branch_a", [ya], [w_a], "nn", [_sds((t, d), F32)])
    tb, merged = _mm("branch_b_merge", [yb], [w_b], "nn", [_sds((t, d), F32), _sds((t, d), BF16)],
                     extras=[ta, ga, gb],
                     epilogue=lambda acc, tav, gav, gbv: (acc, _sigmoid(gav) * tav + _sigmoid(gbv) * acc))
    x1, = _mm("mix_out", [merged], [w_out], "nn", [_sds((t, d), F32)], extras=[x], epilogue=lambda acc, xv: (acc + xv,))
    h2 = _norm_fwd("norm_ffn", x1, small["norm_ffn"])
    gates, ups, acts = [], [], []
    for j in range(N_CHIPS):
        gj, = _mm(f"ffn_gate{j}", [h2], [w_gate[j]], "nn", [_sds((t, dff), F32)])
        uj, aj = _mm(f"ffn_up{j}", [h2], [w_up[j]], "nn", [_sds((t, dff), F32), _sds((t, dff), BF16)], extras=[gj],
                     epilogue=lambda acc, gv: (acc, gv * _sigmoid(gv) * acc))
        gates.append(gj), ups.append(uj), acts.append(aj)
    x2, = _mm("ffn_down", acts, [w_down[j] for j in range(N_CHIPS)], "nn", [_sds((t, d), F32)], extras=[x1],
              epilogue=lambda acc, xv: (acc + xv,))
    h3 = _norm_fwd("norm_ple", x2, small["norm_ple"])
    gp, = _mm("ple_gate", [h3], [w_pg], "nn", [_sds((t, d), F32)])
    pp, x3 = _mm("ple_proj", [p], [w_pp], "nn", [_sds((t, d), F32), _sds((t, d), F32)], extras=[gp, x2],
                 epilogue=lambda acc, gv, xv: (acc, xv + _sigmoid(gv) * acc))
    (dx3,), (d_norm_final, loss_row) = _split2(_final_loss(x3, target, small["norm_final"].reshape(1, d)), 1)

    def ple_bwd(dxv, gv, pv):
        s = _sigmoid(gv)
        return dxv * s, dxv * pv * s * (1.0 - s)

    d_pp, d_gp = _rows("ple_bwd", ple_bwd, [dx3, gp, pp], [_sds((t, d), BF16), _sds((t, d), BF16)])
    g_w_pp = _mm_tn("g_ple_proj", p, d_pp)
    g_w_pg = _mm_tn("g_ple_gate", h3, d_gp)
    dh3, = _mm("d_h3", [d_gp], [w_pg], "nt", [_sds((t, d), F32)])
    (dx2,), (d_norm_ple,) = _split2(_norm_bwd("norm_ple_bwd", dh3, x2, small["norm_ple"], dx3), 1)

    def ffn_bwd(acc, gv, uv):
        s = _sigmoid(gv)
        return acc * uv * (s * (1.0 + gv * (1.0 - s))), acc * (gv * s)

    d_gates, d_ups, g_w_gate, g_w_up, g_w_down = [], [], [], [], []
    for j in range(N_CHIPS):
        dgj, duj = _mm(f"d_act{j}", [dx2], [w_down[j]], "nt", [_sds((t, dff), BF16), _sds((t, dff), BF16)],
                       extras=[gates[j], ups[j]], epilogue=ffn_bwd)
        d_gates.append(dgj), d_ups.append(duj)
        g_w_down.append(_mm_tn(f"g_ffn_down{j}", acts[j], dx2))
        g_w_gate.append(_mm_tn(f"g_ffn_gate{j}", h2, dgj))
        g_w_up.append(_mm_tn(f"g_ffn_up{j}", h2, duj))
    dh2a, = _mm("d_h2_gate", d_gates, [w_gate[j] for j in range(N_CHIPS)], "nt", [_sds((t, d), F32)])
    dh2, = _mm("d_h2_up", d_ups, [w_up[j] for j in range(N_CHIPS)], "nt", [_sds((t, d), F32)], extras=[dh2a],
               epilogue=lambda acc, prev: (acc + prev,))
    (dx1,), (d_norm_ffn,) = _split2(_norm_bwd("norm_ffn_bwd", dh2, x1, small["norm_ffn"], dx2), 1)

    def merge_bwd(acc, tav, tbv, gav, gbv):
        sa, sb = _sigmoid(gav), _sigmoid(gbv)
        return acc * sa, acc * sb, acc * tav * sa * (1.0 - sa), acc * tbv * sb * (1.0 - sb)

    d_ta, d_tb, d_ga, d_gb = _mm("d_merged", [dx1], [w_out], "nt", [_sds((t, d), BF16)] * 4,
                                 extras=[ta, tb, ga, gb], epilogue=merge_bwd)
    g_w_out = _mm_tn("g_w_out", merged, dx1)
    g_w_a = _mm_tn("g_branch_a", ya, d_ta)
    g_w_b = _mm_tn("g_branch_b", yb, d_tb)
    d_ya, = _mm("d_ya", [d_ta], [w_a], "nt", [_sds((t, dp), F32)])
    d_yb, = _mm("d_yb", [d_tb], [w_b], "nt", [_sds((t, dp), BF16)])
    d_u, g_w_pool, d_pool_scale = _pool_bwd(d_ya, pooled, w_pool_b, small["pool_scale"])
    d_q, d_k, d_v = _attn_bwd(q, 0, kv, 0, n_pairs, d_yb, n_pairs)
    d_proj = [jnp.concatenate([d_u, d_q], axis=1), jnp.concatenate([d_k, d_v], axis=1), d_ga, d_gb]
    g_w_in = [_mm_tn(f"g_w_in{j}", h1, d_proj[j]) for j in range(N_CHIPS)]
    dh1, = _mm("d_h1", d_proj, [w_in[j] for j in range(N_CHIPS)], "nt", [_sds((t, d), F32)])
    (grad_x,), (d_norm_mix,) = _split2(_norm_bwd("norm_mix_bwd", dh1, x, small["norm_mix"], dx1), 1)

    big = {
        "w_in": jnp.stack(g_w_in), "w_branch_a": _split_columns(g_w_a), "w_branch_b": _split_columns(g_w_b),
        "w_out": g_w_out.reshape(wf["w_out"].shape), "w_ffn_gate": jnp.stack(g_w_gate), "w_ffn_up": jnp.stack(g_w_up),
        "w_ffn_down": jnp.stack(g_w_down), "w_ple_gate": g_w_pg.reshape(wf["w_ple_gate"].shape),
        "w_ple_proj": _split_columns(g_w_pp),
    }
    small_g = {"norm_mix": d_norm_mix, "w_pool": g_w_pool, "pool_scale": d_pool_scale, "norm_ffn": d_norm_ffn,
               "norm_ple": d_norm_ple, "norm_final": d_norm_final}
    return grad_x, big, small_g, loss_row


def _split2(res, n):
    return res[:n], res[n:]


def _pack_small(small_g, loss_row):
    parts, layout = [], []
    for name in SMALL + ("loss",):
        v = (loss_row if name == "loss" else small_g[name]).reshape(-1, LANES)
        pad = (-v.shape[0]) % 8
        if pad:
            v = jnp.concatenate([v, jnp.zeros((pad, LANES), F32)], axis=0)
        layout.append((name, sum(q.shape[0] for q in parts), v.shape[0]))
        parts.append(v)
    return jnp.concatenate(parts, axis=0), layout


def kernel(x, p, norm_mix, w_in, w_pool, pool_scale, w_branch_a, w_branch_b, w_out, norm_ffn, w_ffn_gate, w_ffn_up, w_ffn_down, norm_ple, w_ple_gate, w_ple_proj, norm_final, loss_target, m_norm_mix, m_w_in, m_w_pool, m_pool_scale, m_w_branch_a, m_w_branch_b, m_w_out, m_norm_ffn, m_w_ffn_gate, m_w_ffn_up, m_w_ffn_down, m_norm_ple, m_w_ple_gate, m_w_ple_proj, m_norm_final, v_norm_mix, v_w_in, v_w_pool, v_pool_scale, v_w_branch_a, v_w_branch_b, v_w_out, v_norm_ffn, v_w_ffn_gate, v_w_ffn_up, v_w_ffn_down, v_norm_ple, v_w_ple_gate, v_w_ple_proj, v_norm_final):
    given = dict(locals())
    names = BIG + SMALL
    order = ("norm_mix", "w_in", "w_pool", "pool_scale", "w_branch_a", "w_branch_b", "w_out", "norm_ffn", "w_ffn_gate",
             "w_ffn_up", "w_ffn_down", "norm_ple", "w_ple_gate", "w_ple_proj", "norm_final")
    t, d = x.shape[1], x.shape[2]
    shard = {n: given[n][0] for n in BIG}
    small = {"norm_mix": norm_mix, "w_pool": w_pool[0], "pool_scale": pool_scale, "norm_ffn": norm_ffn,
             "norm_ple": norm_ple, "norm_final": norm_final}

    gathered = _all_gather_weights([shard[n].astype(BF16) for n in BIG])
    wf = dict(zip(BIG, gathered))
    grad_x, big_g, small_g, loss_row = _local_step(
        x.reshape(t, d), p.reshape(t, p.shape[-1]), loss_target.reshape(t, d), wf, small)

    mine, theirs = _pair_exchange([big_g[n] for n in BIG])
    pair_sums = [_add_cast(f"pair_sum_{n}", a, b) for n, a, b in zip(BIG, mine, theirs)]
    landed = _chip_exchange(pair_sums)
    halves = [_sum_chips(f"chip_sum_{n}", q) for n, q in zip(BIG, landed)]
    grads = dict(zip(BIG, _pair_share(halves)))

    packed, layout = _pack_small(small_g, loss_row)
    reduced = _all_reduce_small(packed)
    for name, start, rows in layout:
        if name == "loss":
            loss = jnp.sum(reduced[start:start + rows])
        else:
            n_el = small[name].size
            grads[name] = reduced[start:start + rows].reshape(-1)[:n_el]

    deltas, new_m, new_v = {}, {}, {}
    for n in order:
        w = shard[n] if n in BIG else small[n]
        shape2 = w.shape if w.ndim == 2 else ((1, w.shape[0]) if w.ndim == 1 else (w.shape[0] * w.shape[1], w.shape[2]))
        g2 = grads[n].reshape(shape2)
        dl, mn, vn = _adamw(f"adamw_{n}", w.reshape(shape2), g2, given["m_" + n].reshape(shape2),
                            given["v_" + n].reshape(shape2))
        full = given[n].shape
        grads[n], deltas[n], new_m[n], new_v[n] = g2.reshape(full), dl.reshape(full), mn.reshape(full), vn.reshape(full)

    return (loss, grad_x.reshape(x.shape), *[grads[n] for n in order], *[deltas[n] for n in order],
            *[new_m[n] for n in order], *[new_v[n] for n in order])
```

```python
import functools
import math

import jax
import jax.numpy as jnp
from jax import lax
from jax.experimental import pallas as pl
from jax.experimental.pallas import tpu as pltpu

F32 = jnp.float32
BF16 = jnp.bfloat16
MESH = pl.DeviceIdType.MESH

RMS_EPS = 1e-6
POOL_WINDOWS = (2, 4, 8, 16)
POOL_HALO = 16
HEAD_DIM = 64
LANES = 128
ATT_BLOCK = 128
ATT_CHUNK = 256
ATT_SLAB = 256
ATT_SCALE = 1.0 / math.sqrt(HEAD_DIM)
ATT_EXIT_BELOW = -104.0
ADAM_LR, ADAM_B1, ADAM_B2, ADAM_EPS, ADAM_WD, ADAM_STEP = 0.001, 0.9, 0.999, 1e-08, 0.01, 10
V7X_VMEM_LIMIT_BYTES = 56 * 1024 * 1024
N_CHIPS = 4
N_DEV = 8


def _params(*semantics):
    return pltpu.CompilerParams(dimension_semantics=semantics, vmem_limit_bytes=V7X_VMEM_LIMIT_BYTES)


def _sigmoid(z):
    return 1.0 / (1.0 + jnp.exp(-z))


def _tiled_spec(shape, tm, tn, n_total):
    rows, width = shape
    if rows == 1:
        if width == n_total:
            return pl.BlockSpec((1, tn), lambda i, j: (0, j))
        return pl.BlockSpec((1, width), lambda i, j: (0, 0))
    if width == n_total:
        return pl.BlockSpec((tm, tn), lambda i, j: (i, j))
    assert tn == n_total, "an operand narrower than the output needs whole output rows per tile"
    return pl.BlockSpec((tm, width), lambda i, j: (i, 0))


def _mm(name, a_list, b_list, mode, out_shapes, epilogue=None, extras=(), tm=512, tn=None):
    m_total = a_list[0].shape[0]
    n_total = b_list[0].shape[1] if mode == "nn" else b_list[0].shape[0]
    tn = n_total if tn is None else tn
    tm = min(tm, m_total)
    assert m_total % tm == 0 and n_total % tn == 0
    n_pairs, n_extra = len(a_list), len(extras)
    dims = (((1,), (0,)), ((), ())) if mode == "nn" else (((1,), (1,)), ((), ()))

    def body(*refs):
        a_refs, b_refs = refs[:n_pairs], refs[n_pairs:2 * n_pairs]
        e_refs = refs[2 * n_pairs:2 * n_pairs + n_extra]
        o_refs = refs[2 * n_pairs + n_extra:]
        acc = None
        for a_ref, b_ref in zip(a_refs, b_refs):
            a = a_ref[...]
            if a.dtype != BF16:
                a = a.astype(BF16)
            d = lax.dot_general(a, b_ref[...], dims, preferred_element_type=F32)
            acc = d if acc is None else acc + d
        outs = (acc,) if epilogue is None else epilogue(acc, *[e[...] for e in e_refs])
        for o_ref, o in zip(o_refs, outs):
            o_ref[...] = o.astype(o_ref.dtype)

    in_specs = [pl.BlockSpec((tm, a.shape[1]), lambda i, j: (i, 0)) for a in a_list]
    if mode == "nn":
        in_specs += [pl.BlockSpec((b.shape[0], tn), lambda i, j: (0, j)) for b in b_list]
    else:
        in_specs += [pl.BlockSpec((tn, b.shape[1]), lambda i, j: (j, 0)) for b in b_list]
    in_specs += [_tiled_spec(e.shape, tm, tn, n_total) for e in extras]
    out_specs = [_tiled_spec(o.shape, tm, tn, n_total) for o in out_shapes]
    res = pl.pallas_call(
        body, name=name, grid=(m_total // tm, n_total // tn), in_specs=in_specs, out_specs=out_specs,
        out_shape=list(out_shapes), compiler_params=_params("parallel", "parallel"),
    )(*a_list, *b_list, *extras)
    return res


def _mm_tn(name, a, b, tmm=512):
    m_total, k = a.shape
    n = b.shape[1]
    tmm = min(tmm, m_total)
    assert m_total % tmm == 0

    def body(a_ref, b_ref, o_ref):
        @pl.when(pl.program_id(0) == 0)
        def _():
            o_ref[...] = jnp.zeros_like(o_ref)

        av, bv = a_ref[...], b_ref[...]
        if av.dtype != BF16:
            av = av.astype(BF16)
        if bv.dtype != BF16:
            bv = bv.astype(BF16)
        o_ref[...] += lax.dot_general(av, bv, (((0,), (0,)), ((), ())), preferred_element_type=F32)

    return pl.pallas_call(
        body, name=name, grid=(m_total // tmm,),
        in_specs=[pl.BlockSpec((tmm, k), lambda m: (m, 0)), pl.BlockSpec((tmm, n), lambda m: (m, 0))],
        out_specs=pl.BlockSpec((k, n), lambda m: (0, 0)),
        out_shape=jax.ShapeDtypeStruct((k, n), F32), compiler_params=_params("arbitrary"),
    )(a, b)


def _rows(name, fn, ins, tile_outs, sum_outs=(), tr=512):
    t_total = max(a.shape[0] for a in ins)
    tr = min(tr, t_total)
    assert t_total % tr == 0
    n_in, n_tile = len(ins), len(tile_outs)

    def body(*refs):
        outs = fn(*[r[...] for r in refs[:n_in]])
        for o_ref, o in zip(refs[n_in:n_in + n_tile], outs[:n_tile]):
            o_ref[...] = o.astype(o_ref.dtype)
        if sum_outs:
            @pl.when(pl.program_id(0) == 0)
            def _():
                for s_ref in refs[n_in + n_tile:]:
                    s_ref[...] = jnp.zeros_like(s_ref)

            for s_ref, s in zip(refs[n_in + n_tile:], outs[n_tile:]):
                s_ref[...] += s

    def spec(shape):
        if shape[0] == 1:
            return pl.BlockSpec(shape, lambda i: (0, 0))
        return pl.BlockSpec((tr, shape[1]), lambda i: (i, 0))

    return pl.pallas_call(
        body, name=name, grid=(t_total // tr,), in_specs=[spec(a.shape) for a in ins],
        out_specs=[spec(o.shape) for o in tile_outs] + [spec(s.shape) for s in sum_outs],
        out_shape=list(tile_outs) + list(sum_outs),
        compiler_params=_params("arbitrary" if sum_outs else "parallel"),
    )(*ins)


def _norm_fwd(name, x, gain):
    def fn(xv, g):
        inv = lax.rsqrt(jnp.mean(xv * xv, axis=-1, keepdims=True) + RMS_EPS)
        return (xv * inv * g,)

    return _rows(name, fn, [x, gain], [jax.ShapeDtypeStruct(x.shape, BF16)])[0]


def _norm_bwd(name, dh, x, gain, dres):
    def fn(dhv, xv, g, dr):
        inv = lax.rsqrt(jnp.mean(xv * xv, axis=-1, keepdims=True) + RMS_EPS)
        xn = xv * inv
        dxn = dhv * g
        dx = inv * (dxn - xn * jnp.mean(dxn * xn, axis=-1, keepdims=True)) + dr
        return dx, jnp.sum(dhv * xn, axis=0, keepdims=True)

    d = x.shape[1]
    return _rows(name, fn, [dh, x, gain, dres], [jax.ShapeDtypeStruct(x.shape, F32)],
                 [jax.ShapeDtypeStruct((1, d), F32)])


def _final_loss(x3, target, gain):
    d = x3.shape[1]

    def fn(xv, tv, g):
        inv = lax.rsqrt(jnp.mean(xv * xv, axis=-1, keepdims=True) + RMS_EPS)
        xn = xv * inv
        err = xn * g - tv
        dy = err * (1.0 / d)
        dxn = dy * g
        dx = inv * (dxn - xn * jnp.mean(dxn * xn, axis=-1, keepdims=True))
        return dx, jnp.sum(dy * xn, axis=0, keepdims=True), (0.5 / d) * jnp.sum(err * err, axis=0, keepdims=True)

    return _rows("final_loss", fn, [x3, target, gain], [jax.ShapeDtypeStruct(x3.shape, F32)],
                 [jax.ShapeDtypeStruct((1, d), F32), jax.ShapeDtypeStruct((1, d), F32)])


def _window_counts(t_pos, w):
    return jnp.minimum(t_pos + 1, w).astype(F32)


def _pool_fwd(u, w_pool, scale, tr=512):
    t_total, width = u.shape
    tr = min(tr, t_total)
    n_groups = len(POOL_WINDOWS)
    gdim = width // n_groups
    ext = tr + POOL_HALO

    def body(u_ref, halo_ref, w_ref, s_ref, pooled_ref, ya_ref):
        i = pl.program_id(0)
        halo = jnp.where(i == 0, 0.0, halo_ref[...])
        t_pos = i * tr + lax.broadcasted_iota(jnp.int32, (tr, 1), 0)
        for g, w in enumerate(POOL_WINDOWS):
            cols = slice(g * gdim, (g + 1) * gdim)
            main = u_ref[:, cols]
            win = jnp.concatenate([halo[:, cols], main], axis=0)
            span = 1
            while span < w:
                win = win + pltpu.roll(win, span, 0)
                span *= 2
            pooled = win[POOL_HALO:, :] * (1.0 / _window_counts(t_pos, w)) - main
            pooled_b = pooled.astype(BF16)
            pooled_ref[:, cols] = pooled_b
            mixed = jnp.dot(pooled_b, w_ref[g], preferred_element_type=F32)
            ya_ref[:, cols] = (mixed * s_ref[:, cols]).astype(BF16)

    hb = tr // POOL_HALO
    return pl.pallas_call(
        body, name="pool_fwd", grid=(t_total // tr,),
        in_specs=[pl.BlockSpec((tr, width), lambda i: (i, 0)),
                  pl.BlockSpec((POOL_HALO, width), lambda i: (jnp.maximum(i * hb - 1, 0), 0)),
                  pl.BlockSpec((n_groups, gdim, gdim), lambda i: (0, 0, 0)),
                  pl.BlockSpec((1, width), lambda i: (0, 0))],
        out_specs=[pl.BlockSpec((tr, width), lambda i: (i, 0)), pl.BlockSpec((tr, width), lambda i: (i, 0))],
        out_shape=[jax.ShapeDtypeStruct(u.shape, BF16), jax.ShapeDtypeStruct(u.shape, BF16)],
        compiler_params=_params("parallel"),
    )(u, u, w_pool, scale)


def _pool_bwd(dya, pooled, w_pool, scale, tr=512):
    t_total, width = dya.shape
    tr = min(tr, t_total)
    n_groups = len(POOL_WINDOWS)
    gdim = width // n_groups
    ext = tr + POOL_HALO
    n_tiles = t_total // tr

    def body(d_ref, halo_ref, p_ref, w_ref, s_ref, du_ref, dw_ref, ds_ref):
        i = pl.program_id(0)

        @pl.when(i == 0)
        def _():
            dw_ref[...] = jnp.zeros_like(dw_ref)
            ds_ref[...] = jnp.zeros_like(ds_ref)

        halo = jnp.where(i == n_tiles - 1, 0.0, halo_ref[...])
        t_pos = i * tr + lax.broadcasted_iota(jnp.int32, (ext, 1), 0)
        for g, w in enumerate(POOL_WINDOWS):
            cols = slice(g * gdim, (g + 1) * gdim)
            sc = s_ref[:, cols]
            d_main = d_ref[:, cols]
            pooled_b = p_ref[:, cols]
            mixed = jnp.dot(pooled_b, w_ref[g], preferred_element_type=F32)
            ds_ref[:, cols] += jnp.sum(d_main * mixed, axis=0, keepdims=True)
            dmix = (jnp.concatenate([d_main, halo[:, cols]], axis=0) * sc).astype(BF16)
            dw_ref[g] += lax.dot_general(pooled_b, dmix[:tr, :], (((0,), (0,)), ((), ())),
                                         preferred_element_type=F32)
            dpool = lax.dot_general(dmix, w_ref[g], (((1,), (1,)), ((), ())), preferred_element_type=F32)
            win = dpool * (1.0 / _window_counts(t_pos, w))
            span = 1
            while span < w:
                win = win + pltpu.roll(win, ext - span, 0)
                span *= 2
            du_ref[:, cols] = (win[:tr, :] - dpool[:tr, :]).astype(BF16)

    hb = tr // POOL_HALO
    last_halo = t_total // POOL_HALO - 1
    return pl.pallas_call(
        body, name="pool_bwd", grid=(n_tiles,),
        in_specs=[pl.BlockSpec((tr, width), lambda i: (i, 0)),
                  pl.BlockSpec((POOL_HALO, width), lambda i: (jnp.minimum((i + 1) * hb, last_halo), 0)),
                  pl.BlockSpec((tr, width), lambda i: (i, 0)),
                  pl.BlockSpec((n_groups, gdim, gdim), lambda i: (0, 0, 0)),
                  pl.BlockSpec((1, width), lambda i: (0, 0))],
        out_specs=[pl.BlockSpec((tr, width), lambda i: (i, 0)),
                   pl.BlockSpec((n_groups, gdim, gdim), lambda i: (0, 0, 0)),
                   pl.BlockSpec((1, width), lambda i: (0, 0))],
        out_shape=[jax.ShapeDtypeStruct(dya.shape, BF16), jax.ShapeDtypeStruct((n_groups, gdim, gdim), F32),
                   jax.ShapeDtypeStruct((1, width), F32)],
        compiler_params=_params("arbitrary"),
    )(dya, dya, pooled, w_pool, scale)


def _head_masks():
    lane = lax.broadcasted_iota(jnp.int32, (1, LANES), 1)
    return lane < HEAD_DIM


def _stack_heads(tile, first):
    zero = jnp.zeros_like(tile)
    return jnp.concatenate([jnp.where(first, tile, zero), jnp.where(first, zero, tile)], axis=0)


def _split_bf16(v):
    hi = v.astype(BF16)
    lo = (v - hi.astype(F32)).astype(BF16)
    return hi, lo


def _slab_scores(q, kd, t_pos, k_start):
    z = lax.dot_general(q, kd, (((1,), (1,)), ((), ())), preferred_element_type=F32)
    col = lax.broadcasted_iota(jnp.int32, (1, 2 * ATT_SLAB), 1)
    mask = k_start + (col & (ATT_SLAB - 1)) < t_pos
    e = jnp.exp(-jnp.abs(z))
    log_fail = jnp.where(mask, -(jnp.maximum(z, 0.0) + jnp.log(1.0 + e)), 0.0)
    return z, mask, log_fail, e


def _tri(upper):
    r = lax.broadcasted_iota(jnp.int32, (ATT_CHUNK, ATT_CHUNK), 0)
    c = lax.broadcasted_iota(jnp.int32, (ATT_CHUNK, ATT_CHUNK), 1)
    return jnp.where(r > c if upper else r < c, 1.0, 0.0).astype(BF16)


def _scan_chunk(v, tri):
    hi, lo = _split_bf16(v)
    return (jnp.dot(hi, tri, preferred_element_type=F32) + jnp.dot(lo, tri, preferred_element_type=F32))


def _lane_bcast(col):
    return jnp.broadcast_to(col, (col.shape[0], LANES))


def _scan_slab(v, tri, carries, from_right):
    n_chunks = ATT_SLAB // ATT_CHUNK
    edge = 0 if from_right else ATT_CHUNK - 1
    parts, new_carries = [None] * (2 * n_chunks), []
    for head in range(2):
        run = carries[head]
        for c in (reversed(range(n_chunks)) if from_right else range(n_chunks)):
            lo_col = head * ATT_SLAB + c * ATT_CHUNK
            vc = v[:, lo_col:lo_col + ATT_CHUNK]
            sc = _scan_chunk(vc, tri)
            parts[head * n_chunks + c] = sc + jnp.concatenate([run] * (ATT_CHUNK // LANES), axis=1)
            run = run + _lane_bcast(sc[:, edge:edge + 1] + vc[:, edge:edge + 1])
        new_carries.append(run)
    return jnp.concatenate(parts, axis=1), new_carries


def _fold_heads(stacked, first):
    s = stacked.shape[0] // 2
    return jnp.where(first, stacked[:s], stacked[s:])


def _attn_fwd(q_src, q_col, kv_src, k_col, v_col, n_pairs=4):
    t_total = q_src.shape[0]
    blk = ATT_BLOCK
    n_blocks = t_total // blk
    assert t_total % ATT_SLAB == 0

    def body(q_ref, k_ref, v_ref, o_ref):
        i = pl.program_id(1)
        first = _head_masks()
        q = q_ref[...] * ATT_SCALE
        t_pos = i * blk + lax.broadcasted_iota(jnp.int32, (blk, 1), 0)
        suffix_tri = _tri(upper=True)

        def more(state):
            slab, reach = state[0], state[1]
            return jnp.logical_and(slab >= 0, reach > ATT_EXIT_BELOW)

        def step(state):
            slab, _, acc, right_a, right_b = state
            k_start = pl.multiple_of(slab * ATT_SLAB, ATT_SLAB)
            kd = _stack_heads(k_ref[pl.ds(k_start, ATT_SLAB), :], first)
            vd = _stack_heads(v_ref[pl.ds(k_start, ATT_SLAB), :], first)
            z, mask, log_fail, _ = _slab_scores(q, kd, t_pos, k_start)
            suffix, (right_a, right_b) = _scan_slab(log_fail, suffix_tri, (right_a, right_b), from_right=True)
            a = jnp.exp(jnp.where(mask, z + log_fail + suffix, -1e30)).astype(BF16)
            acc = acc + jnp.dot(a, vd, preferred_element_type=F32)
            return slab - 1, jnp.max(jnp.maximum(right_a, right_b)), acc, right_a, right_b

        zero = jnp.zeros((blk, LANES), F32)
        state = lax.while_loop(more, step, ((i * blk) // ATT_SLAB, jnp.float32(0.0), zero, zero, zero))
        o_ref[...] = state[2].astype(BF16)

    return pl.pallas_call(
        body, name="attn_fwd", grid=(n_pairs, n_blocks),
        in_specs=[pl.BlockSpec((blk, LANES), lambda h, i: (i, q_col + h)),
                  pl.BlockSpec((t_total, LANES), lambda h, i: (0, k_col + h)),
                  pl.BlockSpec((t_total, LANES), lambda h, i: (0, v_col + h))],
        out_specs=pl.BlockSpec((blk, LANES), lambda h, i: (i, h)),
        out_shape=jax.ShapeDtypeStruct((t_total, n_pairs * LANES), BF16),
        compiler_params=_params("parallel", "parallel"),
    )(q_src, kv_src, kv_src)


def _attn_bwd(q_src, q_col, kv_src, k_col, v_col, dy, n_pairs=4):
    t_total = q_src.shape[0]
    blk = ATT_BLOCK
    n_blocks = t_total // blk
    n_slabs = t_total // ATT_SLAB
    assert t_total % ATT_SLAB == 0

    def body(q_ref, dy_ref, k_ref, v_ref, dq_ref, dk_ref, dv_ref, g_s, sig_s, dk_acc, dv_acc):
        i = pl.program_id(1)

        @pl.when(i == 0)
        def _():
            dk_acc[...] = jnp.zeros_like(dk_acc)
            dv_acc[...] = jnp.zeros_like(dv_acc)

        first = _head_masks()
        q = q_ref[...] * ATT_SCALE
        dy = dy_ref[...]
        t_pos = i * blk + lax.broadcasted_iota(jnp.int32, (blk, 1), 0)
        suffix_tri = _tri(upper=True)
        prefix_tri = _tri(upper=False)
        diag = (i * blk) // ATT_SLAB

        def more(state):
            slab, reach = state[0], state[1]
            return jnp.logical_and(slab >= 0, reach > ATT_EXIT_BELOW)

        def sweep1(state):
            slab, _, right_a, right_b = state
            k_start = pl.multiple_of(slab * ATT_SLAB, ATT_SLAB)
            kd = _stack_heads(k_ref[pl.ds(k_start, ATT_SLAB), :], first)
            vd = _stack_heads(v_ref[pl.ds(k_start, ATT_SLAB), :], first)
            z, mask, log_fail, e = _slab_scores(q, kd, t_pos, k_start)
            suffix, (right_a, right_b) = _scan_slab(log_fail, suffix_tri, (right_a, right_b), from_right=True)
            a = jnp.exp(jnp.where(mask, z + log_fail + suffix, -1e30))
            da = lax.dot_general(dy, vd, (((1,), (1,)), ((), ())), preferred_element_type=F32)
            g_s[slab] = da * a
            sig_s[slab] = jnp.where(z >= 0.0, 1.0, e) / (1.0 + e)
            dv_acc[pl.ds(k_start, ATT_SLAB), :] += _fold_heads(lax.dot_general(
                a.astype(BF16), dy, (((0,), (0,)), ((), ())), preferred_element_type=F32), first)
            return slab - 1, jnp.max(jnp.maximum(right_a, right_b)), right_a, right_b

        zero = jnp.zeros((blk, LANES), F32)
        end = lax.while_loop(more, sweep1, (diag, jnp.float32(0.0), zero, zero))[0]

        def sweep2(slab, carry):
            dq, left_a, left_b = carry
            k_start = pl.multiple_of(slab * ATT_SLAB, ATT_SLAB)
            kd = _stack_heads(k_ref[pl.ds(k_start, ATT_SLAB), :], first)
            g = g_s[slab]
            sig = sig_s[slab]
            prefix, (left_a, left_b) = _scan_slab(g, prefix_tri, (left_a, left_b), from_right=False)
            col = lax.broadcasted_iota(jnp.int32, (1, 2 * ATT_SLAB), 1)
            mask = k_start + (col & (ATT_SLAB - 1)) < t_pos
            dz = jnp.where(mask, g * (1.0 - sig) - sig * prefix, 0.0).astype(BF16)
            dq = dq + jnp.dot(dz, kd, preferred_element_type=F32)
            dk_acc[pl.ds(k_start, ATT_SLAB), :] += _fold_heads(lax.dot_general(
                dz, q, (((0,), (0,)), ((), ())), preferred_element_type=F32), first)
            return dq, left_a, left_b

        dq = lax.fori_loop(end + 1, diag + 1, sweep2, (zero, zero, zero))[0]
        dq_ref[...] = (dq * ATT_SCALE).astype(BF16)

        @pl.when(i == n_blocks - 1)
        def _():
            dk_ref[...] = dk_acc[...].astype(BF16)
            dv_ref[...] = dv_acc[...].astype(BF16)

    out = jax.ShapeDtypeStruct((t_total, n_pairs * LANES), BF16)
    return pl.pallas_call(
        body, name="attn_bwd", grid=(n_pairs, n_blocks),
        in_specs=[pl.BlockSpec((blk, LANES), lambda h, i: (i, q_col + h)),
                  pl.BlockSpec((blk, LANES), lambda h, i: (i, h)),
                  pl.BlockSpec((t_total, LANES), lambda h, i: (0, k_col + h)),
                  pl.BlockSpec((t_total, LANES), lambda h, i: (0, v_col + h))],
        out_specs=[pl.BlockSpec((blk, LANES), lambda h, i: (i, h)),
                   pl.BlockSpec((t_total, LANES), lambda h, i: (0, h)),
                   pl.BlockSpec((t_total, LANES), lambda h, i: (0, h))],
        out_shape=[out, out, out],
        scratch_shapes=[pltpu.VMEM((n_slabs, blk, 2 * ATT_SLAB), F32), pltpu.VMEM((n_slabs, blk, 2 * ATT_SLAB), F32),
                        pltpu.VMEM((t_total, LANES), F32), pltpu.VMEM((t_total, LANES), F32)],
        compiler_params=_params("arbitrary", "arbitrary"),
    )(q_src, dy, kv_src, kv_src)


def _adamw(name, w, g, m, v):
    def fn(wv, gv, mv, vv):
        mn = ADAM_B1 * mv + (1.0 - ADAM_B1) * gv
        vn = ADAM_B2 * vv + (1.0 - ADAM_B2) * (gv * gv)
        m_hat = mn / (1.0 - ADAM_B1 ** ADAM_STEP)
        v_hat = vn / (1.0 - ADAM_B2 ** ADAM_STEP)
        return -ADAM_LR * (m_hat / (jnp.sqrt(v_hat) + ADAM_EPS) + ADAM_WD * wv), mn, vn

    rows = w.shape[0]
    tr = _row_tile(rows)
    shp = jax.ShapeDtypeStruct(w.shape, F32)
    if rows == 1:
        def body(w_ref, g_ref, m_ref, v_ref, d_ref, mo_ref, vo_ref):
            d, mn, vn = fn(w_ref[...], g_ref[...], m_ref[...], v_ref[...])
            d_ref[...], mo_ref[...], vo_ref[...] = d, mn, vn

        return pl.pallas_call(body, name=name, out_shape=[shp, shp, shp])(w, g, m, v)
    return _rows(name, fn, [w, g, m, v], [shp, shp, shp], tr=tr)


def _place():
    return lax.axis_index("x"), lax.axis_index("y"), lax.axis_index("c")


def _other_chips(x, y):
    return [(1 - x, y), (x, 1 - y), (1 - x, 1 - y)]


ANY = pl.BlockSpec(memory_space=pl.ANY)


def _all_gather_weights(shards):
    n_w = len(shards)

    def body(*refs):
        ins, outs = refs[:n_w], refs[n_w:2 * n_w]
        send_sems, recv_sems, pass_send, pass_recv, own_send, own_recv = refs[2 * n_w:]
        x, y, c = _place()
        my_chip = 2 * x + y
        chips = _other_chips(x, y)
        local = [pltpu.make_async_remote_copy(
            src_ref=ins[w], dst_ref=outs[w].at[my_chip], send_sem=own_send.at[w], recv_sem=own_recv.at[w],
            device_id=(x, y, 1 - c), device_id_type=MESH) for w in range(n_w)]
        for cp in local:
            cp.start()

        def half(w, core):
            h = shards[w].shape[0] // 2
            return pl.ds(core * h, h)

        sends = []
        for p, (ox, oy) in enumerate(chips):
            for w in range(n_w):
                sends.append(pltpu.make_async_remote_copy(
                    src_ref=ins[w].at[half(w, c)], dst_ref=outs[w].at[my_chip, half(w, c)],
                    send_sem=send_sems.at[p, w], recv_sem=recv_sems.at[p, w],
                    device_id=(ox, oy, c), device_id_type=MESH))
        for cp in sends:
            cp.start()
        passes = []
        for p, (ox, oy) in enumerate(chips):
            chip = 2 * ox + oy
            for w in range(n_w):
                landed = outs[w].at[chip, half(w, c)]
                pltpu.make_async_remote_copy(
                    src_ref=landed, dst_ref=landed, send_sem=send_sems.at[p, w], recv_sem=recv_sems.at[p, w],
                    device_id=(ox, oy, c), device_id_type=MESH).wait_recv()
                cp = pltpu.make_async_remote_copy(
                    src_ref=landed, dst_ref=landed, send_sem=pass_send.at[p, w], recv_sem=pass_recv.at[p, w],
                    device_id=(x, y, 1 - c), device_id_type=MESH)
                cp.start()
                passes.append(cp)
        for p, (ox, oy) in enumerate(chips):
            chip = 2 * ox + oy
            for w in range(n_w):
                theirs = outs[w].at[chip, half(w, 1 - c)]
                pltpu.make_async_remote_copy(
                    src_ref=theirs, dst_ref=theirs, send_sem=pass_send.at[p, w], recv_sem=pass_recv.at[p, w],
                    device_id=(x, y, 1 - c), device_id_type=MESH).wait_recv()
        for cp in sends + passes:
            cp.wait_send()
        for cp in local:
            cp.wait()

    return pl.pallas_call(
        body, name="all_gather_weights", in_specs=[ANY] * n_w, out_specs=[ANY] * n_w,
        out_shape=[jax.ShapeDtypeStruct((N_CHIPS,) + s.shape, s.dtype) for s in shards],
        scratch_shapes=[pltpu.SemaphoreType.DMA((3, n_w)), pltpu.SemaphoreType.DMA((3, n_w)),
                        pltpu.SemaphoreType.DMA((3, n_w)), pltpu.SemaphoreType.DMA((3, n_w)),
                        pltpu.SemaphoreType.DMA((n_w,)), pltpu.SemaphoreType.DMA((n_w,))],
    )(*shards)


def _pair_exchange(grads):
    n_w = len(grads)

    def halves(w):
        return grads[w].shape[1] // 2

    def body(*refs):
        ins, theirs = refs[:n_w], refs[n_w:2 * n_w]
        send_sems, recv_sems = refs[2 * n_w:]
        x, y, c = _place()
        sends = [pltpu.make_async_remote_copy(
            src_ref=ins[w].at[:, pl.ds((1 - c) * halves(w), halves(w)), :], dst_ref=theirs[w],
            send_sem=send_sems.at[w], recv_sem=recv_sems.at[w], device_id=(x, y, 1 - c), device_id_type=MESH)
            for w in range(n_w)]
        for cp in sends:
            cp.start()
        for cp in sends:
            cp.wait()

    return pl.pallas_call(
        body, name="pair_exchange", in_specs=[ANY] * n_w, out_specs=[ANY] * n_w,
        out_shape=[jax.ShapeDtypeStruct((N_CHIPS, halves(w), grads[w].shape[2]), F32) for w in range(n_w)],
        scratch_shapes=[pltpu.SemaphoreType.DMA((n_w,)), pltpu.SemaphoreType.DMA((n_w,))],
    )(*grads)


def _chip_exchange(pair_sums):
    n_w = len(pair_sums)

    def body(*refs):
        ins, outs = refs[:n_w], refs[n_w:2 * n_w]
        send_sems, recv_sems = refs[2 * n_w:]
        x, y, c = _place()
        sends = []
        for p, (ox, oy) in enumerate(_other_chips(x, y)):
            for w in range(n_w):
                sends.append(pltpu.make_async_remote_copy(
                    src_ref=ins[w].at[2 * ox + oy], dst_ref=outs[w].at[p],
                    send_sem=send_sems.at[p, w], recv_sem=recv_sems.at[p, w],
                    device_id=(ox, oy, c), device_id_type=MESH))
        for cp in sends:
            cp.start()
        for cp in sends:
            cp.wait()

    return pl.pallas_call(
        body, name="chip_exchange", in_specs=[ANY] * n_w, out_specs=[ANY] * n_w,
        out_shape=[jax.ShapeDtypeStruct((3,) + s.shape[1:], s.dtype) for s in pair_sums],
        scratch_shapes=[pltpu.SemaphoreType.DMA((3, n_w)), pltpu.SemaphoreType.DMA((3, n_w))],
    )(*pair_sums)


def _pair_share(shards):
    n_w = len(shards)

    def body(*refs):
        ins, outs = refs[:n_w], refs[n_w:2 * n_w]
        send_sems, recv_sems = refs[2 * n_w:]
        x, y, c = _place()
        sends = []
        for w in range(n_w):
            h = shards[w].shape[0] // 2
            mine = outs[w].at[pl.ds(c * h, h)]
            sends.append(pltpu.make_async_remote_copy(
                src_ref=mine, dst_ref=mine, send_sem=send_sems.at[w], recv_sem=recv_sems.at[w],
                device_id=(x, y, 1 - c), device_id_type=MESH))
        for cp in sends:
            cp.start()
        for w in range(n_w):
            h = shards[w].shape[0] // 2
            theirs = outs[w].at[pl.ds((1 - c) * h, h)]
            pltpu.make_async_remote_copy(
                src_ref=theirs, dst_ref=theirs, send_sem=send_sems.at[w], recv_sem=recv_sems.at[w],
                device_id=(x, y, 1 - c), device_id_type=MESH).wait_recv()
        for cp in sends:
            cp.wait_send()

    return pl.pallas_call(
        body, name="pair_share", in_specs=[ANY] * n_w, out_specs=[ANY] * n_w,
        out_shape=[jax.ShapeDtypeStruct(s.shape, s.dtype) for s in shards],
        input_output_aliases={w: w for w in range(n_w)},
        scratch_shapes=[pltpu.SemaphoreType.DMA((n_w,)), pltpu.SemaphoreType.DMA((n_w,))],
    )(*shards)


def _all_reduce_small(vec):
    rows = vec.shape[0]

    def body(v_ref, o_ref, slots, send_sems, recv_sems):
        x, y, c = _place()
        me = 4 * x + 2 * y + c
        slots[me] = v_ref[...]
        sends = []
        for k in range(1, N_DEV):
            peer = (x ^ (k >> 2), y ^ ((k >> 1) & 1), c ^ (k & 1))
            sends.append(pltpu.make_async_remote_copy(
                src_ref=v_ref, dst_ref=slots.at[me], send_sem=send_sems.at[k - 1], recv_sem=recv_sems.at[k - 1],
                device_id=peer, device_id_type=MESH))
        for cp in sends:
            cp.start()
        for k in range(1, N_DEV):
            px, py, pc = x ^ (k >> 2), y ^ ((k >> 1) & 1), c ^ (k & 1)
            landed = slots.at[4 * px + 2 * py + pc]
            pltpu.make_async_remote_copy(
                src_ref=landed, dst_ref=landed, send_sem=send_sems.at[k - 1], recv_sem=recv_sems.at[k - 1],
                device_id=(px, py, pc), device_id_type=MESH).wait_recv()
        for cp in sends:
            cp.wait_send()
        total = slots[0]
        for d in range(1, N_DEV):
            total = total + slots[d]
        o_ref[...] = total

    vm = pl.BlockSpec(memory_space=pltpu.VMEM)
    return pl.pallas_call(
        body, name="all_reduce_small", in_specs=[vm], out_specs=vm, out_shape=jax.ShapeDtypeStruct(vec.shape, F32),
        scratch_shapes=[pltpu.VMEM((N_DEV, rows, LANES), F32), pltpu.SemaphoreType.DMA((N_DEV - 1,)),
                        pltpu.SemaphoreType.DMA((N_DEV - 1,))],
    )(vec)


def _row_tile(rows):
    for tr in (256, 128, 64, 32, 16):
        if rows % tr == 0:
            return tr
    return rows


def _pair_sum(name, place, grad, theirs):
    n, r, c = grad.shape
    half = r // 2
    tr = _row_tile(half)
    nb = half // tr

    def body(place_ref, g_ref, t_ref, o_ref):
        o_ref[...] = (g_ref[...] + t_ref[...]).astype(BF16)

    return pl.pallas_call(
        body, name=name, out_shape=jax.ShapeDtypeStruct((n, half, c), BF16),
        grid_spec=pltpu.PrefetchScalarGridSpec(
            num_scalar_prefetch=1, grid=(n, nb),
            in_specs=[pl.BlockSpec((1, tr, c), lambda j, i, pr: (j, pr[0] * nb + i, 0)),
                      pl.BlockSpec((1, tr, c), lambda j, i, pr: (j, i, 0))],
            out_specs=pl.BlockSpec((1, tr, c), lambda j, i, pr: (j, i, 0))),
        compiler_params=_params("parallel", "parallel"),
    )(place, grad, theirs)


def _sum_chips(name, place, pair_sums, landed):
    _, half, c = pair_sums.shape
    tr = _row_tile(half)
    nb = half // tr

    def body(place_ref, s_ref, q_ref, o_ref):
        total = s_ref[0].astype(F32)
        for p in range(3):
            total = total + q_ref[p].astype(F32)
        o_ref[...] = total

    return pl.pallas_call(
        body, name=name, out_shape=jax.ShapeDtypeStruct((2 * half, c), F32),
        grid_spec=pltpu.PrefetchScalarGridSpec(
            num_scalar_prefetch=1, grid=(nb,),
            in_specs=[pl.BlockSpec((1, tr, c), lambda i, pr: (pr[1], i, 0)),
                      pl.BlockSpec((3, tr, c), lambda i, pr: (0, i, 0))],
            out_specs=pl.BlockSpec((tr, c), lambda i, pr: (pr[0] * nb + i, 0))),
        compiler_params=_params("parallel"),
    )(place, pair_sums, landed)


BIG = ("w_in", "w_branch_a", "w_branch_b", "w_out", "w_ffn_gate", "w_ffn_up", "w_ffn_down", "w_ple_gate", "w_ple_proj")
COLUMN_SHARDED = ("w_in", "w_branch_a", "w_branch_b", "w_ffn_gate", "w_ffn_up", "w_ple_proj")
SMALL = ("norm_mix", "w_pool", "pool_scale", "norm_ffn", "norm_ple", "norm_final")


def _join_columns(w4):
    return jnp.concatenate([w4[j] for j in range(N_CHIPS)], axis=1)


def _split_columns(g):
    k, n = g.shape
    return g.reshape(k, N_CHIPS, n // N_CHIPS).transpose(1, 0, 2)


def _sds(shape, dtype):
    return jax.ShapeDtypeStruct(shape, dtype)


def _local_step(x, p, target, wf, small):
    t, d = x.shape
    w_in, w_gate, w_up, w_down = wf["w_in"], wf["w_ffn_gate"], wf["w_ffn_up"], wf["w_ffn_down"]
    w_a, w_b, w_pp = _join_columns(wf["w_branch_a"]), _join_columns(wf["w_branch_b"]), _join_columns(wf["w_ple_proj"])
    w_out = wf["w_out"].reshape(d, d)
    w_pg = wf["w_ple_gate"].reshape(d, d)
    w_pool_b = small["w_pool"].astype(BF16)
    dp = w_pool_b.shape[0] * w_pool_b.shape[1]
    dff = w_gate.shape[2]

    h1 = _norm_fwd("norm_mix", x, small["norm_mix"])
    u, q = _mm("proj_uq", [h1], [w_in[0]], "nn", [_sds((t, dp), F32), _sds((t, dp), BF16)],
               epilogue=lambda acc: (acc[:, :dp], acc[:, dp:]))
    kv, = _mm("proj_kv", [h1], [w_in[1]], "nn", [_sds((t, d), BF16)])
    ga, = _mm("proj_ga", [h1], [w_in[2]], "nn", [_sds((t, d), F32)])
    gb, = _mm("proj_gb", [h1], [w_in[3]], "nn", [_sds((t, d), F32)])
    pooled, ya = _pool_fwd(u, w_pool_b, small["pool_scale"])
    n_pairs = dp // LANES
    yb = _attn_fwd(q, 0, kv, 0, n_pairs, n_pairs)
    ta, = _mm("branch_a", [ya], [w_a], "nn", [_sds((t, d), F32)])
    tb, merged = _mm("branch_b_merge", [yb], [w_b], "nn", [_sds((t, d), F32), _sds((t, d), BF16)],
                     extras=[ta, ga, gb],
                     epilogue=lambda acc, tav, gav, gbv: (acc, _sigmoid(gav) * tav + _sigmoid(gbv) * acc))
    x1, = _mm("mix_out", [merged], [w_out], "nn", [_sds((t, d), F32)], extras=[x], epilogue=lambda acc, xv: (acc + xv,))
    h2 = _norm_fwd("norm_ffn", x1, small["norm_ffn"])
    gates, ups, acts = [], [], []
    for j in range(N_CHIPS):
        gj, = _mm(f"ffn_gate{j}", [h2], [w_gate[j]], "nn", [_sds((t, dff), F32)])
        uj, aj = _mm(f"ffn_up{j}", [h2], [w_up[j]], "nn", [_sds((t, dff), F32), _sds((t, dff), BF16)], extras=[gj],
                     epilogue=lambda acc, gv: (acc, gv * _sigmoid(gv) * acc))
        gates.append(gj), ups.append(uj), acts.append(aj)
    x2, = _mm("ffn_down", acts, [w_down[j] for j in range(N_CHIPS)], "nn", [_sds((t, d), F32)], extras=[x1],
              epilogue=lambda acc, xv: (acc + xv,))
    h3 = _norm_fwd("norm_ple", x2, small["norm_ple"])
    gp, = _mm("ple_gate", [h3], [w_pg], "nn", [_sds((t, d), F32)])
    pp, x3 = _mm("ple_proj", [p], [w_pp], "nn", [_sds((t, d), F32), _sds((t, d), F32)], extras=[gp, x2],
                 epilogue=lambda acc, gv, xv: (acc, xv + _sigmoid(gv) * acc))
    (dx3,), (d_norm_final, loss_row) = _split2(_final_loss(x3, target, small["norm_final"].reshape(1, d)), 1)

    def ple_bwd(dxv, gv, pv):
        s = _sigmoid(gv)
        return dxv * s, dxv * pv * s * (1.0 - s)

    d_pp, d_gp = _rows("ple_bwd", ple_bwd, [dx3, gp, pp], [_sds((t, d), BF16), _sds((t, d), BF16)])
    g_w_pp = _mm_tn("g_ple_proj", p, d_pp)
    g_w_pg = _mm_tn("g_ple_gate", h3, d_gp)
    dh3, = _mm("d_h3", [d_gp], [w_pg], "nt", [_sds((t, d), F32)])
    (dx2,), (d_norm_ple,) = _split2(_norm_bwd("norm_ple_bwd", dh3, x2, small["norm_ple"], dx3), 1)

    def ffn_bwd(acc, gv, uv):
        s = _sigmoid(gv)
        return acc * uv * (s * (1.0 + gv * (1.0 - s))), acc * (gv * s)

    d_gates, d_ups, g_w_gate, g_w_up, g_w_down = [], [], [], [], []
    for j in range(N_CHIPS):
        dgj, duj = _mm(f"d_act{j}", [dx2], [w_down[j]], "nt", [_sds((t, dff), BF16), _sds((t, dff), BF16)],
                       extras=[gates[j], ups[j]], epilogue=ffn_bwd)
        d_gates.append(dgj), d_ups.append(duj)
        g_w_down.append(_mm_tn(f"g_ffn_down{j}", acts[j], dx2))
        g_w_gate.append(_mm_tn(f"g_ffn_gate{j}", h2, dgj))
        g_w_up.append(_mm_tn(f"g_ffn_up{j}", h2, duj))
    dh2a, = _mm("d_h2_gate", d_gates, [w_gate[j] for j in range(N_CHIPS)], "nt", [_sds((t, d), F32)])
    dh2, = _mm("d_h2_up", d_ups, [w_up[j] for j in range(N_CHIPS)], "nt", [_sds((t, d), F32)], extras=[dh2a],
               epilogue=lambda acc, prev: (acc + prev,))
    (dx1,), (d_norm_ffn,) = _split2(_norm_bwd("norm_ffn_bwd", dh2, x1, small["norm_ffn"], dx2), 1)

    def merge_bwd(acc, tav, tbv, gav, gbv):
        sa, sb = _sigmoid(gav), _sigmoid(gbv)
        return acc * sa, acc * sb, acc * tav * sa * (1.0 - sa), acc * tbv * sb * (1.0 - sb)

    d_ta, d_tb, d_ga, d_gb = _mm("d_merged", [dx1], [w_out], "nt", [_sds((t, d), BF16)] * 4,
                                 extras=[ta, tb, ga, gb], epilogue=merge_bwd)
    g_w_out = _mm_tn("g_w_out", merged, dx1)
    g_w_a = _mm_tn("g_branch_a", ya, d_ta)
    g_w_b = _mm_tn("g_branch_b", yb, d_tb)
    d_ya, = _mm("d_ya", [d_ta], [w_a], "nt", [_sds((t, dp), F32)])
    d_yb, = _mm("d_yb", [d_tb], [w_b], "nt", [_sds((t, dp), BF16)])
    d_u, g_w_pool, d_pool_scale = _pool_bwd(d_ya, pooled, w_pool_b, small["pool_scale"])
    d_q, d_k, d_v = _attn_bwd(q, 0, kv, 0, n_pairs, d_yb, n_pairs)
    d_proj = [jnp.concatenate([d_u, d_q], axis=1), jnp.concatenate([d_k, d_v], axis=1), d_ga, d_gb]
    g_w_in = [_mm_tn(f"g_w_in{j}", h1, d_proj[j]) for j in range(N_CHIPS)]
    dh1, = _mm("d_h1", d_proj, [w_in[j] for j in range(N_CHIPS)], "nt", [_sds((t, d), F32)])
    (grad_x,), (d_norm_mix,) = _split2(_norm_bwd("norm_mix_bwd", dh1, x, small["norm_mix"], dx1), 1)

    big = {
        "w_in": jnp.stack(g_w_in), "w_branch_a": _split_columns(g_w_a), "w_branch_b": _split_columns(g_w_b),
        "w_out": g_w_out.reshape(wf["w_out"].shape), "w_ffn_gate": jnp.stack(g_w_gate), "w_ffn_up": jnp.stack(g_w_up),
        "w_ffn_down": jnp.stack(g_w_down), "w_ple_gate": g_w_pg.reshape(wf["w_ple_gate"].shape),
        "w_ple_proj": _split_columns(g_w_pp),
    }
    small_g = {"norm_mix": d_norm_mix, "w_pool": g_w_pool, "pool_scale": d_pool_scale, "norm_ffn": d_norm_ffn,
               "norm_ple": d_norm_ple, "norm_final": d_norm_final}
    return grad_x, big, small_g, loss_row


def _split2(res, n):
    return res[:n], res[n:]


def _pack_small(small_g, loss_row):
    parts, layout = [], []
    for name in SMALL + ("loss",):
        v = (loss_row if name == "loss" else small_g[name]).reshape(-1, LANES)
        pad = (-v.shape[0]) % 8
        if pad:
            v = jnp.concatenate([v, jnp.zeros((pad, LANES), F32)], axis=0)
        layout.append((name, sum(q.shape[0] for q in parts), v.shape[0]))
        parts.append(v)
    return jnp.concatenate(parts, axis=0), layout


def kernel(x, p, norm_mix, w_in, w_pool, pool_scale, w_branch_a, w_branch_b, w_out, norm_ffn, w_ffn_gate, w_ffn_up, w_ffn_down, norm_ple, w_ple_gate, w_ple_proj, norm_final, loss_target, m_norm_mix, m_w_in, m_w_pool, m_pool_scale, m_w_branch_a, m_w_branch_b, m_w_out, m_norm_ffn, m_w_ffn_gate, m_w_ffn_up, m_w_ffn_down, m_norm_ple, m_w_ple_gate, m_w_ple_proj, m_norm_final, v_norm_mix, v_w_in, v_w_pool, v_pool_scale, v_w_branch_a, v_w_branch_b, v_w_out, v_norm_ffn, v_w_ffn_gate, v_w_ffn_up, v_w_ffn_down, v_norm_ple, v_w_ple_gate, v_w_ple_proj, v_norm_final):
    given = dict(locals())
    names = BIG + SMALL
    order = ("norm_mix", "w_in", "w_pool", "pool_scale", "w_branch_a", "w_branch_b", "w_out", "norm_ffn", "w_ffn_gate",
             "w_ffn_up", "w_ffn_down", "norm_ple", "w_ple_gate", "w_ple_proj", "norm_final")
    t, d = x.shape[1], x.shape[2]
    shard = {n: given[n][0] for n in BIG}
    small = {"norm_mix": norm_mix, "w_pool": w_pool[0], "pool_scale": pool_scale, "norm_ffn": norm_ffn,
             "norm_ple": norm_ple, "norm_final": norm_final}

    gathered = _all_gather_weights([shard[n].astype(BF16) for n in BIG])
    wf = dict(zip(BIG, gathered))
    grad_x, big_g, small_g, loss_row = _local_step(
        x.reshape(t, d), p.reshape(t, p.shape[-1]), loss_target.reshape(t, d), wf, small)

    place = jnp.stack([lax.axis_index("c"), 2 * lax.axis_index("x") + lax.axis_index("y")]).astype(jnp.int32)
    theirs = _pair_exchange([big_g[n] for n in BIG])
    pair_sums = [_pair_sum(f"pair_sum_{n}", place, big_g[n], t) for n, t in zip(BIG, theirs)]
    landed = _chip_exchange(pair_sums)
    halves = [_sum_chips(f"chip_sum_{n}", place, s, q) for n, s, q in zip(BIG, pair_sums, landed)]
    grads = dict(zip(BIG, _pair_share(halves)))

    packed, layout = _pack_small(small_g, loss_row)
    reduced = _all_reduce_small(packed)
    for name, start, rows in layout:
        if name == "loss":
            loss = jnp.sum(reduced[start:start + rows])
        else:
            n_el = small[name].size
            grads[name] = reduced[start:start + rows].reshape(-1)[:n_el]

    deltas, new_m, new_v = {}, {}, {}
    for n in order:
        w = shard[n] if n in BIG else small[n]
        shape2 = w.shape if w.ndim == 2 else ((1, w.shape[0]) if w.ndim == 1 else (w.shape[0] * w.shape[1], w.shape[2]))
        g2 = grads[n].reshape(shape2)
        dl, mn, vn = _adamw(f"adamw_{n}", w.reshape(shape2), g2, given["m_" + n].reshape(shape2),
                            given["v_" + n].reshape(shape2))
        full = given[n].shape
        grads[n], deltas[n], new_m[n], new_v[n] = g2.reshape(full), dl.reshape(full), mn.reshape(full), vn.reshape(full)

    return (loss, grad_x.reshape(x.shape), *[grads[n] for n in order], *[deltas[n] for n in order],
            *[new_m[n] for n in order], *[new_v[n] for n in order])
```

```python
import functools
import math

import jax
import jax.numpy as jnp
from jax import lax
from jax.experimental import pallas as pl
from jax.experimental.pallas import tpu as pltpu

F32 = jnp.float32
BF16 = jnp.bfloat16
MESH = pl.DeviceIdType.MESH

RMS_EPS = 1e-6
POOL_WINDOWS = (2, 4, 8, 16)
POOL_HALO = 16
HEAD_DIM = 64
LANES = 128
ATT_BLOCK = 128
ATT_CHUNK = 256
ATT_SLAB = 256
ATT_SCALE = 1.0 / math.sqrt(HEAD_DIM)
ATT_EXIT_BELOW = -104.0
ADAM_LR, ADAM_B1, ADAM_B2, ADAM_EPS, ADAM_WD, ADAM_STEP = 0.001, 0.9, 0.999, 1e-08, 0.01, 10
V7X_VMEM_LIMIT_BYTES = 56 * 1024 * 1024
N_CHIPS = 4
N_DEV = 8


def _params(*semantics):
    return pltpu.CompilerParams(dimension_semantics=semantics, vmem_limit_bytes=V7X_VMEM_LIMIT_BYTES)


def _sigmoid(z):
    return 1.0 / (1.0 + jnp.exp(-z))


def _tiled_spec(shape, tm, tn, n_total):
    rows, width = shape
    if rows == 1:
        if width == n_total:
            return pl.BlockSpec((1, tn), lambda i, j: (0, j))
        return pl.BlockSpec((1, width), lambda i, j: (0, 0))
    if width == n_total:
        return pl.BlockSpec((tm, tn), lambda i, j: (i, j))
    assert tn == n_total, "an operand narrower than the output needs whole output rows per tile"
    return pl.BlockSpec((tm, width), lambda i, j: (i, 0))


def _mm(name, a_list, b_list, mode, out_shapes, epilogue=None, extras=(), tm=1024, tn=None, separate=False):
    m_total = a_list[0].shape[0]
    n_total = b_list[0].shape[1] if mode == "nn" else b_list[0].shape[0]
    tn = n_total if tn is None else tn
    tm = min(tm, m_total)
    assert m_total % tm == 0 and n_total % tn == 0
    n_a, n_b, n_extra = len(a_list), len(b_list), len(extras)
    assert n_a in (1, n_b)
    dims = (((1,), (0,)), ((), ())) if mode == "nn" else (((1,), (1,)), ((), ()))

    def body(*refs):
        a_refs, b_refs = refs[:n_a], refs[n_a:n_a + n_b]
        e_refs = refs[n_a + n_b:n_a + n_b + n_extra]
        o_refs = refs[n_a + n_b + n_extra:]
        lefts = [a_ref[...] for a_ref in a_refs]
        lefts = [a if a.dtype == BF16 else a.astype(BF16) for a in lefts]
        products = [lax.dot_general(lefts[s % n_a], b_refs[s][...], dims, preferred_element_type=F32)
                    for s in range(n_b)]
        if not separate:
            products = [functools.reduce(lambda p, r: p + r, products)]
        extra_tiles = [e[...].astype(F32) for e in e_refs]
        outs = products if epilogue is None else epilogue(*products, *extra_tiles)
        for o_ref, o in zip(o_refs, outs):
            o_ref[...] = o.astype(o_ref.dtype)

    in_specs = [pl.BlockSpec((tm, a.shape[1]), lambda i, j: (i, 0)) for a in a_list]
    if mode == "nn":
        in_specs += [pl.BlockSpec((b.shape[0], tn), lambda i, j: (0, j)) for b in b_list]
    else:
        in_specs += [pl.BlockSpec((tn, b.shape[1]), lambda i, j: (j, 0)) for b in b_list]
    in_specs += [_tiled_spec(e.shape, tm, tn, n_total) for e in extras]
    out_specs = [_tiled_spec(o.shape, tm, tn, n_total) for o in out_shapes]
    res = pl.pallas_call(
        body, name=name, grid=(m_total // tm, n_total // tn), in_specs=in_specs, out_specs=out_specs,
        out_shape=list(out_shapes), compiler_params=_params("parallel", "parallel"),
    )(*a_list, *b_list, *extras)
    return res


def _mm_tn(name, a, b, tmm=1024):
    m_total, k = a.shape
    n = b.shape[1]
    tmm = min(tmm, m_total)
    assert m_total % tmm == 0

    def body(a_ref, b_ref, o_ref):
        @pl.when(pl.program_id(0) == 0)
        def _():
            o_ref[...] = jnp.zeros_like(o_ref)

        av, bv = a_ref[...], b_ref[...]
        if av.dtype != BF16:
            av = av.astype(BF16)
        if bv.dtype != BF16:
            bv = bv.astype(BF16)
        o_ref[...] += lax.dot_general(av, bv, (((0,), (0,)), ((), ())), preferred_element_type=F32)

    return pl.pallas_call(
        body, name=name, grid=(m_total // tmm,),
        in_specs=[pl.BlockSpec((tmm, k), lambda m: (m, 0)), pl.BlockSpec((tmm, n), lambda m: (m, 0))],
        out_specs=pl.BlockSpec((k, n), lambda m: (0, 0)),
        out_shape=jax.ShapeDtypeStruct((k, n), F32), compiler_params=_params("arbitrary"),
    )(a, b)


def _rows(name, fn, ins, tile_outs, sum_outs=(), tr=512):
    t_total = max(a.shape[0] for a in ins)
    tr = min(tr, t_total)
    assert t_total % tr == 0
    n_in, n_tile = len(ins), len(tile_outs)

    def body(*refs):
        outs = fn(*[r[...].astype(F32) for r in refs[:n_in]])
        for o_ref, o in zip(refs[n_in:n_in + n_tile], outs[:n_tile]):
            o_ref[...] = o.astype(o_ref.dtype)
        if sum_outs:
            @pl.when(pl.program_id(0) == 0)
            def _():
                for s_ref in refs[n_in + n_tile:]:
                    s_ref[...] = jnp.zeros_like(s_ref)

            for s_ref, s in zip(refs[n_in + n_tile:], outs[n_tile:]):
                s_ref[...] += s

    def spec(shape):
        if shape[0] == 1:
            return pl.BlockSpec(shape, lambda i: (0, 0))
        return pl.BlockSpec((tr, shape[1]), lambda i: (i, 0))

    return pl.pallas_call(
        body, name=name, grid=(t_total // tr,), in_specs=[spec(a.shape) for a in ins],
        out_specs=[spec(o.shape) for o in tile_outs] + [spec(s.shape) for s in sum_outs],
        out_shape=list(tile_outs) + list(sum_outs),
        compiler_params=_params("arbitrary" if sum_outs else "parallel"),
    )(*ins)


def _norm_fwd(name, x, gain):
    def fn(xv, g):
        inv = lax.rsqrt(jnp.mean(xv * xv, axis=-1, keepdims=True) + RMS_EPS)
        return (xv * inv * g,)

    return _rows(name, fn, [x, gain], [jax.ShapeDtypeStruct(x.shape, BF16)])[0]


def _norm_bwd(name, dh, x, gain, dres):
    def fn(dhv, xv, g, dr):
        inv = lax.rsqrt(jnp.mean(xv * xv, axis=-1, keepdims=True) + RMS_EPS)
        xn = xv * inv
        dxn = dhv * g
        dx = inv * (dxn - xn * jnp.mean(dxn * xn, axis=-1, keepdims=True)) + dr
        return dx, jnp.sum(dhv * xn, axis=0, keepdims=True)

    d = x.shape[1]
    return _rows(name, fn, [dh, x, gain, dres], [jax.ShapeDtypeStruct(x.shape, F32)],
                 [jax.ShapeDtypeStruct((1, d), F32)])


def _final_loss(x3, target, gain):
    d = x3.shape[1]

    def fn(xv, tv, g):
        inv = lax.rsqrt(jnp.mean(xv * xv, axis=-1, keepdims=True) + RMS_EPS)
        xn = xv * inv
        err = xn * g - tv
        dy = err * (1.0 / d)
        dxn = dy * g
        dx = inv * (dxn - xn * jnp.mean(dxn * xn, axis=-1, keepdims=True))
        return dx, jnp.sum(dy * xn, axis=0, keepdims=True), (0.5 / d) * jnp.sum(err * err, axis=0, keepdims=True)

    return _rows("final_loss", fn, [x3, target, gain], [jax.ShapeDtypeStruct(x3.shape, F32)],
                 [jax.ShapeDtypeStruct((1, d), F32), jax.ShapeDtypeStruct((1, d), F32)])


def _window_counts(t_pos, w):
    return jnp.minimum(t_pos + 1, w).astype(F32)


def _pool_fwd(u, w_pool, scale, tr=512):
    t_total, width = u.shape
    tr = min(tr, t_total)
    n_groups = len(POOL_WINDOWS)
    gdim = width // n_groups
    ext = tr + POOL_HALO

    def body(u_ref, halo_ref, w_ref, s_ref, pooled_ref, ya_ref):
        i = pl.program_id(0)
        halo = jnp.where(i == 0, 0.0, halo_ref[...])
        t_pos = i * tr + lax.broadcasted_iota(jnp.int32, (tr, 1), 0)
        for g, w in enumerate(POOL_WINDOWS):
            cols = slice(g * gdim, (g + 1) * gdim)
            main = u_ref[:, cols]
            win = jnp.concatenate([halo[:, cols], main], axis=0)
            span = 1
            while span < w:
                win = win + pltpu.roll(win, span, 0)
                span *= 2
            pooled = win[POOL_HALO:, :] * (1.0 / _window_counts(t_pos, w)) - main
            pooled_b = pooled.astype(BF16)
            pooled_ref[:, cols] = pooled_b
            mixed = jnp.dot(pooled_b, w_ref[g], preferred_element_type=F32)
            ya_ref[:, cols] = (mixed * s_ref[:, cols]).astype(BF16)

    hb = tr // POOL_HALO
    return pl.pallas_call(
        body, name="pool_fwd", grid=(t_total // tr,),
        in_specs=[pl.BlockSpec((tr, width), lambda i: (i, 0)),
                  pl.BlockSpec((POOL_HALO, width), lambda i: (jnp.maximum(i * hb - 1, 0), 0)),
                  pl.BlockSpec((n_groups, gdim, gdim), lambda i: (0, 0, 0)),
                  pl.BlockSpec((1, width), lambda i: (0, 0))],
        out_specs=[pl.BlockSpec((tr, width), lambda i: (i, 0)), pl.BlockSpec((tr, width), lambda i: (i, 0))],
        out_shape=[jax.ShapeDtypeStruct(u.shape, BF16), jax.ShapeDtypeStruct(u.shape, BF16)],
        compiler_params=_params("parallel"),
    )(u, u, w_pool, scale)


def _pool_bwd(dya, pooled, w_pool, scale, tr=512):
    t_total, width = dya.shape
    tr = min(tr, t_total)
    n_groups = len(POOL_WINDOWS)
    gdim = width // n_groups
    ext = tr + POOL_HALO
    n_tiles = t_total // tr

    def body(d_ref, halo_ref, p_ref, w_ref, s_ref, du_ref, dw_ref, ds_ref):
        i = pl.program_id(0)

        @pl.when(i == 0)
        def _():
            dw_ref[...] = jnp.zeros_like(dw_ref)
            ds_ref[...] = jnp.zeros_like(ds_ref)

        halo = jnp.where(i == n_tiles - 1, 0.0, halo_ref[...])
        t_pos = i * tr + lax.broadcasted_iota(jnp.int32, (ext, 1), 0)
        for g, w in enumerate(POOL_WINDOWS):
            cols = slice(g * gdim, (g + 1) * gdim)
            sc = s_ref[:, cols]
            d_main = d_ref[:, cols]
            pooled_b = p_ref[:, cols]
            mixed = jnp.dot(pooled_b, w_ref[g], preferred_element_type=F32)
            ds_ref[:, cols] += jnp.sum(d_main * mixed, axis=0, keepdims=True)
            dmix = (jnp.concatenate([d_main, halo[:, cols]], axis=0) * sc).astype(BF16)
            dw_ref[g] += lax.dot_general(pooled_b, dmix[:tr, :], (((0,), (0,)), ((), ())),
                                         preferred_element_type=F32)
            dpool = lax.dot_general(dmix, w_ref[g], (((1,), (1,)), ((), ())), preferred_element_type=F32)
            win = dpool * (1.0 / _window_counts(t_pos, w))
            span = 1
            while span < w:
                win = win + pltpu.roll(win, ext - span, 0)
                span *= 2
            du_ref[:, cols] = (win[:tr, :] - dpool[:tr, :]).astype(BF16)

    hb = tr // POOL_HALO
    last_halo = t_total // POOL_HALO - 1
    return pl.pallas_call(
        body, name="pool_bwd", grid=(n_tiles,),
        in_specs=[pl.BlockSpec((tr, width), lambda i: (i, 0)),
                  pl.BlockSpec((POOL_HALO, width), lambda i: (jnp.minimum((i + 1) * hb, last_halo), 0)),
                  pl.BlockSpec((tr, width), lambda i: (i, 0)),
                  pl.BlockSpec((n_groups, gdim, gdim), lambda i: (0, 0, 0)),
                  pl.BlockSpec((1, width), lambda i: (0, 0))],
        out_specs=[pl.BlockSpec((tr, width), lambda i: (i, 0)),
                   pl.BlockSpec((n_groups, gdim, gdim), lambda i: (0, 0, 0)),
                   pl.BlockSpec((1, width), lambda i: (0, 0))],
        out_shape=[jax.ShapeDtypeStruct(dya.shape, BF16), jax.ShapeDtypeStruct((n_groups, gdim, gdim), F32),
                   jax.ShapeDtypeStruct((1, width), F32)],
        compiler_params=_params("arbitrary"),
    )(dya, dya, pooled, w_pool, scale)


def _head_masks():
    lane = lax.broadcasted_iota(jnp.int32, (1, LANES), 1)
    return lane < HEAD_DIM


def _stack_heads(tile, first):
    zero = jnp.zeros_like(tile)
    return jnp.concatenate([jnp.where(first, tile, zero), jnp.where(first, zero, tile)], axis=0)


def _split_bf16(v):
    hi = v.astype(BF16)
    lo = (v - hi.astype(F32)).astype(BF16)
    return hi, lo


def _slab_scores(q, kd, t_pos, k_start):
    z = lax.dot_general(q, kd, (((1,), (1,)), ((), ())), preferred_element_type=F32)
    col = lax.broadcasted_iota(jnp.int32, (1, 2 * ATT_SLAB), 1)
    mask = k_start + (col & (ATT_SLAB - 1)) < t_pos
    e = jnp.exp(-jnp.abs(z))
    log_fail = jnp.where(mask, -(jnp.maximum(z, 0.0) + jnp.log(1.0 + e)), 0.0)
    return z, mask, log_fail, e


def _tri(upper):
    r = lax.broadcasted_iota(jnp.int32, (ATT_CHUNK, ATT_CHUNK), 0)
    c = lax.broadcasted_iota(jnp.int32, (ATT_CHUNK, ATT_CHUNK), 1)
    return jnp.where(r > c if upper else r < c, 1.0, 0.0).astype(BF16)


def _scan_chunk(v, tri):
    hi, lo = _split_bf16(v)
    return (jnp.dot(hi, tri, preferred_element_type=F32) + jnp.dot(lo, tri, preferred_element_type=F32))


def _lane_bcast(col):
    return jnp.broadcast_to(col, (col.shape[0], LANES))


def _scan_slab(v, tri, carries, from_right):
    n_chunks = ATT_SLAB // ATT_CHUNK
    edge = 0 if from_right else ATT_CHUNK - 1
    parts, new_carries = [None] * (2 * n_chunks), []
    for head in range(2):
        run = carries[head]
        for c in (reversed(range(n_chunks)) if from_right else range(n_chunks)):
            lo_col = head * ATT_SLAB + c * ATT_CHUNK
            vc = v[:, lo_col:lo_col + ATT_CHUNK]
            sc = _scan_chunk(vc, tri)
            parts[head * n_chunks + c] = sc + jnp.concatenate([run] * (ATT_CHUNK // LANES), axis=1)
            run = run + _lane_bcast(sc[:, edge:edge + 1] + vc[:, edge:edge + 1])
        new_carries.append(run)
    return jnp.concatenate(parts, axis=1), new_carries


def _fold_heads(stacked, first):
    s = stacked.shape[0] // 2
    return jnp.where(first, stacked[:s], stacked[s:])


def _attn_fwd(q_src, q_col, kv_src, k_col, v_col, n_pairs=4):
    t_total = q_src.shape[0]
    blk = ATT_BLOCK
    n_blocks = t_total // blk
    assert t_total % ATT_SLAB == 0

    def body(q_ref, k_ref, v_ref, o_ref):
        i = pl.program_id(1)
        first = _head_masks()
        q = q_ref[...] * ATT_SCALE
        t_pos = i * blk + lax.broadcasted_iota(jnp.int32, (blk, 1), 0)
        suffix_tri = _tri(upper=True)

        def more(state):
            slab, reach = state[0], state[1]
            return jnp.logical_and(slab >= 0, reach > ATT_EXIT_BELOW)

        def step(state):
            slab, _, acc, right_a, right_b = state
            k_start = pl.multiple_of(slab * ATT_SLAB, ATT_SLAB)
            kd = _stack_heads(k_ref[pl.ds(k_start, ATT_SLAB), :], first)
            vd = _stack_heads(v_ref[pl.ds(k_start, ATT_SLAB), :], first)
            z, mask, log_fail, _ = _slab_scores(q, kd, t_pos, k_start)
            suffix, (right_a, right_b) = _scan_slab(log_fail, suffix_tri, (right_a, right_b), from_right=True)
            a = jnp.exp(jnp.where(mask, z + log_fail + suffix, -1e30)).astype(BF16)
            acc = acc + jnp.dot(a, vd, preferred_element_type=F32)
            return slab - 1, jnp.max(jnp.maximum(right_a, right_b)), acc, right_a, right_b

        zero = jnp.zeros((blk, LANES), F32)
        state = lax.while_loop(more, step, ((i * blk) // ATT_SLAB, jnp.float32(0.0), zero, zero, zero))
        o_ref[...] = state[2].astype(BF16)

    return pl.pallas_call(
        body, name="attn_fwd", grid=(n_pairs, n_blocks),
        in_specs=[pl.BlockSpec((blk, LANES), lambda h, i: (i, q_col + h)),
                  pl.BlockSpec((t_total, LANES), lambda h, i: (0, k_col + h)),
                  pl.BlockSpec((t_total, LANES), lambda h, i: (0, v_col + h))],
        out_specs=pl.BlockSpec((blk, LANES), lambda h, i: (i, h)),
        out_shape=jax.ShapeDtypeStruct((t_total, n_pairs * LANES), BF16),
        compiler_params=_params("parallel", "parallel"),
    )(q_src, kv_src, kv_src)


def _attn_bwd(q_src, q_col, kv_src, k_col, v_col, dy, n_pairs=4):
    t_total = q_src.shape[0]
    blk = ATT_BLOCK
    n_blocks = t_total // blk
    n_slabs = t_total // ATT_SLAB
    assert t_total % ATT_SLAB == 0

    def body(q_ref, dy_ref, k_ref, v_ref, dq_ref, dk_ref, dv_ref, g_s, sig_s, dk_acc, dv_acc):
        i = pl.program_id(1)

        @pl.when(i == 0)
        def _():
            dk_acc[...] = jnp.zeros_like(dk_acc)
            dv_acc[...] = jnp.zeros_like(dv_acc)

        first = _head_masks()
        q = q_ref[...] * ATT_SCALE
        dy = dy_ref[...]
        t_pos = i * blk + lax.broadcasted_iota(jnp.int32, (blk, 1), 0)
        suffix_tri = _tri(upper=True)
        prefix_tri = _tri(upper=False)
        diag = (i * blk) // ATT_SLAB

        def more(state):
            slab, reach = state[0], state[1]
            return jnp.logical_and(slab >= 0, reach > ATT_EXIT_BELOW)

        def sweep1(state):
            slab, _, right_a, right_b = state
            k_start = pl.multiple_of(slab * ATT_SLAB, ATT_SLAB)
            kd = _stack_heads(k_ref[pl.ds(k_start, ATT_SLAB), :], first)
            vd = _stack_heads(v_ref[pl.ds(k_start, ATT_SLAB), :], first)
            z, mask, log_fail, e = _slab_scores(q, kd, t_pos, k_start)
            suffix, (right_a, right_b) = _scan_slab(log_fail, suffix_tri, (right_a, right_b), from_right=True)
            a = jnp.exp(jnp.where(mask, z + log_fail + suffix, -1e30))
            da = lax.dot_general(dy, vd, (((1,), (1,)), ((), ())), preferred_element_type=F32)
            g_s[slab] = da * a
            sig_s[slab] = jnp.where(z >= 0.0, 1.0, e) / (1.0 + e)
            dv_acc[pl.ds(k_start, ATT_SLAB), :] += _fold_heads(lax.dot_general(
                a.astype(BF16), dy, (((0,), (0,)), ((), ())), preferred_element_type=F32), first)
            return slab - 1, jnp.max(jnp.maximum(right_a, right_b)), right_a, right_b

        zero = jnp.zeros((blk, LANES), F32)
        end = lax.while_loop(more, sweep1, (diag, jnp.float32(0.0), zero, zero))[0]

        def sweep2(slab, carry):
            dq, left_a, left_b = carry
            k_start = pl.multiple_of(slab * ATT_SLAB, ATT_SLAB)
            kd = _stack_heads(k_ref[pl.ds(k_start, ATT_SLAB), :], first)
            g = g_s[slab]
            sig = sig_s[slab]
            prefix, (left_a, left_b) = _scan_slab(g, prefix_tri, (left_a, left_b), from_right=False)
            col = lax.broadcasted_iota(jnp.int32, (1, 2 * ATT_SLAB), 1)
            mask = k_start + (col & (ATT_SLAB - 1)) < t_pos
            dz = jnp.where(mask, g * (1.0 - sig) - sig * prefix, 0.0).astype(BF16)
            dq = dq + jnp.dot(dz, kd, preferred_element_type=F32)
            dk_acc[pl.ds(k_start, ATT_SLAB), :] += _fold_heads(lax.dot_general(
                dz, q, (((0,), (0,)), ((), ())), preferred_element_type=F32), first)
            return dq, left_a, left_b

        dq = lax.fori_loop(end + 1, diag + 1, sweep2, (zero, zero, zero))[0]
        dq_ref[...] = (dq * ATT_SCALE).astype(BF16)

        @pl.when(i == n_blocks - 1)
        def _():
            dk_ref[...] = dk_acc[...].astype(BF16)
            dv_ref[...] = dv_acc[...].astype(BF16)

    out = jax.ShapeDtypeStruct((t_total, n_pairs * LANES), BF16)
    return pl.pallas_call(
        body, name="attn_bwd", grid=(n_pairs, n_blocks),
        in_specs=[pl.BlockSpec((blk, LANES), lambda h, i: (i, q_col + h)),
                  pl.BlockSpec((blk, LANES), lambda h, i: (i, h)),
                  pl.BlockSpec((t_total, LANES), lambda h, i: (0, k_col + h)),
                  pl.BlockSpec((t_total, LANES), lambda h, i: (0, v_col + h))],
        out_specs=[pl.BlockSpec((blk, LANES), lambda h, i: (i, h)),
                   pl.BlockSpec((t_total, LANES), lambda h, i: (0, h)),
                   pl.BlockSpec((t_total, LANES), lambda h, i: (0, h))],
        out_shape=[out, out, out],
        scratch_shapes=[pltpu.VMEM((n_slabs, blk, 2 * ATT_SLAB), F32), pltpu.VMEM((n_slabs, blk, 2 * ATT_SLAB), F32),
                        pltpu.VMEM((t_total, LANES), F32), pltpu.VMEM((t_total, LANES), F32)],
        compiler_params=_params("arbitrary", "arbitrary"),
    )(q_src, dy, kv_src, kv_src)


def _adamw(name, w, g, m, v):
    def fn(wv, gv, mv, vv):
        mn = ADAM_B1 * mv + (1.0 - ADAM_B1) * gv
        vn = ADAM_B2 * vv + (1.0 - ADAM_B2) * (gv * gv)
        m_hat = mn / (1.0 - ADAM_B1 ** ADAM_STEP)
        v_hat = vn / (1.0 - ADAM_B2 ** ADAM_STEP)
        return -ADAM_LR * (m_hat / (jnp.sqrt(v_hat) + ADAM_EPS) + ADAM_WD * wv), mn, vn

    rows = w.shape[0]
    tr = _row_tile(rows)
    shp = jax.ShapeDtypeStruct(w.shape, F32)
    if rows == 1:
        def body(w_ref, g_ref, m_ref, v_ref, d_ref, mo_ref, vo_ref):
            d, mn, vn = fn(w_ref[...], g_ref[...], m_ref[...], v_ref[...])
            d_ref[...], mo_ref[...], vo_ref[...] = d, mn, vn

        return pl.pallas_call(body, name=name, out_shape=[shp, shp, shp])(w, g, m, v)
    return _rows(name, fn, [w, g, m, v], [shp, shp, shp], tr=tr)


def _place():
    return lax.axis_index("x"), lax.axis_index("y"), lax.axis_index("c")


def _other_chips(x, y):
    return [(1 - x, y), (x, 1 - y), (1 - x, 1 - y)]


ANY = pl.BlockSpec(memory_space=pl.ANY)


def _all_gather_weights(shards):
    n_w = len(shards)

    def body(*refs):
        ins, outs = refs[:n_w], refs[n_w:2 * n_w]
        send_sems, recv_sems, pass_send, pass_recv, own_send, own_recv = refs[2 * n_w:]
        x, y, c = _place()
        my_chip = 2 * x + y
        chips = _other_chips(x, y)
        local = [pltpu.make_async_remote_copy(
            src_ref=ins[w], dst_ref=outs[w].at[my_chip], send_sem=own_send.at[w], recv_sem=own_recv.at[w],
            device_id=(x, y, 1 - c), device_id_type=MESH) for w in range(n_w)]
        for cp in local:
            cp.start()

        def half(w, core):
            h = shards[w].shape[0] // 2
            return pl.ds(core * h, h)

        sends = []
        for p, (ox, oy) in enumerate(chips):
            for w in range(n_w):
                sends.append(pltpu.make_async_remote_copy(
                    src_ref=ins[w].at[half(w, c)], dst_ref=outs[w].at[my_chip, half(w, c)],
                    send_sem=send_sems.at[p, w], recv_sem=recv_sems.at[p, w],
                    device_id=(ox, oy, c), device_id_type=MESH))
        for cp in sends:
            cp.start()
        passes = []
        for p, (ox, oy) in enumerate(chips):
            chip = 2 * ox + oy
            for w in range(n_w):
                landed = outs[w].at[chip, half(w, c)]
                pltpu.make_async_remote_copy(
                    src_ref=landed, dst_ref=landed, send_sem=send_sems.at[p, w], recv_sem=recv_sems.at[p, w],
                    device_id=(ox, oy, c), device_id_type=MESH).wait_recv()
                cp = pltpu.make_async_remote_copy(
                    src_ref=landed, dst_ref=landed, send_sem=pass_send.at[p, w], recv_sem=pass_recv.at[p, w],
                    device_id=(x, y, 1 - c), device_id_type=MESH)
                cp.start()
                passes.append(cp)
        for p, (ox, oy) in enumerate(chips):
            chip = 2 * ox + oy
            for w in range(n_w):
                theirs = outs[w].at[chip, half(w, 1 - c)]
                pltpu.make_async_remote_copy(
                    src_ref=theirs, dst_ref=theirs, send_sem=pass_send.at[p, w], recv_sem=pass_recv.at[p, w],
                    device_id=(x, y, 1 - c), device_id_type=MESH).wait_recv()
        for cp in sends + passes:
            cp.wait_send()
        for cp in local:
            cp.wait()

    return pl.pallas_call(
        body, name="all_gather_weights", in_specs=[ANY] * n_w, out_specs=[ANY] * n_w,
        out_shape=[jax.ShapeDtypeStruct((N_CHIPS,) + s.shape, s.dtype) for s in shards],
        scratch_shapes=[pltpu.SemaphoreType.DMA((3, n_w)), pltpu.SemaphoreType.DMA((3, n_w)),
                        pltpu.SemaphoreType.DMA((3, n_w)), pltpu.SemaphoreType.DMA((3, n_w)),
                        pltpu.SemaphoreType.DMA((n_w,)), pltpu.SemaphoreType.DMA((n_w,))],
    )(*shards)


def _pair_exchange(grads):
    n_w = len(grads)

    def halves(w):
        return grads[w].shape[1] // 2

    def body(*refs):
        ins, theirs = refs[:n_w], refs[n_w:2 * n_w]
        send_sems, recv_sems = refs[2 * n_w:]
        x, y, c = _place()
        sends = [pltpu.make_async_remote_copy(
            src_ref=ins[w].at[:, pl.ds((1 - c) * halves(w), halves(w)), :], dst_ref=theirs[w],
            send_sem=send_sems.at[w], recv_sem=recv_sems.at[w], device_id=(x, y, 1 - c), device_id_type=MESH)
            for w in range(n_w)]
        for cp in sends:
            cp.start()
        for cp in sends:
            cp.wait()

    return pl.pallas_call(
        body, name="pair_exchange", in_specs=[ANY] * n_w, out_specs=[ANY] * n_w,
        out_shape=[jax.ShapeDtypeStruct((N_CHIPS, halves(w), grads[w].shape[2]), F32) for w in range(n_w)],
        scratch_shapes=[pltpu.SemaphoreType.DMA((n_w,)), pltpu.SemaphoreType.DMA((n_w,))],
    )(*grads)


def _chip_exchange(pair_sums):
    n_w = len(pair_sums)

    def body(*refs):
        ins, outs = refs[:n_w], refs[n_w:2 * n_w]
        send_sems, recv_sems = refs[2 * n_w:]
        x, y, c = _place()
        sends = []
        for p, (ox, oy) in enumerate(_other_chips(x, y)):
            for w in range(n_w):
                sends.append(pltpu.make_async_remote_copy(
                    src_ref=ins[w].at[2 * ox + oy], dst_ref=outs[w].at[p],
                    send_sem=send_sems.at[p, w], recv_sem=recv_sems.at[p, w],
                    device_id=(ox, oy, c), device_id_type=MESH))
        for cp in sends:
            cp.start()
        for cp in sends:
            cp.wait()

    return pl.pallas_call(
        body, name="chip_exchange", in_specs=[ANY] * n_w, out_specs=[ANY] * n_w,
        out_shape=[jax.ShapeDtypeStruct((3,) + s.shape[1:], s.dtype) for s in pair_sums],
        scratch_shapes=[pltpu.SemaphoreType.DMA((3, n_w)), pltpu.SemaphoreType.DMA((3, n_w))],
    )(*pair_sums)


def _pair_share(shards):
    n_w = len(shards)

    def body(*refs):
        ins, outs = refs[:n_w], refs[n_w:2 * n_w]
        send_sems, recv_sems = refs[2 * n_w:]
        x, y, c = _place()
        sends = []
        for w in range(n_w):
            h = shards[w].shape[0] // 2
            mine = outs[w].at[pl.ds(c * h, h)]
            sends.append(pltpu.make_async_remote_copy(
                src_ref=mine, dst_ref=mine, send_sem=send_sems.at[w], recv_sem=recv_sems.at[w],
                device_id=(x, y, 1 - c), device_id_type=MESH))
        for cp in sends:
            cp.start()
        for w in range(n_w):
            h = shards[w].shape[0] // 2
            theirs = outs[w].at[pl.ds((1 - c) * h, h)]
            pltpu.make_async_remote_copy(
                src_ref=theirs, dst_ref=theirs, send_sem=send_sems.at[w], recv_sem=recv_sems.at[w],
                device_id=(x, y, 1 - c), device_id_type=MESH).wait_recv()
        for cp in sends:
            cp.wait_send()

    return pl.pallas_call(
        body, name="pair_share", in_specs=[ANY] * n_w, out_specs=[ANY] * n_w,
        out_shape=[jax.ShapeDtypeStruct(s.shape, s.dtype) for s in shards],
        input_output_aliases={w: w for w in range(n_w)},
        scratch_shapes=[pltpu.SemaphoreType.DMA((n_w,)), pltpu.SemaphoreType.DMA((n_w,))],
    )(*shards)


def _all_reduce_small(vec):
    rows = vec.shape[0]

    def body(v_ref, o_ref, slots, send_sems, recv_sems):
        x, y, c = _place()
        me = 4 * x + 2 * y + c
        slots[me] = v_ref[...]
        sends = []
        for k in range(1, N_DEV):
            peer = (x ^ (k >> 2), y ^ ((k >> 1) & 1), c ^ (k & 1))
            sends.append(pltpu.make_async_remote_copy(
                src_ref=v_ref, dst_ref=slots.at[me], send_sem=send_sems.at[k - 1], recv_sem=recv_sems.at[k - 1],
                device_id=peer, device_id_type=MESH))
        for cp in sends:
            cp.start()
        for k in range(1, N_DEV):
            px, py, pc = x ^ (k >> 2), y ^ ((k >> 1) & 1), c ^ (k & 1)
            landed = slots.at[4 * px + 2 * py + pc]
            pltpu.make_async_remote_copy(
                src_ref=landed, dst_ref=landed, send_sem=send_sems.at[k - 1], recv_sem=recv_sems.at[k - 1],
                device_id=(px, py, pc), device_id_type=MESH).wait_recv()
        for cp in sends:
            cp.wait_send()
        total = slots[0]
        for d in range(1, N_DEV):
            total = total + slots[d]
        o_ref[...] = total

    vm = pl.BlockSpec(memory_space=pltpu.VMEM)
    return pl.pallas_call(
        body, name="all_reduce_small", in_specs=[vm], out_specs=vm, out_shape=jax.ShapeDtypeStruct(vec.shape, F32),
        scratch_shapes=[pltpu.VMEM((N_DEV, rows, LANES), F32), pltpu.SemaphoreType.DMA((N_DEV - 1,)),
                        pltpu.SemaphoreType.DMA((N_DEV - 1,))],
    )(vec)


def _row_tile(rows):
    for tr in (256, 128, 64, 32, 16):
        if rows % tr == 0:
            return tr
    return rows


def _pair_sum(name, place, grad, theirs):
    n, r, c = grad.shape
    half = r // 2
    tr = _row_tile(half)
    nb = half // tr

    def body(place_ref, g_ref, t_ref, o_ref):
        o_ref[...] = (g_ref[...] + t_ref[...]).astype(BF16)

    return pl.pallas_call(
        body, name=name, out_shape=jax.ShapeDtypeStruct((n, half, c), BF16),
        grid_spec=pltpu.PrefetchScalarGridSpec(
            num_scalar_prefetch=1, grid=(n, nb),
            in_specs=[pl.BlockSpec((1, tr, c), lambda j, i, pr: (j, pr[0] * nb + i, 0)),
                      pl.BlockSpec((1, tr, c), lambda j, i, pr: (j, i, 0))],
            out_specs=pl.BlockSpec((1, tr, c), lambda j, i, pr: (j, i, 0))),
        compiler_params=_params("parallel", "parallel"),
    )(place, grad, theirs)


def _sum_chips(name, place, pair_sums, landed):
    _, half, c = pair_sums.shape
    tr = _row_tile(half)
    nb = half // tr

    def body(place_ref, s_ref, q_ref, o_ref):
        total = s_ref[0].astype(F32)
        for p in range(3):
            total = total + q_ref[p].astype(F32)
        o_ref[...] = total

    return pl.pallas_call(
        body, name=name, out_shape=jax.ShapeDtypeStruct((2 * half, c), F32),
        grid_spec=pltpu.PrefetchScalarGridSpec(
            num_scalar_prefetch=1, grid=(nb,),
            in_specs=[pl.BlockSpec((1, tr, c), lambda i, pr: (pr[1], i, 0)),
                      pl.BlockSpec((3, tr, c), lambda i, pr: (0, i, 0))],
            out_specs=pl.BlockSpec((tr, c), lambda i, pr: (pr[0] * nb + i, 0))),
        compiler_params=_params("parallel"),
    )(place, pair_sums, landed)


BIG = ("w_in", "w_branch_a", "w_branch_b", "w_out", "w_ffn_gate", "w_ffn_up", "w_ffn_down", "w_ple_gate", "w_ple_proj")
COLUMN_SHARDED = ("w_in", "w_branch_a", "w_branch_b", "w_ffn_gate", "w_ffn_up", "w_ple_proj")
SMALL = ("norm_mix", "w_pool", "pool_scale", "norm_ffn", "norm_ple", "norm_final")


def _join_columns(w4):
    return jnp.concatenate([w4[j] for j in range(N_CHIPS)], axis=1)


def _split_columns(g):
    k, n = g.shape
    return g.reshape(k, N_CHIPS, n // N_CHIPS).transpose(1, 0, 2)


def _sds(shape, dtype):
    return jax.ShapeDtypeStruct(shape, dtype)


def _local_step(x, p, target, wf, small):
    t, d = x.shape
    w_in, w_gate, w_up, w_down = wf["w_in"], wf["w_ffn_gate"], wf["w_ffn_up"], wf["w_ffn_down"]
    w_a, w_b, w_pp = _join_columns(wf["w_branch_a"]), _join_columns(wf["w_branch_b"]), _join_columns(wf["w_ple_proj"])
    w_out = wf["w_out"].reshape(d, d)
    w_pg = wf["w_ple_gate"].reshape(d, d)
    w_pool_b = small["w_pool"].astype(BF16)
    dp = w_pool_b.shape[0] * w_pool_b.shape[1]
    dff = w_gate.shape[2]

    h1 = _norm_fwd("norm_mix", x, small["norm_mix"])
    u, q = _mm("proj_uq", [h1], [w_in[0]], "nn", [_sds((t, dp), F32), _sds((t, dp), BF16)],
               epilogue=lambda acc: (acc[:, :dp], acc[:, dp:]))
    kv, = _mm("proj_kv", [h1], [w_in[1]], "nn", [_sds((t, d), BF16)])
    ga, = _mm("proj_ga", [h1], [w_in[2]], "nn", [_sds((t, d), BF16)])
    gb, = _mm("proj_gb", [h1], [w_in[3]], "nn", [_sds((t, d), BF16)])
    pooled, ya = _pool_fwd(u, w_pool_b, small["pool_scale"])
    n_pairs = dp // LANES
    yb = _attn_fwd(q, 0, kv, 0, n_pairs, n_pairs)
    ta, tb, merged = _mm(
        "branches_merge", [ya, yb], [w_a, w_b], "nn", [_sds((t, d), BF16)] * 3, extras=[ga, gb], separate=True,
        epilogue=lambda tav, tbv, gav, gbv: (tav, tbv, _sigmoid(gav) * tav + _sigmoid(gbv) * tbv), tm=512)
    x1, = _mm("mix_out", [merged], [w_out], "nn", [_sds((t, d), F32)], extras=[x], epilogue=lambda acc, xv: (acc + xv,))
    h2 = _norm_fwd("norm_ffn", x1, small["norm_ffn"])
    gates, ups, acts = [], [], []
    for j in range(N_CHIPS):
        gj, uj, aj = _mm(f"ffn_gate_up{j}", [h2], [w_gate[j], w_up[j]], "nn", [_sds((t, dff), BF16)] * 3,
                         separate=True, epilogue=lambda gv, uv: (gv, uv, gv * _sigmoid(gv) * uv))
        gates.append(gj), ups.append(uj), acts.append(aj)
    x2, = _mm("ffn_down", acts, [w_down[j] for j in range(N_CHIPS)], "nn", [_sds((t, d), F32)], extras=[x1],
              epilogue=lambda acc, xv: (acc + xv,), tm=512)
    h3 = _norm_fwd("norm_ple", x2, small["norm_ple"])
    gp, pp, x3 = _mm(
        "ple", [h3, p], [w_pg, w_pp], "nn", [_sds((t, d), BF16), _sds((t, d), BF16), _sds((t, d), F32)], extras=[x2],
        separate=True, epilogue=lambda gv, pv, xv: (gv, pv, xv + _sigmoid(gv) * pv), tm=512)
    (dx3,), (d_norm_final, loss_row) = _split2(_final_loss(x3, target, small["norm_final"].reshape(1, d)), 1)

    def ple_bwd(dxv, gv, pv):
        s = _sigmoid(gv)
        return dxv * s, dxv * pv * s * (1.0 - s)

    d_pp, d_gp = _rows("ple_bwd", ple_bwd, [dx3, gp, pp], [_sds((t, d), BF16), _sds((t, d), BF16)])
    g_w_pp = _mm_tn("g_ple_proj", p, d_pp)
    g_w_pg = _mm_tn("g_ple_gate", h3, d_gp)
    dh3, = _mm("d_h3", [d_gp], [w_pg], "nt", [_sds((t, d), F32)])
    (dx2,), (d_norm_ple,) = _split2(_norm_bwd("norm_ple_bwd", dh3, x2, small["norm_ple"], dx3), 1)

    def ffn_bwd(acc, gv, uv):
        s = _sigmoid(gv)
        return acc * uv * (s * (1.0 + gv * (1.0 - s))), acc * (gv * s)

    d_gates, d_ups, g_w_gate, g_w_up, g_w_down = [], [], [], [], []
    for j in range(N_CHIPS):
        dgj, duj = _mm(f"d_act{j}", [dx2], [w_down[j]], "nt", [_sds((t, dff), BF16), _sds((t, dff), BF16)],
                       extras=[gates[j], ups[j]], epilogue=ffn_bwd)
        d_gates.append(dgj), d_ups.append(duj)
        g_w_down.append(_mm_tn(f"g_ffn_down{j}", acts[j], dx2))
        g_w_gate.append(_mm_tn(f"g_ffn_gate{j}", h2, dgj))
        g_w_up.append(_mm_tn(f"g_ffn_up{j}", h2, duj))
    dh2, = _mm("d_h2", d_gates + d_ups, [w_gate[j] for j in range(N_CHIPS)] + [w_up[j] for j in range(N_CHIPS)], "nt",
               [_sds((t, d), F32)], tm=512)
    (dx1,), (d_norm_ffn,) = _split2(_norm_bwd("norm_ffn_bwd", dh2, x1, small["norm_ffn"], dx2), 1)

    def merge_bwd(acc, tav, tbv, gav, gbv):
        sa, sb = _sigmoid(gav), _sigmoid(gbv)
        return acc * sa, acc * sb, acc * tav * sa * (1.0 - sa), acc * tbv * sb * (1.0 - sb)

    d_ta, d_tb, d_ga, d_gb = _mm("d_merged", [dx1], [w_out], "nt", [_sds((t, d), BF16)] * 4,
                                 extras=[ta, tb, ga, gb], epilogue=merge_bwd, tm=512)
    g_w_out = _mm_tn("g_w_out", merged, dx1)
    g_w_a = _mm_tn("g_branch_a", ya, d_ta)
    g_w_b = _mm_tn("g_branch_b", yb, d_tb)
    d_ya, = _mm("d_ya", [d_ta], [w_a], "nt", [_sds((t, dp), F32)])
    d_yb, = _mm("d_yb", [d_tb], [w_b], "nt", [_sds((t, dp), BF16)])
    d_u, g_w_pool, d_pool_scale = _pool_bwd(d_ya, pooled, w_pool_b, small["pool_scale"])
    d_q, d_k, d_v = _attn_bwd(q, 0, kv, 0, n_pairs, d_yb, n_pairs)
    d_proj = [jnp.concatenate([d_u, d_q], axis=1), jnp.concatenate([d_k, d_v], axis=1), d_ga, d_gb]
    g_w_in = [_mm_tn(f"g_w_in{j}", h1, d_proj[j]) for j in range(N_CHIPS)]
    dh1, = _mm("d_h1", d_proj, [w_in[j] for j in range(N_CHIPS)], "nt", [_sds((t, d), F32)], tm=512)
    (grad_x,), (d_norm_mix,) = _split2(_norm_bwd("norm_mix_bwd", dh1, x, small["norm_mix"], dx1), 1)

    big = {
        "w_in": jnp.stack(g_w_in), "w_branch_a": _split_columns(g_w_a), "w_branch_b": _split_columns(g_w_b),
        "w_out": g_w_out.reshape(wf["w_out"].shape), "w_ffn_gate": jnp.stack(g_w_gate), "w_ffn_up": jnp.stack(g_w_up),
        "w_ffn_down": jnp.stack(g_w_down), "w_ple_gate": g_w_pg.reshape(wf["w_ple_gate"].shape),
        "w_ple_proj": _split_columns(g_w_pp),
    }
    small_g = {"norm_mix": d_norm_mix, "w_pool": g_w_pool, "pool_scale": d_pool_scale, "norm_ffn": d_norm_ffn,
               "norm_ple": d_norm_ple, "norm_final": d_norm_final}
    return grad_x, big, small_g, loss_row


def _split2(res, n):
    return res[:n], res[n:]


def _pack_small(small_g, loss_row):
    parts, layout = [], []
    for name in SMALL + ("loss",):
        v = (loss_row if name == "loss" else small_g[name]).reshape(-1, LANES)
        pad = (-v.shape[0]) % 8
        if pad:
            v = jnp.concatenate([v, jnp.zeros((pad, LANES), F32)], axis=0)
        layout.append((name, sum(q.shape[0] for q in parts), v.shape[0]))
        parts.append(v)
    return jnp.concatenate(parts, axis=0), layout


def kernel(x, p, norm_mix, w_in, w_pool, pool_scale, w_branch_a, w_branch_b, w_out, norm_ffn, w_ffn_gate, w_ffn_up, w_ffn_down, norm_ple, w_ple_gate, w_ple_proj, norm_final, loss_target, m_norm_mix, m_w_in, m_w_pool, m_pool_scale, m_w_branch_a, m_w_branch_b, m_w_out, m_norm_ffn, m_w_ffn_gate, m_w_ffn_up, m_w_ffn_down, m_norm_ple, m_w_ple_gate, m_w_ple_proj, m_norm_final, v_norm_mix, v_w_in, v_w_pool, v_pool_scale, v_w_branch_a, v_w_branch_b, v_w_out, v_norm_ffn, v_w_ffn_gate, v_w_ffn_up, v_w_ffn_down, v_norm_ple, v_w_ple_gate, v_w_ple_proj, v_norm_final):
    given = dict(locals())
    names = BIG + SMALL
    order = ("norm_mix", "w_in", "w_pool", "pool_scale", "w_branch_a", "w_branch_b", "w_out", "norm_ffn", "w_ffn_gate",
             "w_ffn_up", "w_ffn_down", "norm_ple", "w_ple_gate", "w_ple_proj", "norm_final")
    t, d = x.shape[1], x.shape[2]
    shard = {n: given[n][0] for n in BIG}
    small = {"norm_mix": norm_mix, "w_pool": w_pool[0], "pool_scale": pool_scale, "norm_ffn": norm_ffn,
             "norm_ple": norm_ple, "norm_final": norm_final}

    gathered = _all_gather_weights([shard[n].astype(BF16) for n in BIG])
    wf = dict(zip(BIG, gathered))
    grad_x, big_g, small_g, loss_row = _local_step(
        x.reshape(t, d), p.reshape(t, p.shape[-1]), loss_target.reshape(t, d), wf, small)

    place = jnp.stack([lax.axis_index("c"), 2 * lax.axis_index("x") + lax.axis_index("y")]).astype(jnp.int32)
    theirs = _pair_exchange([big_g[n] for n in BIG])
    pair_sums = [_pair_sum(f"pair_sum_{n}", place, big_g[n], t) for n, t in zip(BIG, theirs)]
    landed = _chip_exchange(pair_sums)
    halves = [_sum_chips(f"chip_sum_{n}", place, s, q) for n, s, q in zip(BIG, pair_sums, landed)]
    grads = dict(zip(BIG, _pair_share(halves)))

    packed, layout = _pack_small(small_g, loss_row)
    reduced = _all_reduce_small(packed)
    for name, start, rows in layout:
        if name == "loss":
            loss = jnp.sum(reduced[start:start + rows])
        else:
            n_el = small[name].size
            grads[name] = reduced[start:start + rows].reshape(-1)[:n_el]

    deltas, new_m, new_v = {}, {}, {}
    for n in order:
        w = shard[n] if n in BIG else small[n]
        shape2 = w.shape if w.ndim == 2 else ((1, w.shape[0]) if w.ndim == 1 else (w.shape[0] * w.shape[1], w.shape[2]))
        g2 = grads[n].reshape(shape2)
        dl, mn, vn = _adamw(f"adamw_{n}", w.reshape(shape2), g2, given["m_" + n].reshape(shape2),
                            given["v_" + n].reshape(shape2))
        full = given[n].shape
        grads[n], deltas[n], new_m[n], new_v[n] = g2.reshape(full), dl.reshape(full), mn.reshape(full), vn.reshape(full)

    return (loss, grad_x.reshape(x.shape), *[grads[n] for n in order], *[deltas[n] for n in order],
            *[new_m[n] for n in order], *[new_v[n] for n in order])
```

```python
import functools
import math

import jax
import jax.numpy as jnp
from jax import lax
from jax.experimental import pallas as pl
from jax.experimental.pallas import tpu as pltpu

F32 = jnp.float32
BF16 = jnp.bfloat16
MESH = pl.DeviceIdType.MESH

RMS_EPS = 1e-6
POOL_WINDOWS = (2, 4, 8, 16)
POOL_HALO = 16
HEAD_DIM = 64
LANES = 128
ATT_BLOCK = 256
ATT_CHUNK = 256
ATT_SLAB = 256
ATT_SCALE = 1.0 / math.sqrt(HEAD_DIM)
ATT_EXIT_BELOW = -104.0
ADAM_LR, ADAM_B1, ADAM_B2, ADAM_EPS, ADAM_WD, ADAM_STEP = 0.001, 0.9, 0.999, 1e-08, 0.01, 10
V7X_VMEM_LIMIT_BYTES = 56 * 1024 * 1024
N_CHIPS = 4
N_DEV = 8


def _params(*semantics):
    return pltpu.CompilerParams(dimension_semantics=semantics, vmem_limit_bytes=V7X_VMEM_LIMIT_BYTES)


def _sigmoid(z):
    return 1.0 / (1.0 + jnp.exp(-z))


def _tiled_spec(shape, tm, tn, n_total):
    rows, width = shape
    if rows == 1:
        if width == n_total:
            return pl.BlockSpec((1, tn), lambda i, j: (0, j))
        return pl.BlockSpec((1, width), lambda i, j: (0, 0))
    if width == n_total:
        return pl.BlockSpec((tm, tn), lambda i, j: (i, j))
    assert tn == n_total, "an operand narrower than the output needs whole output rows per tile"
    return pl.BlockSpec((tm, width), lambda i, j: (i, 0))


def _mm(name, a_list, b_list, mode, out_shapes, epilogue=None, extras=(), tm=1024, tn=None, separate=False):
    m_total = a_list[0].shape[0]
    n_total = b_list[0].shape[1] if mode == "nn" else b_list[0].shape[0]
    tn = n_total if tn is None else tn
    tm = min(tm, m_total)
    assert m_total % tm == 0 and n_total % tn == 0
    n_a, n_b, n_extra = len(a_list), len(b_list), len(extras)
    assert n_a in (1, n_b)
    dims = (((1,), (0,)), ((), ())) if mode == "nn" else (((1,), (1,)), ((), ()))

    def body(*refs):
        a_refs, b_refs = refs[:n_a], refs[n_a:n_a + n_b]
        e_refs = refs[n_a + n_b:n_a + n_b + n_extra]
        o_refs = refs[n_a + n_b + n_extra:]
        lefts = [a_ref[...] for a_ref in a_refs]
        lefts = [a if a.dtype == BF16 else a.astype(BF16) for a in lefts]
        products = [lax.dot_general(lefts[s % n_a], b_refs[s][...], dims, preferred_element_type=F32)
                    for s in range(n_b)]
        if not separate:
            products = [functools.reduce(lambda p, r: p + r, products)]
        extra_tiles = [e[...].astype(F32) for e in e_refs]
        outs = products if epilogue is None else epilogue(*products, *extra_tiles)
        for o_ref, o in zip(o_refs, outs):
            o_ref[...] = o.astype(o_ref.dtype)

    in_specs = [pl.BlockSpec((tm, a.shape[1]), lambda i, j: (i, 0)) for a in a_list]
    if mode == "nn":
        in_specs += [pl.BlockSpec((b.shape[0], tn), lambda i, j: (0, j)) for b in b_list]
    else:
        in_specs += [pl.BlockSpec((tn, b.shape[1]), lambda i, j: (j, 0)) for b in b_list]
    in_specs += [_tiled_spec(e.shape, tm, tn, n_total) for e in extras]
    out_specs = [_tiled_spec(o.shape, tm, tn, n_total) for o in out_shapes]
    res = pl.pallas_call(
        body, name=name, grid=(m_total // tm, n_total // tn), in_specs=in_specs, out_specs=out_specs,
        out_shape=list(out_shapes), compiler_params=_params("parallel", "parallel"),
    )(*a_list, *b_list, *extras)
    return res


def _mm_tn(name, a, b, tmm=1024):
    m_total, k = a.shape
    n = b.shape[1]
    tmm = min(tmm, m_total)
    assert m_total % tmm == 0

    def body(a_ref, b_ref, o_ref):
        @pl.when(pl.program_id(0) == 0)
        def _():
            o_ref[...] = jnp.zeros_like(o_ref)

        av, bv = a_ref[...], b_ref[...]
        if av.dtype != BF16:
            av = av.astype(BF16)
        if bv.dtype != BF16:
            bv = bv.astype(BF16)
        o_ref[...] += lax.dot_general(av, bv, (((0,), (0,)), ((), ())), preferred_element_type=F32)

    return pl.pallas_call(
        body, name=name, grid=(m_total // tmm,),
        in_specs=[pl.BlockSpec((tmm, k), lambda m: (m, 0)), pl.BlockSpec((tmm, n), lambda m: (m, 0))],
        out_specs=pl.BlockSpec((k, n), lambda m: (0, 0)),
        out_shape=jax.ShapeDtypeStruct((k, n), F32), compiler_params=_params("arbitrary"),
    )(a, b)


def _rows(name, fn, ins, tile_outs, sum_outs=(), tr=512):
    t_total = max(a.shape[0] for a in ins)
    tr = min(tr, t_total)
    assert t_total % tr == 0
    n_in, n_tile = len(ins), len(tile_outs)

    def body(*refs):
        outs = fn(*[r[...].astype(F32) for r in refs[:n_in]])
        for o_ref, o in zip(refs[n_in:n_in + n_tile], outs[:n_tile]):
            o_ref[...] = o.astype(o_ref.dtype)
        if sum_outs:
            @pl.when(pl.program_id(0) == 0)
            def _():
                for s_ref in refs[n_in + n_tile:]:
                    s_ref[...] = jnp.zeros_like(s_ref)

            for s_ref, s in zip(refs[n_in + n_tile:], outs[n_tile:]):
                s_ref[...] += s

    def spec(shape):
        if shape[0] == 1:
            return pl.BlockSpec(shape, lambda i: (0, 0))
        return pl.BlockSpec((tr, shape[1]), lambda i: (i, 0))

    return pl.pallas_call(
        body, name=name, grid=(t_total // tr,), in_specs=[spec(a.shape) for a in ins],
        out_specs=[spec(o.shape) for o in tile_outs] + [spec(s.shape) for s in sum_outs],
        out_shape=list(tile_outs) + list(sum_outs),
        compiler_params=_params("arbitrary" if sum_outs else "parallel"),
    )(*ins)


def _norm_fwd(name, x, gain):
    def fn(xv, g):
        inv = lax.rsqrt(jnp.mean(xv * xv, axis=-1, keepdims=True) + RMS_EPS)
        return (xv * inv * g,)

    return _rows(name, fn, [x, gain], [jax.ShapeDtypeStruct(x.shape, BF16)])[0]


def _norm_bwd(name, dh, x, gain, dres):
    def fn(dhv, xv, g, dr):
        inv = lax.rsqrt(jnp.mean(xv * xv, axis=-1, keepdims=True) + RMS_EPS)
        xn = xv * inv
        dxn = dhv * g
        dx = inv * (dxn - xn * jnp.mean(dxn * xn, axis=-1, keepdims=True)) + dr
        return dx, jnp.sum(dhv * xn, axis=0, keepdims=True)

    d = x.shape[1]
    return _rows(name, fn, [dh, x, gain, dres], [jax.ShapeDtypeStruct(x.shape, F32)],
                 [jax.ShapeDtypeStruct((1, d), F32)])


def _final_loss(x3, target, gain):
    d = x3.shape[1]

    def fn(xv, tv, g):
        inv = lax.rsqrt(jnp.mean(xv * xv, axis=-1, keepdims=True) + RMS_EPS)
        xn = xv * inv
        err = xn * g - tv
        dy = err * (1.0 / d)
        dxn = dy * g
        dx = inv * (dxn - xn * jnp.mean(dxn * xn, axis=-1, keepdims=True))
        return dx, jnp.sum(dy * xn, axis=0, keepdims=True), (0.5 / d) * jnp.sum(err * err, axis=0, keepdims=True)

    return _rows("final_loss", fn, [x3, target, gain], [jax.ShapeDtypeStruct(x3.shape, F32)],
                 [jax.ShapeDtypeStruct((1, d), F32), jax.ShapeDtypeStruct((1, d), F32)])


def _window_counts(t_pos, w):
    return jnp.minimum(t_pos + 1, w).astype(F32)


def _pool_fwd(u, w_pool, scale, tr=512):
    t_total, width = u.shape
    tr = min(tr, t_total)
    n_groups = len(POOL_WINDOWS)
    gdim = width // n_groups
    ext = tr + POOL_HALO

    def body(u_ref, halo_ref, w_ref, s_ref, pooled_ref, ya_ref):
        i = pl.program_id(0)
        halo = jnp.where(i == 0, 0.0, halo_ref[...])
        t_pos = i * tr + lax.broadcasted_iota(jnp.int32, (tr, 1), 0)
        for g, w in enumerate(POOL_WINDOWS):
            cols = slice(g * gdim, (g + 1) * gdim)
            main = u_ref[:, cols]
            win = jnp.concatenate([halo[:, cols], main], axis=0)
            span = 1
            while span < w:
                win = win + pltpu.roll(win, span, 0)
                span *= 2
            pooled = win[POOL_HALO:, :] * (1.0 / _window_counts(t_pos, w)) - main
            pooled_b = pooled.astype(BF16)
            pooled_ref[:, cols] = pooled_b
            mixed = jnp.dot(pooled_b, w_ref[g], preferred_element_type=F32)
            ya_ref[:, cols] = (mixed * s_ref[:, cols]).astype(BF16)

    hb = tr // POOL_HALO
    return pl.pallas_call(
        body, name="pool_fwd", grid=(t_total // tr,),
        in_specs=[pl.BlockSpec((tr, width), lambda i: (i, 0)),
                  pl.BlockSpec((POOL_HALO, width), lambda i: (jnp.maximum(i * hb - 1, 0), 0)),
                  pl.BlockSpec((n_groups, gdim, gdim), lambda i: (0, 0, 0)),
                  pl.BlockSpec((1, width), lambda i: (0, 0))],
        out_specs=[pl.BlockSpec((tr, width), lambda i: (i, 0)), pl.BlockSpec((tr, width), lambda i: (i, 0))],
        out_shape=[jax.ShapeDtypeStruct(u.shape, BF16), jax.ShapeDtypeStruct(u.shape, BF16)],
        compiler_params=_params("parallel"),
    )(u, u, w_pool, scale)


def _pool_bwd(dya, pooled, w_pool, scale, tr=512):
    t_total, width = dya.shape
    tr = min(tr, t_total)
    n_groups = len(POOL_WINDOWS)
    gdim = width // n_groups
    ext = tr + POOL_HALO
    n_tiles = t_total // tr

    def body(d_ref, halo_ref, p_ref, w_ref, s_ref, du_ref, dw_ref, ds_ref):
        i = pl.program_id(0)

        @pl.when(i == 0)
        def _():
            dw_ref[...] = jnp.zeros_like(dw_ref)
            ds_ref[...] = jnp.zeros_like(ds_ref)

        halo = jnp.where(i == n_tiles - 1, 0.0, halo_ref[...])
        t_pos = i * tr + lax.broadcasted_iota(jnp.int32, (ext, 1), 0)
        for g, w in enumerate(POOL_WINDOWS):
            cols = slice(g * gdim, (g + 1) * gdim)
            sc = s_ref[:, cols]
            d_main = d_ref[:, cols]
            pooled_b = p_ref[:, cols]
            mixed = jnp.dot(pooled_b, w_ref[g], preferred_element_type=F32)
            ds_ref[:, cols] += jnp.sum(d_main * mixed, axis=0, keepdims=True)
            dmix = (jnp.concatenate([d_main, halo[:, cols]], axis=0) * sc).astype(BF16)
            dw_ref[g] += lax.dot_general(pooled_b, dmix[:tr, :], (((0,), (0,)), ((), ())),
                                         preferred_element_type=F32)
            dpool = lax.dot_general(dmix, w_ref[g], (((1,), (1,)), ((), ())), preferred_element_type=F32)
            win = dpool * (1.0 / _window_counts(t_pos, w))
            span = 1
            while span < w:
                win = win + pltpu.roll(win, ext - span, 0)
                span *= 2
            du_ref[:, cols] = (win[:tr, :] - dpool[:tr, :]).astype(BF16)

    hb = tr // POOL_HALO
    last_halo = t_total // POOL_HALO - 1
    return pl.pallas_call(
        body, name="pool_bwd", grid=(n_tiles,),
        in_specs=[pl.BlockSpec((tr, width), lambda i: (i, 0)),
                  pl.BlockSpec((POOL_HALO, width), lambda i: (jnp.minimum((i + 1) * hb, last_halo), 0)),
                  pl.BlockSpec((tr, width), lambda i: (i, 0)),
                  pl.BlockSpec((n_groups, gdim, gdim), lambda i: (0, 0, 0)),
                  pl.BlockSpec((1, width), lambda i: (0, 0))],
        out_specs=[pl.BlockSpec((tr, width), lambda i: (i, 0)),
                   pl.BlockSpec((n_groups, gdim, gdim), lambda i: (0, 0, 0)),
                   pl.BlockSpec((1, width), lambda i: (0, 0))],
        out_shape=[jax.ShapeDtypeStruct(dya.shape, BF16), jax.ShapeDtypeStruct((n_groups, gdim, gdim), F32),
                   jax.ShapeDtypeStruct((1, width), F32)],
        compiler_params=_params("arbitrary"),
    )(dya, dya, pooled, w_pool, scale)


def _head_masks():
    lane = lax.broadcasted_iota(jnp.int32, (1, LANES), 1)
    return lane < HEAD_DIM


def _stack_heads(tile, first):
    zero = jnp.zeros_like(tile)
    return jnp.concatenate([jnp.where(first, tile, zero), jnp.where(first, zero, tile)], axis=0)


def _split_bf16(v):
    hi = v.astype(BF16)
    lo = (v - hi.astype(F32)).astype(BF16)
    return hi, lo


def _slab_scores(q, kd, t_pos, k_start):
    z = lax.dot_general(q, kd, (((1,), (1,)), ((), ())), preferred_element_type=F32)
    col = lax.broadcasted_iota(jnp.int32, (1, 2 * ATT_SLAB), 1)
    mask = k_start + (col & (ATT_SLAB - 1)) < t_pos
    e = jnp.exp(-jnp.abs(z))
    log_fail = jnp.where(mask, -(jnp.maximum(z, 0.0) + jnp.log(1.0 + e)), 0.0)
    return z, mask, log_fail, e


def _tri(upper):
    r = lax.broadcasted_iota(jnp.int32, (ATT_CHUNK, ATT_CHUNK), 0)
    c = lax.broadcasted_iota(jnp.int32, (ATT_CHUNK, ATT_CHUNK), 1)
    return jnp.where(r > c if upper else r < c, 1.0, 0.0).astype(BF16)


def _scan_chunk(v, tri):
    hi, lo = _split_bf16(v)
    return (jnp.dot(hi, tri, preferred_element_type=F32) + jnp.dot(lo, tri, preferred_element_type=F32))


def _lane_bcast(col):
    return jnp.broadcast_to(col, (col.shape[0], LANES))


def _scan_slab(v, tri, carries, from_right):
    n_chunks = ATT_SLAB // ATT_CHUNK
    edge = 0 if from_right else ATT_CHUNK - 1
    parts, new_carries = [None] * (2 * n_chunks), []
    for head in range(2):
        run = carries[head]
        for c in (reversed(range(n_chunks)) if from_right else range(n_chunks)):
            lo_col = head * ATT_SLAB + c * ATT_CHUNK
            vc = v[:, lo_col:lo_col + ATT_CHUNK]
            sc = _scan_chunk(vc, tri)
            parts[head * n_chunks + c] = sc + jnp.concatenate([run] * (ATT_CHUNK // LANES), axis=1)
            run = run + _lane_bcast(sc[:, edge:edge + 1] + vc[:, edge:edge + 1])
        new_carries.append(run)
    return jnp.concatenate(parts, axis=1), new_carries


def _fold_heads(stacked, first):
    s = stacked.shape[0] // 2
    return jnp.where(first, stacked[:s], stacked[s:])


def _attn_fwd(q_src, q_col, kv_src, k_col, v_col, n_pairs=4):
    t_total = q_src.shape[0]
    blk = ATT_BLOCK
    n_blocks = t_total // blk
    assert t_total % ATT_SLAB == 0

    def body(q_ref, k_ref, v_ref, o_ref):
        i = pl.program_id(1)
        first = _head_masks()
        q = q_ref[...] * ATT_SCALE
        t_pos = i * blk + lax.broadcasted_iota(jnp.int32, (blk, 1), 0)
        suffix_tri = _tri(upper=True)

        def more(state):
            slab, reach = state[0], state[1]
            return jnp.logical_and(slab >= 0, reach > ATT_EXIT_BELOW)

        def step(state):
            slab, _, acc, right_a, right_b = state
            k_start = pl.multiple_of(slab * ATT_SLAB, ATT_SLAB)
            kd = _stack_heads(k_ref[pl.ds(k_start, ATT_SLAB), :], first)
            vd = _stack_heads(v_ref[pl.ds(k_start, ATT_SLAB), :], first)
            z, mask, log_fail, _ = _slab_scores(q, kd, t_pos, k_start)
            suffix, (right_a, right_b) = _scan_slab(log_fail, suffix_tri, (right_a, right_b), from_right=True)
            a = jnp.exp(jnp.where(mask, z + log_fail + suffix, -1e30)).astype(BF16)
            acc = acc + jnp.dot(a, vd, preferred_element_type=F32)
            return slab - 1, jnp.max(jnp.maximum(right_a, right_b)), acc, right_a, right_b

        zero = jnp.zeros((blk, LANES), F32)
        state = lax.while_loop(more, step, ((i * blk) // ATT_SLAB, jnp.float32(0.0), zero, zero, zero))
        o_ref[...] = state[2].astype(BF16)

    return pl.pallas_call(
        body, name="attn_fwd", grid=(n_pairs, n_blocks),
        in_specs=[pl.BlockSpec((blk, LANES), lambda h, i: (i, q_col + h)),
                  pl.BlockSpec((t_total, LANES), lambda h, i: (0, k_col + h)),
                  pl.BlockSpec((t_total, LANES), lambda h, i: (0, v_col + h))],
        out_specs=pl.BlockSpec((blk, LANES), lambda h, i: (i, h)),
        out_shape=jax.ShapeDtypeStruct((t_total, n_pairs * LANES), BF16),
        compiler_params=_params("parallel", "parallel"),
    )(q_src, kv_src, kv_src)


def _attn_bwd(q_src, q_col, kv_src, k_col, v_col, dy, n_pairs=4):
    t_total = q_src.shape[0]
    blk = ATT_BLOCK
    n_blocks = t_total // blk
    n_slabs = t_total // ATT_SLAB
    assert t_total % ATT_SLAB == 0

    def body(q_ref, dy_ref, k_ref, v_ref, dq_ref, dk_ref, dv_ref, g_s, dk_acc, dv_acc):
        i = pl.program_id(1)

        @pl.when(i == 0)
        def _():
            dk_acc[...] = jnp.zeros_like(dk_acc)
            dv_acc[...] = jnp.zeros_like(dv_acc)

        first = _head_masks()
        q = q_ref[...] * ATT_SCALE
        dy = dy_ref[...]
        t_pos = i * blk + lax.broadcasted_iota(jnp.int32, (blk, 1), 0)
        suffix_tri = _tri(upper=True)
        prefix_tri = _tri(upper=False)
        diag = (i * blk) // ATT_SLAB

        def more(state):
            slab, reach = state[0], state[1]
            return jnp.logical_and(slab >= 0, reach > ATT_EXIT_BELOW)

        def sweep1(state):
            slab, _, right_a, right_b = state
            k_start = pl.multiple_of(slab * ATT_SLAB, ATT_SLAB)
            kd = _stack_heads(k_ref[pl.ds(k_start, ATT_SLAB), :], first)
            vd = _stack_heads(v_ref[pl.ds(k_start, ATT_SLAB), :], first)
            z, mask, log_fail, _ = _slab_scores(q, kd, t_pos, k_start)
            suffix, (right_a, right_b) = _scan_slab(log_fail, suffix_tri, (right_a, right_b), from_right=True)
            a = jnp.exp(jnp.where(mask, z + log_fail + suffix, -1e30))
            da = lax.dot_general(dy, vd, (((1,), (1,)), ((), ())), preferred_element_type=F32)
            g_s[slab] = da * a
            dv_acc[pl.ds(k_start, ATT_SLAB), :] += _fold_heads(lax.dot_general(
                a.astype(BF16), dy, (((0,), (0,)), ((), ())), preferred_element_type=F32), first)
            return slab - 1, jnp.max(jnp.maximum(right_a, right_b)), right_a, right_b

        zero = jnp.zeros((blk, LANES), F32)
        end = lax.while_loop(more, sweep1, (diag, jnp.float32(0.0), zero, zero))[0]

        def sweep2(slab, carry):
            dq, left_a, left_b = carry
            k_start = pl.multiple_of(slab * ATT_SLAB, ATT_SLAB)
            kd = _stack_heads(k_ref[pl.ds(k_start, ATT_SLAB), :], first)
            g = g_s[slab]
            sig = _sigmoid(lax.dot_general(q, kd, (((1,), (1,)), ((), ())), preferred_element_type=F32))
            prefix, (left_a, left_b) = _scan_slab(g, prefix_tri, (left_a, left_b), from_right=False)
            col = lax.broadcasted_iota(jnp.int32, (1, 2 * ATT_SLAB), 1)
            mask = k_start + (col & (ATT_SLAB - 1)) < t_pos
            dz = jnp.where(mask, g * (1.0 - sig) - sig * prefix, 0.0).astype(BF16)
            dq = dq + jnp.dot(dz, kd, preferred_element_type=F32)
            dk_acc[pl.ds(k_start, ATT_SLAB), :] += _fold_heads(lax.dot_general(
                dz, q, (((0,), (0,)), ((), ())), preferred_element_type=F32), first)
            return dq, left_a, left_b

        dq = lax.fori_loop(end + 1, diag + 1, sweep2, (zero, zero, zero))[0]
        dq_ref[...] = (dq * ATT_SCALE).astype(BF16)

        @pl.when(i == n_blocks - 1)
        def _():
            dk_ref[...] = dk_acc[...].astype(BF16)
            dv_ref[...] = dv_acc[...].astype(BF16)

    out = jax.ShapeDtypeStruct((t_total, n_pairs * LANES), BF16)
    return pl.pallas_call(
        body, name="attn_bwd", grid=(n_pairs, n_blocks),
        in_specs=[pl.BlockSpec((blk, LANES), lambda h, i: (i, q_col + h)),
                  pl.BlockSpec((blk, LANES), lambda h, i: (i, h)),
                  pl.BlockSpec((t_total, LANES), lambda h, i: (0, k_col + h)),
                  pl.BlockSpec((t_total, LANES), lambda h, i: (0, v_col + h))],
        out_specs=[pl.BlockSpec((blk, LANES), lambda h, i: (i, h)),
                   pl.BlockSpec((t_total, LANES), lambda h, i: (0, h)),
                   pl.BlockSpec((t_total, LANES), lambda h, i: (0, h))],
        out_shape=[out, out, out],
        scratch_shapes=[pltpu.VMEM((n_slabs, blk, 2 * ATT_SLAB), F32),
                        pltpu.VMEM((t_total, LANES), F32), pltpu.VMEM((t_total, LANES), F32)],
        compiler_params=_params("arbitrary", "arbitrary"),
    )(q_src, dy, kv_src, kv_src)


def _adamw(name, w, g, m, v):
    def fn(wv, gv, mv, vv):
        mn = ADAM_B1 * mv + (1.0 - ADAM_B1) * gv
        vn = ADAM_B2 * vv + (1.0 - ADAM_B2) * (gv * gv)
        m_hat = mn / (1.0 - ADAM_B1 ** ADAM_STEP)
        v_hat = vn / (1.0 - ADAM_B2 ** ADAM_STEP)
        return -ADAM_LR * (m_hat / (jnp.sqrt(v_hat) + ADAM_EPS) + ADAM_WD * wv), mn, vn

    rows = w.shape[0]
    tr = _row_tile(rows)
    shp = jax.ShapeDtypeStruct(w.shape, F32)
    if rows == 1:
        def body(w_ref, g_ref, m_ref, v_ref, d_ref, mo_ref, vo_ref):
            d, mn, vn = fn(w_ref[...], g_ref[...], m_ref[...], v_ref[...])
            d_ref[...], mo_ref[...], vo_ref[...] = d, mn, vn

        return pl.pallas_call(body, name=name, out_shape=[shp, shp, shp])(w, g, m, v)
    return _rows(name, fn, [w, g, m, v], [shp, shp, shp], tr=tr)


def _place():
    return lax.axis_index("x"), lax.axis_index("y"), lax.axis_index("c")


def _other_chips(x, y):
    return [(1 - x, y), (x, 1 - y), (1 - x, 1 - y)]


ANY = pl.BlockSpec(memory_space=pl.ANY)


def _all_gather_weights(shards):
    n_w = len(shards)

    def body(*refs):
        ins, outs = refs[:n_w], refs[n_w:2 * n_w]
        send_sems, recv_sems, pass_send, pass_recv, own_send, own_recv = refs[2 * n_w:]
        x, y, c = _place()
        my_chip = 2 * x + y
        chips = _other_chips(x, y)
        local = [pltpu.make_async_remote_copy(
            src_ref=ins[w], dst_ref=outs[w].at[my_chip], send_sem=own_send.at[w], recv_sem=own_recv.at[w],
            device_id=(x, y, 1 - c), device_id_type=MESH) for w in range(n_w)]
        for cp in local:
            cp.start()

        def half(w, core):
            h = shards[w].shape[0] // 2
            return pl.ds(core * h, h)

        sends = []
        for p, (ox, oy) in enumerate(chips):
            for w in range(n_w):
                sends.append(pltpu.make_async_remote_copy(
                    src_ref=ins[w].at[half(w, c)], dst_ref=outs[w].at[my_chip, half(w, c)],
                    send_sem=send_sems.at[p, w], recv_sem=recv_sems.at[p, w],
                    device_id=(ox, oy, c), device_id_type=MESH))
        for cp in sends:
            cp.start()
        passes = []
        for p, (ox, oy) in enumerate(chips):
            chip = 2 * ox + oy
            for w in range(n_w):
                landed = outs[w].at[chip, half(w, c)]
                pltpu.make_async_remote_copy(
                    src_ref=landed, dst_ref=landed, send_sem=send_sems.at[p, w], recv_sem=recv_sems.at[p, w],
                    device_id=(ox, oy, c), device_id_type=MESH).wait_recv()
                cp = pltpu.make_async_remote_copy(
                    src_ref=landed, dst_ref=landed, send_sem=pass_send.at[p, w], recv_sem=pass_recv.at[p, w],
                    device_id=(x, y, 1 - c), device_id_type=MESH)
                cp.start()
                passes.append(cp)
        for p, (ox, oy) in enumerate(chips):
            chip = 2 * ox + oy
            for w in range(n_w):
                theirs = outs[w].at[chip, half(w, 1 - c)]
                pltpu.make_async_remote_copy(
                    src_ref=theirs, dst_ref=theirs, send_sem=pass_send.at[p, w], recv_sem=pass_recv.at[p, w],
                    device_id=(x, y, 1 - c), device_id_type=MESH).wait_recv()
        for cp in sends + passes:
            cp.wait_send()
        for cp in local:
            cp.wait()

    return pl.pallas_call(
        body, name="all_gather_weights", in_specs=[ANY] * n_w, out_specs=[ANY] * n_w,
        out_shape=[jax.ShapeDtypeStruct((N_CHIPS,) + s.shape, s.dtype) for s in shards],
        scratch_shapes=[pltpu.SemaphoreType.DMA((3, n_w)), pltpu.SemaphoreType.DMA((3, n_w)),
                        pltpu.SemaphoreType.DMA((3, n_w)), pltpu.SemaphoreType.DMA((3, n_w)),
                        pltpu.SemaphoreType.DMA((n_w,)), pltpu.SemaphoreType.DMA((n_w,))],
    )(*shards)


def _pair_exchange(grads):
    n_w = len(grads)

    def halves(w):
        return grads[w].shape[1] // 2

    def body(*refs):
        ins, theirs = refs[:n_w], refs[n_w:2 * n_w]
        send_sems, recv_sems = refs[2 * n_w:]
        x, y, c = _place()
        sends = [pltpu.make_async_remote_copy(
            src_ref=ins[w].at[:, pl.ds((1 - c) * halves(w), halves(w)), :], dst_ref=theirs[w],
            send_sem=send_sems.at[w], recv_sem=recv_sems.at[w], device_id=(x, y, 1 - c), device_id_type=MESH)
            for w in range(n_w)]
        for cp in sends:
            cp.start()
        for cp in sends:
            cp.wait()

    return pl.pallas_call(
        body, name="pair_exchange", in_specs=[ANY] * n_w, out_specs=[ANY] * n_w,
        out_shape=[jax.ShapeDtypeStruct((N_CHIPS, halves(w), grads[w].shape[2]), F32) for w in range(n_w)],
        scratch_shapes=[pltpu.SemaphoreType.DMA((n_w,)), pltpu.SemaphoreType.DMA((n_w,))],
    )(*grads)


def _chip_exchange(pair_sums):
    n_w = len(pair_sums)

    def body(*refs):
        ins, outs = refs[:n_w], refs[n_w:2 * n_w]
        send_sems, recv_sems = refs[2 * n_w:]
        x, y, c = _place()
        sends = []
        for p, (ox, oy) in enumerate(_other_chips(x, y)):
            for w in range(n_w):
                sends.append(pltpu.make_async_remote_copy(
                    src_ref=ins[w].at[2 * ox + oy], dst_ref=outs[w].at[p],
                    send_sem=send_sems.at[p, w], recv_sem=recv_sems.at[p, w],
                    device_id=(ox, oy, c), device_id_type=MESH))
        for cp in sends:
            cp.start()
        for cp in sends:
            cp.wait()

    return pl.pallas_call(
        body, name="chip_exchange", in_specs=[ANY] * n_w, out_specs=[ANY] * n_w,
        out_shape=[jax.ShapeDtypeStruct((3,) + s.shape[1:], s.dtype) for s in pair_sums],
        scratch_shapes=[pltpu.SemaphoreType.DMA((3, n_w)), pltpu.SemaphoreType.DMA((3, n_w))],
    )(*pair_sums)


def _pair_share(shards):
    n_w = len(shards)

    def body(*refs):
        ins, outs = refs[:n_w], refs[n_w:2 * n_w]
        send_sems, recv_sems = refs[2 * n_w:]
        x, y, c = _place()
        sends = []
        for w in range(n_w):
            h = shards[w].shape[0] // 2
            mine = outs[w].at[pl.ds(c * h, h)]
            sends.append(pltpu.make_async_remote_copy(
                src_ref=mine, dst_ref=mine, send_sem=send_sems.at[w], recv_sem=recv_sems.at[w],
                device_id=(x, y, 1 - c), device_id_type=MESH))
        for cp in sends:
            cp.start()
        for w in range(n_w):
            h = shards[w].shape[0] // 2
            theirs = outs[w].at[pl.ds((1 - c) * h, h)]
            pltpu.make_async_remote_copy(
                src_ref=theirs, dst_ref=theirs, send_sem=send_sems.at[w], recv_sem=recv_sems.at[w],
                device_id=(x, y, 1 - c), device_id_type=MESH).wait_recv()
        for cp in sends:
            cp.wait_send()

    return pl.pallas_call(
        body, name="pair_share", in_specs=[ANY] * n_w, out_specs=[ANY] * n_w,
        out_shape=[jax.ShapeDtypeStruct(s.shape, s.dtype) for s in shards],
        input_output_aliases={w: w for w in range(n_w)},
        scratch_shapes=[pltpu.SemaphoreType.DMA((n_w,)), pltpu.SemaphoreType.DMA((n_w,))],
    )(*shards)


def _all_reduce_small(vec):
    rows = vec.shape[0]

    def body(v_ref, o_ref, slots, send_sems, recv_sems):
        x, y, c = _place()
        me = 4 * x + 2 * y + c
        slots[me] = v_ref[...]
        sends = []
        for k in range(1, N_DEV):
            peer = (x ^ (k >> 2), y ^ ((k >> 1) & 1), c ^ (k & 1))
            sends.append(pltpu.make_async_remote_copy(
                src_ref=v_ref, dst_ref=slots.at[me], send_sem=send_sems.at[k - 1], recv_sem=recv_sems.at[k - 1],
                device_id=peer, device_id_type=MESH))
        for cp in sends:
            cp.start()
        for k in range(1, N_DEV):
            px, py, pc = x ^ (k >> 2), y ^ ((k >> 1) & 1), c ^ (k & 1)
            landed = slots.at[4 * px + 2 * py + pc]
            pltpu.make_async_remote_copy(
                src_ref=landed, dst_ref=landed, send_sem=send_sems.at[k - 1], recv_sem=recv_sems.at[k - 1],
                device_id=(px, py, pc), device_id_type=MESH).wait_recv()
        for cp in sends:
            cp.wait_send()
        total = slots[0]
        for d in range(1, N_DEV):
            total = total + slots[d]
        o_ref[...] = total

    vm = pl.BlockSpec(memory_space=pltpu.VMEM)
    return pl.pallas_call(
        body, name="all_reduce_small", in_specs=[vm], out_specs=vm, out_shape=jax.ShapeDtypeStruct(vec.shape, F32),
        scratch_shapes=[pltpu.VMEM((N_DEV, rows, LANES), F32), pltpu.SemaphoreType.DMA((N_DEV - 1,)),
                        pltpu.SemaphoreType.DMA((N_DEV - 1,))],
    )(vec)


def _row_tile(rows):
    for tr in (256, 128, 64, 32, 16):
        if rows % tr == 0:
            return tr
    return rows


def _pair_sum(name, place, grad, theirs):
    n, r, c = grad.shape
    half = r // 2
    tr = _row_tile(half)
    nb = half // tr

    def body(place_ref, g_ref, t_ref, o_ref):
        o_ref[...] = (g_ref[...] + t_ref[...]).astype(BF16)

    return pl.pallas_call(
        body, name=name, out_shape=jax.ShapeDtypeStruct((n, half, c), BF16),
        grid_spec=pltpu.PrefetchScalarGridSpec(
            num_scalar_prefetch=1, grid=(n, nb),
            in_specs=[pl.BlockSpec((1, tr, c), lambda j, i, pr: (j, pr[0] * nb + i, 0)),
                      pl.BlockSpec((1, tr, c), lambda j, i, pr: (j, i, 0))],
            out_specs=pl.BlockSpec((1, tr, c), lambda j, i, pr: (j, i, 0))),
        compiler_params=_params("parallel", "parallel"),
    )(place, grad, theirs)


def _sum_chips(name, place, pair_sums, landed):
    _, half, c = pair_sums.shape
    tr = _row_tile(half)
    nb = half // tr

    def body(place_ref, s_ref, q_ref, o_ref):
        total = s_ref[0].astype(F32)
        for p in range(3):
            total = total + q_ref[p].astype(F32)
        o_ref[...] = total

    return pl.pallas_call(
        body, name=name, out_shape=jax.ShapeDtypeStruct((2 * half, c), F32),
        grid_spec=pltpu.PrefetchScalarGridSpec(
            num_scalar_prefetch=1, grid=(nb,),
            in_specs=[pl.BlockSpec((1, tr, c), lambda i, pr: (pr[1], i, 0)),
                      pl.BlockSpec((3, tr, c), lambda i, pr: (0, i, 0))],
            out_specs=pl.BlockSpec((tr, c), lambda i, pr: (pr[0] * nb + i, 0))),
        compiler_params=_params("parallel"),
    )(place, pair_sums, landed)


BIG = ("w_in", "w_branch_a", "w_branch_b", "w_out", "w_ffn_gate", "w_ffn_up", "w_ffn_down", "w_ple_gate", "w_ple_proj")
COLUMN_SHARDED = ("w_in", "w_branch_a", "w_branch_b", "w_ffn_gate", "w_ffn_up", "w_ple_proj")
SMALL = ("norm_mix", "w_pool", "pool_scale", "norm_ffn", "norm_ple", "norm_final")


def _join_columns(w4):
    return jnp.concatenate([w4[j] for j in range(N_CHIPS)], axis=1)


def _split_columns(g):
    k, n = g.shape
    return g.reshape(k, N_CHIPS, n // N_CHIPS).transpose(1, 0, 2)


def _sds(shape, dtype):
    return jax.ShapeDtypeStruct(shape, dtype)


def _local_step(x, p, target, wf, small):
    t, d = x.shape
    w_in, w_gate, w_up, w_down = wf["w_in"], wf["w_ffn_gate"], wf["w_ffn_up"], wf["w_ffn_down"]
    w_a, w_b, w_pp = _join_columns(wf["w_branch_a"]), _join_columns(wf["w_branch_b"]), _join_columns(wf["w_ple_proj"])
    w_out = wf["w_out"].reshape(d, d)
    w_pg = wf["w_ple_gate"].reshape(d, d)
    w_pool_b = small["w_pool"].astype(BF16)
    dp = w_pool_b.shape[0] * w_pool_b.shape[1]
    dff = w_gate.shape[2]

    h1 = _norm_fwd("norm_mix", x, small["norm_mix"])
    u, q = _mm("proj_uq", [h1], [w_in[0]], "nn", [_sds((t, dp), F32), _sds((t, dp), BF16)],
               epilogue=lambda acc: (acc[:, :dp], acc[:, dp:]))
    kv, = _mm("proj_kv", [h1], [w_in[1]], "nn", [_sds((t, d), BF16)])
    ga, = _mm("proj_ga", [h1], [w_in[2]], "nn", [_sds((t, d), BF16)])
    gb, = _mm("proj_gb", [h1], [w_in[3]], "nn", [_sds((t, d), BF16)])
    pooled, ya = _pool_fwd(u, w_pool_b, small["pool_scale"])
    n_pairs = dp // LANES
    yb = _attn_fwd(q, 0, kv, 0, n_pairs, n_pairs)
    ta, tb, merged = _mm(
        "branches_merge", [ya, yb], [w_a, w_b], "nn", [_sds((t, d), BF16)] * 3, extras=[ga, gb], separate=True,
        epilogue=lambda tav, tbv, gav, gbv: (tav, tbv, _sigmoid(gav) * tav + _sigmoid(gbv) * tbv), tm=512)
    x1, = _mm("mix_out", [merged], [w_out], "nn", [_sds((t, d), F32)], extras=[x], epilogue=lambda acc, xv: (acc + xv,))
    h2 = _norm_fwd("norm_ffn", x1, small["norm_ffn"])
    gates, ups, acts = [], [], []
    for j in range(N_CHIPS):
        gj, uj, aj = _mm(f"ffn_gate_up{j}", [h2], [w_gate[j], w_up[j]], "nn", [_sds((t, dff), BF16)] * 3,
                         separate=True, epilogue=lambda gv, uv: (gv, uv, gv * _sigmoid(gv) * uv))
        gates.append(gj), ups.append(uj), acts.append(aj)
    x2, = _mm("ffn_down", acts, [w_down[j] for j in range(N_CHIPS)], "nn", [_sds((t, d), F32)], extras=[x1],
              epilogue=lambda acc, xv: (acc + xv,), tm=512)
    h3 = _norm_fwd("norm_ple", x2, small["norm_ple"])
    gp, pp, x3 = _mm(
        "ple", [h3, p], [w_pg, w_pp], "nn", [_sds((t, d), BF16), _sds((t, d), BF16), _sds((t, d), F32)], extras=[x2],
        separate=True, epilogue=lambda gv, pv, xv: (gv, pv, xv + _sigmoid(gv) * pv), tm=512)
    (dx3,), (d_norm_final, loss_row) = _split2(_final_loss(x3, target, small["norm_final"].reshape(1, d)), 1)

    def ple_bwd(dxv, gv, pv):
        s = _sigmoid(gv)
        return dxv * s, dxv * pv * s * (1.0 - s)

    d_pp, d_gp = _rows("ple_bwd", ple_bwd, [dx3, gp, pp], [_sds((t, d), BF16), _sds((t, d), BF16)])
    g_w_pp = _mm_tn("g_ple_proj", p, d_pp)
    g_w_pg = _mm_tn("g_ple_gate", h3, d_gp)
    dh3, = _mm("d_h3", [d_gp], [w_pg], "nt", [_sds((t, d), F32)])
    (dx2,), (d_norm_ple,) = _split2(_norm_bwd("norm_ple_bwd", dh3, x2, small["norm_ple"], dx3), 1)

    def ffn_bwd(acc, gv, uv):
        s = _sigmoid(gv)
        return acc * uv * (s * (1.0 + gv * (1.0 - s))), acc * (gv * s)

    d_gates, d_ups, g_w_gate, g_w_up, g_w_down = [], [], [], [], []
    for j in range(N_CHIPS):
        dgj, duj = _mm(f"d_act{j}", [dx2], [w_down[j]], "nt", [_sds((t, dff), BF16), _sds((t, dff), BF16)],
                       extras=[gates[j], ups[j]], epilogue=ffn_bwd)
        d_gates.append(dgj), d_ups.append(duj)
        g_w_down.append(_mm_tn(f"g_ffn_down{j}", acts[j], dx2))
        g_w_gate.append(_mm_tn(f"g_ffn_gate{j}", h2, dgj))
        g_w_up.append(_mm_tn(f"g_ffn_up{j}", h2, duj))
    dh2, = _mm("d_h2", d_gates + d_ups, [w_gate[j] for j in range(N_CHIPS)] + [w_up[j] for j in range(N_CHIPS)], "nt",
               [_sds((t, d), F32)], tm=512)
    (dx1,), (d_norm_ffn,) = _split2(_norm_bwd("norm_ffn_bwd", dh2, x1, small["norm_ffn"], dx2), 1)

    def merge_bwd(acc, tav, tbv, gav, gbv):
        sa, sb = _sigmoid(gav), _sigmoid(gbv)
        return acc * sa, acc * sb, acc * tav * sa * (1.0 - sa), acc * tbv * sb * (1.0 - sb)

    d_ta, d_tb, d_ga, d_gb = _mm("d_merged", [dx1], [w_out], "nt", [_sds((t, d), BF16)] * 4,
                                 extras=[ta, tb, ga, gb], epilogue=merge_bwd, tm=512)
    g_w_out = _mm_tn("g_w_out", merged, dx1)
    g_w_a = _mm_tn("g_branch_a", ya, d_ta)
    g_w_b = _mm_tn("g_branch_b", yb, d_tb)
    d_ya, = _mm("d_ya", [d_ta], [w_a], "nt", [_sds((t, dp), F32)])
    d_yb, = _mm("d_yb", [d_tb], [w_b], "nt", [_sds((t, dp), BF16)])
    d_u, g_w_pool, d_pool_scale = _pool_bwd(d_ya, pooled, w_pool_b, small["pool_scale"])
    d_q, d_k, d_v = _attn_bwd(q, 0, kv, 0, n_pairs, d_yb, n_pairs)
    d_proj = [jnp.concatenate([d_u, d_q], axis=1), jnp.concatenate([d_k, d_v], axis=1), d_ga, d_gb]
    g_w_in = [_mm_tn(f"g_w_in{j}", h1, d_proj[j]) for j in range(N_CHIPS)]
    dh1, = _mm("d_h1", d_proj, [w_in[j] for j in range(N_CHIPS)], "nt", [_sds((t, d), F32)], tm=512)
    (grad_x,), (d_norm_mix,) = _split2(_norm_bwd("norm_mix_bwd", dh1, x, small["norm_mix"], dx1), 1)

    big = {
        "w_in": jnp.stack(g_w_in), "w_branch_a": _split_columns(g_w_a), "w_branch_b": _split_columns(g_w_b),
        "w_out": g_w_out.reshape(wf["w_out"].shape), "w_ffn_gate": jnp.stack(g_w_gate), "w_ffn_up": jnp.stack(g_w_up),
        "w_ffn_down": jnp.stack(g_w_down), "w_ple_gate": g_w_pg.reshape(wf["w_ple_gate"].shape),
        "w_ple_proj": _split_columns(g_w_pp),
    }
    small_g = {"norm_mix": d_norm_mix, "w_pool": g_w_pool, "pool_scale": d_pool_scale, "norm_ffn": d_norm_ffn,
               "norm_ple": d_norm_ple, "norm_final": d_norm_final}
    return grad_x, big, small_g, loss_row


def _split2(res, n):
    return res[:n], res[n:]


def _pack_small(small_g, loss_row):
    parts, layout = [], []
    for name in SMALL + ("loss",):
        v = (loss_row if name == "loss" else small_g[name]).reshape(-1, LANES)
        pad = (-v.shape[0]) % 8
        if pad:
            v = jnp.concatenate([v, jnp.zeros((pad, LANES), F32)], axis=0)
        layout.append((name, sum(q.shape[0] for q in parts), v.shape[0]))
        parts.append(v)
    return jnp.concatenate(parts, axis=0), layout


def kernel(x, p, norm_mix, w_in, w_pool, pool_scale, w_branch_a, w_branch_b, w_out, norm_ffn, w_ffn_gate, w_ffn_up, w_ffn_down, norm_ple, w_ple_gate, w_ple_proj, norm_final, loss_target, m_norm_mix, m_w_in, m_w_pool, m_pool_scale, m_w_branch_a, m_w_branch_b, m_w_out, m_norm_ffn, m_w_ffn_gate, m_w_ffn_up, m_w_ffn_down, m_norm_ple, m_w_ple_gate, m_w_ple_proj, m_norm_final, v_norm_mix, v_w_in, v_w_pool, v_pool_scale, v_w_branch_a, v_w_branch_b, v_w_out, v_norm_ffn, v_w_ffn_gate, v_w_ffn_up, v_w_ffn_down, v_norm_ple, v_w_ple_gate, v_w_ple_proj, v_norm_final):
    given = dict(locals())
    names = BIG + SMALL
    order = ("norm_mix", "w_in", "w_pool", "pool_scale", "w_branch_a", "w_branch_b", "w_out", "norm_ffn", "w_ffn_gate",
             "w_ffn_up", "w_ffn_down", "norm_ple", "w_ple_gate", "w_ple_proj", "norm_final")
    t, d = x.shape[1], x.shape[2]
    shard = {n: given[n][0] for n in BIG}
    small = {"norm_mix": norm_mix, "w_pool": w_pool[0], "pool_scale": pool_scale, "norm_ffn": norm_ffn,
             "norm_ple": norm_ple, "norm_final": norm_final}

    gathered = _all_gather_weights([shard[n].astype(BF16) for n in BIG])
    wf = dict(zip(BIG, gathered))
    grad_x, big_g, small_g, loss_row = _local_step(
        x.reshape(t, d), p.reshape(t, p.shape[-1]), loss_target.reshape(t, d), wf, small)

    place = jnp.stack([lax.axis_index("c"), 2 * lax.axis_index("x") + lax.axis_index("y")]).astype(jnp.int32)
    theirs = _pair_exchange([big_g[n] for n in BIG])
    pair_sums = [_pair_sum(f"pair_sum_{n}", place, big_g[n], t) for n, t in zip(BIG, theirs)]
    landed = _chip_exchange(pair_sums)
    halves = [_sum_chips(f"chip_sum_{n}", place, s, q) for n, s, q in zip(BIG, pair_sums, landed)]
    grads = dict(zip(BIG, _pair_share(halves)))

    packed, layout = _pack_small(small_g, loss_row)
    reduced = _all_reduce_small(packed)
    for name, start, rows in layout:
        if name == "loss":
            loss = jnp.sum(reduced[start:start + rows])
        else:
            n_el = small[name].size
            grads[name] = reduced[start:start + rows].reshape(-1)[:n_el]

    deltas, new_m, new_v = {}, {}, {}
    for n in order:
        w = shard[n] if n in BIG else small[n]
        shape2 = w.shape if w.ndim == 2 else ((1, w.shape[0]) if w.ndim == 1 else (w.shape[0] * w.shape[1], w.shape[2]))
        g2 = grads[n].reshape(shape2)
        dl, mn, vn = _adamw(f"adamw_{n}", w.reshape(shape2), g2, given["m_" + n].reshape(shape2),
                            given["v_" + n].reshape(shape2))
        full = given[n].shape
        grads[n], deltas[n], new_m[n], new_v[n] = g2.reshape(full), dl.reshape(full), mn.reshape(full), vn.reshape(full)

    return (loss, grad_x.reshape(x.shape), *[grads[n] for n in order], *[deltas[n] for n in order],
            *[new_m[n] for n in order], *[new_v[n] for n in order])
```

```python
import functools
import math

import jax
import jax.numpy as jnp
from jax import lax
from jax.experimental import pallas as pl
from jax.experimental.pallas import tpu as pltpu

F32 = jnp.float32
BF16 = jnp.bfloat16
MESH = pl.DeviceIdType.MESH

RMS_EPS = 1e-6
POOL_WINDOWS = (2, 4, 8, 16)
POOL_HALO = 16
HEAD_DIM = 64
LANES = 128
ATT_BLOCK = 256
ATT_CHUNK = 256
ATT_SLAB = 256
ATT_SCALE = 1.0 / math.sqrt(HEAD_DIM)
ATT_EXIT_BELOW = -104.0
ADAM_LR, ADAM_B1, ADAM_B2, ADAM_EPS, ADAM_WD, ADAM_STEP = 0.001, 0.9, 0.999, 1e-08, 0.01, 10
V7X_VMEM_LIMIT_BYTES = 56 * 1024 * 1024
N_CHIPS = 4
N_DEV = 8


def _params(*semantics):
    return pltpu.CompilerParams(dimension_semantics=semantics, vmem_limit_bytes=V7X_VMEM_LIMIT_BYTES)


def _sigmoid(z):
    return 1.0 / (1.0 + jnp.exp(-z))


def _tiled_spec(shape, tm, tn, n_total):
    rows, width = shape
    if rows == 1:
        if width == n_total:
            return pl.BlockSpec((1, tn), lambda i, j: (0, j))
        return pl.BlockSpec((1, width), lambda i, j: (0, 0))
    if width == n_total:
        return pl.BlockSpec((tm, tn), lambda i, j: (i, j))
    assert tn == n_total, "an operand narrower than the output needs whole output rows per tile"
    return pl.BlockSpec((tm, width), lambda i, j: (i, 0))


def _mm(name, a_list, b_list, mode, out_shapes, epilogue=None, extras=(), tm=1024, tn=None, separate=False):
    m_total = a_list[0].shape[0]
    n_total = b_list[0].shape[1] if mode == "nn" else b_list[0].shape[0]
    tn = n_total if tn is None else tn
    tm = min(tm, m_total)
    assert m_total % tm == 0 and n_total % tn == 0
    n_a, n_b, n_extra = len(a_list), len(b_list), len(extras)
    assert n_a in (1, n_b)
    dims = (((1,), (0,)), ((), ())) if mode == "nn" else (((1,), (1,)), ((), ()))

    def body(*refs):
        a_refs, b_refs = refs[:n_a], refs[n_a:n_a + n_b]
        e_refs = refs[n_a + n_b:n_a + n_b + n_extra]
        o_refs = refs[n_a + n_b + n_extra:]
        lefts = [a_ref[...] for a_ref in a_refs]
        lefts = [a if a.dtype == BF16 else a.astype(BF16) for a in lefts]
        products = [lax.dot_general(lefts[s % n_a], b_refs[s][...], dims, preferred_element_type=F32)
                    for s in range(n_b)]
        if not separate:
            products = [functools.reduce(lambda p, r: p + r, products)]
        extra_tiles = [e[...].astype(F32) for e in e_refs]
        outs = products if epilogue is None else epilogue(*products, *extra_tiles)
        for o_ref, o in zip(o_refs, outs):
            o_ref[...] = o.astype(o_ref.dtype)

    in_specs = [pl.BlockSpec((tm, a.shape[1]), lambda i, j: (i, 0)) for a in a_list]
    if mode == "nn":
        in_specs += [pl.BlockSpec((b.shape[0], tn), lambda i, j: (0, j)) for b in b_list]
    else:
        in_specs += [pl.BlockSpec((tn, b.shape[1]), lambda i, j: (j, 0)) for b in b_list]
    in_specs += [_tiled_spec(e.shape, tm, tn, n_total) for e in extras]
    out_specs = [_tiled_spec(o.shape, tm, tn, n_total) for o in out_shapes]
    res = pl.pallas_call(
        body, name=name, grid=(m_total // tm, n_total // tn), in_specs=in_specs, out_specs=out_specs,
        out_shape=list(out_shapes), compiler_params=_params("parallel", "parallel"),
    )(*a_list, *b_list, *extras)
    return res


def _mm_tn(name, a, b, tmm=1024):
    m_total, k = a.shape
    n = b.shape[1]
    tmm = min(tmm, m_total)
    assert m_total % tmm == 0

    def body(a_ref, b_ref, o_ref):
        @pl.when(pl.program_id(0) == 0)
        def _():
            o_ref[...] = jnp.zeros_like(o_ref)

        av, bv = a_ref[...], b_ref[...]
        if av.dtype != BF16:
            av = av.astype(BF16)
        if bv.dtype != BF16:
            bv = bv.astype(BF16)
        o_ref[...] += lax.dot_general(av, bv, (((0,), (0,)), ((), ())), preferred_element_type=F32)

    return pl.pallas_call(
        body, name=name, grid=(m_total // tmm,),
        in_specs=[pl.BlockSpec((tmm, k), lambda m: (m, 0)), pl.BlockSpec((tmm, n), lambda m: (m, 0))],
        out_specs=pl.BlockSpec((k, n), lambda m: (0, 0)),
        out_shape=jax.ShapeDtypeStruct((k, n), F32), compiler_params=_params("arbitrary"),
    )(a, b)


def _rows(name, fn, ins, tile_outs, sum_outs=(), tr=512):
    t_total = max(a.shape[0] for a in ins)
    tr = min(tr, t_total)
    assert t_total % tr == 0
    n_in, n_tile = len(ins), len(tile_outs)

    def body(*refs):
        outs = fn(*[r[...].astype(F32) for r in refs[:n_in]])
        for o_ref, o in zip(refs[n_in:n_in + n_tile], outs[:n_tile]):
            o_ref[...] = o.astype(o_ref.dtype)
        if sum_outs:
            @pl.when(pl.program_id(0) == 0)
            def _():
                for s_ref in refs[n_in + n_tile:]:
                    s_ref[...] = jnp.zeros_like(s_ref)

            for s_ref, s in zip(refs[n_in + n_tile:], outs[n_tile:]):
                s_ref[...] += s

    def spec(shape):
        if shape[0] == 1:
            return pl.BlockSpec(shape, lambda i: (0, 0))
        return pl.BlockSpec((tr, shape[1]), lambda i: (i, 0))

    return pl.pallas_call(
        body, name=name, grid=(t_total // tr,), in_specs=[spec(a.shape) for a in ins],
        out_specs=[spec(o.shape) for o in tile_outs] + [spec(s.shape) for s in sum_outs],
        out_shape=list(tile_outs) + list(sum_outs),
        compiler_params=_params("arbitrary" if sum_outs else "parallel"),
    )(*ins)


def _norm_fwd(name, x, gain):
    def fn(xv, g):
        inv = lax.rsqrt(jnp.mean(xv * xv, axis=-1, keepdims=True) + RMS_EPS)
        return (xv * inv * g,)

    return _rows(name, fn, [x, gain], [jax.ShapeDtypeStruct(x.shape, BF16)])[0]


def _norm_bwd(name, dh, x, gain, dres):
    def fn(dhv, xv, g, dr):
        inv = lax.rsqrt(jnp.mean(xv * xv, axis=-1, keepdims=True) + RMS_EPS)
        xn = xv * inv
        dxn = dhv * g
        dx = inv * (dxn - xn * jnp.mean(dxn * xn, axis=-1, keepdims=True)) + dr
        return dx, jnp.sum(dhv * xn, axis=0, keepdims=True)

    d = x.shape[1]
    return _rows(name, fn, [dh, x, gain, dres], [jax.ShapeDtypeStruct(x.shape, F32)],
                 [jax.ShapeDtypeStruct((1, d), F32)])


def _final_loss(x3, target, gain):
    d = x3.shape[1]

    def fn(xv, tv, g):
        inv = lax.rsqrt(jnp.mean(xv * xv, axis=-1, keepdims=True) + RMS_EPS)
        xn = xv * inv
        err = xn * g - tv
        dy = err * (1.0 / d)
        dxn = dy * g
        dx = inv * (dxn - xn * jnp.mean(dxn * xn, axis=-1, keepdims=True))
        return dx, jnp.sum(dy * xn, axis=0, keepdims=True), (0.5 / d) * jnp.sum(err * err, axis=0, keepdims=True)

    return _rows("final_loss", fn, [x3, target, gain], [jax.ShapeDtypeStruct(x3.shape, F32)],
                 [jax.ShapeDtypeStruct((1, d), F32), jax.ShapeDtypeStruct((1, d), F32)])


def _window_counts(t_pos, w):
    return jnp.minimum(t_pos + 1, w).astype(F32)


def _pool_fwd(u, w_pool, scale, tr=512):
    t_total, width = u.shape
    tr = min(tr, t_total)
    n_groups = len(POOL_WINDOWS)
    gdim = width // n_groups
    ext = tr + POOL_HALO

    def body(u_ref, halo_ref, w_ref, s_ref, pooled_ref, ya_ref):
        i = pl.program_id(0)
        halo = jnp.where(i == 0, 0.0, halo_ref[...])
        t_pos = i * tr + lax.broadcasted_iota(jnp.int32, (tr, 1), 0)
        for g, w in enumerate(POOL_WINDOWS):
            cols = slice(g * gdim, (g + 1) * gdim)
            main = u_ref[:, cols]
            win = jnp.concatenate([halo[:, cols], main], axis=0)
            span = 1
            while span < w:
                win = win + pltpu.roll(win, span, 0)
                span *= 2
            pooled = win[POOL_HALO:, :] * (1.0 / _window_counts(t_pos, w)) - main
            pooled_b = pooled.astype(BF16)
            pooled_ref[:, cols] = pooled_b
            mixed = jnp.dot(pooled_b, w_ref[g], preferred_element_type=F32)
            ya_ref[:, cols] = (mixed * s_ref[:, cols]).astype(BF16)

    hb = tr // POOL_HALO
    return pl.pallas_call(
        body, name="pool_fwd", grid=(t_total // tr,),
        in_specs=[pl.BlockSpec((tr, width), lambda i: (i, 0)),
                  pl.BlockSpec((POOL_HALO, width), lambda i: (jnp.maximum(i * hb - 1, 0), 0)),
                  pl.BlockSpec((n_groups, gdim, gdim), lambda i: (0, 0, 0)),
                  pl.BlockSpec((1, width), lambda i: (0, 0))],
        out_specs=[pl.BlockSpec((tr, width), lambda i: (i, 0)), pl.BlockSpec((tr, width), lambda i: (i, 0))],
        out_shape=[jax.ShapeDtypeStruct(u.shape, BF16), jax.ShapeDtypeStruct(u.shape, BF16)],
        compiler_params=_params("parallel"),
    )(u, u, w_pool, scale)


def _pool_bwd(dya, pooled, w_pool, scale, tr=512):
    t_total, width = dya.shape
    tr = min(tr, t_total)
    n_groups = len(POOL_WINDOWS)
    gdim = width // n_groups
    ext = tr + POOL_HALO
    n_tiles = t_total // tr

    def body(d_ref, halo_ref, p_ref, w_ref, s_ref, du_ref, dw_ref, ds_ref):
        i = pl.program_id(0)

        @pl.when(i == 0)
        def _():
            dw_ref[...] = jnp.zeros_like(dw_ref)
            ds_ref[...] = jnp.zeros_like(ds_ref)

        halo = jnp.where(i == n_tiles - 1, 0.0, halo_ref[...])
        t_pos = i * tr + lax.broadcasted_iota(jnp.int32, (ext, 1), 0)
        for g, w in enumerate(POOL_WINDOWS):
            cols = slice(g * gdim, (g + 1) * gdim)
            sc = s_ref[:, cols]
            d_main = d_ref[:, cols]
            pooled_b = p_ref[:, cols]
            mixed = jnp.dot(pooled_b, w_ref[g], preferred_element_type=F32)
            ds_ref[:, cols] += jnp.sum(d_main * mixed, axis=0, keepdims=True)
            dmix = (jnp.concatenate([d_main, halo[:, cols]], axis=0) * sc).astype(BF16)
            dw_ref[g] += lax.dot_general(pooled_b, dmix[:tr, :], (((0,), (0,)), ((), ())),
                                         preferred_element_type=F32)
            dpool = lax.dot_general(dmix, w_ref[g], (((1,), (1,)), ((), ())), preferred_element_type=F32)
            win = dpool * (1.0 / _window_counts(t_pos, w))
            span = 1
            while span < w:
                win = win + pltpu.roll(win, ext - span, 0)
                span *= 2
            du_ref[:, cols] = (win[:tr, :] - dpool[:tr, :]).astype(BF16)

    hb = tr // POOL_HALO
    last_halo = t_total // POOL_HALO - 1
    return pl.pallas_call(
        body, name="pool_bwd", grid=(n_tiles,),
        in_specs=[pl.BlockSpec((tr, width), lambda i: (i, 0)),
                  pl.BlockSpec((POOL_HALO, width), lambda i: (jnp.minimum((i + 1) * hb, last_halo), 0)),
                  pl.BlockSpec((tr, width), lambda i: (i, 0)),
                  pl.BlockSpec((n_groups, gdim, gdim), lambda i: (0, 0, 0)),
                  pl.BlockSpec((1, width), lambda i: (0, 0))],
        out_specs=[pl.BlockSpec((tr, width), lambda i: (i, 0)),
                   pl.BlockSpec((n_groups, gdim, gdim), lambda i: (0, 0, 0)),
                   pl.BlockSpec((1, width), lambda i: (0, 0))],
        out_shape=[jax.ShapeDtypeStruct(dya.shape, BF16), jax.ShapeDtypeStruct((n_groups, gdim, gdim), F32),
                   jax.ShapeDtypeStruct((1, width), F32)],
        compiler_params=_params("arbitrary"),
    )(dya, dya, pooled, w_pool, scale)


def _head_masks():
    lane = lax.broadcasted_iota(jnp.int32, (1, LANES), 1)
    return lane < HEAD_DIM


def _stack_heads(tile, first):
    zero = jnp.zeros_like(tile)
    return jnp.concatenate([jnp.where(first, tile, zero), jnp.where(first, zero, tile)], axis=0)


def _split_bf16(v):
    hi = v.astype(BF16)
    lo = (v - hi.astype(F32)).astype(BF16)
    return hi, lo


def _slab_scores(q, kd, t_pos, k_start):
    z = lax.dot_general(q, kd, (((1,), (1,)), ((), ())), preferred_element_type=F32)
    col = lax.broadcasted_iota(jnp.int32, (1, 2 * ATT_SLAB), 1)
    mask = k_start + (col & (ATT_SLAB - 1)) < t_pos
    e = jnp.exp(-jnp.abs(z))
    log_fail = jnp.where(mask, -(jnp.maximum(z, 0.0) + jnp.log(1.0 + e)), 0.0)
    return z, mask, log_fail, e


def _tri(upper):
    r = lax.broadcasted_iota(jnp.int32, (ATT_CHUNK, ATT_CHUNK), 0)
    c = lax.broadcasted_iota(jnp.int32, (ATT_CHUNK, ATT_CHUNK), 1)
    return jnp.where(r > c if upper else r < c, 1.0, 0.0).astype(BF16)


def _scan_chunk(v, tri):
    hi, lo = _split_bf16(v)
    return (jnp.dot(hi, tri, preferred_element_type=F32) + jnp.dot(lo, tri, preferred_element_type=F32))


def _lane_bcast(col):
    return jnp.broadcast_to(col, (col.shape[0], LANES))


def _scan_slab(v, tri, carries, from_right):
    n_chunks = ATT_SLAB // ATT_CHUNK
    edge = 0 if from_right else ATT_CHUNK - 1
    parts, new_carries = [None] * (2 * n_chunks), []
    for head in range(2):
        run = carries[head]
        for c in (reversed(range(n_chunks)) if from_right else range(n_chunks)):
            lo_col = head * ATT_SLAB + c * ATT_CHUNK
            vc = v[:, lo_col:lo_col + ATT_CHUNK]
            sc = _scan_chunk(vc, tri)
            parts[head * n_chunks + c] = sc + jnp.concatenate([run] * (ATT_CHUNK // LANES), axis=1)
            run = run + _lane_bcast(sc[:, edge:edge + 1] + vc[:, edge:edge + 1])
        new_carries.append(run)
    return jnp.concatenate(parts, axis=1), new_carries


def _fold_heads(stacked, first):
    s = stacked.shape[0] // 2
    return jnp.where(first, stacked[:s], stacked[s:])


class _NoRider:
    operands, out_shapes, scratch = (), (), ()

    def split(self, refs, n_base_in, n_base_out):
        n_in, n_out, n_sem = len(self.operands), len(self.out_shapes), len(self.scratch)
        a = n_base_in + n_in
        b = a + n_base_out + n_out
        mine = (refs[n_base_in:a], refs[a + n_base_out:b], refs[b:b + n_sem])
        return refs[:n_base_in], refs[a:a + n_base_out], refs[b + n_sem:], mine

    def start(self, ins, outs, sems):
        pass

    def relay(self, ins, outs, sems):
        pass

    def finish(self, ins, outs, sems):
        pass

    def at_steps(self, refs, first_step, relay_step, last_step):
        if not self.operands:
            return (lambda: None), (lambda: None)

        def top():
            pl.when(first_step)(lambda: self.start(*refs))
            pl.when(relay_step)(lambda: self.relay(*refs))

        return top, lambda: pl.when(last_step)(lambda: self.finish(*refs))


def _attn_fwd(q_src, q_col, kv_src, k_col, v_col, n_pairs=4, rider=_NoRider()):
    t_total = q_src.shape[0]
    blk = ATT_BLOCK
    n_blocks = t_total // blk
    assert t_total % ATT_SLAB == 0

    def body(*refs):
        (q_ref, k_ref, v_ref), (o_ref,), _, riding = rider.split(refs, 3, 1)
        h, i = pl.program_id(0), pl.program_id(1)
        top, bottom = rider.at_steps(riding, (h == 0) & (i == 0), (h == n_pairs - 1) & (i == 0),
                                     (h == n_pairs - 1) & (i == n_blocks - 1))
        top()
        first = _head_masks()
        q = q_ref[...] * ATT_SCALE
        t_pos = i * blk + lax.broadcasted_iota(jnp.int32, (blk, 1), 0)
        suffix_tri = _tri(upper=True)

        def more(state):
            slab, reach = state[0], state[1]
            return jnp.logical_and(slab >= 0, reach > ATT_EXIT_BELOW)

        def step(state):
            slab, _, acc, right_a, right_b = state
            k_start = pl.multiple_of(slab * ATT_SLAB, ATT_SLAB)
            kd = _stack_heads(k_ref[pl.ds(k_start, ATT_SLAB), :], first)
            vd = _stack_heads(v_ref[pl.ds(k_start, ATT_SLAB), :], first)
            z, mask, log_fail, _ = _slab_scores(q, kd, t_pos, k_start)
            suffix, (right_a, right_b) = _scan_slab(log_fail, suffix_tri, (right_a, right_b), from_right=True)
            a = jnp.exp(jnp.where(mask, z + log_fail + suffix, -1e30)).astype(BF16)
            acc = acc + jnp.dot(a, vd, preferred_element_type=F32)
            return slab - 1, jnp.max(jnp.maximum(right_a, right_b)), acc, right_a, right_b

        zero = jnp.zeros((blk, LANES), F32)
        state = lax.while_loop(more, step, ((i * blk) // ATT_SLAB, jnp.float32(0.0), zero, zero, zero))
        o_ref[...] = state[2].astype(BF16)
        bottom()

    res = pl.pallas_call(
        body, name="attn_fwd", grid=(n_pairs, n_blocks),
        in_specs=[pl.BlockSpec((blk, LANES), lambda h, i: (i, q_col + h)),
                  pl.BlockSpec((t_total, LANES), lambda h, i: (0, k_col + h)),
                  pl.BlockSpec((t_total, LANES), lambda h, i: (0, v_col + h))] + [ANY] * len(rider.operands),
        out_specs=[pl.BlockSpec((blk, LANES), lambda h, i: (i, h))] + [ANY] * len(rider.out_shapes),
        out_shape=[jax.ShapeDtypeStruct((t_total, n_pairs * LANES), BF16)] + list(rider.out_shapes),
        scratch_shapes=list(rider.scratch),
        compiler_params=_params("arbitrary", "arbitrary"),
    )(q_src, kv_src, kv_src, *rider.operands)
    return res[0], res[1:]


def _attn_bwd(q_src, q_col, kv_src, k_col, v_col, dy, n_pairs=4, rider=_NoRider()):
    t_total = q_src.shape[0]
    blk = ATT_BLOCK
    n_blocks = t_total // blk
    n_slabs = t_total // ATT_SLAB
    assert t_total % ATT_SLAB == 0

    def body(*refs):
        (q_ref, dy_ref, k_ref, v_ref), (dq_ref, dk_ref, dv_ref), (g_s, dk_acc, dv_acc), riding = rider.split(refs, 4, 3)
        h, i = pl.program_id(0), pl.program_id(1)
        top, bottom = rider.at_steps(riding, (h == 0) & (i == 0), (h == n_pairs - 1) & (i == 0),
                                     (h == n_pairs - 1) & (i == n_blocks - 1))
        top()

        @pl.when(i == 0)
        def _():
            dk_acc[...] = jnp.zeros_like(dk_acc)
            dv_acc[...] = jnp.zeros_like(dv_acc)

        first = _head_masks()
        q = q_ref[...] * ATT_SCALE
        dy = dy_ref[...]
        t_pos = i * blk + lax.broadcasted_iota(jnp.int32, (blk, 1), 0)
        suffix_tri = _tri(upper=True)
        prefix_tri = _tri(upper=False)
        diag = (i * blk) // ATT_SLAB

        def more(state):
            slab, reach = state[0], state[1]
            return jnp.logical_and(slab >= 0, reach > ATT_EXIT_BELOW)

        def sweep1(state):
            slab, _, right_a, right_b = state
            k_start = pl.multiple_of(slab * ATT_SLAB, ATT_SLAB)
            kd = _stack_heads(k_ref[pl.ds(k_start, ATT_SLAB), :], first)
            vd = _stack_heads(v_ref[pl.ds(k_start, ATT_SLAB), :], first)
            z, mask, log_fail, _ = _slab_scores(q, kd, t_pos, k_start)
            suffix, (right_a, right_b) = _scan_slab(log_fail, suffix_tri, (right_a, right_b), from_right=True)
            a = jnp.exp(jnp.where(mask, z + log_fail + suffix, -1e30))
            da = lax.dot_general(dy, vd, (((1,), (1,)), ((), ())), preferred_element_type=F32)
            g_s[slab] = da * a
            dv_acc[pl.ds(k_start, ATT_SLAB), :] += _fold_heads(lax.dot_general(
                a.astype(BF16), dy, (((0,), (0,)), ((), ())), preferred_element_type=F32), first)
            return slab - 1, jnp.max(jnp.maximum(right_a, right_b)), right_a, right_b

        zero = jnp.zeros((blk, LANES), F32)
        end = lax.while_loop(more, sweep1, (diag, jnp.float32(0.0), zero, zero))[0]

        def sweep2(slab, carry):
            dq, left_a, left_b = carry
            k_start = pl.multiple_of(slab * ATT_SLAB, ATT_SLAB)
            kd = _stack_heads(k_ref[pl.ds(k_start, ATT_SLAB), :], first)
            g = g_s[slab]
            sig = _sigmoid(lax.dot_general(q, kd, (((1,), (1,)), ((), ())), preferred_element_type=F32))
            prefix, (left_a, left_b) = _scan_slab(g, prefix_tri, (left_a, left_b), from_right=False)
            col = lax.broadcasted_iota(jnp.int32, (1, 2 * ATT_SLAB), 1)
            mask = k_start + (col & (ATT_SLAB - 1)) < t_pos
            dz = jnp.where(mask, g * (1.0 - sig) - sig * prefix, 0.0).astype(BF16)
            dq = dq + jnp.dot(dz, kd, preferred_element_type=F32)
            dk_acc[pl.ds(k_start, ATT_SLAB), :] += _fold_heads(lax.dot_general(
                dz, q, (((0,), (0,)), ((), ())), preferred_element_type=F32), first)
            return dq, left_a, left_b

        dq = lax.fori_loop(end + 1, diag + 1, sweep2, (zero, zero, zero))[0]
        dq_ref[...] = (dq * ATT_SCALE).astype(BF16)

        @pl.when(i == n_blocks - 1)
        def _():
            dk_ref[...] = dk_acc[...].astype(BF16)
            dv_ref[...] = dv_acc[...].astype(BF16)

        bottom()

    out = jax.ShapeDtypeStruct((t_total, n_pairs * LANES), BF16)
    res = pl.pallas_call(
        body, name="attn_bwd", grid=(n_pairs, n_blocks),
        in_specs=[pl.BlockSpec((blk, LANES), lambda h, i: (i, q_col + h)),
                  pl.BlockSpec((blk, LANES), lambda h, i: (i, h)),
                  pl.BlockSpec((t_total, LANES), lambda h, i: (0, k_col + h)),
                  pl.BlockSpec((t_total, LANES), lambda h, i: (0, v_col + h))] + [ANY] * len(rider.operands),
        out_specs=[pl.BlockSpec((blk, LANES), lambda h, i: (i, h)),
                   pl.BlockSpec((t_total, LANES), lambda h, i: (0, h)),
                   pl.BlockSpec((t_total, LANES), lambda h, i: (0, h))] + [ANY] * len(rider.out_shapes),
        out_shape=[out, out, out] + list(rider.out_shapes),
        scratch_shapes=list(rider.scratch) + [pltpu.VMEM((n_slabs, blk, 2 * ATT_SLAB), F32),
                                              pltpu.VMEM((t_total, LANES), F32), pltpu.VMEM((t_total, LANES), F32)],
        compiler_params=_params("arbitrary", "arbitrary"),
    )(q_src, dy, kv_src, kv_src, *rider.operands)
    return res[:3], res[3:]


def _adamw(name, w, g, m, v):
    def fn(wv, gv, mv, vv):
        mn = ADAM_B1 * mv + (1.0 - ADAM_B1) * gv
        vn = ADAM_B2 * vv + (1.0 - ADAM_B2) * (gv * gv)
        m_hat = mn / (1.0 - ADAM_B1 ** ADAM_STEP)
        v_hat = vn / (1.0 - ADAM_B2 ** ADAM_STEP)
        return -ADAM_LR * (m_hat / (jnp.sqrt(v_hat) + ADAM_EPS) + ADAM_WD * wv), mn, vn

    rows = w.shape[0]
    tr = _row_tile(rows)
    shp = jax.ShapeDtypeStruct(w.shape, F32)
    if rows == 1:
        def body(w_ref, g_ref, m_ref, v_ref, d_ref, mo_ref, vo_ref):
            d, mn, vn = fn(w_ref[...], g_ref[...], m_ref[...], v_ref[...])
            d_ref[...], mo_ref[...], vo_ref[...] = d, mn, vn

        return pl.pallas_call(body, name=name, out_shape=[shp, shp, shp])(w, g, m, v)
    return _rows(name, fn, [w, g, m, v], [shp, shp, shp], tr=tr)


def _place():
    return lax.axis_index("x"), lax.axis_index("y"), lax.axis_index("c")


def _other_chips(x, y):
    return [(1 - x, y), (x, 1 - y), (1 - x, 1 - y)]


ANY = pl.BlockSpec(memory_space=pl.ANY)


def _remote(src, dst, send_sem, recv_sem, to):
    return pltpu.make_async_remote_copy(src_ref=src, dst_ref=dst, send_sem=send_sem, recv_sem=recv_sem,
                                        device_id=to, device_id_type=MESH)


class _WeightGather(_NoRider):
    def __init__(self, shards):
        n_w = len(shards)
        self.operands = list(shards)
        self.out_shapes = [jax.ShapeDtypeStruct((N_CHIPS,) + s.shape, s.dtype) for s in shards]
        self.scratch = [pltpu.SemaphoreType.DMA((3, n_w))] * 4 + [pltpu.SemaphoreType.DMA((n_w,))] * 2

    def _copies(self, ins, outs, sems):
        send_sems, recv_sems, relay_send, relay_recv, own_send, own_recv = sems
        x, y, c = _place()
        my_chip, sibling = 2 * x + y, (x, y, 1 - c)
        n_w = len(ins)

        def half(w, chip, core):
            h = self.operands[w].shape[0] // 2
            return outs[w].at[chip, pl.ds(core * h, h)]

        own = [_remote(ins[w], outs[w].at[my_chip], own_send.at[w], own_recv.at[w], sibling) for w in range(n_w)]
        sends, landed, relays, relayed = [], [], [], []
        for p, (ox, oy) in enumerate(_other_chips(x, y)):
            for w in range(n_w):
                h = self.operands[w].shape[0] // 2
                sends.append(_remote(ins[w].at[pl.ds(c * h, h)], half(w, my_chip, c), send_sems.at[p, w],
                                     recv_sems.at[p, w], (ox, oy, c)))
                here = half(w, 2 * ox + oy, c)
                landed.append(_remote(here, here, send_sems.at[p, w], recv_sems.at[p, w], (ox, oy, c)))
                relays.append(_remote(here, here, relay_send.at[p, w], relay_recv.at[p, w], sibling))
                there = half(w, 2 * ox + oy, 1 - c)
                relayed.append(_remote(there, there, relay_send.at[p, w], relay_recv.at[p, w], sibling))
        return own, sends, landed, relays, relayed

    def start(self, ins, outs, sems):
        own, sends, _, _, _ = self._copies(ins, outs, sems)
        for cp in own + sends:
            cp.start()

    def relay(self, ins, outs, sems):
        _, _, landed, relays, _ = self._copies(ins, outs, sems)
        for arrival, cp in zip(landed, relays):
            arrival.wait_recv()
            cp.start()

    def finish(self, ins, outs, sems):
        own, sends, _, relays, relayed = self._copies(ins, outs, sems)
        for arrival in relayed:
            arrival.wait_recv()
        for cp in sends + relays:
            cp.wait_send()
        for cp in own:
            cp.wait()


class _ChipExchange(_NoRider):
    def __init__(self, pair_sums):
        n_w = len(pair_sums)
        self.operands = list(pair_sums)
        self.out_shapes = [jax.ShapeDtypeStruct((3,) + s.shape[1:], s.dtype) for s in pair_sums]
        self.scratch = [pltpu.SemaphoreType.DMA((3, n_w))] * 2

    def _copies(self, ins, outs, sems):
        send_sems, recv_sems = sems
        x, y, c = _place()
        return [_remote(ins[w].at[2 * ox + oy], outs[w].at[p], send_sems.at[p, w], recv_sems.at[p, w], (ox, oy, c))
                for p, (ox, oy) in enumerate(_other_chips(x, y)) for w in range(len(ins))]

    def start(self, ins, outs, sems):
        for cp in self._copies(ins, outs, sems):
            cp.start()

    def finish(self, ins, outs, sems):
        for cp in self._copies(ins, outs, sems):
            cp.wait()


def _run_exchange(name, plan):
    n_in, n_out = len(plan.operands), len(plan.out_shapes)

    def body(*refs):
        parts = (refs[:n_in], refs[n_in:n_in + n_out], refs[n_in + n_out:])
        plan.start(*parts)
        plan.relay(*parts)
        plan.finish(*parts)

    return pl.pallas_call(body, name=name, in_specs=[ANY] * n_in, out_specs=[ANY] * n_out,
                          out_shape=list(plan.out_shapes), scratch_shapes=list(plan.scratch))(*plan.operands)


def _pair_exchange(name, grads):
    n_w = len(grads)

    def halves(w):
        return grads[w].shape[1] // 2

    def body(*refs):
        ins, theirs = refs[:n_w], refs[n_w:2 * n_w]
        send_sems, recv_sems = refs[2 * n_w:]
        x, y, c = _place()
        sends = [pltpu.make_async_remote_copy(
            src_ref=ins[w].at[:, pl.ds((1 - c) * halves(w), halves(w)), :], dst_ref=theirs[w],
            send_sem=send_sems.at[w], recv_sem=recv_sems.at[w], device_id=(x, y, 1 - c), device_id_type=MESH)
            for w in range(n_w)]
        for cp in sends:
            cp.start()
        for cp in sends:
            cp.wait()

    return pl.pallas_call(
        body, name=name, in_specs=[ANY] * n_w, out_specs=[ANY] * n_w,
        out_shape=[jax.ShapeDtypeStruct((N_CHIPS, halves(w), grads[w].shape[2]), F32) for w in range(n_w)],
        scratch_shapes=[pltpu.SemaphoreType.DMA((n_w,)), pltpu.SemaphoreType.DMA((n_w,))],
    )(*grads)


def _pair_share(shards):
    n_w = len(shards)

    def body(*refs):
        ins, outs = refs[:n_w], refs[n_w:2 * n_w]
        send_sems, recv_sems = refs[2 * n_w:]
        x, y, c = _place()
        sends = []
        for w in range(n_w):
            h = shards[w].shape[0] // 2
            mine = outs[w].at[pl.ds(c * h, h)]
            sends.append(pltpu.make_async_remote_copy(
                src_ref=mine, dst_ref=mine, send_sem=send_sems.at[w], recv_sem=recv_sems.at[w],
                device_id=(x, y, 1 - c), device_id_type=MESH))
        for cp in sends:
            cp.start()
        for w in range(n_w):
            h = shards[w].shape[0] // 2
            theirs = outs[w].at[pl.ds((1 - c) * h, h)]
            pltpu.make_async_remote_copy(
                src_ref=theirs, dst_ref=theirs, send_sem=send_sems.at[w], recv_sem=recv_sems.at[w],
                device_id=(x, y, 1 - c), device_id_type=MESH).wait_recv()
        for cp in sends:
            cp.wait_send()

    return pl.pallas_call(
        body, name="pair_share", in_specs=[ANY] * n_w, out_specs=[ANY] * n_w,
        out_shape=[jax.ShapeDtypeStruct(s.shape, s.dtype) for s in shards],
        input_output_aliases={w: w for w in range(n_w)},
        scratch_shapes=[pltpu.SemaphoreType.DMA((n_w,)), pltpu.SemaphoreType.DMA((n_w,))],
    )(*shards)


def _all_reduce_small(vec):
    rows = vec.shape[0]

    def body(v_ref, o_ref, slots, send_sems, recv_sems):
        x, y, c = _place()
        me = 4 * x + 2 * y + c
        slots[me] = v_ref[...]
        sends = []
        for k in range(1, N_DEV):
            peer = (x ^ (k >> 2), y ^ ((k >> 1) & 1), c ^ (k & 1))
            sends.append(pltpu.make_async_remote_copy(
                src_ref=v_ref, dst_ref=slots.at[me], send_sem=send_sems.at[k - 1], recv_sem=recv_sems.at[k - 1],
                device_id=peer, device_id_type=MESH))
        for cp in sends:
            cp.start()
        for k in range(1, N_DEV):
            px, py, pc = x ^ (k >> 2), y ^ ((k >> 1) & 1), c ^ (k & 1)
            landed = slots.at[4 * px + 2 * py + pc]
            pltpu.make_async_remote_copy(
                src_ref=landed, dst_ref=landed, send_sem=send_sems.at[k - 1], recv_sem=recv_sems.at[k - 1],
                device_id=(px, py, pc), device_id_type=MESH).wait_recv()
        for cp in sends:
            cp.wait_send()
        total = slots[0]
        for d in range(1, N_DEV):
            total = total + slots[d]
        o_ref[...] = total

    vm = pl.BlockSpec(memory_space=pltpu.VMEM)
    return pl.pallas_call(
        body, name="all_reduce_small", in_specs=[vm], out_specs=vm, out_shape=jax.ShapeDtypeStruct(vec.shape, F32),
        scratch_shapes=[pltpu.VMEM((N_DEV, rows, LANES), F32), pltpu.SemaphoreType.DMA((N_DEV - 1,)),
                        pltpu.SemaphoreType.DMA((N_DEV - 1,))],
    )(vec)


def _row_tile(rows):
    for tr in (256, 128, 64, 32, 16):
        if rows % tr == 0:
            return tr
    return rows


def _pair_sum(name, place, grad, theirs):
    n, r, c = grad.shape
    half = r // 2
    tr = _row_tile(half)
    nb = half // tr

    def body(place_ref, g_ref, t_ref, o_ref):
        o_ref[...] = (g_ref[...] + t_ref[...]).astype(BF16)

    return pl.pallas_call(
        body, name=name, out_shape=jax.ShapeDtypeStruct((n, half, c), BF16),
        grid_spec=pltpu.PrefetchScalarGridSpec(
            num_scalar_prefetch=1, grid=(n, nb),
            in_specs=[pl.BlockSpec((1, tr, c), lambda j, i, pr: (j, pr[0] * nb + i, 0)),
                      pl.BlockSpec((1, tr, c), lambda j, i, pr: (j, i, 0))],
            out_specs=pl.BlockSpec((1, tr, c), lambda j, i, pr: (j, i, 0))),
        compiler_params=_params("parallel", "parallel"),
    )(place, grad, theirs)


def _sum_chips(name, place, pair_sums, landed):
    _, half, c = pair_sums.shape
    tr = _row_tile(half)
    nb = half // tr

    def body(place_ref, s_ref, q_ref, o_ref):
        total = s_ref[0].astype(F32)
        for p in range(3):
            total = total + q_ref[p].astype(F32)
        o_ref[...] = total

    return pl.pallas_call(
        body, name=name, out_shape=jax.ShapeDtypeStruct((2 * half, c), F32),
        grid_spec=pltpu.PrefetchScalarGridSpec(
            num_scalar_prefetch=1, grid=(nb,),
            in_specs=[pl.BlockSpec((1, tr, c), lambda i, pr: (pr[1], i, 0)),
                      pl.BlockSpec((3, tr, c), lambda i, pr: (0, i, 0))],
            out_specs=pl.BlockSpec((tr, c), lambda i, pr: (pr[0] * nb + i, 0))),
        compiler_params=_params("parallel"),
    )(place, pair_sums, landed)


BIG = ("w_in", "w_branch_a", "w_branch_b", "w_out", "w_ffn_gate", "w_ffn_up", "w_ffn_down", "w_ple_gate", "w_ple_proj")
LATE = BIG[1:]
COLUMN_SHARDED = ("w_in", "w_branch_a", "w_branch_b", "w_ffn_gate", "w_ffn_up", "w_ple_proj")
SMALL = ("norm_mix", "w_pool", "pool_scale", "norm_ffn", "norm_ple", "norm_final")


def _join_columns(w4):
    return jnp.concatenate([w4[j] for j in range(N_CHIPS)], axis=1)


def _split_columns(g):
    k, n = g.shape
    return g.reshape(k, N_CHIPS, n // N_CHIPS).transpose(1, 0, 2)


def _sds(shape, dtype):
    return jax.ShapeDtypeStruct(shape, dtype)


def _local_step(x, p, target, wf, small, gather_late=None, exchange_early=None):
    t, d = x.shape
    w_in = wf["w_in"]
    w_pool_b = small["w_pool"].astype(BF16)
    dp = w_pool_b.shape[0] * w_pool_b.shape[1]

    h1 = _norm_fwd("norm_mix", x, small["norm_mix"])
    u, q = _mm("proj_uq", [h1], [w_in[0]], "nn", [_sds((t, dp), F32), _sds((t, dp), BF16)],
               epilogue=lambda acc: (acc[:, :dp], acc[:, dp:]))
    kv, = _mm("proj_kv", [h1], [w_in[1]], "nn", [_sds((t, d), BF16)])
    ga, = _mm("proj_ga", [h1], [w_in[2]], "nn", [_sds((t, d), BF16)])
    gb, = _mm("proj_gb", [h1], [w_in[3]], "nn", [_sds((t, d), BF16)])
    pooled, ya = _pool_fwd(u, w_pool_b, small["pool_scale"])
    n_pairs = dp // LANES
    yb, late = _attn_fwd(q, 0, kv, 0, n_pairs, n_pairs, rider=gather_late or _NoRider())
    wf = {**wf, **dict(zip(LATE, late))}
    w_gate, w_up, w_down = wf["w_ffn_gate"], wf["w_ffn_up"], wf["w_ffn_down"]
    w_a, w_b, w_pp = _join_columns(wf["w_branch_a"]), _join_columns(wf["w_branch_b"]), _join_columns(wf["w_ple_proj"])
    w_out = wf["w_out"].reshape(d, d)
    w_pg = wf["w_ple_gate"].reshape(d, d)
    dff = w_gate.shape[2]
    ta, tb, merged = _mm(
        "branches_merge", [ya, yb], [w_a, w_b], "nn", [_sds((t, d), BF16)] * 3, extras=[ga, gb], separate=True,
        epilogue=lambda tav, tbv, gav, gbv: (tav, tbv, _sigmoid(gav) * tav + _sigmoid(gbv) * tbv), tm=512)
    x1, = _mm("mix_out", [merged], [w_out], "nn", [_sds((t, d), F32)], extras=[x], epilogue=lambda acc, xv: (acc + xv,))
    h2 = _norm_fwd("norm_ffn", x1, small["norm_ffn"])
    gates, ups, acts = [], [], []
    for j in range(N_CHIPS):
        gj, uj, aj = _mm(f"ffn_gate_up{j}", [h2], [w_gate[j], w_up[j]], "nn", [_sds((t, dff), BF16)] * 3,
                         separate=True, epilogue=lambda gv, uv: (gv, uv, gv * _sigmoid(gv) * uv))
        gates.append(gj), ups.append(uj), acts.append(aj)
    x2, = _mm("ffn_down", acts, [w_down[j] for j in range(N_CHIPS)], "nn", [_sds((t, d), F32)], extras=[x1],
              epilogue=lambda acc, xv: (acc + xv,), tm=512)
    h3 = _norm_fwd("norm_ple", x2, small["norm_ple"])
    gp, pp, x3 = _mm(
        "ple", [h3, p], [w_pg, w_pp], "nn", [_sds((t, d), BF16), _sds((t, d), BF16), _sds((t, d), F32)], extras=[x2],
        separate=True, epilogue=lambda gv, pv, xv: (gv, pv, xv + _sigmoid(gv) * pv), tm=512)
    (dx3,), (d_norm_final, loss_row) = _split2(_final_loss(x3, target, small["norm_final"].reshape(1, d)), 1)

    def ple_bwd(dxv, gv, pv):
        s = _sigmoid(gv)
        return dxv * s, dxv * pv * s * (1.0 - s)

    d_pp, d_gp = _rows("ple_bwd", ple_bwd, [dx3, gp, pp], [_sds((t, d), BF16), _sds((t, d), BF16)])
    g_w_pp = _mm_tn("g_ple_proj", p, d_pp)
    g_w_pg = _mm_tn("g_ple_gate", h3, d_gp)
    dh3, = _mm("d_h3", [d_gp], [w_pg], "nt", [_sds((t, d), F32)])
    (dx2,), (d_norm_ple,) = _split2(_norm_bwd("norm_ple_bwd", dh3, x2, small["norm_ple"], dx3), 1)

    def ffn_bwd(acc, gv, uv):
        s = _sigmoid(gv)
        return acc * uv * (s * (1.0 + gv * (1.0 - s))), acc * (gv * s)

    d_gates, d_ups, g_w_gate, g_w_up, g_w_down = [], [], [], [], []
    for j in range(N_CHIPS):
        dgj, duj = _mm(f"d_act{j}", [dx2], [w_down[j]], "nt", [_sds((t, dff), BF16), _sds((t, dff), BF16)],
                       extras=[gates[j], ups[j]], epilogue=ffn_bwd)
        d_gates.append(dgj), d_ups.append(duj)
        g_w_down.append(_mm_tn(f"g_ffn_down{j}", acts[j], dx2))
        g_w_gate.append(_mm_tn(f"g_ffn_gate{j}", h2, dgj))
        g_w_up.append(_mm_tn(f"g_ffn_up{j}", h2, duj))
    dh2, = _mm("d_h2", d_gates + d_ups, [w_gate[j] for j in range(N_CHIPS)] + [w_up[j] for j in range(N_CHIPS)], "nt",
               [_sds((t, d), F32)], tm=512)
    (dx1,), (d_norm_ffn,) = _split2(_norm_bwd("norm_ffn_bwd", dh2, x1, small["norm_ffn"], dx2), 1)

    def merge_bwd(acc, tav, tbv, gav, gbv):
        sa, sb = _sigmoid(gav), _sigmoid(gbv)
        return acc * sa, acc * sb, acc * tav * sa * (1.0 - sa), acc * tbv * sb * (1.0 - sb)

    d_ta, d_tb, d_ga, d_gb = _mm("d_merged", [dx1], [w_out], "nt", [_sds((t, d), BF16)] * 4,
                                 extras=[ta, tb, ga, gb], epilogue=merge_bwd, tm=512)
    g_w_out = _mm_tn("g_w_out", merged, dx1)
    g_w_a = _mm_tn("g_branch_a", ya, d_ta)
    g_w_b = _mm_tn("g_branch_b", yb, d_tb)
    d_ya, = _mm("d_ya", [d_ta], [w_a], "nt", [_sds((t, dp), F32)])
    d_yb, = _mm("d_yb", [d_tb], [w_b], "nt", [_sds((t, dp), BF16)])
    d_u, g_w_pool, d_pool_scale = _pool_bwd(d_ya, pooled, w_pool_b, small["pool_scale"])
    big = {
        "w_branch_a": _split_columns(g_w_a), "w_branch_b": _split_columns(g_w_b),
        "w_out": g_w_out.reshape(wf["w_out"].shape), "w_ffn_gate": jnp.stack(g_w_gate), "w_ffn_up": jnp.stack(g_w_up),
        "w_ffn_down": jnp.stack(g_w_down), "w_ple_gate": g_w_pg.reshape(wf["w_ple_gate"].shape),
        "w_ple_proj": _split_columns(g_w_pp),
    }
    rider = exchange_early(big) if exchange_early else _NoRider()
    (d_q, d_k, d_v), early = _attn_bwd(q, 0, kv, 0, n_pairs, d_yb, n_pairs, rider=rider)
    d_proj = [jnp.concatenate([d_u, d_q], axis=1), jnp.concatenate([d_k, d_v], axis=1), d_ga, d_gb]
    big["w_in"] = jnp.stack([_mm_tn(f"g_w_in{j}", h1, d_proj[j]) for j in range(N_CHIPS)])
    dh1, = _mm("d_h1", d_proj, [w_in[j] for j in range(N_CHIPS)], "nt", [_sds((t, d), F32)], tm=512)
    (grad_x,), (d_norm_mix,) = _split2(_norm_bwd("norm_mix_bwd", dh1, x, small["norm_mix"], dx1), 1)
    small_g = {"norm_mix": d_norm_mix, "w_pool": g_w_pool, "pool_scale": d_pool_scale, "norm_ffn": d_norm_ffn,
               "norm_ple": d_norm_ple, "norm_final": d_norm_final}
    return grad_x, big, small_g, loss_row, early


def _split2(res, n):
    return res[:n], res[n:]


def _pack_small(small_g, loss_row):
    parts, layout = [], []
    for name in SMALL + ("loss",):
        v = (loss_row if name == "loss" else small_g[name]).reshape(-1, LANES)
        pad = (-v.shape[0]) % 8
        if pad:
            v = jnp.concatenate([v, jnp.zeros((pad, LANES), F32)], axis=0)
        layout.append((name, sum(q.shape[0] for q in parts), v.shape[0]))
        parts.append(v)
    return jnp.concatenate(parts, axis=0), layout


def kernel(x, p, norm_mix, w_in, w_pool, pool_scale, w_branch_a, w_branch_b, w_out, norm_ffn, w_ffn_gate, w_ffn_up, w_ffn_down, norm_ple, w_ple_gate, w_ple_proj, norm_final, loss_target, m_norm_mix, m_w_in, m_w_pool, m_pool_scale, m_w_branch_a, m_w_branch_b, m_w_out, m_norm_ffn, m_w_ffn_gate, m_w_ffn_up, m_w_ffn_down, m_norm_ple, m_w_ple_gate, m_w_ple_proj, m_norm_final, v_norm_mix, v_w_in, v_w_pool, v_pool_scale, v_w_branch_a, v_w_branch_b, v_w_out, v_norm_ffn, v_w_ffn_gate, v_w_ffn_up, v_w_ffn_down, v_norm_ple, v_w_ple_gate, v_w_ple_proj, v_norm_final):
    given = dict(locals())
    names = BIG + SMALL
    order = ("norm_mix", "w_in", "w_pool", "pool_scale", "w_branch_a", "w_branch_b", "w_out", "norm_ffn", "w_ffn_gate",
             "w_ffn_up", "w_ffn_down", "norm_ple", "w_ple_gate", "w_ple_proj", "norm_final")
    t, d = x.shape[1], x.shape[2]
    shard = {n: given[n][0] for n in BIG}
    small = {"norm_mix": norm_mix, "w_pool": w_pool[0], "pool_scale": pool_scale, "norm_ffn": norm_ffn,
             "norm_ple": norm_ple, "norm_final": norm_final}

    as_bf16 = {n: shard[n].astype(BF16) for n in BIG}
    wf = {"w_in": _run_exchange("gather_w_in", _WeightGather([as_bf16["w_in"]]))[0]}

    place = jnp.stack([lax.axis_index("c"), 2 * lax.axis_index("x") + lax.axis_index("y")]).astype(jnp.int32)
    pair_sums = {}

    def exchange_early(ready):
        theirs = _pair_exchange("pair_exchange_early", [ready[n] for n in LATE])
        for n, other in zip(LATE, theirs):
            pair_sums[n] = _pair_sum(f"pair_sum_{n}", place, ready[n], other)
        return _ChipExchange([pair_sums[n] for n in LATE])

    grad_x, big_g, small_g, loss_row, early = _local_step(
        x.reshape(t, d), p.reshape(t, p.shape[-1]), loss_target.reshape(t, d), wf, small,
        gather_late=_WeightGather([as_bf16[n] for n in LATE]), exchange_early=exchange_early)
    landed = dict(zip(LATE, early))
    theirs, = _pair_exchange("pair_exchange_w_in", [big_g["w_in"]])
    pair_sums["w_in"] = _pair_sum("pair_sum_w_in", place, big_g["w_in"], theirs)
    landed["w_in"], = _run_exchange("chip_exchange_w_in", _ChipExchange([pair_sums["w_in"]]))
    halves = [_sum_chips(f"chip_sum_{n}", place, pair_sums[n], landed[n]) for n in BIG]
    grads = dict(zip(BIG, _pair_share(halves)))

    packed, layout = _pack_small(small_g, loss_row)
    reduced = _all_reduce_small(packed)
    for name, start, rows in layout:
        if name == "loss":
            loss = jnp.sum(reduced[start:start + rows])
        else:
            n_el = small[name].size
            grads[name] = reduced[start:start + rows].reshape(-1)[:n_el]

    deltas, new_m, new_v = {}, {}, {}
    for n in order:
        w = shard[n] if n in BIG else small[n]
        shape2 = w.shape if w.ndim == 2 else ((1, w.shape[0]) if w.ndim == 1 else (w.shape[0] * w.shape[1], w.shape[2]))
        g2 = grads[n].reshape(shape2)
        dl, mn, vn = _adamw(f"adamw_{n}", w.reshape(shape2), g2, given["m_" + n].reshape(shape2),
                            given["v_" + n].reshape(shape2))
        full = given[n].shape
        grads[n], deltas[n], new_m[n], new_v[n] = g2.reshape(full), dl.reshape(full), mn.reshape(full), vn.reshape(full)

    return (loss, grad_x.reshape(x.shape), *[grads[n] for n in order], *[deltas[n] for n in order],
            *[new_m[n] for n in order], *[new_v[n] for n in order])
```

```python
import functools
import math

import jax
import jax.numpy as jnp
from jax import lax
from jax.experimental import pallas as pl
from jax.experimental.pallas import tpu as pltpu

F32 = jnp.float32
BF16 = jnp.bfloat16
MESH = pl.DeviceIdType.MESH

RMS_EPS = 1e-6
POOL_WINDOWS = (2, 4, 8, 16)
POOL_HALO = 16
HEAD_DIM = 64
LANES = 128
ATT_BLOCK = 256
ATT_CHUNK = 256
ATT_SLAB = 256
ATT_SCALE = 1.0 / math.sqrt(HEAD_DIM)
ATT_EXIT_BELOW = -104.0
ADAM_LR, ADAM_B1, ADAM_B2, ADAM_EPS, ADAM_WD, ADAM_STEP = 0.001, 0.9, 0.999, 1e-08, 0.01, 10
V7X_VMEM_LIMIT_BYTES = 56 * 1024 * 1024
N_CHIPS = 4
N_DEV = 8


def _params(*semantics):
    return pltpu.CompilerParams(dimension_semantics=semantics, vmem_limit_bytes=V7X_VMEM_LIMIT_BYTES)


def _sigmoid(z):
    return 1.0 / (1.0 + jnp.exp(-z))


def _tiled_spec(shape, tm, tn, n_total):
    rows, width = shape
    if rows == 1:
        if width == n_total:
            return pl.BlockSpec((1, tn), lambda i, j: (0, j))
        return pl.BlockSpec((1, width), lambda i, j: (0, 0))
    if width == n_total:
        return pl.BlockSpec((tm, tn), lambda i, j: (i, j))
    assert tn == n_total, "an operand narrower than the output needs whole output rows per tile"
    return pl.BlockSpec((tm, width), lambda i, j: (i, 0))


def _column_pieces(operands):
    pieces = [tuple(a) if isinstance(a, (tuple, list)) else (a,) for a in operands]
    return [p for ps in pieces for p in ps], [len(ps) for ps in pieces]


def _load_bf16(refs, counts):
    tiles, k = [], 0
    for n in counts:
        parts = [r[...] for r in refs[k:k + n]]
        parts = [t if t.dtype == BF16 else t.astype(BF16) for t in parts]
        tiles.append(parts[0] if n == 1 else jnp.concatenate(parts, axis=1))
        k += n
    return tiles


def _mm(name, a_list, b_list, mode, out_shapes, epilogue=None, extras=(), tm=1024, tn=None, separate=False,
        sum_shapes=()):
    flat_a, counts = _column_pieces(a_list)
    m_total = flat_a[0].shape[0]
    n_total = b_list[0].shape[1] if mode == "nn" else b_list[0].shape[0]
    tn = n_total if tn is None else tn
    tm = min(tm, m_total)
    assert m_total % tm == 0 and n_total % tn == 0 and (not sum_shapes or tn == n_total)
    n_a, n_b, n_extra, n_out = len(counts), len(b_list), len(extras), len(out_shapes)
    assert n_a in (1, n_b)
    dims = (((1,), (0,)), ((), ())) if mode == "nn" else (((1,), (1,)), ((), ()))

    def body(*refs):
        a_refs, b_refs = refs[:len(flat_a)], refs[len(flat_a):len(flat_a) + n_b]
        e_refs = refs[len(flat_a) + n_b:len(flat_a) + n_b + n_extra]
        o_refs = refs[len(flat_a) + n_b + n_extra:]
        lefts = _load_bf16(a_refs, counts)
        products = [lax.dot_general(lefts[s % n_a], b_refs[s][...], dims, preferred_element_type=F32)
                    for s in range(n_b)]
        if not separate:
            products = [functools.reduce(lambda p, r: p + r, products)]
        extra_tiles = [e[...].astype(F32) for e in e_refs]
        outs = products if epilogue is None else epilogue(*products, *extra_tiles)
        for o_ref, o in zip(o_refs[:n_out], outs[:n_out]):
            o_ref[...] = o.astype(o_ref.dtype)
        if sum_shapes:
            @pl.when(pl.program_id(0) == 0)
            def _():
                for s_ref in o_refs[n_out:]:
                    s_ref[...] = jnp.zeros_like(s_ref)

            for s_ref, s in zip(o_refs[n_out:], outs[n_out:]):
                s_ref[...] += s

    once = dict(pipeline_mode=pl.Buffered(1)) if tn == n_total else {}
    in_specs = [pl.BlockSpec((tm, a.shape[1]), lambda i, j: (i, 0)) for a in flat_a]
    if mode == "nn":
        in_specs += [pl.BlockSpec((b.shape[0], tn), lambda i, j: (0, j), **once) for b in b_list]
    else:
        in_specs += [pl.BlockSpec((tn, b.shape[1]), lambda i, j: (j, 0), **once) for b in b_list]
    in_specs += [_tiled_spec(e.shape, tm, tn, n_total) for e in extras]
    out_specs = [_tiled_spec(o.shape, tm, tn, n_total) for o in out_shapes]
    out_specs += [pl.BlockSpec(s.shape, lambda i, j: (0, 0)) for s in sum_shapes]
    semantics = ("arbitrary", "arbitrary") if sum_shapes else ("parallel", "parallel")
    return pl.pallas_call(
        body, name=name, grid=(m_total // tm, n_total // tn), in_specs=in_specs, out_specs=out_specs,
        out_shape=list(out_shapes) + list(sum_shapes), compiler_params=_params(*semantics),
    )(*flat_a, *b_list, *extras)


def _mm_tn(name, a, b_list, tmm=1024, into=None, stacked=False):
    flat_b, counts = _column_pieces(b_list)
    m_total, k = a.shape
    widths = [sum(p.shape[1] for p in flat_b[sum(counts[:s]):sum(counts[:s + 1])]) for s in range(len(counts))]
    tmm = min(tmm, m_total)
    assert m_total % tmm == 0 and not (stacked and into)
    n_b = len(counts)
    passed = [buf for buf, _ in into if buf is not None] if into else []

    def body(*refs):
        a_ref, b_refs = refs[0], refs[1:1 + len(flat_b)]
        o_refs = refs[1 + len(flat_b) + len(passed):]

        @pl.when(pl.program_id(0) == 0)
        def _():
            for o_ref in o_refs:
                o_ref[...] = jnp.zeros_like(o_ref)

        av, = _load_bf16([a_ref], [1])
        for s, bv in enumerate(_load_bf16(b_refs, counts)):
            product = lax.dot_general(av, bv, (((0,), (0,)), ((), ())), preferred_element_type=F32)
            if stacked:
                o_refs[0][s] += product
            elif into:
                o_refs[s][0] += product
            else:
                o_refs[s][...] += product

    in_specs = [pl.BlockSpec((tmm, k), lambda m: (m, 0))]
    in_specs += [pl.BlockSpec((tmm, b.shape[1]), lambda m: (m, 0)) for b in flat_b] + [ANY] * len(passed)
    aliases = {}
    if stacked:
        out_shape = [jax.ShapeDtypeStruct((n_b, k, widths[0]), F32)]
        out_specs = [pl.BlockSpec((n_b, k, widths[0]), lambda m: (0, 0, 0))]
    elif into:
        out_shape = [jax.ShapeDtypeStruct((N_CHIPS, k, w), F32) for w in widths]
        out_specs = [pl.BlockSpec((1, k, w), lambda m, j=j: (j, 0, 0)) for w, (_, j) in zip(widths, into)]
        for s, (buf, _) in enumerate(into):
            if buf is not None:
                aliases[1 + len(flat_b) + len(aliases)] = s
    else:
        out_shape = [jax.ShapeDtypeStruct((k, w), F32) for w in widths]
        out_specs = [pl.BlockSpec((k, w), lambda m: (0, 0)) for w in widths]
    return pl.pallas_call(
        body, name=name, grid=(m_total // tmm,), in_specs=in_specs, out_specs=out_specs, out_shape=out_shape,
        input_output_aliases=aliases, compiler_params=_params("arbitrary"),
    )(a, *flat_b, *passed)


def _rows(name, fn, ins, tile_outs, sum_outs=(), tr=512):
    t_total = max(a.shape[0] for a in ins)
    tr = min(tr, t_total)
    assert t_total % tr == 0
    n_in, n_tile = len(ins), len(tile_outs)

    def body(*refs):
        outs = fn(*[r[...].astype(F32) for r in refs[:n_in]])
        for o_ref, o in zip(refs[n_in:n_in + n_tile], outs[:n_tile]):
            o_ref[...] = o.astype(o_ref.dtype)
        if sum_outs:
            @pl.when(pl.program_id(0) == 0)
            def _():
                for s_ref in refs[n_in + n_tile:]:
                    s_ref[...] = jnp.zeros_like(s_ref)

            for s_ref, s in zip(refs[n_in + n_tile:], outs[n_tile:]):
                s_ref[...] += s

    def spec(shape):
        if shape[0] == 1:
            return pl.BlockSpec(shape, lambda i: (0, 0))
        return pl.BlockSpec((tr, shape[1]), lambda i: (i, 0))

    return pl.pallas_call(
        body, name=name, grid=(t_total // tr,), in_specs=[spec(a.shape) for a in ins],
        out_specs=[spec(o.shape) for o in tile_outs] + [spec(s.shape) for s in sum_outs],
        out_shape=list(tile_outs) + list(sum_outs),
        compiler_params=_params("arbitrary" if sum_outs else "parallel"),
    )(*ins)


def _norm_fwd(name, x, gain):
    def fn(xv, g):
        inv = lax.rsqrt(jnp.mean(xv * xv, axis=-1, keepdims=True) + RMS_EPS)
        return (xv * inv * g,)

    return _rows(name, fn, [x, gain], [jax.ShapeDtypeStruct(x.shape, BF16)])[0]


def _rms_norm_bwd(dh, xv, g):
    inv = lax.rsqrt(jnp.mean(xv * xv, axis=-1, keepdims=True) + RMS_EPS)
    xn = xv * inv
    dxn = dh * g
    return inv * (dxn - xn * jnp.mean(dxn * xn, axis=-1, keepdims=True)), jnp.sum(dh * xn, axis=0, keepdims=True)


def _final_loss(x3, target, gain, gp, pp):
    d = x3.shape[1]

    def fn(xv, tv, g, gv, pv):
        inv = lax.rsqrt(jnp.mean(xv * xv, axis=-1, keepdims=True) + RMS_EPS)
        err = xv * inv * g - tv
        dx, d_gain = _rms_norm_bwd(err * (1.0 / d), xv, g)
        s = _sigmoid(gv)
        return dx, dx * s, dx * pv * s * (1.0 - s), d_gain, (0.5 / d) * jnp.sum(err * err, axis=0, keepdims=True)

    act = jax.ShapeDtypeStruct(x3.shape, BF16)
    return _rows("final_loss", fn, [x3, target, gain, gp, pp], [jax.ShapeDtypeStruct(x3.shape, F32), act, act],
                 [jax.ShapeDtypeStruct((1, d), F32), jax.ShapeDtypeStruct((1, d), F32)])


def _window_counts(t_pos, w):
    return jnp.minimum(t_pos + 1, w).astype(F32)


def _pool_fwd(u, w_pool, scale, tr=512):
    t_total, width = u.shape
    tr = min(tr, t_total)
    n_groups = len(POOL_WINDOWS)
    gdim = width // n_groups
    ext = tr + POOL_HALO

    def body(u_ref, halo_ref, w_ref, s_ref, pooled_ref, ya_ref):
        i = pl.program_id(0)
        halo = jnp.where(i == 0, 0.0, halo_ref[...])
        t_pos = i * tr + lax.broadcasted_iota(jnp.int32, (tr, 1), 0)
        for g, w in enumerate(POOL_WINDOWS):
            cols = slice(g * gdim, (g + 1) * gdim)
            main = u_ref[:, cols]
            win = jnp.concatenate([halo[:, cols], main], axis=0)
            span = 1
            while span < w:
                win = win + pltpu.roll(win, span, 0)
                span *= 2
            pooled = win[POOL_HALO:, :] * (1.0 / _window_counts(t_pos, w)) - main
            pooled_b = pooled.astype(BF16)
            pooled_ref[:, cols] = pooled_b
            mixed = jnp.dot(pooled_b, w_ref[g], preferred_element_type=F32)
            ya_ref[:, cols] = (mixed * s_ref[:, cols]).astype(BF16)

    hb = tr // POOL_HALO
    return pl.pallas_call(
        body, name="pool_fwd", grid=(t_total // tr,),
        in_specs=[pl.BlockSpec((tr, width), lambda i: (i, 0)),
                  pl.BlockSpec((POOL_HALO, width), lambda i: (jnp.maximum(i * hb - 1, 0), 0)),
                  pl.BlockSpec((n_groups, gdim, gdim), lambda i: (0, 0, 0)),
                  pl.BlockSpec((1, width), lambda i: (0, 0))],
        out_specs=[pl.BlockSpec((tr, width), lambda i: (i, 0)), pl.BlockSpec((tr, width), lambda i: (i, 0))],
        out_shape=[jax.ShapeDtypeStruct(u.shape, BF16), jax.ShapeDtypeStruct(u.shape, BF16)],
        compiler_params=_params("parallel"),
    )(u, u, w_pool, scale)


def _pool_bwd(dya, pooled, w_pool, scale, tr=512):
    t_total, width = dya.shape
    tr = min(tr, t_total)
    n_groups = len(POOL_WINDOWS)
    gdim = width // n_groups
    ext = tr + POOL_HALO
    n_tiles = t_total // tr

    def body(d_ref, halo_ref, p_ref, w_ref, s_ref, du_ref, dw_ref, ds_ref):
        i = pl.program_id(0)

        @pl.when(i == 0)
        def _():
            dw_ref[...] = jnp.zeros_like(dw_ref)
            ds_ref[...] = jnp.zeros_like(ds_ref)

        halo = jnp.where(i == n_tiles - 1, 0.0, halo_ref[...])
        t_pos = i * tr + lax.broadcasted_iota(jnp.int32, (ext, 1), 0)
        for g, w in enumerate(POOL_WINDOWS):
            cols = slice(g * gdim, (g + 1) * gdim)
            sc = s_ref[:, cols]
            d_main = d_ref[:, cols]
            pooled_b = p_ref[:, cols]
            mixed = jnp.dot(pooled_b, w_ref[g], preferred_element_type=F32)
            ds_ref[:, cols] += jnp.sum(d_main * mixed, axis=0, keepdims=True)
            dmix = (jnp.concatenate([d_main, halo[:, cols]], axis=0) * sc).astype(BF16)
            dw_ref[g] += lax.dot_general(pooled_b, dmix[:tr, :], (((0,), (0,)), ((), ())),
                                         preferred_element_type=F32)
            dpool = lax.dot_general(dmix, w_ref[g], (((1,), (1,)), ((), ())), preferred_element_type=F32)
            win = dpool * (1.0 / _window_counts(t_pos, w))
            span = 1
            while span < w:
                win = win + pltpu.roll(win, ext - span, 0)
                span *= 2
            du_ref[:, cols] = (win[:tr, :] - dpool[:tr, :]).astype(BF16)

    hb = tr // POOL_HALO
    last_halo = t_total // POOL_HALO - 1
    return pl.pallas_call(
        body, name="pool_bwd", grid=(n_tiles,),
        in_specs=[pl.BlockSpec((tr, width), lambda i: (i, 0)),
                  pl.BlockSpec((POOL_HALO, width), lambda i: (jnp.minimum((i + 1) * hb, last_halo), 0)),
                  pl.BlockSpec((tr, width), lambda i: (i, 0)),
                  pl.BlockSpec((n_groups, gdim, gdim), lambda i: (0, 0, 0)),
                  pl.BlockSpec((1, width), lambda i: (0, 0))],
        out_specs=[pl.BlockSpec((tr, width), lambda i: (i, 0)),
                   pl.BlockSpec((n_groups, gdim, gdim), lambda i: (0, 0, 0)),
                   pl.BlockSpec((1, width), lambda i: (0, 0))],
        out_shape=[jax.ShapeDtypeStruct(dya.shape, BF16), jax.ShapeDtypeStruct((n_groups, gdim, gdim), F32),
                   jax.ShapeDtypeStruct((1, width), F32)],
        compiler_params=_params("arbitrary"),
    )(dya, dya, pooled, w_pool, scale)


def _head_masks():
    lane = lax.broadcasted_iota(jnp.int32, (1, LANES), 1)
    return lane < HEAD_DIM


def _stack_heads(tile, first):
    zero = jnp.zeros_like(tile)
    return jnp.concatenate([jnp.where(first, tile, zero), jnp.where(first, zero, tile)], axis=0)


def _split_bf16(v):
    hi = v.astype(BF16)
    lo = (v - hi.astype(F32)).astype(BF16)
    return hi, lo


def _slab_scores(q, kd, t_pos, k_start):
    z = lax.dot_general(q, kd, (((1,), (1,)), ((), ())), preferred_element_type=F32)
    col = lax.broadcasted_iota(jnp.int32, (1, 2 * ATT_SLAB), 1)
    mask = k_start + (col & (ATT_SLAB - 1)) < t_pos
    e = jnp.exp(-jnp.abs(z))
    log_fail = jnp.where(mask, -(jnp.maximum(z, 0.0) + jnp.log(1.0 + e)), 0.0)
    return z, mask, log_fail, e


def _tri(upper):
    r = lax.broadcasted_iota(jnp.int32, (ATT_CHUNK, ATT_CHUNK), 0)
    c = lax.broadcasted_iota(jnp.int32, (ATT_CHUNK, ATT_CHUNK), 1)
    return jnp.where(r > c if upper else r < c, 1.0, 0.0).astype(BF16)


def _scan_chunk(v, tri):
    hi, lo = _split_bf16(v)
    return (jnp.dot(hi, tri, preferred_element_type=F32) + jnp.dot(lo, tri, preferred_element_type=F32))


def _lane_bcast(col):
    return jnp.broadcast_to(col, (col.shape[0], LANES))


def _scan_slab(v, tri, carries, from_right):
    n_chunks = ATT_SLAB // ATT_CHUNK
    edge = 0 if from_right else ATT_CHUNK - 1
    parts, new_carries = [None] * (2 * n_chunks), []
    for head in range(2):
        run = carries[head]
        for c in (reversed(range(n_chunks)) if from_right else range(n_chunks)):
            lo_col = head * ATT_SLAB + c * ATT_CHUNK
            vc = v[:, lo_col:lo_col + ATT_CHUNK]
            sc = _scan_chunk(vc, tri)
            parts[head * n_chunks + c] = sc + jnp.concatenate([run] * (ATT_CHUNK // LANES), axis=1)
            run = run + _lane_bcast(sc[:, edge:edge + 1] + vc[:, edge:edge + 1])
        new_carries.append(run)
    return jnp.concatenate(parts, axis=1), new_carries


def _fold_heads(stacked, first):
    s = stacked.shape[0] // 2
    return jnp.where(first, stacked[:s], stacked[s:])


class _NoRider:
    operands, out_shapes, scratch = (), (), ()

    def split(self, refs, n_base_in, n_base_out):
        n_in, n_out, n_sem = len(self.operands), len(self.out_shapes), len(self.scratch)
        a = n_base_in + n_in
        b = a + n_base_out + n_out
        mine = (refs[n_base_in:a], refs[a + n_base_out:b], refs[b:b + n_sem])
        return refs[:n_base_in], refs[a:a + n_base_out], refs[b + n_sem:], mine

    def start(self, ins, outs, sems):
        pass

    def relay(self, ins, outs, sems):
        pass

    def finish(self, ins, outs, sems):
        pass

    def at_steps(self, refs, first_step, relay_step, last_step):
        if not self.operands:
            return (lambda: None), (lambda: None)

        def top():
            pl.when(first_step)(lambda: self.start(*refs))
            pl.when(relay_step)(lambda: self.relay(*refs))

        return top, lambda: pl.when(last_step)(lambda: self.finish(*refs))


def _attn_fwd(q_src, q_col, kv_src, k_col, v_col, n_pairs=4, rider=_NoRider()):
    t_total = q_src.shape[0]
    blk = ATT_BLOCK
    n_blocks = t_total // blk
    assert t_total % ATT_SLAB == 0

    def body(*refs):
        (q_ref, k_ref, v_ref), (o_ref,), _, riding = rider.split(refs, 3, 1)
        h, i = pl.program_id(0), pl.program_id(1)
        top, bottom = rider.at_steps(riding, (h == 0) & (i == 0), (h == n_pairs - 1) & (i == 0),
                                     (h == n_pairs - 1) & (i == n_blocks - 1))
        top()
        first = _head_masks()
        q = q_ref[...] * ATT_SCALE
        t_pos = i * blk + lax.broadcasted_iota(jnp.int32, (blk, 1), 0)
        suffix_tri = _tri(upper=True)

        def more(state):
            slab, reach = state[0], state[1]
            return jnp.logical_and(slab >= 0, reach > ATT_EXIT_BELOW)

        def step(state):
            slab, _, acc, right_a, right_b = state
            k_start = pl.multiple_of(slab * ATT_SLAB, ATT_SLAB)
            kd = _stack_heads(k_ref[pl.ds(k_start, ATT_SLAB), :], first)
            vd = _stack_heads(v_ref[pl.ds(k_start, ATT_SLAB), :], first)
            z, mask, log_fail, _ = _slab_scores(q, kd, t_pos, k_start)
            suffix, (right_a, right_b) = _scan_slab(log_fail, suffix_tri, (right_a, right_b), from_right=True)
            a = jnp.exp(jnp.where(mask, z + log_fail + suffix, -1e30)).astype(BF16)
            acc = acc + jnp.dot(a, vd, preferred_element_type=F32)
            return slab - 1, jnp.max(jnp.maximum(right_a, right_b)), acc, right_a, right_b

        zero = jnp.zeros((blk, LANES), F32)
        state = lax.while_loop(more, step, ((i * blk) // ATT_SLAB, jnp.float32(0.0), zero, zero, zero))
        o_ref[...] = state[2].astype(BF16)
        bottom()

    res = pl.pallas_call(
        body, name="attn_fwd", grid=(n_pairs, n_blocks),
        in_specs=[pl.BlockSpec((blk, LANES), lambda h, i: (i, q_col + h)),
                  pl.BlockSpec((t_total, LANES), lambda h, i: (0, k_col + h)),
                  pl.BlockSpec((t_total, LANES), lambda h, i: (0, v_col + h))] + [ANY] * len(rider.operands),
        out_specs=[pl.BlockSpec((blk, LANES), lambda h, i: (i, h))] + [ANY] * len(rider.out_shapes),
        out_shape=[jax.ShapeDtypeStruct((t_total, n_pairs * LANES), BF16)] + list(rider.out_shapes),
        scratch_shapes=list(rider.scratch),
        compiler_params=_params("arbitrary", "arbitrary"),
    )(q_src, kv_src, kv_src, *rider.operands)
    return res[0], res[1:]


def _attn_bwd(q_src, q_col, kv_src, k_col, v_col, dy, n_pairs=4, rider=_NoRider()):
    t_total = q_src.shape[0]
    blk = ATT_BLOCK
    n_blocks = t_total // blk
    n_slabs = t_total // ATT_SLAB
    assert t_total % ATT_SLAB == 0

    def body(*refs):
        (q_ref, dy_ref, k_ref, v_ref), (dq_ref, dk_ref, dv_ref), (g_s, dk_acc, dv_acc), riding = rider.split(refs, 4, 3)
        h, i = pl.program_id(0), pl.program_id(1)
        top, bottom = rider.at_steps(riding, (h == 0) & (i == 0), (h == n_pairs - 1) & (i == 0),
                                     (h == n_pairs - 1) & (i == n_blocks - 1))
        top()

        @pl.when(i == 0)
        def _():
            dk_acc[...] = jnp.zeros_like(dk_acc)
            dv_acc[...] = jnp.zeros_like(dv_acc)

        first = _head_masks()
        q = q_ref[...] * ATT_SCALE
        dy = dy_ref[...]
        t_pos = i * blk + lax.broadcasted_iota(jnp.int32, (blk, 1), 0)
        suffix_tri = _tri(upper=True)
        prefix_tri = _tri(upper=False)
        diag = (i * blk) // ATT_SLAB

        def more(state):
            slab, reach = state[0], state[1]
            return jnp.logical_and(slab >= 0, reach > ATT_EXIT_BELOW)

        def sweep1(state):
            slab, _, right_a, right_b = state
            k_start = pl.multiple_of(slab * ATT_SLAB, ATT_SLAB)
            kd = _stack_heads(k_ref[pl.ds(k_start, ATT_SLAB), :], first)
            vd = _stack_heads(v_ref[pl.ds(k_start, ATT_SLAB), :], first)
            z, mask, log_fail, _ = _slab_scores(q, kd, t_pos, k_start)
            suffix, (right_a, right_b) = _scan_slab(log_fail, suffix_tri, (right_a, right_b), from_right=True)
            a = jnp.exp(jnp.where(mask, z + log_fail + suffix, -1e30))
            da = lax.dot_general(dy, vd, (((1,), (1,)), ((), ())), preferred_element_type=F32)
            g_s[slab] = da * a
            dv_acc[pl.ds(k_start, ATT_SLAB), :] += _fold_heads(lax.dot_general(
                a.astype(BF16), dy, (((0,), (0,)), ((), ())), preferred_element_type=F32), first)
            return slab - 1, jnp.max(jnp.maximum(right_a, right_b)), right_a, right_b

        zero = jnp.zeros((blk, LANES), F32)
        end = lax.while_loop(more, sweep1, (diag, jnp.float32(0.0), zero, zero))[0]

        def sweep2(slab, carry):
            dq, left_a, left_b = carry
            k_start = pl.multiple_of(slab * ATT_SLAB, ATT_SLAB)
            kd = _stack_heads(k_ref[pl.ds(k_start, ATT_SLAB), :], first)
            g = g_s[slab]
            sig = _sigmoid(lax.dot_general(q, kd, (((1,), (1,)), ((), ())), preferred_element_type=F32))
            prefix, (left_a, left_b) = _scan_slab(g, prefix_tri, (left_a, left_b), from_right=False)
            col = lax.broadcasted_iota(jnp.int32, (1, 2 * ATT_SLAB), 1)
            mask = k_start + (col & (ATT_SLAB - 1)) < t_pos
            dz = jnp.where(mask, g * (1.0 - sig) - sig * prefix, 0.0).astype(BF16)
            dq = dq + jnp.dot(dz, kd, preferred_element_type=F32)
            dk_acc[pl.ds(k_start, ATT_SLAB), :] += _fold_heads(lax.dot_general(
                dz, q, (((0,), (0,)), ((), ())), preferred_element_type=F32), first)
            return dq, left_a, left_b

        dq = lax.fori_loop(end + 1, diag + 1, sweep2, (zero, zero, zero))[0]
        dq_ref[...] = (dq * ATT_SCALE).astype(BF16)

        @pl.when(i == n_blocks - 1)
        def _():
            dk_ref[...] = dk_acc[...].astype(BF16)
            dv_ref[...] = dv_acc[...].astype(BF16)

        bottom()

    out = jax.ShapeDtypeStruct((t_total, n_pairs * LANES), BF16)
    res = pl.pallas_call(
        body, name="attn_bwd", grid=(n_pairs, n_blocks),
        in_specs=[pl.BlockSpec((blk, LANES), lambda h, i: (i, q_col + h)),
                  pl.BlockSpec((blk, LANES), lambda h, i: (i, h)),
                  pl.BlockSpec((t_total, LANES), lambda h, i: (0, k_col + h)),
                  pl.BlockSpec((t_total, LANES), lambda h, i: (0, v_col + h))] + [ANY] * len(rider.operands),
        out_specs=[pl.BlockSpec((blk, LANES), lambda h, i: (i, h)),
                   pl.BlockSpec((t_total, LANES), lambda h, i: (0, h)),
                   pl.BlockSpec((t_total, LANES), lambda h, i: (0, h))] + [ANY] * len(rider.out_shapes),
        out_shape=[out, out, out] + list(rider.out_shapes),
        scratch_shapes=list(rider.scratch) + [pltpu.VMEM((n_slabs, blk, 2 * ATT_SLAB), F32),
                                              pltpu.VMEM((t_total, LANES), F32), pltpu.VMEM((t_total, LANES), F32)],
        compiler_params=_params("arbitrary", "arbitrary"),
    )(q_src, dy, kv_src, kv_src, *rider.operands)
    return res[:3], res[3:]


def _adamw(name, w, g, m, v):
    def fn(wv, gv, mv, vv):
        mn = ADAM_B1 * mv + (1.0 - ADAM_B1) * gv
        vn = ADAM_B2 * vv + (1.0 - ADAM_B2) * (gv * gv)
        m_hat = mn / (1.0 - ADAM_B1 ** ADAM_STEP)
        v_hat = vn / (1.0 - ADAM_B2 ** ADAM_STEP)
        return -ADAM_LR * (m_hat / (jnp.sqrt(v_hat) + ADAM_EPS) + ADAM_WD * wv), mn, vn

    rows = w.shape[0]
    tr = _row_tile(rows)
    shp = jax.ShapeDtypeStruct(w.shape, F32)
    if rows == 1:
        def body(w_ref, g_ref, m_ref, v_ref, d_ref, mo_ref, vo_ref):
            d, mn, vn = fn(w_ref[...], g_ref[...], m_ref[...], v_ref[...])
            d_ref[...], mo_ref[...], vo_ref[...] = d, mn, vn

        return pl.pallas_call(body, name=name, out_shape=[shp, shp, shp])(w, g, m, v)
    return _rows(name, fn, [w, g, m, v], [shp, shp, shp], tr=tr)


def _place():
    return lax.axis_index("x"), lax.axis_index("y"), lax.axis_index("c")


def _other_chips(x, y):
    return [(1 - x, y), (x, 1 - y), (1 - x, 1 - y)]


ANY = pl.BlockSpec(memory_space=pl.ANY)


def _remote(src, dst, send_sem, recv_sem, to):
    return pltpu.make_async_remote_copy(src_ref=src, dst_ref=dst, send_sem=send_sem, recv_sem=recv_sem,
                                        device_id=to, device_id_type=MESH)


class _WeightGather(_NoRider):
    def __init__(self, shards):
        n_w = len(shards)
        self.operands = list(shards)
        self.out_shapes = [jax.ShapeDtypeStruct((N_CHIPS,) + s.shape, s.dtype) for s in shards]
        self.scratch = [pltpu.SemaphoreType.DMA((3, n_w))] * 4 + [pltpu.SemaphoreType.DMA((n_w,))] * 2

    def _copies(self, ins, outs, sems):
        send_sems, recv_sems, relay_send, relay_recv, own_send, own_recv = sems
        x, y, c = _place()
        my_chip, sibling = 2 * x + y, (x, y, 1 - c)
        n_w = len(ins)

        def half(w, chip, core):
            h = self.operands[w].shape[0] // 2
            return outs[w].at[chip, pl.ds(core * h, h)]

        own = [_remote(ins[w], outs[w].at[my_chip], own_send.at[w], own_recv.at[w], sibling) for w in range(n_w)]
        sends, landed, relays, relayed = [], [], [], []
        for p, (ox, oy) in enumerate(_other_chips(x, y)):
            for w in range(n_w):
                h = self.operands[w].shape[0] // 2
                sends.append(_remote(ins[w].at[pl.ds(c * h, h)], half(w, my_chip, c), send_sems.at[p, w],
                                     recv_sems.at[p, w], (ox, oy, c)))
                here = half(w, 2 * ox + oy, c)
                landed.append(_remote(here, here, send_sems.at[p, w], recv_sems.at[p, w], (ox, oy, c)))
                relays.append(_remote(here, here, relay_send.at[p, w], relay_recv.at[p, w], sibling))
                there = half(w, 2 * ox + oy, 1 - c)
                relayed.append(_remote(there, there, relay_send.at[p, w], relay_recv.at[p, w], sibling))
        return own, sends, landed, relays, relayed

    def start(self, ins, outs, sems):
        own, sends, _, _, _ = self._copies(ins, outs, sems)
        for cp in own + sends:
            cp.start()

    def relay(self, ins, outs, sems):
        _, _, landed, relays, _ = self._copies(ins, outs, sems)
        for arrival, cp in zip(landed, relays):
            arrival.wait_recv()
            cp.start()

    def finish(self, ins, outs, sems):
        own, sends, _, relays, relayed = self._copies(ins, outs, sems)
        for arrival in relayed:
            arrival.wait_recv()
        for cp in sends + relays:
            cp.wait_send()
        for cp in own:
            cp.wait()


class _ChipExchange(_NoRider):
    def __init__(self, pair_sums):
        n_w = len(pair_sums)
        self.operands = list(pair_sums)
        self.out_shapes = [jax.ShapeDtypeStruct((3,) + s.shape[1:], s.dtype) for s in pair_sums]
        self.scratch = [pltpu.SemaphoreType.DMA((3, n_w))] * 2

    def _copies(self, ins, outs, sems):
        send_sems, recv_sems = sems
        x, y, c = _place()
        return [_remote(ins[w].at[2 * ox + oy], outs[w].at[p], send_sems.at[p, w], recv_sems.at[p, w], (ox, oy, c))
                for p, (ox, oy) in enumerate(_other_chips(x, y)) for w in range(len(ins))]

    def start(self, ins, outs, sems):
        for cp in self._copies(ins, outs, sems):
            cp.start()

    def finish(self, ins, outs, sems):
        for cp in self._copies(ins, outs, sems):
            cp.wait()


def _run_exchange(name, plan):
    n_in, n_out = len(plan.operands), len(plan.out_shapes)

    def body(*refs):
        parts = (refs[:n_in], refs[n_in:n_in + n_out], refs[n_in + n_out:])
        plan.start(*parts)
        plan.relay(*parts)
        plan.finish(*parts)

    return pl.pallas_call(body, name=name, in_specs=[ANY] * n_in, out_specs=[ANY] * n_out,
                          out_shape=list(plan.out_shapes), scratch_shapes=list(plan.scratch))(*plan.operands)


def _pair_exchange(name, grads):
    n_w = len(grads)

    def halves(w):
        return grads[w].shape[1] // 2

    def body(*refs):
        ins, theirs = refs[:n_w], refs[n_w:2 * n_w]
        send_sems, recv_sems = refs[2 * n_w:]
        x, y, c = _place()
        sends = [pltpu.make_async_remote_copy(
            src_ref=ins[w].at[:, pl.ds((1 - c) * halves(w), halves(w)), :], dst_ref=theirs[w],
            send_sem=send_sems.at[w], recv_sem=recv_sems.at[w], device_id=(x, y, 1 - c), device_id_type=MESH)
            for w in range(n_w)]
        for cp in sends:
            cp.start()
        for cp in sends:
            cp.wait()

    return pl.pallas_call(
        body, name=name, in_specs=[ANY] * n_w, out_specs=[ANY] * n_w,
        out_shape=[jax.ShapeDtypeStruct((N_CHIPS, halves(w), grads[w].shape[2]), F32) for w in range(n_w)],
        scratch_shapes=[pltpu.SemaphoreType.DMA((n_w,)), pltpu.SemaphoreType.DMA((n_w,))],
    )(*grads)


def _pair_share(shards):
    n_w = len(shards)

    def body(*refs):
        ins, outs = refs[:n_w], refs[n_w:2 * n_w]
        send_sems, recv_sems = refs[2 * n_w:]
        x, y, c = _place()
        sends = []
        for w in range(n_w):
            h = shards[w].shape[0] // 2
            mine = outs[w].at[pl.ds(c * h, h)]
            sends.append(pltpu.make_async_remote_copy(
                src_ref=mine, dst_ref=mine, send_sem=send_sems.at[w], recv_sem=recv_sems.at[w],
                device_id=(x, y, 1 - c), device_id_type=MESH))
        for cp in sends:
            cp.start()
        for w in range(n_w):
            h = shards[w].shape[0] // 2
            theirs = outs[w].at[pl.ds((1 - c) * h, h)]
            pltpu.make_async_remote_copy(
                src_ref=theirs, dst_ref=theirs, send_sem=send_sems.at[w], recv_sem=recv_sems.at[w],
                device_id=(x, y, 1 - c), device_id_type=MESH).wait_recv()
        for cp in sends:
            cp.wait_send()

    return pl.pallas_call(
        body, name="pair_share", in_specs=[ANY] * n_w, out_specs=[ANY] * n_w,
        out_shape=[jax.ShapeDtypeStruct(s.shape, s.dtype) for s in shards],
        input_output_aliases={w: w for w in range(n_w)},
        scratch_shapes=[pltpu.SemaphoreType.DMA((n_w,)), pltpu.SemaphoreType.DMA((n_w,))],
    )(*shards)


def _all_reduce_small(vec):
    rows = vec.shape[0]

    def body(v_ref, o_ref, slots, send_sems, recv_sems):
        x, y, c = _place()
        me = 4 * x + 2 * y + c
        slots[me] = v_ref[...]
        sends = []
        for k in range(1, N_DEV):
            peer = (x ^ (k >> 2), y ^ ((k >> 1) & 1), c ^ (k & 1))
            sends.append(pltpu.make_async_remote_copy(
                src_ref=v_ref, dst_ref=slots.at[me], send_sem=send_sems.at[k - 1], recv_sem=recv_sems.at[k - 1],
                device_id=peer, device_id_type=MESH))
        for cp in sends:
            cp.start()
        for k in range(1, N_DEV):
            px, py, pc = x ^ (k >> 2), y ^ ((k >> 1) & 1), c ^ (k & 1)
            landed = slots.at[4 * px + 2 * py + pc]
            pltpu.make_async_remote_copy(
                src_ref=landed, dst_ref=landed, send_sem=send_sems.at[k - 1], recv_sem=recv_sems.at[k - 1],
                device_id=(px, py, pc), device_id_type=MESH).wait_recv()
        for cp in sends:
            cp.wait_send()
        total = slots[0]
        for d in range(1, N_DEV):
            total = total + slots[d]
        o_ref[...] = total

    vm = pl.BlockSpec(memory_space=pltpu.VMEM)
    return pl.pallas_call(
        body, name="all_reduce_small", in_specs=[vm], out_specs=vm, out_shape=jax.ShapeDtypeStruct(vec.shape, F32),
        scratch_shapes=[pltpu.VMEM((N_DEV, rows, LANES), F32), pltpu.SemaphoreType.DMA((N_DEV - 1,)),
                        pltpu.SemaphoreType.DMA((N_DEV - 1,))],
    )(vec)


def _row_tile(rows):
    for tr in (256, 128, 64, 32, 16):
        if rows % tr == 0:
            return tr
    return rows


def _pair_sum(name, place, grad, theirs):
    n, r, c = grad.shape
    half = r // 2
    tr = _row_tile(half)
    nb = half // tr

    def body(place_ref, g_ref, t_ref, o_ref):
        o_ref[...] = (g_ref[...] + t_ref[...]).astype(BF16)

    return pl.pallas_call(
        body, name=name, out_shape=jax.ShapeDtypeStruct((n, half, c), BF16),
        grid_spec=pltpu.PrefetchScalarGridSpec(
            num_scalar_prefetch=1, grid=(n, nb),
            in_specs=[pl.BlockSpec((1, tr, c), lambda j, i, pr: (j, pr[0] * nb + i, 0)),
                      pl.BlockSpec((1, tr, c), lambda j, i, pr: (j, i, 0))],
            out_specs=pl.BlockSpec((1, tr, c), lambda j, i, pr: (j, i, 0))),
        compiler_params=_params("parallel", "parallel"),
    )(place, grad, theirs)


def _sum_chips(name, place, pair_sums, landed):
    _, half, c = pair_sums.shape
    tr = _row_tile(half)
    nb = half // tr

    def body(place_ref, s_ref, q_ref, o_ref):
        total = s_ref[0].astype(F32)
        for p in range(3):
            total = total + q_ref[p].astype(F32)
        o_ref[...] = total

    return pl.pallas_call(
        body, name=name, out_shape=jax.ShapeDtypeStruct((2 * half, c), F32),
        grid_spec=pltpu.PrefetchScalarGridSpec(
            num_scalar_prefetch=1, grid=(nb,),
            in_specs=[pl.BlockSpec((1, tr, c), lambda i, pr: (pr[1], i, 0)),
                      pl.BlockSpec((3, tr, c), lambda i, pr: (0, i, 0))],
            out_specs=pl.BlockSpec((tr, c), lambda i, pr: (pr[0] * nb + i, 0))),
        compiler_params=_params("parallel"),
    )(place, pair_sums, landed)


BIG = ("w_in", "w_branch_a", "w_branch_b", "w_out", "w_ffn_gate", "w_ffn_up", "w_ffn_down", "w_ple_gate", "w_ple_proj")
LATE = BIG[1:]
COLUMN_SHARDED = ("w_in", "w_branch_a", "w_branch_b", "w_ffn_gate", "w_ffn_up", "w_ple_proj")
SMALL = ("norm_mix", "w_pool", "pool_scale", "norm_ffn", "norm_ple", "norm_final")


def _join_columns(w4):
    return jnp.concatenate([w4[j] for j in range(N_CHIPS)], axis=1)


def _split_columns(g):
    k, n = g.shape
    return g.reshape(k, N_CHIPS, n // N_CHIPS).transpose(1, 0, 2)


def _sds(shape, dtype):
    return jax.ShapeDtypeStruct(shape, dtype)


def _local_step(x, p, target, wf, small, gather_late=None, exchange_early=None):
    t, d = x.shape
    w_in = wf["w_in"]
    w_pool_b = small["w_pool"].astype(BF16)
    dp = w_pool_b.shape[0] * w_pool_b.shape[1]

    h1 = _norm_fwd("norm_mix", x, small["norm_mix"])
    u, q = _mm("proj_uq", [h1], [w_in[0]], "nn", [_sds((t, dp), F32), _sds((t, dp), BF16)],
               epilogue=lambda acc: (acc[:, :dp], acc[:, dp:]))
    kv, = _mm("proj_kv", [h1], [w_in[1]], "nn", [_sds((t, d), BF16)])
    ga, = _mm("proj_ga", [h1], [w_in[2]], "nn", [_sds((t, d), BF16)])
    gb, = _mm("proj_gb", [h1], [w_in[3]], "nn", [_sds((t, d), BF16)])
    pooled, ya = _pool_fwd(u, w_pool_b, small["pool_scale"])
    n_pairs = dp // LANES
    yb, late = _attn_fwd(q, 0, kv, 0, n_pairs, n_pairs, rider=gather_late or _NoRider())
    wf = {**wf, **dict(zip(LATE, late))}
    w_gate, w_up, w_down = wf["w_ffn_gate"], wf["w_ffn_up"], wf["w_ffn_down"]
    w_a, w_b, w_pp = _join_columns(wf["w_branch_a"]), _join_columns(wf["w_branch_b"]), _join_columns(wf["w_ple_proj"])
    w_out = wf["w_out"].reshape(d, d)
    w_pg = wf["w_ple_gate"].reshape(d, d)
    dff = w_gate.shape[2]
    ta, tb, merged = _mm(
        "branches_merge", [ya, yb], [w_a, w_b], "nn", [_sds((t, d), BF16)] * 3, extras=[ga, gb], separate=True,
        epilogue=lambda tav, tbv, gav, gbv: (tav, tbv, _sigmoid(gav) * tav + _sigmoid(gbv) * tbv), tm=512)
    def residual_norm(acc, xv, g):
        xn = acc + xv
        return xn, xn * lax.rsqrt(jnp.mean(xn * xn, axis=-1, keepdims=True) + RMS_EPS) * g

    x1, h2 = _mm("mix_out", [merged], [w_out], "nn", [_sds((t, d), F32), _sds((t, d), BF16)],
                 extras=[x, small["norm_ffn"]], epilogue=residual_norm)
    gates, ups, acts = [], [], []
    for j in range(N_CHIPS):
        gj, uj, aj = _mm(f"ffn_gate_up{j}", [h2], [w_gate[j], w_up[j]], "nn", [_sds((t, dff), BF16)] * 3,
                         separate=True, epilogue=lambda gv, uv: (gv, uv, gv * _sigmoid(gv) * uv))
        gates.append(gj), ups.append(uj), acts.append(aj)
    x2, h3 = _mm("ffn_down", acts, [w_down[j] for j in range(N_CHIPS)], "nn", [_sds((t, d), F32), _sds((t, d), BF16)],
                 extras=[x1, small["norm_ple"]], epilogue=residual_norm, tm=512)
    gp, pp, x3 = _mm(
        "ple", [h3, p], [w_pg, w_pp], "nn", [_sds((t, d), BF16), _sds((t, d), BF16), _sds((t, d), F32)], extras=[x2],
        separate=True, epilogue=lambda gv, pv, xv: (gv, pv, xv + _sigmoid(gv) * pv), tm=512)
    (dx3, d_pp, d_gp), (d_norm_final, loss_row) = _split2(
        _final_loss(x3, target, small["norm_final"].reshape(1, d), gp, pp), 3)

    def through_norm(dh, xv, g, dres):
        dx, d_gain = _rms_norm_bwd(dh, xv, g)
        return dx + dres, dx + dres, d_gain

    stream = [_sds((t, d), F32), _sds((t, d), BF16)]
    gain_sum = [_sds((1, d), F32)]
    g_w_pp, = _mm_tn("g_ple_proj", p, [d_pp])
    g_w_pg, = _mm_tn("g_ple_gate", h3, [d_gp])
    dx2, dx2_b, d_norm_ple = _mm("d_h3", [d_gp], [w_pg], "nt", stream, extras=[x2, small["norm_ple"], dx3],
                                 epilogue=through_norm, sum_shapes=gain_sum, tm=512)

    def ffn_bwd(acc, gv, uv):
        s = _sigmoid(gv)
        return acc * uv * (s * (1.0 + gv * (1.0 - s))), acc * (gv * s)

    d_gates, d_ups, g_w_gate, g_w_up, g_w_down = [], [], None, None, None
    for j in range(N_CHIPS):
        dgj, duj = _mm(f"d_act{j}", [dx2_b], [w_down[j]], "nt", [_sds((t, dff), BF16), _sds((t, dff), BF16)],
                       extras=[gates[j], ups[j]], epilogue=ffn_bwd)
        d_gates.append(dgj), d_ups.append(duj)
        g_w_down, = _mm_tn(f"g_ffn_down{j}", acts[j], [dx2_b], into=[(g_w_down, j)])
        g_w_gate, g_w_up = _mm_tn(f"g_ffn_gate_up{j}", h2, [dgj, duj], into=[(g_w_gate, j), (g_w_up, j)])
    dx1, dx1_b, d_norm_ffn = _mm(
        "d_h2", d_gates + d_ups, [w_gate[j] for j in range(N_CHIPS)] + [w_up[j] for j in range(N_CHIPS)], "nt",
        stream, extras=[x1, small["norm_ffn"], dx2], epilogue=through_norm, sum_shapes=gain_sum, tm=512)

    def merge_bwd(acc, tav, tbv, gav, gbv):
        sa, sb = _sigmoid(gav), _sigmoid(gbv)
        return acc * sa, acc * sb, acc * tav * sa * (1.0 - sa), acc * tbv * sb * (1.0 - sb)

    d_ta, d_tb, d_ga, d_gb = _mm("d_merged", [dx1_b], [w_out], "nt", [_sds((t, d), BF16)] * 4,
                                 extras=[ta, tb, ga, gb], epilogue=merge_bwd, tm=512)
    g_w_out, = _mm_tn("g_w_out", merged, [dx1_b])
    g_w_a, = _mm_tn("g_branch_a", ya, [d_ta])
    g_w_b, = _mm_tn("g_branch_b", yb, [d_tb])
    d_ya, = _mm("d_ya", [d_ta], [w_a], "nt", [_sds((t, dp), F32)])
    d_yb, = _mm("d_yb", [d_tb], [w_b], "nt", [_sds((t, dp), BF16)])
    d_u, g_w_pool, d_pool_scale = _pool_bwd(d_ya, pooled, w_pool_b, small["pool_scale"])
    big = {
        "w_branch_a": _split_columns(g_w_a), "w_branch_b": _split_columns(g_w_b),
        "w_out": g_w_out.reshape(wf["w_out"].shape), "w_ffn_gate": g_w_gate, "w_ffn_up": g_w_up,
        "w_ffn_down": g_w_down, "w_ple_gate": g_w_pg.reshape(wf["w_ple_gate"].shape),
        "w_ple_proj": _split_columns(g_w_pp),
    }
    rider = exchange_early(big) if exchange_early else _NoRider()
    (d_q, d_k, d_v), early = _attn_bwd(q, 0, kv, 0, n_pairs, d_yb, n_pairs, rider=rider)
    d_proj = [(d_u, d_q), (d_k, d_v), d_ga, d_gb]
    big["w_in"], = _mm_tn("g_w_in", h1, d_proj, tmm=512, stacked=True)
    grad_x, d_norm_mix = _mm(
        "d_h1", d_proj, [w_in[j] for j in range(N_CHIPS)], "nt", [_sds((t, d), F32)],
        extras=[x, small["norm_mix"], dx1], epilogue=lambda dh, xv, g, dres: through_norm(dh, xv, g, dres)[1:],
        sum_shapes=gain_sum, tm=512)
    small_g = {"norm_mix": d_norm_mix, "w_pool": g_w_pool, "pool_scale": d_pool_scale, "norm_ffn": d_norm_ffn,
               "norm_ple": d_norm_ple, "norm_final": d_norm_final}
    return grad_x, big, small_g, loss_row, early


def _split2(res, n):
    return res[:n], res[n:]


def _pack_small(small_g, loss_row):
    parts, layout = [], []
    for name in SMALL + ("loss",):
        v = (loss_row if name == "loss" else small_g[name]).reshape(-1, LANES)
        pad = (-v.shape[0]) % 8
        if pad:
            v = jnp.concatenate([v, jnp.zeros((pad, LANES), F32)], axis=0)
        layout.append((name, sum(q.shape[0] for q in parts), v.shape[0]))
        parts.append(v)
    return jnp.concatenate(parts, axis=0), layout


def kernel(x, p, norm_mix, w_in, w_pool, pool_scale, w_branch_a, w_branch_b, w_out, norm_ffn, w_ffn_gate, w_ffn_up, w_ffn_down, norm_ple, w_ple_gate, w_ple_proj, norm_final, loss_target, m_norm_mix, m_w_in, m_w_pool, m_pool_scale, m_w_branch_a, m_w_branch_b, m_w_out, m_norm_ffn, m_w_ffn_gate, m_w_ffn_up, m_w_ffn_down, m_norm_ple, m_w_ple_gate, m_w_ple_proj, m_norm_final, v_norm_mix, v_w_in, v_w_pool, v_pool_scale, v_w_branch_a, v_w_branch_b, v_w_out, v_norm_ffn, v_w_ffn_gate, v_w_ffn_up, v_w_ffn_down, v_norm_ple, v_w_ple_gate, v_w_ple_proj, v_norm_final):
    given = dict(locals())
    names = BIG + SMALL
    order = ("norm_mix", "w_in", "w_pool", "pool_scale", "w_branch_a", "w_branch_b", "w_out", "norm_ffn", "w_ffn_gate",
             "w_ffn_up", "w_ffn_down", "norm_ple", "w_ple_gate", "w_ple_proj", "norm_final")
    t, d = x.shape[1], x.shape[2]
    shard = {n: given[n][0] for n in BIG}
    small = {"norm_mix": norm_mix, "w_pool": w_pool[0], "pool_scale": pool_scale, "norm_ffn": norm_ffn,
             "norm_ple": norm_ple, "norm_final": norm_final}

    as_bf16 = {n: shard[n].astype(BF16) for n in BIG}
    wf = {"w_in": _run_exchange("gather_w_in", _WeightGather([as_bf16["w_in"]]))[0]}

    place = jnp.stack([lax.axis_index("c"), 2 * lax.axis_index("x") + lax.axis_index("y")]).astype(jnp.int32)
    pair_sums = {}

    def exchange_early(ready):
        theirs = _pair_exchange("pair_exchange_early", [ready[n] for n in LATE])
        for n, other in zip(LATE, theirs):
            pair_sums[n] = _pair_sum(f"pair_sum_{n}", place, ready[n], other)
        return _ChipExchange([pair_sums[n] for n in LATE])

    grad_x, big_g, small_g, loss_row, early = _local_step(
        x.reshape(t, d), p.reshape(t, p.shape[-1]), loss_target.reshape(t, d), wf, small,
        gather_late=_WeightGather([as_bf16[n] for n in LATE]), exchange_early=exchange_early)
    landed = dict(zip(LATE, early))
    theirs, = _pair_exchange("pair_exchange_w_in", [big_g["w_in"]])
    pair_sums["w_in"] = _pair_sum("pair_sum_w_in", place, big_g["w_in"], theirs)
    landed["w_in"], = _run_exchange("chip_exchange_w_in", _ChipExchange([pair_sums["w_in"]]))
    halves = [_sum_chips(f"chip_sum_{n}", place, pair_sums[n], landed[n]) for n in BIG]
    grads = dict(zip(BIG, _pair_share(halves)))

    packed, layout = _pack_small(small_g, loss_row)
    reduced = _all_reduce_small(packed)
    for name, start, rows in layout:
        if name == "loss":
            loss = jnp.sum(reduced[start:start + rows])
        else:
            n_el = small[name].size
            grads[name] = reduced[start:start + rows].reshape(-1)[:n_el]

    deltas, new_m, new_v = {}, {}, {}
    for n in order:
        w = shard[n] if n in BIG else small[n]
        shape2 = w.shape if w.ndim == 2 else ((1, w.shape[0]) if w.ndim == 1 else (w.shape[0] * w.shape[1], w.shape[2]))
        g2 = grads[n].reshape(shape2)
        dl, mn, vn = _adamw(f"adamw_{n}", w.reshape(shape2), g2, given["m_" + n].reshape(shape2),
                            given["v_" + n].reshape(shape2))
        full = given[n].shape
        grads[n], deltas[n], new_m[n], new_v[n] = g2.reshape(full), dl.reshape(full), mn.reshape(full), vn.reshape(full)

    return (loss, grad_x.reshape(x.shape), *[grads[n] for n in order], *[deltas[n] for n in order],
            *[new_m[n] for n in order], *[new_v[n] for n in order])
```

```python
import functools
import math

import jax
import jax.numpy as jnp
from jax import lax
from jax.experimental import pallas as pl
from jax.experimental.pallas import tpu as pltpu

F32 = jnp.float32
BF16 = jnp.bfloat16
MESH = pl.DeviceIdType.MESH

RMS_EPS = 1e-6
POOL_WINDOWS = (2, 4, 8, 16)
POOL_HALO = 16
HEAD_DIM = 64
LANES = 128
ATT_BLOCK = 256
ATT_CHUNK = 256
ATT_SLAB = 256
ATT_SCALE = 1.0 / math.sqrt(HEAD_DIM)
LOG2_E = 1.4426950408889634
ATT_EXIT_BELOW = -150.5
ADAM_LR, ADAM_B1, ADAM_B2, ADAM_EPS, ADAM_WD, ADAM_STEP = 0.001, 0.9, 0.999, 1e-08, 0.01, 10
V7X_VMEM_LIMIT_BYTES = 56 * 1024 * 1024
N_CHIPS = 4
N_DEV = 8


def _params(*semantics):
    return pltpu.CompilerParams(dimension_semantics=semantics, vmem_limit_bytes=V7X_VMEM_LIMIT_BYTES)


def _sigmoid(z):
    return 1.0 / (1.0 + jnp.exp(-z))


def _tiled_spec(shape, tm, tn, n_total):
    rows, width = shape
    if rows == 1:
        if width == n_total:
            return pl.BlockSpec((1, tn), lambda i, j: (0, j))
        return pl.BlockSpec((1, width), lambda i, j: (0, 0))
    if width == n_total:
        return pl.BlockSpec((tm, tn), lambda i, j: (i, j))
    assert tn == n_total, "an operand narrower than the output needs whole output rows per tile"
    return pl.BlockSpec((tm, width), lambda i, j: (i, 0))


def _column_pieces(operands):
    pieces = [tuple(a) if isinstance(a, (tuple, list)) else (a,) for a in operands]
    return [p for ps in pieces for p in ps], [len(ps) for ps in pieces]


def _load_bf16(refs, counts):
    tiles, k = [], 0
    for n in counts:
        parts = [r[...] for r in refs[k:k + n]]
        parts = [t if t.dtype == BF16 else t.astype(BF16) for t in parts]
        tiles.append(parts[0] if n == 1 else jnp.concatenate(parts, axis=1))
        k += n
    return tiles


def _mm(name, a_list, b_list, mode, out_shapes, epilogue=None, extras=(), tm=1024, tn=None, separate=False,
        sum_shapes=()):
    flat_a, counts = _column_pieces(a_list)
    m_total = flat_a[0].shape[0]
    n_total = b_list[0].shape[1] if mode == "nn" else b_list[0].shape[0]
    tn = n_total if tn is None else tn
    tm = min(tm, m_total)
    assert m_total % tm == 0 and n_total % tn == 0 and (not sum_shapes or tn == n_total)
    n_a, n_b, n_extra, n_out = len(counts), len(b_list), len(extras), len(out_shapes)
    assert n_a in (1, n_b)
    dims = (((1,), (0,)), ((), ())) if mode == "nn" else (((1,), (1,)), ((), ()))

    def body(*refs):
        a_refs, b_refs = refs[:len(flat_a)], refs[len(flat_a):len(flat_a) + n_b]
        e_refs = refs[len(flat_a) + n_b:len(flat_a) + n_b + n_extra]
        o_refs = refs[len(flat_a) + n_b + n_extra:]
        lefts = _load_bf16(a_refs, counts)
        products = [lax.dot_general(lefts[s % n_a], b_refs[s][...], dims, preferred_element_type=F32)
                    for s in range(n_b)]
        if not separate:
            products = [functools.reduce(lambda p, r: p + r, products)]
        extra_tiles = [e[...].astype(F32) for e in e_refs]
        outs = products if epilogue is None else epilogue(*products, *extra_tiles)
        for o_ref, o in zip(o_refs[:n_out], outs[:n_out]):
            o_ref[...] = o.astype(o_ref.dtype)
        if sum_shapes:
            @pl.when(pl.program_id(0) == 0)
            def _():
                for s_ref in o_refs[n_out:]:
                    s_ref[...] = jnp.zeros_like(s_ref)

            for s_ref, s in zip(o_refs[n_out:], outs[n_out:]):
                s_ref[...] += s

    once = dict(pipeline_mode=pl.Buffered(1)) if tn == n_total else {}
    in_specs = [pl.BlockSpec((tm, a.shape[1]), lambda i, j: (i, 0)) for a in flat_a]
    if mode == "nn":
        in_specs += [pl.BlockSpec((b.shape[0], tn), lambda i, j: (0, j), **once) for b in b_list]
    else:
        in_specs += [pl.BlockSpec((tn, b.shape[1]), lambda i, j: (j, 0), **once) for b in b_list]
    in_specs += [_tiled_spec(e.shape, tm, tn, n_total) for e in extras]
    out_specs = [_tiled_spec(o.shape, tm, tn, n_total) for o in out_shapes]
    out_specs += [pl.BlockSpec(s.shape, lambda i, j: (0, 0)) for s in sum_shapes]
    semantics = ("arbitrary", "arbitrary") if sum_shapes else ("parallel", "parallel")
    return pl.pallas_call(
        body, name=name, grid=(m_total // tm, n_total // tn), in_specs=in_specs, out_specs=out_specs,
        out_shape=list(out_shapes) + list(sum_shapes), compiler_params=_params(*semantics),
    )(*flat_a, *b_list, *extras)


def _mm_tn(name, a, b_list, tmm=1024, into=None, stacked=False):
    flat_b, counts = _column_pieces(b_list)
    m_total, k = a.shape
    widths = [sum(p.shape[1] for p in flat_b[sum(counts[:s]):sum(counts[:s + 1])]) for s in range(len(counts))]
    tmm = min(tmm, m_total)
    assert m_total % tmm == 0 and not (stacked and into)
    n_b = len(counts)
    passed = [buf for buf, _ in into if buf is not None] if into else []

    def body(*refs):
        a_ref, b_refs = refs[0], refs[1:1 + len(flat_b)]
        o_refs = refs[1 + len(flat_b) + len(passed):]

        @pl.when(pl.program_id(0) == 0)
        def _():
            for o_ref in o_refs:
                o_ref[...] = jnp.zeros_like(o_ref)

        av, = _load_bf16([a_ref], [1])
        for s, bv in enumerate(_load_bf16(b_refs, counts)):
            product = lax.dot_general(av, bv, (((0,), (0,)), ((), ())), preferred_element_type=F32)
            if stacked:
                o_refs[0][s] += product
            elif into:
                o_refs[s][0] += product
            else:
                o_refs[s][...] += product

    in_specs = [pl.BlockSpec((tmm, k), lambda m: (m, 0))]
    in_specs += [pl.BlockSpec((tmm, b.shape[1]), lambda m: (m, 0)) for b in flat_b] + [ANY] * len(passed)
    aliases = {}
    if stacked:
        out_shape = [jax.ShapeDtypeStruct((n_b, k, widths[0]), F32)]
        out_specs = [pl.BlockSpec((n_b, k, widths[0]), lambda m: (0, 0, 0))]
    elif into:
        out_shape = [jax.ShapeDtypeStruct((N_CHIPS, k, w), F32) for w in widths]
        out_specs = [pl.BlockSpec((1, k, w), lambda m, j=j: (j, 0, 0)) for w, (_, j) in zip(widths, into)]
        for s, (buf, _) in enumerate(into):
            if buf is not None:
                aliases[1 + len(flat_b) + len(aliases)] = s
    else:
        out_shape = [jax.ShapeDtypeStruct((k, w), F32) for w in widths]
        out_specs = [pl.BlockSpec((k, w), lambda m: (0, 0)) for w in widths]
    return pl.pallas_call(
        body, name=name, grid=(m_total // tmm,), in_specs=in_specs, out_specs=out_specs, out_shape=out_shape,
        input_output_aliases=aliases, compiler_params=_params("arbitrary"),
    )(a, *flat_b, *passed)


def _rows(name, fn, ins, tile_outs, sum_outs=(), tr=512):
    t_total = max(a.shape[0] for a in ins)
    tr = min(tr, t_total)
    assert t_total % tr == 0
    n_in, n_tile = len(ins), len(tile_outs)

    def body(*refs):
        outs = fn(*[r[...].astype(F32) for r in refs[:n_in]])
        for o_ref, o in zip(refs[n_in:n_in + n_tile], outs[:n_tile]):
            o_ref[...] = o.astype(o_ref.dtype)
        if sum_outs:
            @pl.when(pl.program_id(0) == 0)
            def _():
                for s_ref in refs[n_in + n_tile:]:
                    s_ref[...] = jnp.zeros_like(s_ref)

            for s_ref, s in zip(refs[n_in + n_tile:], outs[n_tile:]):
                s_ref[...] += s

    def spec(shape):
        if shape[0] == 1:
            return pl.BlockSpec(shape, lambda i: (0, 0))
        return pl.BlockSpec((tr, shape[1]), lambda i: (i, 0))

    return pl.pallas_call(
        body, name=name, grid=(t_total // tr,), in_specs=[spec(a.shape) for a in ins],
        out_specs=[spec(o.shape) for o in tile_outs] + [spec(s.shape) for s in sum_outs],
        out_shape=list(tile_outs) + list(sum_outs),
        compiler_params=_params("arbitrary" if sum_outs else "parallel"),
    )(*ins)


def _norm_fwd(name, x, gain):
    def fn(xv, g):
        inv = lax.rsqrt(jnp.mean(xv * xv, axis=-1, keepdims=True) + RMS_EPS)
        return (xv * inv * g,)

    return _rows(name, fn, [x, gain], [jax.ShapeDtypeStruct(x.shape, BF16)])[0]


def _rms_norm_bwd(dh, xv, g):
    inv = lax.rsqrt(jnp.mean(xv * xv, axis=-1, keepdims=True) + RMS_EPS)
    xn = xv * inv
    dxn = dh * g
    return inv * (dxn - xn * jnp.mean(dxn * xn, axis=-1, keepdims=True)), jnp.sum(dh * xn, axis=0, keepdims=True)


def _final_loss(x3, target, gain, gp, pp):
    d = x3.shape[1]

    def fn(xv, tv, g, gv, pv):
        inv = lax.rsqrt(jnp.mean(xv * xv, axis=-1, keepdims=True) + RMS_EPS)
        err = xv * inv * g - tv
        dx, d_gain = _rms_norm_bwd(err * (1.0 / d), xv, g)
        s = _sigmoid(gv)
        return dx, dx * s, dx * pv * s * (1.0 - s), d_gain, (0.5 / d) * jnp.sum(err * err, axis=0, keepdims=True)

    act = jax.ShapeDtypeStruct(x3.shape, BF16)
    return _rows("final_loss", fn, [x3, target, gain, gp, pp], [jax.ShapeDtypeStruct(x3.shape, F32), act, act],
                 [jax.ShapeDtypeStruct((1, d), F32), jax.ShapeDtypeStruct((1, d), F32)])


def _window_counts(t_pos, w):
    return jnp.minimum(t_pos + 1, w).astype(F32)


def _pool_fwd(u, w_pool, scale, tr=512):
    t_total, width = u.shape
    tr = min(tr, t_total)
    n_groups = len(POOL_WINDOWS)
    gdim = width // n_groups
    ext = tr + POOL_HALO

    def body(u_ref, halo_ref, w_ref, s_ref, pooled_ref, ya_ref):
        i = pl.program_id(0)
        halo = jnp.where(i == 0, 0.0, halo_ref[...])
        t_pos = i * tr + lax.broadcasted_iota(jnp.int32, (tr, 1), 0)
        for g, w in enumerate(POOL_WINDOWS):
            cols = slice(g * gdim, (g + 1) * gdim)
            main = u_ref[:, cols]
            win = jnp.concatenate([halo[:, cols], main], axis=0)
            span = 1
            while span < w:
                win = win + pltpu.roll(win, span, 0)
                span *= 2
            pooled = win[POOL_HALO:, :] * (1.0 / _window_counts(t_pos, w)) - main
            pooled_b = pooled.astype(BF16)
            pooled_ref[:, cols] = pooled_b
            mixed = jnp.dot(pooled_b, w_ref[g], preferred_element_type=F32)
            ya_ref[:, cols] = (mixed * s_ref[:, cols]).astype(BF16)

    hb = tr // POOL_HALO
    return pl.pallas_call(
        body, name="pool_fwd", grid=(t_total // tr,),
        in_specs=[pl.BlockSpec((tr, width), lambda i: (i, 0)),
                  pl.BlockSpec((POOL_HALO, width), lambda i: (jnp.maximum(i * hb - 1, 0), 0)),
                  pl.BlockSpec((n_groups, gdim, gdim), lambda i: (0, 0, 0)),
                  pl.BlockSpec((1, width), lambda i: (0, 0))],
        out_specs=[pl.BlockSpec((tr, width), lambda i: (i, 0)), pl.BlockSpec((tr, width), lambda i: (i, 0))],
        out_shape=[jax.ShapeDtypeStruct(u.shape, BF16), jax.ShapeDtypeStruct(u.shape, BF16)],
        compiler_params=_params("parallel"),
    )(u, u, w_pool, scale)


def _pool_bwd(dya, pooled, w_pool, scale, tr=512):
    t_total, width = dya.shape
    tr = min(tr, t_total)
    n_groups = len(POOL_WINDOWS)
    gdim = width // n_groups
    ext = tr + POOL_HALO
    n_tiles = t_total // tr

    def body(d_ref, halo_ref, p_ref, w_ref, s_ref, du_ref, dw_ref, ds_ref):
        i = pl.program_id(0)

        @pl.when(i == 0)
        def _():
            dw_ref[...] = jnp.zeros_like(dw_ref)
            ds_ref[...] = jnp.zeros_like(ds_ref)

        halo = jnp.where(i == n_tiles - 1, 0.0, halo_ref[...])
        t_pos = i * tr + lax.broadcasted_iota(jnp.int32, (ext, 1), 0)
        for g, w in enumerate(POOL_WINDOWS):
            cols = slice(g * gdim, (g + 1) * gdim)
            sc = s_ref[:, cols]
            d_main = d_ref[:, cols]
            pooled_b = p_ref[:, cols]
            mixed = jnp.dot(pooled_b, w_ref[g], preferred_element_type=F32)
            ds_ref[:, cols] += jnp.sum(d_main * mixed, axis=0, keepdims=True)
            dmix = (jnp.concatenate([d_main, halo[:, cols]], axis=0) * sc).astype(BF16)
            dw_ref[g] += lax.dot_general(pooled_b, dmix[:tr, :], (((0,), (0,)), ((), ())),
                                         preferred_element_type=F32)
            dpool = lax.dot_general(dmix, w_ref[g], (((1,), (1,)), ((), ())), preferred_element_type=F32)
            win = dpool * (1.0 / _window_counts(t_pos, w))
            span = 1
            while span < w:
                win = win + pltpu.roll(win, ext - span, 0)
                span *= 2
            du_ref[:, cols] = (win[:tr, :] - dpool[:tr, :]).astype(BF16)

    hb = tr // POOL_HALO
    last_halo = t_total // POOL_HALO - 1
    return pl.pallas_call(
        body, name="pool_bwd", grid=(n_tiles,),
        in_specs=[pl.BlockSpec((tr, width), lambda i: (i, 0)),
                  pl.BlockSpec((POOL_HALO, width), lambda i: (jnp.minimum((i + 1) * hb, last_halo), 0)),
                  pl.BlockSpec((tr, width), lambda i: (i, 0)),
                  pl.BlockSpec((n_groups, gdim, gdim), lambda i: (0, 0, 0)),
                  pl.BlockSpec((1, width), lambda i: (0, 0))],
        out_specs=[pl.BlockSpec((tr, width), lambda i: (i, 0)),
                   pl.BlockSpec((n_groups, gdim, gdim), lambda i: (0, 0, 0)),
                   pl.BlockSpec((1, width), lambda i: (0, 0))],
        out_shape=[jax.ShapeDtypeStruct(dya.shape, BF16), jax.ShapeDtypeStruct((n_groups, gdim, gdim), F32),
                   jax.ShapeDtypeStruct((1, width), F32)],
        compiler_params=_params("arbitrary"),
    )(dya, dya, pooled, w_pool, scale)


def _head_masks():
    lane = lax.broadcasted_iota(jnp.int32, (1, LANES), 1)
    return lane < HEAD_DIM


def _stack_heads(tile, first):
    zero = jnp.zeros_like(tile)
    return jnp.concatenate([jnp.where(first, tile, zero), jnp.where(first, zero, tile)], axis=0)


def _split_bf16(v):
    hi = v.astype(BF16)
    lo = (v - hi.astype(F32)).astype(BF16)
    return hi, lo


def _causal_mask(t_pos, k_start):
    col = lax.broadcasted_iota(jnp.int32, (1, 2 * ATT_SLAB), 1)
    return k_start + (col & (ATT_SLAB - 1)) < t_pos


def _slab_scores(q, kd, mask):
    z2 = lax.dot_general(q, kd, (((1,), (1,)), ((), ())), preferred_element_type=F32) * LOG2_E
    log_fail = -(jnp.maximum(z2, 0.0) + jnp.log2(1.0 + jnp.exp2(-jnp.abs(z2))))
    return z2, (log_fail if mask is None else jnp.where(mask, log_fail, 0.0))


def _weights(z2, log_fail, suffix, mask):
    arg = z2 + log_fail + suffix
    return jnp.exp2(arg if mask is None else jnp.where(mask, arg, -1e30))


def _tri(upper):
    r = lax.broadcasted_iota(jnp.int32, (2 * ATT_CHUNK, ATT_CHUNK), 0) & (ATT_CHUNK - 1)
    c = lax.broadcasted_iota(jnp.int32, (2 * ATT_CHUNK, ATT_CHUNK), 1)
    return jnp.where(r > c if upper else r < c, 1.0, 0.0).astype(BF16)


def _scan_chunk(v, tri):
    return jnp.dot(jnp.concatenate(_split_bf16(v), axis=1), tri, preferred_element_type=F32)


def _lane_bcast(col):
    return jnp.broadcast_to(col, (col.shape[0], LANES))


def _scan_slab(v, tri, carries, from_right):
    n_chunks = ATT_SLAB // ATT_CHUNK
    edge = 0 if from_right else ATT_CHUNK - 1
    parts, new_carries = [None] * (2 * n_chunks), []
    for head in range(2):
        run = carries[head]
        for c in (reversed(range(n_chunks)) if from_right else range(n_chunks)):
            lo_col = head * ATT_SLAB + c * ATT_CHUNK
            vc = v[:, lo_col:lo_col + ATT_CHUNK]
            sc = _scan_chunk(vc, tri)
            parts[head * n_chunks + c] = sc + jnp.concatenate([run] * (ATT_CHUNK // LANES), axis=1)
            run = run + _lane_bcast(sc[:, edge:edge + 1] + vc[:, edge:edge + 1])
        new_carries.append(run)
    return jnp.concatenate(parts, axis=1), new_carries


def _fold_heads(stacked, first):
    s = stacked.shape[0] // 2
    return jnp.where(first, stacked[:s], stacked[s:])


class _NoRider:
    operands, out_shapes, scratch = (), (), ()

    def split(self, refs, n_base_in, n_base_out):
        n_in, n_out, n_sem = len(self.operands), len(self.out_shapes), len(self.scratch)
        a = n_base_in + n_in
        b = a + n_base_out + n_out
        mine = (refs[n_base_in:a], refs[a + n_base_out:b], refs[b:b + n_sem])
        return refs[:n_base_in], refs[a:a + n_base_out], refs[b + n_sem:], mine

    def start(self, ins, outs, sems):
        pass

    def relay(self, ins, outs, sems):
        pass

    def finish(self, ins, outs, sems):
        pass

    def at_steps(self, refs, first_step, relay_step, last_step):
        if not self.operands:
            return (lambda: None), (lambda: None)

        def top():
            pl.when(first_step)(lambda: self.start(*refs))
            pl.when(relay_step)(lambda: self.relay(*refs))

        return top, lambda: pl.when(last_step)(lambda: self.finish(*refs))


def _attn_fwd(q_src, q_col, kv_src, k_col, v_col, n_pairs=4, rider=_NoRider()):
    t_total = q_src.shape[0]
    blk = ATT_BLOCK
    n_blocks = t_total // blk
    assert t_total % ATT_SLAB == 0 and ATT_SLAB % ATT_BLOCK == 0

    def body(*refs):
        (q_ref, k_ref, v_ref), (o_ref,), _, riding = rider.split(refs, 3, 1)
        h, i = pl.program_id(0), pl.program_id(1)
        top, bottom = rider.at_steps(riding, (h == 0) & (i == 0), (h == n_pairs - 1) & (i == 0),
                                     (h == n_pairs - 1) & (i == n_blocks - 1))
        top()
        first = _head_masks()
        q = q_ref[...] * ATT_SCALE
        t_pos = i * blk + lax.broadcasted_iota(jnp.int32, (blk, 1), 0)
        suffix_tri = _tri(upper=True)

        def more(state):
            slab, reach = state[0], state[1]
            return jnp.logical_and(slab >= 0, reach > ATT_EXIT_BELOW)

        def step(state, on_diagonal):
            slab, _, acc, right_a, right_b = state
            k_start = pl.multiple_of(slab * ATT_SLAB, ATT_SLAB)
            kd = _stack_heads(k_ref[pl.ds(k_start, ATT_SLAB), :], first)
            vd = _stack_heads(v_ref[pl.ds(k_start, ATT_SLAB), :], first)
            mask = _causal_mask(t_pos, k_start) if on_diagonal else None
            z2, log_fail = _slab_scores(q, kd, mask)
            suffix, (right_a, right_b) = _scan_slab(log_fail, suffix_tri, (right_a, right_b), from_right=True)
            a = _weights(z2, log_fail, suffix, mask).astype(BF16)
            acc = acc + jnp.dot(a, vd, preferred_element_type=F32)
            return slab - 1, jnp.max(jnp.maximum(right_a, right_b)), acc, right_a, right_b

        zero = jnp.zeros((blk, LANES), F32)
        state = step(((i * blk) // ATT_SLAB, jnp.float32(0.0), zero, zero, zero), on_diagonal=True)
        state = lax.while_loop(more, functools.partial(step, on_diagonal=False), state)
        o_ref[...] = state[2].astype(BF16)
        bottom()

    res = pl.pallas_call(
        body, name="attn_fwd", grid=(n_pairs, n_blocks),
        in_specs=[pl.BlockSpec((blk, LANES), lambda h, i: (i, q_col + h)),
                  pl.BlockSpec((t_total, LANES), lambda h, i: (0, k_col + h)),
                  pl.BlockSpec((t_total, LANES), lambda h, i: (0, v_col + h))] + [ANY] * len(rider.operands),
        out_specs=[pl.BlockSpec((blk, LANES), lambda h, i: (i, h))] + [ANY] * len(rider.out_shapes),
        out_shape=[jax.ShapeDtypeStruct((t_total, n_pairs * LANES), BF16)] + list(rider.out_shapes),
        scratch_shapes=list(rider.scratch),
        compiler_params=_params("arbitrary", "arbitrary"),
    )(q_src, kv_src, kv_src, *rider.operands)
    return res[0], res[1:]


def _attn_bwd(q_src, q_col, kv_src, k_col, v_col, dy, n_pairs=4, rider=_NoRider()):
    t_total = q_src.shape[0]
    blk = ATT_BLOCK
    n_blocks = t_total // blk
    n_slabs = t_total // ATT_SLAB
    assert t_total % ATT_SLAB == 0 and ATT_SLAB % ATT_BLOCK == 0

    def body(*refs):
        (q_ref, dy_ref, k_ref, v_ref), (dq_ref, dk_ref, dv_ref), (g_s, dk_acc, dv_acc), riding = rider.split(refs, 4, 3)
        h, i = pl.program_id(0), pl.program_id(1)
        top, bottom = rider.at_steps(riding, (h == 0) & (i == 0), (h == n_pairs - 1) & (i == 0),
                                     (h == n_pairs - 1) & (i == n_blocks - 1))
        top()

        @pl.when(i == 0)
        def _():
            dk_acc[...] = jnp.zeros_like(dk_acc)
            dv_acc[...] = jnp.zeros_like(dv_acc)

        first = _head_masks()
        q = q_ref[...] * ATT_SCALE
        dy = dy_ref[...]
        t_pos = i * blk + lax.broadcasted_iota(jnp.int32, (blk, 1), 0)
        suffix_tri = _tri(upper=True)
        prefix_tri = _tri(upper=False)
        diag = (i * blk) // ATT_SLAB

        def more(state):
            slab, reach = state[0], state[1]
            return jnp.logical_and(slab >= 0, reach > ATT_EXIT_BELOW)

        def sweep1(state, on_diagonal):
            slab, _, right_a, right_b = state
            k_start = pl.multiple_of(slab * ATT_SLAB, ATT_SLAB)
            kd = _stack_heads(k_ref[pl.ds(k_start, ATT_SLAB), :], first)
            vd = _stack_heads(v_ref[pl.ds(k_start, ATT_SLAB), :], first)
            mask = _causal_mask(t_pos, k_start) if on_diagonal else None
            z2, log_fail = _slab_scores(q, kd, mask)
            suffix, (right_a, right_b) = _scan_slab(log_fail, suffix_tri, (right_a, right_b), from_right=True)
            a = _weights(z2, log_fail, suffix, mask)
            da = lax.dot_general(dy, vd, (((1,), (1,)), ((), ())), preferred_element_type=F32)
            g_s[slab] = da * a
            dv_acc[pl.ds(k_start, ATT_SLAB), :] += _fold_heads(lax.dot_general(
                a.astype(BF16), dy, (((0,), (0,)), ((), ())), preferred_element_type=F32), first)
            return slab - 1, jnp.max(jnp.maximum(right_a, right_b)), right_a, right_b

        zero = jnp.zeros((blk, LANES), F32)
        state = sweep1((diag, jnp.float32(0.0), zero, zero), on_diagonal=True)
        end = lax.while_loop(more, functools.partial(sweep1, on_diagonal=False), state)[0]

        def sweep2(slab, carry, on_diagonal):
            dq, left_a, left_b = carry
            k_start = pl.multiple_of(slab * ATT_SLAB, ATT_SLAB)
            kd = _stack_heads(k_ref[pl.ds(k_start, ATT_SLAB), :], first)
            g = g_s[slab]
            z2 = lax.dot_general(q, kd, (((1,), (1,)), ((), ())), preferred_element_type=F32) * LOG2_E
            sig = 1.0 / (1.0 + jnp.exp2(-z2))
            prefix, (left_a, left_b) = _scan_slab(g, prefix_tri, (left_a, left_b), from_right=False)
            dz = g * (1.0 - sig) - sig * prefix
            if on_diagonal:
                dz = jnp.where(_causal_mask(t_pos, k_start), dz, 0.0)
            dz = dz.astype(BF16)
            dq = dq + jnp.dot(dz, kd, preferred_element_type=F32)
            dk_acc[pl.ds(k_start, ATT_SLAB), :] += _fold_heads(lax.dot_general(
                dz, q, (((0,), (0,)), ((), ())), preferred_element_type=F32), first)
            return dq, left_a, left_b

        carry = lax.fori_loop(end + 1, diag, functools.partial(sweep2, on_diagonal=False), (zero, zero, zero))
        dq = sweep2(diag, carry, on_diagonal=True)[0]
        dq_ref[...] = (dq * ATT_SCALE).astype(BF16)

        @pl.when(i == n_blocks - 1)
        def _():
            dk_ref[...] = dk_acc[...].astype(BF16)
            dv_ref[...] = dv_acc[...].astype(BF16)

        bottom()

    out = jax.ShapeDtypeStruct((t_total, n_pairs * LANES), BF16)
    res = pl.pallas_call(
        body, name="attn_bwd", grid=(n_pairs, n_blocks),
        in_specs=[pl.BlockSpec((blk, LANES), lambda h, i: (i, q_col + h)),
                  pl.BlockSpec((blk, LANES), lambda h, i: (i, h)),
                  pl.BlockSpec((t_total, LANES), lambda h, i: (0, k_col + h)),
                  pl.BlockSpec((t_total, LANES), lambda h, i: (0, v_col + h))] + [ANY] * len(rider.operands),
        out_specs=[pl.BlockSpec((blk, LANES), lambda h, i: (i, h)),
                   pl.BlockSpec((t_total, LANES), lambda h, i: (0, h)),
                   pl.BlockSpec((t_total, LANES), lambda h, i: (0, h))] + [ANY] * len(rider.out_shapes),
        out_shape=[out, out, out] + list(rider.out_shapes),
        scratch_shapes=list(rider.scratch) + [pltpu.VMEM((n_slabs, blk, 2 * ATT_SLAB), F32),
                                              pltpu.VMEM((t_total, LANES), F32), pltpu.VMEM((t_total, LANES), F32)],
        compiler_params=_params("arbitrary", "arbitrary"),
    )(q_src, dy, kv_src, kv_src, *rider.operands)
    return res[:3], res[3:]


def _adamw(name, w, g, m, v):
    def fn(wv, gv, mv, vv):
        mn = ADAM_B1 * mv + (1.0 - ADAM_B1) * gv
        vn = ADAM_B2 * vv + (1.0 - ADAM_B2) * (gv * gv)
        m_hat = mn / (1.0 - ADAM_B1 ** ADAM_STEP)
        v_hat = vn / (1.0 - ADAM_B2 ** ADAM_STEP)
        return -ADAM_LR * (m_hat / (jnp.sqrt(v_hat) + ADAM_EPS) + ADAM_WD * wv), mn, vn

    rows = w.shape[0]
    tr = _row_tile(rows)
    shp = jax.ShapeDtypeStruct(w.shape, F32)
    if rows == 1:
        def body(w_ref, g_ref, m_ref, v_ref, d_ref, mo_ref, vo_ref):
            d, mn, vn = fn(w_ref[...], g_ref[...], m_ref[...], v_ref[...])
            d_ref[...], mo_ref[...], vo_ref[...] = d, mn, vn

        return pl.pallas_call(body, name=name, out_shape=[shp, shp, shp])(w, g, m, v)
    return _rows(name, fn, [w, g, m, v], [shp, shp, shp], tr=tr)


def _place():
    return lax.axis_index("x"), lax.axis_index("y"), lax.axis_index("c")


def _other_chips(x, y):
    return [(1 - x, y), (x, 1 - y), (1 - x, 1 - y)]


ANY = pl.BlockSpec(memory_space=pl.ANY)


def _remote(src, dst, send_sem, recv_sem, to):
    return pltpu.make_async_remote_copy(src_ref=src, dst_ref=dst, send_sem=send_sem, recv_sem=recv_sem,
                                        device_id=to, device_id_type=MESH)


class _WeightGather(_NoRider):
    def __init__(self, shards):
        n_w = len(shards)
        self.operands = list(shards)
        self.out_shapes = [jax.ShapeDtypeStruct((N_CHIPS,) + s.shape, s.dtype) for s in shards]
        self.scratch = [pltpu.SemaphoreType.DMA((3, n_w))] * 4 + [pltpu.SemaphoreType.DMA((n_w,))] * 2

    def _copies(self, ins, outs, sems):
        send_sems, recv_sems, relay_send, relay_recv, own_send, own_recv = sems
        x, y, c = _place()
        my_chip, sibling = 2 * x + y, (x, y, 1 - c)
        n_w = len(ins)

        def half(w, chip, core):
            h = self.operands[w].shape[0] // 2
            return outs[w].at[chip, pl.ds(core * h, h)]

        own = [_remote(ins[w], outs[w].at[my_chip], own_send.at[w], own_recv.at[w], sibling) for w in range(n_w)]
        sends, landed, relays, relayed = [], [], [], []
        for p, (ox, oy) in enumerate(_other_chips(x, y)):
            for w in range(n_w):
                h = self.operands[w].shape[0] // 2
                sends.append(_remote(ins[w].at[pl.ds(c * h, h)], half(w, my_chip, c), send_sems.at[p, w],
                                     recv_sems.at[p, w], (ox, oy, c)))
                here = half(w, 2 * ox + oy, c)
                landed.append(_remote(here, here, send_sems.at[p, w], recv_sems.at[p, w], (ox, oy, c)))
                relays.append(_remote(here, here, relay_send.at[p, w], relay_recv.at[p, w], sibling))
                there = half(w, 2 * ox + oy, 1 - c)
                relayed.append(_remote(there, there, relay_send.at[p, w], relay_recv.at[p, w], sibling))
        return own, sends, landed, relays, relayed

    def start(self, ins, outs, sems):
        own, sends, _, _, _ = self._copies(ins, outs, sems)
        for cp in own + sends:
            cp.start()

    def relay(self, ins, outs, sems):
        _, _, landed, relays, _ = self._copies(ins, outs, sems)
        for arrival, cp in zip(landed, relays):
            arrival.wait_recv()
            cp.start()

    def finish(self, ins, outs, sems):
        own, sends, _, relays, relayed = self._copies(ins, outs, sems)
        for arrival in relayed:
            arrival.wait_recv()
        for cp in sends + relays:
            cp.wait_send()
        for cp in own:
            cp.wait()


class _ChipExchange(_NoRider):
    def __init__(self, pair_sums):
        n_w = len(pair_sums)
        self.operands = list(pair_sums)
        self.out_shapes = [jax.ShapeDtypeStruct((3,) + s.shape[1:], s.dtype) for s in pair_sums]
        self.scratch = [pltpu.SemaphoreType.DMA((3, n_w))] * 2

    def _copies(self, ins, outs, sems):
        send_sems, recv_sems = sems
        x, y, c = _place()
        return [_remote(ins[w].at[2 * ox + oy], outs[w].at[p], send_sems.at[p, w], recv_sems.at[p, w], (ox, oy, c))
                for p, (ox, oy) in enumerate(_other_chips(x, y)) for w in range(len(ins))]

    def start(self, ins, outs, sems):
        for cp in self._copies(ins, outs, sems):
            cp.start()

    def finish(self, ins, outs, sems):
        for cp in self._copies(ins, outs, sems):
            cp.wait()


def _run_exchange(name, plan):
    n_in, n_out = len(plan.operands), len(plan.out_shapes)

    def body(*refs):
        parts = (refs[:n_in], refs[n_in:n_in + n_out], refs[n_in + n_out:])
        plan.start(*parts)
        plan.relay(*parts)
        plan.finish(*parts)

    return pl.pallas_call(body, name=name, in_specs=[ANY] * n_in, out_specs=[ANY] * n_out,
                          out_shape=list(plan.out_shapes), scratch_shapes=list(plan.scratch))(*plan.operands)


def _pair_exchange(name, grads):
    n_w = len(grads)

    def halves(w):
        return grads[w].shape[1] // 2

    def body(*refs):
        ins, theirs = refs[:n_w], refs[n_w:2 * n_w]
        send_sems, recv_sems = refs[2 * n_w:]
        x, y, c = _place()
        sends = [pltpu.make_async_remote_copy(
            src_ref=ins[w].at[:, pl.ds((1 - c) * halves(w), halves(w)), :], dst_ref=theirs[w],
            send_sem=send_sems.at[w], recv_sem=recv_sems.at[w], device_id=(x, y, 1 - c), device_id_type=MESH)
            for w in range(n_w)]
        for cp in sends:
            cp.start()
        for cp in sends:
            cp.wait()

    return pl.pallas_call(
        body, name=name, in_specs=[ANY] * n_w, out_specs=[ANY] * n_w,
        out_shape=[jax.ShapeDtypeStruct((N_CHIPS, halves(w), grads[w].shape[2]), F32) for w in range(n_w)],
        scratch_shapes=[pltpu.SemaphoreType.DMA((n_w,)), pltpu.SemaphoreType.DMA((n_w,))],
    )(*grads)


def _pair_share(shards):
    n_w = len(shards)

    def body(*refs):
        ins, outs = refs[:n_w], refs[n_w:2 * n_w]
        send_sems, recv_sems = refs[2 * n_w:]
        x, y, c = _place()
        sends = []
        for w in range(n_w):
            h = shards[w].shape[0] // 2
            mine = outs[w].at[pl.ds(c * h, h)]
            sends.append(pltpu.make_async_remote_copy(
                src_ref=mine, dst_ref=mine, send_sem=send_sems.at[w], recv_sem=recv_sems.at[w],
                device_id=(x, y, 1 - c), device_id_type=MESH))
        for cp in sends:
            cp.start()
        for w in range(n_w):
            h = shards[w].shape[0] // 2
            theirs = outs[w].at[pl.ds((1 - c) * h, h)]
            pltpu.make_async_remote_copy(
                src_ref=theirs, dst_ref=theirs, send_sem=send_sems.at[w], recv_sem=recv_sems.at[w],
                device_id=(x, y, 1 - c), device_id_type=MESH).wait_recv()
        for cp in sends:
            cp.wait_send()

    return pl.pallas_call(
        body, name="pair_share", in_specs=[ANY] * n_w, out_specs=[ANY] * n_w,
        out_shape=[jax.ShapeDtypeStruct(s.shape, s.dtype) for s in shards],
        input_output_aliases={w: w for w in range(n_w)},
        scratch_shapes=[pltpu.SemaphoreType.DMA((n_w,)), pltpu.SemaphoreType.DMA((n_w,))],
    )(*shards)


def _all_reduce_small(vec):
    rows = vec.shape[0]

    def body(v_ref, o_ref, slots, send_sems, recv_sems):
        x, y, c = _place()
        me = 4 * x + 2 * y + c
        slots[me] = v_ref[...]
        sends = []
        for k in range(1, N_DEV):
            peer = (x ^ (k >> 2), y ^ ((k >> 1) & 1), c ^ (k & 1))
            sends.append(pltpu.make_async_remote_copy(
                src_ref=v_ref, dst_ref=slots.at[me], send_sem=send_sems.at[k - 1], recv_sem=recv_sems.at[k - 1],
                device_id=peer, device_id_type=MESH))
        for cp in sends:
            cp.start()
        for k in range(1, N_DEV):
            px, py, pc = x ^ (k >> 2), y ^ ((k >> 1) & 1), c ^ (k & 1)
            landed = slots.at[4 * px + 2 * py + pc]
            pltpu.make_async_remote_copy(
                src_ref=landed, dst_ref=landed, send_sem=send_sems.at[k - 1], recv_sem=recv_sems.at[k - 1],
                device_id=(px, py, pc), device_id_type=MESH).wait_recv()
        for cp in sends:
            cp.wait_send()
        total = slots[0]
        for d in range(1, N_DEV):
            total = total + slots[d]
        o_ref[...] = total

    vm = pl.BlockSpec(memory_space=pltpu.VMEM)
    return pl.pallas_call(
        body, name="all_reduce_small", in_specs=[vm], out_specs=vm, out_shape=jax.ShapeDtypeStruct(vec.shape, F32),
        scratch_shapes=[pltpu.VMEM((N_DEV, rows, LANES), F32), pltpu.SemaphoreType.DMA((N_DEV - 1,)),
                        pltpu.SemaphoreType.DMA((N_DEV - 1,))],
    )(vec)


def _row_tile(rows):
    for tr in (256, 128, 64, 32, 16):
        if rows % tr == 0:
            return tr
    return rows


def _pair_sum(name, place, grad, theirs):
    n, r, c = grad.shape
    half = r // 2
    tr = _row_tile(half)
    nb = half // tr

    def body(place_ref, g_ref, t_ref, o_ref):
        o_ref[...] = (g_ref[...] + t_ref[...]).astype(BF16)

    return pl.pallas_call(
        body, name=name, out_shape=jax.ShapeDtypeStruct((n, half, c), BF16),
        grid_spec=pltpu.PrefetchScalarGridSpec(
            num_scalar_prefetch=1, grid=(n, nb),
            in_specs=[pl.BlockSpec((1, tr, c), lambda j, i, pr: (j, pr[0] * nb + i, 0)),
                      pl.BlockSpec((1, tr, c), lambda j, i, pr: (j, i, 0))],
            out_specs=pl.BlockSpec((1, tr, c), lambda j, i, pr: (j, i, 0))),
        compiler_params=_params("parallel", "parallel"),
    )(place, grad, theirs)


def _sum_chips(name, place, pair_sums, landed):
    _, half, c = pair_sums.shape
    tr = _row_tile(half)
    nb = half // tr

    def body(place_ref, s_ref, q_ref, o_ref):
        total = s_ref[0].astype(F32)
        for p in range(3):
            total = total + q_ref[p].astype(F32)
        o_ref[...] = total

    return pl.pallas_call(
        body, name=name, out_shape=jax.ShapeDtypeStruct((2 * half, c), F32),
        grid_spec=pltpu.PrefetchScalarGridSpec(
            num_scalar_prefetch=1, grid=(nb,),
            in_specs=[pl.BlockSpec((1, tr, c), lambda i, pr: (pr[1], i, 0)),
                      pl.BlockSpec((3, tr, c), lambda i, pr: (0, i, 0))],
            out_specs=pl.BlockSpec((tr, c), lambda i, pr: (pr[0] * nb + i, 0))),
        compiler_params=_params("parallel"),
    )(place, pair_sums, landed)


BIG = ("w_in", "w_branch_a", "w_branch_b", "w_out", "w_ffn_gate", "w_ffn_up", "w_ffn_down", "w_ple_gate", "w_ple_proj")
LATE = BIG[1:]
COLUMN_SHARDED = ("w_in", "w_branch_a", "w_branch_b", "w_ffn_gate", "w_ffn_up", "w_ple_proj")
SMALL = ("norm_mix", "w_pool", "pool_scale", "norm_ffn", "norm_ple", "norm_final")


def _join_columns(w4):
    return jnp.concatenate([w4[j] for j in range(N_CHIPS)], axis=1)


def _split_columns(g):
    k, n = g.shape
    return g.reshape(k, N_CHIPS, n // N_CHIPS).transpose(1, 0, 2)


def _sds(shape, dtype):
    return jax.ShapeDtypeStruct(shape, dtype)


def _local_step(x, p, target, wf, small, gather_late=None, exchange_early=None):
    t, d = x.shape
    w_in = wf["w_in"]
    w_pool_b = small["w_pool"].astype(BF16)
    dp = w_pool_b.shape[0] * w_pool_b.shape[1]

    h1 = _norm_fwd("norm_mix", x, small["norm_mix"])
    u, q = _mm("proj_uq", [h1], [w_in[0]], "nn", [_sds((t, dp), F32), _sds((t, dp), BF16)],
               epilogue=lambda acc: (acc[:, :dp], acc[:, dp:]))
    kv, = _mm("proj_kv", [h1], [w_in[1]], "nn", [_sds((t, d), BF16)])
    ga, = _mm("proj_ga", [h1], [w_in[2]], "nn", [_sds((t, d), BF16)])
    gb, = _mm("proj_gb", [h1], [w_in[3]], "nn", [_sds((t, d), BF16)])
    pooled, ya = _pool_fwd(u, w_pool_b, small["pool_scale"])
    n_pairs = dp // LANES
    yb, late = _attn_fwd(q, 0, kv, 0, n_pairs, n_pairs, rider=gather_late or _NoRider())
    wf = {**wf, **dict(zip(LATE, late))}
    w_gate, w_up, w_down = wf["w_ffn_gate"], wf["w_ffn_up"], wf["w_ffn_down"]
    w_a, w_b, w_pp = _join_columns(wf["w_branch_a"]), _join_columns(wf["w_branch_b"]), _join_columns(wf["w_ple_proj"])
    w_out = wf["w_out"].reshape(d, d)
    w_pg = wf["w_ple_gate"].reshape(d, d)
    dff = w_gate.shape[2]
    ta, tb, merged = _mm(
        "branches_merge", [ya, yb], [w_a, w_b], "nn", [_sds((t, d), BF16)] * 3, extras=[ga, gb], separate=True,
        epilogue=lambda tav, tbv, gav, gbv: (tav, tbv, _sigmoid(gav) * tav + _sigmoid(gbv) * tbv), tm=512)
    def residual_norm(acc, xv, g):
        xn = acc + xv
        return xn, xn * lax.rsqrt(jnp.mean(xn * xn, axis=-1, keepdims=True) + RMS_EPS) * g

    x1, h2 = _mm("mix_out", [merged], [w_out], "nn", [_sds((t, d), F32), _sds((t, d), BF16)],
                 extras=[x, small["norm_ffn"]], epilogue=residual_norm)
    gates, ups, acts = [], [], []
    for j in range(N_CHIPS):
        gj, uj, aj = _mm(f"ffn_gate_up{j}", [h2], [w_gate[j], w_up[j]], "nn", [_sds((t, dff), BF16)] * 3,
                         separate=True, epilogue=lambda gv, uv: (gv, uv, gv * _sigmoid(gv) * uv))
        gates.append(gj), ups.append(uj), acts.append(aj)
    x2, h3 = _mm("ffn_down", acts, [w_down[j] for j in range(N_CHIPS)], "nn", [_sds((t, d), F32), _sds((t, d), BF16)],
                 extras=[x1, small["norm_ple"]], epilogue=residual_norm, tm=512)
    gp, pp, x3 = _mm(
        "ple", [h3, p], [w_pg, w_pp], "nn", [_sds((t, d), BF16), _sds((t, d), BF16), _sds((t, d), F32)], extras=[x2],
        separate=True, epilogue=lambda gv, pv, xv: (gv, pv, xv + _sigmoid(gv) * pv), tm=512)
    (dx3, d_pp, d_gp), (d_norm_final, loss_row) = _split2(
        _final_loss(x3, target, small["norm_final"].reshape(1, d), gp, pp), 3)

    def through_norm(dh, xv, g, dres):
        dx, d_gain = _rms_norm_bwd(dh, xv, g)
        return dx + dres, dx + dres, d_gain

    stream = [_sds((t, d), F32), _sds((t, d), BF16)]
    gain_sum = [_sds((1, d), F32)]
    g_w_pp, = _mm_tn("g_ple_proj", p, [d_pp])
    g_w_pg, = _mm_tn("g_ple_gate", h3, [d_gp])
    dx2, dx2_b, d_norm_ple = _mm("d_h3", [d_gp], [w_pg], "nt", stream, extras=[x2, small["norm_ple"], dx3],
                                 epilogue=through_norm, sum_shapes=gain_sum, tm=512)

    def ffn_bwd(acc, gv, uv):
        s = _sigmoid(gv)
        return acc * uv * (s * (1.0 + gv * (1.0 - s))), acc * (gv * s)

    d_gates, d_ups, g_w_gate, g_w_up, g_w_down = [], [], None, None, None
    for j in range(N_CHIPS):
        dgj, duj = _mm(f"d_act{j}", [dx2_b], [w_down[j]], "nt", [_sds((t, dff), BF16), _sds((t, dff), BF16)],
                       extras=[gates[j], ups[j]], epilogue=ffn_bwd)
        d_gates.append(dgj), d_ups.append(duj)
        g_w_down, = _mm_tn(f"g_ffn_down{j}", acts[j], [dx2_b], into=[(g_w_down, j)])
        g_w_gate, g_w_up = _mm_tn(f"g_ffn_gate_up{j}", h2, [dgj, duj], into=[(g_w_gate, j), (g_w_up, j)])
    dx1, dx1_b, d_norm_ffn = _mm(
        "d_h2", d_gates + d_ups, [w_gate[j] for j in range(N_CHIPS)] + [w_up[j] for j in range(N_CHIPS)], "nt",
        stream, extras=[x1, small["norm_ffn"], dx2], epilogue=through_norm, sum_shapes=gain_sum, tm=512)

    def merge_bwd(acc, tav, tbv, gav, gbv):
        sa, sb = _sigmoid(gav), _sigmoid(gbv)
        return acc * sa, acc * sb, acc * tav * sa * (1.0 - sa), acc * tbv * sb * (1.0 - sb)

    d_ta, d_tb, d_ga, d_gb = _mm("d_merged", [dx1_b], [w_out], "nt", [_sds((t, d), BF16)] * 4,
                                 extras=[ta, tb, ga, gb], epilogue=merge_bwd, tm=512)
    g_w_out, = _mm_tn("g_w_out", merged, [dx1_b])
    g_w_a, = _mm_tn("g_branch_a", ya, [d_ta])
    g_w_b, = _mm_tn("g_branch_b", yb, [d_tb])
    d_ya, = _mm("d_ya", [d_ta], [w_a], "nt", [_sds((t, dp), F32)])
    d_yb, = _mm("d_yb", [d_tb], [w_b], "nt", [_sds((t, dp), BF16)])
    d_u, g_w_pool, d_pool_scale = _pool_bwd(d_ya, pooled, w_pool_b, small["pool_scale"])
    big = {
        "w_branch_a": _split_columns(g_w_a), "w_branch_b": _split_columns(g_w_b),
        "w_out": g_w_out.reshape(wf["w_out"].shape), "w_ffn_gate": g_w_gate, "w_ffn_up": g_w_up,
        "w_ffn_down": g_w_down, "w_ple_gate": g_w_pg.reshape(wf["w_ple_gate"].shape),
        "w_ple_proj": _split_columns(g_w_pp),
    }
    rider = exchange_early(big) if exchange_early else _NoRider()
    (d_q, d_k, d_v), early = _attn_bwd(q, 0, kv, 0, n_pairs, d_yb, n_pairs, rider=rider)
    d_proj = [(d_u, d_q), (d_k, d_v), d_ga, d_gb]
    big["w_in"], = _mm_tn("g_w_in", h1, d_proj, tmm=512, stacked=True)
    grad_x, d_norm_mix = _mm(
        "d_h1", d_proj, [w_in[j] for j in range(N_CHIPS)], "nt", [_sds((t, d), F32)],
        extras=[x, small["norm_mix"], dx1], epilogue=lambda dh, xv, g, dres: through_norm(dh, xv, g, dres)[1:],
        sum_shapes=gain_sum, tm=512)
    small_g = {"norm_mix": d_norm_mix, "w_pool": g_w_pool, "pool_scale": d_pool_scale, "norm_ffn": d_norm_ffn,
               "norm_ple": d_norm_ple, "norm_final": d_norm_final}
    return grad_x, big, small_g, loss_row, early


def _split2(res, n):
    return res[:n], res[n:]


def _pack_small(small_g, loss_row):
    parts, layout = [], []
    for name in SMALL + ("loss",):
        v = (loss_row if name == "loss" else small_g[name]).reshape(-1, LANES)
        pad = (-v.shape[0]) % 8
        if pad:
            v = jnp.concatenate([v, jnp.zeros((pad, LANES), F32)], axis=0)
        layout.append((name, sum(q.shape[0] for q in parts), v.shape[0]))
        parts.append(v)
    return jnp.concatenate(parts, axis=0), layout


def kernel(x, p, norm_mix, w_in, w_pool, pool_scale, w_branch_a, w_branch_b, w_out, norm_ffn, w_ffn_gate, w_ffn_up, w_ffn_down, norm_ple, w_ple_gate, w_ple_proj, norm_final, loss_target, m_norm_mix, m_w_in, m_w_pool, m_pool_scale, m_w_branch_a, m_w_branch_b, m_w_out, m_norm_ffn, m_w_ffn_gate, m_w_ffn_up, m_w_ffn_down, m_norm_ple, m_w_ple_gate, m_w_ple_proj, m_norm_final, v_norm_mix, v_w_in, v_w_pool, v_pool_scale, v_w_branch_a, v_w_branch_b, v_w_out, v_norm_ffn, v_w_ffn_gate, v_w_ffn_up, v_w_ffn_down, v_norm_ple, v_w_ple_gate, v_w_ple_proj, v_norm_final):
    given = dict(locals())
    names = BIG + SMALL
    order = ("norm_mix", "w_in", "w_pool", "pool_scale", "w_branch_a", "w_branch_b", "w_out", "norm_ffn", "w_ffn_gate",
             "w_ffn_up", "w_ffn_down", "norm_ple", "w_ple_gate", "w_ple_proj", "norm_final")
    t, d = x.shape[1], x.shape[2]
    shard = {n: given[n][0] for n in BIG}
    small = {"norm_mix": norm_mix, "w_pool": w_pool[0], "pool_scale": pool_scale, "norm_ffn": norm_ffn,
             "norm_ple": norm_ple, "norm_final": norm_final}

    as_bf16 = {n: shard[n].astype(BF16) for n in BIG}
    wf = {"w_in": _run_exchange("gather_w_in", _WeightGather([as_bf16["w_in"]]))[0]}

    place = jnp.stack([lax.axis_index("c"), 2 * lax.axis_index("x") + lax.axis_index("y")]).astype(jnp.int32)
    pair_sums = {}

    def exchange_early(ready):
        theirs = _pair_exchange("pair_exchange_early", [ready[n] for n in LATE])
        for n, other in zip(LATE, theirs):
            pair_sums[n] = _pair_sum(f"pair_sum_{n}", place, ready[n], other)
        return _ChipExchange([pair_sums[n] for n in LATE])

    grad_x, big_g, small_g, loss_row, early = _local_step(
        x.reshape(t, d), p.reshape(t, p.shape[-1]), loss_target.reshape(t, d), wf, small,
        gather_late=_WeightGather([as_bf16[n] for n in LATE]), exchange_early=exchange_early)
    landed = dict(zip(LATE, early))
    theirs, = _pair_exchange("pair_exchange_w_in", [big_g["w_in"]])
    pair_sums["w_in"] = _pair_sum("pair_sum_w_in", place, big_g["w_in"], theirs)
    landed["w_in"], = _run_exchange("chip_exchange_w_in", _ChipExchange([pair_sums["w_in"]]))
    halves = [_sum_chips(f"chip_sum_{n}", place, pair_sums[n], landed[n]) for n in BIG]
    grads = dict(zip(BIG, _pair_share(halves)))

    packed, layout = _pack_small(small_g, loss_row)
    reduced = _all_reduce_small(packed)
    for name, start, rows in layout:
        if name == "loss":
            loss = jnp.sum(reduced[start:start + rows])
        else:
            n_el = small[name].size
            grads[name] = reduced[start:start + rows].reshape(-1)[:n_el]

    deltas, new_m, new_v = {}, {}, {}
    for n in order:
        w = shard[n] if n in BIG else small[n]
        shape2 = w.shape if w.ndim == 2 else ((1, w.shape[0]) if w.ndim == 1 else (w.shape[0] * w.shape[1], w.shape[2]))
        g2 = grads[n].reshape(shape2)
        dl, mn, vn = _adamw(f"adamw_{n}", w.reshape(shape2), g2, given["m_" + n].reshape(shape2),
                            given["v_" + n].reshape(shape2))
        full = given[n].shape
        grads[n], deltas[n], new_m[n], new_v[n] = g2.reshape(full), dl.reshape(full), mn.reshape(full), vn.reshape(full)

    return (loss, grad_x.reshape(x.shape), *[grads[n] for n in order], *[deltas[n] for n in order],
            *[new_m[n] for n in order], *[new_v[n] for n in order])
```

```python
import functools
import math

import jax
import jax.numpy as jnp
from jax import lax
from jax.experimental import pallas as pl
from jax.experimental.pallas import tpu as pltpu

F32 = jnp.float32
BF16 = jnp.bfloat16
MESH = pl.DeviceIdType.MESH

RMS_EPS = 1e-6
POOL_WINDOWS = (2, 4, 8, 16)
POOL_HALO = 16
HEAD_DIM = 64
LANES = 128
ATT_BLOCK = 256
ATT_CHUNK = 256
ATT_SLAB = 256
ATT_SCALE = 1.0 / math.sqrt(HEAD_DIM)
LOG2_E = 1.4426950408889634
ATT_EXIT_BELOW = -150.5
ADAM_LR, ADAM_B1, ADAM_B2, ADAM_EPS, ADAM_WD, ADAM_STEP = 0.001, 0.9, 0.999, 1e-08, 0.01, 10
V7X_VMEM_LIMIT_BYTES = 56 * 1024 * 1024
N_CHIPS = 4
N_DEV = 8


def _params(*semantics):
    return pltpu.CompilerParams(dimension_semantics=semantics, vmem_limit_bytes=V7X_VMEM_LIMIT_BYTES)


def _sigmoid(z):
    return 1.0 / (1.0 + jnp.exp(-z))


def _tiled_spec(shape, tm, tn, n_total):
    rows, width = shape
    if rows == 1:
        if width == n_total:
            return pl.BlockSpec((1, tn), lambda i, j: (0, j))
        return pl.BlockSpec((1, width), lambda i, j: (0, 0))
    if width == n_total:
        return pl.BlockSpec((tm, tn), lambda i, j: (i, j))
    assert tn == n_total, "an operand narrower than the output needs whole output rows per tile"
    return pl.BlockSpec((tm, width), lambda i, j: (i, 0))


def _column_pieces(operands):
    pieces = [tuple(a) if isinstance(a, (tuple, list)) else (a,) for a in operands]
    return [p for ps in pieces for p in ps], [len(ps) for ps in pieces]


def _load_bf16(refs, counts):
    tiles, k = [], 0
    for n in counts:
        parts = [r[...] for r in refs[k:k + n]]
        parts = [t if t.dtype == BF16 else t.astype(BF16) for t in parts]
        tiles.append(parts[0] if n == 1 else jnp.concatenate(parts, axis=1))
        k += n
    return tiles


def _mm(name, a_list, b_list, mode, out_shapes, epilogue=None, extras=(), tm=1024, tn=None, separate=False,
        sum_shapes=(), rider=None):
    flat_a, counts = _column_pieces(a_list)
    m_total = flat_a[0].shape[0]
    n_total = b_list[0].shape[1] if mode == "nn" else b_list[0].shape[0]
    tn = n_total if tn is None else tn
    tm = min(tm, m_total)
    assert m_total % tm == 0 and n_total % tn == 0 and (not sum_shapes or tn == n_total)
    n_a, n_b, n_extra, n_out = len(counts), len(b_list), len(extras), len(out_shapes)
    assert n_a in (1, n_b)
    dims = (((1,), (0,)), ((), ())) if mode == "nn" else (((1,), (1,)), ((), ()))
    rider = rider or _NoRider()
    grid = (m_total // tm, n_total // tn)

    def body(*refs):
        ins, o_refs, _, riding = rider.split(refs, len(flat_a) + n_b + n_extra, n_out + len(sum_shapes))
        a_refs, b_refs, e_refs = ins[:len(flat_a)], ins[len(flat_a):len(flat_a) + n_b], ins[len(flat_a) + n_b:]
        at_first = (pl.program_id(0) == 0) & (pl.program_id(1) == 0)
        at_last = (pl.program_id(0) == grid[0] - 1) & (pl.program_id(1) == grid[1] - 1)
        top, bottom = rider.at_steps(riding, at_first, at_first, at_last)
        top()
        lefts = _load_bf16(a_refs, counts)
        products = [lax.dot_general(lefts[s % n_a], b_refs[s][...], dims, preferred_element_type=F32)
                    for s in range(n_b)]
        if not separate:
            products = [functools.reduce(lambda p, r: p + r, products)]
        extra_tiles = [e[...].astype(F32) for e in e_refs]
        outs = products if epilogue is None else epilogue(*products, *extra_tiles)
        for o_ref, o in zip(o_refs[:n_out], outs[:n_out]):
            o_ref[...] = o.astype(o_ref.dtype)
        if sum_shapes:
            @pl.when(pl.program_id(0) == 0)
            def _():
                for s_ref in o_refs[n_out:]:
                    s_ref[...] = jnp.zeros_like(s_ref)

            for s_ref, s in zip(o_refs[n_out:], outs[n_out:]):
                s_ref[...] += s
        bottom()

    once = dict(pipeline_mode=pl.Buffered(1)) if tn == n_total else {}
    in_specs = [pl.BlockSpec((tm, a.shape[1]), lambda i, j: (i, 0)) for a in flat_a]
    if mode == "nn":
        in_specs += [pl.BlockSpec((b.shape[0], tn), lambda i, j: (0, j), **once) for b in b_list]
    else:
        in_specs += [pl.BlockSpec((tn, b.shape[1]), lambda i, j: (j, 0), **once) for b in b_list]
    in_specs += [_tiled_spec(e.shape, tm, tn, n_total) for e in extras]
    out_specs = [_tiled_spec(o.shape, tm, tn, n_total) for o in out_shapes]
    out_specs += [pl.BlockSpec(s.shape, lambda i, j: (0, 0)) for s in sum_shapes]
    semantics = ("arbitrary", "arbitrary") if sum_shapes or rider.operands else ("parallel", "parallel")
    res = pl.pallas_call(
        body, name=name, grid=grid, in_specs=in_specs + [ANY] * len(rider.operands),
        out_specs=out_specs + [ANY] * len(rider.out_shapes),
        out_shape=list(out_shapes) + list(sum_shapes) + list(rider.out_shapes), scratch_shapes=list(rider.scratch),
        compiler_params=_params(*semantics),
    )(*flat_a, *b_list, *extras, *rider.operands)
    n_own = len(out_shapes) + len(sum_shapes)
    return res if not rider.operands else (res[:n_own], res[n_own:])


def _mm_tn(name, a, b_list, tmm=1024, into=None, stacked=False):
    flat_b, counts = _column_pieces(b_list)
    m_total, k = a.shape
    widths = [sum(p.shape[1] for p in flat_b[sum(counts[:s]):sum(counts[:s + 1])]) for s in range(len(counts))]
    tmm = min(tmm, m_total)
    assert m_total % tmm == 0 and not (stacked and into)
    n_b = len(counts)
    passed = [buf for buf, _ in into if buf is not None] if into else []

    def body(*refs):
        a_ref, b_refs = refs[0], refs[1:1 + len(flat_b)]
        o_refs = refs[1 + len(flat_b) + len(passed):]

        @pl.when(pl.program_id(0) == 0)
        def _():
            for o_ref in o_refs:
                o_ref[...] = jnp.zeros_like(o_ref)

        av, = _load_bf16([a_ref], [1])
        for s, bv in enumerate(_load_bf16(b_refs, counts)):
            product = lax.dot_general(av, bv, (((0,), (0,)), ((), ())), preferred_element_type=F32)
            if stacked:
                o_refs[0][s] += product
            elif into:
                o_refs[s][0] += product
            else:
                o_refs[s][...] += product

    in_specs = [pl.BlockSpec((tmm, k), lambda m: (m, 0))]
    in_specs += [pl.BlockSpec((tmm, b.shape[1]), lambda m: (m, 0)) for b in flat_b] + [ANY] * len(passed)
    aliases = {}
    if stacked:
        out_shape = [jax.ShapeDtypeStruct((n_b, k, widths[0]), F32)]
        out_specs = [pl.BlockSpec((n_b, k, widths[0]), lambda m: (0, 0, 0))]
    elif into:
        out_shape = [jax.ShapeDtypeStruct((N_CHIPS, k, w), F32) for w in widths]
        out_specs = [pl.BlockSpec((1, k, w), lambda m, j=j: (j, 0, 0)) for w, (_, j) in zip(widths, into)]
        for s, (buf, _) in enumerate(into):
            if buf is not None:
                aliases[1 + len(flat_b) + len(aliases)] = s
    else:
        out_shape = [jax.ShapeDtypeStruct((k, w), F32) for w in widths]
        out_specs = [pl.BlockSpec((k, w), lambda m: (0, 0)) for w in widths]
    return pl.pallas_call(
        body, name=name, grid=(m_total // tmm,), in_specs=in_specs, out_specs=out_specs, out_shape=out_shape,
        input_output_aliases=aliases, compiler_params=_params("arbitrary"),
    )(a, *flat_b, *passed)


def _rows(name, fn, ins, tile_outs, sum_outs=(), tr=512):
    t_total = max(a.shape[0] for a in ins)
    tr = min(tr, t_total)
    assert t_total % tr == 0
    n_in, n_tile = len(ins), len(tile_outs)

    def body(*refs):
        outs = fn(*[r[...].astype(F32) for r in refs[:n_in]])
        for o_ref, o in zip(refs[n_in:n_in + n_tile], outs[:n_tile]):
            o_ref[...] = o.astype(o_ref.dtype)
        if sum_outs:
            @pl.when(pl.program_id(0) == 0)
            def _():
                for s_ref in refs[n_in + n_tile:]:
                    s_ref[...] = jnp.zeros_like(s_ref)

            for s_ref, s in zip(refs[n_in + n_tile:], outs[n_tile:]):
                s_ref[...] += s

    def spec(shape):
        if shape[0] == 1:
            return pl.BlockSpec(shape, lambda i: (0, 0))
        return pl.BlockSpec((tr, shape[1]), lambda i: (i, 0))

    return pl.pallas_call(
        body, name=name, grid=(t_total // tr,), in_specs=[spec(a.shape) for a in ins],
        out_specs=[spec(o.shape) for o in tile_outs] + [spec(s.shape) for s in sum_outs],
        out_shape=list(tile_outs) + list(sum_outs),
        compiler_params=_params("arbitrary" if sum_outs else "parallel"),
    )(*ins)


def _norm_fwd(name, x, gain):
    def fn(xv, g):
        inv = lax.rsqrt(jnp.mean(xv * xv, axis=-1, keepdims=True) + RMS_EPS)
        return (xv * inv * g,)

    return _rows(name, fn, [x, gain], [jax.ShapeDtypeStruct(x.shape, BF16)])[0]


def _rms_norm_bwd(dh, xv, g):
    inv = lax.rsqrt(jnp.mean(xv * xv, axis=-1, keepdims=True) + RMS_EPS)
    xn = xv * inv
    dxn = dh * g
    return inv * (dxn - xn * jnp.mean(dxn * xn, axis=-1, keepdims=True)), jnp.sum(dh * xn, axis=0, keepdims=True)


def _final_loss(x3, target, gain, gp, pp):
    d = x3.shape[1]

    def fn(xv, tv, g, gv, pv):
        inv = lax.rsqrt(jnp.mean(xv * xv, axis=-1, keepdims=True) + RMS_EPS)
        err = xv * inv * g - tv
        dx, d_gain = _rms_norm_bwd(err * (1.0 / d), xv, g)
        s = _sigmoid(gv)
        return dx, dx * s, dx * pv * s * (1.0 - s), d_gain, (0.5 / d) * jnp.sum(err * err, axis=0, keepdims=True)

    act = jax.ShapeDtypeStruct(x3.shape, BF16)
    return _rows("final_loss", fn, [x3, target, gain, gp, pp], [jax.ShapeDtypeStruct(x3.shape, F32), act, act],
                 [jax.ShapeDtypeStruct((1, d), F32), jax.ShapeDtypeStruct((1, d), F32)])


def _window_counts(t_pos, w):
    return jnp.minimum(t_pos + 1, w).astype(F32)


def _pool_fwd(u, w_pool, scale, tr=512):
    t_total, width = u.shape
    tr = min(tr, t_total)
    n_groups = len(POOL_WINDOWS)
    gdim = width // n_groups
    ext = tr + POOL_HALO

    def body(u_ref, halo_ref, w_ref, s_ref, pooled_ref, ya_ref):
        i = pl.program_id(0)
        halo = jnp.where(i == 0, 0.0, halo_ref[...])
        t_pos = i * tr + lax.broadcasted_iota(jnp.int32, (tr, 1), 0)
        for g, w in enumerate(POOL_WINDOWS):
            cols = slice(g * gdim, (g + 1) * gdim)
            main = u_ref[:, cols]
            win = jnp.concatenate([halo[:, cols], main], axis=0)
            span = 1
            while span < w:
                win = win + pltpu.roll(win, span, 0)
                span *= 2
            pooled = win[POOL_HALO:, :] * (1.0 / _window_counts(t_pos, w)) - main
            pooled_b = pooled.astype(BF16)
            pooled_ref[:, cols] = pooled_b
            mixed = jnp.dot(pooled_b, w_ref[g], preferred_element_type=F32)
            ya_ref[:, cols] = (mixed * s_ref[:, cols]).astype(BF16)

    hb = tr // POOL_HALO
    return pl.pallas_call(
        body, name="pool_fwd", grid=(t_total // tr,),
        in_specs=[pl.BlockSpec((tr, width), lambda i: (i, 0)),
                  pl.BlockSpec((POOL_HALO, width), lambda i: (jnp.maximum(i * hb - 1, 0), 0)),
                  pl.BlockSpec((n_groups, gdim, gdim), lambda i: (0, 0, 0)),
                  pl.BlockSpec((1, width), lambda i: (0, 0))],
        out_specs=[pl.BlockSpec((tr, width), lambda i: (i, 0)), pl.BlockSpec((tr, width), lambda i: (i, 0))],
        out_shape=[jax.ShapeDtypeStruct(u.shape, BF16), jax.ShapeDtypeStruct(u.shape, BF16)],
        compiler_params=_params("parallel"),
    )(u, u, w_pool, scale)


def _pool_bwd(dya, pooled, w_pool, scale, tr=512):
    t_total, width = dya.shape
    tr = min(tr, t_total)
    n_groups = len(POOL_WINDOWS)
    gdim = width // n_groups
    ext = tr + POOL_HALO
    n_tiles = t_total // tr

    def body(d_ref, halo_ref, p_ref, w_ref, s_ref, du_ref, dw_ref, ds_ref):
        i = pl.program_id(0)

        @pl.when(i == 0)
        def _():
            dw_ref[...] = jnp.zeros_like(dw_ref)
            ds_ref[...] = jnp.zeros_like(ds_ref)

        halo = jnp.where(i == n_tiles - 1, 0.0, halo_ref[...])
        t_pos = i * tr + lax.broadcasted_iota(jnp.int32, (ext, 1), 0)
        for g, w in enumerate(POOL_WINDOWS):
            cols = slice(g * gdim, (g + 1) * gdim)
            sc = s_ref[:, cols]
            d_main = d_ref[:, cols]
            pooled_b = p_ref[:, cols]
            mixed = jnp.dot(pooled_b, w_ref[g], preferred_element_type=F32)
            ds_ref[:, cols] += jnp.sum(d_main * mixed, axis=0, keepdims=True)
            dmix = (jnp.concatenate([d_main, halo[:, cols]], axis=0) * sc).astype(BF16)
            dw_ref[g] += lax.dot_general(pooled_b, dmix[:tr, :], (((0,), (0,)), ((), ())),
                                         preferred_element_type=F32)
            dpool = lax.dot_general(dmix, w_ref[g], (((1,), (1,)), ((), ())), preferred_element_type=F32)
            win = dpool * (1.0 / _window_counts(t_pos, w))
            span = 1
            while span < w:
                win = win + pltpu.roll(win, ext - span, 0)
                span *= 2
            du_ref[:, cols] = (win[:tr, :] - dpool[:tr, :]).astype(BF16)

    hb = tr // POOL_HALO
    last_halo = t_total // POOL_HALO - 1
    return pl.pallas_call(
        body, name="pool_bwd", grid=(n_tiles,),
        in_specs=[pl.BlockSpec((tr, width), lambda i: (i, 0)),
                  pl.BlockSpec((POOL_HALO, width), lambda i: (jnp.minimum((i + 1) * hb, last_halo), 0)),
                  pl.BlockSpec((tr, width), lambda i: (i, 0)),
                  pl.BlockSpec((n_groups, gdim, gdim), lambda i: (0, 0, 0)),
                  pl.BlockSpec((1, width), lambda i: (0, 0))],
        out_specs=[pl.BlockSpec((tr, width), lambda i: (i, 0)),
                   pl.BlockSpec((n_groups, gdim, gdim), lambda i: (0, 0, 0)),
                   pl.BlockSpec((1, width), lambda i: (0, 0))],
        out_shape=[jax.ShapeDtypeStruct(dya.shape, BF16), jax.ShapeDtypeStruct((n_groups, gdim, gdim), F32),
                   jax.ShapeDtypeStruct((1, width), F32)],
        compiler_params=_params("arbitrary"),
    )(dya, dya, pooled, w_pool, scale)


def _head_masks():
    lane = lax.broadcasted_iota(jnp.int32, (1, LANES), 1)
    return lane < HEAD_DIM


def _stack_heads(tile, first):
    zero = jnp.zeros_like(tile)
    return jnp.concatenate([jnp.where(first, tile, zero), jnp.where(first, zero, tile)], axis=0)


def _split_bf16(v):
    hi = v.astype(BF16)
    lo = (v - hi.astype(F32)).astype(BF16)
    return hi, lo


def _causal_mask(t_pos, k_start):
    col = lax.broadcasted_iota(jnp.int32, (1, 2 * ATT_SLAB), 1)
    return k_start + (col & (ATT_SLAB - 1)) < t_pos


def _slab_scores(q, kd, mask):
    z2 = lax.dot_general(q, kd, (((1,), (1,)), ((), ())), preferred_element_type=F32) * LOG2_E
    log_fail = -(jnp.maximum(z2, 0.0) + jnp.log2(1.0 + jnp.exp2(-jnp.abs(z2))))
    return z2, (log_fail if mask is None else jnp.where(mask, log_fail, 0.0))


def _weights(z2, log_fail, suffix, mask):
    arg = z2 + log_fail + suffix
    return jnp.exp2(arg if mask is None else jnp.where(mask, arg, -1e30))


def _tri(upper):
    r = lax.broadcasted_iota(jnp.int32, (2 * ATT_CHUNK, ATT_CHUNK), 0) & (ATT_CHUNK - 1)
    c = lax.broadcasted_iota(jnp.int32, (2 * ATT_CHUNK, ATT_CHUNK), 1)
    return jnp.where(r > c if upper else r < c, 1.0, 0.0).astype(BF16)


def _scan_chunk(v, tri):
    return jnp.dot(jnp.concatenate(_split_bf16(v), axis=1), tri, preferred_element_type=F32)


def _lane_bcast(col):
    return jnp.broadcast_to(col, (col.shape[0], LANES))


def _scan_slab(v, tri, carries, from_right):
    n_chunks = ATT_SLAB // ATT_CHUNK
    edge = 0 if from_right else ATT_CHUNK - 1
    parts, new_carries = [None] * (2 * n_chunks), []
    for head in range(2):
        run = carries[head]
        for c in (reversed(range(n_chunks)) if from_right else range(n_chunks)):
            lo_col = head * ATT_SLAB + c * ATT_CHUNK
            vc = v[:, lo_col:lo_col + ATT_CHUNK]
            sc = _scan_chunk(vc, tri)
            parts[head * n_chunks + c] = sc + jnp.concatenate([run] * (ATT_CHUNK // LANES), axis=1)
            run = run + _lane_bcast(sc[:, edge:edge + 1] + vc[:, edge:edge + 1])
        new_carries.append(run)
    return jnp.concatenate(parts, axis=1), new_carries


def _fold_heads(stacked, first):
    s = stacked.shape[0] // 2
    return jnp.where(first, stacked[:s], stacked[s:])


class _NoRider:
    operands, out_shapes, scratch = (), (), ()

    def split(self, refs, n_base_in, n_base_out):
        n_in, n_out, n_sem = len(self.operands), len(self.out_shapes), len(self.scratch)
        a = n_base_in + n_in
        b = a + n_base_out + n_out
        mine = (refs[n_base_in:a], refs[a + n_base_out:b], refs[b:b + n_sem])
        return refs[:n_base_in], refs[a:a + n_base_out], refs[b + n_sem:], mine

    def start(self, ins, outs, sems):
        pass

    def relay(self, ins, outs, sems):
        pass

    def finish(self, ins, outs, sems):
        pass

    def at_steps(self, refs, first_step, relay_step, last_step):
        if not self.operands:
            return (lambda: None), (lambda: None)

        def top():
            pl.when(first_step)(lambda: self.start(*refs))
            pl.when(relay_step)(lambda: self.relay(*refs))

        return top, lambda: pl.when(last_step)(lambda: self.finish(*refs))


def _attn_fwd(q_src, q_col, kv_src, k_col, v_col, n_pairs=4, rider=_NoRider()):
    t_total = q_src.shape[0]
    blk = ATT_BLOCK
    n_blocks = t_total // blk
    assert t_total % ATT_SLAB == 0 and ATT_SLAB % ATT_BLOCK == 0

    def body(*refs):
        (q_ref, k_ref, v_ref), (o_ref,), _, riding = rider.split(refs, 3, 1)
        h, i = pl.program_id(0), pl.program_id(1)
        top, bottom = rider.at_steps(riding, (h == 0) & (i == 0), (h == n_pairs - 1) & (i == 0),
                                     (h == n_pairs - 1) & (i == n_blocks - 1))
        top()
        first = _head_masks()
        q = q_ref[...] * ATT_SCALE
        t_pos = i * blk + lax.broadcasted_iota(jnp.int32, (blk, 1), 0)
        suffix_tri = _tri(upper=True)

        def more(state):
            slab, reach = state[0], state[1]
            return jnp.logical_and(slab >= 0, reach > ATT_EXIT_BELOW)

        def step(state, on_diagonal):
            slab, _, acc, right_a, right_b = state
            k_start = pl.multiple_of(slab * ATT_SLAB, ATT_SLAB)
            kd = _stack_heads(k_ref[pl.ds(k_start, ATT_SLAB), :], first)
            vd = _stack_heads(v_ref[pl.ds(k_start, ATT_SLAB), :], first)
            mask = _causal_mask(t_pos, k_start) if on_diagonal else None
            z2, log_fail = _slab_scores(q, kd, mask)
            suffix, (right_a, right_b) = _scan_slab(log_fail, suffix_tri, (right_a, right_b), from_right=True)
            a = _weights(z2, log_fail, suffix, mask).astype(BF16)
            acc = acc + jnp.dot(a, vd, preferred_element_type=F32)
            return slab - 1, jnp.max(jnp.maximum(right_a, right_b)), acc, right_a, right_b

        zero = jnp.zeros((blk, LANES), F32)
        state = step(((i * blk) // ATT_SLAB, jnp.float32(0.0), zero, zero, zero), on_diagonal=True)
        state = lax.while_loop(more, functools.partial(step, on_diagonal=False), state)
        o_ref[...] = state[2].astype(BF16)
        bottom()

    res = pl.pallas_call(
        body, name="attn_fwd", grid=(n_pairs, n_blocks),
        in_specs=[pl.BlockSpec((blk, LANES), lambda h, i: (i, q_col + h)),
                  pl.BlockSpec((t_total, LANES), lambda h, i: (0, k_col + h)),
                  pl.BlockSpec((t_total, LANES), lambda h, i: (0, v_col + h))] + [ANY] * len(rider.operands),
        out_specs=[pl.BlockSpec((blk, LANES), lambda h, i: (i, h))] + [ANY] * len(rider.out_shapes),
        out_shape=[jax.ShapeDtypeStruct((t_total, n_pairs * LANES), BF16)] + list(rider.out_shapes),
        scratch_shapes=list(rider.scratch),
        compiler_params=_params("arbitrary", "arbitrary"),
    )(q_src, kv_src, kv_src, *rider.operands)
    return res[0], res[1:]


def _attn_bwd(q_src, q_col, kv_src, k_col, v_col, dy, n_pairs=4, rider=_NoRider()):
    t_total = q_src.shape[0]
    blk = ATT_BLOCK
    n_blocks = t_total // blk
    n_slabs = t_total // ATT_SLAB
    assert t_total % ATT_SLAB == 0 and ATT_SLAB % ATT_BLOCK == 0

    def body(*refs):
        (q_ref, dy_ref, k_ref, v_ref), (dq_ref, dk_ref, dv_ref), (g_s, dk_acc, dv_acc), riding = rider.split(refs, 4, 3)
        h, i = pl.program_id(0), pl.program_id(1)
        top, bottom = rider.at_steps(riding, (h == 0) & (i == 0), (h == n_pairs - 1) & (i == 0),
                                     (h == n_pairs - 1) & (i == n_blocks - 1))
        top()

        @pl.when(i == 0)
        def _():
            dk_acc[...] = jnp.zeros_like(dk_acc)
            dv_acc[...] = jnp.zeros_like(dv_acc)

        first = _head_masks()
        q = q_ref[...] * ATT_SCALE
        dy = dy_ref[...]
        t_pos = i * blk + lax.broadcasted_iota(jnp.int32, (blk, 1), 0)
        suffix_tri = _tri(upper=True)
        prefix_tri = _tri(upper=False)
        diag = (i * blk) // ATT_SLAB

        def more(state):
            slab, reach = state[0], state[1]
            return jnp.logical_and(slab >= 0, reach > ATT_EXIT_BELOW)

        def sweep1(state, on_diagonal):
            slab, _, right_a, right_b = state
            k_start = pl.multiple_of(slab * ATT_SLAB, ATT_SLAB)
            kd = _stack_heads(k_ref[pl.ds(k_start, ATT_SLAB), :], first)
            vd = _stack_heads(v_ref[pl.ds(k_start, ATT_SLAB), :], first)
            mask = _causal_mask(t_pos, k_start) if on_diagonal else None
            z2, log_fail = _slab_scores(q, kd, mask)
            suffix, (right_a, right_b) = _scan_slab(log_fail, suffix_tri, (right_a, right_b), from_right=True)
            a = _weights(z2, log_fail, suffix, mask)
            da = lax.dot_general(dy, vd, (((1,), (1,)), ((), ())), preferred_element_type=F32)
            g_s[slab] = da * a
            dv_acc[pl.ds(k_start, ATT_SLAB), :] += _fold_heads(lax.dot_general(
                a.astype(BF16), dy, (((0,), (0,)), ((), ())), preferred_element_type=F32), first)
            return slab - 1, jnp.max(jnp.maximum(right_a, right_b)), right_a, right_b

        zero = jnp.zeros((blk, LANES), F32)
        state = sweep1((diag, jnp.float32(0.0), zero, zero), on_diagonal=True)
        end = lax.while_loop(more, functools.partial(sweep1, on_diagonal=False), state)[0]

        def sweep2(slab, carry, on_diagonal):
            dq, left_a, left_b = carry
            k_start = pl.multiple_of(slab * ATT_SLAB, ATT_SLAB)
            kd = _stack_heads(k_ref[pl.ds(k_start, ATT_SLAB), :], first)
            g = g_s[slab]
            z2 = lax.dot_general(q, kd, (((1,), (1,)), ((), ())), preferred_element_type=F32) * LOG2_E
            sig = 1.0 / (1.0 + jnp.exp2(-z2))
            prefix, (left_a, left_b) = _scan_slab(g, prefix_tri, (left_a, left_b), from_right=False)
            dz = g * (1.0 - sig) - sig * prefix
            if on_diagonal:
                dz = jnp.where(_causal_mask(t_pos, k_start), dz, 0.0)
            dz = dz.astype(BF16)
            dq = dq + jnp.dot(dz, kd, preferred_element_type=F32)
            dk_acc[pl.ds(k_start, ATT_SLAB), :] += _fold_heads(lax.dot_general(
                dz, q, (((0,), (0,)), ((), ())), preferred_element_type=F32), first)
            return dq, left_a, left_b

        carry = lax.fori_loop(end + 1, diag, functools.partial(sweep2, on_diagonal=False), (zero, zero, zero))
        dq = sweep2(diag, carry, on_diagonal=True)[0]
        dq_ref[...] = (dq * ATT_SCALE).astype(BF16)

        @pl.when(i == n_blocks - 1)
        def _():
            dk_ref[...] = dk_acc[...].astype(BF16)
            dv_ref[...] = dv_acc[...].astype(BF16)

        bottom()

    out = jax.ShapeDtypeStruct((t_total, n_pairs * LANES), BF16)
    res = pl.pallas_call(
        body, name="attn_bwd", grid=(n_pairs, n_blocks),
        in_specs=[pl.BlockSpec((blk, LANES), lambda h, i: (i, q_col + h)),
                  pl.BlockSpec((blk, LANES), lambda h, i: (i, h)),
                  pl.BlockSpec((t_total, LANES), lambda h, i: (0, k_col + h)),
                  pl.BlockSpec((t_total, LANES), lambda h, i: (0, v_col + h))] + [ANY] * len(rider.operands),
        out_specs=[pl.BlockSpec((blk, LANES), lambda h, i: (i, h)),
                   pl.BlockSpec((t_total, LANES), lambda h, i: (0, h)),
                   pl.BlockSpec((t_total, LANES), lambda h, i: (0, h))] + [ANY] * len(rider.out_shapes),
        out_shape=[out, out, out] + list(rider.out_shapes),
        scratch_shapes=list(rider.scratch) + [pltpu.VMEM((n_slabs, blk, 2 * ATT_SLAB), F32),
                                              pltpu.VMEM((t_total, LANES), F32), pltpu.VMEM((t_total, LANES), F32)],
        compiler_params=_params("arbitrary", "arbitrary"),
    )(q_src, dy, kv_src, kv_src, *rider.operands)
    return res[:3], res[3:]


def _adamw(name, w, g, m, v):
    def fn(wv, gv, mv, vv):
        mn = ADAM_B1 * mv + (1.0 - ADAM_B1) * gv
        vn = ADAM_B2 * vv + (1.0 - ADAM_B2) * (gv * gv)
        m_hat = mn / (1.0 - ADAM_B1 ** ADAM_STEP)
        v_hat = vn / (1.0 - ADAM_B2 ** ADAM_STEP)
        return -ADAM_LR * (m_hat / (jnp.sqrt(v_hat) + ADAM_EPS) + ADAM_WD * wv), mn, vn

    rows = w.shape[0]
    tr = _row_tile(rows)
    shp = jax.ShapeDtypeStruct(w.shape, F32)
    if rows == 1:
        def body(w_ref, g_ref, m_ref, v_ref, d_ref, mo_ref, vo_ref):
            d, mn, vn = fn(w_ref[...], g_ref[...], m_ref[...], v_ref[...])
            d_ref[...], mo_ref[...], vo_ref[...] = d, mn, vn

        return pl.pallas_call(body, name=name, out_shape=[shp, shp, shp])(w, g, m, v)
    return _rows(name, fn, [w, g, m, v], [shp, shp, shp], tr=tr)


def _place():
    return lax.axis_index("x"), lax.axis_index("y"), lax.axis_index("c")


def _other_chips(x, y):
    return [(1 - x, y), (x, 1 - y), (1 - x, 1 - y)]


ANY = pl.BlockSpec(memory_space=pl.ANY)


def _remote(src, dst, send_sem, recv_sem, to):
    return pltpu.make_async_remote_copy(src_ref=src, dst_ref=dst, send_sem=send_sem, recv_sem=recv_sem,
                                        device_id=to, device_id_type=MESH)


class _WeightGather(_NoRider):
    def __init__(self, shards):
        n_w = len(shards)
        self.operands = list(shards)
        self.out_shapes = [jax.ShapeDtypeStruct((N_CHIPS,) + s.shape, s.dtype) for s in shards]
        self.scratch = [pltpu.SemaphoreType.DMA((3, n_w))] * 4 + [pltpu.SemaphoreType.DMA((n_w,))] * 2

    def _copies(self, ins, outs, sems):
        send_sems, recv_sems, relay_send, relay_recv, own_send, own_recv = sems
        x, y, c = _place()
        my_chip, sibling = 2 * x + y, (x, y, 1 - c)
        n_w = len(ins)

        def half(w, chip, core):
            h = self.operands[w].shape[0] // 2
            return outs[w].at[chip, pl.ds(core * h, h)]

        own = [_remote(ins[w], outs[w].at[my_chip], own_send.at[w], own_recv.at[w], sibling) for w in range(n_w)]
        sends, landed, relays, relayed = [], [], [], []
        for p, (ox, oy) in enumerate(_other_chips(x, y)):
            for w in range(n_w):
                h = self.operands[w].shape[0] // 2
                sends.append(_remote(ins[w].at[pl.ds(c * h, h)], half(w, my_chip, c), send_sems.at[p, w],
                                     recv_sems.at[p, w], (ox, oy, c)))
                here = half(w, 2 * ox + oy, c)
                landed.append(_remote(here, here, send_sems.at[p, w], recv_sems.at[p, w], (ox, oy, c)))
                relays.append(_remote(here, here, relay_send.at[p, w], relay_recv.at[p, w], sibling))
                there = half(w, 2 * ox + oy, 1 - c)
                relayed.append(_remote(there, there, relay_send.at[p, w], relay_recv.at[p, w], sibling))
        return own, sends, landed, relays, relayed

    def start(self, ins, outs, sems):
        own, sends, _, _, _ = self._copies(ins, outs, sems)
        for cp in own + sends:
            cp.start()

    def relay(self, ins, outs, sems):
        _, _, landed, relays, _ = self._copies(ins, outs, sems)
        for arrival, cp in zip(landed, relays):
            arrival.wait_recv()
            cp.start()

    def finish(self, ins, outs, sems):
        own, sends, _, relays, relayed = self._copies(ins, outs, sems)
        for arrival in relayed:
            arrival.wait_recv()
        for cp in sends + relays:
            cp.wait_send()
        for cp in own:
            cp.wait()


class _ChipExchange(_NoRider):
    def __init__(self, pair_sums):
        n_w = len(pair_sums)
        self.operands = list(pair_sums)
        self.out_shapes = [jax.ShapeDtypeStruct((3,) + s.shape[1:], s.dtype) for s in pair_sums]
        self.scratch = [pltpu.SemaphoreType.DMA((3, n_w))] * 2

    def _copies(self, ins, outs, sems):
        send_sems, recv_sems = sems
        x, y, c = _place()
        return [_remote(ins[w].at[2 * ox + oy], outs[w].at[p], send_sems.at[p, w], recv_sems.at[p, w], (ox, oy, c))
                for p, (ox, oy) in enumerate(_other_chips(x, y)) for w in range(len(ins))]

    def start(self, ins, outs, sems):
        for cp in self._copies(ins, outs, sems):
            cp.start()

    def finish(self, ins, outs, sems):
        for cp in self._copies(ins, outs, sems):
            cp.wait()


def _run_exchange(name, plan):
    n_in, n_out = len(plan.operands), len(plan.out_shapes)

    def body(*refs):
        parts = (refs[:n_in], refs[n_in:n_in + n_out], refs[n_in + n_out:])
        plan.start(*parts)
        plan.relay(*parts)
        plan.finish(*parts)

    return pl.pallas_call(body, name=name, in_specs=[ANY] * n_in, out_specs=[ANY] * n_out,
                          out_shape=list(plan.out_shapes), scratch_shapes=list(plan.scratch))(*plan.operands)


def _pair_exchange(name, grads):
    n_w = len(grads)

    def halves(w):
        return grads[w].shape[1] // 2

    def body(*refs):
        ins, theirs = refs[:n_w], refs[n_w:2 * n_w]
        send_sems, recv_sems = refs[2 * n_w:]
        x, y, c = _place()
        sends = [pltpu.make_async_remote_copy(
            src_ref=ins[w].at[:, pl.ds((1 - c) * halves(w), halves(w)), :], dst_ref=theirs[w],
            send_sem=send_sems.at[w], recv_sem=recv_sems.at[w], device_id=(x, y, 1 - c), device_id_type=MESH)
            for w in range(n_w)]
        for cp in sends:
            cp.start()
        for cp in sends:
            cp.wait()

    return pl.pallas_call(
        body, name=name, in_specs=[ANY] * n_w, out_specs=[ANY] * n_w,
        out_shape=[jax.ShapeDtypeStruct((N_CHIPS, halves(w), grads[w].shape[2]), F32) for w in range(n_w)],
        scratch_shapes=[pltpu.SemaphoreType.DMA((n_w,)), pltpu.SemaphoreType.DMA((n_w,))],
    )(*grads)


def _pair_share(shards):
    n_w = len(shards)

    def body(*refs):
        ins, outs = refs[:n_w], refs[n_w:2 * n_w]
        send_sems, recv_sems = refs[2 * n_w:]
        x, y, c = _place()
        sends = []
        for w in range(n_w):
            h = shards[w].shape[0] // 2
            mine = outs[w].at[pl.ds(c * h, h)]
            sends.append(pltpu.make_async_remote_copy(
                src_ref=mine, dst_ref=mine, send_sem=send_sems.at[w], recv_sem=recv_sems.at[w],
                device_id=(x, y, 1 - c), device_id_type=MESH))
        for cp in sends:
            cp.start()
        for w in range(n_w):
            h = shards[w].shape[0] // 2
            theirs = outs[w].at[pl.ds((1 - c) * h, h)]
            pltpu.make_async_remote_copy(
                src_ref=theirs, dst_ref=theirs, send_sem=send_sems.at[w], recv_sem=recv_sems.at[w],
                device_id=(x, y, 1 - c), device_id_type=MESH).wait_recv()
        for cp in sends:
            cp.wait_send()

    return pl.pallas_call(
        body, name="pair_share", in_specs=[ANY] * n_w, out_specs=[ANY] * n_w,
        out_shape=[jax.ShapeDtypeStruct(s.shape, s.dtype) for s in shards],
        input_output_aliases={w: w for w in range(n_w)},
        scratch_shapes=[pltpu.SemaphoreType.DMA((n_w,)), pltpu.SemaphoreType.DMA((n_w,))],
    )(*shards)


def _all_reduce_small(vec):
    rows = vec.shape[0]

    def body(v_ref, o_ref, slots, send_sems, recv_sems):
        x, y, c = _place()
        me = 4 * x + 2 * y + c
        slots[me] = v_ref[...]
        sends = []
        for k in range(1, N_DEV):
            peer = (x ^ (k >> 2), y ^ ((k >> 1) & 1), c ^ (k & 1))
            sends.append(pltpu.make_async_remote_copy(
                src_ref=v_ref, dst_ref=slots.at[me], send_sem=send_sems.at[k - 1], recv_sem=recv_sems.at[k - 1],
                device_id=peer, device_id_type=MESH))
        for cp in sends:
            cp.start()
        for k in range(1, N_DEV):
            px, py, pc = x ^ (k >> 2), y ^ ((k >> 1) & 1), c ^ (k & 1)
            landed = slots.at[4 * px + 2 * py + pc]
            pltpu.make_async_remote_copy(
                src_ref=landed, dst_ref=landed, send_sem=send_sems.at[k - 1], recv_sem=recv_sems.at[k - 1],
                device_id=(px, py, pc), device_id_type=MESH).wait_recv()
        for cp in sends:
            cp.wait_send()
        total = slots[0]
        for d in range(1, N_DEV):
            total = total + slots[d]
        o_ref[...] = total

    vm = pl.BlockSpec(memory_space=pltpu.VMEM)
    return pl.pallas_call(
        body, name="all_reduce_small", in_specs=[vm], out_specs=vm, out_shape=jax.ShapeDtypeStruct(vec.shape, F32),
        scratch_shapes=[pltpu.VMEM((N_DEV, rows, LANES), F32), pltpu.SemaphoreType.DMA((N_DEV - 1,)),
                        pltpu.SemaphoreType.DMA((N_DEV - 1,))],
    )(vec)


def _row_tile(rows):
    for tr in (256, 128, 64, 32, 16):
        if rows % tr == 0:
            return tr
    return rows


def _pair_sum(name, place, grad, theirs):
    n, r, c = grad.shape
    half = r // 2
    tr = _row_tile(half)
    nb = half // tr

    def body(place_ref, g_ref, t_ref, o_ref):
        o_ref[...] = (g_ref[...] + t_ref[...]).astype(BF16)

    return pl.pallas_call(
        body, name=name, out_shape=jax.ShapeDtypeStruct((n, half, c), BF16),
        grid_spec=pltpu.PrefetchScalarGridSpec(
            num_scalar_prefetch=1, grid=(n, nb),
            in_specs=[pl.BlockSpec((1, tr, c), lambda j, i, pr: (j, pr[0] * nb + i, 0)),
                      pl.BlockSpec((1, tr, c), lambda j, i, pr: (j, i, 0))],
            out_specs=pl.BlockSpec((1, tr, c), lambda j, i, pr: (j, i, 0))),
        compiler_params=_params("parallel", "parallel"),
    )(place, grad, theirs)


def _sum_chips(name, place, pair_sums, landed):
    _, half, c = pair_sums.shape
    tr = _row_tile(half)
    nb = half // tr

    def body(place_ref, s_ref, q_ref, o_ref):
        total = s_ref[0].astype(F32)
        for p in range(3):
            total = total + q_ref[p].astype(F32)
        o_ref[...] = total

    return pl.pallas_call(
        body, name=name, out_shape=jax.ShapeDtypeStruct((2 * half, c), F32),
        grid_spec=pltpu.PrefetchScalarGridSpec(
            num_scalar_prefetch=1, grid=(nb,),
            in_specs=[pl.BlockSpec((1, tr, c), lambda i, pr: (pr[1], i, 0)),
                      pl.BlockSpec((3, tr, c), lambda i, pr: (0, i, 0))],
            out_specs=pl.BlockSpec((tr, c), lambda i, pr: (pr[0] * nb + i, 0))),
        compiler_params=_params("parallel"),
    )(place, pair_sums, landed)


BIG = ("w_in", "w_branch_a", "w_branch_b", "w_out", "w_ffn_gate", "w_ffn_up", "w_ffn_down", "w_ple_gate", "w_ple_proj")
LATE = BIG[1:]
COLUMN_SHARDED = ("w_in", "w_branch_a", "w_branch_b", "w_ffn_gate", "w_ffn_up", "w_ple_proj")
SMALL = ("norm_mix", "w_pool", "pool_scale", "norm_ffn", "norm_ple", "norm_final")


def _join_columns(w4):
    return jnp.concatenate([w4[j] for j in range(N_CHIPS)], axis=1)


def _split_columns(g):
    k, n = g.shape
    return g.reshape(k, N_CHIPS, n // N_CHIPS).transpose(1, 0, 2)


def _sds(shape, dtype):
    return jax.ShapeDtypeStruct(shape, dtype)


def _local_step(x, p, target, wf, small, gather_late=None, exchange_early=None, exchange_last=None):
    t, d = x.shape
    w_in = wf["w_in"]
    w_pool_b = small["w_pool"].astype(BF16)
    dp = w_pool_b.shape[0] * w_pool_b.shape[1]

    h1 = _norm_fwd("norm_mix", x, small["norm_mix"])
    u, q, kv, ga, gb = _mm(
        "proj", [h1], [w_in[j] for j in range(N_CHIPS)], "nn",
        [_sds((t, dp), F32), _sds((t, dp), BF16), _sds((t, d), BF16), _sds((t, d), BF16), _sds((t, d), BF16)],
        separate=True, epilogue=lambda uq, kv_, ga_, gb_: (uq[:, :dp], uq[:, dp:], kv_, ga_, gb_), tm=512)
    pooled, ya = _pool_fwd(u, w_pool_b, small["pool_scale"])
    n_pairs = dp // LANES
    yb, late = _attn_fwd(q, 0, kv, 0, n_pairs, n_pairs, rider=gather_late or _NoRider())
    wf = {**wf, **dict(zip(LATE, late))}
    w_gate, w_up, w_down = wf["w_ffn_gate"], wf["w_ffn_up"], wf["w_ffn_down"]
    w_a, w_b, w_pp = _join_columns(wf["w_branch_a"]), _join_columns(wf["w_branch_b"]), _join_columns(wf["w_ple_proj"])
    w_out = wf["w_out"].reshape(d, d)
    w_pg = wf["w_ple_gate"].reshape(d, d)
    dff = w_gate.shape[2]
    ta, tb, merged = _mm(
        "branches_merge", [ya, yb], [w_a, w_b], "nn", [_sds((t, d), BF16)] * 3, extras=[ga, gb], separate=True,
        epilogue=lambda tav, tbv, gav, gbv: (tav, tbv, _sigmoid(gav) * tav + _sigmoid(gbv) * tbv), tm=512)
    def residual_norm(acc, xv, g):
        xn = acc + xv
        return xn, xn * lax.rsqrt(jnp.mean(xn * xn, axis=-1, keepdims=True) + RMS_EPS) * g

    x1, h2 = _mm("mix_out", [merged], [w_out], "nn", [_sds((t, d), F32), _sds((t, d), BF16)],
                 extras=[x, small["norm_ffn"]], epilogue=residual_norm)
    gates, ups, acts = [], [], []
    for j in range(N_CHIPS):
        gj, uj, aj = _mm(f"ffn_gate_up{j}", [h2], [w_gate[j], w_up[j]], "nn", [_sds((t, dff), BF16)] * 3,
                         separate=True, epilogue=lambda gv, uv: (gv, uv, gv * _sigmoid(gv) * uv))
        gates.append(gj), ups.append(uj), acts.append(aj)
    x2, h3 = _mm("ffn_down", acts, [w_down[j] for j in range(N_CHIPS)], "nn", [_sds((t, d), F32), _sds((t, d), BF16)],
                 extras=[x1, small["norm_ple"]], epilogue=residual_norm, tm=512)
    gp, pp, x3 = _mm(
        "ple", [h3, p], [w_pg, w_pp], "nn", [_sds((t, d), BF16), _sds((t, d), BF16), _sds((t, d), F32)], extras=[x2],
        separate=True, epilogue=lambda gv, pv, xv: (gv, pv, xv + _sigmoid(gv) * pv), tm=512)
    (dx3, d_pp, d_gp), (d_norm_final, loss_row) = _split2(
        _final_loss(x3, target, small["norm_final"].reshape(1, d), gp, pp), 3)

    def through_norm(dh, xv, g, dres):
        dx, d_gain = _rms_norm_bwd(dh, xv, g)
        return dx + dres, dx + dres, d_gain

    stream = [_sds((t, d), F32), _sds((t, d), BF16)]
    gain_sum = [_sds((1, d), F32)]
    g_w_pp, = _mm_tn("g_ple_proj", p, [d_pp])
    g_w_pg, = _mm_tn("g_ple_gate", h3, [d_gp])
    dx2, dx2_b, d_norm_ple = _mm("d_h3", [d_gp], [w_pg], "nt", stream, extras=[x2, small["norm_ple"], dx3],
                                 epilogue=through_norm, sum_shapes=gain_sum, tm=512)

    def ffn_bwd(acc, gv, uv):
        s = _sigmoid(gv)
        return acc * uv * (s * (1.0 + gv * (1.0 - s))), acc * (gv * s)

    d_gates, d_ups, g_w_gate, g_w_up, g_w_down = [], [], None, None, None
    for j in range(N_CHIPS):
        dgj, duj = _mm(f"d_act{j}", [dx2_b], [w_down[j]], "nt", [_sds((t, dff), BF16), _sds((t, dff), BF16)],
                       extras=[gates[j], ups[j]], epilogue=ffn_bwd)
        d_gates.append(dgj), d_ups.append(duj)
        g_w_down, = _mm_tn(f"g_ffn_down{j}", acts[j], [dx2_b], into=[(g_w_down, j)])
        g_w_gate, g_w_up = _mm_tn(f"g_ffn_gate_up{j}", h2, [dgj, duj], into=[(g_w_gate, j), (g_w_up, j)])
    dx1, dx1_b, d_norm_ffn = _mm(
        "d_h2", d_gates + d_ups, [w_gate[j] for j in range(N_CHIPS)] + [w_up[j] for j in range(N_CHIPS)], "nt",
        stream, extras=[x1, small["norm_ffn"], dx2], epilogue=through_norm, sum_shapes=gain_sum, tm=512)

    def merge_bwd(acc, tav, tbv, gav, gbv):
        sa, sb = _sigmoid(gav), _sigmoid(gbv)
        return acc * sa, acc * sb, acc * tav * sa * (1.0 - sa), acc * tbv * sb * (1.0 - sb)

    d_ta, d_tb, d_ga, d_gb = _mm("d_merged", [dx1_b], [w_out], "nt", [_sds((t, d), BF16)] * 4,
                                 extras=[ta, tb, ga, gb], epilogue=merge_bwd, tm=512)
    g_w_out, = _mm_tn("g_w_out", merged, [dx1_b])
    g_w_a, = _mm_tn("g_branch_a", ya, [d_ta])
    g_w_b, = _mm_tn("g_branch_b", yb, [d_tb])
    d_ya, = _mm("d_ya", [d_ta], [w_a], "nt", [_sds((t, dp), F32)])
    d_yb, = _mm("d_yb", [d_tb], [w_b], "nt", [_sds((t, dp), BF16)])
    d_u, g_w_pool, d_pool_scale = _pool_bwd(d_ya, pooled, w_pool_b, small["pool_scale"])
    big = {
        "w_branch_a": _split_columns(g_w_a), "w_branch_b": _split_columns(g_w_b),
        "w_out": g_w_out.reshape(wf["w_out"].shape), "w_ffn_gate": g_w_gate, "w_ffn_up": g_w_up,
        "w_ffn_down": g_w_down, "w_ple_gate": g_w_pg.reshape(wf["w_ple_gate"].shape),
        "w_ple_proj": _split_columns(g_w_pp),
    }
    rider = exchange_early(big) if exchange_early else _NoRider()
    (d_q, d_k, d_v), early = _attn_bwd(q, 0, kv, 0, n_pairs, d_yb, n_pairs, rider=rider)
    d_proj = [(d_u, d_q), (d_k, d_v), d_ga, d_gb]
    big["w_in"], = _mm_tn("g_w_in", h1, d_proj, tmm=512, stacked=True)
    rider = exchange_last(big["w_in"]) if exchange_last else _NoRider()
    res = _mm(
        "d_h1", d_proj, [w_in[j] for j in range(N_CHIPS)], "nt", [_sds((t, d), F32)],
        extras=[x, small["norm_mix"], dx1], epilogue=lambda dh, xv, g, dres: through_norm(dh, xv, g, dres)[1:],
        sum_shapes=gain_sum, tm=512, rider=rider)
    (grad_x, d_norm_mix), last = res if rider.operands else (res, ())
    small_g = {"norm_mix": d_norm_mix, "w_pool": g_w_pool, "pool_scale": d_pool_scale, "norm_ffn": d_norm_ffn,
               "norm_ple": d_norm_ple, "norm_final": d_norm_final}
    return grad_x, big, small_g, loss_row, early, last


def _split2(res, n):
    return res[:n], res[n:]


def _pack_small(small_g, loss_row):
    parts, layout = [], []
    for name in SMALL + ("loss",):
        v = (loss_row if name == "loss" else small_g[name]).reshape(-1, LANES)
        pad = (-v.shape[0]) % 8
        if pad:
            v = jnp.concatenate([v, jnp.zeros((pad, LANES), F32)], axis=0)
        layout.append((name, sum(q.shape[0] for q in parts), v.shape[0]))
        parts.append(v)
    return jnp.concatenate(parts, axis=0), layout


def kernel(x, p, norm_mix, w_in, w_pool, pool_scale, w_branch_a, w_branch_b, w_out, norm_ffn, w_ffn_gate, w_ffn_up, w_ffn_down, norm_ple, w_ple_gate, w_ple_proj, norm_final, loss_target, m_norm_mix, m_w_in, m_w_pool, m_pool_scale, m_w_branch_a, m_w_branch_b, m_w_out, m_norm_ffn, m_w_ffn_gate, m_w_ffn_up, m_w_ffn_down, m_norm_ple, m_w_ple_gate, m_w_ple_proj, m_norm_final, v_norm_mix, v_w_in, v_w_pool, v_pool_scale, v_w_branch_a, v_w_branch_b, v_w_out, v_norm_ffn, v_w_ffn_gate, v_w_ffn_up, v_w_ffn_down, v_norm_ple, v_w_ple_gate, v_w_ple_proj, v_norm_final):
    given = dict(locals())
    names = BIG + SMALL
    order = ("norm_mix", "w_in", "w_pool", "pool_scale", "w_branch_a", "w_branch_b", "w_out", "norm_ffn", "w_ffn_gate",
             "w_ffn_up", "w_ffn_down", "norm_ple", "w_ple_gate", "w_ple_proj", "norm_final")
    t, d = x.shape[1], x.shape[2]
    shard = {n: given[n][0] for n in BIG}
    small = {"norm_mix": norm_mix, "w_pool": w_pool[0], "pool_scale": pool_scale, "norm_ffn": norm_ffn,
             "norm_ple": norm_ple, "norm_final": norm_final}

    as_bf16 = {n: shard[n].astype(BF16) for n in BIG}
    wf = {"w_in": _run_exchange("gather_w_in", _WeightGather([as_bf16["w_in"]]))[0]}

    place = jnp.stack([lax.axis_index("c"), 2 * lax.axis_index("x") + lax.axis_index("y")]).astype(jnp.int32)
    pair_sums = {}

    def exchange_early(ready):
        theirs = _pair_exchange("pair_exchange_early", [ready[n] for n in LATE])
        for n, other in zip(LATE, theirs):
            pair_sums[n] = _pair_sum(f"pair_sum_{n}", place, ready[n], other)
        return _ChipExchange([pair_sums[n] for n in LATE])

    def exchange_last(g_w_in):
        theirs, = _pair_exchange("pair_exchange_w_in", [g_w_in])
        pair_sums["w_in"] = _pair_sum("pair_sum_w_in", place, g_w_in, theirs)
        return _ChipExchange([pair_sums["w_in"]])

    grad_x, big_g, small_g, loss_row, early, last = _local_step(
        x.reshape(t, d), p.reshape(t, p.shape[-1]), loss_target.reshape(t, d), wf, small,
        gather_late=_WeightGather([as_bf16[n] for n in LATE]), exchange_early=exchange_early,
        exchange_last=exchange_last)
    landed = dict(zip(LATE + ("w_in",), tuple(early) + tuple(last)))
    halves = [_sum_chips(f"chip_sum_{n}", place, pair_sums[n], landed[n]) for n in BIG]
    grads = dict(zip(BIG, _pair_share(halves)))

    packed, layout = _pack_small(small_g, loss_row)
    reduced = _all_reduce_small(packed)
    for name, start, rows in layout:
        if name == "loss":
            loss = jnp.sum(reduced[start:start + rows])
        else:
            n_el = small[name].size
            grads[name] = reduced[start:start + rows].reshape(-1)[:n_el]

    deltas, new_m, new_v = {}, {}, {}
    for n in order:
        w = shard[n] if n in BIG else small[n]
        shape2 = w.shape if w.ndim == 2 else ((1, w.shape[0]) if w.ndim == 1 else (w.shape[0] * w.shape[1], w.shape[2]))
        g2 = grads[n].reshape(shape2)
        dl, mn, vn = _adamw(f"adamw_{n}", w.reshape(shape2), g2, given["m_" + n].reshape(shape2),
                            given["v_" + n].reshape(shape2))
        full = given[n].shape
        grads[n], deltas[n], new_m[n], new_v[n] = g2.reshape(full), dl.reshape(full), mn.reshape(full), vn.reshape(full)

    return (loss, grad_x.reshape(x.shape), *[grads[n] for n in order], *[deltas[n] for n in order],
            *[new_m[n] for n in order], *[new_v[n] for n in order])
```

```python
import functools
import math

import jax
import jax.numpy as jnp
from jax import lax
from jax.experimental import pallas as pl
from jax.experimental.pallas import tpu as pltpu

F32 = jnp.float32
BF16 = jnp.bfloat16
MESH = pl.DeviceIdType.MESH

RMS_EPS = 1e-6
POOL_WINDOWS = (2, 4, 8, 16)
POOL_HALO = 16
HEAD_DIM = 64
LANES = 128
ATT_BLOCK = 256
ATT_CHUNK = 256
ATT_SLAB = 256
ATT_SCALE = 1.0 / math.sqrt(HEAD_DIM)
LOG2_E = 1.4426950408889634
ATT_EXIT_BELOW = -150.5
ADAM_LR, ADAM_B1, ADAM_B2, ADAM_EPS, ADAM_WD, ADAM_STEP = 0.001, 0.9, 0.999, 1e-08, 0.01, 10
V7X_VMEM_LIMIT_BYTES = 56 * 1024 * 1024
N_CHIPS = 4
N_DEV = 8


def _params(*semantics):
    return pltpu.CompilerParams(dimension_semantics=semantics, vmem_limit_bytes=V7X_VMEM_LIMIT_BYTES)


def _sigmoid(z):
    return 1.0 / (1.0 + jnp.exp(-z))


def _tiled_spec(shape, tm, tn, n_total, at):
    rows, width = shape
    if rows == 1:
        if width == n_total:
            return pl.BlockSpec((1, tn), at(lambda i, j: (0, j)))
        return pl.BlockSpec((1, width), at(lambda i, j: (0, 0)))
    if width == n_total:
        return pl.BlockSpec((tm, tn), at(lambda i, j: (i, j)))
    assert tn == n_total, "an operand narrower than the output needs whole output rows per tile"
    return pl.BlockSpec((tm, width), at(lambda i, j: (i, 0)))


def _column_pieces(operands):
    pieces = [tuple(a) if isinstance(a, (tuple, list)) else (a,) for a in operands]
    return [p for ps in pieces for p in ps], [len(ps) for ps in pieces]


def _load_bf16(refs, counts):
    tiles, k = [], 0
    for n in counts:
        parts = [r[...] for r in refs[k:k + n]]
        parts = [t if t.dtype == BF16 else t.astype(BF16) for t in parts]
        tiles.append(parts[0] if n == 1 else jnp.concatenate(parts, axis=1))
        k += n
    return tiles


def _mm(name, a_list, b_list, mode, out_shapes, epilogue=None, extras=(), tm=1024, tn=None, separate=False,
        sum_shapes=(), rider=None):
    flat_a, counts = _column_pieces(a_list)
    m_total = flat_a[0].shape[0]
    n_total = b_list[0].shape[1] if mode == "nn" else b_list[0].shape[0]
    tn = n_total if tn is None else tn
    tm = min(tm, m_total)
    assert m_total % tm == 0 and n_total % tn == 0 and (not sum_shapes or tn == n_total)
    n_a, n_b, n_extra, n_out = len(counts), len(b_list), len(extras), len(out_shapes)
    assert n_a in (1, n_b)
    dims = (((1,), (0,)), ((), ())) if mode == "nn" else (((1,), (1,)), ((), ()))
    rider = rider or _NoRider()
    grid = (n_total // tn, m_total // tm)

    def at(index):
        return lambda j, i: index(i, j)

    def body(*refs):
        ins, o_refs, _, riding = rider.split(refs, len(flat_a) + n_b + n_extra, n_out + len(sum_shapes))
        a_refs, b_refs, e_refs = ins[:len(flat_a)], ins[len(flat_a):len(flat_a) + n_b], ins[len(flat_a) + n_b:]
        at_first = (pl.program_id(0) == 0) & (pl.program_id(1) == 0)
        at_last = (pl.program_id(0) == grid[0] - 1) & (pl.program_id(1) == grid[1] - 1)
        top, bottom = rider.at_steps(riding, at_first, at_first, at_last)
        top()
        lefts = _load_bf16(a_refs, counts)
        products = [lax.dot_general(lefts[s % n_a], b_refs[s][...], dims, preferred_element_type=F32)
                    for s in range(n_b)]
        if not separate:
            products = [functools.reduce(lambda p, r: p + r, products)]
        extra_tiles = [e[...].astype(F32) for e in e_refs]
        outs = products if epilogue is None else epilogue(*products, *extra_tiles)
        for o_ref, o in zip(o_refs[:n_out], outs[:n_out]):
            o_ref[...] = o.astype(o_ref.dtype)
        if sum_shapes:
            @pl.when(pl.program_id(1) == 0)
            def _():
                for s_ref in o_refs[n_out:]:
                    s_ref[...] = jnp.zeros_like(s_ref)

            for s_ref, s in zip(o_refs[n_out:], outs[n_out:]):
                s_ref[...] += s
        bottom()

    once = dict(pipeline_mode=pl.Buffered(1)) if tn == n_total else {}
    in_specs = [pl.BlockSpec((tm, a.shape[1]), at(lambda i, j: (i, 0))) for a in flat_a]
    if mode == "nn":
        in_specs += [pl.BlockSpec((b.shape[0], tn), at(lambda i, j: (0, j)), **once) for b in b_list]
    else:
        in_specs += [pl.BlockSpec((tn, b.shape[1]), at(lambda i, j: (j, 0)), **once) for b in b_list]
    in_specs += [_tiled_spec(e.shape, tm, tn, n_total, at) for e in extras]
    out_specs = [_tiled_spec(o.shape, tm, tn, n_total, at) for o in out_shapes]
    out_specs += [pl.BlockSpec(s.shape, at(lambda i, j: (0, 0))) for s in sum_shapes]
    semantics = ("arbitrary", "arbitrary") if sum_shapes or rider.operands else ("parallel", "parallel")
    res = pl.pallas_call(
        body, name=name, grid=grid, in_specs=in_specs + [ANY] * len(rider.operands),
        out_specs=out_specs + [ANY] * len(rider.out_shapes),
        out_shape=list(out_shapes) + list(sum_shapes) + list(rider.out_shapes), scratch_shapes=list(rider.scratch),
        compiler_params=_params(*semantics),
    )(*flat_a, *b_list, *extras, *rider.operands)
    n_own = len(out_shapes) + len(sum_shapes)
    return res if not rider.operands else (res[:n_own], res[n_own:])


def _mm_tn(name, a, b_list, tmm=1024, stacked=False, n_blocks=1):
    flat_b, counts = _column_pieces(b_list)
    m_total, k = a.shape
    widths = [sum(p.shape[1] for p in flat_b[sum(counts[:s]):sum(counts[:s + 1])]) for s in range(len(counts))]
    tmm = min(tmm, m_total)
    assert m_total % tmm == 0 and (n_blocks == 1 or max(counts) == 1) and all(w % n_blocks == 0 for w in widths)
    n_b = len(counts)

    def body(*refs):
        a_ref, b_refs, o_refs = refs[0], refs[1:1 + len(flat_b)], refs[1 + len(flat_b):]

        @pl.when(pl.program_id(1) == 0)
        def _():
            for o_ref in o_refs:
                o_ref[...] = jnp.zeros_like(o_ref)

        av, = _load_bf16([a_ref], [1])
        for s, bv in enumerate(_load_bf16(b_refs, counts)):
            product = lax.dot_general(av, bv, (((0,), (0,)), ((), ())), preferred_element_type=F32)
            if stacked:
                o_refs[0][s] += product
            else:
                o_refs[s][...] += product

    in_specs = [pl.BlockSpec((tmm, k), lambda nb, m: (m, 0))]
    in_specs += [pl.BlockSpec((tmm, b.shape[1] // n_blocks), lambda nb, m: (m, nb)) for b in flat_b]
    if stacked:
        out_shape = [jax.ShapeDtypeStruct((n_b, k, widths[0]), F32)]
        out_specs = [pl.BlockSpec((n_b, k, widths[0] // n_blocks), lambda nb, m: (0, 0, nb))]
    else:
        out_shape = [jax.ShapeDtypeStruct((k, w), F32) for w in widths]
        out_specs = [pl.BlockSpec((k, w // n_blocks), lambda nb, m: (0, nb)) for w in widths]
    return pl.pallas_call(
        body, name=name, grid=(n_blocks, m_total // tmm), in_specs=in_specs, out_specs=out_specs, out_shape=out_shape,
        compiler_params=_params("arbitrary", "arbitrary"),
    )(a, *flat_b)


def _rows(name, fn, ins, tile_outs, sum_outs=(), tr=512):
    t_total = max(a.shape[0] for a in ins)
    tr = min(tr, t_total)
    assert t_total % tr == 0
    n_in, n_tile = len(ins), len(tile_outs)

    def body(*refs):
        outs = fn(*[r[...].astype(F32) for r in refs[:n_in]])
        for o_ref, o in zip(refs[n_in:n_in + n_tile], outs[:n_tile]):
            o_ref[...] = o.astype(o_ref.dtype)
        if sum_outs:
            @pl.when(pl.program_id(0) == 0)
            def _():
                for s_ref in refs[n_in + n_tile:]:
                    s_ref[...] = jnp.zeros_like(s_ref)

            for s_ref, s in zip(refs[n_in + n_tile:], outs[n_tile:]):
                s_ref[...] += s

    def spec(shape):
        if shape[0] == 1:
            return pl.BlockSpec(shape, lambda i: (0, 0))
        return pl.BlockSpec((tr, shape[1]), lambda i: (i, 0))

    return pl.pallas_call(
        body, name=name, grid=(t_total // tr,), in_specs=[spec(a.shape) for a in ins],
        out_specs=[spec(o.shape) for o in tile_outs] + [spec(s.shape) for s in sum_outs],
        out_shape=list(tile_outs) + list(sum_outs),
        compiler_params=_params("arbitrary" if sum_outs else "parallel"),
    )(*ins)


def _norm_fwd(name, x, gain):
    def fn(xv, g):
        inv = lax.rsqrt(jnp.mean(xv * xv, axis=-1, keepdims=True) + RMS_EPS)
        return (xv * inv * g,)

    return _rows(name, fn, [x, gain], [jax.ShapeDtypeStruct(x.shape, BF16)])[0]


def _rms_norm_bwd(dh, xv, g):
    inv = lax.rsqrt(jnp.mean(xv * xv, axis=-1, keepdims=True) + RMS_EPS)
    xn = xv * inv
    dxn = dh * g
    return inv * (dxn - xn * jnp.mean(dxn * xn, axis=-1, keepdims=True)), jnp.sum(dh * xn, axis=0, keepdims=True)


def _final_loss(x3, target, gain, gp, pp):
    d = x3.shape[1]

    def fn(xv, tv, g, gv, pv):
        inv = lax.rsqrt(jnp.mean(xv * xv, axis=-1, keepdims=True) + RMS_EPS)
        err = xv * inv * g - tv
        dx, d_gain = _rms_norm_bwd(err * (1.0 / d), xv, g)
        s = _sigmoid(gv)
        return dx, dx * s, dx * pv * s * (1.0 - s), d_gain, (0.5 / d) * jnp.sum(err * err, axis=0, keepdims=True)

    act = jax.ShapeDtypeStruct(x3.shape, BF16)
    return _rows("final_loss", fn, [x3, target, gain, gp, pp], [jax.ShapeDtypeStruct(x3.shape, F32), act, act],
                 [jax.ShapeDtypeStruct((1, d), F32), jax.ShapeDtypeStruct((1, d), F32)])


def _window_counts(t_pos, w):
    return jnp.minimum(t_pos + 1, w).astype(F32)


def _pool_fwd(u, w_pool, scale, tr=512):
    t_total, width = u.shape
    tr = min(tr, t_total)
    n_groups = len(POOL_WINDOWS)
    gdim = width // n_groups
    ext = tr + POOL_HALO

    def body(u_ref, halo_ref, w_ref, s_ref, pooled_ref, ya_ref):
        i = pl.program_id(0)
        halo = jnp.where(i == 0, 0.0, halo_ref[...])
        t_pos = i * tr + lax.broadcasted_iota(jnp.int32, (tr, 1), 0)
        for g, w in enumerate(POOL_WINDOWS):
            cols = slice(g * gdim, (g + 1) * gdim)
            main = u_ref[:, cols]
            win = jnp.concatenate([halo[:, cols], main], axis=0)
            span = 1
            while span < w:
                win = win + pltpu.roll(win, span, 0)
                span *= 2
            pooled = win[POOL_HALO:, :] * (1.0 / _window_counts(t_pos, w)) - main
            pooled_b = pooled.astype(BF16)
            pooled_ref[:, cols] = pooled_b
            mixed = jnp.dot(pooled_b, w_ref[g], preferred_element_type=F32)
            ya_ref[:, cols] = (mixed * s_ref[:, cols]).astype(BF16)

    hb = tr // POOL_HALO
    return pl.pallas_call(
        body, name="pool_fwd", grid=(t_total // tr,),
        in_specs=[pl.BlockSpec((tr, width), lambda i: (i, 0)),
                  pl.BlockSpec((POOL_HALO, width), lambda i: (jnp.maximum(i * hb - 1, 0), 0)),
                  pl.BlockSpec((n_groups, gdim, gdim), lambda i: (0, 0, 0)),
                  pl.BlockSpec((1, width), lambda i: (0, 0))],
        out_specs=[pl.BlockSpec((tr, width), lambda i: (i, 0)), pl.BlockSpec((tr, width), lambda i: (i, 0))],
        out_shape=[jax.ShapeDtypeStruct(u.shape, BF16), jax.ShapeDtypeStruct(u.shape, BF16)],
        compiler_params=_params("parallel"),
    )(u, u, w_pool, scale)


def _pool_bwd(dya, pooled, w_pool, scale, tr=512):
    t_total, width = dya.shape
    tr = min(tr, t_total)
    n_groups = len(POOL_WINDOWS)
    gdim = width // n_groups
    ext = tr + POOL_HALO
    n_tiles = t_total // tr

    def body(d_ref, halo_ref, p_ref, w_ref, s_ref, du_ref, dw_ref, ds_ref):
        i = pl.program_id(0)

        @pl.when(i == 0)
        def _():
            dw_ref[...] = jnp.zeros_like(dw_ref)
            ds_ref[...] = jnp.zeros_like(ds_ref)

        halo = jnp.where(i == n_tiles - 1, 0.0, halo_ref[...])
        t_pos = i * tr + lax.broadcasted_iota(jnp.int32, (ext, 1), 0)
        for g, w in enumerate(POOL_WINDOWS):
            cols = slice(g * gdim, (g + 1) * gdim)
            sc = s_ref[:, cols]
            d_main = d_ref[:, cols]
            pooled_b = p_ref[:, cols]
            mixed = jnp.dot(pooled_b, w_ref[g], preferred_element_type=F32)
            ds_ref[:, cols] += jnp.sum(d_main * mixed, axis=0, keepdims=True)
            dmix = (jnp.concatenate([d_main, halo[:, cols]], axis=0) * sc).astype(BF16)
            dw_ref[g] += lax.dot_general(pooled_b, dmix[:tr, :], (((0,), (0,)), ((), ())),
                                         preferred_element_type=F32)
            dpool = lax.dot_general(dmix, w_ref[g], (((1,), (1,)), ((), ())), preferred_element_type=F32)
            win = dpool * (1.0 / _window_counts(t_pos, w))
            span = 1
            while span < w:
                win = win + pltpu.roll(win, ext - span, 0)
                span *= 2
            du_ref[:, cols] = (win[:tr, :] - dpool[:tr, :]).astype(BF16)

    hb = tr // POOL_HALO
    last_halo = t_total // POOL_HALO - 1
    return pl.pallas_call(
        body, name="pool_bwd", grid=(n_tiles,),
        in_specs=[pl.BlockSpec((tr, width), lambda i: (i, 0)),
                  pl.BlockSpec((POOL_HALO, width), lambda i: (jnp.minimum((i + 1) * hb, last_halo), 0)),
                  pl.BlockSpec((tr, width), lambda i: (i, 0)),
                  pl.BlockSpec((n_groups, gdim, gdim), lambda i: (0, 0, 0)),
                  pl.BlockSpec((1, width), lambda i: (0, 0))],
        out_specs=[pl.BlockSpec((tr, width), lambda i: (i, 0)),
                   pl.BlockSpec((n_groups, gdim, gdim), lambda i: (0, 0, 0)),
                   pl.BlockSpec((1, width), lambda i: (0, 0))],
        out_shape=[jax.ShapeDtypeStruct(dya.shape, BF16), jax.ShapeDtypeStruct((n_groups, gdim, gdim), F32),
                   jax.ShapeDtypeStruct((1, width), F32)],
        compiler_params=_params("arbitrary"),
    )(dya, dya, pooled, w_pool, scale)


def _head_masks():
    lane = lax.broadcasted_iota(jnp.int32, (1, LANES), 1)
    return lane < HEAD_DIM


def _stack_heads(tile, first):
    zero = jnp.zeros_like(tile)
    return jnp.concatenate([jnp.where(first, tile, zero), jnp.where(first, zero, tile)], axis=0)


def _split_bf16(v):
    hi = v.astype(BF16)
    lo = (v - hi.astype(F32)).astype(BF16)
    return hi, lo


def _causal_mask(t_pos, k_start):
    col = lax.broadcasted_iota(jnp.int32, (1, 2 * ATT_SLAB), 1)
    return k_start + (col & (ATT_SLAB - 1)) < t_pos


def _slab_scores(q, kd, mask):
    z2 = lax.dot_general(q, kd, (((1,), (1,)), ((), ())), preferred_element_type=F32) * LOG2_E
    log_fail = -(jnp.maximum(z2, 0.0) + jnp.log2(1.0 + jnp.exp2(-jnp.abs(z2))))
    return z2, (log_fail if mask is None else jnp.where(mask, log_fail, 0.0))


def _weights(z2, log_fail, suffix, mask):
    arg = z2 + log_fail + suffix
    return jnp.exp2(arg if mask is None else jnp.where(mask, arg, -1e30))


def _tri(upper):
    r = lax.broadcasted_iota(jnp.int32, (2 * ATT_CHUNK, ATT_CHUNK), 0) & (ATT_CHUNK - 1)
    c = lax.broadcasted_iota(jnp.int32, (2 * ATT_CHUNK, ATT_CHUNK), 1)
    return jnp.where(r > c if upper else r < c, 1.0, 0.0).astype(BF16)


def _scan_chunk(v, tri):
    return jnp.dot(jnp.concatenate(_split_bf16(v), axis=1), tri, preferred_element_type=F32)


def _lane_bcast(col):
    return jnp.broadcast_to(col, (col.shape[0], LANES))


def _scan_slab(v, tri, carries, from_right):
    n_chunks = ATT_SLAB // ATT_CHUNK
    edge = 0 if from_right else ATT_CHUNK - 1
    parts, new_carries = [None] * (2 * n_chunks), []
    for head in range(2):
        run = carries[head]
        for c in (reversed(range(n_chunks)) if from_right else range(n_chunks)):
            lo_col = head * ATT_SLAB + c * ATT_CHUNK
            vc = v[:, lo_col:lo_col + ATT_CHUNK]
            sc = _scan_chunk(vc, tri)
            parts[head * n_chunks + c] = sc + jnp.concatenate([run] * (ATT_CHUNK // LANES), axis=1)
            run = run + _lane_bcast(sc[:, edge:edge + 1] + vc[:, edge:edge + 1])
        new_carries.append(run)
    return jnp.concatenate(parts, axis=1), new_carries


def _fold_heads(stacked, first):
    s = stacked.shape[0] // 2
    return jnp.where(first, stacked[:s], stacked[s:])


class _NoRider:
    operands, out_shapes, scratch = (), (), ()

    def split(self, refs, n_base_in, n_base_out):
        n_in, n_out, n_sem = len(self.operands), len(self.out_shapes), len(self.scratch)
        a = n_base_in + n_in
        b = a + n_base_out + n_out
        mine = (refs[n_base_in:a], refs[a + n_base_out:b], refs[b:b + n_sem])
        return refs[:n_base_in], refs[a:a + n_base_out], refs[b + n_sem:], mine

    def start(self, ins, outs, sems):
        pass

    def relay(self, ins, outs, sems):
        pass

    def finish(self, ins, outs, sems):
        pass

    def at_steps(self, refs, first_step, relay_step, last_step):
        if not self.operands:
            return (lambda: None), (lambda: None)

        def top():
            pl.when(first_step)(lambda: self.start(*refs))
            pl.when(relay_step)(lambda: self.relay(*refs))

        return top, lambda: pl.when(last_step)(lambda: self.finish(*refs))


def _attn_fwd(q_src, q_col, kv_src, k_col, v_col, n_pairs=4, rider=_NoRider()):
    t_total = q_src.shape[0]
    blk = ATT_BLOCK
    n_blocks = t_total // blk
    assert t_total % ATT_SLAB == 0 and ATT_SLAB % ATT_BLOCK == 0

    def body(*refs):
        (q_ref, k_ref, v_ref), (o_ref,), _, riding = rider.split(refs, 3, 1)
        h, i = pl.program_id(0), pl.program_id(1)
        top, bottom = rider.at_steps(riding, (h == 0) & (i == 0), (h == n_pairs - 1) & (i == 0),
                                     (h == n_pairs - 1) & (i == n_blocks - 1))
        top()
        first = _head_masks()
        q = q_ref[...] * ATT_SCALE
        t_pos = i * blk + lax.broadcasted_iota(jnp.int32, (blk, 1), 0)
        suffix_tri = _tri(upper=True)

        def more(state):
            slab, reach = state[0], state[1]
            return jnp.logical_and(slab >= 0, reach > ATT_EXIT_BELOW)

        def step(state, on_diagonal):
            slab, _, acc, right_a, right_b = state
            k_start = pl.multiple_of(slab * ATT_SLAB, ATT_SLAB)
            kd = _stack_heads(k_ref[pl.ds(k_start, ATT_SLAB), :], first)
            vd = _stack_heads(v_ref[pl.ds(k_start, ATT_SLAB), :], first)
            mask = _causal_mask(t_pos, k_start) if on_diagonal else None
            z2, log_fail = _slab_scores(q, kd, mask)
            suffix, (right_a, right_b) = _scan_slab(log_fail, suffix_tri, (right_a, right_b), from_right=True)
            a = _weights(z2, log_fail, suffix, mask).astype(BF16)
            acc = acc + jnp.dot(a, vd, preferred_element_type=F32)
            return slab - 1, jnp.max(jnp.maximum(right_a, right_b)), acc, right_a, right_b

        zero = jnp.zeros((blk, LANES), F32)
        state = step(((i * blk) // ATT_SLAB, jnp.float32(0.0), zero, zero, zero), on_diagonal=True)
        state = lax.while_loop(more, functools.partial(step, on_diagonal=False), state)
        o_ref[...] = state[2].astype(BF16)
        bottom()

    res = pl.pallas_call(
        body, name="attn_fwd", grid=(n_pairs, n_blocks),
        in_specs=[pl.BlockSpec((blk, LANES), lambda h, i: (i, q_col + h)),
                  pl.BlockSpec((t_total, LANES), lambda h, i: (0, k_col + h)),
                  pl.BlockSpec((t_total, LANES), lambda h, i: (0, v_col + h))] + [ANY] * len(rider.operands),
        out_specs=[pl.BlockSpec((blk, LANES), lambda h, i: (i, h))] + [ANY] * len(rider.out_shapes),
        out_shape=[jax.ShapeDtypeStruct((t_total, n_pairs * LANES), BF16)] + list(rider.out_shapes),
        scratch_shapes=list(rider.scratch),
        compiler_params=_params("arbitrary", "arbitrary"),
    )(q_src, kv_src, kv_src, *rider.operands)
    return res[0], res[1:]


def _attn_bwd(q_src, q_col, kv_src, k_col, v_col, dy, n_pairs=4, rider=_NoRider()):
    t_total = q_src.shape[0]
    blk = ATT_BLOCK
    n_blocks = t_total // blk
    n_slabs = t_total // ATT_SLAB
    assert t_total % ATT_SLAB == 0 and ATT_SLAB % ATT_BLOCK == 0

    def body(*refs):
        (q_ref, dy_ref, k_ref, v_ref), (dq_ref, dk_ref, dv_ref), (g_s, dk_acc, dv_acc), riding = rider.split(refs, 4, 3)
        h, i = pl.program_id(0), pl.program_id(1)
        top, bottom = rider.at_steps(riding, (h == 0) & (i == 0), (h == n_pairs - 1) & (i == 0),
                                     (h == n_pairs - 1) & (i == n_blocks - 1))
        top()

        @pl.when(i == 0)
        def _():
            dk_acc[...] = jnp.zeros_like(dk_acc)
            dv_acc[...] = jnp.zeros_like(dv_acc)

        first = _head_masks()
        q = q_ref[...] * ATT_SCALE
        dy = dy_ref[...]
        t_pos = i * blk + lax.broadcasted_iota(jnp.int32, (blk, 1), 0)
        suffix_tri = _tri(upper=True)
        prefix_tri = _tri(upper=False)
        diag = (i * blk) // ATT_SLAB

        def more(state):
            slab, reach = state[0], state[1]
            return jnp.logical_and(slab >= 0, reach > ATT_EXIT_BELOW)

        def sweep1(state, on_diagonal):
            slab, _, right_a, right_b = state
            k_start = pl.multiple_of(slab * ATT_SLAB, ATT_SLAB)
            kd = _stack_heads(k_ref[pl.ds(k_start, ATT_SLAB), :], first)
            vd = _stack_heads(v_ref[pl.ds(k_start, ATT_SLAB), :], first)
            mask = _causal_mask(t_pos, k_start) if on_diagonal else None
            z2, log_fail = _slab_scores(q, kd, mask)
            suffix, (right_a, right_b) = _scan_slab(log_fail, suffix_tri, (right_a, right_b), from_right=True)
            a = _weights(z2, log_fail, suffix, mask)
            da = lax.dot_general(dy, vd, (((1,), (1,)), ((), ())), preferred_element_type=F32)
            g_s[slab] = da * a
            dv_acc[pl.ds(k_start, ATT_SLAB), :] += _fold_heads(lax.dot_general(
                a.astype(BF16), dy, (((0,), (0,)), ((), ())), preferred_element_type=F32), first)
            return slab - 1, jnp.max(jnp.maximum(right_a, right_b)), right_a, right_b

        zero = jnp.zeros((blk, LANES), F32)
        state = sweep1((diag, jnp.float32(0.0), zero, zero), on_diagonal=True)
        end = lax.while_loop(more, functools.partial(sweep1, on_diagonal=False), state)[0]

        def sweep2(slab, carry, on_diagonal):
            dq, left_a, left_b = carry
            k_start = pl.multiple_of(slab * ATT_SLAB, ATT_SLAB)
            kd = _stack_heads(k_ref[pl.ds(k_start, ATT_SLAB), :], first)
            g = g_s[slab]
            z2 = lax.dot_general(q, kd, (((1,), (1,)), ((), ())), preferred_element_type=F32) * LOG2_E
            sig = 1.0 / (1.0 + jnp.exp2(-z2))
            prefix, (left_a, left_b) = _scan_slab(g, prefix_tri, (left_a, left_b), from_right=False)
            dz = g * (1.0 - sig) - sig * prefix
            if on_diagonal:
                dz = jnp.where(_causal_mask(t_pos, k_start), dz, 0.0)
            dz = dz.astype(BF16)
            dq = dq + jnp.dot(dz, kd, preferred_element_type=F32)
            dk_acc[pl.ds(k_start, ATT_SLAB), :] += _fold_heads(lax.dot_general(
                dz, q, (((0,), (0,)), ((), ())), preferred_element_type=F32), first)
            return dq, left_a, left_b

        carry = lax.fori_loop(end + 1, diag, functools.partial(sweep2, on_diagonal=False), (zero, zero, zero))
        dq = sweep2(diag, carry, on_diagonal=True)[0]
        dq_ref[...] = (dq * ATT_SCALE).astype(BF16)

        @pl.when(i == n_blocks - 1)
        def _():
            dk_ref[...] = dk_acc[...].astype(BF16)
            dv_ref[...] = dv_acc[...].astype(BF16)

        bottom()

    out = jax.ShapeDtypeStruct((t_total, n_pairs * LANES), BF16)
    res = pl.pallas_call(
        body, name="attn_bwd", grid=(n_pairs, n_blocks),
        in_specs=[pl.BlockSpec((blk, LANES), lambda h, i: (i, q_col + h)),
                  pl.BlockSpec((blk, LANES), lambda h, i: (i, h)),
                  pl.BlockSpec((t_total, LANES), lambda h, i: (0, k_col + h)),
                  pl.BlockSpec((t_total, LANES), lambda h, i: (0, v_col + h))] + [ANY] * len(rider.operands),
        out_specs=[pl.BlockSpec((blk, LANES), lambda h, i: (i, h)),
                   pl.BlockSpec((t_total, LANES), lambda h, i: (0, h)),
                   pl.BlockSpec((t_total, LANES), lambda h, i: (0, h))] + [ANY] * len(rider.out_shapes),
        out_shape=[out, out, out] + list(rider.out_shapes),
        scratch_shapes=list(rider.scratch) + [pltpu.VMEM((n_slabs, blk, 2 * ATT_SLAB), F32),
                                              pltpu.VMEM((t_total, LANES), F32), pltpu.VMEM((t_total, LANES), F32)],
        compiler_params=_params("arbitrary", "arbitrary"),
    )(q_src, dy, kv_src, kv_src, *rider.operands)
    return res[:3], res[3:]


def _adamw(name, w, g, m, v):
    def fn(wv, gv, mv, vv):
        mn = ADAM_B1 * mv + (1.0 - ADAM_B1) * gv
        vn = ADAM_B2 * vv + (1.0 - ADAM_B2) * (gv * gv)
        m_hat = mn / (1.0 - ADAM_B1 ** ADAM_STEP)
        v_hat = vn / (1.0 - ADAM_B2 ** ADAM_STEP)
        return -ADAM_LR * (m_hat / (jnp.sqrt(v_hat) + ADAM_EPS) + ADAM_WD * wv), mn, vn

    rows = w.shape[0]
    tr = _row_tile(rows)
    shp = jax.ShapeDtypeStruct(w.shape, F32)
    if rows == 1:
        def body(w_ref, g_ref, m_ref, v_ref, d_ref, mo_ref, vo_ref):
            d, mn, vn = fn(w_ref[...], g_ref[...], m_ref[...], v_ref[...])
            d_ref[...], mo_ref[...], vo_ref[...] = d, mn, vn

        return pl.pallas_call(body, name=name, out_shape=[shp, shp, shp])(w, g, m, v)
    return _rows(name, fn, [w, g, m, v], [shp, shp, shp], tr=tr)


def _place():
    return lax.axis_index("x"), lax.axis_index("y"), lax.axis_index("c")


def _other_chips(x, y):
    return [(1 - x, y), (x, 1 - y), (1 - x, 1 - y)]


ANY = pl.BlockSpec(memory_space=pl.ANY)


def _remote(src, dst, send_sem, recv_sem, to):
    return pltpu.make_async_remote_copy(src_ref=src, dst_ref=dst, send_sem=send_sem, recv_sem=recv_sem,
                                        device_id=to, device_id_type=MESH)


class _WeightGather(_NoRider):
    def __init__(self, shards):
        n_w = len(shards)
        self.operands = list(shards)
        self.out_shapes = [jax.ShapeDtypeStruct((N_CHIPS,) + s.shape, s.dtype) for s in shards]
        self.scratch = [pltpu.SemaphoreType.DMA((3, n_w))] * 4 + [pltpu.SemaphoreType.DMA((n_w,))] * 2

    def _copies(self, ins, outs, sems):
        send_sems, recv_sems, relay_send, relay_recv, own_send, own_recv = sems
        x, y, c = _place()
        my_chip, sibling = 2 * x + y, (x, y, 1 - c)
        n_w = len(ins)

        def half(w, chip, core):
            h = self.operands[w].shape[0] // 2
            return outs[w].at[chip, pl.ds(core * h, h)]

        own = [_remote(ins[w], outs[w].at[my_chip], own_send.at[w], own_recv.at[w], sibling) for w in range(n_w)]
        sends, landed, relays, relayed = [], [], [], []
        for p, (ox, oy) in enumerate(_other_chips(x, y)):
            for w in range(n_w):
                h = self.operands[w].shape[0] // 2
                sends.append(_remote(ins[w].at[pl.ds(c * h, h)], half(w, my_chip, c), send_sems.at[p, w],
                                     recv_sems.at[p, w], (ox, oy, c)))
                here = half(w, 2 * ox + oy, c)
                landed.append(_remote(here, here, send_sems.at[p, w], recv_sems.at[p, w], (ox, oy, c)))
                relays.append(_remote(here, here, relay_send.at[p, w], relay_recv.at[p, w], sibling))
                there = half(w, 2 * ox + oy, 1 - c)
                relayed.append(_remote(there, there, relay_send.at[p, w], relay_recv.at[p, w], sibling))
        return own, sends, landed, relays, relayed

    def start(self, ins, outs, sems):
        own, sends, _, _, _ = self._copies(ins, outs, sems)
        for cp in own + sends:
            cp.start()

    def relay(self, ins, outs, sems):
        _, _, landed, relays, _ = self._copies(ins, outs, sems)
        for arrival, cp in zip(landed, relays):
            arrival.wait_recv()
            cp.start()

    def finish(self, ins, outs, sems):
        own, sends, _, relays, relayed = self._copies(ins, outs, sems)
        for arrival in relayed:
            arrival.wait_recv()
        for cp in sends + relays:
            cp.wait_send()
        for cp in own:
            cp.wait()


class _ChipExchange(_NoRider):
    def __init__(self, pair_sums):
        n_w = len(pair_sums)
        self.operands = list(pair_sums)
        self.out_shapes = [jax.ShapeDtypeStruct((3,) + s.shape[1:], s.dtype) for s in pair_sums]
        self.scratch = [pltpu.SemaphoreType.DMA((3, n_w))] * 2

    def _copies(self, ins, outs, sems):
        send_sems, recv_sems = sems
        x, y, c = _place()
        return [_remote(ins[w].at[2 * ox + oy], outs[w].at[p], send_sems.at[p, w], recv_sems.at[p, w], (ox, oy, c))
                for p, (ox, oy) in enumerate(_other_chips(x, y)) for w in range(len(ins))]

    def start(self, ins, outs, sems):
        for cp in self._copies(ins, outs, sems):
            cp.start()

    def finish(self, ins, outs, sems):
        for cp in self._copies(ins, outs, sems):
            cp.wait()


def _run_exchange(name, plan):
    n_in, n_out = len(plan.operands), len(plan.out_shapes)

    def body(*refs):
        parts = (refs[:n_in], refs[n_in:n_in + n_out], refs[n_in + n_out:])
        plan.start(*parts)
        plan.relay(*parts)
        plan.finish(*parts)

    return pl.pallas_call(body, name=name, in_specs=[ANY] * n_in, out_specs=[ANY] * n_out,
                          out_shape=list(plan.out_shapes), scratch_shapes=list(plan.scratch))(*plan.operands)


def _pair_exchange(name, grads):
    n_w = len(grads)

    def halves(w):
        return grads[w].shape[1] // 2

    def body(*refs):
        ins, theirs = refs[:n_w], refs[n_w:2 * n_w]
        send_sems, recv_sems = refs[2 * n_w:]
        x, y, c = _place()
        sends = [pltpu.make_async_remote_copy(
            src_ref=ins[w].at[:, pl.ds((1 - c) * halves(w), halves(w)), :], dst_ref=theirs[w],
            send_sem=send_sems.at[w], recv_sem=recv_sems.at[w], device_id=(x, y, 1 - c), device_id_type=MESH)
            for w in range(n_w)]
        for cp in sends:
            cp.start()
        for cp in sends:
            cp.wait()

    return pl.pallas_call(
        body, name=name, in_specs=[ANY] * n_w, out_specs=[ANY] * n_w,
        out_shape=[jax.ShapeDtypeStruct((N_CHIPS, halves(w), grads[w].shape[2]), F32) for w in range(n_w)],
        scratch_shapes=[pltpu.SemaphoreType.DMA((n_w,)), pltpu.SemaphoreType.DMA((n_w,))],
    )(*grads)


def _pair_share(shards):
    n_w = len(shards)

    def body(*refs):
        ins, outs = refs[:n_w], refs[n_w:2 * n_w]
        send_sems, recv_sems = refs[2 * n_w:]
        x, y, c = _place()
        sends = []
        for w in range(n_w):
            h = shards[w].shape[0] // 2
            mine = outs[w].at[pl.ds(c * h, h)]
            sends.append(pltpu.make_async_remote_copy(
                src_ref=mine, dst_ref=mine, send_sem=send_sems.at[w], recv_sem=recv_sems.at[w],
                device_id=(x, y, 1 - c), device_id_type=MESH))
        for cp in sends:
            cp.start()
        for w in range(n_w):
            h = shards[w].shape[0] // 2
            theirs = outs[w].at[pl.ds((1 - c) * h, h)]
            pltpu.make_async_remote_copy(
                src_ref=theirs, dst_ref=theirs, send_sem=send_sems.at[w], recv_sem=recv_sems.at[w],
                device_id=(x, y, 1 - c), device_id_type=MESH).wait_recv()
        for cp in sends:
            cp.wait_send()

    return pl.pallas_call(
        body, name="pair_share", in_specs=[ANY] * n_w, out_specs=[ANY] * n_w,
        out_shape=[jax.ShapeDtypeStruct(s.shape, s.dtype) for s in shards],
        input_output_aliases={w: w for w in range(n_w)},
        scratch_shapes=[pltpu.SemaphoreType.DMA((n_w,)), pltpu.SemaphoreType.DMA((n_w,))],
    )(*shards)


def _all_reduce_small(vec):
    rows = vec.shape[0]

    def body(v_ref, o_ref, slots, send_sems, recv_sems):
        x, y, c = _place()
        me = 4 * x + 2 * y + c
        slots[me] = v_ref[...]
        sends = []
        for k in range(1, N_DEV):
            peer = (x ^ (k >> 2), y ^ ((k >> 1) & 1), c ^ (k & 1))
            sends.append(pltpu.make_async_remote_copy(
                src_ref=v_ref, dst_ref=slots.at[me], send_sem=send_sems.at[k - 1], recv_sem=recv_sems.at[k - 1],
                device_id=peer, device_id_type=MESH))
        for cp in sends:
            cp.start()
        for k in range(1, N_DEV):
            px, py, pc = x ^ (k >> 2), y ^ ((k >> 1) & 1), c ^ (k & 1)
            landed = slots.at[4 * px + 2 * py + pc]
            pltpu.make_async_remote_copy(
                src_ref=landed, dst_ref=landed, send_sem=send_sems.at[k - 1], recv_sem=recv_sems.at[k - 1],
                device_id=(px, py, pc), device_id_type=MESH).wait_recv()
        for cp in sends:
            cp.wait_send()
        total = slots[0]
        for d in range(1, N_DEV):
            total = total + slots[d]
        o_ref[...] = total

    vm = pl.BlockSpec(memory_space=pltpu.VMEM)
    return pl.pallas_call(
        body, name="all_reduce_small", in_specs=[vm], out_specs=vm, out_shape=jax.ShapeDtypeStruct(vec.shape, F32),
        scratch_shapes=[pltpu.VMEM((N_DEV, rows, LANES), F32), pltpu.SemaphoreType.DMA((N_DEV - 1,)),
                        pltpu.SemaphoreType.DMA((N_DEV - 1,))],
    )(vec)


def _row_tile(rows):
    for tr in (256, 128, 64, 32, 16):
        if rows % tr == 0:
            return tr
    return rows


def _pair_sum(name, place, grad, theirs):
    n, r, c = grad.shape
    half = r // 2
    tr = _row_tile(half)
    nb = half // tr

    def body(place_ref, g_ref, t_ref, o_ref):
        o_ref[...] = (g_ref[...] + t_ref[...]).astype(BF16)

    return pl.pallas_call(
        body, name=name, out_shape=jax.ShapeDtypeStruct((n, half, c), BF16),
        grid_spec=pltpu.PrefetchScalarGridSpec(
            num_scalar_prefetch=1, grid=(n, nb),
            in_specs=[pl.BlockSpec((1, tr, c), lambda j, i, pr: (j, pr[0] * nb + i, 0)),
                      pl.BlockSpec((1, tr, c), lambda j, i, pr: (j, i, 0))],
            out_specs=pl.BlockSpec((1, tr, c), lambda j, i, pr: (j, i, 0))),
        compiler_params=_params("parallel", "parallel"),
    )(place, grad, theirs)


def _sum_chips(name, place, pair_sums, landed):
    _, half, c = pair_sums.shape
    tr = _row_tile(half)
    nb = half // tr

    def body(place_ref, s_ref, q_ref, o_ref):
        total = s_ref[0].astype(F32)
        for p in range(3):
            total = total + q_ref[p].astype(F32)
        o_ref[...] = total

    return pl.pallas_call(
        body, name=name, out_shape=jax.ShapeDtypeStruct((2 * half, c), F32),
        grid_spec=pltpu.PrefetchScalarGridSpec(
            num_scalar_prefetch=1, grid=(nb,),
            in_specs=[pl.BlockSpec((1, tr, c), lambda i, pr: (pr[1], i, 0)),
                      pl.BlockSpec((3, tr, c), lambda i, pr: (0, i, 0))],
            out_specs=pl.BlockSpec((tr, c), lambda i, pr: (pr[0] * nb + i, 0))),
        compiler_params=_params("parallel"),
    )(place, pair_sums, landed)


BIG = ("w_in", "w_branch_a", "w_branch_b", "w_out", "w_ffn_gate", "w_ffn_up", "w_ffn_down", "w_ple_gate", "w_ple_proj")
LATE = BIG[1:]
COLUMN_SHARDED = ("w_in", "w_branch_a", "w_branch_b", "w_ffn_gate", "w_ffn_up", "w_ple_proj")
SMALL = ("norm_mix", "w_pool", "pool_scale", "norm_ffn", "norm_ple", "norm_final")


def _join_columns(w4):
    return jnp.concatenate([w4[j] for j in range(N_CHIPS)], axis=1)


def _split_columns(g):
    k, n = g.shape
    return g.reshape(k, N_CHIPS, n // N_CHIPS).transpose(1, 0, 2)


def _sds(shape, dtype):
    return jax.ShapeDtypeStruct(shape, dtype)


def _local_step(x, p, target, wf, small, gather_late=None, exchange_early=None, exchange_last=None):
    t, d = x.shape
    w_in = wf["w_in"]
    w_pool_b = small["w_pool"].astype(BF16)
    dp = w_pool_b.shape[0] * w_pool_b.shape[1]

    h1 = _norm_fwd("norm_mix", x, small["norm_mix"])
    u, q, kv, ga, gb = _mm(
        "proj", [h1], [w_in[j] for j in range(N_CHIPS)], "nn",
        [_sds((t, dp), F32), _sds((t, dp), BF16), _sds((t, d), BF16), _sds((t, d), BF16), _sds((t, d), BF16)],
        separate=True, epilogue=lambda uq, kv_, ga_, gb_: (uq[:, :dp], uq[:, dp:], kv_, ga_, gb_), tm=512)
    pooled, ya = _pool_fwd(u, w_pool_b, small["pool_scale"])
    n_pairs = dp // LANES
    yb, late = _attn_fwd(q, 0, kv, 0, n_pairs, n_pairs, rider=gather_late or _NoRider())
    wf = {**wf, **dict(zip(LATE, late))}
    w_gate, w_up = _join_columns(wf["w_ffn_gate"]), _join_columns(wf["w_ffn_up"])
    dff = w_gate.shape[1]
    w_down = wf["w_ffn_down"].reshape(dff, d)
    w_a, w_b, w_pp = _join_columns(wf["w_branch_a"]), _join_columns(wf["w_branch_b"]), _join_columns(wf["w_ple_proj"])
    w_out = wf["w_out"].reshape(d, d)
    w_pg = wf["w_ple_gate"].reshape(d, d)
    ta, tb, merged = _mm(
        "branches_merge", [ya, yb], [w_a, w_b], "nn", [_sds((t, d), BF16)] * 3, extras=[ga, gb], separate=True,
        epilogue=lambda tav, tbv, gav, gbv: (tav, tbv, _sigmoid(gav) * tav + _sigmoid(gbv) * tbv), tm=512)
    def residual_norm(acc, xv, g):
        xn = acc + xv
        return xn, xn * lax.rsqrt(jnp.mean(xn * xn, axis=-1, keepdims=True) + RMS_EPS) * g

    x1, h2 = _mm("mix_out", [merged], [w_out], "nn", [_sds((t, d), F32), _sds((t, d), BF16)],
                 extras=[x, small["norm_ffn"]], epilogue=residual_norm)
    gate, up, act = _mm("ffn_gate_up", [h2], [w_gate, w_up], "nn", [_sds((t, dff), BF16)] * 3, separate=True,
                        epilogue=lambda gv, uv: (gv, uv, gv * _sigmoid(gv) * uv), tm=512, tn=dff // 2)
    x2, h3 = _mm("ffn_down", [act], [w_down], "nn", [_sds((t, d), F32), _sds((t, d), BF16)],
                 extras=[x1, small["norm_ple"]], epilogue=residual_norm, tm=512)
    gp, pp, x3 = _mm(
        "ple", [h3, p], [w_pg, w_pp], "nn", [_sds((t, d), BF16), _sds((t, d), BF16), _sds((t, d), F32)], extras=[x2],
        separate=True, epilogue=lambda gv, pv, xv: (gv, pv, xv + _sigmoid(gv) * pv), tm=512)
    (dx3, d_pp, d_gp), (d_norm_final, loss_row) = _split2(
        _final_loss(x3, target, small["norm_final"].reshape(1, d), gp, pp), 3)

    def through_norm(dh, xv, g, dres):
        dx, d_gain = _rms_norm_bwd(dh, xv, g)
        return dx + dres, dx + dres, d_gain

    stream = [_sds((t, d), F32), _sds((t, d), BF16)]
    gain_sum = [_sds((1, d), F32)]
    g_w_pp, = _mm_tn("g_ple_proj", p, [d_pp])
    g_w_pg, = _mm_tn("g_ple_gate", h3, [d_gp])
    dx2, dx2_b, d_norm_ple = _mm("d_h3", [d_gp], [w_pg], "nt", stream, extras=[x2, small["norm_ple"], dx3],
                                 epilogue=through_norm, sum_shapes=gain_sum, tm=512)

    def ffn_bwd(acc, gv, uv):
        s = _sigmoid(gv)
        return acc * uv * (s * (1.0 + gv * (1.0 - s))), acc * (gv * s)

    d_gate, d_up = _mm("d_act", [dx2_b], [w_down], "nt", [_sds((t, dff), BF16)] * 2, extras=[gate, up],
                       epilogue=ffn_bwd, tm=512, tn=dff // 2)
    g_w_down, = _mm_tn("g_ffn_down", act, [dx2_b], tmm=512)
    g_w_gate, g_w_up = _mm_tn("g_ffn_gate_up", h2, [d_gate, d_up], n_blocks=2)
    dx1, dx1_b, d_norm_ffn = _mm(
        "d_h2", [d_gate, d_up], [w_gate, w_up], "nt", stream, extras=[x1, small["norm_ffn"], dx2],
        epilogue=through_norm, sum_shapes=gain_sum, tm=512)

    def merge_bwd(acc, tav, tbv, gav, gbv):
        sa, sb = _sigmoid(gav), _sigmoid(gbv)
        return acc * sa, acc * sb, acc * tav * sa * (1.0 - sa), acc * tbv * sb * (1.0 - sb)

    d_ta, d_tb, d_ga, d_gb = _mm("d_merged", [dx1_b], [w_out], "nt", [_sds((t, d), BF16)] * 4,
                                 extras=[ta, tb, ga, gb], epilogue=merge_bwd, tm=512)
    g_w_out, = _mm_tn("g_w_out", merged, [dx1_b])
    g_w_a, = _mm_tn("g_branch_a", ya, [d_ta])
    g_w_b, = _mm_tn("g_branch_b", yb, [d_tb])
    d_ya, = _mm("d_ya", [d_ta], [w_a], "nt", [_sds((t, dp), F32)])
    d_yb, = _mm("d_yb", [d_tb], [w_b], "nt", [_sds((t, dp), BF16)])
    d_u, g_w_pool, d_pool_scale = _pool_bwd(d_ya, pooled, w_pool_b, small["pool_scale"])
    big = {
        "w_branch_a": _split_columns(g_w_a), "w_branch_b": _split_columns(g_w_b),
        "w_out": g_w_out.reshape(wf["w_out"].shape), "w_ffn_gate": _split_columns(g_w_gate),
        "w_ffn_up": _split_columns(g_w_up), "w_ffn_down": g_w_down.reshape(wf["w_ffn_down"].shape),
        "w_ple_gate": g_w_pg.reshape(wf["w_ple_gate"].shape),
        "w_ple_proj": _split_columns(g_w_pp),
    }
    rider = exchange_early(big) if exchange_early else _NoRider()
    (d_q, d_k, d_v), early = _attn_bwd(q, 0, kv, 0, n_pairs, d_yb, n_pairs, rider=rider)
    d_proj = [(d_u, d_q), (d_k, d_v), d_ga, d_gb]
    big["w_in"], = _mm_tn("g_w_in", h1, d_proj, tmm=512, stacked=True)
    rider = exchange_last(big["w_in"]) if exchange_last else _NoRider()
    res = _mm(
        "d_h1", d_proj, [w_in[j] for j in range(N_CHIPS)], "nt", [_sds((t, d), F32)],
        extras=[x, small["norm_mix"], dx1], epilogue=lambda dh, xv, g, dres: through_norm(dh, xv, g, dres)[1:],
        sum_shapes=gain_sum, tm=512, rider=rider)
    (grad_x, d_norm_mix), last = res if rider.operands else (res, ())
    small_g = {"norm_mix": d_norm_mix, "w_pool": g_w_pool, "pool_scale": d_pool_scale, "norm_ffn": d_norm_ffn,
               "norm_ple": d_norm_ple, "norm_final": d_norm_final}
    return grad_x, big, small_g, loss_row, early, last


def _split2(res, n):
    return res[:n], res[n:]


def _pack_small(small_g, loss_row):
    parts, layout = [], []
    for name in SMALL + ("loss",):
        v = (loss_row if name == "loss" else small_g[name]).reshape(-1, LANES)
        pad = (-v.shape[0]) % 8
        if pad:
            v = jnp.concatenate([v, jnp.zeros((pad, LANES), F32)], axis=0)
        layout.append((name, sum(q.shape[0] for q in parts), v.shape[0]))
        parts.append(v)
    return jnp.concatenate(parts, axis=0), layout


def kernel(x, p, norm_mix, w_in, w_pool, pool_scale, w_branch_a, w_branch_b, w_out, norm_ffn, w_ffn_gate, w_ffn_up, w_ffn_down, norm_ple, w_ple_gate, w_ple_proj, norm_final, loss_target, m_norm_mix, m_w_in, m_w_pool, m_pool_scale, m_w_branch_a, m_w_branch_b, m_w_out, m_norm_ffn, m_w_ffn_gate, m_w_ffn_up, m_w_ffn_down, m_norm_ple, m_w_ple_gate, m_w_ple_proj, m_norm_final, v_norm_mix, v_w_in, v_w_pool, v_pool_scale, v_w_branch_a, v_w_branch_b, v_w_out, v_norm_ffn, v_w_ffn_gate, v_w_ffn_up, v_w_ffn_down, v_norm_ple, v_w_ple_gate, v_w_ple_proj, v_norm_final):
    given = dict(locals())
    names = BIG + SMALL
    order = ("norm_mix", "w_in", "w_pool", "pool_scale", "w_branch_a", "w_branch_b", "w_out", "norm_ffn", "w_ffn_gate",
             "w_ffn_up", "w_ffn_down", "norm_ple", "w_ple_gate", "w_ple_proj", "norm_final")
    t, d = x.shape[1], x.shape[2]
    shard = {n: given[n][0] for n in BIG}
    small = {"norm_mix": norm_mix, "w_pool": w_pool[0], "pool_scale": pool_scale, "norm_ffn": norm_ffn,
             "norm_ple": norm_ple, "norm_final": norm_final}

    as_bf16 = {n: shard[n].astype(BF16) for n in BIG}
    wf = {"w_in": _run_exchange("gather_w_in", _WeightGather([as_bf16["w_in"]]))[0]}

    place = jnp.stack([lax.axis_index("c"), 2 * lax.axis_index("x") + lax.axis_index("y")]).astype(jnp.int32)
    pair_sums = {}

    def exchange_early(ready):
        theirs = _pair_exchange("pair_exchange_early", [ready[n] for n in LATE])
        for n, other in zip(LATE, theirs):
            pair_sums[n] = _pair_sum(f"pair_sum_{n}", place, ready[n], other)
        return _ChipExchange([pair_sums[n] for n in LATE])

    def exchange_last(g_w_in):
        theirs, = _pair_exchange("pair_exchange_w_in", [g_w_in])
        pair_sums["w_in"] = _pair_sum("pair_sum_w_in", place, g_w_in, theirs)
        return _ChipExchange([pair_sums["w_in"]])

    grad_x, big_g, small_g, loss_row, early, last = _local_step(
        x.reshape(t, d), p.reshape(t, p.shape[-1]), loss_target.reshape(t, d), wf, small,
        gather_late=_WeightGather([as_bf16[n] for n in LATE]), exchange_early=exchange_early,
        exchange_last=exchange_last)
    landed = dict(zip(LATE + ("w_in",), tuple(early) + tuple(last)))
    halves = [_sum_chips(f"chip_sum_{n}", place, pair_sums[n], landed[n]) for n in BIG]
    grads = dict(zip(BIG, _pair_share(halves)))

    packed, layout = _pack_small(small_g, loss_row)
    reduced = _all_reduce_small(packed)
    for name, start, rows in layout:
        if name == "loss":
            loss = jnp.sum(reduced[start:start + rows])
        else:
            n_el = small[name].size
            grads[name] = reduced[start:start + rows].reshape(-1)[:n_el]

    deltas, new_m, new_v = {}, {}, {}
    for n in order:
        w = shard[n] if n in BIG else small[n]
        shape2 = w.shape if w.ndim == 2 else ((1, w.shape[0]) if w.ndim == 1 else (w.shape[0] * w.shape[1], w.shape[2]))
        g2 = grads[n].reshape(shape2)
        dl, mn, vn = _adamw(f"adamw_{n}", w.reshape(shape2), g2, given["m_" + n].reshape(shape2),
                            given["v_" + n].reshape(shape2))
        full = given[n].shape
        grads[n], deltas[n], new_m[n], new_v[n] = g2.reshape(full), dl.reshape(full), mn.reshape(full), vn.reshape(full)

    return (loss, grad_x.reshape(x.shape), *[grads[n] for n in order], *[deltas[n] for n in order],
            *[new_m[n] for n in order], *[new_v[n] for n in order])
```

```python
import functools
import math

import jax
import jax.numpy as jnp
from jax import lax
from jax.experimental import pallas as pl
from jax.experimental.pallas import tpu as pltpu

F32 = jnp.float32
BF16 = jnp.bfloat16
MESH = pl.DeviceIdType.MESH

RMS_EPS = 1e-6
POOL_WINDOWS = (2, 4, 8, 16)
POOL_HALO = 16
HEAD_DIM = 64
LANES = 128
ATT_BLOCK = 256
ATT_CHUNK = 256
ATT_SLAB = 256
ATT_SCALE = 1.0 / math.sqrt(HEAD_DIM)
LOG2_E = 1.4426950408889634
ATT_EXIT_BELOW = -150.5
ADAM_LR, ADAM_B1, ADAM_B2, ADAM_EPS, ADAM_WD, ADAM_STEP = 0.001, 0.9, 0.999, 1e-08, 0.01, 10
V7X_VMEM_LIMIT_BYTES = 56 * 1024 * 1024
N_CHIPS = 4
N_DEV = 8


def _params(*semantics):
    return pltpu.CompilerParams(dimension_semantics=semantics, vmem_limit_bytes=V7X_VMEM_LIMIT_BYTES)


def _sigmoid(z):
    return 1.0 / (1.0 + jnp.exp(-z))


def _tiled_spec(shape, tm, tn, n_total, at):
    rows, width = shape
    if rows == 1:
        if width == n_total:
            return pl.BlockSpec((1, tn), at(lambda i, j: (0, j)))
        return pl.BlockSpec((1, width), at(lambda i, j: (0, 0)))
    if width == n_total:
        return pl.BlockSpec((tm, tn), at(lambda i, j: (i, j)))
    assert tn == n_total, "an operand narrower than the output needs whole output rows per tile"
    return pl.BlockSpec((tm, width), at(lambda i, j: (i, 0)))


def _column_pieces(operands):
    pieces = [tuple(a) if isinstance(a, (tuple, list)) else (a,) for a in operands]
    return [p for ps in pieces for p in ps], [len(ps) for ps in pieces]


def _load_bf16(refs, counts):
    tiles, k = [], 0
    for n in counts:
        parts = [r[...] for r in refs[k:k + n]]
        parts = [t if t.dtype == BF16 else t.astype(BF16) for t in parts]
        tiles.append(parts[0] if n == 1 else jnp.concatenate(parts, axis=1))
        k += n
    return tiles


def _mm(name, a_list, b_list, mode, out_shapes, epilogue=None, extras=(), tm=1024, tn=None, separate=False,
        sum_shapes=(), rider=None):
    flat_a, counts = _column_pieces(a_list)
    m_total = flat_a[0].shape[0]
    n_total = b_list[0].shape[1] if mode == "nn" else b_list[0].shape[0]
    tn = n_total if tn is None else tn
    tm = min(tm, m_total)
    assert m_total % tm == 0 and n_total % tn == 0 and (not sum_shapes or tn == n_total)
    n_a, n_b, n_extra, n_out = len(counts), len(b_list), len(extras), len(out_shapes)
    assert n_a in (1, n_b)
    dims = (((1,), (0,)), ((), ())) if mode == "nn" else (((1,), (1,)), ((), ()))
    rider = rider or _NoRider()
    grid = (n_total // tn, m_total // tm)

    def at(index):
        return lambda j, i: index(i, j)

    def body(*refs):
        ins, o_refs, _, riding = rider.split(refs, len(flat_a) + n_b + n_extra, n_out + len(sum_shapes))
        a_refs, b_refs, e_refs = ins[:len(flat_a)], ins[len(flat_a):len(flat_a) + n_b], ins[len(flat_a) + n_b:]
        at_first = (pl.program_id(0) == 0) & (pl.program_id(1) == 0)
        at_last = (pl.program_id(0) == grid[0] - 1) & (pl.program_id(1) == grid[1] - 1)
        top, bottom = rider.at_steps(riding, at_first, at_first, at_last)
        top()
        lefts = _load_bf16(a_refs, counts)
        products = [lax.dot_general(lefts[s % n_a], b_refs[s][...], dims, preferred_element_type=F32)
                    for s in range(n_b)]
        if not separate:
            products = [functools.reduce(lambda p, r: p + r, products)]
        extra_tiles = [e[...].astype(F32) for e in e_refs]
        outs = products if epilogue is None else epilogue(*products, *extra_tiles)
        for o_ref, o in zip(o_refs[:n_out], outs[:n_out]):
            o_ref[...] = o.astype(o_ref.dtype)
        if sum_shapes:
            @pl.when(pl.program_id(1) == 0)
            def _():
                for s_ref in o_refs[n_out:]:
                    s_ref[...] = jnp.zeros_like(s_ref)

            for s_ref, s in zip(o_refs[n_out:], outs[n_out:]):
                s_ref[...] += s
        bottom()

    once = dict(pipeline_mode=pl.Buffered(1)) if tn == n_total else {}
    in_specs = [pl.BlockSpec((tm, a.shape[1]), at(lambda i, j: (i, 0))) for a in flat_a]
    if mode == "nn":
        in_specs += [pl.BlockSpec((b.shape[0], tn), at(lambda i, j: (0, j)), **once) for b in b_list]
    else:
        in_specs += [pl.BlockSpec((tn, b.shape[1]), at(lambda i, j: (j, 0)), **once) for b in b_list]
    in_specs += [_tiled_spec(e.shape, tm, tn, n_total, at) for e in extras]
    out_specs = [_tiled_spec(o.shape, tm, tn, n_total, at) for o in out_shapes]
    out_specs += [pl.BlockSpec(s.shape, at(lambda i, j: (0, 0))) for s in sum_shapes]
    semantics = ("arbitrary", "arbitrary") if sum_shapes or rider.operands else ("parallel", "parallel")
    res = pl.pallas_call(
        body, name=name, grid=grid, in_specs=in_specs + [ANY] * len(rider.operands),
        out_specs=out_specs + [ANY] * len(rider.out_shapes),
        out_shape=list(out_shapes) + list(sum_shapes) + list(rider.out_shapes), scratch_shapes=list(rider.scratch),
        compiler_params=_params(*semantics),
    )(*flat_a, *b_list, *extras, *rider.operands)
    n_own = len(out_shapes) + len(sum_shapes)
    return res if not rider.operands else (res[:n_own], res[n_own:])


def _mm_tn(name, a, b_list, tmm=1024, stacked=False, n_blocks=1):
    flat_b, counts = _column_pieces(b_list)
    m_total, k = a.shape
    widths = [sum(p.shape[1] for p in flat_b[sum(counts[:s]):sum(counts[:s + 1])]) for s in range(len(counts))]
    tmm = min(tmm, m_total)
    assert m_total % tmm == 0 and (n_blocks == 1 or max(counts) == 1) and all(w % n_blocks == 0 for w in widths)
    n_b = len(counts)

    def body(*refs):
        a_ref, b_refs, o_refs = refs[0], refs[1:1 + len(flat_b)], refs[1 + len(flat_b):]

        @pl.when(pl.program_id(1) == 0)
        def _():
            for o_ref in o_refs:
                o_ref[...] = jnp.zeros_like(o_ref)

        av, = _load_bf16([a_ref], [1])
        for s, bv in enumerate(_load_bf16(b_refs, counts)):
            product = lax.dot_general(av, bv, (((0,), (0,)), ((), ())), preferred_element_type=F32)
            if stacked:
                o_refs[0][s] += product
            else:
                o_refs[s][...] += product

    in_specs = [pl.BlockSpec((tmm, k), lambda nb, m: (m, 0))]
    in_specs += [pl.BlockSpec((tmm, b.shape[1] // n_blocks), lambda nb, m: (m, nb)) for b in flat_b]
    if stacked:
        out_shape = [jax.ShapeDtypeStruct((n_b, k, widths[0]), F32)]
        out_specs = [pl.BlockSpec((n_b, k, widths[0] // n_blocks), lambda nb, m: (0, 0, nb))]
    else:
        out_shape = [jax.ShapeDtypeStruct((k, w), F32) for w in widths]
        out_specs = [pl.BlockSpec((k, w // n_blocks), lambda nb, m: (0, nb)) for w in widths]
    return pl.pallas_call(
        body, name=name, grid=(n_blocks, m_total // tmm), in_specs=in_specs, out_specs=out_specs, out_shape=out_shape,
        compiler_params=_params("arbitrary", "arbitrary"),
    )(a, *flat_b)


def _rows(name, fn, ins, tile_outs, sum_outs=(), tr=512, rider=None):
    t_total = max(a.shape[0] for a in ins)
    tr = min(tr, t_total)
    assert t_total % tr == 0
    n_in, n_tile = len(ins), len(tile_outs)
    rider = rider or _NoRider()
    n_steps = t_total // tr

    def body(*refs):
        own_ins, own_outs, _, riding = rider.split(refs, n_in, n_tile + len(sum_outs))
        step = pl.program_id(0)
        top, bottom = rider.at_steps(riding, step == 0, step == n_steps - 1, step == n_steps - 1)
        top()
        refs = tuple(own_ins) + tuple(own_outs)
        outs = fn(*[r[...].astype(F32) for r in refs[:n_in]])
        for o_ref, o in zip(refs[n_in:n_in + n_tile], outs[:n_tile]):
            o_ref[...] = o.astype(o_ref.dtype)
        if sum_outs:
            @pl.when(pl.program_id(0) == 0)
            def _():
                for s_ref in refs[n_in + n_tile:]:
                    s_ref[...] = jnp.zeros_like(s_ref)

            for s_ref, s in zip(refs[n_in + n_tile:], outs[n_tile:]):
                s_ref[...] += s
        bottom()

    def spec(shape):
        if shape[0] == 1:
            return pl.BlockSpec(shape, lambda i: (0, 0))
        return pl.BlockSpec((tr, shape[1]), lambda i: (i, 0))

    return pl.pallas_call(
        body, name=name, grid=(n_steps,), in_specs=[spec(a.shape) for a in ins] + [ANY] * len(rider.operands),
        out_specs=[spec(o.shape) for o in tile_outs] + [spec(s.shape) for s in sum_outs] + [ANY] * len(rider.out_shapes),
        out_shape=list(tile_outs) + list(sum_outs) + list(rider.out_shapes), scratch_shapes=list(rider.scratch),
        compiler_params=_params("arbitrary" if sum_outs or rider.operands else "parallel"),
    )(*ins, *rider.operands)


def _norm_fwd(name, x, gain, rider=None):
    def fn(xv, g):
        inv = lax.rsqrt(jnp.mean(xv * xv, axis=-1, keepdims=True) + RMS_EPS)
        return (xv * inv * g,)

    res = _rows(name, fn, [x, gain], [jax.ShapeDtypeStruct(x.shape, BF16)], rider=rider)
    return res[0], res[1:]


def _rms_norm_bwd(dh, xv, g):
    inv = lax.rsqrt(jnp.mean(xv * xv, axis=-1, keepdims=True) + RMS_EPS)
    xn = xv * inv
    dxn = dh * g
    return inv * (dxn - xn * jnp.mean(dxn * xn, axis=-1, keepdims=True)), jnp.sum(dh * xn, axis=0, keepdims=True)


def _final_loss(x3, target, gain, gp, pp):
    d = x3.shape[1]

    def fn(xv, tv, g, gv, pv):
        inv = lax.rsqrt(jnp.mean(xv * xv, axis=-1, keepdims=True) + RMS_EPS)
        err = xv * inv * g - tv
        dx, d_gain = _rms_norm_bwd(err * (1.0 / d), xv, g)
        s = _sigmoid(gv)
        return dx, dx * s, dx * pv * s * (1.0 - s), d_gain, (0.5 / d) * jnp.sum(err * err, axis=0, keepdims=True)

    act = jax.ShapeDtypeStruct(x3.shape, BF16)
    return _rows("final_loss", fn, [x3, target, gain, gp, pp], [jax.ShapeDtypeStruct(x3.shape, F32), act, act],
                 [jax.ShapeDtypeStruct((1, d), F32), jax.ShapeDtypeStruct((1, d), F32)])


def _window_counts(t_pos, w):
    return jnp.minimum(t_pos + 1, w).astype(F32)


def _pool_fwd(u, w_pool, scale, tr=512):
    t_total, width = u.shape
    tr = min(tr, t_total)
    n_groups = len(POOL_WINDOWS)
    gdim = width // n_groups
    ext = tr + POOL_HALO

    def body(u_ref, halo_ref, w_ref, s_ref, pooled_ref, ya_ref):
        i = pl.program_id(0)
        halo = jnp.where(i == 0, 0.0, halo_ref[...])
        t_pos = i * tr + lax.broadcasted_iota(jnp.int32, (tr, 1), 0)
        for g, w in enumerate(POOL_WINDOWS):
            cols = slice(g * gdim, (g + 1) * gdim)
            main = u_ref[:, cols]
            win = jnp.concatenate([halo[:, cols], main], axis=0)
            span = 1
            while span < w:
                win = win + pltpu.roll(win, span, 0)
                span *= 2
            pooled = win[POOL_HALO:, :] * (1.0 / _window_counts(t_pos, w)) - main
            pooled_b = pooled.astype(BF16)
            pooled_ref[:, cols] = pooled_b
            mixed = jnp.dot(pooled_b, w_ref[g], preferred_element_type=F32)
            ya_ref[:, cols] = (mixed * s_ref[:, cols]).astype(BF16)

    hb = tr // POOL_HALO
    return pl.pallas_call(
        body, name="pool_fwd", grid=(t_total // tr,),
        in_specs=[pl.BlockSpec((tr, width), lambda i: (i, 0)),
                  pl.BlockSpec((POOL_HALO, width), lambda i: (jnp.maximum(i * hb - 1, 0), 0)),
                  pl.BlockSpec((n_groups, gdim, gdim), lambda i: (0, 0, 0)),
                  pl.BlockSpec((1, width), lambda i: (0, 0))],
        out_specs=[pl.BlockSpec((tr, width), lambda i: (i, 0)), pl.BlockSpec((tr, width), lambda i: (i, 0))],
        out_shape=[jax.ShapeDtypeStruct(u.shape, BF16), jax.ShapeDtypeStruct(u.shape, BF16)],
        compiler_params=_params("parallel"),
    )(u, u, w_pool, scale)


def _pool_bwd(dya, pooled, w_pool, scale, tr=512):
    t_total, width = dya.shape
    tr = min(tr, t_total)
    n_groups = len(POOL_WINDOWS)
    gdim = width // n_groups
    ext = tr + POOL_HALO
    n_tiles = t_total // tr

    def body(d_ref, halo_ref, p_ref, w_ref, s_ref, du_ref, dw_ref, ds_ref):
        i = pl.program_id(0)

        @pl.when(i == 0)
        def _():
            dw_ref[...] = jnp.zeros_like(dw_ref)
            ds_ref[...] = jnp.zeros_like(ds_ref)

        halo = jnp.where(i == n_tiles - 1, 0.0, halo_ref[...])
        t_pos = i * tr + lax.broadcasted_iota(jnp.int32, (ext, 1), 0)
        for g, w in enumerate(POOL_WINDOWS):
            cols = slice(g * gdim, (g + 1) * gdim)
            sc = s_ref[:, cols]
            d_main = d_ref[:, cols]
            pooled_b = p_ref[:, cols]
            mixed = jnp.dot(pooled_b, w_ref[g], preferred_element_type=F32)
            ds_ref[:, cols] += jnp.sum(d_main * mixed, axis=0, keepdims=True)
            dmix = (jnp.concatenate([d_main, halo[:, cols]], axis=0) * sc).astype(BF16)
            dw_ref[g] += lax.dot_general(pooled_b, dmix[:tr, :], (((0,), (0,)), ((), ())),
                                         preferred_element_type=F32)
            dpool = lax.dot_general(dmix, w_ref[g], (((1,), (1,)), ((), ())), preferred_element_type=F32)
            win = dpool * (1.0 / _window_counts(t_pos, w))
            span = 1
            while span < w:
                win = win + pltpu.roll(win, ext - span, 0)
                span *= 2
            du_ref[:, cols] = (win[:tr, :] - dpool[:tr, :]).astype(BF16)

    hb = tr // POOL_HALO
    last_halo = t_total // POOL_HALO - 1
    return pl.pallas_call(
        body, name="pool_bwd", grid=(n_tiles,),
        in_specs=[pl.BlockSpec((tr, width), lambda i: (i, 0)),
                  pl.BlockSpec((POOL_HALO, width), lambda i: (jnp.minimum((i + 1) * hb, last_halo), 0)),
                  pl.BlockSpec((tr, width), lambda i: (i, 0)),
                  pl.BlockSpec((n_groups, gdim, gdim), lambda i: (0, 0, 0)),
                  pl.BlockSpec((1, width), lambda i: (0, 0))],
        out_specs=[pl.BlockSpec((tr, width), lambda i: (i, 0)),
                   pl.BlockSpec((n_groups, gdim, gdim), lambda i: (0, 0, 0)),
                   pl.BlockSpec((1, width), lambda i: (0, 0))],
        out_shape=[jax.ShapeDtypeStruct(dya.shape, BF16), jax.ShapeDtypeStruct((n_groups, gdim, gdim), F32),
                   jax.ShapeDtypeStruct((1, width), F32)],
        compiler_params=_params("arbitrary"),
    )(dya, dya, pooled, w_pool, scale)


def _head_masks():
    lane = lax.broadcasted_iota(jnp.int32, (1, LANES), 1)
    return lane < HEAD_DIM


def _stack_heads(tile, first):
    zero = jnp.zeros_like(tile)
    return jnp.concatenate([jnp.where(first, tile, zero), jnp.where(first, zero, tile)], axis=0)


def _split_bf16(v):
    hi = v.astype(BF16)
    lo = (v - hi.astype(F32)).astype(BF16)
    return hi, lo


def _causal_mask(t_pos, k_start):
    col = lax.broadcasted_iota(jnp.int32, (1, 2 * ATT_SLAB), 1)
    return k_start + (col & (ATT_SLAB - 1)) < t_pos


def _slab_scores(q, kd, mask):
    z2 = lax.dot_general(q, kd, (((1,), (1,)), ((), ())), preferred_element_type=F32) * LOG2_E
    log_hit = jnp.minimum(z2, 0.0) - jnp.log2(1.0 + jnp.exp2(-jnp.abs(z2)))
    log_fail = log_hit - z2
    return log_hit, (log_fail if mask is None else jnp.where(mask, log_fail, 0.0))


def _weights(log_hit, suffix, mask):
    arg = log_hit + suffix
    return jnp.exp2(arg if mask is None else jnp.where(mask, arg, -1e30))


def _tri(upper):
    r = lax.broadcasted_iota(jnp.int32, (2 * ATT_CHUNK, ATT_CHUNK), 0) & (ATT_CHUNK - 1)
    c = lax.broadcasted_iota(jnp.int32, (2 * ATT_CHUNK, ATT_CHUNK), 1)
    return jnp.where(r > c if upper else r < c, 1.0, 0.0).astype(BF16)


def _scan_chunk(v, tri):
    return jnp.dot(jnp.concatenate(_split_bf16(v), axis=1), tri, preferred_element_type=F32)


def _lane_bcast(col):
    return jnp.broadcast_to(col, (col.shape[0], LANES))


def _scan_slab(v, tri, carries, from_right):
    n_chunks = ATT_SLAB // ATT_CHUNK
    edge = 0 if from_right else ATT_CHUNK - 1
    parts, new_carries = [None] * (2 * n_chunks), []
    for head in range(2):
        run = carries[head]
        for c in (reversed(range(n_chunks)) if from_right else range(n_chunks)):
            lo_col = head * ATT_SLAB + c * ATT_CHUNK
            vc = v[:, lo_col:lo_col + ATT_CHUNK]
            sc = _scan_chunk(vc, tri)
            parts[head * n_chunks + c] = sc + jnp.concatenate([run] * (ATT_CHUNK // LANES), axis=1)
            run = run + _lane_bcast(sc[:, edge:edge + 1] + vc[:, edge:edge + 1])
        new_carries.append(run)
    return jnp.concatenate(parts, axis=1), new_carries


def _fold_heads(stacked, first):
    s = stacked.shape[0] // 2
    return jnp.where(first, stacked[:s], stacked[s:])


class _NoRider:
    operands, out_shapes, scratch = (), (), ()

    def split(self, refs, n_base_in, n_base_out):
        n_in, n_out, n_sem = len(self.operands), len(self.out_shapes), len(self.scratch)
        a = n_base_in + n_in
        b = a + n_base_out + n_out
        mine = (refs[n_base_in:a], refs[a + n_base_out:b], refs[b:b + n_sem])
        return refs[:n_base_in], refs[a:a + n_base_out], refs[b + n_sem:], mine

    def start(self, ins, outs, sems):
        pass

    def relay(self, ins, outs, sems):
        pass

    def finish(self, ins, outs, sems):
        pass

    def at_steps(self, refs, first_step, relay_step, last_step):
        if not self.operands:
            return (lambda: None), (lambda: None)

        def top():
            pl.when(first_step)(lambda: self.start(*refs))
            pl.when(relay_step)(lambda: self.relay(*refs))

        return top, lambda: pl.when(last_step)(lambda: self.finish(*refs))


def _attn_fwd(q_src, q_col, kv_src, k_col, v_col, n_pairs=4, rider=_NoRider()):
    t_total = q_src.shape[0]
    blk = ATT_BLOCK
    n_blocks = t_total // blk
    assert t_total % ATT_SLAB == 0 and ATT_SLAB % ATT_BLOCK == 0

    def body(*refs):
        (q_ref, k_ref, v_ref), (o_ref,), _, riding = rider.split(refs, 3, 1)
        h, i = pl.program_id(0), pl.program_id(1)
        top, bottom = rider.at_steps(riding, (h == 0) & (i == 0), (h == n_pairs - 1) & (i == 0),
                                     (h == n_pairs - 1) & (i == n_blocks - 1))
        top()
        first = _head_masks()
        q = q_ref[...] * ATT_SCALE
        t_pos = i * blk + lax.broadcasted_iota(jnp.int32, (blk, 1), 0)
        suffix_tri = _tri(upper=True)

        def more(state):
            slab, reach = state[0], state[1]
            return jnp.logical_and(slab >= 0, reach > ATT_EXIT_BELOW)

        def step(state, on_diagonal):
            slab, _, acc, right_a, right_b = state
            k_start = pl.multiple_of(slab * ATT_SLAB, ATT_SLAB)
            kd = _stack_heads(k_ref[pl.ds(k_start, ATT_SLAB), :], first)
            vd = _stack_heads(v_ref[pl.ds(k_start, ATT_SLAB), :], first)
            mask = _causal_mask(t_pos, k_start) if on_diagonal else None
            log_hit, log_fail = _slab_scores(q, kd, mask)
            suffix, (right_a, right_b) = _scan_slab(log_fail, suffix_tri, (right_a, right_b), from_right=True)
            a = _weights(log_hit, suffix, mask).astype(BF16)
            acc = acc + jnp.dot(a, vd, preferred_element_type=F32)
            return slab - 1, jnp.max(jnp.maximum(right_a, right_b)), acc, right_a, right_b

        zero = jnp.zeros((blk, LANES), F32)
        state = step(((i * blk) // ATT_SLAB, jnp.float32(0.0), zero, zero, zero), on_diagonal=True)
        state = lax.while_loop(more, functools.partial(step, on_diagonal=False), state)
        o_ref[...] = state[2].astype(BF16)
        bottom()

    res = pl.pallas_call(
        body, name="attn_fwd", grid=(n_pairs, n_blocks),
        in_specs=[pl.BlockSpec((blk, LANES), lambda h, i: (i, q_col + h)),
                  pl.BlockSpec((t_total, LANES), lambda h, i: (0, k_col + h)),
                  pl.BlockSpec((t_total, LANES), lambda h, i: (0, v_col + h))] + [ANY] * len(rider.operands),
        out_specs=[pl.BlockSpec((blk, LANES), lambda h, i: (i, h))] + [ANY] * len(rider.out_shapes),
        out_shape=[jax.ShapeDtypeStruct((t_total, n_pairs * LANES), BF16)] + list(rider.out_shapes),
        scratch_shapes=list(rider.scratch),
        compiler_params=_params("arbitrary", "arbitrary"),
    )(q_src, kv_src, kv_src, *rider.operands)
    return res[0], res[1:]


def _attn_bwd(q_src, q_col, kv_src, k_col, v_col, dy, n_pairs=4, rider=_NoRider()):
    t_total = q_src.shape[0]
    blk = ATT_BLOCK
    n_blocks = t_total // blk
    n_slabs = t_total // ATT_SLAB
    assert t_total % ATT_SLAB == 0 and ATT_SLAB % ATT_BLOCK == 0

    def body(*refs):
        (q_ref, dy_ref, k_ref, v_ref), (dq_ref, dk_ref, dv_ref), (g_s, dk_acc, dv_acc), riding = rider.split(refs, 4, 3)
        h, i = pl.program_id(0), pl.program_id(1)
        top, bottom = rider.at_steps(riding, (h == 0) & (i == 0), (h == n_pairs - 1) & (i == 0),
                                     (h == n_pairs - 1) & (i == n_blocks - 1))
        top()

        @pl.when(i == 0)
        def _():
            dk_acc[...] = jnp.zeros_like(dk_acc)
            dv_acc[...] = jnp.zeros_like(dv_acc)

        first = _head_masks()
        q = q_ref[...] * ATT_SCALE
        dy = dy_ref[...]
        t_pos = i * blk + lax.broadcasted_iota(jnp.int32, (blk, 1), 0)
        suffix_tri = _tri(upper=True)
        prefix_tri = _tri(upper=False)
        diag = (i * blk) // ATT_SLAB

        def more(state):
            slab, reach = state[0], state[1]
            return jnp.logical_and(slab >= 0, reach > ATT_EXIT_BELOW)

        def sweep1(state, on_diagonal):
            slab, _, right_a, right_b = state
            k_start = pl.multiple_of(slab * ATT_SLAB, ATT_SLAB)
            kd = _stack_heads(k_ref[pl.ds(k_start, ATT_SLAB), :], first)
            vd = _stack_heads(v_ref[pl.ds(k_start, ATT_SLAB), :], first)
            mask = _causal_mask(t_pos, k_start) if on_diagonal else None
            log_hit, log_fail = _slab_scores(q, kd, mask)
            suffix, (right_a, right_b) = _scan_slab(log_fail, suffix_tri, (right_a, right_b), from_right=True)
            a = _weights(log_hit, suffix, mask)
            da = lax.dot_general(dy, vd, (((1,), (1,)), ((), ())), preferred_element_type=F32)
            g_s[slab] = da * a
            dv_acc[pl.ds(k_start, ATT_SLAB), :] += _fold_heads(lax.dot_general(
                a.astype(BF16), dy, (((0,), (0,)), ((), ())), preferred_element_type=F32), first)
            return slab - 1, jnp.max(jnp.maximum(right_a, right_b)), right_a, right_b

        zero = jnp.zeros((blk, LANES), F32)
        state = sweep1((diag, jnp.float32(0.0), zero, zero), on_diagonal=True)
        end = lax.while_loop(more, functools.partial(sweep1, on_diagonal=False), state)[0]

        def sweep2(slab, carry, on_diagonal):
            dq, left_a, left_b = carry
            k_start = pl.multiple_of(slab * ATT_SLAB, ATT_SLAB)
            kd = _stack_heads(k_ref[pl.ds(k_start, ATT_SLAB), :], first)
            g = g_s[slab]
            z2 = lax.dot_general(q, kd, (((1,), (1,)), ((), ())), preferred_element_type=F32) * LOG2_E
            sig = 1.0 / (1.0 + jnp.exp2(-z2))
            prefix, (left_a, left_b) = _scan_slab(g, prefix_tri, (left_a, left_b), from_right=False)
            dz = g * (1.0 - sig) - sig * prefix
            if on_diagonal:
                dz = jnp.where(_causal_mask(t_pos, k_start), dz, 0.0)
            dz = dz.astype(BF16)
            dq = dq + jnp.dot(dz, kd, preferred_element_type=F32)
            dk_acc[pl.ds(k_start, ATT_SLAB), :] += _fold_heads(lax.dot_general(
                dz, q, (((0,), (0,)), ((), ())), preferred_element_type=F32), first)
            return dq, left_a, left_b

        carry = lax.fori_loop(end + 1, diag, functools.partial(sweep2, on_diagonal=False), (zero, zero, zero))
        dq = sweep2(diag, carry, on_diagonal=True)[0]
        dq_ref[...] = (dq * ATT_SCALE).astype(BF16)

        @pl.when(i == n_blocks - 1)
        def _():
            dk_ref[...] = dk_acc[...].astype(BF16)
            dv_ref[...] = dv_acc[...].astype(BF16)

        bottom()

    out = jax.ShapeDtypeStruct((t_total, n_pairs * LANES), BF16)
    res = pl.pallas_call(
        body, name="attn_bwd", grid=(n_pairs, n_blocks),
        in_specs=[pl.BlockSpec((blk, LANES), lambda h, i: (i, q_col + h)),
                  pl.BlockSpec((blk, LANES), lambda h, i: (i, h)),
                  pl.BlockSpec((t_total, LANES), lambda h, i: (0, k_col + h)),
                  pl.BlockSpec((t_total, LANES), lambda h, i: (0, v_col + h))] + [ANY] * len(rider.operands),
        out_specs=[pl.BlockSpec((blk, LANES), lambda h, i: (i, h)),
                   pl.BlockSpec((t_total, LANES), lambda h, i: (0, h)),
                   pl.BlockSpec((t_total, LANES), lambda h, i: (0, h))] + [ANY] * len(rider.out_shapes),
        out_shape=[out, out, out] + list(rider.out_shapes),
        scratch_shapes=list(rider.scratch) + [pltpu.VMEM((n_slabs, blk, 2 * ATT_SLAB), F32),
                                              pltpu.VMEM((t_total, LANES), F32), pltpu.VMEM((t_total, LANES), F32)],
        compiler_params=_params("arbitrary", "arbitrary"),
    )(q_src, dy, kv_src, kv_src, *rider.operands)
    return res[:3], res[3:]


def _adamw(name, w, g, m, v):
    def fn(wv, gv, mv, vv):
        mn = ADAM_B1 * mv + (1.0 - ADAM_B1) * gv
        vn = ADAM_B2 * vv + (1.0 - ADAM_B2) * (gv * gv)
        m_hat = mn / (1.0 - ADAM_B1 ** ADAM_STEP)
        v_hat = vn / (1.0 - ADAM_B2 ** ADAM_STEP)
        return -ADAM_LR * (m_hat / (jnp.sqrt(v_hat) + ADAM_EPS) + ADAM_WD * wv), mn, vn

    rows = w.shape[0]
    tr = _row_tile(rows)
    shp = jax.ShapeDtypeStruct(w.shape, F32)
    if rows == 1:
        def body(w_ref, g_ref, m_ref, v_ref, d_ref, mo_ref, vo_ref):
            d, mn, vn = fn(w_ref[...], g_ref[...], m_ref[...], v_ref[...])
            d_ref[...], mo_ref[...], vo_ref[...] = d, mn, vn

        return pl.pallas_call(body, name=name, out_shape=[shp, shp, shp])(w, g, m, v)
    return _rows(name, fn, [w, g, m, v], [shp, shp, shp], tr=tr)


def _place():
    return lax.axis_index("x"), lax.axis_index("y"), lax.axis_index("c")


def _other_chips(x, y):
    return [(1 - x, y), (x, 1 - y), (1 - x, 1 - y)]


ANY = pl.BlockSpec(memory_space=pl.ANY)


def _remote(src, dst, send_sem, recv_sem, to):
    return pltpu.make_async_remote_copy(src_ref=src, dst_ref=dst, send_sem=send_sem, recv_sem=recv_sem,
                                        device_id=to, device_id_type=MESH)


class _WeightGather(_NoRider):
    def __init__(self, shards):
        n_w = len(shards)
        self.operands = list(shards)
        self.out_shapes = [jax.ShapeDtypeStruct((N_CHIPS,) + s.shape, s.dtype) for s in shards]
        self.scratch = [pltpu.SemaphoreType.DMA((3, n_w))] * 4 + [pltpu.SemaphoreType.DMA((n_w,))] * 2

    def _copies(self, ins, outs, sems):
        send_sems, recv_sems, relay_send, relay_recv, own_send, own_recv = sems
        x, y, c = _place()
        my_chip, sibling = 2 * x + y, (x, y, 1 - c)
        n_w = len(ins)

        def half(w, chip, core):
            h = self.operands[w].shape[0] // 2
            return outs[w].at[chip, pl.ds(core * h, h)]

        own = [_remote(ins[w], outs[w].at[my_chip], own_send.at[w], own_recv.at[w], sibling) for w in range(n_w)]
        sends, landed, relays, relayed = [], [], [], []
        for p, (ox, oy) in enumerate(_other_chips(x, y)):
            for w in range(n_w):
                h = self.operands[w].shape[0] // 2
                sends.append(_remote(ins[w].at[pl.ds(c * h, h)], half(w, my_chip, c), send_sems.at[p, w],
                                     recv_sems.at[p, w], (ox, oy, c)))
                here = half(w, 2 * ox + oy, c)
                landed.append(_remote(here, here, send_sems.at[p, w], recv_sems.at[p, w], (ox, oy, c)))
                relays.append(_remote(here, here, relay_send.at[p, w], relay_recv.at[p, w], sibling))
                there = half(w, 2 * ox + oy, 1 - c)
                relayed.append(_remote(there, there, relay_send.at[p, w], relay_recv.at[p, w], sibling))
        return own, sends, landed, relays, relayed

    def start(self, ins, outs, sems):
        own, sends, _, _, _ = self._copies(ins, outs, sems)
        for cp in own + sends:
            cp.start()

    def relay(self, ins, outs, sems):
        _, _, landed, relays, _ = self._copies(ins, outs, sems)
        for arrival, cp in zip(landed, relays):
            arrival.wait_recv()
            cp.start()

    def finish(self, ins, outs, sems):
        own, sends, _, relays, relayed = self._copies(ins, outs, sems)
        for arrival in relayed:
            arrival.wait_recv()
        for cp in sends + relays:
            cp.wait_send()
        for cp in own:
            cp.wait()


class _ChipExchange(_NoRider):
    def __init__(self, pair_sums):
        n_w = len(pair_sums)
        self.operands = list(pair_sums)
        self.out_shapes = [jax.ShapeDtypeStruct((3,) + s.shape[1:], s.dtype) for s in pair_sums]
        self.scratch = [pltpu.SemaphoreType.DMA((3, n_w))] * 2

    def _copies(self, ins, outs, sems):
        send_sems, recv_sems = sems
        x, y, c = _place()
        return [_remote(ins[w].at[2 * ox + oy], outs[w].at[p], send_sems.at[p, w], recv_sems.at[p, w], (ox, oy, c))
                for p, (ox, oy) in enumerate(_other_chips(x, y)) for w in range(len(ins))]

    def start(self, ins, outs, sems):
        for cp in self._copies(ins, outs, sems):
            cp.start()

    def finish(self, ins, outs, sems):
        for cp in self._copies(ins, outs, sems):
            cp.wait()


def _pair_exchange(name, grads):
    n_w = len(grads)

    def halves(w):
        return grads[w].shape[1] // 2

    def body(*refs):
        ins, theirs = refs[:n_w], refs[n_w:2 * n_w]
        send_sems, recv_sems = refs[2 * n_w:]
        x, y, c = _place()
        sends = [pltpu.make_async_remote_copy(
            src_ref=ins[w].at[:, pl.ds((1 - c) * halves(w), halves(w)), :], dst_ref=theirs[w],
            send_sem=send_sems.at[w], recv_sem=recv_sems.at[w], device_id=(x, y, 1 - c), device_id_type=MESH)
            for w in range(n_w)]
        for cp in sends:
            cp.start()
        for cp in sends:
            cp.wait()

    return pl.pallas_call(
        body, name=name, in_specs=[ANY] * n_w, out_specs=[ANY] * n_w,
        out_shape=[jax.ShapeDtypeStruct((N_CHIPS, halves(w), grads[w].shape[2]), F32) for w in range(n_w)],
        scratch_shapes=[pltpu.SemaphoreType.DMA((n_w,)), pltpu.SemaphoreType.DMA((n_w,))],
    )(*grads)


def _pair_share(shards):
    n_w = len(shards)

    def body(*refs):
        ins, outs = refs[:n_w], refs[n_w:2 * n_w]
        send_sems, recv_sems = refs[2 * n_w:]
        x, y, c = _place()
        sends = []
        for w in range(n_w):
            h = shards[w].shape[0] // 2
            mine = outs[w].at[pl.ds(c * h, h)]
            sends.append(pltpu.make_async_remote_copy(
                src_ref=mine, dst_ref=mine, send_sem=send_sems.at[w], recv_sem=recv_sems.at[w],
                device_id=(x, y, 1 - c), device_id_type=MESH))
        for cp in sends:
            cp.start()
        for w in range(n_w):
            h = shards[w].shape[0] // 2
            theirs = outs[w].at[pl.ds((1 - c) * h, h)]
            pltpu.make_async_remote_copy(
                src_ref=theirs, dst_ref=theirs, send_sem=send_sems.at[w], recv_sem=recv_sems.at[w],
                device_id=(x, y, 1 - c), device_id_type=MESH).wait_recv()
        for cp in sends:
            cp.wait_send()

    return pl.pallas_call(
        body, name="pair_share", in_specs=[ANY] * n_w, out_specs=[ANY] * n_w,
        out_shape=[jax.ShapeDtypeStruct(s.shape, s.dtype) for s in shards],
        input_output_aliases={w: w for w in range(n_w)},
        scratch_shapes=[pltpu.SemaphoreType.DMA((n_w,)), pltpu.SemaphoreType.DMA((n_w,))],
    )(*shards)


def _all_reduce_small(vec):
    rows = vec.shape[0]

    def body(v_ref, o_ref, slots, send_sems, recv_sems):
        x, y, c = _place()
        me = 4 * x + 2 * y + c
        slots[me] = v_ref[...]
        sends = []
        for k in range(1, N_DEV):
            peer = (x ^ (k >> 2), y ^ ((k >> 1) & 1), c ^ (k & 1))
            sends.append(pltpu.make_async_remote_copy(
                src_ref=v_ref, dst_ref=slots.at[me], send_sem=send_sems.at[k - 1], recv_sem=recv_sems.at[k - 1],
                device_id=peer, device_id_type=MESH))
        for cp in sends:
            cp.start()
        for k in range(1, N_DEV):
            px, py, pc = x ^ (k >> 2), y ^ ((k >> 1) & 1), c ^ (k & 1)
            landed = slots.at[4 * px + 2 * py + pc]
            pltpu.make_async_remote_copy(
                src_ref=landed, dst_ref=landed, send_sem=send_sems.at[k - 1], recv_sem=recv_sems.at[k - 1],
                device_id=(px, py, pc), device_id_type=MESH).wait_recv()
        for cp in sends:
            cp.wait_send()
        total = slots[0]
        for d in range(1, N_DEV):
            total = total + slots[d]
        o_ref[...] = total

    vm = pl.BlockSpec(memory_space=pltpu.VMEM)
    return pl.pallas_call(
        body, name="all_reduce_small", in_specs=[vm], out_specs=vm, out_shape=jax.ShapeDtypeStruct(vec.shape, F32),
        scratch_shapes=[pltpu.VMEM((N_DEV, rows, LANES), F32), pltpu.SemaphoreType.DMA((N_DEV - 1,)),
                        pltpu.SemaphoreType.DMA((N_DEV - 1,))],
    )(vec)


def _row_tile(rows):
    fits = [tr for tr in range(16, min(rows, 512) + 1, 16) if rows % tr == 0]
    return max(fits) if fits else rows


def _pair_sum(name, place, grad, theirs):
    n, r, c = grad.shape
    half = r // 2
    tr = _row_tile(half)
    nb = half // tr

    def body(place_ref, g_ref, t_ref, o_ref):
        o_ref[...] = (g_ref[...] + t_ref[...]).astype(BF16)

    return pl.pallas_call(
        body, name=name, out_shape=jax.ShapeDtypeStruct((n, half, c), BF16),
        grid_spec=pltpu.PrefetchScalarGridSpec(
            num_scalar_prefetch=1, grid=(n, nb),
            in_specs=[pl.BlockSpec((1, tr, c), lambda j, i, pr: (j, pr[0] * nb + i, 0)),
                      pl.BlockSpec((1, tr, c), lambda j, i, pr: (j, i, 0))],
            out_specs=pl.BlockSpec((1, tr, c), lambda j, i, pr: (j, i, 0))),
        compiler_params=_params("parallel", "parallel"),
    )(place, grad, theirs)


def _sum_chips(name, place, pair_sums, landed):
    _, half, c = pair_sums.shape
    tr = _row_tile(half)
    nb = half // tr

    def body(place_ref, s_ref, q_ref, o_ref):
        total = s_ref[0].astype(F32)
        for p in range(3):
            total = total + q_ref[p].astype(F32)
        o_ref[...] = total

    return pl.pallas_call(
        body, name=name, out_shape=jax.ShapeDtypeStruct((2 * half, c), F32),
        grid_spec=pltpu.PrefetchScalarGridSpec(
            num_scalar_prefetch=1, grid=(nb,),
            in_specs=[pl.BlockSpec((1, tr, c), lambda i, pr: (pr[1], i, 0)),
                      pl.BlockSpec((3, tr, c), lambda i, pr: (0, i, 0))],
            out_specs=pl.BlockSpec((tr, c), lambda i, pr: (pr[0] * nb + i, 0))),
        compiler_params=_params("parallel"),
    )(place, pair_sums, landed)


BIG = ("w_in", "w_branch_a", "w_branch_b", "w_out", "w_ffn_gate", "w_ffn_up", "w_ffn_down", "w_ple_gate", "w_ple_proj")
LATE = BIG[1:]
COLUMN_SHARDED = ("w_in", "w_branch_a", "w_branch_b", "w_ffn_gate", "w_ffn_up", "w_ple_proj")
SMALL = ("norm_mix", "w_pool", "pool_scale", "norm_ffn", "norm_ple", "norm_final")


def _join_columns(w4):
    return jnp.concatenate([w4[j] for j in range(N_CHIPS)], axis=1)


def _split_columns(g):
    k, n = g.shape
    return g.reshape(k, N_CHIPS, n // N_CHIPS).transpose(1, 0, 2)


def _sds(shape, dtype):
    return jax.ShapeDtypeStruct(shape, dtype)


def _local_step(x, p, target, wf, small, gather_first=None, gather_late=None, exchange_early=None,
                exchange_last=None):
    t, d = x.shape
    w_pool_b = small["w_pool"].astype(BF16)
    dp = w_pool_b.shape[0] * w_pool_b.shape[1]

    h1, first = _norm_fwd("norm_mix", x, small["norm_mix"], rider=gather_first)
    w_in = first[0] if gather_first else wf["w_in"]
    u, q, kv, ga, gb = _mm(
        "proj", [h1], [w_in[j] for j in range(N_CHIPS)], "nn",
        [_sds((t, dp), F32), _sds((t, dp), BF16), _sds((t, d), BF16), _sds((t, d), BF16), _sds((t, d), BF16)],
        separate=True, epilogue=lambda uq, kv_, ga_, gb_: (uq[:, :dp], uq[:, dp:], kv_, ga_, gb_), tm=512)
    pooled, ya = _pool_fwd(u, w_pool_b, small["pool_scale"])
    n_pairs = dp // LANES
    yb, late = _attn_fwd(q, 0, kv, 0, n_pairs, n_pairs, rider=gather_late or _NoRider())
    wf = {**wf, **dict(zip(LATE, late))}
    w_gate, w_up = _join_columns(wf["w_ffn_gate"]), _join_columns(wf["w_ffn_up"])
    dff = w_gate.shape[1]
    w_down = wf["w_ffn_down"].reshape(dff, d)
    w_a, w_b, w_pp = _join_columns(wf["w_branch_a"]), _join_columns(wf["w_branch_b"]), _join_columns(wf["w_ple_proj"])
    w_out = wf["w_out"].reshape(d, d)
    w_pg = wf["w_ple_gate"].reshape(d, d)
    ta, tb, merged = _mm(
        "branches_merge", [ya, yb], [w_a, w_b], "nn", [_sds((t, d), BF16)] * 3, extras=[ga, gb], separate=True,
        epilogue=lambda tav, tbv, gav, gbv: (tav, tbv, _sigmoid(gav) * tav + _sigmoid(gbv) * tbv), tm=512)
    def residual_norm(acc, xv, g):
        xn = acc + xv
        return xn, xn * lax.rsqrt(jnp.mean(xn * xn, axis=-1, keepdims=True) + RMS_EPS) * g

    x1, h2 = _mm("mix_out", [merged], [w_out], "nn", [_sds((t, d), F32), _sds((t, d), BF16)],
                 extras=[x, small["norm_ffn"]], epilogue=residual_norm)
    gate, up, act = _mm("ffn_gate_up", [h2], [w_gate, w_up], "nn", [_sds((t, dff), BF16)] * 3, separate=True,
                        epilogue=lambda gv, uv: (gv, uv, gv * _sigmoid(gv) * uv), tm=512, tn=dff // 2)
    x2, h3 = _mm("ffn_down", [act], [w_down], "nn", [_sds((t, d), F32), _sds((t, d), BF16)],
                 extras=[x1, small["norm_ple"]], epilogue=residual_norm, tm=512)
    gp, pp, x3 = _mm(
        "ple", [h3, p], [w_pg, w_pp], "nn", [_sds((t, d), BF16), _sds((t, d), BF16), _sds((t, d), F32)], extras=[x2],
        separate=True, epilogue=lambda gv, pv, xv: (gv, pv, xv + _sigmoid(gv) * pv), tm=512)
    (dx3, d_pp, d_gp), (d_norm_final, loss_row) = _split2(
        _final_loss(x3, target, small["norm_final"].reshape(1, d), gp, pp), 3)

    def through_norm(dh, xv, g, dres):
        dx, d_gain = _rms_norm_bwd(dh, xv, g)
        return dx + dres, dx + dres, d_gain

    stream = [_sds((t, d), F32), _sds((t, d), BF16)]
    gain_sum = [_sds((1, d), F32)]
    g_w_pp, = _mm_tn("g_ple_proj", p, [d_pp])
    g_w_pg, = _mm_tn("g_ple_gate", h3, [d_gp])
    dx2, dx2_b, d_norm_ple = _mm("d_h3", [d_gp], [w_pg], "nt", stream, extras=[x2, small["norm_ple"], dx3],
                                 epilogue=through_norm, sum_shapes=gain_sum, tm=512)

    def ffn_bwd(acc, gv, uv):
        s = _sigmoid(gv)
        return acc * uv * (s * (1.0 + gv * (1.0 - s))), acc * (gv * s)

    d_gate, d_up = _mm("d_act", [dx2_b], [w_down], "nt", [_sds((t, dff), BF16)] * 2, extras=[gate, up],
                       epilogue=ffn_bwd, tm=512, tn=dff // 2)
    g_w_down, = _mm_tn("g_ffn_down", act, [dx2_b], tmm=512)
    g_w_gate, g_w_up = _mm_tn("g_ffn_gate_up", h2, [d_gate, d_up], n_blocks=2)
    dx1, dx1_b, d_norm_ffn = _mm(
        "d_h2", [d_gate, d_up], [w_gate, w_up], "nt", stream, extras=[x1, small["norm_ffn"], dx2],
        epilogue=through_norm, sum_shapes=gain_sum, tm=512)

    def merge_bwd(acc, tav, tbv, gav, gbv):
        sa, sb = _sigmoid(gav), _sigmoid(gbv)
        return acc * sa, acc * sb, acc * tav * sa * (1.0 - sa), acc * tbv * sb * (1.0 - sb)

    d_ta, d_tb, d_ga, d_gb = _mm("d_merged", [dx1_b], [w_out], "nt", [_sds((t, d), BF16)] * 4,
                                 extras=[ta, tb, ga, gb], epilogue=merge_bwd, tm=512)
    g_w_out, = _mm_tn("g_w_out", merged, [dx1_b])
    g_w_a, = _mm_tn("g_branch_a", ya, [d_ta])
    g_w_b, = _mm_tn("g_branch_b", yb, [d_tb])
    d_ya, = _mm("d_ya", [d_ta], [w_a], "nt", [_sds((t, dp), F32)])
    d_yb, = _mm("d_yb", [d_tb], [w_b], "nt", [_sds((t, dp), BF16)])
    d_u, g_w_pool, d_pool_scale = _pool_bwd(d_ya, pooled, w_pool_b, small["pool_scale"])
    big = {
        "w_branch_a": _split_columns(g_w_a), "w_branch_b": _split_columns(g_w_b),
        "w_out": g_w_out.reshape(wf["w_out"].shape), "w_ffn_gate": _split_columns(g_w_gate),
        "w_ffn_up": _split_columns(g_w_up), "w_ffn_down": g_w_down.reshape(wf["w_ffn_down"].shape),
        "w_ple_gate": g_w_pg.reshape(wf["w_ple_gate"].shape),
        "w_ple_proj": _split_columns(g_w_pp),
    }
    rider = exchange_early(big) if exchange_early else _NoRider()
    (d_q, d_k, d_v), early = _attn_bwd(q, 0, kv, 0, n_pairs, d_yb, n_pairs, rider=rider)
    d_proj = [(d_u, d_q), (d_k, d_v), d_ga, d_gb]
    big["w_in"], = _mm_tn("g_w_in", h1, d_proj, tmm=512, stacked=True)
    rider = exchange_last(big["w_in"]) if exchange_last else _NoRider()
    res = _mm(
        "d_h1", d_proj, [w_in[j] for j in range(N_CHIPS)], "nt", [_sds((t, d), F32)],
        extras=[x, small["norm_mix"], dx1], epilogue=lambda dh, xv, g, dres: through_norm(dh, xv, g, dres)[1:],
        sum_shapes=gain_sum, tm=512, rider=rider)
    (grad_x, d_norm_mix), last = res if rider.operands else (res, ())
    small_g = {"norm_mix": d_norm_mix, "w_pool": g_w_pool, "pool_scale": d_pool_scale, "norm_ffn": d_norm_ffn,
               "norm_ple": d_norm_ple, "norm_final": d_norm_final}
    return grad_x, big, small_g, loss_row, early, last


def _split2(res, n):
    return res[:n], res[n:]


def _pack_small(small_g, loss_row):
    parts, layout = [], []
    for name in SMALL + ("loss",):
        v = (loss_row if name == "loss" else small_g[name]).reshape(-1, LANES)
        pad = (-v.shape[0]) % 8
        if pad:
            v = jnp.concatenate([v, jnp.zeros((pad, LANES), F32)], axis=0)
        layout.append((name, sum(q.shape[0] for q in parts), v.shape[0]))
        parts.append(v)
    return jnp.concatenate(parts, axis=0), layout


def kernel(x, p, norm_mix, w_in, w_pool, pool_scale, w_branch_a, w_branch_b, w_out, norm_ffn, w_ffn_gate, w_ffn_up, w_ffn_down, norm_ple, w_ple_gate, w_ple_proj, norm_final, loss_target, m_norm_mix, m_w_in, m_w_pool, m_pool_scale, m_w_branch_a, m_w_branch_b, m_w_out, m_norm_ffn, m_w_ffn_gate, m_w_ffn_up, m_w_ffn_down, m_norm_ple, m_w_ple_gate, m_w_ple_proj, m_norm_final, v_norm_mix, v_w_in, v_w_pool, v_pool_scale, v_w_branch_a, v_w_branch_b, v_w_out, v_norm_ffn, v_w_ffn_gate, v_w_ffn_up, v_w_ffn_down, v_norm_ple, v_w_ple_gate, v_w_ple_proj, v_norm_final):
    given = dict(locals())
    names = BIG + SMALL
    order = ("norm_mix", "w_in", "w_pool", "pool_scale", "w_branch_a", "w_branch_b", "w_out", "norm_ffn", "w_ffn_gate",
             "w_ffn_up", "w_ffn_down", "norm_ple", "w_ple_gate", "w_ple_proj", "norm_final")
    t, d = x.shape[1], x.shape[2]
    shard = {n: given[n][0] for n in BIG}
    small = {"norm_mix": norm_mix, "w_pool": w_pool[0], "pool_scale": pool_scale, "norm_ffn": norm_ffn,
             "norm_ple": norm_ple, "norm_final": norm_final}

    as_bf16 = {n: shard[n].astype(BF16) for n in BIG}

    place = jnp.stack([lax.axis_index("c"), 2 * lax.axis_index("x") + lax.axis_index("y")]).astype(jnp.int32)
    pair_sums = {}

    def exchange_early(ready):
        theirs = _pair_exchange("pair_exchange_early", [ready[n] for n in LATE])
        for n, other in zip(LATE, theirs):
            pair_sums[n] = _pair_sum(f"pair_sum_{n}", place, ready[n], other)
        return _ChipExchange([pair_sums[n] for n in LATE])

    def exchange_last(g_w_in):
        theirs, = _pair_exchange("pair_exchange_w_in", [g_w_in])
        pair_sums["w_in"] = _pair_sum("pair_sum_w_in", place, g_w_in, theirs)
        return _ChipExchange([pair_sums["w_in"]])

    grad_x, big_g, small_g, loss_row, early, last = _local_step(
        x.reshape(t, d), p.reshape(t, p.shape[-1]), loss_target.reshape(t, d), {}, small,
        gather_first=_WeightGather([as_bf16["w_in"]]),
        gather_late=_WeightGather([as_bf16[n] for n in LATE]), exchange_early=exchange_early,
        exchange_last=exchange_last)
    landed = dict(zip(LATE + ("w_in",), tuple(early) + tuple(last)))
    halves = [_sum_chips(f"chip_sum_{n}", place, pair_sums[n], landed[n]) for n in BIG]
    grads = dict(zip(BIG, _pair_share(halves)))

    packed, layout = _pack_small(small_g, loss_row)
    reduced = _all_reduce_small(packed)
    for name, start, rows in layout:
        if name == "loss":
            loss = jnp.sum(reduced[start:start + rows])
        else:
            n_el = small[name].size
            grads[name] = reduced[start:start + rows].reshape(-1)[:n_el]

    deltas, new_m, new_v = {}, {}, {}
    for n in order:
        w = shard[n] if n in BIG else small[n]
        shape2 = w.shape if w.ndim == 2 else ((1, w.shape[0]) if w.ndim == 1 else (w.shape[0] * w.shape[1], w.shape[2]))
        g2 = grads[n].reshape(shape2)
        dl, mn, vn = _adamw(f"adamw_{n}", w.reshape(shape2), g2, given["m_" + n].reshape(shape2),
                            given["v_" + n].reshape(shape2))
        full = given[n].shape
        grads[n], deltas[n], new_m[n], new_v[n] = g2.reshape(full), dl.reshape(full), mn.reshape(full), vn.reshape(full)

    return (loss, grad_x.reshape(x.shape), *[grads[n] for n in order], *[deltas[n] for n in order],
            *[new_m[n] for n in order], *[new_v[n] for n in order])
```

```python
import functools
import math

import jax
import jax.numpy as jnp
from jax import lax
from jax.experimental import pallas as pl
from jax.experimental.pallas import tpu as pltpu

F32 = jnp.float32
BF16 = jnp.bfloat16
MESH = pl.DeviceIdType.MESH

RMS_EPS = 1e-6
POOL_WINDOWS = (2, 4, 8, 16)
POOL_HALO = 16
HEAD_DIM = 64
LANES = 128
ATT_BLOCK = 256
ATT_CHUNK = 256
ATT_SLAB = 256
ATT_SCALE = 1.0 / math.sqrt(HEAD_DIM)
LOG2_E = 1.4426950408889634
ATT_EXIT_BELOW = -150.5
ADAM_LR, ADAM_B1, ADAM_B2, ADAM_EPS, ADAM_WD, ADAM_STEP = 0.001, 0.9, 0.999, 1e-08, 0.01, 10
V7X_VMEM_LIMIT_BYTES = 56 * 1024 * 1024
N_CHIPS = 4
N_DEV = 8


def _params(*semantics):
    return pltpu.CompilerParams(dimension_semantics=semantics, vmem_limit_bytes=V7X_VMEM_LIMIT_BYTES)


def _sigmoid(z):
    return 1.0 / (1.0 + jnp.exp(-z))


def _tiled_spec(shape, tm, tn, n_total, at):
    rows, width = shape
    if rows == 1:
        if width == n_total:
            return pl.BlockSpec((1, tn), at(lambda i, j: (0, j)))
        return pl.BlockSpec((1, width), at(lambda i, j: (0, 0)))
    if width == n_total:
        return pl.BlockSpec((tm, tn), at(lambda i, j: (i, j)))
    assert tn == n_total, "an operand narrower than the output needs whole output rows per tile"
    return pl.BlockSpec((tm, width), at(lambda i, j: (i, 0)))


def _column_pieces(operands):
    pieces = [tuple(a) if isinstance(a, (tuple, list)) else (a,) for a in operands]
    return [p for ps in pieces for p in ps], [len(ps) for ps in pieces]


def _load_bf16(refs, counts):
    tiles, k = [], 0
    for n in counts:
        parts = [r[...] for r in refs[k:k + n]]
        parts = [t if t.dtype == BF16 else t.astype(BF16) for t in parts]
        tiles.append(parts[0] if n == 1 else jnp.concatenate(parts, axis=1))
        k += n
    return tiles


def _mm(name, a_list, b_list, mode, out_shapes, epilogue=None, extras=(), tm=1024, tn=None, separate=False,
        sum_shapes=(), rider=None):
    flat_a, counts = _column_pieces(a_list)
    m_total = flat_a[0].shape[0]
    n_total = b_list[0].shape[1] if mode == "nn" else b_list[0].shape[0]
    tn = n_total if tn is None else tn
    tm = min(tm, m_total)
    assert m_total % tm == 0 and n_total % tn == 0 and (not sum_shapes or tn == n_total)
    n_a, n_b, n_extra, n_out = len(counts), len(b_list), len(extras), len(out_shapes)
    assert n_a in (1, n_b)
    dims = (((1,), (0,)), ((), ())) if mode == "nn" else (((1,), (1,)), ((), ()))
    rider = rider or _NoRider()
    grid = (n_total // tn, m_total // tm)

    def at(index):
        return lambda j, i: index(i, j)

    def body(*refs):
        ins, o_refs, _, riding = rider.split(refs, len(flat_a) + n_b + n_extra, n_out + len(sum_shapes))
        a_refs, b_refs, e_refs = ins[:len(flat_a)], ins[len(flat_a):len(flat_a) + n_b], ins[len(flat_a) + n_b:]
        at_first = (pl.program_id(0) == 0) & (pl.program_id(1) == 0)
        at_last = (pl.program_id(0) == grid[0] - 1) & (pl.program_id(1) == grid[1] - 1)
        top, bottom = rider.at_steps(riding, at_first, at_first, at_last)
        top()
        lefts = _load_bf16(a_refs, counts)
        products = [lax.dot_general(lefts[s % n_a], b_refs[s][...], dims, preferred_element_type=F32)
                    for s in range(n_b)]
        if not separate:
            products = [functools.reduce(lambda p, r: p + r, products)]
        extra_tiles = [e[...].astype(F32) for e in e_refs]
        outs = products if epilogue is None else epilogue(*products, *extra_tiles)
        for o_ref, o in zip(o_refs[:n_out], outs[:n_out]):
            o_ref[...] = o.astype(o_ref.dtype)
        if sum_shapes:
            @pl.when(pl.program_id(1) == 0)
            def _():
                for s_ref in o_refs[n_out:]:
                    s_ref[...] = jnp.zeros_like(s_ref)

            for s_ref, s in zip(o_refs[n_out:], outs[n_out:]):
                s_ref[...] += s
        bottom()

    once = dict(pipeline_mode=pl.Buffered(1)) if tn == n_total else {}
    in_specs = [pl.BlockSpec((tm, a.shape[1]), at(lambda i, j: (i, 0))) for a in flat_a]
    if mode == "nn":
        in_specs += [pl.BlockSpec((b.shape[0], tn), at(lambda i, j: (0, j)), **once) for b in b_list]
    else:
        in_specs += [pl.BlockSpec((tn, b.shape[1]), at(lambda i, j: (j, 0)), **once) for b in b_list]
    in_specs += [_tiled_spec(e.shape, tm, tn, n_total, at) for e in extras]
    out_specs = [_tiled_spec(o.shape, tm, tn, n_total, at) for o in out_shapes]
    out_specs += [pl.BlockSpec(s.shape, at(lambda i, j: (0, 0))) for s in sum_shapes]
    semantics = ("arbitrary", "arbitrary") if sum_shapes or rider.operands else ("parallel", "parallel")
    res = pl.pallas_call(
        body, name=name, grid=grid, in_specs=in_specs + [ANY] * len(rider.operands),
        out_specs=out_specs + [ANY] * len(rider.out_shapes),
        out_shape=list(out_shapes) + list(sum_shapes) + list(rider.out_shapes), scratch_shapes=list(rider.scratch),
        compiler_params=_params(*semantics),
    )(*flat_a, *b_list, *extras, *rider.operands)
    n_own = len(out_shapes) + len(sum_shapes)
    return res if not rider.operands else (res[:n_own], res[n_own:])


def _mm_tn(name, a, b_list, tmm=1024, stacked=False, n_blocks=1):
    flat_b, counts = _column_pieces(b_list)
    m_total, k = a.shape
    widths = [sum(p.shape[1] for p in flat_b[sum(counts[:s]):sum(counts[:s + 1])]) for s in range(len(counts))]
    tmm = min(tmm, m_total)
    assert m_total % tmm == 0 and (n_blocks == 1 or max(counts) == 1) and all(w % n_blocks == 0 for w in widths)
    n_b = len(counts)

    def body(*refs):
        a_ref, b_refs, o_refs = refs[0], refs[1:1 + len(flat_b)], refs[1 + len(flat_b):]

        @pl.when(pl.program_id(1) == 0)
        def _():
            for o_ref in o_refs:
                o_ref[...] = jnp.zeros_like(o_ref)

        av, = _load_bf16([a_ref], [1])
        for s, bv in enumerate(_load_bf16(b_refs, counts)):
            product = lax.dot_general(av, bv, (((0,), (0,)), ((), ())), preferred_element_type=F32)
            if stacked:
                o_refs[0][s] += product
            else:
                o_refs[s][...] += product

    in_specs = [pl.BlockSpec((tmm, k), lambda nb, m: (m, 0))]
    in_specs += [pl.BlockSpec((tmm, b.shape[1] // n_blocks), lambda nb, m: (m, nb)) for b in flat_b]
    if stacked:
        out_shape = [jax.ShapeDtypeStruct((n_b, k, widths[0]), F32)]
        out_specs = [pl.BlockSpec((n_b, k, widths[0] // n_blocks), lambda nb, m: (0, 0, nb))]
    else:
        out_shape = [jax.ShapeDtypeStruct((k, w), F32) for w in widths]
        out_specs = [pl.BlockSpec((k, w // n_blocks), lambda nb, m: (0, nb)) for w in widths]
    return pl.pallas_call(
        body, name=name, grid=(n_blocks, m_total // tmm), in_specs=in_specs, out_specs=out_specs, out_shape=out_shape,
        compiler_params=_params("arbitrary", "arbitrary"),
    )(a, *flat_b)


def _rows(name, fn, ins, tile_outs, sum_outs=(), tr=512, rider=None):
    t_total = max(a.shape[0] for a in ins)
    tr = min(tr, t_total)
    assert t_total % tr == 0
    n_in, n_tile = len(ins), len(tile_outs)
    rider = rider or _NoRider()
    n_steps = t_total // tr

    def body(*refs):
        own_ins, own_outs, _, riding = rider.split(refs, n_in, n_tile + len(sum_outs))
        step = pl.program_id(0)
        top, bottom = rider.at_steps(riding, step == 0, step == n_steps - 1, step == n_steps - 1)
        top()
        refs = tuple(own_ins) + tuple(own_outs)
        outs = fn(*[r[...].astype(F32) for r in refs[:n_in]])
        for o_ref, o in zip(refs[n_in:n_in + n_tile], outs[:n_tile]):
            o_ref[...] = o.astype(o_ref.dtype)
        if sum_outs:
            @pl.when(pl.program_id(0) == 0)
            def _():
                for s_ref in refs[n_in + n_tile:]:
                    s_ref[...] = jnp.zeros_like(s_ref)

            for s_ref, s in zip(refs[n_in + n_tile:], outs[n_tile:]):
                s_ref[...] += s
        bottom()

    def spec(shape):
        if shape[0] == 1:
            return pl.BlockSpec(shape, lambda i: (0, 0))
        return pl.BlockSpec((tr, shape[1]), lambda i: (i, 0))

    return pl.pallas_call(
        body, name=name, grid=(n_steps,), in_specs=[spec(a.shape) for a in ins] + [ANY] * len(rider.operands),
        out_specs=[spec(o.shape) for o in tile_outs] + [spec(s.shape) for s in sum_outs] + [ANY] * len(rider.out_shapes),
        out_shape=list(tile_outs) + list(sum_outs) + list(rider.out_shapes), scratch_shapes=list(rider.scratch),
        compiler_params=_params("arbitrary" if sum_outs or rider.operands else "parallel"),
    )(*ins, *rider.operands)


def _norm_fwd(name, x, gain, rider=None):
    def fn(xv, g):
        inv = lax.rsqrt(jnp.mean(xv * xv, axis=-1, keepdims=True) + RMS_EPS)
        return (xv * inv * g,)

    res = _rows(name, fn, [x, gain], [jax.ShapeDtypeStruct(x.shape, BF16)], rider=rider)
    return res[0], res[1:]


def _rms_norm_bwd(dh, xv, g):
    inv = lax.rsqrt(jnp.mean(xv * xv, axis=-1, keepdims=True) + RMS_EPS)
    xn = xv * inv
    dxn = dh * g
    return inv * (dxn - xn * jnp.mean(dxn * xn, axis=-1, keepdims=True)), jnp.sum(dh * xn, axis=0, keepdims=True)


def _ple_and_loss(gv, pv, x2v, tv, g):
    d = x2v.shape[1]
    s = _sigmoid(gv)
    xv = x2v + s * pv
    inv = lax.rsqrt(jnp.mean(xv * xv, axis=-1, keepdims=True) + RMS_EPS)
    err = xv * inv * g - tv
    dx, d_gain = _rms_norm_bwd(err * (1.0 / d), xv, g)
    return dx, dx * s, dx * pv * s * (1.0 - s), d_gain, (0.5 / d) * jnp.sum(err * err, axis=0, keepdims=True)


def _window_counts(t_pos, w):
    return jnp.minimum(t_pos + 1, w).astype(F32)


def _pool_fwd(u, w_pool, scale, tr=512):
    t_total, width = u.shape
    tr = min(tr, t_total)
    n_groups = len(POOL_WINDOWS)
    gdim = width // n_groups
    ext = tr + POOL_HALO

    def body(u_ref, halo_ref, w_ref, s_ref, pooled_ref, ya_ref):
        i = pl.program_id(0)
        halo = jnp.where(i == 0, 0.0, halo_ref[...])
        t_pos = i * tr + lax.broadcasted_iota(jnp.int32, (tr, 1), 0)
        for g, w in enumerate(POOL_WINDOWS):
            cols = slice(g * gdim, (g + 1) * gdim)
            main = u_ref[:, cols]
            win = jnp.concatenate([halo[:, cols], main], axis=0)
            span = 1
            while span < w:
                win = win + pltpu.roll(win, span, 0)
                span *= 2
            pooled = win[POOL_HALO:, :] * (1.0 / _window_counts(t_pos, w)) - main
            pooled_b = pooled.astype(BF16)
            pooled_ref[:, cols] = pooled_b
            mixed = jnp.dot(pooled_b, w_ref[g], preferred_element_type=F32)
            ya_ref[:, cols] = (mixed * s_ref[:, cols]).astype(BF16)

    hb = tr // POOL_HALO
    return pl.pallas_call(
        body, name="pool_fwd", grid=(t_total // tr,),
        in_specs=[pl.BlockSpec((tr, width), lambda i: (i, 0)),
                  pl.BlockSpec((POOL_HALO, width), lambda i: (jnp.maximum(i * hb - 1, 0), 0)),
                  pl.BlockSpec((n_groups, gdim, gdim), lambda i: (0, 0, 0)),
                  pl.BlockSpec((1, width), lambda i: (0, 0))],
        out_specs=[pl.BlockSpec((tr, width), lambda i: (i, 0)), pl.BlockSpec((tr, width), lambda i: (i, 0))],
        out_shape=[jax.ShapeDtypeStruct(u.shape, BF16), jax.ShapeDtypeStruct(u.shape, BF16)],
        compiler_params=_params("parallel"),
    )(u, u, w_pool, scale)


def _pool_bwd(dya, pooled, w_pool, scale, tr=512):
    t_total, width = dya.shape
    tr = min(tr, t_total)
    n_groups = len(POOL_WINDOWS)
    gdim = width // n_groups
    ext = tr + POOL_HALO
    n_tiles = t_total // tr

    def body(d_ref, halo_ref, p_ref, w_ref, s_ref, du_ref, dw_ref, ds_ref):
        i = pl.program_id(0)

        @pl.when(i == 0)
        def _():
            dw_ref[...] = jnp.zeros_like(dw_ref)
            ds_ref[...] = jnp.zeros_like(ds_ref)

        halo = jnp.where(i == n_tiles - 1, 0.0, halo_ref[...])
        t_pos = i * tr + lax.broadcasted_iota(jnp.int32, (ext, 1), 0)
        for g, w in enumerate(POOL_WINDOWS):
            cols = slice(g * gdim, (g + 1) * gdim)
            sc = s_ref[:, cols]
            d_main = d_ref[:, cols]
            pooled_b = p_ref[:, cols]
            mixed = jnp.dot(pooled_b, w_ref[g], preferred_element_type=F32)
            ds_ref[:, cols] += jnp.sum(d_main * mixed, axis=0, keepdims=True)
            dmix = (jnp.concatenate([d_main, halo[:, cols]], axis=0) * sc).astype(BF16)
            dw_ref[g] += lax.dot_general(pooled_b, dmix[:tr, :], (((0,), (0,)), ((), ())),
                                         preferred_element_type=F32)
            dpool = lax.dot_general(dmix, w_ref[g], (((1,), (1,)), ((), ())), preferred_element_type=F32)
            win = dpool * (1.0 / _window_counts(t_pos, w))
            span = 1
            while span < w:
                win = win + pltpu.roll(win, ext - span, 0)
                span *= 2
            du_ref[:, cols] = (win[:tr, :] - dpool[:tr, :]).astype(BF16)

    hb = tr // POOL_HALO
    last_halo = t_total // POOL_HALO - 1
    return pl.pallas_call(
        body, name="pool_bwd", grid=(n_tiles,),
        in_specs=[pl.BlockSpec((tr, width), lambda i: (i, 0)),
                  pl.BlockSpec((POOL_HALO, width), lambda i: (jnp.minimum((i + 1) * hb, last_halo), 0)),
                  pl.BlockSpec((tr, width), lambda i: (i, 0)),
                  pl.BlockSpec((n_groups, gdim, gdim), lambda i: (0, 0, 0)),
                  pl.BlockSpec((1, width), lambda i: (0, 0))],
        out_specs=[pl.BlockSpec((tr, width), lambda i: (i, 0)),
                   pl.BlockSpec((n_groups, gdim, gdim), lambda i: (0, 0, 0)),
                   pl.BlockSpec((1, width), lambda i: (0, 0))],
        out_shape=[jax.ShapeDtypeStruct(dya.shape, BF16), jax.ShapeDtypeStruct((n_groups, gdim, gdim), F32),
                   jax.ShapeDtypeStruct((1, width), F32)],
        compiler_params=_params("arbitrary"),
    )(dya, dya, pooled, w_pool, scale)


def _head_masks():
    lane = lax.broadcasted_iota(jnp.int32, (1, LANES), 1)
    return lane < HEAD_DIM


def _stack_heads(tile, first):
    zero = jnp.zeros_like(tile)
    return jnp.concatenate([jnp.where(first, tile, zero), jnp.where(first, zero, tile)], axis=0)


def _split_bf16(v):
    hi = v.astype(BF16)
    lo = (v - hi.astype(F32)).astype(BF16)
    return hi, lo


def _causal_mask(t_pos, k_start):
    col = lax.broadcasted_iota(jnp.int32, (1, 2 * ATT_SLAB), 1)
    return k_start + (col & (ATT_SLAB - 1)) < t_pos


def _slab_scores(q, kd, mask):
    z2 = lax.dot_general(q, kd, (((1,), (1,)), ((), ())), preferred_element_type=F32) * LOG2_E
    log_hit = jnp.minimum(z2, 0.0) - jnp.log2(1.0 + jnp.exp2(-jnp.abs(z2)))
    log_fail = log_hit - z2
    return log_hit, (log_fail if mask is None else jnp.where(mask, log_fail, 0.0))


def _weights(log_hit, suffix, mask):
    arg = log_hit + suffix
    return jnp.exp2(arg if mask is None else jnp.where(mask, arg, -1e30))


def _tri(upper):
    r = lax.broadcasted_iota(jnp.int32, (2 * ATT_CHUNK, ATT_CHUNK), 0) & (ATT_CHUNK - 1)
    c = lax.broadcasted_iota(jnp.int32, (2 * ATT_CHUNK, ATT_CHUNK), 1)
    return jnp.where(r > c if upper else r < c, 1.0, 0.0).astype(BF16)


def _tri_spec():
    return pl.BlockSpec((2 * ATT_CHUNK, ATT_CHUNK), lambda h, i: (0, 0), pipeline_mode=pl.Buffered(1))


def _scan_chunk(v, tri):
    return jnp.dot(jnp.concatenate(_split_bf16(v), axis=1), tri, preferred_element_type=F32)


def _lane_bcast(col):
    return jnp.broadcast_to(col, (col.shape[0], LANES))


def _scan_slab(v, tri, carries, from_right):
    n_chunks = ATT_SLAB // ATT_CHUNK
    edge = 0 if from_right else ATT_CHUNK - 1
    parts, new_carries = [None] * (2 * n_chunks), []
    for head in range(2):
        run = carries[head]
        for c in (reversed(range(n_chunks)) if from_right else range(n_chunks)):
            lo_col = head * ATT_SLAB + c * ATT_CHUNK
            vc = v[:, lo_col:lo_col + ATT_CHUNK]
            sc = _scan_chunk(vc, tri)
            parts[head * n_chunks + c] = sc + jnp.concatenate([run] * (ATT_CHUNK // LANES), axis=1)
            run = run + _lane_bcast(sc[:, edge:edge + 1] + vc[:, edge:edge + 1])
        new_carries.append(run)
    return jnp.concatenate(parts, axis=1), new_carries


def _fold_heads(stacked, first):
    s = stacked.shape[0] // 2
    return jnp.where(first, stacked[:s], stacked[s:])


class _NoRider:
    operands, out_shapes, scratch = (), (), ()

    def split(self, refs, n_base_in, n_base_out):
        n_in, n_out, n_sem = len(self.operands), len(self.out_shapes), len(self.scratch)
        a = n_base_in + n_in
        b = a + n_base_out + n_out
        mine = (refs[n_base_in:a], refs[a + n_base_out:b], refs[b:b + n_sem])
        return refs[:n_base_in], refs[a:a + n_base_out], refs[b + n_sem:], mine

    def start(self, ins, outs, sems):
        pass

    def relay(self, ins, outs, sems):
        pass

    def finish(self, ins, outs, sems):
        pass

    def at_steps(self, refs, first_step, relay_step, last_step):
        if not self.operands:
            return (lambda: None), (lambda: None)

        def top():
            pl.when(first_step)(lambda: self.start(*refs))
            pl.when(relay_step)(lambda: self.relay(*refs))

        return top, lambda: pl.when(last_step)(lambda: self.finish(*refs))


def _attn_fwd(q_src, q_col, kv_src, k_col, v_col, n_pairs=4, rider=_NoRider()):
    t_total = q_src.shape[0]
    blk = ATT_BLOCK
    n_blocks = t_total // blk
    assert t_total % ATT_SLAB == 0 and ATT_SLAB % ATT_BLOCK == 0

    def body(*refs):
        (q_ref, k_ref, v_ref, suffix_ref), (o_ref,), _, riding = rider.split(refs, 4, 1)
        h, i = pl.program_id(0), pl.program_id(1)
        top, bottom = rider.at_steps(riding, (h == 0) & (i == 0), (h == n_pairs - 1) & (i == 0),
                                     (h == n_pairs - 1) & (i == n_blocks - 1))
        top()
        first = _head_masks()
        q = q_ref[...] * ATT_SCALE
        t_pos = i * blk + lax.broadcasted_iota(jnp.int32, (blk, 1), 0)
        suffix_tri = suffix_ref[...]

        def more(state):
            slab, reach = state[0], state[1]
            return jnp.logical_and(slab >= 0, reach > ATT_EXIT_BELOW)

        def step(state, on_diagonal):
            slab, _, acc, right_a, right_b = state
            k_start = pl.multiple_of(slab * ATT_SLAB, ATT_SLAB)
            kd = _stack_heads(k_ref[pl.ds(k_start, ATT_SLAB), :], first)
            vd = _stack_heads(v_ref[pl.ds(k_start, ATT_SLAB), :], first)
            mask = _causal_mask(t_pos, k_start) if on_diagonal else None
            log_hit, log_fail = _slab_scores(q, kd, mask)
            suffix, (right_a, right_b) = _scan_slab(log_fail, suffix_tri, (right_a, right_b), from_right=True)
            a = _weights(log_hit, suffix, mask).astype(BF16)
            acc = acc + jnp.dot(a, vd, preferred_element_type=F32)
            return slab - 1, jnp.max(jnp.maximum(right_a, right_b)), acc, right_a, right_b

        zero = jnp.zeros((blk, LANES), F32)
        state = step(((i * blk) // ATT_SLAB, jnp.float32(0.0), zero, zero, zero), on_diagonal=True)
        state = lax.while_loop(more, functools.partial(step, on_diagonal=False), state)
        o_ref[...] = state[2].astype(BF16)
        bottom()

    res = pl.pallas_call(
        body, name="attn_fwd", grid=(n_pairs, n_blocks),
        in_specs=[pl.BlockSpec((blk, LANES), lambda h, i: (i, q_col + h)),
                  pl.BlockSpec((t_total, LANES), lambda h, i: (0, k_col + h)),
                  pl.BlockSpec((t_total, LANES), lambda h, i: (0, v_col + h)), _tri_spec()] + [ANY] * len(rider.operands),
        out_specs=[pl.BlockSpec((blk, LANES), lambda h, i: (i, h))] + [ANY] * len(rider.out_shapes),
        out_shape=[jax.ShapeDtypeStruct((t_total, n_pairs * LANES), BF16)] + list(rider.out_shapes),
        scratch_shapes=list(rider.scratch),
        compiler_params=_params("arbitrary", "arbitrary"),
    )(q_src, kv_src, kv_src, _tri(upper=True), *rider.operands)
    return res[0], res[1:]


def _attn_bwd(q_src, q_col, kv_src, k_col, v_col, dy, n_pairs=4, rider=_NoRider()):
    t_total = q_src.shape[0]
    blk = ATT_BLOCK
    n_blocks = t_total // blk
    n_slabs = t_total // ATT_SLAB
    assert t_total % ATT_SLAB == 0 and ATT_SLAB % ATT_BLOCK == 0

    def body(*refs):
        ins, (dq_ref, dk_ref, dv_ref), (g_s, dk_acc, dv_acc), riding = rider.split(refs, 6, 3)
        q_ref, dy_ref, k_ref, v_ref, suffix_ref, prefix_ref = ins
        h, i = pl.program_id(0), pl.program_id(1)
        top, bottom = rider.at_steps(riding, (h == 0) & (i == 0), (h == n_pairs - 1) & (i == 0),
                                     (h == n_pairs - 1) & (i == n_blocks - 1))
        top()

        @pl.when(i == 0)
        def _():
            dk_acc[...] = jnp.zeros_like(dk_acc)
            dv_acc[...] = jnp.zeros_like(dv_acc)

        first = _head_masks()
        q = q_ref[...] * ATT_SCALE
        dy = dy_ref[...]
        t_pos = i * blk + lax.broadcasted_iota(jnp.int32, (blk, 1), 0)
        suffix_tri = suffix_ref[...]
        prefix_tri = prefix_ref[...]
        diag = (i * blk) // ATT_SLAB

        def more(state):
            slab, reach = state[0], state[1]
            return jnp.logical_and(slab >= 0, reach > ATT_EXIT_BELOW)

        def sweep1(state, on_diagonal):
            slab, _, right_a, right_b = state
            k_start = pl.multiple_of(slab * ATT_SLAB, ATT_SLAB)
            kd = _stack_heads(k_ref[pl.ds(k_start, ATT_SLAB), :], first)
            vd = _stack_heads(v_ref[pl.ds(k_start, ATT_SLAB), :], first)
            mask = _causal_mask(t_pos, k_start) if on_diagonal else None
            log_hit, log_fail = _slab_scores(q, kd, mask)
            suffix, (right_a, right_b) = _scan_slab(log_fail, suffix_tri, (right_a, right_b), from_right=True)
            a = _weights(log_hit, suffix, mask)
            da = lax.dot_general(dy, vd, (((1,), (1,)), ((), ())), preferred_element_type=F32)
            g_s[slab] = da * a
            dv_acc[pl.ds(k_start, ATT_SLAB), :] += _fold_heads(lax.dot_general(
                a.astype(BF16), dy, (((0,), (0,)), ((), ())), preferred_element_type=F32), first)
            return slab - 1, jnp.max(jnp.maximum(right_a, right_b)), right_a, right_b

        zero = jnp.zeros((blk, LANES), F32)
        state = sweep1((diag, jnp.float32(0.0), zero, zero), on_diagonal=True)
        end = lax.while_loop(more, functools.partial(sweep1, on_diagonal=False), state)[0]

        def sweep2(slab, carry, on_diagonal):
            dq, left_a, left_b = carry
            k_start = pl.multiple_of(slab * ATT_SLAB, ATT_SLAB)
            kd = _stack_heads(k_ref[pl.ds(k_start, ATT_SLAB), :], first)
            g = g_s[slab]
            z2 = lax.dot_general(q, kd, (((1,), (1,)), ((), ())), preferred_element_type=F32) * LOG2_E
            sig = 1.0 / (1.0 + jnp.exp2(-z2))
            prefix, (left_a, left_b) = _scan_slab(g, prefix_tri, (left_a, left_b), from_right=False)
            dz = g * (1.0 - sig) - sig * prefix
            if on_diagonal:
                dz = jnp.where(_causal_mask(t_pos, k_start), dz, 0.0)
            dz = dz.astype(BF16)
            dq = dq + jnp.dot(dz, kd, preferred_element_type=F32)
            dk_acc[pl.ds(k_start, ATT_SLAB), :] += _fold_heads(lax.dot_general(
                dz, q, (((0,), (0,)), ((), ())), preferred_element_type=F32), first)
            return dq, left_a, left_b

        carry = lax.fori_loop(end + 1, diag, functools.partial(sweep2, on_diagonal=False), (zero, zero, zero))
        dq = sweep2(diag, carry, on_diagonal=True)[0]
        dq_ref[...] = (dq * ATT_SCALE).astype(BF16)

        @pl.when(i == n_blocks - 1)
        def _():
            dk_ref[...] = dk_acc[...].astype(BF16)
            dv_ref[...] = dv_acc[...].astype(BF16)

        bottom()

    out = jax.ShapeDtypeStruct((t_total, n_pairs * LANES), BF16)
    res = pl.pallas_call(
        body, name="attn_bwd", grid=(n_pairs, n_blocks),
        in_specs=[pl.BlockSpec((blk, LANES), lambda h, i: (i, q_col + h)),
                  pl.BlockSpec((blk, LANES), lambda h, i: (i, h)),
                  pl.BlockSpec((t_total, LANES), lambda h, i: (0, k_col + h)),
                  pl.BlockSpec((t_total, LANES), lambda h, i: (0, v_col + h)), _tri_spec(), _tri_spec()]
        + [ANY] * len(rider.operands),
        out_specs=[pl.BlockSpec((blk, LANES), lambda h, i: (i, h)),
                   pl.BlockSpec((t_total, LANES), lambda h, i: (0, h)),
                   pl.BlockSpec((t_total, LANES), lambda h, i: (0, h))] + [ANY] * len(rider.out_shapes),
        out_shape=[out, out, out] + list(rider.out_shapes),
        scratch_shapes=list(rider.scratch) + [pltpu.VMEM((n_slabs, blk, 2 * ATT_SLAB), F32),
                                              pltpu.VMEM((t_total, LANES), F32), pltpu.VMEM((t_total, LANES), F32)],
        compiler_params=_params("arbitrary", "arbitrary"),
    )(q_src, dy, kv_src, kv_src, _tri(upper=True), _tri(upper=False), *rider.operands)
    return res[:3], res[3:]


def _adamw(name, w, g, m, v):
    def fn(wv, gv, mv, vv):
        mn = ADAM_B1 * mv + (1.0 - ADAM_B1) * gv
        vn = ADAM_B2 * vv + (1.0 - ADAM_B2) * (gv * gv)
        m_hat = mn / (1.0 - ADAM_B1 ** ADAM_STEP)
        v_hat = vn / (1.0 - ADAM_B2 ** ADAM_STEP)
        return -ADAM_LR * (m_hat / (jnp.sqrt(v_hat) + ADAM_EPS) + ADAM_WD * wv), mn, vn

    rows = w.shape[0]
    tr = _row_tile(rows)
    shp = jax.ShapeDtypeStruct(w.shape, F32)
    if rows == 1:
        def body(w_ref, g_ref, m_ref, v_ref, d_ref, mo_ref, vo_ref):
            d, mn, vn = fn(w_ref[...], g_ref[...], m_ref[...], v_ref[...])
            d_ref[...], mo_ref[...], vo_ref[...] = d, mn, vn

        return pl.pallas_call(body, name=name, out_shape=[shp, shp, shp])(w, g, m, v)
    return _rows(name, fn, [w, g, m, v], [shp, shp, shp], tr=tr)


def _place():
    return lax.axis_index("x"), lax.axis_index("y"), lax.axis_index("c")


def _other_chips(x, y):
    return [(1 - x, y), (x, 1 - y), (1 - x, 1 - y)]


ANY = pl.BlockSpec(memory_space=pl.ANY)


def _remote(src, dst, send_sem, recv_sem, to):
    return pltpu.make_async_remote_copy(src_ref=src, dst_ref=dst, send_sem=send_sem, recv_sem=recv_sem,
                                        device_id=to, device_id_type=MESH)


class _WeightGather(_NoRider):
    def __init__(self, shards):
        n_w = len(shards)
        self.operands = list(shards)
        self.out_shapes = [jax.ShapeDtypeStruct((N_CHIPS,) + s.shape, s.dtype) for s in shards]
        self.scratch = [pltpu.SemaphoreType.DMA((3, n_w))] * 4 + [pltpu.SemaphoreType.DMA((n_w,))] * 2

    def _copies(self, ins, outs, sems):
        send_sems, recv_sems, relay_send, relay_recv, own_send, own_recv = sems
        x, y, c = _place()
        my_chip, sibling = 2 * x + y, (x, y, 1 - c)
        n_w = len(ins)

        def half(w, chip, core):
            h = self.operands[w].shape[0] // 2
            return outs[w].at[chip, pl.ds(core * h, h)]

        own = [_remote(ins[w], outs[w].at[my_chip], own_send.at[w], own_recv.at[w], sibling) for w in range(n_w)]
        sends, landed, relays, relayed = [], [], [], []
        for p, (ox, oy) in enumerate(_other_chips(x, y)):
            for w in range(n_w):
                h = self.operands[w].shape[0] // 2
                sends.append(_remote(ins[w].at[pl.ds(c * h, h)], half(w, my_chip, c), send_sems.at[p, w],
                                     recv_sems.at[p, w], (ox, oy, c)))
                here = half(w, 2 * ox + oy, c)
                landed.append(_remote(here, here, send_sems.at[p, w], recv_sems.at[p, w], (ox, oy, c)))
                relays.append(_remote(here, here, relay_send.at[p, w], relay_recv.at[p, w], sibling))
                there = half(w, 2 * ox + oy, 1 - c)
                relayed.append(_remote(there, there, relay_send.at[p, w], relay_recv.at[p, w], sibling))
        return own, sends, landed, relays, relayed

    def start(self, ins, outs, sems):
        own, sends, _, _, _ = self._copies(ins, outs, sems)
        for cp in own + sends:
            cp.start()

    def relay(self, ins, outs, sems):
        _, _, landed, relays, _ = self._copies(ins, outs, sems)
        for arrival, cp in zip(landed, relays):
            arrival.wait_recv()
            cp.start()

    def finish(self, ins, outs, sems):
        own, sends, _, relays, relayed = self._copies(ins, outs, sems)
        for arrival in relayed:
            arrival.wait_recv()
        for cp in sends + relays:
            cp.wait_send()
        for cp in own:
            cp.wait()


class _ChipExchange(_NoRider):
    def __init__(self, pair_sums):
        n_w = len(pair_sums)
        self.operands = list(pair_sums)
        self.out_shapes = [jax.ShapeDtypeStruct((3,) + s.shape[1:], s.dtype) for s in pair_sums]
        self.scratch = [pltpu.SemaphoreType.DMA((3, n_w))] * 2

    def _copies(self, ins, outs, sems):
        send_sems, recv_sems = sems
        x, y, c = _place()
        return [_remote(ins[w].at[2 * ox + oy], outs[w].at[p], send_sems.at[p, w], recv_sems.at[p, w], (ox, oy, c))
                for p, (ox, oy) in enumerate(_other_chips(x, y)) for w in range(len(ins))]

    def start(self, ins, outs, sems):
        for cp in self._copies(ins, outs, sems):
            cp.start()

    def finish(self, ins, outs, sems):
        for cp in self._copies(ins, outs, sems):
            cp.wait()


def _pair_exchange(name, grads):
    n_w = len(grads)

    def halves(w):
        return grads[w].shape[1] // 2

    def body(*refs):
        ins, theirs = refs[:n_w], refs[n_w:2 * n_w]
        send_sems, recv_sems = refs[2 * n_w:]
        x, y, c = _place()
        sends = [pltpu.make_async_remote_copy(
            src_ref=ins[w].at[:, pl.ds((1 - c) * halves(w), halves(w)), :], dst_ref=theirs[w],
            send_sem=send_sems.at[w], recv_sem=recv_sems.at[w], device_id=(x, y, 1 - c), device_id_type=MESH)
            for w in range(n_w)]
        for cp in sends:
            cp.start()
        for cp in sends:
            cp.wait()

    return pl.pallas_call(
        body, name=name, in_specs=[ANY] * n_w, out_specs=[ANY] * n_w,
        out_shape=[jax.ShapeDtypeStruct((N_CHIPS, halves(w), grads[w].shape[2]), F32) for w in range(n_w)],
        scratch_shapes=[pltpu.SemaphoreType.DMA((n_w,)), pltpu.SemaphoreType.DMA((n_w,))],
    )(*grads)


def _pair_share(shards):
    n_w = len(shards)

    def body(*refs):
        ins, outs = refs[:n_w], refs[n_w:2 * n_w]
        send_sems, recv_sems = refs[2 * n_w:]
        x, y, c = _place()
        sends = []
        for w in range(n_w):
            h = shards[w].shape[0] // 2
            mine = outs[w].at[pl.ds(c * h, h)]
            sends.append(pltpu.make_async_remote_copy(
                src_ref=mine, dst_ref=mine, send_sem=send_sems.at[w], recv_sem=recv_sems.at[w],
                device_id=(x, y, 1 - c), device_id_type=MESH))
        for cp in sends:
            cp.start()
        for w in range(n_w):
            h = shards[w].shape[0] // 2
            theirs = outs[w].at[pl.ds((1 - c) * h, h)]
            pltpu.make_async_remote_copy(
                src_ref=theirs, dst_ref=theirs, send_sem=send_sems.at[w], recv_sem=recv_sems.at[w],
                device_id=(x, y, 1 - c), device_id_type=MESH).wait_recv()
        for cp in sends:
            cp.wait_send()

    return pl.pallas_call(
        body, name="pair_share", in_specs=[ANY] * n_w, out_specs=[ANY] * n_w,
        out_shape=[jax.ShapeDtypeStruct(s.shape, s.dtype) for s in shards],
        input_output_aliases={w: w for w in range(n_w)},
        scratch_shapes=[pltpu.SemaphoreType.DMA((n_w,)), pltpu.SemaphoreType.DMA((n_w,))],
    )(*shards)


def _all_reduce_small(vec):
    rows = vec.shape[0]

    def body(v_ref, o_ref, slots, send_sems, recv_sems):
        x, y, c = _place()
        me = 4 * x + 2 * y + c
        slots[me] = v_ref[...]
        sends = []
        for k in range(1, N_DEV):
            peer = (x ^ (k >> 2), y ^ ((k >> 1) & 1), c ^ (k & 1))
            sends.append(pltpu.make_async_remote_copy(
                src_ref=v_ref, dst_ref=slots.at[me], send_sem=send_sems.at[k - 1], recv_sem=recv_sems.at[k - 1],
                device_id=peer, device_id_type=MESH))
        for cp in sends:
            cp.start()
        for k in range(1, N_DEV):
            px, py, pc = x ^ (k >> 2), y ^ ((k >> 1) & 1), c ^ (k & 1)
            landed = slots.at[4 * px + 2 * py + pc]
            pltpu.make_async_remote_copy(
                src_ref=landed, dst_ref=landed, send_sem=send_sems.at[k - 1], recv_sem=recv_sems.at[k - 1],
                device_id=(px, py, pc), device_id_type=MESH).wait_recv()
        for cp in sends:
            cp.wait_send()
        total = slots[0]
        for d in range(1, N_DEV):
            total = total + slots[d]
        o_ref[...] = total

    vm = pl.BlockSpec(memory_space=pltpu.VMEM)
    return pl.pallas_call(
        body, name="all_reduce_small", in_specs=[vm], out_specs=vm, out_shape=jax.ShapeDtypeStruct(vec.shape, F32),
        scratch_shapes=[pltpu.VMEM((N_DEV, rows, LANES), F32), pltpu.SemaphoreType.DMA((N_DEV - 1,)),
                        pltpu.SemaphoreType.DMA((N_DEV - 1,))],
    )(vec)


def _row_tile(rows):
    fits = [tr for tr in range(16, min(rows, 512) + 1, 16) if rows % tr == 0]
    return max(fits) if fits else rows


def _pair_sum(name, place, grad, theirs):
    n, r, c = grad.shape
    half = r // 2
    tr = _row_tile(half)
    nb = half // tr

    def body(place_ref, g_ref, t_ref, o_ref):
        o_ref[...] = (g_ref[...] + t_ref[...]).astype(BF16)

    return pl.pallas_call(
        body, name=name, out_shape=jax.ShapeDtypeStruct((n, half, c), BF16),
        grid_spec=pltpu.PrefetchScalarGridSpec(
            num_scalar_prefetch=1, grid=(n, nb),
            in_specs=[pl.BlockSpec((1, tr, c), lambda j, i, pr: (j, pr[0] * nb + i, 0)),
                      pl.BlockSpec((1, tr, c), lambda j, i, pr: (j, i, 0))],
            out_specs=pl.BlockSpec((1, tr, c), lambda j, i, pr: (j, i, 0))),
        compiler_params=_params("parallel", "parallel"),
    )(place, grad, theirs)


def _sum_chips(name, place, pair_sums, landed):
    _, half, c = pair_sums.shape
    tr = _row_tile(half)
    nb = half // tr

    def body(place_ref, s_ref, q_ref, o_ref):
        total = s_ref[0].astype(F32)
        for p in range(3):
            total = total + q_ref[p].astype(F32)
        o_ref[...] = total

    return pl.pallas_call(
        body, name=name, out_shape=jax.ShapeDtypeStruct((2 * half, c), F32),
        grid_spec=pltpu.PrefetchScalarGridSpec(
            num_scalar_prefetch=1, grid=(nb,),
            in_specs=[pl.BlockSpec((1, tr, c), lambda i, pr: (pr[1], i, 0)),
                      pl.BlockSpec((3, tr, c), lambda i, pr: (0, i, 0))],
            out_specs=pl.BlockSpec((tr, c), lambda i, pr: (pr[0] * nb + i, 0))),
        compiler_params=_params("parallel"),
    )(place, pair_sums, landed)


BIG = ("w_in", "w_branch_a", "w_branch_b", "w_out", "w_ffn_gate", "w_ffn_up", "w_ffn_down", "w_ple_gate", "w_ple_proj")
LATE = BIG[1:]
COLUMN_SHARDED = ("w_in", "w_branch_a", "w_branch_b", "w_ffn_gate", "w_ffn_up", "w_ple_proj")
SMALL = ("norm_mix", "w_pool", "pool_scale", "norm_ffn", "norm_ple", "norm_final")


def _join_columns(w4):
    return jnp.concatenate([w4[j] for j in range(N_CHIPS)], axis=1)


def _split_columns(g):
    k, n = g.shape
    return g.reshape(k, N_CHIPS, n // N_CHIPS).transpose(1, 0, 2)


def _sds(shape, dtype):
    return jax.ShapeDtypeStruct(shape, dtype)


def _local_step(x, p, target, wf, small, gather_first=None, gather_late=None, exchange_early=None,
                exchange_last=None):
    t, d = x.shape
    w_pool_b = small["w_pool"].astype(BF16)
    dp = w_pool_b.shape[0] * w_pool_b.shape[1]

    h1, first = _norm_fwd("norm_mix", x, small["norm_mix"], rider=gather_first)
    w_in = first[0] if gather_first else wf["w_in"]
    u, q, kv, ga, gb = _mm(
        "proj", [h1], [w_in[j] for j in range(N_CHIPS)], "nn",
        [_sds((t, dp), F32), _sds((t, dp), BF16), _sds((t, d), BF16), _sds((t, d), BF16), _sds((t, d), BF16)],
        separate=True, epilogue=lambda uq, kv_, ga_, gb_: (uq[:, :dp], uq[:, dp:], kv_, ga_, gb_), tm=512)
    pooled, ya = _pool_fwd(u, w_pool_b, small["pool_scale"])
    n_pairs = dp // LANES
    yb, late = _attn_fwd(q, 0, kv, 0, n_pairs, n_pairs, rider=gather_late or _NoRider())
    wf = {**wf, **dict(zip(LATE, late))}
    w_gate, w_up = _join_columns(wf["w_ffn_gate"]), _join_columns(wf["w_ffn_up"])
    dff = w_gate.shape[1]
    w_down = wf["w_ffn_down"].reshape(dff, d)
    w_a, w_b, w_pp = _join_columns(wf["w_branch_a"]), _join_columns(wf["w_branch_b"]), _join_columns(wf["w_ple_proj"])
    w_out = wf["w_out"].reshape(d, d)
    w_pg = wf["w_ple_gate"].reshape(d, d)
    ta, tb, merged = _mm(
        "branches_merge", [ya, yb], [w_a, w_b], "nn", [_sds((t, d), BF16)] * 3, extras=[ga, gb], separate=True,
        epilogue=lambda tav, tbv, gav, gbv: (tav, tbv, _sigmoid(gav) * tav + _sigmoid(gbv) * tbv), tm=512)
    def residual_norm(acc, xv, g):
        xn = acc + xv
        return xn, xn * lax.rsqrt(jnp.mean(xn * xn, axis=-1, keepdims=True) + RMS_EPS) * g

    x1, h2 = _mm("mix_out", [merged], [w_out], "nn", [_sds((t, d), F32), _sds((t, d), BF16)],
                 extras=[x, small["norm_ffn"]], epilogue=residual_norm)
    gate, up, act = _mm("ffn_gate_up", [h2], [w_gate, w_up], "nn", [_sds((t, dff), BF16)] * 3, separate=True,
                        epilogue=lambda gv, uv: (gv, uv, gv * _sigmoid(gv) * uv), tm=512, tn=dff // 2)
    x2, h3 = _mm("ffn_down", [act], [w_down], "nn", [_sds((t, d), F32), _sds((t, d), BF16)],
                 extras=[x1, small["norm_ple"]], epilogue=residual_norm, tm=512)
    dx3, d_pp, d_gp, d_norm_final, loss_row = _mm(
        "ple_loss", [h3, p], [w_pg, w_pp], "nn", [_sds((t, d), F32), _sds((t, d), BF16), _sds((t, d), BF16)],
        extras=[x2, target, small["norm_final"].reshape(1, d)], separate=True, epilogue=_ple_and_loss,
        sum_shapes=[_sds((1, d), F32)] * 2, tm=512)

    def through_norm(dh, xv, g, dres):
        dx, d_gain = _rms_norm_bwd(dh, xv, g)
        return dx + dres, dx + dres, d_gain

    stream = [_sds((t, d), F32), _sds((t, d), BF16)]
    gain_sum = [_sds((1, d), F32)]
    g_w_pp, = _mm_tn("g_ple_proj", p, [d_pp])
    g_w_pg, = _mm_tn("g_ple_gate", h3, [d_gp])
    dx2, dx2_b, d_norm_ple = _mm("d_h3", [d_gp], [w_pg], "nt", stream, extras=[x2, small["norm_ple"], dx3],
                                 epilogue=through_norm, sum_shapes=gain_sum, tm=512)

    def ffn_bwd(acc, gv, uv):
        s = _sigmoid(gv)
        return acc * uv * (s * (1.0 + gv * (1.0 - s))), acc * (gv * s)

    d_gate, d_up = _mm("d_act", [dx2_b], [w_down], "nt", [_sds((t, dff), BF16)] * 2, extras=[gate, up],
                       epilogue=ffn_bwd, tm=512, tn=dff // 2)
    g_w_down, = _mm_tn("g_ffn_down", act, [dx2_b], tmm=512)
    g_w_gate, g_w_up = _mm_tn("g_ffn_gate_up", h2, [d_gate, d_up], n_blocks=2)
    dx1, dx1_b, d_norm_ffn = _mm(
        "d_h2", [d_gate, d_up], [w_gate, w_up], "nt", stream, extras=[x1, small["norm_ffn"], dx2],
        epilogue=through_norm, sum_shapes=gain_sum, tm=512)

    def merge_bwd(acc, tav, tbv, gav, gbv):
        sa, sb = _sigmoid(gav), _sigmoid(gbv)
        return acc * sa, acc * sb, acc * tav * sa * (1.0 - sa), acc * tbv * sb * (1.0 - sb)

    d_ta, d_tb, d_ga, d_gb = _mm("d_merged", [dx1_b], [w_out], "nt", [_sds((t, d), BF16)] * 4,
                                 extras=[ta, tb, ga, gb], epilogue=merge_bwd, tm=512)
    g_w_out, = _mm_tn("g_w_out", merged, [dx1_b])
    g_w_a, = _mm_tn("g_branch_a", ya, [d_ta])
    g_w_b, = _mm_tn("g_branch_b", yb, [d_tb])
    d_ya, = _mm("d_ya", [d_ta], [w_a], "nt", [_sds((t, dp), F32)])
    d_yb, = _mm("d_yb", [d_tb], [w_b], "nt", [_sds((t, dp), BF16)])
    d_u, g_w_pool, d_pool_scale = _pool_bwd(d_ya, pooled, w_pool_b, small["pool_scale"])
    big = {
        "w_branch_a": _split_columns(g_w_a), "w_branch_b": _split_columns(g_w_b),
        "w_out": g_w_out.reshape(wf["w_out"].shape), "w_ffn_gate": _split_columns(g_w_gate),
        "w_ffn_up": _split_columns(g_w_up), "w_ffn_down": g_w_down.reshape(wf["w_ffn_down"].shape),
        "w_ple_gate": g_w_pg.reshape(wf["w_ple_gate"].shape),
        "w_ple_proj": _split_columns(g_w_pp),
    }
    rider = exchange_early(big) if exchange_early else _NoRider()
    (d_q, d_k, d_v), early = _attn_bwd(q, 0, kv, 0, n_pairs, d_yb, n_pairs, rider=rider)
    d_proj = [(d_u, d_q), (d_k, d_v), d_ga, d_gb]
    big["w_in"], = _mm_tn("g_w_in", h1, d_proj, tmm=512, stacked=True)
    rider = exchange_last(big["w_in"]) if exchange_last else _NoRider()
    res = _mm(
        "d_h1", d_proj, [w_in[j] for j in range(N_CHIPS)], "nt", [_sds((t, d), F32)],
        extras=[x, small["norm_mix"], dx1], epilogue=lambda dh, xv, g, dres: through_norm(dh, xv, g, dres)[1:],
        sum_shapes=gain_sum, tm=512, rider=rider)
    (grad_x, d_norm_mix), last = res if rider.operands else (res, ())
    small_g = {"norm_mix": d_norm_mix, "w_pool": g_w_pool, "pool_scale": d_pool_scale, "norm_ffn": d_norm_ffn,
               "norm_ple": d_norm_ple, "norm_final": d_norm_final}
    return grad_x, big, small_g, loss_row, early, last


def _split2(res, n):
    return res[:n], res[n:]


def _pack_small(small_g, loss_row):
    parts, layout = [], []
    for name in SMALL + ("loss",):
        v = (loss_row if name == "loss" else small_g[name]).reshape(-1, LANES)
        pad = (-v.shape[0]) % 8
        if pad:
            v = jnp.concatenate([v, jnp.zeros((pad, LANES), F32)], axis=0)
        layout.append((name, sum(q.shape[0] for q in parts), v.shape[0]))
        parts.append(v)
    return jnp.concatenate(parts, axis=0), layout


def kernel(x, p, norm_mix, w_in, w_pool, pool_scale, w_branch_a, w_branch_b, w_out, norm_ffn, w_ffn_gate, w_ffn_up, w_ffn_down, norm_ple, w_ple_gate, w_ple_proj, norm_final, loss_target, m_norm_mix, m_w_in, m_w_pool, m_pool_scale, m_w_branch_a, m_w_branch_b, m_w_out, m_norm_ffn, m_w_ffn_gate, m_w_ffn_up, m_w_ffn_down, m_norm_ple, m_w_ple_gate, m_w_ple_proj, m_norm_final, v_norm_mix, v_w_in, v_w_pool, v_pool_scale, v_w_branch_a, v_w_branch_b, v_w_out, v_norm_ffn, v_w_ffn_gate, v_w_ffn_up, v_w_ffn_down, v_norm_ple, v_w_ple_gate, v_w_ple_proj, v_norm_final):
    given = dict(locals())
    names = BIG + SMALL
    order = ("norm_mix", "w_in", "w_pool", "pool_scale", "w_branch_a", "w_branch_b", "w_out", "norm_ffn", "w_ffn_gate",
             "w_ffn_up", "w_ffn_down", "norm_ple", "w_ple_gate", "w_ple_proj", "norm_final")
    t, d = x.shape[1], x.shape[2]
    shard = {n: given[n][0] for n in BIG}
    small = {"norm_mix": norm_mix, "w_pool": w_pool[0], "pool_scale": pool_scale, "norm_ffn": norm_ffn,
             "norm_ple": norm_ple, "norm_final": norm_final}

    as_bf16 = {n: shard[n].astype(BF16) for n in BIG}

    place = jnp.stack([lax.axis_index("c"), 2 * lax.axis_index("x") + lax.axis_index("y")]).astype(jnp.int32)
    pair_sums = {}

    def exchange_early(ready):
        theirs = _pair_exchange("pair_exchange_early", [ready[n] for n in LATE])
        for n, other in zip(LATE, theirs):
            pair_sums[n] = _pair_sum(f"pair_sum_{n}", place, ready[n], other)
        return _ChipExchange([pair_sums[n] for n in LATE])

    def exchange_last(g_w_in):
        theirs, = _pair_exchange("pair_exchange_w_in", [g_w_in])
        pair_sums["w_in"] = _pair_sum("pair_sum_w_in", place, g_w_in, theirs)
        return _ChipExchange([pair_sums["w_in"]])

    grad_x, big_g, small_g, loss_row, early, last = _local_step(
        x.reshape(t, d), p.reshape(t, p.shape[-1]), loss_target.reshape(t, d), {}, small,
        gather_first=_WeightGather([as_bf16["w_in"]]),
        gather_late=_WeightGather([as_bf16[n] for n in LATE]), exchange_early=exchange_early,
        exchange_last=exchange_last)
    landed = dict(zip(LATE + ("w_in",), tuple(early) + tuple(last)))
    halves = [_sum_chips(f"chip_sum_{n}", place, pair_sums[n], landed[n]) for n in BIG]
    grads = dict(zip(BIG, _pair_share(halves)))

    packed, layout = _pack_small(small_g, loss_row)
    reduced = _all_reduce_small(packed)
    for name, start, rows in layout:
        if name == "loss":
            loss = jnp.sum(reduced[start:start + rows])
        else:
            n_el = small[name].size
            grads[name] = reduced[start:start + rows].reshape(-1)[:n_el]

    deltas, new_m, new_v = {}, {}, {}
    for n in order:
        w = shard[n] if n in BIG else small[n]
        shape2 = w.shape if w.ndim == 2 else ((1, w.shape[0]) if w.ndim == 1 else (w.shape[0] * w.shape[1], w.shape[2]))
        g2 = grads[n].reshape(shape2)
        dl, mn, vn = _adamw(f"adamw_{n}", w.reshape(shape2), g2, given["m_" + n].reshape(shape2),
                            given["v_" + n].reshape(shape2))
        full = given[n].shape
        grads[n], deltas[n], new_m[n], new_v[n] = g2.reshape(full), dl.reshape(full), mn.reshape(full), vn.reshape(full)

    return (loss, grad_x.reshape(x.shape), *[grads[n] for n in order], *[deltas[n] for n in order],
            *[new_m[n] for n in order], *[new_v[n] for n in order])
```

```python
import functools
import math

import jax
import jax.numpy as jnp
from jax import lax
from jax.experimental import pallas as pl
from jax.experimental.pallas import tpu as pltpu

F32 = jnp.float32
BF16 = jnp.bfloat16
MESH = pl.DeviceIdType.MESH

RMS_EPS = 1e-6
POOL_WINDOWS = (2, 4, 8, 16)
POOL_HALO = 16
HEAD_DIM = 64
LANES = 128
ATT_BLOCK = 256
ATT_CHUNK = 256
ATT_SLAB = 256
ATT_SCALE = 1.0 / math.sqrt(HEAD_DIM)
LOG2_E = 1.4426950408889634
ATT_EXIT_BELOW = -150.5
ADAM_LR, ADAM_B1, ADAM_B2, ADAM_EPS, ADAM_WD, ADAM_STEP = 0.001, 0.9, 0.999, 1e-08, 0.01, 10
V7X_VMEM_LIMIT_BYTES = 56 * 1024 * 1024
N_CHIPS = 4
N_DEV = 8


def _params(*semantics):
    return pltpu.CompilerParams(dimension_semantics=semantics, vmem_limit_bytes=V7X_VMEM_LIMIT_BYTES)


def _sigmoid(z):
    return 1.0 / (1.0 + jnp.exp(-z))


def _tiled_spec(shape, tm, tn, n_total, at):
    rows, width = shape
    if rows == 1:
        if width == n_total:
            return pl.BlockSpec((1, tn), at(lambda i, j: (0, j)))
        return pl.BlockSpec((1, width), at(lambda i, j: (0, 0)))
    if width == n_total:
        return pl.BlockSpec((tm, tn), at(lambda i, j: (i, j)))
    assert tn == n_total, "an operand narrower than the output needs whole output rows per tile"
    return pl.BlockSpec((tm, width), at(lambda i, j: (i, 0)))


def _column_pieces(operands):
    pieces = [tuple(a) if isinstance(a, (tuple, list)) else (a,) for a in operands]
    return [p for ps in pieces for p in ps], [len(ps) for ps in pieces]


def _load_bf16(refs, counts):
    tiles, k = [], 0
    for n in counts:
        parts = [r[...] for r in refs[k:k + n]]
        parts = [t if t.dtype == BF16 else t.astype(BF16) for t in parts]
        tiles.append(parts[0] if n == 1 else jnp.concatenate(parts, axis=1))
        k += n
    return tiles


def _mm(name, a_list, b_list, mode, out_shapes, epilogue=None, extras=(), tm=1024, tn=None, separate=False,
        sum_shapes=(), rider=None):
    flat_a, counts = _column_pieces(a_list)
    m_total = flat_a[0].shape[0]
    n_total = b_list[0].shape[1] if mode == "nn" else b_list[0].shape[0]
    tn = n_total if tn is None else tn
    tm = min(tm, m_total)
    assert m_total % tm == 0 and n_total % tn == 0 and (not sum_shapes or tn == n_total)
    n_a, n_b, n_extra, n_out = len(counts), len(b_list), len(extras), len(out_shapes)
    assert n_a in (1, n_b)
    dims = (((1,), (0,)), ((), ())) if mode == "nn" else (((1,), (1,)), ((), ()))
    rider = rider or _NoRider()
    grid = (n_total // tn, m_total // tm)

    def at(index):
        return lambda j, i: index(i, j)

    def body(*refs):
        ins, o_refs, _, riding = rider.split(refs, len(flat_a) + n_b + n_extra, n_out + len(sum_shapes))
        a_refs, b_refs, e_refs = ins[:len(flat_a)], ins[len(flat_a):len(flat_a) + n_b], ins[len(flat_a) + n_b:]
        at_first = (pl.program_id(0) == 0) & (pl.program_id(1) == 0)
        at_last = (pl.program_id(0) == grid[0] - 1) & (pl.program_id(1) == grid[1] - 1)
        top, bottom = rider.at_steps(riding, at_first, at_first, at_last)
        top()
        lefts = _load_bf16(a_refs, counts)
        products = [lax.dot_general(lefts[s % n_a], b_refs[s][...], dims, preferred_element_type=F32)
                    for s in range(n_b)]
        if not separate:
            products = [functools.reduce(lambda p, r: p + r, products)]
        extra_tiles = [e[...].astype(F32) for e in e_refs]
        outs = products if epilogue is None else epilogue(*products, *extra_tiles)
        for o_ref, o in zip(o_refs[:n_out], outs[:n_out]):
            o_ref[...] = o.astype(o_ref.dtype)
        if sum_shapes:
            @pl.when(pl.program_id(1) == 0)
            def _():
                for s_ref in o_refs[n_out:]:
                    s_ref[...] = jnp.zeros_like(s_ref)

            for s_ref, s in zip(o_refs[n_out:], outs[n_out:]):
                s_ref[...] += s
        bottom()

    once = dict(pipeline_mode=pl.Buffered(1)) if tn == n_total else {}
    in_specs = [pl.BlockSpec((tm, a.shape[1]), at(lambda i, j: (i, 0))) for a in flat_a]
    if mode == "nn":
        in_specs += [pl.BlockSpec((b.shape[0], tn), at(lambda i, j: (0, j)), **once) for b in b_list]
    else:
        in_specs += [pl.BlockSpec((tn, b.shape[1]), at(lambda i, j: (j, 0)), **once) for b in b_list]
    in_specs += [_tiled_spec(e.shape, tm, tn, n_total, at) for e in extras]
    out_specs = [_tiled_spec(o.shape, tm, tn, n_total, at) for o in out_shapes]
    out_specs += [pl.BlockSpec(s.shape, at(lambda i, j: (0, 0))) for s in sum_shapes]
    semantics = ("arbitrary", "arbitrary") if sum_shapes or rider.operands else ("parallel", "parallel")
    res = pl.pallas_call(
        body, name=name, grid=grid, in_specs=in_specs + [ANY] * len(rider.operands),
        out_specs=out_specs + [ANY] * len(rider.out_shapes),
        out_shape=list(out_shapes) + list(sum_shapes) + list(rider.out_shapes), scratch_shapes=list(rider.scratch),
        compiler_params=_params(*semantics),
    )(*flat_a, *b_list, *extras, *rider.operands)
    n_own = len(out_shapes) + len(sum_shapes)
    return res if not rider.operands else (res[:n_own], res[n_own:])


def _mm_tn(name, a, b_list, tmm=1024, stacked=False, n_blocks=1):
    flat_b, counts = _column_pieces(b_list)
    m_total, k = a.shape
    widths = [sum(p.shape[1] for p in flat_b[sum(counts[:s]):sum(counts[:s + 1])]) for s in range(len(counts))]
    tmm = min(tmm, m_total)
    assert m_total % tmm == 0 and (n_blocks == 1 or max(counts) == 1) and all(w % n_blocks == 0 for w in widths)
    n_b = len(counts)

    def body(*refs):
        a_ref, b_refs, o_refs = refs[0], refs[1:1 + len(flat_b)], refs[1 + len(flat_b):]

        @pl.when(pl.program_id(1) == 0)
        def _():
            for o_ref in o_refs:
                o_ref[...] = jnp.zeros_like(o_ref)

        av, = _load_bf16([a_ref], [1])
        for s, bv in enumerate(_load_bf16(b_refs, counts)):
            product = lax.dot_general(av, bv, (((0,), (0,)), ((), ())), preferred_element_type=F32)
            if stacked:
                o_refs[0][s] += product
            else:
                o_refs[s][...] += product

    in_specs = [pl.BlockSpec((tmm, k), lambda nb, m: (m, 0))]
    in_specs += [pl.BlockSpec((tmm, b.shape[1] // n_blocks), lambda nb, m: (m, nb)) for b in flat_b]
    if stacked:
        out_shape = [jax.ShapeDtypeStruct((n_b, k, widths[0]), F32)]
        out_specs = [pl.BlockSpec((n_b, k, widths[0] // n_blocks), lambda nb, m: (0, 0, nb))]
    else:
        out_shape = [jax.ShapeDtypeStruct((k, w), F32) for w in widths]
        out_specs = [pl.BlockSpec((k, w // n_blocks), lambda nb, m: (0, nb)) for w in widths]
    return pl.pallas_call(
        body, name=name, grid=(n_blocks, m_total // tmm), in_specs=in_specs, out_specs=out_specs, out_shape=out_shape,
        compiler_params=_params("arbitrary", "arbitrary"),
    )(a, *flat_b)


def _rows(name, fn, ins, tile_outs, sum_outs=(), tr=512, rider=None):
    t_total = max(a.shape[0] for a in ins)
    tr = min(tr, t_total)
    assert t_total % tr == 0
    n_in, n_tile = len(ins), len(tile_outs)
    rider = rider or _NoRider()
    n_steps = t_total // tr

    def body(*refs):
        own_ins, own_outs, _, riding = rider.split(refs, n_in, n_tile + len(sum_outs))
        step = pl.program_id(0)
        top, bottom = rider.at_steps(riding, step == 0, step == n_steps - 1, step == n_steps - 1)
        top()
        refs = tuple(own_ins) + tuple(own_outs)
        outs = fn(*[r[...].astype(F32) for r in refs[:n_in]])
        for o_ref, o in zip(refs[n_in:n_in + n_tile], outs[:n_tile]):
            o_ref[...] = o.astype(o_ref.dtype)
        if sum_outs:
            @pl.when(pl.program_id(0) == 0)
            def _():
                for s_ref in refs[n_in + n_tile:]:
                    s_ref[...] = jnp.zeros_like(s_ref)

            for s_ref, s in zip(refs[n_in + n_tile:], outs[n_tile:]):
                s_ref[...] += s
        bottom()

    def spec(shape):
        if shape[0] == 1:
            return pl.BlockSpec(shape, lambda i: (0, 0))
        return pl.BlockSpec((tr, shape[1]), lambda i: (i, 0))

    return pl.pallas_call(
        body, name=name, grid=(n_steps,), in_specs=[spec(a.shape) for a in ins] + [ANY] * len(rider.operands),
        out_specs=[spec(o.shape) for o in tile_outs] + [spec(s.shape) for s in sum_outs] + [ANY] * len(rider.out_shapes),
        out_shape=list(tile_outs) + list(sum_outs) + list(rider.out_shapes), scratch_shapes=list(rider.scratch),
        compiler_params=_params("arbitrary" if sum_outs or rider.operands else "parallel"),
    )(*ins, *rider.operands)


def _norm_fwd(name, x, gain, rider=None):
    def fn(xv, g):
        inv = lax.rsqrt(jnp.mean(xv * xv, axis=-1, keepdims=True) + RMS_EPS)
        return (xv * inv * g,)

    res = _rows(name, fn, [x, gain], [jax.ShapeDtypeStruct(x.shape, BF16)], rider=rider)
    return res[0], res[1:]


def _rms_norm_bwd(dh, xv, g):
    inv = lax.rsqrt(jnp.mean(xv * xv, axis=-1, keepdims=True) + RMS_EPS)
    xn = xv * inv
    dxn = dh * g
    return inv * (dxn - xn * jnp.mean(dxn * xn, axis=-1, keepdims=True)), jnp.sum(dh * xn, axis=0, keepdims=True)


def _ple_and_loss(gv, pv, x2v, tv, g):
    d = x2v.shape[1]
    s = _sigmoid(gv)
    xv = x2v + s * pv
    inv = lax.rsqrt(jnp.mean(xv * xv, axis=-1, keepdims=True) + RMS_EPS)
    err = xv * inv * g - tv
    dx, d_gain = _rms_norm_bwd(err * (1.0 / d), xv, g)
    return dx, dx * s, dx * pv * s * (1.0 - s), d_gain, (0.5 / d) * jnp.sum(err * err, axis=0, keepdims=True)


def _window_counts(t_pos, w):
    return jnp.minimum(t_pos + 1, w).astype(F32)


def _pool_fwd(u, w_pool, scale, tr=512):
    t_total, width = u.shape
    tr = min(tr, t_total)
    n_groups = len(POOL_WINDOWS)
    gdim = width // n_groups
    ext = tr + POOL_HALO

    def body(u_ref, halo_ref, w_ref, s_ref, pooled_ref, ya_ref):
        i = pl.program_id(0)
        halo = jnp.where(i == 0, 0.0, halo_ref[...])
        t_pos = i * tr + lax.broadcasted_iota(jnp.int32, (tr, 1), 0)
        for g, w in enumerate(POOL_WINDOWS):
            cols = slice(g * gdim, (g + 1) * gdim)
            main = u_ref[:, cols]
            win = jnp.concatenate([halo[:, cols], main], axis=0)
            span = 1
            while span < w:
                win = win + pltpu.roll(win, span, 0)
                span *= 2
            pooled = win[POOL_HALO:, :] * (1.0 / _window_counts(t_pos, w)) - main
            pooled_b = pooled.astype(BF16)
            pooled_ref[:, cols] = pooled_b
            mixed = jnp.dot(pooled_b, w_ref[g], preferred_element_type=F32)
            ya_ref[:, cols] = (mixed * s_ref[:, cols]).astype(BF16)

    hb = tr // POOL_HALO
    return pl.pallas_call(
        body, name="pool_fwd", grid=(t_total // tr,),
        in_specs=[pl.BlockSpec((tr, width), lambda i: (i, 0)),
                  pl.BlockSpec((POOL_HALO, width), lambda i: (jnp.maximum(i * hb - 1, 0), 0)),
                  pl.BlockSpec((n_groups, gdim, gdim), lambda i: (0, 0, 0)),
                  pl.BlockSpec((1, width), lambda i: (0, 0))],
        out_specs=[pl.BlockSpec((tr, width), lambda i: (i, 0)), pl.BlockSpec((tr, width), lambda i: (i, 0))],
        out_shape=[jax.ShapeDtypeStruct(u.shape, BF16), jax.ShapeDtypeStruct(u.shape, BF16)],
        compiler_params=_params("parallel"),
    )(u, u, w_pool, scale)


def _pool_bwd(dya, pooled, w_pool, scale, tr=512):
    t_total, width = dya.shape
    tr = min(tr, t_total)
    n_groups = len(POOL_WINDOWS)
    gdim = width // n_groups
    ext = tr + POOL_HALO
    n_tiles = t_total // tr

    def body(d_ref, halo_ref, p_ref, w_ref, s_ref, du_ref, dw_ref, ds_ref):
        i = pl.program_id(0)

        @pl.when(i == 0)
        def _():
            dw_ref[...] = jnp.zeros_like(dw_ref)
            ds_ref[...] = jnp.zeros_like(ds_ref)

        halo = jnp.where(i == n_tiles - 1, 0.0, halo_ref[...])
        t_pos = i * tr + lax.broadcasted_iota(jnp.int32, (ext, 1), 0)
        for g, w in enumerate(POOL_WINDOWS):
            cols = slice(g * gdim, (g + 1) * gdim)
            sc = s_ref[:, cols]
            d_main = d_ref[:, cols]
            pooled_b = p_ref[:, cols]
            mixed = jnp.dot(pooled_b, w_ref[g], preferred_element_type=F32)
            ds_ref[:, cols] += jnp.sum(d_main * mixed, axis=0, keepdims=True)
            dmix = (jnp.concatenate([d_main, halo[:, cols]], axis=0) * sc).astype(BF16)
            dw_ref[g] += lax.dot_general(pooled_b, dmix[:tr, :], (((0,), (0,)), ((), ())),
                                         preferred_element_type=F32)
            dpool = lax.dot_general(dmix, w_ref[g], (((1,), (1,)), ((), ())), preferred_element_type=F32)
            win = dpool * (1.0 / _window_counts(t_pos, w))
            span = 1
            while span < w:
                win = win + pltpu.roll(win, ext - span, 0)
                span *= 2
            du_ref[:, cols] = (win[:tr, :] - dpool[:tr, :]).astype(BF16)

    hb = tr // POOL_HALO
    last_halo = t_total // POOL_HALO - 1
    return pl.pallas_call(
        body, name="pool_bwd", grid=(n_tiles,),
        in_specs=[pl.BlockSpec((tr, width), lambda i: (i, 0)),
                  pl.BlockSpec((POOL_HALO, width), lambda i: (jnp.minimum((i + 1) * hb, last_halo), 0)),
                  pl.BlockSpec((tr, width), lambda i: (i, 0)),
                  pl.BlockSpec((n_groups, gdim, gdim), lambda i: (0, 0, 0)),
                  pl.BlockSpec((1, width), lambda i: (0, 0))],
        out_specs=[pl.BlockSpec((tr, width), lambda i: (i, 0)),
                   pl.BlockSpec((n_groups, gdim, gdim), lambda i: (0, 0, 0)),
                   pl.BlockSpec((1, width), lambda i: (0, 0))],
        out_shape=[jax.ShapeDtypeStruct(dya.shape, BF16), jax.ShapeDtypeStruct((n_groups, gdim, gdim), F32),
                   jax.ShapeDtypeStruct((1, width), F32)],
        compiler_params=_params("arbitrary"),
    )(dya, dya, pooled, w_pool, scale)


def _head_masks():
    lane = lax.broadcasted_iota(jnp.int32, (1, LANES), 1)
    return lane < HEAD_DIM


def _stack_heads(tile, first):
    zero = jnp.zeros_like(tile)
    return jnp.concatenate([jnp.where(first, tile, zero), jnp.where(first, zero, tile)], axis=0)


def _split_bf16(v):
    hi = v.astype(BF16)
    lo = (v - hi.astype(F32)).astype(BF16)
    return hi, lo


def _causal_mask(t_pos, k_start):
    col = lax.broadcasted_iota(jnp.int32, (1, 2 * ATT_SLAB), 1)
    return k_start + (col & (ATT_SLAB - 1)) < t_pos


def _slab_scores(q, kd, mask):
    z2 = lax.dot_general(q, kd, (((1,), (1,)), ((), ())), preferred_element_type=F32) * LOG2_E
    log_hit = jnp.minimum(z2, 0.0) - jnp.log2(1.0 + jnp.exp2(-jnp.abs(z2)))
    log_fail = log_hit - z2
    return log_hit, (log_fail if mask is None else jnp.where(mask, log_fail, 0.0))


def _weights(log_hit, suffix, mask):
    arg = log_hit + suffix
    return jnp.exp2(arg if mask is None else jnp.where(mask, arg, -1e30))


def _tri(upper):
    r = lax.broadcasted_iota(jnp.int32, (2 * ATT_CHUNK, ATT_CHUNK), 0) & (ATT_CHUNK - 1)
    c = lax.broadcasted_iota(jnp.int32, (2 * ATT_CHUNK, ATT_CHUNK), 1)
    return jnp.where(r > c if upper else r < c, 1.0, 0.0).astype(BF16)


def _tri_spec():
    return pl.BlockSpec((2 * ATT_CHUNK, ATT_CHUNK), lambda h, i: (0, 0), pipeline_mode=pl.Buffered(1))


def _scan_chunk(v, tri):
    return jnp.dot(jnp.concatenate(_split_bf16(v), axis=1), tri, preferred_element_type=F32)


def _lane_bcast(col):
    return jnp.broadcast_to(col, (col.shape[0], LANES))


def _scan_slab(v, tri, carries, from_right):
    n_chunks = ATT_SLAB // ATT_CHUNK
    edge = 0 if from_right else ATT_CHUNK - 1
    parts, new_carries = [None] * (2 * n_chunks), []
    for head in range(2):
        run = carries[head]
        for c in (reversed(range(n_chunks)) if from_right else range(n_chunks)):
            lo_col = head * ATT_SLAB + c * ATT_CHUNK
            vc = v[:, lo_col:lo_col + ATT_CHUNK]
            sc = _scan_chunk(vc, tri)
            parts[head * n_chunks + c] = sc + jnp.concatenate([run] * (ATT_CHUNK // LANES), axis=1)
            run = run + _lane_bcast(sc[:, edge:edge + 1] + vc[:, edge:edge + 1])
        new_carries.append(run)
    return jnp.concatenate(parts, axis=1), new_carries


def _fold_heads(stacked, first):
    s = stacked.shape[0] // 2
    return jnp.where(first, stacked[:s], stacked[s:])


class _NoRider:
    operands, out_shapes, scratch = (), (), ()

    def split(self, refs, n_base_in, n_base_out):
        n_in, n_out, n_sem = len(self.operands), len(self.out_shapes), len(self.scratch)
        a = n_base_in + n_in
        b = a + n_base_out + n_out
        mine = (refs[n_base_in:a], refs[a + n_base_out:b], refs[b:b + n_sem])
        return refs[:n_base_in], refs[a:a + n_base_out], refs[b + n_sem:], mine

    def start(self, ins, outs, sems):
        pass

    def relay(self, ins, outs, sems):
        pass

    def finish(self, ins, outs, sems):
        pass

    def at_steps(self, refs, first_step, relay_step, last_step):
        if not self.operands:
            return (lambda: None), (lambda: None)

        def top():
            pl.when(first_step)(lambda: self.start(*refs))
            pl.when(relay_step)(lambda: self.relay(*refs))

        return top, lambda: pl.when(last_step)(lambda: self.finish(*refs))


def _attn_fwd(q_src, q_col, kv_src, k_col, v_col, n_pairs=4, rider=_NoRider()):
    t_total = q_src.shape[0]
    blk = ATT_BLOCK
    n_blocks = t_total // blk
    assert t_total % ATT_SLAB == 0 and ATT_SLAB % ATT_BLOCK == 0

    def body(*refs):
        (q_ref, k_ref, v_ref, suffix_ref), (o_ref,), _, riding = rider.split(refs, 4, 1)
        h, i = pl.program_id(0), pl.program_id(1)
        top, bottom = rider.at_steps(riding, (h == 0) & (i == 0), (h == n_pairs - 1) & (i == 0),
                                     (h == n_pairs - 1) & (i == n_blocks - 1))
        top()
        first = _head_masks()
        q = q_ref[...] * ATT_SCALE
        t_pos = i * blk + lax.broadcasted_iota(jnp.int32, (blk, 1), 0)
        suffix_tri = suffix_ref[...]

        def more(state):
            slab, reach = state[0], state[1]
            return jnp.logical_and(slab >= 0, reach > ATT_EXIT_BELOW)

        def step(state, on_diagonal):
            slab, _, acc, right_a, right_b = state
            k_start = pl.multiple_of(slab * ATT_SLAB, ATT_SLAB)
            kd = _stack_heads(k_ref[pl.ds(k_start, ATT_SLAB), :], first)
            vd = _stack_heads(v_ref[pl.ds(k_start, ATT_SLAB), :], first)
            mask = _causal_mask(t_pos, k_start) if on_diagonal else None
            log_hit, log_fail = _slab_scores(q, kd, mask)
            suffix, (right_a, right_b) = _scan_slab(log_fail, suffix_tri, (right_a, right_b), from_right=True)
            a = _weights(log_hit, suffix, mask).astype(BF16)
            acc = acc + jnp.dot(a, vd, preferred_element_type=F32)
            return slab - 1, jnp.max(jnp.maximum(right_a, right_b)), acc, right_a, right_b

        zero = jnp.zeros((blk, LANES), F32)
        state = step(((i * blk) // ATT_SLAB, jnp.float32(0.0), zero, zero, zero), on_diagonal=True)
        state = lax.while_loop(more, functools.partial(step, on_diagonal=False), state)
        o_ref[...] = state[2].astype(BF16)
        bottom()

    res = pl.pallas_call(
        body, name="attn_fwd", grid=(n_pairs, n_blocks),
        in_specs=[pl.BlockSpec((blk, LANES), lambda h, i: (i, q_col + h)),
                  pl.BlockSpec((t_total, LANES), lambda h, i: (0, k_col + h)),
                  pl.BlockSpec((t_total, LANES), lambda h, i: (0, v_col + h)), _tri_spec()] + [ANY] * len(rider.operands),
        out_specs=[pl.BlockSpec((blk, LANES), lambda h, i: (i, h))] + [ANY] * len(rider.out_shapes),
        out_shape=[jax.ShapeDtypeStruct((t_total, n_pairs * LANES), BF16)] + list(rider.out_shapes),
        scratch_shapes=list(rider.scratch),
        compiler_params=_params("arbitrary", "arbitrary"),
    )(q_src, kv_src, kv_src, _tri(upper=True), *rider.operands)
    return res[0], res[1:]


def _attn_bwd(q_src, q_col, kv_src, k_col, v_col, dy, n_pairs=4, rider=_NoRider()):
    t_total = q_src.shape[0]
    blk = ATT_BLOCK
    n_blocks = t_total // blk
    n_slabs = t_total // ATT_SLAB
    assert t_total % ATT_SLAB == 0 and ATT_SLAB % ATT_BLOCK == 0

    def body(*refs):
        ins, (dq_ref, dk_ref, dv_ref), (g_s, dk_acc, dv_acc), riding = rider.split(refs, 6, 3)
        q_ref, dy_ref, k_ref, v_ref, suffix_ref, prefix_ref = ins
        h, i = pl.program_id(0), pl.program_id(1)
        top, bottom = rider.at_steps(riding, (h == 0) & (i == 0), (h == n_pairs - 1) & (i == 0),
                                     (h == n_pairs - 1) & (i == n_blocks - 1))
        top()

        @pl.when(i == 0)
        def _():
            dk_acc[...] = jnp.zeros_like(dk_acc)
            dv_acc[...] = jnp.zeros_like(dv_acc)

        first = _head_masks()
        q = q_ref[...] * ATT_SCALE
        dy = dy_ref[...]
        t_pos = i * blk + lax.broadcasted_iota(jnp.int32, (blk, 1), 0)
        suffix_tri = suffix_ref[...]
        prefix_tri = prefix_ref[...]
        diag = (i * blk) // ATT_SLAB

        def more(state):
            slab, reach = state[0], state[1]
            return jnp.logical_and(slab >= 0, reach > ATT_EXIT_BELOW)

        def sweep1(state, on_diagonal):
            slab, _, right_a, right_b = state
            k_start = pl.multiple_of(slab * ATT_SLAB, ATT_SLAB)
            kd = _stack_heads(k_ref[pl.ds(k_start, ATT_SLAB), :], first)
            vd = _stack_heads(v_ref[pl.ds(k_start, ATT_SLAB), :], first)
            mask = _causal_mask(t_pos, k_start) if on_diagonal else None
            log_hit, log_fail = _slab_scores(q, kd, mask)
            suffix, (right_a, right_b) = _scan_slab(log_fail, suffix_tri, (right_a, right_b), from_right=True)
            a = _weights(log_hit, suffix, mask)
            da = lax.dot_general(dy, vd, (((1,), (1,)), ((), ())), preferred_element_type=F32)
            g_s[slab] = da * a
            dv_acc[pl.ds(k_start, ATT_SLAB), :] += _fold_heads(lax.dot_general(
                a.astype(BF16), dy, (((0,), (0,)), ((), ())), preferred_element_type=F32), first)
            return slab - 1, jnp.max(jnp.maximum(right_a, right_b)), right_a, right_b

        zero = jnp.zeros((blk, LANES), F32)
        state = sweep1((diag, jnp.float32(0.0), zero, zero), on_diagonal=True)
        end = lax.while_loop(more, functools.partial(sweep1, on_diagonal=False), state)[0]

        def sweep2(slab, carry, on_diagonal):
            dq, left_a, left_b = carry
            k_start = pl.multiple_of(slab * ATT_SLAB, ATT_SLAB)
            kd = _stack_heads(k_ref[pl.ds(k_start, ATT_SLAB), :], first)
            g = g_s[slab]
            z2 = lax.dot_general(q, kd, (((1,), (1,)), ((), ())), preferred_element_type=F32) * LOG2_E
            sig = 1.0 / (1.0 + jnp.exp2(-z2))
            prefix, (left_a, left_b) = _scan_slab(g, prefix_tri, (left_a, left_b), from_right=False)
            dz = g * (1.0 - sig) - sig * prefix
            if on_diagonal:
                dz = jnp.where(_causal_mask(t_pos, k_start), dz, 0.0)
            dz = dz.astype(BF16)
            dq = dq + jnp.dot(dz, kd, preferred_element_type=F32)
            dk_acc[pl.ds(k_start, ATT_SLAB), :] += _fold_heads(lax.dot_general(
                dz, q, (((0,), (0,)), ((), ())), preferred_element_type=F32), first)
            return dq, left_a, left_b

        carry = lax.fori_loop(end + 1, diag, functools.partial(sweep2, on_diagonal=False), (zero, zero, zero))
        dq = sweep2(diag, carry, on_diagonal=True)[0]
        dq_ref[...] = (dq * ATT_SCALE).astype(BF16)

        @pl.when(i == n_blocks - 1)
        def _():
            dk_ref[...] = dk_acc[...].astype(BF16)
            dv_ref[...] = dv_acc[...].astype(BF16)

        bottom()

    out = jax.ShapeDtypeStruct((t_total, n_pairs * LANES), BF16)
    res = pl.pallas_call(
        body, name="attn_bwd", grid=(n_pairs, n_blocks),
        in_specs=[pl.BlockSpec((blk, LANES), lambda h, i: (i, q_col + h)),
                  pl.BlockSpec((blk, LANES), lambda h, i: (i, h)),
                  pl.BlockSpec((t_total, LANES), lambda h, i: (0, k_col + h)),
                  pl.BlockSpec((t_total, LANES), lambda h, i: (0, v_col + h)), _tri_spec(), _tri_spec()]
        + [ANY] * len(rider.operands),
        out_specs=[pl.BlockSpec((blk, LANES), lambda h, i: (i, h)),
                   pl.BlockSpec((t_total, LANES), lambda h, i: (0, h)),
                   pl.BlockSpec((t_total, LANES), lambda h, i: (0, h))] + [ANY] * len(rider.out_shapes),
        out_shape=[out, out, out] + list(rider.out_shapes),
        scratch_shapes=list(rider.scratch) + [pltpu.VMEM((n_slabs, blk, 2 * ATT_SLAB), F32),
                                              pltpu.VMEM((t_total, LANES), F32), pltpu.VMEM((t_total, LANES), F32)],
        compiler_params=_params("arbitrary", "arbitrary"),
    )(q_src, dy, kv_src, kv_src, _tri(upper=True), _tri(upper=False), *rider.operands)
    return res[:3], res[3:]


def _adamw(name, w, g, m, v):
    def fn(wv, gv, mv, vv):
        mn = ADAM_B1 * mv + (1.0 - ADAM_B1) * gv
        vn = ADAM_B2 * vv + (1.0 - ADAM_B2) * (gv * gv)
        m_hat = mn / (1.0 - ADAM_B1 ** ADAM_STEP)
        v_hat = vn / (1.0 - ADAM_B2 ** ADAM_STEP)
        return -ADAM_LR * (m_hat / (jnp.sqrt(v_hat) + ADAM_EPS) + ADAM_WD * wv), mn, vn

    rows = w.shape[0]
    tr = _row_tile(rows)
    shp = jax.ShapeDtypeStruct(w.shape, F32)
    if rows == 1:
        def body(w_ref, g_ref, m_ref, v_ref, d_ref, mo_ref, vo_ref):
            d, mn, vn = fn(w_ref[...], g_ref[...], m_ref[...], v_ref[...])
            d_ref[...], mo_ref[...], vo_ref[...] = d, mn, vn

        return pl.pallas_call(body, name=name, out_shape=[shp, shp, shp])(w, g, m, v)
    return _rows(name, fn, [w, g, m, v], [shp, shp, shp], tr=tr)


def _place():
    return lax.axis_index("x"), lax.axis_index("y"), lax.axis_index("c")


def _other_chips(x, y):
    return [(1 - x, y), (x, 1 - y), (1 - x, 1 - y)]


ANY = pl.BlockSpec(memory_space=pl.ANY)


def _remote(src, dst, send_sem, recv_sem, to):
    return pltpu.make_async_remote_copy(src_ref=src, dst_ref=dst, send_sem=send_sem, recv_sem=recv_sem,
                                        device_id=to, device_id_type=MESH)


class _WeightGather(_NoRider):
    def __init__(self, shards):
        n_w = len(shards)
        self.operands = list(shards)
        self.out_shapes = [jax.ShapeDtypeStruct((N_CHIPS,) + s.shape, s.dtype) for s in shards]
        self.scratch = [pltpu.SemaphoreType.DMA((3, n_w))] * 4 + [pltpu.SemaphoreType.DMA((n_w,))] * 2

    def _copies(self, ins, outs, sems):
        send_sems, recv_sems, relay_send, relay_recv, own_send, own_recv = sems
        x, y, c = _place()
        my_chip, sibling = 2 * x + y, (x, y, 1 - c)
        n_w = len(ins)

        def half(w, chip, core):
            h = self.operands[w].shape[0] // 2
            return outs[w].at[chip, pl.ds(core * h, h)]

        own = [_remote(ins[w], outs[w].at[my_chip], own_send.at[w], own_recv.at[w], sibling) for w in range(n_w)]
        sends, landed, relays, relayed = [], [], [], []
        for p, (ox, oy) in enumerate(_other_chips(x, y)):
            for w in range(n_w):
                h = self.operands[w].shape[0] // 2
                sends.append(_remote(ins[w].at[pl.ds(c * h, h)], half(w, my_chip, c), send_sems.at[p, w],
                                     recv_sems.at[p, w], (ox, oy, c)))
                here = half(w, 2 * ox + oy, c)
                landed.append(_remote(here, here, send_sems.at[p, w], recv_sems.at[p, w], (ox, oy, c)))
                relays.append(_remote(here, here, relay_send.at[p, w], relay_recv.at[p, w], sibling))
                there = half(w, 2 * ox + oy, 1 - c)
                relayed.append(_remote(there, there, relay_send.at[p, w], relay_recv.at[p, w], sibling))
        return own, sends, landed, relays, relayed

    def start(self, ins, outs, sems):
        own, sends, _, _, _ = self._copies(ins, outs, sems)
        for cp in own + sends:
            cp.start()

    def relay(self, ins, outs, sems):
        _, _, landed, relays, _ = self._copies(ins, outs, sems)
        for arrival, cp in zip(landed, relays):
            arrival.wait_recv()
            cp.start()

    def finish(self, ins, outs, sems):
        own, sends, _, relays, relayed = self._copies(ins, outs, sems)
        for arrival in relayed:
            arrival.wait_recv()
        for cp in sends + relays:
            cp.wait_send()
        for cp in own:
            cp.wait()


class _ChipExchange(_NoRider):
    def __init__(self, pair_sums):
        n_w = len(pair_sums)
        self.operands = list(pair_sums)
        self.out_shapes = [jax.ShapeDtypeStruct((3,) + s.shape[1:], s.dtype) for s in pair_sums]
        self.scratch = [pltpu.SemaphoreType.DMA((3, n_w))] * 2

    def _copies(self, ins, outs, sems):
        send_sems, recv_sems = sems
        x, y, c = _place()
        return [_remote(ins[w].at[2 * ox + oy], outs[w].at[p], send_sems.at[p, w], recv_sems.at[p, w], (ox, oy, c))
                for p, (ox, oy) in enumerate(_other_chips(x, y)) for w in range(len(ins))]

    def start(self, ins, outs, sems):
        for cp in self._copies(ins, outs, sems):
            cp.start()

    def finish(self, ins, outs, sems):
        for cp in self._copies(ins, outs, sems):
            cp.wait()


def _pair_exchange(name, grads):
    n_w = len(grads)

    def halves(w):
        return grads[w].shape[-2] // 2

    def body(*refs):
        ins, theirs = refs[:n_w], refs[n_w:2 * n_w]
        send_sems, recv_sems = refs[2 * n_w:]
        x, y, c = _place()
        sends = []
        for w in range(n_w):
            rows = pl.ds((1 - c) * halves(w), halves(w))
            src = ins[w].at[:, rows, :] if grads[w].ndim == 3 else ins[w].at[rows, :]
            sends.append(_remote(src, theirs[w], send_sems.at[w], recv_sems.at[w], (x, y, 1 - c)))
        for cp in sends:
            cp.start()
        for cp in sends:
            cp.wait()

    return pl.pallas_call(
        body, name=name, in_specs=[ANY] * n_w, out_specs=[ANY] * n_w,
        out_shape=[jax.ShapeDtypeStruct(g.shape[:-2] + (halves(w), g.shape[-1]), F32) for w, g in enumerate(grads)],
        scratch_shapes=[pltpu.SemaphoreType.DMA((n_w,)), pltpu.SemaphoreType.DMA((n_w,))],
    )(*grads)


def _pair_share(shards):
    n_w = len(shards)

    def body(*refs):
        ins, outs = refs[:n_w], refs[n_w:2 * n_w]
        send_sems, recv_sems = refs[2 * n_w:]
        x, y, c = _place()
        sends = []
        for w in range(n_w):
            h = shards[w].shape[0] // 2
            mine = outs[w].at[pl.ds(c * h, h)]
            sends.append(pltpu.make_async_remote_copy(
                src_ref=mine, dst_ref=mine, send_sem=send_sems.at[w], recv_sem=recv_sems.at[w],
                device_id=(x, y, 1 - c), device_id_type=MESH))
        for cp in sends:
            cp.start()
        for w in range(n_w):
            h = shards[w].shape[0] // 2
            theirs = outs[w].at[pl.ds((1 - c) * h, h)]
            pltpu.make_async_remote_copy(
                src_ref=theirs, dst_ref=theirs, send_sem=send_sems.at[w], recv_sem=recv_sems.at[w],
                device_id=(x, y, 1 - c), device_id_type=MESH).wait_recv()
        for cp in sends:
            cp.wait_send()

    return pl.pallas_call(
        body, name="pair_share", in_specs=[ANY] * n_w, out_specs=[ANY] * n_w,
        out_shape=[jax.ShapeDtypeStruct(s.shape, s.dtype) for s in shards],
        input_output_aliases={w: w for w in range(n_w)},
        scratch_shapes=[pltpu.SemaphoreType.DMA((n_w,)), pltpu.SemaphoreType.DMA((n_w,))],
    )(*shards)


def _all_reduce_small(vec):
    rows = vec.shape[0]

    def body(v_ref, o_ref, slots, send_sems, recv_sems):
        x, y, c = _place()
        me = 4 * x + 2 * y + c
        slots[me] = v_ref[...]
        sends = []
        for k in range(1, N_DEV):
            peer = (x ^ (k >> 2), y ^ ((k >> 1) & 1), c ^ (k & 1))
            sends.append(pltpu.make_async_remote_copy(
                src_ref=v_ref, dst_ref=slots.at[me], send_sem=send_sems.at[k - 1], recv_sem=recv_sems.at[k - 1],
                device_id=peer, device_id_type=MESH))
        for cp in sends:
            cp.start()
        for k in range(1, N_DEV):
            px, py, pc = x ^ (k >> 2), y ^ ((k >> 1) & 1), c ^ (k & 1)
            landed = slots.at[4 * px + 2 * py + pc]
            pltpu.make_async_remote_copy(
                src_ref=landed, dst_ref=landed, send_sem=send_sems.at[k - 1], recv_sem=recv_sems.at[k - 1],
                device_id=(px, py, pc), device_id_type=MESH).wait_recv()
        for cp in sends:
            cp.wait_send()
        total = slots[0]
        for d in range(1, N_DEV):
            total = total + slots[d]
        o_ref[...] = total

    vm = pl.BlockSpec(memory_space=pltpu.VMEM)
    return pl.pallas_call(
        body, name="all_reduce_small", in_specs=[vm], out_specs=vm, out_shape=jax.ShapeDtypeStruct(vec.shape, F32),
        scratch_shapes=[pltpu.VMEM((N_DEV, rows, LANES), F32), pltpu.SemaphoreType.DMA((N_DEV - 1,)),
                        pltpu.SemaphoreType.DMA((N_DEV - 1,))],
    )(vec)


def _row_tile(rows):
    fits = [tr for tr in range(16, min(rows, 512) + 1, 16) if rows % tr == 0]
    return max(fits) if fits else rows


def _pair_sum(name, place, grad, theirs):
    if grad.ndim == 2:
        return _pair_sum_joined(name, place, grad, theirs)
    n, r, c = grad.shape
    half = r // 2
    tr = _row_tile(half)
    nb = half // tr

    def body(place_ref, g_ref, t_ref, o_ref):
        o_ref[...] = (g_ref[...] + t_ref[...]).astype(BF16)

    return pl.pallas_call(
        body, name=name, out_shape=jax.ShapeDtypeStruct((n, half, c), BF16),
        grid_spec=pltpu.PrefetchScalarGridSpec(
            num_scalar_prefetch=1, grid=(n, nb),
            in_specs=[pl.BlockSpec((1, tr, c), lambda j, i, pr: (j, pr[0] * nb + i, 0)),
                      pl.BlockSpec((1, tr, c), lambda j, i, pr: (j, i, 0))],
            out_specs=pl.BlockSpec((1, tr, c), lambda j, i, pr: (j, i, 0))),
        compiler_params=_params("parallel", "parallel"),
    )(place, grad, theirs)


def _pair_sum_joined(name, place, grad, theirs):
    r, wide = grad.shape
    half, c = r // 2, wide // N_CHIPS
    tr = _row_tile(half)
    nb = half // tr

    def body(place_ref, g_ref, t_ref, o_ref):
        for j in range(N_CHIPS):
            cols = slice(j * c, (j + 1) * c)
            o_ref[j] = (g_ref[:, cols] + t_ref[:, cols]).astype(BF16)

    return pl.pallas_call(
        body, name=name, out_shape=jax.ShapeDtypeStruct((N_CHIPS, half, c), BF16),
        grid_spec=pltpu.PrefetchScalarGridSpec(
            num_scalar_prefetch=1, grid=(nb,),
            in_specs=[pl.BlockSpec((tr, wide), lambda i, pr: (pr[0] * nb + i, 0)),
                      pl.BlockSpec((tr, wide), lambda i, pr: (i, 0))],
            out_specs=pl.BlockSpec((N_CHIPS, tr, c), lambda i, pr: (0, i, 0))),
        compiler_params=_params("parallel"),
    )(place, grad, theirs)


def _sum_chips(name, place, pair_sums, landed):
    _, half, c = pair_sums.shape
    tr = _row_tile(half)
    nb = half // tr

    def body(place_ref, s_ref, q_ref, o_ref):
        total = s_ref[0].astype(F32)
        for p in range(3):
            total = total + q_ref[p].astype(F32)
        o_ref[...] = total

    return pl.pallas_call(
        body, name=name, out_shape=jax.ShapeDtypeStruct((2 * half, c), F32),
        grid_spec=pltpu.PrefetchScalarGridSpec(
            num_scalar_prefetch=1, grid=(nb,),
            in_specs=[pl.BlockSpec((1, tr, c), lambda i, pr: (pr[1], i, 0)),
                      pl.BlockSpec((3, tr, c), lambda i, pr: (0, i, 0))],
            out_specs=pl.BlockSpec((tr, c), lambda i, pr: (pr[0] * nb + i, 0))),
        compiler_params=_params("parallel"),
    )(place, pair_sums, landed)


BIG = ("w_in", "w_branch_a", "w_branch_b", "w_out", "w_ffn_gate", "w_ffn_up", "w_ffn_down", "w_ple_gate", "w_ple_proj")
LATE = BIG[1:]
COLUMN_SHARDED = ("w_in", "w_branch_a", "w_branch_b", "w_ffn_gate", "w_ffn_up", "w_ple_proj")
SMALL = ("norm_mix", "w_pool", "pool_scale", "norm_ffn", "norm_ple", "norm_final")


def _join_columns(w4):
    return jnp.concatenate([w4[j] for j in range(N_CHIPS)], axis=1)


def _sds(shape, dtype):
    return jax.ShapeDtypeStruct(shape, dtype)


def _local_step(x, p, target, wf, small, gather_first=None, gather_late=None, exchange_early=None,
                exchange_last=None):
    t, d = x.shape
    w_pool_b = small["w_pool"].astype(BF16)
    dp = w_pool_b.shape[0] * w_pool_b.shape[1]

    h1, first = _norm_fwd("norm_mix", x, small["norm_mix"], rider=gather_first)
    w_in = first[0] if gather_first else wf["w_in"]
    u, q, kv, ga, gb = _mm(
        "proj", [h1], [w_in[j] for j in range(N_CHIPS)], "nn",
        [_sds((t, dp), F32), _sds((t, dp), BF16), _sds((t, d), BF16), _sds((t, d), BF16), _sds((t, d), BF16)],
        separate=True, epilogue=lambda uq, kv_, ga_, gb_: (uq[:, :dp], uq[:, dp:], kv_, ga_, gb_), tm=512)
    pooled, ya = _pool_fwd(u, w_pool_b, small["pool_scale"])
    n_pairs = dp // LANES
    yb, late = _attn_fwd(q, 0, kv, 0, n_pairs, n_pairs, rider=gather_late or _NoRider())
    wf = {**wf, **dict(zip(LATE, late))}
    w_gate, w_up = _join_columns(wf["w_ffn_gate"]), _join_columns(wf["w_ffn_up"])
    dff = w_gate.shape[1]
    w_down = wf["w_ffn_down"].reshape(dff, d)
    w_a, w_b, w_pp = _join_columns(wf["w_branch_a"]), _join_columns(wf["w_branch_b"]), _join_columns(wf["w_ple_proj"])
    w_out = wf["w_out"].reshape(d, d)
    w_pg = wf["w_ple_gate"].reshape(d, d)
    ta, tb, merged = _mm(
        "branches_merge", [ya, yb], [w_a, w_b], "nn", [_sds((t, d), BF16)] * 3, extras=[ga, gb], separate=True,
        epilogue=lambda tav, tbv, gav, gbv: (tav, tbv, _sigmoid(gav) * tav + _sigmoid(gbv) * tbv), tm=512)
    def residual_norm(acc, xv, g):
        xn = acc + xv
        return xn, xn * lax.rsqrt(jnp.mean(xn * xn, axis=-1, keepdims=True) + RMS_EPS) * g

    x1, h2 = _mm("mix_out", [merged], [w_out], "nn", [_sds((t, d), F32), _sds((t, d), BF16)],
                 extras=[x, small["norm_ffn"]], epilogue=residual_norm)
    gate, up, act = _mm("ffn_gate_up", [h2], [w_gate, w_up], "nn", [_sds((t, dff), BF16)] * 3, separate=True,
                        epilogue=lambda gv, uv: (gv, uv, gv * _sigmoid(gv) * uv), tm=512, tn=dff // 2)
    x2, h3 = _mm("ffn_down", [act], [w_down], "nn", [_sds((t, d), F32), _sds((t, d), BF16)],
                 extras=[x1, small["norm_ple"]], epilogue=residual_norm, tm=512)
    dx3, d_pp, d_gp, d_norm_final, loss_row = _mm(
        "ple_loss", [h3, p], [w_pg, w_pp], "nn", [_sds((t, d), F32), _sds((t, d), BF16), _sds((t, d), BF16)],
        extras=[x2, target, small["norm_final"].reshape(1, d)], separate=True, epilogue=_ple_and_loss,
        sum_shapes=[_sds((1, d), F32)] * 2, tm=512)

    def through_norm(dh, xv, g, dres):
        dx, d_gain = _rms_norm_bwd(dh, xv, g)
        return dx + dres, dx + dres, d_gain

    stream = [_sds((t, d), F32), _sds((t, d), BF16)]
    gain_sum = [_sds((1, d), F32)]
    g_w_pp, = _mm_tn("g_ple_proj", p, [d_pp])
    g_w_pg, = _mm_tn("g_ple_gate", h3, [d_gp])
    dx2, dx2_b, d_norm_ple = _mm("d_h3", [d_gp], [w_pg], "nt", stream, extras=[x2, small["norm_ple"], dx3],
                                 epilogue=through_norm, sum_shapes=gain_sum, tm=512)

    def ffn_bwd(acc, gv, uv):
        s = _sigmoid(gv)
        return acc * uv * (s * (1.0 + gv * (1.0 - s))), acc * (gv * s)

    d_gate, d_up = _mm("d_act", [dx2_b], [w_down], "nt", [_sds((t, dff), BF16)] * 2, extras=[gate, up],
                       epilogue=ffn_bwd, tm=512, tn=dff // 2)
    g_w_down, = _mm_tn("g_ffn_down", act, [dx2_b], tmm=512)
    g_w_gate, g_w_up = _mm_tn("g_ffn_gate_up", h2, [d_gate, d_up], n_blocks=2)
    dx1, dx1_b, d_norm_ffn = _mm(
        "d_h2", [d_gate, d_up], [w_gate, w_up], "nt", stream, extras=[x1, small["norm_ffn"], dx2],
        epilogue=through_norm, sum_shapes=gain_sum, tm=512)

    def merge_bwd(acc, tav, tbv, gav, gbv):
        sa, sb = _sigmoid(gav), _sigmoid(gbv)
        return acc * sa, acc * sb, acc * tav * sa * (1.0 - sa), acc * tbv * sb * (1.0 - sb)

    d_ta, d_tb, d_ga, d_gb = _mm("d_merged", [dx1_b], [w_out], "nt", [_sds((t, d), BF16)] * 4,
                                 extras=[ta, tb, ga, gb], epilogue=merge_bwd, tm=512)
    g_w_out, = _mm_tn("g_w_out", merged, [dx1_b])
    g_w_a, = _mm_tn("g_branch_a", ya, [d_ta])
    g_w_b, = _mm_tn("g_branch_b", yb, [d_tb])
    d_ya, = _mm("d_ya", [d_ta], [w_a], "nt", [_sds((t, dp), F32)])
    d_yb, = _mm("d_yb", [d_tb], [w_b], "nt", [_sds((t, dp), BF16)])
    d_u, g_w_pool, d_pool_scale = _pool_bwd(d_ya, pooled, w_pool_b, small["pool_scale"])
    big = {
        "w_branch_a": g_w_a, "w_branch_b": g_w_b, "w_out": g_w_out.reshape(wf["w_out"].shape),
        "w_ffn_gate": g_w_gate, "w_ffn_up": g_w_up, "w_ffn_down": g_w_down.reshape(wf["w_ffn_down"].shape),
        "w_ple_gate": g_w_pg.reshape(wf["w_ple_gate"].shape), "w_ple_proj": g_w_pp,
    }
    rider = exchange_early(big) if exchange_early else _NoRider()
    (d_q, d_k, d_v), early = _attn_bwd(q, 0, kv, 0, n_pairs, d_yb, n_pairs, rider=rider)
    d_proj = [(d_u, d_q), (d_k, d_v), d_ga, d_gb]
    big["w_in"], = _mm_tn("g_w_in", h1, d_proj, tmm=512, stacked=True)
    rider = exchange_last(big["w_in"]) if exchange_last else _NoRider()
    res = _mm(
        "d_h1", d_proj, [w_in[j] for j in range(N_CHIPS)], "nt", [_sds((t, d), F32)],
        extras=[x, small["norm_mix"], dx1], epilogue=lambda dh, xv, g, dres: through_norm(dh, xv, g, dres)[1:],
        sum_shapes=gain_sum, tm=512, rider=rider)
    (grad_x, d_norm_mix), last = res if rider.operands else (res, ())
    small_g = {"norm_mix": d_norm_mix, "w_pool": g_w_pool, "pool_scale": d_pool_scale, "norm_ffn": d_norm_ffn,
               "norm_ple": d_norm_ple, "norm_final": d_norm_final}
    return grad_x, big, small_g, loss_row, early, last


def _split2(res, n):
    return res[:n], res[n:]


def _pack_small(small_g, loss_row):
    parts, layout = [], []
    for name in SMALL + ("loss",):
        v = (loss_row if name == "loss" else small_g[name]).reshape(-1, LANES)
        pad = (-v.shape[0]) % 8
        if pad:
            v = jnp.concatenate([v, jnp.zeros((pad, LANES), F32)], axis=0)
        layout.append((name, sum(q.shape[0] for q in parts), v.shape[0]))
        parts.append(v)
    return jnp.concatenate(parts, axis=0), layout


def kernel(x, p, norm_mix, w_in, w_pool, pool_scale, w_branch_a, w_branch_b, w_out, norm_ffn, w_ffn_gate, w_ffn_up, w_ffn_down, norm_ple, w_ple_gate, w_ple_proj, norm_final, loss_target, m_norm_mix, m_w_in, m_w_pool, m_pool_scale, m_w_branch_a, m_w_branch_b, m_w_out, m_norm_ffn, m_w_ffn_gate, m_w_ffn_up, m_w_ffn_down, m_norm_ple, m_w_ple_gate, m_w_ple_proj, m_norm_final, v_norm_mix, v_w_in, v_w_pool, v_pool_scale, v_w_branch_a, v_w_branch_b, v_w_out, v_norm_ffn, v_w_ffn_gate, v_w_ffn_up, v_w_ffn_down, v_norm_ple, v_w_ple_gate, v_w_ple_proj, v_norm_final):
    given = dict(locals())
    names = BIG + SMALL
    order = ("norm_mix", "w_in", "w_pool", "pool_scale", "w_branch_a", "w_branch_b", "w_out", "norm_ffn", "w_ffn_gate",
             "w_ffn_up", "w_ffn_down", "norm_ple", "w_ple_gate", "w_ple_proj", "norm_final")
    t, d = x.shape[1], x.shape[2]
    shard = {n: given[n][0] for n in BIG}
    small = {"norm_mix": norm_mix, "w_pool": w_pool[0], "pool_scale": pool_scale, "norm_ffn": norm_ffn,
             "norm_ple": norm_ple, "norm_final": norm_final}

    as_bf16 = {n: shard[n].astype(BF16) for n in BIG}

    place = jnp.stack([lax.axis_index("c"), 2 * lax.axis_index("x") + lax.axis_index("y")]).astype(jnp.int32)
    pair_sums = {}

    def exchange_early(ready):
        theirs = _pair_exchange("pair_exchange_early", [ready[n] for n in LATE])
        for n, other in zip(LATE, theirs):
            pair_sums[n] = _pair_sum(f"pair_sum_{n}", place, ready[n], other)
        return _ChipExchange([pair_sums[n] for n in LATE])

    def exchange_last(g_w_in):
        theirs, = _pair_exchange("pair_exchange_w_in", [g_w_in])
        pair_sums["w_in"] = _pair_sum("pair_sum_w_in", place, g_w_in, theirs)
        return _ChipExchange([pair_sums["w_in"]])

    grad_x, big_g, small_g, loss_row, early, last = _local_step(
        x.reshape(t, d), p.reshape(t, p.shape[-1]), loss_target.reshape(t, d), {}, small,
        gather_first=_WeightGather([as_bf16["w_in"]]),
        gather_late=_WeightGather([as_bf16[n] for n in LATE]), exchange_early=exchange_early,
        exchange_last=exchange_last)
    landed = dict(zip(LATE + ("w_in",), tuple(early) + tuple(last)))
    halves = [_sum_chips(f"chip_sum_{n}", place, pair_sums[n], landed[n]) for n in BIG]
    grads = dict(zip(BIG, _pair_share(halves)))

    packed, layout = _pack_small(small_g, loss_row)
    reduced = _all_reduce_small(packed)
    for name, start, rows in layout:
        if name == "loss":
            loss = jnp.sum(reduced[start:start + rows])
        else:
            n_el = small[name].size
            grads[name] = reduced[start:start + rows].reshape(-1)[:n_el]

    deltas, new_m, new_v = {}, {}, {}
    for n in order:
        w = shard[n] if n in BIG else small[n]
        shape2 = w.shape if w.ndim == 2 else ((1, w.shape[0]) if w.ndim == 1 else (w.shape[0] * w.shape[1], w.shape[2]))
        g2 = grads[n].reshape(shape2)
        dl, mn, vn = _adamw(f"adamw_{n}", w.reshape(shape2), g2, given["m_" + n].reshape(shape2),
                            given["v_" + n].reshape(shape2))
        full = given[n].shape
        grads[n], deltas[n], new_m[n], new_v[n] = g2.reshape(full), dl.reshape(full), mn.reshape(full), vn.reshape(full)

    return (loss, grad_x.reshape(x.shape), *[grads[n] for n in order], *[deltas[n] for n in order],
            *[new_m[n] for n in order], *[new_v[n] for n in order])
```

```python
import functools
import math

import jax
import jax.numpy as jnp
from jax import lax
from jax.experimental import pallas as pl
from jax.experimental.pallas import tpu as pltpu

F32 = jnp.float32
BF16 = jnp.bfloat16
MESH = pl.DeviceIdType.MESH

RMS_EPS = 1e-6
POOL_WINDOWS = (2, 4, 8, 16)
POOL_HALO = 16
HEAD_DIM = 64
LANES = 128
ATT_BLOCK = 256
ATT_CHUNK = 256
ATT_SLAB = 256
ATT_SCALE = 1.0 / math.sqrt(HEAD_DIM)
LOG2_E = 1.4426950408889634
ATT_EXIT_BELOW = -150.5
ADAM_LR, ADAM_B1, ADAM_B2, ADAM_EPS, ADAM_WD, ADAM_STEP = 0.001, 0.9, 0.999, 1e-08, 0.01, 10
V7X_VMEM_LIMIT_BYTES = 56 * 1024 * 1024
N_CHIPS = 4
N_DEV = 8


def _params(*semantics):
    return pltpu.CompilerParams(dimension_semantics=semantics, vmem_limit_bytes=V7X_VMEM_LIMIT_BYTES)


def _sigmoid(z):
    return 1.0 / (1.0 + jnp.exp(-z))


def _tiled_spec(shape, tm, tn, n_total, at):
    rows, width = shape
    if rows == 1:
        if width == n_total:
            return pl.BlockSpec((1, tn), at(lambda i, j: (0, j)))
        return pl.BlockSpec((1, width), at(lambda i, j: (0, 0)))
    if width == n_total:
        return pl.BlockSpec((tm, tn), at(lambda i, j: (i, j)))
    assert tn == n_total, "an operand narrower than the output needs whole output rows per tile"
    return pl.BlockSpec((tm, width), at(lambda i, j: (i, 0)))


def _column_pieces(operands):
    pieces = [tuple(a) if isinstance(a, (tuple, list)) else (a,) for a in operands]
    return [p for ps in pieces for p in ps], [len(ps) for ps in pieces]


def _load_bf16(refs, counts):
    tiles, k = [], 0
    for n in counts:
        parts = [r[...] for r in refs[k:k + n]]
        parts = [t if t.dtype == BF16 else t.astype(BF16) for t in parts]
        tiles.append(parts[0] if n == 1 else jnp.concatenate(parts, axis=1))
        k += n
    return tiles


def _mm(name, a_list, b_list, mode, out_shapes, epilogue=None, extras=(), tm=1024, tn=None, separate=False,
        sum_shapes=(), rider=None):
    flat_a, counts = _column_pieces(a_list)
    m_total = flat_a[0].shape[0]
    n_total = b_list[0].shape[1] if mode == "nn" else b_list[0].shape[0]
    tn = n_total if tn is None else tn
    tm = min(tm, m_total)
    assert m_total % tm == 0 and n_total % tn == 0 and (not sum_shapes or tn == n_total)
    n_a, n_b, n_extra, n_out = len(counts), len(b_list), len(extras), len(out_shapes)
    assert n_a in (1, n_b)
    dims = (((1,), (0,)), ((), ())) if mode == "nn" else (((1,), (1,)), ((), ()))
    rider = rider or _NoRider()
    grid = (n_total // tn, m_total // tm)

    def at(index):
        return lambda j, i: index(i, j)

    def body(*refs):
        ins, o_refs, _, riding = rider.split(refs, len(flat_a) + n_b + n_extra, n_out + len(sum_shapes))
        a_refs, b_refs, e_refs = ins[:len(flat_a)], ins[len(flat_a):len(flat_a) + n_b], ins[len(flat_a) + n_b:]
        at_first = (pl.program_id(0) == 0) & (pl.program_id(1) == 0)
        at_last = (pl.program_id(0) == grid[0] - 1) & (pl.program_id(1) == grid[1] - 1)
        top, bottom = rider.at_steps(riding, at_first, at_first, at_last)
        top()
        lefts = _load_bf16(a_refs, counts)
        products = [lax.dot_general(lefts[s % n_a], b_refs[s][...], dims, preferred_element_type=F32)
                    for s in range(n_b)]
        if not separate:
            products = [functools.reduce(lambda p, r: p + r, products)]
        extra_tiles = [e[...].astype(F32) for e in e_refs]
        outs = products if epilogue is None else epilogue(*products, *extra_tiles)
        for o_ref, o in zip(o_refs[:n_out], outs[:n_out]):
            o_ref[...] = o.astype(o_ref.dtype)
        if sum_shapes:
            @pl.when(pl.program_id(1) == 0)
            def _():
                for s_ref in o_refs[n_out:]:
                    s_ref[...] = jnp.zeros_like(s_ref)

            for s_ref, s in zip(o_refs[n_out:], outs[n_out:]):
                s_ref[...] += s
        bottom()

    once = dict(pipeline_mode=pl.Buffered(1)) if tn == n_total else {}
    in_specs = [pl.BlockSpec((tm, a.shape[1]), at(lambda i, j: (i, 0))) for a in flat_a]
    if mode == "nn":
        in_specs += [pl.BlockSpec((b.shape[0], tn), at(lambda i, j: (0, j)), **once) for b in b_list]
    else:
        in_specs += [pl.BlockSpec((tn, b.shape[1]), at(lambda i, j: (j, 0)), **once) for b in b_list]
    in_specs += [_tiled_spec(e.shape, tm, tn, n_total, at) for e in extras]
    out_specs = [_tiled_spec(o.shape, tm, tn, n_total, at) for o in out_shapes]
    out_specs += [pl.BlockSpec(s.shape, at(lambda i, j: (0, 0))) for s in sum_shapes]
    semantics = ("arbitrary", "arbitrary") if sum_shapes or rider.operands else ("parallel", "parallel")
    res = pl.pallas_call(
        body, name=name, grid=grid, in_specs=in_specs + [ANY] * len(rider.operands),
        out_specs=out_specs + [ANY] * len(rider.out_shapes),
        out_shape=list(out_shapes) + list(sum_shapes) + list(rider.out_shapes), scratch_shapes=list(rider.scratch),
        compiler_params=_params(*semantics),
    )(*flat_a, *b_list, *extras, *rider.operands)
    n_own = len(out_shapes) + len(sum_shapes)
    return res if not rider.operands else (res[:n_own], res[n_own:])


def _mm_tn(name, a_list, b_list, tmm=1024, stacked=False, n_blocks=1):
    flat_b, counts = _column_pieces(b_list)
    n_a = len(a_list)
    m_total = a_list[0].shape[0]
    ks = [a_list[s % n_a].shape[1] for s in range(len(counts))]
    widths = [sum(p.shape[1] for p in flat_b[sum(counts[:s]):sum(counts[:s + 1])]) for s in range(len(counts))]
    tmm = min(tmm, m_total)
    assert m_total % tmm == 0 and (n_blocks == 1 or max(counts) == 1) and all(w % n_blocks == 0 for w in widths)
    n_b = len(counts)
    assert n_a in (1, n_b) and not (stacked and n_a > 1)

    def body(*refs):
        a_refs, b_refs, o_refs = refs[:n_a], refs[n_a:n_a + len(flat_b)], refs[n_a + len(flat_b):]

        @pl.when(pl.program_id(1) == 0)
        def _():
            for o_ref in o_refs:
                o_ref[...] = jnp.zeros_like(o_ref)

        lefts = _load_bf16(a_refs, [1] * n_a)
        for s, bv in enumerate(_load_bf16(b_refs, counts)):
            product = lax.dot_general(lefts[s % n_a], bv, (((0,), (0,)), ((), ())), preferred_element_type=F32)
            if stacked:
                o_refs[0][s] += product
            else:
                o_refs[s][...] += product

    in_specs = [pl.BlockSpec((tmm, a.shape[1]), lambda nb, m: (m, 0)) for a in a_list]
    in_specs += [pl.BlockSpec((tmm, b.shape[1] // n_blocks), lambda nb, m: (m, nb)) for b in flat_b]
    if stacked:
        out_shape = [jax.ShapeDtypeStruct((n_b, ks[0], widths[0]), F32)]
        out_specs = [pl.BlockSpec((n_b, ks[0], widths[0] // n_blocks), lambda nb, m: (0, 0, nb))]
    else:
        out_shape = [jax.ShapeDtypeStruct((k, w), F32) for k, w in zip(ks, widths)]
        out_specs = [pl.BlockSpec((k, w // n_blocks), lambda nb, m: (0, nb)) for k, w in zip(ks, widths)]
    return pl.pallas_call(
        body, name=name, grid=(n_blocks, m_total // tmm), in_specs=in_specs, out_specs=out_specs, out_shape=out_shape,
        compiler_params=_params("arbitrary", "arbitrary"),
    )(*a_list, *flat_b)


def _rows(name, fn, ins, tile_outs, sum_outs=(), tr=512, rider=None):
    t_total = max(a.shape[0] for a in ins)
    tr = min(tr, t_total)
    assert t_total % tr == 0
    n_in, n_tile = len(ins), len(tile_outs)
    rider = rider or _NoRider()
    n_steps = t_total // tr

    def body(*refs):
        own_ins, own_outs, _, riding = rider.split(refs, n_in, n_tile + len(sum_outs))
        step = pl.program_id(0)
        top, bottom = rider.at_steps(riding, step == 0, step == n_steps - 1, step == n_steps - 1)
        top()
        refs = tuple(own_ins) + tuple(own_outs)
        outs = fn(*[r[...].astype(F32) for r in refs[:n_in]])
        for o_ref, o in zip(refs[n_in:n_in + n_tile], outs[:n_tile]):
            o_ref[...] = o.astype(o_ref.dtype)
        if sum_outs:
            @pl.when(pl.program_id(0) == 0)
            def _():
                for s_ref in refs[n_in + n_tile:]:
                    s_ref[...] = jnp.zeros_like(s_ref)

            for s_ref, s in zip(refs[n_in + n_tile:], outs[n_tile:]):
                s_ref[...] += s
        bottom()

    def spec(shape):
        if shape[0] == 1:
            return pl.BlockSpec(shape, lambda i: (0, 0))
        return pl.BlockSpec((tr, shape[1]), lambda i: (i, 0))

    return pl.pallas_call(
        body, name=name, grid=(n_steps,), in_specs=[spec(a.shape) for a in ins] + [ANY] * len(rider.operands),
        out_specs=[spec(o.shape) for o in tile_outs] + [spec(s.shape) for s in sum_outs] + [ANY] * len(rider.out_shapes),
        out_shape=list(tile_outs) + list(sum_outs) + list(rider.out_shapes), scratch_shapes=list(rider.scratch),
        compiler_params=_params("arbitrary" if sum_outs or rider.operands else "parallel"),
    )(*ins, *rider.operands)


def _norm_fwd(name, x, gain, rider=None):
    def fn(xv, g):
        inv = lax.rsqrt(jnp.mean(xv * xv, axis=-1, keepdims=True) + RMS_EPS)
        return (xv * inv * g,)

    res = _rows(name, fn, [x, gain], [jax.ShapeDtypeStruct(x.shape, BF16)], rider=rider)
    return res[0], res[1:]


def _rms_norm_bwd(dh, xv, g):
    inv = lax.rsqrt(jnp.mean(xv * xv, axis=-1, keepdims=True) + RMS_EPS)
    xn = xv * inv
    dxn = dh * g
    return inv * (dxn - xn * jnp.mean(dxn * xn, axis=-1, keepdims=True)), jnp.sum(dh * xn, axis=0, keepdims=True)


def _ple_and_loss(gv, pv, x2v, tv, g):
    d = x2v.shape[1]
    s = _sigmoid(gv)
    xv = x2v + s * pv
    inv = lax.rsqrt(jnp.mean(xv * xv, axis=-1, keepdims=True) + RMS_EPS)
    err = xv * inv * g - tv
    dx, d_gain = _rms_norm_bwd(err * (1.0 / d), xv, g)
    return dx, dx * s, dx * pv * s * (1.0 - s), d_gain, (0.5 / d) * jnp.sum(err * err, axis=0, keepdims=True)


def _window_counts(t_pos, w):
    return jnp.minimum(t_pos + 1, w).astype(F32)


def _pool_fwd(u, w_pool, scale, tr=512):
    t_total, width = u.shape
    tr = min(tr, t_total)
    n_groups = len(POOL_WINDOWS)
    gdim = width // n_groups
    ext = tr + POOL_HALO

    def body(u_ref, halo_ref, w_ref, s_ref, pooled_ref, ya_ref):
        i = pl.program_id(0)
        halo = jnp.where(i == 0, 0.0, halo_ref[...])
        t_pos = i * tr + lax.broadcasted_iota(jnp.int32, (tr, 1), 0)
        for g, w in enumerate(POOL_WINDOWS):
            cols = slice(g * gdim, (g + 1) * gdim)
            main = u_ref[:, cols]
            win = jnp.concatenate([halo[:, cols], main], axis=0)
            span = 1
            while span < w:
                win = win + pltpu.roll(win, span, 0)
                span *= 2
            pooled = win[POOL_HALO:, :] * (1.0 / _window_counts(t_pos, w)) - main
            pooled_b = pooled.astype(BF16)
            pooled_ref[:, cols] = pooled_b
            mixed = jnp.dot(pooled_b, w_ref[g], preferred_element_type=F32)
            ya_ref[:, cols] = (mixed * s_ref[:, cols]).astype(BF16)

    hb = tr // POOL_HALO
    return pl.pallas_call(
        body, name="pool_fwd", grid=(t_total // tr,),
        in_specs=[pl.BlockSpec((tr, width), lambda i: (i, 0)),
                  pl.BlockSpec((POOL_HALO, width), lambda i: (jnp.maximum(i * hb - 1, 0), 0)),
                  pl.BlockSpec((n_groups, gdim, gdim), lambda i: (0, 0, 0)),
                  pl.BlockSpec((1, width), lambda i: (0, 0))],
        out_specs=[pl.BlockSpec((tr, width), lambda i: (i, 0)), pl.BlockSpec((tr, width), lambda i: (i, 0))],
        out_shape=[jax.ShapeDtypeStruct(u.shape, BF16), jax.ShapeDtypeStruct(u.shape, BF16)],
        compiler_params=_params("parallel"),
    )(u, u, w_pool, scale)


def _pool_bwd(dya, pooled, w_pool, scale, tr=512):
    t_total, width = dya.shape
    tr = min(tr, t_total)
    n_groups = len(POOL_WINDOWS)
    gdim = width // n_groups
    ext = tr + POOL_HALO
    n_tiles = t_total // tr

    def body(d_ref, halo_ref, p_ref, w_ref, s_ref, du_ref, dw_ref, ds_ref):
        i = pl.program_id(0)

        @pl.when(i == 0)
        def _():
            dw_ref[...] = jnp.zeros_like(dw_ref)
            ds_ref[...] = jnp.zeros_like(ds_ref)

        halo = jnp.where(i == n_tiles - 1, 0.0, halo_ref[...])
        t_pos = i * tr + lax.broadcasted_iota(jnp.int32, (ext, 1), 0)
        for g, w in enumerate(POOL_WINDOWS):
            cols = slice(g * gdim, (g + 1) * gdim)
            sc = s_ref[:, cols]
            d_main = d_ref[:, cols]
            pooled_b = p_ref[:, cols]
            mixed = jnp.dot(pooled_b, w_ref[g], preferred_element_type=F32)
            ds_ref[:, cols] += jnp.sum(d_main * mixed, axis=0, keepdims=True)
            dmix = (jnp.concatenate([d_main, halo[:, cols]], axis=0) * sc).astype(BF16)
            dw_ref[g] += lax.dot_general(pooled_b, dmix[:tr, :], (((0,), (0,)), ((), ())),
                                         preferred_element_type=F32)
            dpool = lax.dot_general(dmix, w_ref[g], (((1,), (1,)), ((), ())), preferred_element_type=F32)
            win = dpool * (1.0 / _window_counts(t_pos, w))
            span = 1
            while span < w:
                win = win + pltpu.roll(win, ext - span, 0)
                span *= 2
            du_ref[:, cols] = (win[:tr, :] - dpool[:tr, :]).astype(BF16)

    hb = tr // POOL_HALO
    last_halo = t_total // POOL_HALO - 1
    return pl.pallas_call(
        body, name="pool_bwd", grid=(n_tiles,),
        in_specs=[pl.BlockSpec((tr, width), lambda i: (i, 0)),
                  pl.BlockSpec((POOL_HALO, width), lambda i: (jnp.minimum((i + 1) * hb, last_halo), 0)),
                  pl.BlockSpec((tr, width), lambda i: (i, 0)),
                  pl.BlockSpec((n_groups, gdim, gdim), lambda i: (0, 0, 0)),
                  pl.BlockSpec((1, width), lambda i: (0, 0))],
        out_specs=[pl.BlockSpec((tr, width), lambda i: (i, 0)),
                   pl.BlockSpec((n_groups, gdim, gdim), lambda i: (0, 0, 0)),
                   pl.BlockSpec((1, width), lambda i: (0, 0))],
        out_shape=[jax.ShapeDtypeStruct(dya.shape, BF16), jax.ShapeDtypeStruct((n_groups, gdim, gdim), F32),
                   jax.ShapeDtypeStruct((1, width), F32)],
        compiler_params=_params("arbitrary"),
    )(dya, dya, pooled, w_pool, scale)


def _head_masks():
    lane = lax.broadcasted_iota(jnp.int32, (1, LANES), 1)
    return lane < HEAD_DIM


def _stack_heads(tile, first):
    zero = jnp.zeros_like(tile)
    return jnp.concatenate([jnp.where(first, tile, zero), jnp.where(first, zero, tile)], axis=0)


def _causal_mask(t_pos, k_start):
    col = lax.broadcasted_iota(jnp.int32, (1, 2 * ATT_SLAB), 1)
    return k_start + (col & (ATT_SLAB - 1)) < t_pos


def _slab_scores(q, kd, mask):
    z2 = lax.dot_general(q, kd, (((1,), (1,)), ((), ())), preferred_element_type=F32) * LOG2_E
    log_hit = jnp.minimum(z2, 0.0) - jnp.log2(1.0 + jnp.exp2(-jnp.abs(z2)))
    log_fail = log_hit - z2
    return log_hit, (log_fail if mask is None else jnp.where(mask, log_fail, 0.0))


def _weights(log_hit, suffix, mask):
    arg = log_hit + suffix
    return jnp.exp2(arg if mask is None else jnp.where(mask, arg, -1e30))


def _tri(upper):
    r = lax.broadcasted_iota(jnp.int32, (ATT_CHUNK, ATT_CHUNK), 0)
    c = lax.broadcasted_iota(jnp.int32, (ATT_CHUNK, ATT_CHUNK), 1)
    return jnp.where(r > c if upper else r < c, 1.0, 0.0).astype(BF16)


def _tri_spec():
    return pl.BlockSpec((ATT_CHUNK, ATT_CHUNK), lambda h, i: (0, 0), pipeline_mode=pl.Buffered(1))


def _scan_chunk(v, tri):
    return jnp.dot(v.astype(BF16), tri, preferred_element_type=F32)


def _lane_bcast(col):
    return jnp.broadcast_to(col, (col.shape[0], LANES))


def _scan_slab(v, tri, carries, from_right):
    n_chunks = ATT_SLAB // ATT_CHUNK
    edge = 0 if from_right else ATT_CHUNK - 1
    parts, new_carries = [None] * (2 * n_chunks), []
    for head in range(2):
        run = carries[head]
        for c in (reversed(range(n_chunks)) if from_right else range(n_chunks)):
            lo_col = head * ATT_SLAB + c * ATT_CHUNK
            vc = v[:, lo_col:lo_col + ATT_CHUNK]
            sc = _scan_chunk(vc, tri)
            parts[head * n_chunks + c] = sc + jnp.concatenate([run] * (ATT_CHUNK // LANES), axis=1)
            run = run + _lane_bcast(sc[:, edge:edge + 1] + vc[:, edge:edge + 1])
        new_carries.append(run)
    return jnp.concatenate(parts, axis=1), new_carries


def _fold_heads(stacked, first):
    s = stacked.shape[0] // 2
    return jnp.where(first, stacked[:s], stacked[s:])


class _NoRider:
    operands, out_shapes, scratch = (), (), ()

    def split(self, refs, n_base_in, n_base_out):
        n_in, n_out, n_sem = len(self.operands), len(self.out_shapes), len(self.scratch)
        a = n_base_in + n_in
        b = a + n_base_out + n_out
        mine = (refs[n_base_in:a], refs[a + n_base_out:b], refs[b:b + n_sem])
        return refs[:n_base_in], refs[a:a + n_base_out], refs[b + n_sem:], mine

    def start(self, ins, outs, sems):
        pass

    def relay(self, ins, outs, sems):
        pass

    def finish(self, ins, outs, sems):
        pass

    def at_steps(self, refs, first_step, relay_step, last_step):
        if not self.operands:
            return (lambda: None), (lambda: None)

        def top():
            pl.when(first_step)(lambda: self.start(*refs))
            pl.when(relay_step)(lambda: self.relay(*refs))

        return top, lambda: pl.when(last_step)(lambda: self.finish(*refs))


def _attn_fwd(q_src, q_col, kv_src, k_col, v_col, n_pairs=4, rider=_NoRider()):
    t_total = q_src.shape[0]
    blk = ATT_BLOCK
    n_blocks = t_total // blk
    assert t_total % ATT_SLAB == 0 and ATT_SLAB % ATT_BLOCK == 0

    def body(*refs):
        (q_ref, k_ref, v_ref, suffix_ref), (o_ref,), _, riding = rider.split(refs, 4, 1)
        h, i = pl.program_id(0), pl.program_id(1)
        top, bottom = rider.at_steps(riding, (h == 0) & (i == 0), (h == n_pairs - 1) & (i == 0),
                                     (h == n_pairs - 1) & (i == n_blocks - 1))
        top()
        first = _head_masks()
        q = q_ref[...] * ATT_SCALE
        t_pos = i * blk + lax.broadcasted_iota(jnp.int32, (blk, 1), 0)
        suffix_tri = suffix_ref[...]

        def more(state):
            slab, reach = state[0], state[1]
            return jnp.logical_and(slab >= 0, reach > ATT_EXIT_BELOW)

        def step(state, on_diagonal):
            slab, _, acc, right_a, right_b = state
            k_start = pl.multiple_of(slab * ATT_SLAB, ATT_SLAB)
            kd = _stack_heads(k_ref[pl.ds(k_start, ATT_SLAB), :], first)
            vd = _stack_heads(v_ref[pl.ds(k_start, ATT_SLAB), :], first)
            mask = _causal_mask(t_pos, k_start) if on_diagonal else None
            log_hit, log_fail = _slab_scores(q, kd, mask)
            suffix, (right_a, right_b) = _scan_slab(log_fail, suffix_tri, (right_a, right_b), from_right=True)
            a = _weights(log_hit, suffix, mask).astype(BF16)
            acc = acc + jnp.dot(a, vd, preferred_element_type=F32)
            return slab - 1, jnp.max(jnp.maximum(right_a, right_b)), acc, right_a, right_b

        zero = jnp.zeros((blk, LANES), F32)
        state = step(((i * blk) // ATT_SLAB, jnp.float32(0.0), zero, zero, zero), on_diagonal=True)
        state = lax.while_loop(more, functools.partial(step, on_diagonal=False), state)
        o_ref[...] = state[2].astype(BF16)
        bottom()

    res = pl.pallas_call(
        body, name="attn_fwd", grid=(n_pairs, n_blocks),
        in_specs=[pl.BlockSpec((blk, LANES), lambda h, i: (i, q_col + h)),
                  pl.BlockSpec((t_total, LANES), lambda h, i: (0, k_col + h)),
                  pl.BlockSpec((t_total, LANES), lambda h, i: (0, v_col + h)), _tri_spec()] + [ANY] * len(rider.operands),
        out_specs=[pl.BlockSpec((blk, LANES), lambda h, i: (i, h))] + [ANY] * len(rider.out_shapes),
        out_shape=[jax.ShapeDtypeStruct((t_total, n_pairs * LANES), BF16)] + list(rider.out_shapes),
        scratch_shapes=list(rider.scratch),
        compiler_params=_params("arbitrary", "arbitrary"),
    )(q_src, kv_src, kv_src, _tri(upper=True), *rider.operands)
    return res[0], res[1:]


def _attn_bwd(q_src, q_col, kv_src, k_col, v_col, dy, n_pairs=4, rider=_NoRider()):
    t_total = q_src.shape[0]
    blk = ATT_BLOCK
    n_blocks = t_total // blk
    n_slabs = t_total // ATT_SLAB
    assert t_total % ATT_SLAB == 0 and ATT_SLAB % ATT_BLOCK == 0

    def body(*refs):
        ins, (dq_ref, dk_ref, dv_ref), (g_s, dk_acc, dv_acc), riding = rider.split(refs, 6, 3)
        q_ref, dy_ref, k_ref, v_ref, suffix_ref, prefix_ref = ins
        h, i = pl.program_id(0), pl.program_id(1)
        top, bottom = rider.at_steps(riding, (h == 0) & (i == 0), (h == n_pairs - 1) & (i == 0),
                                     (h == n_pairs - 1) & (i == n_blocks - 1))
        top()

        @pl.when(i == 0)
        def _():
            dk_acc[...] = jnp.zeros_like(dk_acc)
            dv_acc[...] = jnp.zeros_like(dv_acc)

        first = _head_masks()
        q = q_ref[...] * ATT_SCALE
        dy = dy_ref[...]
        t_pos = i * blk + lax.broadcasted_iota(jnp.int32, (blk, 1), 0)
        suffix_tri = suffix_ref[...]
        prefix_tri = prefix_ref[...]
        diag = (i * blk) // ATT_SLAB

        def more(state):
            slab, reach = state[0], state[1]
            return jnp.logical_and(slab >= 0, reach > ATT_EXIT_BELOW)

        def sweep1(state, on_diagonal):
            slab, _, right_a, right_b = state
            k_start = pl.multiple_of(slab * ATT_SLAB, ATT_SLAB)
            kd = _stack_heads(k_ref[pl.ds(k_start, ATT_SLAB), :], first)
            vd = _stack_heads(v_ref[pl.ds(k_start, ATT_SLAB), :], first)
            mask = _causal_mask(t_pos, k_start) if on_diagonal else None
            log_hit, log_fail = _slab_scores(q, kd, mask)
            suffix, (right_a, right_b) = _scan_slab(log_fail, suffix_tri, (right_a, right_b), from_right=True)
            a = _weights(log_hit, suffix, mask)
            da = lax.dot_general(dy, vd, (((1,), (1,)), ((), ())), preferred_element_type=F32)
            g_s[slab] = da * a
            dv_acc[pl.ds(k_start, ATT_SLAB), :] += _fold_heads(lax.dot_general(
                a.astype(BF16), dy, (((0,), (0,)), ((), ())), preferred_element_type=F32), first)
            return slab - 1, jnp.max(jnp.maximum(right_a, right_b)), right_a, right_b

        zero = jnp.zeros((blk, LANES), F32)
        state = sweep1((diag, jnp.float32(0.0), zero, zero), on_diagonal=True)
        end = lax.while_loop(more, functools.partial(sweep1, on_diagonal=False), state)[0]

        def sweep2(slab, carry, on_diagonal):
            dq, left_a, left_b = carry
            k_start = pl.multiple_of(slab * ATT_SLAB, ATT_SLAB)
            kd = _stack_heads(k_ref[pl.ds(k_start, ATT_SLAB), :], first)
            g = g_s[slab]
            z2 = lax.dot_general(q, kd, (((1,), (1,)), ((), ())), preferred_element_type=F32) * LOG2_E
            sig = 1.0 / (1.0 + jnp.exp2(-z2))
            prefix, (left_a, left_b) = _scan_slab(g, prefix_tri, (left_a, left_b), from_right=False)
            dz = g * (1.0 - sig) - sig * prefix
            if on_diagonal:
                dz = jnp.where(_causal_mask(t_pos, k_start), dz, 0.0)
            dz = dz.astype(BF16)
            dq = dq + jnp.dot(dz, kd, preferred_element_type=F32)
            dk_acc[pl.ds(k_start, ATT_SLAB), :] += _fold_heads(lax.dot_general(
                dz, q, (((0,), (0,)), ((), ())), preferred_element_type=F32), first)
            return dq, left_a, left_b

        carry = lax.fori_loop(end + 1, diag, functools.partial(sweep2, on_diagonal=False), (zero, zero, zero))
        dq = sweep2(diag, carry, on_diagonal=True)[0]
        dq_ref[...] = (dq * ATT_SCALE).astype(BF16)

        @pl.when(i == n_blocks - 1)
        def _():
            dk_ref[...] = dk_acc[...].astype(BF16)
            dv_ref[...] = dv_acc[...].astype(BF16)

        bottom()

    out = jax.ShapeDtypeStruct((t_total, n_pairs * LANES), BF16)
    res = pl.pallas_call(
        body, name="attn_bwd", grid=(n_pairs, n_blocks),
        in_specs=[pl.BlockSpec((blk, LANES), lambda h, i: (i, q_col + h)),
                  pl.BlockSpec((blk, LANES), lambda h, i: (i, h)),
                  pl.BlockSpec((t_total, LANES), lambda h, i: (0, k_col + h)),
                  pl.BlockSpec((t_total, LANES), lambda h, i: (0, v_col + h)), _tri_spec(), _tri_spec()]
        + [ANY] * len(rider.operands),
        out_specs=[pl.BlockSpec((blk, LANES), lambda h, i: (i, h)),
                   pl.BlockSpec((t_total, LANES), lambda h, i: (0, h)),
                   pl.BlockSpec((t_total, LANES), lambda h, i: (0, h))] + [ANY] * len(rider.out_shapes),
        out_shape=[out, out, out] + list(rider.out_shapes),
        scratch_shapes=list(rider.scratch) + [pltpu.VMEM((n_slabs, blk, 2 * ATT_SLAB), F32),
                                              pltpu.VMEM((t_total, LANES), F32), pltpu.VMEM((t_total, LANES), F32)],
        compiler_params=_params("arbitrary", "arbitrary"),
    )(q_src, dy, kv_src, kv_src, _tri(upper=True), _tri(upper=False), *rider.operands)
    return res[:3], res[3:]


def _adamw(name, w, g, m, v):
    def fn(wv, gv, mv, vv):
        mn = ADAM_B1 * mv + (1.0 - ADAM_B1) * gv
        vn = ADAM_B2 * vv + (1.0 - ADAM_B2) * (gv * gv)
        m_hat = mn / (1.0 - ADAM_B1 ** ADAM_STEP)
        v_hat = vn / (1.0 - ADAM_B2 ** ADAM_STEP)
        return -ADAM_LR * (m_hat / (jnp.sqrt(v_hat) + ADAM_EPS) + ADAM_WD * wv), mn, vn

    rows = w.shape[0]
    tr = _row_tile(rows)
    shp = jax.ShapeDtypeStruct(w.shape, F32)
    if rows == 1:
        def body(w_ref, g_ref, m_ref, v_ref, d_ref, mo_ref, vo_ref):
            d, mn, vn = fn(w_ref[...], g_ref[...], m_ref[...], v_ref[...])
            d_ref[...], mo_ref[...], vo_ref[...] = d, mn, vn

        return pl.pallas_call(body, name=name, out_shape=[shp, shp, shp])(w, g, m, v)
    return _rows(name, fn, [w, g, m, v], [shp, shp, shp], tr=tr)


def _place():
    return lax.axis_index("x"), lax.axis_index("y"), lax.axis_index("c")


def _other_chips(x, y):
    return [(1 - x, y), (x, 1 - y), (1 - x, 1 - y)]


ANY = pl.BlockSpec(memory_space=pl.ANY)


def _remote(src, dst, send_sem, recv_sem, to):
    return pltpu.make_async_remote_copy(src_ref=src, dst_ref=dst, send_sem=send_sem, recv_sem=recv_sem,
                                        device_id=to, device_id_type=MESH)


class _WeightGather(_NoRider):
    def __init__(self, shards):
        n_w = len(shards)
        self.operands = list(shards)
        self.out_shapes = [jax.ShapeDtypeStruct((N_CHIPS,) + s.shape, s.dtype) for s in shards]
        self.scratch = [pltpu.SemaphoreType.DMA((3, n_w))] * 4 + [pltpu.SemaphoreType.DMA((n_w,))] * 2

    def _copies(self, ins, outs, sems):
        send_sems, recv_sems, relay_send, relay_recv, own_send, own_recv = sems
        x, y, c = _place()
        my_chip, sibling = 2 * x + y, (x, y, 1 - c)
        n_w = len(ins)

        def half(w, chip, core):
            h = self.operands[w].shape[0] // 2
            return outs[w].at[chip, pl.ds(core * h, h)]

        own = [_remote(ins[w], outs[w].at[my_chip], own_send.at[w], own_recv.at[w], sibling) for w in range(n_w)]
        sends, landed, relays, relayed = [], [], [], []
        for p, (ox, oy) in enumerate(_other_chips(x, y)):
            for w in range(n_w):
                h = self.operands[w].shape[0] // 2
                sends.append(_remote(ins[w].at[pl.ds(c * h, h)], half(w, my_chip, c), send_sems.at[p, w],
                                     recv_sems.at[p, w], (ox, oy, c)))
                here = half(w, 2 * ox + oy, c)
                landed.append(_remote(here, here, send_sems.at[p, w], recv_sems.at[p, w], (ox, oy, c)))
                relays.append(_remote(here, here, relay_send.at[p, w], relay_recv.at[p, w], sibling))
                there = half(w, 2 * ox + oy, 1 - c)
                relayed.append(_remote(there, there, relay_send.at[p, w], relay_recv.at[p, w], sibling))
        return own, sends, landed, relays, relayed

    def start(self, ins, outs, sems):
        own, sends, _, _, _ = self._copies(ins, outs, sems)
        for cp in own + sends:
            cp.start()

    def relay(self, ins, outs, sems):
        _, _, landed, relays, _ = self._copies(ins, outs, sems)
        for arrival, cp in zip(landed, relays):
            arrival.wait_recv()
            cp.start()

    def finish(self, ins, outs, sems):
        own, sends, _, relays, relayed = self._copies(ins, outs, sems)
        for arrival in relayed:
            arrival.wait_recv()
        for cp in sends + relays:
            cp.wait_send()
        for cp in own:
            cp.wait()


class _ChipExchange(_NoRider):
    def __init__(self, pair_sums):
        n_w = len(pair_sums)
        self.operands = list(pair_sums)
        self.out_shapes = [jax.ShapeDtypeStruct((3,) + s.shape[1:], s.dtype) for s in pair_sums]
        self.scratch = [pltpu.SemaphoreType.DMA((3, n_w))] * 2

    def _copies(self, ins, outs, sems):
        send_sems, recv_sems = sems
        x, y, c = _place()
        return [_remote(ins[w].at[2 * ox + oy], outs[w].at[p], send_sems.at[p, w], recv_sems.at[p, w], (ox, oy, c))
                for p, (ox, oy) in enumerate(_other_chips(x, y)) for w in range(len(ins))]

    def start(self, ins, outs, sems):
        for cp in self._copies(ins, outs, sems):
            cp.start()

    def finish(self, ins, outs, sems):
        for cp in self._copies(ins, outs, sems):
            cp.wait()


def _pair_exchange(name, grads):
    n_w = len(grads)

    def halves(w):
        return grads[w].shape[-2] // 2

    def body(*refs):
        ins, theirs = refs[:n_w], refs[n_w:2 * n_w]
        send_sems, recv_sems = refs[2 * n_w:]
        x, y, c = _place()
        sends = []
        for w in range(n_w):
            rows = pl.ds((1 - c) * halves(w), halves(w))
            src = ins[w].at[:, rows, :] if grads[w].ndim == 3 else ins[w].at[rows, :]
            sends.append(_remote(src, theirs[w], send_sems.at[w], recv_sems.at[w], (x, y, 1 - c)))
        for cp in sends:
            cp.start()
        for cp in sends:
            cp.wait()

    return pl.pallas_call(
        body, name=name, in_specs=[ANY] * n_w, out_specs=[ANY] * n_w,
        out_shape=[jax.ShapeDtypeStruct(g.shape[:-2] + (halves(w), g.shape[-1]), F32) for w, g in enumerate(grads)],
        scratch_shapes=[pltpu.SemaphoreType.DMA((n_w,)), pltpu.SemaphoreType.DMA((n_w,))],
    )(*grads)


def _pair_share(shards):
    n_w = len(shards)

    def body(*refs):
        ins, outs = refs[:n_w], refs[n_w:2 * n_w]
        send_sems, recv_sems = refs[2 * n_w:]
        x, y, c = _place()
        sends = []
        for w in range(n_w):
            h = shards[w].shape[0] // 2
            mine = outs[w].at[pl.ds(c * h, h)]
            sends.append(pltpu.make_async_remote_copy(
                src_ref=mine, dst_ref=mine, send_sem=send_sems.at[w], recv_sem=recv_sems.at[w],
                device_id=(x, y, 1 - c), device_id_type=MESH))
        for cp in sends:
            cp.start()
        for w in range(n_w):
            h = shards[w].shape[0] // 2
            theirs = outs[w].at[pl.ds((1 - c) * h, h)]
            pltpu.make_async_remote_copy(
                src_ref=theirs, dst_ref=theirs, send_sem=send_sems.at[w], recv_sem=recv_sems.at[w],
                device_id=(x, y, 1 - c), device_id_type=MESH).wait_recv()
        for cp in sends:
            cp.wait_send()

    return pl.pallas_call(
        body, name="pair_share", in_specs=[ANY] * n_w, out_specs=[ANY] * n_w,
        out_shape=[jax.ShapeDtypeStruct(s.shape, s.dtype) for s in shards],
        input_output_aliases={w: w for w in range(n_w)},
        scratch_shapes=[pltpu.SemaphoreType.DMA((n_w,)), pltpu.SemaphoreType.DMA((n_w,))],
    )(*shards)


def _all_reduce_small(vec):
    rows = vec.shape[0]

    def body(v_ref, o_ref, slots, send_sems, recv_sems):
        x, y, c = _place()
        me = 4 * x + 2 * y + c
        slots[me] = v_ref[...]
        sends = []
        for k in range(1, N_DEV):
            peer = (x ^ (k >> 2), y ^ ((k >> 1) & 1), c ^ (k & 1))
            sends.append(pltpu.make_async_remote_copy(
                src_ref=v_ref, dst_ref=slots.at[me], send_sem=send_sems.at[k - 1], recv_sem=recv_sems.at[k - 1],
                device_id=peer, device_id_type=MESH))
        for cp in sends:
            cp.start()
        for k in range(1, N_DEV):
            px, py, pc = x ^ (k >> 2), y ^ ((k >> 1) & 1), c ^ (k & 1)
            landed = slots.at[4 * px + 2 * py + pc]
            pltpu.make_async_remote_copy(
                src_ref=landed, dst_ref=landed, send_sem=send_sems.at[k - 1], recv_sem=recv_sems.at[k - 1],
                device_id=(px, py, pc), device_id_type=MESH).wait_recv()
        for cp in sends:
            cp.wait_send()
        total = slots[0]
        for d in range(1, N_DEV):
            total = total + slots[d]
        o_ref[...] = total

    vm = pl.BlockSpec(memory_space=pltpu.VMEM)
    return pl.pallas_call(
        body, name="all_reduce_small", in_specs=[vm], out_specs=vm, out_shape=jax.ShapeDtypeStruct(vec.shape, F32),
        scratch_shapes=[pltpu.VMEM((N_DEV, rows, LANES), F32), pltpu.SemaphoreType.DMA((N_DEV - 1,)),
                        pltpu.SemaphoreType.DMA((N_DEV - 1,))],
    )(vec)


def _row_tile(rows):
    fits = [tr for tr in range(16, min(rows, 512) + 1, 16) if rows % tr == 0]
    return max(fits) if fits else rows


def _pair_sum(name, place, grad, theirs):
    if grad.ndim == 2:
        return _pair_sum_joined(name, place, grad, theirs)
    n, r, c = grad.shape
    half = r // 2
    tr = _row_tile(half)
    nb = half // tr

    def body(place_ref, g_ref, t_ref, o_ref):
        o_ref[...] = (g_ref[...] + t_ref[...]).astype(BF16)

    return pl.pallas_call(
        body, name=name, out_shape=jax.ShapeDtypeStruct((n, half, c), BF16),
        grid_spec=pltpu.PrefetchScalarGridSpec(
            num_scalar_prefetch=1, grid=(n, nb),
            in_specs=[pl.BlockSpec((1, tr, c), lambda j, i, pr: (j, pr[0] * nb + i, 0)),
                      pl.BlockSpec((1, tr, c), lambda j, i, pr: (j, i, 0))],
            out_specs=pl.BlockSpec((1, tr, c), lambda j, i, pr: (j, i, 0))),
        compiler_params=_params("parallel", "parallel"),
    )(place, grad, theirs)


def _pair_sum_joined(name, place, grad, theirs):
    r, wide = grad.shape
    half, c = r // 2, wide // N_CHIPS
    tr = _row_tile(half)
    nb = half // tr

    def body(place_ref, g_ref, t_ref, o_ref):
        for j in range(N_CHIPS):
            cols = slice(j * c, (j + 1) * c)
            o_ref[j] = (g_ref[:, cols] + t_ref[:, cols]).astype(BF16)

    return pl.pallas_call(
        body, name=name, out_shape=jax.ShapeDtypeStruct((N_CHIPS, half, c), BF16),
        grid_spec=pltpu.PrefetchScalarGridSpec(
            num_scalar_prefetch=1, grid=(nb,),
            in_specs=[pl.BlockSpec((tr, wide), lambda i, pr: (pr[0] * nb + i, 0)),
                      pl.BlockSpec((tr, wide), lambda i, pr: (i, 0))],
            out_specs=pl.BlockSpec((N_CHIPS, tr, c), lambda i, pr: (0, i, 0))),
        compiler_params=_params("parallel"),
    )(place, grad, theirs)


def _sum_chips(name, place, pair_sums, landed):
    _, half, c = pair_sums.shape
    tr = _row_tile(half)
    nb = half // tr

    def body(place_ref, s_ref, q_ref, o_ref):
        total = s_ref[0].astype(F32)
        for p in range(3):
            total = total + q_ref[p].astype(F32)
        o_ref[...] = total

    return pl.pallas_call(
        body, name=name, out_shape=jax.ShapeDtypeStruct((2 * half, c), F32),
        grid_spec=pltpu.PrefetchScalarGridSpec(
            num_scalar_prefetch=1, grid=(nb,),
            in_specs=[pl.BlockSpec((1, tr, c), lambda i, pr: (pr[1], i, 0)),
                      pl.BlockSpec((3, tr, c), lambda i, pr: (0, i, 0))],
            out_specs=pl.BlockSpec((tr, c), lambda i, pr: (pr[0] * nb + i, 0))),
        compiler_params=_params("parallel"),
    )(place, pair_sums, landed)


BIG = ("w_in", "w_branch_a", "w_branch_b", "w_out", "w_ffn_gate", "w_ffn_up", "w_ffn_down", "w_ple_gate", "w_ple_proj")
LATE = BIG[1:]
COLUMN_SHARDED = ("w_in", "w_branch_a", "w_branch_b", "w_ffn_gate", "w_ffn_up", "w_ple_proj")
SMALL = ("norm_mix", "w_pool", "pool_scale", "norm_ffn", "norm_ple", "norm_final")


def _join_columns(w4):
    return jnp.concatenate([w4[j] for j in range(N_CHIPS)], axis=1)


def _sds(shape, dtype):
    return jax.ShapeDtypeStruct(shape, dtype)


def _local_step(x, p, target, wf, small, gather_first=None, gather_late=None, exchange_early=None,
                exchange_last=None):
    t, d = x.shape
    w_pool_b = small["w_pool"].astype(BF16)
    dp = w_pool_b.shape[0] * w_pool_b.shape[1]

    h1, first = _norm_fwd("norm_mix", x, small["norm_mix"], rider=gather_first)
    w_in = first[0] if gather_first else wf["w_in"]
    u, q, kv, ga, gb = _mm(
        "proj", [h1], [w_in[j] for j in range(N_CHIPS)], "nn",
        [_sds((t, dp), F32), _sds((t, dp), BF16), _sds((t, d), BF16), _sds((t, d), BF16), _sds((t, d), BF16)],
        separate=True, epilogue=lambda uq, kv_, ga_, gb_: (uq[:, :dp], uq[:, dp:], kv_, ga_, gb_), tm=512)
    pooled, ya = _pool_fwd(u, w_pool_b, small["pool_scale"])
    n_pairs = dp // LANES
    yb, late = _attn_fwd(q, 0, kv, 0, n_pairs, n_pairs, rider=gather_late or _NoRider())
    wf = {**wf, **dict(zip(LATE, late))}
    w_gate, w_up = _join_columns(wf["w_ffn_gate"]), _join_columns(wf["w_ffn_up"])
    dff = w_gate.shape[1]
    w_down = wf["w_ffn_down"].reshape(dff, d)
    w_a, w_b, w_pp = _join_columns(wf["w_branch_a"]), _join_columns(wf["w_branch_b"]), _join_columns(wf["w_ple_proj"])
    w_out = wf["w_out"].reshape(d, d)
    w_pg = wf["w_ple_gate"].reshape(d, d)
    ta, tb, merged = _mm(
        "branches_merge", [ya, yb], [w_a, w_b], "nn", [_sds((t, d), BF16)] * 3, extras=[ga, gb], separate=True,
        epilogue=lambda tav, tbv, gav, gbv: (tav, tbv, _sigmoid(gav) * tav + _sigmoid(gbv) * tbv), tm=512)
    def residual_norm(acc, xv, g):
        xn = acc + xv
        return xn, xn * lax.rsqrt(jnp.mean(xn * xn, axis=-1, keepdims=True) + RMS_EPS) * g

    x1, h2 = _mm("mix_out", [merged], [w_out], "nn", [_sds((t, d), F32), _sds((t, d), BF16)],
                 extras=[x, small["norm_ffn"]], epilogue=residual_norm)
    gate, up, act = _mm("ffn_gate_up", [h2], [w_gate, w_up], "nn", [_sds((t, dff), BF16)] * 3, separate=True,
                        epilogue=lambda gv, uv: (gv, uv, gv * _sigmoid(gv) * uv), tm=512, tn=dff // 2)
    x2, h3 = _mm("ffn_down", [act], [w_down], "nn", [_sds((t, d), F32), _sds((t, d), BF16)],
                 extras=[x1, small["norm_ple"]], epilogue=residual_norm, tm=512)
    dx3, d_pp, d_gp, d_norm_final, loss_row = _mm(
        "ple_loss", [h3, p], [w_pg, w_pp], "nn", [_sds((t, d), F32), _sds((t, d), BF16), _sds((t, d), BF16)],
        extras=[x2, target, small["norm_final"].reshape(1, d)], separate=True, epilogue=_ple_and_loss,
        sum_shapes=[_sds((1, d), F32)] * 2, tm=512)

    def through_norm(dh, xv, g, dres):
        dx, d_gain = _rms_norm_bwd(dh, xv, g)
        return dx + dres, dx + dres, d_gain

    stream = [_sds((t, d), F32), _sds((t, d), BF16)]
    gain_sum = [_sds((1, d), F32)]
    g_w_pp, g_w_pg = _mm_tn("g_ple", [p, h3], [d_pp, d_gp])
    dx2, dx2_b, d_norm_ple = _mm("d_h3", [d_gp], [w_pg], "nt", stream, extras=[x2, small["norm_ple"], dx3],
                                 epilogue=through_norm, sum_shapes=gain_sum, tm=512)

    def ffn_bwd(acc, gv, uv):
        s = _sigmoid(gv)
        return acc * uv * (s * (1.0 + gv * (1.0 - s))), acc * (gv * s)

    d_gate, d_up = _mm("d_act", [dx2_b], [w_down], "nt", [_sds((t, dff), BF16)] * 2, extras=[gate, up],
                       epilogue=ffn_bwd, tm=512, tn=dff // 2)
    g_w_down, = _mm_tn("g_ffn_down", [act], [dx2_b], tmm=512)
    g_w_gate, g_w_up = _mm_tn("g_ffn_gate_up", [h2], [d_gate, d_up], n_blocks=2)
    dx1, dx1_b, d_norm_ffn = _mm(
        "d_h2", [d_gate, d_up], [w_gate, w_up], "nt", stream, extras=[x1, small["norm_ffn"], dx2],
        epilogue=through_norm, sum_shapes=gain_sum, tm=512)

    def merge_bwd(acc, tav, tbv, gav, gbv):
        sa, sb = _sigmoid(gav), _sigmoid(gbv)
        return acc * sa, acc * sb, acc * tav * sa * (1.0 - sa), acc * tbv * sb * (1.0 - sb)

    d_ta, d_tb, d_ga, d_gb = _mm("d_merged", [dx1_b], [w_out], "nt", [_sds((t, d), BF16)] * 4,
                                 extras=[ta, tb, ga, gb], epilogue=merge_bwd, tm=512)
    g_w_out, g_w_a, g_w_b = _mm_tn("g_mixer", [merged, ya, yb], [dx1_b, d_ta, d_tb])
    d_ya, d_yb = _mm("d_branches", [d_ta, d_tb], [w_a, w_b], "nt", [_sds((t, dp), F32), _sds((t, dp), BF16)],
                     separate=True)
    d_u, g_w_pool, d_pool_scale = _pool_bwd(d_ya, pooled, w_pool_b, small["pool_scale"])
    big = {
        "w_branch_a": g_w_a, "w_branch_b": g_w_b, "w_out": g_w_out.reshape(wf["w_out"].shape),
        "w_ffn_gate": g_w_gate, "w_ffn_up": g_w_up, "w_ffn_down": g_w_down.reshape(wf["w_ffn_down"].shape),
        "w_ple_gate": g_w_pg.reshape(wf["w_ple_gate"].shape), "w_ple_proj": g_w_pp,
    }
    rider = exchange_early(big) if exchange_early else _NoRider()
    (d_q, d_k, d_v), early = _attn_bwd(q, 0, kv, 0, n_pairs, d_yb, n_pairs, rider=rider)
    d_proj = [(d_u, d_q), (d_k, d_v), d_ga, d_gb]
    big["w_in"], = _mm_tn("g_w_in", [h1], d_proj, tmm=512, stacked=True)
    rider = exchange_last(big["w_in"]) if exchange_last else _NoRider()
    res = _mm(
        "d_h1", d_proj, [w_in[j] for j in range(N_CHIPS)], "nt", [_sds((t, d), F32)],
        extras=[x, small["norm_mix"], dx1], epilogue=lambda dh, xv, g, dres: through_norm(dh, xv, g, dres)[1:],
        sum_shapes=gain_sum, tm=512, rider=rider)
    (grad_x, d_norm_mix), last = res if rider.operands else (res, ())
    small_g = {"norm_mix": d_norm_mix, "w_pool": g_w_pool, "pool_scale": d_pool_scale, "norm_ffn": d_norm_ffn,
               "norm_ple": d_norm_ple, "norm_final": d_norm_final}
    return grad_x, big, small_g, loss_row, early, last


def _split2(res, n):
    return res[:n], res[n:]


def _pack_small(small_g, loss_row):
    parts, layout = [], []
    for name in SMALL + ("loss",):
        v = (loss_row if name == "loss" else small_g[name]).reshape(-1, LANES)
        pad = (-v.shape[0]) % 8
        if pad:
            v = jnp.concatenate([v, jnp.zeros((pad, LANES), F32)], axis=0)
        layout.append((name, sum(q.shape[0] for q in parts), v.shape[0]))
        parts.append(v)
    return jnp.concatenate(parts, axis=0), layout


def kernel(x, p, norm_mix, w_in, w_pool, pool_scale, w_branch_a, w_branch_b, w_out, norm_ffn, w_ffn_gate, w_ffn_up, w_ffn_down, norm_ple, w_ple_gate, w_ple_proj, norm_final, loss_target, m_norm_mix, m_w_in, m_w_pool, m_pool_scale, m_w_branch_a, m_w_branch_b, m_w_out, m_norm_ffn, m_w_ffn_gate, m_w_ffn_up, m_w_ffn_down, m_norm_ple, m_w_ple_gate, m_w_ple_proj, m_norm_final, v_norm_mix, v_w_in, v_w_pool, v_pool_scale, v_w_branch_a, v_w_branch_b, v_w_out, v_norm_ffn, v_w_ffn_gate, v_w_ffn_up, v_w_ffn_down, v_norm_ple, v_w_ple_gate, v_w_ple_proj, v_norm_final):
    given = dict(locals())
    names = BIG + SMALL
    order = ("norm_mix", "w_in", "w_pool", "pool_scale", "w_branch_a", "w_branch_b", "w_out", "norm_ffn", "w_ffn_gate",
             "w_ffn_up", "w_ffn_down", "norm_ple", "w_ple_gate", "w_ple_proj", "norm_final")
    t, d = x.shape[1], x.shape[2]
    shard = {n: given[n][0] for n in BIG}
    small = {"norm_mix": norm_mix, "w_pool": w_pool[0], "pool_scale": pool_scale, "norm_ffn": norm_ffn,
             "norm_ple": norm_ple, "norm_final": norm_final}

    as_bf16 = {n: shard[n].astype(BF16) for n in BIG}

    place = jnp.stack([lax.axis_index("c"), 2 * lax.axis_index("x") + lax.axis_index("y")]).astype(jnp.int32)
    pair_sums = {}

    def exchange_early(ready):
        theirs = _pair_exchange("pair_exchange_early", [ready[n] for n in LATE])
        for n, other in zip(LATE, theirs):
            pair_sums[n] = _pair_sum(f"pair_sum_{n}", place, ready[n], other)
        return _ChipExchange([pair_sums[n] for n in LATE])

    def exchange_last(g_w_in):
        theirs, = _pair_exchange("pair_exchange_w_in", [g_w_in])
        pair_sums["w_in"] = _pair_sum("pair_sum_w_in", place, g_w_in, theirs)
        return _ChipExchange([pair_sums["w_in"]])

    grad_x, big_g, small_g, loss_row, early, last = _local_step(
        x.reshape(t, d), p.reshape(t, p.shape[-1]), loss_target.reshape(t, d), {}, small,
        gather_first=_WeightGather([as_bf16["w_in"]]),
        gather_late=_WeightGather([as_bf16[n] for n in LATE]), exchange_early=exchange_early,
        exchange_last=exchange_last)
    landed = dict(zip(LATE + ("w_in",), tuple(early) + tuple(last)))
    halves = [_sum_chips(f"chip_sum_{n}", place, pair_sums[n], landed[n]) for n in BIG]
    grads = dict(zip(BIG, _pair_share(halves)))

    packed, layout = _pack_small(small_g, loss_row)
    reduced = _all_reduce_small(packed)
    for name, start, rows in layout:
        if name == "loss":
            loss = jnp.sum(reduced[start:start + rows])
        else:
            n_el = small[name].size
            grads[name] = reduced[start:start + rows].reshape(-1)[:n_el]

    deltas, new_m, new_v = {}, {}, {}
    for n in order:
        w = shard[n] if n in BIG else small[n]
        shape2 = w.shape if w.ndim == 2 else ((1, w.shape[0]) if w.ndim == 1 else (w.shape[0] * w.shape[1], w.shape[2]))
        g2 = grads[n].reshape(shape2)
        dl, mn, vn = _adamw(f"adamw_{n}", w.reshape(shape2), g2, given["m_" + n].reshape(shape2),
                            given["v_" + n].reshape(shape2))
        full = given[n].shape
        grads[n], deltas[n], new_m[n], new_v[n] = g2.reshape(full), dl.reshape(full), mn.reshape(full), vn.reshape(full)

    return (loss, grad_x.reshape(x.shape), *[grads[n] for n in order], *[deltas[n] for n in order],
            *[new_m[n] for n in order], *[new_v[n] for n in order])
```

```python
import functools
import math

import jax
import jax.numpy as jnp
from jax import lax
from jax.experimental import pallas as pl
from jax.experimental.pallas import tpu as pltpu

F32 = jnp.float32
BF16 = jnp.bfloat16
MESH = pl.DeviceIdType.MESH

RMS_EPS = 1e-6
POOL_WINDOWS = (2, 4, 8, 16)
POOL_HALO = 16
HEAD_DIM = 64
LANES = 128
ATT_BLOCK = 256
ATT_CHAINS = 2
ATT_CHUNK = 256
ATT_SLAB = 256
ATT_SCALE = 1.0 / math.sqrt(HEAD_DIM)
LOG2_E = 1.4426950408889634
ATT_EXIT_BELOW = -150.5
ADAM_LR, ADAM_B1, ADAM_B2, ADAM_EPS, ADAM_WD, ADAM_STEP = 0.001, 0.9, 0.999, 1e-08, 0.01, 10
V7X_VMEM_LIMIT_BYTES = 56 * 1024 * 1024
N_CHIPS = 4
N_DEV = 8


def _params(*semantics):
    return pltpu.CompilerParams(dimension_semantics=semantics, vmem_limit_bytes=V7X_VMEM_LIMIT_BYTES)


def _sigmoid(z):
    return 1.0 / (1.0 + jnp.exp(-z))


def _tiled_spec(shape, tm, tn, n_total, at):
    rows, width = shape
    if rows == 1:
        if width == n_total:
            return pl.BlockSpec((1, tn), at(lambda i, j: (0, j)))
        return pl.BlockSpec((1, width), at(lambda i, j: (0, 0)))
    if width == n_total:
        return pl.BlockSpec((tm, tn), at(lambda i, j: (i, j)))
    assert tn == n_total, "an operand narrower than the output needs whole output rows per tile"
    return pl.BlockSpec((tm, width), at(lambda i, j: (i, 0)))


def _column_pieces(operands):
    pieces = [tuple(a) if isinstance(a, (tuple, list)) else (a,) for a in operands]
    return [p for ps in pieces for p in ps], [len(ps) for ps in pieces]


def _load_bf16(refs, counts):
    tiles, k = [], 0
    for n in counts:
        parts = [r[...] for r in refs[k:k + n]]
        parts = [t if t.dtype == BF16 else t.astype(BF16) for t in parts]
        tiles.append(parts[0] if n == 1 else jnp.concatenate(parts, axis=1))
        k += n
    return tiles


def _mm(name, a_list, b_list, mode, out_shapes, epilogue=None, extras=(), tm=1024, tn=None, separate=False,
        sum_shapes=(), rider=None):
    flat_a, counts = _column_pieces(a_list)
    m_total = flat_a[0].shape[0]
    n_total = b_list[0].shape[1] if mode == "nn" else b_list[0].shape[0]
    tn = n_total if tn is None else tn
    tm = min(tm, m_total)
    assert m_total % tm == 0 and n_total % tn == 0 and (not sum_shapes or tn == n_total)
    n_a, n_b, n_extra, n_out = len(counts), len(b_list), len(extras), len(out_shapes)
    assert n_a in (1, n_b)
    dims = (((1,), (0,)), ((), ())) if mode == "nn" else (((1,), (1,)), ((), ()))
    rider = rider or _NoRider()
    grid = (n_total // tn, m_total // tm)

    def at(index):
        return lambda j, i: index(i, j)

    def body(*refs):
        ins, o_refs, _, riding = rider.split(refs, len(flat_a) + n_b + n_extra, n_out + len(sum_shapes))
        a_refs, b_refs, e_refs = ins[:len(flat_a)], ins[len(flat_a):len(flat_a) + n_b], ins[len(flat_a) + n_b:]
        at_first = (pl.program_id(0) == 0) & (pl.program_id(1) == 0)
        at_last = (pl.program_id(0) == grid[0] - 1) & (pl.program_id(1) == grid[1] - 1)
        top, bottom = rider.at_steps(riding, at_first, at_first, at_last)
        top()
        lefts = _load_bf16(a_refs, counts)
        products = [lax.dot_general(lefts[s % n_a], b_refs[s][...], dims, preferred_element_type=F32)
                    for s in range(n_b)]
        if not separate:
            products = [functools.reduce(lambda p, r: p + r, products)]
        extra_tiles = [e[...].astype(F32) for e in e_refs]
        outs = products if epilogue is None else epilogue(*products, *extra_tiles)
        for o_ref, o in zip(o_refs[:n_out], outs[:n_out]):
            o_ref[...] = o.astype(o_ref.dtype)
        if sum_shapes:
            @pl.when(pl.program_id(1) == 0)
            def _():
                for s_ref in o_refs[n_out:]:
                    s_ref[...] = jnp.zeros_like(s_ref)

            for s_ref, s in zip(o_refs[n_out:], outs[n_out:]):
                s_ref[...] += s
        bottom()

    once = dict(pipeline_mode=pl.Buffered(1)) if tn == n_total else {}
    in_specs = [pl.BlockSpec((tm, a.shape[1]), at(lambda i, j: (i, 0))) for a in flat_a]
    if mode == "nn":
        in_specs += [pl.BlockSpec((b.shape[0], tn), at(lambda i, j: (0, j)), **once) for b in b_list]
    else:
        in_specs += [pl.BlockSpec((tn, b.shape[1]), at(lambda i, j: (j, 0)), **once) for b in b_list]
    in_specs += [_tiled_spec(e.shape, tm, tn, n_total, at) for e in extras]
    out_specs = [_tiled_spec(o.shape, tm, tn, n_total, at) for o in out_shapes]
    out_specs += [pl.BlockSpec(s.shape, at(lambda i, j: (0, 0))) for s in sum_shapes]
    semantics = ("arbitrary", "arbitrary") if sum_shapes or rider.operands else ("parallel", "parallel")
    res = pl.pallas_call(
        body, name=name, grid=grid, in_specs=in_specs + [ANY] * len(rider.operands),
        out_specs=out_specs + [ANY] * len(rider.out_shapes),
        out_shape=list(out_shapes) + list(sum_shapes) + list(rider.out_shapes), scratch_shapes=list(rider.scratch),
        compiler_params=_params(*semantics),
    )(*flat_a, *b_list, *extras, *rider.operands)
    n_own = len(out_shapes) + len(sum_shapes)
    return res if not rider.operands else (res[:n_own], res[n_own:])


def _mm_tn(name, a_list, b_list, tmm=1024, stacked=False, n_blocks=1):
    flat_b, counts = _column_pieces(b_list)
    n_a = len(a_list)
    m_total = a_list[0].shape[0]
    ks = [a_list[s % n_a].shape[1] for s in range(len(counts))]
    widths = [sum(p.shape[1] for p in flat_b[sum(counts[:s]):sum(counts[:s + 1])]) for s in range(len(counts))]
    tmm = min(tmm, m_total)
    assert m_total % tmm == 0 and (n_blocks == 1 or max(counts) == 1) and all(w % n_blocks == 0 for w in widths)
    n_b = len(counts)
    assert n_a in (1, n_b) and not (stacked and n_a > 1)

    def body(*refs):
        a_refs, b_refs, o_refs = refs[:n_a], refs[n_a:n_a + len(flat_b)], refs[n_a + len(flat_b):]

        @pl.when(pl.program_id(1) == 0)
        def _():
            for o_ref in o_refs:
                o_ref[...] = jnp.zeros_like(o_ref)

        lefts = _load_bf16(a_refs, [1] * n_a)
        for s, bv in enumerate(_load_bf16(b_refs, counts)):
            product = lax.dot_general(lefts[s % n_a], bv, (((0,), (0,)), ((), ())), preferred_element_type=F32)
            if stacked:
                o_refs[0][s] += product
            else:
                o_refs[s][...] += product

    in_specs = [pl.BlockSpec((tmm, a.shape[1]), lambda nb, m: (m, 0)) for a in a_list]
    in_specs += [pl.BlockSpec((tmm, b.shape[1] // n_blocks), lambda nb, m: (m, nb)) for b in flat_b]
    if stacked:
        out_shape = [jax.ShapeDtypeStruct((n_b, ks[0], widths[0]), F32)]
        out_specs = [pl.BlockSpec((n_b, ks[0], widths[0] // n_blocks), lambda nb, m: (0, 0, nb))]
    else:
        out_shape = [jax.ShapeDtypeStruct((k, w), F32) for k, w in zip(ks, widths)]
        out_specs = [pl.BlockSpec((k, w // n_blocks), lambda nb, m: (0, nb)) for k, w in zip(ks, widths)]
    return pl.pallas_call(
        body, name=name, grid=(n_blocks, m_total // tmm), in_specs=in_specs, out_specs=out_specs, out_shape=out_shape,
        compiler_params=_params("arbitrary", "arbitrary"),
    )(*a_list, *flat_b)


def _rows(name, fn, ins, tile_outs, sum_outs=(), tr=512, rider=None):
    t_total = max(a.shape[0] for a in ins)
    tr = min(tr, t_total)
    assert t_total % tr == 0
    n_in, n_tile = len(ins), len(tile_outs)
    rider = rider or _NoRider()
    n_steps = t_total // tr

    def body(*refs):
        own_ins, own_outs, _, riding = rider.split(refs, n_in, n_tile + len(sum_outs))
        step = pl.program_id(0)
        top, bottom = rider.at_steps(riding, step == 0, step == n_steps - 1, step == n_steps - 1)
        top()
        refs = tuple(own_ins) + tuple(own_outs)
        outs = fn(*[r[...].astype(F32) for r in refs[:n_in]])
        for o_ref, o in zip(refs[n_in:n_in + n_tile], outs[:n_tile]):
            o_ref[...] = o.astype(o_ref.dtype)
        if sum_outs:
            @pl.when(pl.program_id(0) == 0)
            def _():
                for s_ref in refs[n_in + n_tile:]:
                    s_ref[...] = jnp.zeros_like(s_ref)

            for s_ref, s in zip(refs[n_in + n_tile:], outs[n_tile:]):
                s_ref[...] += s
        bottom()

    def spec(shape):
        if shape[0] == 1:
            return pl.BlockSpec(shape, lambda i: (0, 0))
        return pl.BlockSpec((tr, shape[1]), lambda i: (i, 0))

    return pl.pallas_call(
        body, name=name, grid=(n_steps,), in_specs=[spec(a.shape) for a in ins] + [ANY] * len(rider.operands),
        out_specs=[spec(o.shape) for o in tile_outs] + [spec(s.shape) for s in sum_outs] + [ANY] * len(rider.out_shapes),
        out_shape=list(tile_outs) + list(sum_outs) + list(rider.out_shapes), scratch_shapes=list(rider.scratch),
        compiler_params=_params("arbitrary" if sum_outs or rider.operands else "parallel"),
    )(*ins, *rider.operands)


def _norm_fwd(name, x, gain, rider=None):
    def fn(xv, g):
        inv = lax.rsqrt(jnp.mean(xv * xv, axis=-1, keepdims=True) + RMS_EPS)
        return (xv * inv * g,)

    res = _rows(name, fn, [x, gain], [jax.ShapeDtypeStruct(x.shape, BF16)], rider=rider)
    return res[0], res[1:]


def _rms_norm_bwd(dh, xv, g):
    inv = lax.rsqrt(jnp.mean(xv * xv, axis=-1, keepdims=True) + RMS_EPS)
    xn = xv * inv
    dxn = dh * g
    return inv * (dxn - xn * jnp.mean(dxn * xn, axis=-1, keepdims=True)), jnp.sum(dh * xn, axis=0, keepdims=True)


def _ple_and_loss(gv, pv, x2v, tv, g):
    d = x2v.shape[1]
    s = _sigmoid(gv)
    xv = x2v + s * pv
    inv = lax.rsqrt(jnp.mean(xv * xv, axis=-1, keepdims=True) + RMS_EPS)
    err = xv * inv * g - tv
    dx, d_gain = _rms_norm_bwd(err * (1.0 / d), xv, g)
    return dx, dx * s, dx * pv * s * (1.0 - s), d_gain, (0.5 / d) * jnp.sum(err * err, axis=0, keepdims=True)


def _window_counts(t_pos, w):
    return jnp.minimum(t_pos + 1, w).astype(F32)


def _pool_fwd(u, w_pool, scale, tr=512):
    t_total, width = u.shape
    tr = min(tr, t_total)
    n_groups = len(POOL_WINDOWS)
    gdim = width // n_groups
    ext = tr + POOL_HALO

    def body(u_ref, halo_ref, w_ref, s_ref, pooled_ref, ya_ref):
        i = pl.program_id(0)
        halo = jnp.where(i == 0, 0.0, halo_ref[...])
        t_pos = i * tr + lax.broadcasted_iota(jnp.int32, (tr, 1), 0)
        for g, w in enumerate(POOL_WINDOWS):
            cols = slice(g * gdim, (g + 1) * gdim)
            main = u_ref[:, cols]
            win = jnp.concatenate([halo[:, cols], main], axis=0)
            span = 1
            while span < w:
                win = win + pltpu.roll(win, span, 0)
                span *= 2
            pooled = win[POOL_HALO:, :] * (1.0 / _window_counts(t_pos, w)) - main
            pooled_b = pooled.astype(BF16)
            pooled_ref[:, cols] = pooled_b
            mixed = jnp.dot(pooled_b, w_ref[g], preferred_element_type=F32)
            ya_ref[:, cols] = (mixed * s_ref[:, cols]).astype(BF16)

    hb = tr // POOL_HALO
    return pl.pallas_call(
        body, name="pool_fwd", grid=(t_total // tr,),
        in_specs=[pl.BlockSpec((tr, width), lambda i: (i, 0)),
                  pl.BlockSpec((POOL_HALO, width), lambda i: (jnp.maximum(i * hb - 1, 0), 0)),
                  pl.BlockSpec((n_groups, gdim, gdim), lambda i: (0, 0, 0)),
                  pl.BlockSpec((1, width), lambda i: (0, 0))],
        out_specs=[pl.BlockSpec((tr, width), lambda i: (i, 0)), pl.BlockSpec((tr, width), lambda i: (i, 0))],
        out_shape=[jax.ShapeDtypeStruct(u.shape, BF16), jax.ShapeDtypeStruct(u.shape, BF16)],
        compiler_params=_params("parallel"),
    )(u, u, w_pool, scale)


def _pool_bwd(dya, pooled, w_pool, scale, tr=512):
    t_total, width = dya.shape
    tr = min(tr, t_total)
    n_groups = len(POOL_WINDOWS)
    gdim = width // n_groups
    ext = tr + POOL_HALO
    n_tiles = t_total // tr

    def body(d_ref, halo_ref, p_ref, w_ref, s_ref, du_ref, dw_ref, ds_ref):
        i = pl.program_id(0)

        @pl.when(i == 0)
        def _():
            dw_ref[...] = jnp.zeros_like(dw_ref)
            ds_ref[...] = jnp.zeros_like(ds_ref)

        halo = jnp.where(i == n_tiles - 1, 0.0, halo_ref[...])
        t_pos = i * tr + lax.broadcasted_iota(jnp.int32, (ext, 1), 0)
        for g, w in enumerate(POOL_WINDOWS):
            cols = slice(g * gdim, (g + 1) * gdim)
            sc = s_ref[:, cols]
            d_main = d_ref[:, cols]
            pooled_b = p_ref[:, cols]
            mixed = jnp.dot(pooled_b, w_ref[g], preferred_element_type=F32)
            ds_ref[:, cols] += jnp.sum(d_main * mixed, axis=0, keepdims=True)
            dmix = (jnp.concatenate([d_main, halo[:, cols]], axis=0) * sc).astype(BF16)
            dw_ref[g] += lax.dot_general(pooled_b, dmix[:tr, :], (((0,), (0,)), ((), ())),
                                         preferred_element_type=F32)
            dpool = lax.dot_general(dmix, w_ref[g], (((1,), (1,)), ((), ())), preferred_element_type=F32)
            win = dpool * (1.0 / _window_counts(t_pos, w))
            span = 1
            while span < w:
                win = win + pltpu.roll(win, ext - span, 0)
                span *= 2
            du_ref[:, cols] = (win[:tr, :] - dpool[:tr, :]).astype(BF16)

    hb = tr // POOL_HALO
    last_halo = t_total // POOL_HALO - 1
    return pl.pallas_call(
        body, name="pool_bwd", grid=(n_tiles,),
        in_specs=[pl.BlockSpec((tr, width), lambda i: (i, 0)),
                  pl.BlockSpec((POOL_HALO, width), lambda i: (jnp.minimum((i + 1) * hb, last_halo), 0)),
                  pl.BlockSpec((tr, width), lambda i: (i, 0)),
                  pl.BlockSpec((n_groups, gdim, gdim), lambda i: (0, 0, 0)),
                  pl.BlockSpec((1, width), lambda i: (0, 0))],
        out_specs=[pl.BlockSpec((tr, width), lambda i: (i, 0)),
                   pl.BlockSpec((n_groups, gdim, gdim), lambda i: (0, 0, 0)),
                   pl.BlockSpec((1, width), lambda i: (0, 0))],
        out_shape=[jax.ShapeDtypeStruct(dya.shape, BF16), jax.ShapeDtypeStruct((n_groups, gdim, gdim), F32),
                   jax.ShapeDtypeStruct((1, width), F32)],
        compiler_params=_params("arbitrary"),
    )(dya, dya, pooled, w_pool, scale)


def _head_masks():
    lane = lax.broadcasted_iota(jnp.int32, (1, LANES), 1)
    return lane < HEAD_DIM


def _stack_heads(tile, first):
    zero = jnp.zeros_like(tile)
    return jnp.concatenate([jnp.where(first, tile, zero), jnp.where(first, zero, tile)], axis=0)


def _causal_mask(t_pos, k_start):
    col = lax.broadcasted_iota(jnp.int32, (1, 2 * ATT_SLAB), 1)
    return k_start + (col & (ATT_SLAB - 1)) < t_pos


def _slab_scores(q, kd, mask):
    z2 = lax.dot_general(q, kd, (((1,), (1,)), ((), ())), preferred_element_type=F32) * LOG2_E
    log_hit = jnp.minimum(z2, 0.0) - jnp.log2(1.0 + jnp.exp2(-jnp.abs(z2)))
    log_fail = log_hit - z2
    return log_hit, (log_fail if mask is None else jnp.where(mask, log_fail, 0.0))


def _weights(log_hit, suffix, mask):
    arg = log_hit + suffix
    return jnp.exp2(arg if mask is None else jnp.where(mask, arg, -1e30))


def _tri(upper):
    r = lax.broadcasted_iota(jnp.int32, (ATT_CHUNK, ATT_CHUNK), 0)
    c = lax.broadcasted_iota(jnp.int32, (ATT_CHUNK, ATT_CHUNK), 1)
    return jnp.where(r > c if upper else r < c, 1.0, 0.0).astype(BF16)


def _tri_spec():
    return pl.BlockSpec((ATT_CHUNK, ATT_CHUNK), lambda h, i: (0, 0), pipeline_mode=pl.Buffered(1))


def _scan_chunk(v, tri):
    return jnp.dot(v.astype(BF16), tri, preferred_element_type=F32)


def _lane_bcast(col):
    return jnp.broadcast_to(col, (col.shape[0], LANES))


def _scan_slab(v, tri, carries, from_right):
    n_chunks = ATT_SLAB // ATT_CHUNK
    edge = 0 if from_right else ATT_CHUNK - 1
    parts, new_carries = [None] * (2 * n_chunks), []
    for head in range(2):
        run = carries[head]
        for c in (reversed(range(n_chunks)) if from_right else range(n_chunks)):
            lo_col = head * ATT_SLAB + c * ATT_CHUNK
            vc = v[:, lo_col:lo_col + ATT_CHUNK]
            sc = _scan_chunk(vc, tri)
            parts[head * n_chunks + c] = sc + jnp.concatenate([run] * (ATT_CHUNK // LANES), axis=1)
            run = run + _lane_bcast(sc[:, edge:edge + 1] + vc[:, edge:edge + 1])
        new_carries.append(run)
    return jnp.concatenate(parts, axis=1), new_carries


def _fold_heads(stacked, first):
    s = stacked.shape[0] // 2
    return jnp.where(first, stacked[:s], stacked[s:])


class _NoRider:
    operands, out_shapes, scratch = (), (), ()

    def split(self, refs, n_base_in, n_base_out):
        n_in, n_out, n_sem = len(self.operands), len(self.out_shapes), len(self.scratch)
        a = n_base_in + n_in
        b = a + n_base_out + n_out
        mine = (refs[n_base_in:a], refs[a + n_base_out:b], refs[b:b + n_sem])
        return refs[:n_base_in], refs[a:a + n_base_out], refs[b + n_sem:], mine

    def start(self, ins, outs, sems):
        pass

    def relay(self, ins, outs, sems):
        pass

    def finish(self, ins, outs, sems):
        pass

    def at_steps(self, refs, first_step, relay_step, last_step):
        if not self.operands:
            return (lambda: None), (lambda: None)

        def top():
            pl.when(first_step)(lambda: self.start(*refs))
            pl.when(relay_step)(lambda: self.relay(*refs))

        return top, lambda: pl.when(last_step)(lambda: self.finish(*refs))


def _attn_fwd(q_src, q_col, kv_src, k_col, v_col, n_pairs=4, rider=_NoRider()):
    t_total = q_src.shape[0]
    blk = ATT_BLOCK
    n_steps = t_total // (ATT_CHAINS * blk)
    assert t_total % ATT_SLAB == 0 and ATT_SLAB == ATT_BLOCK

    def body(*refs):
        (q_ref, k_ref, v_ref, suffix_ref), (o_ref,), _, riding = rider.split(refs, 4, 1)
        h, ii = pl.program_id(0), pl.program_id(1)
        top, bottom = rider.at_steps(riding, (h == 0) & (ii == 0), (h == n_pairs - 1) & (ii == 0),
                                     (h == n_pairs - 1) & (ii == n_steps - 1))
        top()
        first = _head_masks()
        suffix_tri = suffix_ref[...]
        blocks = [ATT_CHAINS * ii + c for c in range(ATT_CHAINS)]
        qs = [q_ref[c * blk:(c + 1) * blk, :] * ATT_SCALE for c in range(ATT_CHAINS)]
        t_pos = [b * blk + lax.broadcasted_iota(jnp.int32, (blk, 1), 0) for b in blocks]

        def one(c, t, chain, on_diagonal):
            _, acc, right_a, right_b = chain
            k_start = pl.multiple_of((blocks[c] - t) * ATT_SLAB, ATT_SLAB)
            kd = _stack_heads(k_ref[pl.ds(k_start, ATT_SLAB), :], first)
            vd = _stack_heads(v_ref[pl.ds(k_start, ATT_SLAB), :], first)
            mask = _causal_mask(t_pos[c], k_start) if on_diagonal else None
            log_hit, log_fail = _slab_scores(qs[c], kd, mask)
            suffix, (right_a, right_b) = _scan_slab(log_fail, suffix_tri, (right_a, right_b), from_right=True)
            a = _weights(log_hit, suffix, mask).astype(BF16)
            acc = acc + jnp.dot(a, vd, preferred_element_type=F32)
            return jnp.max(jnp.maximum(right_a, right_b)), acc, right_a, right_b

        def step(state, on_diagonal):
            t, chains = state
            return t + 1, tuple(one(c, t, chains[c], on_diagonal) for c in range(ATT_CHAINS))

        def more(state):
            t, chains = state
            return (t <= blocks[0]) & (functools.reduce(jnp.maximum, [ch[0] for ch in chains]) > ATT_EXIT_BELOW)

        zero = jnp.zeros((blk, LANES), F32)
        state = step((0, ((jnp.float32(0.0), zero, zero, zero),) * ATT_CHAINS), on_diagonal=True)
        t, chains = lax.while_loop(more, functools.partial(step, on_diagonal=False), state)
        for c in range(ATT_CHAINS):
            chain = chains[c]
            if c:
                _, chain = lax.while_loop(
                    lambda s, c=c: (s[0] <= blocks[c]) & (s[1][0] > ATT_EXIT_BELOW),
                    lambda s, c=c: (s[0] + 1, one(c, s[0], s[1], False)), (t, chain))
            o_ref[c * blk:(c + 1) * blk, :] = chain[1].astype(BF16)
        bottom()

    rows = ATT_CHAINS * blk
    res = pl.pallas_call(
        body, name="attn_fwd", grid=(n_pairs, n_steps),
        in_specs=[pl.BlockSpec((rows, LANES), lambda h, i: (i, q_col + h)),
                  pl.BlockSpec((t_total, LANES), lambda h, i: (0, k_col + h)),
                  pl.BlockSpec((t_total, LANES), lambda h, i: (0, v_col + h)), _tri_spec()] + [ANY] * len(rider.operands),
        out_specs=[pl.BlockSpec((rows, LANES), lambda h, i: (i, h))] + [ANY] * len(rider.out_shapes),
        out_shape=[jax.ShapeDtypeStruct((t_total, n_pairs * LANES), BF16)] + list(rider.out_shapes),
        scratch_shapes=list(rider.scratch),
        compiler_params=_params("arbitrary", "arbitrary"),
    )(q_src, kv_src, kv_src, _tri(upper=True), *rider.operands)
    return res[0], res[1:]


def _attn_bwd(q_src, q_col, kv_src, k_col, v_col, dy, n_pairs=4, rider=_NoRider()):
    t_total = q_src.shape[0]
    blk = ATT_BLOCK
    n_steps = t_total // (ATT_CHAINS * blk)
    n_slabs = t_total // ATT_SLAB
    assert t_total % ATT_SLAB == 0 and ATT_SLAB == ATT_BLOCK

    def body(*refs):
        ins, (dq_ref, dk_ref, dv_ref), (g_s, dk_acc, dv_acc), riding = rider.split(refs, 6, 3)
        q_ref, dy_ref, k_ref, v_ref, suffix_ref, prefix_ref = ins
        h, ii = pl.program_id(0), pl.program_id(1)
        top, bottom = rider.at_steps(riding, (h == 0) & (ii == 0), (h == n_pairs - 1) & (ii == 0),
                                     (h == n_pairs - 1) & (ii == n_steps - 1))
        top()

        @pl.when(ii == 0)
        def _():
            dk_acc[...] = jnp.zeros_like(dk_acc)
            dv_acc[...] = jnp.zeros_like(dv_acc)

        first = _head_masks()
        suffix_tri = suffix_ref[...]
        prefix_tri = prefix_ref[...]
        blocks = [ATT_CHAINS * ii + c for c in range(ATT_CHAINS)]
        rows = [slice(c * blk, (c + 1) * blk) for c in range(ATT_CHAINS)]
        qs = [q_ref[r, :] * ATT_SCALE for r in rows]
        dys = [dy_ref[r, :] for r in rows]
        t_pos = [b * blk + lax.broadcasted_iota(jnp.int32, (blk, 1), 0) for b in blocks]

        def one1(c, t, chain, on_diagonal):
            _, right_a, right_b = chain
            slab = blocks[c] - t
            k_start = pl.multiple_of(slab * ATT_SLAB, ATT_SLAB)
            kd = _stack_heads(k_ref[pl.ds(k_start, ATT_SLAB), :], first)
            vd = _stack_heads(v_ref[pl.ds(k_start, ATT_SLAB), :], first)
            mask = _causal_mask(t_pos[c], k_start) if on_diagonal else None
            log_hit, log_fail = _slab_scores(qs[c], kd, mask)
            suffix, (right_a, right_b) = _scan_slab(log_fail, suffix_tri, (right_a, right_b), from_right=True)
            a = _weights(log_hit, suffix, mask)
            da = lax.dot_general(dys[c], vd, (((1,), (1,)), ((), ())), preferred_element_type=F32)
            g_s[c, slab] = (da * a).astype(BF16)
            dv_acc[pl.ds(k_start, ATT_SLAB), :] += _fold_heads(lax.dot_general(
                a.astype(BF16), dys[c], (((0,), (0,)), ((), ())), preferred_element_type=F32), first)
            return jnp.max(jnp.maximum(right_a, right_b)), right_a, right_b

        def step1(state, on_diagonal):
            t, chains = state
            return t + 1, tuple(one1(c, t, chains[c], on_diagonal) for c in range(ATT_CHAINS))

        def more(state):
            t, chains = state
            return (t <= blocks[0]) & (functools.reduce(jnp.maximum, [ch[0] for ch in chains]) > ATT_EXIT_BELOW)

        zero = jnp.zeros((blk, LANES), F32)
        state = step1((0, ((jnp.float32(0.0), zero, zero),) * ATT_CHAINS), on_diagonal=True)
        joint, chains = lax.while_loop(more, functools.partial(step1, on_diagonal=False), state)
        done = [joint]
        for c in range(1, ATT_CHAINS):
            done.append(lax.while_loop(
                lambda s, c=c: (s[0] <= blocks[c]) & (s[1][0] > ATT_EXIT_BELOW),
                lambda s, c=c: (s[0] + 1, one1(c, s[0], s[1], False)), (joint, chains[c]))[0])

        def one2(c, t, carry, on_diagonal):
            dq, left_a, left_b = carry
            slab = blocks[c] - t
            k_start = pl.multiple_of(slab * ATT_SLAB, ATT_SLAB)
            kd = _stack_heads(k_ref[pl.ds(k_start, ATT_SLAB), :], first)
            g = g_s[c, slab]
            z2 = lax.dot_general(qs[c], kd, (((1,), (1,)), ((), ())), preferred_element_type=F32) * LOG2_E
            sig = 1.0 / (1.0 + jnp.exp2(-z2))
            prefix, (left_a, left_b) = _scan_slab(g, prefix_tri, (left_a, left_b), from_right=False)
            dz = g * (1.0 - sig) - sig * prefix
            if on_diagonal:
                dz = jnp.where(_causal_mask(t_pos[c], k_start), dz, 0.0)
            dz = dz.astype(BF16)
            dq = dq + jnp.dot(dz, kd, preferred_element_type=F32)
            dk_acc[pl.ds(k_start, ATT_SLAB), :] += _fold_heads(lax.dot_general(
                dz, qs[c], (((0,), (0,)), ((), ())), preferred_element_type=F32), first)
            return dq, left_a, left_b

        carries = [(zero, zero, zero)]
        for c in range(1, ATT_CHAINS):
            carries.append(lax.fori_loop(
                0, done[c] - joint, lambda n, carry, c=c: one2(c, done[c] - 1 - n, carry, False), (zero, zero, zero)))
        carries = lax.fori_loop(
            0, joint - 1,
            lambda n, cs: tuple(one2(c, joint - 1 - n, cs[c], False) for c in range(ATT_CHAINS)), tuple(carries))
        for c in range(ATT_CHAINS):
            dq_ref[rows[c], :] = (one2(c, 0, carries[c], True)[0] * ATT_SCALE).astype(BF16)

        @pl.when(ii == n_steps - 1)
        def _():
            dk_ref[...] = dk_acc[...].astype(BF16)
            dv_ref[...] = dv_acc[...].astype(BF16)

        bottom()

    out = jax.ShapeDtypeStruct((t_total, n_pairs * LANES), BF16)
    n_rows = ATT_CHAINS * blk
    whole = dict(pipeline_mode=pl.Buffered(1))
    res = pl.pallas_call(
        body, name="attn_bwd", grid=(n_pairs, n_steps),
        in_specs=[pl.BlockSpec((n_rows, LANES), lambda h, i: (i, q_col + h)),
                  pl.BlockSpec((n_rows, LANES), lambda h, i: (i, h)),
                  pl.BlockSpec((t_total, LANES), lambda h, i: (0, k_col + h), **whole),
                  pl.BlockSpec((t_total, LANES), lambda h, i: (0, v_col + h), **whole), _tri_spec(), _tri_spec()]
        + [ANY] * len(rider.operands),
        out_specs=[pl.BlockSpec((n_rows, LANES), lambda h, i: (i, h)),
                   pl.BlockSpec((t_total, LANES), lambda h, i: (0, h)),
                   pl.BlockSpec((t_total, LANES), lambda h, i: (0, h))] + [ANY] * len(rider.out_shapes),
        out_shape=[out, out, out] + list(rider.out_shapes),
        scratch_shapes=list(rider.scratch) + [pltpu.VMEM((ATT_CHAINS, n_slabs, blk, 2 * ATT_SLAB), BF16),
                                              pltpu.VMEM((t_total, LANES), F32), pltpu.VMEM((t_total, LANES), F32)],
        compiler_params=_params("arbitrary", "arbitrary"),
    )(q_src, dy, kv_src, kv_src, _tri(upper=True), _tri(upper=False), *rider.operands)
    return res[:3], res[3:]


def _adamw(name, w, g, m, v):
    def fn(wv, gv, mv, vv):
        mn = ADAM_B1 * mv + (1.0 - ADAM_B1) * gv
        vn = ADAM_B2 * vv + (1.0 - ADAM_B2) * (gv * gv)
        m_hat = mn / (1.0 - ADAM_B1 ** ADAM_STEP)
        v_hat = vn / (1.0 - ADAM_B2 ** ADAM_STEP)
        return -ADAM_LR * (m_hat / (jnp.sqrt(v_hat) + ADAM_EPS) + ADAM_WD * wv), mn, vn

    rows = w.shape[0]
    tr = _row_tile(rows)
    shp = jax.ShapeDtypeStruct(w.shape, F32)
    if rows == 1:
        def body(w_ref, g_ref, m_ref, v_ref, d_ref, mo_ref, vo_ref):
            d, mn, vn = fn(w_ref[...], g_ref[...], m_ref[...], v_ref[...])
            d_ref[...], mo_ref[...], vo_ref[...] = d, mn, vn

        return pl.pallas_call(body, name=name, out_shape=[shp, shp, shp])(w, g, m, v)
    return _rows(name, fn, [w, g, m, v], [shp, shp, shp], tr=tr)


def _place():
    return lax.axis_index("x"), lax.axis_index("y"), lax.axis_index("c")


def _other_chips(x, y):
    return [(1 - x, y), (x, 1 - y), (1 - x, 1 - y)]


ANY = pl.BlockSpec(memory_space=pl.ANY)


def _remote(src, dst, send_sem, recv_sem, to):
    return pltpu.make_async_remote_copy(src_ref=src, dst_ref=dst, send_sem=send_sem, recv_sem=recv_sem,
                                        device_id=to, device_id_type=MESH)


class _WeightGather(_NoRider):
    def __init__(self, shards):
        n_w = len(shards)
        self.operands = list(shards)
        self.out_shapes = [jax.ShapeDtypeStruct((N_CHIPS,) + s.shape, s.dtype) for s in shards]
        self.scratch = [pltpu.SemaphoreType.DMA((3, n_w))] * 4 + [pltpu.SemaphoreType.DMA((n_w,))] * 2

    def _copies(self, ins, outs, sems):
        send_sems, recv_sems, relay_send, relay_recv, own_send, own_recv = sems
        x, y, c = _place()
        my_chip, sibling = 2 * x + y, (x, y, 1 - c)
        n_w = len(ins)

        def half(w, chip, core):
            h = self.operands[w].shape[0] // 2
            return outs[w].at[chip, pl.ds(core * h, h)]

        own = [_remote(ins[w], outs[w].at[my_chip], own_send.at[w], own_recv.at[w], sibling) for w in range(n_w)]
        sends, landed, relays, relayed = [], [], [], []
        for p, (ox, oy) in enumerate(_other_chips(x, y)):
            for w in range(n_w):
                h = self.operands[w].shape[0] // 2
                sends.append(_remote(ins[w].at[pl.ds(c * h, h)], half(w, my_chip, c), send_sems.at[p, w],
                                     recv_sems.at[p, w], (ox, oy, c)))
                here = half(w, 2 * ox + oy, c)
                landed.append(_remote(here, here, send_sems.at[p, w], recv_sems.at[p, w], (ox, oy, c)))
                relays.append(_remote(here, here, relay_send.at[p, w], relay_recv.at[p, w], sibling))
                there = half(w, 2 * ox + oy, 1 - c)
                relayed.append(_remote(there, there, relay_send.at[p, w], relay_recv.at[p, w], sibling))
        return own, sends, landed, relays, relayed

    def start(self, ins, outs, sems):
        own, sends, _, _, _ = self._copies(ins, outs, sems)
        for cp in own + sends:
            cp.start()

    def relay(self, ins, outs, sems):
        _, _, landed, relays, _ = self._copies(ins, outs, sems)
        for arrival, cp in zip(landed, relays):
            arrival.wait_recv()
            cp.start()

    def finish(self, ins, outs, sems):
        own, sends, _, relays, relayed = self._copies(ins, outs, sems)
        for arrival in relayed:
            arrival.wait_recv()
        for cp in sends + relays:
            cp.wait_send()
        for cp in own:
            cp.wait()


class _ChipExchange(_NoRider):
    def __init__(self, pair_sums):
        n_w = len(pair_sums)
        self.operands = list(pair_sums)
        self.out_shapes = [jax.ShapeDtypeStruct((3,) + s.shape[1:], s.dtype) for s in pair_sums]
        self.scratch = [pltpu.SemaphoreType.DMA((3, n_w))] * 2

    def _copies(self, ins, outs, sems):
        send_sems, recv_sems = sems
        x, y, c = _place()
        return [_remote(ins[w].at[2 * ox + oy], outs[w].at[p], send_sems.at[p, w], recv_sems.at[p, w], (ox, oy, c))
                for p, (ox, oy) in enumerate(_other_chips(x, y)) for w in range(len(ins))]

    def start(self, ins, outs, sems):
        for cp in self._copies(ins, outs, sems):
            cp.start()

    def finish(self, ins, outs, sems):
        for cp in self._copies(ins, outs, sems):
            cp.wait()


def _pair_exchange(name, grads):
    n_w = len(grads)

    def halves(w):
        return grads[w].shape[-2] // 2

    def body(*refs):
        ins, theirs = refs[:n_w], refs[n_w:2 * n_w]
        send_sems, recv_sems = refs[2 * n_w:]
        x, y, c = _place()
        sends = []
        for w in range(n_w):
            rows = pl.ds((1 - c) * halves(w), halves(w))
            src = ins[w].at[:, rows, :] if grads[w].ndim == 3 else ins[w].at[rows, :]
            sends.append(_remote(src, theirs[w], send_sems.at[w], recv_sems.at[w], (x, y, 1 - c)))
        for cp in sends:
            cp.start()
        for cp in sends:
            cp.wait()

    return pl.pallas_call(
        body, name=name, in_specs=[ANY] * n_w, out_specs=[ANY] * n_w,
        out_shape=[jax.ShapeDtypeStruct(g.shape[:-2] + (halves(w), g.shape[-1]), F32) for w, g in enumerate(grads)],
        scratch_shapes=[pltpu.SemaphoreType.DMA((n_w,)), pltpu.SemaphoreType.DMA((n_w,))],
    )(*grads)


def _pair_share(shards):
    n_w = len(shards)

    def body(*refs):
        ins, outs = refs[:n_w], refs[n_w:2 * n_w]
        send_sems, recv_sems = refs[2 * n_w:]
        x, y, c = _place()
        sends = []
        for w in range(n_w):
            h = shards[w].shape[0] // 2
            mine = outs[w].at[pl.ds(c * h, h)]
            sends.append(pltpu.make_async_remote_copy(
                src_ref=mine, dst_ref=mine, send_sem=send_sems.at[w], recv_sem=recv_sems.at[w],
                device_id=(x, y, 1 - c), device_id_type=MESH))
        for cp in sends:
            cp.start()
        for w in range(n_w):
            h = shards[w].shape[0] // 2
            theirs = outs[w].at[pl.ds((1 - c) * h, h)]
            pltpu.make_async_remote_copy(
                src_ref=theirs, dst_ref=theirs, send_sem=send_sems.at[w], recv_sem=recv_sems.at[w],
                device_id=(x, y, 1 - c), device_id_type=MESH).wait_recv()
        for cp in sends:
            cp.wait_send()

    return pl.pallas_call(
        body, name="pair_share", in_specs=[ANY] * n_w, out_specs=[ANY] * n_w,
        out_shape=[jax.ShapeDtypeStruct(s.shape, s.dtype) for s in shards],
        input_output_aliases={w: w for w in range(n_w)},
        scratch_shapes=[pltpu.SemaphoreType.DMA((n_w,)), pltpu.SemaphoreType.DMA((n_w,))],
    )(*shards)


def _all_reduce_small(vec):
    rows = vec.shape[0]

    def body(v_ref, o_ref, slots, send_sems, recv_sems):
        x, y, c = _place()
        me = 4 * x + 2 * y + c
        slots[me] = v_ref[...]
        sends = []
        for k in range(1, N_DEV):
            peer = (x ^ (k >> 2), y ^ ((k >> 1) & 1), c ^ (k & 1))
            sends.append(pltpu.make_async_remote_copy(
                src_ref=v_ref, dst_ref=slots.at[me], send_sem=send_sems.at[k - 1], recv_sem=recv_sems.at[k - 1],
                device_id=peer, device_id_type=MESH))
        for cp in sends:
            cp.start()
        for k in range(1, N_DEV):
            px, py, pc = x ^ (k >> 2), y ^ ((k >> 1) & 1), c ^ (k & 1)
            landed = slots.at[4 * px + 2 * py + pc]
            pltpu.make_async_remote_copy(
                src_ref=landed, dst_ref=landed, send_sem=send_sems.at[k - 1], recv_sem=recv_sems.at[k - 1],
                device_id=(px, py, pc), device_id_type=MESH).wait_recv()
        for cp in sends:
            cp.wait_send()
        total = slots[0]
        for d in range(1, N_DEV):
            total = total + slots[d]
        o_ref[...] = total

    vm = pl.BlockSpec(memory_space=pltpu.VMEM)
    return pl.pallas_call(
        body, name="all_reduce_small", in_specs=[vm], out_specs=vm, out_shape=jax.ShapeDtypeStruct(vec.shape, F32),
        scratch_shapes=[pltpu.VMEM((N_DEV, rows, LANES), F32), pltpu.SemaphoreType.DMA((N_DEV - 1,)),
                        pltpu.SemaphoreType.DMA((N_DEV - 1,))],
    )(vec)


def _row_tile(rows):
    fits = [tr for tr in range(16, min(rows, 512) + 1, 16) if rows % tr == 0]
    return max(fits) if fits else rows


def _pair_sum(name, place, grad, theirs):
    if grad.ndim == 2:
        return _pair_sum_joined(name, place, grad, theirs)
    n, r, c = grad.shape
    half = r // 2
    tr = _row_tile(half)
    nb = half // tr

    def body(place_ref, g_ref, t_ref, o_ref):
        o_ref[...] = (g_ref[...] + t_ref[...]).astype(BF16)

    return pl.pallas_call(
        body, name=name, out_shape=jax.ShapeDtypeStruct((n, half, c), BF16),
        grid_spec=pltpu.PrefetchScalarGridSpec(
            num_scalar_prefetch=1, grid=(n, nb),
            in_specs=[pl.BlockSpec((1, tr, c), lambda j, i, pr: (j, pr[0] * nb + i, 0)),
                      pl.BlockSpec((1, tr, c), lambda j, i, pr: (j, i, 0))],
            out_specs=pl.BlockSpec((1, tr, c), lambda j, i, pr: (j, i, 0))),
        compiler_params=_params("parallel", "parallel"),
    )(place, grad, theirs)


def _pair_sum_joined(name, place, grad, theirs):
    r, wide = grad.shape
    half, c = r // 2, wide // N_CHIPS
    tr = _row_tile(half)
    nb = half // tr

    def body(place_ref, g_ref, t_ref, o_ref):
        for j in range(N_CHIPS):
            cols = slice(j * c, (j + 1) * c)
            o_ref[j] = (g_ref[:, cols] + t_ref[:, cols]).astype(BF16)

    return pl.pallas_call(
        body, name=name, out_shape=jax.ShapeDtypeStruct((N_CHIPS, half, c), BF16),
        grid_spec=pltpu.PrefetchScalarGridSpec(
            num_scalar_prefetch=1, grid=(nb,),
            in_specs=[pl.BlockSpec((tr, wide), lambda i, pr: (pr[0] * nb + i, 0)),
                      pl.BlockSpec((tr, wide), lambda i, pr: (i, 0))],
            out_specs=pl.BlockSpec((N_CHIPS, tr, c), lambda i, pr: (0, i, 0))),
        compiler_params=_params("parallel"),
    )(place, grad, theirs)


def _sum_chips(name, place, pair_sums, landed):
    _, half, c = pair_sums.shape
    tr = _row_tile(half)
    nb = half // tr

    def body(place_ref, s_ref, q_ref, o_ref):
        total = s_ref[0].astype(F32)
        for p in range(3):
            total = total + q_ref[p].astype(F32)
        o_ref[...] = total

    return pl.pallas_call(
        body, name=name, out_shape=jax.ShapeDtypeStruct((2 * half, c), F32),
        grid_spec=pltpu.PrefetchScalarGridSpec(
            num_scalar_prefetch=1, grid=(nb,),
            in_specs=[pl.BlockSpec((1, tr, c), lambda i, pr: (pr[1], i, 0)),
                      pl.BlockSpec((3, tr, c), lambda i, pr: (0, i, 0))],
            out_specs=pl.BlockSpec((tr, c), lambda i, pr: (pr[0] * nb + i, 0))),
        compiler_params=_params("parallel"),
    )(place, pair_sums, landed)


BIG = ("w_in", "w_branch_a", "w_branch_b", "w_out", "w_ffn_gate", "w_ffn_up", "w_ffn_down", "w_ple_gate", "w_ple_proj")
LATE = BIG[1:]
COLUMN_SHARDED = ("w_in", "w_branch_a", "w_branch_b", "w_ffn_gate", "w_ffn_up", "w_ple_proj")
SMALL = ("norm_mix", "w_pool", "pool_scale", "norm_ffn", "norm_ple", "norm_final")


def _join_columns(w4):
    return jnp.concatenate([w4[j] for j in range(N_CHIPS)], axis=1)


def _sds(shape, dtype):
    return jax.ShapeDtypeStruct(shape, dtype)


def _local_step(x, p, target, wf, small, gather_first=None, gather_late=None, exchange_early=None,
                exchange_last=None):
    t, d = x.shape
    w_pool_b = small["w_pool"].astype(BF16)
    dp = w_pool_b.shape[0] * w_pool_b.shape[1]

    h1, first = _norm_fwd("norm_mix", x, small["norm_mix"], rider=gather_first)
    w_in = first[0] if gather_first else wf["w_in"]
    u, q, kv, ga, gb = _mm(
        "proj", [h1], [w_in[j] for j in range(N_CHIPS)], "nn",
        [_sds((t, dp), F32), _sds((t, dp), BF16), _sds((t, d), BF16), _sds((t, d), BF16), _sds((t, d), BF16)],
        separate=True, epilogue=lambda uq, kv_, ga_, gb_: (uq[:, :dp], uq[:, dp:], kv_, ga_, gb_), tm=512)
    pooled, ya = _pool_fwd(u, w_pool_b, small["pool_scale"])
    n_pairs = dp // LANES
    yb, late = _attn_fwd(q, 0, kv, 0, n_pairs, n_pairs, rider=gather_late or _NoRider())
    wf = {**wf, **dict(zip(LATE, late))}
    w_gate, w_up = _join_columns(wf["w_ffn_gate"]), _join_columns(wf["w_ffn_up"])
    dff = w_gate.shape[1]
    w_down = wf["w_ffn_down"].reshape(dff, d)
    w_a, w_b, w_pp = _join_columns(wf["w_branch_a"]), _join_columns(wf["w_branch_b"]), _join_columns(wf["w_ple_proj"])
    w_out = wf["w_out"].reshape(d, d)
    w_pg = wf["w_ple_gate"].reshape(d, d)
    ta, tb, merged = _mm(
        "branches_merge", [ya, yb], [w_a, w_b], "nn", [_sds((t, d), BF16)] * 3, extras=[ga, gb], separate=True,
        epilogue=lambda tav, tbv, gav, gbv: (tav, tbv, _sigmoid(gav) * tav + _sigmoid(gbv) * tbv), tm=512)
    def residual_norm(acc, xv, g):
        xn = acc + xv
        return xn, xn * lax.rsqrt(jnp.mean(xn * xn, axis=-1, keepdims=True) + RMS_EPS) * g

    x1, h2 = _mm("mix_out", [merged], [w_out], "nn", [_sds((t, d), F32), _sds((t, d), BF16)],
                 extras=[x, small["norm_ffn"]], epilogue=residual_norm)
    gate, up, act = _mm("ffn_gate_up", [h2], [w_gate, w_up], "nn", [_sds((t, dff), BF16)] * 3, separate=True,
                        epilogue=lambda gv, uv: (gv, uv, gv * _sigmoid(gv) * uv), tm=512, tn=dff // 2)
    x2, h3 = _mm("ffn_down", [act], [w_down], "nn", [_sds((t, d), F32), _sds((t, d), BF16)],
                 extras=[x1, small["norm_ple"]], epilogue=residual_norm, tm=512)
    dx3, d_pp, d_gp, d_norm_final, loss_row = _mm(
        "ple_loss", [h3, p], [w_pg, w_pp], "nn", [_sds((t, d), F32), _sds((t, d), BF16), _sds((t, d), BF16)],
        extras=[x2, target, small["norm_final"].reshape(1, d)], separate=True, epilogue=_ple_and_loss,
        sum_shapes=[_sds((1, d), F32)] * 2, tm=512)

    def through_norm(dh, xv, g, dres):
        dx, d_gain = _rms_norm_bwd(dh, xv, g)
        return dx + dres, dx + dres, d_gain

    stream = [_sds((t, d), F32), _sds((t, d), BF16)]
    gain_sum = [_sds((1, d), F32)]
    g_w_pp, g_w_pg = _mm_tn("g_ple", [p, h3], [d_pp, d_gp])
    dx2, dx2_b, d_norm_ple = _mm("d_h3", [d_gp], [w_pg], "nt", stream, extras=[x2, small["norm_ple"], dx3],
                                 epilogue=through_norm, sum_shapes=gain_sum, tm=512)

    def ffn_bwd(acc, gv, uv):
        s = _sigmoid(gv)
        return acc * uv * (s * (1.0 + gv * (1.0 - s))), acc * (gv * s)

    d_gate, d_up = _mm("d_act", [dx2_b], [w_down], "nt", [_sds((t, dff), BF16)] * 2, extras=[gate, up],
                       epilogue=ffn_bwd, tm=512, tn=dff // 2)
    g_w_down, = _mm_tn("g_ffn_down", [act], [dx2_b], tmm=512)
    g_w_gate, g_w_up = _mm_tn("g_ffn_gate_up", [h2], [d_gate, d_up], n_blocks=2)
    dx1, dx1_b, d_norm_ffn = _mm(
        "d_h2", [d_gate, d_up], [w_gate, w_up], "nt", stream, extras=[x1, small["norm_ffn"], dx2],
        epilogue=through_norm, sum_shapes=gain_sum, tm=512)

    def merge_bwd(acc, tav, tbv, gav, gbv):
        sa, sb = _sigmoid(gav), _sigmoid(gbv)
        return acc * sa, acc * sb, acc * tav * sa * (1.0 - sa), acc * tbv * sb * (1.0 - sb)

    d_ta, d_tb, d_ga, d_gb = _mm("d_merged", [dx1_b], [w_out], "nt", [_sds((t, d), BF16)] * 4,
                                 extras=[ta, tb, ga, gb], epilogue=merge_bwd, tm=512)
    g_w_out, g_w_a, g_w_b = _mm_tn("g_mixer", [merged, ya, yb], [dx1_b, d_ta, d_tb])
    d_ya, d_yb = _mm("d_branches", [d_ta, d_tb], [w_a, w_b], "nt", [_sds((t, dp), F32), _sds((t, dp), BF16)],
                     separate=True)
    d_u, g_w_pool, d_pool_scale = _pool_bwd(d_ya, pooled, w_pool_b, small["pool_scale"])
    big = {
        "w_branch_a": g_w_a, "w_branch_b": g_w_b, "w_out": g_w_out.reshape(wf["w_out"].shape),
        "w_ffn_gate": g_w_gate, "w_ffn_up": g_w_up, "w_ffn_down": g_w_down.reshape(wf["w_ffn_down"].shape),
        "w_ple_gate": g_w_pg.reshape(wf["w_ple_gate"].shape), "w_ple_proj": g_w_pp,
    }
    rider = exchange_early(big) if exchange_early else _NoRider()
    (d_q, d_k, d_v), early = _attn_bwd(q, 0, kv, 0, n_pairs, d_yb, n_pairs, rider=rider)
    d_proj = [(d_u, d_q), (d_k, d_v), d_ga, d_gb]
    big["w_in"], = _mm_tn("g_w_in", [h1], d_proj, tmm=512, stacked=True)
    rider = exchange_last(big["w_in"]) if exchange_last else _NoRider()
    res = _mm(
        "d_h1", d_proj, [w_in[j] for j in range(N_CHIPS)], "nt", [_sds((t, d), F32)],
        extras=[x, small["norm_mix"], dx1], epilogue=lambda dh, xv, g, dres: through_norm(dh, xv, g, dres)[1:],
        sum_shapes=gain_sum, tm=512, rider=rider)
    (grad_x, d_norm_mix), last = res if rider.operands else (res, ())
    small_g = {"norm_mix": d_norm_mix, "w_pool": g_w_pool, "pool_scale": d_pool_scale, "norm_ffn": d_norm_ffn,
               "norm_ple": d_norm_ple, "norm_final": d_norm_final}
    return grad_x, big, small_g, loss_row, early, last


def _split2(res, n):
    return res[:n], res[n:]


def _pack_small(small_g, loss_row):
    parts, layout = [], []
    for name in SMALL + ("loss",):
        v = (loss_row if name == "loss" else small_g[name]).reshape(-1, LANES)
        pad = (-v.shape[0]) % 8
        if pad:
            v = jnp.concatenate([v, jnp.zeros((pad, LANES), F32)], axis=0)
        layout.append((name, sum(q.shape[0] for q in parts), v.shape[0]))
        parts.append(v)
    return jnp.concatenate(parts, axis=0), layout


def kernel(x, p, norm_mix, w_in, w_pool, pool_scale, w_branch_a, w_branch_b, w_out, norm_ffn, w_ffn_gate, w_ffn_up, w_ffn_down, norm_ple, w_ple_gate, w_ple_proj, norm_final, loss_target, m_norm_mix, m_w_in, m_w_pool, m_pool_scale, m_w_branch_a, m_w_branch_b, m_w_out, m_norm_ffn, m_w_ffn_gate, m_w_ffn_up, m_w_ffn_down, m_norm_ple, m_w_ple_gate, m_w_ple_proj, m_norm_final, v_norm_mix, v_w_in, v_w_pool, v_pool_scale, v_w_branch_a, v_w_branch_b, v_w_out, v_norm_ffn, v_w_ffn_gate, v_w_ffn_up, v_w_ffn_down, v_norm_ple, v_w_ple_gate, v_w_ple_proj, v_norm_final):
    given = dict(locals())
    names = BIG + SMALL
    order = ("norm_mix", "w_in", "w_pool", "pool_scale", "w_branch_a", "w_branch_b", "w_out", "norm_ffn", "w_ffn_gate",
             "w_ffn_up", "w_ffn_down", "norm_ple", "w_ple_gate", "w_ple_proj", "norm_final")
    t, d = x.shape[1], x.shape[2]
    shard = {n: given[n][0] for n in BIG}
    small = {"norm_mix": norm_mix, "w_pool": w_pool[0], "pool_scale": pool_scale, "norm_ffn": norm_ffn,
             "norm_ple": norm_ple, "norm_final": norm_final}

    as_bf16 = {n: shard[n].astype(BF16) for n in BIG}

    place = jnp.stack([lax.axis_index("c"), 2 * lax.axis_index("x") + lax.axis_index("y")]).astype(jnp.int32)
    pair_sums = {}

    def exchange_early(ready):
        theirs = _pair_exchange("pair_exchange_early", [ready[n] for n in LATE])
        for n, other in zip(LATE, theirs):
            pair_sums[n] = _pair_sum(f"pair_sum_{n}", place, ready[n], other)
        return _ChipExchange([pair_sums[n] for n in LATE])

    def exchange_last(g_w_in):
        theirs, = _pair_exchange("pair_exchange_w_in", [g_w_in])
        pair_sums["w_in"] = _pair_sum("pair_sum_w_in", place, g_w_in, theirs)
        return _ChipExchange([pair_sums["w_in"]])

    grad_x, big_g, small_g, loss_row, early, last = _local_step(
        x.reshape(t, d), p.reshape(t, p.shape[-1]), loss_target.reshape(t, d), {}, small,
        gather_first=_WeightGather([as_bf16["w_in"]]),
        gather_late=_WeightGather([as_bf16[n] for n in LATE]), exchange_early=exchange_early,
        exchange_last=exchange_last)
    landed = dict(zip(LATE + ("w_in",), tuple(early) + tuple(last)))
    halves = [_sum_chips(f"chip_sum_{n}", place, pair_sums[n], landed[n]) for n in BIG]
    grads = dict(zip(BIG, _pair_share(halves)))

    packed, layout = _pack_small(small_g, loss_row)
    reduced = _all_reduce_small(packed)
    for name, start, rows in layout:
        if name == "loss":
            loss = jnp.sum(reduced[start:start + rows])
        else:
            n_el = small[name].size
            grads[name] = reduced[start:start + rows].reshape(-1)[:n_el]

    deltas, new_m, new_v = {}, {}, {}
    for n in order:
        w = shard[n] if n in BIG else small[n]
        shape2 = w.shape if w.ndim == 2 else ((1, w.shape[0]) if w.ndim == 1 else (w.shape[0] * w.shape[1], w.shape[2]))
        g2 = grads[n].reshape(shape2)
        dl, mn, vn = _adamw(f"adamw_{n}", w.reshape(shape2), g2, given["m_" + n].reshape(shape2),
                            given["v_" + n].reshape(shape2))
        full = given[n].shape
        grads[n], deltas[n], new_m[n], new_v[n] = g2.reshape(full), dl.reshape(full), mn.reshape(full), vn.reshape(full)

    return (loss, grad_x.reshape(x.shape), *[grads[n] for n in order], *[deltas[n] for n in order],
            *[new_m[n] for n in order], *[new_v[n] for n in order])
```

```python
import functools
import math

import jax
import jax.numpy as jnp
from jax import lax
from jax.experimental import pallas as pl
from jax.experimental.pallas import tpu as pltpu

F32 = jnp.float32
BF16 = jnp.bfloat16
MESH = pl.DeviceIdType.MESH

RMS_EPS = 1e-6
POOL_WINDOWS = (2, 4, 8, 16)
POOL_HALO = 16
HEAD_DIM = 64
LANES = 128
ATT_BLOCK = 256
ATT_CHAINS = 2
ATT_CHUNK = 256
ATT_SLAB = 256
ATT_SCALE = 1.0 / math.sqrt(HEAD_DIM)
LOG2_E = 1.4426950408889634
ATT_EXIT_BELOW = -150.5
ADAM_LR, ADAM_B1, ADAM_B2, ADAM_EPS, ADAM_WD, ADAM_STEP = 0.001, 0.9, 0.999, 1e-08, 0.01, 10
V7X_VMEM_LIMIT_BYTES = 56 * 1024 * 1024
N_CHIPS = 4
N_DEV = 8


def _params(*semantics):
    return pltpu.CompilerParams(dimension_semantics=semantics, vmem_limit_bytes=V7X_VMEM_LIMIT_BYTES)


def _sigmoid(z):
    return 1.0 / (1.0 + jnp.exp(-z))


def _tiled_spec(shape, tm, tn, n_total, at):
    rows, width = shape
    if rows == 1:
        if width == n_total:
            return pl.BlockSpec((1, tn), at(lambda i, j: (0, j)))
        return pl.BlockSpec((1, width), at(lambda i, j: (0, 0)))
    if width == n_total:
        return pl.BlockSpec((tm, tn), at(lambda i, j: (i, j)))
    assert tn == n_total, "an operand narrower than the output needs whole output rows per tile"
    return pl.BlockSpec((tm, width), at(lambda i, j: (i, 0)))


def _column_pieces(operands):
    pieces = [tuple(a) if isinstance(a, (tuple, list)) else (a,) for a in operands]
    return [p for ps in pieces for p in ps], [len(ps) for ps in pieces]


def _load_bf16(refs, counts):
    tiles, k = [], 0
    for n in counts:
        parts = [r[...] for r in refs[k:k + n]]
        parts = [t if t.dtype == BF16 else t.astype(BF16) for t in parts]
        tiles.append(parts[0] if n == 1 else jnp.concatenate(parts, axis=1))
        k += n
    return tiles


def _mm(name, a_list, b_list, mode, out_shapes, epilogue=None, extras=(), tm=1024, tn=None, separate=False,
        sum_shapes=(), rider=None, wholes=()):
    flat_a, counts = _column_pieces(a_list)
    m_total = flat_a[0].shape[0]
    n_total = b_list[0].shape[1] if mode == "nn" else b_list[0].shape[0]
    tn = n_total if tn is None else tn
    tm = min(tm, m_total)
    assert m_total % tm == 0 and n_total % tn == 0 and (not sum_shapes or tn == n_total)
    n_a, n_b, n_extra, n_out = len(counts), len(b_list), len(extras), len(out_shapes)
    assert n_a in (1, n_b)
    dims = (((1,), (0,)), ((), ())) if mode == "nn" else (((1,), (1,)), ((), ()))
    rider = rider or _NoRider()
    grid = (n_total // tn, m_total // tm)

    def at(index):
        return lambda j, i: index(i, j)

    def body(*refs):
        ins, o_refs, _, riding = rider.split(refs, len(flat_a) + n_b + n_extra + len(wholes), n_out + len(sum_shapes))
        a_refs, b_refs = ins[:len(flat_a)], ins[len(flat_a):len(flat_a) + n_b]
        e_refs, w_refs = ins[len(flat_a) + n_b:len(flat_a) + n_b + n_extra], ins[len(flat_a) + n_b + n_extra:]
        at_first = (pl.program_id(0) == 0) & (pl.program_id(1) == 0)
        at_last = (pl.program_id(0) == grid[0] - 1) & (pl.program_id(1) == grid[1] - 1)
        top, bottom = rider.at_steps(riding, at_first, at_first, at_last)
        top()
        lefts = _load_bf16(a_refs, counts)
        products = [lax.dot_general(lefts[s % n_a], b_refs[s][...], dims, preferred_element_type=F32)
                    for s in range(n_b)]
        if not separate:
            products = [functools.reduce(lambda p, r: p + r, products)]
        extra_tiles = [e[...].astype(F32) for e in e_refs]
        outs = products if epilogue is None else epilogue(*products, *extra_tiles, *[w[...] for w in w_refs])
        for o_ref, o in zip(o_refs[:n_out], outs[:n_out]):
            o_ref[...] = o.astype(o_ref.dtype)
        if sum_shapes:
            @pl.when(pl.program_id(1) == 0)
            def _():
                for s_ref in o_refs[n_out:]:
                    s_ref[...] = jnp.zeros_like(s_ref)

            for s_ref, s in zip(o_refs[n_out:], outs[n_out:]):
                s_ref[...] += s
        bottom()

    once = dict(pipeline_mode=pl.Buffered(1)) if tn == n_total else {}
    in_specs = [pl.BlockSpec((tm, a.shape[1]), at(lambda i, j: (i, 0))) for a in flat_a]
    if mode == "nn":
        in_specs += [pl.BlockSpec((b.shape[0], tn), at(lambda i, j: (0, j)), **once) for b in b_list]
    else:
        in_specs += [pl.BlockSpec((tn, b.shape[1]), at(lambda i, j: (j, 0)), **once) for b in b_list]
    in_specs += [_tiled_spec(e.shape, tm, tn, n_total, at) for e in extras]
    in_specs += [pl.BlockSpec(w.shape, lambda j, i: (0, 0), pipeline_mode=pl.Buffered(1)) for w in wholes]
    out_specs = [_tiled_spec(o.shape, tm, tn, n_total, at) for o in out_shapes]
    out_specs += [pl.BlockSpec(s.shape, at(lambda i, j: (0, 0))) for s in sum_shapes]
    semantics = ("arbitrary", "arbitrary") if sum_shapes or rider.operands else ("parallel", "parallel")
    res = pl.pallas_call(
        body, name=name, grid=grid, in_specs=in_specs + [ANY] * len(rider.operands),
        out_specs=out_specs + [ANY] * len(rider.out_shapes),
        out_shape=list(out_shapes) + list(sum_shapes) + list(rider.out_shapes), scratch_shapes=list(rider.scratch),
        compiler_params=_params(*semantics),
    )(*flat_a, *b_list, *extras, *wholes, *rider.operands)
    n_own = len(out_shapes) + len(sum_shapes)
    return res if not rider.operands else (res[:n_own], res[n_own:])


def _mm_tn(name, a_list, b_list, tmm=1024, stacked=False, n_blocks=1):
    flat_b, counts = _column_pieces(b_list)
    n_a = len(a_list)
    m_total = a_list[0].shape[0]
    ks = [a_list[s % n_a].shape[1] for s in range(len(counts))]
    widths = [sum(p.shape[1] for p in flat_b[sum(counts[:s]):sum(counts[:s + 1])]) for s in range(len(counts))]
    tmm = min(tmm, m_total)
    assert m_total % tmm == 0 and (n_blocks == 1 or max(counts) == 1) and all(w % n_blocks == 0 for w in widths)
    n_b = len(counts)
    assert n_a in (1, n_b) and not (stacked and n_a > 1)

    def body(*refs):
        a_refs, b_refs, o_refs = refs[:n_a], refs[n_a:n_a + len(flat_b)], refs[n_a + len(flat_b):]

        @pl.when(pl.program_id(1) == 0)
        def _():
            for o_ref in o_refs:
                o_ref[...] = jnp.zeros_like(o_ref)

        lefts = _load_bf16(a_refs, [1] * n_a)
        for s, bv in enumerate(_load_bf16(b_refs, counts)):
            product = lax.dot_general(lefts[s % n_a], bv, (((0,), (0,)), ((), ())), preferred_element_type=F32)
            if stacked:
                o_refs[0][s] += product
            else:
                o_refs[s][...] += product

    in_specs = [pl.BlockSpec((tmm, a.shape[1]), lambda nb, m: (m, 0)) for a in a_list]
    in_specs += [pl.BlockSpec((tmm, b.shape[1] // n_blocks), lambda nb, m: (m, nb)) for b in flat_b]
    if stacked:
        out_shape = [jax.ShapeDtypeStruct((n_b, ks[0], widths[0]), F32)]
        out_specs = [pl.BlockSpec((n_b, ks[0], widths[0] // n_blocks), lambda nb, m: (0, 0, nb))]
    else:
        out_shape = [jax.ShapeDtypeStruct((k, w), F32) for k, w in zip(ks, widths)]
        out_specs = [pl.BlockSpec((k, w // n_blocks), lambda nb, m: (0, nb)) for k, w in zip(ks, widths)]
    return pl.pallas_call(
        body, name=name, grid=(n_blocks, m_total // tmm), in_specs=in_specs, out_specs=out_specs, out_shape=out_shape,
        compiler_params=_params("arbitrary", "arbitrary"),
    )(*a_list, *flat_b)


def _rows(name, fn, ins, tile_outs, sum_outs=(), tr=512, rider=None):
    t_total = max(a.shape[0] for a in ins)
    tr = min(tr, t_total)
    assert t_total % tr == 0
    n_in, n_tile = len(ins), len(tile_outs)
    rider = rider or _NoRider()
    n_steps = t_total // tr

    def body(*refs):
        own_ins, own_outs, _, riding = rider.split(refs, n_in, n_tile + len(sum_outs))
        step = pl.program_id(0)
        top, bottom = rider.at_steps(riding, step == 0, step == n_steps - 1, step == n_steps - 1)
        top()
        refs = tuple(own_ins) + tuple(own_outs)
        outs = fn(*[r[...].astype(F32) for r in refs[:n_in]])
        for o_ref, o in zip(refs[n_in:n_in + n_tile], outs[:n_tile]):
            o_ref[...] = o.astype(o_ref.dtype)
        if sum_outs:
            @pl.when(pl.program_id(0) == 0)
            def _():
                for s_ref in refs[n_in + n_tile:]:
                    s_ref[...] = jnp.zeros_like(s_ref)

            for s_ref, s in zip(refs[n_in + n_tile:], outs[n_tile:]):
                s_ref[...] += s
        bottom()

    def spec(shape):
        if shape[0] == 1:
            return pl.BlockSpec(shape, lambda i: (0, 0))
        return pl.BlockSpec((tr, shape[1]), lambda i: (i, 0))

    return pl.pallas_call(
        body, name=name, grid=(n_steps,), in_specs=[spec(a.shape) for a in ins] + [ANY] * len(rider.operands),
        out_specs=[spec(o.shape) for o in tile_outs] + [spec(s.shape) for s in sum_outs] + [ANY] * len(rider.out_shapes),
        out_shape=list(tile_outs) + list(sum_outs) + list(rider.out_shapes), scratch_shapes=list(rider.scratch),
        compiler_params=_params("arbitrary" if sum_outs or rider.operands else "parallel"),
    )(*ins, *rider.operands)


def _norm_fwd(name, x, gain, rider=None):
    def fn(xv, g):
        inv = lax.rsqrt(jnp.mean(xv * xv, axis=-1, keepdims=True) + RMS_EPS)
        return (xv * inv * g,)

    res = _rows(name, fn, [x, gain], [jax.ShapeDtypeStruct(x.shape, BF16)], rider=rider)
    return res[0], res[1:]


def _rms_norm_bwd(dh, xv, g):
    inv = lax.rsqrt(jnp.mean(xv * xv, axis=-1, keepdims=True) + RMS_EPS)
    xn = xv * inv
    dxn = dh * g
    return inv * (dxn - xn * jnp.mean(dxn * xn, axis=-1, keepdims=True)), jnp.sum(dh * xn, axis=0, keepdims=True)


def _ple_and_loss(gv, pv, x2v, tv, g_final, g_ple, w_pg):
    d = x2v.shape[1]
    s = _sigmoid(gv)
    xv = x2v + s * pv
    inv = lax.rsqrt(jnp.mean(xv * xv, axis=-1, keepdims=True) + RMS_EPS)
    err = xv * inv * g_final - tv
    dx3, d_final = _rms_norm_bwd(err * (1.0 / d), xv, g_final)
    d_pp, d_gp = dx3 * s, dx3 * pv * s * (1.0 - s)
    dh3 = lax.dot_general(d_gp.astype(BF16), w_pg, (((1,), (1,)), ((), ())), preferred_element_type=F32)
    dx2, d_ple = _rms_norm_bwd(dh3, x2v, g_ple)
    dx2 = dx2 + dx3
    return dx2, dx2, d_pp, d_gp, d_final, (0.5 / d) * jnp.sum(err * err, axis=0, keepdims=True), d_ple


def _window_counts(t_pos, w):
    return jnp.minimum(t_pos + 1, w).astype(F32)


def _pool_fwd(u, w_pool, scale, tr=512):
    t_total, width = u.shape
    tr = min(tr, t_total)
    n_groups = len(POOL_WINDOWS)
    gdim = width // n_groups
    ext = tr + POOL_HALO

    def body(u_ref, halo_ref, w_ref, s_ref, pooled_ref, ya_ref):
        i = pl.program_id(0)
        halo = jnp.where(i == 0, 0.0, halo_ref[...])
        t_pos = i * tr + lax.broadcasted_iota(jnp.int32, (tr, 1), 0)
        for g, w in enumerate(POOL_WINDOWS):
            cols = slice(g * gdim, (g + 1) * gdim)
            main = u_ref[:, cols]
            win = jnp.concatenate([halo[:, cols], main], axis=0)
            span = 1
            while span < w:
                win = win + pltpu.roll(win, span, 0)
                span *= 2
            pooled = win[POOL_HALO:, :] * (1.0 / _window_counts(t_pos, w)) - main
            pooled_b = pooled.astype(BF16)
            pooled_ref[:, cols] = pooled_b
            mixed = jnp.dot(pooled_b, w_ref[g], preferred_element_type=F32)
            ya_ref[:, cols] = (mixed * s_ref[:, cols]).astype(BF16)

    hb = tr // POOL_HALO
    return pl.pallas_call(
        body, name="pool_fwd", grid=(t_total // tr,),
        in_specs=[pl.BlockSpec((tr, width), lambda i: (i, 0)),
                  pl.BlockSpec((POOL_HALO, width), lambda i: (jnp.maximum(i * hb - 1, 0), 0)),
                  pl.BlockSpec((n_groups, gdim, gdim), lambda i: (0, 0, 0)),
                  pl.BlockSpec((1, width), lambda i: (0, 0))],
        out_specs=[pl.BlockSpec((tr, width), lambda i: (i, 0)), pl.BlockSpec((tr, width), lambda i: (i, 0))],
        out_shape=[jax.ShapeDtypeStruct(u.shape, BF16), jax.ShapeDtypeStruct(u.shape, BF16)],
        compiler_params=_params("parallel"),
    )(u, u, w_pool, scale)


def _pool_bwd(dya, pooled, w_pool, scale, tr=512):
    t_total, width = dya.shape
    tr = min(tr, t_total)
    n_groups = len(POOL_WINDOWS)
    gdim = width // n_groups
    ext = tr + POOL_HALO
    n_tiles = t_total // tr

    def body(d_ref, halo_ref, p_ref, w_ref, s_ref, du_ref, dw_ref, ds_ref):
        i = pl.program_id(0)

        @pl.when(i == 0)
        def _():
            dw_ref[...] = jnp.zeros_like(dw_ref)
            ds_ref[...] = jnp.zeros_like(ds_ref)

        halo = jnp.where(i == n_tiles - 1, 0.0, halo_ref[...])
        t_pos = i * tr + lax.broadcasted_iota(jnp.int32, (ext, 1), 0)
        for g, w in enumerate(POOL_WINDOWS):
            cols = slice(g * gdim, (g + 1) * gdim)
            sc = s_ref[:, cols]
            d_main = d_ref[:, cols]
            pooled_b = p_ref[:, cols]
            mixed = jnp.dot(pooled_b, w_ref[g], preferred_element_type=F32)
            ds_ref[:, cols] += jnp.sum(d_main * mixed, axis=0, keepdims=True)
            dmix = (jnp.concatenate([d_main, halo[:, cols]], axis=0) * sc).astype(BF16)
            dw_ref[g] += lax.dot_general(pooled_b, dmix[:tr, :], (((0,), (0,)), ((), ())),
                                         preferred_element_type=F32)
            dpool = lax.dot_general(dmix, w_ref[g], (((1,), (1,)), ((), ())), preferred_element_type=F32)
            win = dpool * (1.0 / _window_counts(t_pos, w))
            span = 1
            while span < w:
                win = win + pltpu.roll(win, ext - span, 0)
                span *= 2
            du_ref[:, cols] = (win[:tr, :] - dpool[:tr, :]).astype(BF16)

    hb = tr // POOL_HALO
    last_halo = t_total // POOL_HALO - 1
    return pl.pallas_call(
        body, name="pool_bwd", grid=(n_tiles,),
        in_specs=[pl.BlockSpec((tr, width), lambda i: (i, 0)),
                  pl.BlockSpec((POOL_HALO, width), lambda i: (jnp.minimum((i + 1) * hb, last_halo), 0)),
                  pl.BlockSpec((tr, width), lambda i: (i, 0)),
                  pl.BlockSpec((n_groups, gdim, gdim), lambda i: (0, 0, 0)),
                  pl.BlockSpec((1, width), lambda i: (0, 0))],
        out_specs=[pl.BlockSpec((tr, width), lambda i: (i, 0)),
                   pl.BlockSpec((n_groups, gdim, gdim), lambda i: (0, 0, 0)),
                   pl.BlockSpec((1, width), lambda i: (0, 0))],
        out_shape=[jax.ShapeDtypeStruct(dya.shape, BF16), jax.ShapeDtypeStruct((n_groups, gdim, gdim), F32),
                   jax.ShapeDtypeStruct((1, width), F32)],
        compiler_params=_params("arbitrary"),
    )(dya, dya, pooled, w_pool, scale)


def _head_masks():
    lane = lax.broadcasted_iota(jnp.int32, (1, LANES), 1)
    return lane < HEAD_DIM


def _stack_heads(tile, first):
    zero = jnp.zeros_like(tile)
    return jnp.concatenate([jnp.where(first, tile, zero), jnp.where(first, zero, tile)], axis=0)


def _causal_mask(t_pos, k_start):
    col = lax.broadcasted_iota(jnp.int32, (1, 2 * ATT_SLAB), 1)
    return k_start + (col & (ATT_SLAB - 1)) < t_pos


def _slab_scores(q, kd, mask):
    z2 = lax.dot_general(q, kd, (((1,), (1,)), ((), ())), preferred_element_type=F32) * LOG2_E
    log_hit = jnp.minimum(z2, 0.0) - jnp.log2(1.0 + jnp.exp2(-jnp.abs(z2)))
    log_fail = log_hit - z2
    return log_hit, (log_fail if mask is None else jnp.where(mask, log_fail, 0.0))


def _weights(log_hit, suffix, mask):
    arg = log_hit + suffix
    return jnp.exp2(arg if mask is None else jnp.where(mask, arg, -1e30))


def _tri(upper):
    r = lax.broadcasted_iota(jnp.int32, (ATT_CHUNK, ATT_CHUNK), 0)
    c = lax.broadcasted_iota(jnp.int32, (ATT_CHUNK, ATT_CHUNK), 1)
    return jnp.where(r > c if upper else r < c, 1.0, 0.0).astype(BF16)


def _tri_spec():
    return pl.BlockSpec((ATT_CHUNK, ATT_CHUNK), lambda h, i: (0, 0), pipeline_mode=pl.Buffered(1))


def _scan_chunk(v, tri):
    return jnp.dot(v.astype(BF16), tri, preferred_element_type=F32)


def _lane_bcast(col):
    return jnp.broadcast_to(col, (col.shape[0], LANES))


def _scan_slab(v, tri, carries, from_right):
    n_chunks = ATT_SLAB // ATT_CHUNK
    edge = 0 if from_right else ATT_CHUNK - 1
    parts, new_carries = [None] * (2 * n_chunks), []
    for head in range(2):
        run = carries[head]
        for c in (reversed(range(n_chunks)) if from_right else range(n_chunks)):
            lo_col = head * ATT_SLAB + c * ATT_CHUNK
            vc = v[:, lo_col:lo_col + ATT_CHUNK]
            sc = _scan_chunk(vc, tri)
            parts[head * n_chunks + c] = sc + jnp.concatenate([run] * (ATT_CHUNK // LANES), axis=1)
            run = run + _lane_bcast(sc[:, edge:edge + 1] + vc[:, edge:edge + 1])
        new_carries.append(run)
    return jnp.concatenate(parts, axis=1), new_carries


def _fold_heads(stacked, first):
    s = stacked.shape[0] // 2
    return jnp.where(first, stacked[:s], stacked[s:])


class _NoRider:
    operands, out_shapes, scratch = (), (), ()

    def split(self, refs, n_base_in, n_base_out):
        n_in, n_out, n_sem = len(self.operands), len(self.out_shapes), len(self.scratch)
        a = n_base_in + n_in
        b = a + n_base_out + n_out
        mine = (refs[n_base_in:a], refs[a + n_base_out:b], refs[b:b + n_sem])
        return refs[:n_base_in], refs[a:a + n_base_out], refs[b + n_sem:], mine

    def start(self, ins, outs, sems):
        pass

    def relay(self, ins, outs, sems):
        pass

    def finish(self, ins, outs, sems):
        pass

    def at_steps(self, refs, first_step, relay_step, last_step):
        if not self.operands:
            return (lambda: None), (lambda: None)

        def top():
            pl.when(first_step)(lambda: self.start(*refs))
            pl.when(relay_step)(lambda: self.relay(*refs))

        return top, lambda: pl.when(last_step)(lambda: self.finish(*refs))


def _attn_fwd(q_src, q_col, kv_src, k_col, v_col, n_pairs=4, rider=_NoRider()):
    t_total = q_src.shape[0]
    blk = ATT_BLOCK
    n_steps = t_total // (ATT_CHAINS * blk)
    assert t_total % ATT_SLAB == 0 and ATT_SLAB == ATT_BLOCK

    def body(*refs):
        (q_ref, k_ref, v_ref, suffix_ref), (o_ref,), _, riding = rider.split(refs, 4, 1)
        h, ii = pl.program_id(0), pl.program_id(1)
        top, bottom = rider.at_steps(riding, (h == 0) & (ii == 0), (h == n_pairs - 1) & (ii == 0),
                                     (h == n_pairs - 1) & (ii == n_steps - 1))
        top()
        first = _head_masks()
        suffix_tri = suffix_ref[...]
        blocks = [ATT_CHAINS * ii + c for c in range(ATT_CHAINS)]
        qs = [q_ref[c * blk:(c + 1) * blk, :] * ATT_SCALE for c in range(ATT_CHAINS)]
        t_pos = [b * blk + lax.broadcasted_iota(jnp.int32, (blk, 1), 0) for b in blocks]

        def one(c, t, chain, on_diagonal):
            _, acc, right_a, right_b = chain
            k_start = pl.multiple_of((blocks[c] - t) * ATT_SLAB, ATT_SLAB)
            kd = _stack_heads(k_ref[pl.ds(k_start, ATT_SLAB), :], first)
            vd = _stack_heads(v_ref[pl.ds(k_start, ATT_SLAB), :], first)
            mask = _causal_mask(t_pos[c], k_start) if on_diagonal else None
            log_hit, log_fail = _slab_scores(qs[c], kd, mask)
            suffix, (right_a, right_b) = _scan_slab(log_fail, suffix_tri, (right_a, right_b), from_right=True)
            a = _weights(log_hit, suffix, mask).astype(BF16)
            acc = acc + jnp.dot(a, vd, preferred_element_type=F32)
            return jnp.max(jnp.maximum(right_a, right_b)), acc, right_a, right_b

        def step(state, on_diagonal):
            t, chains = state
            return t + 1, tuple(one(c, t, chains[c], on_diagonal) for c in range(ATT_CHAINS))

        def more(state):
            t, chains = state
            return (t <= blocks[0]) & (functools.reduce(jnp.maximum, [ch[0] for ch in chains]) > ATT_EXIT_BELOW)

        zero = jnp.zeros((blk, LANES), F32)
        state = step((0, ((jnp.float32(0.0), zero, zero, zero),) * ATT_CHAINS), on_diagonal=True)
        t, chains = lax.while_loop(more, functools.partial(step, on_diagonal=False), state)
        for c in range(ATT_CHAINS):
            chain = chains[c]
            if c:
                _, chain = lax.while_loop(
                    lambda s, c=c: (s[0] <= blocks[c]) & (s[1][0] > ATT_EXIT_BELOW),
                    lambda s, c=c: (s[0] + 1, one(c, s[0], s[1], False)), (t, chain))
            o_ref[c * blk:(c + 1) * blk, :] = chain[1].astype(BF16)
        bottom()

    rows = ATT_CHAINS * blk
    res = pl.pallas_call(
        body, name="attn_fwd", grid=(n_pairs, n_steps),
        in_specs=[pl.BlockSpec((rows, LANES), lambda h, i: (i, q_col + h)),
                  pl.BlockSpec((t_total, LANES), lambda h, i: (0, k_col + h)),
                  pl.BlockSpec((t_total, LANES), lambda h, i: (0, v_col + h)), _tri_spec()] + [ANY] * len(rider.operands),
        out_specs=[pl.BlockSpec((rows, LANES), lambda h, i: (i, h))] + [ANY] * len(rider.out_shapes),
        out_shape=[jax.ShapeDtypeStruct((t_total, n_pairs * LANES), BF16)] + list(rider.out_shapes),
        scratch_shapes=list(rider.scratch),
        compiler_params=_params("arbitrary", "arbitrary"),
    )(q_src, kv_src, kv_src, _tri(upper=True), *rider.operands)
    return res[0], res[1:]


def _attn_bwd(q_src, q_col, kv_src, k_col, v_col, dy, n_pairs=4, rider=_NoRider()):
    t_total = q_src.shape[0]
    blk = ATT_BLOCK
    n_steps = t_total // (ATT_CHAINS * blk)
    n_slabs = t_total // ATT_SLAB
    assert t_total % ATT_SLAB == 0 and ATT_SLAB == ATT_BLOCK

    def body(*refs):
        ins, (dq_ref, dk_ref, dv_ref), (g_s, dk_acc, dv_acc), riding = rider.split(refs, 6, 3)
        q_ref, dy_ref, k_ref, v_ref, suffix_ref, prefix_ref = ins
        h, ii = pl.program_id(0), pl.program_id(1)
        top, bottom = rider.at_steps(riding, (h == 0) & (ii == 0), (h == n_pairs - 1) & (ii == 0),
                                     (h == n_pairs - 1) & (ii == n_steps - 1))
        top()

        @pl.when(ii == 0)
        def _():
            dk_acc[...] = jnp.zeros_like(dk_acc)
            dv_acc[...] = jnp.zeros_like(dv_acc)

        first = _head_masks()
        suffix_tri = suffix_ref[...]
        prefix_tri = prefix_ref[...]
        blocks = [ATT_CHAINS * ii + c for c in range(ATT_CHAINS)]
        rows = [slice(c * blk, (c + 1) * blk) for c in range(ATT_CHAINS)]
        qs = [q_ref[r, :] * ATT_SCALE for r in rows]
        dys = [dy_ref[r, :] for r in rows]
        t_pos = [b * blk + lax.broadcasted_iota(jnp.int32, (blk, 1), 0) for b in blocks]

        def one1(c, t, chain, on_diagonal):
            _, right_a, right_b = chain
            slab = blocks[c] - t
            k_start = pl.multiple_of(slab * ATT_SLAB, ATT_SLAB)
            kd = _stack_heads(k_ref[pl.ds(k_start, ATT_SLAB), :], first)
            vd = _stack_heads(v_ref[pl.ds(k_start, ATT_SLAB), :], first)
            mask = _causal_mask(t_pos[c], k_start) if on_diagonal else None
            log_hit, log_fail = _slab_scores(qs[c], kd, mask)
            suffix, (right_a, right_b) = _scan_slab(log_fail, suffix_tri, (right_a, right_b), from_right=True)
            a = _weights(log_hit, suffix, mask)
            da = lax.dot_general(dys[c], vd, (((1,), (1,)), ((), ())), preferred_element_type=F32)
            g_s[c, slab] = (da * a).astype(BF16)
            dv_acc[pl.ds(k_start, ATT_SLAB), :] += _fold_heads(lax.dot_general(
                a.astype(BF16), dys[c], (((0,), (0,)), ((), ())), preferred_element_type=F32), first)
            return jnp.max(jnp.maximum(right_a, right_b)), right_a, right_b

        def step1(state, on_diagonal):
            t, chains = state
            return t + 1, tuple(one1(c, t, chains[c], on_diagonal) for c in range(ATT_CHAINS))

        def more(state):
            t, chains = state
            return (t <= blocks[0]) & (functools.reduce(jnp.maximum, [ch[0] for ch in chains]) > ATT_EXIT_BELOW)

        zero = jnp.zeros((blk, LANES), F32)
        state = step1((0, ((jnp.float32(0.0), zero, zero),) * ATT_CHAINS), on_diagonal=True)
        joint, chains = lax.while_loop(more, functools.partial(step1, on_diagonal=False), state)
        done = [joint]
        for c in range(1, ATT_CHAINS):
            done.append(lax.while_loop(
                lambda s, c=c: (s[0] <= blocks[c]) & (s[1][0] > ATT_EXIT_BELOW),
                lambda s, c=c: (s[0] + 1, one1(c, s[0], s[1], False)), (joint, chains[c]))[0])

        def one2(c, t, carry, on_diagonal):
            dq, left_a, left_b = carry
            slab = blocks[c] - t
            k_start = pl.multiple_of(slab * ATT_SLAB, ATT_SLAB)
            kd = _stack_heads(k_ref[pl.ds(k_start, ATT_SLAB), :], first)
            g = g_s[c, slab]
            z2 = lax.dot_general(qs[c], kd, (((1,), (1,)), ((), ())), preferred_element_type=F32) * LOG2_E
            sig = 1.0 / (1.0 + jnp.exp2(-z2))
            prefix, (left_a, left_b) = _scan_slab(g, prefix_tri, (left_a, left_b), from_right=False)
            dz = g * (1.0 - sig) - sig * prefix
            if on_diagonal:
                dz = jnp.where(_causal_mask(t_pos[c], k_start), dz, 0.0)
            dz = dz.astype(BF16)
            dq = dq + jnp.dot(dz, kd, preferred_element_type=F32)
            dk_acc[pl.ds(k_start, ATT_SLAB), :] += _fold_heads(lax.dot_general(
                dz, qs[c], (((0,), (0,)), ((), ())), preferred_element_type=F32), first)
            return dq, left_a, left_b

        carries = [(zero, zero, zero)]
        for c in range(1, ATT_CHAINS):
            carries.append(lax.fori_loop(
                0, done[c] - joint, lambda n, carry, c=c: one2(c, done[c] - 1 - n, carry, False), (zero, zero, zero)))
        carries = lax.fori_loop(
            0, joint - 1,
            lambda n, cs: tuple(one2(c, joint - 1 - n, cs[c], False) for c in range(ATT_CHAINS)), tuple(carries))
        for c in range(ATT_CHAINS):
            dq_ref[rows[c], :] = (one2(c, 0, carries[c], True)[0] * ATT_SCALE).astype(BF16)

        @pl.when(ii == n_steps - 1)
        def _():
            dk_ref[...] = dk_acc[...].astype(BF16)
            dv_ref[...] = dv_acc[...].astype(BF16)

        bottom()

    out = jax.ShapeDtypeStruct((t_total, n_pairs * LANES), BF16)
    n_rows = ATT_CHAINS * blk
    whole = dict(pipeline_mode=pl.Buffered(1))
    res = pl.pallas_call(
        body, name="attn_bwd", grid=(n_pairs, n_steps),
        in_specs=[pl.BlockSpec((n_rows, LANES), lambda h, i: (i, q_col + h)),
                  pl.BlockSpec((n_rows, LANES), lambda h, i: (i, h)),
                  pl.BlockSpec((t_total, LANES), lambda h, i: (0, k_col + h), **whole),
                  pl.BlockSpec((t_total, LANES), lambda h, i: (0, v_col + h), **whole), _tri_spec(), _tri_spec()]
        + [ANY] * len(rider.operands),
        out_specs=[pl.BlockSpec((n_rows, LANES), lambda h, i: (i, h)),
                   pl.BlockSpec((t_total, LANES), lambda h, i: (0, h)),
                   pl.BlockSpec((t_total, LANES), lambda h, i: (0, h))] + [ANY] * len(rider.out_shapes),
        out_shape=[out, out, out] + list(rider.out_shapes),
        scratch_shapes=list(rider.scratch) + [pltpu.VMEM((ATT_CHAINS, n_slabs, blk, 2 * ATT_SLAB), BF16),
                                              pltpu.VMEM((t_total, LANES), F32), pltpu.VMEM((t_total, LANES), F32)],
        compiler_params=_params("arbitrary", "arbitrary"),
    )(q_src, dy, kv_src, kv_src, _tri(upper=True), _tri(upper=False), *rider.operands)
    return res[:3], res[3:]


def _adamw(name, w, g, m, v):
    def fn(wv, gv, mv, vv):
        mn = ADAM_B1 * mv + (1.0 - ADAM_B1) * gv
        vn = ADAM_B2 * vv + (1.0 - ADAM_B2) * (gv * gv)
        m_hat = mn / (1.0 - ADAM_B1 ** ADAM_STEP)
        v_hat = vn / (1.0 - ADAM_B2 ** ADAM_STEP)
        return -ADAM_LR * (m_hat / (jnp.sqrt(v_hat) + ADAM_EPS) + ADAM_WD * wv), mn, vn

    rows = w.shape[0]
    tr = _row_tile(rows)
    shp = jax.ShapeDtypeStruct(w.shape, F32)
    if rows == 1:
        def body(w_ref, g_ref, m_ref, v_ref, d_ref, mo_ref, vo_ref):
            d, mn, vn = fn(w_ref[...], g_ref[...], m_ref[...], v_ref[...])
            d_ref[...], mo_ref[...], vo_ref[...] = d, mn, vn

        return pl.pallas_call(body, name=name, out_shape=[shp, shp, shp])(w, g, m, v)
    return _rows(name, fn, [w, g, m, v], [shp, shp, shp], tr=tr)


def _place():
    return lax.axis_index("x"), lax.axis_index("y"), lax.axis_index("c")


def _other_chips(x, y):
    return [(1 - x, y), (x, 1 - y), (1 - x, 1 - y)]


ANY = pl.BlockSpec(memory_space=pl.ANY)


def _remote(src, dst, send_sem, recv_sem, to):
    return pltpu.make_async_remote_copy(src_ref=src, dst_ref=dst, send_sem=send_sem, recv_sem=recv_sem,
                                        device_id=to, device_id_type=MESH)


class _WeightGather(_NoRider):
    def __init__(self, shards):
        n_w = len(shards)
        self.operands = list(shards)
        self.out_shapes = [jax.ShapeDtypeStruct((N_CHIPS,) + s.shape, s.dtype) for s in shards]
        self.scratch = [pltpu.SemaphoreType.DMA((3, n_w))] * 4 + [pltpu.SemaphoreType.DMA((n_w,))] * 2

    def _copies(self, ins, outs, sems):
        send_sems, recv_sems, relay_send, relay_recv, own_send, own_recv = sems
        x, y, c = _place()
        my_chip, sibling = 2 * x + y, (x, y, 1 - c)
        n_w = len(ins)

        def half(w, chip, core):
            h = self.operands[w].shape[0] // 2
            return outs[w].at[chip, pl.ds(core * h, h)]

        own = [_remote(ins[w], outs[w].at[my_chip], own_send.at[w], own_recv.at[w], sibling) for w in range(n_w)]
        sends, landed, relays, relayed = [], [], [], []
        for p, (ox, oy) in enumerate(_other_chips(x, y)):
            for w in range(n_w):
                h = self.operands[w].shape[0] // 2
                sends.append(_remote(ins[w].at[pl.ds(c * h, h)], half(w, my_chip, c), send_sems.at[p, w],
                                     recv_sems.at[p, w], (ox, oy, c)))
                here = half(w, 2 * ox + oy, c)
                landed.append(_remote(here, here, send_sems.at[p, w], recv_sems.at[p, w], (ox, oy, c)))
                relays.append(_remote(here, here, relay_send.at[p, w], relay_recv.at[p, w], sibling))
                there = half(w, 2 * ox + oy, 1 - c)
                relayed.append(_remote(there, there, relay_send.at[p, w], relay_recv.at[p, w], sibling))
        return own, sends, landed, relays, relayed

    def start(self, ins, outs, sems):
        own, sends, _, _, _ = self._copies(ins, outs, sems)
        for cp in own + sends:
            cp.start()

    def relay(self, ins, outs, sems):
        _, _, landed, relays, _ = self._copies(ins, outs, sems)
        for arrival, cp in zip(landed, relays):
            arrival.wait_recv()
            cp.start()

    def finish(self, ins, outs, sems):
        own, sends, _, relays, relayed = self._copies(ins, outs, sems)
        for arrival in relayed:
            arrival.wait_recv()
        for cp in sends + relays:
            cp.wait_send()
        for cp in own:
            cp.wait()


class _ChipExchange(_NoRider):
    def __init__(self, pair_sums):
        n_w = len(pair_sums)
        self.operands = list(pair_sums)
        self.out_shapes = [jax.ShapeDtypeStruct((3,) + s.shape[1:], s.dtype) for s in pair_sums]
        self.scratch = [pltpu.SemaphoreType.DMA((3, n_w))] * 2

    def _copies(self, ins, outs, sems):
        send_sems, recv_sems = sems
        x, y, c = _place()
        return [_remote(ins[w].at[2 * ox + oy], outs[w].at[p], send_sems.at[p, w], recv_sems.at[p, w], (ox, oy, c))
                for p, (ox, oy) in enumerate(_other_chips(x, y)) for w in range(len(ins))]

    def start(self, ins, outs, sems):
        for cp in self._copies(ins, outs, sems):
            cp.start()

    def finish(self, ins, outs, sems):
        for cp in self._copies(ins, outs, sems):
            cp.wait()


def _pair_exchange(name, grads):
    n_w = len(grads)

    def halves(w):
        return grads[w].shape[-2] // 2

    def body(*refs):
        ins, theirs = refs[:n_w], refs[n_w:2 * n_w]
        send_sems, recv_sems = refs[2 * n_w:]
        x, y, c = _place()
        sends = []
        for w in range(n_w):
            rows = pl.ds((1 - c) * halves(w), halves(w))
            src = ins[w].at[:, rows, :] if grads[w].ndim == 3 else ins[w].at[rows, :]
            sends.append(_remote(src, theirs[w], send_sems.at[w], recv_sems.at[w], (x, y, 1 - c)))
        for cp in sends:
            cp.start()
        for cp in sends:
            cp.wait()

    return pl.pallas_call(
        body, name=name, in_specs=[ANY] * n_w, out_specs=[ANY] * n_w,
        out_shape=[jax.ShapeDtypeStruct(g.shape[:-2] + (halves(w), g.shape[-1]), F32) for w, g in enumerate(grads)],
        scratch_shapes=[pltpu.SemaphoreType.DMA((n_w,)), pltpu.SemaphoreType.DMA((n_w,))],
    )(*grads)


def _pair_share(shards):
    n_w = len(shards)

    def body(*refs):
        ins, outs = refs[:n_w], refs[n_w:2 * n_w]
        send_sems, recv_sems = refs[2 * n_w:]
        x, y, c = _place()
        sends = []
        for w in range(n_w):
            h = shards[w].shape[0] // 2
            mine = outs[w].at[pl.ds(c * h, h)]
            sends.append(pltpu.make_async_remote_copy(
                src_ref=mine, dst_ref=mine, send_sem=send_sems.at[w], recv_sem=recv_sems.at[w],
                device_id=(x, y, 1 - c), device_id_type=MESH))
        for cp in sends:
            cp.start()
        for w in range(n_w):
            h = shards[w].shape[0] // 2
            theirs = outs[w].at[pl.ds((1 - c) * h, h)]
            pltpu.make_async_remote_copy(
                src_ref=theirs, dst_ref=theirs, send_sem=send_sems.at[w], recv_sem=recv_sems.at[w],
                device_id=(x, y, 1 - c), device_id_type=MESH).wait_recv()
        for cp in sends:
            cp.wait_send()

    return pl.pallas_call(
        body, name="pair_share", in_specs=[ANY] * n_w, out_specs=[ANY] * n_w,
        out_shape=[jax.ShapeDtypeStruct(s.shape, s.dtype) for s in shards],
        input_output_aliases={w: w for w in range(n_w)},
        scratch_shapes=[pltpu.SemaphoreType.DMA((n_w,)), pltpu.SemaphoreType.DMA((n_w,))],
    )(*shards)


def _all_reduce_small(vec):
    rows = vec.shape[0]

    def body(v_ref, o_ref, slots, send_sems, recv_sems):
        x, y, c = _place()
        me = 4 * x + 2 * y + c
        slots[me] = v_ref[...]
        sends = []
        for k in range(1, N_DEV):
            peer = (x ^ (k >> 2), y ^ ((k >> 1) & 1), c ^ (k & 1))
            sends.append(pltpu.make_async_remote_copy(
                src_ref=v_ref, dst_ref=slots.at[me], send_sem=send_sems.at[k - 1], recv_sem=recv_sems.at[k - 1],
                device_id=peer, device_id_type=MESH))
        for cp in sends:
            cp.start()
        for k in range(1, N_DEV):
            px, py, pc = x ^ (k >> 2), y ^ ((k >> 1) & 1), c ^ (k & 1)
            landed = slots.at[4 * px + 2 * py + pc]
            pltpu.make_async_remote_copy(
                src_ref=landed, dst_ref=landed, send_sem=send_sems.at[k - 1], recv_sem=recv_sems.at[k - 1],
                device_id=(px, py, pc), device_id_type=MESH).wait_recv()
        for cp in sends:
            cp.wait_send()
        total = slots[0]
        for d in range(1, N_DEV):
            total = total + slots[d]
        o_ref[...] = total

    vm = pl.BlockSpec(memory_space=pltpu.VMEM)
    return pl.pallas_call(
        body, name="all_reduce_small", in_specs=[vm], out_specs=vm, out_shape=jax.ShapeDtypeStruct(vec.shape, F32),
        scratch_shapes=[pltpu.VMEM((N_DEV, rows, LANES), F32), pltpu.SemaphoreType.DMA((N_DEV - 1,)),
                        pltpu.SemaphoreType.DMA((N_DEV - 1,))],
    )(vec)


def _row_tile(rows):
    fits = [tr for tr in range(16, min(rows, 512) + 1, 16) if rows % tr == 0]
    return max(fits) if fits else rows


def _pair_sum(name, place, grad, theirs):
    if grad.ndim == 2:
        return _pair_sum_joined(name, place, grad, theirs)
    n, r, c = grad.shape
    half = r // 2
    tr = _row_tile(half)
    nb = half // tr

    def body(place_ref, g_ref, t_ref, o_ref):
        o_ref[...] = (g_ref[...] + t_ref[...]).astype(BF16)

    return pl.pallas_call(
        body, name=name, out_shape=jax.ShapeDtypeStruct((n, half, c), BF16),
        grid_spec=pltpu.PrefetchScalarGridSpec(
            num_scalar_prefetch=1, grid=(n, nb),
            in_specs=[pl.BlockSpec((1, tr, c), lambda j, i, pr: (j, pr[0] * nb + i, 0)),
                      pl.BlockSpec((1, tr, c), lambda j, i, pr: (j, i, 0))],
            out_specs=pl.BlockSpec((1, tr, c), lambda j, i, pr: (j, i, 0))),
        compiler_params=_params("parallel", "parallel"),
    )(place, grad, theirs)


def _pair_sum_joined(name, place, grad, theirs):
    r, wide = grad.shape
    half, c = r // 2, wide // N_CHIPS
    tr = _row_tile(half)
    nb = half // tr

    def body(place_ref, g_ref, t_ref, o_ref):
        for j in range(N_CHIPS):
            cols = slice(j * c, (j + 1) * c)
            o_ref[j] = (g_ref[:, cols] + t_ref[:, cols]).astype(BF16)

    return pl.pallas_call(
        body, name=name, out_shape=jax.ShapeDtypeStruct((N_CHIPS, half, c), BF16),
        grid_spec=pltpu.PrefetchScalarGridSpec(
            num_scalar_prefetch=1, grid=(nb,),
            in_specs=[pl.BlockSpec((tr, wide), lambda i, pr: (pr[0] * nb + i, 0)),
                      pl.BlockSpec((tr, wide), lambda i, pr: (i, 0))],
            out_specs=pl.BlockSpec((N_CHIPS, tr, c), lambda i, pr: (0, i, 0))),
        compiler_params=_params("parallel"),
    )(place, grad, theirs)


def _sum_chips(name, place, pair_sums, landed):
    _, half, c = pair_sums.shape
    tr = _row_tile(half)
    nb = half // tr

    def body(place_ref, s_ref, q_ref, o_ref):
        total = s_ref[0].astype(F32)
        for p in range(3):
            total = total + q_ref[p].astype(F32)
        o_ref[...] = total

    return pl.pallas_call(
        body, name=name, out_shape=jax.ShapeDtypeStruct((2 * half, c), F32),
        grid_spec=pltpu.PrefetchScalarGridSpec(
            num_scalar_prefetch=1, grid=(nb,),
            in_specs=[pl.BlockSpec((1, tr, c), lambda i, pr: (pr[1], i, 0)),
                      pl.BlockSpec((3, tr, c), lambda i, pr: (0, i, 0))],
            out_specs=pl.BlockSpec((tr, c), lambda i, pr: (pr[0] * nb + i, 0))),
        compiler_params=_params("parallel"),
    )(place, pair_sums, landed)


BIG = ("w_in", "w_branch_a", "w_branch_b", "w_out", "w_ffn_gate", "w_ffn_up", "w_ffn_down", "w_ple_gate", "w_ple_proj")
LATE = BIG[1:]
COLUMN_SHARDED = ("w_in", "w_branch_a", "w_branch_b", "w_ffn_gate", "w_ffn_up", "w_ple_proj")
SMALL = ("norm_mix", "w_pool", "pool_scale", "norm_ffn", "norm_ple", "norm_final")


def _join_columns(w4):
    return jnp.concatenate([w4[j] for j in range(N_CHIPS)], axis=1)


def _sds(shape, dtype):
    return jax.ShapeDtypeStruct(shape, dtype)


def _local_step(x, p, target, wf, small, gather_first=None, gather_late=None, exchange_early=None,
                exchange_last=None):
    t, d = x.shape
    w_pool_b = small["w_pool"].astype(BF16)
    dp = w_pool_b.shape[0] * w_pool_b.shape[1]

    h1, first = _norm_fwd("norm_mix", x, small["norm_mix"], rider=gather_first)
    w_in = first[0] if gather_first else wf["w_in"]
    u, q, kv, ga, gb = _mm(
        "proj", [h1], [w_in[j] for j in range(N_CHIPS)], "nn",
        [_sds((t, dp), F32), _sds((t, dp), BF16), _sds((t, d), BF16), _sds((t, d), BF16), _sds((t, d), BF16)],
        separate=True, epilogue=lambda uq, kv_, ga_, gb_: (uq[:, :dp], uq[:, dp:], kv_, ga_, gb_), tm=512)
    pooled, ya = _pool_fwd(u, w_pool_b, small["pool_scale"])
    n_pairs = dp // LANES
    yb, late = _attn_fwd(q, 0, kv, 0, n_pairs, n_pairs, rider=gather_late or _NoRider())
    wf = {**wf, **dict(zip(LATE, late))}
    w_gate, w_up = _join_columns(wf["w_ffn_gate"]), _join_columns(wf["w_ffn_up"])
    dff = w_gate.shape[1]
    w_down = wf["w_ffn_down"].reshape(dff, d)
    w_a, w_b, w_pp = _join_columns(wf["w_branch_a"]), _join_columns(wf["w_branch_b"]), _join_columns(wf["w_ple_proj"])
    w_out = wf["w_out"].reshape(d, d)
    w_pg = wf["w_ple_gate"].reshape(d, d)
    def residual_norm(branch, xv, g, w):
        xn = xv + jnp.dot(branch.astype(BF16), w, preferred_element_type=F32)
        return xn, xn * lax.rsqrt(jnp.mean(xn * xn, axis=-1, keepdims=True) + RMS_EPS) * g

    def mixer_tail(tav, tbv, gav, gbv, xv, g, w):
        merged = _sigmoid(gav) * tav + _sigmoid(gbv) * tbv
        return (tav, tbv, merged) + residual_norm(merged, xv, g, w)

    def ffn_tail(gv, uv, xv, g, w):
        act = gv * _sigmoid(gv) * uv
        return (gv, uv, act) + residual_norm(act, xv, g, w)

    stream = [_sds((t, d), F32), _sds((t, d), BF16)]
    ta, tb, merged, x1, h2 = _mm(
        "mixer_out", [ya, yb], [w_a, w_b], "nn", [_sds((t, d), BF16)] * 3 + stream,
        extras=[ga, gb, x, small["norm_ffn"]], wholes=[w_out], separate=True, epilogue=mixer_tail, tm=512)
    gate, up, act, x2, h3 = _mm(
        "ffn", [h2], [w_gate, w_up], "nn", [_sds((t, dff), BF16)] * 3 + stream,
        extras=[x1, small["norm_ple"]], wholes=[w_down], separate=True, epilogue=ffn_tail, tm=256)
    dx2, dx2_b, d_pp, d_gp, d_norm_final, loss_row, d_norm_ple = _mm(
        "ple_loss", [h3, p], [w_pg, w_pp], "nn", stream + [_sds((t, d), BF16)] * 2,
        extras=[x2, target, small["norm_final"].reshape(1, d), small["norm_ple"]], wholes=[w_pg], separate=True,
        epilogue=_ple_and_loss, sum_shapes=[_sds((1, d), F32)] * 3, tm=512)

    def through_norm(dh, xv, g, dres):
        dx, d_gain = _rms_norm_bwd(dh, xv, g)
        return dx + dres, dx + dres, d_gain

    gain_sum = [_sds((1, d), F32)]
    g_w_pp, g_w_pg = _mm_tn("g_ple", [p, h3], [d_pp, d_gp])

    def ffn_bwd(acc, gv, uv):
        s = _sigmoid(gv)
        return acc * uv * (s * (1.0 + gv * (1.0 - s))), acc * (gv * s)

    d_gate, d_up = _mm("d_act", [dx2_b], [w_down], "nt", [_sds((t, dff), BF16)] * 2, extras=[gate, up],
                       epilogue=ffn_bwd, tm=256)
    g_w_down, = _mm_tn("g_ffn_down", [act], [dx2_b], tmm=512)
    g_w_gate, g_w_up = _mm_tn("g_ffn_gate_up", [h2], [d_gate, d_up], n_blocks=2)
    dx1, dx1_b, d_norm_ffn = _mm(
        "d_h2", [d_gate, d_up], [w_gate, w_up], "nt", stream, extras=[x1, small["norm_ffn"], dx2],
        epilogue=through_norm, sum_shapes=gain_sum, tm=512)

    def merge_bwd(acc, tav, tbv, gav, gbv):
        sa, sb = _sigmoid(gav), _sigmoid(gbv)
        return acc * sa, acc * sb, acc * tav * sa * (1.0 - sa), acc * tbv * sb * (1.0 - sb)

    d_ta, d_tb, d_ga, d_gb = _mm("d_merged", [dx1_b], [w_out], "nt", [_sds((t, d), BF16)] * 4,
                                 extras=[ta, tb, ga, gb], epilogue=merge_bwd, tm=512)
    g_w_out, g_w_a, g_w_b = _mm_tn("g_mixer", [merged, ya, yb], [dx1_b, d_ta, d_tb])
    d_ya, d_yb = _mm("d_branches", [d_ta, d_tb], [w_a, w_b], "nt", [_sds((t, dp), F32), _sds((t, dp), BF16)],
                     separate=True)
    d_u, g_w_pool, d_pool_scale = _pool_bwd(d_ya, pooled, w_pool_b, small["pool_scale"])
    big = {
        "w_branch_a": g_w_a, "w_branch_b": g_w_b, "w_out": g_w_out.reshape(wf["w_out"].shape),
        "w_ffn_gate": g_w_gate, "w_ffn_up": g_w_up, "w_ffn_down": g_w_down.reshape(wf["w_ffn_down"].shape),
        "w_ple_gate": g_w_pg.reshape(wf["w_ple_gate"].shape), "w_ple_proj": g_w_pp,
    }
    rider = exchange_early(big) if exchange_early else _NoRider()
    (d_q, d_k, d_v), early = _attn_bwd(q, 0, kv, 0, n_pairs, d_yb, n_pairs, rider=rider)
    d_proj = [(d_u, d_q), (d_k, d_v), d_ga, d_gb]
    big["w_in"], = _mm_tn("g_w_in", [h1], d_proj, tmm=512, stacked=True)
    rider = exchange_last(big["w_in"]) if exchange_last else _NoRider()
    res = _mm(
        "d_h1", d_proj, [w_in[j] for j in range(N_CHIPS)], "nt", [_sds((t, d), F32)],
        extras=[x, small["norm_mix"], dx1], epilogue=lambda dh, xv, g, dres: through_norm(dh, xv, g, dres)[1:],
        sum_shapes=gain_sum, tm=512, rider=rider)
    (grad_x, d_norm_mix), last = res if rider.operands else (res, ())
    small_g = {"norm_mix": d_norm_mix, "w_pool": g_w_pool, "pool_scale": d_pool_scale, "norm_ffn": d_norm_ffn,
               "norm_ple": d_norm_ple, "norm_final": d_norm_final}
    return grad_x, big, small_g, loss_row, early, last


def _split2(res, n):
    return res[:n], res[n:]


def _pack_small(small_g, loss_row):
    parts, layout = [], []
    for name in SMALL + ("loss",):
        v = (loss_row if name == "loss" else small_g[name]).reshape(-1, LANES)
        pad = (-v.shape[0]) % 8
        if pad:
            v = jnp.concatenate([v, jnp.zeros((pad, LANES), F32)], axis=0)
        layout.append((name, sum(q.shape[0] for q in parts), v.shape[0]))
        parts.append(v)
    return jnp.concatenate(parts, axis=0), layout


def kernel(x, p, norm_mix, w_in, w_pool, pool_scale, w_branch_a, w_branch_b, w_out, norm_ffn, w_ffn_gate, w_ffn_up, w_ffn_down, norm_ple, w_ple_gate, w_ple_proj, norm_final, loss_target, m_norm_mix, m_w_in, m_w_pool, m_pool_scale, m_w_branch_a, m_w_branch_b, m_w_out, m_norm_ffn, m_w_ffn_gate, m_w_ffn_up, m_w_ffn_down, m_norm_ple, m_w_ple_gate, m_w_ple_proj, m_norm_final, v_norm_mix, v_w_in, v_w_pool, v_pool_scale, v_w_branch_a, v_w_branch_b, v_w_out, v_norm_ffn, v_w_ffn_gate, v_w_ffn_up, v_w_ffn_down, v_norm_ple, v_w_ple_gate, v_w_ple_proj, v_norm_final):
    given = dict(locals())
    names = BIG + SMALL
    order = ("norm_mix", "w_in", "w_pool", "pool_scale", "w_branch_a", "w_branch_b", "w_out", "norm_ffn", "w_ffn_gate",
             "w_ffn_up", "w_ffn_down", "norm_ple", "w_ple_gate", "w_ple_proj", "norm_final")
    t, d = x.shape[1], x.shape[2]
    shard = {n: given[n][0] for n in BIG}
    small = {"norm_mix": norm_mix, "w_pool": w_pool[0], "pool_scale": pool_scale, "norm_ffn": norm_ffn,
             "norm_ple": norm_ple, "norm_final": norm_final}

    as_bf16 = {n: shard[n].astype(BF16) for n in BIG}

    place = jnp.stack([lax.axis_index("c"), 2 * lax.axis_index("x") + lax.axis_index("y")]).astype(jnp.int32)
    pair_sums = {}

    def exchange_early(ready):
        theirs = _pair_exchange("pair_exchange_early", [ready[n] for n in LATE])
        for n, other in zip(LATE, theirs):
            pair_sums[n] = _pair_sum(f"pair_sum_{n}", place, ready[n], other)
        return _ChipExchange([pair_sums[n] for n in LATE])

    def exchange_last(g_w_in):
        theirs, = _pair_exchange("pair_exchange_w_in", [g_w_in])
        pair_sums["w_in"] = _pair_sum("pair_sum_w_in", place, g_w_in, theirs)
        return _ChipExchange([pair_sums["w_in"]])

    grad_x, big_g, small_g, loss_row, early, last = _local_step(
        x.reshape(t, d), p.reshape(t, p.shape[-1]), loss_target.reshape(t, d), {}, small,
        gather_first=_WeightGather([as_bf16["w_in"]]),
        gather_late=_WeightGather([as_bf16[n] for n in LATE]), exchange_early=exchange_early,
        exchange_last=exchange_last)
    landed = dict(zip(LATE + ("w_in",), tuple(early) + tuple(last)))
    halves = [_sum_chips(f"chip_sum_{n}", place, pair_sums[n], landed[n]) for n in BIG]
    grads = dict(zip(BIG, _pair_share(halves)))

    packed, layout = _pack_small(small_g, loss_row)
    reduced = _all_reduce_small(packed)
    for name, start, rows in layout:
        if name == "loss":
            loss = jnp.sum(reduced[start:start + rows])
        else:
            n_el = small[name].size
            grads[name] = reduced[start:start + rows].reshape(-1)[:n_el]

    deltas, new_m, new_v = {}, {}, {}
    for n in order:
        w = shard[n] if n in BIG else small[n]
        shape2 = w.shape if w.ndim == 2 else ((1, w.shape[0]) if w.ndim == 1 else (w.shape[0] * w.shape[1], w.shape[2]))
        g2 = grads[n].reshape(shape2)
        dl, mn, vn = _adamw(f"adamw_{n}", w.reshape(shape2), g2, given["m_" + n].reshape(shape2),
                            given["v_" + n].reshape(shape2))
        full = given[n].shape
        grads[n], deltas[n], new_m[n], new_v[n] = g2.reshape(full), dl.reshape(full), mn.reshape(full), vn.reshape(full)

    return (loss, grad_x.reshape(x.shape), *[grads[n] for n in order], *[deltas[n] for n in order],
            *[new_m[n] for n in order], *[new_v[n] for n in order])
```

```python
import functools
import math

import jax
import jax.numpy as jnp
from jax import lax
from jax.experimental import pallas as pl
from jax.experimental.pallas import tpu as pltpu

F32 = jnp.float32
BF16 = jnp.bfloat16
MESH = pl.DeviceIdType.MESH

RMS_EPS = 1e-6
POOL_WINDOWS = (2, 4, 8, 16)
POOL_HALO = 16
HEAD_DIM = 64
LANES = 128
ATT_BLOCK = 256
ATT_CHAINS = 2
ATT_CHUNK = 256
ATT_SLAB = 256
ATT_SCALE = 1.0 / math.sqrt(HEAD_DIM)
LOG2_E = 1.4426950408889634
ATT_EXIT_BELOW = -150.5
ADAM_LR, ADAM_B1, ADAM_B2, ADAM_EPS, ADAM_WD, ADAM_STEP = 0.001, 0.9, 0.999, 1e-08, 0.01, 10
V7X_VMEM_LIMIT_BYTES = 56 * 1024 * 1024
N_CHIPS = 4
N_DEV = 8


def _params(*semantics):
    return pltpu.CompilerParams(dimension_semantics=semantics, vmem_limit_bytes=V7X_VMEM_LIMIT_BYTES)


def _sigmoid(z):
    return 1.0 / (1.0 + jnp.exp(-z))


def _tiled_spec(shape, tm, tn, n_total, at):
    rows, width = shape
    if rows == 1:
        if width == n_total:
            return pl.BlockSpec((1, tn), at(lambda i, j: (0, j)))
        return pl.BlockSpec((1, width), at(lambda i, j: (0, 0)))
    if width == n_total:
        return pl.BlockSpec((tm, tn), at(lambda i, j: (i, j)))
    assert tn == n_total, "an operand narrower than the output needs whole output rows per tile"
    return pl.BlockSpec((tm, width), at(lambda i, j: (i, 0)))


def _column_pieces(operands):
    pieces = [tuple(a) if isinstance(a, (tuple, list)) else (a,) for a in operands]
    return [p for ps in pieces for p in ps], [len(ps) for ps in pieces]


def _load_bf16(refs, counts):
    tiles, k = [], 0
    for n in counts:
        parts = [r[...] for r in refs[k:k + n]]
        parts = [t if t.dtype == BF16 else t.astype(BF16) for t in parts]
        tiles.append(parts[0] if n == 1 else jnp.concatenate(parts, axis=1))
        k += n
    return tiles


def _mm(name, a_list, b_list, mode, out_shapes, epilogue=None, extras=(), tm=1024, tn=None, separate=False,
        sum_shapes=(), rider=None, wholes=()):
    flat_a, counts = _column_pieces(a_list)
    m_total = flat_a[0].shape[0]
    n_total = b_list[0].shape[1] if mode == "nn" else b_list[0].shape[0]
    tn = n_total if tn is None else tn
    tm = min(tm, m_total)
    assert m_total % tm == 0 and n_total % tn == 0 and (not sum_shapes or tn == n_total)
    n_a, n_b, n_extra, n_out = len(counts), len(b_list), len(extras), len(out_shapes)
    assert n_a in (1, n_b)
    dims = (((1,), (0,)), ((), ())) if mode == "nn" else (((1,), (1,)), ((), ()))
    rider = rider or _NoRider()
    grid = (n_total // tn, m_total // tm)

    def at(index):
        return lambda j, i: index(i, j)

    def body(*refs):
        ins, o_refs, _, riding = rider.split(refs, len(flat_a) + n_b + n_extra + len(wholes), n_out + len(sum_shapes))
        a_refs, b_refs = ins[:len(flat_a)], ins[len(flat_a):len(flat_a) + n_b]
        e_refs, w_refs = ins[len(flat_a) + n_b:len(flat_a) + n_b + n_extra], ins[len(flat_a) + n_b + n_extra:]
        at_first = (pl.program_id(0) == 0) & (pl.program_id(1) == 0)
        at_last = (pl.program_id(0) == grid[0] - 1) & (pl.program_id(1) == grid[1] - 1)
        top, bottom = rider.at_steps(riding, at_first, at_first, at_last)
        top()
        lefts = _load_bf16(a_refs, counts)
        products = [lax.dot_general(lefts[s % n_a], b_refs[s][...], dims, preferred_element_type=F32)
                    for s in range(n_b)]
        if not separate:
            products = [functools.reduce(lambda p, r: p + r, products)]
        extra_tiles = [e[...].astype(F32) for e in e_refs]
        outs = products if epilogue is None else epilogue(*products, *extra_tiles, *[w[...] for w in w_refs])
        for o_ref, o in zip(o_refs[:n_out], outs[:n_out]):
            o_ref[...] = o.astype(o_ref.dtype)
        if sum_shapes:
            @pl.when(pl.program_id(1) == 0)
            def _():
                for s_ref in o_refs[n_out:]:
                    s_ref[...] = jnp.zeros_like(s_ref)

            for s_ref, s in zip(o_refs[n_out:], outs[n_out:]):
                s_ref[...] += s
        bottom()

    once = dict(pipeline_mode=pl.Buffered(1)) if tn == n_total else {}
    in_specs = [pl.BlockSpec((tm, a.shape[1]), at(lambda i, j: (i, 0))) for a in flat_a]
    if mode == "nn":
        in_specs += [pl.BlockSpec((b.shape[0], tn), at(lambda i, j: (0, j)), **once) for b in b_list]
    else:
        in_specs += [pl.BlockSpec((tn, b.shape[1]), at(lambda i, j: (j, 0)), **once) for b in b_list]
    in_specs += [_tiled_spec(e.shape, tm, tn, n_total, at) for e in extras]
    in_specs += [pl.BlockSpec(w.shape, lambda j, i: (0, 0), pipeline_mode=pl.Buffered(1)) for w in wholes]
    out_specs = [_tiled_spec(o.shape, tm, tn, n_total, at) for o in out_shapes]
    out_specs += [pl.BlockSpec(s.shape, at(lambda i, j: (0, 0))) for s in sum_shapes]
    semantics = ("arbitrary", "arbitrary") if sum_shapes or rider.operands else ("parallel", "parallel")
    res = pl.pallas_call(
        body, name=name, grid=grid, in_specs=in_specs + [ANY] * len(rider.operands),
        out_specs=out_specs + [ANY] * len(rider.out_shapes),
        out_shape=list(out_shapes) + list(sum_shapes) + list(rider.out_shapes), scratch_shapes=list(rider.scratch),
        compiler_params=_params(*semantics),
    )(*flat_a, *b_list, *extras, *wholes, *rider.operands)
    n_own = len(out_shapes) + len(sum_shapes)
    return res if not rider.operands else (res[:n_own], res[n_own:])


def _mm_tn(name, a_list, b_list, tmm=1024, stacked=False, n_blocks=1):
    flat_b, counts = _column_pieces(b_list)
    n_a = len(a_list)
    m_total = a_list[0].shape[0]
    ks = [a_list[s % n_a].shape[1] for s in range(len(counts))]
    widths = [sum(p.shape[1] for p in flat_b[sum(counts[:s]):sum(counts[:s + 1])]) for s in range(len(counts))]
    tmm = min(tmm, m_total)
    assert m_total % tmm == 0 and (n_blocks == 1 or max(counts) == 1) and all(w % n_blocks == 0 for w in widths)
    n_b = len(counts)
    assert n_a in (1, n_b) and not (stacked and n_a > 1)

    def body(*refs):
        a_refs, b_refs, o_refs = refs[:n_a], refs[n_a:n_a + len(flat_b)], refs[n_a + len(flat_b):]

        @pl.when(pl.program_id(1) == 0)
        def _():
            for o_ref in o_refs:
                o_ref[...] = jnp.zeros_like(o_ref)

        lefts = _load_bf16(a_refs, [1] * n_a)
        for s, bv in enumerate(_load_bf16(b_refs, counts)):
            product = lax.dot_general(lefts[s % n_a], bv, (((0,), (0,)), ((), ())), preferred_element_type=F32)
            if stacked:
                o_refs[0][s] += product
            else:
                o_refs[s][...] += product

    in_specs = [pl.BlockSpec((tmm, a.shape[1]), lambda nb, m: (m, 0)) for a in a_list]
    in_specs += [pl.BlockSpec((tmm, b.shape[1] // n_blocks), lambda nb, m: (m, nb)) for b in flat_b]
    if stacked:
        out_shape = [jax.ShapeDtypeStruct((n_b, ks[0], widths[0]), F32)]
        out_specs = [pl.BlockSpec((n_b, ks[0], widths[0] // n_blocks), lambda nb, m: (0, 0, nb))]
    else:
        out_shape = [jax.ShapeDtypeStruct((k, w), F32) for k, w in zip(ks, widths)]
        out_specs = [pl.BlockSpec((k, w // n_blocks), lambda nb, m: (0, nb)) for k, w in zip(ks, widths)]
    return pl.pallas_call(
        body, name=name, grid=(n_blocks, m_total // tmm), in_specs=in_specs, out_specs=out_specs, out_shape=out_shape,
        compiler_params=_params("arbitrary", "arbitrary"),
    )(*a_list, *flat_b)


def _rows(name, fn, ins, tile_outs, sum_outs=(), tr=512, rider=None):
    t_total = max(a.shape[0] for a in ins)
    tr = min(tr, t_total)
    assert t_total % tr == 0
    n_in, n_tile = len(ins), len(tile_outs)
    rider = rider or _NoRider()
    n_steps = t_total // tr

    def body(*refs):
        own_ins, own_outs, _, riding = rider.split(refs, n_in, n_tile + len(sum_outs))
        step = pl.program_id(0)
        top, bottom = rider.at_steps(riding, step == 0, step == n_steps - 1, step == n_steps - 1)
        top()
        refs = tuple(own_ins) + tuple(own_outs)
        outs = fn(*[r[...].astype(F32) for r in refs[:n_in]])
        for o_ref, o in zip(refs[n_in:n_in + n_tile], outs[:n_tile]):
            o_ref[...] = o.astype(o_ref.dtype)
        if sum_outs:
            @pl.when(pl.program_id(0) == 0)
            def _():
                for s_ref in refs[n_in + n_tile:]:
                    s_ref[...] = jnp.zeros_like(s_ref)

            for s_ref, s in zip(refs[n_in + n_tile:], outs[n_tile:]):
                s_ref[...] += s
        bottom()

    def spec(shape):
        if shape[0] == 1:
            return pl.BlockSpec(shape, lambda i: (0, 0))
        return pl.BlockSpec((tr, shape[1]), lambda i: (i, 0))

    return pl.pallas_call(
        body, name=name, grid=(n_steps,), in_specs=[spec(a.shape) for a in ins] + [ANY] * len(rider.operands),
        out_specs=[spec(o.shape) for o in tile_outs] + [spec(s.shape) for s in sum_outs] + [ANY] * len(rider.out_shapes),
        out_shape=list(tile_outs) + list(sum_outs) + list(rider.out_shapes), scratch_shapes=list(rider.scratch),
        compiler_params=_params("arbitrary" if sum_outs or rider.operands else "parallel"),
    )(*ins, *rider.operands)


def _norm_fwd(name, x, gain, rider=None):
    def fn(xv, g):
        inv = lax.rsqrt(jnp.mean(xv * xv, axis=-1, keepdims=True) + RMS_EPS)
        return (xv * inv * g,)

    res = _rows(name, fn, [x, gain], [jax.ShapeDtypeStruct(x.shape, BF16)], rider=rider)
    return res[0], res[1:]


def _rms_norm_bwd(dh, xv, g):
    inv = lax.rsqrt(jnp.mean(xv * xv, axis=-1, keepdims=True) + RMS_EPS)
    xn = xv * inv
    dxn = dh * g
    return inv * (dxn - xn * jnp.mean(dxn * xn, axis=-1, keepdims=True)), jnp.sum(dh * xn, axis=0, keepdims=True)


def _ple_and_loss(gv, pv, x2v, tv, g_final, g_ple, w_pg):
    d = x2v.shape[1]
    s = _sigmoid(gv)
    xv = x2v + s * pv
    inv = lax.rsqrt(jnp.mean(xv * xv, axis=-1, keepdims=True) + RMS_EPS)
    err = xv * inv * g_final - tv
    dx3, d_final = _rms_norm_bwd(err * (1.0 / d), xv, g_final)
    d_pp, d_gp = dx3 * s, dx3 * pv * s * (1.0 - s)
    dh3 = lax.dot_general(d_gp.astype(BF16), w_pg, (((1,), (1,)), ((), ())), preferred_element_type=F32)
    dx2, d_ple = _rms_norm_bwd(dh3, x2v, g_ple)
    dx2 = dx2 + dx3
    return dx2, dx2, d_pp, d_gp, d_final, (0.5 / d) * jnp.sum(err * err, axis=0, keepdims=True), d_ple


def _window_counts(t_pos, w):
    return jnp.minimum(t_pos + 1, w).astype(F32)


def _pool_fwd(u, w_pool, scale, tr=512):
    t_total, width = u.shape
    tr = min(tr, t_total)
    n_groups = len(POOL_WINDOWS)
    gdim = width // n_groups
    ext = tr + POOL_HALO

    def body(u_ref, halo_ref, w_ref, s_ref, pooled_ref, ya_ref):
        i = pl.program_id(0)
        halo = jnp.where(i == 0, 0.0, halo_ref[...])
        t_pos = i * tr + lax.broadcasted_iota(jnp.int32, (tr, 1), 0)
        for g, w in enumerate(POOL_WINDOWS):
            cols = slice(g * gdim, (g + 1) * gdim)
            main = u_ref[:, cols]
            win = jnp.concatenate([halo[:, cols], main], axis=0)
            span = 1
            while span < w:
                win = win + pltpu.roll(win, span, 0)
                span *= 2
            pooled = win[POOL_HALO:, :] * (1.0 / _window_counts(t_pos, w)) - main
            pooled_b = pooled.astype(BF16)
            pooled_ref[:, cols] = pooled_b
            mixed = jnp.dot(pooled_b, w_ref[g], preferred_element_type=F32)
            ya_ref[:, cols] = (mixed * s_ref[:, cols]).astype(BF16)

    hb = tr // POOL_HALO
    return pl.pallas_call(
        body, name="pool_fwd", grid=(t_total // tr,),
        in_specs=[pl.BlockSpec((tr, width), lambda i: (i, 0)),
                  pl.BlockSpec((POOL_HALO, width), lambda i: (jnp.maximum(i * hb - 1, 0), 0)),
                  pl.BlockSpec((n_groups, gdim, gdim), lambda i: (0, 0, 0)),
                  pl.BlockSpec((1, width), lambda i: (0, 0))],
        out_specs=[pl.BlockSpec((tr, width), lambda i: (i, 0)), pl.BlockSpec((tr, width), lambda i: (i, 0))],
        out_shape=[jax.ShapeDtypeStruct(u.shape, BF16), jax.ShapeDtypeStruct(u.shape, BF16)],
        compiler_params=_params("parallel"),
    )(u, u, w_pool, scale)


def _pool_bwd(dya, pooled, w_pool, scale, tr=512):
    t_total, width = dya.shape
    tr = min(tr, t_total)
    n_groups = len(POOL_WINDOWS)
    gdim = width // n_groups
    ext = tr + POOL_HALO
    n_tiles = t_total // tr

    def body(d_ref, halo_ref, p_ref, w_ref, s_ref, du_ref, dw_ref, ds_ref):
        i = pl.program_id(0)

        @pl.when(i == 0)
        def _():
            dw_ref[...] = jnp.zeros_like(dw_ref)
            ds_ref[...] = jnp.zeros_like(ds_ref)

        halo = jnp.where(i == n_tiles - 1, 0.0, halo_ref[...])
        t_pos = i * tr + lax.broadcasted_iota(jnp.int32, (ext, 1), 0)
        for g, w in enumerate(POOL_WINDOWS):
            cols = slice(g * gdim, (g + 1) * gdim)
            sc = s_ref[:, cols]
            d_main = d_ref[:, cols]
            pooled_b = p_ref[:, cols]
            mixed = jnp.dot(pooled_b, w_ref[g], preferred_element_type=F32)
            ds_ref[:, cols] += jnp.sum(d_main * mixed, axis=0, keepdims=True)
            dmix = (jnp.concatenate([d_main, halo[:, cols]], axis=0) * sc).astype(BF16)
            dw_ref[g] += lax.dot_general(pooled_b, dmix[:tr, :], (((0,), (0,)), ((), ())),
                                         preferred_element_type=F32)
            dpool = lax.dot_general(dmix, w_ref[g], (((1,), (1,)), ((), ())), preferred_element_type=F32)
            win = dpool * (1.0 / _window_counts(t_pos, w))
            span = 1
            while span < w:
                win = win + pltpu.roll(win, ext - span, 0)
                span *= 2
            du_ref[:, cols] = (win[:tr, :] - dpool[:tr, :]).astype(BF16)

    hb = tr // POOL_HALO
    last_halo = t_total // POOL_HALO - 1
    return pl.pallas_call(
        body, name="pool_bwd", grid=(n_tiles,),
        in_specs=[pl.BlockSpec((tr, width), lambda i: (i, 0)),
                  pl.BlockSpec((POOL_HALO, width), lambda i: (jnp.minimum((i + 1) * hb, last_halo), 0)),
                  pl.BlockSpec((tr, width), lambda i: (i, 0)),
                  pl.BlockSpec((n_groups, gdim, gdim), lambda i: (0, 0, 0)),
                  pl.BlockSpec((1, width), lambda i: (0, 0))],
        out_specs=[pl.BlockSpec((tr, width), lambda i: (i, 0)),
                   pl.BlockSpec((n_groups, gdim, gdim), lambda i: (0, 0, 0)),
                   pl.BlockSpec((1, width), lambda i: (0, 0))],
        out_shape=[jax.ShapeDtypeStruct(dya.shape, BF16), jax.ShapeDtypeStruct((n_groups, gdim, gdim), F32),
                   jax.ShapeDtypeStruct((1, width), F32)],
        compiler_params=_params("arbitrary"),
    )(dya, dya, pooled, w_pool, scale)


def _head_masks():
    lane = lax.broadcasted_iota(jnp.int32, (1, LANES), 1)
    return lane < HEAD_DIM


def _stack_heads(tile, first):
    zero = jnp.zeros_like(tile)
    return jnp.concatenate([jnp.where(first, tile, zero), jnp.where(first, zero, tile)], axis=0)


def _causal_mask(t_pos, k_start):
    col = lax.broadcasted_iota(jnp.int32, (1, 2 * ATT_SLAB), 1)
    return k_start + (col & (ATT_SLAB - 1)) < t_pos


def _slab_scores(q, kd, mask):
    z2 = lax.dot_general(q, kd, (((1,), (1,)), ((), ())), preferred_element_type=F32) * LOG2_E
    log_hit = jnp.minimum(z2, 0.0) - jnp.log2(1.0 + jnp.exp2(-jnp.abs(z2)))
    log_fail = log_hit - z2
    return log_hit, (log_fail if mask is None else jnp.where(mask, log_fail, 0.0))


def _weights(log_hit, suffix, mask):
    arg = log_hit + suffix
    return jnp.exp2(arg if mask is None else jnp.where(mask, arg, -1e30))


def _tri(upper):
    r = lax.broadcasted_iota(jnp.int32, (ATT_CHUNK, ATT_CHUNK), 0)
    c = lax.broadcasted_iota(jnp.int32, (ATT_CHUNK, ATT_CHUNK), 1)
    return jnp.where(r > c if upper else r < c, 1.0, 0.0).astype(BF16)


def _tri_spec():
    return pl.BlockSpec((ATT_CHUNK, ATT_CHUNK), lambda h, i: (0, 0), pipeline_mode=pl.Buffered(1))


def _scan_chunk(v, tri):
    return jnp.dot(v.astype(BF16), tri, preferred_element_type=F32)


def _lane_bcast(col):
    return jnp.broadcast_to(col, (col.shape[0], LANES))


def _scan_slab(v, tri, carries, from_right):
    n_chunks = ATT_SLAB // ATT_CHUNK
    edge = 0 if from_right else ATT_CHUNK - 1
    parts, new_carries = [None] * (2 * n_chunks), []
    for head in range(2):
        run = carries[head]
        for c in (reversed(range(n_chunks)) if from_right else range(n_chunks)):
            lo_col = head * ATT_SLAB + c * ATT_CHUNK
            vc = v[:, lo_col:lo_col + ATT_CHUNK]
            sc = _scan_chunk(vc, tri)
            parts[head * n_chunks + c] = sc + jnp.concatenate([run] * (ATT_CHUNK // LANES), axis=1)
            run = run + _lane_bcast(sc[:, edge:edge + 1] + vc[:, edge:edge + 1])
        new_carries.append(run)
    return jnp.concatenate(parts, axis=1), new_carries


def _fold_heads(stacked, first):
    s = stacked.shape[0] // 2
    return jnp.where(first, stacked[:s], stacked[s:])


class _NoRider:
    operands, out_shapes, scratch = (), (), ()

    def split(self, refs, n_base_in, n_base_out):
        n_in, n_out, n_sem = len(self.operands), len(self.out_shapes), len(self.scratch)
        a = n_base_in + n_in
        b = a + n_base_out + n_out
        mine = (refs[n_base_in:a], refs[a + n_base_out:b], refs[b:b + n_sem])
        return refs[:n_base_in], refs[a:a + n_base_out], refs[b + n_sem:], mine

    def start(self, ins, outs, sems):
        pass

    def relay(self, ins, outs, sems):
        pass

    def finish(self, ins, outs, sems):
        pass

    def at_steps(self, refs, first_step, relay_step, last_step):
        if not self.operands:
            return (lambda: None), (lambda: None)

        def top():
            pl.when(first_step)(lambda: self.start(*refs))
            pl.when(relay_step)(lambda: self.relay(*refs))

        return top, lambda: pl.when(last_step)(lambda: self.finish(*refs))


def _attn_fwd(q_src, q_col, kv_src, k_col, v_col, n_pairs=4, rider=_NoRider()):
    t_total = q_src.shape[0]
    blk = ATT_BLOCK
    n_steps = t_total // (ATT_CHAINS * blk)
    assert t_total % ATT_SLAB == 0 and ATT_SLAB == ATT_BLOCK

    def body(*refs):
        (q_ref, k_ref, v_ref, suffix_ref), (o_ref,), _, riding = rider.split(refs, 4, 1)
        h, ii = pl.program_id(0), pl.program_id(1)
        top, bottom = rider.at_steps(riding, (h == 0) & (ii == 0), (h == n_pairs - 1) & (ii == 0),
                                     (h == n_pairs - 1) & (ii == n_steps - 1))
        top()
        first = _head_masks()
        suffix_tri = suffix_ref[...]
        blocks = [ATT_CHAINS * ii + c for c in range(ATT_CHAINS)]
        qs = [q_ref[c * blk:(c + 1) * blk, :] * ATT_SCALE for c in range(ATT_CHAINS)]
        t_pos = [b * blk + lax.broadcasted_iota(jnp.int32, (blk, 1), 0) for b in blocks]

        def one(c, t, chain, on_diagonal):
            _, acc, right_a, right_b = chain
            k_start = pl.multiple_of((blocks[c] - t) * ATT_SLAB, ATT_SLAB)
            kd = _stack_heads(k_ref[pl.ds(k_start, ATT_SLAB), :], first)
            vd = _stack_heads(v_ref[pl.ds(k_start, ATT_SLAB), :], first)
            mask = _causal_mask(t_pos[c], k_start) if on_diagonal else None
            log_hit, log_fail = _slab_scores(qs[c], kd, mask)
            suffix, (right_a, right_b) = _scan_slab(log_fail, suffix_tri, (right_a, right_b), from_right=True)
            a = _weights(log_hit, suffix, mask).astype(BF16)
            acc = acc + jnp.dot(a, vd, preferred_element_type=F32)
            return jnp.max(jnp.maximum(right_a, right_b)), acc, right_a, right_b

        def step(state, on_diagonal):
            t, chains = state
            return t + 1, tuple(one(c, t, chains[c], on_diagonal) for c in range(ATT_CHAINS))

        def more(state):
            t, chains = state
            return (t <= blocks[0]) & (functools.reduce(jnp.maximum, [ch[0] for ch in chains]) > ATT_EXIT_BELOW)

        zero = jnp.zeros((blk, LANES), F32)
        state = step((0, ((jnp.float32(0.0), zero, zero, zero),) * ATT_CHAINS), on_diagonal=True)
        t, chains = lax.while_loop(more, functools.partial(step, on_diagonal=False), state)
        for c in range(ATT_CHAINS):
            chain = chains[c]
            if c:
                _, chain = lax.while_loop(
                    lambda s, c=c: (s[0] <= blocks[c]) & (s[1][0] > ATT_EXIT_BELOW),
                    lambda s, c=c: (s[0] + 1, one(c, s[0], s[1], False)), (t, chain))
            o_ref[c * blk:(c + 1) * blk, :] = chain[1].astype(BF16)
        bottom()

    rows = ATT_CHAINS * blk
    res = pl.pallas_call(
        body, name="attn_fwd", grid=(n_pairs, n_steps),
        in_specs=[pl.BlockSpec((rows, LANES), lambda h, i: (i, q_col + h)),
                  pl.BlockSpec((t_total, LANES), lambda h, i: (0, k_col + h)),
                  pl.BlockSpec((t_total, LANES), lambda h, i: (0, v_col + h)), _tri_spec()] + [ANY] * len(rider.operands),
        out_specs=[pl.BlockSpec((rows, LANES), lambda h, i: (i, h))] + [ANY] * len(rider.out_shapes),
        out_shape=[jax.ShapeDtypeStruct((t_total, n_pairs * LANES), BF16)] + list(rider.out_shapes),
        scratch_shapes=list(rider.scratch),
        compiler_params=_params("arbitrary", "arbitrary"),
    )(q_src, kv_src, kv_src, _tri(upper=True), *rider.operands)
    return res[0], res[1:]


def _attn_bwd(q_src, q_col, kv_src, k_col, v_col, dy, n_pairs=4, rider=_NoRider()):
    t_total = q_src.shape[0]
    blk = ATT_BLOCK
    n_steps = t_total // (ATT_CHAINS * blk)
    n_slabs = t_total // ATT_SLAB
    assert t_total % ATT_SLAB == 0 and ATT_SLAB == ATT_BLOCK

    def body(*refs):
        ins, (dq_ref, dk_ref, dv_ref), (g_s, dk_acc, dv_acc), riding = rider.split(refs, 6, 3)
        q_ref, dy_ref, k_ref, v_ref, suffix_ref, prefix_ref = ins
        h, ii = pl.program_id(0), pl.program_id(1)
        top, bottom = rider.at_steps(riding, (h == 0) & (ii == 0), (h == n_pairs - 1) & (ii == 0),
                                     (h == n_pairs - 1) & (ii == n_steps - 1))
        top()

        @pl.when(ii == 0)
        def _():
            dk_acc[...] = jnp.zeros_like(dk_acc)
            dv_acc[...] = jnp.zeros_like(dv_acc)

        first = _head_masks()
        suffix_tri = suffix_ref[...]
        prefix_tri = prefix_ref[...]
        blocks = [ATT_CHAINS * ii + c for c in range(ATT_CHAINS)]
        rows = [slice(c * blk, (c + 1) * blk) for c in range(ATT_CHAINS)]
        qs = [q_ref[r, :] * ATT_SCALE for r in rows]
        dys = [dy_ref[r, :] for r in rows]
        t_pos = [b * blk + lax.broadcasted_iota(jnp.int32, (blk, 1), 0) for b in blocks]

        def one1(c, t, chain, on_diagonal):
            _, right_a, right_b = chain
            slab = blocks[c] - t
            k_start = pl.multiple_of(slab * ATT_SLAB, ATT_SLAB)
            kd = _stack_heads(k_ref[pl.ds(k_start, ATT_SLAB), :], first)
            vd = _stack_heads(v_ref[pl.ds(k_start, ATT_SLAB), :], first)
            mask = _causal_mask(t_pos[c], k_start) if on_diagonal else None
            log_hit, log_fail = _slab_scores(qs[c], kd, mask)
            suffix, (right_a, right_b) = _scan_slab(log_fail, suffix_tri, (right_a, right_b), from_right=True)
            a = _weights(log_hit, suffix, mask)
            da = lax.dot_general(dys[c], vd, (((1,), (1,)), ((), ())), preferred_element_type=F32)
            g_s[c, slab] = (da * a).astype(BF16)
            dv_acc[pl.ds(k_start, ATT_SLAB), :] += _fold_heads(lax.dot_general(
                a.astype(BF16), dys[c], (((0,), (0,)), ((), ())), preferred_element_type=F32), first)
            return jnp.max(jnp.maximum(right_a, right_b)), right_a, right_b

        def step1(state, on_diagonal):
            t, chains = state
            return t + 1, tuple(one1(c, t, chains[c], on_diagonal) for c in range(ATT_CHAINS))

        def more(state):
            t, chains = state
            return (t <= blocks[0]) & (functools.reduce(jnp.maximum, [ch[0] for ch in chains]) > ATT_EXIT_BELOW)

        zero = jnp.zeros((blk, LANES), F32)
        state = step1((0, ((jnp.float32(0.0), zero, zero),) * ATT_CHAINS), on_diagonal=True)
        joint, chains = lax.while_loop(more, functools.partial(step1, on_diagonal=False), state)
        done = [joint]
        for c in range(1, ATT_CHAINS):
            done.append(lax.while_loop(
                lambda s, c=c: (s[0] <= blocks[c]) & (s[1][0] > ATT_EXIT_BELOW),
                lambda s, c=c: (s[0] + 1, one1(c, s[0], s[1], False)), (joint, chains[c]))[0])

        def one2(c, t, carry, on_diagonal):
            dq, left_a, left_b = carry
            slab = blocks[c] - t
            k_start = pl.multiple_of(slab * ATT_SLAB, ATT_SLAB)
            kd = _stack_heads(k_ref[pl.ds(k_start, ATT_SLAB), :], first)
            g = g_s[c, slab]
            z2 = lax.dot_general(qs[c], kd, (((1,), (1,)), ((), ())), preferred_element_type=F32) * LOG2_E
            sig = 1.0 / (1.0 + jnp.exp2(-z2))
            prefix, (left_a, left_b) = _scan_slab(g, prefix_tri, (left_a, left_b), from_right=False)
            dz = g * (1.0 - sig) - sig * prefix
            if on_diagonal:
                dz = jnp.where(_causal_mask(t_pos[c], k_start), dz, 0.0)
            dz = dz.astype(BF16)
            dq = dq + jnp.dot(dz, kd, preferred_element_type=F32)
            dk_acc[pl.ds(k_start, ATT_SLAB), :] += _fold_heads(lax.dot_general(
                dz, qs[c], (((0,), (0,)), ((), ())), preferred_element_type=F32), first)
            return dq, left_a, left_b

        carries = [(zero, zero, zero)]
        for c in range(1, ATT_CHAINS):
            carries.append(lax.fori_loop(
                0, done[c] - joint, lambda n, carry, c=c: one2(c, done[c] - 1 - n, carry, False), (zero, zero, zero)))
        carries = lax.fori_loop(
            0, joint - 1,
            lambda n, cs: tuple(one2(c, joint - 1 - n, cs[c], False) for c in range(ATT_CHAINS)), tuple(carries))
        for c in range(ATT_CHAINS):
            dq_ref[rows[c], :] = (one2(c, 0, carries[c], True)[0] * ATT_SCALE).astype(BF16)

        @pl.when(ii == n_steps - 1)
        def _():
            dk_ref[...] = dk_acc[...].astype(BF16)
            dv_ref[...] = dv_acc[...].astype(BF16)

        bottom()

    out = jax.ShapeDtypeStruct((t_total, n_pairs * LANES), BF16)
    n_rows = ATT_CHAINS * blk
    whole = dict(pipeline_mode=pl.Buffered(1))
    res = pl.pallas_call(
        body, name="attn_bwd", grid=(n_pairs, n_steps),
        in_specs=[pl.BlockSpec((n_rows, LANES), lambda h, i: (i, q_col + h)),
                  pl.BlockSpec((n_rows, LANES), lambda h, i: (i, h)),
                  pl.BlockSpec((t_total, LANES), lambda h, i: (0, k_col + h), **whole),
                  pl.BlockSpec((t_total, LANES), lambda h, i: (0, v_col + h), **whole), _tri_spec(), _tri_spec()]
        + [ANY] * len(rider.operands),
        out_specs=[pl.BlockSpec((n_rows, LANES), lambda h, i: (i, h)),
                   pl.BlockSpec((t_total, LANES), lambda h, i: (0, h)),
                   pl.BlockSpec((t_total, LANES), lambda h, i: (0, h))] + [ANY] * len(rider.out_shapes),
        out_shape=[out, out, out] + list(rider.out_shapes),
        scratch_shapes=list(rider.scratch) + [pltpu.VMEM((ATT_CHAINS, n_slabs, blk, 2 * ATT_SLAB), BF16),
                                              pltpu.VMEM((t_total, LANES), F32), pltpu.VMEM((t_total, LANES), F32)],
        compiler_params=_params("arbitrary", "arbitrary"),
    )(q_src, dy, kv_src, kv_src, _tri(upper=True), _tri(upper=False), *rider.operands)
    return res[:3], res[3:]


def _adamw(name, w, g, m, v):
    def fn(wv, gv, mv, vv):
        mn = ADAM_B1 * mv + (1.0 - ADAM_B1) * gv
        vn = ADAM_B2 * vv + (1.0 - ADAM_B2) * (gv * gv)
        m_hat = mn / (1.0 - ADAM_B1 ** ADAM_STEP)
        v_hat = vn / (1.0 - ADAM_B2 ** ADAM_STEP)
        return -ADAM_LR * (m_hat / (jnp.sqrt(v_hat) + ADAM_EPS) + ADAM_WD * wv), mn, vn

    rows = w.shape[0]
    tr = _row_tile(rows)
    shp = jax.ShapeDtypeStruct(w.shape, F32)
    if rows == 1:
        def body(w_ref, g_ref, m_ref, v_ref, d_ref, mo_ref, vo_ref):
            d, mn, vn = fn(w_ref[...], g_ref[...], m_ref[...], v_ref[...])
            d_ref[...], mo_ref[...], vo_ref[...] = d, mn, vn

        return pl.pallas_call(body, name=name, out_shape=[shp, shp, shp])(w, g, m, v)
    return _rows(name, fn, [w, g, m, v], [shp, shp, shp], tr=tr)


def _place():
    return lax.axis_index("x"), lax.axis_index("y"), lax.axis_index("c")


def _other_chips(x, y):
    return [(1 - x, y), (x, 1 - y), (1 - x, 1 - y)]


ANY = pl.BlockSpec(memory_space=pl.ANY)


def _remote(src, dst, send_sem, recv_sem, to):
    return pltpu.make_async_remote_copy(src_ref=src, dst_ref=dst, send_sem=send_sem, recv_sem=recv_sem,
                                        device_id=to, device_id_type=MESH)


class _WeightGather(_NoRider):
    def __init__(self, shards):
        n_w = len(shards)
        self.operands = list(shards)
        self.out_shapes = [jax.ShapeDtypeStruct((N_CHIPS,) + s.shape, s.dtype) for s in shards]
        self.scratch = [pltpu.SemaphoreType.DMA((3, n_w))] * 4 + [pltpu.SemaphoreType.DMA((n_w,))] * 2

    def _copies(self, ins, outs, sems):
        send_sems, recv_sems, relay_send, relay_recv, own_send, own_recv = sems
        x, y, c = _place()
        my_chip, sibling = 2 * x + y, (x, y, 1 - c)
        n_w = len(ins)

        def half(w, chip, core):
            h = self.operands[w].shape[0] // 2
            return outs[w].at[chip, pl.ds(core * h, h)]

        own = [_remote(ins[w], outs[w].at[my_chip], own_send.at[w], own_recv.at[w], sibling) for w in range(n_w)]
        sends, landed, relays, relayed = [], [], [], []
        for p, (ox, oy) in enumerate(_other_chips(x, y)):
            for w in range(n_w):
                h = self.operands[w].shape[0] // 2
                sends.append(_remote(ins[w].at[pl.ds(c * h, h)], half(w, my_chip, c), send_sems.at[p, w],
                                     recv_sems.at[p, w], (ox, oy, c)))
                here = half(w, 2 * ox + oy, c)
                landed.append(_remote(here, here, send_sems.at[p, w], recv_sems.at[p, w], (ox, oy, c)))
                relays.append(_remote(here, here, relay_send.at[p, w], relay_recv.at[p, w], sibling))
                there = half(w, 2 * ox + oy, 1 - c)
                relayed.append(_remote(there, there, relay_send.at[p, w], relay_recv.at[p, w], sibling))
        return own, sends, landed, relays, relayed

    def start(self, ins, outs, sems):
        own, sends, _, _, _ = self._copies(ins, outs, sems)
        for cp in own + sends:
            cp.start()

    def relay(self, ins, outs, sems):
        _, _, landed, relays, _ = self._copies(ins, outs, sems)
        for arrival, cp in zip(landed, relays):
            arrival.wait_recv()
            cp.start()

    def finish(self, ins, outs, sems):
        own, sends, _, relays, relayed = self._copies(ins, outs, sems)
        for arrival in relayed:
            arrival.wait_recv()
        for cp in sends + relays:
            cp.wait_send()
        for cp in own:
            cp.wait()


class _ChipExchange(_NoRider):
    def __init__(self, pair_sums):
        n_w = len(pair_sums)
        self.operands = list(pair_sums)
        self.out_shapes = [jax.ShapeDtypeStruct((3,) + s.shape[1:], s.dtype) for s in pair_sums]
        self.scratch = [pltpu.SemaphoreType.DMA((3, n_w))] * 2

    def _copies(self, ins, outs, sems):
        send_sems, recv_sems = sems
        x, y, c = _place()
        return [_remote(ins[w].at[2 * ox + oy], outs[w].at[p], send_sems.at[p, w], recv_sems.at[p, w], (ox, oy, c))
                for p, (ox, oy) in enumerate(_other_chips(x, y)) for w in range(len(ins))]

    def start(self, ins, outs, sems):
        for cp in self._copies(ins, outs, sems):
            cp.start()

    def finish(self, ins, outs, sems):
        for cp in self._copies(ins, outs, sems):
            cp.wait()


def _pair_exchange(name, grads):
    n_w = len(grads)

    def halves(w):
        return grads[w].shape[-2] // 2

    def body(*refs):
        ins, theirs = refs[:n_w], refs[n_w:2 * n_w]
        send_sems, recv_sems = refs[2 * n_w:]
        x, y, c = _place()
        sends = []
        for w in range(n_w):
            rows = pl.ds((1 - c) * halves(w), halves(w))
            src = ins[w].at[:, rows, :] if grads[w].ndim == 3 else ins[w].at[rows, :]
            sends.append(_remote(src, theirs[w], send_sems.at[w], recv_sems.at[w], (x, y, 1 - c)))
        for cp in sends:
            cp.start()
        for cp in sends:
            cp.wait()

    return pl.pallas_call(
        body, name=name, in_specs=[ANY] * n_w, out_specs=[ANY] * n_w,
        out_shape=[jax.ShapeDtypeStruct(g.shape[:-2] + (halves(w), g.shape[-1]), F32) for w, g in enumerate(grads)],
        scratch_shapes=[pltpu.SemaphoreType.DMA((n_w,)), pltpu.SemaphoreType.DMA((n_w,))],
    )(*grads)


def _pair_share(shards):
    n_w = len(shards)

    def body(*refs):
        ins, outs = refs[:n_w], refs[n_w:2 * n_w]
        send_sems, recv_sems = refs[2 * n_w:]
        x, y, c = _place()
        sends = []
        for w in range(n_w):
            h = shards[w].shape[0] // 2
            mine = outs[w].at[pl.ds(c * h, h)]
            sends.append(pltpu.make_async_remote_copy(
                src_ref=mine, dst_ref=mine, send_sem=send_sems.at[w], recv_sem=recv_sems.at[w],
                device_id=(x, y, 1 - c), device_id_type=MESH))
        for cp in sends:
            cp.start()
        for w in range(n_w):
            h = shards[w].shape[0] // 2
            theirs = outs[w].at[pl.ds((1 - c) * h, h)]
            pltpu.make_async_remote_copy(
                src_ref=theirs, dst_ref=theirs, send_sem=send_sems.at[w], recv_sem=recv_sems.at[w],
                device_id=(x, y, 1 - c), device_id_type=MESH).wait_recv()
        for cp in sends:
            cp.wait_send()

    return pl.pallas_call(
        body, name="pair_share", in_specs=[ANY] * n_w, out_specs=[ANY] * n_w,
        out_shape=[jax.ShapeDtypeStruct(s.shape, s.dtype) for s in shards],
        input_output_aliases={w: w for w in range(n_w)},
        scratch_shapes=[pltpu.SemaphoreType.DMA((n_w,)), pltpu.SemaphoreType.DMA((n_w,))],
    )(*shards)


def _all_reduce_small(vec):
    rows = vec.shape[0]

    def body(v_ref, o_ref, slots, send_sems, recv_sems):
        x, y, c = _place()
        me = 4 * x + 2 * y + c
        slots[me] = v_ref[...]
        sends = []
        for k in range(1, N_DEV):
            peer = (x ^ (k >> 2), y ^ ((k >> 1) & 1), c ^ (k & 1))
            sends.append(pltpu.make_async_remote_copy(
                src_ref=v_ref, dst_ref=slots.at[me], send_sem=send_sems.at[k - 1], recv_sem=recv_sems.at[k - 1],
                device_id=peer, device_id_type=MESH))
        for cp in sends:
            cp.start()
        for k in range(1, N_DEV):
            px, py, pc = x ^ (k >> 2), y ^ ((k >> 1) & 1), c ^ (k & 1)
            landed = slots.at[4 * px + 2 * py + pc]
            pltpu.make_async_remote_copy(
                src_ref=landed, dst_ref=landed, send_sem=send_sems.at[k - 1], recv_sem=recv_sems.at[k - 1],
                device_id=(px, py, pc), device_id_type=MESH).wait_recv()
        for cp in sends:
            cp.wait_send()
        total = slots[0]
        for d in range(1, N_DEV):
            total = total + slots[d]
        o_ref[...] = total

    vm = pl.BlockSpec(memory_space=pltpu.VMEM)
    return pl.pallas_call(
        body, name="all_reduce_small", in_specs=[vm], out_specs=vm, out_shape=jax.ShapeDtypeStruct(vec.shape, F32),
        scratch_shapes=[pltpu.VMEM((N_DEV, rows, LANES), F32), pltpu.SemaphoreType.DMA((N_DEV - 1,)),
                        pltpu.SemaphoreType.DMA((N_DEV - 1,))],
    )(vec)


def _row_tile(rows):
    fits = [tr for tr in range(16, min(rows, 512) + 1, 16) if rows % tr == 0]
    return max(fits) if fits else rows


def _pair_sum(name, place, grad, theirs):
    if grad.ndim == 2:
        return _pair_sum_joined(name, place, grad, theirs)
    n, r, c = grad.shape
    half = r // 2
    tr = _row_tile(half)
    nb = half // tr

    def body(place_ref, g_ref, t_ref, o_ref):
        o_ref[...] = (g_ref[...] + t_ref[...]).astype(BF16)

    return pl.pallas_call(
        body, name=name, out_shape=jax.ShapeDtypeStruct((n, half, c), BF16),
        grid_spec=pltpu.PrefetchScalarGridSpec(
            num_scalar_prefetch=1, grid=(n, nb),
            in_specs=[pl.BlockSpec((1, tr, c), lambda j, i, pr: (j, pr[0] * nb + i, 0)),
                      pl.BlockSpec((1, tr, c), lambda j, i, pr: (j, i, 0))],
            out_specs=pl.BlockSpec((1, tr, c), lambda j, i, pr: (j, i, 0))),
        compiler_params=_params("parallel", "parallel"),
    )(place, grad, theirs)


def _pair_sum_joined(name, place, grad, theirs):
    r, wide = grad.shape
    half, c = r // 2, wide // N_CHIPS
    tr = _row_tile(half)
    nb = half // tr

    def body(place_ref, g_ref, t_ref, o_ref):
        for j in range(N_CHIPS):
            cols = slice(j * c, (j + 1) * c)
            o_ref[j] = (g_ref[:, cols] + t_ref[:, cols]).astype(BF16)

    return pl.pallas_call(
        body, name=name, out_shape=jax.ShapeDtypeStruct((N_CHIPS, half, c), BF16),
        grid_spec=pltpu.PrefetchScalarGridSpec(
            num_scalar_prefetch=1, grid=(nb,),
            in_specs=[pl.BlockSpec((tr, wide), lambda i, pr: (pr[0] * nb + i, 0)),
                      pl.BlockSpec((tr, wide), lambda i, pr: (i, 0))],
            out_specs=pl.BlockSpec((N_CHIPS, tr, c), lambda i, pr: (0, i, 0))),
        compiler_params=_params("parallel"),
    )(place, grad, theirs)


def _sum_chips(name, place, pair_sums, landed):
    _, half, c = pair_sums.shape
    tr = _row_tile(half)
    nb = half // tr

    def body(place_ref, s_ref, q_ref, o_ref):
        total = s_ref[0].astype(F32)
        for p in range(3):
            total = total + q_ref[p].astype(F32)
        o_ref[...] = total

    return pl.pallas_call(
        body, name=name, out_shape=jax.ShapeDtypeStruct((2 * half, c), F32),
        grid_spec=pltpu.PrefetchScalarGridSpec(
            num_scalar_prefetch=1, grid=(nb,),
            in_specs=[pl.BlockSpec((1, tr, c), lambda i, pr: (pr[1], i, 0)),
                      pl.BlockSpec((3, tr, c), lambda i, pr: (0, i, 0))],
            out_specs=pl.BlockSpec((tr, c), lambda i, pr: (pr[0] * nb + i, 0))),
        compiler_params=_params("parallel"),
    )(place, pair_sums, landed)


BIG = ("w_in", "w_branch_a", "w_branch_b", "w_out", "w_ffn_gate", "w_ffn_up", "w_ffn_down", "w_ple_gate", "w_ple_proj")
LATE = BIG[1:]
COLUMN_SHARDED = ("w_in", "w_branch_a", "w_branch_b", "w_ffn_gate", "w_ffn_up", "w_ple_proj")
SMALL = ("norm_mix", "w_pool", "pool_scale", "norm_ffn", "norm_ple", "norm_final")


def _join_columns(w4):
    return jnp.concatenate([w4[j] for j in range(N_CHIPS)], axis=1)


def _sds(shape, dtype):
    return jax.ShapeDtypeStruct(shape, dtype)


def _local_step(x, p, target, wf, small, gather_first=None, gather_late=None, exchange_early=None,
                exchange_last=None):
    t, d = x.shape
    w_pool_b = small["w_pool"].astype(BF16)
    dp = w_pool_b.shape[0] * w_pool_b.shape[1]

    h1, first = _norm_fwd("norm_mix", x, small["norm_mix"], rider=gather_first)
    w_in = first[0] if gather_first else wf["w_in"]
    u, q, kv, ga, gb = _mm(
        "proj", [h1], [w_in[j] for j in range(N_CHIPS)], "nn",
        [_sds((t, dp), F32), _sds((t, dp), BF16), _sds((t, d), BF16), _sds((t, d), BF16), _sds((t, d), BF16)],
        separate=True, epilogue=lambda uq, kv_, ga_, gb_: (uq[:, :dp], uq[:, dp:], kv_, ga_, gb_), tm=512)
    pooled, ya = _pool_fwd(u, w_pool_b, small["pool_scale"])
    n_pairs = dp // LANES
    yb, late = _attn_fwd(q, 0, kv, 0, n_pairs, n_pairs, rider=gather_late or _NoRider())
    wf = {**wf, **dict(zip(LATE, late))}
    w_gate, w_up = _join_columns(wf["w_ffn_gate"]), _join_columns(wf["w_ffn_up"])
    dff = w_gate.shape[1]
    w_down = wf["w_ffn_down"].reshape(dff, d)
    w_a, w_b, w_pp = _join_columns(wf["w_branch_a"]), _join_columns(wf["w_branch_b"]), _join_columns(wf["w_ple_proj"])
    w_out = wf["w_out"].reshape(d, d)
    w_pg = wf["w_ple_gate"].reshape(d, d)
    def residual_norm(branch, xv, g, w):
        xn = xv + jnp.dot(branch.astype(BF16), w, preferred_element_type=F32)
        return xn, xn * lax.rsqrt(jnp.mean(xn * xn, axis=-1, keepdims=True) + RMS_EPS) * g

    def mixer_tail(tav, tbv, gav, gbv, xv, g, w):
        merged = _sigmoid(gav) * tav + _sigmoid(gbv) * tbv
        return (tav, tbv, merged) + residual_norm(merged, xv, g, w)

    def ffn_tail(gv, uv, xv, g, w):
        act = gv * _sigmoid(gv) * uv
        return (gv, uv, act) + residual_norm(act, xv, g, w)

    stream = [_sds((t, d), F32), _sds((t, d), BF16)]
    ta, tb, merged, x1, h2 = _mm(
        "mixer_out", [ya, yb], [w_a, w_b], "nn", [_sds((t, d), BF16)] * 3 + stream,
        extras=[ga, gb, x, small["norm_ffn"]], wholes=[w_out], separate=True, epilogue=mixer_tail, tm=512)
    gate, up, act, x2, h3 = _mm(
        "ffn", [h2], [w_gate, w_up], "nn", [_sds((t, dff), BF16)] * 3 + stream,
        extras=[x1, small["norm_ple"]], wholes=[w_down], separate=True, epilogue=ffn_tail, tm=256)
    dx2, dx2_b, d_pp, d_gp, d_norm_final, loss_row, d_norm_ple = _mm(
        "ple_loss", [h3, p], [w_pg, w_pp], "nn", stream + [_sds((t, d), BF16)] * 2,
        extras=[x2, target, small["norm_final"].reshape(1, d), small["norm_ple"]], wholes=[w_pg], separate=True,
        epilogue=_ple_and_loss, sum_shapes=[_sds((1, d), F32)] * 3, tm=512)

    def through_norm(dh, xv, g, dres):
        dx, d_gain = _rms_norm_bwd(dh, xv, g)
        return dx + dres, dx + dres, d_gain

    gain_sum = [_sds((1, d), F32)]
    g_w_pp, g_w_pg = _mm_tn("g_ple", [p, h3], [d_pp, d_gp])

    def ffn_bwd(d_act, gv, uv, xv, g, dres, wg, wu):
        s = _sigmoid(gv)
        d_gate, d_up = d_act * uv * (s * (1.0 + gv * (1.0 - s))), d_act * (gv * s)
        nt = (((1,), (1,)), ((), ()))
        dh2 = (lax.dot_general(d_gate.astype(BF16), wg, nt, preferred_element_type=F32)
               + lax.dot_general(d_up.astype(BF16), wu, nt, preferred_element_type=F32))
        return (d_gate, d_up) + through_norm(dh2, xv, g, dres)

    d_gate, d_up, dx1, dx1_b, d_norm_ffn = _mm(
        "ffn_bwd", [dx2_b], [w_down], "nt", [_sds((t, dff), BF16)] * 2 + stream,
        extras=[gate, up, x1, small["norm_ffn"], dx2], wholes=[w_gate, w_up], epilogue=ffn_bwd,
        sum_shapes=gain_sum, tm=256)
    g_w_down, = _mm_tn("g_ffn_down", [act], [dx2_b], tmm=512)
    g_w_gate, g_w_up = _mm_tn("g_ffn_gate_up", [h2], [d_gate, d_up], n_blocks=2)

    def merge_bwd(acc, tav, tbv, gav, gbv):
        sa, sb = _sigmoid(gav), _sigmoid(gbv)
        return acc * sa, acc * sb, acc * tav * sa * (1.0 - sa), acc * tbv * sb * (1.0 - sb)

    d_ta, d_tb, d_ga, d_gb = _mm("d_merged", [dx1_b], [w_out], "nt", [_sds((t, d), BF16)] * 4,
                                 extras=[ta, tb, ga, gb], epilogue=merge_bwd, tm=512)
    g_w_out, g_w_a, g_w_b = _mm_tn("g_mixer", [merged, ya, yb], [dx1_b, d_ta, d_tb])
    d_ya, d_yb = _mm("d_branches", [d_ta, d_tb], [w_a, w_b], "nt", [_sds((t, dp), F32), _sds((t, dp), BF16)],
                     separate=True)
    d_u, g_w_pool, d_pool_scale = _pool_bwd(d_ya, pooled, w_pool_b, small["pool_scale"])
    big = {
        "w_branch_a": g_w_a, "w_branch_b": g_w_b, "w_out": g_w_out.reshape(wf["w_out"].shape),
        "w_ffn_gate": g_w_gate, "w_ffn_up": g_w_up, "w_ffn_down": g_w_down.reshape(wf["w_ffn_down"].shape),
        "w_ple_gate": g_w_pg.reshape(wf["w_ple_gate"].shape), "w_ple_proj": g_w_pp,
    }
    rider = exchange_early(big) if exchange_early else _NoRider()
    (d_q, d_k, d_v), early = _attn_bwd(q, 0, kv, 0, n_pairs, d_yb, n_pairs, rider=rider)
    d_proj = [(d_u, d_q), (d_k, d_v), d_ga, d_gb]
    big["w_in"], = _mm_tn("g_w_in", [h1], d_proj, tmm=512, stacked=True)
    rider = exchange_last(big["w_in"]) if exchange_last else _NoRider()
    res = _mm(
        "d_h1", d_proj, [w_in[j] for j in range(N_CHIPS)], "nt", [_sds((t, d), F32)],
        extras=[x, small["norm_mix"], dx1], epilogue=lambda dh, xv, g, dres: through_norm(dh, xv, g, dres)[1:],
        sum_shapes=gain_sum, tm=512, rider=rider)
    (grad_x, d_norm_mix), last = res if rider.operands else (res, ())
    small_g = {"norm_mix": d_norm_mix, "w_pool": g_w_pool, "pool_scale": d_pool_scale, "norm_ffn": d_norm_ffn,
               "norm_ple": d_norm_ple, "norm_final": d_norm_final}
    return grad_x, big, small_g, loss_row, early, last


def _split2(res, n):
    return res[:n], res[n:]


def _pack_small(small_g, loss_row):
    parts, layout = [], []
    for name in SMALL + ("loss",):
        v = (loss_row if name == "loss" else small_g[name]).reshape(-1, LANES)
        pad = (-v.shape[0]) % 8
        if pad:
            v = jnp.concatenate([v, jnp.zeros((pad, LANES), F32)], axis=0)
        layout.append((name, sum(q.shape[0] for q in parts), v.shape[0]))
        parts.append(v)
    return jnp.concatenate(parts, axis=0), layout


def kernel(x, p, norm_mix, w_in, w_pool, pool_scale, w_branch_a, w_branch_b, w_out, norm_ffn, w_ffn_gate, w_ffn_up, w_ffn_down, norm_ple, w_ple_gate, w_ple_proj, norm_final, loss_target, m_norm_mix, m_w_in, m_w_pool, m_pool_scale, m_w_branch_a, m_w_branch_b, m_w_out, m_norm_ffn, m_w_ffn_gate, m_w_ffn_up, m_w_ffn_down, m_norm_ple, m_w_ple_gate, m_w_ple_proj, m_norm_final, v_norm_mix, v_w_in, v_w_pool, v_pool_scale, v_w_branch_a, v_w_branch_b, v_w_out, v_norm_ffn, v_w_ffn_gate, v_w_ffn_up, v_w_ffn_down, v_norm_ple, v_w_ple_gate, v_w_ple_proj, v_norm_final):
    given = dict(locals())
    names = BIG + SMALL
    order = ("norm_mix", "w_in", "w_pool", "pool_scale", "w_branch_a", "w_branch_b", "w_out", "norm_ffn", "w_ffn_gate",
             "w_ffn_up", "w_ffn_down", "norm_ple", "w_ple_gate", "w_ple_proj", "norm_final")
    t, d = x.shape[1], x.shape[2]
    shard = {n: given[n][0] for n in BIG}
    small = {"norm_mix": norm_mix, "w_pool": w_pool[0], "pool_scale": pool_scale, "norm_ffn": norm_ffn,
             "norm_ple": norm_ple, "norm_final": norm_final}

    as_bf16 = {n: shard[n].astype(BF16) for n in BIG}

    place = jnp.stack([lax.axis_index("c"), 2 * lax.axis_index("x") + lax.axis_index("y")]).astype(jnp.int32)
    pair_sums = {}

    def exchange_early(ready):
        theirs = _pair_exchange("pair_exchange_early", [ready[n] for n in LATE])
        for n, other in zip(LATE, theirs):
            pair_sums[n] = _pair_sum(f"pair_sum_{n}", place, ready[n], other)
        return _ChipExchange([pair_sums[n] for n in LATE])

    def exchange_last(g_w_in):
        theirs, = _pair_exchange("pair_exchange_w_in", [g_w_in])
        pair_sums["w_in"] = _pair_sum("pair_sum_w_in", place, g_w_in, theirs)
        return _ChipExchange([pair_sums["w_in"]])

    grad_x, big_g, small_g, loss_row, early, last = _local_step(
        x.reshape(t, d), p.reshape(t, p.shape[-1]), loss_target.reshape(t, d), {}, small,
        gather_first=_WeightGather([as_bf16["w_in"]]),
        gather_late=_WeightGather([as_bf16[n] for n in LATE]), exchange_early=exchange_early,
        exchange_last=exchange_last)
    landed = dict(zip(LATE + ("w_in",), tuple(early) + tuple(last)))
    halves = [_sum_chips(f"chip_sum_{n}", place, pair_sums[n], landed[n]) for n in BIG]
    grads = dict(zip(BIG, _pair_share(halves)))

    packed, layout = _pack_small(small_g, loss_row)
    reduced = _all_reduce_small(packed)
    for name, start, rows in layout:
        if name == "loss":
            loss = jnp.sum(reduced[start:start + rows])
        else:
            n_el = small[name].size
            grads[name] = reduced[start:start + rows].reshape(-1)[:n_el]

    deltas, new_m, new_v = {}, {}, {}
    for n in order:
        w = shard[n] if n in BIG else small[n]
        shape2 = w.shape if w.ndim == 2 else ((1, w.shape[0]) if w.ndim == 1 else (w.shape[0] * w.shape[1], w.shape[2]))
        g2 = grads[n].reshape(shape2)
        dl, mn, vn = _adamw(f"adamw_{n}", w.reshape(shape2), g2, given["m_" + n].reshape(shape2),
                            given["v_" + n].reshape(shape2))
        full = given[n].shape
        grads[n], deltas[n], new_m[n], new_v[n] = g2.reshape(full), dl.reshape(full), mn.reshape(full), vn.reshape(full)

    return (loss, grad_x.reshape(x.shape), *[grads[n] for n in order], *[deltas[n] for n in order],
            *[new_m[n] for n in order], *[new_v[n] for n in order])
```

```python
import functools
import math

import jax
import jax.numpy as jnp
from jax import lax
from jax.experimental import pallas as pl
from jax.experimental.pallas import tpu as pltpu

F32 = jnp.float32
BF16 = jnp.bfloat16
MESH = pl.DeviceIdType.MESH

RMS_EPS = 1e-6
POOL_WINDOWS = (2, 4, 8, 16)
POOL_HALO = 16
HEAD_DIM = 64
LANES = 128
ATT_BLOCK = 128
ATT_CHAINS = 4
ATT_CHUNK = 128
ATT_SLAB = 128
ATT_SCALE = 1.0 / math.sqrt(HEAD_DIM)
LOG2_E = 1.4426950408889634
ATT_EXIT_BELOW = -150.5
ADAM_LR, ADAM_B1, ADAM_B2, ADAM_EPS, ADAM_WD, ADAM_STEP = 0.001, 0.9, 0.999, 1e-08, 0.01, 10
V7X_VMEM_LIMIT_BYTES = 56 * 1024 * 1024
N_CHIPS = 4
N_DEV = 8


def _params(*semantics):
    return pltpu.CompilerParams(dimension_semantics=semantics, vmem_limit_bytes=V7X_VMEM_LIMIT_BYTES)


def _sigmoid(z):
    return 1.0 / (1.0 + jnp.exp(-z))


def _tiled_spec(shape, tm, tn, n_total, at):
    rows, width = shape
    if rows == 1:
        if width == n_total:
            return pl.BlockSpec((1, tn), at(lambda i, j: (0, j)))
        return pl.BlockSpec((1, width), at(lambda i, j: (0, 0)))
    if width == n_total:
        return pl.BlockSpec((tm, tn), at(lambda i, j: (i, j)))
    assert tn == n_total, "an operand narrower than the output needs whole output rows per tile"
    return pl.BlockSpec((tm, width), at(lambda i, j: (i, 0)))


def _column_pieces(operands):
    pieces = [tuple(a) if isinstance(a, (tuple, list)) else (a,) for a in operands]
    return [p for ps in pieces for p in ps], [len(ps) for ps in pieces]


def _load_bf16(refs, counts):
    tiles, k = [], 0
    for n in counts:
        parts = [r[...] for r in refs[k:k + n]]
        parts = [t if t.dtype == BF16 else t.astype(BF16) for t in parts]
        tiles.append(parts[0] if n == 1 else jnp.concatenate(parts, axis=1))
        k += n
    return tiles


def _mm(name, a_list, b_list, mode, out_shapes, epilogue=None, extras=(), tm=1024, tn=None, separate=False,
        sum_shapes=(), rider=None, wholes=()):
    flat_a, counts = _column_pieces(a_list)
    m_total = flat_a[0].shape[0]
    n_total = b_list[0].shape[1] if mode == "nn" else b_list[0].shape[0]
    tn = n_total if tn is None else tn
    tm = min(tm, m_total)
    assert m_total % tm == 0 and n_total % tn == 0 and (not sum_shapes or tn == n_total)
    n_a, n_b, n_extra, n_out = len(counts), len(b_list), len(extras), len(out_shapes)
    assert n_a in (1, n_b)
    dims = (((1,), (0,)), ((), ())) if mode == "nn" else (((1,), (1,)), ((), ()))
    rider = rider or _NoRider()
    grid = (n_total // tn, m_total // tm)

    def at(index):
        return lambda j, i: index(i, j)

    def body(*refs):
        ins, o_refs, _, riding = rider.split(refs, len(flat_a) + n_b + n_extra + len(wholes), n_out + len(sum_shapes))
        a_refs, b_refs = ins[:len(flat_a)], ins[len(flat_a):len(flat_a) + n_b]
        e_refs, w_refs = ins[len(flat_a) + n_b:len(flat_a) + n_b + n_extra], ins[len(flat_a) + n_b + n_extra:]
        at_first = (pl.program_id(0) == 0) & (pl.program_id(1) == 0)
        at_last = (pl.program_id(0) == grid[0] - 1) & (pl.program_id(1) == grid[1] - 1)
        top, bottom = rider.at_steps(riding, at_first, at_first, at_last)
        top()
        lefts = _load_bf16(a_refs, counts)
        products = [lax.dot_general(lefts[s % n_a], b_refs[s][...], dims, preferred_element_type=F32)
                    for s in range(n_b)]
        if not separate:
            products = [functools.reduce(lambda p, r: p + r, products)]
        extra_tiles = [e[...].astype(F32) for e in e_refs]
        outs = products if epilogue is None else epilogue(*products, *extra_tiles, *[w[...] for w in w_refs])
        for o_ref, o in zip(o_refs[:n_out], outs[:n_out]):
            o_ref[...] = o.astype(o_ref.dtype)
        if sum_shapes:
            @pl.when(pl.program_id(1) == 0)
            def _():
                for s_ref in o_refs[n_out:]:
                    s_ref[...] = jnp.zeros_like(s_ref)

            for s_ref, s in zip(o_refs[n_out:], outs[n_out:]):
                s_ref[...] += s
        bottom()

    once = dict(pipeline_mode=pl.Buffered(1)) if tn == n_total else {}
    in_specs = [pl.BlockSpec((tm, a.shape[1]), at(lambda i, j: (i, 0))) for a in flat_a]
    if mode == "nn":
        in_specs += [pl.BlockSpec((b.shape[0], tn), at(lambda i, j: (0, j)), **once) for b in b_list]
    else:
        in_specs += [pl.BlockSpec((tn, b.shape[1]), at(lambda i, j: (j, 0)), **once) for b in b_list]
    in_specs += [_tiled_spec(e.shape, tm, tn, n_total, at) for e in extras]
    in_specs += [pl.BlockSpec(w.shape, lambda j, i: (0, 0), pipeline_mode=pl.Buffered(1)) for w in wholes]
    out_specs = [_tiled_spec(o.shape, tm, tn, n_total, at) for o in out_shapes]
    out_specs += [pl.BlockSpec(s.shape, at(lambda i, j: (0, 0))) for s in sum_shapes]
    semantics = ("arbitrary", "arbitrary") if sum_shapes or rider.operands else ("parallel", "parallel")
    res = pl.pallas_call(
        body, name=name, grid=grid, in_specs=in_specs + [ANY] * len(rider.operands),
        out_specs=out_specs + [ANY] * len(rider.out_shapes),
        out_shape=list(out_shapes) + list(sum_shapes) + list(rider.out_shapes), scratch_shapes=list(rider.scratch),
        compiler_params=_params(*semantics),
    )(*flat_a, *b_list, *extras, *wholes, *rider.operands)
    n_own = len(out_shapes) + len(sum_shapes)
    return res if not rider.operands else (res[:n_own], res[n_own:])


def _mm_tn(name, a_list, b_list, tmm=1024, stacked=False, n_blocks=1):
    flat_b, counts = _column_pieces(b_list)
    n_a = len(a_list)
    m_total = a_list[0].shape[0]
    ks = [a_list[s % n_a].shape[1] for s in range(len(counts))]
    widths = [sum(p.shape[1] for p in flat_b[sum(counts[:s]):sum(counts[:s + 1])]) for s in range(len(counts))]
    tmm = min(tmm, m_total)
    assert m_total % tmm == 0 and (n_blocks == 1 or max(counts) == 1) and all(w % n_blocks == 0 for w in widths)
    n_b = len(counts)
    assert n_a in (1, n_b) and not (stacked and n_a > 1)

    def body(*refs):
        a_refs, b_refs, o_refs = refs[:n_a], refs[n_a:n_a + len(flat_b)], refs[n_a + len(flat_b):]

        @pl.when(pl.program_id(1) == 0)
        def _():
            for o_ref in o_refs:
                o_ref[...] = jnp.zeros_like(o_ref)

        lefts = _load_bf16(a_refs, [1] * n_a)
        for s, bv in enumerate(_load_bf16(b_refs, counts)):
            product = lax.dot_general(lefts[s % n_a], bv, (((0,), (0,)), ((), ())), preferred_element_type=F32)
            if stacked:
                o_refs[0][s] += product
            else:
                o_refs[s][...] += product

    in_specs = [pl.BlockSpec((tmm, a.shape[1]), lambda nb, m: (m, 0)) for a in a_list]
    in_specs += [pl.BlockSpec((tmm, b.shape[1] // n_blocks), lambda nb, m: (m, nb)) for b in flat_b]
    if stacked:
        out_shape = [jax.ShapeDtypeStruct((n_b, ks[0], widths[0]), F32)]
        out_specs = [pl.BlockSpec((n_b, ks[0], widths[0] // n_blocks), lambda nb, m: (0, 0, nb))]
    else:
        out_shape = [jax.ShapeDtypeStruct((k, w), F32) for k, w in zip(ks, widths)]
        out_specs = [pl.BlockSpec((k, w // n_blocks), lambda nb, m: (0, nb)) for k, w in zip(ks, widths)]
    return pl.pallas_call(
        body, name=name, grid=(n_blocks, m_total // tmm), in_specs=in_specs, out_specs=out_specs, out_shape=out_shape,
        compiler_params=_params("arbitrary", "arbitrary"),
    )(*a_list, *flat_b)


def _rows(name, fn, ins, tile_outs, sum_outs=(), tr=512, rider=None):
    t_total = max(a.shape[0] for a in ins)
    tr = min(tr, t_total)
    assert t_total % tr == 0
    n_in, n_tile = len(ins), len(tile_outs)
    rider = rider or _NoRider()
    n_steps = t_total // tr

    def body(*refs):
        own_ins, own_outs, _, riding = rider.split(refs, n_in, n_tile + len(sum_outs))
        step = pl.program_id(0)
        top, bottom = rider.at_steps(riding, step == 0, step == n_steps - 1, step == n_steps - 1)
        top()
        refs = tuple(own_ins) + tuple(own_outs)
        outs = fn(*[r[...].astype(F32) for r in refs[:n_in]])
        for o_ref, o in zip(refs[n_in:n_in + n_tile], outs[:n_tile]):
            o_ref[...] = o.astype(o_ref.dtype)
        if sum_outs:
            @pl.when(pl.program_id(0) == 0)
            def _():
                for s_ref in refs[n_in + n_tile:]:
                    s_ref[...] = jnp.zeros_like(s_ref)

            for s_ref, s in zip(refs[n_in + n_tile:], outs[n_tile:]):
                s_ref[...] += s
        bottom()

    def spec(shape):
        if shape[0] == 1:
            return pl.BlockSpec(shape, lambda i: (0, 0))
        return pl.BlockSpec((tr, shape[1]), lambda i: (i, 0))

    return pl.pallas_call(
        body, name=name, grid=(n_steps,), in_specs=[spec(a.shape) for a in ins] + [ANY] * len(rider.operands),
        out_specs=[spec(o.shape) for o in tile_outs] + [spec(s.shape) for s in sum_outs] + [ANY] * len(rider.out_shapes),
        out_shape=list(tile_outs) + list(sum_outs) + list(rider.out_shapes), scratch_shapes=list(rider.scratch),
        compiler_params=_params("arbitrary" if sum_outs or rider.operands else "parallel"),
    )(*ins, *rider.operands)


def _norm_fwd(name, x, gain, rider=None):
    def fn(xv, g):
        inv = lax.rsqrt(jnp.mean(xv * xv, axis=-1, keepdims=True) + RMS_EPS)
        return (xv * inv * g,)

    res = _rows(name, fn, [x, gain], [jax.ShapeDtypeStruct(x.shape, BF16)], rider=rider)
    return res[0], res[1:]


def _rms_norm_bwd(dh, xv, g):
    inv = lax.rsqrt(jnp.mean(xv * xv, axis=-1, keepdims=True) + RMS_EPS)
    xn = xv * inv
    dxn = dh * g
    return inv * (dxn - xn * jnp.mean(dxn * xn, axis=-1, keepdims=True)), jnp.sum(dh * xn, axis=0, keepdims=True)


def _ple_and_loss(gv, pv, x2v, tv, g_final, g_ple, w_pg):
    d = x2v.shape[1]
    s = _sigmoid(gv)
    xv = x2v + s * pv
    inv = lax.rsqrt(jnp.mean(xv * xv, axis=-1, keepdims=True) + RMS_EPS)
    err = xv * inv * g_final - tv
    dx3, d_final = _rms_norm_bwd(err * (1.0 / d), xv, g_final)
    d_pp, d_gp = dx3 * s, dx3 * pv * s * (1.0 - s)
    dh3 = lax.dot_general(d_gp.astype(BF16), w_pg, (((1,), (1,)), ((), ())), preferred_element_type=F32)
    dx2, d_ple = _rms_norm_bwd(dh3, x2v, g_ple)
    dx2 = dx2 + dx3
    return dx2, dx2, d_pp, d_gp, d_final, (0.5 / d) * jnp.sum(err * err, axis=0, keepdims=True), d_ple


def _window_counts(t_pos, w):
    return jnp.minimum(t_pos + 1, w).astype(F32)


def _pool_fwd(u, w_pool, scale, tr=512):
    t_total, width = u.shape
    tr = min(tr, t_total)
    n_groups = len(POOL_WINDOWS)
    gdim = width // n_groups
    ext = tr + POOL_HALO

    def body(u_ref, halo_ref, w_ref, s_ref, pooled_ref, ya_ref):
        i = pl.program_id(0)
        halo = jnp.where(i == 0, 0.0, halo_ref[...])
        t_pos = i * tr + lax.broadcasted_iota(jnp.int32, (tr, 1), 0)
        for g, w in enumerate(POOL_WINDOWS):
            cols = slice(g * gdim, (g + 1) * gdim)
            main = u_ref[:, cols]
            win = jnp.concatenate([halo[:, cols], main], axis=0)
            span = 1
            while span < w:
                win = win + pltpu.roll(win, span, 0)
                span *= 2
            pooled = win[POOL_HALO:, :] * (1.0 / _window_counts(t_pos, w)) - main
            pooled_b = pooled.astype(BF16)
            pooled_ref[:, cols] = pooled_b
            mixed = jnp.dot(pooled_b, w_ref[g], preferred_element_type=F32)
            ya_ref[:, cols] = (mixed * s_ref[:, cols]).astype(BF16)

    hb = tr // POOL_HALO
    return pl.pallas_call(
        body, name="pool_fwd", grid=(t_total // tr,),
        in_specs=[pl.BlockSpec((tr, width), lambda i: (i, 0)),
                  pl.BlockSpec((POOL_HALO, width), lambda i: (jnp.maximum(i * hb - 1, 0), 0)),
                  pl.BlockSpec((n_groups, gdim, gdim), lambda i: (0, 0, 0)),
                  pl.BlockSpec((1, width), lambda i: (0, 0))],
        out_specs=[pl.BlockSpec((tr, width), lambda i: (i, 0)), pl.BlockSpec((tr, width), lambda i: (i, 0))],
        out_shape=[jax.ShapeDtypeStruct(u.shape, BF16), jax.ShapeDtypeStruct(u.shape, BF16)],
        compiler_params=_params("parallel"),
    )(u, u, w_pool, scale)


def _pool_bwd(dya, pooled, w_pool, scale, tr=512):
    t_total, width = dya.shape
    tr = min(tr, t_total)
    n_groups = len(POOL_WINDOWS)
    gdim = width // n_groups
    ext = tr + POOL_HALO
    n_tiles = t_total // tr

    def body(d_ref, halo_ref, p_ref, w_ref, s_ref, du_ref, dw_ref, ds_ref):
        i = pl.program_id(0)

        @pl.when(i == 0)
        def _():
            dw_ref[...] = jnp.zeros_like(dw_ref)
            ds_ref[...] = jnp.zeros_like(ds_ref)

        halo = jnp.where(i == n_tiles - 1, 0.0, halo_ref[...])
        t_pos = i * tr + lax.broadcasted_iota(jnp.int32, (ext, 1), 0)
        for g, w in enumerate(POOL_WINDOWS):
            cols = slice(g * gdim, (g + 1) * gdim)
            sc = s_ref[:, cols]
            d_main = d_ref[:, cols]
            pooled_b = p_ref[:, cols]
            mixed = jnp.dot(pooled_b, w_ref[g], preferred_element_type=F32)
            ds_ref[:, cols] += jnp.sum(d_main * mixed, axis=0, keepdims=True)
            dmix = (jnp.concatenate([d_main, halo[:, cols]], axis=0) * sc).astype(BF16)
            dw_ref[g] += lax.dot_general(pooled_b, dmix[:tr, :], (((0,), (0,)), ((), ())),
                                         preferred_element_type=F32)
            dpool = lax.dot_general(dmix, w_ref[g], (((1,), (1,)), ((), ())), preferred_element_type=F32)
            win = dpool * (1.0 / _window_counts(t_pos, w))
            span = 1
            while span < w:
                win = win + pltpu.roll(win, ext - span, 0)
                span *= 2
            du_ref[:, cols] = (win[:tr, :] - dpool[:tr, :]).astype(BF16)

    hb = tr // POOL_HALO
    last_halo = t_total // POOL_HALO - 1
    return pl.pallas_call(
        body, name="pool_bwd", grid=(n_tiles,),
        in_specs=[pl.BlockSpec((tr, width), lambda i: (i, 0)),
                  pl.BlockSpec((POOL_HALO, width), lambda i: (jnp.minimum((i + 1) * hb, last_halo), 0)),
                  pl.BlockSpec((tr, width), lambda i: (i, 0)),
                  pl.BlockSpec((n_groups, gdim, gdim), lambda i: (0, 0, 0)),
                  pl.BlockSpec((1, width), lambda i: (0, 0))],
        out_specs=[pl.BlockSpec((tr, width), lambda i: (i, 0)),
                   pl.BlockSpec((n_groups, gdim, gdim), lambda i: (0, 0, 0)),
                   pl.BlockSpec((1, width), lambda i: (0, 0))],
        out_shape=[jax.ShapeDtypeStruct(dya.shape, BF16), jax.ShapeDtypeStruct((n_groups, gdim, gdim), F32),
                   jax.ShapeDtypeStruct((1, width), F32)],
        compiler_params=_params("arbitrary"),
    )(dya, dya, pooled, w_pool, scale)


def _head_masks():
    lane = lax.broadcasted_iota(jnp.int32, (1, LANES), 1)
    return lane < HEAD_DIM


def _stack_heads(tile, first):
    zero = jnp.zeros_like(tile)
    return jnp.concatenate([jnp.where(first, tile, zero), jnp.where(first, zero, tile)], axis=0)


def _causal_mask(t_pos, k_start):
    col = lax.broadcasted_iota(jnp.int32, (1, 2 * ATT_SLAB), 1)
    return k_start + (col & (ATT_SLAB - 1)) < t_pos


def _slab_scores(q, kd, mask):
    z2 = lax.dot_general(q, kd, (((1,), (1,)), ((), ())), preferred_element_type=F32) * LOG2_E
    log_hit = jnp.minimum(z2, 0.0) - jnp.log2(1.0 + jnp.exp2(-jnp.abs(z2)))
    log_fail = log_hit - z2
    return log_hit, (log_fail if mask is None else jnp.where(mask, log_fail, 0.0))


def _weights(log_hit, suffix, mask):
    arg = log_hit + suffix
    return jnp.exp2(arg if mask is None else jnp.where(mask, arg, -1e30))


def _tri(upper):
    r = lax.broadcasted_iota(jnp.int32, (ATT_CHUNK, ATT_CHUNK), 0)
    c = lax.broadcasted_iota(jnp.int32, (ATT_CHUNK, ATT_CHUNK), 1)
    return jnp.where(r > c if upper else r < c, 1.0, 0.0).astype(BF16)


def _tri_spec():
    return pl.BlockSpec((ATT_CHUNK, ATT_CHUNK), lambda h, i: (0, 0), pipeline_mode=pl.Buffered(1))


def _scan_chunk(v, tri):
    return jnp.dot(v.astype(BF16), tri, preferred_element_type=F32)


def _lane_bcast(col):
    return jnp.broadcast_to(col, (col.shape[0], LANES))


def _scan_slab(v, tri, carries, from_right):
    n_chunks = ATT_SLAB // ATT_CHUNK
    edge = 0 if from_right else ATT_CHUNK - 1
    parts, new_carries = [None] * (2 * n_chunks), []
    for head in range(2):
        run = carries[head]
        for c in (reversed(range(n_chunks)) if from_right else range(n_chunks)):
            lo_col = head * ATT_SLAB + c * ATT_CHUNK
            vc = v[:, lo_col:lo_col + ATT_CHUNK]
            sc = _scan_chunk(vc, tri)
            parts[head * n_chunks + c] = sc + jnp.concatenate([run] * (ATT_CHUNK // LANES), axis=1)
            run = run + _lane_bcast(sc[:, edge:edge + 1] + vc[:, edge:edge + 1])
        new_carries.append(run)
    return jnp.concatenate(parts, axis=1), new_carries


def _fold_heads(stacked, first):
    s = stacked.shape[0] // 2
    return jnp.where(first, stacked[:s], stacked[s:])


class _NoRider:
    operands, out_shapes, scratch = (), (), ()

    def split(self, refs, n_base_in, n_base_out):
        n_in, n_out, n_sem = len(self.operands), len(self.out_shapes), len(self.scratch)
        a = n_base_in + n_in
        b = a + n_base_out + n_out
        mine = (refs[n_base_in:a], refs[a + n_base_out:b], refs[b:b + n_sem])
        return refs[:n_base_in], refs[a:a + n_base_out], refs[b + n_sem:], mine

    def start(self, ins, outs, sems):
        pass

    def relay(self, ins, outs, sems):
        pass

    def finish(self, ins, outs, sems):
        pass

    def at_steps(self, refs, first_step, relay_step, last_step):
        if not self.operands:
            return (lambda: None), (lambda: None)

        def top():
            pl.when(first_step)(lambda: self.start(*refs))
            pl.when(relay_step)(lambda: self.relay(*refs))

        return top, lambda: pl.when(last_step)(lambda: self.finish(*refs))


def _attn_fwd(q_src, q_col, kv_src, k_col, v_col, n_pairs=4, rider=_NoRider()):
    t_total = q_src.shape[0]
    blk = ATT_BLOCK
    n_steps = t_total // (ATT_CHAINS * blk)
    assert t_total % ATT_SLAB == 0 and ATT_SLAB == ATT_BLOCK

    def body(*refs):
        (q_ref, k_ref, v_ref, suffix_ref), (o_ref,), _, riding = rider.split(refs, 4, 1)
        h, ii = pl.program_id(0), pl.program_id(1)
        top, bottom = rider.at_steps(riding, (h == 0) & (ii == 0), (h == n_pairs - 1) & (ii == 0),
                                     (h == n_pairs - 1) & (ii == n_steps - 1))
        top()
        first = _head_masks()
        suffix_tri = suffix_ref[...]
        blocks = [ATT_CHAINS * ii + c for c in range(ATT_CHAINS)]
        qs = [q_ref[c * blk:(c + 1) * blk, :] * ATT_SCALE for c in range(ATT_CHAINS)]
        t_pos = [b * blk + lax.broadcasted_iota(jnp.int32, (blk, 1), 0) for b in blocks]

        def one(c, t, chain, on_diagonal):
            _, acc, right_a, right_b = chain
            k_start = pl.multiple_of((blocks[c] - t) * ATT_SLAB, ATT_SLAB)
            kd = _stack_heads(k_ref[pl.ds(k_start, ATT_SLAB), :], first)
            vd = _stack_heads(v_ref[pl.ds(k_start, ATT_SLAB), :], first)
            mask = _causal_mask(t_pos[c], k_start) if on_diagonal else None
            log_hit, log_fail = _slab_scores(qs[c], kd, mask)
            suffix, (right_a, right_b) = _scan_slab(log_fail, suffix_tri, (right_a, right_b), from_right=True)
            a = _weights(log_hit, suffix, mask).astype(BF16)
            acc = acc + jnp.dot(a, vd, preferred_element_type=F32)
            return jnp.max(jnp.maximum(right_a, right_b)), acc, right_a, right_b

        def step(state, on_diagonal):
            t, chains = state
            return t + 1, tuple(one(c, t, chains[c], on_diagonal) for c in range(ATT_CHAINS))

        def more(state):
            t, chains = state
            return (t <= blocks[0]) & (functools.reduce(jnp.maximum, [ch[0] for ch in chains]) > ATT_EXIT_BELOW)

        zero = jnp.zeros((blk, LANES), F32)
        state = step((0, ((jnp.float32(0.0), zero, zero, zero),) * ATT_CHAINS), on_diagonal=True)
        t, chains = lax.while_loop(more, functools.partial(step, on_diagonal=False), state)
        for c in range(ATT_CHAINS):
            chain = chains[c]
            if c:
                _, chain = lax.while_loop(
                    lambda s, c=c: (s[0] <= blocks[c]) & (s[1][0] > ATT_EXIT_BELOW),
                    lambda s, c=c: (s[0] + 1, one(c, s[0], s[1], False)), (t, chain))
            o_ref[c * blk:(c + 1) * blk, :] = chain[1].astype(BF16)
        bottom()

    rows = ATT_CHAINS * blk
    res = pl.pallas_call(
        body, name="attn_fwd", grid=(n_pairs, n_steps),
        in_specs=[pl.BlockSpec((rows, LANES), lambda h, i: (i, q_col + h)),
                  pl.BlockSpec((t_total, LANES), lambda h, i: (0, k_col + h)),
                  pl.BlockSpec((t_total, LANES), lambda h, i: (0, v_col + h)), _tri_spec()] + [ANY] * len(rider.operands),
        out_specs=[pl.BlockSpec((rows, LANES), lambda h, i: (i, h))] + [ANY] * len(rider.out_shapes),
        out_shape=[jax.ShapeDtypeStruct((t_total, n_pairs * LANES), BF16)] + list(rider.out_shapes),
        scratch_shapes=list(rider.scratch),
        compiler_params=_params("arbitrary", "arbitrary"),
    )(q_src, kv_src, kv_src, _tri(upper=True), *rider.operands)
    return res[0], res[1:]


def _attn_bwd(q_src, q_col, kv_src, k_col, v_col, dy, n_pairs=4, rider=_NoRider()):
    t_total = q_src.shape[0]
    blk = ATT_BLOCK
    n_steps = t_total // (ATT_CHAINS * blk)
    n_slabs = t_total // ATT_SLAB
    assert t_total % ATT_SLAB == 0 and ATT_SLAB == ATT_BLOCK

    def body(*refs):
        ins, (dq_ref, dk_ref, dv_ref), (g_s, dk_acc, dv_acc), riding = rider.split(refs, 6, 3)
        q_ref, dy_ref, k_ref, v_ref, suffix_ref, prefix_ref = ins
        h, ii = pl.program_id(0), pl.program_id(1)
        top, bottom = rider.at_steps(riding, (h == 0) & (ii == 0), (h == n_pairs - 1) & (ii == 0),
                                     (h == n_pairs - 1) & (ii == n_steps - 1))
        top()

        @pl.when(ii == 0)
        def _():
            dk_acc[...] = jnp.zeros_like(dk_acc)
            dv_acc[...] = jnp.zeros_like(dv_acc)

        first = _head_masks()
        suffix_tri = suffix_ref[...]
        prefix_tri = prefix_ref[...]
        blocks = [ATT_CHAINS * ii + c for c in range(ATT_CHAINS)]
        rows = [slice(c * blk, (c + 1) * blk) for c in range(ATT_CHAINS)]
        qs = [q_ref[r, :] * ATT_SCALE for r in rows]
        dys = [dy_ref[r, :] for r in rows]
        t_pos = [b * blk + lax.broadcasted_iota(jnp.int32, (blk, 1), 0) for b in blocks]

        def one1(c, t, chain, on_diagonal):
            _, right_a, right_b = chain
            slab = blocks[c] - t
            k_start = pl.multiple_of(slab * ATT_SLAB, ATT_SLAB)
            kd = _stack_heads(k_ref[pl.ds(k_start, ATT_SLAB), :], first)
            vd = _stack_heads(v_ref[pl.ds(k_start, ATT_SLAB), :], first)
            mask = _causal_mask(t_pos[c], k_start) if on_diagonal else None
            log_hit, log_fail = _slab_scores(qs[c], kd, mask)
            suffix, (right_a, right_b) = _scan_slab(log_fail, suffix_tri, (right_a, right_b), from_right=True)
            a = _weights(log_hit, suffix, mask)
            da = lax.dot_general(dys[c], vd, (((1,), (1,)), ((), ())), preferred_element_type=F32)
            g_s[c, slab] = (da * a).astype(BF16)
            dv_acc[pl.ds(k_start, ATT_SLAB), :] += _fold_heads(lax.dot_general(
                a.astype(BF16), dys[c], (((0,), (0,)), ((), ())), preferred_element_type=F32), first)
            return jnp.max(jnp.maximum(right_a, right_b)), right_a, right_b

        def step1(state, on_diagonal):
            t, chains = state
            return t + 1, tuple(one1(c, t, chains[c], on_diagonal) for c in range(ATT_CHAINS))

        def more(state):
            t, chains = state
            return (t <= blocks[0]) & (functools.reduce(jnp.maximum, [ch[0] for ch in chains]) > ATT_EXIT_BELOW)

        zero = jnp.zeros((blk, LANES), F32)
        state = step1((0, ((jnp.float32(0.0), zero, zero),) * ATT_CHAINS), on_diagonal=True)
        joint, chains = lax.while_loop(more, functools.partial(step1, on_diagonal=False), state)
        done = [joint]
        for c in range(1, ATT_CHAINS):
            done.append(lax.while_loop(
                lambda s, c=c: (s[0] <= blocks[c]) & (s[1][0] > ATT_EXIT_BELOW),
                lambda s, c=c: (s[0] + 1, one1(c, s[0], s[1], False)), (joint, chains[c]))[0])

        def one2(c, t, carry, on_diagonal):
            dq, left_a, left_b = carry
            slab = blocks[c] - t
            k_start = pl.multiple_of(slab * ATT_SLAB, ATT_SLAB)
            kd = _stack_heads(k_ref[pl.ds(k_start, ATT_SLAB), :], first)
            g = g_s[c, slab]
            z2 = lax.dot_general(qs[c], kd, (((1,), (1,)), ((), ())), preferred_element_type=F32) * LOG2_E
            sig = 1.0 / (1.0 + jnp.exp2(-z2))
            prefix, (left_a, left_b) = _scan_slab(g, prefix_tri, (left_a, left_b), from_right=False)
            dz = g * (1.0 - sig) - sig * prefix
            if on_diagonal:
                dz = jnp.where(_causal_mask(t_pos[c], k_start), dz, 0.0)
            dz = dz.astype(BF16)
            dq = dq + jnp.dot(dz, kd, preferred_element_type=F32)
            dk_acc[pl.ds(k_start, ATT_SLAB), :] += _fold_heads(lax.dot_general(
                dz, qs[c], (((0,), (0,)), ((), ())), preferred_element_type=F32), first)
            return dq, left_a, left_b

        carries = [(zero, zero, zero)]
        for c in range(1, ATT_CHAINS):
            carries.append(lax.fori_loop(
                0, done[c] - joint, lambda n, carry, c=c: one2(c, done[c] - 1 - n, carry, False), (zero, zero, zero)))
        carries = lax.fori_loop(
            0, joint - 1,
            lambda n, cs: tuple(one2(c, joint - 1 - n, cs[c], False) for c in range(ATT_CHAINS)), tuple(carries))
        for c in range(ATT_CHAINS):
            dq_ref[rows[c], :] = (one2(c, 0, carries[c], True)[0] * ATT_SCALE).astype(BF16)

        @pl.when(ii == n_steps - 1)
        def _():
            dk_ref[...] = dk_acc[...].astype(BF16)
            dv_ref[...] = dv_acc[...].astype(BF16)

        bottom()

    out = jax.ShapeDtypeStruct((t_total, n_pairs * LANES), BF16)
    n_rows = ATT_CHAINS * blk
    whole = dict(pipeline_mode=pl.Buffered(1))
    res = pl.pallas_call(
        body, name="attn_bwd", grid=(n_pairs, n_steps),
        in_specs=[pl.BlockSpec((n_rows, LANES), lambda h, i: (i, q_col + h)),
                  pl.BlockSpec((n_rows, LANES), lambda h, i: (i, h)),
                  pl.BlockSpec((t_total, LANES), lambda h, i: (0, k_col + h), **whole),
                  pl.BlockSpec((t_total, LANES), lambda h, i: (0, v_col + h), **whole), _tri_spec(), _tri_spec()]
        + [ANY] * len(rider.operands),
        out_specs=[pl.BlockSpec((n_rows, LANES), lambda h, i: (i, h)),
                   pl.BlockSpec((t_total, LANES), lambda h, i: (0, h)),
                   pl.BlockSpec((t_total, LANES), lambda h, i: (0, h))] + [ANY] * len(rider.out_shapes),
        out_shape=[out, out, out] + list(rider.out_shapes),
        scratch_shapes=list(rider.scratch) + [pltpu.VMEM((ATT_CHAINS, n_slabs, blk, 2 * ATT_SLAB), BF16),
                                              pltpu.VMEM((t_total, LANES), F32), pltpu.VMEM((t_total, LANES), F32)],
        compiler_params=_params("arbitrary", "arbitrary"),
    )(q_src, dy, kv_src, kv_src, _tri(upper=True), _tri(upper=False), *rider.operands)
    return res[:3], res[3:]


def _adamw(name, w, g, m, v):
    def fn(wv, gv, mv, vv):
        mn = ADAM_B1 * mv + (1.0 - ADAM_B1) * gv
        vn = ADAM_B2 * vv + (1.0 - ADAM_B2) * (gv * gv)
        m_hat = mn / (1.0 - ADAM_B1 ** ADAM_STEP)
        v_hat = vn / (1.0 - ADAM_B2 ** ADAM_STEP)
        return -ADAM_LR * (m_hat / (jnp.sqrt(v_hat) + ADAM_EPS) + ADAM_WD * wv), mn, vn

    rows = w.shape[0]
    tr = _row_tile(rows)
    shp = jax.ShapeDtypeStruct(w.shape, F32)
    if rows == 1:
        def body(w_ref, g_ref, m_ref, v_ref, d_ref, mo_ref, vo_ref):
            d, mn, vn = fn(w_ref[...], g_ref[...], m_ref[...], v_ref[...])
            d_ref[...], mo_ref[...], vo_ref[...] = d, mn, vn

        return pl.pallas_call(body, name=name, out_shape=[shp, shp, shp])(w, g, m, v)
    return _rows(name, fn, [w, g, m, v], [shp, shp, shp], tr=tr)


def _place():
    return lax.axis_index("x"), lax.axis_index("y"), lax.axis_index("c")


def _other_chips(x, y):
    return [(1 - x, y), (x, 1 - y), (1 - x, 1 - y)]


ANY = pl.BlockSpec(memory_space=pl.ANY)


def _remote(src, dst, send_sem, recv_sem, to):
    return pltpu.make_async_remote_copy(src_ref=src, dst_ref=dst, send_sem=send_sem, recv_sem=recv_sem,
                                        device_id=to, device_id_type=MESH)


class _WeightGather(_NoRider):
    def __init__(self, shards):
        n_w = len(shards)
        self.operands = list(shards)
        self.out_shapes = [jax.ShapeDtypeStruct((N_CHIPS,) + s.shape, s.dtype) for s in shards]
        self.scratch = [pltpu.SemaphoreType.DMA((3, n_w))] * 4 + [pltpu.SemaphoreType.DMA((n_w,))] * 2

    def _copies(self, ins, outs, sems):
        send_sems, recv_sems, relay_send, relay_recv, own_send, own_recv = sems
        x, y, c = _place()
        my_chip, sibling = 2 * x + y, (x, y, 1 - c)
        n_w = len(ins)

        def half(w, chip, core):
            h = self.operands[w].shape[0] // 2
            return outs[w].at[chip, pl.ds(core * h, h)]

        own = [_remote(ins[w], outs[w].at[my_chip], own_send.at[w], own_recv.at[w], sibling) for w in range(n_w)]
        sends, landed, relays, relayed = [], [], [], []
        for p, (ox, oy) in enumerate(_other_chips(x, y)):
            for w in range(n_w):
                h = self.operands[w].shape[0] // 2
                sends.append(_remote(ins[w].at[pl.ds(c * h, h)], half(w, my_chip, c), send_sems.at[p, w],
                                     recv_sems.at[p, w], (ox, oy, c)))
                here = half(w, 2 * ox + oy, c)
                landed.append(_remote(here, here, send_sems.at[p, w], recv_sems.at[p, w], (ox, oy, c)))
                relays.append(_remote(here, here, relay_send.at[p, w], relay_recv.at[p, w], sibling))
                there = half(w, 2 * ox + oy, 1 - c)
                relayed.append(_remote(there, there, relay_send.at[p, w], relay_recv.at[p, w], sibling))
        return own, sends, landed, relays, relayed

    def start(self, ins, outs, sems):
        own, sends, _, _, _ = self._copies(ins, outs, sems)
        for cp in own + sends:
            cp.start()

    def relay(self, ins, outs, sems):
        _, _, landed, relays, _ = self._copies(ins, outs, sems)
        for arrival, cp in zip(landed, relays):
            arrival.wait_recv()
            cp.start()

    def finish(self, ins, outs, sems):
        own, sends, _, relays, relayed = self._copies(ins, outs, sems)
        for arrival in relayed:
            arrival.wait_recv()
        for cp in sends + relays:
            cp.wait_send()
        for cp in own:
            cp.wait()


class _ChipExchange(_NoRider):
    def __init__(self, pair_sums):
        n_w = len(pair_sums)
        self.operands = list(pair_sums)
        self.out_shapes = [jax.ShapeDtypeStruct((3,) + s.shape[1:], s.dtype) for s in pair_sums]
        self.scratch = [pltpu.SemaphoreType.DMA((3, n_w))] * 2

    def _copies(self, ins, outs, sems):
        send_sems, recv_sems = sems
        x, y, c = _place()
        return [_remote(ins[w].at[2 * ox + oy], outs[w].at[p], send_sems.at[p, w], recv_sems.at[p, w], (ox, oy, c))
                for p, (ox, oy) in enumerate(_other_chips(x, y)) for w in range(len(ins))]

    def start(self, ins, outs, sems):
        for cp in self._copies(ins, outs, sems):
            cp.start()

    def finish(self, ins, outs, sems):
        for cp in self._copies(ins, outs, sems):
            cp.wait()


def _pair_exchange(name, grads):
    n_w = len(grads)

    def halves(w):
        return grads[w].shape[-2] // 2

    def body(*refs):
        ins, theirs = refs[:n_w], refs[n_w:2 * n_w]
        send_sems, recv_sems = refs[2 * n_w:]
        x, y, c = _place()
        sends = []
        for w in range(n_w):
            rows = pl.ds((1 - c) * halves(w), halves(w))
            src = ins[w].at[:, rows, :] if grads[w].ndim == 3 else ins[w].at[rows, :]
            sends.append(_remote(src, theirs[w], send_sems.at[w], recv_sems.at[w], (x, y, 1 - c)))
        for cp in sends:
            cp.start()
        for cp in sends:
            cp.wait()

    return pl.pallas_call(
        body, name=name, in_specs=[ANY] * n_w, out_specs=[ANY] * n_w,
        out_shape=[jax.ShapeDtypeStruct(g.shape[:-2] + (halves(w), g.shape[-1]), F32) for w, g in enumerate(grads)],
        scratch_shapes=[pltpu.SemaphoreType.DMA((n_w,)), pltpu.SemaphoreType.DMA((n_w,))],
    )(*grads)


def _pair_share(shards):
    n_w = len(shards)

    def body(*refs):
        ins, outs = refs[:n_w], refs[n_w:2 * n_w]
        send_sems, recv_sems = refs[2 * n_w:]
        x, y, c = _place()
        sends = []
        for w in range(n_w):
            h = shards[w].shape[0] // 2
            mine = outs[w].at[pl.ds(c * h, h)]
            sends.append(pltpu.make_async_remote_copy(
                src_ref=mine, dst_ref=mine, send_sem=send_sems.at[w], recv_sem=recv_sems.at[w],
                device_id=(x, y, 1 - c), device_id_type=MESH))
        for cp in sends:
            cp.start()
        for w in range(n_w):
            h = shards[w].shape[0] // 2
            theirs = outs[w].at[pl.ds((1 - c) * h, h)]
            pltpu.make_async_remote_copy(
                src_ref=theirs, dst_ref=theirs, send_sem=send_sems.at[w], recv_sem=recv_sems.at[w],
                device_id=(x, y, 1 - c), device_id_type=MESH).wait_recv()
        for cp in sends:
            cp.wait_send()

    return pl.pallas_call(
        body, name="pair_share", in_specs=[ANY] * n_w, out_specs=[ANY] * n_w,
        out_shape=[jax.ShapeDtypeStruct(s.shape, s.dtype) for s in shards],
        input_output_aliases={w: w for w in range(n_w)},
        scratch_shapes=[pltpu.SemaphoreType.DMA((n_w,)), pltpu.SemaphoreType.DMA((n_w,))],
    )(*shards)


def _all_reduce_small(vec):
    rows = vec.shape[0]

    def body(v_ref, o_ref, slots, send_sems, recv_sems):
        x, y, c = _place()
        me = 4 * x + 2 * y + c
        slots[me] = v_ref[...]
        sends = []
        for k in range(1, N_DEV):
            peer = (x ^ (k >> 2), y ^ ((k >> 1) & 1), c ^ (k & 1))
            sends.append(pltpu.make_async_remote_copy(
                src_ref=v_ref, dst_ref=slots.at[me], send_sem=send_sems.at[k - 1], recv_sem=recv_sems.at[k - 1],
                device_id=peer, device_id_type=MESH))
        for cp in sends:
            cp.start()
        for k in range(1, N_DEV):
            px, py, pc = x ^ (k >> 2), y ^ ((k >> 1) & 1), c ^ (k & 1)
            landed = slots.at[4 * px + 2 * py + pc]
            pltpu.make_async_remote_copy(
                src_ref=landed, dst_ref=landed, send_sem=send_sems.at[k - 1], recv_sem=recv_sems.at[k - 1],
                device_id=(px, py, pc), device_id_type=MESH).wait_recv()
        for cp in sends:
            cp.wait_send()
        total = slots[0]
        for d in range(1, N_DEV):
            total = total + slots[d]
        o_ref[...] = total

    vm = pl.BlockSpec(memory_space=pltpu.VMEM)
    return pl.pallas_call(
        body, name="all_reduce_small", in_specs=[vm], out_specs=vm, out_shape=jax.ShapeDtypeStruct(vec.shape, F32),
        scratch_shapes=[pltpu.VMEM((N_DEV, rows, LANES), F32), pltpu.SemaphoreType.DMA((N_DEV - 1,)),
                        pltpu.SemaphoreType.DMA((N_DEV - 1,))],
    )(vec)


def _row_tile(rows):
    fits = [tr for tr in range(16, min(rows, 512) + 1, 16) if rows % tr == 0]
    return max(fits) if fits else rows


def _pair_sum(name, place, grad, theirs):
    if grad.ndim == 2:
        return _pair_sum_joined(name, place, grad, theirs)
    n, r, c = grad.shape
    half = r // 2
    tr = _row_tile(half)
    nb = half // tr

    def body(place_ref, g_ref, t_ref, o_ref):
        o_ref[...] = (g_ref[...] + t_ref[...]).astype(BF16)

    return pl.pallas_call(
        body, name=name, out_shape=jax.ShapeDtypeStruct((n, half, c), BF16),
        grid_spec=pltpu.PrefetchScalarGridSpec(
            num_scalar_prefetch=1, grid=(n, nb),
            in_specs=[pl.BlockSpec((1, tr, c), lambda j, i, pr: (j, pr[0] * nb + i, 0)),
                      pl.BlockSpec((1, tr, c), lambda j, i, pr: (j, i, 0))],
            out_specs=pl.BlockSpec((1, tr, c), lambda j, i, pr: (j, i, 0))),
        compiler_params=_params("parallel", "parallel"),
    )(place, grad, theirs)


def _pair_sum_joined(name, place, grad, theirs):
    r, wide = grad.shape
    half, c = r // 2, wide // N_CHIPS
    tr = _row_tile(half)
    nb = half // tr

    def body(place_ref, g_ref, t_ref, o_ref):
        for j in range(N_CHIPS):
            cols = slice(j * c, (j + 1) * c)
            o_ref[j] = (g_ref[:, cols] + t_ref[:, cols]).astype(BF16)

    return pl.pallas_call(
        body, name=name, out_shape=jax.ShapeDtypeStruct((N_CHIPS, half, c), BF16),
        grid_spec=pltpu.PrefetchScalarGridSpec(
            num_scalar_prefetch=1, grid=(nb,),
            in_specs=[pl.BlockSpec((tr, wide), lambda i, pr: (pr[0] * nb + i, 0)),
                      pl.BlockSpec((tr, wide), lambda i, pr: (i, 0))],
            out_specs=pl.BlockSpec((N_CHIPS, tr, c), lambda i, pr: (0, i, 0))),
        compiler_params=_params("parallel"),
    )(place, grad, theirs)


def _sum_chips(name, place, pair_sums, landed):
    _, half, c = pair_sums.shape
    tr = _row_tile(half)
    nb = half // tr

    def body(place_ref, s_ref, q_ref, o_ref):
        total = s_ref[0].astype(F32)
        for p in range(3):
            total = total + q_ref[p].astype(F32)
        o_ref[...] = total

    return pl.pallas_call(
        body, name=name, out_shape=jax.ShapeDtypeStruct((2 * half, c), F32),
        grid_spec=pltpu.PrefetchScalarGridSpec(
            num_scalar_prefetch=1, grid=(nb,),
            in_specs=[pl.BlockSpec((1, tr, c), lambda i, pr: (pr[1], i, 0)),
                      pl.BlockSpec((3, tr, c), lambda i, pr: (0, i, 0))],
            out_specs=pl.BlockSpec((tr, c), lambda i, pr: (pr[0] * nb + i, 0))),
        compiler_params=_params("parallel"),
    )(place, pair_sums, landed)


BIG = ("w_in", "w_branch_a", "w_branch_b", "w_out", "w_ffn_gate", "w_ffn_up", "w_ffn_down", "w_ple_gate", "w_ple_proj")
LATE = BIG[1:]
COLUMN_SHARDED = ("w_in", "w_branch_a", "w_branch_b", "w_ffn_gate", "w_ffn_up", "w_ple_proj")
SMALL = ("norm_mix", "w_pool", "pool_scale", "norm_ffn", "norm_ple", "norm_final")


def _join_columns(w4):
    return jnp.concatenate([w4[j] for j in range(N_CHIPS)], axis=1)


def _sds(shape, dtype):
    return jax.ShapeDtypeStruct(shape, dtype)


def _local_step(x, p, target, wf, small, gather_first=None, gather_late=None, exchange_early=None,
                exchange_last=None):
    t, d = x.shape
    w_pool_b = small["w_pool"].astype(BF16)
    dp = w_pool_b.shape[0] * w_pool_b.shape[1]

    h1, first = _norm_fwd("norm_mix", x, small["norm_mix"], rider=gather_first)
    w_in = first[0] if gather_first else wf["w_in"]
    u, q, kv, ga, gb = _mm(
        "proj", [h1], [w_in[j] for j in range(N_CHIPS)], "nn",
        [_sds((t, dp), F32), _sds((t, dp), BF16), _sds((t, d), BF16), _sds((t, d), BF16), _sds((t, d), BF16)],
        separate=True, epilogue=lambda uq, kv_, ga_, gb_: (uq[:, :dp], uq[:, dp:], kv_, ga_, gb_), tm=512)
    pooled, ya = _pool_fwd(u, w_pool_b, small["pool_scale"])
    n_pairs = dp // LANES
    yb, late = _attn_fwd(q, 0, kv, 0, n_pairs, n_pairs, rider=gather_late or _NoRider())
    wf = {**wf, **dict(zip(LATE, late))}
    w_gate, w_up = _join_columns(wf["w_ffn_gate"]), _join_columns(wf["w_ffn_up"])
    dff = w_gate.shape[1]
    w_down = wf["w_ffn_down"].reshape(dff, d)
    w_a, w_b, w_pp = _join_columns(wf["w_branch_a"]), _join_columns(wf["w_branch_b"]), _join_columns(wf["w_ple_proj"])
    w_out = wf["w_out"].reshape(d, d)
    w_pg = wf["w_ple_gate"].reshape(d, d)
    def residual_norm(branch, xv, g, w):
        xn = xv + jnp.dot(branch.astype(BF16), w, preferred_element_type=F32)
        return xn, xn * lax.rsqrt(jnp.mean(xn * xn, axis=-1, keepdims=True) + RMS_EPS) * g

    def mixer_tail(tav, tbv, gav, gbv, xv, g, w):
        merged = _sigmoid(gav) * tav + _sigmoid(gbv) * tbv
        return (tav, tbv, merged) + residual_norm(merged, xv, g, w)

    def ffn_tail(gv, uv, xv, g, w):
        act = gv * _sigmoid(gv) * uv
        return (gv, uv, act) + residual_norm(act, xv, g, w)

    stream = [_sds((t, d), F32), _sds((t, d), BF16)]
    ta, tb, merged, x1, h2 = _mm(
        "mixer_out", [ya, yb], [w_a, w_b], "nn", [_sds((t, d), BF16)] * 3 + stream,
        extras=[ga, gb, x, small["norm_ffn"]], wholes=[w_out], separate=True, epilogue=mixer_tail, tm=512)
    gate, up, act, x2, h3 = _mm(
        "ffn", [h2], [w_gate, w_up], "nn", [_sds((t, dff), BF16)] * 3 + stream,
        extras=[x1, small["norm_ple"]], wholes=[w_down], separate=True, epilogue=ffn_tail, tm=256)
    dx2, dx2_b, d_pp, d_gp, d_norm_final, loss_row, d_norm_ple = _mm(
        "ple_loss", [h3, p], [w_pg, w_pp], "nn", stream + [_sds((t, d), BF16)] * 2,
        extras=[x2, target, small["norm_final"].reshape(1, d), small["norm_ple"]], wholes=[w_pg], separate=True,
        epilogue=_ple_and_loss, sum_shapes=[_sds((1, d), F32)] * 3, tm=512)

    def through_norm(dh, xv, g, dres):
        dx, d_gain = _rms_norm_bwd(dh, xv, g)
        return dx + dres, dx + dres, d_gain

    gain_sum = [_sds((1, d), F32)]
    g_w_pp, g_w_pg = _mm_tn("g_ple", [p, h3], [d_pp, d_gp])

    def ffn_bwd(d_act, gv, uv, xv, g, dres, wg, wu):
        s = _sigmoid(gv)
        d_gate, d_up = d_act * uv * (s * (1.0 + gv * (1.0 - s))), d_act * (gv * s)
        nt = (((1,), (1,)), ((), ()))
        dh2 = (lax.dot_general(d_gate.astype(BF16), wg, nt, preferred_element_type=F32)
               + lax.dot_general(d_up.astype(BF16), wu, nt, preferred_element_type=F32))
        return (d_gate, d_up) + through_norm(dh2, xv, g, dres)

    d_gate, d_up, dx1, dx1_b, d_norm_ffn = _mm(
        "ffn_bwd", [dx2_b], [w_down], "nt", [_sds((t, dff), BF16)] * 2 + stream,
        extras=[gate, up, x1, small["norm_ffn"], dx2], wholes=[w_gate, w_up], epilogue=ffn_bwd,
        sum_shapes=gain_sum, tm=256)
    g_w_down, = _mm_tn("g_ffn_down", [act], [dx2_b], tmm=512)
    g_w_gate, g_w_up = _mm_tn("g_ffn_gate_up", [h2], [d_gate, d_up], n_blocks=2)

    def merge_bwd(acc, tav, tbv, gav, gbv):
        sa, sb = _sigmoid(gav), _sigmoid(gbv)
        return acc * sa, acc * sb, acc * tav * sa * (1.0 - sa), acc * tbv * sb * (1.0 - sb)

    d_ta, d_tb, d_ga, d_gb = _mm("d_merged", [dx1_b], [w_out], "nt", [_sds((t, d), BF16)] * 4,
                                 extras=[ta, tb, ga, gb], epilogue=merge_bwd, tm=512)
    g_w_out, g_w_a, g_w_b = _mm_tn("g_mixer", [merged, ya, yb], [dx1_b, d_ta, d_tb])
    d_ya, d_yb = _mm("d_branches", [d_ta, d_tb], [w_a, w_b], "nt", [_sds((t, dp), F32), _sds((t, dp), BF16)],
                     separate=True)
    d_u, g_w_pool, d_pool_scale = _pool_bwd(d_ya, pooled, w_pool_b, small["pool_scale"])
    big = {
        "w_branch_a": g_w_a, "w_branch_b": g_w_b, "w_out": g_w_out.reshape(wf["w_out"].shape),
        "w_ffn_gate": g_w_gate, "w_ffn_up": g_w_up, "w_ffn_down": g_w_down.reshape(wf["w_ffn_down"].shape),
        "w_ple_gate": g_w_pg.reshape(wf["w_ple_gate"].shape), "w_ple_proj": g_w_pp,
    }
    rider = exchange_early(big) if exchange_early else _NoRider()
    (d_q, d_k, d_v), early = _attn_bwd(q, 0, kv, 0, n_pairs, d_yb, n_pairs, rider=rider)
    d_proj = [(d_u, d_q), (d_k, d_v), d_ga, d_gb]
    big["w_in"], = _mm_tn("g_w_in", [h1], d_proj, tmm=512, stacked=True)
    rider = exchange_last(big["w_in"]) if exchange_last else _NoRider()
    res = _mm(
        "d_h1", d_proj, [w_in[j] for j in range(N_CHIPS)], "nt", [_sds((t, d), F32)],
        extras=[x, small["norm_mix"], dx1], epilogue=lambda dh, xv, g, dres: through_norm(dh, xv, g, dres)[1:],
        sum_shapes=gain_sum, tm=512, rider=rider)
    (grad_x, d_norm_mix), last = res if rider.operands else (res, ())
    small_g = {"norm_mix": d_norm_mix, "w_pool": g_w_pool, "pool_scale": d_pool_scale, "norm_ffn": d_norm_ffn,
               "norm_ple": d_norm_ple, "norm_final": d_norm_final}
    return grad_x, big, small_g, loss_row, early, last


def _split2(res, n):
    return res[:n], res[n:]


def _pack_small(small_g, loss_row):
    parts, layout = [], []
    for name in SMALL + ("loss",):
        v = (loss_row if name == "loss" else small_g[name]).reshape(-1, LANES)
        pad = (-v.shape[0]) % 8
        if pad:
            v = jnp.concatenate([v, jnp.zeros((pad, LANES), F32)], axis=0)
        layout.append((name, sum(q.shape[0] for q in parts), v.shape[0]))
        parts.append(v)
    return jnp.concatenate(parts, axis=0), layout


def kernel(x, p, norm_mix, w_in, w_pool, pool_scale, w_branch_a, w_branch_b, w_out, norm_ffn, w_ffn_gate, w_ffn_up, w_ffn_down, norm_ple, w_ple_gate, w_ple_proj, norm_final, loss_target, m_norm_mix, m_w_in, m_w_pool, m_pool_scale, m_w_branch_a, m_w_branch_b, m_w_out, m_norm_ffn, m_w_ffn_gate, m_w_ffn_up, m_w_ffn_down, m_norm_ple, m_w_ple_gate, m_w_ple_proj, m_norm_final, v_norm_mix, v_w_in, v_w_pool, v_pool_scale, v_w_branch_a, v_w_branch_b, v_w_out, v_norm_ffn, v_w_ffn_gate, v_w_ffn_up, v_w_ffn_down, v_norm_ple, v_w_ple_gate, v_w_ple_proj, v_norm_final):
    given = dict(locals())
    names = BIG + SMALL
    order = ("norm_mix", "w_in", "w_pool", "pool_scale", "w_branch_a", "w_branch_b", "w_out", "norm_ffn", "w_ffn_gate",
             "w_ffn_up", "w_ffn_down", "norm_ple", "w_ple_gate", "w_ple_proj", "norm_final")
    t, d = x.shape[1], x.shape[2]
    shard = {n: given[n][0] for n in BIG}
    small = {"norm_mix": norm_mix, "w_pool": w_pool[0], "pool_scale": pool_scale, "norm_ffn": norm_ffn,
             "norm_ple": norm_ple, "norm_final": norm_final}

    as_bf16 = {n: shard[n].astype(BF16) for n in BIG}

    place = jnp.stack([lax.axis_index("c"), 2 * lax.axis_index("x") + lax.axis_index("y")]).astype(jnp.int32)
    pair_sums = {}

    def exchange_early(ready):
        theirs = _pair_exchange("pair_exchange_early", [ready[n] for n in LATE])
        for n, other in zip(LATE, theirs):
            pair_sums[n] = _pair_sum(f"pair_sum_{n}", place, ready[n], other)
        return _ChipExchange([pair_sums[n] for n in LATE])

    def exchange_last(g_w_in):
        theirs, = _pair_exchange("pair_exchange_w_in", [g_w_in])
        pair_sums["w_in"] = _pair_sum("pair_sum_w_in", place, g_w_in, theirs)
        return _ChipExchange([pair_sums["w_in"]])

    grad_x, big_g, small_g, loss_row, early, last = _local_step(
        x.reshape(t, d), p.reshape(t, p.shape[-1]), loss_target.reshape(t, d), {}, small,
        gather_first=_WeightGather([as_bf16["w_in"]]),
        gather_late=_WeightGather([as_bf16[n] for n in LATE]), exchange_early=exchange_early,
        exchange_last=exchange_last)
    landed = dict(zip(LATE + ("w_in",), tuple(early) + tuple(last)))
    halves = [_sum_chips(f"chip_sum_{n}", place, pair_sums[n], landed[n]) for n in BIG]
    grads = dict(zip(BIG, _pair_share(halves)))

    packed, layout = _pack_small(small_g, loss_row)
    reduced = _all_reduce_small(packed)
    for name, start, rows in layout:
        if name == "loss":
            loss = jnp.sum(reduced[start:start + rows])
        else:
            n_el = small[name].size
            grads[name] = reduced[start:start + rows].reshape(-1)[:n_el]

    deltas, new_m, new_v = {}, {}, {}
    for n in order:
        w = shard[n] if n in BIG else small[n]
        shape2 = w.shape if w.ndim == 2 else ((1, w.shape[0]) if w.ndim == 1 else (w.shape[0] * w.shape[1], w.shape[2]))
        g2 = grads[n].reshape(shape2)
        dl, mn, vn = _adamw(f"adamw_{n}", w.reshape(shape2), g2, given["m_" + n].reshape(shape2),
                            given["v_" + n].reshape(shape2))
        full = given[n].shape
        grads[n], deltas[n], new_m[n], new_v[n] = g2.reshape(full), dl.reshape(full), mn.reshape(full), vn.reshape(full)

    return (loss, grad_x.reshape(x.shape), *[grads[n] for n in order], *[deltas[n] for n in order],
            *[new_m[n] for n in order], *[new_v[n] for n in order])
```

```python
import functools
import math

import jax
import jax.numpy as jnp
from jax import lax
from jax.experimental import pallas as pl
from jax.experimental.pallas import tpu as pltpu

F32 = jnp.float32
BF16 = jnp.bfloat16
MESH = pl.DeviceIdType.MESH

RMS_EPS = 1e-6
POOL_WINDOWS = (2, 4, 8, 16)
POOL_HALO = 16
HEAD_DIM = 64
LANES = 128
ATT_BLOCK = 256
ATT_CHAINS = 2
ATT_CHUNK = 256
ATT_SLAB = 256
ATT_SCALE = 1.0 / math.sqrt(HEAD_DIM)
LOG2_E = 1.4426950408889634
ATT_EXIT_BELOW = -150.5
ADAM_LR, ADAM_B1, ADAM_B2, ADAM_EPS, ADAM_WD, ADAM_STEP = 0.001, 0.9, 0.999, 1e-08, 0.01, 10
V7X_VMEM_LIMIT_BYTES = 56 * 1024 * 1024
N_CHIPS = 4
N_DEV = 8


def _params(*semantics):
    return pltpu.CompilerParams(dimension_semantics=semantics, vmem_limit_bytes=V7X_VMEM_LIMIT_BYTES)


def _sigmoid(z):
    return 1.0 / (1.0 + jnp.exp(-z))


def _tiled_spec(shape, tm, tn, n_total, at):
    rows, width = shape
    if rows == 1:
        if width == n_total:
            return pl.BlockSpec((1, tn), at(lambda i, j: (0, j)))
        return pl.BlockSpec((1, width), at(lambda i, j: (0, 0)))
    if width == n_total:
        return pl.BlockSpec((tm, tn), at(lambda i, j: (i, j)))
    assert tn == n_total, "an operand narrower than the output needs whole output rows per tile"
    return pl.BlockSpec((tm, width), at(lambda i, j: (i, 0)))


def _column_pieces(operands):
    pieces = [tuple(a) if isinstance(a, (tuple, list)) else (a,) for a in operands]
    return [p for ps in pieces for p in ps], [len(ps) for ps in pieces]


def _load_bf16(refs, counts):
    tiles, k = [], 0
    for n in counts:
        parts = [r[...] for r in refs[k:k + n]]
        parts = [t if t.dtype == BF16 else t.astype(BF16) for t in parts]
        tiles.append(parts[0] if n == 1 else jnp.concatenate(parts, axis=1))
        k += n
    return tiles


def _mm(name, a_list, b_list, mode, out_shapes, epilogue=None, extras=(), tm=1024, tn=None, separate=False,
        sum_shapes=(), rider=None, wholes=()):
    flat_a, counts = _column_pieces(a_list)
    m_total = flat_a[0].shape[0]
    n_total = b_list[0].shape[1] if mode == "nn" else b_list[0].shape[0]
    tn = n_total if tn is None else tn
    tm = min(tm, m_total)
    assert m_total % tm == 0 and n_total % tn == 0 and (not sum_shapes or tn == n_total)
    n_a, n_b, n_extra, n_out = len(counts), len(b_list), len(extras), len(out_shapes)
    assert n_a in (1, n_b)
    dims = (((1,), (0,)), ((), ())) if mode == "nn" else (((1,), (1,)), ((), ()))
    rider = rider or _NoRider()
    grid = (n_total // tn, m_total // tm)

    def at(index):
        return lambda j, i: index(i, j)

    def body(*refs):
        ins, o_refs, _, riding = rider.split(refs, len(flat_a) + n_b + n_extra + len(wholes), n_out + len(sum_shapes))
        a_refs, b_refs = ins[:len(flat_a)], ins[len(flat_a):len(flat_a) + n_b]
        e_refs, w_refs = ins[len(flat_a) + n_b:len(flat_a) + n_b + n_extra], ins[len(flat_a) + n_b + n_extra:]
        at_first = (pl.program_id(0) == 0) & (pl.program_id(1) == 0)
        at_last = (pl.program_id(0) == grid[0] - 1) & (pl.program_id(1) == grid[1] - 1)
        top, bottom = rider.at_steps(riding, at_first, at_first, at_last)
        top()
        lefts = _load_bf16(a_refs, counts)
        products = [lax.dot_general(lefts[s % n_a], b_refs[s][...], dims, preferred_element_type=F32)
                    for s in range(n_b)]
        if not separate:
            products = [functools.reduce(lambda p, r: p + r, products)]
        extra_tiles = [e[...].astype(F32) for e in e_refs]
        outs = products if epilogue is None else epilogue(*products, *extra_tiles, *[w[...] for w in w_refs])
        for o_ref, o in zip(o_refs[:n_out], outs[:n_out]):
            o_ref[...] = o.astype(o_ref.dtype)
        if sum_shapes:
            @pl.when(pl.program_id(1) == 0)
            def _():
                for s_ref in o_refs[n_out:]:
                    s_ref[...] = jnp.zeros_like(s_ref)

            for s_ref, s in zip(o_refs[n_out:], outs[n_out:]):
                s_ref[...] += s
        bottom()

    once = dict(pipeline_mode=pl.Buffered(1)) if tn == n_total else {}
    in_specs = [pl.BlockSpec((tm, a.shape[1]), at(lambda i, j: (i, 0))) for a in flat_a]
    if mode == "nn":
        in_specs += [pl.BlockSpec((b.shape[0], tn), at(lambda i, j: (0, j)), **once) for b in b_list]
    else:
        in_specs += [pl.BlockSpec((tn, b.shape[1]), at(lambda i, j: (j, 0)), **once) for b in b_list]
    in_specs += [_tiled_spec(e.shape, tm, tn, n_total, at) for e in extras]
    in_specs += [pl.BlockSpec(w.shape, lambda j, i: (0, 0), pipeline_mode=pl.Buffered(1)) for w in wholes]
    out_specs = [_tiled_spec(o.shape, tm, tn, n_total, at) for o in out_shapes]
    out_specs += [pl.BlockSpec(s.shape, at(lambda i, j: (0, 0))) for s in sum_shapes]
    semantics = ("arbitrary", "arbitrary") if sum_shapes or rider.operands else ("parallel", "parallel")
    res = pl.pallas_call(
        body, name=name, grid=grid, in_specs=in_specs + [ANY] * len(rider.operands),
        out_specs=out_specs + [ANY] * len(rider.out_shapes),
        out_shape=list(out_shapes) + list(sum_shapes) + list(rider.out_shapes), scratch_shapes=list(rider.scratch),
        compiler_params=_params(*semantics),
    )(*flat_a, *b_list, *extras, *wholes, *rider.operands)
    n_own = len(out_shapes) + len(sum_shapes)
    return res if not rider.operands else (res[:n_own], res[n_own:])


def _mm_tn(name, a_list, b_list, tmm=1024, stacked=False, k_blocks=1):
    flat_b, counts = _column_pieces(b_list)
    n_a, n_b = len(a_list), len(counts)
    n_prod = max(n_a, n_b)
    m_total = a_list[0].shape[0]
    ks = [a_list[s % n_a].shape[1] for s in range(n_prod)]
    widths = [sum(p.shape[1] for p in flat_b[sum(counts[:s]):sum(counts[:s + 1])]) for s in range(n_b)]
    widths = [widths[s % n_b] for s in range(n_prod)]
    tmm = min(tmm, m_total)
    assert m_total % tmm == 0 and all(k % k_blocks == 0 for k in ks)
    assert n_a in (1, n_prod) and n_b in (1, n_prod) and not (stacked and n_a > 1)

    def body(*refs):
        a_refs, b_refs, o_refs = refs[:n_a], refs[n_a:n_a + len(flat_b)], refs[n_a + len(flat_b):]

        @pl.when(pl.program_id(1) == 0)
        def _():
            for o_ref in o_refs:
                o_ref[...] = jnp.zeros_like(o_ref)

        lefts, rights = _load_bf16(a_refs, [1] * n_a), _load_bf16(b_refs, counts)
        for s in range(n_prod):
            product = lax.dot_general(lefts[s % n_a], rights[s % n_b], (((0,), (0,)), ((), ())),
                                      preferred_element_type=F32)
            if stacked:
                o_refs[0][s] += product
            else:
                o_refs[s][...] += product

    in_specs = [pl.BlockSpec((tmm, a.shape[1] // k_blocks), lambda kb, m: (m, kb)) for a in a_list]
    in_specs += [pl.BlockSpec((tmm, b.shape[1]), lambda kb, m: (m, 0)) for b in flat_b]
    if stacked:
        out_shape = [jax.ShapeDtypeStruct((n_prod, ks[0], widths[0]), F32)]
        out_specs = [pl.BlockSpec((n_prod, ks[0] // k_blocks, widths[0]), lambda kb, m: (0, kb, 0))]
    else:
        out_shape = [jax.ShapeDtypeStruct((k, w), F32) for k, w in zip(ks, widths)]
        out_specs = [pl.BlockSpec((k // k_blocks, w), lambda kb, m: (kb, 0)) for k, w in zip(ks, widths)]
    return pl.pallas_call(
        body, name=name, grid=(k_blocks, m_total // tmm), in_specs=in_specs, out_specs=out_specs, out_shape=out_shape,
        compiler_params=_params("arbitrary", "arbitrary"),
    )(*a_list, *flat_b)


def _rows(name, fn, ins, tile_outs, sum_outs=(), tr=512, rider=None):
    t_total = max(a.shape[0] for a in ins)
    tr = min(tr, t_total)
    assert t_total % tr == 0
    n_in, n_tile = len(ins), len(tile_outs)
    rider = rider or _NoRider()
    n_steps = t_total // tr

    def body(*refs):
        own_ins, own_outs, _, riding = rider.split(refs, n_in, n_tile + len(sum_outs))
        step = pl.program_id(0)
        top, bottom = rider.at_steps(riding, step == 0, step == n_steps - 1, step == n_steps - 1)
        top()
        refs = tuple(own_ins) + tuple(own_outs)
        outs = fn(*[r[...].astype(F32) for r in refs[:n_in]])
        for o_ref, o in zip(refs[n_in:n_in + n_tile], outs[:n_tile]):
            o_ref[...] = o.astype(o_ref.dtype)
        if sum_outs:
            @pl.when(pl.program_id(0) == 0)
            def _():
                for s_ref in refs[n_in + n_tile:]:
                    s_ref[...] = jnp.zeros_like(s_ref)

            for s_ref, s in zip(refs[n_in + n_tile:], outs[n_tile:]):
                s_ref[...] += s
        bottom()

    def spec(shape):
        if shape[0] == 1:
            return pl.BlockSpec(shape, lambda i: (0, 0))
        return pl.BlockSpec((tr, shape[1]), lambda i: (i, 0))

    return pl.pallas_call(
        body, name=name, grid=(n_steps,), in_specs=[spec(a.shape) for a in ins] + [ANY] * len(rider.operands),
        out_specs=[spec(o.shape) for o in tile_outs] + [spec(s.shape) for s in sum_outs] + [ANY] * len(rider.out_shapes),
        out_shape=list(tile_outs) + list(sum_outs) + list(rider.out_shapes), scratch_shapes=list(rider.scratch),
        compiler_params=_params("arbitrary" if sum_outs or rider.operands else "parallel"),
    )(*ins, *rider.operands)


def _norm_fwd(name, x, gain, rider=None):
    def fn(xv, g):
        inv = lax.rsqrt(jnp.mean(xv * xv, axis=-1, keepdims=True) + RMS_EPS)
        return (xv * inv * g,)

    res = _rows(name, fn, [x, gain], [jax.ShapeDtypeStruct(x.shape, BF16)], rider=rider)
    return res[0], res[1:]


def _rms_norm_bwd(dh, xv, g):
    inv = lax.rsqrt(jnp.mean(xv * xv, axis=-1, keepdims=True) + RMS_EPS)
    xn = xv * inv
    dxn = dh * g
    return inv * (dxn - xn * jnp.mean(dxn * xn, axis=-1, keepdims=True)), jnp.sum(dh * xn, axis=0, keepdims=True)


def _ple_and_loss(gv, pv, x2v, tv, g_final, g_ple, w_pg):
    d = x2v.shape[1]
    s = _sigmoid(gv)
    xv = x2v + s * pv
    inv = lax.rsqrt(jnp.mean(xv * xv, axis=-1, keepdims=True) + RMS_EPS)
    err = xv * inv * g_final - tv
    dx3, d_final = _rms_norm_bwd(err * (1.0 / d), xv, g_final)
    d_pp, d_gp = dx3 * s, dx3 * pv * s * (1.0 - s)
    dh3 = lax.dot_general(d_gp.astype(BF16), w_pg, (((1,), (1,)), ((), ())), preferred_element_type=F32)
    dx2, d_ple = _rms_norm_bwd(dh3, x2v, g_ple)
    dx2 = dx2 + dx3
    return dx2, dx2, d_pp, d_gp, d_final, (0.5 / d) * jnp.sum(err * err, axis=0, keepdims=True), d_ple


def _window_counts(t_pos, w):
    return jnp.minimum(t_pos + 1, w).astype(F32)


def _pool_fwd(u, w_pool, scale, tr=512):
    t_total, width = u.shape
    tr = min(tr, t_total)
    n_groups = len(POOL_WINDOWS)
    gdim = width // n_groups
    ext = tr + POOL_HALO

    def body(u_ref, halo_ref, w_ref, s_ref, pooled_ref, ya_ref):
        i = pl.program_id(0)
        halo = jnp.where(i == 0, 0.0, halo_ref[...])
        t_pos = i * tr + lax.broadcasted_iota(jnp.int32, (tr, 1), 0)
        for g, w in enumerate(POOL_WINDOWS):
            cols = slice(g * gdim, (g + 1) * gdim)
            main = u_ref[:, cols]
            win = jnp.concatenate([halo[:, cols], main], axis=0)
            span = 1
            while span < w:
                win = win + pltpu.roll(win, span, 0)
                span *= 2
            pooled = win[POOL_HALO:, :] * (1.0 / _window_counts(t_pos, w)) - main
            pooled_b = pooled.astype(BF16)
            pooled_ref[:, cols] = pooled_b
            mixed = jnp.dot(pooled_b, w_ref[g], preferred_element_type=F32)
            ya_ref[:, cols] = (mixed * s_ref[:, cols]).astype(BF16)

    hb = tr // POOL_HALO
    return pl.pallas_call(
        body, name="pool_fwd", grid=(t_total // tr,),
        in_specs=[pl.BlockSpec((tr, width), lambda i: (i, 0)),
                  pl.BlockSpec((POOL_HALO, width), lambda i: (jnp.maximum(i * hb - 1, 0), 0)),
                  pl.BlockSpec((n_groups, gdim, gdim), lambda i: (0, 0, 0)),
                  pl.BlockSpec((1, width), lambda i: (0, 0))],
        out_specs=[pl.BlockSpec((tr, width), lambda i: (i, 0)), pl.BlockSpec((tr, width), lambda i: (i, 0))],
        out_shape=[jax.ShapeDtypeStruct(u.shape, BF16), jax.ShapeDtypeStruct(u.shape, BF16)],
        compiler_params=_params("parallel"),
    )(u, u, w_pool, scale)


def _pool_bwd(dya, pooled, w_pool, scale, tr=512):
    t_total, width = dya.shape
    tr = min(tr, t_total)
    n_groups = len(POOL_WINDOWS)
    gdim = width // n_groups
    ext = tr + POOL_HALO
    n_tiles = t_total // tr

    def body(d_ref, halo_ref, p_ref, w_ref, s_ref, du_ref, dw_ref, ds_ref):
        i = pl.program_id(0)

        @pl.when(i == 0)
        def _():
            dw_ref[...] = jnp.zeros_like(dw_ref)
            ds_ref[...] = jnp.zeros_like(ds_ref)

        halo = jnp.where(i == n_tiles - 1, 0.0, halo_ref[...])
        t_pos = i * tr + lax.broadcasted_iota(jnp.int32, (ext, 1), 0)
        for g, w in enumerate(POOL_WINDOWS):
            cols = slice(g * gdim, (g + 1) * gdim)
            sc = s_ref[:, cols]
            d_main = d_ref[:, cols]
            pooled_b = p_ref[:, cols]
            mixed = jnp.dot(pooled_b, w_ref[g], preferred_element_type=F32)
            ds_ref[:, cols] += jnp.sum(d_main * mixed, axis=0, keepdims=True)
            dmix = (jnp.concatenate([d_main, halo[:, cols]], axis=0) * sc).astype(BF16)
            dw_ref[g] += lax.dot_general(pooled_b, dmix[:tr, :], (((0,), (0,)), ((), ())),
                                         preferred_element_type=F32)
            dpool = lax.dot_general(dmix, w_ref[g], (((1,), (1,)), ((), ())), preferred_element_type=F32)
            win = dpool * (1.0 / _window_counts(t_pos, w))
            span = 1
            while span < w:
                win = win + pltpu.roll(win, ext - span, 0)
                span *= 2
            du_ref[:, cols] = (win[:tr, :] - dpool[:tr, :]).astype(BF16)

    hb = tr // POOL_HALO
    last_halo = t_total // POOL_HALO - 1
    return pl.pallas_call(
        body, name="pool_bwd", grid=(n_tiles,),
        in_specs=[pl.BlockSpec((tr, width), lambda i: (i, 0)),
                  pl.BlockSpec((POOL_HALO, width), lambda i: (jnp.minimum((i + 1) * hb, last_halo), 0)),
                  pl.BlockSpec((tr, width), lambda i: (i, 0)),
                  pl.BlockSpec((n_groups, gdim, gdim), lambda i: (0, 0, 0)),
                  pl.BlockSpec((1, width), lambda i: (0, 0))],
        out_specs=[pl.BlockSpec((tr, width), lambda i: (i, 0)),
                   pl.BlockSpec((n_groups, gdim, gdim), lambda i: (0, 0, 0)),
                   pl.BlockSpec((1, width), lambda i: (0, 0))],
        out_shape=[jax.ShapeDtypeStruct(dya.shape, BF16), jax.ShapeDtypeStruct((n_groups, gdim, gdim), F32),
                   jax.ShapeDtypeStruct((1, width), F32)],
        compiler_params=_params("arbitrary"),
    )(dya, dya, pooled, w_pool, scale)


def _head_masks():
    lane = lax.broadcasted_iota(jnp.int32, (1, LANES), 1)
    return lane < HEAD_DIM


def _stack_heads(tile, first):
    zero = jnp.zeros_like(tile)
    return jnp.concatenate([jnp.where(first, tile, zero), jnp.where(first, zero, tile)], axis=0)


def _causal_mask(t_pos, k_start):
    col = lax.broadcasted_iota(jnp.int32, (1, 2 * ATT_SLAB), 1)
    return k_start + (col & (ATT_SLAB - 1)) < t_pos


def _slab_scores(q, kd, mask):
    z2 = lax.dot_general(q, kd, (((1,), (1,)), ((), ())), preferred_element_type=F32) * LOG2_E
    log_hit = jnp.minimum(z2, 0.0) - jnp.log2(1.0 + jnp.exp2(-jnp.abs(z2)))
    log_fail = log_hit - z2
    return log_hit, (log_fail if mask is None else jnp.where(mask, log_fail, 0.0))


def _weights(log_hit, suffix, mask):
    arg = log_hit + suffix
    return jnp.exp2(arg if mask is None else jnp.where(mask, arg, -1e30))


def _tri(upper):
    r = lax.broadcasted_iota(jnp.int32, (ATT_CHUNK, ATT_CHUNK), 0)
    c = lax.broadcasted_iota(jnp.int32, (ATT_CHUNK, ATT_CHUNK), 1)
    return jnp.where(r > c if upper else r < c, 1.0, 0.0).astype(BF16)


def _tri_spec():
    return pl.BlockSpec((ATT_CHUNK, ATT_CHUNK), lambda h, i: (0, 0), pipeline_mode=pl.Buffered(1))


def _scan_chunk(v, tri):
    return jnp.dot(v.astype(BF16), tri, preferred_element_type=F32)


def _lane_bcast(col):
    return jnp.broadcast_to(col, (col.shape[0], LANES))


def _scan_slab(v, tri, carries, from_right):
    n_chunks = ATT_SLAB // ATT_CHUNK
    edge = 0 if from_right else ATT_CHUNK - 1
    parts, new_carries = [None] * (2 * n_chunks), []
    for head in range(2):
        run = carries[head]
        for c in (reversed(range(n_chunks)) if from_right else range(n_chunks)):
            lo_col = head * ATT_SLAB + c * ATT_CHUNK
            vc = v[:, lo_col:lo_col + ATT_CHUNK]
            sc = _scan_chunk(vc, tri)
            parts[head * n_chunks + c] = sc + jnp.concatenate([run] * (ATT_CHUNK // LANES), axis=1)
            run = run + _lane_bcast(sc[:, edge:edge + 1] + vc[:, edge:edge + 1])
        new_carries.append(run)
    return jnp.concatenate(parts, axis=1), new_carries


def _fold_heads(stacked, first):
    s = stacked.shape[0] // 2
    return jnp.where(first, stacked[:s], stacked[s:])


class _NoRider:
    operands, out_shapes, scratch = (), (), ()

    def split(self, refs, n_base_in, n_base_out):
        n_in, n_out, n_sem = len(self.operands), len(self.out_shapes), len(self.scratch)
        a = n_base_in + n_in
        b = a + n_base_out + n_out
        mine = (refs[n_base_in:a], refs[a + n_base_out:b], refs[b:b + n_sem])
        return refs[:n_base_in], refs[a:a + n_base_out], refs[b + n_sem:], mine

    def start(self, ins, outs, sems):
        pass

    def relay(self, ins, outs, sems):
        pass

    def finish(self, ins, outs, sems):
        pass

    def at_steps(self, refs, first_step, relay_step, last_step):
        if not self.operands:
            return (lambda: None), (lambda: None)

        def top():
            pl.when(first_step)(lambda: self.start(*refs))
            pl.when(relay_step)(lambda: self.relay(*refs))

        return top, lambda: pl.when(last_step)(lambda: self.finish(*refs))


def _attn_fwd(q_src, q_col, kv_src, k_col, v_col, n_pairs=4, rider=_NoRider()):
    t_total = q_src.shape[0]
    blk = ATT_BLOCK
    n_steps = t_total // (ATT_CHAINS * blk)
    assert t_total % ATT_SLAB == 0 and ATT_SLAB == ATT_BLOCK

    def body(*refs):
        (q_ref, k_ref, v_ref, suffix_ref), (o_ref,), _, riding = rider.split(refs, 4, 1)
        h, ii = pl.program_id(0), pl.program_id(1)
        top, bottom = rider.at_steps(riding, (h == 0) & (ii == 0), (h == n_pairs - 1) & (ii == 0),
                                     (h == n_pairs - 1) & (ii == n_steps - 1))
        top()
        first = _head_masks()
        suffix_tri = suffix_ref[...]
        blocks = [ATT_CHAINS * ii + c for c in range(ATT_CHAINS)]
        qs = [q_ref[c * blk:(c + 1) * blk, :] * ATT_SCALE for c in range(ATT_CHAINS)]
        t_pos = [b * blk + lax.broadcasted_iota(jnp.int32, (blk, 1), 0) for b in blocks]

        def one(c, t, chain, on_diagonal):
            _, acc, right_a, right_b = chain
            k_start = pl.multiple_of((blocks[c] - t) * ATT_SLAB, ATT_SLAB)
            kd = _stack_heads(k_ref[pl.ds(k_start, ATT_SLAB), :], first)
            vd = _stack_heads(v_ref[pl.ds(k_start, ATT_SLAB), :], first)
            mask = _causal_mask(t_pos[c], k_start) if on_diagonal else None
            log_hit, log_fail = _slab_scores(qs[c], kd, mask)
            suffix, (right_a, right_b) = _scan_slab(log_fail, suffix_tri, (right_a, right_b), from_right=True)
            a = _weights(log_hit, suffix, mask).astype(BF16)
            acc = acc + jnp.dot(a, vd, preferred_element_type=F32)
            return jnp.max(jnp.maximum(right_a, right_b)), acc, right_a, right_b

        def step(state, on_diagonal):
            t, chains = state
            return t + 1, tuple(one(c, t, chains[c], on_diagonal) for c in range(ATT_CHAINS))

        def more(state):
            t, chains = state
            return (t <= blocks[0]) & (functools.reduce(jnp.maximum, [ch[0] for ch in chains]) > ATT_EXIT_BELOW)

        zero = jnp.zeros((blk, LANES), F32)
        state = step((0, ((jnp.float32(0.0), zero, zero, zero),) * ATT_CHAINS), on_diagonal=True)
        t, chains = lax.while_loop(more, functools.partial(step, on_diagonal=False), state)
        for c in range(ATT_CHAINS):
            chain = chains[c]
            if c:
                _, chain = lax.while_loop(
                    lambda s, c=c: (s[0] <= blocks[c]) & (s[1][0] > ATT_EXIT_BELOW),
                    lambda s, c=c: (s[0] + 1, one(c, s[0], s[1], False)), (t, chain))
            o_ref[c * blk:(c + 1) * blk, :] = chain[1].astype(BF16)
        bottom()

    rows = ATT_CHAINS * blk
    res = pl.pallas_call(
        body, name="attn_fwd", grid=(n_pairs, n_steps),
        in_specs=[pl.BlockSpec((rows, LANES), lambda h, i: (i, q_col + h)),
                  pl.BlockSpec((t_total, LANES), lambda h, i: (0, k_col + h)),
                  pl.BlockSpec((t_total, LANES), lambda h, i: (0, v_col + h)), _tri_spec()] + [ANY] * len(rider.operands),
        out_specs=[pl.BlockSpec((rows, LANES), lambda h, i: (i, h))] + [ANY] * len(rider.out_shapes),
        out_shape=[jax.ShapeDtypeStruct((t_total, n_pairs * LANES), BF16)] + list(rider.out_shapes),
        scratch_shapes=list(rider.scratch),
        compiler_params=_params("arbitrary", "arbitrary"),
    )(q_src, kv_src, kv_src, _tri(upper=True), *rider.operands)
    return res[0], res[1:]


def _attn_bwd(q_src, q_col, kv_src, k_col, v_col, dy, n_pairs=4, rider=_NoRider()):
    t_total = q_src.shape[0]
    blk = ATT_BLOCK
    n_steps = t_total // (ATT_CHAINS * blk)
    n_slabs = t_total // ATT_SLAB
    assert t_total % ATT_SLAB == 0 and ATT_SLAB == ATT_BLOCK

    def body(*refs):
        ins, (dq_ref, dk_ref, dv_ref), (g_s, dk_acc, dv_acc), riding = rider.split(refs, 6, 3)
        q_ref, dy_ref, k_ref, v_ref, suffix_ref, prefix_ref = ins
        h, ii = pl.program_id(0), pl.program_id(1)
        top, bottom = rider.at_steps(riding, (h == 0) & (ii == 0), (h == n_pairs - 1) & (ii == 0),
                                     (h == n_pairs - 1) & (ii == n_steps - 1))
        top()

        @pl.when(ii == 0)
        def _():
            dk_acc[...] = jnp.zeros_like(dk_acc)
            dv_acc[...] = jnp.zeros_like(dv_acc)

        first = _head_masks()
        suffix_tri = suffix_ref[...]
        prefix_tri = prefix_ref[...]
        blocks = [ATT_CHAINS * ii + c for c in range(ATT_CHAINS)]
        rows = [slice(c * blk, (c + 1) * blk) for c in range(ATT_CHAINS)]
        qs = [q_ref[r, :] * ATT_SCALE for r in rows]
        dys = [dy_ref[r, :] for r in rows]
        t_pos = [b * blk + lax.broadcasted_iota(jnp.int32, (blk, 1), 0) for b in blocks]

        def one1(c, t, chain, on_diagonal):
            _, right_a, right_b = chain
            slab = blocks[c] - t
            k_start = pl.multiple_of(slab * ATT_SLAB, ATT_SLAB)
            kd = _stack_heads(k_ref[pl.ds(k_start, ATT_SLAB), :], first)
            vd = _stack_heads(v_ref[pl.ds(k_start, ATT_SLAB), :], first)
            mask = _causal_mask(t_pos[c], k_start) if on_diagonal else None
            log_hit, log_fail = _slab_scores(qs[c], kd, mask)
            suffix, (right_a, right_b) = _scan_slab(log_fail, suffix_tri, (right_a, right_b), from_right=True)
            a = _weights(log_hit, suffix, mask)
            da = lax.dot_general(dys[c], vd, (((1,), (1,)), ((), ())), preferred_element_type=F32)
            g_s[c, slab] = (da * a).astype(BF16)
            dv_acc[pl.ds(k_start, ATT_SLAB), :] += _fold_heads(lax.dot_general(
                a.astype(BF16), dys[c], (((0,), (0,)), ((), ())), preferred_element_type=F32), first)
            return jnp.max(jnp.maximum(right_a, right_b)), right_a, right_b

        def step1(state, on_diagonal):
            t, chains = state
            return t + 1, tuple(one1(c, t, chains[c], on_diagonal) for c in range(ATT_CHAINS))

        def more(state):
            t, chains = state
            return (t <= blocks[0]) & (functools.reduce(jnp.maximum, [ch[0] for ch in chains]) > ATT_EXIT_BELOW)

        zero = jnp.zeros((blk, LANES), F32)
        state = step1((0, ((jnp.float32(0.0), zero, zero),) * ATT_CHAINS), on_diagonal=True)
        joint, chains = lax.while_loop(more, functools.partial(step1, on_diagonal=False), state)
        done = [joint]
        for c in range(1, ATT_CHAINS):
            done.append(lax.while_loop(
                lambda s, c=c: (s[0] <= blocks[c]) & (s[1][0] > ATT_EXIT_BELOW),
                lambda s, c=c: (s[0] + 1, one1(c, s[0], s[1], False)), (joint, chains[c]))[0])

        def one2(c, t, carry, on_diagonal):
            dq, left_a, left_b = carry
            slab = blocks[c] - t
            k_start = pl.multiple_of(slab * ATT_SLAB, ATT_SLAB)
            kd = _stack_heads(k_ref[pl.ds(k_start, ATT_SLAB), :], first)
            g = g_s[c, slab]
            z2 = lax.dot_general(qs[c], kd, (((1,), (1,)), ((), ())), preferred_element_type=F32) * LOG2_E
            sig = 1.0 / (1.0 + jnp.exp2(-z2))
            prefix, (left_a, left_b) = _scan_slab(g, prefix_tri, (left_a, left_b), from_right=False)
            dz = g * (1.0 - sig) - sig * prefix
            if on_diagonal:
                dz = jnp.where(_causal_mask(t_pos[c], k_start), dz, 0.0)
            dz = dz.astype(BF16)
            dq = dq + jnp.dot(dz, kd, preferred_element_type=F32)
            dk_acc[pl.ds(k_start, ATT_SLAB), :] += _fold_heads(lax.dot_general(
                dz, qs[c], (((0,), (0,)), ((), ())), preferred_element_type=F32), first)
            return dq, left_a, left_b

        carries = [(zero, zero, zero)]
        for c in range(1, ATT_CHAINS):
            carries.append(lax.fori_loop(
                0, done[c] - joint, lambda n, carry, c=c: one2(c, done[c] - 1 - n, carry, False), (zero, zero, zero)))
        carries = lax.fori_loop(
            0, joint - 1,
            lambda n, cs: tuple(one2(c, joint - 1 - n, cs[c], False) for c in range(ATT_CHAINS)), tuple(carries))
        for c in range(ATT_CHAINS):
            dq_ref[rows[c], :] = (one2(c, 0, carries[c], True)[0] * ATT_SCALE).astype(BF16)

        @pl.when(ii == n_steps - 1)
        def _():
            dk_ref[...] = dk_acc[...].astype(BF16)
            dv_ref[...] = dv_acc[...].astype(BF16)

        bottom()

    out = jax.ShapeDtypeStruct((t_total, n_pairs * LANES), BF16)
    n_rows = ATT_CHAINS * blk
    whole = dict(pipeline_mode=pl.Buffered(1))
    res = pl.pallas_call(
        body, name="attn_bwd", grid=(n_pairs, n_steps),
        in_specs=[pl.BlockSpec((n_rows, LANES), lambda h, i: (i, q_col + h)),
                  pl.BlockSpec((n_rows, LANES), lambda h, i: (i, h)),
                  pl.BlockSpec((t_total, LANES), lambda h, i: (0, k_col + h), **whole),
                  pl.BlockSpec((t_total, LANES), lambda h, i: (0, v_col + h), **whole), _tri_spec(), _tri_spec()]
        + [ANY] * len(rider.operands),
        out_specs=[pl.BlockSpec((n_rows, LANES), lambda h, i: (i, h)),
                   pl.BlockSpec((t_total, LANES), lambda h, i: (0, h)),
                   pl.BlockSpec((t_total, LANES), lambda h, i: (0, h))] + [ANY] * len(rider.out_shapes),
        out_shape=[out, out, out] + list(rider.out_shapes),
        scratch_shapes=list(rider.scratch) + [pltpu.VMEM((ATT_CHAINS, n_slabs, blk, 2 * ATT_SLAB), BF16),
                                              pltpu.VMEM((t_total, LANES), F32), pltpu.VMEM((t_total, LANES), F32)],
        compiler_params=_params("arbitrary", "arbitrary"),
    )(q_src, dy, kv_src, kv_src, _tri(upper=True), _tri(upper=False), *rider.operands)
    return res[:3], res[3:]


def _adamw(name, w, g, m, v):
    def fn(wv, gv, mv, vv):
        mn = ADAM_B1 * mv + (1.0 - ADAM_B1) * gv
        vn = ADAM_B2 * vv + (1.0 - ADAM_B2) * (gv * gv)
        m_hat = mn / (1.0 - ADAM_B1 ** ADAM_STEP)
        v_hat = vn / (1.0 - ADAM_B2 ** ADAM_STEP)
        return -ADAM_LR * (m_hat / (jnp.sqrt(v_hat) + ADAM_EPS) + ADAM_WD * wv), mn, vn

    rows = w.shape[0]
    tr = _row_tile(rows)
    shp = jax.ShapeDtypeStruct(w.shape, F32)
    if rows == 1:
        def body(w_ref, g_ref, m_ref, v_ref, d_ref, mo_ref, vo_ref):
            d, mn, vn = fn(w_ref[...], g_ref[...], m_ref[...], v_ref[...])
            d_ref[...], mo_ref[...], vo_ref[...] = d, mn, vn

        return pl.pallas_call(body, name=name, out_shape=[shp, shp, shp])(w, g, m, v)
    return _rows(name, fn, [w, g, m, v], [shp, shp, shp], tr=tr)


def _place():
    return lax.axis_index("x"), lax.axis_index("y"), lax.axis_index("c")


def _other_chips(x, y):
    return [(1 - x, y), (x, 1 - y), (1 - x, 1 - y)]


ANY = pl.BlockSpec(memory_space=pl.ANY)


def _remote(src, dst, send_sem, recv_sem, to):
    return pltpu.make_async_remote_copy(src_ref=src, dst_ref=dst, send_sem=send_sem, recv_sem=recv_sem,
                                        device_id=to, device_id_type=MESH)


class _WeightGather(_NoRider):
    def __init__(self, shards):
        n_w = len(shards)
        self.operands = list(shards)
        self.out_shapes = [jax.ShapeDtypeStruct((N_CHIPS,) + s.shape, s.dtype) for s in shards]
        self.scratch = [pltpu.SemaphoreType.DMA((3, n_w))] * 4 + [pltpu.SemaphoreType.DMA((n_w,))] * 2

    def _copies(self, ins, outs, sems):
        send_sems, recv_sems, relay_send, relay_recv, own_send, own_recv = sems
        x, y, c = _place()
        my_chip, sibling = 2 * x + y, (x, y, 1 - c)
        n_w = len(ins)

        def half(w, chip, core):
            h = self.operands[w].shape[0] // 2
            return outs[w].at[chip, pl.ds(core * h, h)]

        own = [_remote(ins[w], outs[w].at[my_chip], own_send.at[w], own_recv.at[w], sibling) for w in range(n_w)]
        sends, landed, relays, relayed = [], [], [], []
        for p, (ox, oy) in enumerate(_other_chips(x, y)):
            for w in range(n_w):
                h = self.operands[w].shape[0] // 2
                sends.append(_remote(ins[w].at[pl.ds(c * h, h)], half(w, my_chip, c), send_sems.at[p, w],
                                     recv_sems.at[p, w], (ox, oy, c)))
                here = half(w, 2 * ox + oy, c)
                landed.append(_remote(here, here, send_sems.at[p, w], recv_sems.at[p, w], (ox, oy, c)))
                relays.append(_remote(here, here, relay_send.at[p, w], relay_recv.at[p, w], sibling))
                there = half(w, 2 * ox + oy, 1 - c)
                relayed.append(_remote(there, there, relay_send.at[p, w], relay_recv.at[p, w], sibling))
        return own, sends, landed, relays, relayed

    def start(self, ins, outs, sems):
        own, sends, _, _, _ = self._copies(ins, outs, sems)
        for cp in own + sends:
            cp.start()

    def relay(self, ins, outs, sems):
        _, _, landed, relays, _ = self._copies(ins, outs, sems)
        for arrival, cp in zip(landed, relays):
            arrival.wait_recv()
            cp.start()

    def finish(self, ins, outs, sems):
        own, sends, _, relays, relayed = self._copies(ins, outs, sems)
        for arrival in relayed:
            arrival.wait_recv()
        for cp in sends + relays:
            cp.wait_send()
        for cp in own:
            cp.wait()


class _ChipExchange(_NoRider):
    def __init__(self, pair_sums):
        n_w = len(pair_sums)
        self.operands = list(pair_sums)
        self.out_shapes = [jax.ShapeDtypeStruct((3,) + s.shape[1:], s.dtype) for s in pair_sums]
        self.scratch = [pltpu.SemaphoreType.DMA((3, n_w))] * 2

    def _copies(self, ins, outs, sems):
        send_sems, recv_sems = sems
        x, y, c = _place()
        return [_remote(ins[w].at[2 * ox + oy], outs[w].at[p], send_sems.at[p, w], recv_sems.at[p, w], (ox, oy, c))
                for p, (ox, oy) in enumerate(_other_chips(x, y)) for w in range(len(ins))]

    def start(self, ins, outs, sems):
        for cp in self._copies(ins, outs, sems):
            cp.start()

    def finish(self, ins, outs, sems):
        for cp in self._copies(ins, outs, sems):
            cp.wait()


def _pair_exchange(name, grads):
    n_w = len(grads)

    def halves(w):
        return grads[w].shape[-2] // 2

    def body(*refs):
        ins, theirs = refs[:n_w], refs[n_w:2 * n_w]
        send_sems, recv_sems = refs[2 * n_w:]
        x, y, c = _place()
        sends = []
        for w in range(n_w):
            rows = pl.ds((1 - c) * halves(w), halves(w))
            src = ins[w].at[:, rows, :] if grads[w].ndim == 3 else ins[w].at[rows, :]
            sends.append(_remote(src, theirs[w], send_sems.at[w], recv_sems.at[w], (x, y, 1 - c)))
        for cp in sends:
            cp.start()
        for cp in sends:
            cp.wait()

    return pl.pallas_call(
        body, name=name, in_specs=[ANY] * n_w, out_specs=[ANY] * n_w,
        out_shape=[jax.ShapeDtypeStruct(g.shape[:-2] + (halves(w), g.shape[-1]), F32) for w, g in enumerate(grads)],
        scratch_shapes=[pltpu.SemaphoreType.DMA((n_w,)), pltpu.SemaphoreType.DMA((n_w,))],
    )(*grads)


def _pair_share(shards):
    n_w = len(shards)

    def body(*refs):
        ins, outs = refs[:n_w], refs[n_w:2 * n_w]
        send_sems, recv_sems = refs[2 * n_w:]
        x, y, c = _place()
        sends = []
        for w in range(n_w):
            h = shards[w].shape[0] // 2
            mine = outs[w].at[pl.ds(c * h, h)]
            sends.append(pltpu.make_async_remote_copy(
                src_ref=mine, dst_ref=mine, send_sem=send_sems.at[w], recv_sem=recv_sems.at[w],
                device_id=(x, y, 1 - c), device_id_type=MESH))
        for cp in sends:
            cp.start()
        for w in range(n_w):
            h = shards[w].shape[0] // 2
            theirs = outs[w].at[pl.ds((1 - c) * h, h)]
            pltpu.make_async_remote_copy(
                src_ref=theirs, dst_ref=theirs, send_sem=send_sems.at[w], recv_sem=recv_sems.at[w],
                device_id=(x, y, 1 - c), device_id_type=MESH).wait_recv()
        for cp in sends:
            cp.wait_send()

    return pl.pallas_call(
        body, name="pair_share", in_specs=[ANY] * n_w, out_specs=[ANY] * n_w,
        out_shape=[jax.ShapeDtypeStruct(s.shape, s.dtype) for s in shards],
        input_output_aliases={w: w for w in range(n_w)},
        scratch_shapes=[pltpu.SemaphoreType.DMA((n_w,)), pltpu.SemaphoreType.DMA((n_w,))],
    )(*shards)


def _all_reduce_small(vec):
    rows = vec.shape[0]

    def body(v_ref, o_ref, slots, send_sems, recv_sems):
        x, y, c = _place()
        me = 4 * x + 2 * y + c
        slots[me] = v_ref[...]
        sends = []
        for k in range(1, N_DEV):
            peer = (x ^ (k >> 2), y ^ ((k >> 1) & 1), c ^ (k & 1))
            sends.append(pltpu.make_async_remote_copy(
                src_ref=v_ref, dst_ref=slots.at[me], send_sem=send_sems.at[k - 1], recv_sem=recv_sems.at[k - 1],
                device_id=peer, device_id_type=MESH))
        for cp in sends:
            cp.start()
        for k in range(1, N_DEV):
            px, py, pc = x ^ (k >> 2), y ^ ((k >> 1) & 1), c ^ (k & 1)
            landed = slots.at[4 * px + 2 * py + pc]
            pltpu.make_async_remote_copy(
                src_ref=landed, dst_ref=landed, send_sem=send_sems.at[k - 1], recv_sem=recv_sems.at[k - 1],
                device_id=(px, py, pc), device_id_type=MESH).wait_recv()
        for cp in sends:
            cp.wait_send()
        total = slots[0]
        for d in range(1, N_DEV):
            total = total + slots[d]
        o_ref[...] = total

    vm = pl.BlockSpec(memory_space=pltpu.VMEM)
    return pl.pallas_call(
        body, name="all_reduce_small", in_specs=[vm], out_specs=vm, out_shape=jax.ShapeDtypeStruct(vec.shape, F32),
        scratch_shapes=[pltpu.VMEM((N_DEV, rows, LANES), F32), pltpu.SemaphoreType.DMA((N_DEV - 1,)),
                        pltpu.SemaphoreType.DMA((N_DEV - 1,))],
    )(vec)


def _row_tile(rows):
    fits = [tr for tr in range(16, min(rows, 512) + 1, 16) if rows % tr == 0]
    return max(fits) if fits else rows


def _pair_sum(name, place, grad, theirs):
    if grad.ndim == 2:
        return _pair_sum_joined(name, place, grad, theirs)
    n, r, c = grad.shape
    half = r // 2
    tr = _row_tile(half)
    nb = half // tr

    def body(place_ref, g_ref, t_ref, o_ref):
        o_ref[...] = (g_ref[...] + t_ref[...]).astype(BF16)

    return pl.pallas_call(
        body, name=name, out_shape=jax.ShapeDtypeStruct((n, half, c), BF16),
        grid_spec=pltpu.PrefetchScalarGridSpec(
            num_scalar_prefetch=1, grid=(n, nb),
            in_specs=[pl.BlockSpec((1, tr, c), lambda j, i, pr: (j, pr[0] * nb + i, 0)),
                      pl.BlockSpec((1, tr, c), lambda j, i, pr: (j, i, 0))],
            out_specs=pl.BlockSpec((1, tr, c), lambda j, i, pr: (j, i, 0))),
        compiler_params=_params("parallel", "parallel"),
    )(place, grad, theirs)


def _pair_sum_joined(name, place, grad, theirs):
    r, wide = grad.shape
    half, c = r // 2, wide // N_CHIPS
    tr = _row_tile(half)
    nb = half // tr

    def body(place_ref, g_ref, t_ref, o_ref):
        for j in range(N_CHIPS):
            cols = slice(j * c, (j + 1) * c)
            o_ref[j] = (g_ref[:, cols] + t_ref[:, cols]).astype(BF16)

    return pl.pallas_call(
        body, name=name, out_shape=jax.ShapeDtypeStruct((N_CHIPS, half, c), BF16),
        grid_spec=pltpu.PrefetchScalarGridSpec(
            num_scalar_prefetch=1, grid=(nb,),
            in_specs=[pl.BlockSpec((tr, wide), lambda i, pr: (pr[0] * nb + i, 0)),
                      pl.BlockSpec((tr, wide), lambda i, pr: (i, 0))],
            out_specs=pl.BlockSpec((N_CHIPS, tr, c), lambda i, pr: (0, i, 0))),
        compiler_params=_params("parallel"),
    )(place, grad, theirs)


def _sum_chips(name, place, pair_sums, landed):
    _, half, c = pair_sums.shape
    tr = _row_tile(half)
    nb = half // tr

    def body(place_ref, s_ref, q_ref, o_ref):
        total = s_ref[0].astype(F32)
        for p in range(3):
            total = total + q_ref[p].astype(F32)
        o_ref[...] = total

    return pl.pallas_call(
        body, name=name, out_shape=jax.ShapeDtypeStruct((2 * half, c), F32),
        grid_spec=pltpu.PrefetchScalarGridSpec(
            num_scalar_prefetch=1, grid=(nb,),
            in_specs=[pl.BlockSpec((1, tr, c), lambda i, pr: (pr[1], i, 0)),
                      pl.BlockSpec((3, tr, c), lambda i, pr: (0, i, 0))],
            out_specs=pl.BlockSpec((tr, c), lambda i, pr: (pr[0] * nb + i, 0))),
        compiler_params=_params("parallel"),
    )(place, pair_sums, landed)


BIG = ("w_in", "w_branch_a", "w_branch_b", "w_out", "w_ffn_gate", "w_ffn_up", "w_ffn_down", "w_ple_gate", "w_ple_proj")
HELD_TRANSPOSED = ("w_ffn_gate", "w_ffn_up")
LATE = BIG[1:]
COLUMN_SHARDED = ("w_in", "w_branch_a", "w_branch_b", "w_ffn_gate", "w_ffn_up", "w_ple_proj")
SMALL = ("norm_mix", "w_pool", "pool_scale", "norm_ffn", "norm_ple", "norm_final")


def _join_columns(w4):
    return jnp.concatenate([w4[j] for j in range(N_CHIPS)], axis=1)


def _sds(shape, dtype):
    return jax.ShapeDtypeStruct(shape, dtype)


def _local_step(x, p, target, wf, small, gather_first=None, gather_late=None, exchange_early=None,
                exchange_last=None):
    t, d = x.shape
    w_pool_b = small["w_pool"].astype(BF16)
    dp = w_pool_b.shape[0] * w_pool_b.shape[1]

    h1, first = _norm_fwd("norm_mix", x, small["norm_mix"], rider=gather_first)
    w_in = first[0] if gather_first else wf["w_in"]
    u, q, kv, ga, gb = _mm(
        "proj", [h1], [w_in[j] for j in range(N_CHIPS)], "nn",
        [_sds((t, dp), F32), _sds((t, dp), BF16), _sds((t, d), BF16), _sds((t, d), BF16), _sds((t, d), BF16)],
        separate=True, epilogue=lambda uq, kv_, ga_, gb_: (uq[:, :dp], uq[:, dp:], kv_, ga_, gb_), tm=512)
    pooled, ya = _pool_fwd(u, w_pool_b, small["pool_scale"])
    n_pairs = dp // LANES
    yb, late = _attn_fwd(q, 0, kv, 0, n_pairs, n_pairs, rider=gather_late or _NoRider())
    wf = {**wf, **dict(zip(LATE, late))}
    w_down = wf["w_ffn_down"].reshape(-1, d)
    dff = w_down.shape[0]
    w_gate_t, w_up_t = wf["w_ffn_gate"].reshape(dff, d), wf["w_ffn_up"].reshape(dff, d)
    w_a, w_b, w_pp = _join_columns(wf["w_branch_a"]), _join_columns(wf["w_branch_b"]), _join_columns(wf["w_ple_proj"])
    w_out = wf["w_out"].reshape(d, d)
    w_pg = wf["w_ple_gate"].reshape(d, d)
    def residual_norm(branch, xv, g, w):
        xn = xv + jnp.dot(branch.astype(BF16), w, preferred_element_type=F32)
        return xn, xn * lax.rsqrt(jnp.mean(xn * xn, axis=-1, keepdims=True) + RMS_EPS) * g

    def mixer_tail(tav, tbv, gav, gbv, xv, g, w):
        merged = _sigmoid(gav) * tav + _sigmoid(gbv) * tbv
        return (tav, tbv, merged) + residual_norm(merged, xv, g, w)

    def ffn_tail(gv, uv, xv, g, w):
        act = gv * _sigmoid(gv) * uv
        return (gv, uv, act) + residual_norm(act, xv, g, w)

    stream = [_sds((t, d), F32), _sds((t, d), BF16)]
    ta, tb, merged, x1, h2 = _mm(
        "mixer_out", [ya, yb], [w_a, w_b], "nn", [_sds((t, d), BF16)] * 3 + stream,
        extras=[ga, gb, x, small["norm_ffn"]], wholes=[w_out], separate=True, epilogue=mixer_tail, tm=512)
    gate, up, act, x2, h3 = _mm(
        "ffn", [h2], [w_gate_t, w_up_t], "nt", [_sds((t, dff), BF16)] * 3 + stream,
        extras=[x1, small["norm_ple"]], wholes=[w_down], separate=True, epilogue=ffn_tail, tm=256)
    dx2, dx2_b, d_pp, d_gp, d_norm_final, loss_row, d_norm_ple = _mm(
        "ple_loss", [h3, p], [w_pg, w_pp], "nn", stream + [_sds((t, d), BF16)] * 2,
        extras=[x2, target, small["norm_final"].reshape(1, d), small["norm_ple"]], wholes=[w_pg], separate=True,
        epilogue=_ple_and_loss, sum_shapes=[_sds((1, d), F32)] * 3, tm=512)

    def through_norm(dh, xv, g, dres):
        dx, d_gain = _rms_norm_bwd(dh, xv, g)
        return dx + dres, dx + dres, d_gain

    gain_sum = [_sds((1, d), F32)]
    g_w_pp, g_w_pg = _mm_tn("g_ple", [p, h3], [d_pp, d_gp])

    def ffn_bwd(d_act, gv, uv, xv, g, dres, wg_t, wu_t):
        s = _sigmoid(gv)
        d_gate, d_up = d_act * uv * (s * (1.0 + gv * (1.0 - s))), d_act * (gv * s)
        dh2 = (jnp.dot(d_gate.astype(BF16), wg_t, preferred_element_type=F32)
               + jnp.dot(d_up.astype(BF16), wu_t, preferred_element_type=F32))
        return (d_gate, d_up) + through_norm(dh2, xv, g, dres)

    d_gate, d_up, dx1, dx1_b, d_norm_ffn = _mm(
        "ffn_bwd", [dx2_b], [w_down], "nt", [_sds((t, dff), BF16)] * 2 + stream,
        extras=[gate, up, x1, small["norm_ffn"], dx2], wholes=[w_gate_t, w_up_t], epilogue=ffn_bwd,
        sum_shapes=gain_sum, tm=256)
    g_w_down, = _mm_tn("g_ffn_down", [act], [dx2_b], tmm=512)
    g_w_gate_t, g_w_up_t = _mm_tn("g_ffn_gate_up", [d_gate, d_up], [h2], k_blocks=2)

    def merge_bwd(acc, tav, tbv, gav, gbv):
        sa, sb = _sigmoid(gav), _sigmoid(gbv)
        return acc * sa, acc * sb, acc * tav * sa * (1.0 - sa), acc * tbv * sb * (1.0 - sb)

    d_ta, d_tb, d_ga, d_gb = _mm("d_merged", [dx1_b], [w_out], "nt", [_sds((t, d), BF16)] * 4,
                                 extras=[ta, tb, ga, gb], epilogue=merge_bwd, tm=512)
    g_w_out, g_w_a, g_w_b = _mm_tn("g_mixer", [merged, ya, yb], [dx1_b, d_ta, d_tb])
    d_ya, d_yb = _mm("d_branches", [d_ta, d_tb], [w_a, w_b], "nt", [_sds((t, dp), F32), _sds((t, dp), BF16)],
                     separate=True)
    d_u, g_w_pool, d_pool_scale = _pool_bwd(d_ya, pooled, w_pool_b, small["pool_scale"])
    big = {
        "w_branch_a": g_w_a, "w_branch_b": g_w_b, "w_out": g_w_out.reshape(wf["w_out"].shape),
        "w_ffn_gate": g_w_gate_t.reshape(wf["w_ffn_gate"].shape), "w_ffn_up": g_w_up_t.reshape(wf["w_ffn_up"].shape),
        "w_ffn_down": g_w_down.reshape(wf["w_ffn_down"].shape),
        "w_ple_gate": g_w_pg.reshape(wf["w_ple_gate"].shape), "w_ple_proj": g_w_pp,
    }
    rider = exchange_early(big) if exchange_early else _NoRider()
    (d_q, d_k, d_v), early = _attn_bwd(q, 0, kv, 0, n_pairs, d_yb, n_pairs, rider=rider)
    d_proj = [(d_u, d_q), (d_k, d_v), d_ga, d_gb]
    big["w_in"], = _mm_tn("g_w_in", [h1], d_proj, tmm=512, stacked=True)
    rider = exchange_last(big["w_in"]) if exchange_last else _NoRider()
    res = _mm(
        "d_h1", d_proj, [w_in[j] for j in range(N_CHIPS)], "nt", [_sds((t, d), F32)],
        extras=[x, small["norm_mix"], dx1], epilogue=lambda dh, xv, g, dres: through_norm(dh, xv, g, dres)[1:],
        sum_shapes=gain_sum, tm=512, rider=rider)
    (grad_x, d_norm_mix), last = res if rider.operands else (res, ())
    small_g = {"norm_mix": d_norm_mix, "w_pool": g_w_pool, "pool_scale": d_pool_scale, "norm_ffn": d_norm_ffn,
               "norm_ple": d_norm_ple, "norm_final": d_norm_final}
    return grad_x, big, small_g, loss_row, early, last


def _split2(res, n):
    return res[:n], res[n:]


def _pack_small(small_g, loss_row):
    parts, layout = [], []
    for name in SMALL + ("loss",):
        v = (loss_row if name == "loss" else small_g[name]).reshape(-1, LANES)
        pad = (-v.shape[0]) % 8
        if pad:
            v = jnp.concatenate([v, jnp.zeros((pad, LANES), F32)], axis=0)
        layout.append((name, sum(q.shape[0] for q in parts), v.shape[0]))
        parts.append(v)
    return jnp.concatenate(parts, axis=0), layout


def kernel(x, p, norm_mix, w_in, w_pool, pool_scale, w_branch_a, w_branch_b, w_out, norm_ffn, w_ffn_gate, w_ffn_up, w_ffn_down, norm_ple, w_ple_gate, w_ple_proj, norm_final, loss_target, m_norm_mix, m_w_in, m_w_pool, m_pool_scale, m_w_branch_a, m_w_branch_b, m_w_out, m_norm_ffn, m_w_ffn_gate, m_w_ffn_up, m_w_ffn_down, m_norm_ple, m_w_ple_gate, m_w_ple_proj, m_norm_final, v_norm_mix, v_w_in, v_w_pool, v_pool_scale, v_w_branch_a, v_w_branch_b, v_w_out, v_norm_ffn, v_w_ffn_gate, v_w_ffn_up, v_w_ffn_down, v_norm_ple, v_w_ple_gate, v_w_ple_proj, v_norm_final):
    given = dict(locals())
    names = BIG + SMALL
    order = ("norm_mix", "w_in", "w_pool", "pool_scale", "w_branch_a", "w_branch_b", "w_out", "norm_ffn", "w_ffn_gate",
             "w_ffn_up", "w_ffn_down", "norm_ple", "w_ple_gate", "w_ple_proj", "norm_final")
    t, d = x.shape[1], x.shape[2]
    def local(a, n):
        return jnp.swapaxes(a[0], 0, 1) if n in HELD_TRANSPOSED else a[0]

    def back(a, n):
        return (jnp.swapaxes(a, 0, 1) if n in HELD_TRANSPOSED else a)[None]

    shard = {n: local(given[n], n) for n in BIG}
    small = {"norm_mix": norm_mix, "w_pool": w_pool[0], "pool_scale": pool_scale, "norm_ffn": norm_ffn,
             "norm_ple": norm_ple, "norm_final": norm_final}

    as_bf16 = {n: shard[n].astype(BF16) for n in BIG}

    place = jnp.stack([lax.axis_index("c"), 2 * lax.axis_index("x") + lax.axis_index("y")]).astype(jnp.int32)
    pair_sums = {}

    def exchange_early(ready):
        theirs = _pair_exchange("pair_exchange_early", [ready[n] for n in LATE])
        for n, other in zip(LATE, theirs):
            pair_sums[n] = _pair_sum(f"pair_sum_{n}", place, ready[n], other)
        return _ChipExchange([pair_sums[n] for n in LATE])

    def exchange_last(g_w_in):
        theirs, = _pair_exchange("pair_exchange_w_in", [g_w_in])
        pair_sums["w_in"] = _pair_sum("pair_sum_w_in", place, g_w_in, theirs)
        return _ChipExchange([pair_sums["w_in"]])

    grad_x, big_g, small_g, loss_row, early, last = _local_step(
        x.reshape(t, d), p.reshape(t, p.shape[-1]), loss_target.reshape(t, d), {}, small,
        gather_first=_WeightGather([as_bf16["w_in"]]),
        gather_late=_WeightGather([as_bf16[n] for n in LATE]), exchange_early=exchange_early,
        exchange_last=exchange_last)
    landed = dict(zip(LATE + ("w_in",), tuple(early) + tuple(last)))
    halves = [_sum_chips(f"chip_sum_{n}", place, pair_sums[n], landed[n]) for n in BIG]
    grads = dict(zip(BIG, _pair_share(halves)))

    packed, layout = _pack_small(small_g, loss_row)
    reduced = _all_reduce_small(packed)
    for name, start, rows in layout:
        if name == "loss":
            loss = jnp.sum(reduced[start:start + rows])
        else:
            n_el = small[name].size
            grads[name] = reduced[start:start + rows].reshape(-1)[:n_el]

    deltas, new_m, new_v = {}, {}, {}
    for n in order:
        if n in BIG:
            w, m, v = shard[n], local(given["m_" + n], n), local(given["v_" + n], n)
            dl, mn, vn = _adamw(f"adamw_{n}", w, grads[n], m, v)
            grads[n], deltas[n], new_m[n], new_v[n] = [back(a, n) for a in (grads[n], dl, mn, vn)]
        else:
            w, full = small[n], given[n].shape
            shape2 = (1, w.shape[0]) if w.ndim == 1 else (w.shape if w.ndim == 2 else (w.shape[0] * w.shape[1], w.shape[2]))
            dl, mn, vn = _adamw(f"adamw_{n}", w.reshape(shape2), grads[n].reshape(shape2),
                                given["m_" + n].reshape(shape2), given["v_" + n].reshape(shape2))
            grads[n], deltas[n], new_m[n], new_v[n] = [a.reshape(full) for a in (grads[n], dl, mn, vn)]

    return (loss, grad_x.reshape(x.shape), *[grads[n] for n in order], *[deltas[n] for n in order],
            *[new_m[n] for n in order], *[new_v[n] for n in order])
```

```python
import functools
import math

import jax
import jax.numpy as jnp
from jax import lax
from jax.experimental import pallas as pl
from jax.experimental.pallas import tpu as pltpu

F32 = jnp.float32
BF16 = jnp.bfloat16
MESH = pl.DeviceIdType.MESH

RMS_EPS = 1e-6
POOL_WINDOWS = (2, 4, 8, 16)
POOL_HALO = 16
HEAD_DIM = 64
LANES = 128
ATT_BLOCK = 256
ATT_CHAINS = 2
ATT_CHUNK = 256
ATT_SLAB = 256
ATT_SCALE = 1.0 / math.sqrt(HEAD_DIM)
LOG2_E = 1.4426950408889634
ATT_EXIT_BELOW = -150.5
ADAM_LR, ADAM_B1, ADAM_B2, ADAM_EPS, ADAM_WD, ADAM_STEP = 0.001, 0.9, 0.999, 1e-08, 0.01, 10
V7X_VMEM_LIMIT_BYTES = 56 * 1024 * 1024
N_CHIPS = 4
N_DEV = 8


def _params(*semantics):
    return pltpu.CompilerParams(dimension_semantics=semantics, vmem_limit_bytes=V7X_VMEM_LIMIT_BYTES)


def _sigmoid(z):
    return 1.0 / (1.0 + jnp.exp(-z))


def _tiled_spec(shape, tm, tn, n_total, at):
    rows, width = shape
    if rows == 1:
        if width == n_total:
            return pl.BlockSpec((1, tn), at(lambda i, j: (0, j)))
        return pl.BlockSpec((1, width), at(lambda i, j: (0, 0)))
    if width == n_total:
        return pl.BlockSpec((tm, tn), at(lambda i, j: (i, j)))
    assert tn == n_total, "an operand narrower than the output needs whole output rows per tile"
    return pl.BlockSpec((tm, width), at(lambda i, j: (i, 0)))


def _column_pieces(operands):
    pieces = [tuple(a) if isinstance(a, (tuple, list)) else (a,) for a in operands]
    return [p for ps in pieces for p in ps], [len(ps) for ps in pieces]


def _load_bf16(refs, counts):
    tiles, k = [], 0
    for n in counts:
        parts = [r[...] for r in refs[k:k + n]]
        parts = [t if t.dtype == BF16 else t.astype(BF16) for t in parts]
        tiles.append(parts[0] if n == 1 else jnp.concatenate(parts, axis=1))
        k += n
    return tiles


def _mm(name, a_list, b_list, mode, out_shapes, epilogue=None, extras=(), tm=1024, tn=None, separate=False,
        sum_shapes=(), rider=None, wholes=()):
    flat_a, counts = _column_pieces(a_list)
    m_total = flat_a[0].shape[0]
    n_total = b_list[0].shape[1] if mode == "nn" else b_list[0].shape[0]
    tn = n_total if tn is None else tn
    tm = min(tm, m_total)
    assert m_total % tm == 0 and n_total % tn == 0 and (not sum_shapes or tn == n_total)
    n_a, n_b, n_extra, n_out = len(counts), len(b_list), len(extras), len(out_shapes)
    assert n_a in (1, n_b)
    dims = (((1,), (0,)), ((), ())) if mode == "nn" else (((1,), (1,)), ((), ()))
    with_rider = rider is not None
    rider = rider or _NoRider()
    grid = (n_total // tn, m_total // tm)

    def at(index):
        return lambda j, i: index(i, j)

    def body(*refs):
        ins, o_refs, _, riding = rider.split(refs, len(flat_a) + n_b + n_extra + len(wholes), n_out + len(sum_shapes))
        a_refs, b_refs = ins[:len(flat_a)], ins[len(flat_a):len(flat_a) + n_b]
        e_refs, w_refs = ins[len(flat_a) + n_b:len(flat_a) + n_b + n_extra], ins[len(flat_a) + n_b + n_extra:]
        at_first = (pl.program_id(0) == 0) & (pl.program_id(1) == 0)
        at_last = (pl.program_id(0) == grid[0] - 1) & (pl.program_id(1) == grid[1] - 1)
        top, bottom = rider.at_steps(riding, at_first, at_first, at_last)
        top()
        lefts = _load_bf16(a_refs, counts)
        products = [lax.dot_general(lefts[s % n_a], b_refs[s][...], dims, preferred_element_type=F32)
                    for s in range(n_b)]
        if not separate:
            products = [functools.reduce(lambda p, r: p + r, products)]
        extra_tiles = [e[...].astype(F32) for e in e_refs]
        outs = products if epilogue is None else epilogue(*products, *extra_tiles, *[w[...] for w in w_refs])
        for o_ref, o in zip(o_refs[:n_out], outs[:n_out]):
            o_ref[...] = o.astype(o_ref.dtype)
        if sum_shapes:
            @pl.when(pl.program_id(1) == 0)
            def _():
                for s_ref in o_refs[n_out:]:
                    s_ref[...] = jnp.zeros_like(s_ref)

            for s_ref, s in zip(o_refs[n_out:], outs[n_out:]):
                s_ref[...] += s
        bottom()

    once = dict(pipeline_mode=pl.Buffered(1)) if tn == n_total else {}
    in_specs = [pl.BlockSpec((tm, a.shape[1]), at(lambda i, j: (i, 0))) for a in flat_a]
    if mode == "nn":
        in_specs += [pl.BlockSpec((b.shape[0], tn), at(lambda i, j: (0, j)), **once) for b in b_list]
    else:
        in_specs += [pl.BlockSpec((tn, b.shape[1]), at(lambda i, j: (j, 0)), **once) for b in b_list]
    in_specs += [_tiled_spec(e.shape, tm, tn, n_total, at) for e in extras]
    in_specs += [pl.BlockSpec(w.shape, lambda j, i: (0, 0), pipeline_mode=pl.Buffered(1)) for w in wholes]
    out_specs = [_tiled_spec(o.shape, tm, tn, n_total, at) for o in out_shapes]
    out_specs += [pl.BlockSpec(s.shape, at(lambda i, j: (0, 0))) for s in sum_shapes]
    semantics = ("arbitrary", "arbitrary") if sum_shapes or rider.operands else ("parallel", "parallel")
    res = pl.pallas_call(
        body, name=name, grid=grid, in_specs=in_specs + [ANY] * len(rider.operands),
        out_specs=out_specs + [ANY] * len(rider.out_shapes),
        out_shape=list(out_shapes) + list(sum_shapes) + list(rider.out_shapes), scratch_shapes=list(rider.scratch),
        compiler_params=_params(*semantics),
    )(*flat_a, *b_list, *extras, *wholes, *rider.operands)
    n_own = len(out_shapes) + len(sum_shapes)
    return (res[:n_own], res[n_own:]) if with_rider else res


def _mm_tn(name, a_list, b_list, tmm=1024, stacked=False, k_blocks=1):
    flat_b, counts = _column_pieces(b_list)
    n_a, n_b = len(a_list), len(counts)
    n_prod = max(n_a, n_b)
    m_total = a_list[0].shape[0]
    ks = [a_list[s % n_a].shape[1] for s in range(n_prod)]
    widths = [sum(p.shape[1] for p in flat_b[sum(counts[:s]):sum(counts[:s + 1])]) for s in range(n_b)]
    widths = [widths[s % n_b] for s in range(n_prod)]
    tmm = min(tmm, m_total)
    assert m_total % tmm == 0 and all(k % k_blocks == 0 for k in ks)
    assert n_a in (1, n_prod) and n_b in (1, n_prod) and not (stacked and n_a > 1)

    def body(*refs):
        a_refs, b_refs, o_refs = refs[:n_a], refs[n_a:n_a + len(flat_b)], refs[n_a + len(flat_b):]

        @pl.when(pl.program_id(1) == 0)
        def _():
            for o_ref in o_refs:
                o_ref[...] = jnp.zeros_like(o_ref)

        lefts, rights = _load_bf16(a_refs, [1] * n_a), _load_bf16(b_refs, counts)
        for s in range(n_prod):
            product = lax.dot_general(lefts[s % n_a], rights[s % n_b], (((0,), (0,)), ((), ())),
                                      preferred_element_type=F32)
            if stacked:
                o_refs[0][s] += product
            else:
                o_refs[s][...] += product

    in_specs = [pl.BlockSpec((tmm, a.shape[1] // k_blocks), lambda kb, m: (m, kb)) for a in a_list]
    in_specs += [pl.BlockSpec((tmm, b.shape[1]), lambda kb, m: (m, 0)) for b in flat_b]
    if stacked:
        out_shape = [jax.ShapeDtypeStruct((n_prod, ks[0], widths[0]), F32)]
        out_specs = [pl.BlockSpec((n_prod, ks[0] // k_blocks, widths[0]), lambda kb, m: (0, kb, 0))]
    else:
        out_shape = [jax.ShapeDtypeStruct((k, w), F32) for k, w in zip(ks, widths)]
        out_specs = [pl.BlockSpec((k // k_blocks, w), lambda kb, m: (kb, 0)) for k, w in zip(ks, widths)]
    return pl.pallas_call(
        body, name=name, grid=(k_blocks, m_total // tmm), in_specs=in_specs, out_specs=out_specs, out_shape=out_shape,
        compiler_params=_params("arbitrary", "arbitrary"),
    )(*a_list, *flat_b)


def _rows(name, fn, ins, tile_outs, sum_outs=(), tr=512, rider=None):
    t_total = max(a.shape[0] for a in ins)
    tr = min(tr, t_total)
    assert t_total % tr == 0
    n_in, n_tile = len(ins), len(tile_outs)
    rider = rider or _NoRider()
    n_steps = t_total // tr

    def body(*refs):
        own_ins, own_outs, _, riding = rider.split(refs, n_in, n_tile + len(sum_outs))
        step = pl.program_id(0)
        top, bottom = rider.at_steps(riding, step == 0, step == n_steps - 1, step == n_steps - 1)
        top()
        refs = tuple(own_ins) + tuple(own_outs)
        outs = fn(*[r[...].astype(F32) for r in refs[:n_in]])
        for o_ref, o in zip(refs[n_in:n_in + n_tile], outs[:n_tile]):
            o_ref[...] = o.astype(o_ref.dtype)
        if sum_outs:
            @pl.when(pl.program_id(0) == 0)
            def _():
                for s_ref in refs[n_in + n_tile:]:
                    s_ref[...] = jnp.zeros_like(s_ref)

            for s_ref, s in zip(refs[n_in + n_tile:], outs[n_tile:]):
                s_ref[...] += s
        bottom()

    def spec(shape):
        if shape[0] == 1:
            return pl.BlockSpec(shape, lambda i: (0, 0))
        return pl.BlockSpec((tr, shape[1]), lambda i: (i, 0))

    return pl.pallas_call(
        body, name=name, grid=(n_steps,), in_specs=[spec(a.shape) for a in ins] + [ANY] * len(rider.operands),
        out_specs=[spec(o.shape) for o in tile_outs] + [spec(s.shape) for s in sum_outs] + [ANY] * len(rider.out_shapes),
        out_shape=list(tile_outs) + list(sum_outs) + list(rider.out_shapes), scratch_shapes=list(rider.scratch),
        compiler_params=_params("arbitrary" if sum_outs or rider.operands else "parallel"),
    )(*ins, *rider.operands)


def _norm_fwd(name, x, gain, rider=None):
    def fn(xv, g):
        inv = lax.rsqrt(jnp.mean(xv * xv, axis=-1, keepdims=True) + RMS_EPS)
        return (xv * inv * g,)

    res = _rows(name, fn, [x, gain], [jax.ShapeDtypeStruct(x.shape, BF16)], rider=rider)
    return res[0], res[1:]


def _rms_norm_bwd(dh, xv, g):
    inv = lax.rsqrt(jnp.mean(xv * xv, axis=-1, keepdims=True) + RMS_EPS)
    xn = xv * inv
    dxn = dh * g
    return inv * (dxn - xn * jnp.mean(dxn * xn, axis=-1, keepdims=True)), jnp.sum(dh * xn, axis=0, keepdims=True)


def _ple_and_loss(gv, pv, x2v, tv, g_final, g_ple, w_pg):
    d = x2v.shape[1]
    s = _sigmoid(gv)
    xv = x2v + s * pv
    inv = lax.rsqrt(jnp.mean(xv * xv, axis=-1, keepdims=True) + RMS_EPS)
    err = xv * inv * g_final - tv
    dx3, d_final = _rms_norm_bwd(err * (1.0 / d), xv, g_final)
    d_pp, d_gp = dx3 * s, dx3 * pv * s * (1.0 - s)
    dh3 = lax.dot_general(d_gp.astype(BF16), w_pg, (((1,), (1,)), ((), ())), preferred_element_type=F32)
    dx2, d_ple = _rms_norm_bwd(dh3, x2v, g_ple)
    dx2 = dx2 + dx3
    return dx2, dx2, d_pp, d_gp, d_final, (0.5 / d) * jnp.sum(err * err, axis=0, keepdims=True), d_ple


def _window_counts(t_pos, w):
    return jnp.minimum(t_pos + 1, w).astype(F32)


def _pool_fwd(u, w_pool, scale, tr=512):
    t_total, width = u.shape
    tr = min(tr, t_total)
    n_groups = len(POOL_WINDOWS)
    gdim = width // n_groups
    ext = tr + POOL_HALO

    def body(u_ref, halo_ref, w_ref, s_ref, pooled_ref, ya_ref):
        i = pl.program_id(0)
        halo = jnp.where(i == 0, 0.0, halo_ref[...])
        t_pos = i * tr + lax.broadcasted_iota(jnp.int32, (tr, 1), 0)
        for g, w in enumerate(POOL_WINDOWS):
            cols = slice(g * gdim, (g + 1) * gdim)
            main = u_ref[:, cols]
            win = jnp.concatenate([halo[:, cols], main], axis=0)
            span = 1
            while span < w:
                win = win + pltpu.roll(win, span, 0)
                span *= 2
            pooled = win[POOL_HALO:, :] * (1.0 / _window_counts(t_pos, w)) - main
            pooled_b = pooled.astype(BF16)
            pooled_ref[:, cols] = pooled_b
            mixed = jnp.dot(pooled_b, w_ref[g], preferred_element_type=F32)
            ya_ref[:, cols] = (mixed * s_ref[:, cols]).astype(BF16)

    hb = tr // POOL_HALO
    return pl.pallas_call(
        body, name="pool_fwd", grid=(t_total // tr,),
        in_specs=[pl.BlockSpec((tr, width), lambda i: (i, 0)),
                  pl.BlockSpec((POOL_HALO, width), lambda i: (jnp.maximum(i * hb - 1, 0), 0)),
                  pl.BlockSpec((n_groups, gdim, gdim), lambda i: (0, 0, 0)),
                  pl.BlockSpec((1, width), lambda i: (0, 0))],
        out_specs=[pl.BlockSpec((tr, width), lambda i: (i, 0)), pl.BlockSpec((tr, width), lambda i: (i, 0))],
        out_shape=[jax.ShapeDtypeStruct(u.shape, BF16), jax.ShapeDtypeStruct(u.shape, BF16)],
        compiler_params=_params("parallel"),
    )(u, u, w_pool, scale)


def _pool_bwd(dya, pooled, w_pool, scale, tr=512):
    t_total, width = dya.shape
    tr = min(tr, t_total)
    n_groups = len(POOL_WINDOWS)
    gdim = width // n_groups
    ext = tr + POOL_HALO
    n_tiles = t_total // tr

    def body(d_ref, halo_ref, p_ref, w_ref, s_ref, du_ref, dw_ref, ds_ref):
        i = pl.program_id(0)

        @pl.when(i == 0)
        def _():
            dw_ref[...] = jnp.zeros_like(dw_ref)
            ds_ref[...] = jnp.zeros_like(ds_ref)

        halo = jnp.where(i == n_tiles - 1, 0.0, halo_ref[...])
        t_pos = i * tr + lax.broadcasted_iota(jnp.int32, (ext, 1), 0)
        for g, w in enumerate(POOL_WINDOWS):
            cols = slice(g * gdim, (g + 1) * gdim)
            sc = s_ref[:, cols]
            d_main = d_ref[:, cols]
            pooled_b = p_ref[:, cols]
            mixed = jnp.dot(pooled_b, w_ref[g], preferred_element_type=F32)
            ds_ref[:, cols] += jnp.sum(d_main * mixed, axis=0, keepdims=True)
            dmix = (jnp.concatenate([d_main, halo[:, cols]], axis=0) * sc).astype(BF16)
            dw_ref[g] += lax.dot_general(pooled_b, dmix[:tr, :], (((0,), (0,)), ((), ())),
                                         preferred_element_type=F32)
            dpool = lax.dot_general(dmix, w_ref[g], (((1,), (1,)), ((), ())), preferred_element_type=F32)
            win = dpool * (1.0 / _window_counts(t_pos, w))
            span = 1
            while span < w:
                win = win + pltpu.roll(win, ext - span, 0)
                span *= 2
            du_ref[:, cols] = (win[:tr, :] - dpool[:tr, :]).astype(BF16)

    hb = tr // POOL_HALO
    last_halo = t_total // POOL_HALO - 1
    return pl.pallas_call(
        body, name="pool_bwd", grid=(n_tiles,),
        in_specs=[pl.BlockSpec((tr, width), lambda i: (i, 0)),
                  pl.BlockSpec((POOL_HALO, width), lambda i: (jnp.minimum((i + 1) * hb, last_halo), 0)),
                  pl.BlockSpec((tr, width), lambda i: (i, 0)),
                  pl.BlockSpec((n_groups, gdim, gdim), lambda i: (0, 0, 0)),
                  pl.BlockSpec((1, width), lambda i: (0, 0))],
        out_specs=[pl.BlockSpec((tr, width), lambda i: (i, 0)),
                   pl.BlockSpec((n_groups, gdim, gdim), lambda i: (0, 0, 0)),
                   pl.BlockSpec((1, width), lambda i: (0, 0))],
        out_shape=[jax.ShapeDtypeStruct(dya.shape, BF16), jax.ShapeDtypeStruct((n_groups, gdim, gdim), F32),
                   jax.ShapeDtypeStruct((1, width), F32)],
        compiler_params=_params("arbitrary"),
    )(dya, dya, pooled, w_pool, scale)


def _head_masks():
    lane = lax.broadcasted_iota(jnp.int32, (1, LANES), 1)
    return lane < HEAD_DIM


def _stack_heads(tile, first):
    zero = jnp.zeros_like(tile)
    return jnp.concatenate([jnp.where(first, tile, zero), jnp.where(first, zero, tile)], axis=0)


def _causal_mask(t_pos, k_start):
    col = lax.broadcasted_iota(jnp.int32, (1, 2 * ATT_SLAB), 1)
    return k_start + (col & (ATT_SLAB - 1)) < t_pos


def _slab_scores(q, kd, mask):
    z2 = lax.dot_general(q, kd, (((1,), (1,)), ((), ())), preferred_element_type=F32) * LOG2_E
    log_hit = jnp.minimum(z2, 0.0) - jnp.log2(1.0 + jnp.exp2(-jnp.abs(z2)))
    log_fail = log_hit - z2
    return log_hit, (log_fail if mask is None else jnp.where(mask, log_fail, 0.0))


def _weights(log_hit, suffix, mask):
    arg = log_hit + suffix
    return jnp.exp2(arg if mask is None else jnp.where(mask, arg, -1e30))


def _tri(upper):
    r = lax.broadcasted_iota(jnp.int32, (ATT_CHUNK, ATT_CHUNK), 0)
    c = lax.broadcasted_iota(jnp.int32, (ATT_CHUNK, ATT_CHUNK), 1)
    return jnp.where(r > c if upper else r < c, 1.0, 0.0).astype(BF16)


def _tri_spec():
    return pl.BlockSpec((ATT_CHUNK, ATT_CHUNK), lambda h, i: (0, 0), pipeline_mode=pl.Buffered(1))


def _scan_chunk(v, tri):
    return jnp.dot(v.astype(BF16), tri, preferred_element_type=F32)


def _lane_bcast(col):
    return jnp.broadcast_to(col, (col.shape[0], LANES))


def _scan_slab(v, tri, carries, from_right):
    n_chunks = ATT_SLAB // ATT_CHUNK
    edge = 0 if from_right else ATT_CHUNK - 1
    parts, new_carries = [None] * (2 * n_chunks), []
    for head in range(2):
        run = carries[head]
        for c in (reversed(range(n_chunks)) if from_right else range(n_chunks)):
            lo_col = head * ATT_SLAB + c * ATT_CHUNK
            vc = v[:, lo_col:lo_col + ATT_CHUNK]
            sc = _scan_chunk(vc, tri)
            parts[head * n_chunks + c] = sc + jnp.concatenate([run] * (ATT_CHUNK // LANES), axis=1)
            run = run + _lane_bcast(sc[:, edge:edge + 1] + vc[:, edge:edge + 1])
        new_carries.append(run)
    return jnp.concatenate(parts, axis=1), new_carries


def _fold_heads(stacked, first):
    s = stacked.shape[0] // 2
    return jnp.where(first, stacked[:s], stacked[s:])


class _NoRider:
    operands, out_shapes, scratch = (), (), ()

    def split(self, refs, n_base_in, n_base_out):
        n_in, n_out, n_sem = len(self.operands), len(self.out_shapes), len(self.scratch)
        a = n_base_in + n_in
        b = a + n_base_out + n_out
        mine = (refs[n_base_in:a], refs[a + n_base_out:b], refs[b:b + n_sem])
        return refs[:n_base_in], refs[a:a + n_base_out], refs[b + n_sem:], mine

    def start(self, ins, outs, sems):
        pass

    def relay(self, ins, outs, sems):
        pass

    def finish(self, ins, outs, sems):
        pass

    def at_steps(self, refs, first_step, relay_step, last_step):
        if not self.operands:
            return (lambda: None), (lambda: None)

        def top():
            pl.when(first_step)(lambda: self.start(*refs))
            pl.when(relay_step)(lambda: self.relay(*refs))

        return top, lambda: pl.when(last_step)(lambda: self.finish(*refs))


def _attn_fwd(q_src, q_col, kv_src, k_col, v_col, n_pairs=4, rider=_NoRider()):
    t_total = q_src.shape[0]
    blk = ATT_BLOCK
    n_steps = t_total // (ATT_CHAINS * blk)
    assert t_total % ATT_SLAB == 0 and ATT_SLAB == ATT_BLOCK

    def body(*refs):
        (q_ref, k_ref, v_ref, suffix_ref), (o_ref,), _, riding = rider.split(refs, 4, 1)
        h, ii = pl.program_id(0), pl.program_id(1)
        top, bottom = rider.at_steps(riding, (h == 0) & (ii == 0), (h == n_pairs - 1) & (ii == 0),
                                     (h == n_pairs - 1) & (ii == n_steps - 1))
        top()
        first = _head_masks()
        suffix_tri = suffix_ref[...]
        blocks = [ATT_CHAINS * ii + c for c in range(ATT_CHAINS)]
        qs = [q_ref[c * blk:(c + 1) * blk, :] * ATT_SCALE for c in range(ATT_CHAINS)]
        t_pos = [b * blk + lax.broadcasted_iota(jnp.int32, (blk, 1), 0) for b in blocks]

        def one(c, t, chain, on_diagonal):
            _, acc, right_a, right_b = chain
            k_start = pl.multiple_of((blocks[c] - t) * ATT_SLAB, ATT_SLAB)
            kd = _stack_heads(k_ref[pl.ds(k_start, ATT_SLAB), :], first)
            vd = _stack_heads(v_ref[pl.ds(k_start, ATT_SLAB), :], first)
            mask = _causal_mask(t_pos[c], k_start) if on_diagonal else None
            log_hit, log_fail = _slab_scores(qs[c], kd, mask)
            suffix, (right_a, right_b) = _scan_slab(log_fail, suffix_tri, (right_a, right_b), from_right=True)
            a = _weights(log_hit, suffix, mask).astype(BF16)
            acc = acc + jnp.dot(a, vd, preferred_element_type=F32)
            return jnp.max(jnp.maximum(right_a, right_b)), acc, right_a, right_b

        def step(state, on_diagonal):
            t, chains = state
            return t + 1, tuple(one(c, t, chains[c], on_diagonal) for c in range(ATT_CHAINS))

        def more(state):
            t, chains = state
            return (t <= blocks[0]) & (functools.reduce(jnp.maximum, [ch[0] for ch in chains]) > ATT_EXIT_BELOW)

        zero = jnp.zeros((blk, LANES), F32)
        state = step((0, ((jnp.float32(0.0), zero, zero, zero),) * ATT_CHAINS), on_diagonal=True)
        t, chains = lax.while_loop(more, functools.partial(step, on_diagonal=False), state)
        for c in range(ATT_CHAINS):
            chain = chains[c]
            if c:
                _, chain = lax.while_loop(
                    lambda s, c=c: (s[0] <= blocks[c]) & (s[1][0] > ATT_EXIT_BELOW),
                    lambda s, c=c: (s[0] + 1, one(c, s[0], s[1], False)), (t, chain))
            o_ref[c * blk:(c + 1) * blk, :] = chain[1].astype(BF16)
        bottom()

    rows = ATT_CHAINS * blk
    res = pl.pallas_call(
        body, name="attn_fwd", grid=(n_pairs, n_steps),
        in_specs=[pl.BlockSpec((rows, LANES), lambda h, i: (i, q_col + h)),
                  pl.BlockSpec((t_total, LANES), lambda h, i: (0, k_col + h)),
                  pl.BlockSpec((t_total, LANES), lambda h, i: (0, v_col + h)), _tri_spec()] + [ANY] * len(rider.operands),
        out_specs=[pl.BlockSpec((rows, LANES), lambda h, i: (i, h))] + [ANY] * len(rider.out_shapes),
        out_shape=[jax.ShapeDtypeStruct((t_total, n_pairs * LANES), BF16)] + list(rider.out_shapes),
        scratch_shapes=list(rider.scratch),
        compiler_params=_params("arbitrary", "arbitrary"),
    )(q_src, kv_src, kv_src, _tri(upper=True), *rider.operands)
    return res[0], res[1:]


def _attn_bwd(q_src, q_col, kv_src, k_col, v_col, dy, n_pairs=4, rider=_NoRider()):
    t_total = q_src.shape[0]
    blk = ATT_BLOCK
    n_steps = t_total // (ATT_CHAINS * blk)
    n_slabs = t_total // ATT_SLAB
    assert t_total % ATT_SLAB == 0 and ATT_SLAB == ATT_BLOCK

    def body(*refs):
        ins, (dq_ref, dk_ref, dv_ref), (g_s, dk_acc, dv_acc), riding = rider.split(refs, 6, 3)
        q_ref, dy_ref, k_ref, v_ref, suffix_ref, prefix_ref = ins
        h, ii = pl.program_id(0), pl.program_id(1)
        top, bottom = rider.at_steps(riding, (h == 0) & (ii == 0), (h == n_pairs - 1) & (ii == 0),
                                     (h == n_pairs - 1) & (ii == n_steps - 1))
        top()

        @pl.when(ii == 0)
        def _():
            dk_acc[...] = jnp.zeros_like(dk_acc)
            dv_acc[...] = jnp.zeros_like(dv_acc)

        first = _head_masks()
        suffix_tri = suffix_ref[...]
        prefix_tri = prefix_ref[...]
        blocks = [ATT_CHAINS * ii + c for c in range(ATT_CHAINS)]
        rows = [slice(c * blk, (c + 1) * blk) for c in range(ATT_CHAINS)]
        qs = [q_ref[r, :] * ATT_SCALE for r in rows]
        dys = [dy_ref[r, :] for r in rows]
        t_pos = [b * blk + lax.broadcasted_iota(jnp.int32, (blk, 1), 0) for b in blocks]

        def one1(c, t, chain, on_diagonal):
            _, right_a, right_b = chain
            slab = blocks[c] - t
            k_start = pl.multiple_of(slab * ATT_SLAB, ATT_SLAB)
            kd = _stack_heads(k_ref[pl.ds(k_start, ATT_SLAB), :], first)
            vd = _stack_heads(v_ref[pl.ds(k_start, ATT_SLAB), :], first)
            mask = _causal_mask(t_pos[c], k_start) if on_diagonal else None
            log_hit, log_fail = _slab_scores(qs[c], kd, mask)
            suffix, (right_a, right_b) = _scan_slab(log_fail, suffix_tri, (right_a, right_b), from_right=True)
            a = _weights(log_hit, suffix, mask)
            da = lax.dot_general(dys[c], vd, (((1,), (1,)), ((), ())), preferred_element_type=F32)
            g_s[c, slab] = (da * a).astype(BF16)
            dv_acc[pl.ds(k_start, ATT_SLAB), :] += _fold_heads(lax.dot_general(
                a.astype(BF16), dys[c], (((0,), (0,)), ((), ())), preferred_element_type=F32), first)
            return jnp.max(jnp.maximum(right_a, right_b)), right_a, right_b

        def step1(state, on_diagonal):
            t, chains = state
            return t + 1, tuple(one1(c, t, chains[c], on_diagonal) for c in range(ATT_CHAINS))

        def more(state):
            t, chains = state
            return (t <= blocks[0]) & (functools.reduce(jnp.maximum, [ch[0] for ch in chains]) > ATT_EXIT_BELOW)

        zero = jnp.zeros((blk, LANES), F32)
        state = step1((0, ((jnp.float32(0.0), zero, zero),) * ATT_CHAINS), on_diagonal=True)
        joint, chains = lax.while_loop(more, functools.partial(step1, on_diagonal=False), state)
        done = [joint]
        for c in range(1, ATT_CHAINS):
            done.append(lax.while_loop(
                lambda s, c=c: (s[0] <= blocks[c]) & (s[1][0] > ATT_EXIT_BELOW),
                lambda s, c=c: (s[0] + 1, one1(c, s[0], s[1], False)), (joint, chains[c]))[0])

        def one2(c, t, carry, on_diagonal):
            dq, left_a, left_b = carry
            slab = blocks[c] - t
            k_start = pl.multiple_of(slab * ATT_SLAB, ATT_SLAB)
            kd = _stack_heads(k_ref[pl.ds(k_start, ATT_SLAB), :], first)
            g = g_s[c, slab]
            z2 = lax.dot_general(qs[c], kd, (((1,), (1,)), ((), ())), preferred_element_type=F32) * LOG2_E
            sig = 1.0 / (1.0 + jnp.exp2(-z2))
            prefix, (left_a, left_b) = _scan_slab(g, prefix_tri, (left_a, left_b), from_right=False)
            dz = g * (1.0 - sig) - sig * prefix
            if on_diagonal:
                dz = jnp.where(_causal_mask(t_pos[c], k_start), dz, 0.0)
            dz = dz.astype(BF16)
            dq = dq + jnp.dot(dz, kd, preferred_element_type=F32)
            dk_acc[pl.ds(k_start, ATT_SLAB), :] += _fold_heads(lax.dot_general(
                dz, qs[c], (((0,), (0,)), ((), ())), preferred_element_type=F32), first)
            return dq, left_a, left_b

        carries = [(zero, zero, zero)]
        for c in range(1, ATT_CHAINS):
            carries.append(lax.fori_loop(
                0, done[c] - joint, lambda n, carry, c=c: one2(c, done[c] - 1 - n, carry, False), (zero, zero, zero)))
        carries = lax.fori_loop(
            0, joint - 1,
            lambda n, cs: tuple(one2(c, joint - 1 - n, cs[c], False) for c in range(ATT_CHAINS)), tuple(carries))
        for c in range(ATT_CHAINS):
            dq_ref[rows[c], :] = (one2(c, 0, carries[c], True)[0] * ATT_SCALE).astype(BF16)

        @pl.when(ii == n_steps - 1)
        def _():
            dk_ref[...] = dk_acc[...].astype(BF16)
            dv_ref[...] = dv_acc[...].astype(BF16)

        bottom()

    out = jax.ShapeDtypeStruct((t_total, n_pairs * LANES), BF16)
    n_rows = ATT_CHAINS * blk
    whole = dict(pipeline_mode=pl.Buffered(1))
    res = pl.pallas_call(
        body, name="attn_bwd", grid=(n_pairs, n_steps),
        in_specs=[pl.BlockSpec((n_rows, LANES), lambda h, i: (i, q_col + h)),
                  pl.BlockSpec((n_rows, LANES), lambda h, i: (i, h)),
                  pl.BlockSpec((t_total, LANES), lambda h, i: (0, k_col + h), **whole),
                  pl.BlockSpec((t_total, LANES), lambda h, i: (0, v_col + h), **whole), _tri_spec(), _tri_spec()]
        + [ANY] * len(rider.operands),
        out_specs=[pl.BlockSpec((n_rows, LANES), lambda h, i: (i, h)),
                   pl.BlockSpec((t_total, LANES), lambda h, i: (0, h)),
                   pl.BlockSpec((t_total, LANES), lambda h, i: (0, h))] + [ANY] * len(rider.out_shapes),
        out_shape=[out, out, out] + list(rider.out_shapes),
        scratch_shapes=list(rider.scratch) + [pltpu.VMEM((ATT_CHAINS, n_slabs, blk, 2 * ATT_SLAB), BF16),
                                              pltpu.VMEM((t_total, LANES), F32), pltpu.VMEM((t_total, LANES), F32)],
        compiler_params=_params("arbitrary", "arbitrary"),
    )(q_src, dy, kv_src, kv_src, _tri(upper=True), _tri(upper=False), *rider.operands)
    return res[:3], res[3:]


def _adamw(name, w, g, m, v):
    def fn(wv, gv, mv, vv):
        mn = ADAM_B1 * mv + (1.0 - ADAM_B1) * gv
        vn = ADAM_B2 * vv + (1.0 - ADAM_B2) * (gv * gv)
        m_hat = mn / (1.0 - ADAM_B1 ** ADAM_STEP)
        v_hat = vn / (1.0 - ADAM_B2 ** ADAM_STEP)
        return -ADAM_LR * (m_hat / (jnp.sqrt(v_hat) + ADAM_EPS) + ADAM_WD * wv), mn, vn

    rows = w.shape[0]
    tr = _row_tile(rows)
    shp = jax.ShapeDtypeStruct(w.shape, F32)
    if rows == 1:
        def body(w_ref, g_ref, m_ref, v_ref, d_ref, mo_ref, vo_ref):
            d, mn, vn = fn(w_ref[...], g_ref[...], m_ref[...], v_ref[...])
            d_ref[...], mo_ref[...], vo_ref[...] = d, mn, vn

        return pl.pallas_call(body, name=name, out_shape=[shp, shp, shp])(w, g, m, v)
    return _rows(name, fn, [w, g, m, v], [shp, shp, shp], tr=tr)


def _place():
    return lax.axis_index("x"), lax.axis_index("y"), lax.axis_index("c")


def _other_chips(x, y):
    return [(1 - x, y), (x, 1 - y), (1 - x, 1 - y)]


ANY = pl.BlockSpec(memory_space=pl.ANY)


def _remote(src, dst, send_sem, recv_sem, to):
    return pltpu.make_async_remote_copy(src_ref=src, dst_ref=dst, send_sem=send_sem, recv_sem=recv_sem,
                                        device_id=to, device_id_type=MESH)


class _WeightGather(_NoRider):
    def __init__(self, shards):
        n_w = len(shards)
        self.operands = list(shards)
        self.out_shapes = [jax.ShapeDtypeStruct((N_CHIPS,) + s.shape, s.dtype) for s in shards]
        self.scratch = [pltpu.SemaphoreType.DMA((3, n_w))] * 4 + [pltpu.SemaphoreType.DMA((n_w,))] * 2

    def _copies(self, ins, outs, sems):
        send_sems, recv_sems, relay_send, relay_recv, own_send, own_recv = sems
        x, y, c = _place()
        my_chip, sibling = 2 * x + y, (x, y, 1 - c)
        n_w = len(ins)

        def half(w, chip, core):
            h = self.operands[w].shape[0] // 2
            return outs[w].at[chip, pl.ds(core * h, h)]

        own = [_remote(ins[w], outs[w].at[my_chip], own_send.at[w], own_recv.at[w], sibling) for w in range(n_w)]
        sends, landed, relays, relayed = [], [], [], []
        for p, (ox, oy) in enumerate(_other_chips(x, y)):
            for w in range(n_w):
                h = self.operands[w].shape[0] // 2
                sends.append(_remote(ins[w].at[pl.ds(c * h, h)], half(w, my_chip, c), send_sems.at[p, w],
                                     recv_sems.at[p, w], (ox, oy, c)))
                here = half(w, 2 * ox + oy, c)
                landed.append(_remote(here, here, send_sems.at[p, w], recv_sems.at[p, w], (ox, oy, c)))
                relays.append(_remote(here, here, relay_send.at[p, w], relay_recv.at[p, w], sibling))
                there = half(w, 2 * ox + oy, 1 - c)
                relayed.append(_remote(there, there, relay_send.at[p, w], relay_recv.at[p, w], sibling))
        return own, sends, landed, relays, relayed

    def start(self, ins, outs, sems):
        own, sends, _, _, _ = self._copies(ins, outs, sems)
        for cp in own + sends:
            cp.start()

    def relay(self, ins, outs, sems):
        _, _, landed, relays, _ = self._copies(ins, outs, sems)
        for arrival, cp in zip(landed, relays):
            arrival.wait_recv()
            cp.start()

    def finish(self, ins, outs, sems):
        own, sends, _, relays, relayed = self._copies(ins, outs, sems)
        for arrival in relayed:
            arrival.wait_recv()
        for cp in sends + relays:
            cp.wait_send()
        for cp in own:
            cp.wait()


class _ChipExchange(_NoRider):
    def __init__(self, pair_sums):
        n_w = len(pair_sums)
        self.operands = list(pair_sums)
        self.out_shapes = [jax.ShapeDtypeStruct((3,) + s.shape[1:], s.dtype) for s in pair_sums]
        self.scratch = [pltpu.SemaphoreType.DMA((3, n_w))] * 2

    def _copies(self, ins, outs, sems):
        send_sems, recv_sems = sems
        x, y, c = _place()
        return [_remote(ins[w].at[2 * ox + oy], outs[w].at[p], send_sems.at[p, w], recv_sems.at[p, w], (ox, oy, c))
                for p, (ox, oy) in enumerate(_other_chips(x, y)) for w in range(len(ins))]

    def start(self, ins, outs, sems):
        for cp in self._copies(ins, outs, sems):
            cp.start()

    def finish(self, ins, outs, sems):
        for cp in self._copies(ins, outs, sems):
            cp.wait()


class _PairExchange(_NoRider):
    def __init__(self, grads):
        n_w = len(grads)
        self.operands = list(grads)
        self.out_shapes = [jax.ShapeDtypeStruct(g.shape[:-2] + (g.shape[-2] // 2, g.shape[-1]), F32) for g in grads]
        self.scratch = [pltpu.SemaphoreType.DMA((n_w,))] * 2

    def _copies(self, ins, theirs, sems):
        send_sems, recv_sems = sems
        x, y, c = _place()
        sends = []
        for w, g in enumerate(self.operands):
            rows = pl.ds((1 - c) * (g.shape[-2] // 2), g.shape[-2] // 2)
            src = ins[w].at[:, rows, :] if g.ndim == 3 else ins[w].at[rows, :]
            sends.append(_remote(src, theirs[w], send_sems.at[w], recv_sems.at[w], (x, y, 1 - c)))
        return sends

    def start(self, ins, outs, sems):
        for cp in self._copies(ins, outs, sems):
            cp.start()

    def finish(self, ins, outs, sems):
        for cp in self._copies(ins, outs, sems):
            cp.wait()


def _run_exchange(name, plan):
    n_in, n_out = len(plan.operands), len(plan.out_shapes)
    if not n_in:
        return ()

    def body(*refs):
        parts = (refs[:n_in], refs[n_in:n_in + n_out], refs[n_in + n_out:])
        plan.start(*parts)
        plan.relay(*parts)
        plan.finish(*parts)

    return pl.pallas_call(body, name=name, in_specs=[ANY] * n_in, out_specs=[ANY] * n_out,
                          out_shape=list(plan.out_shapes), scratch_shapes=list(plan.scratch))(*plan.operands)


class _NoExchanges:
    pair_sums, landed = {}, {}

    def gather(self, names):
        return _NoRider()

    def pair(self, names, grads):
        return _NoRider()

    def paired(self, names, grads, theirs):
        pass

    def chip(self, names):
        return _NoRider()


class _StepExchanges(_NoExchanges):
    def __init__(self, shards_bf16, place):
        self.shards, self.place = shards_bf16, place
        self.pair_sums, self.landed = {}, {}

    def gather(self, names):
        return _WeightGather([self.shards[n] for n in names])

    def pair(self, names, grads):
        return _PairExchange([grads[n] for n in names])

    def paired(self, names, grads, theirs):
        for n, other in zip(names, theirs):
            self.pair_sums[n] = _pair_sum(f"pair_sum_{n}", self.place, grads[n], other)

    def chip(self, names):
        return _ChipExchange([self.pair_sums[n] for n in names])


def _pair_share(shards):
    n_w = len(shards)

    def body(*refs):
        ins, outs = refs[:n_w], refs[n_w:2 * n_w]
        send_sems, recv_sems = refs[2 * n_w:]
        x, y, c = _place()
        sends = []
        for w in range(n_w):
            h = shards[w].shape[0] // 2
            mine = outs[w].at[pl.ds(c * h, h)]
            sends.append(pltpu.make_async_remote_copy(
                src_ref=mine, dst_ref=mine, send_sem=send_sems.at[w], recv_sem=recv_sems.at[w],
                device_id=(x, y, 1 - c), device_id_type=MESH))
        for cp in sends:
            cp.start()
        for w in range(n_w):
            h = shards[w].shape[0] // 2
            theirs = outs[w].at[pl.ds((1 - c) * h, h)]
            pltpu.make_async_remote_copy(
                src_ref=theirs, dst_ref=theirs, send_sem=send_sems.at[w], recv_sem=recv_sems.at[w],
                device_id=(x, y, 1 - c), device_id_type=MESH).wait_recv()
        for cp in sends:
            cp.wait_send()

    return pl.pallas_call(
        body, name="pair_share", in_specs=[ANY] * n_w, out_specs=[ANY] * n_w,
        out_shape=[jax.ShapeDtypeStruct(s.shape, s.dtype) for s in shards],
        input_output_aliases={w: w for w in range(n_w)},
        scratch_shapes=[pltpu.SemaphoreType.DMA((n_w,)), pltpu.SemaphoreType.DMA((n_w,))],
    )(*shards)


def _all_reduce_small(vec):
    rows = vec.shape[0]

    def body(v_ref, o_ref, slots, send_sems, recv_sems):
        x, y, c = _place()
        me = 4 * x + 2 * y + c
        slots[me] = v_ref[...]
        sends = []
        for k in range(1, N_DEV):
            peer = (x ^ (k >> 2), y ^ ((k >> 1) & 1), c ^ (k & 1))
            sends.append(pltpu.make_async_remote_copy(
                src_ref=v_ref, dst_ref=slots.at[me], send_sem=send_sems.at[k - 1], recv_sem=recv_sems.at[k - 1],
                device_id=peer, device_id_type=MESH))
        for cp in sends:
            cp.start()
        for k in range(1, N_DEV):
            px, py, pc = x ^ (k >> 2), y ^ ((k >> 1) & 1), c ^ (k & 1)
            landed = slots.at[4 * px + 2 * py + pc]
            pltpu.make_async_remote_copy(
                src_ref=landed, dst_ref=landed, send_sem=send_sems.at[k - 1], recv_sem=recv_sems.at[k - 1],
                device_id=(px, py, pc), device_id_type=MESH).wait_recv()
        for cp in sends:
            cp.wait_send()
        total = slots[0]
        for d in range(1, N_DEV):
            total = total + slots[d]
        o_ref[...] = total

    vm = pl.BlockSpec(memory_space=pltpu.VMEM)
    return pl.pallas_call(
        body, name="all_reduce_small", in_specs=[vm], out_specs=vm, out_shape=jax.ShapeDtypeStruct(vec.shape, F32),
        scratch_shapes=[pltpu.VMEM((N_DEV, rows, LANES), F32), pltpu.SemaphoreType.DMA((N_DEV - 1,)),
                        pltpu.SemaphoreType.DMA((N_DEV - 1,))],
    )(vec)


def _row_tile(rows):
    fits = [tr for tr in range(16, min(rows, 512) + 1, 16) if rows % tr == 0]
    return max(fits) if fits else rows


def _pair_sum(name, place, grad, theirs):
    if grad.ndim == 2:
        return _pair_sum_joined(name, place, grad, theirs)
    n, r, c = grad.shape
    half = r // 2
    tr = _row_tile(half)
    nb = half // tr

    def body(place_ref, g_ref, t_ref, o_ref):
        o_ref[...] = (g_ref[...] + t_ref[...]).astype(BF16)

    return pl.pallas_call(
        body, name=name, out_shape=jax.ShapeDtypeStruct((n, half, c), BF16),
        grid_spec=pltpu.PrefetchScalarGridSpec(
            num_scalar_prefetch=1, grid=(n, nb),
            in_specs=[pl.BlockSpec((1, tr, c), lambda j, i, pr: (j, pr[0] * nb + i, 0)),
                      pl.BlockSpec((1, tr, c), lambda j, i, pr: (j, i, 0))],
            out_specs=pl.BlockSpec((1, tr, c), lambda j, i, pr: (j, i, 0))),
        compiler_params=_params("parallel", "parallel"),
    )(place, grad, theirs)


def _pair_sum_joined(name, place, grad, theirs):
    r, wide = grad.shape
    half, c = r // 2, wide // N_CHIPS
    tr = _row_tile(half)
    nb = half // tr

    def body(place_ref, g_ref, t_ref, o_ref):
        for j in range(N_CHIPS):
            cols = slice(j * c, (j + 1) * c)
            o_ref[j] = (g_ref[:, cols] + t_ref[:, cols]).astype(BF16)

    return pl.pallas_call(
        body, name=name, out_shape=jax.ShapeDtypeStruct((N_CHIPS, half, c), BF16),
        grid_spec=pltpu.PrefetchScalarGridSpec(
            num_scalar_prefetch=1, grid=(nb,),
            in_specs=[pl.BlockSpec((tr, wide), lambda i, pr: (pr[0] * nb + i, 0)),
                      pl.BlockSpec((tr, wide), lambda i, pr: (i, 0))],
            out_specs=pl.BlockSpec((N_CHIPS, tr, c), lambda i, pr: (0, i, 0))),
        compiler_params=_params("parallel"),
    )(place, grad, theirs)


def _sum_chips(name, place, pair_sums, landed):
    _, half, c = pair_sums.shape
    tr = _row_tile(half)
    nb = half // tr

    def body(place_ref, s_ref, q_ref, o_ref):
        total = s_ref[0].astype(F32)
        for p in range(3):
            total = total + q_ref[p].astype(F32)
        o_ref[...] = total

    return pl.pallas_call(
        body, name=name, out_shape=jax.ShapeDtypeStruct((2 * half, c), F32),
        grid_spec=pltpu.PrefetchScalarGridSpec(
            num_scalar_prefetch=1, grid=(nb,),
            in_specs=[pl.BlockSpec((1, tr, c), lambda i, pr: (pr[1], i, 0)),
                      pl.BlockSpec((3, tr, c), lambda i, pr: (0, i, 0))],
            out_specs=pl.BlockSpec((tr, c), lambda i, pr: (pr[0] * nb + i, 0))),
        compiler_params=_params("parallel"),
    )(place, pair_sums, landed)


BIG = ("w_in", "w_branch_a", "w_branch_b", "w_out", "w_ffn_gate", "w_ffn_up", "w_ffn_down", "w_ple_gate", "w_ple_proj")
HELD_TRANSPOSED = ("w_ffn_gate", "w_ffn_up")
MIXER = ("w_branch_a", "w_branch_b", "w_out")
FFN_PLE = ("w_ffn_gate", "w_ffn_up", "w_ffn_down", "w_ple_gate", "w_ple_proj")
LATE = MIXER + FFN_PLE
COLUMN_SHARDED = ("w_in", "w_branch_a", "w_branch_b", "w_ffn_gate", "w_ffn_up", "w_ple_proj")
SMALL = ("norm_mix", "w_pool", "pool_scale", "norm_ffn", "norm_ple", "norm_final")


def _join_columns(w4):
    return jnp.concatenate([w4[j] for j in range(N_CHIPS)], axis=1)


def _sds(shape, dtype):
    return jax.ShapeDtypeStruct(shape, dtype)


def _local_step(x, p, target, wf, small, ex=None):
    t, d = x.shape
    w_pool_b = small["w_pool"].astype(BF16)
    dp = w_pool_b.shape[0] * w_pool_b.shape[1]

    ex = ex or _NoExchanges()
    h1, first = _norm_fwd("norm_mix", x, small["norm_mix"], rider=ex.gather(("w_in",)))
    wf = {**wf, **dict(zip(("w_in",), first))}
    w_in = wf["w_in"]
    u, q, kv, ga, gb = _mm(
        "proj", [h1], [w_in[j] for j in range(N_CHIPS)], "nn",
        [_sds((t, dp), F32), _sds((t, dp), BF16), _sds((t, d), BF16), _sds((t, d), BF16), _sds((t, d), BF16)],
        separate=True, epilogue=lambda uq, kv_, ga_, gb_: (uq[:, :dp], uq[:, dp:], kv_, ga_, gb_), tm=512)
    pooled, ya = _pool_fwd(u, w_pool_b, small["pool_scale"])
    n_pairs = dp // LANES
    yb, late = _attn_fwd(q, 0, kv, 0, n_pairs, n_pairs, rider=ex.gather(LATE))
    wf = {**wf, **dict(zip(LATE, late))}
    w_down = wf["w_ffn_down"].reshape(-1, d)
    dff = w_down.shape[0]
    w_gate_t, w_up_t = wf["w_ffn_gate"].reshape(dff, d), wf["w_ffn_up"].reshape(dff, d)
    w_a, w_b, w_pp = _join_columns(wf["w_branch_a"]), _join_columns(wf["w_branch_b"]), _join_columns(wf["w_ple_proj"])
    w_out = wf["w_out"].reshape(d, d)
    w_pg = wf["w_ple_gate"].reshape(d, d)
    def residual_norm(branch, xv, g, w):
        xn = xv + jnp.dot(branch.astype(BF16), w, preferred_element_type=F32)
        return xn, xn * lax.rsqrt(jnp.mean(xn * xn, axis=-1, keepdims=True) + RMS_EPS) * g

    def mixer_tail(tav, tbv, gav, gbv, xv, g, w):
        merged = _sigmoid(gav) * tav + _sigmoid(gbv) * tbv
        return (tav, tbv, merged) + residual_norm(merged, xv, g, w)

    def ffn_tail(gv, uv, xv, g, w):
        act = gv * _sigmoid(gv) * uv
        return (gv, uv, act) + residual_norm(act, xv, g, w)

    stream = [_sds((t, d), F32), _sds((t, d), BF16)]
    ta, tb, merged, x1, h2 = _mm(
        "mixer_out", [ya, yb], [w_a, w_b], "nn", [_sds((t, d), BF16)] * 3 + stream,
        extras=[ga, gb, x, small["norm_ffn"]], wholes=[w_out], separate=True, epilogue=mixer_tail, tm=512)
    gate, up, act, x2, h3 = _mm(
        "ffn", [h2], [w_gate_t, w_up_t], "nt", [_sds((t, dff), BF16)] * 3 + stream,
        extras=[x1, small["norm_ple"]], wholes=[w_down], separate=True, epilogue=ffn_tail, tm=256)
    dx2, dx2_b, d_pp, d_gp, d_norm_final, loss_row, d_norm_ple = _mm(
        "ple_loss", [h3, p], [w_pg, w_pp], "nn", stream + [_sds((t, d), BF16)] * 2,
        extras=[x2, target, small["norm_final"].reshape(1, d), small["norm_ple"]], wholes=[w_pg], separate=True,
        epilogue=_ple_and_loss, sum_shapes=[_sds((1, d), F32)] * 3, tm=512)

    def through_norm(dh, xv, g, dres):
        dx, d_gain = _rms_norm_bwd(dh, xv, g)
        return dx + dres, dx + dres, d_gain

    gain_sum = [_sds((1, d), F32)]
    g_w_pp, g_w_pg = _mm_tn("g_ple", [p, h3], [d_pp, d_gp])

    def ffn_bwd(d_act, gv, uv, xv, g, dres, wg_t, wu_t):
        s = _sigmoid(gv)
        d_gate, d_up = d_act * uv * (s * (1.0 + gv * (1.0 - s))), d_act * (gv * s)
        dh2 = (jnp.dot(d_gate.astype(BF16), wg_t, preferred_element_type=F32)
               + jnp.dot(d_up.astype(BF16), wu_t, preferred_element_type=F32))
        return (d_gate, d_up) + through_norm(dh2, xv, g, dres)

    d_gate, d_up, dx1, dx1_b, d_norm_ffn = _mm(
        "ffn_bwd", [dx2_b], [w_down], "nt", [_sds((t, dff), BF16)] * 2 + stream,
        extras=[gate, up, x1, small["norm_ffn"], dx2], wholes=[w_gate_t, w_up_t], epilogue=ffn_bwd,
        sum_shapes=gain_sum, tm=256)
    g_w_down, = _mm_tn("g_ffn_down", [act], [dx2_b], tmm=512)
    g_w_gate_t, g_w_up_t = _mm_tn("g_ffn_gate_up", [d_gate, d_up], [h2], k_blocks=2)

    def merge_bwd(acc, tav, tbv, gav, gbv):
        sa, sb = _sigmoid(gav), _sigmoid(gbv)
        return acc * sa, acc * sb, acc * tav * sa * (1.0 - sa), acc * tbv * sb * (1.0 - sb)

    big = {
        "w_ffn_gate": g_w_gate_t.reshape(wf["w_ffn_gate"].shape), "w_ffn_up": g_w_up_t.reshape(wf["w_ffn_up"].shape),
        "w_ffn_down": g_w_down.reshape(wf["w_ffn_down"].shape),
        "w_ple_gate": g_w_pg.reshape(wf["w_ple_gate"].shape), "w_ple_proj": g_w_pp,
    }
    (d_ta, d_tb, d_ga, d_gb), theirs = _mm(
        "d_merged", [dx1_b], [w_out], "nt", [_sds((t, d), BF16)] * 4, extras=[ta, tb, ga, gb], epilogue=merge_bwd,
        tm=512, rider=ex.pair(FFN_PLE, big))
    ex.paired(FFN_PLE, big, theirs)
    g_w_out, big["w_branch_a"], big["w_branch_b"] = _mm_tn("g_mixer", [merged, ya, yb], [dx1_b, d_ta, d_tb])
    big["w_out"] = g_w_out.reshape(wf["w_out"].shape)
    (d_ya, d_yb), theirs = _mm(
        "d_branches", [d_ta, d_tb], [w_a, w_b], "nt", [_sds((t, dp), F32), _sds((t, dp), BF16)], separate=True,
        rider=ex.pair(MIXER, big))
    ex.paired(MIXER, big, theirs)
    d_u, g_w_pool, d_pool_scale = _pool_bwd(d_ya, pooled, w_pool_b, small["pool_scale"])
    (d_q, d_k, d_v), landed = _attn_bwd(q, 0, kv, 0, n_pairs, d_yb, n_pairs, rider=ex.chip(LATE))
    ex.landed.update(zip(LATE, landed))
    d_proj = [(d_u, d_q), (d_k, d_v), d_ga, d_gb]
    big["w_in"], = _mm_tn("g_w_in", [h1], d_proj, tmm=512, stacked=True)
    ex.paired(("w_in",), big, _run_exchange("pair_exchange_w_in", ex.pair(("w_in",), big)))
    (grad_x, d_norm_mix), landed = _mm(
        "d_h1", d_proj, [w_in[j] for j in range(N_CHIPS)], "nt", [_sds((t, d), F32)],
        extras=[x, small["norm_mix"], dx1], epilogue=lambda dh, xv, g, dres: through_norm(dh, xv, g, dres)[1:],
        sum_shapes=gain_sum, tm=512, rider=ex.chip(("w_in",)))
    ex.landed.update(zip(("w_in",), landed))
    small_g = {"norm_mix": d_norm_mix, "w_pool": g_w_pool, "pool_scale": d_pool_scale, "norm_ffn": d_norm_ffn,
               "norm_ple": d_norm_ple, "norm_final": d_norm_final}
    return grad_x, big, small_g, loss_row


def _split2(res, n):
    return res[:n], res[n:]


def _pack_small(small_g, loss_row):
    parts, layout = [], []
    for name in SMALL + ("loss",):
        v = (loss_row if name == "loss" else small_g[name]).reshape(-1, LANES)
        pad = (-v.shape[0]) % 8
        if pad:
            v = jnp.concatenate([v, jnp.zeros((pad, LANES), F32)], axis=0)
        layout.append((name, sum(q.shape[0] for q in parts), v.shape[0]))
        parts.append(v)
    return jnp.concatenate(parts, axis=0), layout


def kernel(x, p, norm_mix, w_in, w_pool, pool_scale, w_branch_a, w_branch_b, w_out, norm_ffn, w_ffn_gate, w_ffn_up, w_ffn_down, norm_ple, w_ple_gate, w_ple_proj, norm_final, loss_target, m_norm_mix, m_w_in, m_w_pool, m_pool_scale, m_w_branch_a, m_w_branch_b, m_w_out, m_norm_ffn, m_w_ffn_gate, m_w_ffn_up, m_w_ffn_down, m_norm_ple, m_w_ple_gate, m_w_ple_proj, m_norm_final, v_norm_mix, v_w_in, v_w_pool, v_pool_scale, v_w_branch_a, v_w_branch_b, v_w_out, v_norm_ffn, v_w_ffn_gate, v_w_ffn_up, v_w_ffn_down, v_norm_ple, v_w_ple_gate, v_w_ple_proj, v_norm_final):
    given = dict(locals())
    names = BIG + SMALL
    order = ("norm_mix", "w_in", "w_pool", "pool_scale", "w_branch_a", "w_branch_b", "w_out", "norm_ffn", "w_ffn_gate",
             "w_ffn_up", "w_ffn_down", "norm_ple", "w_ple_gate", "w_ple_proj", "norm_final")
    t, d = x.shape[1], x.shape[2]
    def local(a, n):
        return jnp.swapaxes(a[0], 0, 1) if n in HELD_TRANSPOSED else a[0]

    def back(a, n):
        return (jnp.swapaxes(a, 0, 1) if n in HELD_TRANSPOSED else a)[None]

    shard = {n: local(given[n], n) for n in BIG}
    small = {"norm_mix": norm_mix, "w_pool": w_pool[0], "pool_scale": pool_scale, "norm_ffn": norm_ffn,
             "norm_ple": norm_ple, "norm_final": norm_final}

    place = jnp.stack([lax.axis_index("c"), 2 * lax.axis_index("x") + lax.axis_index("y")]).astype(jnp.int32)
    ex = _StepExchanges({n: shard[n].astype(BF16) for n in BIG}, place)
    grad_x, _, small_g, loss_row = _local_step(
        x.reshape(t, d), p.reshape(t, p.shape[-1]), loss_target.reshape(t, d), {}, small, ex)
    halves = [_sum_chips(f"chip_sum_{n}", place, ex.pair_sums[n], ex.landed[n]) for n in BIG]
    grads = dict(zip(BIG, _pair_share(halves)))

    packed, layout = _pack_small(small_g, loss_row)
    reduced = _all_reduce_small(packed)
    for name, start, rows in layout:
        if name == "loss":
            loss = jnp.sum(reduced[start:start + rows])
        else:
            n_el = small[name].size
            grads[name] = reduced[start:start + rows].reshape(-1)[:n_el]

    deltas, new_m, new_v = {}, {}, {}
    for n in order:
        if n in BIG:
            w, m, v = shard[n], local(given["m_" + n], n), local(given["v_" + n], n)
            dl, mn, vn = _adamw(f"adamw_{n}", w, grads[n], m, v)
            grads[n], deltas[n], new_m[n], new_v[n] = [back(a, n) for a in (grads[n], dl, mn, vn)]
        else:
            w, full = small[n], given[n].shape
            shape2 = (1, w.shape[0]) if w.ndim == 1 else (w.shape if w.ndim == 2 else (w.shape[0] * w.shape[1], w.shape[2]))
            dl, mn, vn = _adamw(f"adamw_{n}", w.reshape(shape2), grads[n].reshape(shape2),
                                given["m_" + n].reshape(shape2), given["v_" + n].reshape(shape2))
            grads[n], deltas[n], new_m[n], new_v[n] = [a.reshape(full) for a in (grads[n], dl, mn, vn)]

    return (loss, grad_x.reshape(x.shape), *[grads[n] for n in order], *[deltas[n] for n in order],
            *[new_m[n] for n in order], *[new_v[n] for n in order])
```

```python
import functools
import math

import jax
import jax.numpy as jnp
from jax import lax
from jax.experimental import pallas as pl
from jax.experimental.pallas import tpu as pltpu

F32 = jnp.float32
BF16 = jnp.bfloat16
MESH = pl.DeviceIdType.MESH

RMS_EPS = 1e-6
POOL_WINDOWS = (2, 4, 8, 16)
POOL_HALO = 16
HEAD_DIM = 64
LANES = 128
ATT_BLOCK = 256
ATT_CHAINS = 2
ATT_CHUNK = 256
ATT_SLAB = 256
ATT_SCALE = 1.0 / math.sqrt(HEAD_DIM)
LOG2_E = 1.4426950408889634
ATT_EXIT_BELOW = -150.5
ADAM_LR, ADAM_B1, ADAM_B2, ADAM_EPS, ADAM_WD, ADAM_STEP = 0.001, 0.9, 0.999, 1e-08, 0.01, 10
V7X_VMEM_LIMIT_BYTES = 56 * 1024 * 1024
N_CHIPS = 4
N_DEV = 8


def _params(*semantics):
    return pltpu.CompilerParams(dimension_semantics=semantics, vmem_limit_bytes=V7X_VMEM_LIMIT_BYTES)


def _sigmoid(z):
    return 1.0 / (1.0 + jnp.exp(-z))


def _tiled_spec(shape, tm, tn, n_total, at):
    rows, width = shape
    if rows == 1:
        if width == n_total:
            return pl.BlockSpec((1, tn), at(lambda i, j: (0, j)))
        return pl.BlockSpec((1, width), at(lambda i, j: (0, 0)))
    if width == n_total:
        return pl.BlockSpec((tm, tn), at(lambda i, j: (i, j)))
    assert tn == n_total, "an operand narrower than the output needs whole output rows per tile"
    return pl.BlockSpec((tm, width), at(lambda i, j: (i, 0)))


def _column_pieces(operands):
    pieces = [tuple(a) if isinstance(a, (tuple, list)) else (a,) for a in operands]
    return [p for ps in pieces for p in ps], [len(ps) for ps in pieces]


def _load_bf16(refs, counts):
    tiles, k = [], 0
    for n in counts:
        parts = [r[...] for r in refs[k:k + n]]
        parts = [t if t.dtype == BF16 else t.astype(BF16) for t in parts]
        tiles.append(parts[0] if n == 1 else jnp.concatenate(parts, axis=1))
        k += n
    return tiles


def _mm(name, a_list, b_list, mode, out_shapes, epilogue=None, extras=(), tm=1024, tn=None, separate=False,
        sum_shapes=(), rider=None, wholes=()):
    flat_a, counts = _column_pieces(a_list)
    m_total = flat_a[0].shape[0]
    n_total = b_list[0].shape[1] if mode == "nn" else b_list[0].shape[0]
    tn = n_total if tn is None else tn
    tm = min(tm, m_total)
    assert m_total % tm == 0 and n_total % tn == 0 and (not sum_shapes or tn == n_total)
    n_a, n_b, n_extra, n_out = len(counts), len(b_list), len(extras), len(out_shapes)
    assert n_a in (1, n_b)
    dims = (((1,), (0,)), ((), ())) if mode == "nn" else (((1,), (1,)), ((), ()))
    with_rider = rider is not None
    rider = rider or _NoRider()
    grid = (n_total // tn, m_total // tm)

    def at(index):
        return lambda j, i: index(i, j)

    def body(*refs):
        ins, o_refs, _, riding = rider.split(refs, len(flat_a) + n_b + n_extra + len(wholes), n_out + len(sum_shapes))
        a_refs, b_refs = ins[:len(flat_a)], ins[len(flat_a):len(flat_a) + n_b]
        e_refs, w_refs = ins[len(flat_a) + n_b:len(flat_a) + n_b + n_extra], ins[len(flat_a) + n_b + n_extra:]
        at_first = (pl.program_id(0) == 0) & (pl.program_id(1) == 0)
        at_last = (pl.program_id(0) == grid[0] - 1) & (pl.program_id(1) == grid[1] - 1)
        top, bottom = rider.at_steps(riding, at_first, at_first, at_last)
        top()
        lefts = _load_bf16(a_refs, counts)
        products = [lax.dot_general(lefts[s % n_a], b_refs[s][...], dims, preferred_element_type=F32)
                    for s in range(n_b)]
        if not separate:
            products = [functools.reduce(lambda p, r: p + r, products)]
        extra_tiles = [e[...].astype(F32) for e in e_refs]
        outs = products if epilogue is None else epilogue(*products, *extra_tiles, *[w[...] for w in w_refs])
        for o_ref, o in zip(o_refs[:n_out], outs[:n_out]):
            o_ref[...] = o.astype(o_ref.dtype)
        if sum_shapes:
            @pl.when(pl.program_id(1) == 0)
            def _():
                for s_ref in o_refs[n_out:]:
                    s_ref[...] = jnp.zeros_like(s_ref)

            for s_ref, s in zip(o_refs[n_out:], outs[n_out:]):
                s_ref[...] += s
        bottom()

    once = dict(pipeline_mode=pl.Buffered(1)) if tn == n_total else {}
    in_specs = [pl.BlockSpec((tm, a.shape[1]), at(lambda i, j: (i, 0))) for a in flat_a]
    if mode == "nn":
        in_specs += [pl.BlockSpec((b.shape[0], tn), at(lambda i, j: (0, j)), **once) for b in b_list]
    else:
        in_specs += [pl.BlockSpec((tn, b.shape[1]), at(lambda i, j: (j, 0)), **once) for b in b_list]
    in_specs += [_tiled_spec(e.shape, tm, tn, n_total, at) for e in extras]
    in_specs += [pl.BlockSpec(w.shape, lambda j, i: (0, 0), pipeline_mode=pl.Buffered(1)) for w in wholes]
    out_specs = [_tiled_spec(o.shape, tm, tn, n_total, at) for o in out_shapes]
    out_specs += [pl.BlockSpec(s.shape, at(lambda i, j: (0, 0))) for s in sum_shapes]
    semantics = ("arbitrary", "arbitrary") if sum_shapes or rider.operands else ("parallel", "parallel")
    res = pl.pallas_call(
        body, name=name, grid=grid, in_specs=in_specs + [ANY] * len(rider.operands),
        out_specs=out_specs + [ANY] * len(rider.out_shapes),
        out_shape=list(out_shapes) + list(sum_shapes) + list(rider.out_shapes), scratch_shapes=list(rider.scratch),
        compiler_params=_params(*semantics),
    )(*flat_a, *b_list, *extras, *wholes, *rider.operands)
    n_own = len(out_shapes) + len(sum_shapes)
    return (res[:n_own], res[n_own:]) if with_rider else res


def _mm_tn(name, a_list, b_list, tmm=1024, stacked=False, k_blocks=1):
    flat_b, counts = _column_pieces(b_list)
    n_a, n_b = len(a_list), len(counts)
    n_prod = max(n_a, n_b)
    m_total = a_list[0].shape[0]
    ks = [a_list[s % n_a].shape[1] for s in range(n_prod)]
    widths = [sum(p.shape[1] for p in flat_b[sum(counts[:s]):sum(counts[:s + 1])]) for s in range(n_b)]
    widths = [widths[s % n_b] for s in range(n_prod)]
    tmm = min(tmm, m_total)
    assert m_total % tmm == 0 and all(k % k_blocks == 0 for k in ks)
    assert n_a in (1, n_prod) and n_b in (1, n_prod) and not (stacked and n_a > 1)

    def body(*refs):
        a_refs, b_refs, o_refs = refs[:n_a], refs[n_a:n_a + len(flat_b)], refs[n_a + len(flat_b):]

        @pl.when(pl.program_id(1) == 0)
        def _():
            for o_ref in o_refs:
                o_ref[...] = jnp.zeros_like(o_ref)

        lefts, rights = _load_bf16(a_refs, [1] * n_a), _load_bf16(b_refs, counts)
        for s in range(n_prod):
            product = lax.dot_general(lefts[s % n_a], rights[s % n_b], (((0,), (0,)), ((), ())),
                                      preferred_element_type=F32)
            if stacked:
                o_refs[0][s] += product
            else:
                o_refs[s][...] += product

    in_specs = [pl.BlockSpec((tmm, a.shape[1] // k_blocks), lambda kb, m: (m, kb)) for a in a_list]
    in_specs += [pl.BlockSpec((tmm, b.shape[1]), lambda kb, m: (m, 0)) for b in flat_b]
    if stacked:
        out_shape = [jax.ShapeDtypeStruct((n_prod, ks[0], widths[0]), F32)]
        out_specs = [pl.BlockSpec((n_prod, ks[0] // k_blocks, widths[0]), lambda kb, m: (0, kb, 0))]
    else:
        out_shape = [jax.ShapeDtypeStruct((k, w), F32) for k, w in zip(ks, widths)]
        out_specs = [pl.BlockSpec((k // k_blocks, w), lambda kb, m: (kb, 0)) for k, w in zip(ks, widths)]
    return pl.pallas_call(
        body, name=name, grid=(k_blocks, m_total // tmm), in_specs=in_specs, out_specs=out_specs, out_shape=out_shape,
        compiler_params=_params("arbitrary", "arbitrary"),
    )(*a_list, *flat_b)


def _rows(name, fn, ins, tile_outs, sum_outs=(), tr=512, rider=None):
    t_total = max(a.shape[0] for a in ins)
    tr = min(tr, t_total)
    assert t_total % tr == 0
    n_in, n_tile = len(ins), len(tile_outs)
    rider = rider or _NoRider()
    n_steps = t_total // tr

    def body(*refs):
        own_ins, own_outs, _, riding = rider.split(refs, n_in, n_tile + len(sum_outs))
        step = pl.program_id(0)
        top, bottom = rider.at_steps(riding, step == 0, step == n_steps - 1, step == n_steps - 1)
        top()
        refs = tuple(own_ins) + tuple(own_outs)
        outs = fn(*[r[...].astype(F32) for r in refs[:n_in]])
        for o_ref, o in zip(refs[n_in:n_in + n_tile], outs[:n_tile]):
            o_ref[...] = o.astype(o_ref.dtype)
        if sum_outs:
            @pl.when(pl.program_id(0) == 0)
            def _():
                for s_ref in refs[n_in + n_tile:]:
                    s_ref[...] = jnp.zeros_like(s_ref)

            for s_ref, s in zip(refs[n_in + n_tile:], outs[n_tile:]):
                s_ref[...] += s
        bottom()

    def spec(shape):
        if shape[0] == 1:
            return pl.BlockSpec(shape, lambda i: (0, 0))
        return pl.BlockSpec((tr, shape[1]), lambda i: (i, 0))

    return pl.pallas_call(
        body, name=name, grid=(n_steps,), in_specs=[spec(a.shape) for a in ins] + [ANY] * len(rider.operands),
        out_specs=[spec(o.shape) for o in tile_outs] + [spec(s.shape) for s in sum_outs] + [ANY] * len(rider.out_shapes),
        out_shape=list(tile_outs) + list(sum_outs) + list(rider.out_shapes), scratch_shapes=list(rider.scratch),
        compiler_params=_params("arbitrary" if sum_outs or rider.operands else "parallel"),
    )(*ins, *rider.operands)


def _norm_fwd(name, x, gain, rider=None):
    def fn(xv, g):
        inv = lax.rsqrt(jnp.mean(xv * xv, axis=-1, keepdims=True) + RMS_EPS)
        return (xv * inv * g,)

    res = _rows(name, fn, [x, gain], [jax.ShapeDtypeStruct(x.shape, BF16)], rider=rider)
    return res[0], res[1:]


def _rms_norm_bwd(dh, xv, g):
    inv = lax.rsqrt(jnp.mean(xv * xv, axis=-1, keepdims=True) + RMS_EPS)
    xn = xv * inv
    dxn = dh * g
    return inv * (dxn - xn * jnp.mean(dxn * xn, axis=-1, keepdims=True)), jnp.sum(dh * xn, axis=0, keepdims=True)


def _ple_and_loss(gv, pv, x2v, tv, g_final, g_ple, w_pg):
    d = x2v.shape[1]
    s = _sigmoid(gv)
    xv = x2v + s * pv
    inv = lax.rsqrt(jnp.mean(xv * xv, axis=-1, keepdims=True) + RMS_EPS)
    err = xv * inv * g_final - tv
    dx3, d_final = _rms_norm_bwd(err * (1.0 / d), xv, g_final)
    d_pp, d_gp = dx3 * s, dx3 * pv * s * (1.0 - s)
    dh3 = lax.dot_general(d_gp.astype(BF16), w_pg, (((1,), (1,)), ((), ())), preferred_element_type=F32)
    dx2, d_ple = _rms_norm_bwd(dh3, x2v, g_ple)
    dx2 = dx2 + dx3
    return dx2, dx2, d_pp, d_gp, d_final, (0.5 / d) * jnp.sum(err * err, axis=0, keepdims=True), d_ple


def _window_counts(t_pos, w):
    return jnp.minimum(t_pos + 1, w).astype(F32)


def _pool_fwd(u, w_pool, scale, tr=512):
    t_total, width = u.shape
    tr = min(tr, t_total)
    n_groups = len(POOL_WINDOWS)
    gdim = width // n_groups
    ext = tr + POOL_HALO

    def body(u_ref, halo_ref, w_ref, s_ref, pooled_ref, ya_ref):
        i = pl.program_id(0)
        halo = jnp.where(i == 0, 0.0, halo_ref[...])
        t_pos = i * tr + lax.broadcasted_iota(jnp.int32, (tr, 1), 0)
        for g, w in enumerate(POOL_WINDOWS):
            cols = slice(g * gdim, (g + 1) * gdim)
            main = u_ref[:, cols]
            win = jnp.concatenate([halo[:, cols], main], axis=0)
            span = 1
            while span < w:
                win = win + pltpu.roll(win, span, 0)
                span *= 2
            pooled = win[POOL_HALO:, :] * (1.0 / _window_counts(t_pos, w)) - main
            pooled_b = pooled.astype(BF16)
            pooled_ref[:, cols] = pooled_b
            mixed = jnp.dot(pooled_b, w_ref[g], preferred_element_type=F32)
            ya_ref[:, cols] = (mixed * s_ref[:, cols]).astype(BF16)

    hb = tr // POOL_HALO
    return pl.pallas_call(
        body, name="pool_fwd", grid=(t_total // tr,),
        in_specs=[pl.BlockSpec((tr, width), lambda i: (i, 0)),
                  pl.BlockSpec((POOL_HALO, width), lambda i: (jnp.maximum(i * hb - 1, 0), 0)),
                  pl.BlockSpec((n_groups, gdim, gdim), lambda i: (0, 0, 0)),
                  pl.BlockSpec((1, width), lambda i: (0, 0))],
        out_specs=[pl.BlockSpec((tr, width), lambda i: (i, 0)), pl.BlockSpec((tr, width), lambda i: (i, 0))],
        out_shape=[jax.ShapeDtypeStruct(u.shape, BF16), jax.ShapeDtypeStruct(u.shape, BF16)],
        compiler_params=_params("parallel"),
    )(u, u, w_pool, scale)


def _pool_bwd(dya, pooled, w_pool, scale, tr=512):
    t_total, width = dya.shape
    tr = min(tr, t_total)
    n_groups = len(POOL_WINDOWS)
    gdim = width // n_groups
    ext = tr + POOL_HALO
    n_tiles = t_total // tr

    def body(d_ref, halo_ref, p_ref, w_ref, s_ref, du_ref, dw_ref, ds_ref):
        i = pl.program_id(0)

        @pl.when(i == 0)
        def _():
            dw_ref[...] = jnp.zeros_like(dw_ref)
            ds_ref[...] = jnp.zeros_like(ds_ref)

        halo = jnp.where(i == n_tiles - 1, 0.0, halo_ref[...])
        t_pos = i * tr + lax.broadcasted_iota(jnp.int32, (ext, 1), 0)
        for g, w in enumerate(POOL_WINDOWS):
            cols = slice(g * gdim, (g + 1) * gdim)
            sc = s_ref[:, cols]
            d_main = d_ref[:, cols]
            pooled_b = p_ref[:, cols]
            mixed = jnp.dot(pooled_b, w_ref[g], preferred_element_type=F32)
            ds_ref[:, cols] += jnp.sum(d_main * mixed, axis=0, keepdims=True)
            dmix = (jnp.concatenate([d_main, halo[:, cols]], axis=0) * sc).astype(BF16)
            dw_ref[g] += lax.dot_general(pooled_b, dmix[:tr, :], (((0,), (0,)), ((), ())),
                                         preferred_element_type=F32)
            dpool = lax.dot_general(dmix, w_ref[g], (((1,), (1,)), ((), ())), preferred_element_type=F32)
            win = dpool * (1.0 / _window_counts(t_pos, w))
            span = 1
            while span < w:
                win = win + pltpu.roll(win, ext - span, 0)
                span *= 2
            du_ref[:, cols] = (win[:tr, :] - dpool[:tr, :]).astype(BF16)

    hb = tr // POOL_HALO
    last_halo = t_total // POOL_HALO - 1
    return pl.pallas_call(
        body, name="pool_bwd", grid=(n_tiles,),
        in_specs=[pl.BlockSpec((tr, width), lambda i: (i, 0)),
                  pl.BlockSpec((POOL_HALO, width), lambda i: (jnp.minimum((i + 1) * hb, last_halo), 0)),
                  pl.BlockSpec((tr, width), lambda i: (i, 0)),
                  pl.BlockSpec((n_groups, gdim, gdim), lambda i: (0, 0, 0)),
                  pl.BlockSpec((1, width), lambda i: (0, 0))],
        out_specs=[pl.BlockSpec((tr, width), lambda i: (i, 0)),
                   pl.BlockSpec((n_groups, gdim, gdim), lambda i: (0, 0, 0)),
                   pl.BlockSpec((1, width), lambda i: (0, 0))],
        out_shape=[jax.ShapeDtypeStruct(dya.shape, BF16), jax.ShapeDtypeStruct((n_groups, gdim, gdim), F32),
                   jax.ShapeDtypeStruct((1, width), F32)],
        compiler_params=_params("arbitrary"),
    )(dya, dya, pooled, w_pool, scale)


def _head_masks():
    lane = lax.broadcasted_iota(jnp.int32, (1, LANES), 1)
    return lane < HEAD_DIM


def _stack_heads(tile, first):
    zero = jnp.zeros_like(tile)
    return jnp.concatenate([jnp.where(first, tile, zero), jnp.where(first, zero, tile)], axis=0)


def _causal_mask(t_pos, k_start):
    col = lax.broadcasted_iota(jnp.int32, (1, 2 * ATT_SLAB), 1)
    return k_start + (col & (ATT_SLAB - 1)) < t_pos


def _slab_scores(q, kd, mask):
    z2 = lax.dot_general(q, kd, (((1,), (1,)), ((), ())), preferred_element_type=F32) * LOG2_E
    log_hit = jnp.minimum(z2, 0.0) - jnp.log2(1.0 + jnp.exp2(-jnp.abs(z2)))
    log_fail = log_hit - z2
    return log_hit, (log_fail if mask is None else jnp.where(mask, log_fail, 0.0))


def _weights(log_hit, suffix, mask):
    arg = log_hit + suffix
    return jnp.exp2(arg if mask is None else jnp.where(mask, arg, -1e30))


def _tri(upper):
    r = lax.broadcasted_iota(jnp.int32, (ATT_CHUNK, ATT_CHUNK), 0)
    c = lax.broadcasted_iota(jnp.int32, (ATT_CHUNK, ATT_CHUNK), 1)
    return jnp.where(r > c if upper else r < c, 1.0, 0.0).astype(BF16)


def _tri_spec():
    return pl.BlockSpec((ATT_CHUNK, ATT_CHUNK), lambda h, i: (0, 0), pipeline_mode=pl.Buffered(1))


def _scan_chunk(v, tri):
    return jnp.dot(v.astype(BF16), tri, preferred_element_type=F32)


def _lane_bcast(col):
    return jnp.broadcast_to(col, (col.shape[0], LANES))


def _scan_slab(v, tri, carries, from_right):
    n_chunks = ATT_SLAB // ATT_CHUNK
    edge = 0 if from_right else ATT_CHUNK - 1
    parts, new_carries = [None] * (2 * n_chunks), []
    for head in range(2):
        run = carries[head]
        for c in (reversed(range(n_chunks)) if from_right else range(n_chunks)):
            lo_col = head * ATT_SLAB + c * ATT_CHUNK
            vc = v[:, lo_col:lo_col + ATT_CHUNK]
            sc = _scan_chunk(vc, tri)
            parts[head * n_chunks + c] = sc + jnp.concatenate([run] * (ATT_CHUNK // LANES), axis=1)
            run = run + _lane_bcast(sc[:, edge:edge + 1] + vc[:, edge:edge + 1])
        new_carries.append(run)
    return jnp.concatenate(parts, axis=1), new_carries


def _fold_heads(stacked, first):
    s = stacked.shape[0] // 2
    return jnp.where(first, stacked[:s], stacked[s:])


class _NoRider:
    operands, out_shapes, scratch = (), (), ()

    def split(self, refs, n_base_in, n_base_out):
        n_in, n_out, n_sem = len(self.operands), len(self.out_shapes), len(self.scratch)
        a = n_base_in + n_in
        b = a + n_base_out + n_out
        mine = (refs[n_base_in:a], refs[a + n_base_out:b], refs[b:b + n_sem])
        return refs[:n_base_in], refs[a:a + n_base_out], refs[b + n_sem:], mine

    def start(self, ins, outs, sems):
        pass

    def relay(self, ins, outs, sems):
        pass

    def finish(self, ins, outs, sems):
        pass

    def at_steps(self, refs, first_step, relay_step, last_step):
        if not self.operands:
            return (lambda: None), (lambda: None)

        def top():
            pl.when(first_step)(lambda: self.start(*refs))
            pl.when(relay_step)(lambda: self.relay(*refs))

        return top, lambda: pl.when(last_step)(lambda: self.finish(*refs))


def _attn_fwd(q_src, q_col, kv_src, k_col, v_col, n_pairs=4, rider=_NoRider()):
    t_total = q_src.shape[0]
    blk = ATT_BLOCK
    n_steps = t_total // (ATT_CHAINS * blk)
    assert t_total % ATT_SLAB == 0 and ATT_SLAB == ATT_BLOCK

    def body(*refs):
        (q_ref, k_ref, v_ref, suffix_ref), (o_ref,), _, riding = rider.split(refs, 4, 1)
        h, ii = pl.program_id(0), pl.program_id(1)
        top, bottom = rider.at_steps(riding, (h == 0) & (ii == 0), (h == n_pairs - 1) & (ii == 0),
                                     (h == n_pairs - 1) & (ii == n_steps - 1))
        top()
        first = _head_masks()
        suffix_tri = suffix_ref[...]
        blocks = [ATT_CHAINS * ii + c for c in range(ATT_CHAINS)]
        qs = [q_ref[c * blk:(c + 1) * blk, :] * ATT_SCALE for c in range(ATT_CHAINS)]
        t_pos = [b * blk + lax.broadcasted_iota(jnp.int32, (blk, 1), 0) for b in blocks]

        def one(c, t, chain, on_diagonal):
            _, acc, right_a, right_b = chain
            k_start = pl.multiple_of((blocks[c] - t) * ATT_SLAB, ATT_SLAB)
            kd = _stack_heads(k_ref[pl.ds(k_start, ATT_SLAB), :], first)
            vd = _stack_heads(v_ref[pl.ds(k_start, ATT_SLAB), :], first)
            mask = _causal_mask(t_pos[c], k_start) if on_diagonal else None
            log_hit, log_fail = _slab_scores(qs[c], kd, mask)
            suffix, (right_a, right_b) = _scan_slab(log_fail, suffix_tri, (right_a, right_b), from_right=True)
            a = _weights(log_hit, suffix, mask).astype(BF16)
            acc = acc + jnp.dot(a, vd, preferred_element_type=F32)
            return jnp.max(jnp.maximum(right_a, right_b)), acc, right_a, right_b

        def step(state, on_diagonal):
            t, chains = state
            return t + 1, tuple(one(c, t, chains[c], on_diagonal) for c in range(ATT_CHAINS))

        def more(state):
            t, chains = state
            return (t <= blocks[0]) & (functools.reduce(jnp.maximum, [ch[0] for ch in chains]) > ATT_EXIT_BELOW)

        zero = jnp.zeros((blk, LANES), F32)
        state = step((0, ((jnp.float32(0.0), zero, zero, zero),) * ATT_CHAINS), on_diagonal=True)
        t, chains = lax.while_loop(more, functools.partial(step, on_diagonal=False), state)
        for c in range(ATT_CHAINS):
            chain = chains[c]
            if c:
                _, chain = lax.while_loop(
                    lambda s, c=c: (s[0] <= blocks[c]) & (s[1][0] > ATT_EXIT_BELOW),
                    lambda s, c=c: (s[0] + 1, one(c, s[0], s[1], False)), (t, chain))
            o_ref[c * blk:(c + 1) * blk, :] = chain[1].astype(BF16)
        bottom()

    rows = ATT_CHAINS * blk
    res = pl.pallas_call(
        body, name="attn_fwd", grid=(n_pairs, n_steps),
        in_specs=[pl.BlockSpec((rows, LANES), lambda h, i: (i, q_col + h)),
                  pl.BlockSpec((t_total, LANES), lambda h, i: (0, k_col + h)),
                  pl.BlockSpec((t_total, LANES), lambda h, i: (0, v_col + h)), _tri_spec()] + [ANY] * len(rider.operands),
        out_specs=[pl.BlockSpec((rows, LANES), lambda h, i: (i, h))] + [ANY] * len(rider.out_shapes),
        out_shape=[jax.ShapeDtypeStruct((t_total, n_pairs * LANES), BF16)] + list(rider.out_shapes),
        scratch_shapes=list(rider.scratch),
        compiler_params=_params("arbitrary", "arbitrary"),
    )(q_src, kv_src, kv_src, _tri(upper=True), *rider.operands)
    return res[0], res[1:]


def _attn_bwd(q_src, q_col, kv_src, k_col, v_col, dy, n_pairs=4, rider=_NoRider()):
    t_total = q_src.shape[0]
    blk = ATT_BLOCK
    n_steps = t_total // (ATT_CHAINS * blk)
    n_slabs = t_total // ATT_SLAB
    assert t_total % ATT_SLAB == 0 and ATT_SLAB == ATT_BLOCK

    def body(*refs):
        ins, (dq_ref, dk_ref, dv_ref), (g_s, dk_acc, dv_acc), riding = rider.split(refs, 6, 3)
        q_ref, dy_ref, k_ref, v_ref, suffix_ref, prefix_ref = ins
        h, ii = pl.program_id(0), pl.program_id(1)
        top, bottom = rider.at_steps(riding, (h == 0) & (ii == 0), (h == n_pairs - 1) & (ii == 0),
                                     (h == n_pairs - 1) & (ii == n_steps - 1))
        top()

        @pl.when(ii == 0)
        def _():
            dk_acc[...] = jnp.zeros_like(dk_acc)
            dv_acc[...] = jnp.zeros_like(dv_acc)

        first = _head_masks()
        suffix_tri = suffix_ref[...]
        prefix_tri = prefix_ref[...]
        blocks = [ATT_CHAINS * ii + c for c in range(ATT_CHAINS)]
        rows = [slice(c * blk, (c + 1) * blk) for c in range(ATT_CHAINS)]
        qs = [q_ref[r, :] * ATT_SCALE for r in rows]
        dys = [dy_ref[r, :] for r in rows]
        t_pos = [b * blk + lax.broadcasted_iota(jnp.int32, (blk, 1), 0) for b in blocks]

        def one1(c, t, chain, on_diagonal):
            _, right_a, right_b = chain
            slab = blocks[c] - t
            k_start = pl.multiple_of(slab * ATT_SLAB, ATT_SLAB)
            kd = _stack_heads(k_ref[pl.ds(k_start, ATT_SLAB), :], first)
            vd = _stack_heads(v_ref[pl.ds(k_start, ATT_SLAB), :], first)
            mask = _causal_mask(t_pos[c], k_start) if on_diagonal else None
            log_hit, log_fail = _slab_scores(qs[c], kd, mask)
            suffix, (right_a, right_b) = _scan_slab(log_fail, suffix_tri, (right_a, right_b), from_right=True)
            a = _weights(log_hit, suffix, mask)
            da = lax.dot_general(dys[c], vd, (((1,), (1,)), ((), ())), preferred_element_type=F32)
            g_s[c, slab] = (da * a).astype(BF16)
            dv_acc[pl.ds(k_start, ATT_SLAB), :] += _fold_heads(lax.dot_general(
                a.astype(BF16), dys[c], (((0,), (0,)), ((), ())), preferred_element_type=F32), first)
            return jnp.max(jnp.maximum(right_a, right_b)), right_a, right_b

        def step1(state, on_diagonal):
            t, chains = state
            return t + 1, tuple(one1(c, t, chains[c], on_diagonal) for c in range(ATT_CHAINS))

        def more(state):
            t, chains = state
            return (t <= blocks[0]) & (functools.reduce(jnp.maximum, [ch[0] for ch in chains]) > ATT_EXIT_BELOW)

        zero = jnp.zeros((blk, LANES), F32)
        state = step1((0, ((jnp.float32(0.0), zero, zero),) * ATT_CHAINS), on_diagonal=True)
        joint, chains = lax.while_loop(more, functools.partial(step1, on_diagonal=False), state)
        done = [joint]
        for c in range(1, ATT_CHAINS):
            done.append(lax.while_loop(
                lambda s, c=c: (s[0] <= blocks[c]) & (s[1][0] > ATT_EXIT_BELOW),
                lambda s, c=c: (s[0] + 1, one1(c, s[0], s[1], False)), (joint, chains[c]))[0])

        def one2(c, t, carry, on_diagonal):
            dq, left_a, left_b = carry
            slab = blocks[c] - t
            k_start = pl.multiple_of(slab * ATT_SLAB, ATT_SLAB)
            kd = _stack_heads(k_ref[pl.ds(k_start, ATT_SLAB), :], first)
            g = g_s[c, slab]
            z2 = lax.dot_general(qs[c], kd, (((1,), (1,)), ((), ())), preferred_element_type=F32) * LOG2_E
            sig = 1.0 / (1.0 + jnp.exp2(-z2))
            prefix, (left_a, left_b) = _scan_slab(g, prefix_tri, (left_a, left_b), from_right=False)
            dz = g * (1.0 - sig) - sig * prefix
            if on_diagonal:
                dz = jnp.where(_causal_mask(t_pos[c], k_start), dz, 0.0)
            dz = dz.astype(BF16)
            dq = dq + jnp.dot(dz, kd, preferred_element_type=F32)
            dk_acc[pl.ds(k_start, ATT_SLAB), :] += _fold_heads(lax.dot_general(
                dz, qs[c], (((0,), (0,)), ((), ())), preferred_element_type=F32), first)
            return dq, left_a, left_b

        carries = [(zero, zero, zero)]
        for c in range(1, ATT_CHAINS):
            carries.append(lax.fori_loop(
                0, done[c] - joint, lambda n, carry, c=c: one2(c, done[c] - 1 - n, carry, False), (zero, zero, zero)))
        carries = lax.fori_loop(
            0, joint - 1,
            lambda n, cs: tuple(one2(c, joint - 1 - n, cs[c], False) for c in range(ATT_CHAINS)), tuple(carries))
        for c in range(ATT_CHAINS):
            dq_ref[rows[c], :] = (one2(c, 0, carries[c], True)[0] * ATT_SCALE).astype(BF16)

        @pl.when(ii == n_steps - 1)
        def _():
            dk_ref[...] = dk_acc[...].astype(BF16)
            dv_ref[...] = dv_acc[...].astype(BF16)

        bottom()

    out = jax.ShapeDtypeStruct((t_total, n_pairs * LANES), BF16)
    n_rows = ATT_CHAINS * blk
    whole = dict(pipeline_mode=pl.Buffered(1))
    res = pl.pallas_call(
        body, name="attn_bwd", grid=(n_pairs, n_steps),
        in_specs=[pl.BlockSpec((n_rows, LANES), lambda h, i: (i, q_col + h)),
                  pl.BlockSpec((n_rows, LANES), lambda h, i: (i, h)),
                  pl.BlockSpec((t_total, LANES), lambda h, i: (0, k_col + h), **whole),
                  pl.BlockSpec((t_total, LANES), lambda h, i: (0, v_col + h), **whole), _tri_spec(), _tri_spec()]
        + [ANY] * len(rider.operands),
        out_specs=[pl.BlockSpec((n_rows, LANES), lambda h, i: (i, h)),
                   pl.BlockSpec((t_total, LANES), lambda h, i: (0, h)),
                   pl.BlockSpec((t_total, LANES), lambda h, i: (0, h))] + [ANY] * len(rider.out_shapes),
        out_shape=[out, out, out] + list(rider.out_shapes),
        scratch_shapes=list(rider.scratch) + [pltpu.VMEM((ATT_CHAINS, n_slabs, blk, 2 * ATT_SLAB), BF16),
                                              pltpu.VMEM((t_total, LANES), F32), pltpu.VMEM((t_total, LANES), F32)],
        compiler_params=_params("arbitrary", "arbitrary"),
    )(q_src, dy, kv_src, kv_src, _tri(upper=True), _tri(upper=False), *rider.operands)
    return res[:3], res[3:]


def _adamw(name, w, g, m, v):
    def fn(wv, gv, mv, vv):
        mn = ADAM_B1 * mv + (1.0 - ADAM_B1) * gv
        vn = ADAM_B2 * vv + (1.0 - ADAM_B2) * (gv * gv)
        m_hat = mn / (1.0 - ADAM_B1 ** ADAM_STEP)
        v_hat = vn / (1.0 - ADAM_B2 ** ADAM_STEP)
        return -ADAM_LR * (m_hat / (jnp.sqrt(v_hat) + ADAM_EPS) + ADAM_WD * wv), mn, vn

    rows = w.shape[0]
    tr = _row_tile(rows)
    shp = jax.ShapeDtypeStruct(w.shape, F32)
    if rows == 1:
        def body(w_ref, g_ref, m_ref, v_ref, d_ref, mo_ref, vo_ref):
            d, mn, vn = fn(w_ref[...], g_ref[...], m_ref[...], v_ref[...])
            d_ref[...], mo_ref[...], vo_ref[...] = d, mn, vn

        return pl.pallas_call(body, name=name, out_shape=[shp, shp, shp])(w, g, m, v)
    return _rows(name, fn, [w, g, m, v], [shp, shp, shp], tr=tr)


def _place():
    return lax.axis_index("x"), lax.axis_index("y"), lax.axis_index("c")


def _other_chips(x, y):
    return [(1 - x, y), (x, 1 - y), (1 - x, 1 - y)]


ANY = pl.BlockSpec(memory_space=pl.ANY)


def _remote(src, dst, send_sem, recv_sem, to):
    return pltpu.make_async_remote_copy(src_ref=src, dst_ref=dst, send_sem=send_sem, recv_sem=recv_sem,
                                        device_id=to, device_id_type=MESH)


class _WeightGather(_NoRider):
    def __init__(self, shards):
        n_w = len(shards)
        self.operands = list(shards)
        self.out_shapes = [jax.ShapeDtypeStruct((N_CHIPS,) + s.shape, s.dtype) for s in shards]
        self.scratch = [pltpu.SemaphoreType.DMA((3, n_w))] * 4 + [pltpu.SemaphoreType.DMA((n_w,))] * 2

    def _copies(self, ins, outs, sems):
        send_sems, recv_sems, relay_send, relay_recv, own_send, own_recv = sems
        x, y, c = _place()
        my_chip, sibling = 2 * x + y, (x, y, 1 - c)
        n_w = len(ins)

        def half(w, chip, core):
            h = self.operands[w].shape[0] // 2
            return outs[w].at[chip, pl.ds(core * h, h)]

        own = [_remote(ins[w], outs[w].at[my_chip], own_send.at[w], own_recv.at[w], sibling) for w in range(n_w)]
        sends, landed, relays, relayed = [], [], [], []
        for p, (ox, oy) in enumerate(_other_chips(x, y)):
            for w in range(n_w):
                h = self.operands[w].shape[0] // 2
                sends.append(_remote(ins[w].at[pl.ds(c * h, h)], half(w, my_chip, c), send_sems.at[p, w],
                                     recv_sems.at[p, w], (ox, oy, c)))
                here = half(w, 2 * ox + oy, c)
                landed.append(_remote(here, here, send_sems.at[p, w], recv_sems.at[p, w], (ox, oy, c)))
                relays.append(_remote(here, here, relay_send.at[p, w], relay_recv.at[p, w], sibling))
                there = half(w, 2 * ox + oy, 1 - c)
                relayed.append(_remote(there, there, relay_send.at[p, w], relay_recv.at[p, w], sibling))
        return own, sends, landed, relays, relayed

    def start(self, ins, outs, sems):
        own, sends, _, _, _ = self._copies(ins, outs, sems)
        for cp in own + sends:
            cp.start()

    def relay(self, ins, outs, sems):
        _, _, landed, relays, _ = self._copies(ins, outs, sems)
        for arrival, cp in zip(landed, relays):
            arrival.wait_recv()
            cp.start()

    def finish(self, ins, outs, sems):
        own, sends, _, relays, relayed = self._copies(ins, outs, sems)
        for arrival in relayed:
            arrival.wait_recv()
        for cp in sends + relays:
            cp.wait_send()
        for cp in own:
            cp.wait()


class _ChipExchange(_NoRider):
    def __init__(self, pair_sums):
        n_w = len(pair_sums)
        self.operands = list(pair_sums)
        self.out_shapes = [jax.ShapeDtypeStruct((3,) + s.shape[1:], s.dtype) for s in pair_sums]
        self.scratch = [pltpu.SemaphoreType.DMA((3, n_w))] * 2

    def _copies(self, ins, outs, sems):
        send_sems, recv_sems = sems
        x, y, c = _place()
        return [_remote(ins[w].at[2 * ox + oy], outs[w].at[p], send_sems.at[p, w], recv_sems.at[p, w], (ox, oy, c))
                for p, (ox, oy) in enumerate(_other_chips(x, y)) for w in range(len(ins))]

    def start(self, ins, outs, sems):
        for cp in self._copies(ins, outs, sems):
            cp.start()

    def finish(self, ins, outs, sems):
        for cp in self._copies(ins, outs, sems):
            cp.wait()


class _PairExchange(_NoRider):
    def __init__(self, grads):
        n_w = len(grads)
        self.operands = list(grads)
        self.out_shapes = [jax.ShapeDtypeStruct(g.shape[:-2] + (g.shape[-2] // 2, g.shape[-1]), F32) for g in grads]
        self.scratch = [pltpu.SemaphoreType.DMA((n_w,))] * 2

    def _copies(self, ins, theirs, sems):
        send_sems, recv_sems = sems
        x, y, c = _place()
        sends = []
        for w, g in enumerate(self.operands):
            rows = pl.ds((1 - c) * (g.shape[-2] // 2), g.shape[-2] // 2)
            src = ins[w].at[:, rows, :] if g.ndim == 3 else ins[w].at[rows, :]
            sends.append(_remote(src, theirs[w], send_sems.at[w], recv_sems.at[w], (x, y, 1 - c)))
        return sends

    def start(self, ins, outs, sems):
        for cp in self._copies(ins, outs, sems):
            cp.start()

    def finish(self, ins, outs, sems):
        for cp in self._copies(ins, outs, sems):
            cp.wait()


def _run_exchange(name, plan):
    n_in, n_out = len(plan.operands), len(plan.out_shapes)
    if not n_in:
        return ()

    def body(*refs):
        parts = (refs[:n_in], refs[n_in:n_in + n_out], refs[n_in + n_out:])
        plan.start(*parts)
        plan.relay(*parts)
        plan.finish(*parts)

    return pl.pallas_call(body, name=name, in_specs=[ANY] * n_in, out_specs=[ANY] * n_out,
                          out_shape=list(plan.out_shapes), scratch_shapes=list(plan.scratch))(*plan.operands)


class _NoExchanges:
    pair_sums, landed = {}, {}

    def gather(self, names):
        return _NoRider()

    def pair(self, names, grads):
        return _NoRider()

    def paired(self, names, grads, theirs):
        pass

    def chip(self, names):
        return _NoRider()


class _StepExchanges(_NoExchanges):
    def __init__(self, shards_bf16, place):
        self.shards, self.place = shards_bf16, place
        self.pair_sums, self.landed = {}, {}

    def gather(self, names):
        return _WeightGather([self.shards[n] for n in names])

    def pair(self, names, grads):
        return _PairExchange([grads[n] for n in names])

    def paired(self, names, grads, theirs):
        for n, other in zip(names, theirs):
            self.pair_sums[n] = _pair_sum(f"pair_sum_{n}", self.place, grads[n], other)

    def chip(self, names):
        return _ChipExchange([self.pair_sums[n] for n in names])


def _finish_gradients(shards, vec):
    n_w = len(shards)
    rows = vec.shape[0]

    def body(*refs):
        outs, v_ref, o_ref = refs[n_w + 1:2 * n_w + 1], refs[n_w], refs[2 * n_w + 1]
        half_send, half_recv, slots, vec_send, vec_recv = refs[2 * n_w + 2:]
        x, y, c = _place()
        me, sibling = 4 * x + 2 * y + c, (x, y, 1 - c)
        halves = []
        for w in range(n_w):
            h = shards[w].shape[0] // 2
            mine, theirs = outs[w].at[pl.ds(c * h, h)], outs[w].at[pl.ds((1 - c) * h, h)]
            halves.append((_remote(mine, mine, half_send.at[w], half_recv.at[w], sibling),
                           _remote(theirs, theirs, half_send.at[w], half_recv.at[w], sibling)))
        slots[me] = v_ref[...]
        spread = []
        for k in range(1, N_DEV):
            peer = (x ^ (k >> 2), y ^ ((k >> 1) & 1), c ^ (k & 1))
            landed = slots.at[4 * peer[0] + 2 * peer[1] + peer[2]]
            spread.append((_remote(v_ref, slots.at[me], vec_send.at[k - 1], vec_recv.at[k - 1], peer),
                           _remote(landed, landed, vec_send.at[k - 1], vec_recv.at[k - 1], peer)))
        for send, _ in halves + spread:
            send.start()
        for send, arrival in spread:
            arrival.wait_recv()
            send.wait_send()
        total = slots[0]
        for d in range(1, N_DEV):
            total = total + slots[d]
        o_ref[...] = total
        for send, arrival in halves:
            arrival.wait_recv()
            send.wait_send()

    vm = pl.BlockSpec(memory_space=pltpu.VMEM)
    res = pl.pallas_call(
        body, name="finish_gradients", in_specs=[ANY] * n_w + [vm], out_specs=[ANY] * n_w + [vm],
        out_shape=[jax.ShapeDtypeStruct(s.shape, s.dtype) for s in shards] + [jax.ShapeDtypeStruct(vec.shape, F32)],
        input_output_aliases={w: w for w in range(n_w)},
        scratch_shapes=[pltpu.SemaphoreType.DMA((n_w,)), pltpu.SemaphoreType.DMA((n_w,)),
                        pltpu.VMEM((N_DEV, rows, LANES), F32), pltpu.SemaphoreType.DMA((N_DEV - 1,)),
                        pltpu.SemaphoreType.DMA((N_DEV - 1,))],
    )(*shards, vec)
    return res[:n_w], res[n_w]


def _row_tile(rows):
    fits = [tr for tr in range(16, min(rows, 512) + 1, 16) if rows % tr == 0]
    return max(fits) if fits else rows


def _pair_sum(name, place, grad, theirs):
    if grad.ndim == 2:
        return _pair_sum_joined(name, place, grad, theirs)
    n, r, c = grad.shape
    half = r // 2
    tr = _row_tile(half)
    nb = half // tr

    def body(place_ref, g_ref, t_ref, o_ref):
        o_ref[...] = (g_ref[...] + t_ref[...]).astype(BF16)

    return pl.pallas_call(
        body, name=name, out_shape=jax.ShapeDtypeStruct((n, half, c), BF16),
        grid_spec=pltpu.PrefetchScalarGridSpec(
            num_scalar_prefetch=1, grid=(n, nb),
            in_specs=[pl.BlockSpec((1, tr, c), lambda j, i, pr: (j, pr[0] * nb + i, 0)),
                      pl.BlockSpec((1, tr, c), lambda j, i, pr: (j, i, 0))],
            out_specs=pl.BlockSpec((1, tr, c), lambda j, i, pr: (j, i, 0))),
        compiler_params=_params("parallel", "parallel"),
    )(place, grad, theirs)


def _pair_sum_joined(name, place, grad, theirs):
    r, wide = grad.shape
    half, c = r // 2, wide // N_CHIPS
    tr = _row_tile(half)
    nb = half // tr

    def body(place_ref, g_ref, t_ref, o_ref):
        for j in range(N_CHIPS):
            cols = slice(j * c, (j + 1) * c)
            o_ref[j] = (g_ref[:, cols] + t_ref[:, cols]).astype(BF16)

    return pl.pallas_call(
        body, name=name, out_shape=jax.ShapeDtypeStruct((N_CHIPS, half, c), BF16),
        grid_spec=pltpu.PrefetchScalarGridSpec(
            num_scalar_prefetch=1, grid=(nb,),
            in_specs=[pl.BlockSpec((tr, wide), lambda i, pr: (pr[0] * nb + i, 0)),
                      pl.BlockSpec((tr, wide), lambda i, pr: (i, 0))],
            out_specs=pl.BlockSpec((N_CHIPS, tr, c), lambda i, pr: (0, i, 0))),
        compiler_params=_params("parallel"),
    )(place, grad, theirs)


def _sum_chips(name, place, pair_sums, landed):
    _, half, c = pair_sums.shape
    tr = _row_tile(half)
    nb = half // tr

    def body(place_ref, s_ref, q_ref, o_ref):
        total = s_ref[0].astype(F32)
        for p in range(3):
            total = total + q_ref[p].astype(F32)
        o_ref[...] = total

    return pl.pallas_call(
        body, name=name, out_shape=jax.ShapeDtypeStruct((2 * half, c), F32),
        grid_spec=pltpu.PrefetchScalarGridSpec(
            num_scalar_prefetch=1, grid=(nb,),
            in_specs=[pl.BlockSpec((1, tr, c), lambda i, pr: (pr[1], i, 0)),
                      pl.BlockSpec((3, tr, c), lambda i, pr: (0, i, 0))],
            out_specs=pl.BlockSpec((tr, c), lambda i, pr: (pr[0] * nb + i, 0))),
        compiler_params=_params("parallel"),
    )(place, pair_sums, landed)


MIXER = ("w_branch_a", "w_branch_b", "w_out")
FFN_PLE = ("w_ffn_gate", "w_ffn_up", "w_ffn_down", "w_ple_gate", "w_ple_proj")
LATE = MIXER + FFN_PLE
BIG = ("w_in",) + LATE
HELD_TRANSPOSED = ("w_ffn_gate", "w_ffn_up")
SMALL = ("norm_mix", "w_pool", "pool_scale", "norm_ffn", "norm_ple", "norm_final")


def _join_columns(w4):
    return jnp.concatenate([w4[j] for j in range(N_CHIPS)], axis=1)


def _sds(shape, dtype):
    return jax.ShapeDtypeStruct(shape, dtype)


def _local_step(x, p, target, wf, small, ex=None):
    t, d = x.shape
    w_pool_b = small["w_pool"].astype(BF16)
    dp = w_pool_b.shape[0] * w_pool_b.shape[1]

    ex = ex or _NoExchanges()
    h1, first = _norm_fwd("norm_mix", x, small["norm_mix"], rider=ex.gather(("w_in",)))
    wf = {**wf, **dict(zip(("w_in",), first))}
    w_in = wf["w_in"]
    u, q, kv, ga, gb = _mm(
        "proj", [h1], [w_in[j] for j in range(N_CHIPS)], "nn",
        [_sds((t, dp), F32), _sds((t, dp), BF16), _sds((t, d), BF16), _sds((t, d), BF16), _sds((t, d), BF16)],
        separate=True, epilogue=lambda uq, kv_, ga_, gb_: (uq[:, :dp], uq[:, dp:], kv_, ga_, gb_), tm=512)
    pooled, ya = _pool_fwd(u, w_pool_b, small["pool_scale"])
    n_pairs = dp // LANES
    yb, late = _attn_fwd(q, 0, kv, 0, n_pairs, n_pairs, rider=ex.gather(LATE))
    wf = {**wf, **dict(zip(LATE, late))}
    w_down = wf["w_ffn_down"].reshape(-1, d)
    dff = w_down.shape[0]
    w_gate_t, w_up_t = wf["w_ffn_gate"].reshape(dff, d), wf["w_ffn_up"].reshape(dff, d)
    w_a, w_b, w_pp = _join_columns(wf["w_branch_a"]), _join_columns(wf["w_branch_b"]), _join_columns(wf["w_ple_proj"])
    w_out = wf["w_out"].reshape(d, d)
    w_pg = wf["w_ple_gate"].reshape(d, d)
    def residual_norm(branch, xv, g, w):
        xn = xv + jnp.dot(branch.astype(BF16), w, preferred_element_type=F32)
        return xn, xn * lax.rsqrt(jnp.mean(xn * xn, axis=-1, keepdims=True) + RMS_EPS) * g

    def mixer_tail(tav, tbv, gav, gbv, xv, g, w):
        merged = _sigmoid(gav) * tav + _sigmoid(gbv) * tbv
        return (tav, tbv, merged) + residual_norm(merged, xv, g, w)

    def ffn_tail(gv, uv, xv, g, w):
        act = gv * _sigmoid(gv) * uv
        return (gv, uv, act) + residual_norm(act, xv, g, w)

    stream = [_sds((t, d), F32), _sds((t, d), BF16)]
    ta, tb, merged, x1, h2 = _mm(
        "mixer_out", [ya, yb], [w_a, w_b], "nn", [_sds((t, d), BF16)] * 3 + stream,
        extras=[ga, gb, x, small["norm_ffn"]], wholes=[w_out], separate=True, epilogue=mixer_tail, tm=512)
    gate, up, act, x2, h3 = _mm(
        "ffn", [h2], [w_gate_t, w_up_t], "nt", [_sds((t, dff), BF16)] * 3 + stream,
        extras=[x1, small["norm_ple"]], wholes=[w_down], separate=True, epilogue=ffn_tail, tm=256)
    dx2, dx2_b, d_pp, d_gp, d_norm_final, loss_row, d_norm_ple = _mm(
        "ple_loss", [h3, p], [w_pg, w_pp], "nn", stream + [_sds((t, d), BF16)] * 2,
        extras=[x2, target, small["norm_final"].reshape(1, d), small["norm_ple"]], wholes=[w_pg], separate=True,
        epilogue=_ple_and_loss, sum_shapes=[_sds((1, d), F32)] * 3, tm=512)

    def through_norm(dh, xv, g, dres):
        dx, d_gain = _rms_norm_bwd(dh, xv, g)
        return dx + dres, dx + dres, d_gain

    gain_sum = [_sds((1, d), F32)]
    g_w_pp, g_w_pg = _mm_tn("g_ple", [p, h3], [d_pp, d_gp])

    def ffn_bwd(d_act, gv, uv, xv, g, dres, wg_t, wu_t):
        s = _sigmoid(gv)
        d_gate, d_up = d_act * uv * (s * (1.0 + gv * (1.0 - s))), d_act * (gv * s)
        dh2 = (jnp.dot(d_gate.astype(BF16), wg_t, preferred_element_type=F32)
               + jnp.dot(d_up.astype(BF16), wu_t, preferred_element_type=F32))
        return (d_gate, d_up) + through_norm(dh2, xv, g, dres)

    d_gate, d_up, dx1, dx1_b, d_norm_ffn = _mm(
        "ffn_bwd", [dx2_b], [w_down], "nt", [_sds((t, dff), BF16)] * 2 + stream,
        extras=[gate, up, x1, small["norm_ffn"], dx2], wholes=[w_gate_t, w_up_t], epilogue=ffn_bwd,
        sum_shapes=gain_sum, tm=256)
    g_w_down, = _mm_tn("g_ffn_down", [act], [dx2_b], tmm=512)
    g_w_gate_t, g_w_up_t = _mm_tn("g_ffn_gate_up", [d_gate, d_up], [h2], k_blocks=2)

    def merge_bwd(acc, tav, tbv, gav, gbv):
        sa, sb = _sigmoid(gav), _sigmoid(gbv)
        return acc * sa, acc * sb, acc * tav * sa * (1.0 - sa), acc * tbv * sb * (1.0 - sb)

    big = {
        "w_ffn_gate": g_w_gate_t.reshape(wf["w_ffn_gate"].shape), "w_ffn_up": g_w_up_t.reshape(wf["w_ffn_up"].shape),
        "w_ffn_down": g_w_down.reshape(wf["w_ffn_down"].shape),
        "w_ple_gate": g_w_pg.reshape(wf["w_ple_gate"].shape), "w_ple_proj": g_w_pp,
    }
    (d_ta, d_tb, d_ga, d_gb), theirs = _mm(
        "d_merged", [dx1_b], [w_out], "nt", [_sds((t, d), BF16)] * 4, extras=[ta, tb, ga, gb], epilogue=merge_bwd,
        tm=512, rider=ex.pair(FFN_PLE, big))
    ex.paired(FFN_PLE, big, theirs)
    g_w_out, big["w_branch_a"], big["w_branch_b"] = _mm_tn("g_mixer", [merged, ya, yb], [dx1_b, d_ta, d_tb])
    big["w_out"] = g_w_out.reshape(wf["w_out"].shape)
    (d_ya, d_yb), theirs = _mm(
        "d_branches", [d_ta, d_tb], [w_a, w_b], "nt", [_sds((t, dp), F32), _sds((t, dp), BF16)], separate=True,
        rider=ex.pair(MIXER, big))
    ex.paired(MIXER, big, theirs)
    d_u, g_w_pool, d_pool_scale = _pool_bwd(d_ya, pooled, w_pool_b, small["pool_scale"])
    (d_q, d_k, d_v), landed = _attn_bwd(q, 0, kv, 0, n_pairs, d_yb, n_pairs, rider=ex.chip(LATE))
    ex.landed.update(zip(LATE, landed))
    d_proj = [(d_u, d_q), (d_k, d_v), d_ga, d_gb]
    big["w_in"], = _mm_tn("g_w_in", [h1], d_proj, tmm=512, stacked=True)
    ex.paired(("w_in",), big, _run_exchange("pair_exchange_w_in", ex.pair(("w_in",), big)))
    (grad_x, d_norm_mix), landed = _mm(
        "d_h1", d_proj, [w_in[j] for j in range(N_CHIPS)], "nt", [_sds((t, d), F32)],
        extras=[x, small["norm_mix"], dx1], epilogue=lambda dh, xv, g, dres: through_norm(dh, xv, g, dres)[1:],
        sum_shapes=gain_sum, tm=512, rider=ex.chip(("w_in",)))
    ex.landed.update(zip(("w_in",), landed))
    small_g = {"norm_mix": d_norm_mix, "w_pool": g_w_pool, "pool_scale": d_pool_scale, "norm_ffn": d_norm_ffn,
               "norm_ple": d_norm_ple, "norm_final": d_norm_final}
    return grad_x, big, small_g, loss_row


def _pack_small(small_g, loss_row):
    parts, layout = [], []
    for name in SMALL + ("loss",):
        v = (loss_row if name == "loss" else small_g[name]).reshape(-1, LANES)
        pad = (-v.shape[0]) % 8
        if pad:
            v = jnp.concatenate([v, jnp.zeros((pad, LANES), F32)], axis=0)
        layout.append((name, sum(q.shape[0] for q in parts), v.shape[0]))
        parts.append(v)
    return jnp.concatenate(parts, axis=0), layout


def kernel(x, p, norm_mix, w_in, w_pool, pool_scale, w_branch_a, w_branch_b, w_out, norm_ffn, w_ffn_gate, w_ffn_up, w_ffn_down, norm_ple, w_ple_gate, w_ple_proj, norm_final, loss_target, m_norm_mix, m_w_in, m_w_pool, m_pool_scale, m_w_branch_a, m_w_branch_b, m_w_out, m_norm_ffn, m_w_ffn_gate, m_w_ffn_up, m_w_ffn_down, m_norm_ple, m_w_ple_gate, m_w_ple_proj, m_norm_final, v_norm_mix, v_w_in, v_w_pool, v_pool_scale, v_w_branch_a, v_w_branch_b, v_w_out, v_norm_ffn, v_w_ffn_gate, v_w_ffn_up, v_w_ffn_down, v_norm_ple, v_w_ple_gate, v_w_ple_proj, v_norm_final):
    given = dict(locals())
    order = ("norm_mix", "w_in", "w_pool", "pool_scale", "w_branch_a", "w_branch_b", "w_out", "norm_ffn", "w_ffn_gate",
             "w_ffn_up", "w_ffn_down", "norm_ple", "w_ple_gate", "w_ple_proj", "norm_final")
    t, d = x.shape[1], x.shape[2]
    def local(a, n):
        return jnp.swapaxes(a[0], 0, 1) if n in HELD_TRANSPOSED else a[0]

    def back(a, n):
        return (jnp.swapaxes(a, 0, 1) if n in HELD_TRANSPOSED else a)[None]

    shard = {n: local(given[n], n) for n in BIG}
    small = {"norm_mix": norm_mix, "w_pool": w_pool[0], "pool_scale": pool_scale, "norm_ffn": norm_ffn,
             "norm_ple": norm_ple, "norm_final": norm_final}

    place = jnp.stack([lax.axis_index("c"), 2 * lax.axis_index("x") + lax.axis_index("y")]).astype(jnp.int32)
    ex = _StepExchanges({n: shard[n].astype(BF16) for n in BIG}, place)
    grad_x, _, small_g, loss_row = _local_step(
        x.reshape(t, d), p.reshape(t, p.shape[-1]), loss_target.reshape(t, d), {}, small, ex)
    halves = [_sum_chips(f"chip_sum_{n}", place, ex.pair_sums[n], ex.landed[n]) for n in BIG]
    packed, layout = _pack_small(small_g, loss_row)
    filled, reduced = _finish_gradients(halves, packed)
    grads = dict(zip(BIG, filled))
    for name, start, rows in layout:
        if name == "loss":
            loss = jnp.sum(reduced[start:start + rows])
        else:
            n_el = small[name].size
            grads[name] = reduced[start:start + rows].reshape(-1)[:n_el]

    deltas, new_m, new_v = {}, {}, {}
    for n in order:
        if n in BIG:
            w, m, v = shard[n], local(given["m_" + n], n), local(given["v_" + n], n)
            dl, mn, vn = _adamw(f"adamw_{n}", w, grads[n], m, v)
            grads[n], deltas[n], new_m[n], new_v[n] = [back(a, n) for a in (grads[n], dl, mn, vn)]
        else:
            w, full = small[n], given[n].shape
            shape2 = (1, w.shape[0]) if w.ndim == 1 else (w.shape if w.ndim == 2 else (w.shape[0] * w.shape[1], w.shape[2]))
            dl, mn, vn = _adamw(f"adamw_{n}", w.reshape(shape2), grads[n].reshape(shape2),
                                given["m_" + n].reshape(shape2), given["v_" + n].reshape(shape2))
            grads[n], deltas[n], new_m[n], new_v[n] = [a.reshape(full) for a in (grads[n], dl, mn, vn)]

    return (loss, grad_x.reshape(x.shape), *[grads[n] for n in order], *[deltas[n] for n in order],
            *[new_m[n] for n in order], *[new_v[n] for n in order])
```

```python
import functools
import math

import jax
import jax.numpy as jnp
from jax import lax
from jax.experimental import pallas as pl
from jax.experimental.pallas import tpu as pltpu

F32 = jnp.float32
BF16 = jnp.bfloat16
MESH = pl.DeviceIdType.MESH

RMS_EPS = 1e-6
POOL_WINDOWS = (2, 4, 8, 16)
POOL_HALO = 16
HEAD_DIM = 64
LANES = 128
ATT_BLOCK = 256
ATT_CHAINS = 2
ATT_FWD_CHAINS = 4
ATT_CHUNK = 256
ATT_SLAB = 256
ATT_SCALE = 1.0 / math.sqrt(HEAD_DIM)
LOG2_E = 1.4426950408889634
ATT_EXIT_BELOW = -150.5
ADAM_LR, ADAM_B1, ADAM_B2, ADAM_EPS, ADAM_WD, ADAM_STEP = 0.001, 0.9, 0.999, 1e-08, 0.01, 10
V7X_VMEM_LIMIT_BYTES = 56 * 1024 * 1024
N_CHIPS = 4
N_DEV = 8


def _params(*semantics):
    return pltpu.CompilerParams(dimension_semantics=semantics, vmem_limit_bytes=V7X_VMEM_LIMIT_BYTES)


def _sigmoid(z):
    return 1.0 / (1.0 + jnp.exp(-z))


def _tiled_spec(shape, tm, tn, n_total, at):
    rows, width = shape
    if rows == 1:
        if width == n_total:
            return pl.BlockSpec((1, tn), at(lambda i, j: (0, j)))
        return pl.BlockSpec((1, width), at(lambda i, j: (0, 0)))
    if width == n_total:
        return pl.BlockSpec((tm, tn), at(lambda i, j: (i, j)))
    assert tn == n_total, "an operand narrower than the output needs whole output rows per tile"
    return pl.BlockSpec((tm, width), at(lambda i, j: (i, 0)))


def _column_pieces(operands):
    pieces = [tuple(a) if isinstance(a, (tuple, list)) else (a,) for a in operands]
    return [p for ps in pieces for p in ps], [len(ps) for ps in pieces]


def _load_bf16(refs, counts):
    tiles, k = [], 0
    for n in counts:
        parts = [r[...] for r in refs[k:k + n]]
        parts = [t if t.dtype == BF16 else t.astype(BF16) for t in parts]
        tiles.append(parts[0] if n == 1 else jnp.concatenate(parts, axis=1))
        k += n
    return tiles


def _mm(name, a_list, b_list, mode, out_shapes, epilogue=None, extras=(), tm=1024, tn=None, separate=False,
        sum_shapes=(), rider=None, wholes=()):
    flat_a, counts = _column_pieces(a_list)
    m_total = flat_a[0].shape[0]
    n_total = b_list[0].shape[1] if mode == "nn" else b_list[0].shape[0]
    tn = n_total if tn is None else tn
    tm = min(tm, m_total)
    assert m_total % tm == 0 and n_total % tn == 0 and (not sum_shapes or tn == n_total)
    n_a, n_b, n_extra, n_out = len(counts), len(b_list), len(extras), len(out_shapes)
    assert n_a in (1, n_b)
    dims = (((1,), (0,)), ((), ())) if mode == "nn" else (((1,), (1,)), ((), ()))
    with_rider = rider is not None
    rider = rider or _NoRider()
    grid = (n_total // tn, m_total // tm)

    def at(index):
        return lambda j, i: index(i, j)

    def body(*refs):
        ins, o_refs, _, riding = rider.split(refs, len(flat_a) + n_b + n_extra + len(wholes), n_out + len(sum_shapes))
        a_refs, b_refs = ins[:len(flat_a)], ins[len(flat_a):len(flat_a) + n_b]
        e_refs, w_refs = ins[len(flat_a) + n_b:len(flat_a) + n_b + n_extra], ins[len(flat_a) + n_b + n_extra:]
        at_first = (pl.program_id(0) == 0) & (pl.program_id(1) == 0)
        at_last = (pl.program_id(0) == grid[0] - 1) & (pl.program_id(1) == grid[1] - 1)
        top, bottom = rider.at_steps(riding, at_first, at_first, at_last)
        top()
        lefts = _load_bf16(a_refs, counts)
        products = [lax.dot_general(lefts[s % n_a], b_refs[s][...], dims, preferred_element_type=F32)
                    for s in range(n_b)]
        if not separate:
            products = [functools.reduce(lambda p, r: p + r, products)]
        extra_tiles = [e[...].astype(F32) for e in e_refs]
        outs = products if epilogue is None else epilogue(*products, *extra_tiles, *[w[...] for w in w_refs])
        for o_ref, o in zip(o_refs[:n_out], outs[:n_out]):
            o_ref[...] = o.astype(o_ref.dtype)
        if sum_shapes:
            @pl.when(pl.program_id(1) == 0)
            def _():
                for s_ref in o_refs[n_out:]:
                    s_ref[...] = jnp.zeros_like(s_ref)

            for s_ref, s in zip(o_refs[n_out:], outs[n_out:]):
                s_ref[...] += s
        bottom()

    once = dict(pipeline_mode=pl.Buffered(1)) if tn == n_total else {}
    in_specs = [pl.BlockSpec((tm, a.shape[1]), at(lambda i, j: (i, 0))) for a in flat_a]
    if mode == "nn":
        in_specs += [pl.BlockSpec((b.shape[0], tn), at(lambda i, j: (0, j)), **once) for b in b_list]
    else:
        in_specs += [pl.BlockSpec((tn, b.shape[1]), at(lambda i, j: (j, 0)), **once) for b in b_list]
    in_specs += [_tiled_spec(e.shape, tm, tn, n_total, at) for e in extras]
    in_specs += [pl.BlockSpec(w.shape, lambda j, i: (0, 0), pipeline_mode=pl.Buffered(1)) for w in wholes]
    out_specs = [_tiled_spec(o.shape, tm, tn, n_total, at) for o in out_shapes]
    out_specs += [pl.BlockSpec(s.shape, at(lambda i, j: (0, 0))) for s in sum_shapes]
    semantics = ("arbitrary", "arbitrary") if sum_shapes or rider.operands else ("parallel", "parallel")
    res = pl.pallas_call(
        body, name=name, grid=grid, in_specs=in_specs + [ANY] * len(rider.operands),
        out_specs=out_specs + [ANY] * len(rider.out_shapes),
        out_shape=list(out_shapes) + list(sum_shapes) + list(rider.out_shapes), scratch_shapes=list(rider.scratch),
        compiler_params=_params(*semantics),
    )(*flat_a, *b_list, *extras, *wholes, *rider.operands)
    n_own = len(out_shapes) + len(sum_shapes)
    return (res[:n_own], res[n_own:]) if with_rider else res


def _mm_tn(name, a_list, b_list, tmm=1024, stacked=False, k_blocks=1):
    flat_b, counts = _column_pieces(b_list)
    n_a, n_b = len(a_list), len(counts)
    n_prod = max(n_a, n_b)
    m_total = a_list[0].shape[0]
    ks = [a_list[s % n_a].shape[1] for s in range(n_prod)]
    widths = [sum(p.shape[1] for p in flat_b[sum(counts[:s]):sum(counts[:s + 1])]) for s in range(n_b)]
    widths = [widths[s % n_b] for s in range(n_prod)]
    tmm = min(tmm, m_total)
    assert m_total % tmm == 0 and all(k % k_blocks == 0 for k in ks)
    assert n_a in (1, n_prod) and n_b in (1, n_prod) and not (stacked and n_a > 1)

    def body(*refs):
        a_refs, b_refs, o_refs = refs[:n_a], refs[n_a:n_a + len(flat_b)], refs[n_a + len(flat_b):]

        @pl.when(pl.program_id(1) == 0)
        def _():
            for o_ref in o_refs:
                o_ref[...] = jnp.zeros_like(o_ref)

        lefts, rights = _load_bf16(a_refs, [1] * n_a), _load_bf16(b_refs, counts)
        for s in range(n_prod):
            product = lax.dot_general(lefts[s % n_a], rights[s % n_b], (((0,), (0,)), ((), ())),
                                      preferred_element_type=F32)
            if stacked:
                o_refs[0][s] += product
            else:
                o_refs[s][...] += product

    in_specs = [pl.BlockSpec((tmm, a.shape[1] // k_blocks), lambda kb, m: (m, kb)) for a in a_list]
    in_specs += [pl.BlockSpec((tmm, b.shape[1]), lambda kb, m: (m, 0)) for b in flat_b]
    if stacked:
        out_shape = [jax.ShapeDtypeStruct((n_prod, ks[0], widths[0]), F32)]
        out_specs = [pl.BlockSpec((n_prod, ks[0] // k_blocks, widths[0]), lambda kb, m: (0, kb, 0))]
    else:
        out_shape = [jax.ShapeDtypeStruct((k, w), F32) for k, w in zip(ks, widths)]
        out_specs = [pl.BlockSpec((k // k_blocks, w), lambda kb, m: (kb, 0)) for k, w in zip(ks, widths)]
    return pl.pallas_call(
        body, name=name, grid=(k_blocks, m_total // tmm), in_specs=in_specs, out_specs=out_specs, out_shape=out_shape,
        compiler_params=_params("arbitrary", "arbitrary"),
    )(*a_list, *flat_b)


def _rows(name, fn, ins, tile_outs, sum_outs=(), tr=512, rider=None):
    t_total = max(a.shape[0] for a in ins)
    tr = min(tr, t_total)
    assert t_total % tr == 0
    n_in, n_tile = len(ins), len(tile_outs)
    rider = rider or _NoRider()
    n_steps = t_total // tr

    def body(*refs):
        own_ins, own_outs, _, riding = rider.split(refs, n_in, n_tile + len(sum_outs))
        step = pl.program_id(0)
        top, bottom = rider.at_steps(riding, step == 0, step == n_steps - 1, step == n_steps - 1)
        top()
        refs = tuple(own_ins) + tuple(own_outs)
        outs = fn(*[r[...].astype(F32) for r in refs[:n_in]])
        for o_ref, o in zip(refs[n_in:n_in + n_tile], outs[:n_tile]):
            o_ref[...] = o.astype(o_ref.dtype)
        if sum_outs:
            @pl.when(pl.program_id(0) == 0)
            def _():
                for s_ref in refs[n_in + n_tile:]:
                    s_ref[...] = jnp.zeros_like(s_ref)

            for s_ref, s in zip(refs[n_in + n_tile:], outs[n_tile:]):
                s_ref[...] += s
        bottom()

    def spec(shape):
        if shape[0] == 1:
            return pl.BlockSpec(shape, lambda i: (0, 0))
        return pl.BlockSpec((tr, shape[1]), lambda i: (i, 0))

    return pl.pallas_call(
        body, name=name, grid=(n_steps,), in_specs=[spec(a.shape) for a in ins] + [ANY] * len(rider.operands),
        out_specs=[spec(o.shape) for o in tile_outs] + [spec(s.shape) for s in sum_outs] + [ANY] * len(rider.out_shapes),
        out_shape=list(tile_outs) + list(sum_outs) + list(rider.out_shapes), scratch_shapes=list(rider.scratch),
        compiler_params=_params("arbitrary" if sum_outs or rider.operands else "parallel"),
    )(*ins, *rider.operands)


def _norm_fwd(name, x, gain, rider=None):
    def fn(xv, g):
        inv = lax.rsqrt(jnp.mean(xv * xv, axis=-1, keepdims=True) + RMS_EPS)
        return (xv * inv * g,)

    res = _rows(name, fn, [x, gain], [jax.ShapeDtypeStruct(x.shape, BF16)], rider=rider)
    return res[0], res[1:]


def _rms_norm_bwd(dh, xv, g):
    inv = lax.rsqrt(jnp.mean(xv * xv, axis=-1, keepdims=True) + RMS_EPS)
    xn = xv * inv
    dxn = dh * g
    return inv * (dxn - xn * jnp.mean(dxn * xn, axis=-1, keepdims=True)), jnp.sum(dh * xn, axis=0, keepdims=True)


def _ple_and_loss(gv, pv, x2v, tv, g_final, g_ple, w_pg):
    d = x2v.shape[1]
    s = _sigmoid(gv)
    xv = x2v + s * pv
    inv = lax.rsqrt(jnp.mean(xv * xv, axis=-1, keepdims=True) + RMS_EPS)
    err = xv * inv * g_final - tv
    dx3, d_final = _rms_norm_bwd(err * (1.0 / d), xv, g_final)
    d_pp, d_gp = dx3 * s, dx3 * pv * s * (1.0 - s)
    dh3 = lax.dot_general(d_gp.astype(BF16), w_pg, (((1,), (1,)), ((), ())), preferred_element_type=F32)
    dx2, d_ple = _rms_norm_bwd(dh3, x2v, g_ple)
    dx2 = dx2 + dx3
    return dx2, dx2, d_pp, d_gp, d_final, (0.5 / d) * jnp.sum(err * err, axis=0, keepdims=True), d_ple


def _window_counts(t_pos, w):
    return jnp.minimum(t_pos + 1, w).astype(F32)


def _pool_fwd(u, w_pool, scale, tr=512):
    t_total, width = u.shape
    tr = min(tr, t_total)
    n_groups = len(POOL_WINDOWS)
    gdim = width // n_groups
    ext = tr + POOL_HALO

    def body(u_ref, halo_ref, w_ref, s_ref, pooled_ref, ya_ref):
        i = pl.program_id(0)
        halo = jnp.where(i == 0, 0.0, halo_ref[...])
        t_pos = i * tr + lax.broadcasted_iota(jnp.int32, (tr, 1), 0)
        for g, w in enumerate(POOL_WINDOWS):
            cols = slice(g * gdim, (g + 1) * gdim)
            main = u_ref[:, cols]
            win = jnp.concatenate([halo[:, cols], main], axis=0)
            span = 1
            while span < w:
                win = win + pltpu.roll(win, span, 0)
                span *= 2
            pooled = win[POOL_HALO:, :] * (1.0 / _window_counts(t_pos, w)) - main
            pooled_b = pooled.astype(BF16)
            pooled_ref[:, cols] = pooled_b
            mixed = jnp.dot(pooled_b, w_ref[g], preferred_element_type=F32)
            ya_ref[:, cols] = (mixed * s_ref[:, cols]).astype(BF16)

    hb = tr // POOL_HALO
    return pl.pallas_call(
        body, name="pool_fwd", grid=(t_total // tr,),
        in_specs=[pl.BlockSpec((tr, width), lambda i: (i, 0)),
                  pl.BlockSpec((POOL_HALO, width), lambda i: (jnp.maximum(i * hb - 1, 0), 0)),
                  pl.BlockSpec((n_groups, gdim, gdim), lambda i: (0, 0, 0)),
                  pl.BlockSpec((1, width), lambda i: (0, 0))],
        out_specs=[pl.BlockSpec((tr, width), lambda i: (i, 0)), pl.BlockSpec((tr, width), lambda i: (i, 0))],
        out_shape=[jax.ShapeDtypeStruct(u.shape, BF16), jax.ShapeDtypeStruct(u.shape, BF16)],
        compiler_params=_params("parallel"),
    )(u, u, w_pool, scale)


def _pool_bwd(dya, pooled, w_pool, scale, tr=512):
    t_total, width = dya.shape
    tr = min(tr, t_total)
    n_groups = len(POOL_WINDOWS)
    gdim = width // n_groups
    ext = tr + POOL_HALO
    n_tiles = t_total // tr

    def body(d_ref, halo_ref, p_ref, w_ref, s_ref, du_ref, dw_ref, ds_ref):
        i = pl.program_id(0)

        @pl.when(i == 0)
        def _():
            dw_ref[...] = jnp.zeros_like(dw_ref)
            ds_ref[...] = jnp.zeros_like(ds_ref)

        halo = jnp.where(i == n_tiles - 1, 0.0, halo_ref[...])
        t_pos = i * tr + lax.broadcasted_iota(jnp.int32, (ext, 1), 0)
        for g, w in enumerate(POOL_WINDOWS):
            cols = slice(g * gdim, (g + 1) * gdim)
            sc = s_ref[:, cols]
            d_main = d_ref[:, cols]
            pooled_b = p_ref[:, cols]
            mixed = jnp.dot(pooled_b, w_ref[g], preferred_element_type=F32)
            ds_ref[:, cols] += jnp.sum(d_main * mixed, axis=0, keepdims=True)
            dmix = (jnp.concatenate([d_main, halo[:, cols]], axis=0) * sc).astype(BF16)
            dw_ref[g] += lax.dot_general(pooled_b, dmix[:tr, :], (((0,), (0,)), ((), ())),
                                         preferred_element_type=F32)
            dpool = lax.dot_general(dmix, w_ref[g], (((1,), (1,)), ((), ())), preferred_element_type=F32)
            win = dpool * (1.0 / _window_counts(t_pos, w))
            span = 1
            while span < w:
                win = win + pltpu.roll(win, ext - span, 0)
                span *= 2
            du_ref[:, cols] = (win[:tr, :] - dpool[:tr, :]).astype(BF16)

    hb = tr // POOL_HALO
    last_halo = t_total // POOL_HALO - 1
    return pl.pallas_call(
        body, name="pool_bwd", grid=(n_tiles,),
        in_specs=[pl.BlockSpec((tr, width), lambda i: (i, 0)),
                  pl.BlockSpec((POOL_HALO, width), lambda i: (jnp.minimum((i + 1) * hb, last_halo), 0)),
                  pl.BlockSpec((tr, width), lambda i: (i, 0)),
                  pl.BlockSpec((n_groups, gdim, gdim), lambda i: (0, 0, 0)),
                  pl.BlockSpec((1, width), lambda i: (0, 0))],
        out_specs=[pl.BlockSpec((tr, width), lambda i: (i, 0)),
                   pl.BlockSpec((n_groups, gdim, gdim), lambda i: (0, 0, 0)),
                   pl.BlockSpec((1, width), lambda i: (0, 0))],
        out_shape=[jax.ShapeDtypeStruct(dya.shape, BF16), jax.ShapeDtypeStruct((n_groups, gdim, gdim), F32),
                   jax.ShapeDtypeStruct((1, width), F32)],
        compiler_params=_params("arbitrary"),
    )(dya, dya, pooled, w_pool, scale)


def _head_masks():
    lane = lax.broadcasted_iota(jnp.int32, (1, LANES), 1)
    return lane < HEAD_DIM


def _stack_heads(tile, first):
    zero = jnp.zeros_like(tile)
    return jnp.concatenate([jnp.where(first, tile, zero), jnp.where(first, zero, tile)], axis=0)


def _causal_mask(t_pos, k_start):
    col = lax.broadcasted_iota(jnp.int32, (1, 2 * ATT_SLAB), 1)
    return k_start + (col & (ATT_SLAB - 1)) < t_pos


def _slab_scores(q, kd, mask):
    z2 = lax.dot_general(q, kd, (((1,), (1,)), ((), ())), preferred_element_type=F32) * LOG2_E
    log_hit = jnp.minimum(z2, 0.0) - jnp.log2(1.0 + jnp.exp2(-jnp.abs(z2)))
    log_fail = log_hit - z2
    return log_hit, (log_fail if mask is None else jnp.where(mask, log_fail, 0.0))


def _weights(log_hit, suffix, mask):
    arg = log_hit + suffix
    return jnp.exp2(arg if mask is None else jnp.where(mask, arg, -1e30))


def _tri(upper):
    r = lax.broadcasted_iota(jnp.int32, (ATT_CHUNK, ATT_CHUNK), 0)
    c = lax.broadcasted_iota(jnp.int32, (ATT_CHUNK, ATT_CHUNK), 1)
    return jnp.where(r > c if upper else r < c, 1.0, 0.0).astype(BF16)


def _tri_spec():
    return pl.BlockSpec((ATT_CHUNK, ATT_CHUNK), lambda h, i: (0, 0), pipeline_mode=pl.Buffered(1))


def _scan_chunk(v, tri):
    return jnp.dot(v.astype(BF16), tri, preferred_element_type=F32)


def _lane_bcast(col):
    return jnp.broadcast_to(col, (col.shape[0], LANES))


def _scan_slab(v, tri, carries, from_right):
    n_chunks = ATT_SLAB // ATT_CHUNK
    edge = 0 if from_right else ATT_CHUNK - 1
    parts, new_carries = [None] * (2 * n_chunks), []
    for head in range(2):
        run = carries[head]
        for c in (reversed(range(n_chunks)) if from_right else range(n_chunks)):
            lo_col = head * ATT_SLAB + c * ATT_CHUNK
            vc = v[:, lo_col:lo_col + ATT_CHUNK]
            sc = _scan_chunk(vc, tri)
            parts[head * n_chunks + c] = sc + jnp.concatenate([run] * (ATT_CHUNK // LANES), axis=1)
            run = run + _lane_bcast(sc[:, edge:edge + 1] + vc[:, edge:edge + 1])
        new_carries.append(run)
    return jnp.concatenate(parts, axis=1), new_carries


def _fold_heads(stacked, first):
    s = stacked.shape[0] // 2
    return jnp.where(first, stacked[:s], stacked[s:])


class _NoRider:
    operands, out_shapes, scratch = (), (), ()

    def split(self, refs, n_base_in, n_base_out):
        n_in, n_out, n_sem = len(self.operands), len(self.out_shapes), len(self.scratch)
        a = n_base_in + n_in
        b = a + n_base_out + n_out
        mine = (refs[n_base_in:a], refs[a + n_base_out:b], refs[b:b + n_sem])
        return refs[:n_base_in], refs[a:a + n_base_out], refs[b + n_sem:], mine

    def start(self, ins, outs, sems):
        pass

    def relay(self, ins, outs, sems):
        pass

    def finish(self, ins, outs, sems):
        pass

    def at_steps(self, refs, first_step, relay_step, last_step):
        if not self.operands:
            return (lambda: None), (lambda: None)

        def top():
            pl.when(first_step)(lambda: self.start(*refs))
            pl.when(relay_step)(lambda: self.relay(*refs))

        return top, lambda: pl.when(last_step)(lambda: self.finish(*refs))


def _attn_fwd(q_src, q_col, kv_src, k_col, v_col, n_pairs=4, rider=_NoRider()):
    t_total = q_src.shape[0]
    blk = ATT_BLOCK
    n_chains = ATT_FWD_CHAINS if t_total % (ATT_FWD_CHAINS * blk) == 0 else ATT_CHAINS
    n_steps = t_total // (n_chains * blk)
    assert t_total % ATT_SLAB == 0 and ATT_SLAB == ATT_BLOCK

    def body(*refs):
        (q_ref, k_ref, v_ref, suffix_ref), (o_ref,), _, riding = rider.split(refs, 4, 1)
        h, ii = pl.program_id(0), pl.program_id(1)
        top, bottom = rider.at_steps(riding, (h == 0) & (ii == 0), (h == n_pairs - 1) & (ii == 0),
                                     (h == n_pairs - 1) & (ii == n_steps - 1))
        top()
        first = _head_masks()
        suffix_tri = suffix_ref[...]
        blocks = [n_chains * ii + c for c in range(n_chains)]
        qs = [q_ref[c * blk:(c + 1) * blk, :] * ATT_SCALE for c in range(n_chains)]
        t_pos = [b * blk + lax.broadcasted_iota(jnp.int32, (blk, 1), 0) for b in blocks]

        def one(c, t, chain, on_diagonal):
            _, acc, right_a, right_b = chain
            k_start = pl.multiple_of((blocks[c] - t) * ATT_SLAB, ATT_SLAB)
            kd = _stack_heads(k_ref[pl.ds(k_start, ATT_SLAB), :], first)
            vd = _stack_heads(v_ref[pl.ds(k_start, ATT_SLAB), :], first)
            mask = _causal_mask(t_pos[c], k_start) if on_diagonal else None
            log_hit, log_fail = _slab_scores(qs[c], kd, mask)
            suffix, (right_a, right_b) = _scan_slab(log_fail, suffix_tri, (right_a, right_b), from_right=True)
            a = _weights(log_hit, suffix, mask).astype(BF16)
            acc = acc + jnp.dot(a, vd, preferred_element_type=F32)
            return jnp.max(jnp.maximum(right_a, right_b)), acc, right_a, right_b

        def step(state, on_diagonal):
            t, chains = state
            return t + 1, tuple(one(c, t, chains[c], on_diagonal) for c in range(n_chains))

        def more(state):
            t, chains = state
            return (t <= blocks[0]) & (functools.reduce(jnp.maximum, [ch[0] for ch in chains]) > ATT_EXIT_BELOW)

        zero = jnp.zeros((blk, LANES), F32)
        state = step((0, ((jnp.float32(0.0), zero, zero, zero),) * n_chains), on_diagonal=True)
        t, chains = lax.while_loop(more, functools.partial(step, on_diagonal=False), state)
        for c in range(n_chains):
            chain = chains[c]
            if c:
                _, chain = lax.while_loop(
                    lambda s, c=c: (s[0] <= blocks[c]) & (s[1][0] > ATT_EXIT_BELOW),
                    lambda s, c=c: (s[0] + 1, one(c, s[0], s[1], False)), (t, chain))
            o_ref[c * blk:(c + 1) * blk, :] = chain[1].astype(BF16)
        bottom()

    rows = n_chains * blk
    res = pl.pallas_call(
        body, name="attn_fwd", grid=(n_pairs, n_steps),
        in_specs=[pl.BlockSpec((rows, LANES), lambda h, i: (i, q_col + h)),
                  pl.BlockSpec((t_total, LANES), lambda h, i: (0, k_col + h)),
                  pl.BlockSpec((t_total, LANES), lambda h, i: (0, v_col + h)), _tri_spec()] + [ANY] * len(rider.operands),
        out_specs=[pl.BlockSpec((rows, LANES), lambda h, i: (i, h))] + [ANY] * len(rider.out_shapes),
        out_shape=[jax.ShapeDtypeStruct((t_total, n_pairs * LANES), BF16)] + list(rider.out_shapes),
        scratch_shapes=list(rider.scratch),
        compiler_params=_params("arbitrary", "arbitrary"),
    )(q_src, kv_src, kv_src, _tri(upper=True), *rider.operands)
    return res[0], res[1:]


def _attn_bwd(q_src, q_col, kv_src, k_col, v_col, dy, n_pairs=4, rider=_NoRider()):
    t_total = q_src.shape[0]
    blk = ATT_BLOCK
    n_steps = t_total // (ATT_CHAINS * blk)
    n_slabs = t_total // ATT_SLAB
    assert t_total % ATT_SLAB == 0 and ATT_SLAB == ATT_BLOCK

    def body(*refs):
        ins, (dq_ref, dk_ref, dv_ref), (g_s, dk_acc, dv_acc), riding = rider.split(refs, 6, 3)
        q_ref, dy_ref, k_ref, v_ref, suffix_ref, prefix_ref = ins
        h, ii = pl.program_id(0), pl.program_id(1)
        top, bottom = rider.at_steps(riding, (h == 0) & (ii == 0), (h == n_pairs - 1) & (ii == 0),
                                     (h == n_pairs - 1) & (ii == n_steps - 1))
        top()

        @pl.when(ii == 0)
        def _():
            dk_acc[...] = jnp.zeros_like(dk_acc)
            dv_acc[...] = jnp.zeros_like(dv_acc)

        first = _head_masks()
        suffix_tri = suffix_ref[...]
        prefix_tri = prefix_ref[...]
        blocks = [ATT_CHAINS * ii + c for c in range(ATT_CHAINS)]
        rows = [slice(c * blk, (c + 1) * blk) for c in range(ATT_CHAINS)]
        qs = [q_ref[r, :] * ATT_SCALE for r in rows]
        dys = [dy_ref[r, :] for r in rows]
        t_pos = [b * blk + lax.broadcasted_iota(jnp.int32, (blk, 1), 0) for b in blocks]

        def one1(c, t, chain, on_diagonal):
            _, right_a, right_b = chain
            slab = blocks[c] - t
            k_start = pl.multiple_of(slab * ATT_SLAB, ATT_SLAB)
            kd = _stack_heads(k_ref[pl.ds(k_start, ATT_SLAB), :], first)
            vd = _stack_heads(v_ref[pl.ds(k_start, ATT_SLAB), :], first)
            mask = _causal_mask(t_pos[c], k_start) if on_diagonal else None
            log_hit, log_fail = _slab_scores(qs[c], kd, mask)
            suffix, (right_a, right_b) = _scan_slab(log_fail, suffix_tri, (right_a, right_b), from_right=True)
            a = _weights(log_hit, suffix, mask)
            da = lax.dot_general(dys[c], vd, (((1,), (1,)), ((), ())), preferred_element_type=F32)
            g_s[c, slab] = (da * a).astype(BF16)
            dv_acc[pl.ds(k_start, ATT_SLAB), :] += _fold_heads(lax.dot_general(
                a.astype(BF16), dys[c], (((0,), (0,)), ((), ())), preferred_element_type=F32), first)
            return jnp.max(jnp.maximum(right_a, right_b)), right_a, right_b

        def step1(state, on_diagonal):
            t, chains = state
            return t + 1, tuple(one1(c, t, chains[c], on_diagonal) for c in range(ATT_CHAINS))

        def more(state):
            t, chains = state
            return (t <= blocks[0]) & (functools.reduce(jnp.maximum, [ch[0] for ch in chains]) > ATT_EXIT_BELOW)

        zero = jnp.zeros((blk, LANES), F32)
        state = step1((0, ((jnp.float32(0.0), zero, zero),) * ATT_CHAINS), on_diagonal=True)
        joint, chains = lax.while_loop(more, functools.partial(step1, on_diagonal=False), state)
        done = [joint]
        for c in range(1, ATT_CHAINS):
            done.append(lax.while_loop(
                lambda s, c=c: (s[0] <= blocks[c]) & (s[1][0] > ATT_EXIT_BELOW),
                lambda s, c=c: (s[0] + 1, one1(c, s[0], s[1], False)), (joint, chains[c]))[0])

        def one2(c, t, carry, on_diagonal):
            dq, left_a, left_b = carry
            slab = blocks[c] - t
            k_start = pl.multiple_of(slab * ATT_SLAB, ATT_SLAB)
            kd = _stack_heads(k_ref[pl.ds(k_start, ATT_SLAB), :], first)
            g = g_s[c, slab]
            z2 = lax.dot_general(qs[c], kd, (((1,), (1,)), ((), ())), preferred_element_type=F32) * LOG2_E
            sig = 1.0 / (1.0 + jnp.exp2(-z2))
            prefix, (left_a, left_b) = _scan_slab(g, prefix_tri, (left_a, left_b), from_right=False)
            dz = g * (1.0 - sig) - sig * prefix
            if on_diagonal:
                dz = jnp.where(_causal_mask(t_pos[c], k_start), dz, 0.0)
            dz = dz.astype(BF16)
            dq = dq + jnp.dot(dz, kd, preferred_element_type=F32)
            dk_acc[pl.ds(k_start, ATT_SLAB), :] += _fold_heads(lax.dot_general(
                dz, qs[c], (((0,), (0,)), ((), ())), preferred_element_type=F32), first)
            return dq, left_a, left_b

        carries = [(zero, zero, zero)]
        for c in range(1, ATT_CHAINS):
            carries.append(lax.fori_loop(
                0, done[c] - joint, lambda n, carry, c=c: one2(c, done[c] - 1 - n, carry, False), (zero, zero, zero)))
        carries = lax.fori_loop(
            0, joint - 1,
            lambda n, cs: tuple(one2(c, joint - 1 - n, cs[c], False) for c in range(ATT_CHAINS)), tuple(carries))
        for c in range(ATT_CHAINS):
            dq_ref[rows[c], :] = (one2(c, 0, carries[c], True)[0] * ATT_SCALE).astype(BF16)

        @pl.when(ii == n_steps - 1)
        def _():
            dk_ref[...] = dk_acc[...].astype(BF16)
            dv_ref[...] = dv_acc[...].astype(BF16)

        bottom()

    out = jax.ShapeDtypeStruct((t_total, n_pairs * LANES), BF16)
    n_rows = ATT_CHAINS * blk
    whole = dict(pipeline_mode=pl.Buffered(1))
    res = pl.pallas_call(
        body, name="attn_bwd", grid=(n_pairs, n_steps),
        in_specs=[pl.BlockSpec((n_rows, LANES), lambda h, i: (i, q_col + h)),
                  pl.BlockSpec((n_rows, LANES), lambda h, i: (i, h)),
                  pl.BlockSpec((t_total, LANES), lambda h, i: (0, k_col + h), **whole),
                  pl.BlockSpec((t_total, LANES), lambda h, i: (0, v_col + h), **whole), _tri_spec(), _tri_spec()]
        + [ANY] * len(rider.operands),
        out_specs=[pl.BlockSpec((n_rows, LANES), lambda h, i: (i, h)),
                   pl.BlockSpec((t_total, LANES), lambda h, i: (0, h)),
                   pl.BlockSpec((t_total, LANES), lambda h, i: (0, h))] + [ANY] * len(rider.out_shapes),
        out_shape=[out, out, out] + list(rider.out_shapes),
        scratch_shapes=list(rider.scratch) + [pltpu.VMEM((ATT_CHAINS, n_slabs, blk, 2 * ATT_SLAB), BF16),
                                              pltpu.VMEM((t_total, LANES), F32), pltpu.VMEM((t_total, LANES), F32)],
        compiler_params=_params("arbitrary", "arbitrary"),
    )(q_src, dy, kv_src, kv_src, _tri(upper=True), _tri(upper=False), *rider.operands)
    return res[:3], res[3:]


def _adamw(name, w, g, m, v):
    def fn(wv, gv, mv, vv):
        mn = ADAM_B1 * mv + (1.0 - ADAM_B1) * gv
        vn = ADAM_B2 * vv + (1.0 - ADAM_B2) * (gv * gv)
        m_hat = mn / (1.0 - ADAM_B1 ** ADAM_STEP)
        v_hat = vn / (1.0 - ADAM_B2 ** ADAM_STEP)
        return -ADAM_LR * (m_hat / (jnp.sqrt(v_hat) + ADAM_EPS) + ADAM_WD * wv), mn, vn

    rows = w.shape[0]
    tr = _row_tile(rows)
    shp = jax.ShapeDtypeStruct(w.shape, F32)
    if rows == 1:
        def body(w_ref, g_ref, m_ref, v_ref, d_ref, mo_ref, vo_ref):
            d, mn, vn = fn(w_ref[...], g_ref[...], m_ref[...], v_ref[...])
            d_ref[...], mo_ref[...], vo_ref[...] = d, mn, vn

        return pl.pallas_call(body, name=name, out_shape=[shp, shp, shp])(w, g, m, v)
    return _rows(name, fn, [w, g, m, v], [shp, shp, shp], tr=tr)


def _place():
    return lax.axis_index("x"), lax.axis_index("y"), lax.axis_index("c")


def _other_chips(x, y):
    return [(1 - x, y), (x, 1 - y), (1 - x, 1 - y)]


ANY = pl.BlockSpec(memory_space=pl.ANY)


def _remote(src, dst, send_sem, recv_sem, to):
    return pltpu.make_async_remote_copy(src_ref=src, dst_ref=dst, send_sem=send_sem, recv_sem=recv_sem,
                                        device_id=to, device_id_type=MESH)


class _WeightGather(_NoRider):
    def __init__(self, shards):
        n_w = len(shards)
        self.operands = list(shards)
        self.out_shapes = [jax.ShapeDtypeStruct((N_CHIPS,) + s.shape, s.dtype) for s in shards]
        self.scratch = [pltpu.SemaphoreType.DMA((3, n_w))] * 4 + [pltpu.SemaphoreType.DMA((n_w,))] * 2

    def _copies(self, ins, outs, sems):
        send_sems, recv_sems, relay_send, relay_recv, own_send, own_recv = sems
        x, y, c = _place()
        my_chip, sibling = 2 * x + y, (x, y, 1 - c)
        n_w = len(ins)

        def half(w, chip, core):
            h = self.operands[w].shape[0] // 2
            return outs[w].at[chip, pl.ds(core * h, h)]

        own = [_remote(ins[w], outs[w].at[my_chip], own_send.at[w], own_recv.at[w], sibling) for w in range(n_w)]
        sends, landed, relays, relayed = [], [], [], []
        for p, (ox, oy) in enumerate(_other_chips(x, y)):
            for w in range(n_w):
                h = self.operands[w].shape[0] // 2
                sends.append(_remote(ins[w].at[pl.ds(c * h, h)], half(w, my_chip, c), send_sems.at[p, w],
                                     recv_sems.at[p, w], (ox, oy, c)))
                here = half(w, 2 * ox + oy, c)
                landed.append(_remote(here, here, send_sems.at[p, w], recv_sems.at[p, w], (ox, oy, c)))
                relays.append(_remote(here, here, relay_send.at[p, w], relay_recv.at[p, w], sibling))
                there = half(w, 2 * ox + oy, 1 - c)
                relayed.append(_remote(there, there, relay_send.at[p, w], relay_recv.at[p, w], sibling))
        return own, sends, landed, relays, relayed

    def start(self, ins, outs, sems):
        own, sends, _, _, _ = self._copies(ins, outs, sems)
        for cp in own + sends:
            cp.start()

    def relay(self, ins, outs, sems):
        _, _, landed, relays, _ = self._copies(ins, outs, sems)
        for arrival, cp in zip(landed, relays):
            arrival.wait_recv()
            cp.start()

    def finish(self, ins, outs, sems):
        own, sends, _, relays, relayed = self._copies(ins, outs, sems)
        for arrival in relayed:
            arrival.wait_recv()
        for cp in sends + relays:
            cp.wait_send()
        for cp in own:
            cp.wait()


class _ChipExchange(_NoRider):
    def __init__(self, pair_sums):
        n_w = len(pair_sums)
        self.operands = list(pair_sums)
        self.out_shapes = [jax.ShapeDtypeStruct((3,) + s.shape[1:], s.dtype) for s in pair_sums]
        self.scratch = [pltpu.SemaphoreType.DMA((3, n_w))] * 2

    def _copies(self, ins, outs, sems):
        send_sems, recv_sems = sems
        x, y, c = _place()
        return [_remote(ins[w].at[2 * ox + oy], outs[w].at[p], send_sems.at[p, w], recv_sems.at[p, w], (ox, oy, c))
                for p, (ox, oy) in enumerate(_other_chips(x, y)) for w in range(len(ins))]

    def start(self, ins, outs, sems):
        for cp in self._copies(ins, outs, sems):
            cp.start()

    def finish(self, ins, outs, sems):
        for cp in self._copies(ins, outs, sems):
            cp.wait()


class _PairExchange(_NoRider):
    def __init__(self, grads):
        n_w = len(grads)
        self.operands = list(grads)
        self.out_shapes = [jax.ShapeDtypeStruct(g.shape[:-2] + (g.shape[-2] // 2, g.shape[-1]), F32) for g in grads]
        self.scratch = [pltpu.SemaphoreType.DMA((n_w,))] * 2

    def _copies(self, ins, theirs, sems):
        send_sems, recv_sems = sems
        x, y, c = _place()
        sends = []
        for w, g in enumerate(self.operands):
            rows = pl.ds((1 - c) * (g.shape[-2] // 2), g.shape[-2] // 2)
            src = ins[w].at[:, rows, :] if g.ndim == 3 else ins[w].at[rows, :]
            sends.append(_remote(src, theirs[w], send_sems.at[w], recv_sems.at[w], (x, y, 1 - c)))
        return sends

    def start(self, ins, outs, sems):
        for cp in self._copies(ins, outs, sems):
            cp.start()

    def finish(self, ins, outs, sems):
        for cp in self._copies(ins, outs, sems):
            cp.wait()


def _run_exchange(name, plan):
    n_in, n_out = len(plan.operands), len(plan.out_shapes)
    if not n_in:
        return ()

    def body(*refs):
        parts = (refs[:n_in], refs[n_in:n_in + n_out], refs[n_in + n_out:])
        plan.start(*parts)
        plan.relay(*parts)
        plan.finish(*parts)

    return pl.pallas_call(body, name=name, in_specs=[ANY] * n_in, out_specs=[ANY] * n_out,
                          out_shape=list(plan.out_shapes), scratch_shapes=list(plan.scratch))(*plan.operands)


class _NoExchanges:
    pair_sums, landed = {}, {}

    def gather(self, names):
        return _NoRider()

    def pair(self, names, grads):
        return _NoRider()

    def paired(self, names, grads, theirs):
        pass

    def chip(self, names):
        return _NoRider()


class _StepExchanges(_NoExchanges):
    def __init__(self, shards_bf16, place):
        self.shards, self.place = shards_bf16, place
        self.pair_sums, self.landed = {}, {}

    def gather(self, names):
        return _WeightGather([self.shards[n] for n in names])

    def pair(self, names, grads):
        return _PairExchange([grads[n] for n in names])

    def paired(self, names, grads, theirs):
        for n, other in zip(names, theirs):
            self.pair_sums[n] = _pair_sum(f"pair_sum_{n}", self.place, grads[n], other)

    def chip(self, names):
        return _ChipExchange([self.pair_sums[n] for n in names])


def _finish_gradients(shards, vec):
    n_w = len(shards)
    rows = vec.shape[0]

    def body(*refs):
        outs, v_ref, o_ref = refs[n_w + 1:2 * n_w + 1], refs[n_w], refs[2 * n_w + 1]
        half_send, half_recv, slots, vec_send, vec_recv = refs[2 * n_w + 2:]
        x, y, c = _place()
        me, sibling = 4 * x + 2 * y + c, (x, y, 1 - c)
        halves = []
        for w in range(n_w):
            h = shards[w].shape[0] // 2
            mine, theirs = outs[w].at[pl.ds(c * h, h)], outs[w].at[pl.ds((1 - c) * h, h)]
            halves.append((_remote(mine, mine, half_send.at[w], half_recv.at[w], sibling),
                           _remote(theirs, theirs, half_send.at[w], half_recv.at[w], sibling)))
        slots[me] = v_ref[...]
        spread = []
        for k in range(1, N_DEV):
            peer = (x ^ (k >> 2), y ^ ((k >> 1) & 1), c ^ (k & 1))
            landed = slots.at[4 * peer[0] + 2 * peer[1] + peer[2]]
            spread.append((_remote(v_ref, slots.at[me], vec_send.at[k - 1], vec_recv.at[k - 1], peer),
                           _remote(landed, landed, vec_send.at[k - 1], vec_recv.at[k - 1], peer)))
        for send, _ in halves + spread:
            send.start()
        for send, arrival in spread:
            arrival.wait_recv()
            send.wait_send()
        total = slots[0]
        for d in range(1, N_DEV):
            total = total + slots[d]
        o_ref[...] = total
        for send, arrival in halves:
            arrival.wait_recv()
            send.wait_send()

    vm = pl.BlockSpec(memory_space=pltpu.VMEM)
    res = pl.pallas_call(
        body, name="finish_gradients", in_specs=[ANY] * n_w + [vm], out_specs=[ANY] * n_w + [vm],
        out_shape=[jax.ShapeDtypeStruct(s.shape, s.dtype) for s in shards] + [jax.ShapeDtypeStruct(vec.shape, F32)],
        input_output_aliases={w: w for w in range(n_w)},
        scratch_shapes=[pltpu.SemaphoreType.DMA((n_w,)), pltpu.SemaphoreType.DMA((n_w,)),
                        pltpu.VMEM((N_DEV, rows, LANES), F32), pltpu.SemaphoreType.DMA((N_DEV - 1,)),
                        pltpu.SemaphoreType.DMA((N_DEV - 1,))],
    )(*shards, vec)
    return res[:n_w], res[n_w]


def _row_tile(rows):
    fits = [tr for tr in range(16, min(rows, 512) + 1, 16) if rows % tr == 0]
    return max(fits) if fits else rows


def _pair_sum(name, place, grad, theirs):
    if grad.ndim == 2:
        return _pair_sum_joined(name, place, grad, theirs)
    n, r, c = grad.shape
    half = r // 2
    tr = _row_tile(half)
    nb = half // tr

    def body(place_ref, g_ref, t_ref, o_ref):
        o_ref[...] = (g_ref[...] + t_ref[...]).astype(BF16)

    return pl.pallas_call(
        body, name=name, out_shape=jax.ShapeDtypeStruct((n, half, c), BF16),
        grid_spec=pltpu.PrefetchScalarGridSpec(
            num_scalar_prefetch=1, grid=(n, nb),
            in_specs=[pl.BlockSpec((1, tr, c), lambda j, i, pr: (j, pr[0] * nb + i, 0)),
                      pl.BlockSpec((1, tr, c), lambda j, i, pr: (j, i, 0))],
            out_specs=pl.BlockSpec((1, tr, c), lambda j, i, pr: (j, i, 0))),
        compiler_params=_params("parallel", "parallel"),
    )(place, grad, theirs)


def _pair_sum_joined(name, place, grad, theirs):
    r, wide = grad.shape
    half, c = r // 2, wide // N_CHIPS
    tr = _row_tile(half)
    nb = half // tr

    def body(place_ref, g_ref, t_ref, o_ref):
        for j in range(N_CHIPS):
            cols = slice(j * c, (j + 1) * c)
            o_ref[j] = (g_ref[:, cols] + t_ref[:, cols]).astype(BF16)

    return pl.pallas_call(
        body, name=name, out_shape=jax.ShapeDtypeStruct((N_CHIPS, half, c), BF16),
        grid_spec=pltpu.PrefetchScalarGridSpec(
            num_scalar_prefetch=1, grid=(nb,),
            in_specs=[pl.BlockSpec((tr, wide), lambda i, pr: (pr[0] * nb + i, 0)),
                      pl.BlockSpec((tr, wide), lambda i, pr: (i, 0))],
            out_specs=pl.BlockSpec((N_CHIPS, tr, c), lambda i, pr: (0, i, 0))),
        compiler_params=_params("parallel"),
    )(place, grad, theirs)


def _sum_chips(name, place, pair_sums, landed):
    _, half, c = pair_sums.shape
    tr = _row_tile(half)
    nb = half // tr

    def body(place_ref, s_ref, q_ref, o_ref):
        total = s_ref[0].astype(F32)
        for p in range(3):
            total = total + q_ref[p].astype(F32)
        o_ref[...] = total

    return pl.pallas_call(
        body, name=name, out_shape=jax.ShapeDtypeStruct((2 * half, c), F32),
        grid_spec=pltpu.PrefetchScalarGridSpec(
            num_scalar_prefetch=1, grid=(nb,),
            in_specs=[pl.BlockSpec((1, tr, c), lambda i, pr: (pr[1], i, 0)),
                      pl.BlockSpec((3, tr, c), lambda i, pr: (0, i, 0))],
            out_specs=pl.BlockSpec((tr, c), lambda i, pr: (pr[0] * nb + i, 0))),
        compiler_params=_params("parallel"),
    )(place, pair_sums, landed)


MIXER = ("w_branch_a", "w_branch_b", "w_out")
FFN_PLE = ("w_ffn_gate", "w_ffn_up", "w_ffn_down", "w_ple_gate", "w_ple_proj")
LATE = MIXER + FFN_PLE
BIG = ("w_in",) + LATE
HELD_TRANSPOSED = ("w_ffn_gate", "w_ffn_up")
SMALL = ("norm_mix", "w_pool", "pool_scale", "norm_ffn", "norm_ple", "norm_final")


def _join_columns(w4):
    return jnp.concatenate([w4[j] for j in range(N_CHIPS)], axis=1)


def _sds(shape, dtype):
    return jax.ShapeDtypeStruct(shape, dtype)


def _local_step(x, p, target, wf, small, ex=None):
    t, d = x.shape
    w_pool_b = small["w_pool"].astype(BF16)
    dp = w_pool_b.shape[0] * w_pool_b.shape[1]

    ex = ex or _NoExchanges()
    h1, first = _norm_fwd("norm_mix", x, small["norm_mix"], rider=ex.gather(("w_in",)))
    wf = {**wf, **dict(zip(("w_in",), first))}
    w_in = wf["w_in"]
    u, q, kv, ga, gb = _mm(
        "proj", [h1], [w_in[j] for j in range(N_CHIPS)], "nn",
        [_sds((t, dp), F32), _sds((t, dp), BF16), _sds((t, d), BF16), _sds((t, d), BF16), _sds((t, d), BF16)],
        separate=True, epilogue=lambda uq, kv_, ga_, gb_: (uq[:, :dp], uq[:, dp:], kv_, ga_, gb_), tm=512)
    pooled, ya = _pool_fwd(u, w_pool_b, small["pool_scale"])
    n_pairs = dp // LANES
    yb, late = _attn_fwd(q, 0, kv, 0, n_pairs, n_pairs, rider=ex.gather(LATE))
    wf = {**wf, **dict(zip(LATE, late))}
    w_down = wf["w_ffn_down"].reshape(-1, d)
    dff = w_down.shape[0]
    w_gate_t, w_up_t = wf["w_ffn_gate"].reshape(dff, d), wf["w_ffn_up"].reshape(dff, d)
    w_a, w_b, w_pp = _join_columns(wf["w_branch_a"]), _join_columns(wf["w_branch_b"]), _join_columns(wf["w_ple_proj"])
    w_out = wf["w_out"].reshape(d, d)
    w_pg = wf["w_ple_gate"].reshape(d, d)
    def residual_norm(branch, xv, g, w):
        xn = xv + jnp.dot(branch.astype(BF16), w, preferred_element_type=F32)
        return xn, xn * lax.rsqrt(jnp.mean(xn * xn, axis=-1, keepdims=True) + RMS_EPS) * g

    def mixer_tail(tav, tbv, gav, gbv, xv, g, w):
        merged = _sigmoid(gav) * tav + _sigmoid(gbv) * tbv
        return (tav, tbv, merged) + residual_norm(merged, xv, g, w)

    def ffn_tail(gv, uv, xv, g, w):
        act = gv * _sigmoid(gv) * uv
        return (gv, uv, act) + residual_norm(act, xv, g, w)

    stream = [_sds((t, d), F32), _sds((t, d), BF16)]
    ta, tb, merged, x1, h2 = _mm(
        "mixer_out", [ya, yb], [w_a, w_b], "nn", [_sds((t, d), BF16)] * 3 + stream,
        extras=[ga, gb, x, small["norm_ffn"]], wholes=[w_out], separate=True, epilogue=mixer_tail, tm=512)
    gate, up, act, x2, h3 = _mm(
        "ffn", [h2], [w_gate_t, w_up_t], "nt", [_sds((t, dff), BF16)] * 3 + stream,
        extras=[x1, small["norm_ple"]], wholes=[w_down], separate=True, epilogue=ffn_tail, tm=256)
    dx2, dx2_b, d_pp, d_gp, d_norm_final, loss_row, d_norm_ple = _mm(
        "ple_loss", [h3, p], [w_pg, w_pp], "nn", stream + [_sds((t, d), BF16)] * 2,
        extras=[x2, target, small["norm_final"].reshape(1, d), small["norm_ple"]], wholes=[w_pg], separate=True,
        epilogue=_ple_and_loss, sum_shapes=[_sds((1, d), F32)] * 3, tm=512)

    def through_norm(dh, xv, g, dres):
        dx, d_gain = _rms_norm_bwd(dh, xv, g)
        return dx + dres, dx + dres, d_gain

    gain_sum = [_sds((1, d), F32)]
    g_w_pp, g_w_pg = _mm_tn("g_ple", [p, h3], [d_pp, d_gp])

    def ffn_bwd(d_act, gv, uv, xv, g, dres, wg_t, wu_t):
        s = _sigmoid(gv)
        d_gate, d_up = d_act * uv * (s * (1.0 + gv * (1.0 - s))), d_act * (gv * s)
        dh2 = (jnp.dot(d_gate.astype(BF16), wg_t, preferred_element_type=F32)
               + jnp.dot(d_up.astype(BF16), wu_t, preferred_element_type=F32))
        return (d_gate, d_up) + through_norm(dh2, xv, g, dres)

    d_gate, d_up, dx1, dx1_b, d_norm_ffn = _mm(
        "ffn_bwd", [dx2_b], [w_down], "nt", [_sds((t, dff), BF16)] * 2 + stream,
        extras=[gate, up, x1, small["norm_ffn"], dx2], wholes=[w_gate_t, w_up_t], epilogue=ffn_bwd,
        sum_shapes=gain_sum, tm=256)
    g_w_down, = _mm_tn("g_ffn_down", [act], [dx2_b], tmm=512)
    g_w_gate_t, g_w_up_t = _mm_tn("g_ffn_gate_up", [d_gate, d_up], [h2], k_blocks=2)

    def merge_bwd(acc, tav, tbv, gav, gbv):
        sa, sb = _sigmoid(gav), _sigmoid(gbv)
        return acc * sa, acc * sb, acc * tav * sa * (1.0 - sa), acc * tbv * sb * (1.0 - sb)

    big = {
        "w_ffn_gate": g_w_gate_t.reshape(wf["w_ffn_gate"].shape), "w_ffn_up": g_w_up_t.reshape(wf["w_ffn_up"].shape),
        "w_ffn_down": g_w_down.reshape(wf["w_ffn_down"].shape),
        "w_ple_gate": g_w_pg.reshape(wf["w_ple_gate"].shape), "w_ple_proj": g_w_pp,
    }
    (d_ta, d_tb, d_ga, d_gb), theirs = _mm(
        "d_merged", [dx1_b], [w_out], "nt", [_sds((t, d), BF16)] * 4, extras=[ta, tb, ga, gb], epilogue=merge_bwd,
        tm=512, rider=ex.pair(FFN_PLE, big))
    ex.paired(FFN_PLE, big, theirs)
    g_w_out, big["w_branch_a"], big["w_branch_b"] = _mm_tn("g_mixer", [merged, ya, yb], [dx1_b, d_ta, d_tb])
    big["w_out"] = g_w_out.reshape(wf["w_out"].shape)
    (d_ya, d_yb), theirs = _mm(
        "d_branches", [d_ta, d_tb], [w_a, w_b], "nt", [_sds((t, dp), F32), _sds((t, dp), BF16)], separate=True,
        rider=ex.pair(MIXER, big))
    ex.paired(MIXER, big, theirs)
    d_u, g_w_pool, d_pool_scale = _pool_bwd(d_ya, pooled, w_pool_b, small["pool_scale"])
    (d_q, d_k, d_v), landed = _attn_bwd(q, 0, kv, 0, n_pairs, d_yb, n_pairs, rider=ex.chip(LATE))
    ex.landed.update(zip(LATE, landed))
    d_proj = [(d_u, d_q), (d_k, d_v), d_ga, d_gb]
    big["w_in"], = _mm_tn("g_w_in", [h1], d_proj, tmm=512, stacked=True)
    ex.paired(("w_in",), big, _run_exchange("pair_exchange_w_in", ex.pair(("w_in",), big)))
    (grad_x, d_norm_mix), landed = _mm(
        "d_h1", d_proj, [w_in[j] for j in range(N_CHIPS)], "nt", [_sds((t, d), F32)],
        extras=[x, small["norm_mix"], dx1], epilogue=lambda dh, xv, g, dres: through_norm(dh, xv, g, dres)[1:],
        sum_shapes=gain_sum, tm=512, rider=ex.chip(("w_in",)))
    ex.landed.update(zip(("w_in",), landed))
    small_g = {"norm_mix": d_norm_mix, "w_pool": g_w_pool, "pool_scale": d_pool_scale, "norm_ffn": d_norm_ffn,
               "norm_ple": d_norm_ple, "norm_final": d_norm_final}
    return grad_x, big, small_g, loss_row


def _pack_small(small_g, loss_row):
    parts, layout = [], []
    for name in SMALL + ("loss",):
        v = (loss_row if name == "loss" else small_g[name]).reshape(-1, LANES)
        pad = (-v.shape[0]) % 8
        if pad:
            v = jnp.concatenate([v, jnp.zeros((pad, LANES), F32)], axis=0)
        layout.append((name, sum(q.shape[0] for q in parts), v.shape[0]))
        parts.append(v)
    return jnp.concatenate(parts, axis=0), layout


def kernel(x, p, norm_mix, w_in, w_pool, pool_scale, w_branch_a, w_branch_b, w_out, norm_ffn, w_ffn_gate, w_ffn_up, w_ffn_down, norm_ple, w_ple_gate, w_ple_proj, norm_final, loss_target, m_norm_mix, m_w_in, m_w_pool, m_pool_scale, m_w_branch_a, m_w_branch_b, m_w_out, m_norm_ffn, m_w_ffn_gate, m_w_ffn_up, m_w_ffn_down, m_norm_ple, m_w_ple_gate, m_w_ple_proj, m_norm_final, v_norm_mix, v_w_in, v_w_pool, v_pool_scale, v_w_branch_a, v_w_branch_b, v_w_out, v_norm_ffn, v_w_ffn_gate, v_w_ffn_up, v_w_ffn_down, v_norm_ple, v_w_ple_gate, v_w_ple_proj, v_norm_final):
    given = dict(locals())
    order = ("norm_mix", "w_in", "w_pool", "pool_scale", "w_branch_a", "w_branch_b", "w_out", "norm_ffn", "w_ffn_gate",
             "w_ffn_up", "w_ffn_down", "norm_ple", "w_ple_gate", "w_ple_proj", "norm_final")
    t, d = x.shape[1], x.shape[2]
    def local(a, n):
        return jnp.swapaxes(a[0], 0, 1) if n in HELD_TRANSPOSED else a[0]

    def back(a, n):
        return (jnp.swapaxes(a, 0, 1) if n in HELD_TRANSPOSED else a)[None]

    shard = {n: local(given[n], n) for n in BIG}
    small = {"norm_mix": norm_mix, "w_pool": w_pool[0], "pool_scale": pool_scale, "norm_ffn": norm_ffn,
             "norm_ple": norm_ple, "norm_final": norm_final}

    place = jnp.stack([lax.axis_index("c"), 2 * lax.axis_index("x") + lax.axis_index("y")]).astype(jnp.int32)
    ex = _StepExchanges({n: shard[n].astype(BF16) for n in BIG}, place)
    grad_x, _, small_g, loss_row = _local_step(
        x.reshape(t, d), p.reshape(t, p.shape[-1]), loss_target.reshape(t, d), {}, small, ex)
    halves = [_sum_chips(f"chip_sum_{n}", place, ex.pair_sums[n], ex.landed[n]) for n in BIG]
    packed, layout = _pack_small(small_g, loss_row)
    filled, reduced = _finish_gradients(halves, packed)
    grads = dict(zip(BIG, filled))
    for name, start, rows in layout:
        if name == "loss":
            loss = jnp.sum(reduced[start:start + rows])
        else:
            n_el = small[name].size
            grads[name] = reduced[start:start + rows].reshape(-1)[:n_el]

    deltas, new_m, new_v = {}, {}, {}
    for n in order:
        if n in BIG:
            w, m, v = shard[n], local(given["m_" + n], n), local(given["v_" + n], n)
            dl, mn, vn = _adamw(f"adamw_{n}", w, grads[n], m, v)
            grads[n], deltas[n], new_m[n], new_v[n] = [back(a, n) for a in (grads[n], dl, mn, vn)]
        else:
            w, full = small[n], given[n].shape
            shape2 = (1, w.shape[0]) if w.ndim == 1 else (w.shape if w.ndim == 2 else (w.shape[0] * w.shape[1], w.shape[2]))
            dl, mn, vn = _adamw(f"adamw_{n}", w.reshape(shape2), grads[n].reshape(shape2),
                                given["m_" + n].reshape(shape2), given["v_" + n].reshape(shape2))
            grads[n], deltas[n], new_m[n], new_v[n] = [a.reshape(full) for a in (grads[n], dl, mn, vn)]

    return (loss, grad_x.reshape(x.shape), *[grads[n] for n in order], *[deltas[n] for n in order],
            *[new_m[n] for n in order], *[new_v[n] for n in order])
```

```python
import functools
import math

import jax
import jax.numpy as jnp
from jax import lax
from jax.experimental import pallas as pl
from jax.experimental.pallas import tpu as pltpu

F32 = jnp.float32
BF16 = jnp.bfloat16
MESH = pl.DeviceIdType.MESH

RMS_EPS = 1e-6
POOL_WINDOWS = (2, 4, 8, 16)
POOL_HALO = 16
HEAD_DIM = 64
LANES = 128
ATT_BLOCK = 256
ATT_CHAINS = 2
ATT_FWD_CHAINS = 4
ATT_CHUNK = 256
ATT_SLAB = 256
ATT_SCALE = 1.0 / math.sqrt(HEAD_DIM)
LOG2_E = 1.4426950408889634
ATT_EXIT_BELOW = -150.5
ADAM_LR, ADAM_B1, ADAM_B2, ADAM_EPS, ADAM_WD, ADAM_STEP = 0.001, 0.9, 0.999, 1e-08, 0.01, 10
V7X_VMEM_LIMIT_BYTES = 56 * 1024 * 1024
N_CHIPS = 4
N_DEV = 8


def _params(*semantics):
    return pltpu.CompilerParams(dimension_semantics=semantics, vmem_limit_bytes=V7X_VMEM_LIMIT_BYTES)


def _sigmoid(z):
    return 1.0 / (1.0 + jnp.exp(-z))


def _tiled_spec(shape, tm, tn, n_total, at):
    rows, width = shape
    if rows == 1:
        if width == n_total:
            return pl.BlockSpec((1, tn), at(lambda i, j: (0, j)))
        return pl.BlockSpec((1, width), at(lambda i, j: (0, 0)))
    if width == n_total:
        return pl.BlockSpec((tm, tn), at(lambda i, j: (i, j)))
    assert tn == n_total, "an operand narrower than the output needs whole output rows per tile"
    return pl.BlockSpec((tm, width), at(lambda i, j: (i, 0)))


def _column_pieces(operands):
    pieces = [tuple(a) if isinstance(a, (tuple, list)) else (a,) for a in operands]
    return [p for ps in pieces for p in ps], [len(ps) for ps in pieces]


def _load_bf16(refs, counts):
    tiles, k = [], 0
    for n in counts:
        parts = [r[...] for r in refs[k:k + n]]
        parts = [t if t.dtype == BF16 else t.astype(BF16) for t in parts]
        tiles.append(parts[0] if n == 1 else jnp.concatenate(parts, axis=1))
        k += n
    return tiles


def _mm(name, a_list, b_list, mode, out_shapes, epilogue=None, extras=(), tm=1024, tn=None, separate=False,
        sum_shapes=(), rider=None, wholes=()):
    flat_a, counts = _column_pieces(a_list)
    m_total = flat_a[0].shape[0]
    n_total = b_list[0].shape[1] if mode == "nn" else b_list[0].shape[0]
    tn = n_total if tn is None else tn
    tm = min(tm, m_total)
    assert m_total % tm == 0 and n_total % tn == 0 and (not sum_shapes or tn == n_total)
    n_a, n_b, n_extra, n_out = len(counts), len(b_list), len(extras), len(out_shapes)
    assert n_a in (1, n_b)
    dims = (((1,), (0,)), ((), ())) if mode == "nn" else (((1,), (1,)), ((), ()))
    with_rider = rider is not None
    rider = rider or _NoRider()
    grid = (n_total // tn, m_total // tm)

    def at(index):
        return lambda j, i: index(i, j)

    def body(*refs):
        ins, o_refs, _, riding = rider.split(refs, len(flat_a) + n_b + n_extra + len(wholes), n_out + len(sum_shapes))
        a_refs, b_refs = ins[:len(flat_a)], ins[len(flat_a):len(flat_a) + n_b]
        e_refs, w_refs = ins[len(flat_a) + n_b:len(flat_a) + n_b + n_extra], ins[len(flat_a) + n_b + n_extra:]
        at_first = (pl.program_id(0) == 0) & (pl.program_id(1) == 0)
        at_last = (pl.program_id(0) == grid[0] - 1) & (pl.program_id(1) == grid[1] - 1)
        top, bottom = rider.at_steps(riding, at_first, at_first, at_last)
        top()
        lefts = _load_bf16(a_refs, counts)
        products = [lax.dot_general(lefts[s % n_a], b_refs[s][...], dims, preferred_element_type=F32)
                    for s in range(n_b)]
        if not separate:
            products = [functools.reduce(lambda p, r: p + r, products)]
        extra_tiles = [e[...].astype(F32) for e in e_refs]
        outs = products if epilogue is None else epilogue(*products, *extra_tiles, *[w[...] for w in w_refs])
        for o_ref, o in zip(o_refs[:n_out], outs[:n_out]):
            o_ref[...] = o.astype(o_ref.dtype)
        if sum_shapes:
            @pl.when(pl.program_id(1) == 0)
            def _():
                for s_ref in o_refs[n_out:]:
                    s_ref[...] = jnp.zeros_like(s_ref)

            for s_ref, s in zip(o_refs[n_out:], outs[n_out:]):
                s_ref[...] += s
        bottom()

    once = dict(pipeline_mode=pl.Buffered(1)) if tn == n_total else {}
    in_specs = [pl.BlockSpec((tm, a.shape[1]), at(lambda i, j: (i, 0))) for a in flat_a]
    if mode == "nn":
        in_specs += [pl.BlockSpec((b.shape[0], tn), at(lambda i, j: (0, j)), **once) for b in b_list]
    else:
        in_specs += [pl.BlockSpec((tn, b.shape[1]), at(lambda i, j: (j, 0)), **once) for b in b_list]
    in_specs += [_tiled_spec(e.shape, tm, tn, n_total, at) for e in extras]
    in_specs += [pl.BlockSpec(w.shape, lambda j, i: (0, 0), pipeline_mode=pl.Buffered(1)) for w in wholes]
    out_specs = [_tiled_spec(o.shape, tm, tn, n_total, at) for o in out_shapes]
    out_specs += [pl.BlockSpec(s.shape, at(lambda i, j: (0, 0))) for s in sum_shapes]
    semantics = ("arbitrary", "arbitrary") if sum_shapes or rider.operands else ("parallel", "parallel")
    res = pl.pallas_call(
        body, name=name, grid=grid, in_specs=in_specs + [ANY] * len(rider.operands),
        out_specs=out_specs + [ANY] * len(rider.out_shapes),
        out_shape=list(out_shapes) + list(sum_shapes) + list(rider.out_shapes), scratch_shapes=list(rider.scratch),
        compiler_params=_params(*semantics),
    )(*flat_a, *b_list, *extras, *wholes, *rider.operands)
    n_own = len(out_shapes) + len(sum_shapes)
    return (res[:n_own], res[n_own:]) if with_rider else res


def _mm_tn(name, a_list, b_list, tmm=1024, stacked=False, k_blocks=1):
    flat_b, counts = _column_pieces(b_list)
    n_a, n_b = len(a_list), len(counts)
    n_prod = max(n_a, n_b)
    m_total = a_list[0].shape[0]
    ks = [a_list[s % n_a].shape[1] for s in range(n_prod)]
    widths = [sum(p.shape[1] for p in flat_b[sum(counts[:s]):sum(counts[:s + 1])]) for s in range(n_b)]
    widths = [widths[s % n_b] for s in range(n_prod)]
    tmm = min(tmm, m_total)
    assert m_total % tmm == 0 and all(k % k_blocks == 0 for k in ks)
    assert n_a in (1, n_prod) and n_b in (1, n_prod) and not (stacked and n_a > 1)

    def body(*refs):
        a_refs, b_refs, o_refs = refs[:n_a], refs[n_a:n_a + len(flat_b)], refs[n_a + len(flat_b):]

        @pl.when(pl.program_id(1) == 0)
        def _():
            for o_ref in o_refs:
                o_ref[...] = jnp.zeros_like(o_ref)

        lefts, rights = _load_bf16(a_refs, [1] * n_a), _load_bf16(b_refs, counts)
        for s in range(n_prod):
            product = lax.dot_general(lefts[s % n_a], rights[s % n_b], (((0,), (0,)), ((), ())),
                                      preferred_element_type=F32)
            if stacked:
                o_refs[0][s] += product
            else:
                o_refs[s][...] += product

    in_specs = [pl.BlockSpec((tmm, a.shape[1] // k_blocks), lambda kb, m: (m, kb)) for a in a_list]
    in_specs += [pl.BlockSpec((tmm, b.shape[1]), lambda kb, m: (m, 0)) for b in flat_b]
    if stacked:
        out_shape = [jax.ShapeDtypeStruct((n_prod, ks[0], widths[0]), F32)]
        out_specs = [pl.BlockSpec((n_prod, ks[0] // k_blocks, widths[0]), lambda kb, m: (0, kb, 0))]
    else:
        out_shape = [jax.ShapeDtypeStruct((k, w), F32) for k, w in zip(ks, widths)]
        out_specs = [pl.BlockSpec((k // k_blocks, w), lambda kb, m: (kb, 0)) for k, w in zip(ks, widths)]
    return pl.pallas_call(
        body, name=name, grid=(k_blocks, m_total // tmm), in_specs=in_specs, out_specs=out_specs, out_shape=out_shape,
        compiler_params=_params("arbitrary", "arbitrary"),
    )(*a_list, *flat_b)


def _rows(name, fn, ins, tile_outs, sum_outs=(), tr=512, rider=None):
    t_total = max(a.shape[0] for a in ins)
    tr = min(tr, t_total)
    assert t_total % tr == 0
    n_in, n_tile = len(ins), len(tile_outs)
    rider = rider or _NoRider()
    n_steps = t_total // tr

    def body(*refs):
        own_ins, own_outs, _, riding = rider.split(refs, n_in, n_tile + len(sum_outs))
        step = pl.program_id(0)
        top, bottom = rider.at_steps(riding, step == 0, step == n_steps - 1, step == n_steps - 1)
        top()
        refs = tuple(own_ins) + tuple(own_outs)
        outs = fn(*[r[...].astype(F32) for r in refs[:n_in]])
        for o_ref, o in zip(refs[n_in:n_in + n_tile], outs[:n_tile]):
            o_ref[...] = o.astype(o_ref.dtype)
        if sum_outs:
            @pl.when(pl.program_id(0) == 0)
            def _():
                for s_ref in refs[n_in + n_tile:]:
                    s_ref[...] = jnp.zeros_like(s_ref)

            for s_ref, s in zip(refs[n_in + n_tile:], outs[n_tile:]):
                s_ref[...] += s
        bottom()

    def spec(shape):
        if shape[0] == 1:
            return pl.BlockSpec(shape, lambda i: (0, 0))
        return pl.BlockSpec((tr, shape[1]), lambda i: (i, 0))

    return pl.pallas_call(
        body, name=name, grid=(n_steps,), in_specs=[spec(a.shape) for a in ins] + [ANY] * len(rider.operands),
        out_specs=[spec(o.shape) for o in tile_outs] + [spec(s.shape) for s in sum_outs] + [ANY] * len(rider.out_shapes),
        out_shape=list(tile_outs) + list(sum_outs) + list(rider.out_shapes), scratch_shapes=list(rider.scratch),
        compiler_params=_params("arbitrary" if sum_outs or rider.operands else "parallel"),
    )(*ins, *rider.operands)


def _norm_fwd(name, x, gain, rider=None):
    def fn(xv, g):
        inv = lax.rsqrt(jnp.mean(xv * xv, axis=-1, keepdims=True) + RMS_EPS)
        return (xv * inv * g,)

    res = _rows(name, fn, [x, gain], [jax.ShapeDtypeStruct(x.shape, BF16)], rider=rider)
    return res[0], res[1:]


def _rms_norm_bwd(dh, xv, g):
    inv = lax.rsqrt(jnp.mean(xv * xv, axis=-1, keepdims=True) + RMS_EPS)
    xn = xv * inv
    dxn = dh * g
    return inv * (dxn - xn * jnp.mean(dxn * xn, axis=-1, keepdims=True)), jnp.sum(dh * xn, axis=0, keepdims=True)


def _ple_and_loss(gv, pv, x2v, tv, g_final, g_ple, w_pg):
    d = x2v.shape[1]
    s = _sigmoid(gv)
    xv = x2v + s * pv
    inv = lax.rsqrt(jnp.mean(xv * xv, axis=-1, keepdims=True) + RMS_EPS)
    err = xv * inv * g_final - tv
    dx3, d_final = _rms_norm_bwd(err * (1.0 / d), xv, g_final)
    d_pp, d_gp = dx3 * s, dx3 * pv * s * (1.0 - s)
    dh3 = lax.dot_general(d_gp.astype(BF16), w_pg, (((1,), (1,)), ((), ())), preferred_element_type=F32)
    dx2, d_ple = _rms_norm_bwd(dh3, x2v, g_ple)
    dx2 = dx2 + dx3
    return dx2, dx2, d_pp, d_gp, d_final, (0.5 / d) * jnp.sum(err * err, axis=0, keepdims=True), d_ple


def _window_counts(t_pos, w):
    return jnp.minimum(t_pos + 1, w).astype(F32)


def _pool_fwd(u, w_pool, scale, tr=512):
    t_total, width = u.shape
    tr = min(tr, t_total)
    n_groups = len(POOL_WINDOWS)
    gdim = width // n_groups
    ext = tr + POOL_HALO

    def body(u_ref, halo_ref, w_ref, s_ref, pooled_ref, ya_ref):
        i = pl.program_id(0)
        halo = jnp.where(i == 0, 0.0, halo_ref[...])
        t_pos = i * tr + lax.broadcasted_iota(jnp.int32, (tr, 1), 0)
        for g, w in enumerate(POOL_WINDOWS):
            cols = slice(g * gdim, (g + 1) * gdim)
            main = u_ref[:, cols]
            win = jnp.concatenate([halo[:, cols], main], axis=0)
            span = 1
            while span < w:
                win = win + pltpu.roll(win, span, 0)
                span *= 2
            pooled = win[POOL_HALO:, :] * (1.0 / _window_counts(t_pos, w)) - main
            pooled_b = pooled.astype(BF16)
            pooled_ref[:, cols] = pooled_b
            mixed = jnp.dot(pooled_b, w_ref[g], preferred_element_type=F32)
            ya_ref[:, cols] = (mixed * s_ref[:, cols]).astype(BF16)

    hb = tr // POOL_HALO
    return pl.pallas_call(
        body, name="pool_fwd", grid=(t_total // tr,),
        in_specs=[pl.BlockSpec((tr, width), lambda i: (i, 0)),
                  pl.BlockSpec((POOL_HALO, width), lambda i: (jnp.maximum(i * hb - 1, 0), 0)),
                  pl.BlockSpec((n_groups, gdim, gdim), lambda i: (0, 0, 0)),
                  pl.BlockSpec((1, width), lambda i: (0, 0))],
        out_specs=[pl.BlockSpec((tr, width), lambda i: (i, 0)), pl.BlockSpec((tr, width), lambda i: (i, 0))],
        out_shape=[jax.ShapeDtypeStruct(u.shape, BF16), jax.ShapeDtypeStruct(u.shape, BF16)],
        compiler_params=_params("parallel"),
    )(u, u, w_pool, scale)


def _pool_bwd(dya, pooled, w_pool, scale, tr=512):
    t_total, width = dya.shape
    tr = min(tr, t_total)
    n_groups = len(POOL_WINDOWS)
    gdim = width // n_groups
    ext = tr + POOL_HALO
    n_tiles = t_total // tr

    def body(d_ref, halo_ref, p_ref, w_ref, s_ref, du_ref, dw_ref, ds_ref):
        i = pl.program_id(0)

        @pl.when(i == 0)
        def _():
            dw_ref[...] = jnp.zeros_like(dw_ref)
            ds_ref[...] = jnp.zeros_like(ds_ref)

        halo = jnp.where(i == n_tiles - 1, 0.0, halo_ref[...])
        t_pos = i * tr + lax.broadcasted_iota(jnp.int32, (ext, 1), 0)
        for g, w in enumerate(POOL_WINDOWS):
            cols = slice(g * gdim, (g + 1) * gdim)
            sc = s_ref[:, cols]
            d_main = d_ref[:, cols]
            pooled_b = p_ref[:, cols]
            mixed = jnp.dot(pooled_b, w_ref[g], preferred_element_type=F32)
            ds_ref[:, cols] += jnp.sum(d_main * mixed, axis=0, keepdims=True)
            dmix = (jnp.concatenate([d_main, halo[:, cols]], axis=0) * sc).astype(BF16)
            dw_ref[g] += lax.dot_general(pooled_b, dmix[:tr, :], (((0,), (0,)), ((), ())),
                                         preferred_element_type=F32)
            dpool = lax.dot_general(dmix, w_ref[g], (((1,), (1,)), ((), ())), preferred_element_type=F32)
            win = dpool * (1.0 / _window_counts(t_pos, w))
            span = 1
            while span < w:
                win = win + pltpu.roll(win, ext - span, 0)
                span *= 2
            du_ref[:, cols] = (win[:tr, :] - dpool[:tr, :]).astype(BF16)

    hb = tr // POOL_HALO
    last_halo = t_total // POOL_HALO - 1
    return pl.pallas_call(
        body, name="pool_bwd", grid=(n_tiles,),
        in_specs=[pl.BlockSpec((tr, width), lambda i: (i, 0)),
                  pl.BlockSpec((POOL_HALO, width), lambda i: (jnp.minimum((i + 1) * hb, last_halo), 0)),
                  pl.BlockSpec((tr, width), lambda i: (i, 0)),
                  pl.BlockSpec((n_groups, gdim, gdim), lambda i: (0, 0, 0)),
                  pl.BlockSpec((1, width), lambda i: (0, 0))],
        out_specs=[pl.BlockSpec((tr, width), lambda i: (i, 0)),
                   pl.BlockSpec((n_groups, gdim, gdim), lambda i: (0, 0, 0)),
                   pl.BlockSpec((1, width), lambda i: (0, 0))],
        out_shape=[jax.ShapeDtypeStruct(dya.shape, BF16), jax.ShapeDtypeStruct((n_groups, gdim, gdim), F32),
                   jax.ShapeDtypeStruct((1, width), F32)],
        compiler_params=_params("arbitrary"),
    )(dya, dya, pooled, w_pool, scale)


def _head_masks():
    lane = lax.broadcasted_iota(jnp.int32, (1, LANES), 1)
    return lane < HEAD_DIM


def _stack_heads(tile, first):
    zero = jnp.zeros_like(tile)
    return jnp.concatenate([jnp.where(first, tile, zero), jnp.where(first, zero, tile)], axis=0)


def _causal_mask(t_pos, k_start):
    col = lax.broadcasted_iota(jnp.int32, (1, 2 * ATT_SLAB), 1)
    return k_start + (col & (ATT_SLAB - 1)) < t_pos


def _slab_scores(q, kd, mask):
    z2 = lax.dot_general(q, kd, (((1,), (1,)), ((), ())), preferred_element_type=F32) * LOG2_E
    log_hit = jnp.minimum(z2, 0.0) - jnp.log2(1.0 + jnp.exp2(-jnp.abs(z2)))
    log_fail = log_hit - z2
    return log_hit, (log_fail if mask is None else jnp.where(mask, log_fail, 0.0))


def _weights(log_hit, suffix, mask):
    arg = log_hit + suffix
    return jnp.exp2(arg if mask is None else jnp.where(mask, arg, -1e30))


def _tri(upper):
    r = lax.broadcasted_iota(jnp.int32, (ATT_CHUNK, ATT_CHUNK), 0)
    c = lax.broadcasted_iota(jnp.int32, (ATT_CHUNK, ATT_CHUNK), 1)
    return jnp.where(r > c if upper else r < c, 1.0, 0.0).astype(BF16)


def _tri_spec():
    return pl.BlockSpec((ATT_CHUNK, ATT_CHUNK), lambda h, i: (0, 0), pipeline_mode=pl.Buffered(1))


def _scan_chunk(v, tri):
    return jnp.dot(v.astype(BF16), tri, preferred_element_type=F32)


def _lane_bcast(col):
    return jnp.broadcast_to(col, (col.shape[0], LANES))


def _scan_slab(v, tri, carries, from_right):
    n_chunks = ATT_SLAB // ATT_CHUNK
    edge = 0 if from_right else ATT_CHUNK - 1
    parts, new_carries = [None] * (2 * n_chunks), []
    for head in range(2):
        run = carries[head]
        for c in (reversed(range(n_chunks)) if from_right else range(n_chunks)):
            lo_col = head * ATT_SLAB + c * ATT_CHUNK
            vc = v[:, lo_col:lo_col + ATT_CHUNK]
            sc = _scan_chunk(vc, tri)
            parts[head * n_chunks + c] = sc + jnp.concatenate([run] * (ATT_CHUNK // LANES), axis=1)
            run = run + _lane_bcast(sc[:, edge:edge + 1] + vc[:, edge:edge + 1])
        new_carries.append(run)
    return jnp.concatenate(parts, axis=1), new_carries


def _fold_heads(stacked, first):
    s = stacked.shape[0] // 2
    return jnp.where(first, stacked[:s], stacked[s:])


class _NoRider:
    operands, out_shapes, scratch = (), (), ()

    def split(self, refs, n_base_in, n_base_out):
        n_in, n_out, n_sem = len(self.operands), len(self.out_shapes), len(self.scratch)
        a = n_base_in + n_in
        b = a + n_base_out + n_out
        mine = (refs[n_base_in:a], refs[a + n_base_out:b], refs[b:b + n_sem])
        return refs[:n_base_in], refs[a:a + n_base_out], refs[b + n_sem:], mine

    def start(self, ins, outs, sems):
        pass

    def relay(self, ins, outs, sems):
        pass

    def finish(self, ins, outs, sems):
        pass

    def at_steps(self, refs, first_step, relay_step, last_step):
        if not self.operands:
            return (lambda: None), (lambda: None)

        def top():
            pl.when(first_step)(lambda: self.start(*refs))
            pl.when(relay_step)(lambda: self.relay(*refs))

        return top, lambda: pl.when(last_step)(lambda: self.finish(*refs))


def _attn_fwd(q_src, q_col, kv_src, k_col, v_col, n_pairs=4, rider=_NoRider()):
    t_total = q_src.shape[0]
    blk = ATT_BLOCK
    n_chains = ATT_FWD_CHAINS if t_total % (ATT_FWD_CHAINS * blk) == 0 else ATT_CHAINS
    n_steps = t_total // (n_chains * blk)
    assert t_total % ATT_SLAB == 0 and ATT_SLAB == ATT_BLOCK

    def body(*refs):
        (q_ref, k_ref, v_ref, suffix_ref), (o_ref,), _, riding = rider.split(refs, 4, 1)
        h, ii = pl.program_id(0), pl.program_id(1)
        top, bottom = rider.at_steps(riding, (h == 0) & (ii == 0), (h == n_pairs - 1) & (ii == 0),
                                     (h == n_pairs - 1) & (ii == n_steps - 1))
        top()
        first = _head_masks()
        suffix_tri = suffix_ref[...]
        blocks = [n_chains * ii + c for c in range(n_chains)]
        qs = [q_ref[c * blk:(c + 1) * blk, :] * ATT_SCALE for c in range(n_chains)]
        t_pos = [b * blk + lax.broadcasted_iota(jnp.int32, (blk, 1), 0) for b in blocks]

        def one(c, t, chain, on_diagonal):
            _, acc, right_a, right_b = chain
            k_start = pl.multiple_of((blocks[c] - t) * ATT_SLAB, ATT_SLAB)
            kd = _stack_heads(k_ref[pl.ds(k_start, ATT_SLAB), :], first)
            vd = _stack_heads(v_ref[pl.ds(k_start, ATT_SLAB), :], first)
            mask = _causal_mask(t_pos[c], k_start) if on_diagonal else None
            log_hit, log_fail = _slab_scores(qs[c], kd, mask)
            suffix, (right_a, right_b) = _scan_slab(log_fail, suffix_tri, (right_a, right_b), from_right=True)
            a = _weights(log_hit, suffix, mask).astype(BF16)
            acc = acc + jnp.dot(a, vd, preferred_element_type=F32)
            return jnp.max(jnp.maximum(right_a, right_b)), acc, right_a, right_b

        def step(state, on_diagonal):
            t, chains = state
            return t + 1, tuple(one(c, t, chains[c], on_diagonal) for c in range(n_chains))

        def more(state):
            t, chains = state
            return (t <= blocks[0]) & (functools.reduce(jnp.maximum, [ch[0] for ch in chains]) > ATT_EXIT_BELOW)

        zero = jnp.zeros((blk, LANES), F32)
        state = step((0, ((jnp.float32(0.0), zero, zero, zero),) * n_chains), on_diagonal=True)
        t, chains = lax.while_loop(more, functools.partial(step, on_diagonal=False), state)
        for c in range(n_chains):
            chain = chains[c]
            if c:
                _, chain = lax.while_loop(
                    lambda s, c=c: (s[0] <= blocks[c]) & (s[1][0] > ATT_EXIT_BELOW),
                    lambda s, c=c: (s[0] + 1, one(c, s[0], s[1], False)), (t, chain))
            o_ref[c * blk:(c + 1) * blk, :] = chain[1].astype(BF16)
        bottom()

    rows = n_chains * blk
    res = pl.pallas_call(
        body, name="attn_fwd", grid=(n_pairs, n_steps),
        in_specs=[pl.BlockSpec((rows, LANES), lambda h, i: (i, q_col + h)),
                  pl.BlockSpec((t_total, LANES), lambda h, i: (0, k_col + h)),
                  pl.BlockSpec((t_total, LANES), lambda h, i: (0, v_col + h)), _tri_spec()] + [ANY] * len(rider.operands),
        out_specs=[pl.BlockSpec((rows, LANES), lambda h, i: (i, h))] + [ANY] * len(rider.out_shapes),
        out_shape=[jax.ShapeDtypeStruct((t_total, n_pairs * LANES), BF16)] + list(rider.out_shapes),
        scratch_shapes=list(rider.scratch),
        compiler_params=_params("arbitrary", "arbitrary"),
    )(q_src, kv_src, kv_src, _tri(upper=True), *rider.operands)
    return res[0], res[1:]


def _attn_bwd(q_src, q_col, kv_src, k_col, v_col, dy, n_pairs=4, rider=_NoRider()):
    t_total = q_src.shape[0]
    blk = ATT_BLOCK
    n_steps = t_total // (ATT_CHAINS * blk)
    n_slabs = t_total // ATT_SLAB
    assert t_total % ATT_SLAB == 0 and ATT_SLAB == ATT_BLOCK

    def body(*refs):
        ins, (dq_ref, dk_ref, dv_ref), (g_s, dk_acc, dv_acc), riding = rider.split(refs, 6, 3)
        q_ref, dy_ref, k_ref, v_ref, suffix_ref, prefix_ref = ins
        h, ii = pl.program_id(0), pl.program_id(1)
        top, bottom = rider.at_steps(riding, (h == 0) & (ii == 0), (h == n_pairs - 1) & (ii == 0),
                                     (h == n_pairs - 1) & (ii == n_steps - 1))
        top()

        @pl.when(ii == 0)
        def _():
            dk_acc[...] = jnp.zeros_like(dk_acc)
            dv_acc[...] = jnp.zeros_like(dv_acc)

        first = _head_masks()
        suffix_tri = suffix_ref[...]
        prefix_tri = prefix_ref[...]
        blocks = [ATT_CHAINS * ii + c for c in range(ATT_CHAINS)]
        rows = [slice(c * blk, (c + 1) * blk) for c in range(ATT_CHAINS)]
        qs = [q_ref[r, :] * ATT_SCALE for r in rows]
        dys = [dy_ref[r, :] for r in rows]
        t_pos = [b * blk + lax.broadcasted_iota(jnp.int32, (blk, 1), 0) for b in blocks]

        def one1(c, t, chain, on_diagonal):
            _, right_a, right_b = chain
            slab = blocks[c] - t
            k_start = pl.multiple_of(slab * ATT_SLAB, ATT_SLAB)
            kd = _stack_heads(k_ref[pl.ds(k_start, ATT_SLAB), :], first)
            vd = _stack_heads(v_ref[pl.ds(k_start, ATT_SLAB), :], first)
            mask = _causal_mask(t_pos[c], k_start) if on_diagonal else None
            log_hit, log_fail = _slab_scores(qs[c], kd, mask)
            suffix, (right_a, right_b) = _scan_slab(log_fail, suffix_tri, (right_a, right_b), from_right=True)
            a = _weights(log_hit, suffix, mask)
            da = lax.dot_general(dys[c], vd, (((1,), (1,)), ((), ())), preferred_element_type=F32)
            g_s[c, slab] = (da * a).astype(BF16)
            dv_acc[pl.ds(k_start, ATT_SLAB), :] += _fold_heads(lax.dot_general(
                a.astype(BF16), dys[c], (((0,), (0,)), ((), ())), preferred_element_type=F32), first)
            return jnp.max(jnp.maximum(right_a, right_b)), right_a, right_b

        def step1(state, on_diagonal):
            t, chains = state
            return t + 1, tuple(one1(c, t, chains[c], on_diagonal) for c in range(ATT_CHAINS))

        def more(state):
            t, chains = state
            return (t <= blocks[0]) & (functools.reduce(jnp.maximum, [ch[0] for ch in chains]) > ATT_EXIT_BELOW)

        zero = jnp.zeros((blk, LANES), F32)
        state = step1((0, ((jnp.float32(0.0), zero, zero),) * ATT_CHAINS), on_diagonal=True)
        joint, chains = lax.while_loop(more, functools.partial(step1, on_diagonal=False), state)
        done = [joint]
        for c in range(1, ATT_CHAINS):
            done.append(lax.while_loop(
                lambda s, c=c: (s[0] <= blocks[c]) & (s[1][0] > ATT_EXIT_BELOW),
                lambda s, c=c: (s[0] + 1, one1(c, s[0], s[1], False)), (joint, chains[c]))[0])

        def one2(c, t, carry, on_diagonal):
            dq, left_a, left_b = carry
            slab = blocks[c] - t
            k_start = pl.multiple_of(slab * ATT_SLAB, ATT_SLAB)
            kd = _stack_heads(k_ref[pl.ds(k_start, ATT_SLAB), :], first)
            g = g_s[c, slab]
            z2 = lax.dot_general(qs[c], kd, (((1,), (1,)), ((), ())), preferred_element_type=F32) * LOG2_E
            sig = 1.0 / (1.0 + jnp.exp2(-z2))
            prefix, (left_a, left_b) = _scan_slab(g, prefix_tri, (left_a, left_b), from_right=False)
            dz = g * (1.0 - sig) - sig * prefix
            if on_diagonal:
                dz = jnp.where(_causal_mask(t_pos[c], k_start), dz, 0.0)
            dz = dz.astype(BF16)
            dq = dq + jnp.dot(dz, kd, preferred_element_type=F32)
            dk_acc[pl.ds(k_start, ATT_SLAB), :] += _fold_heads(lax.dot_general(
                dz, qs[c], (((0,), (0,)), ((), ())), preferred_element_type=F32), first)
            return dq, left_a, left_b

        carries = [(zero, zero, zero)]
        for c in range(1, ATT_CHAINS):
            carries.append(lax.fori_loop(
                0, done[c] - joint, lambda n, carry, c=c: one2(c, done[c] - 1 - n, carry, False), (zero, zero, zero)))
        carries = lax.fori_loop(
            0, joint - 1,
            lambda n, cs: tuple(one2(c, joint - 1 - n, cs[c], False) for c in range(ATT_CHAINS)), tuple(carries))
        for c in range(ATT_CHAINS):
            dq_ref[rows[c], :] = (one2(c, 0, carries[c], True)[0] * ATT_SCALE).astype(BF16)

        @pl.when(ii == n_steps - 1)
        def _():
            dk_ref[...] = dk_acc[...].astype(BF16)
            dv_ref[...] = dv_acc[...].astype(BF16)

        bottom()

    out = jax.ShapeDtypeStruct((t_total, n_pairs * LANES), BF16)
    n_rows = ATT_CHAINS * blk
    whole = dict(pipeline_mode=pl.Buffered(1))
    res = pl.pallas_call(
        body, name="attn_bwd", grid=(n_pairs, n_steps),
        in_specs=[pl.BlockSpec((n_rows, LANES), lambda h, i: (i, q_col + h)),
                  pl.BlockSpec((n_rows, LANES), lambda h, i: (i, h)),
                  pl.BlockSpec((t_total, LANES), lambda h, i: (0, k_col + h), **whole),
                  pl.BlockSpec((t_total, LANES), lambda h, i: (0, v_col + h), **whole), _tri_spec(), _tri_spec()]
        + [ANY] * len(rider.operands),
        out_specs=[pl.BlockSpec((n_rows, LANES), lambda h, i: (i, h)),
                   pl.BlockSpec((t_total, LANES), lambda h, i: (0, h)),
                   pl.BlockSpec((t_total, LANES), lambda h, i: (0, h))] + [ANY] * len(rider.out_shapes),
        out_shape=[out, out, out] + list(rider.out_shapes),
        scratch_shapes=list(rider.scratch) + [pltpu.VMEM((ATT_CHAINS, n_slabs, blk, 2 * ATT_SLAB), BF16),
                                              pltpu.VMEM((t_total, LANES), F32), pltpu.VMEM((t_total, LANES), F32)],
        compiler_params=_params("arbitrary", "arbitrary"),
    )(q_src, dy, kv_src, kv_src, _tri(upper=True), _tri(upper=False), *rider.operands)
    return res[:3], res[3:]


def _adamw(name, w, g, m, v):
    def fn(wv, gv, mv, vv):
        mn = ADAM_B1 * mv + (1.0 - ADAM_B1) * gv
        vn = ADAM_B2 * vv + (1.0 - ADAM_B2) * (gv * gv)
        m_hat = mn / (1.0 - ADAM_B1 ** ADAM_STEP)
        v_hat = vn / (1.0 - ADAM_B2 ** ADAM_STEP)
        return -ADAM_LR * (m_hat / (jnp.sqrt(v_hat) + ADAM_EPS) + ADAM_WD * wv), mn, vn

    rows = w.shape[0]
    tr = _row_tile(rows)
    shp = jax.ShapeDtypeStruct(w.shape, F32)
    if rows == 1:
        def body(w_ref, g_ref, m_ref, v_ref, d_ref, mo_ref, vo_ref):
            d, mn, vn = fn(w_ref[...], g_ref[...], m_ref[...], v_ref[...])
            d_ref[...], mo_ref[...], vo_ref[...] = d, mn, vn

        return pl.pallas_call(body, name=name, out_shape=[shp, shp, shp])(w, g, m, v)
    return _rows(name, fn, [w, g, m, v], [shp, shp, shp], tr=tr)


def _place():
    return lax.axis_index("x"), lax.axis_index("y"), lax.axis_index("c")


def _other_chips(x, y):
    return [(1 - x, y), (x, 1 - y), (1 - x, 1 - y)]


ANY = pl.BlockSpec(memory_space=pl.ANY)


def _remote(src, dst, send_sem, recv_sem, to):
    return pltpu.make_async_remote_copy(src_ref=src, dst_ref=dst, send_sem=send_sem, recv_sem=recv_sem,
                                        device_id=to, device_id_type=MESH)


class _WeightGather(_NoRider):
    def __init__(self, shards):
        n_w = len(shards)
        self.operands = list(shards)
        self.out_shapes = [jax.ShapeDtypeStruct((N_CHIPS,) + s.shape, s.dtype) for s in shards]
        self.scratch = [pltpu.SemaphoreType.DMA((3, n_w))] * 4 + [pltpu.SemaphoreType.DMA((n_w,))] * 2

    def _copies(self, ins, outs, sems):
        send_sems, recv_sems, relay_send, relay_recv, own_send, own_recv = sems
        x, y, c = _place()
        my_chip, sibling = 2 * x + y, (x, y, 1 - c)
        n_w = len(ins)

        def half(w, chip, core):
            h = self.operands[w].shape[0] // 2
            return outs[w].at[chip, pl.ds(core * h, h)]

        own = [_remote(ins[w], outs[w].at[my_chip], own_send.at[w], own_recv.at[w], sibling) for w in range(n_w)]
        sends, landed, relays, relayed = [], [], [], []
        for p, (ox, oy) in enumerate(_other_chips(x, y)):
            for w in range(n_w):
                h = self.operands[w].shape[0] // 2
                sends.append(_remote(ins[w].at[pl.ds(c * h, h)], half(w, my_chip, c), send_sems.at[p, w],
                                     recv_sems.at[p, w], (ox, oy, c)))
                here = half(w, 2 * ox + oy, c)
                landed.append(_remote(here, here, send_sems.at[p, w], recv_sems.at[p, w], (ox, oy, c)))
                relays.append(_remote(here, here, relay_send.at[p, w], relay_recv.at[p, w], sibling))
                there = half(w, 2 * ox + oy, 1 - c)
                relayed.append(_remote(there, there, relay_send.at[p, w], relay_recv.at[p, w], sibling))
        return own, sends, landed, relays, relayed

    def start(self, ins, outs, sems):
        own, sends, _, _, _ = self._copies(ins, outs, sems)
        for cp in own + sends:
            cp.start()

    def relay(self, ins, outs, sems):
        _, _, landed, relays, _ = self._copies(ins, outs, sems)
        for arrival, cp in zip(landed, relays):
            arrival.wait_recv()
            cp.start()

    def finish(self, ins, outs, sems):
        own, sends, _, relays, relayed = self._copies(ins, outs, sems)
        for arrival in relayed:
            arrival.wait_recv()
        for cp in sends + relays:
            cp.wait_send()
        for cp in own:
            cp.wait()


class _ChipExchange(_NoRider):
    def __init__(self, pair_sums):
        n_w = len(pair_sums)
        self.operands = list(pair_sums)
        self.out_shapes = [jax.ShapeDtypeStruct((3,) + s.shape[1:], s.dtype) for s in pair_sums]
        self.scratch = [pltpu.SemaphoreType.DMA((3, n_w))] * 2

    def _copies(self, ins, outs, sems):
        send_sems, recv_sems = sems
        x, y, c = _place()
        return [_remote(ins[w].at[2 * ox + oy], outs[w].at[p], send_sems.at[p, w], recv_sems.at[p, w], (ox, oy, c))
                for p, (ox, oy) in enumerate(_other_chips(x, y)) for w in range(len(ins))]

    def start(self, ins, outs, sems):
        for cp in self._copies(ins, outs, sems):
            cp.start()

    def finish(self, ins, outs, sems):
        for cp in self._copies(ins, outs, sems):
            cp.wait()


class _PairExchange(_NoRider):
    def __init__(self, grads):
        n_w = len(grads)
        self.operands = list(grads)
        self.out_shapes = [jax.ShapeDtypeStruct(g.shape[:-2] + (g.shape[-2] // 2, g.shape[-1]), F32) for g in grads]
        self.scratch = [pltpu.SemaphoreType.DMA((n_w,))] * 2

    def _copies(self, ins, theirs, sems):
        send_sems, recv_sems = sems
        x, y, c = _place()
        sends = []
        for w, g in enumerate(self.operands):
            rows = pl.ds((1 - c) * (g.shape[-2] // 2), g.shape[-2] // 2)
            src = ins[w].at[:, rows, :] if g.ndim == 3 else ins[w].at[rows, :]
            sends.append(_remote(src, theirs[w], send_sems.at[w], recv_sems.at[w], (x, y, 1 - c)))
        return sends

    def start(self, ins, outs, sems):
        for cp in self._copies(ins, outs, sems):
            cp.start()

    def finish(self, ins, outs, sems):
        for cp in self._copies(ins, outs, sems):
            cp.wait()


def _run_exchange(name, plan):
    n_in, n_out = len(plan.operands), len(plan.out_shapes)
    if not n_in:
        return ()

    def body(*refs):
        parts = (refs[:n_in], refs[n_in:n_in + n_out], refs[n_in + n_out:])
        plan.start(*parts)
        plan.relay(*parts)
        plan.finish(*parts)

    return pl.pallas_call(body, name=name, in_specs=[ANY] * n_in, out_specs=[ANY] * n_out,
                          out_shape=list(plan.out_shapes), scratch_shapes=list(plan.scratch))(*plan.operands)


class _NoExchanges:
    pair_sums, landed = {}, {}

    def gather(self, names):
        return _NoRider()

    def pair(self, names, grads):
        return _NoRider()

    def paired(self, names, grads, theirs):
        pass

    def chip(self, names):
        return _NoRider()


class _StepExchanges(_NoExchanges):
    def __init__(self, shards_bf16, place):
        self.shards, self.place = shards_bf16, place
        self.pair_sums, self.landed = {}, {}

    def gather(self, names):
        return _WeightGather([self.shards[n] for n in names])

    def pair(self, names, grads):
        return _PairExchange([grads[n] for n in names])

    def paired(self, names, grads, theirs):
        for n, other in zip(names, theirs):
            self.pair_sums[n] = _pair_sum(f"pair_sum_{n}", self.place, grads[n], other)

    def chip(self, names):
        return _ChipExchange([self.pair_sums[n] for n in names])


def _finish_gradients(shards, vec):
    n_w = len(shards)
    rows = vec.shape[0]

    def body(*refs):
        outs, v_ref, o_ref = refs[n_w + 1:2 * n_w + 1], refs[n_w], refs[2 * n_w + 1]
        half_send, half_recv, slots, core_sums, vec_send, vec_recv, sum_send, sum_recv = refs[2 * n_w + 2:]
        x, y, c = _place()
        my_chip, sibling = 2 * x + y, (x, y, 1 - c)
        halves = []
        for w in range(n_w):
            h = shards[w].shape[0] // 2
            mine, theirs = outs[w].at[pl.ds(c * h, h)], outs[w].at[pl.ds((1 - c) * h, h)]
            halves.append((_remote(mine, mine, half_send.at[w], half_recv.at[w], sibling),
                           _remote(theirs, theirs, half_send.at[w], half_recv.at[w], sibling)))
        slots[my_chip] = v_ref[...]
        spread = []
        for p, (ox, oy) in enumerate(_other_chips(x, y)):
            landed = slots.at[2 * ox + oy]
            spread.append((_remote(v_ref, slots.at[my_chip], vec_send.at[p], vec_recv.at[p], (ox, oy, c)),
                           _remote(landed, landed, vec_send.at[p], vec_recv.at[p], (ox, oy, c))))
        for send, _ in halves + spread:
            send.start()
        for send, arrival in spread:
            arrival.wait_recv()
            send.wait_send()
        core_sums[c] = functools.reduce(lambda total, chip: total + slots[chip], range(1, N_CHIPS), slots[0])
        mine, theirs = core_sums.at[c], core_sums.at[1 - c]
        to_sibling = _remote(mine, mine, sum_send.at[0], sum_recv.at[0], sibling)
        to_sibling.start()
        _remote(theirs, theirs, sum_send.at[0], sum_recv.at[0], sibling).wait_recv()
        to_sibling.wait_send()
        o_ref[...] = core_sums[0] + core_sums[1]
        for send, arrival in halves:
            arrival.wait_recv()
            send.wait_send()

    vm = pl.BlockSpec(memory_space=pltpu.VMEM)
    res = pl.pallas_call(
        body, name="finish_gradients", in_specs=[ANY] * n_w + [vm], out_specs=[ANY] * n_w + [vm],
        out_shape=[jax.ShapeDtypeStruct(s.shape, s.dtype) for s in shards] + [jax.ShapeDtypeStruct(vec.shape, F32)],
        input_output_aliases={w: w for w in range(n_w)},
        scratch_shapes=[pltpu.SemaphoreType.DMA((n_w,)), pltpu.SemaphoreType.DMA((n_w,)),
                        pltpu.VMEM((N_CHIPS, rows, LANES), F32), pltpu.VMEM((2, rows, LANES), F32),
                        pltpu.SemaphoreType.DMA((N_CHIPS - 1,)), pltpu.SemaphoreType.DMA((N_CHIPS - 1,)),
                        pltpu.SemaphoreType.DMA((1,)), pltpu.SemaphoreType.DMA((1,))],
    )(*shards, vec)
    return res[:n_w], res[n_w]


def _row_tile(rows):
    fits = [tr for tr in range(16, min(rows, 512) + 1, 16) if rows % tr == 0]
    return max(fits) if fits else rows


def _pair_sum(name, place, grad, theirs):
    if grad.ndim == 2:
        return _pair_sum_joined(name, place, grad, theirs)
    n, r, c = grad.shape
    half = r // 2
    tr = _row_tile(half)
    nb = half // tr

    def body(place_ref, g_ref, t_ref, o_ref):
        o_ref[...] = (g_ref[...] + t_ref[...]).astype(BF16)

    return pl.pallas_call(
        body, name=name, out_shape=jax.ShapeDtypeStruct((n, half, c), BF16),
        grid_spec=pltpu.PrefetchScalarGridSpec(
            num_scalar_prefetch=1, grid=(n, nb),
            in_specs=[pl.BlockSpec((1, tr, c), lambda j, i, pr: (j, pr[0] * nb + i, 0)),
                      pl.BlockSpec((1, tr, c), lambda j, i, pr: (j, i, 0))],
            out_specs=pl.BlockSpec((1, tr, c), lambda j, i, pr: (j, i, 0))),
        compiler_params=_params("parallel", "parallel"),
    )(place, grad, theirs)


def _pair_sum_joined(name, place, grad, theirs):
    r, wide = grad.shape
    half, c = r // 2, wide // N_CHIPS
    tr = _row_tile(half)
    nb = half // tr

    def body(place_ref, g_ref, t_ref, o_ref):
        for j in range(N_CHIPS):
            cols = slice(j * c, (j + 1) * c)
            o_ref[j] = (g_ref[:, cols] + t_ref[:, cols]).astype(BF16)

    return pl.pallas_call(
        body, name=name, out_shape=jax.ShapeDtypeStruct((N_CHIPS, half, c), BF16),
        grid_spec=pltpu.PrefetchScalarGridSpec(
            num_scalar_prefetch=1, grid=(nb,),
            in_specs=[pl.BlockSpec((tr, wide), lambda i, pr: (pr[0] * nb + i, 0)),
                      pl.BlockSpec((tr, wide), lambda i, pr: (i, 0))],
            out_specs=pl.BlockSpec((N_CHIPS, tr, c), lambda i, pr: (0, i, 0))),
        compiler_params=_params("parallel"),
    )(place, grad, theirs)


def _sum_chips(name, place, pair_sums, landed):
    _, half, c = pair_sums.shape
    tr = _row_tile(half)
    nb = half // tr

    def body(place_ref, s_ref, q_ref, o_ref):
        total = s_ref[0].astype(F32)
        for p in range(3):
            total = total + q_ref[p].astype(F32)
        o_ref[...] = total

    return pl.pallas_call(
        body, name=name, out_shape=jax.ShapeDtypeStruct((2 * half, c), F32),
        grid_spec=pltpu.PrefetchScalarGridSpec(
            num_scalar_prefetch=1, grid=(nb,),
            in_specs=[pl.BlockSpec((1, tr, c), lambda i, pr: (pr[1], i, 0)),
                      pl.BlockSpec((3, tr, c), lambda i, pr: (0, i, 0))],
            out_specs=pl.BlockSpec((tr, c), lambda i, pr: (pr[0] * nb + i, 0))),
        compiler_params=_params("parallel"),
    )(place, pair_sums, landed)


MIXER = ("w_branch_a", "w_branch_b", "w_out")
FFN_PLE = ("w_ffn_gate", "w_ffn_up", "w_ffn_down", "w_ple_gate", "w_ple_proj")
LATE = MIXER + FFN_PLE
BIG = ("w_in",) + LATE
HELD_TRANSPOSED = ("w_ffn_gate", "w_ffn_up")
SMALL = ("norm_mix", "w_pool", "pool_scale", "norm_ffn", "norm_ple", "norm_final")


def _join_columns(w4):
    return jnp.concatenate([w4[j] for j in range(N_CHIPS)], axis=1)


def _sds(shape, dtype):
    return jax.ShapeDtypeStruct(shape, dtype)


def _local_step(x, p, target, wf, small, ex=None):
    t, d = x.shape
    w_pool_b = small["w_pool"].astype(BF16)
    dp = w_pool_b.shape[0] * w_pool_b.shape[1]

    ex = ex or _NoExchanges()
    h1, first = _norm_fwd("norm_mix", x, small["norm_mix"], rider=ex.gather(("w_in",)))
    wf = {**wf, **dict(zip(("w_in",), first))}
    w_in = wf["w_in"]
    u, q, kv, ga, gb = _mm(
        "proj", [h1], [w_in[j] for j in range(N_CHIPS)], "nn",
        [_sds((t, dp), F32), _sds((t, dp), BF16), _sds((t, d), BF16), _sds((t, d), BF16), _sds((t, d), BF16)],
        separate=True, epilogue=lambda uq, kv_, ga_, gb_: (uq[:, :dp], uq[:, dp:], kv_, ga_, gb_), tm=512)
    pooled, ya = _pool_fwd(u, w_pool_b, small["pool_scale"])
    n_pairs = dp // LANES
    yb, late = _attn_fwd(q, 0, kv, 0, n_pairs, n_pairs, rider=ex.gather(LATE))
    wf = {**wf, **dict(zip(LATE, late))}
    w_down = wf["w_ffn_down"].reshape(-1, d)
    dff = w_down.shape[0]
    w_gate_t, w_up_t = wf["w_ffn_gate"].reshape(dff, d), wf["w_ffn_up"].reshape(dff, d)
    w_a, w_b, w_pp = _join_columns(wf["w_branch_a"]), _join_columns(wf["w_branch_b"]), _join_columns(wf["w_ple_proj"])
    w_out = wf["w_out"].reshape(d, d)
    w_pg = wf["w_ple_gate"].reshape(d, d)
    def residual_norm(branch, xv, g, w):
        xn = xv + jnp.dot(branch.astype(BF16), w, preferred_element_type=F32)
        return xn, xn * lax.rsqrt(jnp.mean(xn * xn, axis=-1, keepdims=True) + RMS_EPS) * g

    def mixer_tail(tav, tbv, gav, gbv, xv, g, w):
        merged = _sigmoid(gav) * tav + _sigmoid(gbv) * tbv
        return (tav, tbv, merged) + residual_norm(merged, xv, g, w)

    def ffn_tail(gv, uv, xv, g, w):
        act = gv * _sigmoid(gv) * uv
        return (gv, uv, act) + residual_norm(act, xv, g, w)

    stream = [_sds((t, d), F32), _sds((t, d), BF16)]
    ta, tb, merged, x1, h2 = _mm(
        "mixer_out", [ya, yb], [w_a, w_b], "nn", [_sds((t, d), BF16)] * 3 + stream,
        extras=[ga, gb, x, small["norm_ffn"]], wholes=[w_out], separate=True, epilogue=mixer_tail, tm=512)
    gate, up, act, x2, h3 = _mm(
        "ffn", [h2], [w_gate_t, w_up_t], "nt", [_sds((t, dff), BF16)] * 3 + stream,
        extras=[x1, small["norm_ple"]], wholes=[w_down], separate=True, epilogue=ffn_tail, tm=256)
    dx2, dx2_b, d_pp, d_gp, d_norm_final, loss_row, d_norm_ple = _mm(
        "ple_loss", [h3, p], [w_pg, w_pp], "nn", stream + [_sds((t, d), BF16)] * 2,
        extras=[x2, target, small["norm_final"].reshape(1, d), small["norm_ple"]], wholes=[w_pg], separate=True,
        epilogue=_ple_and_loss, sum_shapes=[_sds((1, d), F32)] * 3, tm=512)

    def through_norm(dh, xv, g, dres):
        dx, d_gain = _rms_norm_bwd(dh, xv, g)
        return dx + dres, dx + dres, d_gain

    gain_sum = [_sds((1, d), F32)]
    g_w_pp, g_w_pg = _mm_tn("g_ple", [p, h3], [d_pp, d_gp])

    def ffn_bwd(d_act, gv, uv, xv, g, dres, wg_t, wu_t):
        s = _sigmoid(gv)
        d_gate, d_up = d_act * uv * (s * (1.0 + gv * (1.0 - s))), d_act * (gv * s)
        dh2 = (jnp.dot(d_gate.astype(BF16), wg_t, preferred_element_type=F32)
               + jnp.dot(d_up.astype(BF16), wu_t, preferred_element_type=F32))
        return (d_gate, d_up) + through_norm(dh2, xv, g, dres)

    d_gate, d_up, dx1, dx1_b, d_norm_ffn = _mm(
        "ffn_bwd", [dx2_b], [w_down], "nt", [_sds((t, dff), BF16)] * 2 + stream,
        extras=[gate, up, x1, small["norm_ffn"], dx2], wholes=[w_gate_t, w_up_t], epilogue=ffn_bwd,
        sum_shapes=gain_sum, tm=256)
    g_w_down, = _mm_tn("g_ffn_down", [act], [dx2_b], tmm=512)
    g_w_gate_t, g_w_up_t = _mm_tn("g_ffn_gate_up", [d_gate, d_up], [h2], k_blocks=2)

    def merge_bwd(acc, tav, tbv, gav, gbv):
        sa, sb = _sigmoid(gav), _sigmoid(gbv)
        return acc * sa, acc * sb, acc * tav * sa * (1.0 - sa), acc * tbv * sb * (1.0 - sb)

    big = {
        "w_ffn_gate": g_w_gate_t.reshape(wf["w_ffn_gate"].shape), "w_ffn_up": g_w_up_t.reshape(wf["w_ffn_up"].shape),
        "w_ffn_down": g_w_down.reshape(wf["w_ffn_down"].shape),
        "w_ple_gate": g_w_pg.reshape(wf["w_ple_gate"].shape), "w_ple_proj": g_w_pp,
    }
    (d_ta, d_tb, d_ga, d_gb), theirs = _mm(
        "d_merged", [dx1_b], [w_out], "nt", [_sds((t, d), BF16)] * 4, extras=[ta, tb, ga, gb], epilogue=merge_bwd,
        tm=512, rider=ex.pair(FFN_PLE, big))
    ex.paired(FFN_PLE, big, theirs)
    g_w_out, big["w_branch_a"], big["w_branch_b"] = _mm_tn("g_mixer", [merged, ya, yb], [dx1_b, d_ta, d_tb])
    big["w_out"] = g_w_out.reshape(wf["w_out"].shape)
    (d_ya, d_yb), theirs = _mm(
        "d_branches", [d_ta, d_tb], [w_a, w_b], "nt", [_sds((t, dp), F32), _sds((t, dp), BF16)], separate=True,
        rider=ex.pair(MIXER, big))
    ex.paired(MIXER, big, theirs)
    d_u, g_w_pool, d_pool_scale = _pool_bwd(d_ya, pooled, w_pool_b, small["pool_scale"])
    (d_q, d_k, d_v), landed = _attn_bwd(q, 0, kv, 0, n_pairs, d_yb, n_pairs, rider=ex.chip(LATE))
    ex.landed.update(zip(LATE, landed))
    d_proj = [(d_u, d_q), (d_k, d_v), d_ga, d_gb]
    big["w_in"], = _mm_tn("g_w_in", [h1], d_proj, tmm=512, stacked=True)
    ex.paired(("w_in",), big, _run_exchange("pair_exchange_w_in", ex.pair(("w_in",), big)))
    (grad_x, d_norm_mix), landed = _mm(
        "d_h1", d_proj, [w_in[j] for j in range(N_CHIPS)], "nt", [_sds((t, d), F32)],
        extras=[x, small["norm_mix"], dx1], epilogue=lambda dh, xv, g, dres: through_norm(dh, xv, g, dres)[1:],
        sum_shapes=gain_sum, tm=512, rider=ex.chip(("w_in",)))
    ex.landed.update(zip(("w_in",), landed))
    small_g = {"norm_mix": d_norm_mix, "w_pool": g_w_pool, "pool_scale": d_pool_scale, "norm_ffn": d_norm_ffn,
               "norm_ple": d_norm_ple, "norm_final": d_norm_final}
    return grad_x, big, small_g, loss_row


def _pack_small(small_g, loss_row):
    parts, layout = [], []
    for name in SMALL + ("loss",):
        v = (loss_row if name == "loss" else small_g[name]).reshape(-1, LANES)
        pad = (-v.shape[0]) % 8
        if pad:
            v = jnp.concatenate([v, jnp.zeros((pad, LANES), F32)], axis=0)
        layout.append((name, sum(q.shape[0] for q in parts), v.shape[0]))
        parts.append(v)
    return jnp.concatenate(parts, axis=0), layout


def kernel(x, p, norm_mix, w_in, w_pool, pool_scale, w_branch_a, w_branch_b, w_out, norm_ffn, w_ffn_gate, w_ffn_up, w_ffn_down, norm_ple, w_ple_gate, w_ple_proj, norm_final, loss_target, m_norm_mix, m_w_in, m_w_pool, m_pool_scale, m_w_branch_a, m_w_branch_b, m_w_out, m_norm_ffn, m_w_ffn_gate, m_w_ffn_up, m_w_ffn_down, m_norm_ple, m_w_ple_gate, m_w_ple_proj, m_norm_final, v_norm_mix, v_w_in, v_w_pool, v_pool_scale, v_w_branch_a, v_w_branch_b, v_w_out, v_norm_ffn, v_w_ffn_gate, v_w_ffn_up, v_w_ffn_down, v_norm_ple, v_w_ple_gate, v_w_ple_proj, v_norm_final):
    given = dict(locals())
    order = ("norm_mix", "w_in", "w_pool", "pool_scale", "w_branch_a", "w_branch_b", "w_out", "norm_ffn", "w_ffn_gate",
             "w_ffn_up", "w_ffn_down", "norm_ple", "w_ple_gate", "w_ple_proj", "norm_final")
    t, d = x.shape[1], x.shape[2]
    def local(a, n):
        return jnp.swapaxes(a[0], 0, 1) if n in HELD_TRANSPOSED else a[0]

    def back(a, n):
        return (jnp.swapaxes(a, 0, 1) if n in HELD_TRANSPOSED else a)[None]

    shard = {n: local(given[n], n) for n in BIG}
    small = {"norm_mix": norm_mix, "w_pool": w_pool[0], "pool_scale": pool_scale, "norm_ffn": norm_ffn,
             "norm_ple": norm_ple, "norm_final": norm_final}

    place = jnp.stack([lax.axis_index("c"), 2 * lax.axis_index("x") + lax.axis_index("y")]).astype(jnp.int32)
    ex = _StepExchanges({n: shard[n].astype(BF16) for n in BIG}, place)
    grad_x, _, small_g, loss_row = _local_step(
        x.reshape(t, d), p.reshape(t, p.shape[-1]), loss_target.reshape(t, d), {}, small, ex)
    halves = [_sum_chips(f"chip_sum_{n}", place, ex.pair_sums[n], ex.landed[n]) for n in BIG]
    packed, layout = _pack_small(small_g, loss_row)
    filled, reduced = _finish_gradients(halves, packed)
    grads = dict(zip(BIG, filled))
    for name, start, rows in layout:
        if name == "loss":
            loss = jnp.sum(reduced[start:start + rows])
        else:
            n_el = small[name].size
            grads[name] = reduced[start:start + rows].reshape(-1)[:n_el]

    deltas, new_m, new_v = {}, {}, {}
    for n in order:
        if n in BIG:
            w, m, v = shard[n], local(given["m_" + n], n), local(given["v_" + n], n)
            dl, mn, vn = _adamw(f"adamw_{n}", w, grads[n], m, v)
            grads[n], deltas[n], new_m[n], new_v[n] = [back(a, n) for a in (grads[n], dl, mn, vn)]
        else:
            w, full = small[n], given[n].shape
            shape2 = (1, w.shape[0]) if w.ndim == 1 else (w.shape if w.ndim == 2 else (w.shape[0] * w.shape[1], w.shape[2]))
            dl, mn, vn = _adamw(f"adamw_{n}", w.reshape(shape2), grads[n].reshape(shape2),
                                given["m_" + n].reshape(shape2), given["v_" + n].reshape(shape2))
            grads[n], deltas[n], new_m[n], new_v[n] = [a.reshape(full) for a in (grads[n], dl, mn, vn)]

    return (loss, grad_x.reshape(x.shape), *[grads[n] for n in order], *[deltas[n] for n in order],
            *[new_m[n] for n in order], *[new_v[n] for n in order])
```

```python
import functools
import math

import jax
import jax.numpy as jnp
from jax import lax
from jax.experimental import pallas as pl
from jax.experimental.pallas import tpu as pltpu

F32 = jnp.float32
BF16 = jnp.bfloat16
MESH = pl.DeviceIdType.MESH

RMS_EPS = 1e-6
POOL_WINDOWS = (2, 4, 8, 16)
POOL_HALO = 16
HEAD_DIM = 64
LANES = 128
ATT_BLOCK = 256
ATT_CHAINS = 2
ATT_FWD_CHAINS = 4
ATT_CHUNK = 256
ATT_SLAB = 256
ATT_SCALE = 1.0 / math.sqrt(HEAD_DIM)
LOG2_E = 1.4426950408889634
ATT_EXIT_BELOW = -150.5
ADAM_LR, ADAM_B1, ADAM_B2, ADAM_EPS, ADAM_WD, ADAM_STEP = 0.001, 0.9, 0.999, 1e-08, 0.01, 10
V7X_VMEM_LIMIT_BYTES = 56 * 1024 * 1024
N_CHIPS = 4
N_DEV = 8


def _params(*semantics):
    return pltpu.CompilerParams(dimension_semantics=semantics, vmem_limit_bytes=V7X_VMEM_LIMIT_BYTES)


def _sigmoid(z):
    return 1.0 / (1.0 + jnp.exp(-z))


def _tiled_spec(shape, tm, tn, n_total, at):
    rows, width = shape
    if rows == 1:
        if width == n_total:
            return pl.BlockSpec((1, tn), at(lambda i, j: (0, j)))
        return pl.BlockSpec((1, width), at(lambda i, j: (0, 0)))
    if width == n_total:
        return pl.BlockSpec((tm, tn), at(lambda i, j: (i, j)))
    assert tn == n_total, "an operand narrower than the output needs whole output rows per tile"
    return pl.BlockSpec((tm, width), at(lambda i, j: (i, 0)))


def _column_pieces(operands):
    pieces = [tuple(a) if isinstance(a, (tuple, list)) else (a,) for a in operands]
    return [p for ps in pieces for p in ps], [len(ps) for ps in pieces]


def _load_bf16(refs, counts):
    tiles, k = [], 0
    for n in counts:
        parts = [r[...] for r in refs[k:k + n]]
        parts = [t if t.dtype == BF16 else t.astype(BF16) for t in parts]
        tiles.append(parts[0] if n == 1 else jnp.concatenate(parts, axis=1))
        k += n
    return tiles


def _mm(name, a_list, b_list, mode, out_shapes, epilogue=None, extras=(), tm=1024, tn=None, separate=False,
        sum_shapes=(), rider=None, wholes=()):
    flat_a, counts = _column_pieces(a_list)
    m_total = flat_a[0].shape[0]
    n_total = b_list[0].shape[1] if mode == "nn" else b_list[0].shape[0]
    tn = n_total if tn is None else tn
    tm = min(tm, m_total)
    assert m_total % tm == 0 and n_total % tn == 0 and (not sum_shapes or tn == n_total)
    n_a, n_b, n_extra, n_out = len(counts), len(b_list), len(extras), len(out_shapes)
    assert n_a in (1, n_b)
    dims = (((1,), (0,)), ((), ())) if mode == "nn" else (((1,), (1,)), ((), ()))
    with_rider = rider is not None
    rider = rider or _NoRider()
    grid = (n_total // tn, m_total // tm)

    def at(index):
        return lambda j, i: index(i, j)

    def body(*refs):
        ins, o_refs, _, riding = rider.split(refs, len(flat_a) + n_b + n_extra + len(wholes), n_out + len(sum_shapes))
        a_refs, b_refs = ins[:len(flat_a)], ins[len(flat_a):len(flat_a) + n_b]
        e_refs, w_refs = ins[len(flat_a) + n_b:len(flat_a) + n_b + n_extra], ins[len(flat_a) + n_b + n_extra:]
        at_first = (pl.program_id(0) == 0) & (pl.program_id(1) == 0)
        at_last = (pl.program_id(0) == grid[0] - 1) & (pl.program_id(1) == grid[1] - 1)
        top, bottom = rider.at_steps(riding, at_first, at_first, at_last)
        top()
        lefts = _load_bf16(a_refs, counts)
        products = [lax.dot_general(lefts[s % n_a], b_refs[s][...], dims, preferred_element_type=F32)
                    for s in range(n_b)]
        if not separate:
            products = [functools.reduce(lambda p, r: p + r, products)]
        extra_tiles = [e[...].astype(F32) for e in e_refs]
        outs = products if epilogue is None else epilogue(*products, *extra_tiles, *[w[...] for w in w_refs])
        for o_ref, o in zip(o_refs[:n_out], outs[:n_out]):
            o_ref[...] = o.astype(o_ref.dtype)
        if sum_shapes:
            @pl.when(pl.program_id(1) == 0)
            def _():
                for s_ref in o_refs[n_out:]:
                    s_ref[...] = jnp.zeros_like(s_ref)

            for s_ref, s in zip(o_refs[n_out:], outs[n_out:]):
                s_ref[...] += s
        bottom()

    once = dict(pipeline_mode=pl.Buffered(1)) if tn == n_total else {}
    in_specs = [pl.BlockSpec((tm, a.shape[1]), at(lambda i, j: (i, 0))) for a in flat_a]
    if mode == "nn":
        in_specs += [pl.BlockSpec((b.shape[0], tn), at(lambda i, j: (0, j)), **once) for b in b_list]
    else:
        in_specs += [pl.BlockSpec((tn, b.shape[1]), at(lambda i, j: (j, 0)), **once) for b in b_list]
    in_specs += [_tiled_spec(e.shape, tm, tn, n_total, at) for e in extras]
    in_specs += [pl.BlockSpec(w.shape, lambda j, i: (0, 0), pipeline_mode=pl.Buffered(1)) for w in wholes]
    out_specs = [_tiled_spec(o.shape, tm, tn, n_total, at) for o in out_shapes]
    out_specs += [pl.BlockSpec(s.shape, at(lambda i, j: (0, 0))) for s in sum_shapes]
    semantics = ("arbitrary", "arbitrary") if sum_shapes or rider.operands else ("parallel", "parallel")
    res = pl.pallas_call(
        body, name=name, grid=grid, in_specs=in_specs + [ANY] * len(rider.operands),
        out_specs=out_specs + [ANY] * len(rider.out_shapes),
        out_shape=list(out_shapes) + list(sum_shapes) + list(rider.out_shapes), scratch_shapes=list(rider.scratch),
        compiler_params=_params(*semantics),
    )(*flat_a, *b_list, *extras, *wholes, *rider.operands)
    n_own = len(out_shapes) + len(sum_shapes)
    return (res[:n_own], res[n_own:]) if with_rider else res


def _mm_tn(name, a_list, b_list, tmm=1024, stacked=False, k_blocks=1):
    flat_b, counts = _column_pieces(b_list)
    n_a, n_b = len(a_list), len(counts)
    n_prod = max(n_a, n_b)
    m_total = a_list[0].shape[0]
    ks = [a_list[s % n_a].shape[1] for s in range(n_prod)]
    widths = [sum(p.shape[1] for p in flat_b[sum(counts[:s]):sum(counts[:s + 1])]) for s in range(n_b)]
    widths = [widths[s % n_b] for s in range(n_prod)]
    tmm = min(tmm, m_total)
    assert m_total % tmm == 0 and all(k % k_blocks == 0 for k in ks)
    assert n_a in (1, n_prod) and n_b in (1, n_prod) and not (stacked and n_a > 1)

    def body(*refs):
        a_refs, b_refs, o_refs = refs[:n_a], refs[n_a:n_a + len(flat_b)], refs[n_a + len(flat_b):]

        @pl.when(pl.program_id(1) == 0)
        def _():
            for o_ref in o_refs:
                o_ref[...] = jnp.zeros_like(o_ref)

        lefts, rights = _load_bf16(a_refs, [1] * n_a), _load_bf16(b_refs, counts)
        for s in range(n_prod):
            product = lax.dot_general(lefts[s % n_a], rights[s % n_b], (((0,), (0,)), ((), ())),
                                      preferred_element_type=F32)
            if stacked:
                o_refs[0][s] += product
            else:
                o_refs[s][...] += product

    in_specs = [pl.BlockSpec((tmm, a.shape[1] // k_blocks), lambda kb, m: (m, kb)) for a in a_list]
    in_specs += [pl.BlockSpec((tmm, b.shape[1]), lambda kb, m: (m, 0)) for b in flat_b]
    if stacked:
        out_shape = [jax.ShapeDtypeStruct((n_prod, ks[0], widths[0]), F32)]
        out_specs = [pl.BlockSpec((n_prod, ks[0] // k_blocks, widths[0]), lambda kb, m: (0, kb, 0))]
    else:
        out_shape = [jax.ShapeDtypeStruct((k, w), F32) for k, w in zip(ks, widths)]
        out_specs = [pl.BlockSpec((k // k_blocks, w), lambda kb, m: (kb, 0)) for k, w in zip(ks, widths)]
    return pl.pallas_call(
        body, name=name, grid=(k_blocks, m_total // tmm), in_specs=in_specs, out_specs=out_specs, out_shape=out_shape,
        compiler_params=_params("arbitrary", "arbitrary"),
    )(*a_list, *flat_b)


def _rows(name, fn, ins, tile_outs, sum_outs=(), tr=512, rider=None):
    t_total = max(a.shape[0] for a in ins)
    tr = min(tr, t_total)
    assert t_total % tr == 0
    n_in, n_tile = len(ins), len(tile_outs)
    rider = rider or _NoRider()
    n_steps = t_total // tr

    def body(*refs):
        own_ins, own_outs, _, riding = rider.split(refs, n_in, n_tile + len(sum_outs))
        step = pl.program_id(0)
        top, bottom = rider.at_steps(riding, step == 0, step == n_steps - 1, step == n_steps - 1)
        top()
        refs = tuple(own_ins) + tuple(own_outs)
        outs = fn(*[r[...].astype(F32) for r in refs[:n_in]])
        for o_ref, o in zip(refs[n_in:n_in + n_tile], outs[:n_tile]):
            o_ref[...] = o.astype(o_ref.dtype)
        if sum_outs:
            @pl.when(pl.program_id(0) == 0)
            def _():
                for s_ref in refs[n_in + n_tile:]:
                    s_ref[...] = jnp.zeros_like(s_ref)

            for s_ref, s in zip(refs[n_in + n_tile:], outs[n_tile:]):
                s_ref[...] += s
        bottom()

    def spec(shape):
        if shape[0] == 1:
            return pl.BlockSpec(shape, lambda i: (0, 0))
        return pl.BlockSpec((tr, shape[1]), lambda i: (i, 0))

    return pl.pallas_call(
        body, name=name, grid=(n_steps,), in_specs=[spec(a.shape) for a in ins] + [ANY] * len(rider.operands),
        out_specs=[spec(o.shape) for o in tile_outs] + [spec(s.shape) for s in sum_outs] + [ANY] * len(rider.out_shapes),
        out_shape=list(tile_outs) + list(sum_outs) + list(rider.out_shapes), scratch_shapes=list(rider.scratch),
        compiler_params=_params("arbitrary" if sum_outs or rider.operands else "parallel"),
    )(*ins, *rider.operands)


def _norm_fwd(name, x, gain, rider=None):
    def fn(xv, g):
        inv = lax.rsqrt(jnp.mean(xv * xv, axis=-1, keepdims=True) + RMS_EPS)
        return (xv * inv * g,)

    res = _rows(name, fn, [x, gain], [jax.ShapeDtypeStruct(x.shape, BF16)], rider=rider)
    return res[0], res[1:]


def _rms_norm_bwd(dh, xv, g):
    inv = lax.rsqrt(jnp.mean(xv * xv, axis=-1, keepdims=True) + RMS_EPS)
    xn = xv * inv
    dxn = dh * g
    return inv * (dxn - xn * jnp.mean(dxn * xn, axis=-1, keepdims=True)), jnp.sum(dh * xn, axis=0, keepdims=True)


def _ple_and_loss(gv, pv, x2v, tv, g_final, g_ple, w_pg):
    d = x2v.shape[1]
    s = _sigmoid(gv)
    xv = x2v + s * pv
    inv = lax.rsqrt(jnp.mean(xv * xv, axis=-1, keepdims=True) + RMS_EPS)
    err = xv * inv * g_final - tv
    dx3, d_final = _rms_norm_bwd(err * (1.0 / d), xv, g_final)
    d_pp, d_gp = dx3 * s, dx3 * pv * s * (1.0 - s)
    dh3 = lax.dot_general(d_gp.astype(BF16), w_pg, (((1,), (1,)), ((), ())), preferred_element_type=F32)
    dx2, d_ple = _rms_norm_bwd(dh3, x2v, g_ple)
    dx2 = dx2 + dx3
    return dx2, dx2, d_pp, d_gp, d_final, (0.5 / d) * jnp.sum(err * err, axis=0, keepdims=True), d_ple


def _window_counts(t_pos, w):
    return jnp.minimum(t_pos + 1, w).astype(F32)


def _pool_fwd(u, w_pool, scale, tr=512):
    t_total, width = u.shape
    tr = min(tr, t_total)
    n_groups = len(POOL_WINDOWS)
    gdim = width // n_groups
    ext = tr + POOL_HALO

    def body(u_ref, halo_ref, w_ref, s_ref, pooled_ref, ya_ref):
        i = pl.program_id(0)
        halo = jnp.where(i == 0, 0.0, halo_ref[...])
        t_pos = i * tr + lax.broadcasted_iota(jnp.int32, (tr, 1), 0)
        for g, w in enumerate(POOL_WINDOWS):
            cols = slice(g * gdim, (g + 1) * gdim)
            main = u_ref[:, cols]
            win = jnp.concatenate([halo[:, cols], main], axis=0)
            span = 1
            while span < w:
                win = win + pltpu.roll(win, span, 0)
                span *= 2
            pooled = win[POOL_HALO:, :] * (1.0 / _window_counts(t_pos, w)) - main
            pooled_b = pooled.astype(BF16)
            pooled_ref[:, cols] = pooled_b
            mixed = jnp.dot(pooled_b, w_ref[g], preferred_element_type=F32)
            ya_ref[:, cols] = (mixed * s_ref[:, cols]).astype(BF16)

    hb = tr // POOL_HALO
    return pl.pallas_call(
        body, name="pool_fwd", grid=(t_total // tr,),
        in_specs=[pl.BlockSpec((tr, width), lambda i: (i, 0)),
                  pl.BlockSpec((POOL_HALO, width), lambda i: (jnp.maximum(i * hb - 1, 0), 0)),
                  pl.BlockSpec((n_groups, gdim, gdim), lambda i: (0, 0, 0)),
                  pl.BlockSpec((1, width), lambda i: (0, 0))],
        out_specs=[pl.BlockSpec((tr, width), lambda i: (i, 0)), pl.BlockSpec((tr, width), lambda i: (i, 0))],
        out_shape=[jax.ShapeDtypeStruct(u.shape, BF16), jax.ShapeDtypeStruct(u.shape, BF16)],
        compiler_params=_params("parallel"),
    )(u, u, w_pool, scale)


def _pool_bwd(dya, pooled, w_pool, scale, tr=512):
    t_total, width = dya.shape
    tr = min(tr, t_total)
    n_groups = len(POOL_WINDOWS)
    gdim = width // n_groups
    ext = tr + POOL_HALO
    n_tiles = t_total // tr

    def body(d_ref, halo_ref, p_ref, w_ref, s_ref, du_ref, dw_ref, ds_ref):
        i = pl.program_id(0)

        @pl.when(i == 0)
        def _():
            dw_ref[...] = jnp.zeros_like(dw_ref)
            ds_ref[...] = jnp.zeros_like(ds_ref)

        halo = jnp.where(i == n_tiles - 1, 0.0, halo_ref[...])
        t_pos = i * tr + lax.broadcasted_iota(jnp.int32, (ext, 1), 0)
        for g, w in enumerate(POOL_WINDOWS):
            cols = slice(g * gdim, (g + 1) * gdim)
            sc = s_ref[:, cols]
            d_main = d_ref[:, cols]
            pooled_b = p_ref[:, cols]
            mixed = jnp.dot(pooled_b, w_ref[g], preferred_element_type=F32)
            ds_ref[:, cols] += jnp.sum(d_main * mixed, axis=0, keepdims=True)
            dmix = (jnp.concatenate([d_main, halo[:, cols]], axis=0) * sc).astype(BF16)
            dw_ref[g] += lax.dot_general(pooled_b, dmix[:tr, :], (((0,), (0,)), ((), ())),
                                         preferred_element_type=F32)
            dpool = lax.dot_general(dmix, w_ref[g], (((1,), (1,)), ((), ())), preferred_element_type=F32)
            win = dpool * (1.0 / _window_counts(t_pos, w))
            span = 1
            while span < w:
                win = win + pltpu.roll(win, ext - span, 0)
                span *= 2
            du_ref[:, cols] = (win[:tr, :] - dpool[:tr, :]).astype(BF16)

    hb = tr // POOL_HALO
    last_halo = t_total // POOL_HALO - 1
    return pl.pallas_call(
        body, name="pool_bwd", grid=(n_tiles,),
        in_specs=[pl.BlockSpec((tr, width), lambda i: (i, 0)),
                  pl.BlockSpec((POOL_HALO, width), lambda i: (jnp.minimum((i + 1) * hb, last_halo), 0)),
                  pl.BlockSpec((tr, width), lambda i: (i, 0)),
                  pl.BlockSpec((n_groups, gdim, gdim), lambda i: (0, 0, 0)),
                  pl.BlockSpec((1, width), lambda i: (0, 0))],
        out_specs=[pl.BlockSpec((tr, width), lambda i: (i, 0)),
                   pl.BlockSpec((n_groups, gdim, gdim), lambda i: (0, 0, 0)),
                   pl.BlockSpec((1, width), lambda i: (0, 0))],
        out_shape=[jax.ShapeDtypeStruct(dya.shape, BF16), jax.ShapeDtypeStruct((n_groups, gdim, gdim), F32),
                   jax.ShapeDtypeStruct((1, width), F32)],
        compiler_params=_params("arbitrary"),
    )(dya, dya, pooled, w_pool, scale)


def _head_masks():
    lane = lax.broadcasted_iota(jnp.int32, (1, LANES), 1)
    return lane < HEAD_DIM


def _stack_heads(tile, first):
    zero = jnp.zeros_like(tile)
    return jnp.concatenate([jnp.where(first, tile, zero), jnp.where(first, zero, tile)], axis=0)


def _causal_mask(t_pos, k_start):
    col = lax.broadcasted_iota(jnp.int32, (1, 2 * ATT_SLAB), 1)
    return k_start + (col & (ATT_SLAB - 1)) < t_pos


def _slab_scores(q, kd, mask):
    z2 = lax.dot_general(q, kd, (((1,), (1,)), ((), ())), preferred_element_type=F32) * LOG2_E
    log_hit = jnp.minimum(z2, 0.0) - jnp.log2(1.0 + jnp.exp2(-jnp.abs(z2)))
    log_fail = log_hit - z2
    return log_hit, (log_fail if mask is None else jnp.where(mask, log_fail, 0.0))


def _weights(log_hit, suffix, mask):
    arg = log_hit + suffix
    return jnp.exp2(arg if mask is None else jnp.where(mask, arg, -1e30))


def _tri(upper):
    r = lax.broadcasted_iota(jnp.int32, (ATT_CHUNK, ATT_CHUNK), 0)
    c = lax.broadcasted_iota(jnp.int32, (ATT_CHUNK, ATT_CHUNK), 1)
    return jnp.where(r > c if upper else r < c, 1.0, 0.0).astype(BF16)


def _tri_spec():
    return pl.BlockSpec((ATT_CHUNK, ATT_CHUNK), lambda h, i: (0, 0), pipeline_mode=pl.Buffered(1))


def _scan_chunk(v, tri):
    return jnp.dot(v.astype(BF16), tri, preferred_element_type=F32)


def _lane_bcast(col):
    return jnp.broadcast_to(col, (col.shape[0], LANES))


def _scan_slab(v, tri, carries, from_right):
    n_chunks = ATT_SLAB // ATT_CHUNK
    edge = 0 if from_right else ATT_CHUNK - 1
    parts, new_carries = [None] * (2 * n_chunks), []
    for head in range(2):
        run = carries[head]
        for c in (reversed(range(n_chunks)) if from_right else range(n_chunks)):
            lo_col = head * ATT_SLAB + c * ATT_CHUNK
            vc = v[:, lo_col:lo_col + ATT_CHUNK]
            sc = _scan_chunk(vc, tri)
            parts[head * n_chunks + c] = sc + jnp.concatenate([run] * (ATT_CHUNK // LANES), axis=1)
            run = run + _lane_bcast(sc[:, edge:edge + 1] + vc[:, edge:edge + 1])
        new_carries.append(run)
    return jnp.concatenate(parts, axis=1), new_carries


def _fold_heads(stacked, first):
    s = stacked.shape[0] // 2
    return jnp.where(first, stacked[:s], stacked[s:])


class _NoRider:
    operands, out_shapes, scratch = (), (), ()

    def split(self, refs, n_base_in, n_base_out):
        n_in, n_out, n_sem = len(self.operands), len(self.out_shapes), len(self.scratch)
        a = n_base_in + n_in
        b = a + n_base_out + n_out
        mine = (refs[n_base_in:a], refs[a + n_base_out:b], refs[b:b + n_sem])
        return refs[:n_base_in], refs[a:a + n_base_out], refs[b + n_sem:], mine

    def start(self, ins, outs, sems):
        pass

    def relay(self, ins, outs, sems):
        pass

    def finish(self, ins, outs, sems):
        pass

    def at_steps(self, refs, first_step, relay_step, last_step):
        if not self.operands:
            return (lambda: None), (lambda: None)

        def top():
            pl.when(first_step)(lambda: self.start(*refs))
            pl.when(relay_step)(lambda: self.relay(*refs))

        return top, lambda: pl.when(last_step)(lambda: self.finish(*refs))


def _attn_fwd(q_src, q_col, kv_src, k_col, v_col, n_pairs=4, rider=_NoRider()):
    t_total = q_src.shape[0]
    blk = ATT_BLOCK
    n_chains = ATT_FWD_CHAINS if t_total % (ATT_FWD_CHAINS * blk) == 0 else ATT_CHAINS
    n_steps = t_total // (n_chains * blk)
    assert t_total % ATT_SLAB == 0 and ATT_SLAB == ATT_BLOCK

    def body(*refs):
        (q_ref, k_ref, v_ref, suffix_ref), (o_ref,), _, riding = rider.split(refs, 4, 1)
        h, ii = pl.program_id(0), pl.program_id(1)
        top, bottom = rider.at_steps(riding, (h == 0) & (ii == 0), (h == n_pairs - 1) & (ii == 0),
                                     (h == n_pairs - 1) & (ii == n_steps - 1))
        top()
        first = _head_masks()
        suffix_tri = suffix_ref[...]
        blocks = [n_chains * ii + c for c in range(n_chains)]
        qs = [q_ref[c * blk:(c + 1) * blk, :] * ATT_SCALE for c in range(n_chains)]
        t_pos = [b * blk + lax.broadcasted_iota(jnp.int32, (blk, 1), 0) for b in blocks]

        def one(c, t, chain, on_diagonal):
            _, acc, right_a, right_b = chain
            k_start = pl.multiple_of((blocks[c] - t) * ATT_SLAB, ATT_SLAB)
            kd = _stack_heads(k_ref[pl.ds(k_start, ATT_SLAB), :], first)
            vd = _stack_heads(v_ref[pl.ds(k_start, ATT_SLAB), :], first)
            mask = _causal_mask(t_pos[c], k_start) if on_diagonal else None
            log_hit, log_fail = _slab_scores(qs[c], kd, mask)
            suffix, (right_a, right_b) = _scan_slab(log_fail, suffix_tri, (right_a, right_b), from_right=True)
            a = _weights(log_hit, suffix, mask).astype(BF16)
            acc = acc + jnp.dot(a, vd, preferred_element_type=F32)
            return jnp.max(jnp.maximum(right_a, right_b)), acc, right_a, right_b

        def step(state, on_diagonal):
            t, chains = state
            return t + 1, tuple(one(c, t, chains[c], on_diagonal) for c in range(n_chains))

        def more(state):
            t, chains = state
            return (t <= blocks[0]) & (functools.reduce(jnp.maximum, [ch[0] for ch in chains]) > ATT_EXIT_BELOW)

        zero = jnp.zeros((blk, LANES), F32)
        state = step((0, ((jnp.float32(0.0), zero, zero, zero),) * n_chains), on_diagonal=True)
        t, chains = lax.while_loop(more, functools.partial(step, on_diagonal=False), state)
        for c in range(n_chains):
            chain = chains[c]
            if c:
                _, chain = lax.while_loop(
                    lambda s, c=c: (s[0] <= blocks[c]) & (s[1][0] > ATT_EXIT_BELOW),
                    lambda s, c=c: (s[0] + 1, one(c, s[0], s[1], False)), (t, chain))
            o_ref[c * blk:(c + 1) * blk, :] = chain[1].astype(BF16)
        bottom()

    rows = n_chains * blk
    res = pl.pallas_call(
        body, name="attn_fwd", grid=(n_pairs, n_steps),
        in_specs=[pl.BlockSpec((rows, LANES), lambda h, i: (i, q_col + h)),
                  pl.BlockSpec((t_total, LANES), lambda h, i: (0, k_col + h)),
                  pl.BlockSpec((t_total, LANES), lambda h, i: (0, v_col + h)), _tri_spec()] + [ANY] * len(rider.operands),
        out_specs=[pl.BlockSpec((rows, LANES), lambda h, i: (i, h))] + [ANY] * len(rider.out_shapes),
        out_shape=[jax.ShapeDtypeStruct((t_total, n_pairs * LANES), BF16)] + list(rider.out_shapes),
        scratch_shapes=list(rider.scratch),
        compiler_params=_params("arbitrary", "arbitrary"),
    )(q_src, kv_src, kv_src, _tri(upper=True), *rider.operands)
    return res[0], res[1:]


def _attn_bwd(q_src, q_col, kv_src, k_col, v_col, dy, n_pairs=4, rider=_NoRider()):
    t_total = q_src.shape[0]
    blk = ATT_BLOCK
    n_steps = t_total // (ATT_CHAINS * blk)
    n_slabs = t_total // ATT_SLAB
    assert t_total % ATT_SLAB == 0 and ATT_SLAB == ATT_BLOCK

    def body(*refs):
        ins, (dq_ref, dk_ref, dv_ref), (g_s, dk_acc, dv_acc), riding = rider.split(refs, 6, 3)
        q_ref, dy_ref, k_ref, v_ref, suffix_ref, prefix_ref = ins
        h, ii = pl.program_id(0), pl.program_id(1)
        top, bottom = rider.at_steps(riding, (h == 0) & (ii == 0), (h == n_pairs - 1) & (ii == 0),
                                     (h == n_pairs - 1) & (ii == n_steps - 1))
        top()

        @pl.when(ii == 0)
        def _():
            dk_acc[...] = jnp.zeros_like(dk_acc)
            dv_acc[...] = jnp.zeros_like(dv_acc)

        first = _head_masks()
        suffix_tri = suffix_ref[...]
        prefix_tri = prefix_ref[...]
        blocks = [ATT_CHAINS * ii + c for c in range(ATT_CHAINS)]
        rows = [slice(c * blk, (c + 1) * blk) for c in range(ATT_CHAINS)]
        qs = [q_ref[r, :] * ATT_SCALE for r in rows]
        dys = [dy_ref[r, :] for r in rows]
        t_pos = [b * blk + lax.broadcasted_iota(jnp.int32, (blk, 1), 0) for b in blocks]

        def one1(c, t, chain, on_diagonal):
            _, right_a, right_b = chain
            slab = blocks[c] - t
            k_start = pl.multiple_of(slab * ATT_SLAB, ATT_SLAB)
            kd = _stack_heads(k_ref[pl.ds(k_start, ATT_SLAB), :], first)
            vd = _stack_heads(v_ref[pl.ds(k_start, ATT_SLAB), :], first)
            mask = _causal_mask(t_pos[c], k_start) if on_diagonal else None
            log_hit, log_fail = _slab_scores(qs[c], kd, mask)
            suffix, (right_a, right_b) = _scan_slab(log_fail, suffix_tri, (right_a, right_b), from_right=True)
            a = _weights(log_hit, suffix, mask)
            da = lax.dot_general(dys[c], vd, (((1,), (1,)), ((), ())), preferred_element_type=F32)
            g_s[c, slab] = (da * a).astype(BF16)
            dv_acc[pl.ds(k_start, ATT_SLAB), :] += _fold_heads(lax.dot_general(
                a.astype(BF16), dys[c], (((0,), (0,)), ((), ())), preferred_element_type=F32), first)
            return jnp.max(jnp.maximum(right_a, right_b)), right_a, right_b

        def step1(state, on_diagonal):
            t, chains = state
            return t + 1, tuple(one1(c, t, chains[c], on_diagonal) for c in range(ATT_CHAINS))

        def more(state):
            t, chains = state
            return (t <= blocks[0]) & (functools.reduce(jnp.maximum, [ch[0] for ch in chains]) > ATT_EXIT_BELOW)

        zero = jnp.zeros((blk, LANES), F32)
        state = step1((0, ((jnp.float32(0.0), zero, zero),) * ATT_CHAINS), on_diagonal=True)
        joint, chains = lax.while_loop(more, functools.partial(step1, on_diagonal=False), state)
        done = [joint]
        for c in range(1, ATT_CHAINS):
            done.append(lax.while_loop(
                lambda s, c=c: (s[0] <= blocks[c]) & (s[1][0] > ATT_EXIT_BELOW),
                lambda s, c=c: (s[0] + 1, one1(c, s[0], s[1], False)), (joint, chains[c]))[0])

        def one2(c, t, carry, on_diagonal):
            dq, left_a, left_b = carry
            slab = blocks[c] - t
            k_start = pl.multiple_of(slab * ATT_SLAB, ATT_SLAB)
            kd = _stack_heads(k_ref[pl.ds(k_start, ATT_SLAB), :], first)
            g = g_s[c, slab]
            z2 = lax.dot_general(qs[c], kd, (((1,), (1,)), ((), ())), preferred_element_type=F32) * LOG2_E
            sig = 1.0 / (1.0 + jnp.exp2(-z2))
            prefix, (left_a, left_b) = _scan_slab(g, prefix_tri, (left_a, left_b), from_right=False)
            dz = g * (1.0 - sig) - sig * prefix
            if on_diagonal:
                dz = jnp.where(_causal_mask(t_pos[c], k_start), dz, 0.0)
            dz = dz.astype(BF16)
            dq = dq + jnp.dot(dz, kd, preferred_element_type=F32)
            dk_acc[pl.ds(k_start, ATT_SLAB), :] += _fold_heads(lax.dot_general(
                dz, qs[c], (((0,), (0,)), ((), ())), preferred_element_type=F32), first)
            return dq, left_a, left_b

        carries = [(zero, zero, zero)]
        for c in range(1, ATT_CHAINS):
            carries.append(lax.fori_loop(
                0, done[c] - joint, lambda n, carry, c=c: one2(c, done[c] - 1 - n, carry, False), (zero, zero, zero)))
        carries = lax.fori_loop(
            0, joint - 1,
            lambda n, cs: tuple(one2(c, joint - 1 - n, cs[c], False) for c in range(ATT_CHAINS)), tuple(carries))
        for c in range(ATT_CHAINS):
            dq_ref[rows[c], :] = (one2(c, 0, carries[c], True)[0] * ATT_SCALE).astype(BF16)

        @pl.when(ii == n_steps - 1)
        def _():
            dk_ref[...] = dk_acc[...].astype(BF16)
            dv_ref[...] = dv_acc[...].astype(BF16)

        bottom()

    out = jax.ShapeDtypeStruct((t_total, n_pairs * LANES), BF16)
    n_rows = ATT_CHAINS * blk
    whole = dict(pipeline_mode=pl.Buffered(1))
    res = pl.pallas_call(
        body, name="attn_bwd", grid=(n_pairs, n_steps),
        in_specs=[pl.BlockSpec((n_rows, LANES), lambda h, i: (i, q_col + h)),
                  pl.BlockSpec((n_rows, LANES), lambda h, i: (i, h)),
                  pl.BlockSpec((t_total, LANES), lambda h, i: (0, k_col + h), **whole),
                  pl.BlockSpec((t_total, LANES), lambda h, i: (0, v_col + h), **whole), _tri_spec(), _tri_spec()]
        + [ANY] * len(rider.operands),
        out_specs=[pl.BlockSpec((n_rows, LANES), lambda h, i: (i, h)),
                   pl.BlockSpec((t_total, LANES), lambda h, i: (0, h)),
                   pl.BlockSpec((t_total, LANES), lambda h, i: (0, h))] + [ANY] * len(rider.out_shapes),
        out_shape=[out, out, out] + list(rider.out_shapes),
        scratch_shapes=list(rider.scratch) + [pltpu.VMEM((ATT_CHAINS, n_slabs, blk, 2 * ATT_SLAB), BF16),
                                              pltpu.VMEM((t_total, LANES), F32), pltpu.VMEM((t_total, LANES), F32)],
        compiler_params=_params("arbitrary", "arbitrary"),
    )(q_src, dy, kv_src, kv_src, _tri(upper=True), _tri(upper=False), *rider.operands)
    return res[:3], res[3:]


def _adamw(name, w, g, m, v):
    def fn(wv, gv, mv, vv):
        mn = ADAM_B1 * mv + (1.0 - ADAM_B1) * gv
        vn = ADAM_B2 * vv + (1.0 - ADAM_B2) * (gv * gv)
        m_hat = mn / (1.0 - ADAM_B1 ** ADAM_STEP)
        v_hat = vn / (1.0 - ADAM_B2 ** ADAM_STEP)
        return -ADAM_LR * (m_hat / (jnp.sqrt(v_hat) + ADAM_EPS) + ADAM_WD * wv), mn, vn

    rows = w.shape[0]
    tr = _row_tile(rows)
    shp = jax.ShapeDtypeStruct(w.shape, F32)
    if rows == 1:
        def body(w_ref, g_ref, m_ref, v_ref, d_ref, mo_ref, vo_ref):
            d, mn, vn = fn(w_ref[...], g_ref[...], m_ref[...], v_ref[...])
            d_ref[...], mo_ref[...], vo_ref[...] = d, mn, vn

        return pl.pallas_call(body, name=name, out_shape=[shp, shp, shp])(w, g, m, v)
    return _rows(name, fn, [w, g, m, v], [shp, shp, shp], tr=tr)


def _place():
    return lax.axis_index("x"), lax.axis_index("y"), lax.axis_index("c")


def _other_chips(x, y):
    return [(1 - x, y), (x, 1 - y), (1 - x, 1 - y)]


ANY = pl.BlockSpec(memory_space=pl.ANY)


def _remote(src, dst, send_sem, recv_sem, to):
    return pltpu.make_async_remote_copy(src_ref=src, dst_ref=dst, send_sem=send_sem, recv_sem=recv_sem,
                                        device_id=to, device_id_type=MESH)


class _WeightGather(_NoRider):
    def __init__(self, shards):
        n_w = len(shards)
        self.operands = list(shards)
        self.out_shapes = [jax.ShapeDtypeStruct((N_CHIPS,) + s.shape, s.dtype) for s in shards]
        self.scratch = [pltpu.SemaphoreType.DMA((3, n_w))] * 4 + [pltpu.SemaphoreType.DMA((n_w,))] * 2

    def _copies(self, ins, outs, sems):
        send_sems, recv_sems, relay_send, relay_recv, own_send, own_recv = sems
        x, y, c = _place()
        my_chip, sibling = 2 * x + y, (x, y, 1 - c)
        n_w = len(ins)

        def half(w, chip, core):
            h = self.operands[w].shape[0] // 2
            return outs[w].at[chip, pl.ds(core * h, h)]

        own = [_remote(ins[w], outs[w].at[my_chip], own_send.at[w], own_recv.at[w], sibling) for w in range(n_w)]
        sends, landed, relays, relayed = [], [], [], []
        for p, (ox, oy) in enumerate(_other_chips(x, y)):
            for w in range(n_w):
                h = self.operands[w].shape[0] // 2
                sends.append(_remote(ins[w].at[pl.ds(c * h, h)], half(w, my_chip, c), send_sems.at[p, w],
                                     recv_sems.at[p, w], (ox, oy, c)))
                here = half(w, 2 * ox + oy, c)
                landed.append(_remote(here, here, send_sems.at[p, w], recv_sems.at[p, w], (ox, oy, c)))
                relays.append(_remote(here, here, relay_send.at[p, w], relay_recv.at[p, w], sibling))
                there = half(w, 2 * ox + oy, 1 - c)
                relayed.append(_remote(there, there, relay_send.at[p, w], relay_recv.at[p, w], sibling))
        return own, sends, landed, relays, relayed

    def start(self, ins, outs, sems):
        own, sends, _, _, _ = self._copies(ins, outs, sems)
        for cp in own + sends:
            cp.start()

    def relay(self, ins, outs, sems):
        _, _, landed, relays, _ = self._copies(ins, outs, sems)
        for arrival, cp in zip(landed, relays):
            arrival.wait_recv()
            cp.start()

    def finish(self, ins, outs, sems):
        own, sends, _, relays, relayed = self._copies(ins, outs, sems)
        for arrival in relayed:
            arrival.wait_recv()
        for cp in sends + relays:
            cp.wait_send()
        for cp in own:
            cp.wait()


class _ChipExchange(_NoRider):
    def __init__(self, pair_sums):
        n_w = len(pair_sums)
        self.operands = list(pair_sums)
        self.out_shapes = [jax.ShapeDtypeStruct((3,) + s.shape[1:], s.dtype) for s in pair_sums]
        self.scratch = [pltpu.SemaphoreType.DMA((3, n_w))] * 2

    def _copies(self, ins, outs, sems):
        send_sems, recv_sems = sems
        x, y, c = _place()
        return [_remote(ins[w].at[2 * ox + oy], outs[w].at[p], send_sems.at[p, w], recv_sems.at[p, w], (ox, oy, c))
                for p, (ox, oy) in enumerate(_other_chips(x, y)) for w in range(len(ins))]

    def start(self, ins, outs, sems):
        for cp in self._copies(ins, outs, sems):
            cp.start()

    def finish(self, ins, outs, sems):
        for cp in self._copies(ins, outs, sems):
            cp.wait()


class _PairExchange(_NoRider):
    def __init__(self, grads):
        n_w = len(grads)
        self.operands = list(grads)
        self.out_shapes = [jax.ShapeDtypeStruct(g.shape[:-2] + (g.shape[-2] // 2, g.shape[-1]), F32) for g in grads]
        self.scratch = [pltpu.SemaphoreType.DMA((n_w,))] * 2

    def _copies(self, ins, theirs, sems):
        send_sems, recv_sems = sems
        x, y, c = _place()
        sends = []
        for w, g in enumerate(self.operands):
            rows = pl.ds((1 - c) * (g.shape[-2] // 2), g.shape[-2] // 2)
            src = ins[w].at[:, rows, :] if g.ndim == 3 else ins[w].at[rows, :]
            sends.append(_remote(src, theirs[w], send_sems.at[w], recv_sems.at[w], (x, y, 1 - c)))
        return sends

    def start(self, ins, outs, sems):
        for cp in self._copies(ins, outs, sems):
            cp.start()

    def finish(self, ins, outs, sems):
        for cp in self._copies(ins, outs, sems):
            cp.wait()


def _run_exchange(name, plan):
    n_in, n_out = len(plan.operands), len(plan.out_shapes)
    if not n_in:
        return ()

    def body(*refs):
        parts = (refs[:n_in], refs[n_in:n_in + n_out], refs[n_in + n_out:])
        plan.start(*parts)
        plan.relay(*parts)
        plan.finish(*parts)

    return pl.pallas_call(body, name=name, in_specs=[ANY] * n_in, out_specs=[ANY] * n_out,
                          out_shape=list(plan.out_shapes), scratch_shapes=list(plan.scratch))(*plan.operands)


class _NoExchanges:
    pair_sums, landed = {}, {}

    def gather(self, names):
        return _NoRider()

    def pair(self, names, grads):
        return _NoRider()

    def paired(self, names, grads, theirs):
        pass

    def chip(self, names):
        return _NoRider()


class _StepExchanges(_NoExchanges):
    def __init__(self, shards_bf16, place):
        self.shards, self.place = shards_bf16, place
        self.pair_sums, self.landed = {}, {}

    def gather(self, names):
        return _WeightGather([self.shards[n] for n in names])

    def pair(self, names, grads):
        return _PairExchange([grads[n] for n in names])

    def paired(self, names, grads, theirs):
        for n, other in zip(names, theirs):
            self.pair_sums[n] = _pair_sum(f"pair_sum_{n}", self.place, grads[n], other)

    def chip(self, names):
        return _ChipExchange([self.pair_sums[n] for n in names])


SUM_ROWS = 32


def _finish_gradients(place, pair_sums, landed, vec):
    n_w = len(pair_sums)
    rows = vec.shape[0]
    halves = [s.shape[1:] for s in pair_sums]

    def body(place_ref, *refs):
        sums, lands, v_ref = refs[:n_w], refs[n_w:2 * n_w], refs[2 * n_w]
        outs, o_ref = refs[2 * n_w + 1:3 * n_w + 1], refs[3 * n_w + 1]
        stage = refs[3 * n_w + 2:4 * n_w + 2]
        kept, half_send, half_recv, slots, core_sums, vec_send, vec_recv, sum_send, sum_recv = refs[4 * n_w + 2:]
        x, y, c = _place()
        my_chip, sibling = 2 * x + y, (x, y, 1 - c)
        slots[my_chip] = v_ref[...]
        spread = []
        for p, (ox, oy) in enumerate(_other_chips(x, y)):
            here = slots.at[2 * ox + oy]
            spread.append((_remote(v_ref, slots.at[my_chip], vec_send.at[p], vec_recv.at[p], (ox, oy, c)),
                           _remote(here, here, vec_send.at[p], vec_recv.at[p], (ox, oy, c))))
        for send, _ in spread:
            send.start()
        copies = []
        for w, (h, _) in enumerate(halves):
            step = SUM_ROWS if h % SUM_ROWS == 0 else h

            def sum_rows(i, _, w=w, step=step):
                r = pl.ds(pl.multiple_of(i * step, step), step)
                stage[w][r, :] = functools.reduce(lambda total, p: total + lands[w][p, r, :].astype(F32), range(3),
                                                  sums[w][0, r, :].astype(F32))
                return 0

            lax.fori_loop(0, h // step, sum_rows, 0)
            mine, theirs = outs[w].at[pl.ds(c * h, h)], outs[w].at[pl.ds((1 - c) * h, h)]
            copies.append((pltpu.make_async_copy(stage[w], mine, kept.at[w]),
                           _remote(stage[w], mine, half_send.at[w], half_recv.at[w], sibling),
                           _remote(theirs, theirs, half_send.at[w], half_recv.at[w], sibling)))
            copies[-1][0].start()
            copies[-1][1].start()
        for send, arrival in spread:
            arrival.wait_recv()
            send.wait_send()
        core_sums[c] = functools.reduce(lambda total, chip: total + slots[chip], range(1, N_CHIPS), slots[0])
        mine, theirs = core_sums.at[c], core_sums.at[1 - c]
        to_sibling = _remote(mine, mine, sum_send.at[0], sum_recv.at[0], sibling)
        to_sibling.start()
        _remote(theirs, theirs, sum_send.at[0], sum_recv.at[0], sibling).wait_recv()
        to_sibling.wait_send()
        o_ref[...] = core_sums[0] + core_sums[1]
        for keep, send, arrival in copies:
            keep.wait()
            arrival.wait_recv()
            send.wait_send()

    once = dict(pipeline_mode=pl.Buffered(1))
    vm = pl.BlockSpec((rows, LANES), lambda i, pr: (0, 0))
    res = pl.pallas_call(
        body, name="finish_gradients",
        grid_spec=pltpu.PrefetchScalarGridSpec(
            num_scalar_prefetch=1, grid=(1,),
            in_specs=[pl.BlockSpec((1,) + hc, lambda i, pr: (pr[1], 0, 0), **once) for hc in halves]
            + [pl.BlockSpec((3,) + hc, lambda i, pr: (0, 0, 0), **once) for hc in halves] + [vm],
            out_specs=[ANY] * n_w + [vm],
            scratch_shapes=[pltpu.VMEM(hc, F32) for hc in halves]
            + [pltpu.SemaphoreType.DMA((n_w,))] * 3
            + [pltpu.VMEM((N_CHIPS, rows, LANES), F32), pltpu.VMEM((2, rows, LANES), F32),
               pltpu.SemaphoreType.DMA((N_CHIPS - 1,)), pltpu.SemaphoreType.DMA((N_CHIPS - 1,)),
               pltpu.SemaphoreType.DMA((1,)), pltpu.SemaphoreType.DMA((1,))]),
        out_shape=[_sds((2 * h, cols), F32) for h, cols in halves] + [_sds(vec.shape, F32)],
        compiler_params=_params("arbitrary"),
    )(place, *pair_sums, *landed, vec)
    return res[:n_w], res[n_w]


def _row_tile(rows):
    fits = [tr for tr in range(16, min(rows, 512) + 1, 16) if rows % tr == 0]
    return max(fits) if fits else rows


def _pair_sum(name, place, grad, theirs):
    if grad.ndim == 2:
        return _pair_sum_joined(name, place, grad, theirs)
    n, r, c = grad.shape
    half = r // 2
    tr = _row_tile(half)
    nb = half // tr

    def body(place_ref, g_ref, t_ref, o_ref):
        o_ref[...] = (g_ref[...] + t_ref[...]).astype(BF16)

    return pl.pallas_call(
        body, name=name, out_shape=jax.ShapeDtypeStruct((n, half, c), BF16),
        grid_spec=pltpu.PrefetchScalarGridSpec(
            num_scalar_prefetch=1, grid=(n, nb),
            in_specs=[pl.BlockSpec((1, tr, c), lambda j, i, pr: (j, pr[0] * nb + i, 0)),
                      pl.BlockSpec((1, tr, c), lambda j, i, pr: (j, i, 0))],
            out_specs=pl.BlockSpec((1, tr, c), lambda j, i, pr: (j, i, 0))),
        compiler_params=_params("parallel", "parallel"),
    )(place, grad, theirs)


def _pair_sum_joined(name, place, grad, theirs):
    r, wide = grad.shape
    half, c = r // 2, wide // N_CHIPS
    tr = _row_tile(half)
    nb = half // tr

    def body(place_ref, g_ref, t_ref, o_ref):
        for j in range(N_CHIPS):
            cols = slice(j * c, (j + 1) * c)
            o_ref[j] = (g_ref[:, cols] + t_ref[:, cols]).astype(BF16)

    return pl.pallas_call(
        body, name=name, out_shape=jax.ShapeDtypeStruct((N_CHIPS, half, c), BF16),
        grid_spec=pltpu.PrefetchScalarGridSpec(
            num_scalar_prefetch=1, grid=(nb,),
            in_specs=[pl.BlockSpec((tr, wide), lambda i, pr: (pr[0] * nb + i, 0)),
                      pl.BlockSpec((tr, wide), lambda i, pr: (i, 0))],
            out_specs=pl.BlockSpec((N_CHIPS, tr, c), lambda i, pr: (0, i, 0))),
        compiler_params=_params("parallel"),
    )(place, grad, theirs)


MIXER = ("w_branch_a", "w_branch_b", "w_out")
FFN_PLE = ("w_ffn_gate", "w_ffn_up", "w_ffn_down", "w_ple_gate", "w_ple_proj")
LATE = MIXER + FFN_PLE
BIG = ("w_in",) + LATE
HELD_TRANSPOSED = ("w_ffn_gate", "w_ffn_up")
SMALL = ("norm_mix", "w_pool", "pool_scale", "norm_ffn", "norm_ple", "norm_final")


def _join_columns(w4):
    return jnp.concatenate([w4[j] for j in range(N_CHIPS)], axis=1)


def _sds(shape, dtype):
    return jax.ShapeDtypeStruct(shape, dtype)


def _local_step(x, p, target, wf, small, ex=None):
    t, d = x.shape
    w_pool_b = small["w_pool"].astype(BF16)
    dp = w_pool_b.shape[0] * w_pool_b.shape[1]

    ex = ex or _NoExchanges()
    h1, first = _norm_fwd("norm_mix", x, small["norm_mix"], rider=ex.gather(("w_in",)))
    wf = {**wf, **dict(zip(("w_in",), first))}
    w_in = wf["w_in"]
    u, q, kv, ga, gb = _mm(
        "proj", [h1], [w_in[j] for j in range(N_CHIPS)], "nn",
        [_sds((t, dp), F32), _sds((t, dp), BF16), _sds((t, d), BF16), _sds((t, d), BF16), _sds((t, d), BF16)],
        separate=True, epilogue=lambda uq, kv_, ga_, gb_: (uq[:, :dp], uq[:, dp:], kv_, ga_, gb_), tm=512)
    pooled, ya = _pool_fwd(u, w_pool_b, small["pool_scale"])
    n_pairs = dp // LANES
    yb, late = _attn_fwd(q, 0, kv, 0, n_pairs, n_pairs, rider=ex.gather(LATE))
    wf = {**wf, **dict(zip(LATE, late))}
    w_down = wf["w_ffn_down"].reshape(-1, d)
    dff = w_down.shape[0]
    w_gate_t, w_up_t = wf["w_ffn_gate"].reshape(dff, d), wf["w_ffn_up"].reshape(dff, d)
    w_a, w_b, w_pp = _join_columns(wf["w_branch_a"]), _join_columns(wf["w_branch_b"]), _join_columns(wf["w_ple_proj"])
    w_out = wf["w_out"].reshape(d, d)
    w_pg = wf["w_ple_gate"].reshape(d, d)
    def residual_norm(branch, xv, g, w):
        xn = xv + jnp.dot(branch.astype(BF16), w, preferred_element_type=F32)
        return xn, xn * lax.rsqrt(jnp.mean(xn * xn, axis=-1, keepdims=True) + RMS_EPS) * g

    def mixer_tail(tav, tbv, gav, gbv, xv, g, w):
        merged = _sigmoid(gav) * tav + _sigmoid(gbv) * tbv
        return (tav, tbv, merged) + residual_norm(merged, xv, g, w)

    def ffn_tail(gv, uv, xv, g, w):
        act = gv * _sigmoid(gv) * uv
        return (gv, uv, act) + residual_norm(act, xv, g, w)

    stream = [_sds((t, d), F32), _sds((t, d), BF16)]
    ta, tb, merged, x1, h2 = _mm(
        "mixer_out", [ya, yb], [w_a, w_b], "nn", [_sds((t, d), BF16)] * 3 + stream,
        extras=[ga, gb, x, small["norm_ffn"]], wholes=[w_out], separate=True, epilogue=mixer_tail, tm=512)
    gate, up, act, x2, h3 = _mm(
        "ffn", [h2], [w_gate_t, w_up_t], "nt", [_sds((t, dff), BF16)] * 3 + stream,
        extras=[x1, small["norm_ple"]], wholes=[w_down], separate=True, epilogue=ffn_tail, tm=256)
    dx2, dx2_b, d_pp, d_gp, d_norm_final, loss_row, d_norm_ple = _mm(
        "ple_loss", [h3, p], [w_pg, w_pp], "nn", stream + [_sds((t, d), BF16)] * 2,
        extras=[x2, target, small["norm_final"].reshape(1, d), small["norm_ple"]], wholes=[w_pg], separate=True,
        epilogue=_ple_and_loss, sum_shapes=[_sds((1, d), F32)] * 3, tm=512)

    def through_norm(dh, xv, g, dres):
        dx, d_gain = _rms_norm_bwd(dh, xv, g)
        return dx + dres, dx + dres, d_gain

    gain_sum = [_sds((1, d), F32)]
    g_w_pp, g_w_pg = _mm_tn("g_ple", [p, h3], [d_pp, d_gp])

    def ffn_bwd(d_act, gv, uv, xv, g, dres, wg_t, wu_t):
        s = _sigmoid(gv)
        d_gate, d_up = d_act * uv * (s * (1.0 + gv * (1.0 - s))), d_act * (gv * s)
        dh2 = (jnp.dot(d_gate.astype(BF16), wg_t, preferred_element_type=F32)
               + jnp.dot(d_up.astype(BF16), wu_t, preferred_element_type=F32))
        return (d_gate, d_up) + through_norm(dh2, xv, g, dres)

    d_gate, d_up, dx1, dx1_b, d_norm_ffn = _mm(
        "ffn_bwd", [dx2_b], [w_down], "nt", [_sds((t, dff), BF16)] * 2 + stream,
        extras=[gate, up, x1, small["norm_ffn"], dx2], wholes=[w_gate_t, w_up_t], epilogue=ffn_bwd,
        sum_shapes=gain_sum, tm=256)
    g_w_down, = _mm_tn("g_ffn_down", [act], [dx2_b], tmm=512)
    g_w_gate_t, g_w_up_t = _mm_tn("g_ffn_gate_up", [d_gate, d_up], [h2], k_blocks=2)

    def merge_bwd(acc, tav, tbv, gav, gbv):
        sa, sb = _sigmoid(gav), _sigmoid(gbv)
        return acc * sa, acc * sb, acc * tav * sa * (1.0 - sa), acc * tbv * sb * (1.0 - sb)

    big = {
        "w_ffn_gate": g_w_gate_t.reshape(wf["w_ffn_gate"].shape), "w_ffn_up": g_w_up_t.reshape(wf["w_ffn_up"].shape),
        "w_ffn_down": g_w_down.reshape(wf["w_ffn_down"].shape),
        "w_ple_gate": g_w_pg.reshape(wf["w_ple_gate"].shape), "w_ple_proj": g_w_pp,
    }
    (d_ta, d_tb, d_ga, d_gb), theirs = _mm(
        "d_merged", [dx1_b], [w_out], "nt", [_sds((t, d), BF16)] * 4, extras=[ta, tb, ga, gb], epilogue=merge_bwd,
        tm=512, rider=ex.pair(FFN_PLE, big))
    ex.paired(FFN_PLE, big, theirs)
    g_w_out, big["w_branch_a"], big["w_branch_b"] = _mm_tn("g_mixer", [merged, ya, yb], [dx1_b, d_ta, d_tb])
    big["w_out"] = g_w_out.reshape(wf["w_out"].shape)
    (d_ya, d_yb), theirs = _mm(
        "d_branches", [d_ta, d_tb], [w_a, w_b], "nt", [_sds((t, dp), F32), _sds((t, dp), BF16)], separate=True,
        rider=ex.pair(MIXER, big))
    ex.paired(MIXER, big, theirs)
    d_u, g_w_pool, d_pool_scale = _pool_bwd(d_ya, pooled, w_pool_b, small["pool_scale"])
    (d_q, d_k, d_v), landed = _attn_bwd(q, 0, kv, 0, n_pairs, d_yb, n_pairs, rider=ex.chip(LATE))
    ex.landed.update(zip(LATE, landed))
    d_proj = [(d_u, d_q), (d_k, d_v), d_ga, d_gb]
    big["w_in"], = _mm_tn("g_w_in", [h1], d_proj, tmm=512, stacked=True)
    ex.paired(("w_in",), big, _run_exchange("pair_exchange_w_in", ex.pair(("w_in",), big)))
    (grad_x, d_norm_mix), landed = _mm(
        "d_h1", d_proj, [w_in[j] for j in range(N_CHIPS)], "nt", [_sds((t, d), F32)],
        extras=[x, small["norm_mix"], dx1], epilogue=lambda dh, xv, g, dres: through_norm(dh, xv, g, dres)[1:],
        sum_shapes=gain_sum, tm=512, rider=ex.chip(("w_in",)))
    ex.landed.update(zip(("w_in",), landed))
    small_g = {"norm_mix": d_norm_mix, "w_pool": g_w_pool, "pool_scale": d_pool_scale, "norm_ffn": d_norm_ffn,
               "norm_ple": d_norm_ple, "norm_final": d_norm_final}
    return grad_x, big, small_g, loss_row


def _pack_small(small_g, loss_row):
    parts, layout = [], []
    for name in SMALL + ("loss",):
        v = (loss_row if name == "loss" else small_g[name]).reshape(-1, LANES)
        pad = (-v.shape[0]) % 8
        if pad:
            v = jnp.concatenate([v, jnp.zeros((pad, LANES), F32)], axis=0)
        layout.append((name, sum(q.shape[0] for q in parts), v.shape[0]))
        parts.append(v)
    return jnp.concatenate(parts, axis=0), layout


def kernel(x, p, norm_mix, w_in, w_pool, pool_scale, w_branch_a, w_branch_b, w_out, norm_ffn, w_ffn_gate, w_ffn_up, w_ffn_down, norm_ple, w_ple_gate, w_ple_proj, norm_final, loss_target, m_norm_mix, m_w_in, m_w_pool, m_pool_scale, m_w_branch_a, m_w_branch_b, m_w_out, m_norm_ffn, m_w_ffn_gate, m_w_ffn_up, m_w_ffn_down, m_norm_ple, m_w_ple_gate, m_w_ple_proj, m_norm_final, v_norm_mix, v_w_in, v_w_pool, v_pool_scale, v_w_branch_a, v_w_branch_b, v_w_out, v_norm_ffn, v_w_ffn_gate, v_w_ffn_up, v_w_ffn_down, v_norm_ple, v_w_ple_gate, v_w_ple_proj, v_norm_final):
    given = dict(locals())
    order = ("norm_mix", "w_in", "w_pool", "pool_scale", "w_branch_a", "w_branch_b", "w_out", "norm_ffn", "w_ffn_gate",
             "w_ffn_up", "w_ffn_down", "norm_ple", "w_ple_gate", "w_ple_proj", "norm_final")
    t, d = x.shape[1], x.shape[2]
    def local(a, n):
        return jnp.swapaxes(a[0], 0, 1) if n in HELD_TRANSPOSED else a[0]

    def back(a, n):
        return (jnp.swapaxes(a, 0, 1) if n in HELD_TRANSPOSED else a)[None]

    shard = {n: local(given[n], n) for n in BIG}
    small = {"norm_mix": norm_mix, "w_pool": w_pool[0], "pool_scale": pool_scale, "norm_ffn": norm_ffn,
             "norm_ple": norm_ple, "norm_final": norm_final}

    place = jnp.stack([lax.axis_index("c"), 2 * lax.axis_index("x") + lax.axis_index("y")]).astype(jnp.int32)
    ex = _StepExchanges({n: shard[n].astype(BF16) for n in BIG}, place)
    grad_x, _, small_g, loss_row = _local_step(
        x.reshape(t, d), p.reshape(t, p.shape[-1]), loss_target.reshape(t, d), {}, small, ex)
    packed, layout = _pack_small(small_g, loss_row)
    filled, reduced = _finish_gradients(place, [ex.pair_sums[n] for n in BIG], [ex.landed[n] for n in BIG], packed)
    grads = dict(zip(BIG, filled))
    for name, start, rows in layout:
        if name == "loss":
            loss = jnp.sum(reduced[start:start + rows])
        else:
            n_el = small[name].size
            grads[name] = reduced[start:start + rows].reshape(-1)[:n_el]

    deltas, new_m, new_v = {}, {}, {}
    for n in order:
        if n in BIG:
            w, m, v = shard[n], local(given["m_" + n], n), local(given["v_" + n], n)
            dl, mn, vn = _adamw(f"adamw_{n}", w, grads[n], m, v)
            grads[n], deltas[n], new_m[n], new_v[n] = [back(a, n) for a in (grads[n], dl, mn, vn)]
        else:
            w, full = small[n], given[n].shape
            shape2 = (1, w.shape[0]) if w.ndim == 1 else (w.shape if w.ndim == 2 else (w.shape[0] * w.shape[1], w.shape[2]))
            dl, mn, vn = _adamw(f"adamw_{n}", w.reshape(shape2), grads[n].reshape(shape2),
                                given["m_" + n].reshape(shape2), given["v_" + n].reshape(shape2))
            grads[n], deltas[n], new_m[n], new_v[n] = [a.reshape(full) for a in (grads[n], dl, mn, vn)]

    return (loss, grad_x.reshape(x.shape), *[grads[n] for n in order], *[deltas[n] for n in order],
            *[new_m[n] for n in order], *[new_v[n] for n in order])
```

```python
import functools
import math

import jax
import jax.numpy as jnp
from jax import lax
from jax.experimental import pallas as pl
from jax.experimental.pallas import tpu as pltpu

F32 = jnp.float32
BF16 = jnp.bfloat16
MESH = pl.DeviceIdType.MESH

RMS_EPS = 1e-6
POOL_WINDOWS = (2, 4, 8, 16)
POOL_HALO = 16
HEAD_DIM = 64
LANES = 128
ATT_BLOCK = 256
ATT_CHAINS = 2
ATT_FWD_CHAINS = 4
ATT_CHUNK = 256
ATT_SLAB = 256
ATT_SCALE = 1.0 / math.sqrt(HEAD_DIM)
LOG2_E = 1.4426950408889634
ATT_EXIT_BELOW = -150.5
ADAM_LR, ADAM_B1, ADAM_B2, ADAM_EPS, ADAM_WD, ADAM_STEP = 0.001, 0.9, 0.999, 1e-08, 0.01, 10
V7X_VMEM_LIMIT_BYTES = 56 * 1024 * 1024
N_CHIPS = 4
N_DEV = 8


def _params(*semantics):
    return pltpu.CompilerParams(dimension_semantics=semantics, vmem_limit_bytes=V7X_VMEM_LIMIT_BYTES)


def _sigmoid(z):
    return 1.0 / (1.0 + jnp.exp(-z))


def _tiled_spec(shape, tm, tn, n_total, at):
    rows, width = shape
    if rows == 1:
        if width == n_total:
            return pl.BlockSpec((1, tn), at(lambda i, j: (0, j)))
        return pl.BlockSpec((1, width), at(lambda i, j: (0, 0)))
    if width == n_total:
        return pl.BlockSpec((tm, tn), at(lambda i, j: (i, j)))
    assert tn == n_total, "an operand narrower than the output needs whole output rows per tile"
    return pl.BlockSpec((tm, width), at(lambda i, j: (i, 0)))


def _column_pieces(operands):
    pieces = [tuple(a) if isinstance(a, (tuple, list)) else (a,) for a in operands]
    return [p for ps in pieces for p in ps], [len(ps) for ps in pieces]


def _load_bf16(refs, counts, rows=slice(None)):
    tiles, k = [], 0
    for n in counts:
        parts = [r[rows, :] for r in refs[k:k + n]]
        parts = [t if t.dtype == BF16 else t.astype(BF16) for t in parts]
        tiles.append(parts[0] if n == 1 else jnp.concatenate(parts, axis=1))
        k += n
    return tiles


def _mm(name, a_list, b_list, mode, out_shapes, epilogue=None, extras=(), tm=1024, tn=None, separate=False,
        sum_shapes=(), rider=None, wholes=(), passes=1):
    flat_a, counts = _column_pieces(a_list)
    m_total = flat_a[0].shape[0]
    n_total = b_list[0].shape[1] if mode == "nn" else b_list[0].shape[0]
    tn = n_total if tn is None else tn
    tm = min(tm, m_total)
    assert m_total % tm == 0 and n_total % tn == 0 and (not sum_shapes or tn == n_total) and tm % passes == 0
    n_a, n_b, n_extra, n_out = len(counts), len(b_list), len(extras), len(out_shapes)
    assert n_a in (1, n_b)
    dims = (((1,), (0,)), ((), ())) if mode == "nn" else (((1,), (1,)), ((), ()))
    with_rider = rider is not None
    rider = rider or _NoRider()
    grid = (n_total // tn, m_total // tm)

    def at(index):
        return lambda j, i: index(i, j)

    def body(*refs):
        ins, o_refs, _, riding = rider.split(refs, len(flat_a) + n_b + n_extra + len(wholes), n_out + len(sum_shapes))
        a_refs, b_refs = ins[:len(flat_a)], ins[len(flat_a):len(flat_a) + n_b]
        e_refs, w_refs = ins[len(flat_a) + n_b:len(flat_a) + n_b + n_extra], ins[len(flat_a) + n_b + n_extra:]
        at_first = (pl.program_id(0) == 0) & (pl.program_id(1) == 0)
        at_last = (pl.program_id(0) == grid[0] - 1) & (pl.program_id(1) == grid[1] - 1)
        top, bottom = rider.at_steps(riding, at_first, at_first, at_last)
        top()
        row_sums = []
        for block in range(passes):
            rows = slice(block * (tm // passes), (block + 1) * (tm // passes))
            lefts = _load_bf16(a_refs, counts, rows)
            products = [lax.dot_general(lefts[s % n_a], b_refs[s][...], dims, preferred_element_type=F32)
                        for s in range(n_b)]
            if not separate:
                products = [functools.reduce(lambda p, r: p + r, products)]
            extra_tiles = [(e[...] if e.shape[0] == 1 else e[rows, :]).astype(F32) for e in e_refs]
            outs = products if epilogue is None else epilogue(*products, *extra_tiles, *[w[...] for w in w_refs])
            for o_ref, o in zip(o_refs[:n_out], outs[:n_out]):
                o_ref[rows, :] = o.astype(o_ref.dtype)
            row_sums = [s + more for s, more in zip(row_sums, outs[n_out:])] if block else list(outs[n_out:])
        if sum_shapes:
            @pl.when(pl.program_id(1) == 0)
            def _():
                for s_ref in o_refs[n_out:]:
                    s_ref[...] = jnp.zeros_like(s_ref)

            for s_ref, s in zip(o_refs[n_out:], row_sums):
                s_ref[...] += s
        bottom()

    once = dict(pipeline_mode=pl.Buffered(1)) if tn == n_total else {}
    in_specs = [pl.BlockSpec((tm, a.shape[1]), at(lambda i, j: (i, 0))) for a in flat_a]
    if mode == "nn":
        in_specs += [pl.BlockSpec((b.shape[0], tn), at(lambda i, j: (0, j)), **once) for b in b_list]
    else:
        in_specs += [pl.BlockSpec((tn, b.shape[1]), at(lambda i, j: (j, 0)), **once) for b in b_list]
    in_specs += [_tiled_spec(e.shape, tm, tn, n_total, at) for e in extras]
    in_specs += [pl.BlockSpec(w.shape, lambda j, i: (0, 0), pipeline_mode=pl.Buffered(1)) for w in wholes]
    out_specs = [_tiled_spec(o.shape, tm, tn, n_total, at) for o in out_shapes]
    out_specs += [pl.BlockSpec(s.shape, at(lambda i, j: (0, 0))) for s in sum_shapes]
    semantics = ("arbitrary", "arbitrary") if sum_shapes or rider.operands else ("parallel", "parallel")
    res = pl.pallas_call(
        body, name=name, grid=grid, in_specs=in_specs + [ANY] * len(rider.operands),
        out_specs=out_specs + [ANY] * len(rider.out_shapes),
        out_shape=list(out_shapes) + list(sum_shapes) + list(rider.out_shapes), scratch_shapes=list(rider.scratch),
        compiler_params=_params(*semantics),
    )(*flat_a, *b_list, *extras, *wholes, *rider.operands)
    n_own = len(out_shapes) + len(sum_shapes)
    return (res[:n_own], res[n_own:]) if with_rider else res


def _mm_tn(name, a_list, b_list, tmm=1024, stacked=False, k_blocks=1):
    flat_b, counts = _column_pieces(b_list)
    n_a, n_b = len(a_list), len(counts)
    n_prod = max(n_a, n_b)
    m_total = a_list[0].shape[0]
    ks = [a_list[s % n_a].shape[1] for s in range(n_prod)]
    widths = [sum(p.shape[1] for p in flat_b[sum(counts[:s]):sum(counts[:s + 1])]) for s in range(n_b)]
    widths = [widths[s % n_b] for s in range(n_prod)]
    tmm = min(tmm, m_total)
    assert m_total % tmm == 0 and all(k % k_blocks == 0 for k in ks)
    assert n_a in (1, n_prod) and n_b in (1, n_prod) and not (stacked and n_a > 1)

    def body(*refs):
        a_refs, b_refs, o_refs = refs[:n_a], refs[n_a:n_a + len(flat_b)], refs[n_a + len(flat_b):]

        @pl.when(pl.program_id(1) == 0)
        def _():
            for o_ref in o_refs:
                o_ref[...] = jnp.zeros_like(o_ref)

        lefts, rights = _load_bf16(a_refs, [1] * n_a), _load_bf16(b_refs, counts)
        for s in range(n_prod):
            product = lax.dot_general(lefts[s % n_a], rights[s % n_b], (((0,), (0,)), ((), ())),
                                      preferred_element_type=F32)
            if stacked:
                o_refs[0][s] += product
            else:
                o_refs[s][...] += product

    in_specs = [pl.BlockSpec((tmm, a.shape[1] // k_blocks), lambda kb, m: (m, kb)) for a in a_list]
    in_specs += [pl.BlockSpec((tmm, b.shape[1]), lambda kb, m: (m, 0)) for b in flat_b]
    if stacked:
        out_shape = [jax.ShapeDtypeStruct((n_prod, ks[0], widths[0]), F32)]
        out_specs = [pl.BlockSpec((n_prod, ks[0] // k_blocks, widths[0]), lambda kb, m: (0, kb, 0))]
    else:
        out_shape = [jax.ShapeDtypeStruct((k, w), F32) for k, w in zip(ks, widths)]
        out_specs = [pl.BlockSpec((k // k_blocks, w), lambda kb, m: (kb, 0)) for k, w in zip(ks, widths)]
    return pl.pallas_call(
        body, name=name, grid=(k_blocks, m_total // tmm), in_specs=in_specs, out_specs=out_specs, out_shape=out_shape,
        compiler_params=_params("arbitrary", "arbitrary"),
    )(*a_list, *flat_b)


def _rows(name, fn, ins, tile_outs, sum_outs=(), tr=512, rider=None):
    t_total = max(a.shape[0] for a in ins)
    tr = min(tr, t_total)
    assert t_total % tr == 0
    n_in, n_tile = len(ins), len(tile_outs)
    rider = rider or _NoRider()
    n_steps = t_total // tr

    def body(*refs):
        own_ins, own_outs, _, riding = rider.split(refs, n_in, n_tile + len(sum_outs))
        step = pl.program_id(0)
        top, bottom = rider.at_steps(riding, step == 0, step == n_steps - 1, step == n_steps - 1)
        top()
        refs = tuple(own_ins) + tuple(own_outs)
        outs = fn(*[r[...].astype(F32) for r in refs[:n_in]])
        for o_ref, o in zip(refs[n_in:n_in + n_tile], outs[:n_tile]):
            o_ref[...] = o.astype(o_ref.dtype)
        if sum_outs:
            @pl.when(pl.program_id(0) == 0)
            def _():
                for s_ref in refs[n_in + n_tile:]:
                    s_ref[...] = jnp.zeros_like(s_ref)

            for s_ref, s in zip(refs[n_in + n_tile:], outs[n_tile:]):
                s_ref[...] += s
        bottom()

    def spec(shape):
        if shape[0] == 1:
            return pl.BlockSpec(shape, lambda i: (0, 0))
        return pl.BlockSpec((tr, shape[1]), lambda i: (i, 0))

    return pl.pallas_call(
        body, name=name, grid=(n_steps,), in_specs=[spec(a.shape) for a in ins] + [ANY] * len(rider.operands),
        out_specs=[spec(o.shape) for o in tile_outs] + [spec(s.shape) for s in sum_outs] + [ANY] * len(rider.out_shapes),
        out_shape=list(tile_outs) + list(sum_outs) + list(rider.out_shapes), scratch_shapes=list(rider.scratch),
        compiler_params=_params("arbitrary" if sum_outs or rider.operands else "parallel"),
    )(*ins, *rider.operands)


def _norm_fwd(name, x, gain, rider=None):
    def fn(xv, g):
        inv = lax.rsqrt(jnp.mean(xv * xv, axis=-1, keepdims=True) + RMS_EPS)
        return (xv * inv * g,)

    res = _rows(name, fn, [x, gain], [jax.ShapeDtypeStruct(x.shape, BF16)], rider=rider)
    return res[0], res[1:]


def _rms_norm_bwd(dh, xv, g):
    inv = lax.rsqrt(jnp.mean(xv * xv, axis=-1, keepdims=True) + RMS_EPS)
    xn = xv * inv
    dxn = dh * g
    return inv * (dxn - xn * jnp.mean(dxn * xn, axis=-1, keepdims=True)), jnp.sum(dh * xn, axis=0, keepdims=True)


def _ple_and_loss(gv, pv, x2v, tv, g_final, g_ple, w_pg):
    d = x2v.shape[1]
    s = _sigmoid(gv)
    xv = x2v + s * pv
    inv = lax.rsqrt(jnp.mean(xv * xv, axis=-1, keepdims=True) + RMS_EPS)
    err = xv * inv * g_final - tv
    dx3, d_final = _rms_norm_bwd(err * (1.0 / d), xv, g_final)
    d_pp, d_gp = dx3 * s, dx3 * pv * s * (1.0 - s)
    dh3 = lax.dot_general(d_gp.astype(BF16), w_pg, (((1,), (1,)), ((), ())), preferred_element_type=F32)
    dx2, d_ple = _rms_norm_bwd(dh3, x2v, g_ple)
    dx2 = dx2 + dx3
    return dx2, dx2, d_pp, d_gp, d_final, (0.5 / d) * jnp.sum(err * err, axis=0, keepdims=True), d_ple


def _window_counts(t_pos, w):
    return jnp.minimum(t_pos + 1, w).astype(F32)


def _pool_fwd(u, w_pool, scale, tr=512):
    t_total, width = u.shape
    tr = min(tr, t_total)
    n_groups = len(POOL_WINDOWS)
    gdim = width // n_groups
    ext = tr + POOL_HALO

    def body(u_ref, halo_ref, w_ref, s_ref, pooled_ref, ya_ref):
        i = pl.program_id(0)
        halo = jnp.where(i == 0, 0.0, halo_ref[...])
        t_pos = i * tr + lax.broadcasted_iota(jnp.int32, (tr, 1), 0)
        for g, w in enumerate(POOL_WINDOWS):
            cols = slice(g * gdim, (g + 1) * gdim)
            main = u_ref[:, cols]
            win = jnp.concatenate([halo[:, cols], main], axis=0)
            span = 1
            while span < w:
                win = win + pltpu.roll(win, span, 0)
                span *= 2
            pooled = win[POOL_HALO:, :] * (1.0 / _window_counts(t_pos, w)) - main
            pooled_b = pooled.astype(BF16)
            pooled_ref[:, cols] = pooled_b
            mixed = jnp.dot(pooled_b, w_ref[g], preferred_element_type=F32)
            ya_ref[:, cols] = (mixed * s_ref[:, cols]).astype(BF16)

    hb = tr // POOL_HALO
    return pl.pallas_call(
        body, name="pool_fwd", grid=(t_total // tr,),
        in_specs=[pl.BlockSpec((tr, width), lambda i: (i, 0)),
                  pl.BlockSpec((POOL_HALO, width), lambda i: (jnp.maximum(i * hb - 1, 0), 0)),
                  pl.BlockSpec((n_groups, gdim, gdim), lambda i: (0, 0, 0)),
                  pl.BlockSpec((1, width), lambda i: (0, 0))],
        out_specs=[pl.BlockSpec((tr, width), lambda i: (i, 0)), pl.BlockSpec((tr, width), lambda i: (i, 0))],
        out_shape=[jax.ShapeDtypeStruct(u.shape, BF16), jax.ShapeDtypeStruct(u.shape, BF16)],
        compiler_params=_params("parallel"),
    )(u, u, w_pool, scale)


def _pool_bwd(dya, pooled, w_pool, scale, tr=512):
    t_total, width = dya.shape
    tr = min(tr, t_total)
    n_groups = len(POOL_WINDOWS)
    gdim = width // n_groups
    ext = tr + POOL_HALO
    n_tiles = t_total // tr

    def body(d_ref, halo_ref, p_ref, w_ref, s_ref, du_ref, dw_ref, ds_ref):
        i = pl.program_id(0)

        @pl.when(i == 0)
        def _():
            dw_ref[...] = jnp.zeros_like(dw_ref)
            ds_ref[...] = jnp.zeros_like(ds_ref)

        halo = jnp.where(i == n_tiles - 1, 0.0, halo_ref[...])
        t_pos = i * tr + lax.broadcasted_iota(jnp.int32, (ext, 1), 0)
        for g, w in enumerate(POOL_WINDOWS):
            cols = slice(g * gdim, (g + 1) * gdim)
            sc = s_ref[:, cols]
            d_main = d_ref[:, cols]
            pooled_b = p_ref[:, cols]
            mixed = jnp.dot(pooled_b, w_ref[g], preferred_element_type=F32)
            ds_ref[:, cols] += jnp.sum(d_main * mixed, axis=0, keepdims=True)
            dmix = (jnp.concatenate([d_main, halo[:, cols]], axis=0) * sc).astype(BF16)
            dw_ref[g] += lax.dot_general(pooled_b, dmix[:tr, :], (((0,), (0,)), ((), ())),
                                         preferred_element_type=F32)
            dpool = lax.dot_general(dmix, w_ref[g], (((1,), (1,)), ((), ())), preferred_element_type=F32)
            win = dpool * (1.0 / _window_counts(t_pos, w))
            span = 1
            while span < w:
                win = win + pltpu.roll(win, ext - span, 0)
                span *= 2
            du_ref[:, cols] = (win[:tr, :] - dpool[:tr, :]).astype(BF16)

    hb = tr // POOL_HALO
    last_halo = t_total // POOL_HALO - 1
    return pl.pallas_call(
        body, name="pool_bwd", grid=(n_tiles,),
        in_specs=[pl.BlockSpec((tr, width), lambda i: (i, 0)),
                  pl.BlockSpec((POOL_HALO, width), lambda i: (jnp.minimum((i + 1) * hb, last_halo), 0)),
                  pl.BlockSpec((tr, width), lambda i: (i, 0)),
                  pl.BlockSpec((n_groups, gdim, gdim), lambda i: (0, 0, 0)),
                  pl.BlockSpec((1, width), lambda i: (0, 0))],
        out_specs=[pl.BlockSpec((tr, width), lambda i: (i, 0)),
                   pl.BlockSpec((n_groups, gdim, gdim), lambda i: (0, 0, 0)),
                   pl.BlockSpec((1, width), lambda i: (0, 0))],
        out_shape=[jax.ShapeDtypeStruct(dya.shape, BF16), jax.ShapeDtypeStruct((n_groups, gdim, gdim), F32),
                   jax.ShapeDtypeStruct((1, width), F32)],
        compiler_params=_params("arbitrary"),
    )(dya, dya, pooled, w_pool, scale)


def _head_masks():
    lane = lax.broadcasted_iota(jnp.int32, (1, LANES), 1)
    return lane < HEAD_DIM


def _stack_heads(tile, first):
    zero = jnp.zeros_like(tile)
    return jnp.concatenate([jnp.where(first, tile, zero), jnp.where(first, zero, tile)], axis=0)


def _causal_mask(t_pos, k_start):
    col = lax.broadcasted_iota(jnp.int32, (1, 2 * ATT_SLAB), 1)
    return k_start + (col & (ATT_SLAB - 1)) < t_pos


def _slab_scores(q, kd, mask):
    z2 = lax.dot_general(q, kd, (((1,), (1,)), ((), ())), preferred_element_type=F32) * LOG2_E
    log_hit = jnp.minimum(z2, 0.0) - jnp.log2(1.0 + jnp.exp2(-jnp.abs(z2)))
    log_fail = log_hit - z2
    return log_hit, (log_fail if mask is None else jnp.where(mask, log_fail, 0.0))


def _weights(log_hit, suffix, mask):
    arg = log_hit + suffix
    return jnp.exp2(arg if mask is None else jnp.where(mask, arg, -1e30))


def _tri(upper):
    r = lax.broadcasted_iota(jnp.int32, (ATT_CHUNK, ATT_CHUNK), 0)
    c = lax.broadcasted_iota(jnp.int32, (ATT_CHUNK, ATT_CHUNK), 1)
    return jnp.where(r > c if upper else r < c, 1.0, 0.0).astype(BF16)


def _tri_spec():
    return pl.BlockSpec((ATT_CHUNK, ATT_CHUNK), lambda h, i: (0, 0), pipeline_mode=pl.Buffered(1))


def _scan_chunk(v, tri):
    return jnp.dot(v.astype(BF16), tri, preferred_element_type=F32)


def _lane_bcast(col):
    return jnp.broadcast_to(col, (col.shape[0], LANES))


def _scan_slab(v, tri, carries, from_right):
    n_chunks = ATT_SLAB // ATT_CHUNK
    edge = 0 if from_right else ATT_CHUNK - 1
    parts, new_carries = [None] * (2 * n_chunks), []
    for head in range(2):
        run = carries[head]
        for c in (reversed(range(n_chunks)) if from_right else range(n_chunks)):
            lo_col = head * ATT_SLAB + c * ATT_CHUNK
            vc = v[:, lo_col:lo_col + ATT_CHUNK]
            sc = _scan_chunk(vc, tri)
            parts[head * n_chunks + c] = sc + jnp.concatenate([run] * (ATT_CHUNK // LANES), axis=1)
            run = run + _lane_bcast(sc[:, edge:edge + 1] + vc[:, edge:edge + 1])
        new_carries.append(run)
    return jnp.concatenate(parts, axis=1), new_carries


def _fold_heads(stacked, first):
    s = stacked.shape[0] // 2
    return jnp.where(first, stacked[:s], stacked[s:])


class _NoRider:
    operands, out_shapes, scratch = (), (), ()

    def split(self, refs, n_base_in, n_base_out):
        n_in, n_out, n_sem = len(self.operands), len(self.out_shapes), len(self.scratch)
        a = n_base_in + n_in
        b = a + n_base_out + n_out
        mine = (refs[n_base_in:a], refs[a + n_base_out:b], refs[b:b + n_sem])
        return refs[:n_base_in], refs[a:a + n_base_out], refs[b + n_sem:], mine

    def start(self, ins, outs, sems):
        pass

    def relay(self, ins, outs, sems):
        pass

    def finish(self, ins, outs, sems):
        pass

    def at_steps(self, refs, first_step, relay_step, last_step):
        if not self.operands:
            return (lambda: None), (lambda: None)

        def top():
            pl.when(first_step)(lambda: self.start(*refs))
            pl.when(relay_step)(lambda: self.relay(*refs))

        return top, lambda: pl.when(last_step)(lambda: self.finish(*refs))


def _attn_fwd(q_src, q_col, kv_src, k_col, v_col, n_pairs=4, rider=_NoRider()):
    t_total = q_src.shape[0]
    blk = ATT_BLOCK
    n_chains = ATT_FWD_CHAINS if t_total % (ATT_FWD_CHAINS * blk) == 0 else ATT_CHAINS
    n_steps = t_total // (n_chains * blk)
    assert t_total % ATT_SLAB == 0 and ATT_SLAB == ATT_BLOCK

    def body(*refs):
        (q_ref, k_ref, v_ref, suffix_ref), (o_ref,), _, riding = rider.split(refs, 4, 1)
        h, ii = pl.program_id(0), pl.program_id(1)
        top, bottom = rider.at_steps(riding, (h == 0) & (ii == 0), (h == n_pairs - 1) & (ii == 0),
                                     (h == n_pairs - 1) & (ii == n_steps - 1))
        top()
        first = _head_masks()
        suffix_tri = suffix_ref[...]
        blocks = [n_chains * ii + c for c in range(n_chains)]
        qs = [q_ref[c * blk:(c + 1) * blk, :] * ATT_SCALE for c in range(n_chains)]
        t_pos = [b * blk + lax.broadcasted_iota(jnp.int32, (blk, 1), 0) for b in blocks]

        def one(c, t, chain, on_diagonal):
            _, acc, right_a, right_b = chain
            k_start = pl.multiple_of((blocks[c] - t) * ATT_SLAB, ATT_SLAB)
            kd = _stack_heads(k_ref[pl.ds(k_start, ATT_SLAB), :], first)
            vd = _stack_heads(v_ref[pl.ds(k_start, ATT_SLAB), :], first)
            mask = _causal_mask(t_pos[c], k_start) if on_diagonal else None
            log_hit, log_fail = _slab_scores(qs[c], kd, mask)
            suffix, (right_a, right_b) = _scan_slab(log_fail, suffix_tri, (right_a, right_b), from_right=True)
            a = _weights(log_hit, suffix, mask).astype(BF16)
            acc = acc + jnp.dot(a, vd, preferred_element_type=F32)
            return jnp.max(jnp.maximum(right_a, right_b)), acc, right_a, right_b

        def step(state, on_diagonal):
            t, chains = state
            return t + 1, tuple(one(c, t, chains[c], on_diagonal) for c in range(n_chains))

        def more(state):
            t, chains = state
            return (t <= blocks[0]) & (functools.reduce(jnp.maximum, [ch[0] for ch in chains]) > ATT_EXIT_BELOW)

        zero = jnp.zeros((blk, LANES), F32)
        state = step((0, ((jnp.float32(0.0), zero, zero, zero),) * n_chains), on_diagonal=True)
        t, chains = lax.while_loop(more, functools.partial(step, on_diagonal=False), state)
        for c in range(n_chains):
            chain = chains[c]
            if c:
                _, chain = lax.while_loop(
                    lambda s, c=c: (s[0] <= blocks[c]) & (s[1][0] > ATT_EXIT_BELOW),
                    lambda s, c=c: (s[0] + 1, one(c, s[0], s[1], False)), (t, chain))
            o_ref[c * blk:(c + 1) * blk, :] = chain[1].astype(BF16)
        bottom()

    rows = n_chains * blk
    res = pl.pallas_call(
        body, name="attn_fwd", grid=(n_pairs, n_steps),
        in_specs=[pl.BlockSpec((rows, LANES), lambda h, i: (i, q_col + h)),
                  pl.BlockSpec((t_total, LANES), lambda h, i: (0, k_col + h)),
                  pl.BlockSpec((t_total, LANES), lambda h, i: (0, v_col + h)), _tri_spec()] + [ANY] * len(rider.operands),
        out_specs=[pl.BlockSpec((rows, LANES), lambda h, i: (i, h))] + [ANY] * len(rider.out_shapes),
        out_shape=[jax.ShapeDtypeStruct((t_total, n_pairs * LANES), BF16)] + list(rider.out_shapes),
        scratch_shapes=list(rider.scratch),
        compiler_params=_params("arbitrary", "arbitrary"),
    )(q_src, kv_src, kv_src, _tri(upper=True), *rider.operands)
    return res[0], res[1:]


def _attn_bwd(q_src, q_col, kv_src, k_col, v_col, dy, n_pairs=4, rider=_NoRider()):
    t_total = q_src.shape[0]
    blk = ATT_BLOCK
    n_steps = t_total // (ATT_CHAINS * blk)
    n_slabs = t_total // ATT_SLAB
    assert t_total % ATT_SLAB == 0 and ATT_SLAB == ATT_BLOCK

    def body(*refs):
        ins, (dq_ref, dk_ref, dv_ref), (g_s, dk_acc, dv_acc), riding = rider.split(refs, 6, 3)
        q_ref, dy_ref, k_ref, v_ref, suffix_ref, prefix_ref = ins
        h, ii = pl.program_id(0), pl.program_id(1)
        top, bottom = rider.at_steps(riding, (h == 0) & (ii == 0), (h == n_pairs - 1) & (ii == 0),
                                     (h == n_pairs - 1) & (ii == n_steps - 1))
        top()

        @pl.when(ii == 0)
        def _():
            dk_acc[...] = jnp.zeros_like(dk_acc)
            dv_acc[...] = jnp.zeros_like(dv_acc)

        first = _head_masks()
        suffix_tri = suffix_ref[...]
        prefix_tri = prefix_ref[...]
        blocks = [ATT_CHAINS * ii + c for c in range(ATT_CHAINS)]
        rows = [slice(c * blk, (c + 1) * blk) for c in range(ATT_CHAINS)]
        qs = [q_ref[r, :] * ATT_SCALE for r in rows]
        dys = [dy_ref[r, :] for r in rows]
        t_pos = [b * blk + lax.broadcasted_iota(jnp.int32, (blk, 1), 0) for b in blocks]

        def one1(c, t, chain, on_diagonal):
            _, right_a, right_b = chain
            slab = blocks[c] - t
            k_start = pl.multiple_of(slab * ATT_SLAB, ATT_SLAB)
            kd = _stack_heads(k_ref[pl.ds(k_start, ATT_SLAB), :], first)
            vd = _stack_heads(v_ref[pl.ds(k_start, ATT_SLAB), :], first)
            mask = _causal_mask(t_pos[c], k_start) if on_diagonal else None
            log_hit, log_fail = _slab_scores(qs[c], kd, mask)
            suffix, (right_a, right_b) = _scan_slab(log_fail, suffix_tri, (right_a, right_b), from_right=True)
            a = _weights(log_hit, suffix, mask)
            da = lax.dot_general(dys[c], vd, (((1,), (1,)), ((), ())), preferred_element_type=F32)
            g_s[c, slab] = (da * a).astype(BF16)
            dv_acc[pl.ds(k_start, ATT_SLAB), :] += _fold_heads(lax.dot_general(
                a.astype(BF16), dys[c], (((0,), (0,)), ((), ())), preferred_element_type=F32), first)
            return jnp.max(jnp.maximum(right_a, right_b)), right_a, right_b

        def step1(state, on_diagonal):
            t, chains = state
            return t + 1, tuple(one1(c, t, chains[c], on_diagonal) for c in range(ATT_CHAINS))

        def more(state):
            t, chains = state
            return (t <= blocks[0]) & (functools.reduce(jnp.maximum, [ch[0] for ch in chains]) > ATT_EXIT_BELOW)

        zero = jnp.zeros((blk, LANES), F32)
        state = step1((0, ((jnp.float32(0.0), zero, zero),) * ATT_CHAINS), on_diagonal=True)
        joint, chains = lax.while_loop(more, functools.partial(step1, on_diagonal=False), state)
        done = [joint]
        for c in range(1, ATT_CHAINS):
            done.append(lax.while_loop(
                lambda s, c=c: (s[0] <= blocks[c]) & (s[1][0] > ATT_EXIT_BELOW),
                lambda s, c=c: (s[0] + 1, one1(c, s[0], s[1], False)), (joint, chains[c]))[0])

        def one2(c, t, carry, on_diagonal):
            dq, left_a, left_b = carry
            slab = blocks[c] - t
            k_start = pl.multiple_of(slab * ATT_SLAB, ATT_SLAB)
            kd = _stack_heads(k_ref[pl.ds(k_start, ATT_SLAB), :], first)
            g = g_s[c, slab]
            z2 = lax.dot_general(qs[c], kd, (((1,), (1,)), ((), ())), preferred_element_type=F32) * LOG2_E
            sig = 1.0 / (1.0 + jnp.exp2(-z2))
            prefix, (left_a, left_b) = _scan_slab(g, prefix_tri, (left_a, left_b), from_right=False)
            dz = g * (1.0 - sig) - sig * prefix
            if on_diagonal:
                dz = jnp.where(_causal_mask(t_pos[c], k_start), dz, 0.0)
            dz = dz.astype(BF16)
            dq = dq + jnp.dot(dz, kd, preferred_element_type=F32)
            dk_acc[pl.ds(k_start, ATT_SLAB), :] += _fold_heads(lax.dot_general(
                dz, qs[c], (((0,), (0,)), ((), ())), preferred_element_type=F32), first)
            return dq, left_a, left_b

        carries = [(zero, zero, zero)]
        for c in range(1, ATT_CHAINS):
            carries.append(lax.fori_loop(
                0, done[c] - joint, lambda n, carry, c=c: one2(c, done[c] - 1 - n, carry, False), (zero, zero, zero)))
        carries = lax.fori_loop(
            0, joint - 1,
            lambda n, cs: tuple(one2(c, joint - 1 - n, cs[c], False) for c in range(ATT_CHAINS)), tuple(carries))
        for c in range(ATT_CHAINS):
            dq_ref[rows[c], :] = (one2(c, 0, carries[c], True)[0] * ATT_SCALE).astype(BF16)

        @pl.when(ii == n_steps - 1)
        def _():
            dk_ref[...] = dk_acc[...].astype(BF16)
            dv_ref[...] = dv_acc[...].astype(BF16)

        bottom()

    out = jax.ShapeDtypeStruct((t_total, n_pairs * LANES), BF16)
    n_rows = ATT_CHAINS * blk
    whole = dict(pipeline_mode=pl.Buffered(1))
    res = pl.pallas_call(
        body, name="attn_bwd", grid=(n_pairs, n_steps),
        in_specs=[pl.BlockSpec((n_rows, LANES), lambda h, i: (i, q_col + h)),
                  pl.BlockSpec((n_rows, LANES), lambda h, i: (i, h)),
                  pl.BlockSpec((t_total, LANES), lambda h, i: (0, k_col + h), **whole),
                  pl.BlockSpec((t_total, LANES), lambda h, i: (0, v_col + h), **whole), _tri_spec(), _tri_spec()]
        + [ANY] * len(rider.operands),
        out_specs=[pl.BlockSpec((n_rows, LANES), lambda h, i: (i, h)),
                   pl.BlockSpec((t_total, LANES), lambda h, i: (0, h)),
                   pl.BlockSpec((t_total, LANES), lambda h, i: (0, h))] + [ANY] * len(rider.out_shapes),
        out_shape=[out, out, out] + list(rider.out_shapes),
        scratch_shapes=list(rider.scratch) + [pltpu.VMEM((ATT_CHAINS, n_slabs, blk, 2 * ATT_SLAB), BF16),
                                              pltpu.VMEM((t_total, LANES), F32), pltpu.VMEM((t_total, LANES), F32)],
        compiler_params=_params("arbitrary", "arbitrary"),
    )(q_src, dy, kv_src, kv_src, _tri(upper=True), _tri(upper=False), *rider.operands)
    return res[:3], res[3:]


def _adamw(name, w, g, m, v):
    def fn(wv, gv, mv, vv):
        mn = ADAM_B1 * mv + (1.0 - ADAM_B1) * gv
        vn = ADAM_B2 * vv + (1.0 - ADAM_B2) * (gv * gv)
        m_hat = mn / (1.0 - ADAM_B1 ** ADAM_STEP)
        v_hat = vn / (1.0 - ADAM_B2 ** ADAM_STEP)
        return -ADAM_LR * (m_hat / (jnp.sqrt(v_hat) + ADAM_EPS) + ADAM_WD * wv), mn, vn

    rows = w.shape[0]
    tr = _row_tile(rows)
    shp = jax.ShapeDtypeStruct(w.shape, F32)
    if rows == 1:
        def body(w_ref, g_ref, m_ref, v_ref, d_ref, mo_ref, vo_ref):
            d, mn, vn = fn(w_ref[...], g_ref[...], m_ref[...], v_ref[...])
            d_ref[...], mo_ref[...], vo_ref[...] = d, mn, vn

        return pl.pallas_call(body, name=name, out_shape=[shp, shp, shp])(w, g, m, v)
    return _rows(name, fn, [w, g, m, v], [shp, shp, shp], tr=tr)


def _place():
    return lax.axis_index("x"), lax.axis_index("y"), lax.axis_index("c")


def _other_chips(x, y):
    return [(1 - x, y), (x, 1 - y), (1 - x, 1 - y)]


ANY = pl.BlockSpec(memory_space=pl.ANY)


def _remote(src, dst, send_sem, recv_sem, to):
    return pltpu.make_async_remote_copy(src_ref=src, dst_ref=dst, send_sem=send_sem, recv_sem=recv_sem,
                                        device_id=to, device_id_type=MESH)


class _WeightGather(_NoRider):
    def __init__(self, shards):
        n_w = len(shards)
        self.operands = list(shards)
        self.out_shapes = [jax.ShapeDtypeStruct((N_CHIPS,) + s.shape, s.dtype) for s in shards]
        self.scratch = [pltpu.SemaphoreType.DMA((3, n_w))] * 4 + [pltpu.SemaphoreType.DMA((n_w,))] * 2

    def _copies(self, ins, outs, sems):
        send_sems, recv_sems, relay_send, relay_recv, own_send, own_recv = sems
        x, y, c = _place()
        my_chip, sibling = 2 * x + y, (x, y, 1 - c)
        n_w = len(ins)

        def half(w, chip, core):
            h = self.operands[w].shape[0] // 2
            return outs[w].at[chip, pl.ds(core * h, h)]

        own = [_remote(ins[w], outs[w].at[my_chip], own_send.at[w], own_recv.at[w], sibling) for w in range(n_w)]
        sends, landed, relays, relayed = [], [], [], []
        for p, (ox, oy) in enumerate(_other_chips(x, y)):
            for w in range(n_w):
                h = self.operands[w].shape[0] // 2
                sends.append(_remote(ins[w].at[pl.ds(c * h, h)], half(w, my_chip, c), send_sems.at[p, w],
                                     recv_sems.at[p, w], (ox, oy, c)))
                here = half(w, 2 * ox + oy, c)
                landed.append(_remote(here, here, send_sems.at[p, w], recv_sems.at[p, w], (ox, oy, c)))
                relays.append(_remote(here, here, relay_send.at[p, w], relay_recv.at[p, w], sibling))
                there = half(w, 2 * ox + oy, 1 - c)
                relayed.append(_remote(there, there, relay_send.at[p, w], relay_recv.at[p, w], sibling))
        return own, sends, landed, relays, relayed

    def start(self, ins, outs, sems):
        own, sends, _, _, _ = self._copies(ins, outs, sems)
        for cp in own + sends:
            cp.start()

    def relay(self, ins, outs, sems):
        _, _, landed, relays, _ = self._copies(ins, outs, sems)
        for arrival, cp in zip(landed, relays):
            arrival.wait_recv()
            cp.start()

    def finish(self, ins, outs, sems):
        own, sends, _, relays, relayed = self._copies(ins, outs, sems)
        for arrival in relayed:
            arrival.wait_recv()
        for cp in sends + relays:
            cp.wait_send()
        for cp in own:
            cp.wait()


class _ChipExchange(_NoRider):
    def __init__(self, pair_sums):
        n_w = len(pair_sums)
        self.operands = list(pair_sums)
        self.out_shapes = [jax.ShapeDtypeStruct((3,) + s.shape[1:], s.dtype) for s in pair_sums]
        self.scratch = [pltpu.SemaphoreType.DMA((3, n_w))] * 2

    def _copies(self, ins, outs, sems):
        send_sems, recv_sems = sems
        x, y, c = _place()
        return [_remote(ins[w].at[2 * ox + oy], outs[w].at[p], send_sems.at[p, w], recv_sems.at[p, w], (ox, oy, c))
                for p, (ox, oy) in enumerate(_other_chips(x, y)) for w in range(len(ins))]

    def start(self, ins, outs, sems):
        for cp in self._copies(ins, outs, sems):
            cp.start()

    def finish(self, ins, outs, sems):
        for cp in self._copies(ins, outs, sems):
            cp.wait()


class _PairExchange(_NoRider):
    def __init__(self, grads):
        n_w = len(grads)
        self.operands = list(grads)
        self.out_shapes = [jax.ShapeDtypeStruct(g.shape[:-2] + (g.shape[-2] // 2, g.shape[-1]), F32) for g in grads]
        self.scratch = [pltpu.SemaphoreType.DMA((n_w,))] * 2

    def _copies(self, ins, theirs, sems):
        send_sems, recv_sems = sems
        x, y, c = _place()
        sends = []
        for w, g in enumerate(self.operands):
            rows = pl.ds((1 - c) * (g.shape[-2] // 2), g.shape[-2] // 2)
            src = ins[w].at[:, rows, :] if g.ndim == 3 else ins[w].at[rows, :]
            sends.append(_remote(src, theirs[w], send_sems.at[w], recv_sems.at[w], (x, y, 1 - c)))
        return sends

    def start(self, ins, outs, sems):
        for cp in self._copies(ins, outs, sems):
            cp.start()

    def finish(self, ins, outs, sems):
        for cp in self._copies(ins, outs, sems):
            cp.wait()


def _run_exchange(name, plan):
    n_in, n_out = len(plan.operands), len(plan.out_shapes)
    if not n_in:
        return ()

    def body(*refs):
        parts = (refs[:n_in], refs[n_in:n_in + n_out], refs[n_in + n_out:])
        plan.start(*parts)
        plan.relay(*parts)
        plan.finish(*parts)

    return pl.pallas_call(body, name=name, in_specs=[ANY] * n_in, out_specs=[ANY] * n_out,
                          out_shape=list(plan.out_shapes), scratch_shapes=list(plan.scratch))(*plan.operands)


class _NoExchanges:
    pair_sums, landed = {}, {}

    def gather(self, names):
        return _NoRider()

    def pair(self, names, grads):
        return _NoRider()

    def paired(self, names, grads, theirs):
        pass

    def chip(self, names):
        return _NoRider()


class _StepExchanges(_NoExchanges):
    def __init__(self, shards_bf16, place):
        self.shards, self.place = shards_bf16, place
        self.pair_sums, self.landed = {}, {}

    def gather(self, names):
        return _WeightGather([self.shards[n] for n in names])

    def pair(self, names, grads):
        return _PairExchange([grads[n] for n in names])

    def paired(self, names, grads, theirs):
        for n, other in zip(names, theirs):
            self.pair_sums[n] = _pair_sum(f"pair_sum_{n}", self.place, grads[n], other)

    def chip(self, names):
        return _ChipExchange([self.pair_sums[n] for n in names])


SUM_ROWS = 32


def _finish_gradients(place, pair_sums, landed, vec):
    n_w = len(pair_sums)
    rows = vec.shape[0]
    halves = [s.shape[1:] for s in pair_sums]

    def body(place_ref, *refs):
        sums, lands, v_ref = refs[:n_w], refs[n_w:2 * n_w], refs[2 * n_w]
        outs, o_ref = refs[2 * n_w + 1:3 * n_w + 1], refs[3 * n_w + 1]
        stage = refs[3 * n_w + 2:4 * n_w + 2]
        kept, half_send, half_recv, slots, core_sums, vec_send, vec_recv, sum_send, sum_recv = refs[4 * n_w + 2:]
        x, y, c = _place()
        my_chip, sibling = 2 * x + y, (x, y, 1 - c)
        slots[my_chip] = v_ref[...]
        spread = []
        for p, (ox, oy) in enumerate(_other_chips(x, y)):
            here = slots.at[2 * ox + oy]
            spread.append((_remote(v_ref, slots.at[my_chip], vec_send.at[p], vec_recv.at[p], (ox, oy, c)),
                           _remote(here, here, vec_send.at[p], vec_recv.at[p], (ox, oy, c))))
        for send, _ in spread:
            send.start()
        copies = []
        for w, (h, _) in enumerate(halves):
            step = SUM_ROWS if h % SUM_ROWS == 0 else h

            def sum_rows(i, _, w=w, step=step):
                r = pl.ds(pl.multiple_of(i * step, step), step)
                stage[w][r, :] = functools.reduce(lambda total, p: total + lands[w][p, r, :].astype(F32), range(3),
                                                  sums[w][0, r, :].astype(F32))
                return 0

            lax.fori_loop(0, h // step, sum_rows, 0)
            mine, theirs = outs[w].at[pl.ds(c * h, h)], outs[w].at[pl.ds((1 - c) * h, h)]
            copies.append((pltpu.make_async_copy(stage[w], mine, kept.at[w]),
                           _remote(stage[w], mine, half_send.at[w], half_recv.at[w], sibling),
                           _remote(theirs, theirs, half_send.at[w], half_recv.at[w], sibling)))
            copies[-1][0].start()
            copies[-1][1].start()
        for send, arrival in spread:
            arrival.wait_recv()
            send.wait_send()
        core_sums[c] = functools.reduce(lambda total, chip: total + slots[chip], range(1, N_CHIPS), slots[0])
        mine, theirs = core_sums.at[c], core_sums.at[1 - c]
        to_sibling = _remote(mine, mine, sum_send.at[0], sum_recv.at[0], sibling)
        to_sibling.start()
        _remote(theirs, theirs, sum_send.at[0], sum_recv.at[0], sibling).wait_recv()
        to_sibling.wait_send()
        o_ref[...] = core_sums[0] + core_sums[1]
        for keep, send, arrival in copies:
            keep.wait()
            arrival.wait_recv()
            send.wait_send()

    once = dict(pipeline_mode=pl.Buffered(1))
    vm = pl.BlockSpec((rows, LANES), lambda i, pr: (0, 0))
    res = pl.pallas_call(
        body, name="finish_gradients",
        grid_spec=pltpu.PrefetchScalarGridSpec(
            num_scalar_prefetch=1, grid=(1,),
            in_specs=[pl.BlockSpec((1,) + hc, lambda i, pr: (pr[1], 0, 0), **once) for hc in halves]
            + [pl.BlockSpec((3,) + hc, lambda i, pr: (0, 0, 0), **once) for hc in halves] + [vm],
            out_specs=[ANY] * n_w + [vm],
            scratch_shapes=[pltpu.VMEM(hc, F32) for hc in halves]
            + [pltpu.SemaphoreType.DMA((n_w,))] * 3
            + [pltpu.VMEM((N_CHIPS, rows, LANES), F32), pltpu.VMEM((2, rows, LANES), F32),
               pltpu.SemaphoreType.DMA((N_CHIPS - 1,)), pltpu.SemaphoreType.DMA((N_CHIPS - 1,)),
               pltpu.SemaphoreType.DMA((1,)), pltpu.SemaphoreType.DMA((1,))]),
        out_shape=[_sds((2 * h, cols), F32) for h, cols in halves] + [_sds(vec.shape, F32)],
        compiler_params=_params("arbitrary"),
    )(place, *pair_sums, *landed, vec)
    return res[:n_w], res[n_w]


def _row_tile(rows):
    fits = [tr for tr in range(16, min(rows, 512) + 1, 16) if rows % tr == 0]
    return max(fits) if fits else rows


def _pair_sum(name, place, grad, theirs):
    if grad.ndim == 2:
        return _pair_sum_joined(name, place, grad, theirs)
    n, r, c = grad.shape
    half = r // 2
    tr = _row_tile(half)
    nb = half // tr

    def body(place_ref, g_ref, t_ref, o_ref):
        o_ref[...] = (g_ref[...] + t_ref[...]).astype(BF16)

    return pl.pallas_call(
        body, name=name, out_shape=jax.ShapeDtypeStruct((n, half, c), BF16),
        grid_spec=pltpu.PrefetchScalarGridSpec(
            num_scalar_prefetch=1, grid=(n, nb),
            in_specs=[pl.BlockSpec((1, tr, c), lambda j, i, pr: (j, pr[0] * nb + i, 0)),
                      pl.BlockSpec((1, tr, c), lambda j, i, pr: (j, i, 0))],
            out_specs=pl.BlockSpec((1, tr, c), lambda j, i, pr: (j, i, 0))),
        compiler_params=_params("parallel", "parallel"),
    )(place, grad, theirs)


def _pair_sum_joined(name, place, grad, theirs):
    r, wide = grad.shape
    half, c = r // 2, wide // N_CHIPS
    tr = _row_tile(half)
    nb = half // tr

    def body(place_ref, g_ref, t_ref, o_ref):
        for j in range(N_CHIPS):
            cols = slice(j * c, (j + 1) * c)
            o_ref[j] = (g_ref[:, cols] + t_ref[:, cols]).astype(BF16)

    return pl.pallas_call(
        body, name=name, out_shape=jax.ShapeDtypeStruct((N_CHIPS, half, c), BF16),
        grid_spec=pltpu.PrefetchScalarGridSpec(
            num_scalar_prefetch=1, grid=(nb,),
            in_specs=[pl.BlockSpec((tr, wide), lambda i, pr: (pr[0] * nb + i, 0)),
                      pl.BlockSpec((tr, wide), lambda i, pr: (i, 0))],
            out_specs=pl.BlockSpec((N_CHIPS, tr, c), lambda i, pr: (0, i, 0))),
        compiler_params=_params("parallel"),
    )(place, grad, theirs)


MIXER = ("w_branch_a", "w_branch_b", "w_out")
FFN_PLE = ("w_ffn_gate", "w_ffn_up", "w_ffn_down", "w_ple_gate", "w_ple_proj")
LATE = MIXER + FFN_PLE
BIG = ("w_in",) + LATE
HELD_TRANSPOSED = ("w_ffn_gate", "w_ffn_up")
SMALL = ("norm_mix", "w_pool", "pool_scale", "norm_ffn", "norm_ple", "norm_final")


def _join_columns(w4):
    return jnp.concatenate([w4[j] for j in range(N_CHIPS)], axis=1)


def _sds(shape, dtype):
    return jax.ShapeDtypeStruct(shape, dtype)


def _local_step(x, p, target, wf, small, ex=None):
    t, d = x.shape
    w_pool_b = small["w_pool"].astype(BF16)
    dp = w_pool_b.shape[0] * w_pool_b.shape[1]

    ex = ex or _NoExchanges()
    h1, first = _norm_fwd("norm_mix", x, small["norm_mix"], rider=ex.gather(("w_in",)))
    wf = {**wf, **dict(zip(("w_in",), first))}
    w_in = wf["w_in"]
    u, q, kv, ga, gb = _mm(
        "proj", [h1], [w_in[j] for j in range(N_CHIPS)], "nn",
        [_sds((t, dp), F32), _sds((t, dp), BF16), _sds((t, d), BF16), _sds((t, d), BF16), _sds((t, d), BF16)],
        separate=True, epilogue=lambda uq, kv_, ga_, gb_: (uq[:, :dp], uq[:, dp:], kv_, ga_, gb_), tm=512)
    pooled, ya = _pool_fwd(u, w_pool_b, small["pool_scale"])
    n_pairs = dp // LANES
    yb, late = _attn_fwd(q, 0, kv, 0, n_pairs, n_pairs, rider=ex.gather(LATE))
    wf = {**wf, **dict(zip(LATE, late))}
    w_down = wf["w_ffn_down"].reshape(-1, d)
    dff = w_down.shape[0]
    w_gate_t, w_up_t = wf["w_ffn_gate"].reshape(dff, d), wf["w_ffn_up"].reshape(dff, d)
    w_a, w_b, w_pp = _join_columns(wf["w_branch_a"]), _join_columns(wf["w_branch_b"]), _join_columns(wf["w_ple_proj"])
    w_out = wf["w_out"].reshape(d, d)
    w_pg = wf["w_ple_gate"].reshape(d, d)
    def residual_norm(branch, xv, g, w):
        xn = xv + jnp.dot(branch.astype(BF16), w, preferred_element_type=F32)
        return xn, xn * lax.rsqrt(jnp.mean(xn * xn, axis=-1, keepdims=True) + RMS_EPS) * g

    def mixer_tail(tav, tbv, gav, gbv, xv, g, w):
        merged = _sigmoid(gav) * tav + _sigmoid(gbv) * tbv
        return (tav, tbv, merged) + residual_norm(merged, xv, g, w)

    def ffn_tail(gv, uv, xv, g, w):
        act = gv * _sigmoid(gv) * uv
        return (gv, uv, act) + residual_norm(act, xv, g, w)

    stream = [_sds((t, d), F32), _sds((t, d), BF16)]
    ta, tb, merged, x1, h2 = _mm(
        "mixer_out", [ya, yb], [w_a, w_b], "nn", [_sds((t, d), BF16)] * 3 + stream,
        extras=[ga, gb, x, small["norm_ffn"]], wholes=[w_out], separate=True, epilogue=mixer_tail, tm=512, passes=2)
    gate, up, act, x2, h3 = _mm(
        "ffn", [h2], [w_gate_t, w_up_t], "nt", [_sds((t, dff), BF16)] * 3 + stream,
        extras=[x1, small["norm_ple"]], wholes=[w_down], separate=True, epilogue=ffn_tail, tm=256)
    dx2, dx2_b, d_pp, d_gp, d_norm_final, loss_row, d_norm_ple = _mm(
        "ple_loss", [h3, p], [w_pg, w_pp], "nn", stream + [_sds((t, d), BF16)] * 2,
        extras=[x2, target, small["norm_final"].reshape(1, d), small["norm_ple"]], wholes=[w_pg], separate=True,
        epilogue=_ple_and_loss, sum_shapes=[_sds((1, d), F32)] * 3, tm=512, passes=2)

    def through_norm(dh, xv, g, dres):
        dx, d_gain = _rms_norm_bwd(dh, xv, g)
        return dx + dres, dx + dres, d_gain

    gain_sum = [_sds((1, d), F32)]
    g_w_pp, g_w_pg = _mm_tn("g_ple", [p, h3], [d_pp, d_gp])

    def ffn_bwd(d_act, gv, uv, xv, g, dres, wg_t, wu_t):
        s = _sigmoid(gv)
        d_gate, d_up = d_act * uv * (s * (1.0 + gv * (1.0 - s))), d_act * (gv * s)
        dh2 = (jnp.dot(d_gate.astype(BF16), wg_t, preferred_element_type=F32)
               + jnp.dot(d_up.astype(BF16), wu_t, preferred_element_type=F32))
        return (d_gate, d_up) + through_norm(dh2, xv, g, dres)

    d_gate, d_up, dx1, dx1_b, d_norm_ffn = _mm(
        "ffn_bwd", [dx2_b], [w_down], "nt", [_sds((t, dff), BF16)] * 2 + stream,
        extras=[gate, up, x1, small["norm_ffn"], dx2], wholes=[w_gate_t, w_up_t], epilogue=ffn_bwd,
        sum_shapes=gain_sum, tm=256)
    g_w_down, = _mm_tn("g_ffn_down", [act], [dx2_b], tmm=512)
    g_w_gate_t, g_w_up_t = _mm_tn("g_ffn_gate_up", [d_gate, d_up], [h2], k_blocks=2)

    def merge_bwd(acc, tav, tbv, gav, gbv):
        sa, sb = _sigmoid(gav), _sigmoid(gbv)
        return acc * sa, acc * sb, acc * tav * sa * (1.0 - sa), acc * tbv * sb * (1.0 - sb)

    big = {
        "w_ffn_gate": g_w_gate_t.reshape(wf["w_ffn_gate"].shape), "w_ffn_up": g_w_up_t.reshape(wf["w_ffn_up"].shape),
        "w_ffn_down": g_w_down.reshape(wf["w_ffn_down"].shape),
        "w_ple_gate": g_w_pg.reshape(wf["w_ple_gate"].shape), "w_ple_proj": g_w_pp,
    }
    (d_ta, d_tb, d_ga, d_gb), theirs = _mm(
        "d_merged", [dx1_b], [w_out], "nt", [_sds((t, d), BF16)] * 4, extras=[ta, tb, ga, gb], epilogue=merge_bwd,
        tm=512, passes=2, rider=ex.pair(FFN_PLE, big))
    ex.paired(FFN_PLE, big, theirs)
    g_w_out, big["w_branch_a"], big["w_branch_b"] = _mm_tn("g_mixer", [merged, ya, yb], [dx1_b, d_ta, d_tb])
    big["w_out"] = g_w_out.reshape(wf["w_out"].shape)
    (d_ya, d_yb), theirs = _mm(
        "d_branches", [d_ta, d_tb], [w_a, w_b], "nt", [_sds((t, dp), F32), _sds((t, dp), BF16)], separate=True,
        rider=ex.pair(MIXER, big))
    ex.paired(MIXER, big, theirs)
    d_u, g_w_pool, d_pool_scale = _pool_bwd(d_ya, pooled, w_pool_b, small["pool_scale"])
    (d_q, d_k, d_v), landed = _attn_bwd(q, 0, kv, 0, n_pairs, d_yb, n_pairs, rider=ex.chip(LATE))
    ex.landed.update(zip(LATE, landed))
    d_proj = [(d_u, d_q), (d_k, d_v), d_ga, d_gb]
    big["w_in"], = _mm_tn("g_w_in", [h1], d_proj, tmm=512, stacked=True)
    ex.paired(("w_in",), big, _run_exchange("pair_exchange_w_in", ex.pair(("w_in",), big)))
    (grad_x, d_norm_mix), landed = _mm(
        "d_h1", d_proj, [w_in[j] for j in range(N_CHIPS)], "nt", [_sds((t, d), F32)],
        extras=[x, small["norm_mix"], dx1], epilogue=lambda dh, xv, g, dres: through_norm(dh, xv, g, dres)[1:],
        sum_shapes=gain_sum, tm=512, passes=2, rider=ex.chip(("w_in",)))
    ex.landed.update(zip(("w_in",), landed))
    small_g = {"norm_mix": d_norm_mix, "w_pool": g_w_pool, "pool_scale": d_pool_scale, "norm_ffn": d_norm_ffn,
               "norm_ple": d_norm_ple, "norm_final": d_norm_final}
    return grad_x, big, small_g, loss_row


def _pack_small(small_g, loss_row):
    parts, layout = [], []
    for name in SMALL + ("loss",):
        v = (loss_row if name == "loss" else small_g[name]).reshape(-1, LANES)
        pad = (-v.shape[0]) % 8
        if pad:
            v = jnp.concatenate([v, jnp.zeros((pad, LANES), F32)], axis=0)
        layout.append((name, sum(q.shape[0] for q in parts), v.shape[0]))
        parts.append(v)
    return jnp.concatenate(parts, axis=0), layout


def kernel(x, p, norm_mix, w_in, w_pool, pool_scale, w_branch_a, w_branch_b, w_out, norm_ffn, w_ffn_gate, w_ffn_up, w_ffn_down, norm_ple, w_ple_gate, w_ple_proj, norm_final, loss_target, m_norm_mix, m_w_in, m_w_pool, m_pool_scale, m_w_branch_a, m_w_branch_b, m_w_out, m_norm_ffn, m_w_ffn_gate, m_w_ffn_up, m_w_ffn_down, m_norm_ple, m_w_ple_gate, m_w_ple_proj, m_norm_final, v_norm_mix, v_w_in, v_w_pool, v_pool_scale, v_w_branch_a, v_w_branch_b, v_w_out, v_norm_ffn, v_w_ffn_gate, v_w_ffn_up, v_w_ffn_down, v_norm_ple, v_w_ple_gate, v_w_ple_proj, v_norm_final):
    given = dict(locals())
    order = ("norm_mix", "w_in", "w_pool", "pool_scale", "w_branch_a", "w_branch_b", "w_out", "norm_ffn", "w_ffn_gate",
             "w_ffn_up", "w_ffn_down", "norm_ple", "w_ple_gate", "w_ple_proj", "norm_final")
    t, d = x.shape[1], x.shape[2]
    def local(a, n):
        return jnp.swapaxes(a[0], 0, 1) if n in HELD_TRANSPOSED else a[0]

    def back(a, n):
        return (jnp.swapaxes(a, 0, 1) if n in HELD_TRANSPOSED else a)[None]

    shard = {n: local(given[n], n) for n in BIG}
    small = {"norm_mix": norm_mix, "w_pool": w_pool[0], "pool_scale": pool_scale, "norm_ffn": norm_ffn,
             "norm_ple": norm_ple, "norm_final": norm_final}

    place = jnp.stack([lax.axis_index("c"), 2 * lax.axis_index("x") + lax.axis_index("y")]).astype(jnp.int32)
    ex = _StepExchanges({n: shard[n].astype(BF16) for n in BIG}, place)
    grad_x, _, small_g, loss_row = _local_step(
        x.reshape(t, d), p.reshape(t, p.shape[-1]), loss_target.reshape(t, d), {}, small, ex)
    packed, layout = _pack_small(small_g, loss_row)
    filled, reduced = _finish_gradients(place, [ex.pair_sums[n] for n in BIG], [ex.landed[n] for n in BIG], packed)
    grads = dict(zip(BIG, filled))
    for name, start, rows in layout:
        if name == "loss":
            loss = jnp.sum(reduced[start:start + rows])
        else:
            n_el = small[name].size
            grads[name] = reduced[start:start + rows].reshape(-1)[:n_el]

    deltas, new_m, new_v = {}, {}, {}
    for n in order:
        if n in BIG:
            w, m, v = shard[n], local(given["m_" + n], n), local(given["v_" + n], n)
            dl, mn, vn = _adamw(f"adamw_{n}", w, grads[n], m, v)
            grads[n], deltas[n], new_m[n], new_v[n] = [back(a, n) for a in (grads[n], dl, mn, vn)]
        else:
            w, full = small[n], given[n].shape
            shape2 = (1, w.shape[0]) if w.ndim == 1 else (w.shape if w.ndim == 2 else (w.shape[0] * w.shape[1], w.shape[2]))
            dl, mn, vn = _adamw(f"adamw_{n}", w.reshape(shape2), grads[n].reshape(shape2),
                                given["m_" + n].reshape(shape2), given["v_" + n].reshape(shape2))
            grads[n], deltas[n], new_m[n], new_v[n] = [a.reshape(full) for a in (grads[n], dl, mn, vn)]

    return (loss, grad_x.reshape(x.shape), *[grads[n] for n in order], *[deltas[n] for n in order],
            *[new_m[n] for n in order], *[new_v[n] for n in order])
```

```python
import functools
import math

import jax
import jax.numpy as jnp
from jax import lax
from jax.experimental import pallas as pl
from jax.experimental.pallas import tpu as pltpu

F32 = jnp.float32
BF16 = jnp.bfloat16
MESH = pl.DeviceIdType.MESH

RMS_EPS = 1e-6
POOL_WINDOWS = (2, 4, 8, 16)
POOL_HALO = 16
HEAD_DIM = 64
LANES = 128
ATT_BLOCK = 256
ATT_CHAINS = 2
ATT_FWD_CHAINS = 4
ATT_CHUNK = 256
ATT_SLAB = 256
ATT_SCALE = 1.0 / math.sqrt(HEAD_DIM)
LOG2_E = 1.4426950408889634
ATT_EXIT_BELOW = -150.5
ADAM_LR, ADAM_B1, ADAM_B2, ADAM_EPS, ADAM_WD, ADAM_STEP = 0.001, 0.9, 0.999, 1e-08, 0.01, 10
V7X_VMEM_LIMIT_BYTES = 56 * 1024 * 1024
N_CHIPS = 4
N_DEV = 8


def _params(*semantics):
    return pltpu.CompilerParams(dimension_semantics=semantics, vmem_limit_bytes=V7X_VMEM_LIMIT_BYTES)


def _sigmoid(z):
    return 1.0 / (1.0 + jnp.exp(-z))


def _tiled_spec(shape, tm, tn, n_total, at):
    rows, width = shape
    if rows == 1:
        if width == n_total:
            return pl.BlockSpec((1, tn), at(lambda i, j: (0, j)))
        return pl.BlockSpec((1, width), at(lambda i, j: (0, 0)))
    if width == n_total:
        return pl.BlockSpec((tm, tn), at(lambda i, j: (i, j)))
    assert tn == n_total, "an operand narrower than the output needs whole output rows per tile"
    return pl.BlockSpec((tm, width), at(lambda i, j: (i, 0)))


def _column_pieces(operands):
    pieces = [tuple(a) if isinstance(a, (tuple, list)) else (a,) for a in operands]
    return [p for ps in pieces for p in ps], [len(ps) for ps in pieces]


def _load_bf16(refs, counts):
    tiles, k = [], 0
    for n in counts:
        parts = [r[...] for r in refs[k:k + n]]
        parts = [t if t.dtype == BF16 else t.astype(BF16) for t in parts]
        tiles.append(parts[0] if n == 1 else jnp.concatenate(parts, axis=1))
        k += n
    return tiles


def _mm(name, a_list, b_list, mode, out_shapes, epilogue=None, extras=(), tm=1024, tn=None, separate=False,
        sum_shapes=(), rider=None, wholes=()):
    flat_a, counts = _column_pieces(a_list)
    m_total = flat_a[0].shape[0]
    n_total = b_list[0].shape[1] if mode == "nn" else b_list[0].shape[0]
    tn = n_total if tn is None else tn
    tm = min(tm, m_total)
    assert m_total % tm == 0 and n_total % tn == 0 and (not sum_shapes or tn == n_total)
    n_a, n_b, n_extra, n_out = len(counts), len(b_list), len(extras), len(out_shapes)
    assert n_a in (1, n_b)
    dims = (((1,), (0,)), ((), ())) if mode == "nn" else (((1,), (1,)), ((), ()))
    with_rider = rider is not None
    rider = rider or _NoRider()
    grid = (n_total // tn, m_total // tm)

    def at(index):
        return lambda j, i: index(i, j)

    def body(*refs):
        ins, o_refs, _, riding = rider.split(refs, len(flat_a) + n_b + n_extra + len(wholes), n_out + len(sum_shapes))
        a_refs, b_refs = ins[:len(flat_a)], ins[len(flat_a):len(flat_a) + n_b]
        e_refs, w_refs = ins[len(flat_a) + n_b:len(flat_a) + n_b + n_extra], ins[len(flat_a) + n_b + n_extra:]
        at_first = (pl.program_id(0) == 0) & (pl.program_id(1) == 0)
        at_last = (pl.program_id(0) == grid[0] - 1) & (pl.program_id(1) == grid[1] - 1)
        top, bottom = rider.at_steps(riding, at_first, at_first, at_last)
        top()
        lefts = _load_bf16(a_refs, counts)
        products = [lax.dot_general(lefts[s % n_a], b_refs[s][...], dims, preferred_element_type=F32)
                    for s in range(n_b)]
        if not separate:
            products = [functools.reduce(lambda p, r: p + r, products)]
        extra_tiles = [e[...].astype(F32) for e in e_refs]
        outs = products if epilogue is None else epilogue(*products, *extra_tiles, *[w[...] for w in w_refs])
        for o_ref, o in zip(o_refs[:n_out], outs[:n_out]):
            o_ref[...] = o.astype(o_ref.dtype)
        if sum_shapes:
            @pl.when(pl.program_id(1) == 0)
            def _():
                for s_ref in o_refs[n_out:]:
                    s_ref[...] = jnp.zeros_like(s_ref)

            for s_ref, s in zip(o_refs[n_out:], outs[n_out:]):
                s_ref[...] += s
        bottom()

    once = dict(pipeline_mode=pl.Buffered(1)) if tn == n_total else {}
    in_specs = [pl.BlockSpec((tm, a.shape[1]), at(lambda i, j: (i, 0))) for a in flat_a]
    if mode == "nn":
        in_specs += [pl.BlockSpec((b.shape[0], tn), at(lambda i, j: (0, j)), **once) for b in b_list]
    else:
        in_specs += [pl.BlockSpec((tn, b.shape[1]), at(lambda i, j: (j, 0)), **once) for b in b_list]
    in_specs += [_tiled_spec(e.shape, tm, tn, n_total, at) for e in extras]
    in_specs += [pl.BlockSpec(w.shape, lambda j, i: (0, 0), pipeline_mode=pl.Buffered(1)) for w in wholes]
    out_specs = [_tiled_spec(o.shape, tm, tn, n_total, at) for o in out_shapes]
    out_specs += [pl.BlockSpec(s.shape, at(lambda i, j: (0, 0))) for s in sum_shapes]
    semantics = ("arbitrary", "arbitrary") if sum_shapes or rider.operands else ("parallel", "parallel")
    res = pl.pallas_call(
        body, name=name, grid=grid, in_specs=in_specs + [ANY] * len(rider.operands),
        out_specs=out_specs + [ANY] * len(rider.out_shapes),
        out_shape=list(out_shapes) + list(sum_shapes) + list(rider.out_shapes), scratch_shapes=list(rider.scratch),
        compiler_params=_params(*semantics),
    )(*flat_a, *b_list, *extras, *wholes, *rider.operands)
    n_own = len(out_shapes) + len(sum_shapes)
    return (res[:n_own], res[n_own:]) if with_rider else res


def _mm_tn(name, a_list, b_list, tmm=1024, stacked=False, k_blocks=1):
    flat_b, counts = _column_pieces(b_list)
    n_a, n_b = len(a_list), len(counts)
    n_prod = max(n_a, n_b)
    m_total = a_list[0].shape[0]
    ks = [a_list[s % n_a].shape[1] for s in range(n_prod)]
    widths = [sum(p.shape[1] for p in flat_b[sum(counts[:s]):sum(counts[:s + 1])]) for s in range(n_b)]
    widths = [widths[s % n_b] for s in range(n_prod)]
    tmm = min(tmm, m_total)
    assert m_total % tmm == 0 and all(k % k_blocks == 0 for k in ks)
    assert n_a in (1, n_prod) and n_b in (1, n_prod) and not (stacked and n_a > 1)

    def body(*refs):
        a_refs, b_refs, o_refs = refs[:n_a], refs[n_a:n_a + len(flat_b)], refs[n_a + len(flat_b):]

        @pl.when(pl.program_id(1) == 0)
        def _():
            for o_ref in o_refs:
                o_ref[...] = jnp.zeros_like(o_ref)

        lefts, rights = _load_bf16(a_refs, [1] * n_a), _load_bf16(b_refs, counts)
        for s in range(n_prod):
            product = lax.dot_general(lefts[s % n_a], rights[s % n_b], (((0,), (0,)), ((), ())),
                                      preferred_element_type=F32)
            if stacked:
                o_refs[0][s] += product
            else:
                o_refs[s][...] += product

    in_specs = [pl.BlockSpec((tmm, a.shape[1] // k_blocks), lambda kb, m: (m, kb)) for a in a_list]
    in_specs += [pl.BlockSpec((tmm, b.shape[1]), lambda kb, m: (m, 0)) for b in flat_b]
    if stacked:
        out_shape = [jax.ShapeDtypeStruct((n_prod, ks[0], widths[0]), F32)]
        out_specs = [pl.BlockSpec((n_prod, ks[0] // k_blocks, widths[0]), lambda kb, m: (0, kb, 0))]
    else:
        out_shape = [jax.ShapeDtypeStruct((k, w), F32) for k, w in zip(ks, widths)]
        out_specs = [pl.BlockSpec((k // k_blocks, w), lambda kb, m: (kb, 0)) for k, w in zip(ks, widths)]
    return pl.pallas_call(
        body, name=name, grid=(k_blocks, m_total // tmm), in_specs=in_specs, out_specs=out_specs, out_shape=out_shape,
        compiler_params=_params("arbitrary", "arbitrary"),
    )(*a_list, *flat_b)


def _rows(name, fn, ins, tile_outs, sum_outs=(), tr=512, rider=None):
    t_total = max(a.shape[0] for a in ins)
    tr = min(tr, t_total)
    assert t_total % tr == 0
    n_in, n_tile = len(ins), len(tile_outs)
    rider = rider or _NoRider()
    n_steps = t_total // tr

    def body(*refs):
        own_ins, own_outs, _, riding = rider.split(refs, n_in, n_tile + len(sum_outs))
        step = pl.program_id(0)
        top, bottom = rider.at_steps(riding, step == 0, step == n_steps - 1, step == n_steps - 1)
        top()
        refs = tuple(own_ins) + tuple(own_outs)
        outs = fn(*[r[...].astype(F32) for r in refs[:n_in]])
        for o_ref, o in zip(refs[n_in:n_in + n_tile], outs[:n_tile]):
            o_ref[...] = o.astype(o_ref.dtype)
        if sum_outs:
            @pl.when(pl.program_id(0) == 0)
            def _():
                for s_ref in refs[n_in + n_tile:]:
                    s_ref[...] = jnp.zeros_like(s_ref)

            for s_ref, s in zip(refs[n_in + n_tile:], outs[n_tile:]):
                s_ref[...] += s
        bottom()

    def spec(shape):
        if shape[0] == 1:
            return pl.BlockSpec(shape, lambda i: (0, 0))
        return pl.BlockSpec((tr, shape[1]), lambda i: (i, 0))

    return pl.pallas_call(
        body, name=name, grid=(n_steps,), in_specs=[spec(a.shape) for a in ins] + [ANY] * len(rider.operands),
        out_specs=[spec(o.shape) for o in tile_outs] + [spec(s.shape) for s in sum_outs] + [ANY] * len(rider.out_shapes),
        out_shape=list(tile_outs) + list(sum_outs) + list(rider.out_shapes), scratch_shapes=list(rider.scratch),
        compiler_params=_params("arbitrary" if sum_outs or rider.operands else "parallel"),
    )(*ins, *rider.operands)


def _norm_fwd(name, x, gain, rider=None):
    def fn(xv, g):
        inv = lax.rsqrt(jnp.mean(xv * xv, axis=-1, keepdims=True) + RMS_EPS)
        return (xv * inv * g,)

    res = _rows(name, fn, [x, gain], [jax.ShapeDtypeStruct(x.shape, BF16)], rider=rider)
    return res[0], res[1:]


def _rms_norm_bwd(dh, xv, g):
    inv = lax.rsqrt(jnp.mean(xv * xv, axis=-1, keepdims=True) + RMS_EPS)
    xn = xv * inv
    dxn = dh * g
    return inv * (dxn - xn * jnp.mean(dxn * xn, axis=-1, keepdims=True)), jnp.sum(dh * xn, axis=0, keepdims=True)


def _ple_and_loss(gv, pv, x2v, tv, g_final, g_ple, w_pg):
    d = x2v.shape[1]
    s = _sigmoid(gv)
    xv = x2v + s * pv
    inv = lax.rsqrt(jnp.mean(xv * xv, axis=-1, keepdims=True) + RMS_EPS)
    err = xv * inv * g_final - tv
    dx3, d_final = _rms_norm_bwd(err * (1.0 / d), xv, g_final)
    d_pp, d_gp = dx3 * s, dx3 * pv * s * (1.0 - s)
    dh3 = lax.dot_general(d_gp.astype(BF16), w_pg, (((1,), (1,)), ((), ())), preferred_element_type=F32)
    dx2, d_ple = _rms_norm_bwd(dh3, x2v, g_ple)
    dx2 = dx2 + dx3
    return dx2, dx2, d_pp, d_gp, d_final, (0.5 / d) * jnp.sum(err * err, axis=0, keepdims=True), d_ple


def _window_counts(t_pos, w):
    return jnp.minimum(t_pos + 1, w).astype(F32)


def _pool_fwd(u, w_pool, scale, tr=512):
    t_total, width = u.shape
    tr = min(tr, t_total)
    n_groups = len(POOL_WINDOWS)
    gdim = width // n_groups
    ext = tr + POOL_HALO

    def body(u_ref, halo_ref, w_ref, s_ref, pooled_ref, ya_ref):
        i = pl.program_id(0)
        halo = jnp.where(i == 0, 0.0, halo_ref[...])
        t_pos = i * tr + lax.broadcasted_iota(jnp.int32, (tr, 1), 0)
        for g, w in enumerate(POOL_WINDOWS):
            cols = slice(g * gdim, (g + 1) * gdim)
            main = u_ref[:, cols]
            win = jnp.concatenate([halo[:, cols], main], axis=0)
            span = 1
            while span < w:
                win = win + pltpu.roll(win, span, 0)
                span *= 2
            pooled = win[POOL_HALO:, :] * (1.0 / _window_counts(t_pos, w)) - main
            pooled_b = pooled.astype(BF16)
            pooled_ref[:, cols] = pooled_b
            mixed = jnp.dot(pooled_b, w_ref[g], preferred_element_type=F32)
            ya_ref[:, cols] = (mixed * s_ref[:, cols]).astype(BF16)

    hb = tr // POOL_HALO
    return pl.pallas_call(
        body, name="pool_fwd", grid=(t_total // tr,),
        in_specs=[pl.BlockSpec((tr, width), lambda i: (i, 0)),
                  pl.BlockSpec((POOL_HALO, width), lambda i: (jnp.maximum(i * hb - 1, 0), 0)),
                  pl.BlockSpec((n_groups, gdim, gdim), lambda i: (0, 0, 0)),
                  pl.BlockSpec((1, width), lambda i: (0, 0))],
        out_specs=[pl.BlockSpec((tr, width), lambda i: (i, 0)), pl.BlockSpec((tr, width), lambda i: (i, 0))],
        out_shape=[jax.ShapeDtypeStruct(u.shape, BF16), jax.ShapeDtypeStruct(u.shape, BF16)],
        compiler_params=_params("parallel"),
    )(u, u, w_pool, scale)


def _pool_bwd(dya, pooled, w_pool, scale, tr=512):
    t_total, width = dya.shape
    tr = min(tr, t_total)
    n_groups = len(POOL_WINDOWS)
    gdim = width // n_groups
    ext = tr + POOL_HALO
    n_tiles = t_total // tr

    def body(d_ref, halo_ref, p_ref, w_ref, s_ref, du_ref, dw_ref, ds_ref):
        i = pl.program_id(0)

        @pl.when(i == 0)
        def _():
            dw_ref[...] = jnp.zeros_like(dw_ref)
            ds_ref[...] = jnp.zeros_like(ds_ref)

        halo = jnp.where(i == n_tiles - 1, 0.0, halo_ref[...])
        t_pos = i * tr + lax.broadcasted_iota(jnp.int32, (ext, 1), 0)
        for g, w in enumerate(POOL_WINDOWS):
            cols = slice(g * gdim, (g + 1) * gdim)
            sc = s_ref[:, cols]
            d_main = d_ref[:, cols]
            pooled_b = p_ref[:, cols]
            mixed = jnp.dot(pooled_b, w_ref[g], preferred_element_type=F32)
            ds_ref[:, cols] += jnp.sum(d_main * mixed, axis=0, keepdims=True)
            dmix = (jnp.concatenate([d_main, halo[:, cols]], axis=0) * sc).astype(BF16)
            dw_ref[g] += lax.dot_general(pooled_b, dmix[:tr, :], (((0,), (0,)), ((), ())),
                                         preferred_element_type=F32)
            dpool = lax.dot_general(dmix, w_ref[g], (((1,), (1,)), ((), ())), preferred_element_type=F32)
            win = dpool * (1.0 / _window_counts(t_pos, w))
            span = 1
            while span < w:
                win = win + pltpu.roll(win, ext - span, 0)
                span *= 2
            du_ref[:, cols] = (win[:tr, :] - dpool[:tr, :]).astype(BF16)

    hb = tr // POOL_HALO
    last_halo = t_total // POOL_HALO - 1
    return pl.pallas_call(
        body, name="pool_bwd", grid=(n_tiles,),
        in_specs=[pl.BlockSpec((tr, width), lambda i: (i, 0)),
                  pl.BlockSpec((POOL_HALO, width), lambda i: (jnp.minimum((i + 1) * hb, last_halo), 0)),
                  pl.BlockSpec((tr, width), lambda i: (i, 0)),
                  pl.BlockSpec((n_groups, gdim, gdim), lambda i: (0, 0, 0)),
                  pl.BlockSpec((1, width), lambda i: (0, 0))],
        out_specs=[pl.BlockSpec((tr, width), lambda i: (i, 0)),
                   pl.BlockSpec((n_groups, gdim, gdim), lambda i: (0, 0, 0)),
                   pl.BlockSpec((1, width), lambda i: (0, 0))],
        out_shape=[jax.ShapeDtypeStruct(dya.shape, BF16), jax.ShapeDtypeStruct((n_groups, gdim, gdim), F32),
                   jax.ShapeDtypeStruct((1, width), F32)],
        compiler_params=_params("arbitrary"),
    )(dya, dya, pooled, w_pool, scale)


def _head_masks():
    lane = lax.broadcasted_iota(jnp.int32, (1, LANES), 1)
    return lane < HEAD_DIM


def _stack_heads(tile, first):
    zero = jnp.zeros_like(tile)
    return jnp.concatenate([jnp.where(first, tile, zero), jnp.where(first, zero, tile)], axis=0)


def _causal_mask(t_pos, k_start):
    col = lax.broadcasted_iota(jnp.int32, (1, 2 * ATT_SLAB), 1)
    return k_start + (col & (ATT_SLAB - 1)) < t_pos


def _slab_scores(q, kd, mask):
    z2 = lax.dot_general(q, kd, (((1,), (1,)), ((), ())), preferred_element_type=F32) * LOG2_E
    log_hit = jnp.minimum(z2, 0.0) - jnp.log2(1.0 + jnp.exp2(-jnp.abs(z2)))
    log_fail = log_hit - z2
    return log_hit, (log_fail if mask is None else jnp.where(mask, log_fail, 0.0))


def _weights(log_hit, suffix, mask):
    arg = log_hit + suffix
    return jnp.exp2(arg if mask is None else jnp.where(mask, arg, -1e30))


def _tri(upper):
    r = lax.broadcasted_iota(jnp.int32, (ATT_CHUNK, ATT_CHUNK), 0)
    c = lax.broadcasted_iota(jnp.int32, (ATT_CHUNK, ATT_CHUNK), 1)
    return jnp.where(r > c if upper else r < c, 1.0, 0.0).astype(BF16)


def _tri_spec():
    return pl.BlockSpec((ATT_CHUNK, ATT_CHUNK), lambda h, i: (0, 0), pipeline_mode=pl.Buffered(1))


def _scan_chunk(v, tri):
    return jnp.dot(v.astype(BF16), tri, preferred_element_type=F32)


def _lane_bcast(col):
    return jnp.broadcast_to(col, (col.shape[0], LANES))


def _scan_slab(v, tri, carries, from_right):
    n_chunks = ATT_SLAB // ATT_CHUNK
    edge = 0 if from_right else ATT_CHUNK - 1
    parts, new_carries = [None] * (2 * n_chunks), []
    for head in range(2):
        run = carries[head]
        for c in (reversed(range(n_chunks)) if from_right else range(n_chunks)):
            lo_col = head * ATT_SLAB + c * ATT_CHUNK
            vc = v[:, lo_col:lo_col + ATT_CHUNK]
            sc = _scan_chunk(vc, tri)
            parts[head * n_chunks + c] = sc + jnp.concatenate([run] * (ATT_CHUNK // LANES), axis=1)
            run = run + _lane_bcast(sc[:, edge:edge + 1] + vc[:, edge:edge + 1])
        new_carries.append(run)
    return jnp.concatenate(parts, axis=1), new_carries


def _fold_heads(stacked, first):
    s = stacked.shape[0] // 2
    return jnp.where(first, stacked[:s], stacked[s:])


class _NoRider:
    operands, out_shapes, scratch = (), (), ()

    def split(self, refs, n_base_in, n_base_out):
        n_in, n_out, n_sem = len(self.operands), len(self.out_shapes), len(self.scratch)
        a = n_base_in + n_in
        b = a + n_base_out + n_out
        mine = (refs[n_base_in:a], refs[a + n_base_out:b], refs[b:b + n_sem])
        return refs[:n_base_in], refs[a:a + n_base_out], refs[b + n_sem:], mine

    def start(self, ins, outs, sems):
        pass

    def relay(self, ins, outs, sems):
        pass

    def finish(self, ins, outs, sems):
        pass

    def at_steps(self, refs, first_step, relay_step, last_step):
        if not self.operands:
            return (lambda: None), (lambda: None)

        def top():
            pl.when(first_step)(lambda: self.start(*refs))
            pl.when(relay_step)(lambda: self.relay(*refs))

        return top, lambda: pl.when(last_step)(lambda: self.finish(*refs))


def _attn_fwd(q_src, q_col, kv_src, k_col, v_col, n_pairs=4, rider=_NoRider()):
    t_total = q_src.shape[0]
    blk = ATT_BLOCK
    n_chains = ATT_FWD_CHAINS if t_total % (ATT_FWD_CHAINS * blk) == 0 else ATT_CHAINS
    n_steps = t_total // (n_chains * blk)
    assert t_total % ATT_SLAB == 0 and ATT_SLAB == ATT_BLOCK

    def body(*refs):
        (q_ref, k_ref, v_ref, suffix_ref), (o_ref,), _, riding = rider.split(refs, 4, 1)
        h, ii = pl.program_id(0), pl.program_id(1)
        top, bottom = rider.at_steps(riding, (h == 0) & (ii == 0), (h == n_pairs - 1) & (ii == 0),
                                     (h == n_pairs - 1) & (ii == n_steps - 1))
        top()
        first = _head_masks()
        suffix_tri = suffix_ref[...]
        blocks = [n_chains * ii + c for c in range(n_chains)]
        qs = [q_ref[c * blk:(c + 1) * blk, :] * ATT_SCALE for c in range(n_chains)]
        t_pos = [b * blk + lax.broadcasted_iota(jnp.int32, (blk, 1), 0) for b in blocks]

        def one(c, t, chain, on_diagonal):
            _, acc, right_a, right_b = chain
            k_start = pl.multiple_of((blocks[c] - t) * ATT_SLAB, ATT_SLAB)
            kd = _stack_heads(k_ref[pl.ds(k_start, ATT_SLAB), :], first)
            vd = _stack_heads(v_ref[pl.ds(k_start, ATT_SLAB), :], first)
            mask = _causal_mask(t_pos[c], k_start) if on_diagonal else None
            log_hit, log_fail = _slab_scores(qs[c], kd, mask)
            suffix, (right_a, right_b) = _scan_slab(log_fail, suffix_tri, (right_a, right_b), from_right=True)
            a = _weights(log_hit, suffix, mask).astype(BF16)
            acc = acc + jnp.dot(a, vd, preferred_element_type=F32)
            return jnp.max(jnp.maximum(right_a, right_b)), acc, right_a, right_b

        def step(state, on_diagonal):
            t, chains = state
            return t + 1, tuple(one(c, t, chains[c], on_diagonal) for c in range(n_chains))

        def more(state):
            t, chains = state
            return (t <= blocks[0]) & (functools.reduce(jnp.maximum, [ch[0] for ch in chains]) > ATT_EXIT_BELOW)

        zero = jnp.zeros((blk, LANES), F32)
        state = step((0, ((jnp.float32(0.0), zero, zero, zero),) * n_chains), on_diagonal=True)
        t, chains = lax.while_loop(more, functools.partial(step, on_diagonal=False), state)
        for c in range(n_chains):
            chain = chains[c]
            if c:
                _, chain = lax.while_loop(
                    lambda s, c=c: (s[0] <= blocks[c]) & (s[1][0] > ATT_EXIT_BELOW),
                    lambda s, c=c: (s[0] + 1, one(c, s[0], s[1], False)), (t, chain))
            o_ref[c * blk:(c + 1) * blk, :] = chain[1].astype(BF16)
        bottom()

    rows = n_chains * blk
    res = pl.pallas_call(
        body, name="attn_fwd", grid=(n_pairs, n_steps),
        in_specs=[pl.BlockSpec((rows, LANES), lambda h, i: (i, q_col + h)),
                  pl.BlockSpec((t_total, LANES), lambda h, i: (0, k_col + h)),
                  pl.BlockSpec((t_total, LANES), lambda h, i: (0, v_col + h)), _tri_spec()] + [ANY] * len(rider.operands),
        out_specs=[pl.BlockSpec((rows, LANES), lambda h, i: (i, h))] + [ANY] * len(rider.out_shapes),
        out_shape=[jax.ShapeDtypeStruct((t_total, n_pairs * LANES), BF16)] + list(rider.out_shapes),
        scratch_shapes=list(rider.scratch),
        compiler_params=_params("arbitrary", "arbitrary"),
    )(q_src, kv_src, kv_src, _tri(upper=True), *rider.operands)
    return res[0], res[1:]


def _attn_bwd(q_src, q_col, kv_src, k_col, v_col, dy, n_pairs=4, rider=_NoRider()):
    t_total = q_src.shape[0]
    blk = ATT_BLOCK
    n_steps = t_total // (ATT_CHAINS * blk)
    n_slabs = t_total // ATT_SLAB
    assert t_total % ATT_SLAB == 0 and ATT_SLAB == ATT_BLOCK

    def body(*refs):
        ins, (dq_ref, dk_ref, dv_ref), (g_s, dk_acc, dv_acc), riding = rider.split(refs, 6, 3)
        q_ref, dy_ref, k_ref, v_ref, suffix_ref, prefix_ref = ins
        h, ii = pl.program_id(0), pl.program_id(1)
        top, bottom = rider.at_steps(riding, (h == 0) & (ii == 0), (h == n_pairs - 1) & (ii == 0),
                                     (h == n_pairs - 1) & (ii == n_steps - 1))
        top()

        @pl.when(ii == 0)
        def _():
            dk_acc[...] = jnp.zeros_like(dk_acc)
            dv_acc[...] = jnp.zeros_like(dv_acc)

        first = _head_masks()
        suffix_tri = suffix_ref[...]
        prefix_tri = prefix_ref[...]
        blocks = [ATT_CHAINS * ii + c for c in range(ATT_CHAINS)]
        rows = [slice(c * blk, (c + 1) * blk) for c in range(ATT_CHAINS)]
        qs = [q_ref[r, :] * ATT_SCALE for r in rows]
        dys = [dy_ref[r, :] for r in rows]
        t_pos = [b * blk + lax.broadcasted_iota(jnp.int32, (blk, 1), 0) for b in blocks]

        def one1(c, t, chain, on_diagonal):
            _, right_a, right_b = chain
            slab = blocks[c] - t
            k_start = pl.multiple_of(slab * ATT_SLAB, ATT_SLAB)
            kd = _stack_heads(k_ref[pl.ds(k_start, ATT_SLAB), :], first)
            vd = _stack_heads(v_ref[pl.ds(k_start, ATT_SLAB), :], first)
            mask = _causal_mask(t_pos[c], k_start) if on_diagonal else None
            log_hit, log_fail = _slab_scores(qs[c], kd, mask)
            suffix, (right_a, right_b) = _scan_slab(log_fail, suffix_tri, (right_a, right_b), from_right=True)
            a = _weights(log_hit, suffix, mask)
            da = lax.dot_general(dys[c], vd, (((1,), (1,)), ((), ())), preferred_element_type=F32)
            g_s[c, slab] = (da * a).astype(BF16)
            dv_acc[pl.ds(k_start, ATT_SLAB), :] += _fold_heads(lax.dot_general(
                a.astype(BF16), dys[c], (((0,), (0,)), ((), ())), preferred_element_type=F32), first)
            return jnp.max(jnp.maximum(right_a, right_b)), right_a, right_b

        def step1(state, on_diagonal):
            t, chains = state
            return t + 1, tuple(one1(c, t, chains[c], on_diagonal) for c in range(ATT_CHAINS))

        def more(state):
            t, chains = state
            return (t <= blocks[0]) & (functools.reduce(jnp.maximum, [ch[0] for ch in chains]) > ATT_EXIT_BELOW)

        zero = jnp.zeros((blk, LANES), F32)
        state = step1((0, ((jnp.float32(0.0), zero, zero),) * ATT_CHAINS), on_diagonal=True)
        joint, chains = lax.while_loop(more, functools.partial(step1, on_diagonal=False), state)
        done = [joint]
        for c in range(1, ATT_CHAINS):
            done.append(lax.while_loop(
                lambda s, c=c: (s[0] <= blocks[c]) & (s[1][0] > ATT_EXIT_BELOW),
                lambda s, c=c: (s[0] + 1, one1(c, s[0], s[1], False)), (joint, chains[c]))[0])

        def one2(c, t, carry, on_diagonal):
            dq, left_a, left_b = carry
            slab = blocks[c] - t
            k_start = pl.multiple_of(slab * ATT_SLAB, ATT_SLAB)
            kd = _stack_heads(k_ref[pl.ds(k_start, ATT_SLAB), :], first)
            g = g_s[c, slab]
            z2 = lax.dot_general(qs[c], kd, (((1,), (1,)), ((), ())), preferred_element_type=F32) * LOG2_E
            sig = 1.0 / (1.0 + jnp.exp2(-z2))
            prefix, (left_a, left_b) = _scan_slab(g, prefix_tri, (left_a, left_b), from_right=False)
            dz = g * (1.0 - sig) - sig * prefix
            if on_diagonal:
                dz = jnp.where(_causal_mask(t_pos[c], k_start), dz, 0.0)
            dz = dz.astype(BF16)
            dq = dq + jnp.dot(dz, kd, preferred_element_type=F32)
            dk_acc[pl.ds(k_start, ATT_SLAB), :] += _fold_heads(lax.dot_general(
                dz, qs[c], (((0,), (0,)), ((), ())), preferred_element_type=F32), first)
            return dq, left_a, left_b

        carries = [(zero, zero, zero)]
        for c in range(1, ATT_CHAINS):
            carries.append(lax.fori_loop(
                0, done[c] - joint, lambda n, carry, c=c: one2(c, done[c] - 1 - n, carry, False), (zero, zero, zero)))
        carries = lax.fori_loop(
            0, joint - 1,
            lambda n, cs: tuple(one2(c, joint - 1 - n, cs[c], False) for c in range(ATT_CHAINS)), tuple(carries))
        for c in range(ATT_CHAINS):
            dq_ref[rows[c], :] = (one2(c, 0, carries[c], True)[0] * ATT_SCALE).astype(BF16)

        @pl.when(ii == n_steps - 1)
        def _():
            dk_ref[...] = dk_acc[...].astype(BF16)
            dv_ref[...] = dv_acc[...].astype(BF16)

        bottom()

    out = jax.ShapeDtypeStruct((t_total, n_pairs * LANES), BF16)
    n_rows = ATT_CHAINS * blk
    whole = dict(pipeline_mode=pl.Buffered(1))
    res = pl.pallas_call(
        body, name="attn_bwd", grid=(n_pairs, n_steps),
        in_specs=[pl.BlockSpec((n_rows, LANES), lambda h, i: (i, q_col + h)),
                  pl.BlockSpec((n_rows, LANES), lambda h, i: (i, h)),
                  pl.BlockSpec((t_total, LANES), lambda h, i: (0, k_col + h), **whole),
                  pl.BlockSpec((t_total, LANES), lambda h, i: (0, v_col + h), **whole), _tri_spec(), _tri_spec()]
        + [ANY] * len(rider.operands),
        out_specs=[pl.BlockSpec((n_rows, LANES), lambda h, i: (i, h)),
                   pl.BlockSpec((t_total, LANES), lambda h, i: (0, h)),
                   pl.BlockSpec((t_total, LANES), lambda h, i: (0, h))] + [ANY] * len(rider.out_shapes),
        out_shape=[out, out, out] + list(rider.out_shapes),
        scratch_shapes=list(rider.scratch) + [pltpu.VMEM((ATT_CHAINS, n_slabs, blk, 2 * ATT_SLAB), BF16),
                                              pltpu.VMEM((t_total, LANES), F32), pltpu.VMEM((t_total, LANES), F32)],
        compiler_params=_params("arbitrary", "arbitrary"),
    )(q_src, dy, kv_src, kv_src, _tri(upper=True), _tri(upper=False), *rider.operands)
    return res[:3], res[3:]


def _adamw(name, w, g, m, v):
    def fn(wv, gv, mv, vv):
        mn = ADAM_B1 * mv + (1.0 - ADAM_B1) * gv
        vn = ADAM_B2 * vv + (1.0 - ADAM_B2) * (gv * gv)
        m_hat = mn / (1.0 - ADAM_B1 ** ADAM_STEP)
        v_hat = vn / (1.0 - ADAM_B2 ** ADAM_STEP)
        return -ADAM_LR * (m_hat / (jnp.sqrt(v_hat) + ADAM_EPS) + ADAM_WD * wv), mn, vn

    rows = w.shape[0]
    tr = _row_tile(rows)
    shp = jax.ShapeDtypeStruct(w.shape, F32)
    if rows == 1:
        def body(w_ref, g_ref, m_ref, v_ref, d_ref, mo_ref, vo_ref):
            d, mn, vn = fn(w_ref[...], g_ref[...], m_ref[...], v_ref[...])
            d_ref[...], mo_ref[...], vo_ref[...] = d, mn, vn

        return pl.pallas_call(body, name=name, out_shape=[shp, shp, shp])(w, g, m, v)
    return _rows(name, fn, [w, g, m, v], [shp, shp, shp], tr=tr)


def _place():
    return lax.axis_index("x"), lax.axis_index("y"), lax.axis_index("c")


def _other_chips(x, y):
    return [(1 - x, y), (x, 1 - y), (1 - x, 1 - y)]


ANY = pl.BlockSpec(memory_space=pl.ANY)


def _remote(src, dst, send_sem, recv_sem, to):
    return pltpu.make_async_remote_copy(src_ref=src, dst_ref=dst, send_sem=send_sem, recv_sem=recv_sem,
                                        device_id=to, device_id_type=MESH)


class _WeightGather(_NoRider):
    def __init__(self, shards):
        n_w = len(shards)
        self.operands = list(shards)
        self.out_shapes = [jax.ShapeDtypeStruct((N_CHIPS,) + s.shape, s.dtype) for s in shards]
        self.scratch = [pltpu.SemaphoreType.DMA((3, n_w))] * 4 + [pltpu.SemaphoreType.DMA((n_w,))] * 2

    def _copies(self, ins, outs, sems):
        send_sems, recv_sems, relay_send, relay_recv, own_send, own_recv = sems
        x, y, c = _place()
        my_chip, sibling = 2 * x + y, (x, y, 1 - c)
        n_w = len(ins)

        def half(w, chip, core):
            h = self.operands[w].shape[0] // 2
            return outs[w].at[chip, pl.ds(core * h, h)]

        own = [_remote(ins[w], outs[w].at[my_chip], own_send.at[w], own_recv.at[w], sibling) for w in range(n_w)]
        sends, landed, relays, relayed = [], [], [], []
        for p, (ox, oy) in enumerate(_other_chips(x, y)):
            for w in range(n_w):
                h = self.operands[w].shape[0] // 2
                sends.append(_remote(ins[w].at[pl.ds(c * h, h)], half(w, my_chip, c), send_sems.at[p, w],
                                     recv_sems.at[p, w], (ox, oy, c)))
                here = half(w, 2 * ox + oy, c)
                landed.append(_remote(here, here, send_sems.at[p, w], recv_sems.at[p, w], (ox, oy, c)))
                relays.append(_remote(here, here, relay_send.at[p, w], relay_recv.at[p, w], sibling))
                there = half(w, 2 * ox + oy, 1 - c)
                relayed.append(_remote(there, there, relay_send.at[p, w], relay_recv.at[p, w], sibling))
        return own, sends, landed, relays, relayed

    def start(self, ins, outs, sems):
        own, sends, _, _, _ = self._copies(ins, outs, sems)
        for cp in own + sends:
            cp.start()

    def relay(self, ins, outs, sems):
        _, _, landed, relays, _ = self._copies(ins, outs, sems)
        for arrival, cp in zip(landed, relays):
            arrival.wait_recv()
            cp.start()

    def finish(self, ins, outs, sems):
        own, sends, _, relays, relayed = self._copies(ins, outs, sems)
        for arrival in relayed:
            arrival.wait_recv()
        for cp in sends + relays:
            cp.wait_send()
        for cp in own:
            cp.wait()


class _ChipExchange(_NoRider):
    def __init__(self, pair_sums):
        n_w = len(pair_sums)
        self.operands = list(pair_sums)
        self.out_shapes = [jax.ShapeDtypeStruct((3,) + s.shape[1:], s.dtype) for s in pair_sums]
        self.scratch = [pltpu.SemaphoreType.DMA((3, n_w))] * 2

    def _copies(self, ins, outs, sems):
        send_sems, recv_sems = sems
        x, y, c = _place()
        return [_remote(ins[w].at[2 * ox + oy], outs[w].at[p], send_sems.at[p, w], recv_sems.at[p, w], (ox, oy, c))
                for p, (ox, oy) in enumerate(_other_chips(x, y)) for w in range(len(ins))]

    def start(self, ins, outs, sems):
        for cp in self._copies(ins, outs, sems):
            cp.start()

    def finish(self, ins, outs, sems):
        for cp in self._copies(ins, outs, sems):
            cp.wait()


class _PairExchange(_NoRider):
    def __init__(self, grads):
        n_w = len(grads)
        self.operands = list(grads)
        self.out_shapes = [jax.ShapeDtypeStruct(g.shape[:-2] + (g.shape[-2] // 2, g.shape[-1]), F32) for g in grads]
        self.scratch = [pltpu.SemaphoreType.DMA((n_w,))] * 2

    def _copies(self, ins, theirs, sems):
        send_sems, recv_sems = sems
        x, y, c = _place()
        sends = []
        for w, g in enumerate(self.operands):
            rows = pl.ds((1 - c) * (g.shape[-2] // 2), g.shape[-2] // 2)
            src = ins[w].at[:, rows, :] if g.ndim == 3 else ins[w].at[rows, :]
            sends.append(_remote(src, theirs[w], send_sems.at[w], recv_sems.at[w], (x, y, 1 - c)))
        return sends

    def start(self, ins, outs, sems):
        for cp in self._copies(ins, outs, sems):
            cp.start()

    def finish(self, ins, outs, sems):
        for cp in self._copies(ins, outs, sems):
            cp.wait()


def _exchange_pair_sum(name, place, grad):
    n, r, c = grad.shape
    half = r // 2

    def body(place_ref, g_all, g_ref, o_ref, theirs, send_sems, recv_sems):
        j = pl.program_id(0)
        x, y, core = _place()

        def copy(k):
            return _remote(g_all.at[k, pl.ds((1 - core) * half, half)], theirs.at[k], send_sems.at[k],
                           recv_sems.at[k], (x, y, 1 - core))

        @pl.when(j == 0)
        def _():
            for k in range(n):
                copy(k).start()

        copy(j).wait_recv()
        o_ref[0] = (g_ref[0] + theirs[j]).astype(BF16)

        @pl.when(j == n - 1)
        def _():
            for k in range(n):
                copy(k).wait_send()

    return pl.pallas_call(
        body, name=name, out_shape=jax.ShapeDtypeStruct((n, half, c), BF16),
        grid_spec=pltpu.PrefetchScalarGridSpec(
            num_scalar_prefetch=1, grid=(n,),
            in_specs=[ANY, pl.BlockSpec((1, half, c), lambda j, pr: (j, pr[0], 0))],
            out_specs=pl.BlockSpec((1, half, c), lambda j, pr: (j, 0, 0)),
            scratch_shapes=[pltpu.VMEM((n, half, c), F32), pltpu.SemaphoreType.DMA((n,)),
                            pltpu.SemaphoreType.DMA((n,))]),
        compiler_params=_params("arbitrary"),
    )(place, grad, grad)


class _NoExchanges:
    pair_sums, landed = {}, {}

    def gather(self, names):
        return _NoRider()

    def pair(self, names, grads):
        return _NoRider()

    def paired(self, names, grads, theirs):
        pass

    def pair_now(self, names, grads):
        pass

    def chip(self, names):
        return _NoRider()


class _StepExchanges(_NoExchanges):
    def __init__(self, shards_bf16, place):
        self.shards, self.place = shards_bf16, place
        self.pair_sums, self.landed = {}, {}

    def gather(self, names):
        return _WeightGather([self.shards[n] for n in names])

    def pair(self, names, grads):
        return _PairExchange([grads[n] for n in names])

    def paired(self, names, grads, theirs):
        for n, other in zip(names, theirs):
            self.pair_sums[n] = _pair_sum(f"pair_sum_{n}", self.place, grads[n], other)

    def pair_now(self, names, grads):
        for n in names:
            self.pair_sums[n] = _exchange_pair_sum(f"pair_sum_{n}", self.place, grads[n])

    def chip(self, names):
        return _ChipExchange([self.pair_sums[n] for n in names])


SUM_ROWS = 32


def _finish_gradients(place, pair_sums, landed, vec):
    n_w = len(pair_sums)
    rows = vec.shape[0]
    halves = [s.shape[1:] for s in pair_sums]

    def body(place_ref, *refs):
        sums, lands, v_ref = refs[:n_w], refs[n_w:2 * n_w], refs[2 * n_w]
        outs, o_ref = refs[2 * n_w + 1:3 * n_w + 1], refs[3 * n_w + 1]
        stage = refs[3 * n_w + 2:4 * n_w + 2]
        kept, half_send, half_recv, slots, core_sums, vec_send, vec_recv, sum_send, sum_recv = refs[4 * n_w + 2:]
        x, y, c = _place()
        my_chip, sibling = 2 * x + y, (x, y, 1 - c)
        slots[my_chip] = v_ref[...]
        spread = []
        for p, (ox, oy) in enumerate(_other_chips(x, y)):
            here = slots.at[2 * ox + oy]
            spread.append((_remote(v_ref, slots.at[my_chip], vec_send.at[p], vec_recv.at[p], (ox, oy, c)),
                           _remote(here, here, vec_send.at[p], vec_recv.at[p], (ox, oy, c))))
        for send, _ in spread:
            send.start()
        copies = []
        for w, (h, _) in enumerate(halves):
            step = SUM_ROWS if h % SUM_ROWS == 0 else h

            def sum_rows(i, _, w=w, step=step):
                r = pl.ds(pl.multiple_of(i * step, step), step)
                stage[w][r, :] = functools.reduce(lambda total, p: total + lands[w][p, r, :].astype(F32), range(3),
                                                  sums[w][0, r, :].astype(F32))
                return 0

            lax.fori_loop(0, h // step, sum_rows, 0)
            mine, theirs = outs[w].at[pl.ds(c * h, h)], outs[w].at[pl.ds((1 - c) * h, h)]
            copies.append((pltpu.make_async_copy(stage[w], mine, kept.at[w]),
                           _remote(stage[w], mine, half_send.at[w], half_recv.at[w], sibling),
                           _remote(theirs, theirs, half_send.at[w], half_recv.at[w], sibling)))
            copies[-1][0].start()
            copies[-1][1].start()
        for send, arrival in spread:
            arrival.wait_recv()
            send.wait_send()
        core_sums[c] = functools.reduce(lambda total, chip: total + slots[chip], range(1, N_CHIPS), slots[0])
        mine, theirs = core_sums.at[c], core_sums.at[1 - c]
        to_sibling = _remote(mine, mine, sum_send.at[0], sum_recv.at[0], sibling)
        to_sibling.start()
        _remote(theirs, theirs, sum_send.at[0], sum_recv.at[0], sibling).wait_recv()
        to_sibling.wait_send()
        o_ref[...] = core_sums[0] + core_sums[1]
        for keep, send, arrival in copies:
            keep.wait()
            arrival.wait_recv()
            send.wait_send()

    once = dict(pipeline_mode=pl.Buffered(1))
    vm = pl.BlockSpec((rows, LANES), lambda i, pr: (0, 0))
    res = pl.pallas_call(
        body, name="finish_gradients",
        grid_spec=pltpu.PrefetchScalarGridSpec(
            num_scalar_prefetch=1, grid=(1,),
            in_specs=[pl.BlockSpec((1,) + hc, lambda i, pr: (pr[1], 0, 0), **once) for hc in halves]
            + [pl.BlockSpec((3,) + hc, lambda i, pr: (0, 0, 0), **once) for hc in halves] + [vm],
            out_specs=[ANY] * n_w + [vm],
            scratch_shapes=[pltpu.VMEM(hc, F32) for hc in halves]
            + [pltpu.SemaphoreType.DMA((n_w,))] * 3
            + [pltpu.VMEM((N_CHIPS, rows, LANES), F32), pltpu.VMEM((2, rows, LANES), F32),
               pltpu.SemaphoreType.DMA((N_CHIPS - 1,)), pltpu.SemaphoreType.DMA((N_CHIPS - 1,)),
               pltpu.SemaphoreType.DMA((1,)), pltpu.SemaphoreType.DMA((1,))]),
        out_shape=[_sds((2 * h, cols), F32) for h, cols in halves] + [_sds(vec.shape, F32)],
        compiler_params=_params("arbitrary"),
    )(place, *pair_sums, *landed, vec)
    return res[:n_w], res[n_w]


def _row_tile(rows):
    fits = [tr for tr in range(16, min(rows, 512) + 1, 16) if rows % tr == 0]
    return max(fits) if fits else rows


def _pair_sum(name, place, grad, theirs):
    if grad.ndim == 2:
        return _pair_sum_joined(name, place, grad, theirs)
    n, r, c = grad.shape
    half = r // 2
    tr = _row_tile(half)
    nb = half // tr

    def body(place_ref, g_ref, t_ref, o_ref):
        o_ref[...] = (g_ref[...] + t_ref[...]).astype(BF16)

    return pl.pallas_call(
        body, name=name, out_shape=jax.ShapeDtypeStruct((n, half, c), BF16),
        grid_spec=pltpu.PrefetchScalarGridSpec(
            num_scalar_prefetch=1, grid=(n, nb),
            in_specs=[pl.BlockSpec((1, tr, c), lambda j, i, pr: (j, pr[0] * nb + i, 0)),
                      pl.BlockSpec((1, tr, c), lambda j, i, pr: (j, i, 0))],
            out_specs=pl.BlockSpec((1, tr, c), lambda j, i, pr: (j, i, 0))),
        compiler_params=_params("parallel", "parallel"),
    )(place, grad, theirs)


def _pair_sum_joined(name, place, grad, theirs):
    r, wide = grad.shape
    half, c = r // 2, wide // N_CHIPS
    tr = _row_tile(half)
    nb = half // tr

    def body(place_ref, g_ref, t_ref, o_ref):
        for j in range(N_CHIPS):
            cols = slice(j * c, (j + 1) * c)
            o_ref[j] = (g_ref[:, cols] + t_ref[:, cols]).astype(BF16)

    return pl.pallas_call(
        body, name=name, out_shape=jax.ShapeDtypeStruct((N_CHIPS, half, c), BF16),
        grid_spec=pltpu.PrefetchScalarGridSpec(
            num_scalar_prefetch=1, grid=(nb,),
            in_specs=[pl.BlockSpec((tr, wide), lambda i, pr: (pr[0] * nb + i, 0)),
                      pl.BlockSpec((tr, wide), lambda i, pr: (i, 0))],
            out_specs=pl.BlockSpec((N_CHIPS, tr, c), lambda i, pr: (0, i, 0))),
        compiler_params=_params("parallel"),
    )(place, grad, theirs)


MIXER = ("w_branch_a", "w_branch_b", "w_out")
FFN_PLE = ("w_ffn_gate", "w_ffn_up", "w_ffn_down", "w_ple_gate", "w_ple_proj")
LATE = MIXER + FFN_PLE
BIG = ("w_in",) + LATE
HELD_TRANSPOSED = ("w_ffn_gate", "w_ffn_up")
SMALL = ("norm_mix", "w_pool", "pool_scale", "norm_ffn", "norm_ple", "norm_final")


def _join_columns(w4):
    return jnp.concatenate([w4[j] for j in range(N_CHIPS)], axis=1)


def _sds(shape, dtype):
    return jax.ShapeDtypeStruct(shape, dtype)


def _local_step(x, p, target, wf, small, ex=None):
    t, d = x.shape
    w_pool_b = small["w_pool"].astype(BF16)
    dp = w_pool_b.shape[0] * w_pool_b.shape[1]

    ex = ex or _NoExchanges()
    h1, first = _norm_fwd("norm_mix", x, small["norm_mix"], rider=ex.gather(("w_in",)))
    wf = {**wf, **dict(zip(("w_in",), first))}
    w_in = wf["w_in"]
    u, q, kv, ga, gb = _mm(
        "proj", [h1], [w_in[j] for j in range(N_CHIPS)], "nn",
        [_sds((t, dp), F32), _sds((t, dp), BF16), _sds((t, d), BF16), _sds((t, d), BF16), _sds((t, d), BF16)],
        separate=True, epilogue=lambda uq, kv_, ga_, gb_: (uq[:, :dp], uq[:, dp:], kv_, ga_, gb_), tm=512)
    pooled, ya = _pool_fwd(u, w_pool_b, small["pool_scale"])
    n_pairs = dp // LANES
    yb, late = _attn_fwd(q, 0, kv, 0, n_pairs, n_pairs, rider=ex.gather(LATE))
    wf = {**wf, **dict(zip(LATE, late))}
    w_down = wf["w_ffn_down"].reshape(-1, d)
    dff = w_down.shape[0]
    w_gate_t, w_up_t = wf["w_ffn_gate"].reshape(dff, d), wf["w_ffn_up"].reshape(dff, d)
    w_a, w_b, w_pp = _join_columns(wf["w_branch_a"]), _join_columns(wf["w_branch_b"]), _join_columns(wf["w_ple_proj"])
    w_out = wf["w_out"].reshape(d, d)
    w_pg = wf["w_ple_gate"].reshape(d, d)
    def residual_norm(branch, xv, g, w):
        xn = xv + jnp.dot(branch.astype(BF16), w, preferred_element_type=F32)
        return xn, xn * lax.rsqrt(jnp.mean(xn * xn, axis=-1, keepdims=True) + RMS_EPS) * g

    def mixer_tail(tav, tbv, gav, gbv, xv, g, w):
        merged = _sigmoid(gav) * tav + _sigmoid(gbv) * tbv
        return (tav, tbv, merged) + residual_norm(merged, xv, g, w)

    def ffn_tail(gv, uv, xv, g, w):
        act = gv * _sigmoid(gv) * uv
        return (gv, uv, act) + residual_norm(act, xv, g, w)

    stream = [_sds((t, d), F32), _sds((t, d), BF16)]
    ta, tb, merged, x1, h2 = _mm(
        "mixer_out", [ya, yb], [w_a, w_b], "nn", [_sds((t, d), BF16)] * 3 + stream,
        extras=[ga, gb, x, small["norm_ffn"]], wholes=[w_out], separate=True, epilogue=mixer_tail, tm=512)
    gate, up, act, x2, h3 = _mm(
        "ffn", [h2], [w_gate_t, w_up_t], "nt", [_sds((t, dff), BF16)] * 3 + stream,
        extras=[x1, small["norm_ple"]], wholes=[w_down], separate=True, epilogue=ffn_tail, tm=256)
    dx2, dx2_b, d_pp, d_gp, d_norm_final, loss_row, d_norm_ple = _mm(
        "ple_loss", [h3, p], [w_pg, w_pp], "nn", stream + [_sds((t, d), BF16)] * 2,
        extras=[x2, target, small["norm_final"].reshape(1, d), small["norm_ple"]], wholes=[w_pg], separate=True,
        epilogue=_ple_and_loss, sum_shapes=[_sds((1, d), F32)] * 3, tm=512)

    def through_norm(dh, xv, g, dres):
        dx, d_gain = _rms_norm_bwd(dh, xv, g)
        return dx + dres, dx + dres, d_gain

    gain_sum = [_sds((1, d), F32)]
    g_w_pp, g_w_pg = _mm_tn("g_ple", [p, h3], [d_pp, d_gp])

    def ffn_bwd(d_act, gv, uv, xv, g, dres, wg_t, wu_t):
        s = _sigmoid(gv)
        d_gate, d_up = d_act * uv * (s * (1.0 + gv * (1.0 - s))), d_act * (gv * s)
        dh2 = (jnp.dot(d_gate.astype(BF16), wg_t, preferred_element_type=F32)
               + jnp.dot(d_up.astype(BF16), wu_t, preferred_element_type=F32))
        return (d_gate, d_up) + through_norm(dh2, xv, g, dres)

    d_gate, d_up, dx1, dx1_b, d_norm_ffn = _mm(
        "ffn_bwd", [dx2_b], [w_down], "nt", [_sds((t, dff), BF16)] * 2 + stream,
        extras=[gate, up, x1, small["norm_ffn"], dx2], wholes=[w_gate_t, w_up_t], epilogue=ffn_bwd,
        sum_shapes=gain_sum, tm=256)
    g_w_down, = _mm_tn("g_ffn_down", [act], [dx2_b], tmm=512)
    g_w_gate_t, g_w_up_t = _mm_tn("g_ffn_gate_up", [d_gate, d_up], [h2], k_blocks=2)

    def merge_bwd(acc, tav, tbv, gav, gbv):
        sa, sb = _sigmoid(gav), _sigmoid(gbv)
        return acc * sa, acc * sb, acc * tav * sa * (1.0 - sa), acc * tbv * sb * (1.0 - sb)

    big = {
        "w_ffn_gate": g_w_gate_t.reshape(wf["w_ffn_gate"].shape), "w_ffn_up": g_w_up_t.reshape(wf["w_ffn_up"].shape),
        "w_ffn_down": g_w_down.reshape(wf["w_ffn_down"].shape),
        "w_ple_gate": g_w_pg.reshape(wf["w_ple_gate"].shape), "w_ple_proj": g_w_pp,
    }
    (d_ta, d_tb, d_ga, d_gb), theirs = _mm(
        "d_merged", [dx1_b], [w_out], "nt", [_sds((t, d), BF16)] * 4, extras=[ta, tb, ga, gb], epilogue=merge_bwd,
        tm=512, rider=ex.pair(FFN_PLE, big))
    ex.paired(FFN_PLE, big, theirs)
    g_w_out, big["w_branch_a"], big["w_branch_b"] = _mm_tn("g_mixer", [merged, ya, yb], [dx1_b, d_ta, d_tb])
    big["w_out"] = g_w_out.reshape(wf["w_out"].shape)
    (d_ya, d_yb), theirs = _mm(
        "d_branches", [d_ta, d_tb], [w_a, w_b], "nt", [_sds((t, dp), F32), _sds((t, dp), BF16)], separate=True,
        rider=ex.pair(MIXER, big))
    ex.paired(MIXER, big, theirs)
    d_u, g_w_pool, d_pool_scale = _pool_bwd(d_ya, pooled, w_pool_b, small["pool_scale"])
    (d_q, d_k, d_v), landed = _attn_bwd(q, 0, kv, 0, n_pairs, d_yb, n_pairs, rider=ex.chip(LATE))
    ex.landed.update(zip(LATE, landed))
    d_proj = [(d_u, d_q), (d_k, d_v), d_ga, d_gb]
    big["w_in"], = _mm_tn("g_w_in", [h1], d_proj, tmm=512, stacked=True)
    ex.pair_now(("w_in",), big)
    (grad_x, d_norm_mix), landed = _mm(
        "d_h1", d_proj, [w_in[j] for j in range(N_CHIPS)], "nt", [_sds((t, d), F32)],
        extras=[x, small["norm_mix"], dx1], epilogue=lambda dh, xv, g, dres: through_norm(dh, xv, g, dres)[1:],
        sum_shapes=gain_sum, tm=512, rider=ex.chip(("w_in",)))
    ex.landed.update(zip(("w_in",), landed))
    small_g = {"norm_mix": d_norm_mix, "w_pool": g_w_pool, "pool_scale": d_pool_scale, "norm_ffn": d_norm_ffn,
               "norm_ple": d_norm_ple, "norm_final": d_norm_final}
    return grad_x, big, small_g, loss_row


def _pack_small(small_g, loss_row):
    parts, layout = [], []
    for name in SMALL + ("loss",):
        v = (loss_row if name == "loss" else small_g[name]).reshape(-1, LANES)
        pad = (-v.shape[0]) % 8
        if pad:
            v = jnp.concatenate([v, jnp.zeros((pad, LANES), F32)], axis=0)
        layout.append((name, sum(q.shape[0] for q in parts), v.shape[0]))
        parts.append(v)
    return jnp.concatenate(parts, axis=0), layout


def kernel(x, p, norm_mix, w_in, w_pool, pool_scale, w_branch_a, w_branch_b, w_out, norm_ffn, w_ffn_gate, w_ffn_up, w_ffn_down, norm_ple, w_ple_gate, w_ple_proj, norm_final, loss_target, m_norm_mix, m_w_in, m_w_pool, m_pool_scale, m_w_branch_a, m_w_branch_b, m_w_out, m_norm_ffn, m_w_ffn_gate, m_w_ffn_up, m_w_ffn_down, m_norm_ple, m_w_ple_gate, m_w_ple_proj, m_norm_final, v_norm_mix, v_w_in, v_w_pool, v_pool_scale, v_w_branch_a, v_w_branch_b, v_w_out, v_norm_ffn, v_w_ffn_gate, v_w_ffn_up, v_w_ffn_down, v_norm_ple, v_w_ple_gate, v_w_ple_proj, v_norm_final):
    given = dict(locals())
    order = ("norm_mix", "w_in", "w_pool", "pool_scale", "w_branch_a", "w_branch_b", "w_out", "norm_ffn", "w_ffn_gate",
             "w_ffn_up", "w_ffn_down", "norm_ple", "w_ple_gate", "w_ple_proj", "norm_final")
    t, d = x.shape[1], x.shape[2]
    def local(a, n):
        return jnp.swapaxes(a[0], 0, 1) if n in HELD_TRANSPOSED else a[0]

    def back(a, n):
        return (jnp.swapaxes(a, 0, 1) if n in HELD_TRANSPOSED else a)[None]

    shard = {n: local(given[n], n) for n in BIG}
    small = {"norm_mix": norm_mix, "w_pool": w_pool[0], "pool_scale": pool_scale, "norm_ffn": norm_ffn,
             "norm_ple": norm_ple, "norm_final": norm_final}

    place = jnp.stack([lax.axis_index("c"), 2 * lax.axis_index("x") + lax.axis_index("y")]).astype(jnp.int32)
    ex = _StepExchanges({n: shard[n].astype(BF16) for n in BIG}, place)
    grad_x, _, small_g, loss_row = _local_step(
        x.reshape(t, d), p.reshape(t, p.shape[-1]), loss_target.reshape(t, d), {}, small, ex)
    packed, layout = _pack_small(small_g, loss_row)
    filled, reduced = _finish_gradients(place, [ex.pair_sums[n] for n in BIG], [ex.landed[n] for n in BIG], packed)
    grads = dict(zip(BIG, filled))
    for name, start, rows in layout:
        if name == "loss":
            loss = jnp.sum(reduced[start:start + rows])
        else:
            n_el = small[name].size
            grads[name] = reduced[start:start + rows].reshape(-1)[:n_el]

    deltas, new_m, new_v = {}, {}, {}
    for n in order:
        if n in BIG:
            w, m, v = shard[n], local(given["m_" + n], n), local(given["v_" + n], n)
            dl, mn, vn = _adamw(f"adamw_{n}", w, grads[n], m, v)
            grads[n], deltas[n], new_m[n], new_v[n] = [back(a, n) for a in (grads[n], dl, mn, vn)]
        else:
            w, full = small[n], given[n].shape
            shape2 = (1, w.shape[0]) if w.ndim == 1 else (w.shape if w.ndim == 2 else (w.shape[0] * w.shape[1], w.shape[2]))
            dl, mn, vn = _adamw(f"adamw_{n}", w.reshape(shape2), grads[n].reshape(shape2),
                                given["m_" + n].reshape(shape2), given["v_" + n].reshape(shape2))
            grads[n], deltas[n], new_m[n], new_v[n] = [a.reshape(full) for a in (grads[n], dl, mn, vn)]

    return (loss, grad_x.reshape(x.shape), *[grads[n] for n in order], *[deltas[n] for n in order],
            *[new_m[n] for n in order], *[new_v[n] for n in order])
```

```python
import functools
import math

import jax
import jax.numpy as jnp
from jax import lax
from jax.experimental import pallas as pl
from jax.experimental.pallas import tpu as pltpu

F32 = jnp.float32
BF16 = jnp.bfloat16
MESH = pl.DeviceIdType.MESH

RMS_EPS = 1e-6
POOL_WINDOWS = (2, 4, 8, 16)
POOL_HALO = 16
HEAD_DIM = 64
LANES = 128
ATT_BLOCK = 256
ATT_CHAINS = 2
ATT_FWD_CHAINS = 4
ATT_CHUNK = 256
ATT_SLAB = 256
ATT_SCALE = 1.0 / math.sqrt(HEAD_DIM)
LOG2_E = 1.4426950408889634
ATT_EXIT_BELOW = -150.5
ADAM_LR, ADAM_B1, ADAM_B2, ADAM_EPS, ADAM_WD, ADAM_STEP = 0.001, 0.9, 0.999, 1e-08, 0.01, 10
V7X_VMEM_LIMIT_BYTES = 56 * 1024 * 1024
N_CHIPS = 4
N_DEV = 8


def _params(*semantics):
    return pltpu.CompilerParams(dimension_semantics=semantics, vmem_limit_bytes=V7X_VMEM_LIMIT_BYTES)


def _sigmoid(z):
    return 1.0 / (1.0 + jnp.exp(-z))


def _tiled_spec(shape, tm, tn, n_total, at):
    rows, width = shape
    if rows == 1:
        if width == n_total:
            return pl.BlockSpec((1, tn), at(lambda i, j: (0, j)))
        return pl.BlockSpec((1, width), at(lambda i, j: (0, 0)))
    if width == n_total:
        return pl.BlockSpec((tm, tn), at(lambda i, j: (i, j)))
    assert tn == n_total, "an operand narrower than the output needs whole output rows per tile"
    return pl.BlockSpec((tm, width), at(lambda i, j: (i, 0)))


def _column_pieces(operands):
    pieces = [tuple(a) if isinstance(a, (tuple, list)) else (a,) for a in operands]
    return [p for ps in pieces for p in ps], [len(ps) for ps in pieces]


def _load_bf16(refs, counts):
    tiles, k = [], 0
    for n in counts:
        parts = [r[...] for r in refs[k:k + n]]
        parts = [t if t.dtype == BF16 else t.astype(BF16) for t in parts]
        tiles.append(parts[0] if n == 1 else jnp.concatenate(parts, axis=1))
        k += n
    return tiles


def _mm(name, a_list, b_list, mode, out_shapes, epilogue=None, extras=(), tm=1024, tn=None, separate=False,
        sum_shapes=(), rider=None, wholes=()):
    flat_a, counts = _column_pieces(a_list)
    m_total = flat_a[0].shape[0]
    n_total = b_list[0].shape[1] if mode == "nn" else b_list[0].shape[0]
    tn = n_total if tn is None else tn
    tm = min(tm, m_total)
    assert m_total % tm == 0 and n_total % tn == 0 and (not sum_shapes or tn == n_total)
    n_a, n_b, n_extra, n_out = len(counts), len(b_list), len(extras), len(out_shapes)
    assert n_a in (1, n_b)
    dims = (((1,), (0,)), ((), ())) if mode == "nn" else (((1,), (1,)), ((), ()))
    with_rider = rider is not None
    rider = rider or _NoRider()
    grid = (n_total // tn, m_total // tm)

    def at(index):
        return lambda j, i: index(i, j)

    def body(*refs):
        ins, o_refs, _, riding = rider.split(refs, len(flat_a) + n_b + n_extra + len(wholes), n_out + len(sum_shapes))
        a_refs, b_refs = ins[:len(flat_a)], ins[len(flat_a):len(flat_a) + n_b]
        e_refs, w_refs = ins[len(flat_a) + n_b:len(flat_a) + n_b + n_extra], ins[len(flat_a) + n_b + n_extra:]
        at_first = (pl.program_id(0) == 0) & (pl.program_id(1) == 0)
        at_last = (pl.program_id(0) == grid[0] - 1) & (pl.program_id(1) == grid[1] - 1)
        top, bottom = rider.at_steps(riding, at_first, at_first, at_last)
        top()
        lefts = _load_bf16(a_refs, counts)
        products = [lax.dot_general(lefts[s % n_a], b_refs[s][...], dims, preferred_element_type=F32)
                    for s in range(n_b)]
        if not separate:
            products = [functools.reduce(lambda p, r: p + r, products)]
        extra_tiles = [e[...].astype(F32) for e in e_refs]
        outs = products if epilogue is None else epilogue(*products, *extra_tiles, *[w[...] for w in w_refs])
        for o_ref, o in zip(o_refs[:n_out], outs[:n_out]):
            o_ref[...] = o.astype(o_ref.dtype)
        if sum_shapes:
            @pl.when(pl.program_id(1) == 0)
            def _():
                for s_ref in o_refs[n_out:]:
                    s_ref[...] = jnp.zeros_like(s_ref)

            for s_ref, s in zip(o_refs[n_out:], outs[n_out:]):
                s_ref[...] += s
        bottom()

    once = dict(pipeline_mode=pl.Buffered(1)) if tn == n_total else {}
    in_specs = [pl.BlockSpec((tm, a.shape[1]), at(lambda i, j: (i, 0))) for a in flat_a]
    if mode == "nn":
        in_specs += [pl.BlockSpec((b.shape[0], tn), at(lambda i, j: (0, j)), **once) for b in b_list]
    else:
        in_specs += [pl.BlockSpec((tn, b.shape[1]), at(lambda i, j: (j, 0)), **once) for b in b_list]
    in_specs += [_tiled_spec(e.shape, tm, tn, n_total, at) for e in extras]
    in_specs += [pl.BlockSpec(w.shape, lambda j, i: (0, 0), pipeline_mode=pl.Buffered(1)) for w in wholes]
    out_specs = [_tiled_spec(o.shape, tm, tn, n_total, at) for o in out_shapes]
    out_specs += [pl.BlockSpec(s.shape, at(lambda i, j: (0, 0))) for s in sum_shapes]
    semantics = ("arbitrary", "arbitrary") if sum_shapes or rider.operands else ("parallel", "parallel")
    res = pl.pallas_call(
        body, name=name, grid=grid, in_specs=in_specs + [ANY] * len(rider.operands),
        out_specs=out_specs + [ANY] * len(rider.out_shapes),
        out_shape=list(out_shapes) + list(sum_shapes) + list(rider.out_shapes), scratch_shapes=list(rider.scratch),
        compiler_params=_params(*semantics),
    )(*flat_a, *b_list, *extras, *wholes, *rider.operands)
    n_own = len(out_shapes) + len(sum_shapes)
    return (res[:n_own], res[n_own:]) if with_rider else res


def _mm_tn(name, a_list, b_list, tmm=1024, stacked=False, k_blocks=1):
    flat_b, counts = _column_pieces(b_list)
    n_a, n_b = len(a_list), len(counts)
    n_prod = max(n_a, n_b)
    m_total = a_list[0].shape[0]
    ks = [a_list[s % n_a].shape[1] for s in range(n_prod)]
    widths = [sum(p.shape[1] for p in flat_b[sum(counts[:s]):sum(counts[:s + 1])]) for s in range(n_b)]
    widths = [widths[s % n_b] for s in range(n_prod)]
    tmm = min(tmm, m_total)
    assert m_total % tmm == 0 and all(k % k_blocks == 0 for k in ks)
    assert n_a in (1, n_prod) and n_b in (1, n_prod) and not (stacked and n_a > 1)

    def body(*refs):
        a_refs, b_refs, o_refs = refs[:n_a], refs[n_a:n_a + len(flat_b)], refs[n_a + len(flat_b):]

        @pl.when(pl.program_id(1) == 0)
        def _():
            for o_ref in o_refs:
                o_ref[...] = jnp.zeros_like(o_ref)

        lefts, rights = _load_bf16(a_refs, [1] * n_a), _load_bf16(b_refs, counts)
        for s in range(n_prod):
            product = lax.dot_general(lefts[s % n_a], rights[s % n_b], (((0,), (0,)), ((), ())),
                                      preferred_element_type=F32)
            if stacked:
                o_refs[0][s] += product
            else:
                o_refs[s][...] += product

    in_specs = [pl.BlockSpec((tmm, a.shape[1] // k_blocks), lambda kb, m: (m, kb)) for a in a_list]
    in_specs += [pl.BlockSpec((tmm, b.shape[1]), lambda kb, m: (m, 0)) for b in flat_b]
    if stacked:
        out_shape = [jax.ShapeDtypeStruct((n_prod, ks[0], widths[0]), F32)]
        out_specs = [pl.BlockSpec((n_prod, ks[0] // k_blocks, widths[0]), lambda kb, m: (0, kb, 0))]
    else:
        out_shape = [jax.ShapeDtypeStruct((k, w), F32) for k, w in zip(ks, widths)]
        out_specs = [pl.BlockSpec((k // k_blocks, w), lambda kb, m: (kb, 0)) for k, w in zip(ks, widths)]
    return pl.pallas_call(
        body, name=name, grid=(k_blocks, m_total // tmm), in_specs=in_specs, out_specs=out_specs, out_shape=out_shape,
        compiler_params=_params("arbitrary", "arbitrary"),
    )(*a_list, *flat_b)


def _rows(name, fn, ins, tile_outs, sum_outs=(), tr=512, rider=None):
    t_total = max(a.shape[0] for a in ins)
    tr = min(tr, t_total)
    assert t_total % tr == 0
    n_in, n_tile = len(ins), len(tile_outs)
    rider = rider or _NoRider()
    n_steps = t_total // tr

    def body(*refs):
        own_ins, own_outs, _, riding = rider.split(refs, n_in, n_tile + len(sum_outs))
        step = pl.program_id(0)
        top, bottom = rider.at_steps(riding, step == 0, step == n_steps - 1, step == n_steps - 1)
        top()
        refs = tuple(own_ins) + tuple(own_outs)
        outs = fn(*[r[...].astype(F32) for r in refs[:n_in]])
        for o_ref, o in zip(refs[n_in:n_in + n_tile], outs[:n_tile]):
            o_ref[...] = o.astype(o_ref.dtype)
        if sum_outs:
            @pl.when(pl.program_id(0) == 0)
            def _():
                for s_ref in refs[n_in + n_tile:]:
                    s_ref[...] = jnp.zeros_like(s_ref)

            for s_ref, s in zip(refs[n_in + n_tile:], outs[n_tile:]):
                s_ref[...] += s
        bottom()

    def spec(shape):
        if shape[0] == 1:
            return pl.BlockSpec(shape, lambda i: (0, 0))
        return pl.BlockSpec((tr, shape[1]), lambda i: (i, 0))

    return pl.pallas_call(
        body, name=name, grid=(n_steps,), in_specs=[spec(a.shape) for a in ins] + [ANY] * len(rider.operands),
        out_specs=[spec(o.shape) for o in tile_outs] + [spec(s.shape) for s in sum_outs] + [ANY] * len(rider.out_shapes),
        out_shape=list(tile_outs) + list(sum_outs) + list(rider.out_shapes), scratch_shapes=list(rider.scratch),
        compiler_params=_params("arbitrary" if sum_outs or rider.operands else "parallel"),
    )(*ins, *rider.operands)


def _norm_fwd(name, x, gain, rider=None):
    def fn(xv, g):
        inv = lax.rsqrt(jnp.mean(xv * xv, axis=-1, keepdims=True) + RMS_EPS)
        return (xv * inv * g,)

    res = _rows(name, fn, [x, gain], [jax.ShapeDtypeStruct(x.shape, BF16)], rider=rider)
    return res[0], res[1:]


def _rms_norm_bwd(dh, xv, g):
    inv = lax.rsqrt(jnp.mean(xv * xv, axis=-1, keepdims=True) + RMS_EPS)
    xn = xv * inv
    dxn = dh * g
    return inv * (dxn - xn * jnp.mean(dxn * xn, axis=-1, keepdims=True)), jnp.sum(dh * xn, axis=0, keepdims=True)


def _ple_and_loss(gv, pv, x2v, tv, g_final, g_ple, w_pg):
    d = x2v.shape[1]
    s = _sigmoid(gv)
    xv = x2v + s * pv
    inv = lax.rsqrt(jnp.mean(xv * xv, axis=-1, keepdims=True) + RMS_EPS)
    err = xv * inv * g_final - tv
    dx3, d_final = _rms_norm_bwd(err * (1.0 / d), xv, g_final)
    d_pp, d_gp = dx3 * s, dx3 * pv * s * (1.0 - s)
    dh3 = lax.dot_general(d_gp.astype(BF16), w_pg, (((1,), (1,)), ((), ())), preferred_element_type=F32)
    dx2, d_ple = _rms_norm_bwd(dh3, x2v, g_ple)
    dx2 = dx2 + dx3
    return dx2, dx2, d_pp, d_gp, d_final, (0.5 / d) * jnp.sum(err * err, axis=0, keepdims=True), d_ple


def _window_counts(t_pos, w):
    return jnp.minimum(t_pos + 1, w).astype(F32)


def _pool_fwd(u, w_pool, scale, tr=512):
    t_total, width = u.shape
    tr = min(tr, t_total)
    n_groups = len(POOL_WINDOWS)
    gdim = width // n_groups
    ext = tr + POOL_HALO

    def body(u_ref, halo_ref, w_ref, s_ref, pooled_ref, ya_ref):
        i = pl.program_id(0)
        halo = jnp.where(i == 0, 0.0, halo_ref[...])
        t_pos = i * tr + lax.broadcasted_iota(jnp.int32, (tr, 1), 0)
        for g, w in enumerate(POOL_WINDOWS):
            cols = slice(g * gdim, (g + 1) * gdim)
            main = u_ref[:, cols]
            win = jnp.concatenate([halo[:, cols], main], axis=0)
            span = 1
            while span < w:
                win = win + pltpu.roll(win, span, 0)
                span *= 2
            pooled = win[POOL_HALO:, :] * (1.0 / _window_counts(t_pos, w)) - main
            pooled_b = pooled.astype(BF16)
            pooled_ref[:, cols] = pooled_b
            mixed = jnp.dot(pooled_b, w_ref[g], preferred_element_type=F32)
            ya_ref[:, cols] = (mixed * s_ref[:, cols]).astype(BF16)

    hb = tr // POOL_HALO
    return pl.pallas_call(
        body, name="pool_fwd", grid=(t_total // tr,),
        in_specs=[pl.BlockSpec((tr, width), lambda i: (i, 0)),
                  pl.BlockSpec((POOL_HALO, width), lambda i: (jnp.maximum(i * hb - 1, 0), 0)),
                  pl.BlockSpec((n_groups, gdim, gdim), lambda i: (0, 0, 0)),
                  pl.BlockSpec((1, width), lambda i: (0, 0))],
        out_specs=[pl.BlockSpec((tr, width), lambda i: (i, 0)), pl.BlockSpec((tr, width), lambda i: (i, 0))],
        out_shape=[jax.ShapeDtypeStruct(u.shape, BF16), jax.ShapeDtypeStruct(u.shape, BF16)],
        compiler_params=_params("parallel"),
    )(u, u, w_pool, scale)


def _pool_bwd(dya, pooled, w_pool, scale, tr=512):
    t_total, width = dya.shape
    tr = min(tr, t_total)
    n_groups = len(POOL_WINDOWS)
    gdim = width // n_groups
    ext = tr + POOL_HALO
    n_tiles = t_total // tr

    def body(d_ref, halo_ref, p_ref, w_ref, s_ref, du_ref, dw_ref, ds_ref):
        i = pl.program_id(0)

        @pl.when(i == 0)
        def _():
            dw_ref[...] = jnp.zeros_like(dw_ref)
            ds_ref[...] = jnp.zeros_like(ds_ref)

        halo = jnp.where(i == n_tiles - 1, 0.0, halo_ref[...])
        t_pos = i * tr + lax.broadcasted_iota(jnp.int32, (ext, 1), 0)
        for g, w in enumerate(POOL_WINDOWS):
            cols = slice(g * gdim, (g + 1) * gdim)
            sc = s_ref[:, cols]
            d_main = d_ref[:, cols]
            pooled_b = p_ref[:, cols]
            mixed = jnp.dot(pooled_b, w_ref[g], preferred_element_type=F32)
            ds_ref[:, cols] += jnp.sum(d_main * mixed, axis=0, keepdims=True)
            dmix = (jnp.concatenate([d_main, halo[:, cols]], axis=0) * sc).astype(BF16)
            dw_ref[g] += lax.dot_general(pooled_b, dmix[:tr, :], (((0,), (0,)), ((), ())),
                                         preferred_element_type=F32)
            dpool = lax.dot_general(dmix, w_ref[g], (((1,), (1,)), ((), ())), preferred_element_type=F32)
            win = dpool * (1.0 / _window_counts(t_pos, w))
            span = 1
            while span < w:
                win = win + pltpu.roll(win, ext - span, 0)
                span *= 2
            du_ref[:, cols] = (win[:tr, :] - dpool[:tr, :]).astype(BF16)

    hb = tr // POOL_HALO
    last_halo = t_total // POOL_HALO - 1
    return pl.pallas_call(
        body, name="pool_bwd", grid=(n_tiles,),
        in_specs=[pl.BlockSpec((tr, width), lambda i: (i, 0)),
                  pl.BlockSpec((POOL_HALO, width), lambda i: (jnp.minimum((i + 1) * hb, last_halo), 0)),
                  pl.BlockSpec((tr, width), lambda i: (i, 0)),
                  pl.BlockSpec((n_groups, gdim, gdim), lambda i: (0, 0, 0)),
                  pl.BlockSpec((1, width), lambda i: (0, 0))],
        out_specs=[pl.BlockSpec((tr, width), lambda i: (i, 0)),
                   pl.BlockSpec((n_groups, gdim, gdim), lambda i: (0, 0, 0)),
                   pl.BlockSpec((1, width), lambda i: (0, 0))],
        out_shape=[jax.ShapeDtypeStruct(dya.shape, BF16), jax.ShapeDtypeStruct((n_groups, gdim, gdim), F32),
                   jax.ShapeDtypeStruct((1, width), F32)],
        compiler_params=_params("arbitrary"),
    )(dya, dya, pooled, w_pool, scale)


def _head_masks():
    lane = lax.broadcasted_iota(jnp.int32, (1, LANES), 1)
    return lane < HEAD_DIM


def _stack_heads(tile, first):
    zero = jnp.zeros_like(tile)
    return jnp.concatenate([jnp.where(first, tile, zero), jnp.where(first, zero, tile)], axis=0)


def _causal_mask(t_pos, k_start):
    col = lax.broadcasted_iota(jnp.int32, (1, 2 * ATT_SLAB), 1)
    return k_start + (col & (ATT_SLAB - 1)) < t_pos


def _slab_scores(q, kd, mask):
    z2 = lax.dot_general(q, kd, (((1,), (1,)), ((), ())), preferred_element_type=F32) * LOG2_E
    log_hit = jnp.minimum(z2, 0.0) - jnp.log2(1.0 + jnp.exp2(-jnp.abs(z2)))
    log_fail = log_hit - z2
    return log_hit, (log_fail if mask is None else jnp.where(mask, log_fail, 0.0))


def _weights(log_hit, suffix, mask):
    arg = log_hit + suffix
    return jnp.exp2(arg if mask is None else jnp.where(mask, arg, -1e30))


def _tri(upper):
    r = lax.broadcasted_iota(jnp.int32, (ATT_CHUNK, ATT_CHUNK), 0)
    c = lax.broadcasted_iota(jnp.int32, (ATT_CHUNK, ATT_CHUNK), 1)
    return jnp.where(r > c if upper else r < c, 1.0, 0.0).astype(BF16)


def _tri_spec():
    return pl.BlockSpec((ATT_CHUNK, ATT_CHUNK), lambda h, i: (0, 0), pipeline_mode=pl.Buffered(1))


def _scan_chunk(v, tri):
    return jnp.dot(v.astype(BF16), tri, preferred_element_type=F32)


def _lane_bcast(col):
    return jnp.broadcast_to(col, (col.shape[0], LANES))


def _scan_slab(v, tri, carries, from_right):
    n_chunks = ATT_SLAB // ATT_CHUNK
    edge = 0 if from_right else ATT_CHUNK - 1
    parts, new_carries = [None] * (2 * n_chunks), []
    for head in range(2):
        run = carries[head]
        for c in (reversed(range(n_chunks)) if from_right else range(n_chunks)):
            lo_col = head * ATT_SLAB + c * ATT_CHUNK
            vc = v[:, lo_col:lo_col + ATT_CHUNK]
            sc = _scan_chunk(vc, tri)
            parts[head * n_chunks + c] = sc + jnp.concatenate([run] * (ATT_CHUNK // LANES), axis=1)
            run = run + _lane_bcast(sc[:, edge:edge + 1] + vc[:, edge:edge + 1])
        new_carries.append(run)
    return jnp.concatenate(parts, axis=1), new_carries


def _fold_heads(stacked, first):
    s = stacked.shape[0] // 2
    return jnp.where(first, stacked[:s], stacked[s:])


class _NoRider:
    operands, out_shapes, scratch = (), (), ()

    def split(self, refs, n_base_in, n_base_out):
        n_in, n_out, n_sem = len(self.operands), len(self.out_shapes), len(self.scratch)
        a = n_base_in + n_in
        b = a + n_base_out + n_out
        mine = (refs[n_base_in:a], refs[a + n_base_out:b], refs[b:b + n_sem])
        return refs[:n_base_in], refs[a:a + n_base_out], refs[b + n_sem:], mine

    def start(self, ins, outs, sems):
        pass

    def relay(self, ins, outs, sems):
        pass

    def finish(self, ins, outs, sems):
        pass

    def at_steps(self, refs, first_step, relay_step, last_step):
        if not self.operands:
            return (lambda: None), (lambda: None)

        def top():
            pl.when(first_step)(lambda: self.start(*refs))
            pl.when(relay_step)(lambda: self.relay(*refs))

        return top, lambda: pl.when(last_step)(lambda: self.finish(*refs))


def _attn_fwd(q_src, q_col, kv_src, k_col, v_col, n_pairs=4, rider=_NoRider()):
    t_total = q_src.shape[0]
    blk = ATT_BLOCK
    n_chains = ATT_FWD_CHAINS if t_total % (ATT_FWD_CHAINS * blk) == 0 else ATT_CHAINS
    n_steps = t_total // (n_chains * blk)
    assert t_total % ATT_SLAB == 0 and ATT_SLAB == ATT_BLOCK

    def body(*refs):
        (q_ref, k_ref, v_ref, suffix_ref), (o_ref,), _, riding = rider.split(refs, 4, 1)
        h, ii = pl.program_id(0), pl.program_id(1)
        top, bottom = rider.at_steps(riding, (h == 0) & (ii == 0), (h == n_pairs - 1) & (ii == 0),
                                     (h == n_pairs - 1) & (ii == n_steps - 1))
        top()
        first = _head_masks()
        suffix_tri = suffix_ref[...]
        blocks = [n_chains * ii + c for c in range(n_chains)]
        qs = [q_ref[c * blk:(c + 1) * blk, :] * ATT_SCALE for c in range(n_chains)]
        t_pos = [b * blk + lax.broadcasted_iota(jnp.int32, (blk, 1), 0) for b in blocks]

        def one(c, t, chain, on_diagonal):
            _, acc, right_a, right_b = chain
            k_start = pl.multiple_of((blocks[c] - t) * ATT_SLAB, ATT_SLAB)
            kd = _stack_heads(k_ref[pl.ds(k_start, ATT_SLAB), :], first)
            vd = _stack_heads(v_ref[pl.ds(k_start, ATT_SLAB), :], first)
            mask = _causal_mask(t_pos[c], k_start) if on_diagonal else None
            log_hit, log_fail = _slab_scores(qs[c], kd, mask)
            suffix, (right_a, right_b) = _scan_slab(log_fail, suffix_tri, (right_a, right_b), from_right=True)
            a = _weights(log_hit, suffix, mask).astype(BF16)
            acc = acc + jnp.dot(a, vd, preferred_element_type=F32)
            return jnp.max(jnp.maximum(right_a, right_b)), acc, right_a, right_b

        def step(state, on_diagonal):
            t, chains = state
            return t + 1, tuple(one(c, t, chains[c], on_diagonal) for c in range(n_chains))

        def more(state):
            t, chains = state
            return (t <= blocks[0]) & (functools.reduce(jnp.maximum, [ch[0] for ch in chains]) > ATT_EXIT_BELOW)

        zero = jnp.zeros((blk, LANES), F32)
        state = step((0, ((jnp.float32(0.0), zero, zero, zero),) * n_chains), on_diagonal=True)
        t, chains = lax.while_loop(more, functools.partial(step, on_diagonal=False), state)
        for c in range(n_chains):
            chain = chains[c]
            if c:
                _, chain = lax.while_loop(
                    lambda s, c=c: (s[0] <= blocks[c]) & (s[1][0] > ATT_EXIT_BELOW),
                    lambda s, c=c: (s[0] + 1, one(c, s[0], s[1], False)), (t, chain))
            o_ref[c * blk:(c + 1) * blk, :] = chain[1].astype(BF16)
        bottom()

    rows = n_chains * blk
    res = pl.pallas_call(
        body, name="attn_fwd", grid=(n_pairs, n_steps),
        in_specs=[pl.BlockSpec((rows, LANES), lambda h, i: (i, q_col + h)),
                  pl.BlockSpec((t_total, LANES), lambda h, i: (0, k_col + h)),
                  pl.BlockSpec((t_total, LANES), lambda h, i: (0, v_col + h)), _tri_spec()] + [ANY] * len(rider.operands),
        out_specs=[pl.BlockSpec((rows, LANES), lambda h, i: (i, h))] + [ANY] * len(rider.out_shapes),
        out_shape=[jax.ShapeDtypeStruct((t_total, n_pairs * LANES), BF16)] + list(rider.out_shapes),
        scratch_shapes=list(rider.scratch),
        compiler_params=_params("arbitrary", "arbitrary"),
    )(q_src, kv_src, kv_src, _tri(upper=True), *rider.operands)
    return res[0], res[1:]


def _attn_bwd(q_src, q_col, kv_src, k_col, v_col, dy, n_pairs=4, rider=_NoRider()):
    t_total = q_src.shape[0]
    blk = ATT_BLOCK
    n_steps = t_total // (ATT_CHAINS * blk)
    n_slabs = t_total // ATT_SLAB
    assert t_total % ATT_SLAB == 0 and ATT_SLAB == ATT_BLOCK

    def body(*refs):
        ins, (dq_ref, dk_ref, dv_ref), (g_s, dk_acc, dv_acc), riding = rider.split(refs, 6, 3)
        q_ref, dy_ref, k_ref, v_ref, suffix_ref, prefix_ref = ins
        h, ii = pl.program_id(0), pl.program_id(1)
        top, bottom = rider.at_steps(riding, (h == 0) & (ii == 0), (h == n_pairs - 1) & (ii == 0),
                                     (h == n_pairs - 1) & (ii == n_steps - 1))
        top()

        @pl.when(ii == 0)
        def _():
            dk_acc[...] = jnp.zeros_like(dk_acc)
            dv_acc[...] = jnp.zeros_like(dv_acc)

        first = _head_masks()
        suffix_tri = suffix_ref[...]
        prefix_tri = prefix_ref[...]
        blocks = [ATT_CHAINS * ii + c for c in range(ATT_CHAINS)]
        rows = [slice(c * blk, (c + 1) * blk) for c in range(ATT_CHAINS)]
        qs = [q_ref[r, :] * ATT_SCALE for r in rows]
        dys = [dy_ref[r, :] for r in rows]
        t_pos = [b * blk + lax.broadcasted_iota(jnp.int32, (blk, 1), 0) for b in blocks]

        def one1(c, t, chain, on_diagonal):
            _, right_a, right_b = chain
            slab = blocks[c] - t
            k_start = pl.multiple_of(slab * ATT_SLAB, ATT_SLAB)
            kd = _stack_heads(k_ref[pl.ds(k_start, ATT_SLAB), :], first)
            vd = _stack_heads(v_ref[pl.ds(k_start, ATT_SLAB), :], first)
            mask = _causal_mask(t_pos[c], k_start) if on_diagonal else None
            log_hit, log_fail = _slab_scores(qs[c], kd, mask)
            suffix, (right_a, right_b) = _scan_slab(log_fail, suffix_tri, (right_a, right_b), from_right=True)
            a = _weights(log_hit, suffix, mask)
            da = lax.dot_general(dys[c], vd, (((1,), (1,)), ((), ())), preferred_element_type=F32)
            g_s[c, slab] = (da * a).astype(BF16)
            dv_acc[pl.ds(k_start, ATT_SLAB), :] += _fold_heads(lax.dot_general(
                a.astype(BF16), dys[c], (((0,), (0,)), ((), ())), preferred_element_type=F32), first)
            return jnp.max(jnp.maximum(right_a, right_b)), right_a, right_b

        def step1(state, on_diagonal):
            t, chains = state
            return t + 1, tuple(one1(c, t, chains[c], on_diagonal) for c in range(ATT_CHAINS))

        def more(state):
            t, chains = state
            return (t <= blocks[0]) & (functools.reduce(jnp.maximum, [ch[0] for ch in chains]) > ATT_EXIT_BELOW)

        zero = jnp.zeros((blk, LANES), F32)
        state = step1((0, ((jnp.float32(0.0), zero, zero),) * ATT_CHAINS), on_diagonal=True)
        joint, chains = lax.while_loop(more, functools.partial(step1, on_diagonal=False), state)
        done = [joint]
        for c in range(1, ATT_CHAINS):
            done.append(lax.while_loop(
                lambda s, c=c: (s[0] <= blocks[c]) & (s[1][0] > ATT_EXIT_BELOW),
                lambda s, c=c: (s[0] + 1, one1(c, s[0], s[1], False)), (joint, chains[c]))[0])

        def one2(c, t, carry, on_diagonal):
            dq, left_a, left_b = carry
            slab = blocks[c] - t
            k_start = pl.multiple_of(slab * ATT_SLAB, ATT_SLAB)
            kd = _stack_heads(k_ref[pl.ds(k_start, ATT_SLAB), :], first)
            g = g_s[c, slab]
            z2 = lax.dot_general(qs[c], kd, (((1,), (1,)), ((), ())), preferred_element_type=F32) * LOG2_E
            sig = 1.0 / (1.0 + jnp.exp2(-z2))
            prefix, (left_a, left_b) = _scan_slab(g, prefix_tri, (left_a, left_b), from_right=False)
            dz = g * (1.0 - sig) - sig * prefix
            if on_diagonal:
                dz = jnp.where(_causal_mask(t_pos[c], k_start), dz, 0.0)
            dz = dz.astype(BF16)
            dq = dq + jnp.dot(dz, kd, preferred_element_type=F32)
            dk_acc[pl.ds(k_start, ATT_SLAB), :] += _fold_heads(lax.dot_general(
                dz, qs[c], (((0,), (0,)), ((), ())), preferred_element_type=F32), first)
            return dq, left_a, left_b

        carries = [(zero, zero, zero)]
        for c in range(1, ATT_CHAINS):
            carries.append(lax.fori_loop(
                0, done[c] - joint, lambda n, carry, c=c: one2(c, done[c] - 1 - n, carry, False), (zero, zero, zero)))
        carries = lax.fori_loop(
            0, joint - 1,
            lambda n, cs: tuple(one2(c, joint - 1 - n, cs[c], False) for c in range(ATT_CHAINS)), tuple(carries))
        for c in range(ATT_CHAINS):
            dq_ref[rows[c], :] = (one2(c, 0, carries[c], True)[0] * ATT_SCALE).astype(BF16)

        @pl.when(ii == n_steps - 1)
        def _():
            dk_ref[...] = dk_acc[...].astype(BF16)
            dv_ref[...] = dv_acc[...].astype(BF16)

        bottom()

    out = jax.ShapeDtypeStruct((t_total, n_pairs * LANES), BF16)
    n_rows = ATT_CHAINS * blk
    whole = dict(pipeline_mode=pl.Buffered(1))
    res = pl.pallas_call(
        body, name="attn_bwd", grid=(n_pairs, n_steps),
        in_specs=[pl.BlockSpec((n_rows, LANES), lambda h, i: (i, q_col + h)),
                  pl.BlockSpec((n_rows, LANES), lambda h, i: (i, h)),
                  pl.BlockSpec((t_total, LANES), lambda h, i: (0, k_col + h), **whole),
                  pl.BlockSpec((t_total, LANES), lambda h, i: (0, v_col + h), **whole), _tri_spec(), _tri_spec()]
        + [ANY] * len(rider.operands),
        out_specs=[pl.BlockSpec((n_rows, LANES), lambda h, i: (i, h)),
                   pl.BlockSpec((t_total, LANES), lambda h, i: (0, h)),
                   pl.BlockSpec((t_total, LANES), lambda h, i: (0, h))] + [ANY] * len(rider.out_shapes),
        out_shape=[out, out, out] + list(rider.out_shapes),
        scratch_shapes=list(rider.scratch) + [pltpu.VMEM((ATT_CHAINS, n_slabs, blk, 2 * ATT_SLAB), BF16),
                                              pltpu.VMEM((t_total, LANES), F32), pltpu.VMEM((t_total, LANES), F32)],
        compiler_params=_params("arbitrary", "arbitrary"),
    )(q_src, dy, kv_src, kv_src, _tri(upper=True), _tri(upper=False), *rider.operands)
    return res[:3], res[3:]


def _adamw(name, w, g, m, v):
    def fn(wv, gv, mv, vv):
        mn = ADAM_B1 * mv + (1.0 - ADAM_B1) * gv
        vn = ADAM_B2 * vv + (1.0 - ADAM_B2) * (gv * gv)
        m_hat = mn / (1.0 - ADAM_B1 ** ADAM_STEP)
        v_hat = vn / (1.0 - ADAM_B2 ** ADAM_STEP)
        return -ADAM_LR * (m_hat / (jnp.sqrt(v_hat) + ADAM_EPS) + ADAM_WD * wv), mn, vn

    rows = w.shape[0]
    tr = _row_tile(rows)
    shp = jax.ShapeDtypeStruct(w.shape, F32)
    if rows == 1:
        def body(w_ref, g_ref, m_ref, v_ref, d_ref, mo_ref, vo_ref):
            d, mn, vn = fn(w_ref[...], g_ref[...], m_ref[...], v_ref[...])
            d_ref[...], mo_ref[...], vo_ref[...] = d, mn, vn

        return pl.pallas_call(body, name=name, out_shape=[shp, shp, shp])(w, g, m, v)
    return _rows(name, fn, [w, g, m, v], [shp, shp, shp], tr=tr)


def _place():
    return lax.axis_index("x"), lax.axis_index("y"), lax.axis_index("c")


def _other_chips(x, y):
    return [(1 - x, y), (x, 1 - y), (1 - x, 1 - y)]


ANY = pl.BlockSpec(memory_space=pl.ANY)


def _remote(src, dst, send_sem, recv_sem, to):
    return pltpu.make_async_remote_copy(src_ref=src, dst_ref=dst, send_sem=send_sem, recv_sem=recv_sem,
                                        device_id=to, device_id_type=MESH)


class _WeightGather(_NoRider):
    def __init__(self, shards):
        n_w = len(shards)
        self.operands = list(shards)
        self.out_shapes = [jax.ShapeDtypeStruct((N_CHIPS,) + s.shape, s.dtype) for s in shards]
        self.scratch = [pltpu.SemaphoreType.DMA((3, n_w))] * 4 + [pltpu.SemaphoreType.DMA((n_w,))] * 2

    def _copies(self, ins, outs, sems):
        send_sems, recv_sems, relay_send, relay_recv, own_send, own_recv = sems
        x, y, c = _place()
        my_chip, sibling = 2 * x + y, (x, y, 1 - c)
        n_w = len(ins)

        def half(w, chip, core):
            h = self.operands[w].shape[0] // 2
            return outs[w].at[chip, pl.ds(core * h, h)]

        own = [_remote(ins[w], outs[w].at[my_chip], own_send.at[w], own_recv.at[w], sibling) for w in range(n_w)]
        sends, landed, relays, relayed = [], [], [], []
        for p, (ox, oy) in enumerate(_other_chips(x, y)):
            for w in range(n_w):
                h = self.operands[w].shape[0] // 2
                sends.append(_remote(ins[w].at[pl.ds(c * h, h)], half(w, my_chip, c), send_sems.at[p, w],
                                     recv_sems.at[p, w], (ox, oy, c)))
                here = half(w, 2 * ox + oy, c)
                landed.append(_remote(here, here, send_sems.at[p, w], recv_sems.at[p, w], (ox, oy, c)))
                relays.append(_remote(here, here, relay_send.at[p, w], relay_recv.at[p, w], sibling))
                there = half(w, 2 * ox + oy, 1 - c)
                relayed.append(_remote(there, there, relay_send.at[p, w], relay_recv.at[p, w], sibling))
        return own, sends, landed, relays, relayed

    def start(self, ins, outs, sems):
        own, sends, _, _, _ = self._copies(ins, outs, sems)
        for cp in own + sends:
            cp.start()

    def relay(self, ins, outs, sems):
        _, _, landed, relays, _ = self._copies(ins, outs, sems)
        for arrival, cp in zip(landed, relays):
            arrival.wait_recv()
            cp.start()

    def finish(self, ins, outs, sems):
        own, sends, _, relays, relayed = self._copies(ins, outs, sems)
        for arrival in relayed:
            arrival.wait_recv()
        for cp in sends + relays:
            cp.wait_send()
        for cp in own:
            cp.wait()


class _ChipExchange(_NoRider):
    def __init__(self, pair_sums):
        n_w = len(pair_sums)
        self.operands = list(pair_sums)
        self.out_shapes = [jax.ShapeDtypeStruct((3,) + s.shape[1:], s.dtype) for s in pair_sums]
        self.scratch = [pltpu.SemaphoreType.DMA((3, n_w))] * 2

    def _copies(self, ins, outs, sems):
        send_sems, recv_sems = sems
        x, y, c = _place()
        return [_remote(ins[w].at[2 * ox + oy], outs[w].at[p], send_sems.at[p, w], recv_sems.at[p, w], (ox, oy, c))
                for p, (ox, oy) in enumerate(_other_chips(x, y)) for w in range(len(ins))]

    def start(self, ins, outs, sems):
        for cp in self._copies(ins, outs, sems):
            cp.start()

    def finish(self, ins, outs, sems):
        for cp in self._copies(ins, outs, sems):
            cp.wait()


class _PairExchange(_NoRider):
    def __init__(self, grads):
        n_w = len(grads)
        self.operands = list(grads)
        self.out_shapes = [jax.ShapeDtypeStruct(g.shape[:-2] + (g.shape[-2] // 2, g.shape[-1]), F32) for g in grads]
        self.scratch = [pltpu.SemaphoreType.DMA((n_w,))] * 2

    def _copies(self, ins, theirs, sems):
        send_sems, recv_sems = sems
        x, y, c = _place()
        sends = []
        for w, g in enumerate(self.operands):
            rows = pl.ds((1 - c) * (g.shape[-2] // 2), g.shape[-2] // 2)
            src = ins[w].at[:, rows, :] if g.ndim == 3 else ins[w].at[rows, :]
            sends.append(_remote(src, theirs[w], send_sems.at[w], recv_sems.at[w], (x, y, 1 - c)))
        return sends

    def start(self, ins, outs, sems):
        for cp in self._copies(ins, outs, sems):
            cp.start()

    def finish(self, ins, outs, sems):
        for cp in self._copies(ins, outs, sems):
            cp.wait()


def _exchange_pair_sum(name, place, grad):
    n, r, c = grad.shape
    half = r // 2

    def body(place_ref, g_all, g_ref, o_ref, theirs, send_sems, recv_sems):
        j = pl.program_id(0)
        x, y, core = _place()

        def copy(k):
            return _remote(g_all.at[k, pl.ds((1 - core) * half, half)], theirs.at[k], send_sems.at[k],
                           recv_sems.at[k], (x, y, 1 - core))

        @pl.when(j == 0)
        def _():
            for k in range(n):
                copy(k).start()

        copy(j).wait_recv()
        o_ref[0] = (g_ref[0] + theirs[j]).astype(BF16)

        @pl.when(j == n - 1)
        def _():
            for k in range(n):
                copy(k).wait_send()

    return pl.pallas_call(
        body, name=name, out_shape=jax.ShapeDtypeStruct((n, half, c), BF16),
        grid_spec=pltpu.PrefetchScalarGridSpec(
            num_scalar_prefetch=1, grid=(n,),
            in_specs=[ANY, pl.BlockSpec((1, half, c), lambda j, pr: (j, pr[0], 0))],
            out_specs=pl.BlockSpec((1, half, c), lambda j, pr: (j, 0, 0)),
            scratch_shapes=[pltpu.VMEM((n, half, c), F32), pltpu.SemaphoreType.DMA((n,)),
                            pltpu.SemaphoreType.DMA((n,))]),
        compiler_params=_params("arbitrary"),
    )(place, grad, grad)


class _NoExchanges:
    pair_sums, landed = {}, {}

    def gather(self, names):
        return _NoRider()

    def pair(self, names, grads):
        return _NoRider()

    def paired(self, names, grads, theirs):
        pass

    def pair_now(self, names, grads):
        pass

    def chip(self, names):
        return _NoRider()


class _StepExchanges(_NoExchanges):
    def __init__(self, shards_bf16, place):
        self.shards, self.place = shards_bf16, place
        self.pair_sums, self.landed = {}, {}

    def gather(self, names):
        return _WeightGather([self.shards[n] for n in names])

    def pair(self, names, grads):
        return _PairExchange([grads[n] for n in names])

    def paired(self, names, grads, theirs):
        for n, other in zip(names, theirs):
            self.pair_sums[n] = _pair_sum(f"pair_sum_{n}", self.place, grads[n], other)

    def pair_now(self, names, grads):
        for n in names:
            self.pair_sums[n] = _exchange_pair_sum(f"pair_sum_{n}", self.place, grads[n])

    def chip(self, names):
        return _ChipExchange([self.pair_sums[n] for n in names])


SUM_ROWS = 32


def _finish_gradients(place, pair_sums, landed, vec):
    n_w = len(pair_sums)
    rows = vec.shape[0]
    halves = [s.shape[1:] for s in pair_sums]
    late = [w for w in range(n_w) if landed[w] is None]

    def body(place_ref, *refs):
        sums, given, v_ref = refs[:n_w], refs[n_w:2 * n_w], refs[2 * n_w]
        outs, o_ref = refs[2 * n_w + 1:3 * n_w + 1], refs[3 * n_w + 1]
        stage, arrived = refs[3 * n_w + 2:4 * n_w + 2], refs[4 * n_w + 2:4 * n_w + 2 + len(late)]
        (kept, half_send, half_recv, chip_send, chip_recv, slots, core_sums, vec_send, vec_recv, sum_send,
         sum_recv) = refs[4 * n_w + 2 + len(late):]
        x, y, c = _place()
        my_chip, sibling = 2 * x + y, (x, y, 1 - c)
        slots[my_chip] = v_ref[...]
        spread = []
        for p, (ox, oy) in enumerate(_other_chips(x, y)):
            here = slots.at[2 * ox + oy]
            spread.append((_remote(v_ref, slots.at[my_chip], vec_send.at[p], vec_recv.at[p], (ox, oy, c)),
                           _remote(here, here, vec_send.at[p], vec_recv.at[p], (ox, oy, c))))
        for send, _ in spread:
            send.start()
        lands, crossing = list(given), {}
        for k, w in enumerate(late):
            lands[w] = arrived[k]
            crossing[w] = [_remote(given[w].at[2 * ox + oy], arrived[k].at[p], chip_send.at[p, k], chip_recv.at[p, k],
                                   (ox, oy, c)) for p, (ox, oy) in enumerate(_other_chips(x, y))]
            for cp in crossing[w]:
                cp.start()
        copies = []
        for w in [w for w in range(n_w) if w not in late] + late:
            h = halves[w][0]
            step = SUM_ROWS if h % SUM_ROWS == 0 else h
            for cp in crossing.get(w, ()):
                cp.wait()

            def sum_rows(i, _, w=w, step=step):
                r = pl.ds(pl.multiple_of(i * step, step), step)
                stage[w][r, :] = functools.reduce(lambda total, p: total + lands[w][p, r, :].astype(F32), range(3),
                                                  sums[w][0, r, :].astype(F32))
                return 0

            lax.fori_loop(0, h // step, sum_rows, 0)
            mine, theirs = outs[w].at[pl.ds(c * h, h)], outs[w].at[pl.ds((1 - c) * h, h)]
            copies.append((pltpu.make_async_copy(stage[w], mine, kept.at[w]),
                           _remote(stage[w], mine, half_send.at[w], half_recv.at[w], sibling),
                           _remote(theirs, theirs, half_send.at[w], half_recv.at[w], sibling)))
            copies[-1][0].start()
            copies[-1][1].start()
        for send, arrival in spread:
            arrival.wait_recv()
            send.wait_send()
        core_sums[c] = functools.reduce(lambda total, chip: total + slots[chip], range(1, N_CHIPS), slots[0])
        mine, theirs = core_sums.at[c], core_sums.at[1 - c]
        to_sibling = _remote(mine, mine, sum_send.at[0], sum_recv.at[0], sibling)
        to_sibling.start()
        _remote(theirs, theirs, sum_send.at[0], sum_recv.at[0], sibling).wait_recv()
        to_sibling.wait_send()
        o_ref[...] = core_sums[0] + core_sums[1]
        for keep, send, arrival in copies:
            keep.wait()
            arrival.wait_recv()
            send.wait_send()

    once = dict(pipeline_mode=pl.Buffered(1))
    vm = pl.BlockSpec((rows, LANES), lambda i, pr: (0, 0))
    res = pl.pallas_call(
        body, name="finish_gradients",
        grid_spec=pltpu.PrefetchScalarGridSpec(
            num_scalar_prefetch=1, grid=(1,),
            in_specs=[pl.BlockSpec((1,) + hc, lambda i, pr: (pr[1], 0, 0), **once) for hc in halves]
            + [ANY if w in late else pl.BlockSpec((3,) + hc, lambda i, pr: (0, 0, 0), **once)
               for w, hc in enumerate(halves)] + [vm],
            out_specs=[ANY] * n_w + [vm],
            scratch_shapes=[pltpu.VMEM(hc, F32) for hc in halves]
            + [pltpu.VMEM((3,) + halves[w], BF16) for w in late]
            + [pltpu.SemaphoreType.DMA((n_w,))] * 3 + [pltpu.SemaphoreType.DMA((3, max(len(late), 1)))] * 2
            + [pltpu.VMEM((N_CHIPS, rows, LANES), F32), pltpu.VMEM((2, rows, LANES), F32),
               pltpu.SemaphoreType.DMA((N_CHIPS - 1,)), pltpu.SemaphoreType.DMA((N_CHIPS - 1,)),
               pltpu.SemaphoreType.DMA((1,)), pltpu.SemaphoreType.DMA((1,))]),
        out_shape=[_sds((2 * h, cols), F32) for h, cols in halves] + [_sds(vec.shape, F32)],
        compiler_params=_params("arbitrary"),
    )(place, *pair_sums, *[pair_sums[w] if w in late else landed[w] for w in range(n_w)], vec)
    return res[:n_w], res[n_w]


def _row_tile(rows):
    fits = [tr for tr in range(16, min(rows, 512) + 1, 16) if rows % tr == 0]
    return max(fits) if fits else rows


def _pair_sum(name, place, grad, theirs):
    if grad.ndim == 2:
        return _pair_sum_joined(name, place, grad, theirs)
    n, r, c = grad.shape
    half = r // 2
    tr = _row_tile(half)
    nb = half // tr

    def body(place_ref, g_ref, t_ref, o_ref):
        o_ref[...] = (g_ref[...] + t_ref[...]).astype(BF16)

    return pl.pallas_call(
        body, name=name, out_shape=jax.ShapeDtypeStruct((n, half, c), BF16),
        grid_spec=pltpu.PrefetchScalarGridSpec(
            num_scalar_prefetch=1, grid=(n, nb),
            in_specs=[pl.BlockSpec((1, tr, c), lambda j, i, pr: (j, pr[0] * nb + i, 0)),
                      pl.BlockSpec((1, tr, c), lambda j, i, pr: (j, i, 0))],
            out_specs=pl.BlockSpec((1, tr, c), lambda j, i, pr: (j, i, 0))),
        compiler_params=_params("parallel", "parallel"),
    )(place, grad, theirs)


def _pair_sum_joined(name, place, grad, theirs):
    r, wide = grad.shape
    half, c = r // 2, wide // N_CHIPS
    tr = _row_tile(half)
    nb = half // tr

    def body(place_ref, g_ref, t_ref, o_ref):
        for j in range(N_CHIPS):
            cols = slice(j * c, (j + 1) * c)
            o_ref[j] = (g_ref[:, cols] + t_ref[:, cols]).astype(BF16)

    return pl.pallas_call(
        body, name=name, out_shape=jax.ShapeDtypeStruct((N_CHIPS, half, c), BF16),
        grid_spec=pltpu.PrefetchScalarGridSpec(
            num_scalar_prefetch=1, grid=(nb,),
            in_specs=[pl.BlockSpec((tr, wide), lambda i, pr: (pr[0] * nb + i, 0)),
                      pl.BlockSpec((tr, wide), lambda i, pr: (i, 0))],
            out_specs=pl.BlockSpec((N_CHIPS, tr, c), lambda i, pr: (0, i, 0))),
        compiler_params=_params("parallel"),
    )(place, grad, theirs)


MIXER = ("w_branch_a", "w_branch_b", "w_out")
FFN_PLE = ("w_ffn_gate", "w_ffn_up", "w_ffn_down", "w_ple_gate", "w_ple_proj")
LATE = MIXER + FFN_PLE
BIG = ("w_in",) + LATE
HELD_TRANSPOSED = ("w_ffn_gate", "w_ffn_up")
SMALL = ("norm_mix", "w_pool", "pool_scale", "norm_ffn", "norm_ple", "norm_final")


def _join_columns(w4):
    return jnp.concatenate([w4[j] for j in range(N_CHIPS)], axis=1)


def _sds(shape, dtype):
    return jax.ShapeDtypeStruct(shape, dtype)


def _local_step(x, p, target, wf, small, ex=None):
    t, d = x.shape
    w_pool_b = small["w_pool"].astype(BF16)
    dp = w_pool_b.shape[0] * w_pool_b.shape[1]

    ex = ex or _NoExchanges()
    h1, first = _norm_fwd("norm_mix", x, small["norm_mix"], rider=ex.gather(("w_in",)))
    wf = {**wf, **dict(zip(("w_in",), first))}
    w_in = wf["w_in"]
    u, q, kv, ga, gb = _mm(
        "proj", [h1], [w_in[j] for j in range(N_CHIPS)], "nn",
        [_sds((t, dp), F32), _sds((t, dp), BF16), _sds((t, d), BF16), _sds((t, d), BF16), _sds((t, d), BF16)],
        separate=True, epilogue=lambda uq, kv_, ga_, gb_: (uq[:, :dp], uq[:, dp:], kv_, ga_, gb_), tm=512)
    pooled, ya = _pool_fwd(u, w_pool_b, small["pool_scale"])
    n_pairs = dp // LANES
    yb, late = _attn_fwd(q, 0, kv, 0, n_pairs, n_pairs, rider=ex.gather(LATE))
    wf = {**wf, **dict(zip(LATE, late))}
    w_down = wf["w_ffn_down"].reshape(-1, d)
    dff = w_down.shape[0]
    w_gate_t, w_up_t = wf["w_ffn_gate"].reshape(dff, d), wf["w_ffn_up"].reshape(dff, d)
    w_a, w_b, w_pp = _join_columns(wf["w_branch_a"]), _join_columns(wf["w_branch_b"]), _join_columns(wf["w_ple_proj"])
    w_out = wf["w_out"].reshape(d, d)
    w_pg = wf["w_ple_gate"].reshape(d, d)
    def residual_norm(branch, xv, g, w):
        xn = xv + jnp.dot(branch.astype(BF16), w, preferred_element_type=F32)
        return xn, xn * lax.rsqrt(jnp.mean(xn * xn, axis=-1, keepdims=True) + RMS_EPS) * g

    def mixer_tail(tav, tbv, gav, gbv, xv, g, w):
        merged = _sigmoid(gav) * tav + _sigmoid(gbv) * tbv
        return (tav, tbv, merged) + residual_norm(merged, xv, g, w)

    def ffn_tail(gv, uv, xv, g, w):
        act = gv * _sigmoid(gv) * uv
        return (gv, uv, act) + residual_norm(act, xv, g, w)

    stream = [_sds((t, d), F32), _sds((t, d), BF16)]
    ta, tb, merged, x1, h2 = _mm(
        "mixer_out", [ya, yb], [w_a, w_b], "nn", [_sds((t, d), BF16)] * 3 + stream,
        extras=[ga, gb, x, small["norm_ffn"]], wholes=[w_out], separate=True, epilogue=mixer_tail, tm=512)
    gate, up, act, x2, h3 = _mm(
        "ffn", [h2], [w_gate_t, w_up_t], "nt", [_sds((t, dff), BF16)] * 3 + stream,
        extras=[x1, small["norm_ple"]], wholes=[w_down], separate=True, epilogue=ffn_tail, tm=256)
    dx2, dx2_b, d_pp, d_gp, d_norm_final, loss_row, d_norm_ple = _mm(
        "ple_loss", [h3, p], [w_pg, w_pp], "nn", stream + [_sds((t, d), BF16)] * 2,
        extras=[x2, target, small["norm_final"].reshape(1, d), small["norm_ple"]], wholes=[w_pg], separate=True,
        epilogue=_ple_and_loss, sum_shapes=[_sds((1, d), F32)] * 3, tm=512)

    def through_norm(dh, xv, g, dres):
        dx, d_gain = _rms_norm_bwd(dh, xv, g)
        return dx + dres, dx + dres, d_gain

    gain_sum = [_sds((1, d), F32)]
    g_w_pp, g_w_pg = _mm_tn("g_ple", [p, h3], [d_pp, d_gp])

    def ffn_bwd(d_act, gv, uv, xv, g, dres, wg_t, wu_t):
        s = _sigmoid(gv)
        d_gate, d_up = d_act * uv * (s * (1.0 + gv * (1.0 - s))), d_act * (gv * s)
        dh2 = (jnp.dot(d_gate.astype(BF16), wg_t, preferred_element_type=F32)
               + jnp.dot(d_up.astype(BF16), wu_t, preferred_element_type=F32))
        return (d_gate, d_up) + through_norm(dh2, xv, g, dres)

    d_gate, d_up, dx1, dx1_b, d_norm_ffn = _mm(
        "ffn_bwd", [dx2_b], [w_down], "nt", [_sds((t, dff), BF16)] * 2 + stream,
        extras=[gate, up, x1, small["norm_ffn"], dx2], wholes=[w_gate_t, w_up_t], epilogue=ffn_bwd,
        sum_shapes=gain_sum, tm=256)
    g_w_down, = _mm_tn("g_ffn_down", [act], [dx2_b], tmm=512)
    g_w_gate_t, g_w_up_t = _mm_tn("g_ffn_gate_up", [d_gate, d_up], [h2], k_blocks=2)

    def merge_bwd(acc, tav, tbv, gav, gbv):
        sa, sb = _sigmoid(gav), _sigmoid(gbv)
        return acc * sa, acc * sb, acc * tav * sa * (1.0 - sa), acc * tbv * sb * (1.0 - sb)

    big = {
        "w_ffn_gate": g_w_gate_t.reshape(wf["w_ffn_gate"].shape), "w_ffn_up": g_w_up_t.reshape(wf["w_ffn_up"].shape),
        "w_ffn_down": g_w_down.reshape(wf["w_ffn_down"].shape),
        "w_ple_gate": g_w_pg.reshape(wf["w_ple_gate"].shape), "w_ple_proj": g_w_pp,
    }
    (d_ta, d_tb, d_ga, d_gb), theirs = _mm(
        "d_merged", [dx1_b], [w_out], "nt", [_sds((t, d), BF16)] * 4, extras=[ta, tb, ga, gb], epilogue=merge_bwd,
        tm=512, rider=ex.pair(FFN_PLE, big))
    ex.paired(FFN_PLE, big, theirs)
    g_w_out, big["w_branch_a"], big["w_branch_b"] = _mm_tn("g_mixer", [merged, ya, yb], [dx1_b, d_ta, d_tb])
    big["w_out"] = g_w_out.reshape(wf["w_out"].shape)
    (d_ya, d_yb), theirs = _mm(
        "d_branches", [d_ta, d_tb], [w_a, w_b], "nt", [_sds((t, dp), F32), _sds((t, dp), BF16)], separate=True,
        rider=ex.pair(MIXER, big))
    ex.paired(MIXER, big, theirs)
    d_u, g_w_pool, d_pool_scale = _pool_bwd(d_ya, pooled, w_pool_b, small["pool_scale"])
    (d_q, d_k, d_v), landed = _attn_bwd(q, 0, kv, 0, n_pairs, d_yb, n_pairs, rider=ex.chip(LATE))
    ex.landed.update(zip(LATE, landed))
    d_proj = [(d_u, d_q), (d_k, d_v), d_ga, d_gb]
    big["w_in"], = _mm_tn("g_w_in", [h1], d_proj, tmm=512, stacked=True)
    ex.pair_now(("w_in",), big)
    grad_x, d_norm_mix = _mm(
        "d_h1", d_proj, [w_in[j] for j in range(N_CHIPS)], "nt", [_sds((t, d), F32)],
        extras=[x, small["norm_mix"], dx1], epilogue=lambda dh, xv, g, dres: through_norm(dh, xv, g, dres)[1:],
        sum_shapes=gain_sum, tm=512)
    small_g = {"norm_mix": d_norm_mix, "w_pool": g_w_pool, "pool_scale": d_pool_scale, "norm_ffn": d_norm_ffn,
               "norm_ple": d_norm_ple, "norm_final": d_norm_final}
    return grad_x, big, small_g, loss_row


def _pack_small(small_g, loss_row):
    parts, layout = [], []
    for name in SMALL + ("loss",):
        v = (loss_row if name == "loss" else small_g[name]).reshape(-1, LANES)
        pad = (-v.shape[0]) % 8
        if pad:
            v = jnp.concatenate([v, jnp.zeros((pad, LANES), F32)], axis=0)
        layout.append((name, sum(q.shape[0] for q in parts), v.shape[0]))
        parts.append(v)
    return jnp.concatenate(parts, axis=0), layout


def kernel(x, p, norm_mix, w_in, w_pool, pool_scale, w_branch_a, w_branch_b, w_out, norm_ffn, w_ffn_gate, w_ffn_up, w_ffn_down, norm_ple, w_ple_gate, w_ple_proj, norm_final, loss_target, m_norm_mix, m_w_in, m_w_pool, m_pool_scale, m_w_branch_a, m_w_branch_b, m_w_out, m_norm_ffn, m_w_ffn_gate, m_w_ffn_up, m_w_ffn_down, m_norm_ple, m_w_ple_gate, m_w_ple_proj, m_norm_final, v_norm_mix, v_w_in, v_w_pool, v_pool_scale, v_w_branch_a, v_w_branch_b, v_w_out, v_norm_ffn, v_w_ffn_gate, v_w_ffn_up, v_w_ffn_down, v_norm_ple, v_w_ple_gate, v_w_ple_proj, v_norm_final):
    given = dict(locals())
    order = ("norm_mix", "w_in", "w_pool", "pool_scale", "w_branch_a", "w_branch_b", "w_out", "norm_ffn", "w_ffn_gate",
             "w_ffn_up", "w_ffn_down", "norm_ple", "w_ple_gate", "w_ple_proj", "norm_final")
    t, d = x.shape[1], x.shape[2]
    def local(a, n):
        return jnp.swapaxes(a[0], 0, 1) if n in HELD_TRANSPOSED else a[0]

    def back(a, n):
        return (jnp.swapaxes(a, 0, 1) if n in HELD_TRANSPOSED else a)[None]

    shard = {n: local(given[n], n) for n in BIG}
    small = {"norm_mix": norm_mix, "w_pool": w_pool[0], "pool_scale": pool_scale, "norm_ffn": norm_ffn,
             "norm_ple": norm_ple, "norm_final": norm_final}

    place = jnp.stack([lax.axis_index("c"), 2 * lax.axis_index("x") + lax.axis_index("y")]).astype(jnp.int32)
    ex = _StepExchanges({n: shard[n].astype(BF16) for n in BIG}, place)
    grad_x, _, small_g, loss_row = _local_step(
        x.reshape(t, d), p.reshape(t, p.shape[-1]), loss_target.reshape(t, d), {}, small, ex)
    packed, layout = _pack_small(small_g, loss_row)
    filled, reduced = _finish_gradients(place, [ex.pair_sums[n] for n in BIG], [ex.landed.get(n) for n in BIG], packed)
    grads = dict(zip(BIG, filled))
    for name, start, rows in layout:
        if name == "loss":
            loss = jnp.sum(reduced[start:start + rows])
        else:
            n_el = small[name].size
            grads[name] = reduced[start:start + rows].reshape(-1)[:n_el]

    deltas, new_m, new_v = {}, {}, {}
    for n in order:
        if n in BIG:
            w, m, v = shard[n], local(given["m_" + n], n), local(given["v_" + n], n)
            dl, mn, vn = _adamw(f"adamw_{n}", w, grads[n], m, v)
            grads[n], deltas[n], new_m[n], new_v[n] = [back(a, n) for a in (grads[n], dl, mn, vn)]
        else:
            w, full = small[n], given[n].shape
            shape2 = (1, w.shape[0]) if w.ndim == 1 else (w.shape if w.ndim == 2 else (w.shape[0] * w.shape[1], w.shape[2]))
            dl, mn, vn = _adamw(f"adamw_{n}", w.reshape(shape2), grads[n].reshape(shape2),
                                given["m_" + n].reshape(shape2), given["v_" + n].reshape(shape2))
            grads[n], deltas[n], new_m[n], new_v[n] = [a.reshape(full) for a in (grads[n], dl, mn, vn)]

    return (loss, grad_x.reshape(x.shape), *[grads[n] for n in order], *[deltas[n] for n in order],
            *[new_m[n] for n in order], *[new_v[n] for n in order])
```

```python
import functools
import math

import jax
import jax.numpy as jnp
from jax import lax
from jax.experimental import pallas as pl
from jax.experimental.pallas import tpu as pltpu

F32 = jnp.float32
BF16 = jnp.bfloat16
MESH = pl.DeviceIdType.MESH

RMS_EPS = 1e-6
POOL_WINDOWS = (2, 4, 8, 16)
POOL_HALO = 16
HEAD_DIM = 64
LANES = 128
ATT_BLOCK = 256
ATT_CHAINS = 2
ATT_FWD_CHAINS = 4
ATT_CHUNK = 256
ATT_SLAB = 256
ATT_SCALE = 1.0 / math.sqrt(HEAD_DIM)
LOG2_E = 1.4426950408889634
ATT_EXIT_BELOW = -150.5
ADAM_LR, ADAM_B1, ADAM_B2, ADAM_EPS, ADAM_WD, ADAM_STEP = 0.001, 0.9, 0.999, 1e-08, 0.01, 10
V7X_VMEM_LIMIT_BYTES = 56 * 1024 * 1024
N_CHIPS = 4
N_DEV = 8


def _params(*semantics):
    return pltpu.CompilerParams(dimension_semantics=semantics, vmem_limit_bytes=V7X_VMEM_LIMIT_BYTES)


def _sigmoid(z):
    return 0.5 * jnp.tanh(0.5 * z) + 0.5


def _tiled_spec(shape, tm, tn, n_total, at):
    rows, width = shape
    if rows == 1:
        if width == n_total:
            return pl.BlockSpec((1, tn), at(lambda i, j: (0, j)))
        return pl.BlockSpec((1, width), at(lambda i, j: (0, 0)))
    if width == n_total:
        return pl.BlockSpec((tm, tn), at(lambda i, j: (i, j)))
    assert tn == n_total, "an operand narrower than the output needs whole output rows per tile"
    return pl.BlockSpec((tm, width), at(lambda i, j: (i, 0)))


def _column_pieces(operands):
    pieces = [tuple(a) if isinstance(a, (tuple, list)) else (a,) for a in operands]
    return [p for ps in pieces for p in ps], [len(ps) for ps in pieces]


def _load_bf16(refs, counts):
    tiles, k = [], 0
    for n in counts:
        parts = [r[...] for r in refs[k:k + n]]
        parts = [t if t.dtype == BF16 else t.astype(BF16) for t in parts]
        tiles.append(parts[0] if n == 1 else jnp.concatenate(parts, axis=1))
        k += n
    return tiles


def _mm(name, a_list, b_list, mode, out_shapes, epilogue=None, extras=(), tm=1024, tn=None, separate=False,
        sum_shapes=(), rider=None, wholes=()):
    flat_a, counts = _column_pieces(a_list)
    m_total = flat_a[0].shape[0]
    n_total = b_list[0].shape[1] if mode == "nn" else b_list[0].shape[0]
    tn = n_total if tn is None else tn
    tm = min(tm, m_total)
    assert m_total % tm == 0 and n_total % tn == 0 and (not sum_shapes or tn == n_total)
    n_a, n_b, n_extra, n_out = len(counts), len(b_list), len(extras), len(out_shapes)
    assert n_a in (1, n_b)
    dims = (((1,), (0,)), ((), ())) if mode == "nn" else (((1,), (1,)), ((), ()))
    with_rider = rider is not None
    rider = rider or _NoRider()
    grid = (n_total // tn, m_total // tm)

    def at(index):
        return lambda j, i: index(i, j)

    def body(*refs):
        ins, o_refs, _, riding = rider.split(refs, len(flat_a) + n_b + n_extra + len(wholes), n_out + len(sum_shapes))
        a_refs, b_refs = ins[:len(flat_a)], ins[len(flat_a):len(flat_a) + n_b]
        e_refs, w_refs = ins[len(flat_a) + n_b:len(flat_a) + n_b + n_extra], ins[len(flat_a) + n_b + n_extra:]
        at_first = (pl.program_id(0) == 0) & (pl.program_id(1) == 0)
        at_last = (pl.program_id(0) == grid[0] - 1) & (pl.program_id(1) == grid[1] - 1)
        top, bottom = rider.at_steps(riding, at_first, at_first, at_last)
        top()
        lefts = _load_bf16(a_refs, counts)
        products = [lax.dot_general(lefts[s % n_a], b_refs[s][...], dims, preferred_element_type=F32)
                    for s in range(n_b)]
        if not separate:
            products = [functools.reduce(lambda p, r: p + r, products)]
        extra_tiles = [e[...].astype(F32) for e in e_refs]
        outs = products if epilogue is None else epilogue(*products, *extra_tiles, *[w[...] for w in w_refs])
        for o_ref, o in zip(o_refs[:n_out], outs[:n_out]):
            o_ref[...] = o.astype(o_ref.dtype)
        if sum_shapes:
            @pl.when(pl.program_id(1) == 0)
            def _():
                for s_ref in o_refs[n_out:]:
                    s_ref[...] = jnp.zeros_like(s_ref)

            for s_ref, s in zip(o_refs[n_out:], outs[n_out:]):
                s_ref[...] += s
        bottom()

    once = dict(pipeline_mode=pl.Buffered(1)) if tn == n_total else {}
    in_specs = [pl.BlockSpec((tm, a.shape[1]), at(lambda i, j: (i, 0))) for a in flat_a]
    if mode == "nn":
        in_specs += [pl.BlockSpec((b.shape[0], tn), at(lambda i, j: (0, j)), **once) for b in b_list]
    else:
        in_specs += [pl.BlockSpec((tn, b.shape[1]), at(lambda i, j: (j, 0)), **once) for b in b_list]
    in_specs += [_tiled_spec(e.shape, tm, tn, n_total, at) for e in extras]
    in_specs += [pl.BlockSpec(w.shape, lambda j, i: (0, 0), pipeline_mode=pl.Buffered(1)) for w in wholes]
    out_specs = [_tiled_spec(o.shape, tm, tn, n_total, at) for o in out_shapes]
    out_specs += [pl.BlockSpec(s.shape, at(lambda i, j: (0, 0))) for s in sum_shapes]
    semantics = ("arbitrary", "arbitrary") if sum_shapes or rider.operands else ("parallel", "parallel")
    res = pl.pallas_call(
        body, name=name, grid=grid, in_specs=in_specs + [ANY] * len(rider.operands),
        out_specs=out_specs + [ANY] * len(rider.out_shapes),
        out_shape=list(out_shapes) + list(sum_shapes) + list(rider.out_shapes), scratch_shapes=list(rider.scratch),
        compiler_params=_params(*semantics),
    )(*flat_a, *b_list, *extras, *wholes, *rider.operands)
    n_own = len(out_shapes) + len(sum_shapes)
    return (res[:n_own], res[n_own:]) if with_rider else res


def _mm_tn(name, a_list, b_list, tmm=1024, stacked=False, k_blocks=1):
    flat_b, counts = _column_pieces(b_list)
    n_a, n_b = len(a_list), len(counts)
    n_prod = max(n_a, n_b)
    m_total = a_list[0].shape[0]
    ks = [a_list[s % n_a].shape[1] for s in range(n_prod)]
    widths = [sum(p.shape[1] for p in flat_b[sum(counts[:s]):sum(counts[:s + 1])]) for s in range(n_b)]
    widths = [widths[s % n_b] for s in range(n_prod)]
    tmm = min(tmm, m_total)
    assert m_total % tmm == 0 and all(k % k_blocks == 0 for k in ks)
    assert n_a in (1, n_prod) and n_b in (1, n_prod) and not (stacked and n_a > 1)

    def body(*refs):
        a_refs, b_refs, o_refs = refs[:n_a], refs[n_a:n_a + len(flat_b)], refs[n_a + len(flat_b):]

        @pl.when(pl.program_id(1) == 0)
        def _():
            for o_ref in o_refs:
                o_ref[...] = jnp.zeros_like(o_ref)

        lefts, rights = _load_bf16(a_refs, [1] * n_a), _load_bf16(b_refs, counts)
        for s in range(n_prod):
            product = lax.dot_general(lefts[s % n_a], rights[s % n_b], (((0,), (0,)), ((), ())),
                                      preferred_element_type=F32)
            if stacked:
                o_refs[0][s] += product
            else:
                o_refs[s][...] += product

    in_specs = [pl.BlockSpec((tmm, a.shape[1] // k_blocks), lambda kb, m: (m, kb)) for a in a_list]
    in_specs += [pl.BlockSpec((tmm, b.shape[1]), lambda kb, m: (m, 0)) for b in flat_b]
    if stacked:
        out_shape = [jax.ShapeDtypeStruct((n_prod, ks[0], widths[0]), F32)]
        out_specs = [pl.BlockSpec((n_prod, ks[0] // k_blocks, widths[0]), lambda kb, m: (0, kb, 0))]
    else:
        out_shape = [jax.ShapeDtypeStruct((k, w), F32) for k, w in zip(ks, widths)]
        out_specs = [pl.BlockSpec((k // k_blocks, w), lambda kb, m: (kb, 0)) for k, w in zip(ks, widths)]
    return pl.pallas_call(
        body, name=name, grid=(k_blocks, m_total // tmm), in_specs=in_specs, out_specs=out_specs, out_shape=out_shape,
        compiler_params=_params("arbitrary", "arbitrary"),
    )(*a_list, *flat_b)


def _rows(name, fn, ins, tile_outs, sum_outs=(), tr=512, rider=None):
    t_total = max(a.shape[0] for a in ins)
    tr = min(tr, t_total)
    assert t_total % tr == 0
    n_in, n_tile = len(ins), len(tile_outs)
    rider = rider or _NoRider()
    n_steps = t_total // tr

    def body(*refs):
        own_ins, own_outs, _, riding = rider.split(refs, n_in, n_tile + len(sum_outs))
        step = pl.program_id(0)
        top, bottom = rider.at_steps(riding, step == 0, step == n_steps - 1, step == n_steps - 1)
        top()
        refs = tuple(own_ins) + tuple(own_outs)
        outs = fn(*[r[...].astype(F32) for r in refs[:n_in]])
        for o_ref, o in zip(refs[n_in:n_in + n_tile], outs[:n_tile]):
            o_ref[...] = o.astype(o_ref.dtype)
        if sum_outs:
            @pl.when(pl.program_id(0) == 0)
            def _():
                for s_ref in refs[n_in + n_tile:]:
                    s_ref[...] = jnp.zeros_like(s_ref)

            for s_ref, s in zip(refs[n_in + n_tile:], outs[n_tile:]):
                s_ref[...] += s
        bottom()

    def spec(shape):
        if shape[0] == 1:
            return pl.BlockSpec(shape, lambda i: (0, 0))
        return pl.BlockSpec((tr, shape[1]), lambda i: (i, 0))

    return pl.pallas_call(
        body, name=name, grid=(n_steps,), in_specs=[spec(a.shape) for a in ins] + [ANY] * len(rider.operands),
        out_specs=[spec(o.shape) for o in tile_outs] + [spec(s.shape) for s in sum_outs] + [ANY] * len(rider.out_shapes),
        out_shape=list(tile_outs) + list(sum_outs) + list(rider.out_shapes), scratch_shapes=list(rider.scratch),
        compiler_params=_params("arbitrary" if sum_outs or rider.operands else "parallel"),
    )(*ins, *rider.operands)


def _norm_fwd(name, x, gain, rider=None):
    def fn(xv, g):
        inv = lax.rsqrt(jnp.mean(xv * xv, axis=-1, keepdims=True) + RMS_EPS)
        return (xv * inv * g,)

    res = _rows(name, fn, [x, gain], [jax.ShapeDtypeStruct(x.shape, BF16)], rider=rider)
    return res[0], res[1:]


def _rms_norm_bwd(dh, xv, g):
    inv = lax.rsqrt(jnp.mean(xv * xv, axis=-1, keepdims=True) + RMS_EPS)
    xn = xv * inv
    dxn = dh * g
    return inv * (dxn - xn * jnp.mean(dxn * xn, axis=-1, keepdims=True)), jnp.sum(dh * xn, axis=0, keepdims=True)


def _ple_and_loss(gv, pv, x2v, tv, g_final, g_ple, w_pg):
    d = x2v.shape[1]
    s = _sigmoid(gv)
    xv = x2v + s * pv
    inv = lax.rsqrt(jnp.mean(xv * xv, axis=-1, keepdims=True) + RMS_EPS)
    err = xv * inv * g_final - tv
    dx3, d_final = _rms_norm_bwd(err * (1.0 / d), xv, g_final)
    d_pp, d_gp = dx3 * s, dx3 * pv * s * (1.0 - s)
    dh3 = lax.dot_general(d_gp.astype(BF16), w_pg, (((1,), (1,)), ((), ())), preferred_element_type=F32)
    dx2, d_ple = _rms_norm_bwd(dh3, x2v, g_ple)
    dx2 = dx2 + dx3
    return dx2, dx2, d_pp, d_gp, d_final, (0.5 / d) * jnp.sum(err * err, axis=0, keepdims=True), d_ple


def _window_counts(t_pos, w):
    return jnp.minimum(t_pos + 1, w).astype(F32)


def _pool_fwd(u, w_pool, scale, tr=512):
    t_total, width = u.shape
    tr = min(tr, t_total)
    n_groups = len(POOL_WINDOWS)
    gdim = width // n_groups
    ext = tr + POOL_HALO

    def body(u_ref, halo_ref, w_ref, s_ref, pooled_ref, ya_ref):
        i = pl.program_id(0)
        halo = jnp.where(i == 0, 0.0, halo_ref[...])
        t_pos = i * tr + lax.broadcasted_iota(jnp.int32, (tr, 1), 0)
        for g, w in enumerate(POOL_WINDOWS):
            cols = slice(g * gdim, (g + 1) * gdim)
            main = u_ref[:, cols]
            win = jnp.concatenate([halo[:, cols], main], axis=0)
            span = 1
            while span < w:
                win = win + pltpu.roll(win, span, 0)
                span *= 2
            pooled = win[POOL_HALO:, :] * (1.0 / _window_counts(t_pos, w)) - main
            pooled_b = pooled.astype(BF16)
            pooled_ref[:, cols] = pooled_b
            mixed = jnp.dot(pooled_b, w_ref[g], preferred_element_type=F32)
            ya_ref[:, cols] = (mixed * s_ref[:, cols]).astype(BF16)

    hb = tr // POOL_HALO
    return pl.pallas_call(
        body, name="pool_fwd", grid=(t_total // tr,),
        in_specs=[pl.BlockSpec((tr, width), lambda i: (i, 0)),
                  pl.BlockSpec((POOL_HALO, width), lambda i: (jnp.maximum(i * hb - 1, 0), 0)),
                  pl.BlockSpec((n_groups, gdim, gdim), lambda i: (0, 0, 0)),
                  pl.BlockSpec((1, width), lambda i: (0, 0))],
        out_specs=[pl.BlockSpec((tr, width), lambda i: (i, 0)), pl.BlockSpec((tr, width), lambda i: (i, 0))],
        out_shape=[jax.ShapeDtypeStruct(u.shape, BF16), jax.ShapeDtypeStruct(u.shape, BF16)],
        compiler_params=_params("parallel"),
    )(u, u, w_pool, scale)


def _pool_bwd(dya, pooled, w_pool, scale, tr=512):
    t_total, width = dya.shape
    tr = min(tr, t_total)
    n_groups = len(POOL_WINDOWS)
    gdim = width // n_groups
    ext = tr + POOL_HALO
    n_tiles = t_total // tr

    def body(d_ref, halo_ref, p_ref, w_ref, s_ref, du_ref, dw_ref, ds_ref):
        i = pl.program_id(0)

        @pl.when(i == 0)
        def _():
            dw_ref[...] = jnp.zeros_like(dw_ref)
            ds_ref[...] = jnp.zeros_like(ds_ref)

        halo = jnp.where(i == n_tiles - 1, 0.0, halo_ref[...])
        t_pos = i * tr + lax.broadcasted_iota(jnp.int32, (ext, 1), 0)
        for g, w in enumerate(POOL_WINDOWS):
            cols = slice(g * gdim, (g + 1) * gdim)
            sc = s_ref[:, cols]
            d_main = d_ref[:, cols]
            pooled_b = p_ref[:, cols]
            mixed = jnp.dot(pooled_b, w_ref[g], preferred_element_type=F32)
            ds_ref[:, cols] += jnp.sum(d_main * mixed, axis=0, keepdims=True)
            dmix = (jnp.concatenate([d_main, halo[:, cols]], axis=0) * sc).astype(BF16)
            dw_ref[g] += lax.dot_general(pooled_b, dmix[:tr, :], (((0,), (0,)), ((), ())),
                                         preferred_element_type=F32)
            dpool = lax.dot_general(dmix, w_ref[g], (((1,), (1,)), ((), ())), preferred_element_type=F32)
            win = dpool * (1.0 / _window_counts(t_pos, w))
            span = 1
            while span < w:
                win = win + pltpu.roll(win, ext - span, 0)
                span *= 2
            du_ref[:, cols] = (win[:tr, :] - dpool[:tr, :]).astype(BF16)

    hb = tr // POOL_HALO
    last_halo = t_total // POOL_HALO - 1
    return pl.pallas_call(
        body, name="pool_bwd", grid=(n_tiles,),
        in_specs=[pl.BlockSpec((tr, width), lambda i: (i, 0)),
                  pl.BlockSpec((POOL_HALO, width), lambda i: (jnp.minimum((i + 1) * hb, last_halo), 0)),
                  pl.BlockSpec((tr, width), lambda i: (i, 0)),
                  pl.BlockSpec((n_groups, gdim, gdim), lambda i: (0, 0, 0)),
                  pl.BlockSpec((1, width), lambda i: (0, 0))],
        out_specs=[pl.BlockSpec((tr, width), lambda i: (i, 0)),
                   pl.BlockSpec((n_groups, gdim, gdim), lambda i: (0, 0, 0)),
                   pl.BlockSpec((1, width), lambda i: (0, 0))],
        out_shape=[jax.ShapeDtypeStruct(dya.shape, BF16), jax.ShapeDtypeStruct((n_groups, gdim, gdim), F32),
                   jax.ShapeDtypeStruct((1, width), F32)],
        compiler_params=_params("arbitrary"),
    )(dya, dya, pooled, w_pool, scale)


def _head_masks():
    lane = lax.broadcasted_iota(jnp.int32, (1, LANES), 1)
    return lane < HEAD_DIM


def _stack_heads(tile, first):
    zero = jnp.zeros_like(tile)
    return jnp.concatenate([jnp.where(first, tile, zero), jnp.where(first, zero, tile)], axis=0)


def _causal_mask(t_pos, k_start):
    col = lax.broadcasted_iota(jnp.int32, (1, 2 * ATT_SLAB), 1)
    return k_start + (col & (ATT_SLAB - 1)) < t_pos


def _slab_scores(q, kd, mask):
    z2 = lax.dot_general(q, kd, (((1,), (1,)), ((), ())), preferred_element_type=F32) * LOG2_E
    log_hit = jnp.minimum(z2, 0.0) - jnp.log2(1.0 + jnp.exp2(-jnp.abs(z2)))
    log_fail = log_hit - z2
    return log_hit, (log_fail if mask is None else jnp.where(mask, log_fail, 0.0))


def _weights(log_hit, suffix, mask):
    arg = log_hit + suffix
    return jnp.exp2(arg if mask is None else jnp.where(mask, arg, -1e30))


def _tri(upper):
    r = lax.broadcasted_iota(jnp.int32, (ATT_CHUNK, ATT_CHUNK), 0)
    c = lax.broadcasted_iota(jnp.int32, (ATT_CHUNK, ATT_CHUNK), 1)
    return jnp.where(r > c if upper else r < c, 1.0, 0.0).astype(BF16)


def _tri_spec():
    return pl.BlockSpec((ATT_CHUNK, ATT_CHUNK), lambda h, i: (0, 0), pipeline_mode=pl.Buffered(1))


def _scan_chunk(v, tri):
    return jnp.dot(v.astype(BF16), tri, preferred_element_type=F32)


def _lane_bcast(col):
    return jnp.broadcast_to(col, (col.shape[0], LANES))


def _scan_slab(v, tri, carries, from_right):
    n_chunks = ATT_SLAB // ATT_CHUNK
    edge = 0 if from_right else ATT_CHUNK - 1
    parts, new_carries = [None] * (2 * n_chunks), []
    for head in range(2):
        run = carries[head]
        for c in (reversed(range(n_chunks)) if from_right else range(n_chunks)):
            lo_col = head * ATT_SLAB + c * ATT_CHUNK
            vc = v[:, lo_col:lo_col + ATT_CHUNK]
            sc = _scan_chunk(vc, tri)
            parts[head * n_chunks + c] = sc + jnp.concatenate([run] * (ATT_CHUNK // LANES), axis=1)
            run = run + _lane_bcast(sc[:, edge:edge + 1] + vc[:, edge:edge + 1])
        new_carries.append(run)
    return jnp.concatenate(parts, axis=1), new_carries


def _fold_heads(stacked, first):
    s = stacked.shape[0] // 2
    return jnp.where(first, stacked[:s], stacked[s:])


class _NoRider:
    operands, out_shapes, scratch = (), (), ()

    def split(self, refs, n_base_in, n_base_out):
        n_in, n_out, n_sem = len(self.operands), len(self.out_shapes), len(self.scratch)
        a = n_base_in + n_in
        b = a + n_base_out + n_out
        mine = (refs[n_base_in:a], refs[a + n_base_out:b], refs[b:b + n_sem])
        return refs[:n_base_in], refs[a:a + n_base_out], refs[b + n_sem:], mine

    def start(self, ins, outs, sems):
        pass

    def relay(self, ins, outs, sems):
        pass

    def finish(self, ins, outs, sems):
        pass

    def at_steps(self, refs, first_step, relay_step, last_step):
        if not self.operands:
            return (lambda: None), (lambda: None)

        def top():
            pl.when(first_step)(lambda: self.start(*refs))
            pl.when(relay_step)(lambda: self.relay(*refs))

        return top, lambda: pl.when(last_step)(lambda: self.finish(*refs))


def _attn_fwd(q_src, q_col, kv_src, k_col, v_col, n_pairs=4, rider=_NoRider()):
    t_total = q_src.shape[0]
    blk = ATT_BLOCK
    n_chains = ATT_FWD_CHAINS if t_total % (ATT_FWD_CHAINS * blk) == 0 else ATT_CHAINS
    n_steps = t_total // (n_chains * blk)
    assert t_total % ATT_SLAB == 0 and ATT_SLAB == ATT_BLOCK

    def body(*refs):
        (q_ref, k_ref, v_ref, suffix_ref), (o_ref,), _, riding = rider.split(refs, 4, 1)
        h, ii = pl.program_id(0), pl.program_id(1)
        top, bottom = rider.at_steps(riding, (h == 0) & (ii == 0), (h == n_pairs - 1) & (ii == 0),
                                     (h == n_pairs - 1) & (ii == n_steps - 1))
        top()
        first = _head_masks()
        suffix_tri = suffix_ref[...]
        blocks = [n_chains * ii + c for c in range(n_chains)]
        qs = [q_ref[c * blk:(c + 1) * blk, :] * ATT_SCALE for c in range(n_chains)]
        t_pos = [b * blk + lax.broadcasted_iota(jnp.int32, (blk, 1), 0) for b in blocks]

        def one(c, t, chain, on_diagonal):
            _, acc, right_a, right_b = chain
            k_start = pl.multiple_of((blocks[c] - t) * ATT_SLAB, ATT_SLAB)
            kd = _stack_heads(k_ref[pl.ds(k_start, ATT_SLAB), :], first)
            vd = _stack_heads(v_ref[pl.ds(k_start, ATT_SLAB), :], first)
            mask = _causal_mask(t_pos[c], k_start) if on_diagonal else None
            log_hit, log_fail = _slab_scores(qs[c], kd, mask)
            suffix, (right_a, right_b) = _scan_slab(log_fail, suffix_tri, (right_a, right_b), from_right=True)
            a = _weights(log_hit, suffix, mask).astype(BF16)
            acc = acc + jnp.dot(a, vd, preferred_element_type=F32)
            return jnp.max(jnp.maximum(right_a, right_b)), acc, right_a, right_b

        def step(state, on_diagonal):
            t, chains = state
            return t + 1, tuple(one(c, t, chains[c], on_diagonal) for c in range(n_chains))

        def more(state):
            t, chains = state
            return (t <= blocks[0]) & (functools.reduce(jnp.maximum, [ch[0] for ch in chains]) > ATT_EXIT_BELOW)

        zero = jnp.zeros((blk, LANES), F32)
        state = step((0, ((jnp.float32(0.0), zero, zero, zero),) * n_chains), on_diagonal=True)
        t, chains = lax.while_loop(more, functools.partial(step, on_diagonal=False), state)
        for c in range(n_chains):
            chain = chains[c]
            if c:
                _, chain = lax.while_loop(
                    lambda s, c=c: (s[0] <= blocks[c]) & (s[1][0] > ATT_EXIT_BELOW),
                    lambda s, c=c: (s[0] + 1, one(c, s[0], s[1], False)), (t, chain))
            o_ref[c * blk:(c + 1) * blk, :] = chain[1].astype(BF16)
        bottom()

    rows = n_chains * blk
    res = pl.pallas_call(
        body, name="attn_fwd", grid=(n_pairs, n_steps),
        in_specs=[pl.BlockSpec((rows, LANES), lambda h, i: (i, q_col + h)),
                  pl.BlockSpec((t_total, LANES), lambda h, i: (0, k_col + h)),
                  pl.BlockSpec((t_total, LANES), lambda h, i: (0, v_col + h)), _tri_spec()] + [ANY] * len(rider.operands),
        out_specs=[pl.BlockSpec((rows, LANES), lambda h, i: (i, h))] + [ANY] * len(rider.out_shapes),
        out_shape=[jax.ShapeDtypeStruct((t_total, n_pairs * LANES), BF16)] + list(rider.out_shapes),
        scratch_shapes=list(rider.scratch),
        compiler_params=_params("arbitrary", "arbitrary"),
    )(q_src, kv_src, kv_src, _tri(upper=True), *rider.operands)
    return res[0], res[1:]


def _attn_bwd(q_src, q_col, kv_src, k_col, v_col, dy, n_pairs=4, rider=_NoRider()):
    t_total = q_src.shape[0]
    blk = ATT_BLOCK
    n_steps = t_total // (ATT_CHAINS * blk)
    n_slabs = t_total // ATT_SLAB
    assert t_total % ATT_SLAB == 0 and ATT_SLAB == ATT_BLOCK

    def body(*refs):
        ins, (dq_ref, dk_ref, dv_ref), (g_s, dk_acc, dv_acc), riding = rider.split(refs, 6, 3)
        q_ref, dy_ref, k_ref, v_ref, suffix_ref, prefix_ref = ins
        h, ii = pl.program_id(0), pl.program_id(1)
        top, bottom = rider.at_steps(riding, (h == 0) & (ii == 0), (h == n_pairs - 1) & (ii == 0),
                                     (h == n_pairs - 1) & (ii == n_steps - 1))
        top()

        @pl.when(ii == 0)
        def _():
            dk_acc[...] = jnp.zeros_like(dk_acc)
            dv_acc[...] = jnp.zeros_like(dv_acc)

        first = _head_masks()
        suffix_tri = suffix_ref[...]
        prefix_tri = prefix_ref[...]
        blocks = [ATT_CHAINS * ii + c for c in range(ATT_CHAINS)]
        rows = [slice(c * blk, (c + 1) * blk) for c in range(ATT_CHAINS)]
        qs = [q_ref[r, :] * ATT_SCALE for r in rows]
        dys = [dy_ref[r, :] for r in rows]
        t_pos = [b * blk + lax.broadcasted_iota(jnp.int32, (blk, 1), 0) for b in blocks]

        def one1(c, t, chain, on_diagonal):
            _, right_a, right_b = chain
            slab = blocks[c] - t
            k_start = pl.multiple_of(slab * ATT_SLAB, ATT_SLAB)
            kd = _stack_heads(k_ref[pl.ds(k_start, ATT_SLAB), :], first)
            vd = _stack_heads(v_ref[pl.ds(k_start, ATT_SLAB), :], first)
            mask = _causal_mask(t_pos[c], k_start) if on_diagonal else None
            log_hit, log_fail = _slab_scores(qs[c], kd, mask)
            suffix, (right_a, right_b) = _scan_slab(log_fail, suffix_tri, (right_a, right_b), from_right=True)
            a = _weights(log_hit, suffix, mask)
            da = lax.dot_general(dys[c], vd, (((1,), (1,)), ((), ())), preferred_element_type=F32)
            g_s[c, slab] = (da * a).astype(BF16)
            dv_acc[pl.ds(k_start, ATT_SLAB), :] += _fold_heads(lax.dot_general(
                a.astype(BF16), dys[c], (((0,), (0,)), ((), ())), preferred_element_type=F32), first)
            return jnp.max(jnp.maximum(right_a, right_b)), right_a, right_b

        def step1(state, on_diagonal):
            t, chains = state
            return t + 1, tuple(one1(c, t, chains[c], on_diagonal) for c in range(ATT_CHAINS))

        def more(state):
            t, chains = state
            return (t <= blocks[0]) & (functools.reduce(jnp.maximum, [ch[0] for ch in chains]) > ATT_EXIT_BELOW)

        zero = jnp.zeros((blk, LANES), F32)
        state = step1((0, ((jnp.float32(0.0), zero, zero),) * ATT_CHAINS), on_diagonal=True)
        joint, chains = lax.while_loop(more, functools.partial(step1, on_diagonal=False), state)
        done = [joint]
        for c in range(1, ATT_CHAINS):
            done.append(lax.while_loop(
                lambda s, c=c: (s[0] <= blocks[c]) & (s[1][0] > ATT_EXIT_BELOW),
                lambda s, c=c: (s[0] + 1, one1(c, s[0], s[1], False)), (joint, chains[c]))[0])

        def one2(c, t, carry, on_diagonal):
            dq, left_a, left_b = carry
            slab = blocks[c] - t
            k_start = pl.multiple_of(slab * ATT_SLAB, ATT_SLAB)
            kd = _stack_heads(k_ref[pl.ds(k_start, ATT_SLAB), :], first)
            g = g_s[c, slab]
            z2 = lax.dot_general(qs[c], kd, (((1,), (1,)), ((), ())), preferred_element_type=F32) * LOG2_E
            sig = 1.0 / (1.0 + jnp.exp2(-z2))
            prefix, (left_a, left_b) = _scan_slab(g, prefix_tri, (left_a, left_b), from_right=False)
            dz = g * (1.0 - sig) - sig * prefix
            if on_diagonal:
                dz = jnp.where(_causal_mask(t_pos[c], k_start), dz, 0.0)
            dz = dz.astype(BF16)
            dq = dq + jnp.dot(dz, kd, preferred_element_type=F32)
            dk_acc[pl.ds(k_start, ATT_SLAB), :] += _fold_heads(lax.dot_general(
                dz, qs[c], (((0,), (0,)), ((), ())), preferred_element_type=F32), first)
            return dq, left_a, left_b

        carries = [(zero, zero, zero)]
        for c in range(1, ATT_CHAINS):
            carries.append(lax.fori_loop(
                0, done[c] - joint, lambda n, carry, c=c: one2(c, done[c] - 1 - n, carry, False), (zero, zero, zero)))
        carries = lax.fori_loop(
            0, joint - 1,
            lambda n, cs: tuple(one2(c, joint - 1 - n, cs[c], False) for c in range(ATT_CHAINS)), tuple(carries))
        for c in range(ATT_CHAINS):
            dq_ref[rows[c], :] = (one2(c, 0, carries[c], True)[0] * ATT_SCALE).astype(BF16)

        @pl.when(ii == n_steps - 1)
        def _():
            dk_ref[...] = dk_acc[...].astype(BF16)
            dv_ref[...] = dv_acc[...].astype(BF16)

        bottom()

    out = jax.ShapeDtypeStruct((t_total, n_pairs * LANES), BF16)
    n_rows = ATT_CHAINS * blk
    whole = dict(pipeline_mode=pl.Buffered(1))
    res = pl.pallas_call(
        body, name="attn_bwd", grid=(n_pairs, n_steps),
        in_specs=[pl.BlockSpec((n_rows, LANES), lambda h, i: (i, q_col + h)),
                  pl.BlockSpec((n_rows, LANES), lambda h, i: (i, h)),
                  pl.BlockSpec((t_total, LANES), lambda h, i: (0, k_col + h), **whole),
                  pl.BlockSpec((t_total, LANES), lambda h, i: (0, v_col + h), **whole), _tri_spec(), _tri_spec()]
        + [ANY] * len(rider.operands),
        out_specs=[pl.BlockSpec((n_rows, LANES), lambda h, i: (i, h)),
                   pl.BlockSpec((t_total, LANES), lambda h, i: (0, h)),
                   pl.BlockSpec((t_total, LANES), lambda h, i: (0, h))] + [ANY] * len(rider.out_shapes),
        out_shape=[out, out, out] + list(rider.out_shapes),
        scratch_shapes=list(rider.scratch) + [pltpu.VMEM((ATT_CHAINS, n_slabs, blk, 2 * ATT_SLAB), BF16),
                                              pltpu.VMEM((t_total, LANES), F32), pltpu.VMEM((t_total, LANES), F32)],
        compiler_params=_params("arbitrary", "arbitrary"),
    )(q_src, dy, kv_src, kv_src, _tri(upper=True), _tri(upper=False), *rider.operands)
    return res[:3], res[3:]


def _adamw(name, w, g, m, v):
    def fn(wv, gv, mv, vv):
        mn = ADAM_B1 * mv + (1.0 - ADAM_B1) * gv
        vn = ADAM_B2 * vv + (1.0 - ADAM_B2) * (gv * gv)
        m_hat = mn / (1.0 - ADAM_B1 ** ADAM_STEP)
        v_hat = vn / (1.0 - ADAM_B2 ** ADAM_STEP)
        return -ADAM_LR * (m_hat / (jnp.sqrt(v_hat) + ADAM_EPS) + ADAM_WD * wv), mn, vn

    rows = w.shape[0]
    tr = _row_tile(rows)
    shp = jax.ShapeDtypeStruct(w.shape, F32)
    if rows == 1:
        def body(w_ref, g_ref, m_ref, v_ref, d_ref, mo_ref, vo_ref):
            d, mn, vn = fn(w_ref[...], g_ref[...], m_ref[...], v_ref[...])
            d_ref[...], mo_ref[...], vo_ref[...] = d, mn, vn

        return pl.pallas_call(body, name=name, out_shape=[shp, shp, shp])(w, g, m, v)
    return _rows(name, fn, [w, g, m, v], [shp, shp, shp], tr=tr)


def _place():
    return lax.axis_index("x"), lax.axis_index("y"), lax.axis_index("c")


def _other_chips(x, y):
    return [(1 - x, y), (x, 1 - y), (1 - x, 1 - y)]


ANY = pl.BlockSpec(memory_space=pl.ANY)


def _remote(src, dst, send_sem, recv_sem, to):
    return pltpu.make_async_remote_copy(src_ref=src, dst_ref=dst, send_sem=send_sem, recv_sem=recv_sem,
                                        device_id=to, device_id_type=MESH)


class _WeightGather(_NoRider):
    def __init__(self, shards):
        n_w = len(shards)
        self.operands = list(shards)
        self.out_shapes = [jax.ShapeDtypeStruct((N_CHIPS,) + s.shape, s.dtype) for s in shards]
        self.scratch = [pltpu.SemaphoreType.DMA((3, n_w))] * 4 + [pltpu.SemaphoreType.DMA((n_w,))] * 2

    def _copies(self, ins, outs, sems):
        send_sems, recv_sems, relay_send, relay_recv, own_send, own_recv = sems
        x, y, c = _place()
        my_chip, sibling = 2 * x + y, (x, y, 1 - c)
        n_w = len(ins)

        def half(w, chip, core):
            h = self.operands[w].shape[0] // 2
            return outs[w].at[chip, pl.ds(core * h, h)]

        own = [_remote(ins[w], outs[w].at[my_chip], own_send.at[w], own_recv.at[w], sibling) for w in range(n_w)]
        sends, landed, relays, relayed = [], [], [], []
        for p, (ox, oy) in enumerate(_other_chips(x, y)):
            for w in range(n_w):
                h = self.operands[w].shape[0] // 2
                sends.append(_remote(ins[w].at[pl.ds(c * h, h)], half(w, my_chip, c), send_sems.at[p, w],
                                     recv_sems.at[p, w], (ox, oy, c)))
                here = half(w, 2 * ox + oy, c)
                landed.append(_remote(here, here, send_sems.at[p, w], recv_sems.at[p, w], (ox, oy, c)))
                relays.append(_remote(here, here, relay_send.at[p, w], relay_recv.at[p, w], sibling))
                there = half(w, 2 * ox + oy, 1 - c)
                relayed.append(_remote(there, there, relay_send.at[p, w], relay_recv.at[p, w], sibling))
        return own, sends, landed, relays, relayed

    def start(self, ins, outs, sems):
        own, sends, _, _, _ = self._copies(ins, outs, sems)
        for cp in own + sends:
            cp.start()

    def relay(self, ins, outs, sems):
        _, _, landed, relays, _ = self._copies(ins, outs, sems)
        for arrival, cp in zip(landed, relays):
            arrival.wait_recv()
            cp.start()

    def finish(self, ins, outs, sems):
        own, sends, _, relays, relayed = self._copies(ins, outs, sems)
        for arrival in relayed:
            arrival.wait_recv()
        for cp in sends + relays:
            cp.wait_send()
        for cp in own:
            cp.wait()


class _ChipExchange(_NoRider):
    def __init__(self, pair_sums):
        n_w = len(pair_sums)
        self.operands = list(pair_sums)
        self.out_shapes = [jax.ShapeDtypeStruct((3,) + s.shape[1:], s.dtype) for s in pair_sums]
        self.scratch = [pltpu.SemaphoreType.DMA((3, n_w))] * 2

    def _copies(self, ins, outs, sems):
        send_sems, recv_sems = sems
        x, y, c = _place()
        return [_remote(ins[w].at[2 * ox + oy], outs[w].at[p], send_sems.at[p, w], recv_sems.at[p, w], (ox, oy, c))
                for p, (ox, oy) in enumerate(_other_chips(x, y)) for w in range(len(ins))]

    def start(self, ins, outs, sems):
        for cp in self._copies(ins, outs, sems):
            cp.start()

    def finish(self, ins, outs, sems):
        for cp in self._copies(ins, outs, sems):
            cp.wait()


class _PairExchange(_NoRider):
    def __init__(self, grads):
        n_w = len(grads)
        self.operands = list(grads)
        self.out_shapes = [jax.ShapeDtypeStruct(g.shape[:-2] + (g.shape[-2] // 2, g.shape[-1]), F32) for g in grads]
        self.scratch = [pltpu.SemaphoreType.DMA((n_w,))] * 2

    def _copies(self, ins, theirs, sems):
        send_sems, recv_sems = sems
        x, y, c = _place()
        sends = []
        for w, g in enumerate(self.operands):
            rows = pl.ds((1 - c) * (g.shape[-2] // 2), g.shape[-2] // 2)
            src = ins[w].at[:, rows, :] if g.ndim == 3 else ins[w].at[rows, :]
            sends.append(_remote(src, theirs[w], send_sems.at[w], recv_sems.at[w], (x, y, 1 - c)))
        return sends

    def start(self, ins, outs, sems):
        for cp in self._copies(ins, outs, sems):
            cp.start()

    def finish(self, ins, outs, sems):
        for cp in self._copies(ins, outs, sems):
            cp.wait()


def _exchange_pair_sum(name, place, grad):
    n, r, c = grad.shape
    half = r // 2

    def body(place_ref, g_all, g_ref, o_ref, theirs, send_sems, recv_sems):
        j = pl.program_id(0)
        x, y, core = _place()

        def copy(k):
            return _remote(g_all.at[k, pl.ds((1 - core) * half, half)], theirs.at[k], send_sems.at[k],
                           recv_sems.at[k], (x, y, 1 - core))

        @pl.when(j == 0)
        def _():
            for k in range(n):
                copy(k).start()

        copy(j).wait_recv()
        o_ref[0] = (g_ref[0] + theirs[j]).astype(BF16)

        @pl.when(j == n - 1)
        def _():
            for k in range(n):
                copy(k).wait_send()

    return pl.pallas_call(
        body, name=name, out_shape=jax.ShapeDtypeStruct((n, half, c), BF16),
        grid_spec=pltpu.PrefetchScalarGridSpec(
            num_scalar_prefetch=1, grid=(n,),
            in_specs=[ANY, pl.BlockSpec((1, half, c), lambda j, pr: (j, pr[0], 0))],
            out_specs=pl.BlockSpec((1, half, c), lambda j, pr: (j, 0, 0)),
            scratch_shapes=[pltpu.VMEM((n, half, c), F32), pltpu.SemaphoreType.DMA((n,)),
                            pltpu.SemaphoreType.DMA((n,))]),
        compiler_params=_params("arbitrary"),
    )(place, grad, grad)


class _NoExchanges:
    pair_sums, landed = {}, {}

    def gather(self, names):
        return _NoRider()

    def pair(self, names, grads):
        return _NoRider()

    def paired(self, names, grads, theirs):
        pass

    def pair_now(self, names, grads):
        pass

    def chip(self, names):
        return _NoRider()


class _StepExchanges(_NoExchanges):
    def __init__(self, shards_bf16, place):
        self.shards, self.place = shards_bf16, place
        self.pair_sums, self.landed = {}, {}

    def gather(self, names):
        return _WeightGather([self.shards[n] for n in names])

    def pair(self, names, grads):
        return _PairExchange([grads[n] for n in names])

    def paired(self, names, grads, theirs):
        for n, other in zip(names, theirs):
            self.pair_sums[n] = _pair_sum(f"pair_sum_{n}", self.place, grads[n], other)

    def pair_now(self, names, grads):
        for n in names:
            self.pair_sums[n] = _exchange_pair_sum(f"pair_sum_{n}", self.place, grads[n])

    def chip(self, names):
        return _ChipExchange([self.pair_sums[n] for n in names])


SUM_ROWS = 32


def _finish_gradients(place, pair_sums, landed, vec):
    n_w = len(pair_sums)
    rows = vec.shape[0]
    halves = [s.shape[1:] for s in pair_sums]

    def body(place_ref, *refs):
        sums, lands, v_ref = refs[:n_w], refs[n_w:2 * n_w], refs[2 * n_w]
        outs, o_ref = refs[2 * n_w + 1:3 * n_w + 1], refs[3 * n_w + 1]
        stage = refs[3 * n_w + 2:4 * n_w + 2]
        kept, half_send, half_recv, slots, core_sums, vec_send, vec_recv, sum_send, sum_recv = refs[4 * n_w + 2:]
        x, y, c = _place()
        my_chip, sibling = 2 * x + y, (x, y, 1 - c)
        slots[my_chip] = v_ref[...]
        spread = []
        for p, (ox, oy) in enumerate(_other_chips(x, y)):
            here = slots.at[2 * ox + oy]
            spread.append((_remote(v_ref, slots.at[my_chip], vec_send.at[p], vec_recv.at[p], (ox, oy, c)),
                           _remote(here, here, vec_send.at[p], vec_recv.at[p], (ox, oy, c))))
        for send, _ in spread:
            send.start()
        copies = []
        for w, (h, _) in enumerate(halves):
            step = SUM_ROWS if h % SUM_ROWS == 0 else h

            def sum_rows(i, _, w=w, step=step):
                r = pl.ds(pl.multiple_of(i * step, step), step)
                stage[w][r, :] = functools.reduce(lambda total, p: total + lands[w][p, r, :].astype(F32), range(3),
                                                  sums[w][0, r, :].astype(F32))
                return 0

            lax.fori_loop(0, h // step, sum_rows, 0)
            mine, theirs = outs[w].at[pl.ds(c * h, h)], outs[w].at[pl.ds((1 - c) * h, h)]
            copies.append((pltpu.make_async_copy(stage[w], mine, kept.at[w]),
                           _remote(stage[w], mine, half_send.at[w], half_recv.at[w], sibling),
                           _remote(theirs, theirs, half_send.at[w], half_recv.at[w], sibling)))
            copies[-1][0].start()
            copies[-1][1].start()
        for send, arrival in spread:
            arrival.wait_recv()
            send.wait_send()
        core_sums[c] = functools.reduce(lambda total, chip: total + slots[chip], range(1, N_CHIPS), slots[0])
        mine, theirs = core_sums.at[c], core_sums.at[1 - c]
        to_sibling = _remote(mine, mine, sum_send.at[0], sum_recv.at[0], sibling)
        to_sibling.start()
        _remote(theirs, theirs, sum_send.at[0], sum_recv.at[0], sibling).wait_recv()
        to_sibling.wait_send()
        o_ref[...] = core_sums[0] + core_sums[1]
        for keep, send, arrival in copies:
            keep.wait()
            arrival.wait_recv()
            send.wait_send()

    once = dict(pipeline_mode=pl.Buffered(1))
    vm = pl.BlockSpec((rows, LANES), lambda i, pr: (0, 0))
    res = pl.pallas_call(
        body, name="finish_gradients",
        grid_spec=pltpu.PrefetchScalarGridSpec(
            num_scalar_prefetch=1, grid=(1,),
            in_specs=[pl.BlockSpec((1,) + hc, lambda i, pr: (pr[1], 0, 0), **once) for hc in halves]
            + [pl.BlockSpec((3,) + hc, lambda i, pr: (0, 0, 0), **once) for hc in halves] + [vm],
            out_specs=[ANY] * n_w + [vm],
            scratch_shapes=[pltpu.VMEM(hc, F32) for hc in halves]
            + [pltpu.SemaphoreType.DMA((n_w,))] * 3
            + [pltpu.VMEM((N_CHIPS, rows, LANES), F32), pltpu.VMEM((2, rows, LANES), F32),
               pltpu.SemaphoreType.DMA((N_CHIPS - 1,)), pltpu.SemaphoreType.DMA((N_CHIPS - 1,)),
               pltpu.SemaphoreType.DMA((1,)), pltpu.SemaphoreType.DMA((1,))]),
        out_shape=[_sds((2 * h, cols), F32) for h, cols in halves] + [_sds(vec.shape, F32)],
        compiler_params=_params("arbitrary"),
    )(place, *pair_sums, *landed, vec)
    return res[:n_w], res[n_w]


def _row_tile(rows):
    fits = [tr for tr in range(16, min(rows, 512) + 1, 16) if rows % tr == 0]
    return max(fits) if fits else rows


def _pair_sum(name, place, grad, theirs):
    if grad.ndim == 2:
        return _pair_sum_joined(name, place, grad, theirs)
    n, r, c = grad.shape
    half = r // 2
    tr = _row_tile(half)
    nb = half // tr

    def body(place_ref, g_ref, t_ref, o_ref):
        o_ref[...] = (g_ref[...] + t_ref[...]).astype(BF16)

    return pl.pallas_call(
        body, name=name, out_shape=jax.ShapeDtypeStruct((n, half, c), BF16),
        grid_spec=pltpu.PrefetchScalarGridSpec(
            num_scalar_prefetch=1, grid=(n, nb),
            in_specs=[pl.BlockSpec((1, tr, c), lambda j, i, pr: (j, pr[0] * nb + i, 0)),
                      pl.BlockSpec((1, tr, c), lambda j, i, pr: (j, i, 0))],
            out_specs=pl.BlockSpec((1, tr, c), lambda j, i, pr: (j, i, 0))),
        compiler_params=_params("parallel", "parallel"),
    )(place, grad, theirs)


def _pair_sum_joined(name, place, grad, theirs):
    r, wide = grad.shape
    half, c = r // 2, wide // N_CHIPS
    tr = _row_tile(half)
    nb = half // tr

    def body(place_ref, g_ref, t_ref, o_ref):
        for j in range(N_CHIPS):
            cols = slice(j * c, (j + 1) * c)
            o_ref[j] = (g_ref[:, cols] + t_ref[:, cols]).astype(BF16)

    return pl.pallas_call(
        body, name=name, out_shape=jax.ShapeDtypeStruct((N_CHIPS, half, c), BF16),
        grid_spec=pltpu.PrefetchScalarGridSpec(
            num_scalar_prefetch=1, grid=(nb,),
            in_specs=[pl.BlockSpec((tr, wide), lambda i, pr: (pr[0] * nb + i, 0)),
                      pl.BlockSpec((tr, wide), lambda i, pr: (i, 0))],
            out_specs=pl.BlockSpec((N_CHIPS, tr, c), lambda i, pr: (0, i, 0))),
        compiler_params=_params("parallel"),
    )(place, grad, theirs)


MIXER = ("w_branch_a", "w_branch_b", "w_out")
FFN_PLE = ("w_ffn_gate", "w_ffn_up", "w_ffn_down", "w_ple_gate", "w_ple_proj")
LATE = MIXER + FFN_PLE
BIG = ("w_in",) + LATE
HELD_TRANSPOSED = ("w_ffn_gate", "w_ffn_up")
SMALL = ("norm_mix", "w_pool", "pool_scale", "norm_ffn", "norm_ple", "norm_final")


def _join_columns(w4):
    return jnp.concatenate([w4[j] for j in range(N_CHIPS)], axis=1)


def _sds(shape, dtype):
    return jax.ShapeDtypeStruct(shape, dtype)


def _local_step(x, p, target, wf, small, ex=None):
    t, d = x.shape
    w_pool_b = small["w_pool"].astype(BF16)
    dp = w_pool_b.shape[0] * w_pool_b.shape[1]

    ex = ex or _NoExchanges()
    h1, first = _norm_fwd("norm_mix", x, small["norm_mix"], rider=ex.gather(("w_in",)))
    wf = {**wf, **dict(zip(("w_in",), first))}
    w_in = wf["w_in"]
    u, q, kv, ga, gb = _mm(
        "proj", [h1], [w_in[j] for j in range(N_CHIPS)], "nn",
        [_sds((t, dp), F32), _sds((t, dp), BF16), _sds((t, d), BF16), _sds((t, d), BF16), _sds((t, d), BF16)],
        separate=True, epilogue=lambda uq, kv_, ga_, gb_: (uq[:, :dp], uq[:, dp:], kv_, ga_, gb_), tm=512)
    pooled, ya = _pool_fwd(u, w_pool_b, small["pool_scale"])
    n_pairs = dp // LANES
    yb, late = _attn_fwd(q, 0, kv, 0, n_pairs, n_pairs, rider=ex.gather(LATE))
    wf = {**wf, **dict(zip(LATE, late))}
    w_down = wf["w_ffn_down"].reshape(-1, d)
    dff = w_down.shape[0]
    w_gate_t, w_up_t = wf["w_ffn_gate"].reshape(dff, d), wf["w_ffn_up"].reshape(dff, d)
    w_a, w_b, w_pp = _join_columns(wf["w_branch_a"]), _join_columns(wf["w_branch_b"]), _join_columns(wf["w_ple_proj"])
    w_out = wf["w_out"].reshape(d, d)
    w_pg = wf["w_ple_gate"].reshape(d, d)
    def residual_norm(branch, xv, g, w):
        xn = xv + jnp.dot(branch.astype(BF16), w, preferred_element_type=F32)
        return xn, xn * lax.rsqrt(jnp.mean(xn * xn, axis=-1, keepdims=True) + RMS_EPS) * g

    def mixer_tail(tav, tbv, gav, gbv, xv, g, w):
        merged = _sigmoid(gav) * tav + _sigmoid(gbv) * tbv
        return (tav, tbv, merged) + residual_norm(merged, xv, g, w)

    def ffn_tail(gv, uv, xv, g, w):
        act = gv * _sigmoid(gv) * uv
        return (gv, uv, act) + residual_norm(act, xv, g, w)

    stream = [_sds((t, d), F32), _sds((t, d), BF16)]
    ta, tb, merged, x1, h2 = _mm(
        "mixer_out", [ya, yb], [w_a, w_b], "nn", [_sds((t, d), BF16)] * 3 + stream,
        extras=[ga, gb, x, small["norm_ffn"]], wholes=[w_out], separate=True, epilogue=mixer_tail, tm=512)
    gate, up, act, x2, h3 = _mm(
        "ffn", [h2], [w_gate_t, w_up_t], "nt", [_sds((t, dff), BF16)] * 3 + stream,
        extras=[x1, small["norm_ple"]], wholes=[w_down], separate=True, epilogue=ffn_tail, tm=256)
    dx2, dx2_b, d_pp, d_gp, d_norm_final, loss_row, d_norm_ple = _mm(
        "ple_loss", [h3, p], [w_pg, w_pp], "nn", stream + [_sds((t, d), BF16)] * 2,
        extras=[x2, target, small["norm_final"].reshape(1, d), small["norm_ple"]], wholes=[w_pg], separate=True,
        epilogue=_ple_and_loss, sum_shapes=[_sds((1, d), F32)] * 3, tm=512)

    def through_norm(dh, xv, g, dres):
        dx, d_gain = _rms_norm_bwd(dh, xv, g)
        return dx + dres, dx + dres, d_gain

    gain_sum = [_sds((1, d), F32)]
    g_w_pp, g_w_pg = _mm_tn("g_ple", [p, h3], [d_pp, d_gp])

    def ffn_bwd(d_act, gv, uv, xv, g, dres, wg_t, wu_t):
        s = _sigmoid(gv)
        d_gate, d_up = d_act * uv * (s * (1.0 + gv * (1.0 - s))), d_act * (gv * s)
        dh2 = (jnp.dot(d_gate.astype(BF16), wg_t, preferred_element_type=F32)
               + jnp.dot(d_up.astype(BF16), wu_t, preferred_element_type=F32))
        return (d_gate, d_up) + through_norm(dh2, xv, g, dres)

    d_gate, d_up, dx1, dx1_b, d_norm_ffn = _mm(
        "ffn_bwd", [dx2_b], [w_down], "nt", [_sds((t, dff), BF16)] * 2 + stream,
        extras=[gate, up, x1, small["norm_ffn"], dx2], wholes=[w_gate_t, w_up_t], epilogue=ffn_bwd,
        sum_shapes=gain_sum, tm=256)
    g_w_down, = _mm_tn("g_ffn_down", [act], [dx2_b], tmm=512)
    g_w_gate_t, g_w_up_t = _mm_tn("g_ffn_gate_up", [d_gate, d_up], [h2], k_blocks=2)

    def merge_bwd(acc, tav, tbv, gav, gbv):
        sa, sb = _sigmoid(gav), _sigmoid(gbv)
        return acc * sa, acc * sb, acc * tav * sa * (1.0 - sa), acc * tbv * sb * (1.0 - sb)

    big = {
        "w_ffn_gate": g_w_gate_t.reshape(wf["w_ffn_gate"].shape), "w_ffn_up": g_w_up_t.reshape(wf["w_ffn_up"].shape),
        "w_ffn_down": g_w_down.reshape(wf["w_ffn_down"].shape),
        "w_ple_gate": g_w_pg.reshape(wf["w_ple_gate"].shape), "w_ple_proj": g_w_pp,
    }
    (d_ta, d_tb, d_ga, d_gb), theirs = _mm(
        "d_merged", [dx1_b], [w_out], "nt", [_sds((t, d), BF16)] * 4, extras=[ta, tb, ga, gb], epilogue=merge_bwd,
        tm=512, rider=ex.pair(FFN_PLE, big))
    ex.paired(FFN_PLE, big, theirs)
    g_w_out, big["w_branch_a"], big["w_branch_b"] = _mm_tn("g_mixer", [merged, ya, yb], [dx1_b, d_ta, d_tb])
    big["w_out"] = g_w_out.reshape(wf["w_out"].shape)
    (d_ya, d_yb), theirs = _mm(
        "d_branches", [d_ta, d_tb], [w_a, w_b], "nt", [_sds((t, dp), F32), _sds((t, dp), BF16)], separate=True,
        rider=ex.pair(MIXER, big))
    ex.paired(MIXER, big, theirs)
    d_u, g_w_pool, d_pool_scale = _pool_bwd(d_ya, pooled, w_pool_b, small["pool_scale"])
    (d_q, d_k, d_v), landed = _attn_bwd(q, 0, kv, 0, n_pairs, d_yb, n_pairs, rider=ex.chip(LATE))
    ex.landed.update(zip(LATE, landed))
    d_proj = [(d_u, d_q), (d_k, d_v), d_ga, d_gb]
    big["w_in"], = _mm_tn("g_w_in", [h1], d_proj, tmm=512, stacked=True)
    ex.pair_now(("w_in",), big)
    (grad_x, d_norm_mix), landed = _mm(
        "d_h1", d_proj, [w_in[j] for j in range(N_CHIPS)], "nt", [_sds((t, d), F32)],
        extras=[x, small["norm_mix"], dx1], epilogue=lambda dh, xv, g, dres: through_norm(dh, xv, g, dres)[1:],
        sum_shapes=gain_sum, tm=512, rider=ex.chip(("w_in",)))
    ex.landed.update(zip(("w_in",), landed))
    small_g = {"norm_mix": d_norm_mix, "w_pool": g_w_pool, "pool_scale": d_pool_scale, "norm_ffn": d_norm_ffn,
               "norm_ple": d_norm_ple, "norm_final": d_norm_final}
    return grad_x, big, small_g, loss_row


def _pack_small(small_g, loss_row):
    parts, layout = [], []
    for name in SMALL + ("loss",):
        v = (loss_row if name == "loss" else small_g[name]).reshape(-1, LANES)
        pad = (-v.shape[0]) % 8
        if pad:
            v = jnp.concatenate([v, jnp.zeros((pad, LANES), F32)], axis=0)
        layout.append((name, sum(q.shape[0] for q in parts), v.shape[0]))
        parts.append(v)
    return jnp.concatenate(parts, axis=0), layout


def kernel(x, p, norm_mix, w_in, w_pool, pool_scale, w_branch_a, w_branch_b, w_out, norm_ffn, w_ffn_gate, w_ffn_up, w_ffn_down, norm_ple, w_ple_gate, w_ple_proj, norm_final, loss_target, m_norm_mix, m_w_in, m_w_pool, m_pool_scale, m_w_branch_a, m_w_branch_b, m_w_out, m_norm_ffn, m_w_ffn_gate, m_w_ffn_up, m_w_ffn_down, m_norm_ple, m_w_ple_gate, m_w_ple_proj, m_norm_final, v_norm_mix, v_w_in, v_w_pool, v_pool_scale, v_w_branch_a, v_w_branch_b, v_w_out, v_norm_ffn, v_w_ffn_gate, v_w_ffn_up, v_w_ffn_down, v_norm_ple, v_w_ple_gate, v_w_ple_proj, v_norm_final):
    given = dict(locals())
    order = ("norm_mix", "w_in", "w_pool", "pool_scale", "w_branch_a", "w_branch_b", "w_out", "norm_ffn", "w_ffn_gate",
             "w_ffn_up", "w_ffn_down", "norm_ple", "w_ple_gate", "w_ple_proj", "norm_final")
    t, d = x.shape[1], x.shape[2]
    def local(a, n):
        return jnp.swapaxes(a[0], 0, 1) if n in HELD_TRANSPOSED else a[0]

    def back(a, n):
        return (jnp.swapaxes(a, 0, 1) if n in HELD_TRANSPOSED else a)[None]

    shard = {n: local(given[n], n) for n in BIG}
    small = {"norm_mix": norm_mix, "w_pool": w_pool[0], "pool_scale": pool_scale, "norm_ffn": norm_ffn,
             "norm_ple": norm_ple, "norm_final": norm_final}

    place = jnp.stack([lax.axis_index("c"), 2 * lax.axis_index("x") + lax.axis_index("y")]).astype(jnp.int32)
    ex = _StepExchanges({n: shard[n].astype(BF16) for n in BIG}, place)
    grad_x, _, small_g, loss_row = _local_step(
        x.reshape(t, d), p.reshape(t, p.shape[-1]), loss_target.reshape(t, d), {}, small, ex)
    packed, layout = _pack_small(small_g, loss_row)
    filled, reduced = _finish_gradients(place, [ex.pair_sums[n] for n in BIG], [ex.landed[n] for n in BIG], packed)
    grads = dict(zip(BIG, filled))
    for name, start, rows in layout:
        if name == "loss":
            loss = jnp.sum(reduced[start:start + rows])
        else:
            n_el = small[name].size
            grads[name] = reduced[start:start + rows].reshape(-1)[:n_el]

    deltas, new_m, new_v = {}, {}, {}
    for n in order:
        if n in BIG:
            w, m, v = shard[n], local(given["m_" + n], n), local(given["v_" + n], n)
            dl, mn, vn = _adamw(f"adamw_{n}", w, grads[n], m, v)
            grads[n], deltas[n], new_m[n], new_v[n] = [back(a, n) for a in (grads[n], dl, mn, vn)]
        else:
            w, full = small[n], given[n].shape
            shape2 = (1, w.shape[0]) if w.ndim == 1 else (w.shape if w.ndim == 2 else (w.shape[0] * w.shape[1], w.shape[2]))
            dl, mn, vn = _adamw(f"adamw_{n}", w.reshape(shape2), grads[n].reshape(shape2),
                                given["m_" + n].reshape(shape2), given["v_" + n].reshape(shape2))
            grads[n], deltas[n], new_m[n], new_v[n] = [a.reshape(full) for a in (grads[n], dl, mn, vn)]

    return (loss, grad_x.reshape(x.shape), *[grads[n] for n in order], *[deltas[n] for n in order],
            *[new_m[n] for n in order], *[new_v[n] for n in order])
```

```python
import functools
import math

import jax
import jax.numpy as jnp
from jax import lax
from jax.experimental import pallas as pl
from jax.experimental.pallas import tpu as pltpu

F32 = jnp.float32
BF16 = jnp.bfloat16
MESH = pl.DeviceIdType.MESH

RMS_EPS = 1e-6
POOL_WINDOWS = (2, 4, 8, 16)
POOL_HALO = 16
HEAD_DIM = 64
LANES = 128
ATT_BLOCK = 256
ATT_CHAINS = 2
ATT_FWD_CHAINS = 8
ATT_CHUNK = 256
ATT_SLAB = 256
ATT_SCALE = 1.0 / math.sqrt(HEAD_DIM)
LOG2_E = 1.4426950408889634
ATT_EXIT_BELOW = -150.5
ADAM_LR, ADAM_B1, ADAM_B2, ADAM_EPS, ADAM_WD, ADAM_STEP = 0.001, 0.9, 0.999, 1e-08, 0.01, 10
V7X_VMEM_LIMIT_BYTES = 56 * 1024 * 1024
N_CHIPS = 4
N_DEV = 8


def _params(*semantics):
    return pltpu.CompilerParams(dimension_semantics=semantics, vmem_limit_bytes=V7X_VMEM_LIMIT_BYTES)


def _sigmoid(z):
    return 0.5 * jnp.tanh(0.5 * z) + 0.5


def _tiled_spec(shape, tm, tn, n_total, at):
    rows, width = shape
    if rows == 1:
        if width == n_total:
            return pl.BlockSpec((1, tn), at(lambda i, j: (0, j)))
        return pl.BlockSpec((1, width), at(lambda i, j: (0, 0)))
    if width == n_total:
        return pl.BlockSpec((tm, tn), at(lambda i, j: (i, j)))
    assert tn == n_total, "an operand narrower than the output needs whole output rows per tile"
    return pl.BlockSpec((tm, width), at(lambda i, j: (i, 0)))


def _column_pieces(operands):
    pieces = [tuple(a) if isinstance(a, (tuple, list)) else (a,) for a in operands]
    return [p for ps in pieces for p in ps], [len(ps) for ps in pieces]


def _load_bf16(refs, counts):
    tiles, k = [], 0
    for n in counts:
        parts = [r[...] for r in refs[k:k + n]]
        parts = [t if t.dtype == BF16 else t.astype(BF16) for t in parts]
        tiles.append(parts[0] if n == 1 else jnp.concatenate(parts, axis=1))
        k += n
    return tiles


def _mm(name, a_list, b_list, mode, out_shapes, epilogue=None, extras=(), tm=1024, tn=None, separate=False,
        sum_shapes=(), rider=None, wholes=()):
    flat_a, counts = _column_pieces(a_list)
    m_total = flat_a[0].shape[0]
    n_total = b_list[0].shape[1] if mode == "nn" else b_list[0].shape[0]
    tn = n_total if tn is None else tn
    tm = min(tm, m_total)
    assert m_total % tm == 0 and n_total % tn == 0 and (not sum_shapes or tn == n_total)
    n_a, n_b, n_extra, n_out = len(counts), len(b_list), len(extras), len(out_shapes)
    assert n_a in (1, n_b)
    dims = (((1,), (0,)), ((), ())) if mode == "nn" else (((1,), (1,)), ((), ()))
    with_rider = rider is not None
    rider = rider or _NoRider()
    grid = (n_total // tn, m_total // tm)

    def at(index):
        return lambda j, i: index(i, j)

    def body(*refs):
        ins, o_refs, _, riding = rider.split(refs, len(flat_a) + n_b + n_extra + len(wholes), n_out + len(sum_shapes))
        a_refs, b_refs = ins[:len(flat_a)], ins[len(flat_a):len(flat_a) + n_b]
        e_refs, w_refs = ins[len(flat_a) + n_b:len(flat_a) + n_b + n_extra], ins[len(flat_a) + n_b + n_extra:]
        at_first = (pl.program_id(0) == 0) & (pl.program_id(1) == 0)
        at_last = (pl.program_id(0) == grid[0] - 1) & (pl.program_id(1) == grid[1] - 1)
        top, bottom = rider.at_steps(riding, at_first, at_first, at_last)
        top()
        lefts = _load_bf16(a_refs, counts)
        products = [lax.dot_general(lefts[s % n_a], b_refs[s][...], dims, preferred_element_type=F32)
                    for s in range(n_b)]
        if not separate:
            products = [functools.reduce(lambda p, r: p + r, products)]
        extra_tiles = [e[...].astype(F32) for e in e_refs]
        outs = products if epilogue is None else epilogue(*products, *extra_tiles, *[w[...] for w in w_refs])
        for o_ref, o in zip(o_refs[:n_out], outs[:n_out]):
            o_ref[...] = o.astype(o_ref.dtype)
        if sum_shapes:
            @pl.when(pl.program_id(1) == 0)
            def _():
                for s_ref in o_refs[n_out:]:
                    s_ref[...] = jnp.zeros_like(s_ref)

            for s_ref, s in zip(o_refs[n_out:], outs[n_out:]):
                s_ref[...] += s
        bottom()

    once = dict(pipeline_mode=pl.Buffered(1)) if tn == n_total else {}
    in_specs = [pl.BlockSpec((tm, a.shape[1]), at(lambda i, j: (i, 0))) for a in flat_a]
    if mode == "nn":
        in_specs += [pl.BlockSpec((b.shape[0], tn), at(lambda i, j: (0, j)), **once) for b in b_list]
    else:
        in_specs += [pl.BlockSpec((tn, b.shape[1]), at(lambda i, j: (j, 0)), **once) for b in b_list]
    in_specs += [_tiled_spec(e.shape, tm, tn, n_total, at) for e in extras]
    in_specs += [pl.BlockSpec(w.shape, lambda j, i: (0, 0), pipeline_mode=pl.Buffered(1)) for w in wholes]
    out_specs = [_tiled_spec(o.shape, tm, tn, n_total, at) for o in out_shapes]
    out_specs += [pl.BlockSpec(s.shape, at(lambda i, j: (0, 0))) for s in sum_shapes]
    semantics = ("arbitrary", "arbitrary") if sum_shapes or rider.operands else ("parallel", "parallel")
    res = pl.pallas_call(
        body, name=name, grid=grid, in_specs=in_specs + [ANY] * len(rider.operands),
        out_specs=out_specs + [ANY] * len(rider.out_shapes),
        out_shape=list(out_shapes) + list(sum_shapes) + list(rider.out_shapes), scratch_shapes=list(rider.scratch),
        compiler_params=_params(*semantics),
    )(*flat_a, *b_list, *extras, *wholes, *rider.operands)
    n_own = len(out_shapes) + len(sum_shapes)
    return (res[:n_own], res[n_own:]) if with_rider else res


def _mm_tn(name, a_list, b_list, tmm=1024, stacked=False, k_blocks=1):
    flat_b, counts = _column_pieces(b_list)
    n_a, n_b = len(a_list), len(counts)
    n_prod = max(n_a, n_b)
    m_total = a_list[0].shape[0]
    ks = [a_list[s % n_a].shape[1] for s in range(n_prod)]
    widths = [sum(p.shape[1] for p in flat_b[sum(counts[:s]):sum(counts[:s + 1])]) for s in range(n_b)]
    widths = [widths[s % n_b] for s in range(n_prod)]
    tmm = min(tmm, m_total)
    assert m_total % tmm == 0 and all(k % k_blocks == 0 for k in ks)
    assert n_a in (1, n_prod) and n_b in (1, n_prod) and not (stacked and n_a > 1)

    def body(*refs):
        a_refs, b_refs, o_refs = refs[:n_a], refs[n_a:n_a + len(flat_b)], refs[n_a + len(flat_b):]

        @pl.when(pl.program_id(1) == 0)
        def _():
            for o_ref in o_refs:
                o_ref[...] = jnp.zeros_like(o_ref)

        lefts, rights = _load_bf16(a_refs, [1] * n_a), _load_bf16(b_refs, counts)
        for s in range(n_prod):
            product = lax.dot_general(lefts[s % n_a], rights[s % n_b], (((0,), (0,)), ((), ())),
                                      preferred_element_type=F32)
            if stacked:
                o_refs[0][s] += product
            else:
                o_refs[s][...] += product

    in_specs = [pl.BlockSpec((tmm, a.shape[1] // k_blocks), lambda kb, m: (m, kb)) for a in a_list]
    in_specs += [pl.BlockSpec((tmm, b.shape[1]), lambda kb, m: (m, 0)) for b in flat_b]
    if stacked:
        out_shape = [jax.ShapeDtypeStruct((n_prod, ks[0], widths[0]), F32)]
        out_specs = [pl.BlockSpec((n_prod, ks[0] // k_blocks, widths[0]), lambda kb, m: (0, kb, 0))]
    else:
        out_shape = [jax.ShapeDtypeStruct((k, w), F32) for k, w in zip(ks, widths)]
        out_specs = [pl.BlockSpec((k // k_blocks, w), lambda kb, m: (kb, 0)) for k, w in zip(ks, widths)]
    return pl.pallas_call(
        body, name=name, grid=(k_blocks, m_total // tmm), in_specs=in_specs, out_specs=out_specs, out_shape=out_shape,
        compiler_params=_params("arbitrary", "arbitrary"),
    )(*a_list, *flat_b)


def _rows(name, fn, ins, tile_outs, sum_outs=(), tr=512, rider=None):
    t_total = max(a.shape[0] for a in ins)
    tr = min(tr, t_total)
    assert t_total % tr == 0
    n_in, n_tile = len(ins), len(tile_outs)
    rider = rider or _NoRider()
    n_steps = t_total // tr

    def body(*refs):
        own_ins, own_outs, _, riding = rider.split(refs, n_in, n_tile + len(sum_outs))
        step = pl.program_id(0)
        top, bottom = rider.at_steps(riding, step == 0, step == n_steps - 1, step == n_steps - 1)
        top()
        refs = tuple(own_ins) + tuple(own_outs)
        outs = fn(*[r[...].astype(F32) for r in refs[:n_in]])
        for o_ref, o in zip(refs[n_in:n_in + n_tile], outs[:n_tile]):
            o_ref[...] = o.astype(o_ref.dtype)
        if sum_outs:
            @pl.when(pl.program_id(0) == 0)
            def _():
                for s_ref in refs[n_in + n_tile:]:
                    s_ref[...] = jnp.zeros_like(s_ref)

            for s_ref, s in zip(refs[n_in + n_tile:], outs[n_tile:]):
                s_ref[...] += s
        bottom()

    def spec(shape):
        if shape[0] == 1:
            return pl.BlockSpec(shape, lambda i: (0, 0))
        return pl.BlockSpec((tr, shape[1]), lambda i: (i, 0))

    return pl.pallas_call(
        body, name=name, grid=(n_steps,), in_specs=[spec(a.shape) for a in ins] + [ANY] * len(rider.operands),
        out_specs=[spec(o.shape) for o in tile_outs] + [spec(s.shape) for s in sum_outs] + [ANY] * len(rider.out_shapes),
        out_shape=list(tile_outs) + list(sum_outs) + list(rider.out_shapes), scratch_shapes=list(rider.scratch),
        compiler_params=_params("arbitrary" if sum_outs or rider.operands else "parallel"),
    )(*ins, *rider.operands)


def _norm_fwd(name, x, gain, rider=None):
    def fn(xv, g):
        inv = lax.rsqrt(jnp.mean(xv * xv, axis=-1, keepdims=True) + RMS_EPS)
        return (xv * inv * g,)

    res = _rows(name, fn, [x, gain], [jax.ShapeDtypeStruct(x.shape, BF16)], rider=rider)
    return res[0], res[1:]


def _rms_norm_bwd(dh, xv, g):
    inv = lax.rsqrt(jnp.mean(xv * xv, axis=-1, keepdims=True) + RMS_EPS)
    xn = xv * inv
    dxn = dh * g
    return inv * (dxn - xn * jnp.mean(dxn * xn, axis=-1, keepdims=True)), jnp.sum(dh * xn, axis=0, keepdims=True)


def _ple_and_loss(gv, pv, x2v, tv, g_final, g_ple, w_pg):
    d = x2v.shape[1]
    s = _sigmoid(gv)
    xv = x2v + s * pv
    inv = lax.rsqrt(jnp.mean(xv * xv, axis=-1, keepdims=True) + RMS_EPS)
    err = xv * inv * g_final - tv
    dx3, d_final = _rms_norm_bwd(err * (1.0 / d), xv, g_final)
    d_pp, d_gp = dx3 * s, dx3 * pv * s * (1.0 - s)
    dh3 = lax.dot_general(d_gp.astype(BF16), w_pg, (((1,), (1,)), ((), ())), preferred_element_type=F32)
    dx2, d_ple = _rms_norm_bwd(dh3, x2v, g_ple)
    dx2 = dx2 + dx3
    return dx2, dx2, d_pp, d_gp, d_final, (0.5 / d) * jnp.sum(err * err, axis=0, keepdims=True), d_ple


def _window_counts(t_pos, w):
    return jnp.minimum(t_pos + 1, w).astype(F32)


def _pool_fwd(u, w_pool, scale, tr=512):
    t_total, width = u.shape
    tr = min(tr, t_total)
    n_groups = len(POOL_WINDOWS)
    gdim = width // n_groups
    ext = tr + POOL_HALO

    def body(u_ref, halo_ref, w_ref, s_ref, pooled_ref, ya_ref):
        i = pl.program_id(0)
        halo = jnp.where(i == 0, 0.0, halo_ref[...])
        t_pos = i * tr + lax.broadcasted_iota(jnp.int32, (tr, 1), 0)
        for g, w in enumerate(POOL_WINDOWS):
            cols = slice(g * gdim, (g + 1) * gdim)
            main = u_ref[:, cols]
            win = jnp.concatenate([halo[:, cols], main], axis=0)
            span = 1
            while span < w:
                win = win + pltpu.roll(win, span, 0)
                span *= 2
            pooled = win[POOL_HALO:, :] * (1.0 / _window_counts(t_pos, w)) - main
            pooled_b = pooled.astype(BF16)
            pooled_ref[:, cols] = pooled_b
            mixed = jnp.dot(pooled_b, w_ref[g], preferred_element_type=F32)
            ya_ref[:, cols] = (mixed * s_ref[:, cols]).astype(BF16)

    hb = tr // POOL_HALO
    return pl.pallas_call(
        body, name="pool_fwd", grid=(t_total // tr,),
        in_specs=[pl.BlockSpec((tr, width), lambda i: (i, 0)),
                  pl.BlockSpec((POOL_HALO, width), lambda i: (jnp.maximum(i * hb - 1, 0), 0)),
                  pl.BlockSpec((n_groups, gdim, gdim), lambda i: (0, 0, 0)),
                  pl.BlockSpec((1, width), lambda i: (0, 0))],
        out_specs=[pl.BlockSpec((tr, width), lambda i: (i, 0)), pl.BlockSpec((tr, width), lambda i: (i, 0))],
        out_shape=[jax.ShapeDtypeStruct(u.shape, BF16), jax.ShapeDtypeStruct(u.shape, BF16)],
        compiler_params=_params("parallel"),
    )(u, u, w_pool, scale)


def _pool_bwd(dya, pooled, w_pool, scale, tr=512):
    t_total, width = dya.shape
    tr = min(tr, t_total)
    n_groups = len(POOL_WINDOWS)
    gdim = width // n_groups
    ext = tr + POOL_HALO
    n_tiles = t_total // tr

    def body(d_ref, halo_ref, p_ref, w_ref, s_ref, du_ref, dw_ref, ds_ref):
        i = pl.program_id(0)

        @pl.when(i == 0)
        def _():
            dw_ref[...] = jnp.zeros_like(dw_ref)
            ds_ref[...] = jnp.zeros_like(ds_ref)

        halo = jnp.where(i == n_tiles - 1, 0.0, halo_ref[...])
        t_pos = i * tr + lax.broadcasted_iota(jnp.int32, (ext, 1), 0)
        for g, w in enumerate(POOL_WINDOWS):
            cols = slice(g * gdim, (g + 1) * gdim)
            sc = s_ref[:, cols]
            d_main = d_ref[:, cols]
            pooled_b = p_ref[:, cols]
            mixed = jnp.dot(pooled_b, w_ref[g], preferred_element_type=F32)
            ds_ref[:, cols] += jnp.sum(d_main * mixed, axis=0, keepdims=True)
            dmix = (jnp.concatenate([d_main, halo[:, cols]], axis=0) * sc).astype(BF16)
            dw_ref[g] += lax.dot_general(pooled_b, dmix[:tr, :], (((0,), (0,)), ((), ())),
                                         preferred_element_type=F32)
            dpool = lax.dot_general(dmix, w_ref[g], (((1,), (1,)), ((), ())), preferred_element_type=F32)
            win = dpool * (1.0 / _window_counts(t_pos, w))
            span = 1
            while span < w:
                win = win + pltpu.roll(win, ext - span, 0)
                span *= 2
            du_ref[:, cols] = (win[:tr, :] - dpool[:tr, :]).astype(BF16)

    hb = tr // POOL_HALO
    last_halo = t_total // POOL_HALO - 1
    return pl.pallas_call(
        body, name="pool_bwd", grid=(n_tiles,),
        in_specs=[pl.BlockSpec((tr, width), lambda i: (i, 0)),
                  pl.BlockSpec((POOL_HALO, width), lambda i: (jnp.minimum((i + 1) * hb, last_halo), 0)),
                  pl.BlockSpec((tr, width), lambda i: (i, 0)),
                  pl.BlockSpec((n_groups, gdim, gdim), lambda i: (0, 0, 0)),
                  pl.BlockSpec((1, width), lambda i: (0, 0))],
        out_specs=[pl.BlockSpec((tr, width), lambda i: (i, 0)),
                   pl.BlockSpec((n_groups, gdim, gdim), lambda i: (0, 0, 0)),
                   pl.BlockSpec((1, width), lambda i: (0, 0))],
        out_shape=[jax.ShapeDtypeStruct(dya.shape, BF16), jax.ShapeDtypeStruct((n_groups, gdim, gdim), F32),
                   jax.ShapeDtypeStruct((1, width), F32)],
        compiler_params=_params("arbitrary"),
    )(dya, dya, pooled, w_pool, scale)


def _head_masks():
    lane = lax.broadcasted_iota(jnp.int32, (1, LANES), 1)
    return lane < HEAD_DIM


def _stack_heads(tile, first):
    zero = jnp.zeros_like(tile)
    return jnp.concatenate([jnp.where(first, tile, zero), jnp.where(first, zero, tile)], axis=0)


def _causal_mask(t_pos, k_start):
    col = lax.broadcasted_iota(jnp.int32, (1, 2 * ATT_SLAB), 1)
    return k_start + (col & (ATT_SLAB - 1)) < t_pos


def _slab_scores(q, kd, mask):
    z2 = lax.dot_general(q, kd, (((1,), (1,)), ((), ())), preferred_element_type=F32) * LOG2_E
    log_hit = jnp.minimum(z2, 0.0) - jnp.log2(1.0 + jnp.exp2(-jnp.abs(z2)))
    log_fail = log_hit - z2
    return log_hit, (log_fail if mask is None else jnp.where(mask, log_fail, 0.0))


def _weights(log_hit, suffix, mask):
    arg = log_hit + suffix
    return jnp.exp2(arg if mask is None else jnp.where(mask, arg, -1e30))


def _tri(upper):
    r = lax.broadcasted_iota(jnp.int32, (ATT_CHUNK, ATT_CHUNK), 0)
    c = lax.broadcasted_iota(jnp.int32, (ATT_CHUNK, ATT_CHUNK), 1)
    return jnp.where(r > c if upper else r < c, 1.0, 0.0).astype(BF16)


def _tri_spec():
    return pl.BlockSpec((ATT_CHUNK, ATT_CHUNK), lambda h, i: (0, 0), pipeline_mode=pl.Buffered(1))


def _scan_chunk(v, tri):
    return jnp.dot(v.astype(BF16), tri, preferred_element_type=F32)


def _lane_bcast(col):
    return jnp.broadcast_to(col, (col.shape[0], LANES))


def _scan_slab(v, tri, carries, from_right):
    n_chunks = ATT_SLAB // ATT_CHUNK
    edge = 0 if from_right else ATT_CHUNK - 1
    parts, new_carries = [None] * (2 * n_chunks), []
    for head in range(2):
        run = carries[head]
        for c in (reversed(range(n_chunks)) if from_right else range(n_chunks)):
            lo_col = head * ATT_SLAB + c * ATT_CHUNK
            vc = v[:, lo_col:lo_col + ATT_CHUNK]
            sc = _scan_chunk(vc, tri)
            parts[head * n_chunks + c] = sc + jnp.concatenate([run] * (ATT_CHUNK // LANES), axis=1)
            run = run + _lane_bcast(sc[:, edge:edge + 1] + vc[:, edge:edge + 1])
        new_carries.append(run)
    return jnp.concatenate(parts, axis=1), new_carries


def _fold_heads(stacked, first):
    s = stacked.shape[0] // 2
    return jnp.where(first, stacked[:s], stacked[s:])


class _NoRider:
    operands, out_shapes, scratch = (), (), ()

    def split(self, refs, n_base_in, n_base_out):
        n_in, n_out, n_sem = len(self.operands), len(self.out_shapes), len(self.scratch)
        a = n_base_in + n_in
        b = a + n_base_out + n_out
        mine = (refs[n_base_in:a], refs[a + n_base_out:b], refs[b:b + n_sem])
        return refs[:n_base_in], refs[a:a + n_base_out], refs[b + n_sem:], mine

    def start(self, ins, outs, sems):
        pass

    def relay(self, ins, outs, sems):
        pass

    def finish(self, ins, outs, sems):
        pass

    def at_steps(self, refs, first_step, relay_step, last_step):
        if not self.operands:
            return (lambda: None), (lambda: None)

        def top():
            pl.when(first_step)(lambda: self.start(*refs))
            pl.when(relay_step)(lambda: self.relay(*refs))

        return top, lambda: pl.when(last_step)(lambda: self.finish(*refs))


def _attn_fwd(q_src, q_col, kv_src, k_col, v_col, n_pairs=4, rider=_NoRider()):
    t_total = q_src.shape[0]
    blk = ATT_BLOCK
    n_chains = ATT_FWD_CHAINS if t_total % (ATT_FWD_CHAINS * blk) == 0 else ATT_CHAINS
    n_steps = t_total // (n_chains * blk)
    assert t_total % ATT_SLAB == 0 and ATT_SLAB == ATT_BLOCK

    def body(*refs):
        (q_ref, k_ref, v_ref, suffix_ref), (o_ref,), _, riding = rider.split(refs, 4, 1)
        h, ii = pl.program_id(0), pl.program_id(1)
        top, bottom = rider.at_steps(riding, (h == 0) & (ii == 0), (h == n_pairs - 1) & (ii == 0),
                                     (h == n_pairs - 1) & (ii == n_steps - 1))
        top()
        first = _head_masks()
        suffix_tri = suffix_ref[...]
        blocks = [n_chains * ii + c for c in range(n_chains)]
        qs = [q_ref[c * blk:(c + 1) * blk, :] * ATT_SCALE for c in range(n_chains)]
        t_pos = [b * blk + lax.broadcasted_iota(jnp.int32, (blk, 1), 0) for b in blocks]

        def one(c, t, chain, on_diagonal):
            _, acc, right_a, right_b = chain
            k_start = pl.multiple_of((blocks[c] - t) * ATT_SLAB, ATT_SLAB)
            kd = _stack_heads(k_ref[pl.ds(k_start, ATT_SLAB), :], first)
            vd = _stack_heads(v_ref[pl.ds(k_start, ATT_SLAB), :], first)
            mask = _causal_mask(t_pos[c], k_start) if on_diagonal else None
            log_hit, log_fail = _slab_scores(qs[c], kd, mask)
            suffix, (right_a, right_b) = _scan_slab(log_fail, suffix_tri, (right_a, right_b), from_right=True)
            a = _weights(log_hit, suffix, mask).astype(BF16)
            acc = acc + jnp.dot(a, vd, preferred_element_type=F32)
            return jnp.max(jnp.maximum(right_a, right_b)), acc, right_a, right_b

        def step(state, on_diagonal):
            t, chains = state
            return t + 1, tuple(one(c, t, chains[c], on_diagonal) for c in range(n_chains))

        def more(state):
            t, chains = state
            return (t <= blocks[0]) & (functools.reduce(jnp.maximum, [ch[0] for ch in chains]) > ATT_EXIT_BELOW)

        zero = jnp.zeros((blk, LANES), F32)
        state = step((0, ((jnp.float32(0.0), zero, zero, zero),) * n_chains), on_diagonal=True)
        t, chains = lax.while_loop(more, functools.partial(step, on_diagonal=False), state)
        for c in range(n_chains):
            chain = chains[c]
            if c:
                _, chain = lax.while_loop(
                    lambda s, c=c: (s[0] <= blocks[c]) & (s[1][0] > ATT_EXIT_BELOW),
                    lambda s, c=c: (s[0] + 1, one(c, s[0], s[1], False)), (t, chain))
            o_ref[c * blk:(c + 1) * blk, :] = chain[1].astype(BF16)
        bottom()

    rows = n_chains * blk
    res = pl.pallas_call(
        body, name="attn_fwd", grid=(n_pairs, n_steps),
        in_specs=[pl.BlockSpec((rows, LANES), lambda h, i: (i, q_col + h)),
                  pl.BlockSpec((t_total, LANES), lambda h, i: (0, k_col + h)),
                  pl.BlockSpec((t_total, LANES), lambda h, i: (0, v_col + h)), _tri_spec()] + [ANY] * len(rider.operands),
        out_specs=[pl.BlockSpec((rows, LANES), lambda h, i: (i, h))] + [ANY] * len(rider.out_shapes),
        out_shape=[jax.ShapeDtypeStruct((t_total, n_pairs * LANES), BF16)] + list(rider.out_shapes),
        scratch_shapes=list(rider.scratch),
        compiler_params=_params("arbitrary", "arbitrary"),
    )(q_src, kv_src, kv_src, _tri(upper=True), *rider.operands)
    return res[0], res[1:]


def _attn_bwd(q_src, q_col, kv_src, k_col, v_col, dy, n_pairs=4, rider=_NoRider()):
    t_total = q_src.shape[0]
    blk = ATT_BLOCK
    n_steps = t_total // (ATT_CHAINS * blk)
    n_slabs = t_total // ATT_SLAB
    assert t_total % ATT_SLAB == 0 and ATT_SLAB == ATT_BLOCK

    def body(*refs):
        ins, (dq_ref, dk_ref, dv_ref), (g_s, dk_acc, dv_acc), riding = rider.split(refs, 6, 3)
        q_ref, dy_ref, k_ref, v_ref, suffix_ref, prefix_ref = ins
        h, ii = pl.program_id(0), pl.program_id(1)
        top, bottom = rider.at_steps(riding, (h == 0) & (ii == 0), (h == n_pairs - 1) & (ii == 0),
                                     (h == n_pairs - 1) & (ii == n_steps - 1))
        top()

        @pl.when(ii == 0)
        def _():
            dk_acc[...] = jnp.zeros_like(dk_acc)
            dv_acc[...] = jnp.zeros_like(dv_acc)

        first = _head_masks()
        suffix_tri = suffix_ref[...]
        prefix_tri = prefix_ref[...]
        blocks = [ATT_CHAINS * ii + c for c in range(ATT_CHAINS)]
        rows = [slice(c * blk, (c + 1) * blk) for c in range(ATT_CHAINS)]
        qs = [q_ref[r, :] * ATT_SCALE for r in rows]
        dys = [dy_ref[r, :] for r in rows]
        t_pos = [b * blk + lax.broadcasted_iota(jnp.int32, (blk, 1), 0) for b in blocks]

        def one1(c, t, chain, on_diagonal):
            _, right_a, right_b = chain
            slab = blocks[c] - t
            k_start = pl.multiple_of(slab * ATT_SLAB, ATT_SLAB)
            kd = _stack_heads(k_ref[pl.ds(k_start, ATT_SLAB), :], first)
            vd = _stack_heads(v_ref[pl.ds(k_start, ATT_SLAB), :], first)
            mask = _causal_mask(t_pos[c], k_start) if on_diagonal else None
            log_hit, log_fail = _slab_scores(qs[c], kd, mask)
            suffix, (right_a, right_b) = _scan_slab(log_fail, suffix_tri, (right_a, right_b), from_right=True)
            a = _weights(log_hit, suffix, mask)
            da = lax.dot_general(dys[c], vd, (((1,), (1,)), ((), ())), preferred_element_type=F32)
            g_s[c, slab] = (da * a).astype(BF16)
            dv_acc[pl.ds(k_start, ATT_SLAB), :] += _fold_heads(lax.dot_general(
                a.astype(BF16), dys[c], (((0,), (0,)), ((), ())), preferred_element_type=F32), first)
            return jnp.max(jnp.maximum(right_a, right_b)), right_a, right_b

        def step1(state, on_diagonal):
            t, chains = state
            return t + 1, tuple(one1(c, t, chains[c], on_diagonal) for c in range(ATT_CHAINS))

        def more(state):
            t, chains = state
            return (t <= blocks[0]) & (functools.reduce(jnp.maximum, [ch[0] for ch in chains]) > ATT_EXIT_BELOW)

        zero = jnp.zeros((blk, LANES), F32)
        state = step1((0, ((jnp.float32(0.0), zero, zero),) * ATT_CHAINS), on_diagonal=True)
        joint, chains = lax.while_loop(more, functools.partial(step1, on_diagonal=False), state)
        done = [joint]
        for c in range(1, ATT_CHAINS):
            done.append(lax.while_loop(
                lambda s, c=c: (s[0] <= blocks[c]) & (s[1][0] > ATT_EXIT_BELOW),
                lambda s, c=c: (s[0] + 1, one1(c, s[0], s[1], False)), (joint, chains[c]))[0])

        def one2(c, t, carry, on_diagonal):
            dq, left_a, left_b = carry
            slab = blocks[c] - t
            k_start = pl.multiple_of(slab * ATT_SLAB, ATT_SLAB)
            kd = _stack_heads(k_ref[pl.ds(k_start, ATT_SLAB), :], first)
            g = g_s[c, slab]
            z2 = lax.dot_general(qs[c], kd, (((1,), (1,)), ((), ())), preferred_element_type=F32) * LOG2_E
            sig = 1.0 / (1.0 + jnp.exp2(-z2))
            prefix, (left_a, left_b) = _scan_slab(g, prefix_tri, (left_a, left_b), from_right=False)
            dz = g * (1.0 - sig) - sig * prefix
            if on_diagonal:
                dz = jnp.where(_causal_mask(t_pos[c], k_start), dz, 0.0)
            dz = dz.astype(BF16)
            dq = dq + jnp.dot(dz, kd, preferred_element_type=F32)
            dk_acc[pl.ds(k_start, ATT_SLAB), :] += _fold_heads(lax.dot_general(
                dz, qs[c], (((0,), (0,)), ((), ())), preferred_element_type=F32), first)
            return dq, left_a, left_b

        carries = [(zero, zero, zero)]
        for c in range(1, ATT_CHAINS):
            carries.append(lax.fori_loop(
                0, done[c] - joint, lambda n, carry, c=c: one2(c, done[c] - 1 - n, carry, False), (zero, zero, zero)))
        carries = lax.fori_loop(
            0, joint - 1,
            lambda n, cs: tuple(one2(c, joint - 1 - n, cs[c], False) for c in range(ATT_CHAINS)), tuple(carries))
        for c in range(ATT_CHAINS):
            dq_ref[rows[c], :] = (one2(c, 0, carries[c], True)[0] * ATT_SCALE).astype(BF16)

        @pl.when(ii == n_steps - 1)
        def _():
            dk_ref[...] = dk_acc[...].astype(BF16)
            dv_ref[...] = dv_acc[...].astype(BF16)

        bottom()

    out = jax.ShapeDtypeStruct((t_total, n_pairs * LANES), BF16)
    n_rows = ATT_CHAINS * blk
    whole = dict(pipeline_mode=pl.Buffered(1))
    res = pl.pallas_call(
        body, name="attn_bwd", grid=(n_pairs, n_steps),
        in_specs=[pl.BlockSpec((n_rows, LANES), lambda h, i: (i, q_col + h)),
                  pl.BlockSpec((n_rows, LANES), lambda h, i: (i, h)),
                  pl.BlockSpec((t_total, LANES), lambda h, i: (0, k_col + h), **whole),
                  pl.BlockSpec((t_total, LANES), lambda h, i: (0, v_col + h), **whole), _tri_spec(), _tri_spec()]
        + [ANY] * len(rider.operands),
        out_specs=[pl.BlockSpec((n_rows, LANES), lambda h, i: (i, h)),
                   pl.BlockSpec((t_total, LANES), lambda h, i: (0, h)),
                   pl.BlockSpec((t_total, LANES), lambda h, i: (0, h))] + [ANY] * len(rider.out_shapes),
        out_shape=[out, out, out] + list(rider.out_shapes),
        scratch_shapes=list(rider.scratch) + [pltpu.VMEM((ATT_CHAINS, n_slabs, blk, 2 * ATT_SLAB), BF16),
                                              pltpu.VMEM((t_total, LANES), F32), pltpu.VMEM((t_total, LANES), F32)],
        compiler_params=_params("arbitrary", "arbitrary"),
    )(q_src, dy, kv_src, kv_src, _tri(upper=True), _tri(upper=False), *rider.operands)
    return res[:3], res[3:]


def _adamw(name, w, g, m, v):
    def fn(wv, gv, mv, vv):
        mn = ADAM_B1 * mv + (1.0 - ADAM_B1) * gv
        vn = ADAM_B2 * vv + (1.0 - ADAM_B2) * (gv * gv)
        m_hat = mn / (1.0 - ADAM_B1 ** ADAM_STEP)
        v_hat = vn / (1.0 - ADAM_B2 ** ADAM_STEP)
        return -ADAM_LR * (m_hat / (jnp.sqrt(v_hat) + ADAM_EPS) + ADAM_WD * wv), mn, vn

    rows = w.shape[0]
    tr = _row_tile(rows, 256)
    shp = jax.ShapeDtypeStruct(w.shape, F32)
    if rows == 1:
        def body(w_ref, g_ref, m_ref, v_ref, d_ref, mo_ref, vo_ref):
            d, mn, vn = fn(w_ref[...], g_ref[...], m_ref[...], v_ref[...])
            d_ref[...], mo_ref[...], vo_ref[...] = d, mn, vn

        return pl.pallas_call(body, name=name, out_shape=[shp, shp, shp])(w, g, m, v)
    return _rows(name, fn, [w, g, m, v], [shp, shp, shp], tr=tr)


def _place():
    return lax.axis_index("x"), lax.axis_index("y"), lax.axis_index("c")


def _other_chips(x, y):
    return [(1 - x, y), (x, 1 - y), (1 - x, 1 - y)]


ANY = pl.BlockSpec(memory_space=pl.ANY)


def _remote(src, dst, send_sem, recv_sem, to):
    return pltpu.make_async_remote_copy(src_ref=src, dst_ref=dst, send_sem=send_sem, recv_sem=recv_sem,
                                        device_id=to, device_id_type=MESH)


class _WeightGather(_NoRider):
    def __init__(self, shards):
        n_w = len(shards)
        self.operands = list(shards)
        self.out_shapes = [jax.ShapeDtypeStruct((N_CHIPS,) + s.shape, s.dtype) for s in shards]
        self.scratch = [pltpu.SemaphoreType.DMA((3, n_w))] * 4 + [pltpu.SemaphoreType.DMA((n_w,))] * 2

    def _copies(self, ins, outs, sems):
        send_sems, recv_sems, relay_send, relay_recv, own_send, own_recv = sems
        x, y, c = _place()
        my_chip, sibling = 2 * x + y, (x, y, 1 - c)
        n_w = len(ins)

        def half(w, chip, core):
            h = self.operands[w].shape[0] // 2
            return outs[w].at[chip, pl.ds(core * h, h)]

        own = [_remote(ins[w], outs[w].at[my_chip], own_send.at[w], own_recv.at[w], sibling) for w in range(n_w)]
        sends, landed, relays, relayed = [], [], [], []
        for p, (ox, oy) in enumerate(_other_chips(x, y)):
            for w in range(n_w):
                h = self.operands[w].shape[0] // 2
                sends.append(_remote(ins[w].at[pl.ds(c * h, h)], half(w, my_chip, c), send_sems.at[p, w],
                                     recv_sems.at[p, w], (ox, oy, c)))
                here = half(w, 2 * ox + oy, c)
                landed.append(_remote(here, here, send_sems.at[p, w], recv_sems.at[p, w], (ox, oy, c)))
                relays.append(_remote(here, here, relay_send.at[p, w], relay_recv.at[p, w], sibling))
                there = half(w, 2 * ox + oy, 1 - c)
                relayed.append(_remote(there, there, relay_send.at[p, w], relay_recv.at[p, w], sibling))
        return own, sends, landed, relays, relayed

    def start(self, ins, outs, sems):
        own, sends, _, _, _ = self._copies(ins, outs, sems)
        for cp in own + sends:
            cp.start()

    def relay(self, ins, outs, sems):
        _, _, landed, relays, _ = self._copies(ins, outs, sems)
        for arrival, cp in zip(landed, relays):
            arrival.wait_recv()
            cp.start()

    def finish(self, ins, outs, sems):
        own, sends, _, relays, relayed = self._copies(ins, outs, sems)
        for arrival in relayed:
            arrival.wait_recv()
        for cp in sends + relays:
            cp.wait_send()
        for cp in own:
            cp.wait()


class _ChipExchange(_NoRider):
    def __init__(self, pair_sums):
        n_w = len(pair_sums)
        self.operands = list(pair_sums)
        self.out_shapes = [jax.ShapeDtypeStruct((3,) + s.shape[1:], s.dtype) for s in pair_sums]
        self.scratch = [pltpu.SemaphoreType.DMA((3, n_w))] * 2

    def _copies(self, ins, outs, sems):
        send_sems, recv_sems = sems
        x, y, c = _place()
        return [_remote(ins[w].at[2 * ox + oy], outs[w].at[p], send_sems.at[p, w], recv_sems.at[p, w], (ox, oy, c))
                for p, (ox, oy) in enumerate(_other_chips(x, y)) for w in range(len(ins))]

    def start(self, ins, outs, sems):
        for cp in self._copies(ins, outs, sems):
            cp.start()

    def finish(self, ins, outs, sems):
        for cp in self._copies(ins, outs, sems):
            cp.wait()


class _PairExchange(_NoRider):
    def __init__(self, grads):
        n_w = len(grads)
        self.operands = list(grads)
        self.out_shapes = [jax.ShapeDtypeStruct(g.shape[:-2] + (g.shape[-2] // 2, g.shape[-1]), F32) for g in grads]
        self.scratch = [pltpu.SemaphoreType.DMA((n_w,))] * 2

    def _copies(self, ins, theirs, sems):
        send_sems, recv_sems = sems
        x, y, c = _place()
        sends = []
        for w, g in enumerate(self.operands):
            rows = pl.ds((1 - c) * (g.shape[-2] // 2), g.shape[-2] // 2)
            src = ins[w].at[:, rows, :] if g.ndim == 3 else ins[w].at[rows, :]
            sends.append(_remote(src, theirs[w], send_sems.at[w], recv_sems.at[w], (x, y, 1 - c)))
        return sends

    def start(self, ins, outs, sems):
        for cp in self._copies(ins, outs, sems):
            cp.start()

    def finish(self, ins, outs, sems):
        for cp in self._copies(ins, outs, sems):
            cp.wait()


def _exchange_pair_sum(name, place, grad):
    n, r, c = grad.shape
    half = r // 2

    def body(place_ref, g_all, g_ref, o_ref, theirs, send_sems, recv_sems):
        j = pl.program_id(0)
        x, y, core = _place()

        def copy(k):
            return _remote(g_all.at[k, pl.ds((1 - core) * half, half)], theirs.at[k], send_sems.at[k],
                           recv_sems.at[k], (x, y, 1 - core))

        @pl.when(j == 0)
        def _():
            for k in range(n):
                copy(k).start()

        copy(j).wait_recv()
        o_ref[0] = (g_ref[0] + theirs[j]).astype(BF16)

        @pl.when(j == n - 1)
        def _():
            for k in range(n):
                copy(k).wait_send()

    return pl.pallas_call(
        body, name=name, out_shape=jax.ShapeDtypeStruct((n, half, c), BF16),
        grid_spec=pltpu.PrefetchScalarGridSpec(
            num_scalar_prefetch=1, grid=(n,),
            in_specs=[ANY, pl.BlockSpec((1, half, c), lambda j, pr: (j, pr[0], 0))],
            out_specs=pl.BlockSpec((1, half, c), lambda j, pr: (j, 0, 0)),
            scratch_shapes=[pltpu.VMEM((n, half, c), F32), pltpu.SemaphoreType.DMA((n,)),
                            pltpu.SemaphoreType.DMA((n,))]),
        compiler_params=_params("arbitrary"),
    )(place, grad, grad)


class _NoExchanges:
    pair_sums, landed = {}, {}

    def gather(self, names):
        return _NoRider()

    def pair(self, names, grads):
        return _NoRider()

    def paired(self, names, grads, theirs):
        pass

    def pair_now(self, names, grads):
        pass

    def chip(self, names):
        return _NoRider()


class _StepExchanges(_NoExchanges):
    def __init__(self, shards_bf16, place):
        self.shards, self.place = shards_bf16, place
        self.pair_sums, self.landed = {}, {}

    def gather(self, names):
        return _WeightGather([self.shards[n] for n in names])

    def pair(self, names, grads):
        return _PairExchange([grads[n] for n in names])

    def paired(self, names, grads, theirs):
        for n, other in zip(names, theirs):
            self.pair_sums[n] = _pair_sum(f"pair_sum_{n}", self.place, grads[n], other)

    def pair_now(self, names, grads):
        for n in names:
            self.pair_sums[n] = _exchange_pair_sum(f"pair_sum_{n}", self.place, grads[n])

    def chip(self, names):
        return _ChipExchange([self.pair_sums[n] for n in names])


SUM_ROWS = 32


def _finish_gradients(place, pair_sums, landed, vec):
    n_w = len(pair_sums)
    rows = vec.shape[0]
    halves = [s.shape[1:] for s in pair_sums]

    def body(place_ref, *refs):
        sums, lands, v_ref = refs[:n_w], refs[n_w:2 * n_w], refs[2 * n_w]
        outs, o_ref = refs[2 * n_w + 1:3 * n_w + 1], refs[3 * n_w + 1]
        stage = refs[3 * n_w + 2:4 * n_w + 2]
        kept, half_send, half_recv, slots, core_sums, vec_send, vec_recv, sum_send, sum_recv = refs[4 * n_w + 2:]
        x, y, c = _place()
        my_chip, sibling = 2 * x + y, (x, y, 1 - c)
        slots[my_chip] = v_ref[...]
        spread = []
        for p, (ox, oy) in enumerate(_other_chips(x, y)):
            here = slots.at[2 * ox + oy]
            spread.append((_remote(v_ref, slots.at[my_chip], vec_send.at[p], vec_recv.at[p], (ox, oy, c)),
                           _remote(here, here, vec_send.at[p], vec_recv.at[p], (ox, oy, c))))
        for send, _ in spread:
            send.start()
        copies = []
        for w, (h, _) in enumerate(halves):
            step = SUM_ROWS if h % SUM_ROWS == 0 else h

            def sum_rows(i, _, w=w, step=step):
                r = pl.ds(pl.multiple_of(i * step, step), step)
                stage[w][r, :] = functools.reduce(lambda total, p: total + lands[w][p, r, :].astype(F32), range(3),
                                                  sums[w][0, r, :].astype(F32))
                return 0

            lax.fori_loop(0, h // step, sum_rows, 0)
            mine, theirs = outs[w].at[pl.ds(c * h, h)], outs[w].at[pl.ds((1 - c) * h, h)]
            copies.append((pltpu.make_async_copy(stage[w], mine, kept.at[w]),
                           _remote(stage[w], mine, half_send.at[w], half_recv.at[w], sibling),
                           _remote(theirs, theirs, half_send.at[w], half_recv.at[w], sibling)))
            copies[-1][0].start()
            copies[-1][1].start()
        for send, arrival in spread:
            arrival.wait_recv()
            send.wait_send()
        core_sums[c] = functools.reduce(lambda total, chip: total + slots[chip], range(1, N_CHIPS), slots[0])
        mine, theirs = core_sums.at[c], core_sums.at[1 - c]
        to_sibling = _remote(mine, mine, sum_send.at[0], sum_recv.at[0], sibling)
        to_sibling.start()
        _remote(theirs, theirs, sum_send.at[0], sum_recv.at[0], sibling).wait_recv()
        to_sibling.wait_send()
        o_ref[...] = core_sums[0] + core_sums[1]
        for keep, send, arrival in copies:
            keep.wait()
            arrival.wait_recv()
            send.wait_send()

    once = dict(pipeline_mode=pl.Buffered(1))
    vm = pl.BlockSpec((rows, LANES), lambda i, pr: (0, 0))
    res = pl.pallas_call(
        body, name="finish_gradients",
        grid_spec=pltpu.PrefetchScalarGridSpec(
            num_scalar_prefetch=1, grid=(1,),
            in_specs=[pl.BlockSpec((1,) + hc, lambda i, pr: (pr[1], 0, 0), **once) for hc in halves]
            + [pl.BlockSpec((3,) + hc, lambda i, pr: (0, 0, 0), **once) for hc in halves] + [vm],
            out_specs=[ANY] * n_w + [vm],
            scratch_shapes=[pltpu.VMEM(hc, F32) for hc in halves]
            + [pltpu.SemaphoreType.DMA((n_w,))] * 3
            + [pltpu.VMEM((N_CHIPS, rows, LANES), F32), pltpu.VMEM((2, rows, LANES), F32),
               pltpu.SemaphoreType.DMA((N_CHIPS - 1,)), pltpu.SemaphoreType.DMA((N_CHIPS - 1,)),
               pltpu.SemaphoreType.DMA((1,)), pltpu.SemaphoreType.DMA((1,))]),
        out_shape=[_sds((2 * h, cols), F32) for h, cols in halves] + [_sds(vec.shape, F32)],
        compiler_params=_params("arbitrary"),
    )(place, *pair_sums, *landed, vec)
    return res[:n_w], res[n_w]


def _row_tile(rows, most=512):
    fits = [tr for tr in range(16, min(rows, most) + 1, 16) if rows % tr == 0]
    return max(fits) if fits else rows


def _pair_sum(name, place, grad, theirs):
    if grad.ndim == 2:
        return _pair_sum_joined(name, place, grad, theirs)
    n, r, c = grad.shape
    half = r // 2
    tr = _row_tile(half)
    nb = half // tr

    def body(place_ref, g_ref, t_ref, o_ref):
        o_ref[...] = (g_ref[...] + t_ref[...]).astype(BF16)

    return pl.pallas_call(
        body, name=name, out_shape=jax.ShapeDtypeStruct((n, half, c), BF16),
        grid_spec=pltpu.PrefetchScalarGridSpec(
            num_scalar_prefetch=1, grid=(n, nb),
            in_specs=[pl.BlockSpec((1, tr, c), lambda j, i, pr: (j, pr[0] * nb + i, 0)),
                      pl.BlockSpec((1, tr, c), lambda j, i, pr: (j, i, 0))],
            out_specs=pl.BlockSpec((1, tr, c), lambda j, i, pr: (j, i, 0))),
        compiler_params=_params("parallel", "parallel"),
    )(place, grad, theirs)


def _pair_sum_joined(name, place, grad, theirs):
    r, wide = grad.shape
    half, c = r // 2, wide // N_CHIPS
    tr = _row_tile(half)
    nb = half // tr

    def body(place_ref, g_ref, t_ref, o_ref):
        for j in range(N_CHIPS):
            cols = slice(j * c, (j + 1) * c)
            o_ref[j] = (g_ref[:, cols] + t_ref[:, cols]).astype(BF16)

    return pl.pallas_call(
        body, name=name, out_shape=jax.ShapeDtypeStruct((N_CHIPS, half, c), BF16),
        grid_spec=pltpu.PrefetchScalarGridSpec(
            num_scalar_prefetch=1, grid=(nb,),
            in_specs=[pl.BlockSpec((tr, wide), lambda i, pr: (pr[0] * nb + i, 0)),
                      pl.BlockSpec((tr, wide), lambda i, pr: (i, 0))],
            out_specs=pl.BlockSpec((N_CHIPS, tr, c), lambda i, pr: (0, i, 0))),
        compiler_params=_params("parallel"),
    )(place, grad, theirs)


MIXER = ("w_branch_a", "w_branch_b", "w_out")
FFN_PLE = ("w_ffn_gate", "w_ffn_up", "w_ffn_down", "w_ple_gate", "w_ple_proj")
LATE = MIXER + FFN_PLE
BIG = ("w_in",) + LATE
HELD_TRANSPOSED = ("w_ffn_gate", "w_ffn_up")
SMALL = ("norm_mix", "w_pool", "pool_scale", "norm_ffn", "norm_ple", "norm_final")


def _join_columns(w4):
    return jnp.concatenate([w4[j] for j in range(N_CHIPS)], axis=1)


def _sds(shape, dtype):
    return jax.ShapeDtypeStruct(shape, dtype)


def _local_step(x, p, target, wf, small, ex=None):
    t, d = x.shape
    w_pool_b = small["w_pool"].astype(BF16)
    dp = w_pool_b.shape[0] * w_pool_b.shape[1]

    ex = ex or _NoExchanges()
    h1, first = _norm_fwd("norm_mix", x, small["norm_mix"], rider=ex.gather(("w_in",)))
    wf = {**wf, **dict(zip(("w_in",), first))}
    w_in = wf["w_in"]
    u, q, kv, ga, gb = _mm(
        "proj", [h1], [w_in[j] for j in range(N_CHIPS)], "nn",
        [_sds((t, dp), F32), _sds((t, dp), BF16), _sds((t, d), BF16), _sds((t, d), BF16), _sds((t, d), BF16)],
        separate=True, epilogue=lambda uq, kv_, ga_, gb_: (uq[:, :dp], uq[:, dp:], kv_, ga_, gb_), tm=512)
    pooled, ya = _pool_fwd(u, w_pool_b, small["pool_scale"])
    n_pairs = dp // LANES
    yb, late = _attn_fwd(q, 0, kv, 0, n_pairs, n_pairs, rider=ex.gather(LATE))
    wf = {**wf, **dict(zip(LATE, late))}
    w_down = wf["w_ffn_down"].reshape(-1, d)
    dff = w_down.shape[0]
    w_gate_t, w_up_t = wf["w_ffn_gate"].reshape(dff, d), wf["w_ffn_up"].reshape(dff, d)
    w_a, w_b, w_pp = _join_columns(wf["w_branch_a"]), _join_columns(wf["w_branch_b"]), _join_columns(wf["w_ple_proj"])
    w_out = wf["w_out"].reshape(d, d)
    w_pg = wf["w_ple_gate"].reshape(d, d)
    def residual_norm(branch, xv, g, w):
        xn = xv + jnp.dot(branch.astype(BF16), w, preferred_element_type=F32)
        return xn, xn * lax.rsqrt(jnp.mean(xn * xn, axis=-1, keepdims=True) + RMS_EPS) * g

    def mixer_tail(tav, tbv, gav, gbv, xv, g, w):
        merged = _sigmoid(gav) * tav + _sigmoid(gbv) * tbv
        return (tav, tbv, merged) + residual_norm(merged, xv, g, w)

    def ffn_tail(gv, uv, xv, g, w):
        act = gv * _sigmoid(gv) * uv
        return (gv, uv, act) + residual_norm(act, xv, g, w)

    stream = [_sds((t, d), F32), _sds((t, d), BF16)]
    ta, tb, merged, x1, h2 = _mm(
        "mixer_out", [ya, yb], [w_a, w_b], "nn", [_sds((t, d), BF16)] * 3 + stream,
        extras=[ga, gb, x, small["norm_ffn"]], wholes=[w_out], separate=True, epilogue=mixer_tail, tm=512)
    gate, up, act, x2, h3 = _mm(
        "ffn", [h2], [w_gate_t, w_up_t], "nt", [_sds((t, dff), BF16)] * 3 + stream,
        extras=[x1, small["norm_ple"]], wholes=[w_down], separate=True, epilogue=ffn_tail, tm=256)
    dx2, dx2_b, d_pp, d_gp, d_norm_final, loss_row, d_norm_ple = _mm(
        "ple_loss", [h3, p], [w_pg, w_pp], "nn", stream + [_sds((t, d), BF16)] * 2,
        extras=[x2, target, small["norm_final"].reshape(1, d), small["norm_ple"]], wholes=[w_pg], separate=True,
        epilogue=_ple_and_loss, sum_shapes=[_sds((1, d), F32)] * 3, tm=512)

    def through_norm(dh, xv, g, dres):
        dx, d_gain = _rms_norm_bwd(dh, xv, g)
        return dx + dres, dx + dres, d_gain

    gain_sum = [_sds((1, d), F32)]
    g_w_pp, g_w_pg = _mm_tn("g_ple", [p, h3], [d_pp, d_gp])

    def ffn_bwd(d_act, gv, uv, xv, g, dres, wg_t, wu_t):
        s = _sigmoid(gv)
        d_gate, d_up = d_act * uv * (s * (1.0 + gv * (1.0 - s))), d_act * (gv * s)
        dh2 = (jnp.dot(d_gate.astype(BF16), wg_t, preferred_element_type=F32)
               + jnp.dot(d_up.astype(BF16), wu_t, preferred_element_type=F32))
        return (d_gate, d_up) + through_norm(dh2, xv, g, dres)

    d_gate, d_up, dx1, dx1_b, d_norm_ffn = _mm(
        "ffn_bwd", [dx2_b], [w_down], "nt", [_sds((t, dff), BF16)] * 2 + stream,
        extras=[gate, up, x1, small["norm_ffn"], dx2], wholes=[w_gate_t, w_up_t], epilogue=ffn_bwd,
        sum_shapes=gain_sum, tm=256)
    g_w_down, = _mm_tn("g_ffn_down", [act], [dx2_b], tmm=512)
    g_w_gate_t, g_w_up_t = _mm_tn("g_ffn_gate_up", [d_gate, d_up], [h2], k_blocks=2)

    def merge_bwd(acc, tav, tbv, gav, gbv):
        sa, sb = _sigmoid(gav), _sigmoid(gbv)
        return acc * sa, acc * sb, acc * tav * sa * (1.0 - sa), acc * tbv * sb * (1.0 - sb)

    big = {
        "w_ffn_gate": g_w_gate_t.reshape(wf["w_ffn_gate"].shape), "w_ffn_up": g_w_up_t.reshape(wf["w_ffn_up"].shape),
        "w_ffn_down": g_w_down.reshape(wf["w_ffn_down"].shape),
        "w_ple_gate": g_w_pg.reshape(wf["w_ple_gate"].shape), "w_ple_proj": g_w_pp,
    }
    (d_ta, d_tb, d_ga, d_gb), theirs = _mm(
        "d_merged", [dx1_b], [w_out], "nt", [_sds((t, d), BF16)] * 4, extras=[ta, tb, ga, gb], epilogue=merge_bwd,
        tm=512, rider=ex.pair(FFN_PLE, big))
    ex.paired(FFN_PLE, big, theirs)
    g_w_out, big["w_branch_a"], big["w_branch_b"] = _mm_tn("g_mixer", [merged, ya, yb], [dx1_b, d_ta, d_tb])
    big["w_out"] = g_w_out.reshape(wf["w_out"].shape)
    (d_ya, d_yb), theirs = _mm(
        "d_branches", [d_ta, d_tb], [w_a, w_b], "nt", [_sds((t, dp), F32), _sds((t, dp), BF16)], separate=True,
        rider=ex.pair(MIXER, big))
    ex.paired(MIXER, big, theirs)
    d_u, g_w_pool, d_pool_scale = _pool_bwd(d_ya, pooled, w_pool_b, small["pool_scale"])
    (d_q, d_k, d_v), landed = _attn_bwd(q, 0, kv, 0, n_pairs, d_yb, n_pairs, rider=ex.chip(LATE))
    ex.landed.update(zip(LATE, landed))
    d_proj = [(d_u, d_q), (d_k, d_v), d_ga, d_gb]
    big["w_in"], = _mm_tn("g_w_in", [h1], d_proj, tmm=512, stacked=True)
    ex.pair_now(("w_in",), big)
    (grad_x, d_norm_mix), landed = _mm(
        "d_h1", d_proj, [w_in[j] for j in range(N_CHIPS)], "nt", [_sds((t, d), F32)],
        extras=[x, small["norm_mix"], dx1], epilogue=lambda dh, xv, g, dres: through_norm(dh, xv, g, dres)[1:],
        sum_shapes=gain_sum, tm=512, rider=ex.chip(("w_in",)))
    ex.landed.update(zip(("w_in",), landed))
    small_g = {"norm_mix": d_norm_mix, "w_pool": g_w_pool, "pool_scale": d_pool_scale, "norm_ffn": d_norm_ffn,
               "norm_ple": d_norm_ple, "norm_final": d_norm_final}
    return grad_x, big, small_g, loss_row


def _pack_small(small_g, loss_row):
    parts, layout = [], []
    for name in SMALL + ("loss",):
        v = (loss_row if name == "loss" else small_g[name]).reshape(-1, LANES)
        pad = (-v.shape[0]) % 8
        if pad:
            v = jnp.concatenate([v, jnp.zeros((pad, LANES), F32)], axis=0)
        layout.append((name, sum(q.shape[0] for q in parts), v.shape[0]))
        parts.append(v)
    return jnp.concatenate(parts, axis=0), layout


def kernel(x, p, norm_mix, w_in, w_pool, pool_scale, w_branch_a, w_branch_b, w_out, norm_ffn, w_ffn_gate, w_ffn_up, w_ffn_down, norm_ple, w_ple_gate, w_ple_proj, norm_final, loss_target, m_norm_mix, m_w_in, m_w_pool, m_pool_scale, m_w_branch_a, m_w_branch_b, m_w_out, m_norm_ffn, m_w_ffn_gate, m_w_ffn_up, m_w_ffn_down, m_norm_ple, m_w_ple_gate, m_w_ple_proj, m_norm_final, v_norm_mix, v_w_in, v_w_pool, v_pool_scale, v_w_branch_a, v_w_branch_b, v_w_out, v_norm_ffn, v_w_ffn_gate, v_w_ffn_up, v_w_ffn_down, v_norm_ple, v_w_ple_gate, v_w_ple_proj, v_norm_final):
    given = dict(locals())
    order = ("norm_mix", "w_in", "w_pool", "pool_scale", "w_branch_a", "w_branch_b", "w_out", "norm_ffn", "w_ffn_gate",
             "w_ffn_up", "w_ffn_down", "norm_ple", "w_ple_gate", "w_ple_proj", "norm_final")
    t, d = x.shape[1], x.shape[2]
    def local(a, n):
        return jnp.swapaxes(a[0], 0, 1) if n in HELD_TRANSPOSED else a[0]

    def back(a, n):
        return (jnp.swapaxes(a, 0, 1) if n in HELD_TRANSPOSED else a)[None]

    shard = {n: local(given[n], n) for n in BIG}
    small = {"norm_mix": norm_mix, "w_pool": w_pool[0], "pool_scale": pool_scale, "norm_ffn": norm_ffn,
             "norm_ple": norm_ple, "norm_final": norm_final}

    place = jnp.stack([lax.axis_index("c"), 2 * lax.axis_index("x") + lax.axis_index("y")]).astype(jnp.int32)
    ex = _StepExchanges({n: shard[n].astype(BF16) for n in BIG}, place)
    grad_x, _, small_g, loss_row = _local_step(
        x.reshape(t, d), p.reshape(t, p.shape[-1]), loss_target.reshape(t, d), {}, small, ex)
    packed, layout = _pack_small(small_g, loss_row)
    filled, reduced = _finish_gradients(place, [ex.pair_sums[n] for n in BIG], [ex.landed[n] for n in BIG], packed)
    grads = dict(zip(BIG, filled))
    for name, start, rows in layout:
        if name == "loss":
            loss = jnp.sum(reduced[start:start + rows])
        else:
            n_el = small[name].size
            grads[name] = reduced[start:start + rows].reshape(-1)[:n_el]

    deltas, new_m, new_v = {}, {}, {}
    for n in order:
        if n in BIG:
            w, m, v = shard[n], local(given["m_" + n], n), local(given["v_" + n], n)
            dl, mn, vn = _adamw(f"adamw_{n}", w, grads[n], m, v)
            grads[n], deltas[n], new_m[n], new_v[n] = [back(a, n) for a in (grads[n], dl, mn, vn)]
        else:
            w, full = small[n], given[n].shape
            shape2 = (1, w.shape[0]) if w.ndim == 1 else (w.shape if w.ndim == 2 else (w.shape[0] * w.shape[1], w.shape[2]))
            dl, mn, vn = _adamw(f"adamw_{n}", w.reshape(shape2), grads[n].reshape(shape2),
                                given["m_" + n].reshape(shape2), given["v_" + n].reshape(shape2))
            grads[n], deltas[n], new_m[n], new_v[n] = [a.reshape(full) for a in (grads[n], dl, mn, vn)]

    return (loss, grad_x.reshape(x.shape), *[grads[n] for n in order], *[deltas[n] for n in order],
            *[new_m[n] for n in order], *[new_v[n] for n in order])
```

```python
import functools
import math

import jax
import jax.numpy as jnp
from jax import lax
from jax.experimental import pallas as pl
from jax.experimental.pallas import tpu as pltpu

F32 = jnp.float32
BF16 = jnp.bfloat16
MESH = pl.DeviceIdType.MESH

RMS_EPS = 1e-6
POOL_WINDOWS = (2, 4, 8, 16)
POOL_HALO = 16
HEAD_DIM = 64
LANES = 128
ATT_BLOCK = 256
ATT_CHAINS = 2
ATT_FWD_CHAINS = 4
ATT_CHUNK = 256
ATT_SLAB = 256
ATT_SCALE = 1.0 / math.sqrt(HEAD_DIM)
LOG2_E = 1.4426950408889634
ATT_EXIT_BELOW = -150.5
ADAM_LR, ADAM_B1, ADAM_B2, ADAM_EPS, ADAM_WD, ADAM_STEP = 0.001, 0.9, 0.999, 1e-08, 0.01, 10
V7X_VMEM_LIMIT_BYTES = 56 * 1024 * 1024
N_CHIPS = 4
N_DEV = 8


def _params(*semantics):
    return pltpu.CompilerParams(dimension_semantics=semantics, vmem_limit_bytes=V7X_VMEM_LIMIT_BYTES)


def _sigmoid(z):
    return 0.5 * jnp.tanh(0.5 * z) + 0.5


def _tiled_spec(shape, tm, tn, n_total, at):
    rows, width = shape
    if rows == 1:
        if width == n_total:
            return pl.BlockSpec((1, tn), at(lambda i, j: (0, j)))
        return pl.BlockSpec((1, width), at(lambda i, j: (0, 0)))
    if width == n_total:
        return pl.BlockSpec((tm, tn), at(lambda i, j: (i, j)))
    assert tn == n_total, "an operand narrower than the output needs whole output rows per tile"
    return pl.BlockSpec((tm, width), at(lambda i, j: (i, 0)))


def _column_pieces(operands):
    pieces = [tuple(a) if isinstance(a, (tuple, list)) else (a,) for a in operands]
    return [p for ps in pieces for p in ps], [len(ps) for ps in pieces]


def _load_bf16(refs, counts):
    tiles, k = [], 0
    for n in counts:
        parts = [r[...] for r in refs[k:k + n]]
        parts = [t if t.dtype == BF16 else t.astype(BF16) for t in parts]
        tiles.append(parts[0] if n == 1 else jnp.concatenate(parts, axis=1))
        k += n
    return tiles


def _mm(name, a_list, b_list, mode, out_shapes, epilogue=None, extras=(), tm=1024, tn=None, separate=False,
        sum_shapes=(), rider=None, wholes=()):
    flat_a, counts = _column_pieces(a_list)
    m_total = flat_a[0].shape[0]
    n_total = b_list[0].shape[1] if mode == "nn" else b_list[0].shape[0]
    tn = n_total if tn is None else tn
    tm = min(tm, m_total)
    assert m_total % tm == 0 and n_total % tn == 0 and (not sum_shapes or tn == n_total)
    n_a, n_b, n_extra, n_out = len(counts), len(b_list), len(extras), len(out_shapes)
    assert n_a in (1, n_b)
    dims = (((1,), (0,)), ((), ())) if mode == "nn" else (((1,), (1,)), ((), ()))
    with_rider = rider is not None
    rider = rider or _NoRider()
    grid = (n_total // tn, m_total // tm)

    def at(index):
        return lambda j, i: index(i, j)

    def body(*refs):
        ins, o_refs, _, riding = rider.split(refs, len(flat_a) + n_b + n_extra + len(wholes), n_out + len(sum_shapes))
        a_refs, b_refs = ins[:len(flat_a)], ins[len(flat_a):len(flat_a) + n_b]
        e_refs, w_refs = ins[len(flat_a) + n_b:len(flat_a) + n_b + n_extra], ins[len(flat_a) + n_b + n_extra:]
        at_first = (pl.program_id(0) == 0) & (pl.program_id(1) == 0)
        at_last = (pl.program_id(0) == grid[0] - 1) & (pl.program_id(1) == grid[1] - 1)
        top, bottom = rider.at_steps(riding, at_first, at_first, at_last)
        top()
        lefts = _load_bf16(a_refs, counts)
        products = [lax.dot_general(lefts[s % n_a], b_refs[s][...], dims, preferred_element_type=F32)
                    for s in range(n_b)]
        if not separate:
            products = [functools.reduce(lambda p, r: p + r, products)]
        extra_tiles = [e[...].astype(F32) for e in e_refs]
        outs = products if epilogue is None else epilogue(*products, *extra_tiles, *[w[...] for w in w_refs])
        for o_ref, o in zip(o_refs[:n_out], outs[:n_out]):
            o_ref[...] = o.astype(o_ref.dtype)
        if sum_shapes:
            @pl.when(pl.program_id(1) == 0)
            def _():
                for s_ref in o_refs[n_out:]:
                    s_ref[...] = jnp.zeros_like(s_ref)

            for s_ref, s in zip(o_refs[n_out:], outs[n_out:]):
                s_ref[...] += s
        bottom()

    once = dict(pipeline_mode=pl.Buffered(1)) if tn == n_total else {}
    in_specs = [pl.BlockSpec((tm, a.shape[1]), at(lambda i, j: (i, 0))) for a in flat_a]
    if mode == "nn":
        in_specs += [pl.BlockSpec((b.shape[0], tn), at(lambda i, j: (0, j)), **once) for b in b_list]
    else:
        in_specs += [pl.BlockSpec((tn, b.shape[1]), at(lambda i, j: (j, 0)), **once) for b in b_list]
    in_specs += [_tiled_spec(e.shape, tm, tn, n_total, at) for e in extras]
    in_specs += [pl.BlockSpec(w.shape, lambda j, i: (0, 0), pipeline_mode=pl.Buffered(1)) for w in wholes]
    out_specs = [_tiled_spec(o.shape, tm, tn, n_total, at) for o in out_shapes]
    out_specs += [pl.BlockSpec(s.shape, at(lambda i, j: (0, 0))) for s in sum_shapes]
    semantics = ("arbitrary", "arbitrary") if sum_shapes or rider.operands else ("parallel", "parallel")
    res = pl.pallas_call(
        body, name=name, grid=grid, in_specs=in_specs + [ANY] * len(rider.operands),
        out_specs=out_specs + [ANY] * len(rider.out_shapes),
        out_shape=list(out_shapes) + list(sum_shapes) + list(rider.out_shapes), scratch_shapes=list(rider.scratch),
        compiler_params=_params(*semantics),
    )(*flat_a, *b_list, *extras, *wholes, *rider.operands)
    n_own = len(out_shapes) + len(sum_shapes)
    return (res[:n_own], res[n_own:]) if with_rider else res


def _mm_tn(name, a_list, b_list, tmm=1024, stacked=False, k_blocks=1):
    flat_b, counts = _column_pieces(b_list)
    n_a, n_b = len(a_list), len(counts)
    n_prod = max(n_a, n_b)
    m_total = a_list[0].shape[0]
    ks = [a_list[s % n_a].shape[1] for s in range(n_prod)]
    widths = [sum(p.shape[1] for p in flat_b[sum(counts[:s]):sum(counts[:s + 1])]) for s in range(n_b)]
    widths = [widths[s % n_b] for s in range(n_prod)]
    tmm = min(tmm, m_total)
    assert m_total % tmm == 0 and all(k % k_blocks == 0 for k in ks)
    assert n_a in (1, n_prod) and n_b in (1, n_prod) and not (stacked and n_a > 1)

    def body(*refs):
        a_refs, b_refs, o_refs = refs[:n_a], refs[n_a:n_a + len(flat_b)], refs[n_a + len(flat_b):]

        @pl.when(pl.program_id(1) == 0)
        def _():
            for o_ref in o_refs:
                o_ref[...] = jnp.zeros_like(o_ref)

        lefts, rights = _load_bf16(a_refs, [1] * n_a), _load_bf16(b_refs, counts)
        for s in range(n_prod):
            product = lax.dot_general(lefts[s % n_a], rights[s % n_b], (((0,), (0,)), ((), ())),
                                      preferred_element_type=F32)
            if stacked:
                o_refs[0][s] += product
            else:
                o_refs[s][...] += product

    in_specs = [pl.BlockSpec((tmm, a.shape[1] // k_blocks), lambda kb, m: (m, kb)) for a in a_list]
    in_specs += [pl.BlockSpec((tmm, b.shape[1]), lambda kb, m: (m, 0)) for b in flat_b]
    if stacked:
        out_shape = [jax.ShapeDtypeStruct((n_prod, ks[0], widths[0]), F32)]
        out_specs = [pl.BlockSpec((n_prod, ks[0] // k_blocks, widths[0]), lambda kb, m: (0, kb, 0))]
    else:
        out_shape = [jax.ShapeDtypeStruct((k, w), F32) for k, w in zip(ks, widths)]
        out_specs = [pl.BlockSpec((k // k_blocks, w), lambda kb, m: (kb, 0)) for k, w in zip(ks, widths)]
    return pl.pallas_call(
        body, name=name, grid=(k_blocks, m_total // tmm), in_specs=in_specs, out_specs=out_specs, out_shape=out_shape,
        compiler_params=_params("arbitrary", "arbitrary"),
    )(*a_list, *flat_b)


def _rows(name, fn, ins, tile_outs, sum_outs=(), tr=512, rider=None):
    t_total = max(a.shape[0] for a in ins)
    tr = min(tr, t_total)
    assert t_total % tr == 0
    n_in, n_tile = len(ins), len(tile_outs)
    rider = rider or _NoRider()
    n_steps = t_total // tr

    def body(*refs):
        own_ins, own_outs, _, riding = rider.split(refs, n_in, n_tile + len(sum_outs))
        step = pl.program_id(0)
        top, bottom = rider.at_steps(riding, step == 0, step == n_steps - 1, step == n_steps - 1)
        top()
        refs = tuple(own_ins) + tuple(own_outs)
        outs = fn(*[r[...].astype(F32) for r in refs[:n_in]])
        for o_ref, o in zip(refs[n_in:n_in + n_tile], outs[:n_tile]):
            o_ref[...] = o.astype(o_ref.dtype)
        if sum_outs:
            @pl.when(pl.program_id(0) == 0)
            def _():
                for s_ref in refs[n_in + n_tile:]:
                    s_ref[...] = jnp.zeros_like(s_ref)

            for s_ref, s in zip(refs[n_in + n_tile:], outs[n_tile:]):
                s_ref[...] += s
        bottom()

    def spec(shape):
        if shape[0] == 1:
            return pl.BlockSpec(shape, lambda i: (0, 0))
        return pl.BlockSpec((tr, shape[1]), lambda i: (i, 0))

    return pl.pallas_call(
        body, name=name, grid=(n_steps,), in_specs=[spec(a.shape) for a in ins] + [ANY] * len(rider.operands),
        out_specs=[spec(o.shape) for o in tile_outs] + [spec(s.shape) for s in sum_outs] + [ANY] * len(rider.out_shapes),
        out_shape=list(tile_outs) + list(sum_outs) + list(rider.out_shapes), scratch_shapes=list(rider.scratch),
        compiler_params=_params("arbitrary" if sum_outs or rider.operands else "parallel"),
    )(*ins, *rider.operands)


def _norm_fwd(name, x, gain, rider=None):
    def fn(xv, g):
        inv = lax.rsqrt(jnp.mean(xv * xv, axis=-1, keepdims=True) + RMS_EPS)
        return (xv * inv * g,)

    res = _rows(name, fn, [x, gain], [jax.ShapeDtypeStruct(x.shape, BF16)], rider=rider)
    return res[0], res[1:]


def _rms_norm_bwd(dh, xv, g):
    inv = lax.rsqrt(jnp.mean(xv * xv, axis=-1, keepdims=True) + RMS_EPS)
    xn = xv * inv
    dxn = dh * g
    return inv * (dxn - xn * jnp.mean(dxn * xn, axis=-1, keepdims=True)), jnp.sum(dh * xn, axis=0, keepdims=True)


def _ple_and_loss(gv, pv, x2v, tv, g_final, g_ple, w_pg):
    d = x2v.shape[1]
    s = _sigmoid(gv)
    xv = x2v + s * pv
    inv = lax.rsqrt(jnp.mean(xv * xv, axis=-1, keepdims=True) + RMS_EPS)
    err = xv * inv * g_final - tv
    dx3, d_final = _rms_norm_bwd(err * (1.0 / d), xv, g_final)
    d_pp, d_gp = dx3 * s, dx3 * pv * s * (1.0 - s)
    dh3 = lax.dot_general(d_gp.astype(BF16), w_pg, (((1,), (1,)), ((), ())), preferred_element_type=F32)
    dx2, d_ple = _rms_norm_bwd(dh3, x2v, g_ple)
    dx2 = dx2 + dx3
    return dx2, dx2, d_pp, d_gp, d_final, (0.5 / d) * jnp.sum(err * err, axis=0, keepdims=True), d_ple


def _window_counts(t_pos, w):
    return jnp.minimum(t_pos + 1, w).astype(F32)


def _pool_fwd(u, w_pool, scale, tr=512):
    t_total, width = u.shape
    tr = min(tr, t_total)
    n_groups = len(POOL_WINDOWS)
    gdim = width // n_groups
    ext = tr + POOL_HALO

    def body(u_ref, halo_ref, w_ref, s_ref, pooled_ref, ya_ref):
        i = pl.program_id(0)
        halo = jnp.where(i == 0, 0.0, halo_ref[...])
        t_pos = i * tr + lax.broadcasted_iota(jnp.int32, (tr, 1), 0)
        for g, w in enumerate(POOL_WINDOWS):
            cols = slice(g * gdim, (g + 1) * gdim)
            main = u_ref[:, cols]
            win = jnp.concatenate([halo[:, cols], main], axis=0)
            span = 1
            while span < w:
                win = win + pltpu.roll(win, span, 0)
                span *= 2
            pooled = win[POOL_HALO:, :] * (1.0 / _window_counts(t_pos, w)) - main
            pooled_b = pooled.astype(BF16)
            pooled_ref[:, cols] = pooled_b
            mixed = jnp.dot(pooled_b, w_ref[g], preferred_element_type=F32)
            ya_ref[:, cols] = (mixed * s_ref[:, cols]).astype(BF16)

    hb = tr // POOL_HALO
    return pl.pallas_call(
        body, name="pool_fwd", grid=(t_total // tr,),
        in_specs=[pl.BlockSpec((tr, width), lambda i: (i, 0)),
                  pl.BlockSpec((POOL_HALO, width), lambda i: (jnp.maximum(i * hb - 1, 0), 0)),
                  pl.BlockSpec((n_groups, gdim, gdim), lambda i: (0, 0, 0)),
                  pl.BlockSpec((1, width), lambda i: (0, 0))],
        out_specs=[pl.BlockSpec((tr, width), lambda i: (i, 0)), pl.BlockSpec((tr, width), lambda i: (i, 0))],
        out_shape=[jax.ShapeDtypeStruct(u.shape, BF16), jax.ShapeDtypeStruct(u.shape, BF16)],
        compiler_params=_params("parallel"),
    )(u, u, w_pool, scale)


def _pool_bwd(dya, pooled, w_pool, scale, tr=512):
    t_total, width = dya.shape
    tr = min(tr, t_total)
    n_groups = len(POOL_WINDOWS)
    gdim = width // n_groups
    ext = tr + POOL_HALO
    n_tiles = t_total // tr

    def body(d_ref, halo_ref, p_ref, w_ref, s_ref, du_ref, dw_ref, ds_ref):
        i = pl.program_id(0)

        @pl.when(i == 0)
        def _():
            dw_ref[...] = jnp.zeros_like(dw_ref)
            ds_ref[...] = jnp.zeros_like(ds_ref)

        halo = jnp.where(i == n_tiles - 1, 0.0, halo_ref[...])
        t_pos = i * tr + lax.broadcasted_iota(jnp.int32, (ext, 1), 0)
        for g, w in enumerate(POOL_WINDOWS):
            cols = slice(g * gdim, (g + 1) * gdim)
            sc = s_ref[:, cols]
            d_main = d_ref[:, cols]
            pooled_b = p_ref[:, cols]
            mixed = jnp.dot(pooled_b, w_ref[g], preferred_element_type=F32)
            ds_ref[:, cols] += jnp.sum(d_main * mixed, axis=0, keepdims=True)
            dmix = (jnp.concatenate([d_main, halo[:, cols]], axis=0) * sc).astype(BF16)
            dw_ref[g] += lax.dot_general(pooled_b, dmix[:tr, :], (((0,), (0,)), ((), ())),
                                         preferred_element_type=F32)
            dpool = lax.dot_general(dmix, w_ref[g], (((1,), (1,)), ((), ())), preferred_element_type=F32)
            win = dpool * (1.0 / _window_counts(t_pos, w))
            span = 1
            while span < w:
                win = win + pltpu.roll(win, ext - span, 0)
                span *= 2
            du_ref[:, cols] = (win[:tr, :] - dpool[:tr, :]).astype(BF16)

    hb = tr // POOL_HALO
    last_halo = t_total // POOL_HALO - 1
    return pl.pallas_call(
        body, name="pool_bwd", grid=(n_tiles,),
        in_specs=[pl.BlockSpec((tr, width), lambda i: (i, 0)),
                  pl.BlockSpec((POOL_HALO, width), lambda i: (jnp.minimum((i + 1) * hb, last_halo), 0)),
                  pl.BlockSpec((tr, width), lambda i: (i, 0)),
                  pl.BlockSpec((n_groups, gdim, gdim), lambda i: (0, 0, 0)),
                  pl.BlockSpec((1, width), lambda i: (0, 0))],
        out_specs=[pl.BlockSpec((tr, width), lambda i: (i, 0)),
                   pl.BlockSpec((n_groups, gdim, gdim), lambda i: (0, 0, 0)),
                   pl.BlockSpec((1, width), lambda i: (0, 0))],
        out_shape=[jax.ShapeDtypeStruct(dya.shape, BF16), jax.ShapeDtypeStruct((n_groups, gdim, gdim), F32),
                   jax.ShapeDtypeStruct((1, width), F32)],
        compiler_params=_params("arbitrary"),
    )(dya, dya, pooled, w_pool, scale)


def _head_masks():
    lane = lax.broadcasted_iota(jnp.int32, (1, LANES), 1)
    return lane < HEAD_DIM


def _stack_heads(tile, first):
    zero = jnp.zeros_like(tile)
    return jnp.concatenate([jnp.where(first, tile, zero), jnp.where(first, zero, tile)], axis=0)


def _causal_mask(t_pos, k_start):
    col = lax.broadcasted_iota(jnp.int32, (1, 2 * ATT_SLAB), 1)
    return k_start + (col & (ATT_SLAB - 1)) < t_pos


def _slab_scores(q, kd, mask):
    z2 = lax.dot_general(q, kd, (((1,), (1,)), ((), ())), preferred_element_type=F32) * LOG2_E
    log_hit = jnp.minimum(z2, 0.0) - jnp.log2(1.0 + jnp.exp2(-jnp.abs(z2)))
    log_fail = log_hit - z2
    return log_hit, (log_fail if mask is None else jnp.where(mask, log_fail, 0.0))


def _weights(log_hit, suffix, mask):
    arg = log_hit + suffix
    return jnp.exp2(arg if mask is None else jnp.where(mask, arg, -1e30))


def _tri(upper):
    r = lax.broadcasted_iota(jnp.int32, (ATT_CHUNK, ATT_CHUNK), 0)
    c = lax.broadcasted_iota(jnp.int32, (ATT_CHUNK, ATT_CHUNK), 1)
    return jnp.where(r > c if upper else r < c, 1.0, 0.0).astype(BF16)


def _tri_spec():
    return pl.BlockSpec((ATT_CHUNK, ATT_CHUNK), lambda h, i: (0, 0), pipeline_mode=pl.Buffered(1))


def _scan_chunk(v, tri):
    return jnp.dot(v.astype(BF16), tri, preferred_element_type=F32)


def _lane_bcast(col):
    return jnp.broadcast_to(col, (col.shape[0], LANES))


def _scan_slab(v, tri, carries, from_right):
    n_chunks = ATT_SLAB // ATT_CHUNK
    edge = 0 if from_right else ATT_CHUNK - 1
    parts, new_carries = [None] * (2 * n_chunks), []
    for head in range(2):
        run = carries[head]
        for c in (reversed(range(n_chunks)) if from_right else range(n_chunks)):
            lo_col = head * ATT_SLAB + c * ATT_CHUNK
            vc = v[:, lo_col:lo_col + ATT_CHUNK]
            sc = _scan_chunk(vc, tri)
            parts[head * n_chunks + c] = sc + jnp.concatenate([run] * (ATT_CHUNK // LANES), axis=1)
            run = run + _lane_bcast(sc[:, edge:edge + 1] + vc[:, edge:edge + 1])
        new_carries.append(run)
    return jnp.concatenate(parts, axis=1), new_carries


def _fold_heads(stacked, first):
    s = stacked.shape[0] // 2
    return jnp.where(first, stacked[:s], stacked[s:])


class _NoRider:
    operands, out_shapes, scratch = (), (), ()

    def split(self, refs, n_base_in, n_base_out):
        n_in, n_out, n_sem = len(self.operands), len(self.out_shapes), len(self.scratch)
        a = n_base_in + n_in
        b = a + n_base_out + n_out
        mine = (refs[n_base_in:a], refs[a + n_base_out:b], refs[b:b + n_sem])
        return refs[:n_base_in], refs[a:a + n_base_out], refs[b + n_sem:], mine

    def start(self, ins, outs, sems):
        pass

    def relay(self, ins, outs, sems):
        pass

    def finish(self, ins, outs, sems):
        pass

    def at_steps(self, refs, first_step, relay_step, last_step):
        if not self.operands:
            return (lambda: None), (lambda: None)

        def top():
            pl.when(first_step)(lambda: self.start(*refs))
            pl.when(relay_step)(lambda: self.relay(*refs))

        return top, lambda: pl.when(last_step)(lambda: self.finish(*refs))


def _attn_fwd(q_src, q_col, kv_src, k_col, v_col, n_pairs=4, rider=_NoRider()):
    t_total = q_src.shape[0]
    blk = ATT_BLOCK
    n_chains = ATT_FWD_CHAINS if t_total % (ATT_FWD_CHAINS * blk) == 0 else ATT_CHAINS
    n_steps = t_total // (n_chains * blk)
    assert t_total % ATT_SLAB == 0 and ATT_SLAB == ATT_BLOCK

    def body(*refs):
        (q_ref, k_ref, v_ref, suffix_ref), (o_ref,), _, riding = rider.split(refs, 4, 1)
        h, ii = pl.program_id(0), pl.program_id(1)
        top, bottom = rider.at_steps(riding, (h == 0) & (ii == 0), (h == n_pairs - 1) & (ii == 0),
                                     (h == n_pairs - 1) & (ii == n_steps - 1))
        top()
        first = _head_masks()
        suffix_tri = suffix_ref[...]
        blocks = [n_chains * ii + c for c in range(n_chains)]
        qs = [q_ref[c * blk:(c + 1) * blk, :] * ATT_SCALE for c in range(n_chains)]
        t_pos = [b * blk + lax.broadcasted_iota(jnp.int32, (blk, 1), 0) for b in blocks]

        def one(c, t, chain, on_diagonal):
            _, acc, right_a, right_b = chain
            k_start = pl.multiple_of((blocks[c] - t) * ATT_SLAB, ATT_SLAB)
            kd = _stack_heads(k_ref[pl.ds(k_start, ATT_SLAB), :], first)
            vd = _stack_heads(v_ref[pl.ds(k_start, ATT_SLAB), :], first)
            mask = _causal_mask(t_pos[c], k_start) if on_diagonal else None
            log_hit, log_fail = _slab_scores(qs[c], kd, mask)
            suffix, (right_a, right_b) = _scan_slab(log_fail, suffix_tri, (right_a, right_b), from_right=True)
            a = _weights(log_hit, suffix, mask).astype(BF16)
            acc = acc + jnp.dot(a, vd, preferred_element_type=F32)
            return jnp.max(jnp.maximum(right_a, right_b)), acc, right_a, right_b

        def step(state, on_diagonal):
            t, chains = state
            return t + 1, tuple(one(c, t, chains[c], on_diagonal) for c in range(n_chains))

        def more(state):
            t, chains = state
            return (t <= blocks[0]) & (functools.reduce(jnp.maximum, [ch[0] for ch in chains]) > ATT_EXIT_BELOW)

        zero = jnp.zeros((blk, LANES), F32)
        state = step((0, ((jnp.float32(0.0), zero, zero, zero),) * n_chains), on_diagonal=True)
        t, chains = lax.while_loop(more, functools.partial(step, on_diagonal=False), state)
        for c in range(n_chains):
            chain = chains[c]
            if c:
                _, chain = lax.while_loop(
                    lambda s, c=c: (s[0] <= blocks[c]) & (s[1][0] > ATT_EXIT_BELOW),
                    lambda s, c=c: (s[0] + 1, one(c, s[0], s[1], False)), (t, chain))
            o_ref[c * blk:(c + 1) * blk, :] = chain[1].astype(BF16)
        bottom()

    rows = n_chains * blk
    res = pl.pallas_call(
        body, name="attn_fwd", grid=(n_pairs, n_steps),
        in_specs=[pl.BlockSpec((rows, LANES), lambda h, i: (i, q_col + h)),
                  pl.BlockSpec((t_total, LANES), lambda h, i: (0, k_col + h)),
                  pl.BlockSpec((t_total, LANES), lambda h, i: (0, v_col + h)), _tri_spec()] + [ANY] * len(rider.operands),
        out_specs=[pl.BlockSpec((rows, LANES), lambda h, i: (i, h))] + [ANY] * len(rider.out_shapes),
        out_shape=[jax.ShapeDtypeStruct((t_total, n_pairs * LANES), BF16)] + list(rider.out_shapes),
        scratch_shapes=list(rider.scratch),
        compiler_params=_params("arbitrary", "arbitrary"),
    )(q_src, kv_src, kv_src, _tri(upper=True), *rider.operands)
    return res[0], res[1:]


def _attn_bwd(q_src, q_col, kv_src, k_col, v_col, dy, n_pairs=4, rider=_NoRider()):
    t_total = q_src.shape[0]
    blk = ATT_BLOCK
    n_steps = t_total // (ATT_CHAINS * blk)
    n_slabs = t_total // ATT_SLAB
    assert t_total % ATT_SLAB == 0 and ATT_SLAB == ATT_BLOCK

    def body(*refs):
        ins, (dq_ref, dk_ref, dv_ref), (g_s, dk_acc, dv_acc), riding = rider.split(refs, 6, 3)
        q_ref, dy_ref, k_ref, v_ref, suffix_ref, prefix_ref = ins
        h, ii = pl.program_id(0), pl.program_id(1)
        top, bottom = rider.at_steps(riding, (h == 0) & (ii == 0), (h == n_pairs - 1) & (ii == 0),
                                     (h == n_pairs - 1) & (ii == n_steps - 1))
        top()

        @pl.when(ii == 0)
        def _():
            dk_acc[...] = jnp.zeros_like(dk_acc)
            dv_acc[...] = jnp.zeros_like(dv_acc)

        first = _head_masks()
        suffix_tri = suffix_ref[...]
        prefix_tri = prefix_ref[...]
        blocks = [ATT_CHAINS * ii + c for c in range(ATT_CHAINS)]
        rows = [slice(c * blk, (c + 1) * blk) for c in range(ATT_CHAINS)]
        qs = [q_ref[r, :] * ATT_SCALE for r in rows]
        dys = [dy_ref[r, :] for r in rows]
        t_pos = [b * blk + lax.broadcasted_iota(jnp.int32, (blk, 1), 0) for b in blocks]

        def one1(c, t, chain, on_diagonal):
            _, right_a, right_b = chain
            slab = blocks[c] - t
            k_start = pl.multiple_of(slab * ATT_SLAB, ATT_SLAB)
            kd = _stack_heads(k_ref[pl.ds(k_start, ATT_SLAB), :], first)
            vd = _stack_heads(v_ref[pl.ds(k_start, ATT_SLAB), :], first)
            mask = _causal_mask(t_pos[c], k_start) if on_diagonal else None
            log_hit, log_fail = _slab_scores(qs[c], kd, mask)
            suffix, (right_a, right_b) = _scan_slab(log_fail, suffix_tri, (right_a, right_b), from_right=True)
            a = _weights(log_hit, suffix, mask)
            da = lax.dot_general(dys[c], vd, (((1,), (1,)), ((), ())), preferred_element_type=F32)
            g_s[c, slab] = (da * a).astype(BF16)
            dv_acc[pl.ds(k_start, ATT_SLAB), :] += _fold_heads(lax.dot_general(
                a.astype(BF16), dys[c], (((0,), (0,)), ((), ())), preferred_element_type=F32), first)
            return jnp.max(jnp.maximum(right_a, right_b)), right_a, right_b

        def step1(state, on_diagonal):
            t, chains = state
            return t + 1, tuple(one1(c, t, chains[c], on_diagonal) for c in range(ATT_CHAINS))

        def more(state):
            t, chains = state
            return (t <= blocks[0]) & (functools.reduce(jnp.maximum, [ch[0] for ch in chains]) > ATT_EXIT_BELOW)

        zero = jnp.zeros((blk, LANES), F32)
        state = step1((0, ((jnp.float32(0.0), zero, zero),) * ATT_CHAINS), on_diagonal=True)
        joint, chains = lax.while_loop(more, functools.partial(step1, on_diagonal=False), state)
        done = [joint]
        for c in range(1, ATT_CHAINS):
            done.append(lax.while_loop(
                lambda s, c=c: (s[0] <= blocks[c]) & (s[1][0] > ATT_EXIT_BELOW),
                lambda s, c=c: (s[0] + 1, one1(c, s[0], s[1], False)), (joint, chains[c]))[0])

        def one2(c, t, carry, on_diagonal):
            dq, left_a, left_b = carry
            slab = blocks[c] - t
            k_start = pl.multiple_of(slab * ATT_SLAB, ATT_SLAB)
            kd = _stack_heads(k_ref[pl.ds(k_start, ATT_SLAB), :], first)
            g = g_s[c, slab]
            sig = _sigmoid(lax.dot_general(qs[c], kd, (((1,), (1,)), ((), ())), preferred_element_type=F32))
            prefix, (left_a, left_b) = _scan_slab(g, prefix_tri, (left_a, left_b), from_right=False)
            dz = g * (1.0 - sig) - sig * prefix
            if on_diagonal:
                dz = jnp.where(_causal_mask(t_pos[c], k_start), dz, 0.0)
            dz = dz.astype(BF16)
            dq = dq + jnp.dot(dz, kd, preferred_element_type=F32)
            dk_acc[pl.ds(k_start, ATT_SLAB), :] += _fold_heads(lax.dot_general(
                dz, qs[c], (((0,), (0,)), ((), ())), preferred_element_type=F32), first)
            return dq, left_a, left_b

        carries = [(zero, zero, zero)]
        for c in range(1, ATT_CHAINS):
            carries.append(lax.fori_loop(
                0, done[c] - joint, lambda n, carry, c=c: one2(c, done[c] - 1 - n, carry, False), (zero, zero, zero)))
        carries = lax.fori_loop(
            0, joint - 1,
            lambda n, cs: tuple(one2(c, joint - 1 - n, cs[c], False) for c in range(ATT_CHAINS)), tuple(carries))
        for c in range(ATT_CHAINS):
            dq_ref[rows[c], :] = (one2(c, 0, carries[c], True)[0] * ATT_SCALE).astype(BF16)

        @pl.when(ii == n_steps - 1)
        def _():
            dk_ref[...] = dk_acc[...].astype(BF16)
            dv_ref[...] = dv_acc[...].astype(BF16)

        bottom()

    out = jax.ShapeDtypeStruct((t_total, n_pairs * LANES), BF16)
    n_rows = ATT_CHAINS * blk
    whole = dict(pipeline_mode=pl.Buffered(1))
    res = pl.pallas_call(
        body, name="attn_bwd", grid=(n_pairs, n_steps),
        in_specs=[pl.BlockSpec((n_rows, LANES), lambda h, i: (i, q_col + h)),
                  pl.BlockSpec((n_rows, LANES), lambda h, i: (i, h)),
                  pl.BlockSpec((t_total, LANES), lambda h, i: (0, k_col + h), **whole),
                  pl.BlockSpec((t_total, LANES), lambda h, i: (0, v_col + h), **whole), _tri_spec(), _tri_spec()]
        + [ANY] * len(rider.operands),
        out_specs=[pl.BlockSpec((n_rows, LANES), lambda h, i: (i, h)),
                   pl.BlockSpec((t_total, LANES), lambda h, i: (0, h)),
                   pl.BlockSpec((t_total, LANES), lambda h, i: (0, h))] + [ANY] * len(rider.out_shapes),
        out_shape=[out, out, out] + list(rider.out_shapes),
        scratch_shapes=list(rider.scratch) + [pltpu.VMEM((ATT_CHAINS, n_slabs, blk, 2 * ATT_SLAB), BF16),
                                              pltpu.VMEM((t_total, LANES), F32), pltpu.VMEM((t_total, LANES), F32)],
        compiler_params=_params("arbitrary", "arbitrary"),
    )(q_src, dy, kv_src, kv_src, _tri(upper=True), _tri(upper=False), *rider.operands)
    return res[:3], res[3:]


def _adamw(name, w, g, m, v):
    def fn(wv, gv, mv, vv):
        mn = ADAM_B1 * mv + (1.0 - ADAM_B1) * gv
        vn = ADAM_B2 * vv + (1.0 - ADAM_B2) * (gv * gv)
        m_hat = mn / (1.0 - ADAM_B1 ** ADAM_STEP)
        v_hat = vn / (1.0 - ADAM_B2 ** ADAM_STEP)
        return -ADAM_LR * (m_hat / (jnp.sqrt(v_hat) + ADAM_EPS) + ADAM_WD * wv), mn, vn

    rows = w.shape[0]
    tr = _row_tile(rows)
    shp = jax.ShapeDtypeStruct(w.shape, F32)
    if rows == 1:
        def body(w_ref, g_ref, m_ref, v_ref, d_ref, mo_ref, vo_ref):
            d, mn, vn = fn(w_ref[...], g_ref[...], m_ref[...], v_ref[...])
            d_ref[...], mo_ref[...], vo_ref[...] = d, mn, vn

        return pl.pallas_call(body, name=name, out_shape=[shp, shp, shp])(w, g, m, v)
    return _rows(name, fn, [w, g, m, v], [shp, shp, shp], tr=tr)


def _place():
    return lax.axis_index("x"), lax.axis_index("y"), lax.axis_index("c")


def _other_chips(x, y):
    return [(1 - x, y), (x, 1 - y), (1 - x, 1 - y)]


ANY = pl.BlockSpec(memory_space=pl.ANY)


def _remote(src, dst, send_sem, recv_sem, to):
    return pltpu.make_async_remote_copy(src_ref=src, dst_ref=dst, send_sem=send_sem, recv_sem=recv_sem,
                                        device_id=to, device_id_type=MESH)


class _WeightGather(_NoRider):
    def __init__(self, shards):
        n_w = len(shards)
        self.operands = list(shards)
        self.out_shapes = [jax.ShapeDtypeStruct((N_CHIPS,) + s.shape, s.dtype) for s in shards]
        self.scratch = [pltpu.SemaphoreType.DMA((3, n_w))] * 4 + [pltpu.SemaphoreType.DMA((n_w,))] * 2

    def _copies(self, ins, outs, sems):
        send_sems, recv_sems, relay_send, relay_recv, own_send, own_recv = sems
        x, y, c = _place()
        my_chip, sibling = 2 * x + y, (x, y, 1 - c)
        n_w = len(ins)

        def half(w, chip, core):
            h = self.operands[w].shape[0] // 2
            return outs[w].at[chip, pl.ds(core * h, h)]

        own = [_remote(ins[w], outs[w].at[my_chip], own_send.at[w], own_recv.at[w], sibling) for w in range(n_w)]
        sends, landed, relays, relayed = [], [], [], []
        for p, (ox, oy) in enumerate(_other_chips(x, y)):
            for w in range(n_w):
                h = self.operands[w].shape[0] // 2
                sends.append(_remote(ins[w].at[pl.ds(c * h, h)], half(w, my_chip, c), send_sems.at[p, w],
                                     recv_sems.at[p, w], (ox, oy, c)))
                here = half(w, 2 * ox + oy, c)
                landed.append(_remote(here, here, send_sems.at[p, w], recv_sems.at[p, w], (ox, oy, c)))
                relays.append(_remote(here, here, relay_send.at[p, w], relay_recv.at[p, w], sibling))
                there = half(w, 2 * ox + oy, 1 - c)
                relayed.append(_remote(there, there, relay_send.at[p, w], relay_recv.at[p, w], sibling))
        return own, sends, landed, relays, relayed

    def start(self, ins, outs, sems):
        own, sends, _, _, _ = self._copies(ins, outs, sems)
        for cp in own + sends:
            cp.start()

    def relay(self, ins, outs, sems):
        _, _, landed, relays, _ = self._copies(ins, outs, sems)
        for arrival, cp in zip(landed, relays):
            arrival.wait_recv()
            cp.start()

    def finish(self, ins, outs, sems):
        own, sends, _, relays, relayed = self._copies(ins, outs, sems)
        for arrival in relayed:
            arrival.wait_recv()
        for cp in sends + relays:
            cp.wait_send()
        for cp in own:
            cp.wait()


class _ChipExchange(_NoRider):
    def __init__(self, pair_sums):
        n_w = len(pair_sums)
        self.operands = list(pair_sums)
        self.out_shapes = [jax.ShapeDtypeStruct((3,) + s.shape[1:], s.dtype) for s in pair_sums]
        self.scratch = [pltpu.SemaphoreType.DMA((3, n_w))] * 2

    def _copies(self, ins, outs, sems):
        send_sems, recv_sems = sems
        x, y, c = _place()
        return [_remote(ins[w].at[2 * ox + oy], outs[w].at[p], send_sems.at[p, w], recv_sems.at[p, w], (ox, oy, c))
                for p, (ox, oy) in enumerate(_other_chips(x, y)) for w in range(len(ins))]

    def start(self, ins, outs, sems):
        for cp in self._copies(ins, outs, sems):
            cp.start()

    def finish(self, ins, outs, sems):
        for cp in self._copies(ins, outs, sems):
            cp.wait()


class _PairExchange(_NoRider):
    def __init__(self, grads):
        n_w = len(grads)
        self.operands = list(grads)
        self.out_shapes = [jax.ShapeDtypeStruct(g.shape[:-2] + (g.shape[-2] // 2, g.shape[-1]), F32) for g in grads]
        self.scratch = [pltpu.SemaphoreType.DMA((n_w,))] * 2

    def _copies(self, ins, theirs, sems):
        send_sems, recv_sems = sems
        x, y, c = _place()
        sends = []
        for w, g in enumerate(self.operands):
            rows = pl.ds((1 - c) * (g.shape[-2] // 2), g.shape[-2] // 2)
            src = ins[w].at[:, rows, :] if g.ndim == 3 else ins[w].at[rows, :]
            sends.append(_remote(src, theirs[w], send_sems.at[w], recv_sems.at[w], (x, y, 1 - c)))
        return sends

    def start(self, ins, outs, sems):
        for cp in self._copies(ins, outs, sems):
            cp.start()

    def finish(self, ins, outs, sems):
        for cp in self._copies(ins, outs, sems):
            cp.wait()


def _exchange_pair_sum(name, place, grad):
    n, r, c = grad.shape
    half = r // 2

    def body(place_ref, g_all, g_ref, o_ref, theirs, send_sems, recv_sems):
        j = pl.program_id(0)
        x, y, core = _place()

        def copy(k):
            return _remote(g_all.at[k, pl.ds((1 - core) * half, half)], theirs.at[k], send_sems.at[k],
                           recv_sems.at[k], (x, y, 1 - core))

        @pl.when(j == 0)
        def _():
            for k in range(n):
                copy(k).start()

        copy(j).wait_recv()
        o_ref[0] = (g_ref[0] + theirs[j]).astype(BF16)

        @pl.when(j == n - 1)
        def _():
            for k in range(n):
                copy(k).wait_send()

    return pl.pallas_call(
        body, name=name, out_shape=jax.ShapeDtypeStruct((n, half, c), BF16),
        grid_spec=pltpu.PrefetchScalarGridSpec(
            num_scalar_prefetch=1, grid=(n,),
            in_specs=[ANY, pl.BlockSpec((1, half, c), lambda j, pr: (j, pr[0], 0))],
            out_specs=pl.BlockSpec((1, half, c), lambda j, pr: (j, 0, 0)),
            scratch_shapes=[pltpu.VMEM((n, half, c), F32), pltpu.SemaphoreType.DMA((n,)),
                            pltpu.SemaphoreType.DMA((n,))]),
        compiler_params=_params("arbitrary"),
    )(place, grad, grad)


class _NoExchanges:
    pair_sums, landed = {}, {}

    def gather(self, names):
        return _NoRider()

    def pair(self, names, grads):
        return _NoRider()

    def paired(self, names, grads, theirs):
        pass

    def pair_now(self, names, grads):
        pass

    def chip(self, names):
        return _NoRider()


class _StepExchanges(_NoExchanges):
    def __init__(self, shards_bf16, place):
        self.shards, self.place = shards_bf16, place
        self.pair_sums, self.landed = {}, {}

    def gather(self, names):
        return _WeightGather([self.shards[n] for n in names])

    def pair(self, names, grads):
        return _PairExchange([grads[n] for n in names])

    def paired(self, names, grads, theirs):
        for n, other in zip(names, theirs):
            self.pair_sums[n] = _pair_sum(f"pair_sum_{n}", self.place, grads[n], other)

    def pair_now(self, names, grads):
        for n in names:
            self.pair_sums[n] = _exchange_pair_sum(f"pair_sum_{n}", self.place, grads[n])

    def chip(self, names):
        return _ChipExchange([self.pair_sums[n] for n in names])


SUM_ROWS = 32


def _finish_gradients(place, pair_sums, landed, vec):
    n_w = len(pair_sums)
    rows = vec.shape[0]
    halves = [s.shape[1:] for s in pair_sums]

    def body(place_ref, *refs):
        sums, lands, v_ref = refs[:n_w], refs[n_w:2 * n_w], refs[2 * n_w]
        outs, o_ref = refs[2 * n_w + 1:3 * n_w + 1], refs[3 * n_w + 1]
        stage = refs[3 * n_w + 2:4 * n_w + 2]
        kept, half_send, half_recv, slots, core_sums, vec_send, vec_recv, sum_send, sum_recv = refs[4 * n_w + 2:]
        x, y, c = _place()
        my_chip, sibling = 2 * x + y, (x, y, 1 - c)
        slots[my_chip] = v_ref[...]
        spread = []
        for p, (ox, oy) in enumerate(_other_chips(x, y)):
            here = slots.at[2 * ox + oy]
            spread.append((_remote(v_ref, slots.at[my_chip], vec_send.at[p], vec_recv.at[p], (ox, oy, c)),
                           _remote(here, here, vec_send.at[p], vec_recv.at[p], (ox, oy, c))))
        for send, _ in spread:
            send.start()
        copies = []
        for w, (h, _) in enumerate(halves):
            step = SUM_ROWS if h % SUM_ROWS == 0 else h

            def sum_rows(i, _, w=w, step=step):
                r = pl.ds(pl.multiple_of(i * step, step), step)
                stage[w][r, :] = functools.reduce(lambda total, p: total + lands[w][p, r, :].astype(F32), range(3),
                                                  sums[w][0, r, :].astype(F32))
                return 0

            lax.fori_loop(0, h // step, sum_rows, 0)
            mine, theirs = outs[w].at[pl.ds(c * h, h)], outs[w].at[pl.ds((1 - c) * h, h)]
            copies.append((pltpu.make_async_copy(stage[w], mine, kept.at[w]),
                           _remote(stage[w], mine, half_send.at[w], half_recv.at[w], sibling),
                           _remote(theirs, theirs, half_send.at[w], half_recv.at[w], sibling)))
            copies[-1][0].start()
            copies[-1][1].start()
        for send, arrival in spread:
            arrival.wait_recv()
            send.wait_send()
        core_sums[c] = functools.reduce(lambda total, chip: total + slots[chip], range(1, N_CHIPS), slots[0])
        mine, theirs = core_sums.at[c], core_sums.at[1 - c]
        to_sibling = _remote(mine, mine, sum_send.at[0], sum_recv.at[0], sibling)
        to_sibling.start()
        _remote(theirs, theirs, sum_send.at[0], sum_recv.at[0], sibling).wait_recv()
        to_sibling.wait_send()
        o_ref[...] = core_sums[0] + core_sums[1]
        for keep, send, arrival in copies:
            keep.wait()
            arrival.wait_recv()
            send.wait_send()

    once = dict(pipeline_mode=pl.Buffered(1))
    vm = pl.BlockSpec((rows, LANES), lambda i, pr: (0, 0))
    res = pl.pallas_call(
        body, name="finish_gradients",
        grid_spec=pltpu.PrefetchScalarGridSpec(
            num_scalar_prefetch=1, grid=(1,),
            in_specs=[pl.BlockSpec((1,) + hc, lambda i, pr: (pr[1], 0, 0), **once) for hc in halves]
            + [pl.BlockSpec((3,) + hc, lambda i, pr: (0, 0, 0), **once) for hc in halves] + [vm],
            out_specs=[ANY] * n_w + [vm],
            scratch_shapes=[pltpu.VMEM(hc, F32) for hc in halves]
            + [pltpu.SemaphoreType.DMA((n_w,))] * 3
            + [pltpu.VMEM((N_CHIPS, rows, LANES), F32), pltpu.VMEM((2, rows, LANES), F32),
               pltpu.SemaphoreType.DMA((N_CHIPS - 1,)), pltpu.SemaphoreType.DMA((N_CHIPS - 1,)),
               pltpu.SemaphoreType.DMA((1,)), pltpu.SemaphoreType.DMA((1,))]),
        out_shape=[_sds((2 * h, cols), F32) for h, cols in halves] + [_sds(vec.shape, F32)],
        compiler_params=_params("arbitrary"),
    )(place, *pair_sums, *landed, vec)
    return res[:n_w], res[n_w]


def _row_tile(rows):
    fits = [tr for tr in range(16, min(rows, 512) + 1, 16) if rows % tr == 0]
    return max(fits) if fits else rows


def _pair_sum(name, place, grad, theirs):
    if grad.ndim == 2:
        return _pair_sum_joined(name, place, grad, theirs)
    n, r, c = grad.shape
    half = r // 2
    tr = _row_tile(half)
    nb = half // tr

    def body(place_ref, g_ref, t_ref, o_ref):
        o_ref[...] = (g_ref[...] + t_ref[...]).astype(BF16)

    return pl.pallas_call(
        body, name=name, out_shape=jax.ShapeDtypeStruct((n, half, c), BF16),
        grid_spec=pltpu.PrefetchScalarGridSpec(
            num_scalar_prefetch=1, grid=(n, nb),
            in_specs=[pl.BlockSpec((1, tr, c), lambda j, i, pr: (j, pr[0] * nb + i, 0)),
                      pl.BlockSpec((1, tr, c), lambda j, i, pr: (j, i, 0))],
            out_specs=pl.BlockSpec((1, tr, c), lambda j, i, pr: (j, i, 0))),
        compiler_params=_params("parallel", "parallel"),
    )(place, grad, theirs)


def _pair_sum_joined(name, place, grad, theirs):
    r, wide = grad.shape
    half, c = r // 2, wide // N_CHIPS
    tr = _row_tile(half)
    nb = half // tr

    def body(place_ref, g_ref, t_ref, o_ref):
        for j in range(N_CHIPS):
            cols = slice(j * c, (j + 1) * c)
            o_ref[j] = (g_ref[:, cols] + t_ref[:, cols]).astype(BF16)

    return pl.pallas_call(
        body, name=name, out_shape=jax.ShapeDtypeStruct((N_CHIPS, half, c), BF16),
        grid_spec=pltpu.PrefetchScalarGridSpec(
            num_scalar_prefetch=1, grid=(nb,),
            in_specs=[pl.BlockSpec((tr, wide), lambda i, pr: (pr[0] * nb + i, 0)),
                      pl.BlockSpec((tr, wide), lambda i, pr: (i, 0))],
            out_specs=pl.BlockSpec((N_CHIPS, tr, c), lambda i, pr: (0, i, 0))),
        compiler_params=_params("parallel"),
    )(place, grad, theirs)


MIXER = ("w_branch_a", "w_branch_b", "w_out")
FFN_PLE = ("w_ffn_gate", "w_ffn_up", "w_ffn_down", "w_ple_gate", "w_ple_proj")
LATE = MIXER + FFN_PLE
BIG = ("w_in",) + LATE
HELD_TRANSPOSED = ("w_ffn_gate", "w_ffn_up")
SMALL = ("norm_mix", "w_pool", "pool_scale", "norm_ffn", "norm_ple", "norm_final")


def _join_columns(w4):
    return jnp.concatenate([w4[j] for j in range(N_CHIPS)], axis=1)


def _sds(shape, dtype):
    return jax.ShapeDtypeStruct(shape, dtype)


def _local_step(x, p, target, wf, small, ex=None):
    t, d = x.shape
    w_pool_b = small["w_pool"].astype(BF16)
    dp = w_pool_b.shape[0] * w_pool_b.shape[1]

    ex = ex or _NoExchanges()
    h1, first = _norm_fwd("norm_mix", x, small["norm_mix"], rider=ex.gather(("w_in",)))
    wf = {**wf, **dict(zip(("w_in",), first))}
    w_in = wf["w_in"]
    u, q, kv, ga, gb = _mm(
        "proj", [h1], [w_in[j] for j in range(N_CHIPS)], "nn",
        [_sds((t, dp), F32), _sds((t, dp), BF16), _sds((t, d), BF16), _sds((t, d), BF16), _sds((t, d), BF16)],
        separate=True, epilogue=lambda uq, kv_, ga_, gb_: (uq[:, :dp], uq[:, dp:], kv_, ga_, gb_), tm=512)
    pooled, ya = _pool_fwd(u, w_pool_b, small["pool_scale"])
    n_pairs = dp // LANES
    yb, late = _attn_fwd(q, 0, kv, 0, n_pairs, n_pairs, rider=ex.gather(LATE))
    wf = {**wf, **dict(zip(LATE, late))}
    w_down = wf["w_ffn_down"].reshape(-1, d)
    dff = w_down.shape[0]
    w_gate_t, w_up_t = wf["w_ffn_gate"].reshape(dff, d), wf["w_ffn_up"].reshape(dff, d)
    w_a, w_b, w_pp = _join_columns(wf["w_branch_a"]), _join_columns(wf["w_branch_b"]), _join_columns(wf["w_ple_proj"])
    w_out = wf["w_out"].reshape(d, d)
    w_pg = wf["w_ple_gate"].reshape(d, d)
    def residual_norm(branch, xv, g, w):
        xn = xv + jnp.dot(branch.astype(BF16), w, preferred_element_type=F32)
        return xn, xn * lax.rsqrt(jnp.mean(xn * xn, axis=-1, keepdims=True) + RMS_EPS) * g

    def mixer_tail(tav, tbv, gav, gbv, xv, g, w):
        merged = _sigmoid(gav) * tav + _sigmoid(gbv) * tbv
        return (tav, tbv, merged) + residual_norm(merged, xv, g, w)

    def ffn_tail(gv, uv, xv, g, w):
        act = gv * _sigmoid(gv) * uv
        return (gv, uv, act) + residual_norm(act, xv, g, w)

    stream = [_sds((t, d), F32), _sds((t, d), BF16)]
    ta, tb, merged, x1, h2 = _mm(
        "mixer_out", [ya, yb], [w_a, w_b], "nn", [_sds((t, d), BF16)] * 3 + stream,
        extras=[ga, gb, x, small["norm_ffn"]], wholes=[w_out], separate=True, epilogue=mixer_tail, tm=512)
    gate, up, act, x2, h3 = _mm(
        "ffn", [h2], [w_gate_t, w_up_t], "nt", [_sds((t, dff), BF16)] * 3 + stream,
        extras=[x1, small["norm_ple"]], wholes=[w_down], separate=True, epilogue=ffn_tail, tm=256)
    dx2, dx2_b, d_pp, d_gp, d_norm_final, loss_row, d_norm_ple = _mm(
        "ple_loss", [h3, p], [w_pg, w_pp], "nn", stream + [_sds((t, d), BF16)] * 2,
        extras=[x2, target, small["norm_final"].reshape(1, d), small["norm_ple"]], wholes=[w_pg], separate=True,
        epilogue=_ple_and_loss, sum_shapes=[_sds((1, d), F32)] * 3, tm=512)

    def through_norm(dh, xv, g, dres):
        dx, d_gain = _rms_norm_bwd(dh, xv, g)
        return dx + dres, dx + dres, d_gain

    gain_sum = [_sds((1, d), F32)]
    g_w_pp, g_w_pg = _mm_tn("g_ple", [p, h3], [d_pp, d_gp])

    def ffn_bwd(d_act, gv, uv, xv, g, dres, wg_t, wu_t):
        s = _sigmoid(gv)
        d_gate, d_up = d_act * uv * (s * (1.0 + gv * (1.0 - s))), d_act * (gv * s)
        dh2 = (jnp.dot(d_gate.astype(BF16), wg_t, preferred_element_type=F32)
               + jnp.dot(d_up.astype(BF16), wu_t, preferred_element_type=F32))
        return (d_gate, d_up) + through_norm(dh2, xv, g, dres)

    d_gate, d_up, dx1, dx1_b, d_norm_ffn = _mm(
        "ffn_bwd", [dx2_b], [w_down], "nt", [_sds((t, dff), BF16)] * 2 + stream,
        extras=[gate, up, x1, small["norm_ffn"], dx2], wholes=[w_gate_t, w_up_t], epilogue=ffn_bwd,
        sum_shapes=gain_sum, tm=256)
    g_w_down, = _mm_tn("g_ffn_down", [act], [dx2_b], tmm=512)
    g_w_gate_t, g_w_up_t = _mm_tn("g_ffn_gate_up", [d_gate, d_up], [h2], k_blocks=2)

    def merge_bwd(acc, tav, tbv, gav, gbv):
        sa, sb = _sigmoid(gav), _sigmoid(gbv)
        return acc * sa, acc * sb, acc * tav * sa * (1.0 - sa), acc * tbv * sb * (1.0 - sb)

    big = {
        "w_ffn_gate": g_w_gate_t.reshape(wf["w_ffn_gate"].shape), "w_ffn_up": g_w_up_t.reshape(wf["w_ffn_up"].shape),
        "w_ffn_down": g_w_down.reshape(wf["w_ffn_down"].shape),
        "w_ple_gate": g_w_pg.reshape(wf["w_ple_gate"].shape), "w_ple_proj": g_w_pp,
    }
    (d_ta, d_tb, d_ga, d_gb), theirs = _mm(
        "d_merged", [dx1_b], [w_out], "nt", [_sds((t, d), BF16)] * 4, extras=[ta, tb, ga, gb], epilogue=merge_bwd,
        tm=512, rider=ex.pair(FFN_PLE, big))
    ex.paired(FFN_PLE, big, theirs)
    g_w_out, big["w_branch_a"], big["w_branch_b"] = _mm_tn("g_mixer", [merged, ya, yb], [dx1_b, d_ta, d_tb])
    big["w_out"] = g_w_out.reshape(wf["w_out"].shape)
    (d_ya, d_yb), theirs = _mm(
        "d_branches", [d_ta, d_tb], [w_a, w_b], "nt", [_sds((t, dp), F32), _sds((t, dp), BF16)], separate=True,
        rider=ex.pair(MIXER, big))
    ex.paired(MIXER, big, theirs)
    d_u, g_w_pool, d_pool_scale = _pool_bwd(d_ya, pooled, w_pool_b, small["pool_scale"])
    (d_q, d_k, d_v), landed = _attn_bwd(q, 0, kv, 0, n_pairs, d_yb, n_pairs, rider=ex.chip(LATE))
    ex.landed.update(zip(LATE, landed))
    d_proj = [(d_u, d_q), (d_k, d_v), d_ga, d_gb]
    big["w_in"], = _mm_tn("g_w_in", [h1], d_proj, tmm=512, stacked=True)
    ex.pair_now(("w_in",), big)
    (grad_x, d_norm_mix), landed = _mm(
        "d_h1", d_proj, [w_in[j] for j in range(N_CHIPS)], "nt", [_sds((t, d), F32)],
        extras=[x, small["norm_mix"], dx1], epilogue=lambda dh, xv, g, dres: through_norm(dh, xv, g, dres)[1:],
        sum_shapes=gain_sum, tm=512, rider=ex.chip(("w_in",)))
    ex.landed.update(zip(("w_in",), landed))
    small_g = {"norm_mix": d_norm_mix, "w_pool": g_w_pool, "pool_scale": d_pool_scale, "norm_ffn": d_norm_ffn,
               "norm_ple": d_norm_ple, "norm_final": d_norm_final}
    return grad_x, big, small_g, loss_row


def _pack_small(small_g, loss_row):
    parts, layout = [], []
    for name in SMALL + ("loss",):
        v = (loss_row if name == "loss" else small_g[name]).reshape(-1, LANES)
        pad = (-v.shape[0]) % 8
        if pad:
            v = jnp.concatenate([v, jnp.zeros((pad, LANES), F32)], axis=0)
        layout.append((name, sum(q.shape[0] for q in parts), v.shape[0]))
        parts.append(v)
    return jnp.concatenate(parts, axis=0), layout


def kernel(x, p, norm_mix, w_in, w_pool, pool_scale, w_branch_a, w_branch_b, w_out, norm_ffn, w_ffn_gate, w_ffn_up, w_ffn_down, norm_ple, w_ple_gate, w_ple_proj, norm_final, loss_target, m_norm_mix, m_w_in, m_w_pool, m_pool_scale, m_w_branch_a, m_w_branch_b, m_w_out, m_norm_ffn, m_w_ffn_gate, m_w_ffn_up, m_w_ffn_down, m_norm_ple, m_w_ple_gate, m_w_ple_proj, m_norm_final, v_norm_mix, v_w_in, v_w_pool, v_pool_scale, v_w_branch_a, v_w_branch_b, v_w_out, v_norm_ffn, v_w_ffn_gate, v_w_ffn_up, v_w_ffn_down, v_norm_ple, v_w_ple_gate, v_w_ple_proj, v_norm_final):
    given = dict(locals())
    order = ("norm_mix", "w_in", "w_pool", "pool_scale", "w_branch_a", "w_branch_b", "w_out", "norm_ffn", "w_ffn_gate",
             "w_ffn_up", "w_ffn_down", "norm_ple", "w_ple_gate", "w_ple_proj", "norm_final")
    t, d = x.shape[1], x.shape[2]
    def local(a, n):
        return jnp.swapaxes(a[0], 0, 1) if n in HELD_TRANSPOSED else a[0]

    def back(a, n):
        return (jnp.swapaxes(a, 0, 1) if n in HELD_TRANSPOSED else a)[None]

    shard = {n: local(given[n], n) for n in BIG}
    small = {"norm_mix": norm_mix, "w_pool": w_pool[0], "pool_scale": pool_scale, "norm_ffn": norm_ffn,
             "norm_ple": norm_ple, "norm_final": norm_final}

    place = jnp.stack([lax.axis_index("c"), 2 * lax.axis_index("x") + lax.axis_index("y")]).astype(jnp.int32)
    ex = _StepExchanges({n: shard[n].astype(BF16) for n in BIG}, place)
    grad_x, _, small_g, loss_row = _local_step(
        x.reshape(t, d), p.reshape(t, p.shape[-1]), loss_target.reshape(t, d), {}, small, ex)
    packed, layout = _pack_small(small_g, loss_row)
    filled, reduced = _finish_gradients(place, [ex.pair_sums[n] for n in BIG], [ex.landed[n] for n in BIG], packed)
    grads = dict(zip(BIG, filled))
    for name, start, rows in layout:
        if name == "loss":
            loss = jnp.sum(reduced[start:start + rows])
        else:
            n_el = small[name].size
            grads[name] = reduced[start:start + rows].reshape(-1)[:n_el]

    deltas, new_m, new_v = {}, {}, {}
    for n in order:
        if n in BIG:
            w, m, v = shard[n], local(given["m_" + n], n), local(given["v_" + n], n)
            dl, mn, vn = _adamw(f"adamw_{n}", w, grads[n], m, v)
            grads[n], deltas[n], new_m[n], new_v[n] = [back(a, n) for a in (grads[n], dl, mn, vn)]
        else:
            w, full = small[n], given[n].shape
            shape2 = (1, w.shape[0]) if w.ndim == 1 else (w.shape if w.ndim == 2 else (w.shape[0] * w.shape[1], w.shape[2]))
            dl, mn, vn = _adamw(f"adamw_{n}", w.reshape(shape2), grads[n].reshape(shape2),
                                given["m_" + n].reshape(shape2), given["v_" + n].reshape(shape2))
            grads[n], deltas[n], new_m[n], new_v[n] = [a.reshape(full) for a in (grads[n], dl, mn, vn)]

    return (loss, grad_x.reshape(x.shape), *[grads[n] for n in order], *[deltas[n] for n in order],
            *[new_m[n] for n in order], *[new_v[n] for n in order])
```

```python
import functools
import math

import jax
import jax.numpy as jnp
from jax import lax
from jax.experimental import pallas as pl
from jax.experimental.pallas import tpu as pltpu

F32 = jnp.float32
BF16 = jnp.bfloat16
MESH = pl.DeviceIdType.MESH

RMS_EPS = 1e-6
POOL_WINDOWS = (2, 4, 8, 16)
POOL_HALO = 16
HEAD_DIM = 64
LANES = 128
ATT_BLOCK = 256
ATT_CHAINS = 2
ATT_FWD_CHAINS = 4
ATT_CHUNK = 256
ATT_SLAB = 256
ATT_SCALE = 1.0 / math.sqrt(HEAD_DIM)
LOG2_E = 1.4426950408889634
ATT_EXIT_BELOW = -150.5
ADAM_LR, ADAM_B1, ADAM_B2, ADAM_EPS, ADAM_WD, ADAM_STEP = 0.001, 0.9, 0.999, 1e-08, 0.01, 10
V7X_VMEM_LIMIT_BYTES = 56 * 1024 * 1024
N_CHIPS = 4
N_DEV = 8


def _params(*semantics):
    return pltpu.CompilerParams(dimension_semantics=semantics, vmem_limit_bytes=V7X_VMEM_LIMIT_BYTES)


def _sigmoid(z):
    return 0.5 * jnp.tanh(0.5 * z) + 0.5


def _tiled_spec(shape, tm, tn, n_total, at):
    rows, width = shape
    if rows == 1:
        if width == n_total:
            return pl.BlockSpec((1, tn), at(lambda i, j: (0, j)))
        return pl.BlockSpec((1, width), at(lambda i, j: (0, 0)))
    if width == n_total:
        return pl.BlockSpec((tm, tn), at(lambda i, j: (i, j)))
    assert tn == n_total, "an operand narrower than the output needs whole output rows per tile"
    return pl.BlockSpec((tm, width), at(lambda i, j: (i, 0)))


def _column_pieces(operands):
    pieces = [tuple(a) if isinstance(a, (tuple, list)) else (a,) for a in operands]
    return [p for ps in pieces for p in ps], [len(ps) for ps in pieces]


def _load_bf16(refs, counts):
    tiles, k = [], 0
    for n in counts:
        parts = [r[...] for r in refs[k:k + n]]
        parts = [t if t.dtype == BF16 else t.astype(BF16) for t in parts]
        tiles.append(parts[0] if n == 1 else jnp.concatenate(parts, axis=1))
        k += n
    return tiles


def _mm(name, a_list, b_list, mode, out_shapes, epilogue=None, extras=(), tm=1024, tn=None, separate=False,
        sum_shapes=(), rider=None, wholes=()):
    flat_a, counts = _column_pieces(a_list)
    m_total = flat_a[0].shape[0]
    n_total = b_list[0].shape[1] if mode == "nn" else b_list[0].shape[0]
    tn = n_total if tn is None else tn
    tm = min(tm, m_total)
    assert m_total % tm == 0 and n_total % tn == 0 and (not sum_shapes or tn == n_total)
    n_a, n_b, n_extra, n_out = len(counts), len(b_list), len(extras), len(out_shapes)
    assert n_a in (1, n_b)
    dims = (((1,), (0,)), ((), ())) if mode == "nn" else (((1,), (1,)), ((), ()))
    with_rider = rider is not None
    rider = rider or _NoRider()
    grid = (n_total // tn, m_total // tm)

    def at(index):
        return lambda j, i: index(i, j)

    def body(*refs):
        ins, o_refs, _, riding = rider.split(refs, len(flat_a) + n_b + n_extra + len(wholes), n_out + len(sum_shapes))
        a_refs, b_refs = ins[:len(flat_a)], ins[len(flat_a):len(flat_a) + n_b]
        e_refs, w_refs = ins[len(flat_a) + n_b:len(flat_a) + n_b + n_extra], ins[len(flat_a) + n_b + n_extra:]
        at_first = (pl.program_id(0) == 0) & (pl.program_id(1) == 0)
        at_last = (pl.program_id(0) == grid[0] - 1) & (pl.program_id(1) == grid[1] - 1)
        top, bottom = rider.at_steps(riding, at_first, at_first, at_last)
        top()
        lefts = _load_bf16(a_refs, counts)
        products = [lax.dot_general(lefts[s % n_a], b_refs[s][...], dims, preferred_element_type=F32)
                    for s in range(n_b)]
        if not separate:
            products = [functools.reduce(lambda p, r: p + r, products)]
        extra_tiles = [e[...].astype(F32) for e in e_refs]
        outs = products if epilogue is None else epilogue(*products, *extra_tiles, *[w[...] for w in w_refs])
        for o_ref, o in zip(o_refs[:n_out], outs[:n_out]):
            o_ref[...] = o.astype(o_ref.dtype)
        if sum_shapes:
            @pl.when(pl.program_id(1) == 0)
            def _():
                for s_ref in o_refs[n_out:]:
                    s_ref[...] = jnp.zeros_like(s_ref)

            for s_ref, s in zip(o_refs[n_out:], outs[n_out:]):
                s_ref[...] += s
        bottom()

    once = dict(pipeline_mode=pl.Buffered(1)) if tn == n_total else {}
    in_specs = [pl.BlockSpec((tm, a.shape[1]), at(lambda i, j: (i, 0))) for a in flat_a]
    if mode == "nn":
        in_specs += [pl.BlockSpec((b.shape[0], tn), at(lambda i, j: (0, j)), **once) for b in b_list]
    else:
        in_specs += [pl.BlockSpec((tn, b.shape[1]), at(lambda i, j: (j, 0)), **once) for b in b_list]
    in_specs += [_tiled_spec(e.shape, tm, tn, n_total, at) for e in extras]
    in_specs += [pl.BlockSpec(w.shape, lambda j, i: (0, 0), pipeline_mode=pl.Buffered(1)) for w in wholes]
    out_specs = [_tiled_spec(o.shape, tm, tn, n_total, at) for o in out_shapes]
    out_specs += [pl.BlockSpec(s.shape, at(lambda i, j: (0, 0))) for s in sum_shapes]
    semantics = ("arbitrary", "arbitrary") if sum_shapes or rider.operands else ("parallel", "parallel")
    res = pl.pallas_call(
        body, name=name, grid=grid, in_specs=in_specs + [ANY] * len(rider.operands),
        out_specs=out_specs + [ANY] * len(rider.out_shapes),
        out_shape=list(out_shapes) + list(sum_shapes) + list(rider.out_shapes), scratch_shapes=list(rider.scratch),
        compiler_params=_params(*semantics),
    )(*flat_a, *b_list, *extras, *wholes, *rider.operands)
    n_own = len(out_shapes) + len(sum_shapes)
    return (res[:n_own], res[n_own:]) if with_rider else res


def _mm_tn(name, a_list, b_list, tmm=1024, stacked=False, k_blocks=1):
    flat_b, counts = _column_pieces(b_list)
    n_a, n_b = len(a_list), len(counts)
    n_prod = max(n_a, n_b)
    m_total = a_list[0].shape[0]
    ks = [a_list[s % n_a].shape[1] for s in range(n_prod)]
    widths = [sum(p.shape[1] for p in flat_b[sum(counts[:s]):sum(counts[:s + 1])]) for s in range(n_b)]
    widths = [widths[s % n_b] for s in range(n_prod)]
    tmm = min(tmm, m_total)
    assert m_total % tmm == 0 and all(k % k_blocks == 0 for k in ks)
    assert n_a in (1, n_prod) and n_b in (1, n_prod) and not (stacked and n_a > 1)

    def body(*refs):
        a_refs, b_refs, o_refs = refs[:n_a], refs[n_a:n_a + len(flat_b)], refs[n_a + len(flat_b):]

        @pl.when(pl.program_id(1) == 0)
        def _():
            for o_ref in o_refs:
                o_ref[...] = jnp.zeros_like(o_ref)

        lefts, rights = _load_bf16(a_refs, [1] * n_a), _load_bf16(b_refs, counts)
        for s in range(n_prod):
            product = lax.dot_general(lefts[s % n_a], rights[s % n_b], (((0,), (0,)), ((), ())),
                                      preferred_element_type=F32)
            if stacked:
                o_refs[0][s] += product
            else:
                o_refs[s][...] += product

    in_specs = [pl.BlockSpec((tmm, a.shape[1] // k_blocks), lambda kb, m: (m, kb)) for a in a_list]
    in_specs += [pl.BlockSpec((tmm, b.shape[1]), lambda kb, m: (m, 0)) for b in flat_b]
    if stacked:
        out_shape = [jax.ShapeDtypeStruct((n_prod, ks[0], widths[0]), F32)]
        out_specs = [pl.BlockSpec((n_prod, ks[0] // k_blocks, widths[0]), lambda kb, m: (0, kb, 0))]
    else:
        out_shape = [jax.ShapeDtypeStruct((k, w), F32) for k, w in zip(ks, widths)]
        out_specs = [pl.BlockSpec((k // k_blocks, w), lambda kb, m: (kb, 0)) for k, w in zip(ks, widths)]
    return pl.pallas_call(
        body, name=name, grid=(k_blocks, m_total // tmm), in_specs=in_specs, out_specs=out_specs, out_shape=out_shape,
        compiler_params=_params("arbitrary", "arbitrary"),
    )(*a_list, *flat_b)


def _rows(name, fn, ins, tile_outs, sum_outs=(), tr=512, rider=None):
    t_total = max(a.shape[0] for a in ins)
    tr = min(tr, t_total)
    assert t_total % tr == 0
    n_in, n_tile = len(ins), len(tile_outs)
    rider = rider or _NoRider()
    n_steps = t_total // tr

    def body(*refs):
        own_ins, own_outs, _, riding = rider.split(refs, n_in, n_tile + len(sum_outs))
        step = pl.program_id(0)
        top, bottom = rider.at_steps(riding, step == 0, step == n_steps - 1, step == n_steps - 1)
        top()
        refs = tuple(own_ins) + tuple(own_outs)
        outs = fn(*[r[...].astype(F32) for r in refs[:n_in]])
        for o_ref, o in zip(refs[n_in:n_in + n_tile], outs[:n_tile]):
            o_ref[...] = o.astype(o_ref.dtype)
        if sum_outs:
            @pl.when(pl.program_id(0) == 0)
            def _():
                for s_ref in refs[n_in + n_tile:]:
                    s_ref[...] = jnp.zeros_like(s_ref)

            for s_ref, s in zip(refs[n_in + n_tile:], outs[n_tile:]):
                s_ref[...] += s
        bottom()

    def spec(shape):
        if shape[0] == 1:
            return pl.BlockSpec(shape, lambda i: (0, 0))
        return pl.BlockSpec((tr, shape[1]), lambda i: (i, 0))

    return pl.pallas_call(
        body, name=name, grid=(n_steps,), in_specs=[spec(a.shape) for a in ins] + [ANY] * len(rider.operands),
        out_specs=[spec(o.shape) for o in tile_outs] + [spec(s.shape) for s in sum_outs] + [ANY] * len(rider.out_shapes),
        out_shape=list(tile_outs) + list(sum_outs) + list(rider.out_shapes), scratch_shapes=list(rider.scratch),
        compiler_params=_params("arbitrary" if sum_outs or rider.operands else "parallel"),
    )(*ins, *rider.operands)


def _norm_fwd(name, x, gain, rider=None):
    def fn(xv, g):
        inv = lax.rsqrt(jnp.mean(xv * xv, axis=-1, keepdims=True) + RMS_EPS)
        return (xv * inv * g,)

    res = _rows(name, fn, [x, gain], [jax.ShapeDtypeStruct(x.shape, BF16)], rider=rider)
    return res[0], res[1:]


def _rms_norm_bwd(dh, xv, g):
    inv = lax.rsqrt(jnp.mean(xv * xv, axis=-1, keepdims=True) + RMS_EPS)
    xn = xv * inv
    dxn = dh * g
    return inv * (dxn - xn * jnp.mean(dxn * xn, axis=-1, keepdims=True)), jnp.sum(dh * xn, axis=0, keepdims=True)


def _ple_and_loss(gv, pv, x2v, tv, g_final, g_ple, w_pg):
    d = x2v.shape[1]
    s = _sigmoid(gv)
    xv = x2v + s * pv
    inv = lax.rsqrt(jnp.mean(xv * xv, axis=-1, keepdims=True) + RMS_EPS)
    err = xv * inv * g_final - tv
    dx3, d_final = _rms_norm_bwd(err * (1.0 / d), xv, g_final)
    d_pp, d_gp = dx3 * s, dx3 * pv * s * (1.0 - s)
    dh3 = lax.dot_general(d_gp.astype(BF16), w_pg, (((1,), (1,)), ((), ())), preferred_element_type=F32)
    dx2, d_ple = _rms_norm_bwd(dh3, x2v, g_ple)
    dx2 = dx2 + dx3
    return dx2, dx2, d_pp, d_gp, d_final, (0.5 / d) * jnp.sum(err * err, axis=0, keepdims=True), d_ple


def _window_counts(t_pos, w):
    return jnp.minimum(t_pos + 1, w).astype(F32)


def _pool_fwd(u, w_pool, scale, tr=512):
    t_total, width = u.shape
    tr = min(tr, t_total)
    n_groups = len(POOL_WINDOWS)
    gdim = width // n_groups
    ext = tr + POOL_HALO

    def body(u_ref, halo_ref, w_ref, s_ref, pooled_ref, ya_ref):
        i = pl.program_id(0)
        halo = jnp.where(i == 0, 0.0, halo_ref[...])
        t_pos = i * tr + lax.broadcasted_iota(jnp.int32, (tr, 1), 0)
        for g, w in enumerate(POOL_WINDOWS):
            cols = slice(g * gdim, (g + 1) * gdim)
            main = u_ref[:, cols]
            win = jnp.concatenate([halo[:, cols], main], axis=0)
            span = 1
            while span < w:
                win = win + pltpu.roll(win, span, 0)
                span *= 2
            pooled = win[POOL_HALO:, :] * (1.0 / _window_counts(t_pos, w)) - main
            pooled_b = pooled.astype(BF16)
            pooled_ref[:, cols] = pooled_b
            mixed = jnp.dot(pooled_b, w_ref[g], preferred_element_type=F32)
            ya_ref[:, cols] = (mixed * s_ref[:, cols]).astype(BF16)

    hb = tr // POOL_HALO
    return pl.pallas_call(
        body, name="pool_fwd", grid=(t_total // tr,),
        in_specs=[pl.BlockSpec((tr, width), lambda i: (i, 0)),
                  pl.BlockSpec((POOL_HALO, width), lambda i: (jnp.maximum(i * hb - 1, 0), 0)),
                  pl.BlockSpec((n_groups, gdim, gdim), lambda i: (0, 0, 0)),
                  pl.BlockSpec((1, width), lambda i: (0, 0))],
        out_specs=[pl.BlockSpec((tr, width), lambda i: (i, 0)), pl.BlockSpec((tr, width), lambda i: (i, 0))],
        out_shape=[jax.ShapeDtypeStruct(u.shape, BF16), jax.ShapeDtypeStruct(u.shape, BF16)],
        compiler_params=_params("parallel"),
    )(u, u, w_pool, scale)


def _pool_bwd(dya, pooled, w_pool, scale, tr=512):
    t_total, width = dya.shape
    tr = min(tr, t_total)
    n_groups = len(POOL_WINDOWS)
    gdim = width // n_groups
    ext = tr + POOL_HALO
    n_tiles = t_total // tr

    def body(d_ref, halo_ref, p_ref, w_ref, s_ref, du_ref, dw_ref, ds_ref):
        i = pl.program_id(0)

        @pl.when(i == 0)
        def _():
            dw_ref[...] = jnp.zeros_like(dw_ref)
            ds_ref[...] = jnp.zeros_like(ds_ref)

        halo = jnp.where(i == n_tiles - 1, 0.0, halo_ref[...])
        t_pos = i * tr + lax.broadcasted_iota(jnp.int32, (ext, 1), 0)
        for g, w in enumerate(POOL_WINDOWS):
            cols = slice(g * gdim, (g + 1) * gdim)
            sc = s_ref[:, cols]
            d_main = d_ref[:, cols]
            pooled_b = p_ref[:, cols]
            mixed = jnp.dot(pooled_b, w_ref[g], preferred_element_type=F32)
            ds_ref[:, cols] += jnp.sum(d_main * mixed, axis=0, keepdims=True)
            dmix = (jnp.concatenate([d_main, halo[:, cols]], axis=0) * sc).astype(BF16)
            dw_ref[g] += lax.dot_general(pooled_b, dmix[:tr, :], (((0,), (0,)), ((), ())),
                                         preferred_element_type=F32)
            dpool = lax.dot_general(dmix, w_ref[g], (((1,), (1,)), ((), ())), preferred_element_type=F32)
            win = dpool * (1.0 / _window_counts(t_pos, w))
            span = 1
            while span < w:
                win = win + pltpu.roll(win, ext - span, 0)
                span *= 2
            du_ref[:, cols] = (win[:tr, :] - dpool[:tr, :]).astype(BF16)

    hb = tr // POOL_HALO
    last_halo = t_total // POOL_HALO - 1
    return pl.pallas_call(
        body, name="pool_bwd", grid=(n_tiles,),
        in_specs=[pl.BlockSpec((tr, width), lambda i: (i, 0)),
                  pl.BlockSpec((POOL_HALO, width), lambda i: (jnp.minimum((i + 1) * hb, last_halo), 0)),
                  pl.BlockSpec((tr, width), lambda i: (i, 0)),
                  pl.BlockSpec((n_groups, gdim, gdim), lambda i: (0, 0, 0)),
                  pl.BlockSpec((1, width), lambda i: (0, 0))],
        out_specs=[pl.BlockSpec((tr, width), lambda i: (i, 0)),
                   pl.BlockSpec((n_groups, gdim, gdim), lambda i: (0, 0, 0)),
                   pl.BlockSpec((1, width), lambda i: (0, 0))],
        out_shape=[jax.ShapeDtypeStruct(dya.shape, BF16), jax.ShapeDtypeStruct((n_groups, gdim, gdim), F32),
                   jax.ShapeDtypeStruct((1, width), F32)],
        compiler_params=_params("arbitrary"),
    )(dya, dya, pooled, w_pool, scale)


def _head_masks():
    lane = lax.broadcasted_iota(jnp.int32, (1, LANES), 1)
    return lane < HEAD_DIM


def _stack_heads(tile, first):
    zero = jnp.zeros_like(tile)
    return jnp.concatenate([jnp.where(first, tile, zero), jnp.where(first, zero, tile)], axis=0)


def _causal_mask(t_pos, k_start):
    col = lax.broadcasted_iota(jnp.int32, (1, 2 * ATT_SLAB), 1)
    return k_start + (col & (ATT_SLAB - 1)) < t_pos


def _slab_scores(q, kd, mask):
    z2 = lax.dot_general(q, kd, (((1,), (1,)), ((), ())), preferred_element_type=F32) * LOG2_E
    log_hit = jnp.minimum(z2, 0.0) - jnp.log2(1.0 + jnp.exp2(-jnp.abs(z2)))
    log_fail = log_hit - z2
    return log_hit, (log_fail if mask is None else jnp.where(mask, log_fail, 0.0))


def _weights(log_hit, suffix, mask):
    arg = log_hit + suffix
    return jnp.exp2(arg if mask is None else jnp.where(mask, arg, -1e30))


def _tri(upper):
    r = lax.broadcasted_iota(jnp.int32, (ATT_CHUNK, ATT_CHUNK), 0)
    c = lax.broadcasted_iota(jnp.int32, (ATT_CHUNK, ATT_CHUNK), 1)
    return jnp.where(r > c if upper else r < c, 1.0, 0.0).astype(BF16)


def _tri_spec():
    return pl.BlockSpec((ATT_CHUNK, ATT_CHUNK), lambda h, i: (0, 0), pipeline_mode=pl.Buffered(1))


def _scan_chunk(v, tri):
    return jnp.dot(v.astype(BF16), tri, preferred_element_type=F32)


def _lane_bcast(col):
    return jnp.broadcast_to(col, (col.shape[0], LANES))


def _scan_slab(v, tri, carries, from_right):
    n_chunks = ATT_SLAB // ATT_CHUNK
    edge = 0 if from_right else ATT_CHUNK - 1
    parts, new_carries = [None] * (2 * n_chunks), []
    for head in range(2):
        run = carries[head]
        for c in (reversed(range(n_chunks)) if from_right else range(n_chunks)):
            lo_col = head * ATT_SLAB + c * ATT_CHUNK
            vc = v[:, lo_col:lo_col + ATT_CHUNK]
            sc = _scan_chunk(vc, tri)
            parts[head * n_chunks + c] = sc + jnp.concatenate([run] * (ATT_CHUNK // LANES), axis=1)
            run = run + _lane_bcast(sc[:, edge:edge + 1] + vc[:, edge:edge + 1])
        new_carries.append(run)
    return jnp.concatenate(parts, axis=1), new_carries


def _fold_heads(stacked, first):
    s = stacked.shape[0] // 2
    return jnp.where(first, stacked[:s], stacked[s:])


class _NoRider:
    operands, out_shapes, scratch = (), (), ()

    def split(self, refs, n_base_in, n_base_out):
        n_in, n_out, n_sem = len(self.operands), len(self.out_shapes), len(self.scratch)
        a = n_base_in + n_in
        b = a + n_base_out + n_out
        mine = (refs[n_base_in:a], refs[a + n_base_out:b], refs[b:b + n_sem])
        return refs[:n_base_in], refs[a:a + n_base_out], refs[b + n_sem:], mine

    def start(self, ins, outs, sems):
        pass

    def relay(self, ins, outs, sems):
        pass

    def finish(self, ins, outs, sems):
        pass

    def at_steps(self, refs, first_step, relay_step, last_step):
        if not self.operands:
            return (lambda: None), (lambda: None)

        def top():
            pl.when(first_step)(lambda: self.start(*refs))
            pl.when(relay_step)(lambda: self.relay(*refs))

        return top, lambda: pl.when(last_step)(lambda: self.finish(*refs))


def _attn_fwd(q_src, q_col, kv_src, k_col, v_col, n_pairs=4, rider=_NoRider()):
    t_total = q_src.shape[0]
    blk = ATT_BLOCK
    n_chains = ATT_FWD_CHAINS if t_total % (ATT_FWD_CHAINS * blk) == 0 else ATT_CHAINS
    n_steps = t_total // (n_chains * blk)
    assert t_total % ATT_SLAB == 0 and ATT_SLAB == ATT_BLOCK

    def body(*refs):
        (q_ref, k_ref, v_ref, suffix_ref), (o_ref,), _, riding = rider.split(refs, 4, 1)
        h, ii = pl.program_id(0), pl.program_id(1)
        top, bottom = rider.at_steps(riding, (h == 0) & (ii == 0), (h == n_pairs - 1) & (ii == 0),
                                     (h == n_pairs - 1) & (ii == n_steps - 1))
        top()
        first = _head_masks()
        suffix_tri = suffix_ref[...]
        blocks = [n_chains * ii + c for c in range(n_chains)]
        qs = [q_ref[c * blk:(c + 1) * blk, :] * ATT_SCALE for c in range(n_chains)]
        t_pos = [b * blk + lax.broadcasted_iota(jnp.int32, (blk, 1), 0) for b in blocks]

        def one(c, t, chain, on_diagonal):
            _, acc, right_a, right_b = chain
            k_start = pl.multiple_of((blocks[c] - t) * ATT_SLAB, ATT_SLAB)
            kd = _stack_heads(k_ref[pl.ds(k_start, ATT_SLAB), :], first)
            vd = _stack_heads(v_ref[pl.ds(k_start, ATT_SLAB), :], first)
            mask = _causal_mask(t_pos[c], k_start) if on_diagonal else None
            log_hit, log_fail = _slab_scores(qs[c], kd, mask)
            suffix, (right_a, right_b) = _scan_slab(log_fail, suffix_tri, (right_a, right_b), from_right=True)
            a = _weights(log_hit, suffix, mask).astype(BF16)
            acc = acc + jnp.dot(a, vd, preferred_element_type=F32)
            return jnp.max(jnp.maximum(right_a, right_b)), acc, right_a, right_b

        def step(state, on_diagonal):
            t, chains = state
            return t + 1, tuple(one(c, t, chains[c], on_diagonal) for c in range(n_chains))

        def more(state):
            t, chains = state
            return (t <= blocks[0]) & (functools.reduce(jnp.maximum, [ch[0] for ch in chains]) > ATT_EXIT_BELOW)

        zero = jnp.zeros((blk, LANES), F32)
        state = step((0, ((jnp.float32(0.0), zero, zero, zero),) * n_chains), on_diagonal=True)
        t, chains = lax.while_loop(more, functools.partial(step, on_diagonal=False), state)
        for c in range(n_chains):
            chain = chains[c]
            if c:
                _, chain = lax.while_loop(
                    lambda s, c=c: (s[0] <= blocks[c]) & (s[1][0] > ATT_EXIT_BELOW),
                    lambda s, c=c: (s[0] + 1, one(c, s[0], s[1], False)), (t, chain))
            o_ref[c * blk:(c + 1) * blk, :] = chain[1].astype(BF16)
        bottom()

    rows = n_chains * blk
    res = pl.pallas_call(
        body, name="attn_fwd", grid=(n_pairs, n_steps),
        in_specs=[pl.BlockSpec((rows, LANES), lambda h, i: (i, q_col + h)),
                  pl.BlockSpec((t_total, LANES), lambda h, i: (0, k_col + h)),
                  pl.BlockSpec((t_total, LANES), lambda h, i: (0, v_col + h)), _tri_spec()] + [ANY] * len(rider.operands),
        out_specs=[pl.BlockSpec((rows, LANES), lambda h, i: (i, h))] + [ANY] * len(rider.out_shapes),
        out_shape=[jax.ShapeDtypeStruct((t_total, n_pairs * LANES), BF16)] + list(rider.out_shapes),
        scratch_shapes=list(rider.scratch),
        compiler_params=_params("arbitrary", "arbitrary"),
    )(q_src, kv_src, kv_src, _tri(upper=True), *rider.operands)
    return res[0], res[1:]


def _attn_bwd(q_src, q_col, kv_src, k_col, v_col, dy, n_pairs=4, rider=_NoRider()):
    t_total = q_src.shape[0]
    blk = ATT_BLOCK
    n_steps = t_total // (ATT_CHAINS * blk)
    n_slabs = t_total // ATT_SLAB
    assert t_total % ATT_SLAB == 0 and ATT_SLAB == ATT_BLOCK

    def body(*refs):
        ins, (dq_ref, dk_ref, dv_ref), (g_s, dk_acc, dv_acc), riding = rider.split(refs, 6, 3)
        q_ref, dy_ref, k_ref, v_ref, suffix_ref, prefix_ref = ins
        h, ii = pl.program_id(0), pl.program_id(1)
        top, bottom = rider.at_steps(riding, (h == 0) & (ii == 0), (h == n_pairs - 1) & (ii == 0),
                                     (h == n_pairs - 1) & (ii == n_steps - 1))
        top()

        @pl.when(ii == 0)
        def _():
            dk_acc[...] = jnp.zeros_like(dk_acc)
            dv_acc[...] = jnp.zeros_like(dv_acc)

        first = _head_masks()
        suffix_tri = suffix_ref[...]
        prefix_tri = prefix_ref[...]
        blocks = [ATT_CHAINS * ii + c for c in range(ATT_CHAINS)]
        rows = [slice(c * blk, (c + 1) * blk) for c in range(ATT_CHAINS)]
        qs = [q_ref[r, :] * ATT_SCALE for r in rows]
        dys = [dy_ref[r, :] for r in rows]
        t_pos = [b * blk + lax.broadcasted_iota(jnp.int32, (blk, 1), 0) for b in blocks]

        def one1(c, t, chain, on_diagonal):
            _, right_a, right_b = chain
            slab = blocks[c] - t
            k_start = pl.multiple_of(slab * ATT_SLAB, ATT_SLAB)
            kd = _stack_heads(k_ref[pl.ds(k_start, ATT_SLAB), :], first)
            vd = _stack_heads(v_ref[pl.ds(k_start, ATT_SLAB), :], first)
            mask = _causal_mask(t_pos[c], k_start) if on_diagonal else None
            log_hit, log_fail = _slab_scores(qs[c], kd, mask)
            suffix, (right_a, right_b) = _scan_slab(log_fail, suffix_tri, (right_a, right_b), from_right=True)
            a = _weights(log_hit, suffix, mask)
            da = lax.dot_general(dys[c], vd, (((1,), (1,)), ((), ())), preferred_element_type=F32)
            g_s[c, slab] = (da * a).astype(BF16)
            dv_acc[pl.ds(k_start, ATT_SLAB), :] += _fold_heads(lax.dot_general(
                a.astype(BF16), dys[c], (((0,), (0,)), ((), ())), preferred_element_type=F32), first)
            return jnp.max(jnp.maximum(right_a, right_b)), right_a, right_b

        def step1(state, on_diagonal):
            t, chains = state
            return t + 1, tuple(one1(c, t, chains[c], on_diagonal) for c in range(ATT_CHAINS))

        def more(state):
            t, chains = state
            return (t <= blocks[0]) & (functools.reduce(jnp.maximum, [ch[0] for ch in chains]) > ATT_EXIT_BELOW)

        zero = jnp.zeros((blk, LANES), F32)
        state = step1((0, ((jnp.float32(0.0), zero, zero),) * ATT_CHAINS), on_diagonal=True)
        joint, chains = lax.while_loop(more, functools.partial(step1, on_diagonal=False), state)
        done = [joint]
        for c in range(1, ATT_CHAINS):
            done.append(lax.while_loop(
                lambda s, c=c: (s[0] <= blocks[c]) & (s[1][0] > ATT_EXIT_BELOW),
                lambda s, c=c: (s[0] + 1, one1(c, s[0], s[1], False)), (joint, chains[c]))[0])

        def one2(c, t, carry, on_diagonal):
            dq, left_a, left_b = carry
            slab = blocks[c] - t
            k_start = pl.multiple_of(slab * ATT_SLAB, ATT_SLAB)
            kd = _stack_heads(k_ref[pl.ds(k_start, ATT_SLAB), :], first)
            g = g_s[c, slab]
            sig = _sigmoid(lax.dot_general(qs[c], kd, (((1,), (1,)), ((), ())), preferred_element_type=F32))
            prefix, (left_a, left_b) = _scan_slab(g, prefix_tri, (left_a, left_b), from_right=False)
            dz = g * (1.0 - sig) - sig * prefix
            if on_diagonal:
                dz = jnp.where(_causal_mask(t_pos[c], k_start), dz, 0.0)
            dz = dz.astype(BF16)
            dq = dq + jnp.dot(dz, kd, preferred_element_type=F32)
            dk_acc[pl.ds(k_start, ATT_SLAB), :] += _fold_heads(lax.dot_general(
                dz, qs[c], (((0,), (0,)), ((), ())), preferred_element_type=F32), first)
            return dq, left_a, left_b

        carries = [(zero, zero, zero)]
        for c in range(1, ATT_CHAINS):
            carries.append(lax.fori_loop(
                0, done[c] - joint, lambda n, carry, c=c: one2(c, done[c] - 1 - n, carry, False), (zero, zero, zero)))
        carries = lax.fori_loop(
            0, joint - 1,
            lambda n, cs: tuple(one2(c, joint - 1 - n, cs[c], False) for c in range(ATT_CHAINS)), tuple(carries))
        for c in range(ATT_CHAINS):
            dq_ref[rows[c], :] = (one2(c, 0, carries[c], True)[0] * ATT_SCALE).astype(BF16)

        @pl.when(ii == n_steps - 1)
        def _():
            dk_ref[...] = dk_acc[...].astype(BF16)
            dv_ref[...] = dv_acc[...].astype(BF16)

        bottom()

    out = jax.ShapeDtypeStruct((t_total, n_pairs * LANES), BF16)
    n_rows = ATT_CHAINS * blk
    whole = dict(pipeline_mode=pl.Buffered(1))
    res = pl.pallas_call(
        body, name="attn_bwd", grid=(n_pairs, n_steps),
        in_specs=[pl.BlockSpec((n_rows, LANES), lambda h, i: (i, q_col + h)),
                  pl.BlockSpec((n_rows, LANES), lambda h, i: (i, h)),
                  pl.BlockSpec((t_total, LANES), lambda h, i: (0, k_col + h), **whole),
                  pl.BlockSpec((t_total, LANES), lambda h, i: (0, v_col + h), **whole), _tri_spec(), _tri_spec()]
        + [ANY] * len(rider.operands),
        out_specs=[pl.BlockSpec((n_rows, LANES), lambda h, i: (i, h)),
                   pl.BlockSpec((t_total, LANES), lambda h, i: (0, h)),
                   pl.BlockSpec((t_total, LANES), lambda h, i: (0, h))] + [ANY] * len(rider.out_shapes),
        out_shape=[out, out, out] + list(rider.out_shapes),
        scratch_shapes=list(rider.scratch) + [pltpu.VMEM((ATT_CHAINS, n_slabs, blk, 2 * ATT_SLAB), BF16),
                                              pltpu.VMEM((t_total, LANES), F32), pltpu.VMEM((t_total, LANES), F32)],
        compiler_params=_params("arbitrary", "arbitrary"),
    )(q_src, dy, kv_src, kv_src, _tri(upper=True), _tri(upper=False), *rider.operands)
    return res[:3], res[3:]


def _adamw(name, w, g, m, v):
    def fn(wv, gv, mv, vv):
        mn = ADAM_B1 * mv + (1.0 - ADAM_B1) * gv
        vn = ADAM_B2 * vv + (1.0 - ADAM_B2) * (gv * gv)
        m_hat = mn / (1.0 - ADAM_B1 ** ADAM_STEP)
        v_hat = vn / (1.0 - ADAM_B2 ** ADAM_STEP)
        return -ADAM_LR * (m_hat / (jnp.sqrt(v_hat) + ADAM_EPS) + ADAM_WD * wv), mn, vn

    rows = w.shape[0]
    tr = _row_tile(rows)
    shp = jax.ShapeDtypeStruct(w.shape, F32)
    if rows == 1:
        def body(w_ref, g_ref, m_ref, v_ref, d_ref, mo_ref, vo_ref):
            d, mn, vn = fn(w_ref[...], g_ref[...], m_ref[...], v_ref[...])
            d_ref[...], mo_ref[...], vo_ref[...] = d, mn, vn

        return pl.pallas_call(body, name=name, out_shape=[shp, shp, shp])(w, g, m, v)
    return _rows(name, fn, [w, g, m, v], [shp, shp, shp], tr=tr)


def _place():
    return lax.axis_index("x"), lax.axis_index("y"), lax.axis_index("c")


def _other_chips(x, y):
    return [(1 - x, y), (x, 1 - y), (1 - x, 1 - y)]


ANY = pl.BlockSpec(memory_space=pl.ANY)


def _remote(src, dst, send_sem, recv_sem, to):
    return pltpu.make_async_remote_copy(src_ref=src, dst_ref=dst, send_sem=send_sem, recv_sem=recv_sem,
                                        device_id=to, device_id_type=MESH)


class _WeightGather(_NoRider):
    def __init__(self, shards):
        n_w = len(shards)
        self.operands = list(shards)
        self.out_shapes = [jax.ShapeDtypeStruct((N_CHIPS,) + s.shape, s.dtype) for s in shards]
        self.scratch = [pltpu.SemaphoreType.DMA((3, n_w))] * 4 + [pltpu.SemaphoreType.DMA((n_w,))] * 2

    def _copies(self, ins, outs, sems):
        send_sems, recv_sems, relay_send, relay_recv, own_send, own_recv = sems
        x, y, c = _place()
        my_chip, sibling = 2 * x + y, (x, y, 1 - c)
        n_w = len(ins)

        def half(w, chip, core):
            h = self.operands[w].shape[0] // 2
            return outs[w].at[chip, pl.ds(core * h, h)]

        own = [_remote(ins[w], outs[w].at[my_chip], own_send.at[w], own_recv.at[w], sibling) for w in range(n_w)]
        sends, landed, relays, relayed = [], [], [], []
        for p, (ox, oy) in enumerate(_other_chips(x, y)):
            for w in range(n_w):
                h = self.operands[w].shape[0] // 2
                sends.append(_remote(ins[w].at[pl.ds(c * h, h)], half(w, my_chip, c), send_sems.at[p, w],
                                     recv_sems.at[p, w], (ox, oy, c)))
                here = half(w, 2 * ox + oy, c)
                landed.append(_remote(here, here, send_sems.at[p, w], recv_sems.at[p, w], (ox, oy, c)))
                relays.append(_remote(here, here, relay_send.at[p, w], relay_recv.at[p, w], sibling))
                there = half(w, 2 * ox + oy, 1 - c)
                relayed.append(_remote(there, there, relay_send.at[p, w], relay_recv.at[p, w], sibling))
        return own, sends, landed, relays, relayed

    def start(self, ins, outs, sems):
        own, sends, _, _, _ = self._copies(ins, outs, sems)
        for cp in own + sends:
            cp.start()

    def relay(self, ins, outs, sems):
        _, _, landed, relays, _ = self._copies(ins, outs, sems)
        for arrival, cp in zip(landed, relays):
            arrival.wait_recv()
            cp.start()

    def finish(self, ins, outs, sems):
        own, sends, _, relays, relayed = self._copies(ins, outs, sems)
        for arrival in relayed:
            arrival.wait_recv()
        for cp in sends + relays:
            cp.wait_send()
        for cp in own:
            cp.wait()


class _ChipExchange(_NoRider):
    def __init__(self, pair_sums):
        n_w = len(pair_sums)
        self.operands = list(pair_sums)
        self.out_shapes = [jax.ShapeDtypeStruct((3,) + s.shape[1:], s.dtype) for s in pair_sums]
        self.scratch = [pltpu.SemaphoreType.DMA((3, n_w))] * 2

    def _copies(self, ins, outs, sems):
        send_sems, recv_sems = sems
        x, y, c = _place()
        return [_remote(ins[w].at[2 * ox + oy], outs[w].at[p], send_sems.at[p, w], recv_sems.at[p, w], (ox, oy, c))
                for p, (ox, oy) in enumerate(_other_chips(x, y)) for w in range(len(ins))]

    def start(self, ins, outs, sems):
        for cp in self._copies(ins, outs, sems):
            cp.start()

    def finish(self, ins, outs, sems):
        for cp in self._copies(ins, outs, sems):
            cp.wait()


class _PairExchange(_NoRider):
    def __init__(self, grads):
        n_w = len(grads)
        self.operands = list(grads)
        self.out_shapes = [jax.ShapeDtypeStruct(g.shape[:-2] + (g.shape[-2] // 2, g.shape[-1]), F32) for g in grads]
        self.scratch = [pltpu.SemaphoreType.DMA((n_w,))] * 2

    def _copies(self, ins, theirs, sems):
        send_sems, recv_sems = sems
        x, y, c = _place()
        sends = []
        for w, g in enumerate(self.operands):
            rows = pl.ds((1 - c) * (g.shape[-2] // 2), g.shape[-2] // 2)
            src = ins[w].at[:, rows, :] if g.ndim == 3 else ins[w].at[rows, :]
            sends.append(_remote(src, theirs[w], send_sems.at[w], recv_sems.at[w], (x, y, 1 - c)))
        return sends

    def start(self, ins, outs, sems):
        for cp in self._copies(ins, outs, sems):
            cp.start()

    def finish(self, ins, outs, sems):
        for cp in self._copies(ins, outs, sems):
            cp.wait()


def _exchange_pair_sum(name, place, grad):
    n, r, c = grad.shape
    half = r // 2

    def body(place_ref, g_all, g_ref, o_ref, theirs, send_sems, recv_sems):
        j = pl.program_id(0)
        x, y, core = _place()

        def copy(k):
            return _remote(g_all.at[k, pl.ds((1 - core) * half, half)], theirs.at[k], send_sems.at[k],
                           recv_sems.at[k], (x, y, 1 - core))

        @pl.when(j == 0)
        def _():
            for k in range(n):
                copy(k).start()

        copy(j).wait_recv()
        o_ref[0] = (g_ref[0] + theirs[j]).astype(BF16)

        @pl.when(j == n - 1)
        def _():
            for k in range(n):
                copy(k).wait_send()

    return pl.pallas_call(
        body, name=name, out_shape=jax.ShapeDtypeStruct((n, half, c), BF16),
        grid_spec=pltpu.PrefetchScalarGridSpec(
            num_scalar_prefetch=1, grid=(n,),
            in_specs=[ANY, pl.BlockSpec((1, half, c), lambda j, pr: (j, pr[0], 0))],
            out_specs=pl.BlockSpec((1, half, c), lambda j, pr: (j, 0, 0)),
            scratch_shapes=[pltpu.VMEM((n, half, c), F32), pltpu.SemaphoreType.DMA((n,)),
                            pltpu.SemaphoreType.DMA((n,))]),
        compiler_params=_params("arbitrary"),
    )(place, grad, grad)


class _NoExchanges:
    pair_sums, landed = {}, {}

    def gather(self, names):
        return _NoRider()

    def pair(self, names, grads):
        return _NoRider()

    def paired(self, names, grads, theirs):
        pass

    def pair_now(self, names, grads):
        pass

    def chip(self, names):
        return _NoRider()


class _StepExchanges(_NoExchanges):
    def __init__(self, shards_bf16, place):
        self.shards, self.place = shards_bf16, place
        self.pair_sums, self.landed = {}, {}

    def gather(self, names):
        return _WeightGather([self.shards[n] for n in names])

    def pair(self, names, grads):
        return _PairExchange([grads[n] for n in names])

    def paired(self, names, grads, theirs):
        sums = _pair_sums_streamed("pair_sums_" + names[0], [grads[n] for n in names], list(theirs))
        self.pair_sums.update(zip(names, sums))

    def pair_now(self, names, grads):
        for n in names:
            self.pair_sums[n] = _exchange_pair_sum(f"pair_sum_{n}", self.place, grads[n])

    def chip(self, names):
        return _ChipExchange([self.pair_sums[n] for n in names])


SUM_ROWS = 32


def _finish_gradients(place, pair_sums, landed, vec):
    n_w = len(pair_sums)
    rows = vec.shape[0]
    halves = [s.shape[1:] for s in pair_sums]

    def body(place_ref, *refs):
        sums, lands, v_ref = refs[:n_w], refs[n_w:2 * n_w], refs[2 * n_w]
        outs, o_ref = refs[2 * n_w + 1:3 * n_w + 1], refs[3 * n_w + 1]
        stage = refs[3 * n_w + 2:4 * n_w + 2]
        kept, half_send, half_recv, slots, core_sums, vec_send, vec_recv, sum_send, sum_recv = refs[4 * n_w + 2:]
        x, y, c = _place()
        my_chip, sibling = 2 * x + y, (x, y, 1 - c)
        slots[my_chip] = v_ref[...]
        spread = []
        for p, (ox, oy) in enumerate(_other_chips(x, y)):
            here = slots.at[2 * ox + oy]
            spread.append((_remote(v_ref, slots.at[my_chip], vec_send.at[p], vec_recv.at[p], (ox, oy, c)),
                           _remote(here, here, vec_send.at[p], vec_recv.at[p], (ox, oy, c))))
        for send, _ in spread:
            send.start()
        copies = []
        for w, (h, _) in enumerate(halves):
            step = SUM_ROWS if h % SUM_ROWS == 0 else h

            def sum_rows(i, _, w=w, step=step):
                r = pl.ds(pl.multiple_of(i * step, step), step)
                stage[w][r, :] = functools.reduce(lambda total, p: total + lands[w][p, r, :].astype(F32), range(3),
                                                  sums[w][0, r, :].astype(F32))
                return 0

            lax.fori_loop(0, h // step, sum_rows, 0)
            mine, theirs = outs[w].at[pl.ds(c * h, h)], outs[w].at[pl.ds((1 - c) * h, h)]
            copies.append((pltpu.make_async_copy(stage[w], mine, kept.at[w]),
                           _remote(stage[w], mine, half_send.at[w], half_recv.at[w], sibling),
                           _remote(theirs, theirs, half_send.at[w], half_recv.at[w], sibling)))
            copies[-1][0].start()
            copies[-1][1].start()
        for send, arrival in spread:
            arrival.wait_recv()
            send.wait_send()
        core_sums[c] = functools.reduce(lambda total, chip: total + slots[chip], range(1, N_CHIPS), slots[0])
        mine, theirs = core_sums.at[c], core_sums.at[1 - c]
        to_sibling = _remote(mine, mine, sum_send.at[0], sum_recv.at[0], sibling)
        to_sibling.start()
        _remote(theirs, theirs, sum_send.at[0], sum_recv.at[0], sibling).wait_recv()
        to_sibling.wait_send()
        o_ref[...] = core_sums[0] + core_sums[1]
        for keep, send, arrival in copies:
            keep.wait()
            arrival.wait_recv()
            send.wait_send()

    once = dict(pipeline_mode=pl.Buffered(1))
    vm = pl.BlockSpec((rows, LANES), lambda i, pr: (0, 0))
    res = pl.pallas_call(
        body, name="finish_gradients",
        grid_spec=pltpu.PrefetchScalarGridSpec(
            num_scalar_prefetch=1, grid=(1,),
            in_specs=[pl.BlockSpec((1,) + hc, lambda i, pr: (pr[1], 0, 0), **once) for hc in halves]
            + [pl.BlockSpec((3,) + hc, lambda i, pr: (0, 0, 0), **once) for hc in halves] + [vm],
            out_specs=[ANY] * n_w + [vm],
            scratch_shapes=[pltpu.VMEM(hc, F32) for hc in halves]
            + [pltpu.SemaphoreType.DMA((n_w,))] * 3
            + [pltpu.VMEM((N_CHIPS, rows, LANES), F32), pltpu.VMEM((2, rows, LANES), F32),
               pltpu.SemaphoreType.DMA((N_CHIPS - 1,)), pltpu.SemaphoreType.DMA((N_CHIPS - 1,)),
               pltpu.SemaphoreType.DMA((1,)), pltpu.SemaphoreType.DMA((1,))]),
        out_shape=[_sds((2 * h, cols), F32) for h, cols in halves] + [_sds(vec.shape, F32)],
        compiler_params=_params("arbitrary"),
    )(place, *pair_sums, *landed, vec)
    return res[:n_w], res[n_w]


def _row_tile(rows):
    fits = [tr for tr in range(16, min(rows, 512) + 1, 16) if rows % tr == 0]
    return max(fits) if fits else rows


def _pair_sum(name, place, grad, theirs):
    if grad.ndim == 2:
        return _pair_sum_joined(name, place, grad, theirs)
    n, r, c = grad.shape
    half = r // 2
    tr = _row_tile(half)
    nb = half // tr

    def body(place_ref, g_ref, t_ref, o_ref):
        o_ref[...] = (g_ref[...] + t_ref[...]).astype(BF16)

    return pl.pallas_call(
        body, name=name, out_shape=jax.ShapeDtypeStruct((n, half, c), BF16),
        grid_spec=pltpu.PrefetchScalarGridSpec(
            num_scalar_prefetch=1, grid=(n, nb),
            in_specs=[pl.BlockSpec((1, tr, c), lambda j, i, pr: (j, pr[0] * nb + i, 0)),
                      pl.BlockSpec((1, tr, c), lambda j, i, pr: (j, i, 0))],
            out_specs=pl.BlockSpec((1, tr, c), lambda j, i, pr: (j, i, 0))),
        compiler_params=_params("parallel", "parallel"),
    )(place, grad, theirs)


def _pair_sum_joined(name, place, grad, theirs):
    r, wide = grad.shape
    half, c = r // 2, wide // N_CHIPS
    tr = _row_tile(half)
    nb = half // tr

    def body(place_ref, g_ref, t_ref, o_ref):
        for j in range(N_CHIPS):
            cols = slice(j * c, (j + 1) * c)
            o_ref[j] = (g_ref[:, cols] + t_ref[:, cols]).astype(BF16)

    return pl.pallas_call(
        body, name=name, out_shape=jax.ShapeDtypeStruct((N_CHIPS, half, c), BF16),
        grid_spec=pltpu.PrefetchScalarGridSpec(
            num_scalar_prefetch=1, grid=(nb,),
            in_specs=[pl.BlockSpec((tr, wide), lambda i, pr: (pr[0] * nb + i, 0)),
                      pl.BlockSpec((tr, wide), lambda i, pr: (i, 0))],
            out_specs=pl.BlockSpec((N_CHIPS, tr, c), lambda i, pr: (0, i, 0))),
        compiler_params=_params("parallel"),
    )(place, grad, theirs)


def _pair_sums_streamed(name, grads, theirs):
    n_w = len(grads)
    chunks, out_shapes = [], []
    for w, g in enumerate(grads):
        half = g.shape[-2] // 2
        if g.ndim == 3:
            chunks += [(w, j, half, g.shape[2]) for j in range(g.shape[0])]
            out_shapes.append(_sds((g.shape[0], half, g.shape[2]), BF16))
        else:
            chunks.append((w, None, half, g.shape[1]))
            out_shapes.append(_sds((N_CHIPS, half, g.shape[1] // N_CHIPS), BF16))
    slot_shape = (2, max(k[2] for k in chunks), max(k[3] for k in chunks))

    def body(*refs):
        g_refs, t_refs, o_refs = refs[:n_w], refs[n_w:2 * n_w], refs[2 * n_w:3 * n_w]
        mine, other, summed, in_sems, out_sems = refs[3 * n_w:]
        core = lax.axis_index("c")

        def fetches(k):
            w, j, h, c = chunks[k]
            rows, slot = pl.ds(core * h, h), k % 2
            own = g_refs[w].at[rows, :] if j is None else g_refs[w].at[j, rows, :]
            sent = t_refs[w] if j is None else t_refs[w].at[j]
            return (pltpu.make_async_copy(own, mine.at[slot, pl.ds(0, h), pl.ds(0, c)], in_sems.at[slot, 0]),
                    pltpu.make_async_copy(sent, other.at[slot, pl.ds(0, h), pl.ds(0, c)], in_sems.at[slot, 1]))

        def stores(k):
            w, j, h, c = chunks[k]
            slot = k % 2
            if j is not None:
                return [pltpu.make_async_copy(summed.at[slot, pl.ds(0, h), pl.ds(0, c)], o_refs[w].at[j],
                                              out_sems.at[slot, 0])]
            part = c // N_CHIPS
            return [pltpu.make_async_copy(summed.at[slot, pl.ds(0, h), pl.ds(i * part, part)], o_refs[w].at[i],
                                          out_sems.at[slot, i]) for i in range(N_CHIPS)]

        for cp in fetches(0):
            cp.start()
        for k, (_, _, h, c) in enumerate(chunks):
            slot = k % 2
            for cp in fetches(k + 1) if k + 1 < len(chunks) else ():
                cp.start()
            for cp in fetches(k):
                cp.wait()
            for cp in stores(k - 2) if k >= 2 else ():
                cp.wait()
            summed[slot, :h, :c] = (mine[slot, :h, :c] + other[slot, :h, :c]).astype(BF16)
            for cp in stores(k):
                cp.start()
        for k in range(max(len(chunks) - 2, 0), len(chunks)):
            for cp in stores(k):
                cp.wait()

    return pl.pallas_call(
        body, name=name, in_specs=[ANY] * (2 * n_w), out_specs=[ANY] * n_w, out_shape=out_shapes,
        scratch_shapes=[pltpu.VMEM(slot_shape, F32), pltpu.VMEM(slot_shape, F32), pltpu.VMEM(slot_shape, BF16),
                        pltpu.SemaphoreType.DMA((2, 2)), pltpu.SemaphoreType.DMA((2, N_CHIPS))],
        compiler_params=_params(),
    )(*grads, *theirs)


MIXER = ("w_branch_a", "w_branch_b", "w_out")
FFN_PLE = ("w_ffn_gate", "w_ffn_up", "w_ffn_down", "w_ple_gate", "w_ple_proj")
LATE = MIXER + FFN_PLE
BIG = ("w_in",) + LATE
HELD_TRANSPOSED = ("w_ffn_gate", "w_ffn_up")
SMALL = ("norm_mix", "w_pool", "pool_scale", "norm_ffn", "norm_ple", "norm_final")


def _join_columns(w4):
    return jnp.concatenate([w4[j] for j in range(N_CHIPS)], axis=1)


def _sds(shape, dtype):
    return jax.ShapeDtypeStruct(shape, dtype)


def _local_step(x, p, target, wf, small, ex=None):
    t, d = x.shape
    w_pool_b = small["w_pool"].astype(BF16)
    dp = w_pool_b.shape[0] * w_pool_b.shape[1]

    ex = ex or _NoExchanges()
    h1, first = _norm_fwd("norm_mix", x, small["norm_mix"], rider=ex.gather(("w_in",)))
    wf = {**wf, **dict(zip(("w_in",), first))}
    w_in = wf["w_in"]
    u, q, kv, ga, gb = _mm(
        "proj", [h1], [w_in[j] for j in range(N_CHIPS)], "nn",
        [_sds((t, dp), F32), _sds((t, dp), BF16), _sds((t, d), BF16), _sds((t, d), BF16), _sds((t, d), BF16)],
        separate=True, epilogue=lambda uq, kv_, ga_, gb_: (uq[:, :dp], uq[:, dp:], kv_, ga_, gb_), tm=512)
    pooled, ya = _pool_fwd(u, w_pool_b, small["pool_scale"])
    n_pairs = dp // LANES
    yb, late = _attn_fwd(q, 0, kv, 0, n_pairs, n_pairs, rider=ex.gather(LATE))
    wf = {**wf, **dict(zip(LATE, late))}
    w_down = wf["w_ffn_down"].reshape(-1, d)
    dff = w_down.shape[0]
    w_gate_t, w_up_t = wf["w_ffn_gate"].reshape(dff, d), wf["w_ffn_up"].reshape(dff, d)
    w_a, w_b, w_pp = _join_columns(wf["w_branch_a"]), _join_columns(wf["w_branch_b"]), _join_columns(wf["w_ple_proj"])
    w_out = wf["w_out"].reshape(d, d)
    w_pg = wf["w_ple_gate"].reshape(d, d)
    def residual_norm(branch, xv, g, w):
        xn = xv + jnp.dot(branch.astype(BF16), w, preferred_element_type=F32)
        return xn, xn * lax.rsqrt(jnp.mean(xn * xn, axis=-1, keepdims=True) + RMS_EPS) * g

    def mixer_tail(tav, tbv, gav, gbv, xv, g, w):
        merged = _sigmoid(gav) * tav + _sigmoid(gbv) * tbv
        return (tav, tbv, merged) + residual_norm(merged, xv, g, w)

    def ffn_tail(gv, uv, xv, g, w):
        act = gv * _sigmoid(gv) * uv
        return (gv, uv, act) + residual_norm(act, xv, g, w)

    stream = [_sds((t, d), F32), _sds((t, d), BF16)]
    ta, tb, merged, x1, h2 = _mm(
        "mixer_out", [ya, yb], [w_a, w_b], "nn", [_sds((t, d), BF16)] * 3 + stream,
        extras=[ga, gb, x, small["norm_ffn"]], wholes=[w_out], separate=True, epilogue=mixer_tail, tm=512)
    gate, up, act, x2, h3 = _mm(
        "ffn", [h2], [w_gate_t, w_up_t], "nt", [_sds((t, dff), BF16)] * 3 + stream,
        extras=[x1, small["norm_ple"]], wholes=[w_down], separate=True, epilogue=ffn_tail, tm=256)
    dx2, dx2_b, d_pp, d_gp, d_norm_final, loss_row, d_norm_ple = _mm(
        "ple_loss", [h3, p], [w_pg, w_pp], "nn", stream + [_sds((t, d), BF16)] * 2,
        extras=[x2, target, small["norm_final"].reshape(1, d), small["norm_ple"]], wholes=[w_pg], separate=True,
        epilogue=_ple_and_loss, sum_shapes=[_sds((1, d), F32)] * 3, tm=512)

    def through_norm(dh, xv, g, dres):
        dx, d_gain = _rms_norm_bwd(dh, xv, g)
        return dx + dres, dx + dres, d_gain

    gain_sum = [_sds((1, d), F32)]
    g_w_pp, g_w_pg = _mm_tn("g_ple", [p, h3], [d_pp, d_gp])

    def ffn_bwd(d_act, gv, uv, xv, g, dres, wg_t, wu_t):
        s = _sigmoid(gv)
        d_gate, d_up = d_act * uv * (s * (1.0 + gv * (1.0 - s))), d_act * (gv * s)
        dh2 = (jnp.dot(d_gate.astype(BF16), wg_t, preferred_element_type=F32)
               + jnp.dot(d_up.astype(BF16), wu_t, preferred_element_type=F32))
        return (d_gate, d_up) + through_norm(dh2, xv, g, dres)

    d_gate, d_up, dx1, dx1_b, d_norm_ffn = _mm(
        "ffn_bwd", [dx2_b], [w_down], "nt", [_sds((t, dff), BF16)] * 2 + stream,
        extras=[gate, up, x1, small["norm_ffn"], dx2], wholes=[w_gate_t, w_up_t], epilogue=ffn_bwd,
        sum_shapes=gain_sum, tm=256)
    g_w_down, = _mm_tn("g_ffn_down", [act], [dx2_b], tmm=512)
    g_w_gate_t, g_w_up_t = _mm_tn("g_ffn_gate_up", [d_gate, d_up], [h2], k_blocks=2)

    def merge_bwd(acc, tav, tbv, gav, gbv):
        sa, sb = _sigmoid(gav), _sigmoid(gbv)
        return acc * sa, acc * sb, acc * tav * sa * (1.0 - sa), acc * tbv * sb * (1.0 - sb)

    big = {
        "w_ffn_gate": g_w_gate_t.reshape(wf["w_ffn_gate"].shape), "w_ffn_up": g_w_up_t.reshape(wf["w_ffn_up"].shape),
        "w_ffn_down": g_w_down.reshape(wf["w_ffn_down"].shape),
        "w_ple_gate": g_w_pg.reshape(wf["w_ple_gate"].shape), "w_ple_proj": g_w_pp,
    }
    (d_ta, d_tb, d_ga, d_gb), theirs = _mm(
        "d_merged", [dx1_b], [w_out], "nt", [_sds((t, d), BF16)] * 4, extras=[ta, tb, ga, gb], epilogue=merge_bwd,
        tm=512, rider=ex.pair(FFN_PLE, big))
    ex.paired(FFN_PLE, big, theirs)
    g_w_out, big["w_branch_a"], big["w_branch_b"] = _mm_tn("g_mixer", [merged, ya, yb], [dx1_b, d_ta, d_tb])
    big["w_out"] = g_w_out.reshape(wf["w_out"].shape)
    (d_ya, d_yb), theirs = _mm(
        "d_branches", [d_ta, d_tb], [w_a, w_b], "nt", [_sds((t, dp), F32), _sds((t, dp), BF16)], separate=True,
        rider=ex.pair(MIXER, big))
    ex.paired(MIXER, big, theirs)
    d_u, g_w_pool, d_pool_scale = _pool_bwd(d_ya, pooled, w_pool_b, small["pool_scale"])
    (d_q, d_k, d_v), landed = _attn_bwd(q, 0, kv, 0, n_pairs, d_yb, n_pairs, rider=ex.chip(LATE))
    ex.landed.update(zip(LATE, landed))
    d_proj = [(d_u, d_q), (d_k, d_v), d_ga, d_gb]
    big["w_in"], = _mm_tn("g_w_in", [h1], d_proj, tmm=512, stacked=True)
    ex.pair_now(("w_in",), big)
    (grad_x, d_norm_mix), landed = _mm(
        "d_h1", d_proj, [w_in[j] for j in range(N_CHIPS)], "nt", [_sds((t, d), F32)],
        extras=[x, small["norm_mix"], dx1], epilogue=lambda dh, xv, g, dres: through_norm(dh, xv, g, dres)[1:],
        sum_shapes=gain_sum, tm=512, rider=ex.chip(("w_in",)))
    ex.landed.update(zip(("w_in",), landed))
    small_g = {"norm_mix": d_norm_mix, "w_pool": g_w_pool, "pool_scale": d_pool_scale, "norm_ffn": d_norm_ffn,
               "norm_ple": d_norm_ple, "norm_final": d_norm_final}
    return grad_x, big, small_g, loss_row


def _pack_small(small_g, loss_row):
    parts, layout = [], []
    for name in SMALL + ("loss",):
        v = (loss_row if name == "loss" else small_g[name]).reshape(-1, LANES)
        pad = (-v.shape[0]) % 8
        if pad:
            v = jnp.concatenate([v, jnp.zeros((pad, LANES), F32)], axis=0)
        layout.append((name, sum(q.shape[0] for q in parts), v.shape[0]))
        parts.append(v)
    return jnp.concatenate(parts, axis=0), layout


def kernel(x, p, norm_mix, w_in, w_pool, pool_scale, w_branch_a, w_branch_b, w_out, norm_ffn, w_ffn_gate, w_ffn_up, w_ffn_down, norm_ple, w_ple_gate, w_ple_proj, norm_final, loss_target, m_norm_mix, m_w_in, m_w_pool, m_pool_scale, m_w_branch_a, m_w_branch_b, m_w_out, m_norm_ffn, m_w_ffn_gate, m_w_ffn_up, m_w_ffn_down, m_norm_ple, m_w_ple_gate, m_w_ple_proj, m_norm_final, v_norm_mix, v_w_in, v_w_pool, v_pool_scale, v_w_branch_a, v_w_branch_b, v_w_out, v_norm_ffn, v_w_ffn_gate, v_w_ffn_up, v_w_ffn_down, v_norm_ple, v_w_ple_gate, v_w_ple_proj, v_norm_final):
    given = dict(locals())
    order = ("norm_mix", "w_in", "w_pool", "pool_scale", "w_branch_a", "w_branch_b", "w_out", "norm_ffn", "w_ffn_gate",
             "w_ffn_up", "w_ffn_down", "norm_ple", "w_ple_gate", "w_ple_proj", "norm_final")
    t, d = x.shape[1], x.shape[2]
    def local(a, n):
        return jnp.swapaxes(a[0], 0, 1) if n in HELD_TRANSPOSED else a[0]

    def back(a, n):
        return (jnp.swapaxes(a, 0, 1) if n in HELD_TRANSPOSED else a)[None]

    shard = {n: local(given[n], n) for n in BIG}
    small = {"norm_mix": norm_mix, "w_pool": w_pool[0], "pool_scale": pool_scale, "norm_ffn": norm_ffn,
             "norm_ple": norm_ple, "norm_final": norm_final}

    place = jnp.stack([lax.axis_index("c"), 2 * lax.axis_index("x") + lax.axis_index("y")]).astype(jnp.int32)
    ex = _StepExchanges({n: shard[n].astype(BF16) for n in BIG}, place)
    grad_x, _, small_g, loss_row = _local_step(
        x.reshape(t, d), p.reshape(t, p.shape[-1]), loss_target.reshape(t, d), {}, small, ex)
    packed, layout = _pack_small(small_g, loss_row)
    filled, reduced = _finish_gradients(place, [ex.pair_sums[n] for n in BIG], [ex.landed[n] for n in BIG], packed)
    grads = dict(zip(BIG, filled))
    for name, start, rows in layout:
        if name == "loss":
            loss = jnp.sum(reduced[start:start + rows])
        else:
            n_el = small[name].size
            grads[name] = reduced[start:start + rows].reshape(-1)[:n_el]

    deltas, new_m, new_v = {}, {}, {}
    for n in order:
        if n in BIG:
            w, m, v = shard[n], local(given["m_" + n], n), local(given["v_" + n], n)
            dl, mn, vn = _adamw(f"adamw_{n}", w, grads[n], m, v)
            grads[n], deltas[n], new_m[n], new_v[n] = [back(a, n) for a in (grads[n], dl, mn, vn)]
        else:
            w, full = small[n], given[n].shape
            shape2 = (1, w.shape[0]) if w.ndim == 1 else (w.shape if w.ndim == 2 else (w.shape[0] * w.shape[1], w.shape[2]))
            dl, mn, vn = _adamw(f"adamw_{n}", w.reshape(shape2), grads[n].reshape(shape2),
                                given["m_" + n].reshape(shape2), given["v_" + n].reshape(shape2))
            grads[n], deltas[n], new_m[n], new_v[n] = [a.reshape(full) for a in (grads[n], dl, mn, vn)]

    return (loss, grad_x.reshape(x.shape), *[grads[n] for n in order], *[deltas[n] for n in order],
            *[new_m[n] for n in order], *[new_v[n] for n in order])
```

```python
import functools
import math

import jax
import jax.numpy as jnp
from jax import lax
from jax.experimental import pallas as pl
from jax.experimental.pallas import tpu as pltpu

F32 = jnp.float32
BF16 = jnp.bfloat16
MESH = pl.DeviceIdType.MESH

RMS_EPS = 1e-6
POOL_WINDOWS = (2, 4, 8, 16)
POOL_HALO = 16
HEAD_DIM = 64
LANES = 128
ATT_BLOCK = 256
ATT_CHAINS = 2
ATT_FWD_CHAINS = 4
ATT_CHUNK = 256
ATT_SLAB = 256
ATT_SCALE = 1.0 / math.sqrt(HEAD_DIM)
LOG2_E = 1.4426950408889634
ATT_EXIT_BELOW = -150.5
ADAM_LR, ADAM_B1, ADAM_B2, ADAM_EPS, ADAM_WD, ADAM_STEP = 0.001, 0.9, 0.999, 1e-08, 0.01, 10
V7X_VMEM_LIMIT_BYTES = 56 * 1024 * 1024
N_CHIPS = 4
N_DEV = 8


def _params(*semantics):
    return pltpu.CompilerParams(dimension_semantics=semantics, vmem_limit_bytes=V7X_VMEM_LIMIT_BYTES)


def _sigmoid(z):
    return 0.5 * jnp.tanh(0.5 * z) + 0.5


def _tiled_spec(shape, tm, tn, n_total, at):
    rows, width = shape
    if rows == 1:
        if width == n_total:
            return pl.BlockSpec((1, tn), at(lambda i, j: (0, j)))
        return pl.BlockSpec((1, width), at(lambda i, j: (0, 0)))
    if width == n_total:
        return pl.BlockSpec((tm, tn), at(lambda i, j: (i, j)))
    assert tn == n_total, "an operand narrower than the output needs whole output rows per tile"
    return pl.BlockSpec((tm, width), at(lambda i, j: (i, 0)))


def _column_pieces(operands):
    pieces = [tuple(a) if isinstance(a, (tuple, list)) else (a,) for a in operands]
    return [p for ps in pieces for p in ps], [len(ps) for ps in pieces]


def _load_bf16(refs, counts):
    tiles, k = [], 0
    for n in counts:
        parts = [r[...] for r in refs[k:k + n]]
        parts = [t if t.dtype == BF16 else t.astype(BF16) for t in parts]
        tiles.append(parts[0] if n == 1 else jnp.concatenate(parts, axis=1))
        k += n
    return tiles


def _mm(name, a_list, b_list, mode, out_shapes, epilogue=None, extras=(), tm=1024, tn=None, separate=False,
        sum_shapes=(), rider=None, wholes=()):
    flat_a, counts = _column_pieces(a_list)
    m_total = flat_a[0].shape[0]
    n_total = b_list[0].shape[1] if mode == "nn" else b_list[0].shape[0]
    tn = n_total if tn is None else tn
    tm = min(tm, m_total)
    assert m_total % tm == 0 and n_total % tn == 0 and (not sum_shapes or tn == n_total)
    n_a, n_b, n_extra, n_out = len(counts), len(b_list), len(extras), len(out_shapes)
    assert n_a in (1, n_b)
    dims = (((1,), (0,)), ((), ())) if mode == "nn" else (((1,), (1,)), ((), ()))
    with_rider = rider is not None
    rider = rider or _NoRider()
    grid = (n_total // tn, m_total // tm)

    def at(index):
        return lambda j, i: index(i, j)

    def body(*refs):
        ins, o_refs, _, riding = rider.split(refs, len(flat_a) + n_b + n_extra + len(wholes), n_out + len(sum_shapes))
        a_refs, b_refs = ins[:len(flat_a)], ins[len(flat_a):len(flat_a) + n_b]
        e_refs, w_refs = ins[len(flat_a) + n_b:len(flat_a) + n_b + n_extra], ins[len(flat_a) + n_b + n_extra:]
        at_first = (pl.program_id(0) == 0) & (pl.program_id(1) == 0)
        at_last = (pl.program_id(0) == grid[0] - 1) & (pl.program_id(1) == grid[1] - 1)
        top, bottom = rider.at_steps(riding, at_first, at_first, at_last)
        top()
        lefts = _load_bf16(a_refs, counts)
        products = [lax.dot_general(lefts[s % n_a], b_refs[s][...], dims, preferred_element_type=F32)
                    for s in range(n_b)]
        if not separate:
            products = [functools.reduce(lambda p, r: p + r, products)]
        extra_tiles = [e[...].astype(F32) for e in e_refs]
        outs = products if epilogue is None else epilogue(*products, *extra_tiles, *[w[...] for w in w_refs])
        for o_ref, o in zip(o_refs[:n_out], outs[:n_out]):
            o_ref[...] = o.astype(o_ref.dtype)
        if sum_shapes:
            @pl.when(pl.program_id(1) == 0)
            def _():
                for s_ref in o_refs[n_out:]:
                    s_ref[...] = jnp.zeros_like(s_ref)

            for s_ref, s in zip(o_refs[n_out:], outs[n_out:]):
                s_ref[...] += s
        bottom()

    once = dict(pipeline_mode=pl.Buffered(1)) if tn == n_total else {}
    in_specs = [pl.BlockSpec((tm, a.shape[1]), at(lambda i, j: (i, 0))) for a in flat_a]
    if mode == "nn":
        in_specs += [pl.BlockSpec((b.shape[0], tn), at(lambda i, j: (0, j)), **once) for b in b_list]
    else:
        in_specs += [pl.BlockSpec((tn, b.shape[1]), at(lambda i, j: (j, 0)), **once) for b in b_list]
    in_specs += [_tiled_spec(e.shape, tm, tn, n_total, at) for e in extras]
    in_specs += [pl.BlockSpec(w.shape, lambda j, i: (0, 0), pipeline_mode=pl.Buffered(1)) for w in wholes]
    out_specs = [_tiled_spec(o.shape, tm, tn, n_total, at) for o in out_shapes]
    out_specs += [pl.BlockSpec(s.shape, at(lambda i, j: (0, 0))) for s in sum_shapes]
    semantics = ("arbitrary", "arbitrary") if sum_shapes or rider.operands else ("parallel", "parallel")
    res = pl.pallas_call(
        body, name=name, grid=grid, in_specs=in_specs + [ANY] * len(rider.operands),
        out_specs=out_specs + [ANY] * len(rider.out_shapes),
        out_shape=list(out_shapes) + list(sum_shapes) + list(rider.out_shapes), scratch_shapes=list(rider.scratch),
        compiler_params=_params(*semantics),
    )(*flat_a, *b_list, *extras, *wholes, *rider.operands)
    n_own = len(out_shapes) + len(sum_shapes)
    return (res[:n_own], res[n_own:]) if with_rider else res


def _mm_tn(name, a_list, b_list, tmm=1024, stacked=False, k_blocks=1):
    flat_b, counts = _column_pieces(b_list)
    n_a, n_b = len(a_list), len(counts)
    n_prod = max(n_a, n_b)
    m_total = a_list[0].shape[0]
    ks = [a_list[s % n_a].shape[1] for s in range(n_prod)]
    widths = [sum(p.shape[1] for p in flat_b[sum(counts[:s]):sum(counts[:s + 1])]) for s in range(n_b)]
    widths = [widths[s % n_b] for s in range(n_prod)]
    tmm = min(tmm, m_total)
    assert m_total % tmm == 0 and all(k % k_blocks == 0 for k in ks)
    assert n_a in (1, n_prod) and n_b in (1, n_prod) and not (stacked and n_a > 1)

    def body(*refs):
        a_refs, b_refs, o_refs = refs[:n_a], refs[n_a:n_a + len(flat_b)], refs[n_a + len(flat_b):]

        @pl.when(pl.program_id(1) == 0)
        def _():
            for o_ref in o_refs:
                o_ref[...] = jnp.zeros_like(o_ref)

        lefts, rights = _load_bf16(a_refs, [1] * n_a), _load_bf16(b_refs, counts)
        for s in range(n_prod):
            product = lax.dot_general(lefts[s % n_a], rights[s % n_b], (((0,), (0,)), ((), ())),
                                      preferred_element_type=F32)
            if stacked:
                o_refs[0][s] += product
            else:
                o_refs[s][...] += product

    in_specs = [pl.BlockSpec((tmm, a.shape[1] // k_blocks), lambda kb, m: (m, kb)) for a in a_list]
    in_specs += [pl.BlockSpec((tmm, b.shape[1]), lambda kb, m: (m, 0)) for b in flat_b]
    if stacked:
        out_shape = [jax.ShapeDtypeStruct((n_prod, ks[0], widths[0]), F32)]
        out_specs = [pl.BlockSpec((n_prod, ks[0] // k_blocks, widths[0]), lambda kb, m: (0, kb, 0))]
    else:
        out_shape = [jax.ShapeDtypeStruct((k, w), F32) for k, w in zip(ks, widths)]
        out_specs = [pl.BlockSpec((k // k_blocks, w), lambda kb, m: (kb, 0)) for k, w in zip(ks, widths)]
    return pl.pallas_call(
        body, name=name, grid=(k_blocks, m_total // tmm), in_specs=in_specs, out_specs=out_specs, out_shape=out_shape,
        compiler_params=_params("arbitrary", "arbitrary"),
    )(*a_list, *flat_b)


def _rows(name, fn, ins, tile_outs, sum_outs=(), tr=512, rider=None):
    t_total = max(a.shape[0] for a in ins)
    tr = min(tr, t_total)
    assert t_total % tr == 0
    n_in, n_tile = len(ins), len(tile_outs)
    rider = rider or _NoRider()
    n_steps = t_total // tr

    def body(*refs):
        own_ins, own_outs, _, riding = rider.split(refs, n_in, n_tile + len(sum_outs))
        step = pl.program_id(0)
        top, bottom = rider.at_steps(riding, step == 0, step == n_steps - 1, step == n_steps - 1)
        top()
        refs = tuple(own_ins) + tuple(own_outs)
        outs = fn(*[r[...].astype(F32) for r in refs[:n_in]])
        for o_ref, o in zip(refs[n_in:n_in + n_tile], outs[:n_tile]):
            o_ref[...] = o.astype(o_ref.dtype)
        if sum_outs:
            @pl.when(pl.program_id(0) == 0)
            def _():
                for s_ref in refs[n_in + n_tile:]:
                    s_ref[...] = jnp.zeros_like(s_ref)

            for s_ref, s in zip(refs[n_in + n_tile:], outs[n_tile:]):
                s_ref[...] += s
        bottom()

    def spec(shape):
        if shape[0] == 1:
            return pl.BlockSpec(shape, lambda i: (0, 0))
        return pl.BlockSpec((tr, shape[1]), lambda i: (i, 0))

    return pl.pallas_call(
        body, name=name, grid=(n_steps,), in_specs=[spec(a.shape) for a in ins] + [ANY] * len(rider.operands),
        out_specs=[spec(o.shape) for o in tile_outs] + [spec(s.shape) for s in sum_outs] + [ANY] * len(rider.out_shapes),
        out_shape=list(tile_outs) + list(sum_outs) + list(rider.out_shapes), scratch_shapes=list(rider.scratch),
        compiler_params=_params("arbitrary" if sum_outs or rider.operands else "parallel"),
    )(*ins, *rider.operands)


def _norm_fwd(name, x, gain, rider=None):
    def fn(xv, g):
        inv = lax.rsqrt(jnp.mean(xv * xv, axis=-1, keepdims=True) + RMS_EPS)
        return (xv * inv * g,)

    res = _rows(name, fn, [x, gain], [jax.ShapeDtypeStruct(x.shape, BF16)], rider=rider)
    return res[0], res[1:]


def _rms_norm_bwd(dh, xv, g):
    inv = lax.rsqrt(jnp.mean(xv * xv, axis=-1, keepdims=True) + RMS_EPS)
    xn = xv * inv
    dxn = dh * g
    return inv * (dxn - xn * jnp.mean(dxn * xn, axis=-1, keepdims=True)), jnp.sum(dh * xn, axis=0, keepdims=True)


def _ple_and_loss(gv, pv, x2v, tv, g_final, g_ple, w_pg):
    d = x2v.shape[1]
    s = _sigmoid(gv)
    xv = x2v + s * pv
    inv = lax.rsqrt(jnp.mean(xv * xv, axis=-1, keepdims=True) + RMS_EPS)
    err = xv * inv * g_final - tv
    dx3, d_final = _rms_norm_bwd(err * (1.0 / d), xv, g_final)
    d_pp, d_gp = dx3 * s, dx3 * pv * s * (1.0 - s)
    dh3 = lax.dot_general(d_gp.astype(BF16), w_pg, (((1,), (1,)), ((), ())), preferred_element_type=F32)
    dx2, d_ple = _rms_norm_bwd(dh3, x2v, g_ple)
    dx2 = dx2 + dx3
    return dx2, dx2, d_pp, d_gp, d_final, (0.5 / d) * jnp.sum(err * err, axis=0, keepdims=True), d_ple


def _window_counts(t_pos, w):
    return jnp.minimum(t_pos + 1, w).astype(F32)


def _pool_fwd(u, w_pool, scale, tr=512):
    t_total, width = u.shape
    tr = min(tr, t_total)
    n_groups = len(POOL_WINDOWS)
    gdim = width // n_groups
    ext = tr + POOL_HALO

    def body(u_ref, halo_ref, w_ref, s_ref, pooled_ref, ya_ref):
        i = pl.program_id(0)
        halo = jnp.where(i == 0, 0.0, halo_ref[...])
        t_pos = i * tr + lax.broadcasted_iota(jnp.int32, (tr, 1), 0)
        for g, w in enumerate(POOL_WINDOWS):
            cols = slice(g * gdim, (g + 1) * gdim)
            main = u_ref[:, cols]
            win = jnp.concatenate([halo[:, cols], main], axis=0)
            span = 1
            while span < w:
                win = win + pltpu.roll(win, span, 0)
                span *= 2
            pooled = win[POOL_HALO:, :] * (1.0 / _window_counts(t_pos, w)) - main
            pooled_b = pooled.astype(BF16)
            pooled_ref[:, cols] = pooled_b
            mixed = jnp.dot(pooled_b, w_ref[g], preferred_element_type=F32)
            ya_ref[:, cols] = (mixed * s_ref[:, cols]).astype(BF16)

    hb = tr // POOL_HALO
    return pl.pallas_call(
        body, name="pool_fwd", grid=(t_total // tr,),
        in_specs=[pl.BlockSpec((tr, width), lambda i: (i, 0)),
                  pl.BlockSpec((POOL_HALO, width), lambda i: (jnp.maximum(i * hb - 1, 0), 0)),
                  pl.BlockSpec((n_groups, gdim, gdim), lambda i: (0, 0, 0)),
                  pl.BlockSpec((1, width), lambda i: (0, 0))],
        out_specs=[pl.BlockSpec((tr, width), lambda i: (i, 0)), pl.BlockSpec((tr, width), lambda i: (i, 0))],
        out_shape=[jax.ShapeDtypeStruct(u.shape, BF16), jax.ShapeDtypeStruct(u.shape, BF16)],
        compiler_params=_params("parallel"),
    )(u, u, w_pool, scale)


def _pool_bwd(dya, pooled, w_pool, scale, tr=512):
    t_total, width = dya.shape
    tr = min(tr, t_total)
    n_groups = len(POOL_WINDOWS)
    gdim = width // n_groups
    ext = tr + POOL_HALO
    n_tiles = t_total // tr

    def body(d_ref, halo_ref, p_ref, w_ref, s_ref, du_ref, dw_ref, ds_ref):
        i = pl.program_id(0)

        @pl.when(i == 0)
        def _():
            dw_ref[...] = jnp.zeros_like(dw_ref)
            ds_ref[...] = jnp.zeros_like(ds_ref)

        halo = jnp.where(i == n_tiles - 1, 0.0, halo_ref[...])
        t_pos = i * tr + lax.broadcasted_iota(jnp.int32, (ext, 1), 0)
        for g, w in enumerate(POOL_WINDOWS):
            cols = slice(g * gdim, (g + 1) * gdim)
            sc = s_ref[:, cols]
            d_main = d_ref[:, cols]
            pooled_b = p_ref[:, cols]
            mixed = jnp.dot(pooled_b, w_ref[g], preferred_element_type=F32)
            ds_ref[:, cols] += jnp.sum(d_main * mixed, axis=0, keepdims=True)
            dmix = (jnp.concatenate([d_main, halo[:, cols]], axis=0) * sc).astype(BF16)
            dw_ref[g] += lax.dot_general(pooled_b, dmix[:tr, :], (((0,), (0,)), ((), ())),
                                         preferred_element_type=F32)
            dpool = lax.dot_general(dmix, w_ref[g], (((1,), (1,)), ((), ())), preferred_element_type=F32)
            win = dpool * (1.0 / _window_counts(t_pos, w))
            span = 1
            while span < w:
                win = win + pltpu.roll(win, ext - span, 0)
                span *= 2
            du_ref[:, cols] = (win[:tr, :] - dpool[:tr, :]).astype(BF16)

    hb = tr // POOL_HALO
    last_halo = t_total // POOL_HALO - 1
    return pl.pallas_call(
        body, name="pool_bwd", grid=(n_tiles,),
        in_specs=[pl.BlockSpec((tr, width), lambda i: (i, 0)),
                  pl.BlockSpec((POOL_HALO, width), lambda i: (jnp.minimum((i + 1) * hb, last_halo), 0)),
                  pl.BlockSpec((tr, width), lambda i: (i, 0)),
                  pl.BlockSpec((n_groups, gdim, gdim), lambda i: (0, 0, 0)),
                  pl.BlockSpec((1, width), lambda i: (0, 0))],
        out_specs=[pl.BlockSpec((tr, width), lambda i: (i, 0)),
                   pl.BlockSpec((n_groups, gdim, gdim), lambda i: (0, 0, 0)),
                   pl.BlockSpec((1, width), lambda i: (0, 0))],
        out_shape=[jax.ShapeDtypeStruct(dya.shape, BF16), jax.ShapeDtypeStruct((n_groups, gdim, gdim), F32),
                   jax.ShapeDtypeStruct((1, width), F32)],
        compiler_params=_params("arbitrary"),
    )(dya, dya, pooled, w_pool, scale)


def _head_masks():
    lane = lax.broadcasted_iota(jnp.int32, (1, LANES), 1)
    return lane < HEAD_DIM


def _stack_heads(tile, first):
    zero = jnp.zeros_like(tile)
    return jnp.concatenate([jnp.where(first, tile, zero), jnp.where(first, zero, tile)], axis=0)


def _causal_mask(t_pos, k_start):
    col = lax.broadcasted_iota(jnp.int32, (1, 2 * ATT_SLAB), 1)
    return k_start + (col & (ATT_SLAB - 1)) < t_pos


def _slab_scores(q, kd, mask):
    z2 = lax.dot_general(q, kd, (((1,), (1,)), ((), ())), preferred_element_type=F32) * LOG2_E
    log_hit = jnp.minimum(z2, 0.0) - jnp.log2(1.0 + jnp.exp2(-jnp.abs(z2)))
    log_fail = log_hit - z2
    return log_hit, (log_fail if mask is None else jnp.where(mask, log_fail, 0.0))


def _weights(log_hit, suffix, mask):
    arg = log_hit + suffix
    return jnp.exp2(arg if mask is None else jnp.where(mask, arg, -1e30))


def _tri(upper):
    r = lax.broadcasted_iota(jnp.int32, (ATT_CHUNK, ATT_CHUNK), 0)
    c = lax.broadcasted_iota(jnp.int32, (ATT_CHUNK, ATT_CHUNK), 1)
    return jnp.where(r > c if upper else r < c, 1.0, 0.0).astype(BF16)


def _tri_spec():
    return pl.BlockSpec((ATT_CHUNK, ATT_CHUNK), lambda h, i: (0, 0), pipeline_mode=pl.Buffered(1))


def _scan_chunk(v, tri):
    return jnp.dot(v.astype(BF16), tri, preferred_element_type=F32)


def _lane_bcast(col):
    return jnp.broadcast_to(col, (col.shape[0], LANES))


def _scan_slab(v, tri, carries, from_right):
    n_chunks = ATT_SLAB // ATT_CHUNK
    edge = 0 if from_right else ATT_CHUNK - 1
    parts, new_carries = [None] * (2 * n_chunks), []
    for head in range(2):
        run = carries[head]
        for c in (reversed(range(n_chunks)) if from_right else range(n_chunks)):
            lo_col = head * ATT_SLAB + c * ATT_CHUNK
            vc = v[:, lo_col:lo_col + ATT_CHUNK]
            sc = _scan_chunk(vc, tri)
            parts[head * n_chunks + c] = sc + jnp.concatenate([run] * (ATT_CHUNK // LANES), axis=1)
            run = run + _lane_bcast(sc[:, edge:edge + 1] + vc[:, edge:edge + 1])
        new_carries.append(run)
    return jnp.concatenate(parts, axis=1), new_carries


def _fold_heads(stacked, first):
    s = stacked.shape[0] // 2
    return jnp.where(first, stacked[:s], stacked[s:])


class _NoRider:
    operands, out_shapes, scratch = (), (), ()

    def split(self, refs, n_base_in, n_base_out):
        n_in, n_out, n_sem = len(self.operands), len(self.out_shapes), len(self.scratch)
        a = n_base_in + n_in
        b = a + n_base_out + n_out
        mine = (refs[n_base_in:a], refs[a + n_base_out:b], refs[b:b + n_sem])
        return refs[:n_base_in], refs[a:a + n_base_out], refs[b + n_sem:], mine

    def start(self, ins, outs, sems):
        pass

    def relay(self, ins, outs, sems):
        pass

    def finish(self, ins, outs, sems):
        pass

    def at_steps(self, refs, first_step, relay_step, last_step):
        if not self.operands:
            return (lambda: None), (lambda: None)

        def top():
            pl.when(first_step)(lambda: self.start(*refs))
            pl.when(relay_step)(lambda: self.relay(*refs))

        return top, lambda: pl.when(last_step)(lambda: self.finish(*refs))


def _attn_fwd(q_src, q_col, kv_src, k_col, v_col, n_pairs=4, rider=_NoRider()):
    t_total = q_src.shape[0]
    blk = ATT_BLOCK
    n_chains = ATT_FWD_CHAINS if t_total % (ATT_FWD_CHAINS * blk) == 0 else ATT_CHAINS
    n_steps = t_total // (n_chains * blk)
    assert t_total % ATT_SLAB == 0 and ATT_SLAB == ATT_BLOCK

    def body(*refs):
        (q_ref, k_ref, v_ref, suffix_ref), (o_ref,), _, riding = rider.split(refs, 4, 1)
        h, ii = pl.program_id(0), pl.program_id(1)
        top, bottom = rider.at_steps(riding, (h == 0) & (ii == 0), (h == n_pairs - 1) & (ii == 0),
                                     (h == n_pairs - 1) & (ii == n_steps - 1))
        top()
        first = _head_masks()
        suffix_tri = suffix_ref[...]
        blocks = [n_chains * ii + c for c in range(n_chains)]
        qs = [q_ref[c * blk:(c + 1) * blk, :] * ATT_SCALE for c in range(n_chains)]
        t_pos = [b * blk + lax.broadcasted_iota(jnp.int32, (blk, 1), 0) for b in blocks]

        def one(c, t, chain, on_diagonal):
            _, acc, right_a, right_b = chain
            k_start = pl.multiple_of((blocks[c] - t) * ATT_SLAB, ATT_SLAB)
            kd = _stack_heads(k_ref[pl.ds(k_start, ATT_SLAB), :], first)
            vd = _stack_heads(v_ref[pl.ds(k_start, ATT_SLAB), :], first)
            mask = _causal_mask(t_pos[c], k_start) if on_diagonal else None
            log_hit, log_fail = _slab_scores(qs[c], kd, mask)
            suffix, (right_a, right_b) = _scan_slab(log_fail, suffix_tri, (right_a, right_b), from_right=True)
            a = _weights(log_hit, suffix, mask).astype(BF16)
            acc = acc + jnp.dot(a, vd, preferred_element_type=F32)
            return jnp.max(jnp.maximum(right_a, right_b)), acc, right_a, right_b

        def step(state, on_diagonal):
            t, chains = state
            return t + 1, tuple(one(c, t, chains[c], on_diagonal) for c in range(n_chains))

        def more(state):
            t, chains = state
            return (t <= blocks[0]) & (functools.reduce(jnp.maximum, [ch[0] for ch in chains]) > ATT_EXIT_BELOW)

        zero = jnp.zeros((blk, LANES), F32)
        state = step((0, ((jnp.float32(0.0), zero, zero, zero),) * n_chains), on_diagonal=True)
        t, chains = lax.while_loop(more, functools.partial(step, on_diagonal=False), state)
        for c in range(n_chains):
            chain = chains[c]
            if c:
                _, chain = lax.while_loop(
                    lambda s, c=c: (s[0] <= blocks[c]) & (s[1][0] > ATT_EXIT_BELOW),
                    lambda s, c=c: (s[0] + 1, one(c, s[0], s[1], False)), (t, chain))
            o_ref[c * blk:(c + 1) * blk, :] = chain[1].astype(BF16)
        bottom()

    rows = n_chains * blk
    res = pl.pallas_call(
        body, name="attn_fwd", grid=(n_pairs, n_steps),
        in_specs=[pl.BlockSpec((rows, LANES), lambda h, i: (i, q_col + h)),
                  pl.BlockSpec((t_total, LANES), lambda h, i: (0, k_col + h)),
                  pl.BlockSpec((t_total, LANES), lambda h, i: (0, v_col + h)), _tri_spec()] + [ANY] * len(rider.operands),
        out_specs=[pl.BlockSpec((rows, LANES), lambda h, i: (i, h))] + [ANY] * len(rider.out_shapes),
        out_shape=[jax.ShapeDtypeStruct((t_total, n_pairs * LANES), BF16)] + list(rider.out_shapes),
        scratch_shapes=list(rider.scratch),
        compiler_params=_params("arbitrary", "arbitrary"),
    )(q_src, kv_src, kv_src, _tri(upper=True), *rider.operands)
    return res[0], res[1:]


def _attn_bwd(q_src, q_col, kv_src, k_col, v_col, dy, n_pairs=4, rider=_NoRider()):
    t_total = q_src.shape[0]
    blk = ATT_BLOCK
    n_steps = t_total // (ATT_CHAINS * blk)
    n_slabs = t_total // ATT_SLAB
    assert t_total % ATT_SLAB == 0 and ATT_SLAB == ATT_BLOCK

    def body(*refs):
        ins, (dq_ref, dk_ref, dv_ref), (g_s, dk_acc, dv_acc), riding = rider.split(refs, 6, 3)
        q_ref, dy_ref, k_ref, v_ref, suffix_ref, prefix_ref = ins
        h, ii = pl.program_id(0), pl.program_id(1)
        top, bottom = rider.at_steps(riding, (h == 0) & (ii == 0), (h == n_pairs - 1) & (ii == 0),
                                     (h == n_pairs - 1) & (ii == n_steps - 1))
        top()

        @pl.when(ii == 0)
        def _():
            dk_acc[...] = jnp.zeros_like(dk_acc)
            dv_acc[...] = jnp.zeros_like(dv_acc)

        first = _head_masks()
        suffix_tri = suffix_ref[...]
        prefix_tri = prefix_ref[...]
        blocks = [ATT_CHAINS * ii + c for c in range(ATT_CHAINS)]
        rows = [slice(c * blk, (c + 1) * blk) for c in range(ATT_CHAINS)]
        qs = [q_ref[r, :] * ATT_SCALE for r in rows]
        dys = [dy_ref[r, :] for r in rows]
        t_pos = [b * blk + lax.broadcasted_iota(jnp.int32, (blk, 1), 0) for b in blocks]

        def one1(c, t, chain, on_diagonal):
            _, right_a, right_b = chain
            slab = blocks[c] - t
            k_start = pl.multiple_of(slab * ATT_SLAB, ATT_SLAB)
            kd = _stack_heads(k_ref[pl.ds(k_start, ATT_SLAB), :], first)
            vd = _stack_heads(v_ref[pl.ds(k_start, ATT_SLAB), :], first)
            mask = _causal_mask(t_pos[c], k_start) if on_diagonal else None
            log_hit, log_fail = _slab_scores(qs[c], kd, mask)
            suffix, (right_a, right_b) = _scan_slab(log_fail, suffix_tri, (right_a, right_b), from_right=True)
            a = _weights(log_hit, suffix, mask)
            da = lax.dot_general(dys[c], vd, (((1,), (1,)), ((), ())), preferred_element_type=F32)
            g_s[c, slab] = (da * a).astype(BF16)
            dv_acc[pl.ds(k_start, ATT_SLAB), :] += _fold_heads(lax.dot_general(
                a.astype(BF16), dys[c], (((0,), (0,)), ((), ())), preferred_element_type=F32), first)
            return jnp.max(jnp.maximum(right_a, right_b)), right_a, right_b

        def step1(state, on_diagonal):
            t, chains = state
            return t + 1, tuple(one1(c, t, chains[c], on_diagonal) for c in range(ATT_CHAINS))

        def more(state):
            t, chains = state
            return (t <= blocks[0]) & (functools.reduce(jnp.maximum, [ch[0] for ch in chains]) > ATT_EXIT_BELOW)

        zero = jnp.zeros((blk, LANES), F32)
        state = step1((0, ((jnp.float32(0.0), zero, zero),) * ATT_CHAINS), on_diagonal=True)
        joint, chains = lax.while_loop(more, functools.partial(step1, on_diagonal=False), state)
        done = [joint]
        for c in range(1, ATT_CHAINS):
            done.append(lax.while_loop(
                lambda s, c=c: (s[0] <= blocks[c]) & (s[1][0] > ATT_EXIT_BELOW),
                lambda s, c=c: (s[0] + 1, one1(c, s[0], s[1], False)), (joint, chains[c]))[0])

        def one2(c, t, carry, on_diagonal):
            dq, left_a, left_b = carry
            slab = blocks[c] - t
            k_start = pl.multiple_of(slab * ATT_SLAB, ATT_SLAB)
            kd = _stack_heads(k_ref[pl.ds(k_start, ATT_SLAB), :], first)
            g = g_s[c, slab]
            sig = _sigmoid(lax.dot_general(qs[c], kd, (((1,), (1,)), ((), ())), preferred_element_type=F32))
            prefix, (left_a, left_b) = _scan_slab(g, prefix_tri, (left_a, left_b), from_right=False)
            dz = g * (1.0 - sig) - sig * prefix
            if on_diagonal:
                dz = jnp.where(_causal_mask(t_pos[c], k_start), dz, 0.0)
            dz = dz.astype(BF16)
            dq = dq + jnp.dot(dz, kd, preferred_element_type=F32)
            dk_acc[pl.ds(k_start, ATT_SLAB), :] += _fold_heads(lax.dot_general(
                dz, qs[c], (((0,), (0,)), ((), ())), preferred_element_type=F32), first)
            return dq, left_a, left_b

        carries = [(zero, zero, zero)]
        for c in range(1, ATT_CHAINS):
            carries.append(lax.fori_loop(
                0, done[c] - joint, lambda n, carry, c=c: one2(c, done[c] - 1 - n, carry, False), (zero, zero, zero)))
        carries = lax.fori_loop(
            0, joint - 1,
            lambda n, cs: tuple(one2(c, joint - 1 - n, cs[c], False) for c in range(ATT_CHAINS)), tuple(carries))
        for c in range(ATT_CHAINS):
            dq_ref[rows[c], :] = (one2(c, 0, carries[c], True)[0] * ATT_SCALE).astype(BF16)

        @pl.when(ii == n_steps - 1)
        def _():
            dk_ref[...] = dk_acc[...].astype(BF16)
            dv_ref[...] = dv_acc[...].astype(BF16)

        bottom()

    out = jax.ShapeDtypeStruct((t_total, n_pairs * LANES), BF16)
    n_rows = ATT_CHAINS * blk
    whole = dict(pipeline_mode=pl.Buffered(1))
    res = pl.pallas_call(
        body, name="attn_bwd", grid=(n_pairs, n_steps),
        in_specs=[pl.BlockSpec((n_rows, LANES), lambda h, i: (i, q_col + h)),
                  pl.BlockSpec((n_rows, LANES), lambda h, i: (i, h)),
                  pl.BlockSpec((t_total, LANES), lambda h, i: (0, k_col + h), **whole),
                  pl.BlockSpec((t_total, LANES), lambda h, i: (0, v_col + h), **whole), _tri_spec(), _tri_spec()]
        + [ANY] * len(rider.operands),
        out_specs=[pl.BlockSpec((n_rows, LANES), lambda h, i: (i, h)),
                   pl.BlockSpec((t_total, LANES), lambda h, i: (0, h)),
                   pl.BlockSpec((t_total, LANES), lambda h, i: (0, h))] + [ANY] * len(rider.out_shapes),
        out_shape=[out, out, out] + list(rider.out_shapes),
        scratch_shapes=list(rider.scratch) + [pltpu.VMEM((ATT_CHAINS, n_slabs, blk, 2 * ATT_SLAB), BF16),
                                              pltpu.VMEM((t_total, LANES), F32), pltpu.VMEM((t_total, LANES), F32)],
        compiler_params=_params("arbitrary", "arbitrary"),
    )(q_src, dy, kv_src, kv_src, _tri(upper=True), _tri(upper=False), *rider.operands)
    return res[:3], res[3:]


def _adamw_math(wv, gv, mv, vv):
    mn = ADAM_B1 * mv + (1.0 - ADAM_B1) * gv
    vn = ADAM_B2 * vv + (1.0 - ADAM_B2) * (gv * gv)
    m_hat = mn / (1.0 - ADAM_B1 ** ADAM_STEP)
    v_hat = vn / (1.0 - ADAM_B2 ** ADAM_STEP)
    return -ADAM_LR * (m_hat / (jnp.sqrt(v_hat) + ADAM_EPS) + ADAM_WD * wv), mn, vn


def _adamw_streamed(name, ws, gs, ms, vs):
    n_w = len(ws)
    chunks = []
    for w, a in enumerate(ws):
        step = _row_tile(a.shape[0])
        assert step % ADAMW_STRIP == 0
        chunks += [(w, r, step, a.shape[1]) for r in range(0, a.shape[0], step)]
    slot_shape = (2, max(k[2] for k in chunks), max(k[3] for k in chunks))

    def body(*refs):
        ins = [refs[i * n_w:(i + 1) * n_w] for i in range(4)]
        outs = [refs[(4 + i) * n_w:(5 + i) * n_w] for i in range(3)]
        in_bufs, out_bufs = refs[7 * n_w:7 * n_w + 4], refs[7 * n_w + 4:7 * n_w + 7]
        in_sems, out_sems = refs[7 * n_w + 7:]

        def fetches(k):
            w, r, h, c = chunks[k]
            return [pltpu.make_async_copy(ins[i][w].at[pl.ds(r, h), :], in_bufs[i].at[k % 2, pl.ds(0, h), pl.ds(0, c)],
                                          in_sems.at[k % 2, i]) for i in range(4)]

        def stores(k):
            w, r, h, c = chunks[k]
            return [pltpu.make_async_copy(out_bufs[i].at[k % 2, pl.ds(0, h), pl.ds(0, c)], outs[i][w].at[pl.ds(r, h), :],
                                          out_sems.at[k % 2, i]) for i in range(3)]

        for cp in fetches(0):
            cp.start()
        for k, (_, _, h, c) in enumerate(chunks):
            slot = k % 2
            for cp in fetches(k + 1) if k + 1 < len(chunks) else ():
                cp.start()
            for cp in fetches(k):
                cp.wait()
            for cp in stores(k - 2) if k >= 2 else ():
                cp.wait()

            def strip(i, _, slot=slot, c=c):
                rows = pl.ds(pl.multiple_of(i * ADAMW_STRIP, ADAMW_STRIP), ADAMW_STRIP)
                for buf, value in zip(out_bufs, _adamw_math(*[b[slot, rows, :c] for b in in_bufs])):
                    buf[slot, rows, :c] = value
                return 0

            lax.fori_loop(0, h // ADAMW_STRIP, strip, 0)
            for cp in stores(k):
                cp.start()
        for k in range(max(len(chunks) - 2, 0), len(chunks)):
            for cp in stores(k):
                cp.wait()

    res = pl.pallas_call(
        body, name=name, in_specs=[ANY] * (4 * n_w), out_specs=[ANY] * (3 * n_w),
        out_shape=[jax.ShapeDtypeStruct(a.shape, F32) for _ in range(3) for a in ws],
        scratch_shapes=[pltpu.VMEM(slot_shape, F32)] * 7
        + [pltpu.SemaphoreType.DMA((2, 4)), pltpu.SemaphoreType.DMA((2, 3))],
        compiler_params=_params(),
    )(*ws, *gs, *ms, *vs)
    return [(res[w], res[n_w + w], res[2 * n_w + w]) for w in range(n_w)]


ADAMW_STRIP = 32


def _adamw(name, w, g, m, v):
    fn = _adamw_math
    rows = w.shape[0]
    tr = _row_tile(rows)
    shp = jax.ShapeDtypeStruct(w.shape, F32)
    if rows == 1:
        def body(w_ref, g_ref, m_ref, v_ref, d_ref, mo_ref, vo_ref):
            d, mn, vn = fn(w_ref[...], g_ref[...], m_ref[...], v_ref[...])
            d_ref[...], mo_ref[...], vo_ref[...] = d, mn, vn

        return pl.pallas_call(body, name=name, out_shape=[shp, shp, shp])(w, g, m, v)
    return _rows(name, fn, [w, g, m, v], [shp, shp, shp], tr=tr)


def _place():
    return lax.axis_index("x"), lax.axis_index("y"), lax.axis_index("c")


def _other_chips(x, y):
    return [(1 - x, y), (x, 1 - y), (1 - x, 1 - y)]


ANY = pl.BlockSpec(memory_space=pl.ANY)


def _remote(src, dst, send_sem, recv_sem, to):
    return pltpu.make_async_remote_copy(src_ref=src, dst_ref=dst, send_sem=send_sem, recv_sem=recv_sem,
                                        device_id=to, device_id_type=MESH)


class _WeightGather(_NoRider):
    def __init__(self, shards):
        n_w = len(shards)
        self.operands = list(shards)
        self.out_shapes = [jax.ShapeDtypeStruct((N_CHIPS,) + s.shape, s.dtype) for s in shards]
        self.scratch = [pltpu.SemaphoreType.DMA((3, n_w))] * 4 + [pltpu.SemaphoreType.DMA((n_w,))] * 2

    def _copies(self, ins, outs, sems):
        send_sems, recv_sems, relay_send, relay_recv, own_send, own_recv = sems
        x, y, c = _place()
        my_chip, sibling = 2 * x + y, (x, y, 1 - c)
        n_w = len(ins)

        def half(w, chip, core):
            h = self.operands[w].shape[0] // 2
            return outs[w].at[chip, pl.ds(core * h, h)]

        own = [_remote(ins[w], outs[w].at[my_chip], own_send.at[w], own_recv.at[w], sibling) for w in range(n_w)]
        sends, landed, relays, relayed = [], [], [], []
        for p, (ox, oy) in enumerate(_other_chips(x, y)):
            for w in range(n_w):
                h = self.operands[w].shape[0] // 2
                sends.append(_remote(ins[w].at[pl.ds(c * h, h)], half(w, my_chip, c), send_sems.at[p, w],
                                     recv_sems.at[p, w], (ox, oy, c)))
                here = half(w, 2 * ox + oy, c)
                landed.append(_remote(here, here, send_sems.at[p, w], recv_sems.at[p, w], (ox, oy, c)))
                relays.append(_remote(here, here, relay_send.at[p, w], relay_recv.at[p, w], sibling))
                there = half(w, 2 * ox + oy, 1 - c)
                relayed.append(_remote(there, there, relay_send.at[p, w], relay_recv.at[p, w], sibling))
        return own, sends, landed, relays, relayed

    def start(self, ins, outs, sems):
        own, sends, _, _, _ = self._copies(ins, outs, sems)
        for cp in own + sends:
            cp.start()

    def relay(self, ins, outs, sems):
        _, _, landed, relays, _ = self._copies(ins, outs, sems)
        for arrival, cp in zip(landed, relays):
            arrival.wait_recv()
            cp.start()

    def finish(self, ins, outs, sems):
        own, sends, _, relays, relayed = self._copies(ins, outs, sems)
        for arrival in relayed:
            arrival.wait_recv()
        for cp in sends + relays:
            cp.wait_send()
        for cp in own:
            cp.wait()


class _ChipExchange(_NoRider):
    def __init__(self, pair_sums):
        n_w = len(pair_sums)
        self.operands = list(pair_sums)
        self.out_shapes = [jax.ShapeDtypeStruct((3,) + s.shape[1:], s.dtype) for s in pair_sums]
        self.scratch = [pltpu.SemaphoreType.DMA((3, n_w))] * 2

    def _copies(self, ins, outs, sems):
        send_sems, recv_sems = sems
        x, y, c = _place()
        return [_remote(ins[w].at[2 * ox + oy], outs[w].at[p], send_sems.at[p, w], recv_sems.at[p, w], (ox, oy, c))
                for p, (ox, oy) in enumerate(_other_chips(x, y)) for w in range(len(ins))]

    def start(self, ins, outs, sems):
        for cp in self._copies(ins, outs, sems):
            cp.start()

    def finish(self, ins, outs, sems):
        for cp in self._copies(ins, outs, sems):
            cp.wait()


class _PairExchange(_NoRider):
    def __init__(self, grads):
        n_w = len(grads)
        self.operands = list(grads)
        self.out_shapes = [jax.ShapeDtypeStruct(g.shape[:-2] + (g.shape[-2] // 2, g.shape[-1]), F32) for g in grads]
        self.scratch = [pltpu.SemaphoreType.DMA((n_w,))] * 2

    def _copies(self, ins, theirs, sems):
        send_sems, recv_sems = sems
        x, y, c = _place()
        sends = []
        for w, g in enumerate(self.operands):
            rows = pl.ds((1 - c) * (g.shape[-2] // 2), g.shape[-2] // 2)
            src = ins[w].at[:, rows, :] if g.ndim == 3 else ins[w].at[rows, :]
            sends.append(_remote(src, theirs[w], send_sems.at[w], recv_sems.at[w], (x, y, 1 - c)))
        return sends

    def start(self, ins, outs, sems):
        for cp in self._copies(ins, outs, sems):
            cp.start()

    def finish(self, ins, outs, sems):
        for cp in self._copies(ins, outs, sems):
            cp.wait()


def _exchange_pair_sum(name, place, grad):
    n, r, c = grad.shape
    half = r // 2

    def body(place_ref, g_all, g_ref, o_ref, theirs, send_sems, recv_sems):
        j = pl.program_id(0)
        x, y, core = _place()

        def copy(k):
            return _remote(g_all.at[k, pl.ds((1 - core) * half, half)], theirs.at[k], send_sems.at[k],
                           recv_sems.at[k], (x, y, 1 - core))

        @pl.when(j == 0)
        def _():
            for k in range(n):
                copy(k).start()

        copy(j).wait_recv()
        o_ref[0] = (g_ref[0] + theirs[j]).astype(BF16)

        @pl.when(j == n - 1)
        def _():
            for k in range(n):
                copy(k).wait_send()

    return pl.pallas_call(
        body, name=name, out_shape=jax.ShapeDtypeStruct((n, half, c), BF16),
        grid_spec=pltpu.PrefetchScalarGridSpec(
            num_scalar_prefetch=1, grid=(n,),
            in_specs=[ANY, pl.BlockSpec((1, half, c), lambda j, pr: (j, pr[0], 0))],
            out_specs=pl.BlockSpec((1, half, c), lambda j, pr: (j, 0, 0)),
            scratch_shapes=[pltpu.VMEM((n, half, c), F32), pltpu.SemaphoreType.DMA((n,)),
                            pltpu.SemaphoreType.DMA((n,))]),
        compiler_params=_params("arbitrary"),
    )(place, grad, grad)


class _NoExchanges:
    pair_sums, landed = {}, {}

    def gather(self, names):
        return _NoRider()

    def pair(self, names, grads):
        return _NoRider()

    def paired(self, names, grads, theirs):
        pass

    def pair_now(self, names, grads):
        pass

    def chip(self, names):
        return _NoRider()


class _StepExchanges(_NoExchanges):
    def __init__(self, shards_bf16, place):
        self.shards, self.place = shards_bf16, place
        self.pair_sums, self.landed = {}, {}

    def gather(self, names):
        return _WeightGather([self.shards[n] for n in names])

    def pair(self, names, grads):
        return _PairExchange([grads[n] for n in names])

    def paired(self, names, grads, theirs):
        sums = _pair_sums_streamed("pair_sums_" + names[0], [grads[n] for n in names], list(theirs))
        self.pair_sums.update(zip(names, sums))

    def pair_now(self, names, grads):
        for n in names:
            self.pair_sums[n] = _exchange_pair_sum(f"pair_sum_{n}", self.place, grads[n])

    def chip(self, names):
        return _ChipExchange([self.pair_sums[n] for n in names])


SUM_ROWS = 32


def _finish_gradients(place, pair_sums, landed, vec):
    n_w = len(pair_sums)
    rows = vec.shape[0]
    halves = [s.shape[1:] for s in pair_sums]

    def body(place_ref, *refs):
        sums, lands, v_ref = refs[:n_w], refs[n_w:2 * n_w], refs[2 * n_w]
        outs, o_ref = refs[2 * n_w + 1:3 * n_w + 1], refs[3 * n_w + 1]
        stage = refs[3 * n_w + 2:4 * n_w + 2]
        kept, half_send, half_recv, slots, core_sums, vec_send, vec_recv, sum_send, sum_recv = refs[4 * n_w + 2:]
        x, y, c = _place()
        my_chip, sibling = 2 * x + y, (x, y, 1 - c)
        slots[my_chip] = v_ref[...]
        spread = []
        for p, (ox, oy) in enumerate(_other_chips(x, y)):
            here = slots.at[2 * ox + oy]
            spread.append((_remote(v_ref, slots.at[my_chip], vec_send.at[p], vec_recv.at[p], (ox, oy, c)),
                           _remote(here, here, vec_send.at[p], vec_recv.at[p], (ox, oy, c))))
        for send, _ in spread:
            send.start()
        copies = []
        for w, (h, _) in enumerate(halves):
            step = SUM_ROWS if h % SUM_ROWS == 0 else h

            def sum_rows(i, _, w=w, step=step):
                r = pl.ds(pl.multiple_of(i * step, step), step)
                stage[w][r, :] = functools.reduce(lambda total, p: total + lands[w][p, r, :].astype(F32), range(3),
                                                  sums[w][0, r, :].astype(F32))
                return 0

            lax.fori_loop(0, h // step, sum_rows, 0)
            mine, theirs = outs[w].at[pl.ds(c * h, h)], outs[w].at[pl.ds((1 - c) * h, h)]
            copies.append((pltpu.make_async_copy(stage[w], mine, kept.at[w]),
                           _remote(stage[w], mine, half_send.at[w], half_recv.at[w], sibling),
                           _remote(theirs, theirs, half_send.at[w], half_recv.at[w], sibling)))
            copies[-1][0].start()
            copies[-1][1].start()
        for send, arrival in spread:
            arrival.wait_recv()
            send.wait_send()
        core_sums[c] = functools.reduce(lambda total, chip: total + slots[chip], range(1, N_CHIPS), slots[0])
        mine, theirs = core_sums.at[c], core_sums.at[1 - c]
        to_sibling = _remote(mine, mine, sum_send.at[0], sum_recv.at[0], sibling)
        to_sibling.start()
        _remote(theirs, theirs, sum_send.at[0], sum_recv.at[0], sibling).wait_recv()
        to_sibling.wait_send()
        o_ref[...] = core_sums[0] + core_sums[1]
        for keep, send, arrival in copies:
            keep.wait()
            arrival.wait_recv()
            send.wait_send()

    once = dict(pipeline_mode=pl.Buffered(1))
    vm = pl.BlockSpec((rows, LANES), lambda i, pr: (0, 0))
    res = pl.pallas_call(
        body, name="finish_gradients",
        grid_spec=pltpu.PrefetchScalarGridSpec(
            num_scalar_prefetch=1, grid=(1,),
            in_specs=[pl.BlockSpec((1,) + hc, lambda i, pr: (pr[1], 0, 0), **once) for hc in halves]
            + [pl.BlockSpec((3,) + hc, lambda i, pr: (0, 0, 0), **once) for hc in halves] + [vm],
            out_specs=[ANY] * n_w + [vm],
            scratch_shapes=[pltpu.VMEM(hc, F32) for hc in halves]
            + [pltpu.SemaphoreType.DMA((n_w,))] * 3
            + [pltpu.VMEM((N_CHIPS, rows, LANES), F32), pltpu.VMEM((2, rows, LANES), F32),
               pltpu.SemaphoreType.DMA((N_CHIPS - 1,)), pltpu.SemaphoreType.DMA((N_CHIPS - 1,)),
               pltpu.SemaphoreType.DMA((1,)), pltpu.SemaphoreType.DMA((1,))]),
        out_shape=[_sds((2 * h, cols), F32) for h, cols in halves] + [_sds(vec.shape, F32)],
        compiler_params=_params("arbitrary"),
    )(place, *pair_sums, *landed, vec)
    return res[:n_w], res[n_w]


def _row_tile(rows):
    fits = [tr for tr in range(16, min(rows, 512) + 1, 16) if rows % tr == 0]
    return max(fits) if fits else rows


def _pair_sum(name, place, grad, theirs):
    if grad.ndim == 2:
        return _pair_sum_joined(name, place, grad, theirs)
    n, r, c = grad.shape
    half = r // 2
    tr = _row_tile(half)
    nb = half // tr

    def body(place_ref, g_ref, t_ref, o_ref):
        o_ref[...] = (g_ref[...] + t_ref[...]).astype(BF16)

    return pl.pallas_call(
        body, name=name, out_shape=jax.ShapeDtypeStruct((n, half, c), BF16),
        grid_spec=pltpu.PrefetchScalarGridSpec(
            num_scalar_prefetch=1, grid=(n, nb),
            in_specs=[pl.BlockSpec((1, tr, c), lambda j, i, pr: (j, pr[0] * nb + i, 0)),
                      pl.BlockSpec((1, tr, c), lambda j, i, pr: (j, i, 0))],
            out_specs=pl.BlockSpec((1, tr, c), lambda j, i, pr: (j, i, 0))),
        compiler_params=_params("parallel", "parallel"),
    )(place, grad, theirs)


def _pair_sum_joined(name, place, grad, theirs):
    r, wide = grad.shape
    half, c = r // 2, wide // N_CHIPS
    tr = _row_tile(half)
    nb = half // tr

    def body(place_ref, g_ref, t_ref, o_ref):
        for j in range(N_CHIPS):
            cols = slice(j * c, (j + 1) * c)
            o_ref[j] = (g_ref[:, cols] + t_ref[:, cols]).astype(BF16)

    return pl.pallas_call(
        body, name=name, out_shape=jax.ShapeDtypeStruct((N_CHIPS, half, c), BF16),
        grid_spec=pltpu.PrefetchScalarGridSpec(
            num_scalar_prefetch=1, grid=(nb,),
            in_specs=[pl.BlockSpec((tr, wide), lambda i, pr: (pr[0] * nb + i, 0)),
                      pl.BlockSpec((tr, wide), lambda i, pr: (i, 0))],
            out_specs=pl.BlockSpec((N_CHIPS, tr, c), lambda i, pr: (0, i, 0))),
        compiler_params=_params("parallel"),
    )(place, grad, theirs)


def _pair_sums_streamed(name, grads, theirs):
    n_w = len(grads)
    chunks, out_shapes = [], []
    for w, g in enumerate(grads):
        half = g.shape[-2] // 2
        if g.ndim == 3:
            chunks += [(w, j, half, g.shape[2]) for j in range(g.shape[0])]
            out_shapes.append(_sds((g.shape[0], half, g.shape[2]), BF16))
        else:
            chunks.append((w, None, half, g.shape[1]))
            out_shapes.append(_sds((N_CHIPS, half, g.shape[1] // N_CHIPS), BF16))
    slot_shape = (2, max(k[2] for k in chunks), max(k[3] for k in chunks))

    def body(*refs):
        g_refs, t_refs, o_refs = refs[:n_w], refs[n_w:2 * n_w], refs[2 * n_w:3 * n_w]
        mine, other, summed, in_sems, out_sems = refs[3 * n_w:]
        core = lax.axis_index("c")

        def fetches(k):
            w, j, h, c = chunks[k]
            rows, slot = pl.ds(core * h, h), k % 2
            own = g_refs[w].at[rows, :] if j is None else g_refs[w].at[j, rows, :]
            sent = t_refs[w] if j is None else t_refs[w].at[j]
            return (pltpu.make_async_copy(own, mine.at[slot, pl.ds(0, h), pl.ds(0, c)], in_sems.at[slot, 0]),
                    pltpu.make_async_copy(sent, other.at[slot, pl.ds(0, h), pl.ds(0, c)], in_sems.at[slot, 1]))

        def stores(k):
            w, j, h, c = chunks[k]
            slot = k % 2
            if j is not None:
                return [pltpu.make_async_copy(summed.at[slot, pl.ds(0, h), pl.ds(0, c)], o_refs[w].at[j],
                                              out_sems.at[slot, 0])]
            part = c // N_CHIPS
            return [pltpu.make_async_copy(summed.at[slot, pl.ds(0, h), pl.ds(i * part, part)], o_refs[w].at[i],
                                          out_sems.at[slot, i]) for i in range(N_CHIPS)]

        for cp in fetches(0):
            cp.start()
        for k, (_, _, h, c) in enumerate(chunks):
            slot = k % 2
            for cp in fetches(k + 1) if k + 1 < len(chunks) else ():
                cp.start()
            for cp in fetches(k):
                cp.wait()
            for cp in stores(k - 2) if k >= 2 else ():
                cp.wait()
            summed[slot, :h, :c] = (mine[slot, :h, :c] + other[slot, :h, :c]).astype(BF16)
            for cp in stores(k):
                cp.start()
        for k in range(max(len(chunks) - 2, 0), len(chunks)):
            for cp in stores(k):
                cp.wait()

    return pl.pallas_call(
        body, name=name, in_specs=[ANY] * (2 * n_w), out_specs=[ANY] * n_w, out_shape=out_shapes,
        scratch_shapes=[pltpu.VMEM(slot_shape, F32), pltpu.VMEM(slot_shape, F32), pltpu.VMEM(slot_shape, BF16),
                        pltpu.SemaphoreType.DMA((2, 2)), pltpu.SemaphoreType.DMA((2, N_CHIPS))],
        compiler_params=_params(),
    )(*grads, *theirs)


MIXER = ("w_branch_a", "w_branch_b", "w_out")
FFN_PLE = ("w_ffn_gate", "w_ffn_up", "w_ffn_down", "w_ple_gate", "w_ple_proj")
LATE = MIXER + FFN_PLE
BIG = ("w_in",) + LATE
HELD_TRANSPOSED = ("w_ffn_gate", "w_ffn_up")
SMALL = ("norm_mix", "w_pool", "pool_scale", "norm_ffn", "norm_ple", "norm_final")


def _join_columns(w4):
    return jnp.concatenate([w4[j] for j in range(N_CHIPS)], axis=1)


def _sds(shape, dtype):
    return jax.ShapeDtypeStruct(shape, dtype)


def _local_step(x, p, target, wf, small, ex=None):
    t, d = x.shape
    w_pool_b = small["w_pool"].astype(BF16)
    dp = w_pool_b.shape[0] * w_pool_b.shape[1]

    ex = ex or _NoExchanges()
    h1, first = _norm_fwd("norm_mix", x, small["norm_mix"], rider=ex.gather(("w_in",)))
    wf = {**wf, **dict(zip(("w_in",), first))}
    w_in = wf["w_in"]
    u, q, kv, ga, gb = _mm(
        "proj", [h1], [w_in[j] for j in range(N_CHIPS)], "nn",
        [_sds((t, dp), F32), _sds((t, dp), BF16), _sds((t, d), BF16), _sds((t, d), BF16), _sds((t, d), BF16)],
        separate=True, epilogue=lambda uq, kv_, ga_, gb_: (uq[:, :dp], uq[:, dp:], kv_, ga_, gb_), tm=512)
    pooled, ya = _pool_fwd(u, w_pool_b, small["pool_scale"])
    n_pairs = dp // LANES
    yb, late = _attn_fwd(q, 0, kv, 0, n_pairs, n_pairs, rider=ex.gather(LATE))
    wf = {**wf, **dict(zip(LATE, late))}
    w_down = wf["w_ffn_down"].reshape(-1, d)
    dff = w_down.shape[0]
    w_gate_t, w_up_t = wf["w_ffn_gate"].reshape(dff, d), wf["w_ffn_up"].reshape(dff, d)
    w_a, w_b, w_pp = _join_columns(wf["w_branch_a"]), _join_columns(wf["w_branch_b"]), _join_columns(wf["w_ple_proj"])
    w_out = wf["w_out"].reshape(d, d)
    w_pg = wf["w_ple_gate"].reshape(d, d)
    def residual_norm(branch, xv, g, w):
        xn = xv + jnp.dot(branch.astype(BF16), w, preferred_element_type=F32)
        return xn, xn * lax.rsqrt(jnp.mean(xn * xn, axis=-1, keepdims=True) + RMS_EPS) * g

    def mixer_tail(tav, tbv, gav, gbv, xv, g, w):
        merged = _sigmoid(gav) * tav + _sigmoid(gbv) * tbv
        return (tav, tbv, merged) + residual_norm(merged, xv, g, w)

    def ffn_tail(gv, uv, xv, g, w):
        act = gv * _sigmoid(gv) * uv
        return (gv, uv, act) + residual_norm(act, xv, g, w)

    stream = [_sds((t, d), F32), _sds((t, d), BF16)]
    ta, tb, merged, x1, h2 = _mm(
        "mixer_out", [ya, yb], [w_a, w_b], "nn", [_sds((t, d), BF16)] * 3 + stream,
        extras=[ga, gb, x, small["norm_ffn"]], wholes=[w_out], separate=True, epilogue=mixer_tail, tm=512)
    gate, up, act, x2, h3 = _mm(
        "ffn", [h2], [w_gate_t, w_up_t], "nt", [_sds((t, dff), BF16)] * 3 + stream,
        extras=[x1, small["norm_ple"]], wholes=[w_down], separate=True, epilogue=ffn_tail, tm=256)
    dx2, dx2_b, d_pp, d_gp, d_norm_final, loss_row, d_norm_ple = _mm(
        "ple_loss", [h3, p], [w_pg, w_pp], "nn", stream + [_sds((t, d), BF16)] * 2,
        extras=[x2, target, small["norm_final"].reshape(1, d), small["norm_ple"]], wholes=[w_pg], separate=True,
        epilogue=_ple_and_loss, sum_shapes=[_sds((1, d), F32)] * 3, tm=512)

    def through_norm(dh, xv, g, dres):
        dx, d_gain = _rms_norm_bwd(dh, xv, g)
        return dx + dres, dx + dres, d_gain

    gain_sum = [_sds((1, d), F32)]
    g_w_pp, g_w_pg = _mm_tn("g_ple", [p, h3], [d_pp, d_gp])

    def ffn_bwd(d_act, gv, uv, xv, g, dres, wg_t, wu_t):
        s = _sigmoid(gv)
        d_gate, d_up = d_act * uv * (s * (1.0 + gv * (1.0 - s))), d_act * (gv * s)
        dh2 = (jnp.dot(d_gate.astype(BF16), wg_t, preferred_element_type=F32)
               + jnp.dot(d_up.astype(BF16), wu_t, preferred_element_type=F32))
        return (d_gate, d_up) + through_norm(dh2, xv, g, dres)

    d_gate, d_up, dx1, dx1_b, d_norm_ffn = _mm(
        "ffn_bwd", [dx2_b], [w_down], "nt", [_sds((t, dff), BF16)] * 2 + stream,
        extras=[gate, up, x1, small["norm_ffn"], dx2], wholes=[w_gate_t, w_up_t], epilogue=ffn_bwd,
        sum_shapes=gain_sum, tm=256)
    g_w_down, = _mm_tn("g_ffn_down", [act], [dx2_b], tmm=512)
    g_w_gate_t, g_w_up_t = _mm_tn("g_ffn_gate_up", [d_gate, d_up], [h2], k_blocks=2)

    def merge_bwd(acc, tav, tbv, gav, gbv):
        sa, sb = _sigmoid(gav), _sigmoid(gbv)
        return acc * sa, acc * sb, acc * tav * sa * (1.0 - sa), acc * tbv * sb * (1.0 - sb)

    big = {
        "w_ffn_gate": g_w_gate_t.reshape(wf["w_ffn_gate"].shape), "w_ffn_up": g_w_up_t.reshape(wf["w_ffn_up"].shape),
        "w_ffn_down": g_w_down.reshape(wf["w_ffn_down"].shape),
        "w_ple_gate": g_w_pg.reshape(wf["w_ple_gate"].shape), "w_ple_proj": g_w_pp,
    }
    (d_ta, d_tb, d_ga, d_gb), theirs = _mm(
        "d_merged", [dx1_b], [w_out], "nt", [_sds((t, d), BF16)] * 4, extras=[ta, tb, ga, gb], epilogue=merge_bwd,
        tm=512, rider=ex.pair(FFN_PLE, big))
    ex.paired(FFN_PLE, big, theirs)
    g_w_out, big["w_branch_a"], big["w_branch_b"] = _mm_tn("g_mixer", [merged, ya, yb], [dx1_b, d_ta, d_tb])
    big["w_out"] = g_w_out.reshape(wf["w_out"].shape)
    (d_ya, d_yb), theirs = _mm(
        "d_branches", [d_ta, d_tb], [w_a, w_b], "nt", [_sds((t, dp), F32), _sds((t, dp), BF16)], separate=True,
        rider=ex.pair(MIXER, big))
    ex.paired(MIXER, big, theirs)
    d_u, g_w_pool, d_pool_scale = _pool_bwd(d_ya, pooled, w_pool_b, small["pool_scale"])
    (d_q, d_k, d_v), landed = _attn_bwd(q, 0, kv, 0, n_pairs, d_yb, n_pairs, rider=ex.chip(LATE))
    ex.landed.update(zip(LATE, landed))
    d_proj = [(d_u, d_q), (d_k, d_v), d_ga, d_gb]
    big["w_in"], = _mm_tn("g_w_in", [h1], d_proj, tmm=512, stacked=True)
    ex.pair_now(("w_in",), big)
    (grad_x, d_norm_mix), landed = _mm(
        "d_h1", d_proj, [w_in[j] for j in range(N_CHIPS)], "nt", [_sds((t, d), F32)],
        extras=[x, small["norm_mix"], dx1], epilogue=lambda dh, xv, g, dres: through_norm(dh, xv, g, dres)[1:],
        sum_shapes=gain_sum, tm=512, rider=ex.chip(("w_in",)))
    ex.landed.update(zip(("w_in",), landed))
    small_g = {"norm_mix": d_norm_mix, "w_pool": g_w_pool, "pool_scale": d_pool_scale, "norm_ffn": d_norm_ffn,
               "norm_ple": d_norm_ple, "norm_final": d_norm_final}
    return grad_x, big, small_g, loss_row


def _pack_small(small_g, loss_row):
    parts, layout = [], []
    for name in SMALL + ("loss",):
        v = (loss_row if name == "loss" else small_g[name]).reshape(-1, LANES)
        pad = (-v.shape[0]) % 8
        if pad:
            v = jnp.concatenate([v, jnp.zeros((pad, LANES), F32)], axis=0)
        layout.append((name, sum(q.shape[0] for q in parts), v.shape[0]))
        parts.append(v)
    return jnp.concatenate(parts, axis=0), layout


def kernel(x, p, norm_mix, w_in, w_pool, pool_scale, w_branch_a, w_branch_b, w_out, norm_ffn, w_ffn_gate, w_ffn_up, w_ffn_down, norm_ple, w_ple_gate, w_ple_proj, norm_final, loss_target, m_norm_mix, m_w_in, m_w_pool, m_pool_scale, m_w_branch_a, m_w_branch_b, m_w_out, m_norm_ffn, m_w_ffn_gate, m_w_ffn_up, m_w_ffn_down, m_norm_ple, m_w_ple_gate, m_w_ple_proj, m_norm_final, v_norm_mix, v_w_in, v_w_pool, v_pool_scale, v_w_branch_a, v_w_branch_b, v_w_out, v_norm_ffn, v_w_ffn_gate, v_w_ffn_up, v_w_ffn_down, v_norm_ple, v_w_ple_gate, v_w_ple_proj, v_norm_final):
    given = dict(locals())
    order = ("norm_mix", "w_in", "w_pool", "pool_scale", "w_branch_a", "w_branch_b", "w_out", "norm_ffn", "w_ffn_gate",
             "w_ffn_up", "w_ffn_down", "norm_ple", "w_ple_gate", "w_ple_proj", "norm_final")
    t, d = x.shape[1], x.shape[2]
    def local(a, n):
        return jnp.swapaxes(a[0], 0, 1) if n in HELD_TRANSPOSED else a[0]

    def back(a, n):
        return (jnp.swapaxes(a, 0, 1) if n in HELD_TRANSPOSED else a)[None]

    shard = {n: local(given[n], n) for n in BIG}
    small = {"norm_mix": norm_mix, "w_pool": w_pool[0], "pool_scale": pool_scale, "norm_ffn": norm_ffn,
             "norm_ple": norm_ple, "norm_final": norm_final}

    place = jnp.stack([lax.axis_index("c"), 2 * lax.axis_index("x") + lax.axis_index("y")]).astype(jnp.int32)
    ex = _StepExchanges({n: shard[n].astype(BF16) for n in BIG}, place)
    grad_x, _, small_g, loss_row = _local_step(
        x.reshape(t, d), p.reshape(t, p.shape[-1]), loss_target.reshape(t, d), {}, small, ex)
    packed, layout = _pack_small(small_g, loss_row)
    filled, reduced = _finish_gradients(place, [ex.pair_sums[n] for n in BIG], [ex.landed[n] for n in BIG], packed)
    grads = dict(zip(BIG, filled))
    for name, start, rows in layout:
        if name == "loss":
            loss = jnp.sum(reduced[start:start + rows])
        else:
            n_el = small[name].size
            grads[name] = reduced[start:start + rows].reshape(-1)[:n_el]

    deltas, new_m, new_v = {}, {}, {}
    updated = dict(zip(BIG, _adamw_streamed(
        "adamw_big", [shard[n] for n in BIG], [grads[n] for n in BIG], [local(given["m_" + n], n) for n in BIG],
        [local(given["v_" + n], n) for n in BIG])))
    for n in order:
        if n in BIG:
            dl, mn, vn = updated[n]
            grads[n], deltas[n], new_m[n], new_v[n] = [back(a, n) for a in (grads[n], dl, mn, vn)]
        else:
            w, full = small[n], given[n].shape
            shape2 = (1, w.shape[0]) if w.ndim == 1 else (w.shape if w.ndim == 2 else (w.shape[0] * w.shape[1], w.shape[2]))
            dl, mn, vn = _adamw(f"adamw_{n}", w.reshape(shape2), grads[n].reshape(shape2),
                                given["m_" + n].reshape(shape2), given["v_" + n].reshape(shape2))
            grads[n], deltas[n], new_m[n], new_v[n] = [a.reshape(full) for a in (grads[n], dl, mn, vn)]

    return (loss, grad_x.reshape(x.shape), *[grads[n] for n in order], *[deltas[n] for n in order],
            *[new_m[n] for n in order], *[new_v[n] for n in order])
```

```python
import functools
import math

import jax
import jax.numpy as jnp
from jax import lax
from jax.experimental import pallas as pl
from jax.experimental.pallas import tpu as pltpu

F32 = jnp.float32
BF16 = jnp.bfloat16
MESH = pl.DeviceIdType.MESH

RMS_EPS = 1e-6
POOL_WINDOWS = (2, 4, 8, 16)
POOL_HALO = 16
HEAD_DIM = 64
LANES = 128
ATT_BLOCK = 256
ATT_CHAINS = 2
ATT_FWD_CHAINS = 4
ATT_CHUNK = 256
ATT_SLAB = 256
ATT_SCALE = 1.0 / math.sqrt(HEAD_DIM)
LOG2_E = 1.4426950408889634
ATT_EXIT_BELOW = -150.5
ADAM_LR, ADAM_B1, ADAM_B2, ADAM_EPS, ADAM_WD, ADAM_STEP = 0.001, 0.9, 0.999, 1e-08, 0.01, 10
V7X_VMEM_LIMIT_BYTES = 56 * 1024 * 1024
N_CHIPS = 4
N_DEV = 8


def _params(*semantics):
    return pltpu.CompilerParams(dimension_semantics=semantics, vmem_limit_bytes=V7X_VMEM_LIMIT_BYTES)


def _sigmoid(z):
    return 0.5 * jnp.tanh(0.5 * z) + 0.5


def _tiled_spec(shape, tm, tn, n_total, at):
    rows, width = shape
    if rows == 1:
        if width == n_total:
            return pl.BlockSpec((1, tn), at(lambda i, j: (0, j)))
        return pl.BlockSpec((1, width), at(lambda i, j: (0, 0)))
    if width == n_total:
        return pl.BlockSpec((tm, tn), at(lambda i, j: (i, j)))
    assert tn == n_total, "an operand narrower than the output needs whole output rows per tile"
    return pl.BlockSpec((tm, width), at(lambda i, j: (i, 0)))


def _column_pieces(operands):
    pieces = [tuple(a) if isinstance(a, (tuple, list)) else (a,) for a in operands]
    return [p for ps in pieces for p in ps], [len(ps) for ps in pieces]


def _load_bf16(refs, counts):
    tiles, k = [], 0
    for n in counts:
        parts = [r[...] for r in refs[k:k + n]]
        parts = [t if t.dtype == BF16 else t.astype(BF16) for t in parts]
        tiles.append(parts[0] if n == 1 else jnp.concatenate(parts, axis=1))
        k += n
    return tiles


def _mm(name, a_list, b_list, mode, out_shapes, epilogue=None, extras=(), tm=1024, tn=None, separate=False,
        sum_shapes=(), rider=None, wholes=()):
    flat_a, counts = _column_pieces(a_list)
    m_total = flat_a[0].shape[0]
    n_total = b_list[0].shape[1] if mode == "nn" else b_list[0].shape[0]
    tn = n_total if tn is None else tn
    tm = min(tm, m_total)
    assert m_total % tm == 0 and n_total % tn == 0 and (not sum_shapes or tn == n_total)
    n_a, n_b, n_extra, n_out = len(counts), len(b_list), len(extras), len(out_shapes)
    assert n_a in (1, n_b)
    dims = (((1,), (0,)), ((), ())) if mode == "nn" else (((1,), (1,)), ((), ()))
    with_rider = rider is not None
    rider = rider or _NoRider()
    grid = (n_total // tn, m_total // tm)

    def at(index):
        return lambda j, i: index(i, j)

    def body(*refs):
        ins, o_refs, _, riding = rider.split(refs, len(flat_a) + n_b + n_extra + len(wholes), n_out + len(sum_shapes))
        a_refs, b_refs = ins[:len(flat_a)], ins[len(flat_a):len(flat_a) + n_b]
        e_refs, w_refs = ins[len(flat_a) + n_b:len(flat_a) + n_b + n_extra], ins[len(flat_a) + n_b + n_extra:]
        at_first = (pl.program_id(0) == 0) & (pl.program_id(1) == 0)
        at_last = (pl.program_id(0) == grid[0] - 1) & (pl.program_id(1) == grid[1] - 1)
        top, bottom = rider.at_steps(riding, at_first, at_first, at_last)
        top()
        lefts = _load_bf16(a_refs, counts)
        products = [lax.dot_general(lefts[s % n_a], b_refs[s][...], dims, preferred_element_type=F32)
                    for s in range(n_b)]
        if not separate:
            products = [functools.reduce(lambda p, r: p + r, products)]
        extra_tiles = [e[...].astype(F32) for e in e_refs]
        outs = products if epilogue is None else epilogue(*products, *extra_tiles, *[w[...] for w in w_refs])
        for o_ref, o in zip(o_refs[:n_out], outs[:n_out]):
            o_ref[...] = o.astype(o_ref.dtype)
        if sum_shapes:
            @pl.when(pl.program_id(1) == 0)
            def _():
                for s_ref in o_refs[n_out:]:
                    s_ref[...] = jnp.zeros_like(s_ref)

            for s_ref, s in zip(o_refs[n_out:], outs[n_out:]):
                s_ref[...] += s
        bottom()

    once = dict(pipeline_mode=pl.Buffered(1)) if tn == n_total else {}
    in_specs = [pl.BlockSpec((tm, a.shape[1]), at(lambda i, j: (i, 0))) for a in flat_a]
    if mode == "nn":
        in_specs += [pl.BlockSpec((b.shape[0], tn), at(lambda i, j: (0, j)), **once) for b in b_list]
    else:
        in_specs += [pl.BlockSpec((tn, b.shape[1]), at(lambda i, j: (j, 0)), **once) for b in b_list]
    in_specs += [_tiled_spec(e.shape, tm, tn, n_total, at) for e in extras]
    in_specs += [pl.BlockSpec(w.shape, lambda j, i: (0, 0), pipeline_mode=pl.Buffered(1)) for w in wholes]
    out_specs = [_tiled_spec(o.shape, tm, tn, n_total, at) for o in out_shapes]
    out_specs += [pl.BlockSpec(s.shape, at(lambda i, j: (0, 0))) for s in sum_shapes]
    semantics = ("arbitrary", "arbitrary") if sum_shapes or rider.operands else ("parallel", "parallel")
    res = pl.pallas_call(
        body, name=name, grid=grid, in_specs=in_specs + [ANY] * len(rider.operands),
        out_specs=out_specs + [ANY] * len(rider.out_shapes),
        out_shape=list(out_shapes) + list(sum_shapes) + list(rider.out_shapes), scratch_shapes=list(rider.scratch),
        compiler_params=_params(*semantics),
    )(*flat_a, *b_list, *extras, *wholes, *rider.operands)
    n_own = len(out_shapes) + len(sum_shapes)
    return (res[:n_own], res[n_own:]) if with_rider else res


def _mm_tn(name, a_list, b_list, tmm=1024, stacked=False, k_blocks=1):
    flat_b, counts = _column_pieces(b_list)
    n_a, n_b = len(a_list), len(counts)
    n_prod = max(n_a, n_b)
    m_total = a_list[0].shape[0]
    ks = [a_list[s % n_a].shape[1] for s in range(n_prod)]
    widths = [sum(p.shape[1] for p in flat_b[sum(counts[:s]):sum(counts[:s + 1])]) for s in range(n_b)]
    widths = [widths[s % n_b] for s in range(n_prod)]
    tmm = min(tmm, m_total)
    assert m_total % tmm == 0 and all(k % k_blocks == 0 for k in ks)
    assert n_a in (1, n_prod) and n_b in (1, n_prod) and not (stacked and n_a > 1)

    def body(*refs):
        a_refs, b_refs, o_refs = refs[:n_a], refs[n_a:n_a + len(flat_b)], refs[n_a + len(flat_b):]

        @pl.when(pl.program_id(1) == 0)
        def _():
            for o_ref in o_refs:
                o_ref[...] = jnp.zeros_like(o_ref)

        lefts, rights = _load_bf16(a_refs, [1] * n_a), _load_bf16(b_refs, counts)
        for s in range(n_prod):
            product = lax.dot_general(lefts[s % n_a], rights[s % n_b], (((0,), (0,)), ((), ())),
                                      preferred_element_type=F32)
            if stacked:
                o_refs[0][s] += product
            else:
                o_refs[s][...] += product

    in_specs = [pl.BlockSpec((tmm, a.shape[1] // k_blocks), lambda kb, m: (m, kb)) for a in a_list]
    in_specs += [pl.BlockSpec((tmm, b.shape[1]), lambda kb, m: (m, 0)) for b in flat_b]
    if stacked:
        out_shape = [jax.ShapeDtypeStruct((n_prod, ks[0], widths[0]), F32)]
        out_specs = [pl.BlockSpec((n_prod, ks[0] // k_blocks, widths[0]), lambda kb, m: (0, kb, 0))]
    else:
        out_shape = [jax.ShapeDtypeStruct((k, w), F32) for k, w in zip(ks, widths)]
        out_specs = [pl.BlockSpec((k // k_blocks, w), lambda kb, m: (kb, 0)) for k, w in zip(ks, widths)]
    return pl.pallas_call(
        body, name=name, grid=(k_blocks, m_total // tmm), in_specs=in_specs, out_specs=out_specs, out_shape=out_shape,
        compiler_params=_params("arbitrary", "arbitrary"),
    )(*a_list, *flat_b)


def _rows(name, fn, ins, tile_outs, sum_outs=(), tr=512, rider=None):
    t_total = max(a.shape[0] for a in ins)
    tr = min(tr, t_total)
    assert t_total % tr == 0
    n_in, n_tile = len(ins), len(tile_outs)
    rider = rider or _NoRider()
    n_steps = t_total // tr

    def body(*refs):
        own_ins, own_outs, _, riding = rider.split(refs, n_in, n_tile + len(sum_outs))
        step = pl.program_id(0)
        top, bottom = rider.at_steps(riding, step == 0, step == n_steps - 1, step == n_steps - 1)
        top()
        refs = tuple(own_ins) + tuple(own_outs)
        outs = fn(*[r[...].astype(F32) for r in refs[:n_in]])
        for o_ref, o in zip(refs[n_in:n_in + n_tile], outs[:n_tile]):
            o_ref[...] = o.astype(o_ref.dtype)
        if sum_outs:
            @pl.when(pl.program_id(0) == 0)
            def _():
                for s_ref in refs[n_in + n_tile:]:
                    s_ref[...] = jnp.zeros_like(s_ref)

            for s_ref, s in zip(refs[n_in + n_tile:], outs[n_tile:]):
                s_ref[...] += s
        bottom()

    def spec(shape):
        if shape[0] == 1:
            return pl.BlockSpec(shape, lambda i: (0, 0))
        return pl.BlockSpec((tr, shape[1]), lambda i: (i, 0))

    return pl.pallas_call(
        body, name=name, grid=(n_steps,), in_specs=[spec(a.shape) for a in ins] + [ANY] * len(rider.operands),
        out_specs=[spec(o.shape) for o in tile_outs] + [spec(s.shape) for s in sum_outs] + [ANY] * len(rider.out_shapes),
        out_shape=list(tile_outs) + list(sum_outs) + list(rider.out_shapes), scratch_shapes=list(rider.scratch),
        compiler_params=_params("arbitrary" if sum_outs or rider.operands else "parallel"),
    )(*ins, *rider.operands)


def _norm_fwd(name, x, gain, rider=None):
    def fn(xv, g):
        inv = lax.rsqrt(jnp.mean(xv * xv, axis=-1, keepdims=True) + RMS_EPS)
        return (xv * inv * g,)

    res = _rows(name, fn, [x, gain], [jax.ShapeDtypeStruct(x.shape, BF16)], rider=rider)
    return res[0], res[1:]


def _rms_norm_bwd(dh, xv, g):
    inv = lax.rsqrt(jnp.mean(xv * xv, axis=-1, keepdims=True) + RMS_EPS)
    xn = xv * inv
    dxn = dh * g
    return inv * (dxn - xn * jnp.mean(dxn * xn, axis=-1, keepdims=True)), jnp.sum(dh * xn, axis=0, keepdims=True)


def _ple_and_loss(gv, pv, x2v, tv, g_final, g_ple, w_pg):
    d = x2v.shape[1]
    s = _sigmoid(gv)
    xv = x2v + s * pv
    inv = lax.rsqrt(jnp.mean(xv * xv, axis=-1, keepdims=True) + RMS_EPS)
    err = xv * inv * g_final - tv
    dx3, d_final = _rms_norm_bwd(err * (1.0 / d), xv, g_final)
    d_pp, d_gp = dx3 * s, dx3 * pv * s * (1.0 - s)
    dh3 = lax.dot_general(d_gp.astype(BF16), w_pg, (((1,), (1,)), ((), ())), preferred_element_type=F32)
    dx2, d_ple = _rms_norm_bwd(dh3, x2v, g_ple)
    dx2 = dx2 + dx3
    return dx2, dx2, d_pp, d_gp, d_final, (0.5 / d) * jnp.sum(err * err, axis=0, keepdims=True), d_ple


def _window_counts(t_pos, w):
    return jnp.minimum(t_pos + 1, w).astype(F32)


def _pool_fwd(u, w_pool, scale, tr=512):
    t_total, width = u.shape
    tr = min(tr, t_total)
    n_groups = len(POOL_WINDOWS)
    gdim = width // n_groups
    ext = tr + POOL_HALO

    def body(u_ref, halo_ref, w_ref, s_ref, pooled_ref, ya_ref):
        i = pl.program_id(0)
        halo = jnp.where(i == 0, 0.0, halo_ref[...])
        t_pos = i * tr + lax.broadcasted_iota(jnp.int32, (tr, 1), 0)
        for g, w in enumerate(POOL_WINDOWS):
            cols = slice(g * gdim, (g + 1) * gdim)
            main = u_ref[:, cols]
            win = jnp.concatenate([halo[:, cols], main], axis=0)
            span = 1
            while span < w:
                win = win + pltpu.roll(win, span, 0)
                span *= 2
            pooled = win[POOL_HALO:, :] * (1.0 / _window_counts(t_pos, w)) - main
            pooled_b = pooled.astype(BF16)
            pooled_ref[:, cols] = pooled_b
            mixed = jnp.dot(pooled_b, w_ref[g], preferred_element_type=F32)
            ya_ref[:, cols] = (mixed * s_ref[:, cols]).astype(BF16)

    hb = tr // POOL_HALO
    return pl.pallas_call(
        body, name="pool_fwd", grid=(t_total // tr,),
        in_specs=[pl.BlockSpec((tr, width), lambda i: (i, 0)),
                  pl.BlockSpec((POOL_HALO, width), lambda i: (jnp.maximum(i * hb - 1, 0), 0)),
                  pl.BlockSpec((n_groups, gdim, gdim), lambda i: (0, 0, 0)),
                  pl.BlockSpec((1, width), lambda i: (0, 0))],
        out_specs=[pl.BlockSpec((tr, width), lambda i: (i, 0)), pl.BlockSpec((tr, width), lambda i: (i, 0))],
        out_shape=[jax.ShapeDtypeStruct(u.shape, BF16), jax.ShapeDtypeStruct(u.shape, BF16)],
        compiler_params=_params("parallel"),
    )(u, u, w_pool, scale)


def _pool_bwd(dya, pooled, w_pool, scale, tr=512):
    t_total, width = dya.shape
    tr = min(tr, t_total)
    n_groups = len(POOL_WINDOWS)
    gdim = width // n_groups
    ext = tr + POOL_HALO
    n_tiles = t_total // tr

    def body(d_ref, halo_ref, p_ref, w_ref, s_ref, du_ref, dw_ref, ds_ref):
        i = pl.program_id(0)

        @pl.when(i == 0)
        def _():
            dw_ref[...] = jnp.zeros_like(dw_ref)
            ds_ref[...] = jnp.zeros_like(ds_ref)

        halo = jnp.where(i == n_tiles - 1, 0.0, halo_ref[...])
        t_pos = i * tr + lax.broadcasted_iota(jnp.int32, (ext, 1), 0)
        for g, w in enumerate(POOL_WINDOWS):
            cols = slice(g * gdim, (g + 1) * gdim)
            sc = s_ref[:, cols]
            d_main = d_ref[:, cols]
            pooled_b = p_ref[:, cols]
            mixed = jnp.dot(pooled_b, w_ref[g], preferred_element_type=F32)
            ds_ref[:, cols] += jnp.sum(d_main * mixed, axis=0, keepdims=True)
            dmix = (jnp.concatenate([d_main, halo[:, cols]], axis=0) * sc).astype(BF16)
            dw_ref[g] += lax.dot_general(pooled_b, dmix[:tr, :], (((0,), (0,)), ((), ())),
                                         preferred_element_type=F32)
            dpool = lax.dot_general(dmix, w_ref[g], (((1,), (1,)), ((), ())), preferred_element_type=F32)
            win = dpool * (1.0 / _window_counts(t_pos, w))
            span = 1
            while span < w:
                win = win + pltpu.roll(win, ext - span, 0)
                span *= 2
            du_ref[:, cols] = (win[:tr, :] - dpool[:tr, :]).astype(BF16)

    hb = tr // POOL_HALO
    last_halo = t_total // POOL_HALO - 1
    return pl.pallas_call(
        body, name="pool_bwd", grid=(n_tiles,),
        in_specs=[pl.BlockSpec((tr, width), lambda i: (i, 0)),
                  pl.BlockSpec((POOL_HALO, width), lambda i: (jnp.minimum((i + 1) * hb, last_halo), 0)),
                  pl.BlockSpec((tr, width), lambda i: (i, 0)),
                  pl.BlockSpec((n_groups, gdim, gdim), lambda i: (0, 0, 0)),
                  pl.BlockSpec((1, width), lambda i: (0, 0))],
        out_specs=[pl.BlockSpec((tr, width), lambda i: (i, 0)),
                   pl.BlockSpec((n_groups, gdim, gdim), lambda i: (0, 0, 0)),
                   pl.BlockSpec((1, width), lambda i: (0, 0))],
        out_shape=[jax.ShapeDtypeStruct(dya.shape, BF16), jax.ShapeDtypeStruct((n_groups, gdim, gdim), F32),
                   jax.ShapeDtypeStruct((1, width), F32)],
        compiler_params=_params("arbitrary"),
    )(dya, dya, pooled, w_pool, scale)


def _head_masks():
    lane = lax.broadcasted_iota(jnp.int32, (1, LANES), 1)
    return lane < HEAD_DIM


def _stack_heads(tile, first):
    zero = jnp.zeros_like(tile)
    return jnp.concatenate([jnp.where(first, tile, zero), jnp.where(first, zero, tile)], axis=0)


def _causal_mask(t_pos, k_start):
    col = lax.broadcasted_iota(jnp.int32, (1, 2 * ATT_SLAB), 1)
    return k_start + (col & (ATT_SLAB - 1)) < t_pos


def _slab_scores(q, kd, mask):
    z2 = lax.dot_general(q, kd, (((1,), (1,)), ((), ())), preferred_element_type=F32) * LOG2_E
    log_hit = jnp.minimum(z2, 0.0) - jnp.log2(1.0 + jnp.exp2(-jnp.abs(z2)))
    log_fail = log_hit - z2
    return log_hit, (log_fail if mask is None else jnp.where(mask, log_fail, 0.0))


def _weights(log_hit, suffix, mask):
    arg = log_hit + suffix
    return jnp.exp2(arg if mask is None else jnp.where(mask, arg, -1e30))


def _tri(upper):
    r = lax.broadcasted_iota(jnp.int32, (ATT_CHUNK, ATT_CHUNK), 0)
    c = lax.broadcasted_iota(jnp.int32, (ATT_CHUNK, ATT_CHUNK), 1)
    return jnp.where(r > c if upper else r < c, 1.0, 0.0).astype(BF16)


def _tri_spec():
    return pl.BlockSpec((ATT_CHUNK, ATT_CHUNK), lambda h, i: (0, 0), pipeline_mode=pl.Buffered(1))


def _scan_chunk(v, tri):
    return jnp.dot(v.astype(BF16), tri, preferred_element_type=F32)


def _lane_bcast(col):
    return jnp.broadcast_to(col, (col.shape[0], LANES))


def _scan_slab(v, tri, carries, from_right):
    n_chunks = ATT_SLAB // ATT_CHUNK
    edge = 0 if from_right else ATT_CHUNK - 1
    parts, new_carries = [None] * (2 * n_chunks), []
    for head in range(2):
        run = carries[head]
        for c in (reversed(range(n_chunks)) if from_right else range(n_chunks)):
            lo_col = head * ATT_SLAB + c * ATT_CHUNK
            vc = v[:, lo_col:lo_col + ATT_CHUNK]
            sc = _scan_chunk(vc, tri)
            parts[head * n_chunks + c] = sc + jnp.concatenate([run] * (ATT_CHUNK // LANES), axis=1)
            run = run + _lane_bcast(sc[:, edge:edge + 1] + vc[:, edge:edge + 1])
        new_carries.append(run)
    return jnp.concatenate(parts, axis=1), new_carries


def _fold_heads(stacked, first):
    s = stacked.shape[0] // 2
    return jnp.where(first, stacked[:s], stacked[s:])


class _NoRider:
    operands, out_shapes, scratch = (), (), ()

    def split(self, refs, n_base_in, n_base_out):
        n_in, n_out, n_sem = len(self.operands), len(self.out_shapes), len(self.scratch)
        a = n_base_in + n_in
        b = a + n_base_out + n_out
        mine = (refs[n_base_in:a], refs[a + n_base_out:b], refs[b:b + n_sem])
        return refs[:n_base_in], refs[a:a + n_base_out], refs[b + n_sem:], mine

    def start(self, ins, outs, sems):
        pass

    def relay(self, ins, outs, sems):
        pass

    def finish(self, ins, outs, sems):
        pass

    def at_steps(self, refs, first_step, relay_step, last_step):
        if not self.operands:
            return (lambda: None), (lambda: None)

        def top():
            pl.when(first_step)(lambda: self.start(*refs))
            pl.when(relay_step)(lambda: self.relay(*refs))

        return top, lambda: pl.when(last_step)(lambda: self.finish(*refs))


def _attn_fwd(q_src, q_col, kv_src, k_col, v_col, n_pairs=4, rider=_NoRider()):
    t_total = q_src.shape[0]
    blk = ATT_BLOCK
    n_chains = ATT_FWD_CHAINS if t_total % (ATT_FWD_CHAINS * blk) == 0 else ATT_CHAINS
    n_steps = t_total // (n_chains * blk)
    assert t_total % ATT_SLAB == 0 and ATT_SLAB == ATT_BLOCK

    def body(*refs):
        (q_ref, k_ref, v_ref, suffix_ref), (o_ref,), _, riding = rider.split(refs, 4, 1)
        h, ii = pl.program_id(0), pl.program_id(1)
        top, bottom = rider.at_steps(riding, (h == 0) & (ii == 0), (h == n_pairs - 1) & (ii == 0),
                                     (h == n_pairs - 1) & (ii == n_steps - 1))
        top()
        first = _head_masks()
        suffix_tri = suffix_ref[...]
        blocks = [n_chains * ii + c for c in range(n_chains)]
        qs = [q_ref[c * blk:(c + 1) * blk, :] * ATT_SCALE for c in range(n_chains)]
        t_pos = [b * blk + lax.broadcasted_iota(jnp.int32, (blk, 1), 0) for b in blocks]

        def one(c, t, chain, on_diagonal):
            _, acc, right_a, right_b = chain
            k_start = pl.multiple_of((blocks[c] - t) * ATT_SLAB, ATT_SLAB)
            kd = _stack_heads(k_ref[pl.ds(k_start, ATT_SLAB), :], first)
            vd = _stack_heads(v_ref[pl.ds(k_start, ATT_SLAB), :], first)
            mask = _causal_mask(t_pos[c], k_start) if on_diagonal else None
            log_hit, log_fail = _slab_scores(qs[c], kd, mask)
            suffix, (right_a, right_b) = _scan_slab(log_fail, suffix_tri, (right_a, right_b), from_right=True)
            a = _weights(log_hit, suffix, mask).astype(BF16)
            acc = acc + jnp.dot(a, vd, preferred_element_type=F32)
            return jnp.max(jnp.maximum(right_a, right_b)), acc, right_a, right_b

        def step(state, on_diagonal):
            t, chains = state
            return t + 1, tuple(one(c, t, chains[c], on_diagonal) for c in range(n_chains))

        def more(state):
            t, chains = state
            return (t <= blocks[0]) & (functools.reduce(jnp.maximum, [ch[0] for ch in chains]) > ATT_EXIT_BELOW)

        zero = jnp.zeros((blk, LANES), F32)
        state = step((0, ((jnp.float32(0.0), zero, zero, zero),) * n_chains), on_diagonal=True)
        t, chains = lax.while_loop(more, functools.partial(step, on_diagonal=False), state)
        for c in range(n_chains):
            chain = chains[c]
            if c:
                _, chain = lax.while_loop(
                    lambda s, c=c: (s[0] <= blocks[c]) & (s[1][0] > ATT_EXIT_BELOW),
                    lambda s, c=c: (s[0] + 1, one(c, s[0], s[1], False)), (t, chain))
            o_ref[c * blk:(c + 1) * blk, :] = chain[1].astype(BF16)
        bottom()

    rows = n_chains * blk
    res = pl.pallas_call(
        body, name="attn_fwd", grid=(n_pairs, n_steps),
        in_specs=[pl.BlockSpec((rows, LANES), lambda h, i: (i, q_col + h)),
                  pl.BlockSpec((t_total, LANES), lambda h, i: (0, k_col + h)),
                  pl.BlockSpec((t_total, LANES), lambda h, i: (0, v_col + h)), _tri_spec()] + [ANY] * len(rider.operands),
        out_specs=[pl.BlockSpec((rows, LANES), lambda h, i: (i, h))] + [ANY] * len(rider.out_shapes),
        out_shape=[jax.ShapeDtypeStruct((t_total, n_pairs * LANES), BF16)] + list(rider.out_shapes),
        scratch_shapes=list(rider.scratch),
        compiler_params=_params("arbitrary", "arbitrary"),
    )(q_src, kv_src, kv_src, _tri(upper=True), *rider.operands)
    return res[0], res[1:]


def _attn_bwd(q_src, q_col, kv_src, k_col, v_col, dy, n_pairs=4, rider=_NoRider()):
    t_total = q_src.shape[0]
    blk = ATT_BLOCK
    n_steps = t_total // (ATT_CHAINS * blk)
    n_slabs = t_total // ATT_SLAB
    assert t_total % ATT_SLAB == 0 and ATT_SLAB == ATT_BLOCK

    def body(*refs):
        ins, (dq_ref, dk_ref, dv_ref), (g_s, dk_acc, dv_acc), riding = rider.split(refs, 6, 3)
        q_ref, dy_ref, k_ref, v_ref, suffix_ref, prefix_ref = ins
        h, ii = pl.program_id(0), pl.program_id(1)
        top, bottom = rider.at_steps(riding, (h == 0) & (ii == 0), (h == n_pairs - 1) & (ii == 0),
                                     (h == n_pairs - 1) & (ii == n_steps - 1))
        top()

        @pl.when(ii == 0)
        def _():
            dk_acc[...] = jnp.zeros_like(dk_acc)
            dv_acc[...] = jnp.zeros_like(dv_acc)

        first = _head_masks()
        suffix_tri = suffix_ref[...]
        prefix_tri = prefix_ref[...]
        blocks = [ATT_CHAINS * ii + c for c in range(ATT_CHAINS)]
        rows = [slice(c * blk, (c + 1) * blk) for c in range(ATT_CHAINS)]
        qs = [q_ref[r, :] * ATT_SCALE for r in rows]
        dys = [dy_ref[r, :] for r in rows]
        t_pos = [b * blk + lax.broadcasted_iota(jnp.int32, (blk, 1), 0) for b in blocks]

        def one1(c, t, chain, on_diagonal):
            _, right_a, right_b = chain
            slab = blocks[c] - t
            k_start = pl.multiple_of(slab * ATT_SLAB, ATT_SLAB)
            kd = _stack_heads(k_ref[pl.ds(k_start, ATT_SLAB), :], first)
            vd = _stack_heads(v_ref[pl.ds(k_start, ATT_SLAB), :], first)
            mask = _causal_mask(t_pos[c], k_start) if on_diagonal else None
            log_hit, log_fail = _slab_scores(qs[c], kd, mask)
            suffix, (right_a, right_b) = _scan_slab(log_fail, suffix_tri, (right_a, right_b), from_right=True)
            a = _weights(log_hit, suffix, mask)
            da = lax.dot_general(dys[c], vd, (((1,), (1,)), ((), ())), preferred_element_type=F32)
            g_s[c, slab] = (da * a).astype(BF16)
            dv_acc[pl.ds(k_start, ATT_SLAB), :] += _fold_heads(lax.dot_general(
                a.astype(BF16), dys[c], (((0,), (0,)), ((), ())), preferred_element_type=F32), first)
            return jnp.max(jnp.maximum(right_a, right_b)), right_a, right_b

        def step1(state, on_diagonal):
            t, chains = state
            return t + 1, tuple(one1(c, t, chains[c], on_diagonal) for c in range(ATT_CHAINS))

        def more(state):
            t, chains = state
            return (t <= blocks[0]) & (functools.reduce(jnp.maximum, [ch[0] for ch in chains]) > ATT_EXIT_BELOW)

        zero = jnp.zeros((blk, LANES), F32)
        state = step1((0, ((jnp.float32(0.0), zero, zero),) * ATT_CHAINS), on_diagonal=True)
        joint, chains = lax.while_loop(more, functools.partial(step1, on_diagonal=False), state)
        done = [joint]
        for c in range(1, ATT_CHAINS):
            done.append(lax.while_loop(
                lambda s, c=c: (s[0] <= blocks[c]) & (s[1][0] > ATT_EXIT_BELOW),
                lambda s, c=c: (s[0] + 1, one1(c, s[0], s[1], False)), (joint, chains[c]))[0])

        def one2(c, t, carry, on_diagonal):
            dq, left_a, left_b = carry
            slab = blocks[c] - t
            k_start = pl.multiple_of(slab * ATT_SLAB, ATT_SLAB)
            kd = _stack_heads(k_ref[pl.ds(k_start, ATT_SLAB), :], first)
            g = g_s[c, slab]
            sig = _sigmoid(lax.dot_general(qs[c], kd, (((1,), (1,)), ((), ())), preferred_element_type=F32))
            prefix, (left_a, left_b) = _scan_slab(g, prefix_tri, (left_a, left_b), from_right=False)
            dz = g * (1.0 - sig) - sig * prefix
            if on_diagonal:
                dz = jnp.where(_causal_mask(t_pos[c], k_start), dz, 0.0)
            dz = dz.astype(BF16)
            dq = dq + jnp.dot(dz, kd, preferred_element_type=F32)
            dk_acc[pl.ds(k_start, ATT_SLAB), :] += _fold_heads(lax.dot_general(
                dz, qs[c], (((0,), (0,)), ((), ())), preferred_element_type=F32), first)
            return dq, left_a, left_b

        carries = [(zero, zero, zero)]
        for c in range(1, ATT_CHAINS):
            carries.append(lax.fori_loop(
                0, done[c] - joint, lambda n, carry, c=c: one2(c, done[c] - 1 - n, carry, False), (zero, zero, zero)))
        carries = lax.fori_loop(
            0, joint - 1,
            lambda n, cs: tuple(one2(c, joint - 1 - n, cs[c], False) for c in range(ATT_CHAINS)), tuple(carries))
        for c in range(ATT_CHAINS):
            dq_ref[rows[c], :] = (one2(c, 0, carries[c], True)[0] * ATT_SCALE).astype(BF16)

        @pl.when(ii == n_steps - 1)
        def _():
            dk_ref[...] = dk_acc[...].astype(BF16)
            dv_ref[...] = dv_acc[...].astype(BF16)

        bottom()

    out = jax.ShapeDtypeStruct((t_total, n_pairs * LANES), BF16)
    n_rows = ATT_CHAINS * blk
    whole = dict(pipeline_mode=pl.Buffered(1))
    res = pl.pallas_call(
        body, name="attn_bwd", grid=(n_pairs, n_steps),
        in_specs=[pl.BlockSpec((n_rows, LANES), lambda h, i: (i, q_col + h)),
                  pl.BlockSpec((n_rows, LANES), lambda h, i: (i, h)),
                  pl.BlockSpec((t_total, LANES), lambda h, i: (0, k_col + h), **whole),
                  pl.BlockSpec((t_total, LANES), lambda h, i: (0, v_col + h), **whole), _tri_spec(), _tri_spec()]
        + [ANY] * len(rider.operands),
        out_specs=[pl.BlockSpec((n_rows, LANES), lambda h, i: (i, h)),
                   pl.BlockSpec((t_total, LANES), lambda h, i: (0, h)),
                   pl.BlockSpec((t_total, LANES), lambda h, i: (0, h))] + [ANY] * len(rider.out_shapes),
        out_shape=[out, out, out] + list(rider.out_shapes),
        scratch_shapes=list(rider.scratch) + [pltpu.VMEM((ATT_CHAINS, n_slabs, blk, 2 * ATT_SLAB), BF16),
                                              pltpu.VMEM((t_total, LANES), F32), pltpu.VMEM((t_total, LANES), F32)],
        compiler_params=_params("arbitrary", "arbitrary"),
    )(q_src, dy, kv_src, kv_src, _tri(upper=True), _tri(upper=False), *rider.operands)
    return res[:3], res[3:]


def _adamw_math(wv, gv, mv, vv):
    mn = ADAM_B1 * mv + (1.0 - ADAM_B1) * gv
    vn = ADAM_B2 * vv + (1.0 - ADAM_B2) * (gv * gv)
    m_hat = mn / (1.0 - ADAM_B1 ** ADAM_STEP)
    v_hat = vn / (1.0 - ADAM_B2 ** ADAM_STEP)
    return -ADAM_LR * (m_hat / (jnp.sqrt(v_hat) + ADAM_EPS) + ADAM_WD * wv), mn, vn


def _adamw_streamed(name, ws, gs, ms, vs):
    n_w = len(ws)
    chunks = []
    for w, a in enumerate(ws):
        step = _row_tile(a.shape[0])
        assert step % ADAMW_STRIP == 0
        chunks += [(w, r, step, a.shape[1]) for r in range(0, a.shape[0], step)]
    slot_shape = (2, max(k[2] for k in chunks), max(k[3] for k in chunks))

    def body(*refs):
        ins = [refs[i * n_w:(i + 1) * n_w] for i in range(4)]
        outs = [refs[(4 + i) * n_w:(5 + i) * n_w] for i in range(3)]
        in_bufs, out_bufs = refs[7 * n_w:7 * n_w + 4], refs[7 * n_w + 4:7 * n_w + 7]
        in_sems, out_sems = refs[7 * n_w + 7:]

        def fetches(k):
            w, r, h, c = chunks[k]
            return [pltpu.make_async_copy(ins[i][w].at[pl.ds(r, h), :], in_bufs[i].at[k % 2, pl.ds(0, h), pl.ds(0, c)],
                                          in_sems.at[k % 2, i]) for i in range(4)]

        def stores(k):
            w, r, h, c = chunks[k]
            return [pltpu.make_async_copy(out_bufs[i].at[k % 2, pl.ds(0, h), pl.ds(0, c)], outs[i][w].at[pl.ds(r, h), :],
                                          out_sems.at[k % 2, i]) for i in range(3)]

        for cp in fetches(0):
            cp.start()
        for k, (_, _, h, c) in enumerate(chunks):
            slot = k % 2
            for cp in fetches(k + 1) if k + 1 < len(chunks) else ():
                cp.start()
            for cp in fetches(k):
                cp.wait()
            for cp in stores(k - 2) if k >= 2 else ():
                cp.wait()

            def strip(i, _, slot=slot, c=c):
                rows = pl.ds(pl.multiple_of(i * ADAMW_STRIP, ADAMW_STRIP), ADAMW_STRIP)
                for buf, value in zip(out_bufs, _adamw_math(*[b[slot, rows, :c] for b in in_bufs])):
                    buf[slot, rows, :c] = value
                return 0

            lax.fori_loop(0, h // ADAMW_STRIP, strip, 0)
            for cp in stores(k):
                cp.start()
        for k in range(max(len(chunks) - 2, 0), len(chunks)):
            for cp in stores(k):
                cp.wait()

    res = pl.pallas_call(
        body, name=name, in_specs=[ANY] * (4 * n_w), out_specs=[ANY] * (3 * n_w),
        out_shape=[jax.ShapeDtypeStruct(a.shape, F32) for _ in range(3) for a in ws],
        scratch_shapes=[pltpu.VMEM(slot_shape, F32)] * 7
        + [pltpu.SemaphoreType.DMA((2, 4)), pltpu.SemaphoreType.DMA((2, 3))],
        compiler_params=_params(),
    )(*ws, *gs, *ms, *vs)
    return [(res[w], res[n_w + w], res[2 * n_w + w]) for w in range(n_w)]


def _bf16_streamed(name, arrays):
    chunks = [(w, r, _row_tile(a.shape[0]), a.shape[1])
              for w, a in enumerate(arrays) for r in range(0, a.shape[0], _row_tile(a.shape[0]))]
    slot_shape = (2, max(k[2] for k in chunks), max(k[3] for k in chunks))

    def body(*refs):
        ins, outs = refs[:len(arrays)], refs[len(arrays):2 * len(arrays)]
        wide, narrow, in_sems, out_sems = refs[2 * len(arrays):]

        def fetch(k):
            w, r, h, c = chunks[k]
            return pltpu.make_async_copy(ins[w].at[pl.ds(r, h), :], wide.at[k % 2, pl.ds(0, h), pl.ds(0, c)],
                                         in_sems.at[k % 2])

        def store(k):
            w, r, h, c = chunks[k]
            return pltpu.make_async_copy(narrow.at[k % 2, pl.ds(0, h), pl.ds(0, c)], outs[w].at[pl.ds(r, h), :],
                                         out_sems.at[k % 2])

        fetch(0).start()
        for k, (_, _, h, c) in enumerate(chunks):
            if k + 1 < len(chunks):
                fetch(k + 1).start()
            fetch(k).wait()
            if k >= 2:
                store(k - 2).wait()
            narrow[k % 2, :h, :c] = wide[k % 2, :h, :c].astype(BF16)
            store(k).start()
        for k in range(max(len(chunks) - 2, 0), len(chunks)):
            store(k).wait()

    return pl.pallas_call(
        body, name=name, in_specs=[ANY] * len(arrays), out_specs=[ANY] * len(arrays),
        out_shape=[jax.ShapeDtypeStruct(a.shape, BF16) for a in arrays],
        scratch_shapes=[pltpu.VMEM(slot_shape, F32), pltpu.VMEM(slot_shape, BF16), pltpu.SemaphoreType.DMA((2,)),
                        pltpu.SemaphoreType.DMA((2,))],
        compiler_params=_params(),
    )(*arrays)


ADAMW_STRIP = 32


def _adamw(name, w, g, m, v):
    fn = _adamw_math
    rows = w.shape[0]
    tr = _row_tile(rows)
    shp = jax.ShapeDtypeStruct(w.shape, F32)
    if rows == 1:
        def body(w_ref, g_ref, m_ref, v_ref, d_ref, mo_ref, vo_ref):
            d, mn, vn = fn(w_ref[...], g_ref[...], m_ref[...], v_ref[...])
            d_ref[...], mo_ref[...], vo_ref[...] = d, mn, vn

        return pl.pallas_call(body, name=name, out_shape=[shp, shp, shp])(w, g, m, v)
    return _rows(name, fn, [w, g, m, v], [shp, shp, shp], tr=tr)


def _place():
    return lax.axis_index("x"), lax.axis_index("y"), lax.axis_index("c")


def _other_chips(x, y):
    return [(1 - x, y), (x, 1 - y), (1 - x, 1 - y)]


ANY = pl.BlockSpec(memory_space=pl.ANY)


def _remote(src, dst, send_sem, recv_sem, to):
    return pltpu.make_async_remote_copy(src_ref=src, dst_ref=dst, send_sem=send_sem, recv_sem=recv_sem,
                                        device_id=to, device_id_type=MESH)


class _WeightGather(_NoRider):
    def __init__(self, shards):
        n_w = len(shards)
        self.operands = list(shards)
        self.out_shapes = [jax.ShapeDtypeStruct((N_CHIPS,) + s.shape, s.dtype) for s in shards]
        self.scratch = [pltpu.SemaphoreType.DMA((3, n_w))] * 4 + [pltpu.SemaphoreType.DMA((n_w,))] * 2

    def _copies(self, ins, outs, sems):
        send_sems, recv_sems, relay_send, relay_recv, own_send, own_recv = sems
        x, y, c = _place()
        my_chip, sibling = 2 * x + y, (x, y, 1 - c)
        n_w = len(ins)

        def half(w, chip, core):
            h = self.operands[w].shape[0] // 2
            return outs[w].at[chip, pl.ds(core * h, h)]

        own = [_remote(ins[w], outs[w].at[my_chip], own_send.at[w], own_recv.at[w], sibling) for w in range(n_w)]
        sends, landed, relays, relayed = [], [], [], []
        for p, (ox, oy) in enumerate(_other_chips(x, y)):
            for w in range(n_w):
                h = self.operands[w].shape[0] // 2
                sends.append(_remote(ins[w].at[pl.ds(c * h, h)], half(w, my_chip, c), send_sems.at[p, w],
                                     recv_sems.at[p, w], (ox, oy, c)))
                here = half(w, 2 * ox + oy, c)
                landed.append(_remote(here, here, send_sems.at[p, w], recv_sems.at[p, w], (ox, oy, c)))
                relays.append(_remote(here, here, relay_send.at[p, w], relay_recv.at[p, w], sibling))
                there = half(w, 2 * ox + oy, 1 - c)
                relayed.append(_remote(there, there, relay_send.at[p, w], relay_recv.at[p, w], sibling))
        return own, sends, landed, relays, relayed

    def start(self, ins, outs, sems):
        own, sends, _, _, _ = self._copies(ins, outs, sems)
        for cp in own + sends:
            cp.start()

    def relay(self, ins, outs, sems):
        _, _, landed, relays, _ = self._copies(ins, outs, sems)
        for arrival, cp in zip(landed, relays):
            arrival.wait_recv()
            cp.start()

    def finish(self, ins, outs, sems):
        own, sends, _, relays, relayed = self._copies(ins, outs, sems)
        for arrival in relayed:
            arrival.wait_recv()
        for cp in sends + relays:
            cp.wait_send()
        for cp in own:
            cp.wait()


class _ChipExchange(_NoRider):
    def __init__(self, pair_sums):
        n_w = len(pair_sums)
        self.operands = list(pair_sums)
        self.out_shapes = [jax.ShapeDtypeStruct((3,) + s.shape[1:], s.dtype) for s in pair_sums]
        self.scratch = [pltpu.SemaphoreType.DMA((3, n_w))] * 2

    def _copies(self, ins, outs, sems):
        send_sems, recv_sems = sems
        x, y, c = _place()
        return [_remote(ins[w].at[2 * ox + oy], outs[w].at[p], send_sems.at[p, w], recv_sems.at[p, w], (ox, oy, c))
                for p, (ox, oy) in enumerate(_other_chips(x, y)) for w in range(len(ins))]

    def start(self, ins, outs, sems):
        for cp in self._copies(ins, outs, sems):
            cp.start()

    def finish(self, ins, outs, sems):
        for cp in self._copies(ins, outs, sems):
            cp.wait()


class _PairExchange(_NoRider):
    def __init__(self, grads):
        n_w = len(grads)
        self.operands = list(grads)
        self.out_shapes = [jax.ShapeDtypeStruct(g.shape[:-2] + (g.shape[-2] // 2, g.shape[-1]), F32) for g in grads]
        self.scratch = [pltpu.SemaphoreType.DMA((n_w,))] * 2

    def _copies(self, ins, theirs, sems):
        send_sems, recv_sems = sems
        x, y, c = _place()
        sends = []
        for w, g in enumerate(self.operands):
            rows = pl.ds((1 - c) * (g.shape[-2] // 2), g.shape[-2] // 2)
            src = ins[w].at[:, rows, :] if g.ndim == 3 else ins[w].at[rows, :]
            sends.append(_remote(src, theirs[w], send_sems.at[w], recv_sems.at[w], (x, y, 1 - c)))
        return sends

    def start(self, ins, outs, sems):
        for cp in self._copies(ins, outs, sems):
            cp.start()

    def finish(self, ins, outs, sems):
        for cp in self._copies(ins, outs, sems):
            cp.wait()


def _exchange_pair_sum(name, place, grad):
    n, r, c = grad.shape
    half = r // 2

    def body(place_ref, g_all, g_ref, o_ref, theirs, send_sems, recv_sems):
        j = pl.program_id(0)
        x, y, core = _place()

        def copy(k):
            return _remote(g_all.at[k, pl.ds((1 - core) * half, half)], theirs.at[k], send_sems.at[k],
                           recv_sems.at[k], (x, y, 1 - core))

        @pl.when(j == 0)
        def _():
            for k in range(n):
                copy(k).start()

        copy(j).wait_recv()
        o_ref[0] = (g_ref[0] + theirs[j]).astype(BF16)

        @pl.when(j == n - 1)
        def _():
            for k in range(n):
                copy(k).wait_send()

    return pl.pallas_call(
        body, name=name, out_shape=jax.ShapeDtypeStruct((n, half, c), BF16),
        grid_spec=pltpu.PrefetchScalarGridSpec(
            num_scalar_prefetch=1, grid=(n,),
            in_specs=[ANY, pl.BlockSpec((1, half, c), lambda j, pr: (j, pr[0], 0))],
            out_specs=pl.BlockSpec((1, half, c), lambda j, pr: (j, 0, 0)),
            scratch_shapes=[pltpu.VMEM((n, half, c), F32), pltpu.SemaphoreType.DMA((n,)),
                            pltpu.SemaphoreType.DMA((n,))]),
        compiler_params=_params("arbitrary"),
    )(place, grad, grad)


class _NoExchanges:
    pair_sums, landed = {}, {}

    def gather(self, names):
        return _NoRider()

    def pair(self, names, grads):
        return _NoRider()

    def paired(self, names, grads, theirs):
        pass

    def pair_now(self, names, grads):
        pass

    def chip(self, names):
        return _NoRider()


class _StepExchanges(_NoExchanges):
    def __init__(self, shards_bf16, place):
        self.shards, self.place = shards_bf16, place
        self.pair_sums, self.landed = {}, {}

    def gather(self, names):
        return _WeightGather([self.shards[n] for n in names])

    def pair(self, names, grads):
        return _PairExchange([grads[n] for n in names])

    def paired(self, names, grads, theirs):
        sums = _pair_sums_streamed("pair_sums_" + names[0], [grads[n] for n in names], list(theirs))
        self.pair_sums.update(zip(names, sums))

    def pair_now(self, names, grads):
        for n in names:
            self.pair_sums[n] = _exchange_pair_sum(f"pair_sum_{n}", self.place, grads[n])

    def chip(self, names):
        return _ChipExchange([self.pair_sums[n] for n in names])


SUM_ROWS = 32


def _finish_gradients(place, pair_sums, landed, vec):
    n_w = len(pair_sums)
    rows = vec.shape[0]
    halves = [s.shape[1:] for s in pair_sums]

    def body(place_ref, *refs):
        sums, lands, v_ref = refs[:n_w], refs[n_w:2 * n_w], refs[2 * n_w]
        outs, o_ref = refs[2 * n_w + 1:3 * n_w + 1], refs[3 * n_w + 1]
        stage = refs[3 * n_w + 2:4 * n_w + 2]
        kept, half_send, half_recv, slots, core_sums, vec_send, vec_recv, sum_send, sum_recv = refs[4 * n_w + 2:]
        x, y, c = _place()
        my_chip, sibling = 2 * x + y, (x, y, 1 - c)
        slots[my_chip] = v_ref[...]
        spread = []
        for p, (ox, oy) in enumerate(_other_chips(x, y)):
            here = slots.at[2 * ox + oy]
            spread.append((_remote(v_ref, slots.at[my_chip], vec_send.at[p], vec_recv.at[p], (ox, oy, c)),
                           _remote(here, here, vec_send.at[p], vec_recv.at[p], (ox, oy, c))))
        for send, _ in spread:
            send.start()
        copies = []
        for w, (h, _) in enumerate(halves):
            step = SUM_ROWS if h % SUM_ROWS == 0 else h

            def sum_rows(i, _, w=w, step=step):
                r = pl.ds(pl.multiple_of(i * step, step), step)
                stage[w][r, :] = functools.reduce(lambda total, p: total + lands[w][p, r, :].astype(F32), range(3),
                                                  sums[w][0, r, :].astype(F32))
                return 0

            lax.fori_loop(0, h // step, sum_rows, 0)
            mine, theirs = outs[w].at[pl.ds(c * h, h)], outs[w].at[pl.ds((1 - c) * h, h)]
            copies.append((pltpu.make_async_copy(stage[w], mine, kept.at[w]),
                           _remote(stage[w], mine, half_send.at[w], half_recv.at[w], sibling),
                           _remote(theirs, theirs, half_send.at[w], half_recv.at[w], sibling)))
            copies[-1][0].start()
            copies[-1][1].start()
        for send, arrival in spread:
            arrival.wait_recv()
            send.wait_send()
        core_sums[c] = functools.reduce(lambda total, chip: total + slots[chip], range(1, N_CHIPS), slots[0])
        mine, theirs = core_sums.at[c], core_sums.at[1 - c]
        to_sibling = _remote(mine, mine, sum_send.at[0], sum_recv.at[0], sibling)
        to_sibling.start()
        _remote(theirs, theirs, sum_send.at[0], sum_recv.at[0], sibling).wait_recv()
        to_sibling.wait_send()
        o_ref[...] = core_sums[0] + core_sums[1]
        for keep, send, arrival in copies:
            keep.wait()
            arrival.wait_recv()
            send.wait_send()

    once = dict(pipeline_mode=pl.Buffered(1))
    vm = pl.BlockSpec((rows, LANES), lambda i, pr: (0, 0))
    res = pl.pallas_call(
        body, name="finish_gradients",
        grid_spec=pltpu.PrefetchScalarGridSpec(
            num_scalar_prefetch=1, grid=(1,),
            in_specs=[pl.BlockSpec((1,) + hc, lambda i, pr: (pr[1], 0, 0), **once) for hc in halves]
            + [pl.BlockSpec((3,) + hc, lambda i, pr: (0, 0, 0), **once) for hc in halves] + [vm],
            out_specs=[ANY] * n_w + [vm],
            scratch_shapes=[pltpu.VMEM(hc, F32) for hc in halves]
            + [pltpu.SemaphoreType.DMA((n_w,))] * 3
            + [pltpu.VMEM((N_CHIPS, rows, LANES), F32), pltpu.VMEM((2, rows, LANES), F32),
               pltpu.SemaphoreType.DMA((N_CHIPS - 1,)), pltpu.SemaphoreType.DMA((N_CHIPS - 1,)),
               pltpu.SemaphoreType.DMA((1,)), pltpu.SemaphoreType.DMA((1,))]),
        out_shape=[_sds((2 * h, cols), F32) for h, cols in halves] + [_sds(vec.shape, F32)],
        compiler_params=_params("arbitrary"),
    )(place, *pair_sums, *landed, vec)
    return res[:n_w], res[n_w]


def _row_tile(rows):
    fits = [tr for tr in range(16, min(rows, 512) + 1, 16) if rows % tr == 0]
    return max(fits) if fits else rows


def _pair_sum(name, place, grad, theirs):
    if grad.ndim == 2:
        return _pair_sum_joined(name, place, grad, theirs)
    n, r, c = grad.shape
    half = r // 2
    tr = _row_tile(half)
    nb = half // tr

    def body(place_ref, g_ref, t_ref, o_ref):
        o_ref[...] = (g_ref[...] + t_ref[...]).astype(BF16)

    return pl.pallas_call(
        body, name=name, out_shape=jax.ShapeDtypeStruct((n, half, c), BF16),
        grid_spec=pltpu.PrefetchScalarGridSpec(
            num_scalar_prefetch=1, grid=(n, nb),
            in_specs=[pl.BlockSpec((1, tr, c), lambda j, i, pr: (j, pr[0] * nb + i, 0)),
                      pl.BlockSpec((1, tr, c), lambda j, i, pr: (j, i, 0))],
            out_specs=pl.BlockSpec((1, tr, c), lambda j, i, pr: (j, i, 0))),
        compiler_params=_params("parallel", "parallel"),
    )(place, grad, theirs)


def _pair_sum_joined(name, place, grad, theirs):
    r, wide = grad.shape
    half, c = r // 2, wide // N_CHIPS
    tr = _row_tile(half)
    nb = half // tr

    def body(place_ref, g_ref, t_ref, o_ref):
        for j in range(N_CHIPS):
            cols = slice(j * c, (j + 1) * c)
            o_ref[j] = (g_ref[:, cols] + t_ref[:, cols]).astype(BF16)

    return pl.pallas_call(
        body, name=name, out_shape=jax.ShapeDtypeStruct((N_CHIPS, half, c), BF16),
        grid_spec=pltpu.PrefetchScalarGridSpec(
            num_scalar_prefetch=1, grid=(nb,),
            in_specs=[pl.BlockSpec((tr, wide), lambda i, pr: (pr[0] * nb + i, 0)),
                      pl.BlockSpec((tr, wide), lambda i, pr: (i, 0))],
            out_specs=pl.BlockSpec((N_CHIPS, tr, c), lambda i, pr: (0, i, 0))),
        compiler_params=_params("parallel"),
    )(place, grad, theirs)


def _pair_sums_streamed(name, grads, theirs):
    n_w = len(grads)
    chunks, out_shapes = [], []
    for w, g in enumerate(grads):
        half = g.shape[-2] // 2
        if g.ndim == 3:
            chunks += [(w, j, half, g.shape[2]) for j in range(g.shape[0])]
            out_shapes.append(_sds((g.shape[0], half, g.shape[2]), BF16))
        else:
            chunks.append((w, None, half, g.shape[1]))
            out_shapes.append(_sds((N_CHIPS, half, g.shape[1] // N_CHIPS), BF16))
    slot_shape = (2, max(k[2] for k in chunks), max(k[3] for k in chunks))

    def body(*refs):
        g_refs, t_refs, o_refs = refs[:n_w], refs[n_w:2 * n_w], refs[2 * n_w:3 * n_w]
        mine, other, summed, in_sems, out_sems = refs[3 * n_w:]
        core = lax.axis_index("c")

        def fetches(k):
            w, j, h, c = chunks[k]
            rows, slot = pl.ds(core * h, h), k % 2
            own = g_refs[w].at[rows, :] if j is None else g_refs[w].at[j, rows, :]
            sent = t_refs[w] if j is None else t_refs[w].at[j]
            return (pltpu.make_async_copy(own, mine.at[slot, pl.ds(0, h), pl.ds(0, c)], in_sems.at[slot, 0]),
                    pltpu.make_async_copy(sent, other.at[slot, pl.ds(0, h), pl.ds(0, c)], in_sems.at[slot, 1]))

        def stores(k):
            w, j, h, c = chunks[k]
            slot = k % 2
            if j is not None:
                return [pltpu.make_async_copy(summed.at[slot, pl.ds(0, h), pl.ds(0, c)], o_refs[w].at[j],
                                              out_sems.at[slot, 0])]
            part = c // N_CHIPS
            return [pltpu.make_async_copy(summed.at[slot, pl.ds(0, h), pl.ds(i * part, part)], o_refs[w].at[i],
                                          out_sems.at[slot, i]) for i in range(N_CHIPS)]

        for cp in fetches(0):
            cp.start()
        for k, (_, _, h, c) in enumerate(chunks):
            slot = k % 2
            for cp in fetches(k + 1) if k + 1 < len(chunks) else ():
                cp.start()
            for cp in fetches(k):
                cp.wait()
            for cp in stores(k - 2) if k >= 2 else ():
                cp.wait()
            summed[slot, :h, :c] = (mine[slot, :h, :c] + other[slot, :h, :c]).astype(BF16)
            for cp in stores(k):
                cp.start()
        for k in range(max(len(chunks) - 2, 0), len(chunks)):
            for cp in stores(k):
                cp.wait()

    return pl.pallas_call(
        body, name=name, in_specs=[ANY] * (2 * n_w), out_specs=[ANY] * n_w, out_shape=out_shapes,
        scratch_shapes=[pltpu.VMEM(slot_shape, F32), pltpu.VMEM(slot_shape, F32), pltpu.VMEM(slot_shape, BF16),
                        pltpu.SemaphoreType.DMA((2, 2)), pltpu.SemaphoreType.DMA((2, N_CHIPS))],
        compiler_params=_params(),
    )(*grads, *theirs)


MIXER = ("w_branch_a", "w_branch_b", "w_out")
FFN_PLE = ("w_ffn_gate", "w_ffn_up", "w_ffn_down", "w_ple_gate", "w_ple_proj")
LATE = MIXER + FFN_PLE
BIG = ("w_in",) + LATE
HELD_TRANSPOSED = ("w_ffn_gate", "w_ffn_up")
SMALL = ("norm_mix", "w_pool", "pool_scale", "norm_ffn", "norm_ple", "norm_final")


def _join_columns(w4):
    return jnp.concatenate([w4[j] for j in range(N_CHIPS)], axis=1)


def _sds(shape, dtype):
    return jax.ShapeDtypeStruct(shape, dtype)


def _local_step(x, p, target, wf, small, ex=None):
    t, d = x.shape
    w_pool_b = small["w_pool"].astype(BF16)
    dp = w_pool_b.shape[0] * w_pool_b.shape[1]

    ex = ex or _NoExchanges()
    h1, first = _norm_fwd("norm_mix", x, small["norm_mix"], rider=ex.gather(("w_in",)))
    wf = {**wf, **dict(zip(("w_in",), first))}
    w_in = wf["w_in"]
    u, q, kv, ga, gb = _mm(
        "proj", [h1], [w_in[j] for j in range(N_CHIPS)], "nn",
        [_sds((t, dp), F32), _sds((t, dp), BF16), _sds((t, d), BF16), _sds((t, d), BF16), _sds((t, d), BF16)],
        separate=True, epilogue=lambda uq, kv_, ga_, gb_: (uq[:, :dp], uq[:, dp:], kv_, ga_, gb_), tm=512)
    pooled, ya = _pool_fwd(u, w_pool_b, small["pool_scale"])
    n_pairs = dp // LANES
    yb, late = _attn_fwd(q, 0, kv, 0, n_pairs, n_pairs, rider=ex.gather(LATE))
    wf = {**wf, **dict(zip(LATE, late))}
    w_down = wf["w_ffn_down"].reshape(-1, d)
    dff = w_down.shape[0]
    w_gate_t, w_up_t = wf["w_ffn_gate"].reshape(dff, d), wf["w_ffn_up"].reshape(dff, d)
    w_a, w_b, w_pp = _join_columns(wf["w_branch_a"]), _join_columns(wf["w_branch_b"]), _join_columns(wf["w_ple_proj"])
    w_out = wf["w_out"].reshape(d, d)
    w_pg = wf["w_ple_gate"].reshape(d, d)
    def residual_norm(branch, xv, g, w):
        xn = xv + jnp.dot(branch.astype(BF16), w, preferred_element_type=F32)
        return xn, xn * lax.rsqrt(jnp.mean(xn * xn, axis=-1, keepdims=True) + RMS_EPS) * g

    def mixer_tail(tav, tbv, gav, gbv, xv, g, w):
        merged = _sigmoid(gav) * tav + _sigmoid(gbv) * tbv
        return (tav, tbv, merged) + residual_norm(merged, xv, g, w)

    def ffn_tail(gv, uv, xv, g, w):
        act = gv * _sigmoid(gv) * uv
        return (gv, uv, act) + residual_norm(act, xv, g, w)

    stream = [_sds((t, d), F32), _sds((t, d), BF16)]
    ta, tb, merged, x1, h2 = _mm(
        "mixer_out", [ya, yb], [w_a, w_b], "nn", [_sds((t, d), BF16)] * 3 + stream,
        extras=[ga, gb, x, small["norm_ffn"]], wholes=[w_out], separate=True, epilogue=mixer_tail, tm=512)
    gate, up, act, x2, h3 = _mm(
        "ffn", [h2], [w_gate_t, w_up_t], "nt", [_sds((t, dff), BF16)] * 3 + stream,
        extras=[x1, small["norm_ple"]], wholes=[w_down], separate=True, epilogue=ffn_tail, tm=256)
    dx2, dx2_b, d_pp, d_gp, d_norm_final, loss_row, d_norm_ple = _mm(
        "ple_loss", [h3, p], [w_pg, w_pp], "nn", stream + [_sds((t, d), BF16)] * 2,
        extras=[x2, target, small["norm_final"].reshape(1, d), small["norm_ple"]], wholes=[w_pg], separate=True,
        epilogue=_ple_and_loss, sum_shapes=[_sds((1, d), F32)] * 3, tm=512)

    def through_norm(dh, xv, g, dres):
        dx, d_gain = _rms_norm_bwd(dh, xv, g)
        return dx + dres, dx + dres, d_gain

    gain_sum = [_sds((1, d), F32)]
    g_w_pp, g_w_pg = _mm_tn("g_ple", [p, h3], [d_pp, d_gp])

    def ffn_bwd(d_act, gv, uv, xv, g, dres, wg_t, wu_t):
        s = _sigmoid(gv)
        d_gate, d_up = d_act * uv * (s * (1.0 + gv * (1.0 - s))), d_act * (gv * s)
        dh2 = (jnp.dot(d_gate.astype(BF16), wg_t, preferred_element_type=F32)
               + jnp.dot(d_up.astype(BF16), wu_t, preferred_element_type=F32))
        return (d_gate, d_up) + through_norm(dh2, xv, g, dres)

    d_gate, d_up, dx1, dx1_b, d_norm_ffn = _mm(
        "ffn_bwd", [dx2_b], [w_down], "nt", [_sds((t, dff), BF16)] * 2 + stream,
        extras=[gate, up, x1, small["norm_ffn"], dx2], wholes=[w_gate_t, w_up_t], epilogue=ffn_bwd,
        sum_shapes=gain_sum, tm=256)
    g_w_down, = _mm_tn("g_ffn_down", [act], [dx2_b], tmm=512)
    g_w_gate_t, g_w_up_t = _mm_tn("g_ffn_gate_up", [d_gate, d_up], [h2], k_blocks=2)

    def merge_bwd(acc, tav, tbv, gav, gbv):
        sa, sb = _sigmoid(gav), _sigmoid(gbv)
        return acc * sa, acc * sb, acc * tav * sa * (1.0 - sa), acc * tbv * sb * (1.0 - sb)

    big = {
        "w_ffn_gate": g_w_gate_t.reshape(wf["w_ffn_gate"].shape), "w_ffn_up": g_w_up_t.reshape(wf["w_ffn_up"].shape),
        "w_ffn_down": g_w_down.reshape(wf["w_ffn_down"].shape),
        "w_ple_gate": g_w_pg.reshape(wf["w_ple_gate"].shape), "w_ple_proj": g_w_pp,
    }
    (d_ta, d_tb, d_ga, d_gb), theirs = _mm(
        "d_merged", [dx1_b], [w_out], "nt", [_sds((t, d), BF16)] * 4, extras=[ta, tb, ga, gb], epilogue=merge_bwd,
        tm=512, rider=ex.pair(FFN_PLE, big))
    ex.paired(FFN_PLE, big, theirs)
    g_w_out, big["w_branch_a"], big["w_branch_b"] = _mm_tn("g_mixer", [merged, ya, yb], [dx1_b, d_ta, d_tb])
    big["w_out"] = g_w_out.reshape(wf["w_out"].shape)
    (d_ya, d_yb), theirs = _mm(
        "d_branches", [d_ta, d_tb], [w_a, w_b], "nt", [_sds((t, dp), F32), _sds((t, dp), BF16)], separate=True,
        rider=ex.pair(MIXER, big))
    ex.paired(MIXER, big, theirs)
    d_u, g_w_pool, d_pool_scale = _pool_bwd(d_ya, pooled, w_pool_b, small["pool_scale"])
    (d_q, d_k, d_v), landed = _attn_bwd(q, 0, kv, 0, n_pairs, d_yb, n_pairs, rider=ex.chip(LATE))
    ex.landed.update(zip(LATE, landed))
    d_proj = [(d_u, d_q), (d_k, d_v), d_ga, d_gb]
    big["w_in"], = _mm_tn("g_w_in", [h1], d_proj, tmm=512, stacked=True)
    ex.pair_now(("w_in",), big)
    (grad_x, d_norm_mix), landed = _mm(
        "d_h1", d_proj, [w_in[j] for j in range(N_CHIPS)], "nt", [_sds((t, d), F32)],
        extras=[x, small["norm_mix"], dx1], epilogue=lambda dh, xv, g, dres: through_norm(dh, xv, g, dres)[1:],
        sum_shapes=gain_sum, tm=512, rider=ex.chip(("w_in",)))
    ex.landed.update(zip(("w_in",), landed))
    small_g = {"norm_mix": d_norm_mix, "w_pool": g_w_pool, "pool_scale": d_pool_scale, "norm_ffn": d_norm_ffn,
               "norm_ple": d_norm_ple, "norm_final": d_norm_final}
    return grad_x, big, small_g, loss_row


def _pack_small(small_g, loss_row):
    parts, layout = [], []
    for name in SMALL + ("loss",):
        v = (loss_row if name == "loss" else small_g[name]).reshape(-1, LANES)
        pad = (-v.shape[0]) % 8
        if pad:
            v = jnp.concatenate([v, jnp.zeros((pad, LANES), F32)], axis=0)
        layout.append((name, sum(q.shape[0] for q in parts), v.shape[0]))
        parts.append(v)
    return jnp.concatenate(parts, axis=0), layout


def kernel(x, p, norm_mix, w_in, w_pool, pool_scale, w_branch_a, w_branch_b, w_out, norm_ffn, w_ffn_gate, w_ffn_up, w_ffn_down, norm_ple, w_ple_gate, w_ple_proj, norm_final, loss_target, m_norm_mix, m_w_in, m_w_pool, m_pool_scale, m_w_branch_a, m_w_branch_b, m_w_out, m_norm_ffn, m_w_ffn_gate, m_w_ffn_up, m_w_ffn_down, m_norm_ple, m_w_ple_gate, m_w_ple_proj, m_norm_final, v_norm_mix, v_w_in, v_w_pool, v_pool_scale, v_w_branch_a, v_w_branch_b, v_w_out, v_norm_ffn, v_w_ffn_gate, v_w_ffn_up, v_w_ffn_down, v_norm_ple, v_w_ple_gate, v_w_ple_proj, v_norm_final):
    given = dict(locals())
    order = ("norm_mix", "w_in", "w_pool", "pool_scale", "w_branch_a", "w_branch_b", "w_out", "norm_ffn", "w_ffn_gate",
             "w_ffn_up", "w_ffn_down", "norm_ple", "w_ple_gate", "w_ple_proj", "norm_final")
    t, d = x.shape[1], x.shape[2]
    def local(a, n):
        return jnp.swapaxes(a[0], 0, 1) if n in HELD_TRANSPOSED else a[0]

    def back(a, n):
        return (jnp.swapaxes(a, 0, 1) if n in HELD_TRANSPOSED else a)[None]

    shard = {n: local(given[n], n) for n in BIG}
    small = {"norm_mix": norm_mix, "w_pool": w_pool[0], "pool_scale": pool_scale, "norm_ffn": norm_ffn,
             "norm_ple": norm_ple, "norm_final": norm_final}

    place = jnp.stack([lax.axis_index("c"), 2 * lax.axis_index("x") + lax.axis_index("y")]).astype(jnp.int32)
    ex = _StepExchanges(dict(zip(BIG, _bf16_streamed("bf16_shards", [shard[n] for n in BIG]))), place)
    grad_x, _, small_g, loss_row = _local_step(
        x.reshape(t, d), p.reshape(t, p.shape[-1]), loss_target.reshape(t, d), {}, small, ex)
    packed, layout = _pack_small(small_g, loss_row)
    filled, reduced = _finish_gradients(place, [ex.pair_sums[n] for n in BIG], [ex.landed[n] for n in BIG], packed)
    grads = dict(zip(BIG, filled))
    for name, start, rows in layout:
        if name == "loss":
            loss = jnp.sum(reduced[start:start + rows])
        else:
            n_el = small[name].size
            grads[name] = reduced[start:start + rows].reshape(-1)[:n_el]

    deltas, new_m, new_v = {}, {}, {}
    updated = dict(zip(BIG, _adamw_streamed(
        "adamw_big", [shard[n] for n in BIG], [grads[n] for n in BIG], [local(given["m_" + n], n) for n in BIG],
        [local(given["v_" + n], n) for n in BIG])))
    for n in order:
        if n in BIG:
            dl, mn, vn = updated[n]
            grads[n], deltas[n], new_m[n], new_v[n] = [back(a, n) for a in (grads[n], dl, mn, vn)]
        else:
            w, full = small[n], given[n].shape
            shape2 = (1, w.shape[0]) if w.ndim == 1 else (w.shape if w.ndim == 2 else (w.shape[0] * w.shape[1], w.shape[2]))
            dl, mn, vn = _adamw(f"adamw_{n}", w.reshape(shape2), grads[n].reshape(shape2),
                                given["m_" + n].reshape(shape2), given["v_" + n].reshape(shape2))
            grads[n], deltas[n], new_m[n], new_v[n] = [a.reshape(full) for a in (grads[n], dl, mn, vn)]

    return (loss, grad_x.reshape(x.shape), *[grads[n] for n in order], *[deltas[n] for n in order],
            *[new_m[n] for n in order], *[new_v[n] for n in order])
```
